```python
import math
import jax, jax.numpy as jnp
from jax import lax
import numpy as np

D_MODEL = 1024
BATCH = 8
SEQ = 8192
DEPTH = 2

N_MIXERS = 2
N_HEADS = 16
N_KV_HEADS = 4
HEAD_DIM = D_MODEL // N_HEADS
GROUP = N_HEADS // N_KV_HEADS
ROT_DIM = HEAD_DIM // 4
ROPE_THETA = 500000.0
WINDOW = 128
BLK = 128
QKV_DIM = (N_HEADS + 2 * N_KV_HEADS) * HEAD_DIM
CONV_CH = D_MODEL
CONV_WIDTH = 31
CONV_PAD = (CONV_WIDTH - 1) // 2
D_FF = ((8 * D_MODEL // 3 + 255) // 256) * 256
N_ATTN_LAYERS = (DEPTH + 1) // 2
N_CONV_LAYERS = DEPTH // 2
EPS = 1e-6
NEG = -1e30

kernel_name = "hybrid_window_gqa_conformer_conv_encoder"


def rmsnorm(x, g):
    xf = x.astype(jnp.float32)
    y = xf * lax.rsqrt(jnp.mean(xf * xf, axis=-1, keepdims=True) + EPS)
    return (y * g.astype(jnp.float32)).astype(x.dtype)


def layernorm(x, g, b):
    xf = x.astype(jnp.float32)
    mu = jnp.mean(xf, axis=-1, keepdims=True)
    var = jnp.mean(jnp.square(xf - mu), axis=-1, keepdims=True)
    y = (xf - mu) * lax.rsqrt(var + EPS)
    return (y * g.astype(jnp.float32) + b.astype(jnp.float32)).astype(x.dtype)


def partial_rope(x, cos, sin):
    half = ROT_DIM // 2
    x1 = x[..., :half].astype(jnp.float32)
    x2 = x[..., half:ROT_DIM].astype(jnp.float32)
    rot = jnp.concatenate([x1 * cos - x2 * sin, x2 * cos + x1 * sin], axis=-1)
    return jnp.concatenate([rot.astype(x.dtype), x[..., ROT_DIM:]], axis=-1)


def window_attention(h, w_qkv, w_o, sink):
    B, S, _ = h.shape
    nb = S // BLK
    qkv = h @ w_qkv
    q = qkv[..., :N_HEADS * HEAD_DIM].reshape(B, S, N_HEADS, HEAD_DIM)
    k = qkv[..., N_HEADS * HEAD_DIM:(N_HEADS + N_KV_HEADS) * HEAD_DIM].reshape(B, S, N_KV_HEADS, HEAD_DIM)
    v = qkv[..., (N_HEADS + N_KV_HEADS) * HEAD_DIM:].reshape(B, S, N_KV_HEADS, HEAD_DIM)

    pos = jnp.arange(S, dtype=jnp.float32)
    inv_freq = ROPE_THETA ** (-jnp.arange(0, ROT_DIM, 2, dtype=jnp.float32) / ROT_DIM)
    ang = pos[:, None] * inv_freq[None, :]
    cos = jnp.cos(ang)[:, None, :]
    sin = jnp.sin(ang)[:, None, :]
    q = partial_rope(q, cos, sin)
    k = partial_rope(k, cos, sin)

    qb = q.reshape(B, nb, BLK, N_KV_HEADS, GROUP, HEAD_DIM)
    pad = ((0, 0), (BLK, BLK), (0, 0), (0, 0))
    kp = jnp.pad(k, pad).reshape(B, nb + 2, BLK, N_KV_HEADS, HEAD_DIM)
    vp = jnp.pad(v, pad).reshape(B, nb + 2, BLK, N_KV_HEADS, HEAD_DIM)
    kb = jnp.concatenate([kp[:, :-2], kp[:, 1:-1], kp[:, 2:]], axis=2)
    vb = jnp.concatenate([vp[:, :-2], vp[:, 1:-1], vp[:, 2:]], axis=2)

    scale = 1.0 / math.sqrt(HEAD_DIM)
    s = jnp.einsum('bnqkgd,bnckd->bnkgqc', qb, kb).astype(jnp.float32) * scale

    qi = jnp.arange(BLK)
    ci = jnp.arange(3 * BLK)
    rel = ci[None, :] - BLK - qi[:, None]
    in_window = jnp.abs(rel) <= WINDOW
    key_pos = jnp.arange(nb)[:, None] * BLK - BLK + ci[None, :]
    in_range = (key_pos >= 0) & (key_pos < S)
    valid = in_window[None, :, :] & in_range[:, None, :]
    s = jnp.where(valid[None, :, None, None, :, :], s, NEG)

    sink_l = sink.astype(jnp.float32).reshape(1, 1, N_KV_HEADS, GROUP, 1, 1)
    m = jnp.maximum(jnp.max(s, axis=-1, keepdims=True), sink_l)
    e = jnp.exp(s - m)
    p = e / (jnp.sum(e, axis=-1, keepdims=True) + jnp.exp(sink_l - m))

    o = jnp.einsum('bnkgqc,bnckd->bnqkgd', p.astype(vb.dtype), vb)
    o = o.reshape(B, S, N_HEADS * HEAD_DIM)
    return o @ w_o


def conformer_conv(h, w_pw1, b_pw1, w_dw, b_dw, ln_g, ln_b, w_pw2, b_pw2):
    u = h @ w_pw1 + b_pw1
    a, gate = u[..., :CONV_CH], u[..., CONV_CH:]
    u = a * jax.nn.sigmoid(gate)
    u = lax.conv_general_dilated(
        u, w_dw[:, None, :].astype(u.dtype), window_strides=(1,),
        padding=[(CONV_PAD, CONV_PAD)],
        dimension_numbers=('NWC', 'WIO', 'NWC'),
        feature_group_count=CONV_CH) + b_dw
    u = layernorm(u, ln_g, ln_b)
    u = jax.nn.silu(u)
    return u @ w_pw2 + b_pw2


def swiglu_ffn(h, w_gu, w_down):
    gu = h @ w_gu
    return (jax.nn.silu(gu[..., :D_FF]) * gu[..., D_FF:]) @ w_down


def _fwd_setup_inputs(seed: int = 0) -> dict:
    key = jax.random.key(seed)
    ks = jax.random.split(key, 24)
    f32 = jnp.float32
    nrm = lambda k, shp, sc: jax.random.normal(k, shp, f32) * sc
    gain = lambda k, shp: 1.0 + 0.02 * jax.random.normal(k, shp, f32)
    na, nc = N_ATTN_LAYERS, N_CONV_LAYERS
    return {
        "x": nrm(ks[0], (BATCH, SEQ, D_MODEL), 1.0),
        "attn_norm": gain(ks[1], (na, D_MODEL)),
        "attn_w_qkv": nrm(ks[2], (na, D_MODEL, QKV_DIM), D_MODEL ** -0.5),
        "attn_w_o": nrm(ks[3], (na, N_HEADS * HEAD_DIM, D_MODEL), (N_HEADS * HEAD_DIM) ** -0.5),
        "attn_sink": nrm(ks[4], (na, N_HEADS), 0.5),
        "conv_norm": gain(ks[5], (nc, D_MODEL)),
        "conv_w_pw1": nrm(ks[6], (nc, D_MODEL, 2 * CONV_CH), D_MODEL ** -0.5),
        "conv_b_pw1": nrm(ks[7], (nc, 2 * CONV_CH), 0.02),
        "conv_w_dw": nrm(ks[8], (nc, CONV_WIDTH, CONV_CH), CONV_WIDTH ** -0.5),
        "conv_b_dw": nrm(ks[9], (nc, CONV_CH), 0.02),
        "conv_ln_g": gain(ks[10], (nc, CONV_CH)),
        "conv_ln_b": nrm(ks[11], (nc, CONV_CH), 0.02),
        "conv_w_pw2": nrm(ks[12], (nc, CONV_CH, D_MODEL), CONV_CH ** -0.5),
        "conv_b_pw2": nrm(ks[13], (nc, D_MODEL), 0.02),
        "ffn_norm": gain(ks[14], (DEPTH, D_MODEL)),
        "ffn_w_gu": nrm(ks[15], (DEPTH, D_MODEL, 2 * D_FF), D_MODEL ** -0.5),
        "ffn_w_down": nrm(ks[16], (DEPTH, D_FF, D_MODEL), D_FF ** -0.5),
        "final_norm": gain(ks[17], (D_MODEL,)),
    }


def _fwd_reference(x, attn_norm, attn_w_qkv, attn_w_o, attn_sink,
              conv_norm, conv_w_pw1, conv_b_pw1, conv_w_dw, conv_b_dw,
              conv_ln_g, conv_ln_b, conv_w_pw2, conv_b_pw2,
              ffn_norm, ffn_w_gu, ffn_w_down, final_norm):
    for i in range(DEPTH):
        j = i // N_MIXERS
        if i % N_MIXERS == 0:
            h = rmsnorm(x, attn_norm[j])
            x = x + window_attention(h, attn_w_qkv[j], attn_w_o[j], attn_sink[j])
        else:
            h = rmsnorm(x, conv_norm[j])
            x = x + conformer_conv(h, conv_w_pw1[j], conv_b_pw1[j], conv_w_dw[j],
                                   conv_b_dw[j], conv_ln_g[j], conv_ln_b[j],
                                   conv_w_pw2[j], conv_b_pw2[j])
        h = rmsnorm(x, ffn_norm[i])
        x = x + swiglu_ffn(h, ffn_w_gu[i], ffn_w_down[i])
    return rmsnorm(x, final_norm)


import jax as _jax
import jax.numpy as _jnp

TWIN_FORMAT = 'train_step'
FWD_PARAMS = ['x', 'attn_norm', 'attn_w_qkv', 'attn_w_o', 'attn_sink', 'conv_norm', 'conv_w_pw1', 'conv_b_pw1', 'conv_w_dw', 'conv_b_dw', 'conv_ln_g', 'conv_ln_b', 'conv_w_pw2', 'conv_b_pw2', 'ffn_norm', 'ffn_w_gu', 'ffn_w_down', 'final_norm']
TWIN_WEIGHTS = ['attn_norm', 'attn_w_qkv', 'attn_w_o', 'attn_sink', 'conv_norm', 'conv_w_pw1', 'conv_b_pw1', 'conv_w_dw', 'conv_b_dw', 'conv_ln_g', 'conv_ln_b', 'conv_w_pw2', 'conv_b_pw2', 'ffn_norm', 'ffn_w_gu', 'ffn_w_down', 'final_norm']
TWIN_DIFF_INPUT = 'x'
TWIN_INPUTS = ['x', 'attn_norm', 'attn_w_qkv', 'attn_w_o', 'attn_sink', 'conv_norm', 'conv_w_pw1', 'conv_b_pw1', 'conv_w_dw', 'conv_b_dw', 'conv_ln_g', 'conv_ln_b', 'conv_w_pw2', 'conv_b_pw2', 'ffn_norm', 'ffn_w_gu', 'ffn_w_down', 'final_norm', 'loss_target', 'm_attn_norm', 'm_attn_w_qkv', 'm_attn_w_o', 'm_attn_sink', 'm_conv_norm', 'm_conv_w_pw1', 'm_conv_b_pw1', 'm_conv_w_dw', 'm_conv_b_dw', 'm_conv_ln_g', 'm_conv_ln_b', 'm_conv_w_pw2', 'm_conv_b_pw2', 'm_ffn_norm', 'm_ffn_w_gu', 'm_ffn_w_down', 'm_final_norm', 'v_attn_norm', 'v_attn_w_qkv', 'v_attn_w_o', 'v_attn_sink', 'v_conv_norm', 'v_conv_w_pw1', 'v_conv_b_pw1', 'v_conv_w_dw', 'v_conv_b_dw', 'v_conv_ln_g', 'v_conv_ln_b', 'v_conv_w_pw2', 'v_conv_b_pw2', 'v_ffn_norm', 'v_ffn_w_gu', 'v_ffn_w_down', 'v_final_norm']
TWIN_OUTPUTS = ['loss', 'grad_x', 'grad_attn_norm', 'grad_attn_w_qkv', 'grad_attn_w_o', 'grad_attn_sink', 'grad_conv_norm', 'grad_conv_w_pw1', 'grad_conv_b_pw1', 'grad_conv_w_dw', 'grad_conv_b_dw', 'grad_conv_ln_g', 'grad_conv_ln_b', 'grad_conv_w_pw2', 'grad_conv_b_pw2', 'grad_ffn_norm', 'grad_ffn_w_gu', 'grad_ffn_w_down', 'grad_final_norm', 'delta_attn_norm', 'delta_attn_w_qkv', 'delta_attn_w_o', 'delta_attn_sink', 'delta_conv_norm', 'delta_conv_w_pw1', 'delta_conv_b_pw1', 'delta_conv_w_dw', 'delta_conv_b_dw', 'delta_conv_ln_g', 'delta_conv_ln_b', 'delta_conv_w_pw2', 'delta_conv_b_pw2', 'delta_ffn_norm', 'delta_ffn_w_gu', 'delta_ffn_w_down', 'delta_final_norm', 'new_m_attn_norm', 'new_m_attn_w_qkv', 'new_m_attn_w_o', 'new_m_attn_sink', 'new_m_conv_norm', 'new_m_conv_w_pw1', 'new_m_conv_b_pw1', 'new_m_conv_w_dw', 'new_m_conv_b_dw', 'new_m_conv_ln_g', 'new_m_conv_ln_b', 'new_m_conv_w_pw2', 'new_m_conv_b_pw2', 'new_m_ffn_norm', 'new_m_ffn_w_gu', 'new_m_ffn_w_down', 'new_m_final_norm', 'new_v_attn_norm', 'new_v_attn_w_qkv', 'new_v_attn_w_o', 'new_v_attn_sink', 'new_v_conv_norm', 'new_v_conv_w_pw1', 'new_v_conv_b_pw1', 'new_v_conv_w_dw', 'new_v_conv_b_dw', 'new_v_conv_ln_g', 'new_v_conv_ln_b', 'new_v_conv_w_pw2', 'new_v_conv_b_pw2', 'new_v_ffn_norm', 'new_v_ffn_w_gu', 'new_v_ffn_w_down', 'new_v_final_norm']
TWIN_LEAF_KINDS = {'loss': 'loss', 'grad_x': 'grad_x', 'grad_attn_norm': 'grad_w', 'grad_attn_w_qkv': 'grad_w', 'grad_attn_w_o': 'grad_w', 'grad_attn_sink': 'grad_w', 'grad_conv_norm': 'grad_w', 'grad_conv_w_pw1': 'grad_w', 'grad_conv_b_pw1': 'grad_w', 'grad_conv_w_dw': 'grad_w', 'grad_conv_b_dw': 'grad_w', 'grad_conv_ln_g': 'grad_w', 'grad_conv_ln_b': 'grad_w', 'grad_conv_w_pw2': 'grad_w', 'grad_conv_b_pw2': 'grad_w', 'grad_ffn_norm': 'grad_w', 'grad_ffn_w_gu': 'grad_w', 'grad_ffn_w_down': 'grad_w', 'grad_final_norm': 'grad_w', 'delta_attn_norm': 'delta_w', 'delta_attn_w_qkv': 'delta_w', 'delta_attn_w_o': 'delta_w', 'delta_attn_sink': 'delta_w', 'delta_conv_norm': 'delta_w', 'delta_conv_w_pw1': 'delta_w', 'delta_conv_b_pw1': 'delta_w', 'delta_conv_w_dw': 'delta_w', 'delta_conv_b_dw': 'delta_w', 'delta_conv_ln_g': 'delta_w', 'delta_conv_ln_b': 'delta_w', 'delta_conv_w_pw2': 'delta_w', 'delta_conv_b_pw2': 'delta_w', 'delta_ffn_norm': 'delta_w', 'delta_ffn_w_gu': 'delta_w', 'delta_ffn_w_down': 'delta_w', 'delta_final_norm': 'delta_w', 'new_m_attn_norm': 'new_m', 'new_m_attn_w_qkv': 'new_m', 'new_m_attn_w_o': 'new_m', 'new_m_attn_sink': 'new_m', 'new_m_conv_norm': 'new_m', 'new_m_conv_w_pw1': 'new_m', 'new_m_conv_b_pw1': 'new_m', 'new_m_conv_w_dw': 'new_m', 'new_m_conv_b_dw': 'new_m', 'new_m_conv_ln_g': 'new_m', 'new_m_conv_ln_b': 'new_m', 'new_m_conv_w_pw2': 'new_m', 'new_m_conv_b_pw2': 'new_m', 'new_m_ffn_norm': 'new_m', 'new_m_ffn_w_gu': 'new_m', 'new_m_ffn_w_down': 'new_m', 'new_m_final_norm': 'new_m', 'new_v_attn_norm': 'new_v', 'new_v_attn_w_qkv': 'new_v', 'new_v_attn_w_o': 'new_v', 'new_v_attn_sink': 'new_v', 'new_v_conv_norm': 'new_v', 'new_v_conv_w_pw1': 'new_v', 'new_v_conv_b_pw1': 'new_v', 'new_v_conv_w_dw': 'new_v', 'new_v_conv_b_dw': 'new_v', 'new_v_conv_ln_g': 'new_v', 'new_v_conv_ln_b': 'new_v', 'new_v_conv_w_pw2': 'new_v', 'new_v_conv_b_pw2': 'new_v', 'new_v_ffn_norm': 'new_v', 'new_v_ffn_w_gu': 'new_v', 'new_v_ffn_w_down': 'new_v', 'new_v_final_norm': 'new_v'}


def _forward(args):
    return _fwd_reference(*[args[k] for k in FWD_PARAMS])


def _output_shape():
    def fwd():
        inp = _fwd_setup_inputs(0)
        return _fwd_reference(*[inp[k] for k in FWD_PARAMS])
    out = _jax.eval_shape(fwd)
    return out.shape, out.dtype

N_MICROBATCH = 1
ADAM_LR = 0.001
ADAM_B1 = 0.9
ADAM_B2 = 0.999
ADAM_EPS = 1e-08
ADAM_WD = 0.01
ADAM_STEP = 10
PER_EXAMPLE_BATCH_AXIS = {'x': 0, 'loss_target': 0}
SHARED_INPUTS = []
_WEIGHT_DTYPES = {'attn_norm': _jnp.float32, 'attn_w_qkv': _jnp.float32, 'attn_w_o': _jnp.float32, 'attn_sink': _jnp.float32, 'conv_norm': _jnp.float32, 'conv_w_pw1': _jnp.float32, 'conv_b_pw1': _jnp.float32, 'conv_w_dw': _jnp.float32, 'conv_b_dw': _jnp.float32, 'conv_ln_g': _jnp.float32, 'conv_ln_b': _jnp.float32, 'conv_w_pw2': _jnp.float32, 'conv_b_pw2': _jnp.float32, 'ffn_norm': _jnp.float32, 'ffn_w_gu': _jnp.float32, 'ffn_w_down': _jnp.float32, 'final_norm': _jnp.float32}
MOMENT_SCALE = {'attn_norm': 6.790586e-02, 'attn_w_qkv': 5.316589e-02, 'attn_w_o': 3.797325e-02, 'attn_sink': 1.271114e-03, 'conv_norm': 1.358866e-01, 'conv_w_pw1': 9.736418e-02, 'conv_b_pw1': 1.000832e-01, 'conv_w_dw': 1.283003e-01, 'conv_b_dw': 2.710566e-01, 'conv_ln_g': 1.540409e-01, 'conv_ln_b': 1.484363e-01, 'conv_w_pw2': 1.293562e-01, 'conv_b_pw2': 2.748851e-01, 'ffn_norm': 1.959543e-01, 'ffn_w_gu': 7.634966e-02, 'ffn_w_down': 1.249979e-01, 'final_norm': 6.408514e+01}


def _to_microbatches(a, axis):
    t = _jnp.moveaxis(a, axis, 0)
    t = t.reshape((N_MICROBATCH, t.shape[0] // N_MICROBATCH) + t.shape[1:])
    return _jnp.moveaxis(t, 1, axis + 1)


def setup_inputs(seed: int = 0) -> dict:
    inp = _fwd_setup_inputs(seed)
    key = _jax.random.fold_in(_jax.random.key(seed), 7919)
    shape, _ = _output_shape()
    out = dict(inp)
    out["loss_target"] = _jax.random.normal(_jax.random.fold_in(key, 0), shape, _jnp.float32)
    for i, name in enumerate(TWIN_WEIGHTS):
        w = inp[name].astype(_jnp.float32)
        if MOMENT_SCALE is None:
            s = _jnp.sqrt(_jnp.mean(_jnp.square(w)) + 1e-30)
        else:
            s = MOMENT_SCALE[name]
        km, kv = _jax.random.split(_jax.random.fold_in(key, i + 1))
        out[name] = w
        out["m_" + name] = s * _jax.random.normal(km, w.shape, _jnp.float32)
        out["v_" + name] = (s * s) * _jax.random.uniform(kv, w.shape, _jnp.float32, 0.5, 1.5)
    if N_MICROBATCH > 1:
        for name, axis in PER_EXAMPLE_BATCH_AXIS.items():
            out[name] = _to_microbatches(out[name], axis)
    return {'x': out['x'], 'attn_norm': out['attn_norm'], 'attn_w_qkv': out['attn_w_qkv'], 'attn_w_o': out['attn_w_o'], 'attn_sink': out['attn_sink'], 'conv_norm': out['conv_norm'], 'conv_w_pw1': out['conv_w_pw1'], 'conv_b_pw1': out['conv_b_pw1'], 'conv_w_dw': out['conv_w_dw'], 'conv_b_dw': out['conv_b_dw'], 'conv_ln_g': out['conv_ln_g'], 'conv_ln_b': out['conv_ln_b'], 'conv_w_pw2': out['conv_w_pw2'], 'conv_b_pw2': out['conv_b_pw2'], 'ffn_norm': out['ffn_norm'], 'ffn_w_gu': out['ffn_w_gu'], 'ffn_w_down': out['ffn_w_down'], 'final_norm': out['final_norm'], 'loss_target': out['loss_target'], 'm_attn_norm': out['m_attn_norm'], 'm_attn_w_qkv': out['m_attn_w_qkv'], 'm_attn_w_o': out['m_attn_w_o'], 'm_attn_sink': out['m_attn_sink'], 'm_conv_norm': out['m_conv_norm'], 'm_conv_w_pw1': out['m_conv_w_pw1'], 'm_conv_b_pw1': out['m_conv_b_pw1'], 'm_conv_w_dw': out['m_conv_w_dw'], 'm_conv_b_dw': out['m_conv_b_dw'], 'm_conv_ln_g': out['m_conv_ln_g'], 'm_conv_ln_b': out['m_conv_ln_b'], 'm_conv_w_pw2': out['m_conv_w_pw2'], 'm_conv_b_pw2': out['m_conv_b_pw2'], 'm_ffn_norm': out['m_ffn_norm'], 'm_ffn_w_gu': out['m_ffn_w_gu'], 'm_ffn_w_down': out['m_ffn_w_down'], 'm_final_norm': out['m_final_norm'], 'v_attn_norm': out['v_attn_norm'], 'v_attn_w_qkv': out['v_attn_w_qkv'], 'v_attn_w_o': out['v_attn_w_o'], 'v_attn_sink': out['v_attn_sink'], 'v_conv_norm': out['v_conv_norm'], 'v_conv_w_pw1': out['v_conv_w_pw1'], 'v_conv_b_pw1': out['v_conv_b_pw1'], 'v_conv_w_dw': out['v_conv_w_dw'], 'v_conv_b_dw': out['v_conv_b_dw'], 'v_conv_ln_g': out['v_conv_ln_g'], 'v_conv_ln_b': out['v_conv_ln_b'], 'v_conv_w_pw2': out['v_conv_w_pw2'], 'v_conv_b_pw2': out['v_conv_b_pw2'], 'v_ffn_norm': out['v_ffn_norm'], 'v_ffn_w_gu': out['v_ffn_w_gu'], 'v_ffn_w_down': out['v_ffn_w_down'], 'v_final_norm': out['v_final_norm']}


def _loss(weights, diff, rest, loss_target):
    with _jax.named_scope("forward"):
        args = {**rest, TWIN_DIFF_INPUT: diff, **{k: w.astype(_WEIGHT_DTYPES[k]) for k, w in weights.items()}}
        y = _forward(args)
    with _jax.named_scope("loss_head"):
        err = _jnp.square(y.astype(_jnp.float32) - loss_target)
        return 0.5 * _jnp.sum(_jnp.mean(err, axis=-1)) if err.ndim else 0.5 * err


def _adamw(w, g, m, v):
    m = ADAM_B1 * m + (1.0 - ADAM_B1) * g
    v = ADAM_B2 * v + (1.0 - ADAM_B2) * _jnp.square(g)
    m_hat = m / (1.0 - ADAM_B1 ** ADAM_STEP)
    v_hat = v / (1.0 - ADAM_B2 ** ADAM_STEP)
    delta = -ADAM_LR * (m_hat / (_jnp.sqrt(v_hat) + ADAM_EPS) + ADAM_WD * w)
    return delta, m, v


def reference(x, attn_norm, attn_w_qkv, attn_w_o, attn_sink, conv_norm, conv_w_pw1, conv_b_pw1, conv_w_dw, conv_b_dw, conv_ln_g, conv_ln_b, conv_w_pw2, conv_b_pw2, ffn_norm, ffn_w_gu, ffn_w_down, final_norm, loss_target, m_attn_norm, m_attn_w_qkv, m_attn_w_o, m_attn_sink, m_conv_norm, m_conv_w_pw1, m_conv_b_pw1, m_conv_w_dw, m_conv_b_dw, m_conv_ln_g, m_conv_ln_b, m_conv_w_pw2, m_conv_b_pw2, m_ffn_norm, m_ffn_w_gu, m_ffn_w_down, m_final_norm, v_attn_norm, v_attn_w_qkv, v_attn_w_o, v_attn_sink, v_conv_norm, v_conv_w_pw1, v_conv_b_pw1, v_conv_w_dw, v_conv_b_dw, v_conv_ln_g, v_conv_ln_b, v_conv_w_pw2, v_conv_b_pw2, v_ffn_norm, v_ffn_w_gu, v_ffn_w_down, v_final_norm):
    given = dict(x=x, attn_norm=attn_norm, attn_w_qkv=attn_w_qkv, attn_w_o=attn_w_o, attn_sink=attn_sink, conv_norm=conv_norm, conv_w_pw1=conv_w_pw1, conv_b_pw1=conv_b_pw1, conv_w_dw=conv_w_dw, conv_b_dw=conv_b_dw, conv_ln_g=conv_ln_g, conv_ln_b=conv_ln_b, conv_w_pw2=conv_w_pw2, conv_b_pw2=conv_b_pw2, ffn_norm=ffn_norm, ffn_w_gu=ffn_w_gu, ffn_w_down=ffn_w_down, final_norm=final_norm, loss_target=loss_target, m_attn_norm=m_attn_norm, m_attn_w_qkv=m_attn_w_qkv, m_attn_w_o=m_attn_w_o, m_attn_sink=m_attn_sink, m_conv_norm=m_conv_norm, m_conv_w_pw1=m_conv_w_pw1, m_conv_b_pw1=m_conv_b_pw1, m_conv_w_dw=m_conv_w_dw, m_conv_b_dw=m_conv_b_dw, m_conv_ln_g=m_conv_ln_g, m_conv_ln_b=m_conv_ln_b, m_conv_w_pw2=m_conv_w_pw2, m_conv_b_pw2=m_conv_b_pw2, m_ffn_norm=m_ffn_norm, m_ffn_w_gu=m_ffn_w_gu, m_ffn_w_down=m_ffn_w_down, m_final_norm=m_final_norm, v_attn_norm=v_attn_norm, v_attn_w_qkv=v_attn_w_qkv, v_attn_w_o=v_attn_w_o, v_attn_sink=v_attn_sink, v_conv_norm=v_conv_norm, v_conv_w_pw1=v_conv_w_pw1, v_conv_b_pw1=v_conv_b_pw1, v_conv_w_dw=v_conv_w_dw, v_conv_b_dw=v_conv_b_dw, v_conv_ln_g=v_conv_ln_g, v_conv_ln_b=v_conv_ln_b, v_conv_w_pw2=v_conv_w_pw2, v_conv_b_pw2=v_conv_b_pw2, v_ffn_norm=v_ffn_norm, v_ffn_w_gu=v_ffn_w_gu, v_ffn_w_down=v_ffn_w_down, v_final_norm=v_final_norm)
    weights = {n: given[n] for n in TWIN_WEIGHTS}
    shared = {n: given[n] for n in SHARED_INPUTS}
    per_example = {n: given[n] for n in ['x']}
    grad_fn = _jax.value_and_grad(_loss, argnums=(0, 1))

    def one_microbatch(ex, loss_target):
        ex = dict(ex)
        diff = ex.pop(TWIN_DIFF_INPUT)
        return grad_fn(weights, diff, {**shared, **ex}, loss_target)

    if N_MICROBATCH == 1:
        loss, (grad_w, grad_x) = one_microbatch(per_example, given["loss_target"])
    else:
        def body(carry, xs):
            loss_sum, grad_sum = carry
            l_k, (gw_k, gx_k) = one_microbatch(xs[0], xs[1])
            with _jax.named_scope("update"):
                return (loss_sum + l_k, _jax.tree.map(_jnp.add, grad_sum, gw_k)), gx_k

        init = (_jnp.zeros((), _jnp.float32), _jax.tree.map(_jnp.zeros_like, weights))
        (loss, grad_w), grad_x = _jax.lax.scan(body, init, (per_example, given["loss_target"]))
    with _jax.named_scope("update"):
        delta_w, new_m, new_v = {}, {}, {}
        for n in TWIN_WEIGHTS:
            delta_w[n], new_m[n], new_v[n] = _adamw(weights[n], grad_w[n], given["m_" + n], given["v_" + n])
    return (loss, grad_x, *[grad_w[n] for n in TWIN_WEIGHTS], *[delta_w[n] for n in TWIN_WEIGHTS],
            *[new_m[n] for n in TWIN_WEIGHTS], *[new_v[n] for n in TWIN_WEIGHTS])
```

```python
import functools
import math

import jax
import jax.numpy as jnp
from jax import lax
from jax.experimental import pallas as pl
from jax.experimental.pallas import tpu as pltpu

F32 = jnp.float32
BF16 = jnp.bfloat16

D = 1024
N_HEADS = 16
N_KV = 4
GROUP = N_HEADS // N_KV
HD = 64
ROT = 16
THETA = 500000.0
BLK = 128
QKV = (N_HEADS + 2 * N_KV) * HD
KV_OFF = N_HEADS * HD
DFF = 2816
CONV_W = 31
CONV_PAD = 15
HALO = 16
EPS = 1e-6
NEG = -1e30
N_CHIPS = 4
N_DEV = 8
LANES = 128
SUBLANES = 8

ADAM_LR, ADAM_B1, ADAM_B2, ADAM_EPS, ADAM_WD, ADAM_STEP = 0.001, 0.9, 0.999, 1e-08, 0.01, 10

VMEM_LIMIT = 56 * 1024 * 1024
MESH = pl.DeviceIdType.MESH


def _params(*sem):
    return pltpu.CompilerParams(dimension_semantics=sem, vmem_limit_bytes=VMEM_LIMIT)


def _tile(n, want):
    if n <= want:
        return n
    for t in range(want, 7, -1):
        if n % t == 0 and t % 8 == 0:
            return t
    return n


def _sigmoid(v):
    return 1.0 / (1.0 + jnp.exp(-v))


def _rms_fwd(xv, gain):
    r = lax.rsqrt(jnp.mean(xv * xv, axis=-1, keepdims=True) + EPS)
    return xv * r * gain


def _rms_bwd(dh, xv, gain, dres):
    r = lax.rsqrt(jnp.mean(xv * xv, axis=-1, keepdims=True) + EPS)
    xhat = xv * r
    gy = dh * gain
    dx = r * (gy - xhat * jnp.mean(gy * xhat, axis=-1, keepdims=True))
    return dx + dres, dh * xhat


def _rope(blk, c, s1, s2):
    return blk * c + pltpu.roll(blk, LANES - ROT // 2, 1) * s1 + pltpu.roll(blk, ROT // 2, 1) * s2


def _dot(a, b):
    return jnp.dot(a, b, preferred_element_type=F32)


def _dot_tb(a, b):
    return lax.dot_general(a, b, (((1,), (1,)), ((), ())), preferred_element_type=F32)


def _dot_ta(a, b):
    return lax.dot_general(a, b, (((0,), (0,)), ((), ())), preferred_element_type=F32)


def rms_qkv(x, gain, w, rc, rs1, rs2):
    T = x.shape[0]
    tm = _tile(T, 512)

    def body(x_ref, g_ref, w_ref, c_ref, s1_ref, s2_ref, h_ref, qkv_ref):
        h = _rms_fwd(x_ref[...], g_ref[...]).astype(BF16)
        h_ref[...] = h
        acc = _dot(h, w_ref[...])
        c, s1, s2 = c_ref[...], s1_ref[...], s2_ref[...]
        n_rot = (KV_OFF + N_KV * HD) // LANES
        for j in range(n_rot):
            sl = slice(LANES * j, LANES * (j + 1))
            qkv_ref[:, sl] = _rope(acc[:, sl], c, s1, s2).astype(BF16)
        qkv_ref[:, n_rot * LANES:] = acc[:, n_rot * LANES:].astype(BF16)

    row = lambda i: (i, 0)
    full = lambda i: (0, 0)
    return pl.pallas_call(
        body, name="rms_qkv", grid=(T // tm,),
        in_specs=[pl.BlockSpec((tm, D), row), pl.BlockSpec((1, D), full), pl.BlockSpec((D, QKV), full),
                  pl.BlockSpec((tm, LANES), row), pl.BlockSpec((tm, LANES), row), pl.BlockSpec((tm, LANES), row)],
        out_specs=[pl.BlockSpec((tm, D), row), pl.BlockSpec((tm, QKV), row)],
        out_shape=[jax.ShapeDtypeStruct((T, D), BF16), jax.ShapeDtypeStruct((T, QKV), BF16)],
        compiler_params=_params("parallel"),
    )(x, gain, w, rc, rs1, rs2)


def _attn_mask(n, T):
    qi = lax.broadcasted_iota(jnp.int32, (BLK, 3 * BLK), 0)
    ci = lax.broadcasted_iota(jnp.int32, (BLK, 3 * BLK), 1)
    rel = ci - BLK - qi
    key_pos = n * BLK - BLK + ci
    return (jnp.abs(rel) <= BLK) & (key_pos >= 0) & (key_pos < T)


def _attn_specs(T):
    nb = T // BLK
    kv_blk = 2 * N_KV * HD
    kv_col = KV_OFF // kv_blk
    q_spec = pl.BlockSpec((BLK, KV_OFF), lambda n: (n, 0))
    prev = pl.BlockSpec((BLK, kv_blk), lambda n: (jnp.maximum(n - 1, 0), kv_col))
    own = pl.BlockSpec((BLK, kv_blk), lambda n: (n, kv_col))
    nxt = pl.BlockSpec((BLK, kv_blk), lambda n: (jnp.minimum(n + 1, nb - 1), kv_col))
    return nb, q_spec, prev, own, nxt


def attn_fwd(qkv, sink):
    T = qkv.shape[0]
    nb, q_spec, prev, own, nxt = _attn_specs(T)
    scale = 1.0 / math.sqrt(HD)

    def body(sink_ref, q_ref, kp_ref, ko_ref, kn_ref, o_ref):
        n = pl.program_id(0)
        valid = _attn_mask(n, T)
        kv = jnp.concatenate([kp_ref[...], ko_ref[...], kn_ref[...]], axis=0)
        for h in range(N_HEADS):
            g = h // GROUP
            q = q_ref[:, h * HD:(h + 1) * HD]
            k = kv[:, g * HD:(g + 1) * HD]
            v = kv[:, N_KV * HD + g * HD:N_KV * HD + (g + 1) * HD]
            s = jnp.where(valid, _dot_tb(q, k) * scale, NEG)
            sk = sink_ref[h]
            m = jnp.maximum(jnp.max(s, axis=-1, keepdims=True), sk)
            e = jnp.exp(s - m)
            p = e / (jnp.sum(e, axis=-1, keepdims=True) + jnp.exp(sk - m))
            o_ref[:, h * HD:(h + 1) * HD] = _dot(p.astype(BF16), v).astype(BF16)

    return pl.pallas_call(
        body, name="attn_fwd", grid=(nb,),
        in_specs=[pl.BlockSpec(memory_space=pltpu.SMEM), q_spec, prev, own, nxt],
        out_specs=pl.BlockSpec((BLK, D), lambda n: (n, 0)),
        out_shape=jax.ShapeDtypeStruct((T, D), BF16),
        compiler_params=_params("parallel"),
    )(sink, qkv, qkv, qkv, qkv)


def mm_res(a, w, resid, bias, name):
    T, K = a.shape
    tm = _tile(T, 512)

    def body(a_ref, w_ref, r_ref, b_ref, o_ref):
        o_ref[...] = _dot(a_ref[...], w_ref[...]) + b_ref[...] + r_ref[...]

    row = lambda i: (i, 0)
    full = lambda i: (0, 0)
    return pl.pallas_call(
        body, name=name, grid=(T // tm,),
        in_specs=[pl.BlockSpec((tm, K), row), pl.BlockSpec((K, D), full), pl.BlockSpec((tm, D), row),
                  pl.BlockSpec((1, D), full)],
        out_specs=pl.BlockSpec((tm, D), row),
        out_shape=jax.ShapeDtypeStruct((T, D), F32),
        compiler_params=_params("parallel"),
    )(a, w, resid, bias)


def rms_mm_gate(x, gain, w, bias, H, swiglu, act_dtype, name):
    T = x.shape[0]
    tm = _tile(T, 512)
    tn = 1408 if H % 1408 == 0 else 512
    nj = H // tn

    def body(x_ref, g_ref, w1_ref, w2_ref, b1_ref, b2_ref, h_ref, pre_ref, act_ref, hs):
        @pl.when(pl.program_id(1) == 0)
        def _():
            h = _rms_fwd(x_ref[...], g_ref[...]).astype(BF16)
            hs[...] = h
            h_ref[...] = h

        h = hs[...]
        a = _dot(h, w1_ref[...]) + b1_ref[...]
        b = _dot(h, w2_ref[...]) + b2_ref[...]
        pre_ref[0] = a.astype(BF16)
        pre_ref[1] = b.astype(BF16)
        if swiglu:
            act = a * _sigmoid(a) * b
        else:
            act = a * _sigmoid(b)
        act_ref[...] = act.astype(act_dtype)

    return pl.pallas_call(
        body, name=name, grid=(T // tm, nj),
        in_specs=[pl.BlockSpec((tm, D), lambda i, j: (i, 0)), pl.BlockSpec((1, D), lambda i, j: (0, 0)),
                  pl.BlockSpec((D, tn), lambda i, j: (0, j)), pl.BlockSpec((D, tn), lambda i, j: (0, nj + j)),
                  pl.BlockSpec((1, tn), lambda i, j: (0, j)), pl.BlockSpec((1, tn), lambda i, j: (0, nj + j))],
        out_specs=[pl.BlockSpec((tm, D), lambda i, j: (i, 0)), pl.BlockSpec((2, tm, tn), lambda i, j: (0, i, j)),
                   pl.BlockSpec((tm, tn), lambda i, j: (i, j))],
        out_shape=[jax.ShapeDtypeStruct((T, D), BF16), jax.ShapeDtypeStruct((2, T, H), BF16),
                   jax.ShapeDtypeStruct((T, H), act_dtype)],
        scratch_shapes=[pltpu.VMEM((tm, D), BF16)],
        compiler_params=_params("parallel", "arbitrary"),
    )(x, gain, w, w, bias, bias)


def _conv_tiles(T):
    tt = _tile(T, 256)
    return tt, tt // SUBLANES, D // LANES


def _conv_specs(T, tt):
    main = pl.BlockSpec((tt, D), lambda i: (i, 0))
    per = tt // HALO
    prev = pl.BlockSpec((HALO, D), lambda i: (jnp.maximum(i * per - 1, 0), 0))
    nxt = pl.BlockSpec((HALO, D), lambda i: (jnp.minimum((i + 1) * per, T // HALO - 1), 0))
    return main, prev, nxt


def _fill_pad(pad, main_ref, prev_ref, next_ref, i, n_i, tt, nlt):
    keep_p = (i > 0).astype(F32)
    keep_n = (i < n_i - 1).astype(F32)
    for lt in range(nlt):
        sl = slice(lt * LANES, (lt + 1) * LANES)
        pad[lt, 0:HALO, :] = prev_ref[:, sl] * keep_p
        pad[lt, HALO:HALO + tt, :] = main_ref[:, sl]
        pad[lt, HALO + tt:2 * HALO + tt, :] = next_ref[:, sl] * keep_n


def conv_fwd(glu, w_dw, b_dw, ln_g, ln_b):
    T = glu.shape[0]
    tt, L, nlt = _conv_tiles(T)
    n_i = T // tt
    main, prev, nxt = _conv_specs(T, tt)

    def body(x_ref, xp_ref, xn_ref, w_ref, b_ref, g_ref, bb_ref, dwc_ref, sw_ref, pad, ob):
        i = pl.program_id(0)
        _fill_pad(pad, x_ref, xp_ref, xn_ref, i, n_i, tt, nlt)
        for lt in range(nlt):
            sl = slice(lt * LANES, (lt + 1) * LANES)
            p, o = pad.at[lt], ob.at[lt]
            wk = [w_ref[k:k + 1, sl] for k in range(CONV_W)]

            def jbody(j, carry):
                acc = jnp.zeros((SUBLANES, LANES), F32)
                for k in range(CONV_W):
                    acc = acc + p[pl.ds(j + k + 1, SUBLANES, stride=L), :] * wk[k]
                o[pl.ds(j, SUBLANES, stride=L), :] = acc
                return carry

            lax.fori_loop(0, L, jbody, 0)
        y = jnp.concatenate([ob[lt] for lt in range(nlt)], axis=1) + b_ref[...]
        dwc_ref[...] = y
        mu = jnp.mean(y, axis=-1, keepdims=True)
        yc = y - mu
        var = jnp.mean(yc * yc, axis=-1, keepdims=True)
        z = yc * lax.rsqrt(var + EPS) * g_ref[...] + bb_ref[...]
        sw_ref[...] = (z * _sigmoid(z)).astype(BF16)

    full = lambda i: (0, 0)
    return pl.pallas_call(
        body, name="conv_fwd", grid=(n_i,),
        in_specs=[main, prev, nxt, pl.BlockSpec((32, D), full), pl.BlockSpec((1, D), full),
                  pl.BlockSpec((1, D), full), pl.BlockSpec((1, D), full)],
        out_specs=[pl.BlockSpec((tt, D), lambda i: (i, 0)), pl.BlockSpec((tt, D), lambda i: (i, 0))],
        out_shape=[jax.ShapeDtypeStruct((T, D), F32), jax.ShapeDtypeStruct((T, D), BF16)],
        scratch_shapes=[pltpu.VMEM((nlt, tt + 2 * HALO, LANES), F32), pltpu.VMEM((nlt, tt, LANES), F32)],
        compiler_params=_params("parallel"),
    )(glu, glu, glu, w_dw, b_dw, ln_g, ln_b)


def final_loss(x, gain, target):
    T = x.shape[0]
    tm = _tile(T, 512)

    def body(x_ref, g_ref, t_ref, dx_ref, loss_ref, dg_ref):
        @pl.when(pl.program_id(0) == 0)
        def _():
            loss_ref[...] = jnp.zeros_like(loss_ref)
            dg_ref[...] = jnp.zeros_like(dg_ref)

        xv, gain_v = x_ref[...], g_ref[...]
        err = _rms_fwd(xv, gain_v) - t_ref[...]
        part = 0.5 * jnp.sum(jnp.mean(err * err, axis=-1, keepdims=True), axis=0, keepdims=True)
        loss_ref[...] += jnp.broadcast_to(part, loss_ref.shape)
        dx, dgr = _rms_bwd(err * (1.0 / D), xv, gain_v, 0.0)
        dx_ref[...] = dx
        dg_ref[...] += jnp.sum(dgr, axis=0, keepdims=True)

    row = lambda i: (i, 0)
    full = lambda i: (0, 0)
    return pl.pallas_call(
        body, name="final_loss", grid=(T // tm,),
        in_specs=[pl.BlockSpec((tm, D), row), pl.BlockSpec((1, D), full), pl.BlockSpec((tm, D), row)],
        out_specs=[pl.BlockSpec((tm, D), row), pl.BlockSpec((1, LANES), full), pl.BlockSpec((1, D), full)],
        out_shape=[jax.ShapeDtypeStruct((T, D), F32), jax.ShapeDtypeStruct((1, LANES), F32),
                   jax.ShapeDtypeStruct((1, D), F32)],
        compiler_params=_params("arbitrary"),
    )(x, gain, target)


def swiglu_bwd(dx, w_down, pre, name):
    T = dx.shape[0]
    H = w_down.shape[0]
    tm = _tile(T, 512)
    tn = 1408
    nj = H // tn

    def body(dx_ref, w_ref, pre_ref, dpre_ref, dxs):
        @pl.when(pl.program_id(1) == 0)
        def _():
            dxs[...] = dx_ref[...].astype(BF16)

        dact = _dot_tb(dxs[...], w_ref[...])
        g = pre_ref[0].astype(F32)
        u = pre_ref[1].astype(F32)
        sg = _sigmoid(g)
        dpre_ref[0] = (dact * u * sg * (1.0 + g * (1.0 - sg))).astype(BF16)
        dpre_ref[1] = (dact * g * sg).astype(BF16)

    return pl.pallas_call(
        body, name=name, grid=(T // tm, nj),
        in_specs=[pl.BlockSpec((tm, D), lambda i, j: (i, 0)), pl.BlockSpec((tn, D), lambda i, j: (j, 0)),
                  pl.BlockSpec((2, tm, tn), lambda i, j: (0, i, j))],
        out_specs=pl.BlockSpec((2, tm, tn), lambda i, j: (0, i, j)),
        out_shape=jax.ShapeDtypeStruct((2, T, H), BF16),
        scratch_shapes=[pltpu.VMEM((tm, D), BF16)],
        compiler_params=_params("parallel", "arbitrary"),
    )(dx, w_down, pre)


def mm_bt_rmsbwd(dpre, w, x, gain, dres, name):
    nh, T, H = dpre.shape
    tm = _tile(T, 512)
    tk = 1408 if H % 1408 == 0 else (1024 if H % 1024 == 0 else H)
    nk = H // tk

    def body(dp_ref, w_ref, x_ref, g_ref, dres_ref, dx_ref, dg_ref, acc):
        i, hf, kk = pl.program_id(0), pl.program_id(1), pl.program_id(2)

        @pl.when((i == 0) & (hf == 0) & (kk == 0))
        def _():
            dg_ref[...] = jnp.zeros_like(dg_ref)

        @pl.when((hf == 0) & (kk == 0))
        def _():
            acc[...] = jnp.zeros_like(acc)

        acc[...] += _dot_tb(dp_ref[...], w_ref[...])

        @pl.when((hf == nh - 1) & (kk == nk - 1))
        def _():
            dx, dgr = _rms_bwd(acc[...], x_ref[...], g_ref[...], dres_ref[...])
            dx_ref[...] = dx
            dg_ref[...] += jnp.sum(dgr, axis=0, keepdims=True)

    return pl.pallas_call(
        body, name=name, grid=(T // tm, nh, nk),
        in_specs=[pl.BlockSpec((None, tm, tk), lambda i, hf, kk: (hf, i, kk)),
                  pl.BlockSpec((D, tk), lambda i, hf, kk: (0, hf * nk + kk)),
                  pl.BlockSpec((tm, D), lambda i, hf, kk: (i, 0)), pl.BlockSpec((1, D), lambda i, hf, kk: (0, 0)),
                  pl.BlockSpec((tm, D), lambda i, hf, kk: (i, 0))],
        out_specs=[pl.BlockSpec((tm, D), lambda i, hf, kk: (i, 0)), pl.BlockSpec((1, D), lambda i, hf, kk: (0, 0))],
        out_shape=[jax.ShapeDtypeStruct((T, D), F32), jax.ShapeDtypeStruct((1, D), F32)],
        scratch_shapes=[pltpu.VMEM((tm, D), F32)],
        compiler_params=_params("arbitrary", "arbitrary", "arbitrary"),
    )(dpre, w, x, gain, dres)


def dw_col(a, dpre, name):
    T = a.shape[0]
    nh, _, H = dpre.shape
    per = nh * H // N_CHIPS
    bph = N_CHIPS // nh
    tt = _tile(T, 512)
    nt = T // tt

    def body(a_ref, b_ref, o_ref, acc):
        t = pl.program_id(1)

        @pl.when(t == 0)
        def _():
            acc[...] = jnp.zeros_like(acc)

        acc[...] += _dot_ta(a_ref[...], b_ref[...])

        @pl.when(t == nt - 1)
        def _():
            o_ref[...] = acc[...].astype(BF16)

    return pl.pallas_call(
        body, name=name, grid=(N_CHIPS, nt),
        in_specs=[pl.BlockSpec((tt, D), lambda q, t: (t, 0)),
                  pl.BlockSpec((None, tt, per), lambda q, t: (q // bph, t, q % bph))],
        out_specs=pl.BlockSpec((None, D, per), lambda q, t: (q, 0, 0)),
        out_shape=jax.ShapeDtypeStruct((N_CHIPS, D, per), BF16),
        scratch_shapes=[pltpu.VMEM((D, per), F32)],
        compiler_params=_params("parallel", "arbitrary"),
    )(a, dpre)


def dw_row(a, b, name):
    T, R = a.shape
    cw = 1408 if R % 1408 == 0 else 512
    tt = _tile(T, 512)
    nt = T // tt

    def body(a_ref, b_ref, o_ref, acc):
        t = pl.program_id(1)

        @pl.when(t == 0)
        def _():
            acc[...] = jnp.zeros_like(acc)

        acc[...] += _dot_ta(a_ref[...], b_ref[...].astype(BF16))

        @pl.when(t == nt - 1)
        def _():
            o_ref[...] = acc[...].astype(BF16)

    out = pl.pallas_call(
        body, name=name, grid=(R // cw, nt),
        in_specs=[pl.BlockSpec((tt, cw), lambda q, t: (t, q)), pl.BlockSpec((tt, D), lambda q, t: (t, 0))],
        out_specs=pl.BlockSpec((cw, D), lambda q, t: (q, 0)),
        out_shape=jax.ShapeDtypeStruct((R, D), BF16),
        scratch_shapes=[pltpu.VMEM((cw, D), F32)],
        compiler_params=_params("parallel", "arbitrary"),
    )(a, b)
    return out.reshape(N_CHIPS, R // N_CHIPS, D)


def ln_silu_bwd(dx, w_pw2, dwc, ln_g, ln_b):
    T = dx.shape[0]
    tm = _tile(T, 512)

    def body(dx_ref, w_ref, y_ref, g_ref, b_ref, dy_ref, dg_ref, db_ref, dbo_ref):
        @pl.when(pl.program_id(0) == 0)
        def _():
            dg_ref[...] = jnp.zeros_like(dg_ref)
            db_ref[...] = jnp.zeros_like(db_ref)
            dbo_ref[...] = jnp.zeros_like(dbo_ref)

        dxv = dx_ref[...]
        dsw = _dot_tb(dxv.astype(BF16), w_ref[...])
        y = y_ref[...]
        mu = jnp.mean(y, axis=-1, keepdims=True)
        yc = y - mu
        rstd = lax.rsqrt(jnp.mean(yc * yc, axis=-1, keepdims=True) + EPS)
        xhat = yc * rstd
        z = xhat * g_ref[...] + b_ref[...]
        sg = _sigmoid(z)
        dz = dsw * sg * (1.0 + z * (1.0 - sg))
        dxh = dz * g_ref[...]
        dy_ref[...] = rstd * (dxh - jnp.mean(dxh, axis=-1, keepdims=True)
                              - xhat * jnp.mean(dxh * xhat, axis=-1, keepdims=True))
        dg_ref[...] += jnp.sum(dz * xhat, axis=0, keepdims=True)
        db_ref[...] += jnp.sum(dz, axis=0, keepdims=True)
        dbo_ref[...] += jnp.sum(dxv, axis=0, keepdims=True)

    row = lambda i: (i, 0)
    full = lambda i: (0, 0)
    vec = pl.BlockSpec((1, D), full)
    return pl.pallas_call(
        body, name="ln_silu_bwd", grid=(T // tm,),
        in_specs=[pl.BlockSpec((tm, D), row), pl.BlockSpec((D, D), full), pl.BlockSpec((tm, D), row), vec, vec],
        out_specs=[pl.BlockSpec((tm, D), row), vec, vec, vec],
        out_shape=[jax.ShapeDtypeStruct((T, D), F32)] + [jax.ShapeDtypeStruct((1, D), F32)] * 3,
        compiler_params=_params("arbitrary"),
    )(dx, w_pw2, dwc, ln_g, ln_b)


def conv_bwd(ddwc, glu, pre, w_dw):
    T = ddwc.shape[0]
    tt, L, nlt = _conv_tiles(T)
    n_i = T // tt
    main, prev, nxt = _conv_specs(T, tt)

    def body(d_ref, dp_ref, dn_ref, x_ref, xp_ref, xn_ref, pre_ref, w_ref,
             dpre_ref, dw_ref, dbd_ref, dbp_ref, padd, padx, ob):
        i = pl.program_id(0)

        @pl.when(i == 0)
        def _():
            dw_ref[...] = jnp.zeros_like(dw_ref)
            dbd_ref[...] = jnp.zeros_like(dbd_ref)
            dbp_ref[...] = jnp.zeros_like(dbp_ref)

        _fill_pad(padd, d_ref, dp_ref, dn_ref, i, n_i, tt, nlt)
        _fill_pad(padx, x_ref, xp_ref, xn_ref, i, n_i, tt, nlt)
        for lt in range(nlt):
            sl = slice(lt * LANES, (lt + 1) * LANES)
            pd, px, o = padd.at[lt], padx.at[lt], ob.at[lt]
            wk = [w_ref[k:k + 1, sl] for k in range(CONV_W)]

            def jbody(j, accs):
                dcur = pd[pl.ds(j + HALO, SUBLANES, stride=L), :]
                dglu = jnp.zeros((SUBLANES, LANES), F32)
                new = []
                for k in range(CONV_W):
                    dglu = dglu + pd[pl.ds(j + 2 * HALO - 1 - k, SUBLANES, stride=L), :] * wk[k]
                    new.append(accs[k] + dcur * px[pl.ds(j + k + 1, SUBLANES, stride=L), :])
                o[pl.ds(j, SUBLANES, stride=L), :] = dglu
                return tuple(new)

            accs = lax.fori_loop(0, L, jbody, tuple(jnp.zeros((SUBLANES, LANES), F32) for _ in range(CONV_W)))
            for k in range(CONV_W):
                dw_ref[k:k + 1, sl] += jnp.sum(accs[k], axis=0, keepdims=True)
        dglu = jnp.concatenate([ob[lt] for lt in range(nlt)], axis=1)
        a = pre_ref[0].astype(F32)
        gate = pre_ref[1].astype(F32)
        sg = _sigmoid(gate)
        da = dglu * sg
        dgate = dglu * a * sg * (1.0 - sg)
        dpre_ref[0] = da.astype(BF16)
        dpre_ref[1] = dgate.astype(BF16)
        dbd_ref[...] += jnp.sum(d_ref[...], axis=0, keepdims=True)
        dbp_ref[0] += jnp.sum(da, axis=0, keepdims=True)
        dbp_ref[1] += jnp.sum(dgate, axis=0, keepdims=True)

    full = lambda i: (0, 0)
    return pl.pallas_call(
        body, name="conv_bwd", grid=(n_i,),
        in_specs=[main, prev, nxt, main, prev, nxt, pl.BlockSpec((2, tt, D), lambda i: (0, i, 0)),
                  pl.BlockSpec((32, D), full)],
        out_specs=[pl.BlockSpec((2, tt, D), lambda i: (0, i, 0)), pl.BlockSpec((32, D), full),
                   pl.BlockSpec((1, D), full), pl.BlockSpec((2, 1, D), lambda i: (0, 0, 0))],
        out_shape=[jax.ShapeDtypeStruct((2, T, D), BF16), jax.ShapeDtypeStruct((32, D), F32),
                   jax.ShapeDtypeStruct((1, D), F32), jax.ShapeDtypeStruct((2, 1, D), F32)],
        scratch_shapes=[pltpu.VMEM((nlt, tt + 2 * HALO, LANES), F32), pltpu.VMEM((nlt, tt + 2 * HALO, LANES), F32),
                        pltpu.VMEM((nlt, tt, LANES), F32)],
        compiler_params=_params("arbitrary"),
    )(ddwc, ddwc, ddwc, glu, glu, glu, pre, w_dw)


def mm_bt(a, w, name):
    T = a.shape[0]
    N = w.shape[0]
    tm = _tile(T, 512)

    def body(a_ref, w_ref, o_ref):
        o_ref[...] = _dot_tb(a_ref[...].astype(BF16), w_ref[...]).astype(BF16)

    return pl.pallas_call(
        body, name=name, grid=(T // tm,),
        in_specs=[pl.BlockSpec((tm, D), lambda i: (i, 0)), pl.BlockSpec((N, D), lambda i: (0, 0))],
        out_specs=pl.BlockSpec((tm, N), lambda i: (i, 0)),
        out_shape=jax.ShapeDtypeStruct((T, N), BF16),
        compiler_params=_params("parallel"),
    )(a, w)


def attn_bwd(qkv, o, do, sink, rc, rs1, rs2):
    T = qkv.shape[0]
    nb, q_spec, prev, own, nxt = _attn_specs(T)
    scale = 1.0 / math.sqrt(HD)
    kvw = N_KV * HD

    def body(sink_ref, q_ref, kp_ref, ko_ref, kn_ref, o_ref, do_ref, c_ref, s1_ref, s2_ref,
             dq_ref, dkc_ref, dvc_ref, dsink_ref, dqs):
        n = pl.program_id(0)

        @pl.when(n == 0)
        def _():
            dsink_ref[...] = jnp.zeros_like(dsink_ref)

        valid = _attn_mask(n, T)
        kv = jnp.concatenate([kp_ref[...], ko_ref[...], kn_ref[...]], axis=0)
        lane = lax.broadcasted_iota(jnp.int32, (1, N_HEADS), 1)
        dsink = jnp.zeros((1, N_HEADS), F32)
        for g in range(N_KV):
            k = kv[:, g * HD:(g + 1) * HD]
            v = kv[:, kvw + g * HD:kvw + (g + 1) * HD]
            dk = jnp.zeros((3 * BLK, HD), F32)
            dv = jnp.zeros((3 * BLK, HD), F32)
            for hg in range(GROUP):
                h = g * GROUP + hg
                hs = slice(h * HD, (h + 1) * HD)
                q = q_ref[:, hs]
                doh = do_ref[:, hs]
                s = jnp.where(valid, _dot_tb(q, k) * scale, NEG)
                sk = sink_ref[h]
                m = jnp.maximum(jnp.max(s, axis=-1, keepdims=True), sk)
                e = jnp.exp(s - m)
                inv = 1.0 / (jnp.sum(e, axis=-1, keepdims=True) + jnp.exp(sk - m))
                p = e * inv
                delta = jnp.sum(doh.astype(F32) * o_ref[:, hs].astype(F32), axis=-1, keepdims=True)
                dp = _dot_tb(doh, v)
                ds = (p * (dp - delta) * scale).astype(BF16)
                pb = p.astype(BF16)
                dqs[:, hs] = _dot(ds, k)
                dk = dk + _dot_ta(ds, q)
                dv = dv + _dot_ta(pb, doh)
                dsk = -jnp.sum(jnp.exp(sk - m) * inv * delta, axis=0, keepdims=True)
                dsink = dsink + jnp.where(lane == h, dsk, 0.0)
            dkc_ref[:, g * HD:(g + 1) * HD] = dk
            dvc_ref[:, g * HD:(g + 1) * HD] = dv
        dsink_ref[...] += dsink
        c, s1, s2 = c_ref[...], s1_ref[...], s2_ref[...]
        for j in range(KV_OFF // LANES):
            sl = slice(LANES * j, LANES * (j + 1))
            dq_ref[:, sl] = _rope(dqs[:, sl], c, -s1, -s2).astype(BF16)

    row = lambda n: (n, 0)
    tab = pl.BlockSpec((BLK, LANES), row)
    return pl.pallas_call(
        body, name="attn_bwd", grid=(nb,),
        in_specs=[pl.BlockSpec(memory_space=pltpu.SMEM), q_spec, prev, own, nxt,
                  pl.BlockSpec((BLK, D), row), pl.BlockSpec((BLK, D), row), tab, tab, tab],
        out_specs=[pl.BlockSpec((BLK, D), row), pl.BlockSpec((None, 3 * BLK, kvw), lambda n: (n, 0, 0)),
                   pl.BlockSpec((None, 3 * BLK, kvw), lambda n: (n, 0, 0)), pl.BlockSpec((1, N_HEADS), lambda n: (0, 0))],
        out_shape=[jax.ShapeDtypeStruct((T, D), BF16), jax.ShapeDtypeStruct((nb, 3 * BLK, kvw), F32),
                   jax.ShapeDtypeStruct((nb, 3 * BLK, kvw), F32), jax.ShapeDtypeStruct((1, N_HEADS), F32)],
        scratch_shapes=[pltpu.VMEM((BLK, D), F32)],
        compiler_params=_params("arbitrary"),
    )(sink, qkv, qkv, qkv, qkv, o, do, rc, rs1, rs2)


def kv_sum(dkc, dvc, rc, rs1, rs2):
    nb = dkc.shape[0]
    T = nb * BLK
    kvw = N_KV * HD

    def body(kp_ref, ko_ref, kn_ref, vp_ref, vo_ref, vn_ref, c_ref, s1_ref, s2_ref, out_ref):
        m = pl.program_id(0)
        has_p = (m > 0).astype(F32)
        has_n = (m < nb - 1).astype(F32)
        dk = kp_ref[...] * has_p + ko_ref[...] + kn_ref[...] * has_n
        dv = vp_ref[...] * has_p + vo_ref[...] + vn_ref[...] * has_n
        c, s1, s2 = c_ref[...], s1_ref[...], s2_ref[...]
        for j in range(kvw // LANES):
            sl = slice(LANES * j, LANES * (j + 1))
            out_ref[:, sl] = _rope(dk[:, sl], c, -s1, -s2).astype(BF16)
        out_ref[:, kvw:] = dv.astype(BF16)

    from_prev = pl.BlockSpec((None, BLK, kvw), lambda m: (jnp.maximum(m - 1, 0), 2, 0))
    from_own = pl.BlockSpec((None, BLK, kvw), lambda m: (m, 1, 0))
    from_next = pl.BlockSpec((None, BLK, kvw), lambda m: (jnp.minimum(m + 1, nb - 1), 0, 0))
    tab = pl.BlockSpec((BLK, LANES), lambda m: (m, 0))
    return pl.pallas_call(
        body, name="kv_sum", grid=(nb,),
        in_specs=[from_prev, from_own, from_next, from_prev, from_own, from_next, tab, tab, tab],
        out_specs=pl.BlockSpec((BLK, 2 * kvw), lambda m: (m, 0)),
        out_shape=jax.ShapeDtypeStruct((T, 2 * kvw), BF16),
        compiler_params=_params("parallel"),
    )(dkc, dkc, dkc, dvc, dvc, dvc, rc, rs1, rs2)


def _me():
    return lax.axis_index("x"), lax.axis_index("y"), lax.axis_index("c")


def _half_rows(ref, sharded_rows, chip, core):
    R, C = ref.shape[-2], ref.shape[-1]
    lead = (slice(None),) * (len(ref.shape) - 2)
    if sharded_rows:
        per = R // N_CHIPS
        return ref.at[lead + (pl.ds(chip * per + core * (per // 2), per // 2), slice(None))]
    per = C // N_CHIPS
    return ref.at[lead + (pl.ds(core * (R // 2), R // 2), pl.ds(chip * per, per))]


def gather_weights(shards, sharded_rows):
    n = len(shards)
    full_shapes = []
    for s, rows in zip(shards, sharded_rows):
        shp = list(s.shape)
        shp[-2 if rows else -1] *= N_CHIPS
        full_shapes.append(jax.ShapeDtypeStruct(tuple(shp), s.dtype))

    def body(*refs):
        ins, outs = refs[:n], refs[n:2 * n]
        send_sems, recv_sems, local_sems = refs[2 * n:]
        x, y, c = _me()
        me_chip = 2 * x + y
        sibling = (x, y, 1 - c)
        chips = [(1 - x, y), (x, 1 - y), (1 - x, 1 - y)]

        def half_src(w, core):
            s = ins[w]
            R = s.shape[-2]
            lead = (slice(None),) * (len(s.shape) - 2)
            return s.at[lead + (pl.ds(core * (R // 2), R // 2), slice(None))]

        def copy(w, k, src, chip, core, to):
            return pltpu.make_async_remote_copy(
                src_ref=src, dst_ref=_half_rows(outs[w], sharded_rows[w], chip, core),
                send_sem=send_sems.at[w, k], recv_sem=recv_sems.at[w, k], device_id=to, device_id_type=MESH)

        local = []
        for w in range(n):
            for core in range(2):
                cp = pltpu.make_async_copy(half_src(w, core), _half_rows(outs[w], sharded_rows[w], me_chip, core),
                                           local_sems.at[w, core])
                cp.start()
                local.append(cp)
        first = []
        for w in range(n):
            for j, (qx, qy) in enumerate(chips):
                cp = copy(w, j, half_src(w, c), me_chip, c, (qx, qy, c))
                cp.start()
                first.append(cp)
        passed = []
        for w in range(n):
            for j, (qx, qy) in enumerate(chips):
                q = 2 * qx + qy
                landed = _half_rows(outs[w], sharded_rows[w], q, c)
                copy(w, j, landed, q, c, (x, y, c)).wait_recv()
                cp = copy(w, 3 + j, landed, q, c, sibling)
                cp.start()
                passed.append(cp)
        for w in range(n):
            for j, (qx, qy) in enumerate(chips):
                q = 2 * qx + qy
                copy(w, 3 + j, _half_rows(outs[w], sharded_rows[w], q, 1 - c), q, 1 - c, (x, y, c)).wait_recv()
        for cp in first + passed:
            cp.wait_send()
        for cp in local:
            cp.wait()

    any_spec = pl.BlockSpec(memory_space=pl.ANY)
    return pl.pallas_call(
        body, name="gather_weights",
        in_specs=[any_spec] * n, out_specs=[any_spec] * n, out_shape=full_shapes,
        scratch_shapes=[pltpu.SemaphoreType.DMA((n, 6)), pltpu.SemaphoreType.DMA((n, 6)),
                        pltpu.SemaphoreType.DMA((n, 2))],
    )(*shards)


def scatter_grads(grads, small, name):
    n = len(grads)
    out_shapes = [jax.ShapeDtypeStruct((N_DEV, g.shape[1] // 2, g.shape[2]), g.dtype) for g in grads]
    out_shapes.append(jax.ShapeDtypeStruct((N_DEV,) + small.shape, small.dtype))

    def body(*refs):
        ins, outs = refs[:n + 1], refs[n + 1:2 * n + 2]
        send_sems, recv_sems, local_sems = refs[2 * n + 2:]
        x, y, c = _me()
        me = 4 * x + 2 * y + c

        def piece(w, chip, core):
            if w == n:
                return ins[w]
            half = ins[w].shape[1] // 2
            return ins[w].at[chip, pl.ds(core * half, half), :]

        sends, local = [], []
        for w in range(n + 1):
            cp = pltpu.make_async_copy(piece(w, 2 * x + y, c), outs[w].at[me], local_sems.at[w])
            cp.start()
            local.append(cp)
            for k in range(1, N_DEV):
                fx, fy, fc = (k >> 2) & 1, (k >> 1) & 1, k & 1
                px, py, pc = x ^ fx, y ^ fy, c ^ fc
                cp = pltpu.make_async_remote_copy(
                    src_ref=piece(w, 2 * px + py, pc), dst_ref=outs[w].at[me],
                    send_sem=send_sems.at[w, k], recv_sem=recv_sems.at[w, k],
                    device_id=(px, py, pc), device_id_type=MESH)
                cp.start()
                sends.append(cp)
        for w in range(n + 1):
            for k in range(1, N_DEV):
                fx, fy, fc = (k >> 2) & 1, (k >> 1) & 1, k & 1
                px, py, pc = x ^ fx, y ^ fy, c ^ fc
                peer = 4 * px + 2 * py + pc
                pltpu.make_async_remote_copy(
                    src_ref=piece(w, 2 * x + y, c), dst_ref=outs[w].at[peer],
                    send_sem=send_sems.at[w, k], recv_sem=recv_sems.at[w, k],
                    device_id=(px, py, pc), device_id_type=MESH).wait_recv()
        for cp in sends:
            cp.wait_send()
        for cp in local:
            cp.wait()

    any_spec = pl.BlockSpec(memory_space=pl.ANY)
    return pl.pallas_call(
        body, name=name,
        in_specs=[any_spec] * (n + 1), out_specs=[any_spec] * (n + 1), out_shape=out_shapes,
        scratch_shapes=[pltpu.SemaphoreType.DMA((n + 1, N_DEV)), pltpu.SemaphoreType.DMA((n + 1, N_DEV)),
                        pltpu.SemaphoreType.DMA((n + 1,))],
    )(*grads, small)


def swap_halves(halves, layers):
    n = len(halves)
    out_shapes, owner = [], []
    i = 0
    for nl in layers:
        r2, cc = halves[i].shape
        out_shapes.append(jax.ShapeDtypeStruct((nl, 2 * r2, cc), F32))
        owner += [(len(out_shapes) - 1, l) for l in range(nl)]
        i += nl
    n_out = len(out_shapes)

    def body(*refs):
        ins, outs = refs[:n], refs[n:n + n_out]
        send_sems, recv_sems, local_sems = refs[n + n_out:]
        x, y, c = _me()

        def dst(w, core):
            o, l = owner[w]
            r2 = ins[w].shape[0]
            return outs[o].at[l, pl.ds(core * r2, r2), :]

        cps = []
        for w in range(n):
            lc = pltpu.make_async_copy(ins[w], dst(w, c), local_sems.at[w])
            lc.start()
            rc = pltpu.make_async_remote_copy(src_ref=ins[w], dst_ref=dst(w, c), send_sem=send_sems.at[w],
                                              recv_sem=recv_sems.at[w], device_id=(x, y, 1 - c), device_id_type=MESH)
            rc.start()
            cps.append((lc, rc))
        for w in range(n):
            pltpu.make_async_remote_copy(src_ref=ins[w], dst_ref=dst(w, 1 - c), send_sem=send_sems.at[w],
                                         recv_sem=recv_sems.at[w], device_id=(x, y, 1 - c),
                                         device_id_type=MESH).wait_recv()
        for lc, rc in cps:
            rc.wait_send()
            lc.wait()

    any_spec = pl.BlockSpec(memory_space=pl.ANY)
    return pl.pallas_call(
        body, name="swap_halves",
        in_specs=[any_spec] * n, out_specs=[any_spec] * n_out, out_shape=out_shapes,
        scratch_shapes=[pltpu.SemaphoreType.DMA((n,)), pltpu.SemaphoreType.DMA((n,)), pltpu.SemaphoreType.DMA((n,))],
    )(*halves)


def sum_pieces(pieces, name):
    _, R, C = pieces.shape
    tr = _tile(R, 128) if R % 128 == 0 else R

    def body(p_ref, o_ref):
        acc = p_ref[0].astype(F32)
        for d in range(1, N_DEV):
            acc = acc + p_ref[d].astype(F32)
        o_ref[...] = acc

    return pl.pallas_call(
        body, name=name, grid=(R // tr,),
        in_specs=[pl.BlockSpec((N_DEV, tr, C), lambda i: (0, i, 0))],
        out_specs=pl.BlockSpec((tr, C), lambda i: (i, 0)),
        out_shape=jax.ShapeDtypeStruct((R, C), F32),
        compiler_params=_params("parallel"),
    )(pieces)


def adamw(w, g, m, v, name):
    Lyr, R, C = w.shape
    tr = _tile(R, 256) if R % 8 == 0 else R
    c1 = 1.0 / (1.0 - ADAM_B1 ** ADAM_STEP)
    c2 = 1.0 / (1.0 - ADAM_B2 ** ADAM_STEP)

    def body(w_ref, g_ref, m_ref, v_ref, d_ref, nm_ref, nv_ref):
        gv = g_ref[...]
        nm = ADAM_B1 * m_ref[...] + (1.0 - ADAM_B1) * gv
        nv = ADAM_B2 * v_ref[...] + (1.0 - ADAM_B2) * (gv * gv)
        nm_ref[...] = nm
        nv_ref[...] = nv
        d_ref[...] = -ADAM_LR * ((nm * c1) / (jnp.sqrt(nv * c2) + ADAM_EPS) + ADAM_WD * w_ref[...])

    spec = pl.BlockSpec((None, tr, C), lambda l, i: (l, i, 0))
    shp = jax.ShapeDtypeStruct(w.shape, F32)
    return pl.pallas_call(
        body, name=name, grid=(Lyr, R // tr),
        in_specs=[spec] * 4, out_specs=[spec] * 3, out_shape=[shp] * 3,
        compiler_params=_params("parallel", "parallel"),
    )(w, g, m, v)


def _rope_tables(T):
    pos = jnp.arange(T, dtype=F32)
    inv_freq = THETA ** (-jnp.arange(0, ROT, 2, dtype=F32) / ROT)
    ang = pos[:, None] * inv_freq[None, :]
    cos, sin = jnp.cos(ang), jnp.sin(ang)
    half = ROT // 2
    one = jnp.ones((T, HD - ROT), F32)
    zero = jnp.zeros((T, HD - ROT), F32)
    zh = jnp.zeros((T, half), F32)
    c = jnp.concatenate([cos, cos, one], axis=1)
    s1 = jnp.concatenate([-sin, zh, zero], axis=1)
    s2 = jnp.concatenate([zh, sin, zero], axis=1)
    two = lambda t: jnp.concatenate([t, t], axis=1)
    return two(c), two(s1), two(s2)


def kernel(x, attn_norm, attn_w_qkv, attn_w_o, attn_sink, conv_norm, conv_w_pw1, conv_b_pw1, conv_w_dw, conv_b_dw, conv_ln_g, conv_ln_b, conv_w_pw2, conv_b_pw2, ffn_norm, ffn_w_gu, ffn_w_down, final_norm, loss_target, m_attn_norm, m_attn_w_qkv, m_attn_w_o, m_attn_sink, m_conv_norm, m_conv_w_pw1, m_conv_b_pw1, m_conv_w_dw, m_conv_b_dw, m_conv_ln_g, m_conv_ln_b, m_conv_w_pw2, m_conv_b_pw2, m_ffn_norm, m_ffn_w_gu, m_ffn_w_down, m_final_norm, v_attn_norm, v_attn_w_qkv, v_attn_w_o, v_attn_sink, v_conv_norm, v_conv_w_pw1, v_conv_b_pw1, v_conv_w_dw, v_conv_b_dw, v_conv_ln_g, v_conv_ln_b, v_conv_w_pw2, v_conv_b_pw2, v_ffn_norm, v_ffn_w_gu, v_ffn_w_down, v_final_norm):
    T = x.shape[1]
    x0 = x[0]
    target = loss_target[0]
    ix, iy = lax.axis_index("x"), lax.axis_index("y")
    chip = 2 * ix + iy
    rc, rs1, rs2 = _rope_tables(T)

    big_shards = [attn_w_qkv[0].astype(BF16), attn_w_o[0].astype(BF16), ffn_w_gu.astype(BF16),
                  ffn_w_down.astype(BF16), conv_w_pw1[0].astype(BF16), conv_w_pw2[0].astype(BF16)]
    w_qkv, w_o, w_gu, w_down, w_pw1, w_pw2 = gather_weights(big_shards, [False, True, False, True, False, True])

    def place(vec, width):
        return lax.dynamic_update_slice(jnp.zeros((vec.shape[0], N_CHIPS * width), F32), vec, (0, chip * width))

    small_rows = jnp.concatenate([
        place(conv_norm, 256), place(conv_b_pw1, 512).reshape(2, D), place(conv_b_dw, 256), place(conv_ln_g, 256),
        place(conv_ln_b, 256), place(conv_b_pw2, 256), jnp.zeros((1, D), F32),
        place(conv_w_dw[0], 256), jnp.zeros((1, D), F32)], axis=0)
    zero_g = [jnp.zeros((N_CHIPS, 32, LANES), BF16)]
    got = scatter_grads(zero_g, small_rows, "gather_small_params")[1]
    psmall = sum_pieces(got, "sum_small_params") * 0.5
    p_conv_norm, p_b_pw1 = psmall[0:1], psmall[1:3].reshape(1, 2 * D)
    p_b_dw, p_ln_g, p_ln_b, p_b_pw2 = psmall[3:4], psmall[4:5], psmall[5:6], psmall[6:7]
    p_w_dw = psmall[8:40]

    h0, qkv = rms_qkv(x0, attn_norm, w_qkv, rc, rs1, rs2)
    sink = attn_sink[0]
    o = attn_fwd(qkv, sink)
    zero_b = jnp.zeros((1, D), F32)
    x1 = mm_res(o, w_o, x0, zero_b, "attn_out")
    zero_gu = jnp.zeros((1, 2 * DFF), F32)
    h1, gu0, act0 = rms_mm_gate(x1, ffn_norm[0:1], w_gu[0], zero_gu, DFF, True, BF16, "ffn0_up")
    x2 = mm_res(act0, w_down[0], x1, zero_b, "ffn0_down")
    h2, pre, glu = rms_mm_gate(x2, p_conv_norm, w_pw1, p_b_pw1, D, False, F32, "conv_pw1")
    dwc, sw = conv_fwd(glu, p_w_dw, p_b_dw, p_ln_g, p_ln_b)
    x3 = mm_res(sw, w_pw2, x2, p_b_pw2, "conv_pw2")
    h3, gu1, act1 = rms_mm_gate(x3, ffn_norm[1:2], w_gu[1], zero_gu, DFF, True, BF16, "ffn1_up")
    x4 = mm_res(act1, w_down[1], x3, zero_b, "ffn1_down")
    dx4, loss_part, d_final = final_loss(x4, final_norm.reshape(1, D), target)
    loss = lax.psum(loss_part[0, 0], ("x", "y", "c"))

    dgu1 = swiglu_bwd(dx4, w_down[1], gu1, "ffn1_down_bwd")
    g_down1 = dw_row(act1, dx4, "ffn1_down_dw")
    dx3, d_ffn1 = mm_bt_rmsbwd(dgu1, w_gu[1], x3, ffn_norm[1:2], dx4, "ffn1_up_bwd")
    g_gu1 = dw_col(h3, dgu1, "ffn1_up_dw")

    ddwc, d_ln_g, d_ln_b, d_b_pw2 = ln_silu_bwd(dx3, w_pw2, dwc, p_ln_g, p_ln_b)
    g_pw2 = dw_row(sw, dx3, "conv_pw2_dw")
    dpre, d_w_dw, d_b_dw, d_b_pw1 = conv_bwd(ddwc, glu, pre, p_w_dw)
    dx2, d_conv_norm = mm_bt_rmsbwd(dpre, w_pw1, x2, p_conv_norm, dx3, "conv_pw1_bwd")
    g_pw1 = dw_col(h2, dpre, "conv_pw1_dw")

    dgu0 = swiglu_bwd(dx2, w_down[0], gu0, "ffn0_down_bwd")
    g_down0 = dw_row(act0, dx2, "ffn0_down_dw")
    dx1, d_ffn0 = mm_bt_rmsbwd(dgu0, w_gu[0], x1, ffn_norm[0:1], dx2, "ffn0_up_bwd")
    g_gu0 = dw_col(h1, dgu0, "ffn0_up_dw")

    do = mm_bt(dx1, w_o, "attn_out_bwd")
    g_o = dw_row(o, dx1, "attn_out_dw")
    dq, dkc, dvc, d_sink = attn_bwd(qkv, o, do, sink, rc, rs1, rs2)
    dkv = kv_sum(dkc, dvc, rc, rs1, rs2)
    dqkv = jnp.concatenate([dq, dkv], axis=1)[None]
    dx0, d_attn_norm = mm_bt_rmsbwd(dqkv, w_qkv, x0, attn_norm, dx1, "attn_qkv_bwd")
    g_qkv = dw_col(h0, dqkv, "attn_qkv_dw")

    pad16 = lambda t: jnp.concatenate([t, jnp.zeros((1, D - t.shape[1]), F32)], axis=1)
    small_g = jnp.concatenate([
        d_attn_norm, pad16(d_sink), d_conv_norm, d_b_pw1.reshape(2, D), d_b_dw, d_ln_g, d_ln_b, d_b_pw2,
        d_ffn0, d_ffn1, d_final, jnp.zeros((4, D), F32), d_w_dw], axis=0)
    big_g = [g_qkv, g_o, g_gu0, g_gu1, g_down0, g_down1, g_pw1, g_pw2]
    names = ["qkv", "o", "gu0", "gu1", "down0", "down1", "pw1", "pw2"]
    got = scatter_grads(big_g, small_g, "scatter_grads")
    halves = [sum_pieces(p, "sum_" + nm) for p, nm in zip(got[:-1], names)]
    gs = sum_pieces(got[-1], "sum_small_grads")
    gf_qkv, gf_o, gf_gu, gf_down, gf_pw1, gf_pw2 = swap_halves(halves, [1, 1, 2, 2, 1, 1])

    def take(row0, nrows, width):
        return lax.dynamic_slice(gs, (row0, chip * width), (nrows, width))

    grads = {
        "attn_norm": gs[0:1], "attn_w_qkv": gf_qkv, "attn_w_o": gf_o, "attn_sink": gs[1:2, :N_HEADS],
        "conv_norm": take(2, 1, 256), "conv_w_pw1": gf_pw1,
        "conv_b_pw1": lax.dynamic_slice(gs[3:5].reshape(1, 2 * D), (0, chip * 512), (1, 512)),
        "conv_w_dw": take(16, 32, 256)[None, :CONV_W], "conv_b_dw": take(5, 1, 256), "conv_ln_g": take(6, 1, 256),
        "conv_ln_b": take(7, 1, 256), "conv_w_pw2": gf_pw2, "conv_b_pw2": take(8, 1, 256),
        "ffn_norm": gs[9:11], "ffn_w_gu": gf_gu, "ffn_w_down": gf_down, "final_norm": gs[11],
    }
    weights = dict(attn_norm=attn_norm, attn_w_qkv=attn_w_qkv, attn_w_o=attn_w_o, attn_sink=attn_sink,
                   conv_norm=conv_norm, conv_w_pw1=conv_w_pw1, conv_b_pw1=conv_b_pw1, conv_w_dw=conv_w_dw,
                   conv_b_dw=conv_b_dw, conv_ln_g=conv_ln_g, conv_ln_b=conv_ln_b, conv_w_pw2=conv_w_pw2,
                   conv_b_pw2=conv_b_pw2, ffn_norm=ffn_norm, ffn_w_gu=ffn_w_gu, ffn_w_down=ffn_w_down,
                   final_norm=final_norm)
    m_in = dict(attn_norm=m_attn_norm, attn_w_qkv=m_attn_w_qkv, attn_w_o=m_attn_w_o, attn_sink=m_attn_sink,
                conv_norm=m_conv_norm, conv_w_pw1=m_conv_w_pw1, conv_b_pw1=m_conv_b_pw1, conv_w_dw=m_conv_w_dw,
                conv_b_dw=m_conv_b_dw, conv_ln_g=m_conv_ln_g, conv_ln_b=m_conv_ln_b, conv_w_pw2=m_conv_w_pw2,
                conv_b_pw2=m_conv_b_pw2, ffn_norm=m_ffn_norm, ffn_w_gu=m_ffn_w_gu, ffn_w_down=m_ffn_w_down,
                final_norm=m_final_norm)
    v_in = dict(attn_norm=v_attn_norm, attn_w_qkv=v_attn_w_qkv, attn_w_o=v_attn_w_o, attn_sink=v_attn_sink,
                conv_norm=v_conv_norm, conv_w_pw1=v_conv_w_pw1, conv_b_pw1=v_conv_b_pw1, conv_w_dw=v_conv_w_dw,
                conv_b_dw=v_conv_b_dw, conv_ln_g=v_conv_ln_g, conv_ln_b=v_conv_ln_b, conv_w_pw2=v_conv_w_pw2,
                conv_b_pw2=v_conv_b_pw2, ffn_norm=v_ffn_norm, ffn_w_gu=v_ffn_w_gu, ffn_w_down=v_ffn_w_down,
                final_norm=v_final_norm)
    order = list(weights)
    g_out, d_out, m_out, v_out = [], [], [], []
    for nm in order:
        w = weights[nm]
        shape = w.shape
        as3 = lambda t: t.reshape((1,) * (3 - len(shape)) + shape) if len(shape) < 3 else t.reshape(shape)
        g3 = as3(grads[nm].reshape(shape))
        delta, nm_, nv_ = adamw(as3(w), g3, as3(m_in[nm]), as3(v_in[nm]), "adamw_" + nm)
        g_out.append(g3.reshape(shape))
        d_out.append(delta.reshape(shape))
        m_out.append(nm_.reshape(shape))
        v_out.append(nv_.reshape(shape))
    return (loss, dx0[None], *g_out, *d_out, *m_out, *v_out)
```

```python
import functools
import math

import jax
import jax.numpy as jnp
from jax import lax
from jax.experimental import pallas as pl
from jax.experimental.pallas import tpu as pltpu

F32 = jnp.float32
BF16 = jnp.bfloat16

D = 1024
N_HEADS = 16
N_KV = 4
GROUP = N_HEADS // N_KV
HD = 64
ROT = 16
THETA = 500000.0
BLK = 128
QKV = (N_HEADS + 2 * N_KV) * HD
KV_OFF = N_HEADS * HD
DFF = 2816
CONV_W = 31
CONV_PAD = 15
HALO = 16
CONV_JB = 4
EPS = 1e-6
NEG = -1e30
N_CHIPS = 4
N_DEV = 8
LANES = 128
SUBLANES = 8

ADAM_LR, ADAM_B1, ADAM_B2, ADAM_EPS, ADAM_WD, ADAM_STEP = 0.001, 0.9, 0.999, 1e-08, 0.01, 10

VMEM_LIMIT = 56 * 1024 * 1024
MESH = pl.DeviceIdType.MESH


def _params(*sem):
    return pltpu.CompilerParams(dimension_semantics=sem, vmem_limit_bytes=VMEM_LIMIT)


def _tile(n, want):
    if n <= want:
        return n
    for t in range(want, 7, -1):
        if n % t == 0 and t % 8 == 0:
            return t
    return n


def _sigmoid(v):
    return 1.0 / (1.0 + jnp.exp(-v))


def _rms_fwd(xv, gain):
    r = lax.rsqrt(jnp.mean(xv * xv, axis=-1, keepdims=True) + EPS)
    return xv * r * gain


def _rms_bwd(dh, xv, gain, dres):
    r = lax.rsqrt(jnp.mean(xv * xv, axis=-1, keepdims=True) + EPS)
    xhat = xv * r
    gy = dh * gain
    dx = r * (gy - xhat * jnp.mean(gy * xhat, axis=-1, keepdims=True))
    return dx + dres, dh * xhat


def _rope(blk, c, s1, s2):
    return blk * c + pltpu.roll(blk, LANES - ROT // 2, 1) * s1 + pltpu.roll(blk, ROT // 2, 1) * s2


def _dot(a, b):
    return jnp.dot(a, b, preferred_element_type=F32)


def _dot_tb(a, b):
    return lax.dot_general(a, b, (((1,), (1,)), ((), ())), preferred_element_type=F32)


def _dot_ta(a, b):
    return lax.dot_general(a, b, (((0,), (0,)), ((), ())), preferred_element_type=F32)


def rms_qkv(x, gain, w, rc, rs1, rs2):
    T = x.shape[0]
    tm = _tile(T, 512)

    def body(x_ref, g_ref, w_ref, c_ref, s1_ref, s2_ref, h_ref, qkv_ref):
        h = _rms_fwd(x_ref[...], g_ref[...]).astype(BF16)
        h_ref[...] = h
        acc = _dot(h, w_ref[...])
        c, s1, s2 = c_ref[...], s1_ref[...], s2_ref[...]
        n_rot = (KV_OFF + N_KV * HD) // LANES
        for j in range(n_rot):
            sl = slice(LANES * j, LANES * (j + 1))
            qkv_ref[:, sl] = _rope(acc[:, sl], c, s1, s2).astype(BF16)
        qkv_ref[:, n_rot * LANES:] = acc[:, n_rot * LANES:].astype(BF16)

    row = lambda i: (i, 0)
    full = lambda i: (0, 0)
    return pl.pallas_call(
        body, name="rms_qkv", grid=(T // tm,),
        in_specs=[pl.BlockSpec((tm, D), row), pl.BlockSpec((1, D), full), pl.BlockSpec((D, QKV), full),
                  pl.BlockSpec((tm, LANES), row), pl.BlockSpec((tm, LANES), row), pl.BlockSpec((tm, LANES), row)],
        out_specs=[pl.BlockSpec((tm, D), row), pl.BlockSpec((tm, QKV), row)],
        out_shape=[jax.ShapeDtypeStruct((T, D), BF16), jax.ShapeDtypeStruct((T, QKV), BF16)],
        compiler_params=_params("parallel"),
    )(x, gain, w, rc, rs1, rs2)


def _attn_mask(n, T):
    qi = lax.broadcasted_iota(jnp.int32, (GROUP * BLK, 3 * BLK), 0) & (BLK - 1)
    ci = lax.broadcasted_iota(jnp.int32, (GROUP * BLK, 3 * BLK), 1)
    rel = ci - BLK - qi
    key_pos = n * BLK - BLK + ci
    return (jnp.abs(rel) <= BLK) & (key_pos >= 0) & (key_pos < T)


def _stack_heads(ref, g):
    return jnp.concatenate([ref[:, (g * GROUP + u) * HD:(g * GROUP + u + 1) * HD] for u in range(GROUP)], axis=0)


def _stack_sinks(sink_ref, g):
    return jnp.concatenate([jnp.full((BLK, 1), sink_ref[g * GROUP + u], F32) for u in range(GROUP)], axis=0)


def _attn_specs(T):
    nb = T // BLK
    kv_blk = 2 * N_KV * HD
    kv_col = KV_OFF // kv_blk
    q_spec = pl.BlockSpec((BLK, KV_OFF), lambda n: (n, 0))
    prev = pl.BlockSpec((BLK, kv_blk), lambda n: (jnp.maximum(n - 1, 0), kv_col))
    own = pl.BlockSpec((BLK, kv_blk), lambda n: (n, kv_col))
    nxt = pl.BlockSpec((BLK, kv_blk), lambda n: (jnp.minimum(n + 1, nb - 1), kv_col))
    return nb, q_spec, prev, own, nxt


def attn_fwd(qkv, sink):
    T = qkv.shape[0]
    nb, q_spec, prev, own, nxt = _attn_specs(T)
    scale = 1.0 / math.sqrt(HD)

    def body(sink_ref, q_ref, kp_ref, ko_ref, kn_ref, o_ref):
        n = pl.program_id(0)
        valid = _attn_mask(n, T)
        kv = jnp.concatenate([kp_ref[...], ko_ref[...], kn_ref[...]], axis=0)
        ss = [_dot_tb(_stack_heads(q_ref, g), kv[:, g * HD:(g + 1) * HD]) for g in range(N_KV)]
        ps = []
        for g in range(N_KV):
            s = jnp.where(valid, ss[g] * scale, NEG)
            sk = _stack_sinks(sink_ref, g)
            m = jnp.maximum(jnp.max(s, axis=-1, keepdims=True), sk)
            e = jnp.exp(s - m)
            ps.append((e / (jnp.sum(e, axis=-1, keepdims=True) + jnp.exp(sk - m))).astype(BF16))
        for g in range(N_KV):
            o = _dot(ps[g], kv[:, N_KV * HD + g * HD:N_KV * HD + (g + 1) * HD]).astype(BF16)
            for u in range(GROUP):
                h = g * GROUP + u
                o_ref[:, h * HD:(h + 1) * HD] = o[u * BLK:(u + 1) * BLK]

    return pl.pallas_call(
        body, name="attn_fwd", grid=(nb,),
        in_specs=[pl.BlockSpec(memory_space=pltpu.SMEM), q_spec, prev, own, nxt],
        out_specs=pl.BlockSpec((BLK, D), lambda n: (n, 0)),
        out_shape=jax.ShapeDtypeStruct((T, D), BF16),
        compiler_params=_params("parallel"),
    )(sink, qkv, qkv, qkv, qkv)


def mm_res(a, w, resid, bias, name):
    T, K = a.shape
    tm = _tile(T, 512)

    def body(a_ref, w_ref, r_ref, b_ref, o_ref):
        o_ref[...] = _dot(a_ref[...], w_ref[...]) + b_ref[...] + r_ref[...]

    row = lambda i: (i, 0)
    full = lambda i: (0, 0)
    return pl.pallas_call(
        body, name=name, grid=(T // tm,),
        in_specs=[pl.BlockSpec((tm, K), row), pl.BlockSpec((K, D), full), pl.BlockSpec((tm, D), row),
                  pl.BlockSpec((1, D), full)],
        out_specs=pl.BlockSpec((tm, D), row),
        out_shape=jax.ShapeDtypeStruct((T, D), F32),
        compiler_params=_params("parallel"),
    )(a, w, resid, bias)


def rms_mm_gate(x, gain, w, bias, H, swiglu, act_dtype, name):
    T = x.shape[0]
    tm = _tile(T, 512)
    tn = 1408 if H % 1408 == 0 else 512
    nj = H // tn

    def body(x_ref, g_ref, w1_ref, w2_ref, b1_ref, b2_ref, h_ref, pre_ref, act_ref, hs):
        @pl.when(pl.program_id(1) == 0)
        def _():
            h = _rms_fwd(x_ref[...], g_ref[...]).astype(BF16)
            hs[...] = h
            h_ref[...] = h

        h = hs[...]
        a = _dot(h, w1_ref[...]) + b1_ref[...]
        b = _dot(h, w2_ref[...]) + b2_ref[...]
        pre_ref[0] = a.astype(BF16)
        pre_ref[1] = b.astype(BF16)
        if swiglu:
            act = a * _sigmoid(a) * b
        else:
            act = a * _sigmoid(b)
        act_ref[...] = act.astype(act_dtype)

    return pl.pallas_call(
        body, name=name, grid=(T // tm, nj),
        in_specs=[pl.BlockSpec((tm, D), lambda i, j: (i, 0)), pl.BlockSpec((1, D), lambda i, j: (0, 0)),
                  pl.BlockSpec((D, tn), lambda i, j: (0, j)), pl.BlockSpec((D, tn), lambda i, j: (0, nj + j)),
                  pl.BlockSpec((1, tn), lambda i, j: (0, j)), pl.BlockSpec((1, tn), lambda i, j: (0, nj + j))],
        out_specs=[pl.BlockSpec((tm, D), lambda i, j: (i, 0)), pl.BlockSpec((2, tm, tn), lambda i, j: (0, i, j)),
                   pl.BlockSpec((tm, tn), lambda i, j: (i, j))],
        out_shape=[jax.ShapeDtypeStruct((T, D), BF16), jax.ShapeDtypeStruct((2, T, H), BF16),
                   jax.ShapeDtypeStruct((T, H), act_dtype)],
        scratch_shapes=[pltpu.VMEM((tm, D), BF16)],
        compiler_params=_params("parallel", "arbitrary"),
    )(x, gain, w, w, bias, bias)


def _conv_tiles(T):
    tt = _tile(T, 512)
    return tt, tt // SUBLANES, D // LANES


def _fill_strided(ext, p, L):
    def ibody(i, carry):
        ext[i] = p[pl.ds(i + 1, SUBLANES, stride=L), :]
        return carry

    lax.fori_loop(0, L + CONV_W - 1, ibody, 0, unroll=2)


def _conv_specs(T, tt):
    main = pl.BlockSpec((tt, D), lambda i: (i, 0))
    per = tt // HALO
    prev = pl.BlockSpec((HALO, D), lambda i: (jnp.maximum(i * per - 1, 0), 0))
    nxt = pl.BlockSpec((HALO, D), lambda i: (jnp.minimum((i + 1) * per, T // HALO - 1), 0))
    return main, prev, nxt


def _fill_pad(pad, main_ref, prev_ref, next_ref, i, n_i, tt, nlt):
    keep_p = (i > 0).astype(F32)
    keep_n = (i < n_i - 1).astype(F32)
    for lt in range(nlt):
        sl = slice(lt * LANES, (lt + 1) * LANES)
        pad[lt, 0:HALO, :] = prev_ref[:, sl] * keep_p
        pad[lt, HALO:HALO + tt, :] = main_ref[:, sl]
        pad[lt, HALO + tt:2 * HALO + tt, :] = next_ref[:, sl] * keep_n


def conv_fwd(glu, w_dw, b_dw, ln_g, ln_b):
    T = glu.shape[0]
    tt, L, nlt = _conv_tiles(T)
    n_i = T // tt
    main, prev, nxt = _conv_specs(T, tt)

    def body(x_ref, xp_ref, xn_ref, w_ref, b_ref, g_ref, bb_ref, dwc_ref, sw_ref, pad, ob, ext):
        i = pl.program_id(0)
        _fill_pad(pad, x_ref, xp_ref, xn_ref, i, n_i, tt, nlt)
        for lt in range(nlt):
            sl = slice(lt * LANES, (lt + 1) * LANES)
            o = ob.at[lt]
            _fill_strided(ext, pad.at[lt], L)
            wk = [jnp.broadcast_to(w_ref[k:k + 1, sl], (SUBLANES, LANES)) for k in range(CONV_W)]

            def jbody(jb, carry):
                j = jb * CONV_JB
                accs = [None] * CONV_JB
                for m in range(CONV_W + CONV_JB - 1):
                    e = ext[j + m]
                    for u in range(CONV_JB):
                        if 0 <= m - u < CONV_W:
                            t = e * wk[m - u]
                            accs[u] = t if accs[u] is None else accs[u] + t
                for u in range(CONV_JB):
                    o[pl.ds(j + u, SUBLANES, stride=L), :] = accs[u]
                return carry

            lax.fori_loop(0, L // CONV_JB, jbody, 0)
        y = jnp.concatenate([ob[lt] for lt in range(nlt)], axis=1) + b_ref[...]
        dwc_ref[...] = y
        mu = jnp.mean(y, axis=-1, keepdims=True)
        yc = y - mu
        var = jnp.mean(yc * yc, axis=-1, keepdims=True)
        z = yc * lax.rsqrt(var + EPS) * g_ref[...] + bb_ref[...]
        sw_ref[...] = (z * _sigmoid(z)).astype(BF16)

    full = lambda i: (0, 0)
    return pl.pallas_call(
        body, name="conv_fwd", grid=(n_i,),
        in_specs=[main, prev, nxt, pl.BlockSpec((32, D), full), pl.BlockSpec((1, D), full),
                  pl.BlockSpec((1, D), full), pl.BlockSpec((1, D), full)],
        out_specs=[pl.BlockSpec((tt, D), lambda i: (i, 0)), pl.BlockSpec((tt, D), lambda i: (i, 0))],
        out_shape=[jax.ShapeDtypeStruct((T, D), F32), jax.ShapeDtypeStruct((T, D), BF16)],
        scratch_shapes=[pltpu.VMEM((nlt, tt + 2 * HALO, LANES), F32), pltpu.VMEM((nlt, tt, LANES), F32),
                        pltpu.VMEM((L + 2 * HALO, SUBLANES, LANES), F32)],
        compiler_params=_params("parallel"),
    )(glu, glu, glu, w_dw, b_dw, ln_g, ln_b)


def final_loss(x, gain, target):
    T = x.shape[0]
    tm = _tile(T, 512)

    def body(x_ref, g_ref, t_ref, dx_ref, loss_ref, dg_ref):
        @pl.when(pl.program_id(0) == 0)
        def _():
            loss_ref[...] = jnp.zeros_like(loss_ref)
            dg_ref[...] = jnp.zeros_like(dg_ref)

        xv, gain_v = x_ref[...], g_ref[...]
        err = _rms_fwd(xv, gain_v) - t_ref[...]
        part = 0.5 * jnp.sum(jnp.mean(err * err, axis=-1, keepdims=True), axis=0, keepdims=True)
        loss_ref[...] += jnp.broadcast_to(part, loss_ref.shape)
        dx, dgr = _rms_bwd(err * (1.0 / D), xv, gain_v, 0.0)
        dx_ref[...] = dx
        dg_ref[...] += jnp.sum(dgr, axis=0, keepdims=True)

    row = lambda i: (i, 0)
    full = lambda i: (0, 0)
    return pl.pallas_call(
        body, name="final_loss", grid=(T // tm,),
        in_specs=[pl.BlockSpec((tm, D), row), pl.BlockSpec((1, D), full), pl.BlockSpec((tm, D), row)],
        out_specs=[pl.BlockSpec((tm, D), row), pl.BlockSpec((1, LANES), full), pl.BlockSpec((1, D), full)],
        out_shape=[jax.ShapeDtypeStruct((T, D), F32), jax.ShapeDtypeStruct((1, LANES), F32),
                   jax.ShapeDtypeStruct((1, D), F32)],
        compiler_params=_params("arbitrary"),
    )(x, gain, target)


def swiglu_bwd(dx, w_down, pre, name):
    T = dx.shape[0]
    H = w_down.shape[0]
    tm = _tile(T, 512)
    tn = 1408
    nj = H // tn

    def body(dx_ref, w_ref, pre_ref, dpre_ref, dxs):
        @pl.when(pl.program_id(1) == 0)
        def _():
            dxs[...] = dx_ref[...].astype(BF16)

        dact = _dot_tb(dxs[...], w_ref[...])
        g = pre_ref[0].astype(F32)
        u = pre_ref[1].astype(F32)
        sg = _sigmoid(g)
        dpre_ref[0] = (dact * u * sg * (1.0 + g * (1.0 - sg))).astype(BF16)
        dpre_ref[1] = (dact * g * sg).astype(BF16)

    return pl.pallas_call(
        body, name=name, grid=(T // tm, nj),
        in_specs=[pl.BlockSpec((tm, D), lambda i, j: (i, 0)), pl.BlockSpec((tn, D), lambda i, j: (j, 0)),
                  pl.BlockSpec((2, tm, tn), lambda i, j: (0, i, j))],
        out_specs=pl.BlockSpec((2, tm, tn), lambda i, j: (0, i, j)),
        out_shape=jax.ShapeDtypeStruct((2, T, H), BF16),
        scratch_shapes=[pltpu.VMEM((tm, D), BF16)],
        compiler_params=_params("parallel", "arbitrary"),
    )(dx, w_down, pre)


def mm_bt_rmsbwd(dpre, w, x, gain, dres, name):
    nh, T, H = dpre.shape
    tm = _tile(T, 512)
    tk = 1408 if H % 1408 == 0 else (1024 if H % 1024 == 0 else H)
    nk = H // tk

    def body(dp_ref, w_ref, x_ref, g_ref, dres_ref, dx_ref, dg_ref, acc):
        i, hf, kk = pl.program_id(0), pl.program_id(1), pl.program_id(2)

        @pl.when((i == 0) & (hf == 0) & (kk == 0))
        def _():
            dg_ref[...] = jnp.zeros_like(dg_ref)

        @pl.when((hf == 0) & (kk == 0))
        def _():
            acc[...] = jnp.zeros_like(acc)

        acc[...] += _dot_tb(dp_ref[...], w_ref[...])

        @pl.when((hf == nh - 1) & (kk == nk - 1))
        def _():
            dx, dgr = _rms_bwd(acc[...], x_ref[...], g_ref[...], dres_ref[...])
            dx_ref[...] = dx
            dg_ref[...] += jnp.sum(dgr, axis=0, keepdims=True)

    return pl.pallas_call(
        body, name=name, grid=(T // tm, nh, nk),
        in_specs=[pl.BlockSpec((None, tm, tk), lambda i, hf, kk: (hf, i, kk)),
                  pl.BlockSpec((D, tk), lambda i, hf, kk: (0, hf * nk + kk)),
                  pl.BlockSpec((tm, D), lambda i, hf, kk: (i, 0)), pl.BlockSpec((1, D), lambda i, hf, kk: (0, 0)),
                  pl.BlockSpec((tm, D), lambda i, hf, kk: (i, 0))],
        out_specs=[pl.BlockSpec((tm, D), lambda i, hf, kk: (i, 0)), pl.BlockSpec((1, D), lambda i, hf, kk: (0, 0))],
        out_shape=[jax.ShapeDtypeStruct((T, D), F32), jax.ShapeDtypeStruct((1, D), F32)],
        scratch_shapes=[pltpu.VMEM((tm, D), F32)],
        compiler_params=_params("arbitrary", "arbitrary", "arbitrary"),
    )(dpre, w, x, gain, dres)


def dw_col(a, dpre, name):
    T = a.shape[0]
    nh, _, H = dpre.shape
    per = nh * H // N_CHIPS
    bph = N_CHIPS // nh
    tt = _tile(T, 512)
    nt = T // tt

    def body(a_ref, b_ref, o_ref, acc):
        t = pl.program_id(1)

        @pl.when(t == 0)
        def _():
            acc[...] = jnp.zeros_like(acc)

        acc[...] += _dot_ta(a_ref[...], b_ref[...])

        @pl.when(t == nt - 1)
        def _():
            o_ref[...] = acc[...].astype(BF16)

    return pl.pallas_call(
        body, name=name, grid=(N_CHIPS, nt),
        in_specs=[pl.BlockSpec((tt, D), lambda q, t: (t, 0)),
                  pl.BlockSpec((None, tt, per), lambda q, t: (q // bph, t, q % bph))],
        out_specs=pl.BlockSpec((None, D, per), lambda q, t: (q, 0, 0)),
        out_shape=jax.ShapeDtypeStruct((N_CHIPS, D, per), BF16),
        scratch_shapes=[pltpu.VMEM((D, per), F32)],
        compiler_params=_params("parallel", "arbitrary"),
    )(a, dpre)


def dw_row(a, b, name):
    T, R = a.shape
    cw = 1408 if R % 1408 == 0 else 512
    tt = _tile(T, 512)
    nt = T // tt

    def body(a_ref, b_ref, o_ref, acc):
        t = pl.program_id(1)

        @pl.when(t == 0)
        def _():
            acc[...] = jnp.zeros_like(acc)

        acc[...] += _dot_ta(a_ref[...], b_ref[...].astype(BF16))

        @pl.when(t == nt - 1)
        def _():
            o_ref[...] = acc[...].astype(BF16)

    out = pl.pallas_call(
        body, name=name, grid=(R // cw, nt),
        in_specs=[pl.BlockSpec((tt, cw), lambda q, t: (t, q)), pl.BlockSpec((tt, D), lambda q, t: (t, 0))],
        out_specs=pl.BlockSpec((cw, D), lambda q, t: (q, 0)),
        out_shape=jax.ShapeDtypeStruct((R, D), BF16),
        scratch_shapes=[pltpu.VMEM((cw, D), F32)],
        compiler_params=_params("parallel", "arbitrary"),
    )(a, b)
    return out.reshape(N_CHIPS, R // N_CHIPS, D)


def ln_silu_bwd(dx, w_pw2, dwc, ln_g, ln_b):
    T = dx.shape[0]
    tm = _tile(T, 512)

    def body(dx_ref, w_ref, y_ref, g_ref, b_ref, dy_ref, dg_ref, db_ref, dbo_ref):
        @pl.when(pl.program_id(0) == 0)
        def _():
            dg_ref[...] = jnp.zeros_like(dg_ref)
            db_ref[...] = jnp.zeros_like(db_ref)
            dbo_ref[...] = jnp.zeros_like(dbo_ref)

        dxv = dx_ref[...]
        dsw = _dot_tb(dxv.astype(BF16), w_ref[...])
        y = y_ref[...]
        mu = jnp.mean(y, axis=-1, keepdims=True)
        yc = y - mu
        rstd = lax.rsqrt(jnp.mean(yc * yc, axis=-1, keepdims=True) + EPS)
        xhat = yc * rstd
        z = xhat * g_ref[...] + b_ref[...]
        sg = _sigmoid(z)
        dz = dsw * sg * (1.0 + z * (1.0 - sg))
        dxh = dz * g_ref[...]
        dy_ref[...] = rstd * (dxh - jnp.mean(dxh, axis=-1, keepdims=True)
                              - xhat * jnp.mean(dxh * xhat, axis=-1, keepdims=True))
        dg_ref[...] += jnp.sum(dz * xhat, axis=0, keepdims=True)
        db_ref[...] += jnp.sum(dz, axis=0, keepdims=True)
        dbo_ref[...] += jnp.sum(dxv, axis=0, keepdims=True)

    row = lambda i: (i, 0)
    full = lambda i: (0, 0)
    vec = pl.BlockSpec((1, D), full)
    return pl.pallas_call(
        body, name="ln_silu_bwd", grid=(T // tm,),
        in_specs=[pl.BlockSpec((tm, D), row), pl.BlockSpec((D, D), full), pl.BlockSpec((tm, D), row), vec, vec],
        out_specs=[pl.BlockSpec((tm, D), row), vec, vec, vec],
        out_shape=[jax.ShapeDtypeStruct((T, D), F32)] + [jax.ShapeDtypeStruct((1, D), F32)] * 3,
        compiler_params=_params("arbitrary"),
    )(dx, w_pw2, dwc, ln_g, ln_b)


def conv_bwd(ddwc, glu, pre, w_dw):
    T = ddwc.shape[0]
    tt, L, nlt = _conv_tiles(T)
    n_i = T // tt
    main, prev, nxt = _conv_specs(T, tt)

    def body(d_ref, dp_ref, dn_ref, x_ref, xp_ref, xn_ref, pre_ref, w_ref,
             dpre_ref, dw_ref, dbd_ref, dbp_ref, padd, padx, ob, extd, extx):
        i = pl.program_id(0)

        @pl.when(i == 0)
        def _():
            dw_ref[...] = jnp.zeros_like(dw_ref)
            dbd_ref[...] = jnp.zeros_like(dbd_ref)
            dbp_ref[...] = jnp.zeros_like(dbp_ref)

        _fill_pad(padd, d_ref, dp_ref, dn_ref, i, n_i, tt, nlt)
        _fill_pad(padx, x_ref, xp_ref, xn_ref, i, n_i, tt, nlt)
        for lt in range(nlt):
            sl = slice(lt * LANES, (lt + 1) * LANES)
            o = ob.at[lt]
            _fill_strided(extd, padd.at[lt], L)
            _fill_strided(extx, padx.at[lt], L)
            wk = [jnp.broadcast_to(w_ref[k:k + 1, sl], (SUBLANES, LANES)) for k in range(CONV_W)]

            def jbody(jb, accs):
                j = jb * 2
                accs = list(accs)
                d0, d1 = extd[j + CONV_PAD], extd[j + 1 + CONV_PAD]
                g0 = g1 = None
                for m in range(CONV_W + 1):
                    ed = extd[j + 2 * CONV_PAD + 1 - m]
                    ex = extx[j + m]
                    if m < CONV_W:
                        t = ed * wk[m]
                        g1 = t if g1 is None else g1 + t
                        accs[m] = accs[m] + d0 * ex
                    if m >= 1:
                        t = ed * wk[m - 1]
                        g0 = t if g0 is None else g0 + t
                        accs[m - 1] = accs[m - 1] + d1 * ex
                o[pl.ds(j, SUBLANES, stride=L), :] = g0
                o[pl.ds(j + 1, SUBLANES, stride=L), :] = g1
                return tuple(accs)

            accs = lax.fori_loop(0, L // 2, jbody, tuple(jnp.zeros((SUBLANES, LANES), F32) for _ in range(CONV_W)))
            for k in range(CONV_W):
                dw_ref[k:k + 1, sl] += jnp.sum(accs[k], axis=0, keepdims=True)
        dglu = jnp.concatenate([ob[lt] for lt in range(nlt)], axis=1)
        a = pre_ref[0].astype(F32)
        gate = pre_ref[1].astype(F32)
        sg = _sigmoid(gate)
        da = dglu * sg
        dgate = dglu * a * sg * (1.0 - sg)
        dpre_ref[0] = da.astype(BF16)
        dpre_ref[1] = dgate.astype(BF16)
        dbd_ref[...] += jnp.sum(d_ref[...], axis=0, keepdims=True)
        dbp_ref[0] += jnp.sum(da, axis=0, keepdims=True)
        dbp_ref[1] += jnp.sum(dgate, axis=0, keepdims=True)

    full = lambda i: (0, 0)
    return pl.pallas_call(
        body, name="conv_bwd", grid=(n_i,),
        in_specs=[main, prev, nxt, main, prev, nxt, pl.BlockSpec((2, tt, D), lambda i: (0, i, 0)),
                  pl.BlockSpec((32, D), full)],
        out_specs=[pl.BlockSpec((2, tt, D), lambda i: (0, i, 0)), pl.BlockSpec((32, D), full),
                   pl.BlockSpec((1, D), full), pl.BlockSpec((2, 1, D), lambda i: (0, 0, 0))],
        out_shape=[jax.ShapeDtypeStruct((2, T, D), BF16), jax.ShapeDtypeStruct((32, D), F32),
                   jax.ShapeDtypeStruct((1, D), F32), jax.ShapeDtypeStruct((2, 1, D), F32)],
        scratch_shapes=[pltpu.VMEM((nlt, tt + 2 * HALO, LANES), F32), pltpu.VMEM((nlt, tt + 2 * HALO, LANES), F32),
                        pltpu.VMEM((nlt, tt, LANES), F32), pltpu.VMEM((L + 2 * HALO, SUBLANES, LANES), F32),
                        pltpu.VMEM((L + 2 * HALO, SUBLANES, LANES), F32)],
        compiler_params=_params("arbitrary"),
    )(ddwc, ddwc, ddwc, glu, glu, glu, pre, w_dw)


def mm_bt(a, w, name):
    T = a.shape[0]
    N = w.shape[0]
    tm = _tile(T, 512)

    def body(a_ref, w_ref, o_ref):
        o_ref[...] = _dot_tb(a_ref[...].astype(BF16), w_ref[...]).astype(BF16)

    return pl.pallas_call(
        body, name=name, grid=(T // tm,),
        in_specs=[pl.BlockSpec((tm, D), lambda i: (i, 0)), pl.BlockSpec((N, D), lambda i: (0, 0))],
        out_specs=pl.BlockSpec((tm, N), lambda i: (i, 0)),
        out_shape=jax.ShapeDtypeStruct((T, N), BF16),
        compiler_params=_params("parallel"),
    )(a, w)


def attn_bwd(qkv, o, do, sink, rc, rs1, rs2):
    T = qkv.shape[0]
    nb, q_spec, prev, own, nxt = _attn_specs(T)
    scale = 1.0 / math.sqrt(HD)
    kvw = N_KV * HD

    def body(sink_ref, q_ref, kp_ref, ko_ref, kn_ref, o_ref, do_ref, c_ref, s1_ref, s2_ref,
             dq_ref, dkc_ref, dvc_ref, dsink_ref, dqs):
        n = pl.program_id(0)

        @pl.when(n == 0)
        def _():
            dsink_ref[...] = jnp.zeros_like(dsink_ref)

        valid = _attn_mask(n, T)
        kv = jnp.concatenate([kp_ref[...], ko_ref[...], kn_ref[...]], axis=0)
        lane = lax.broadcasted_iota(jnp.int32, (1, N_HEADS), 1)
        dsink = jnp.zeros((1, N_HEADS), F32)
        ks = [kv[:, g * HD:(g + 1) * HD] for g in range(N_KV)]
        qs = [_stack_heads(q_ref, g) for g in range(N_KV)]
        dos = [_stack_heads(do_ref, g) for g in range(N_KV)]
        ss = [_dot_tb(qs[g], ks[g]) for g in range(N_KV)]
        dps = [_dot_tb(dos[g], kv[:, kvw + g * HD:kvw + (g + 1) * HD]) for g in range(N_KV)]
        pbs, dss = [], []
        for g in range(N_KV):
            s = jnp.where(valid, ss[g] * scale, NEG)
            sk = _stack_sinks(sink_ref, g)
            m = jnp.maximum(jnp.max(s, axis=-1, keepdims=True), sk)
            e = jnp.exp(s - m)
            inv = 1.0 / (jnp.sum(e, axis=-1, keepdims=True) + jnp.exp(sk - m))
            p = e * inv
            delta = jnp.sum(dos[g].astype(F32) * _stack_heads(o_ref, g).astype(F32), axis=-1, keepdims=True)
            dss.append((p * (dps[g] - delta) * scale).astype(BF16))
            pbs.append(p.astype(BF16))
            dsk = jnp.exp(sk - m) * inv * delta
            for u in range(GROUP):
                part = -jnp.sum(dsk[u * BLK:(u + 1) * BLK], axis=0, keepdims=True)
                dsink = dsink + jnp.where(lane == g * GROUP + u, part, 0.0)
        for g in range(N_KV):
            dq = _dot(dss[g], ks[g])
            dkc_ref[:, g * HD:(g + 1) * HD] = _dot_ta(dss[g], qs[g])
            dvc_ref[:, g * HD:(g + 1) * HD] = _dot_ta(pbs[g], dos[g])
            for u in range(GROUP):
                h = g * GROUP + u
                dqs[:, h * HD:(h + 1) * HD] = dq[u * BLK:(u + 1) * BLK]
        dsink_ref[...] += dsink
        c, s1, s2 = c_ref[...], s1_ref[...], s2_ref[...]
        for j in range(KV_OFF // LANES):
            sl = slice(LANES * j, LANES * (j + 1))
            dq_ref[:, sl] = _rope(dqs[:, sl], c, -s1, -s2).astype(BF16)

    row = lambda n: (n, 0)
    tab = pl.BlockSpec((BLK, LANES), row)
    return pl.pallas_call(
        body, name="attn_bwd", grid=(nb,),
        in_specs=[pl.BlockSpec(memory_space=pltpu.SMEM), q_spec, prev, own, nxt,
                  pl.BlockSpec((BLK, D), row), pl.BlockSpec((BLK, D), row), tab, tab, tab],
        out_specs=[pl.BlockSpec((BLK, D), row), pl.BlockSpec((None, 3 * BLK, kvw), lambda n: (n, 0, 0)),
                   pl.BlockSpec((None, 3 * BLK, kvw), lambda n: (n, 0, 0)), pl.BlockSpec((1, N_HEADS), lambda n: (0, 0))],
        out_shape=[jax.ShapeDtypeStruct((T, D), BF16), jax.ShapeDtypeStruct((nb, 3 * BLK, kvw), F32),
                   jax.ShapeDtypeStruct((nb, 3 * BLK, kvw), F32), jax.ShapeDtypeStruct((1, N_HEADS), F32)],
        scratch_shapes=[pltpu.VMEM((BLK, D), F32)],
        compiler_params=_params("arbitrary"),
    )(sink, qkv, qkv, qkv, qkv, o, do, rc, rs1, rs2)


def kv_sum(dkc, dvc, rc, rs1, rs2):
    nb = dkc.shape[0]
    T = nb * BLK
    kvw = N_KV * HD

    def body(kp_ref, ko_ref, kn_ref, vp_ref, vo_ref, vn_ref, c_ref, s1_ref, s2_ref, out_ref):
        m = pl.program_id(0)
        has_p = (m > 0).astype(F32)
        has_n = (m < nb - 1).astype(F32)
        dk = kp_ref[...] * has_p + ko_ref[...] + kn_ref[...] * has_n
        dv = vp_ref[...] * has_p + vo_ref[...] + vn_ref[...] * has_n
        c, s1, s2 = c_ref[...], s1_ref[...], s2_ref[...]
        for j in range(kvw // LANES):
            sl = slice(LANES * j, LANES * (j + 1))
            out_ref[:, sl] = _rope(dk[:, sl], c, -s1, -s2).astype(BF16)
        out_ref[:, kvw:] = dv.astype(BF16)

    from_prev = pl.BlockSpec((None, BLK, kvw), lambda m: (jnp.maximum(m - 1, 0), 2, 0))
    from_own = pl.BlockSpec((None, BLK, kvw), lambda m: (m, 1, 0))
    from_next = pl.BlockSpec((None, BLK, kvw), lambda m: (jnp.minimum(m + 1, nb - 1), 0, 0))
    tab = pl.BlockSpec((BLK, LANES), lambda m: (m, 0))
    return pl.pallas_call(
        body, name="kv_sum", grid=(nb,),
        in_specs=[from_prev, from_own, from_next, from_prev, from_own, from_next, tab, tab, tab],
        out_specs=pl.BlockSpec((BLK, 2 * kvw), lambda m: (m, 0)),
        out_shape=jax.ShapeDtypeStruct((T, 2 * kvw), BF16),
        compiler_params=_params("parallel"),
    )(dkc, dkc, dkc, dvc, dvc, dvc, rc, rs1, rs2)


def _me():
    return lax.axis_index("x"), lax.axis_index("y"), lax.axis_index("c")


def _half_rows(ref, sharded_rows, chip, core):
    R, C = ref.shape[-2], ref.shape[-1]
    lead = (slice(None),) * (len(ref.shape) - 2)
    if sharded_rows:
        per = R // N_CHIPS
        return ref.at[lead + (pl.ds(chip * per + core * (per // 2), per // 2), slice(None))]
    per = C // N_CHIPS
    return ref.at[lead + (pl.ds(core * (R // 2), R // 2), pl.ds(chip * per, per))]


def gather_weights(shards, sharded_rows):
    n = len(shards)
    full_shapes = []
    for s, rows in zip(shards, sharded_rows):
        shp = list(s.shape)
        shp[-2 if rows else -1] *= N_CHIPS
        full_shapes.append(jax.ShapeDtypeStruct(tuple(shp), s.dtype))

    def body(*refs):
        ins, outs = refs[:n], refs[n:2 * n]
        send_sems, recv_sems, local_sems = refs[2 * n:]
        x, y, c = _me()
        me_chip = 2 * x + y
        sibling = (x, y, 1 - c)
        chips = [(1 - x, y), (x, 1 - y), (1 - x, 1 - y)]

        def half_src(w, core):
            s = ins[w]
            R = s.shape[-2]
            lead = (slice(None),) * (len(s.shape) - 2)
            return s.at[lead + (pl.ds(core * (R // 2), R // 2), slice(None))]

        def copy(w, k, src, chip, core, to):
            return pltpu.make_async_remote_copy(
                src_ref=src, dst_ref=_half_rows(outs[w], sharded_rows[w], chip, core),
                send_sem=send_sems.at[w, k], recv_sem=recv_sems.at[w, k], device_id=to, device_id_type=MESH)

        local = []
        for w in range(n):
            for core in range(2):
                cp = pltpu.make_async_copy(half_src(w, core), _half_rows(outs[w], sharded_rows[w], me_chip, core),
                                           local_sems.at[w, core])
                cp.start()
                local.append(cp)
        first = []
        for w in range(n):
            for j, (qx, qy) in enumerate(chips):
                cp = copy(w, j, half_src(w, c), me_chip, c, (qx, qy, c))
                cp.start()
                first.append(cp)
        passed = []
        for w in range(n):
            for j, (qx, qy) in enumerate(chips):
                q = 2 * qx + qy
                landed = _half_rows(outs[w], sharded_rows[w], q, c)
                copy(w, j, landed, q, c, (x, y, c)).wait_recv()
                cp = copy(w, 3 + j, landed, q, c, sibling)
                cp.start()
                passed.append(cp)
        for w in range(n):
            for j, (qx, qy) in enumerate(chips):
                q = 2 * qx + qy
                copy(w, 3 + j, _half_rows(outs[w], sharded_rows[w], q, 1 - c), q, 1 - c, (x, y, c)).wait_recv()
        for cp in first + passed:
            cp.wait_send()
        for cp in local:
            cp.wait()

    any_spec = pl.BlockSpec(memory_space=pl.ANY)
    return pl.pallas_call(
        body, name="gather_weights",
        in_specs=[any_spec] * n, out_specs=[any_spec] * n, out_shape=full_shapes,
        scratch_shapes=[pltpu.SemaphoreType.DMA((n, 6)), pltpu.SemaphoreType.DMA((n, 6)),
                        pltpu.SemaphoreType.DMA((n, 2))],
    )(*shards)


def scatter_grads(grads, small, name):
    n = len(grads)
    out_shapes = [jax.ShapeDtypeStruct((N_DEV, g.shape[1] // 2, g.shape[2]), g.dtype) for g in grads]
    out_shapes.append(jax.ShapeDtypeStruct((N_DEV,) + small.shape, small.dtype))

    def body(*refs):
        ins, outs = refs[:n + 1], refs[n + 1:2 * n + 2]
        send_sems, recv_sems, local_sems = refs[2 * n + 2:]
        x, y, c = _me()
        me = 4 * x + 2 * y + c

        def piece(w, chip, core):
            if w == n:
                return ins[w]
            half = ins[w].shape[1] // 2
            return ins[w].at[chip, pl.ds(core * half, half), :]

        sends, local = [], []
        for w in range(n + 1):
            cp = pltpu.make_async_copy(piece(w, 2 * x + y, c), outs[w].at[me], local_sems.at[w])
            cp.start()
            local.append(cp)
            for k in range(1, N_DEV):
                fx, fy, fc = (k >> 2) & 1, (k >> 1) & 1, k & 1
                px, py, pc = x ^ fx, y ^ fy, c ^ fc
                cp = pltpu.make_async_remote_copy(
                    src_ref=piece(w, 2 * px + py, pc), dst_ref=outs[w].at[me],
                    send_sem=send_sems.at[w, k], recv_sem=recv_sems.at[w, k],
                    device_id=(px, py, pc), device_id_type=MESH)
                cp.start()
                sends.append(cp)
        for w in range(n + 1):
            for k in range(1, N_DEV):
                fx, fy, fc = (k >> 2) & 1, (k >> 1) & 1, k & 1
                px, py, pc = x ^ fx, y ^ fy, c ^ fc
                peer = 4 * px + 2 * py + pc
                pltpu.make_async_remote_copy(
                    src_ref=piece(w, 2 * x + y, c), dst_ref=outs[w].at[peer],
                    send_sem=send_sems.at[w, k], recv_sem=recv_sems.at[w, k],
                    device_id=(px, py, pc), device_id_type=MESH).wait_recv()
        for cp in sends:
            cp.wait_send()
        for cp in local:
            cp.wait()

    any_spec = pl.BlockSpec(memory_space=pl.ANY)
    return pl.pallas_call(
        body, name=name,
        in_specs=[any_spec] * (n + 1), out_specs=[any_spec] * (n + 1), out_shape=out_shapes,
        scratch_shapes=[pltpu.SemaphoreType.DMA((n + 1, N_DEV)), pltpu.SemaphoreType.DMA((n + 1, N_DEV)),
                        pltpu.SemaphoreType.DMA((n + 1,))],
    )(*grads, small)


def swap_halves(halves, layers):
    n = len(halves)
    out_shapes, owner = [], []
    i = 0
    for nl in layers:
        r2, cc = halves[i].shape
        out_shapes.append(jax.ShapeDtypeStruct((nl, 2 * r2, cc), F32))
        owner += [(len(out_shapes) - 1, l) for l in range(nl)]
        i += nl
    n_out = len(out_shapes)

    def body(*refs):
        ins, outs = refs[:n], refs[n:n + n_out]
        send_sems, recv_sems, local_sems = refs[n + n_out:]
        x, y, c = _me()

        def dst(w, core):
            o, l = owner[w]
            r2 = ins[w].shape[0]
            return outs[o].at[l, pl.ds(core * r2, r2), :]

        cps = []
        for w in range(n):
            lc = pltpu.make_async_copy(ins[w], dst(w, c), local_sems.at[w])
            lc.start()
            rc = pltpu.make_async_remote_copy(src_ref=ins[w], dst_ref=dst(w, c), send_sem=send_sems.at[w],
                                              recv_sem=recv_sems.at[w], device_id=(x, y, 1 - c), device_id_type=MESH)
            rc.start()
            cps.append((lc, rc))
        for w in range(n):
            pltpu.make_async_remote_copy(src_ref=ins[w], dst_ref=dst(w, 1 - c), send_sem=send_sems.at[w],
                                         recv_sem=recv_sems.at[w], device_id=(x, y, 1 - c),
                                         device_id_type=MESH).wait_recv()
        for lc, rc in cps:
            rc.wait_send()
            lc.wait()

    any_spec = pl.BlockSpec(memory_space=pl.ANY)
    return pl.pallas_call(
        body, name="swap_halves",
        in_specs=[any_spec] * n, out_specs=[any_spec] * n_out, out_shape=out_shapes,
        scratch_shapes=[pltpu.SemaphoreType.DMA((n,)), pltpu.SemaphoreType.DMA((n,)), pltpu.SemaphoreType.DMA((n,))],
    )(*halves)


def sum_pieces(pieces, name):
    _, R, C = pieces.shape
    tr = _tile(R, 128) if R % 128 == 0 else R

    def body(p_ref, o_ref):
        acc = p_ref[0].astype(F32)
        for d in range(1, N_DEV):
            acc = acc + p_ref[d].astype(F32)
        o_ref[...] = acc

    return pl.pallas_call(
        body, name=name, grid=(R // tr,),
        in_specs=[pl.BlockSpec((N_DEV, tr, C), lambda i: (0, i, 0))],
        out_specs=pl.BlockSpec((tr, C), lambda i: (i, 0)),
        out_shape=jax.ShapeDtypeStruct((R, C), F32),
        compiler_params=_params("parallel"),
    )(pieces)


def adamw(w, g, m, v, name):
    Lyr, R, C = w.shape
    tr = _tile(R, 256) if R % 8 == 0 else R
    c1 = 1.0 / (1.0 - ADAM_B1 ** ADAM_STEP)
    c2 = 1.0 / (1.0 - ADAM_B2 ** ADAM_STEP)

    def body(w_ref, g_ref, m_ref, v_ref, d_ref, nm_ref, nv_ref):
        gv = g_ref[...]
        nm = ADAM_B1 * m_ref[...] + (1.0 - ADAM_B1) * gv
        nv = ADAM_B2 * v_ref[...] + (1.0 - ADAM_B2) * (gv * gv)
        nm_ref[...] = nm
        nv_ref[...] = nv
        d_ref[...] = -ADAM_LR * ((nm * c1) / (jnp.sqrt(nv * c2) + ADAM_EPS) + ADAM_WD * w_ref[...])

    spec = pl.BlockSpec((None, tr, C), lambda l, i: (l, i, 0))
    shp = jax.ShapeDtypeStruct(w.shape, F32)
    return pl.pallas_call(
        body, name=name, grid=(Lyr, R // tr),
        in_specs=[spec] * 4, out_specs=[spec] * 3, out_shape=[shp] * 3,
        compiler_params=_params("parallel", "parallel"),
    )(w, g, m, v)


def _rope_tables(T):
    pos = jnp.arange(T, dtype=F32)
    inv_freq = THETA ** (-jnp.arange(0, ROT, 2, dtype=F32) / ROT)
    ang = pos[:, None] * inv_freq[None, :]
    cos, sin = jnp.cos(ang), jnp.sin(ang)
    half = ROT // 2
    one = jnp.ones((T, HD - ROT), F32)
    zero = jnp.zeros((T, HD - ROT), F32)
    zh = jnp.zeros((T, half), F32)
    c = jnp.concatenate([cos, cos, one], axis=1)
    s1 = jnp.concatenate([-sin, zh, zero], axis=1)
    s2 = jnp.concatenate([zh, sin, zero], axis=1)
    two = lambda t: jnp.concatenate([t, t], axis=1)
    return two(c), two(s1), two(s2)


def kernel(x, attn_norm, attn_w_qkv, attn_w_o, attn_sink, conv_norm, conv_w_pw1, conv_b_pw1, conv_w_dw, conv_b_dw, conv_ln_g, conv_ln_b, conv_w_pw2, conv_b_pw2, ffn_norm, ffn_w_gu, ffn_w_down, final_norm, loss_target, m_attn_norm, m_attn_w_qkv, m_attn_w_o, m_attn_sink, m_conv_norm, m_conv_w_pw1, m_conv_b_pw1, m_conv_w_dw, m_conv_b_dw, m_conv_ln_g, m_conv_ln_b, m_conv_w_pw2, m_conv_b_pw2, m_ffn_norm, m_ffn_w_gu, m_ffn_w_down, m_final_norm, v_attn_norm, v_attn_w_qkv, v_attn_w_o, v_attn_sink, v_conv_norm, v_conv_w_pw1, v_conv_b_pw1, v_conv_w_dw, v_conv_b_dw, v_conv_ln_g, v_conv_ln_b, v_conv_w_pw2, v_conv_b_pw2, v_ffn_norm, v_ffn_w_gu, v_ffn_w_down, v_final_norm):
    T = x.shape[1]
    x0 = x[0]
    target = loss_target[0]
    ix, iy = lax.axis_index("x"), lax.axis_index("y")
    chip = 2 * ix + iy
    rc, rs1, rs2 = _rope_tables(T)

    big_shards = [attn_w_qkv[0].astype(BF16), attn_w_o[0].astype(BF16), ffn_w_gu.astype(BF16),
                  ffn_w_down.astype(BF16), conv_w_pw1[0].astype(BF16), conv_w_pw2[0].astype(BF16)]
    w_qkv, w_o, w_gu, w_down, w_pw1, w_pw2 = gather_weights(big_shards, [False, True, False, True, False, True])

    def place(vec, width):
        return lax.dynamic_update_slice(jnp.zeros((vec.shape[0], N_CHIPS * width), F32), vec, (0, chip * width))

    small_rows = jnp.concatenate([
        place(conv_norm, 256), place(conv_b_pw1, 512).reshape(2, D), place(conv_b_dw, 256), place(conv_ln_g, 256),
        place(conv_ln_b, 256), place(conv_b_pw2, 256), jnp.zeros((1, D), F32),
        place(conv_w_dw[0], 256), jnp.zeros((1, D), F32)], axis=0)
    zero_g = [jnp.zeros((N_CHIPS, 32, LANES), BF16)]
    got = scatter_grads(zero_g, small_rows, "gather_small_params")[1]
    psmall = sum_pieces(got, "sum_small_params") * 0.5
    p_conv_norm, p_b_pw1 = psmall[0:1], psmall[1:3].reshape(1, 2 * D)
    p_b_dw, p_ln_g, p_ln_b, p_b_pw2 = psmall[3:4], psmall[4:5], psmall[5:6], psmall[6:7]
    p_w_dw = psmall[8:40]

    h0, qkv = rms_qkv(x0, attn_norm, w_qkv, rc, rs1, rs2)
    sink = attn_sink[0]
    o = attn_fwd(qkv, sink)
    zero_b = jnp.zeros((1, D), F32)
    x1 = mm_res(o, w_o, x0, zero_b, "attn_out")
    zero_gu = jnp.zeros((1, 2 * DFF), F32)
    h1, gu0, act0 = rms_mm_gate(x1, ffn_norm[0:1], w_gu[0], zero_gu, DFF, True, BF16, "ffn0_up")
    x2 = mm_res(act0, w_down[0], x1, zero_b, "ffn0_down")
    h2, pre, glu = rms_mm_gate(x2, p_conv_norm, w_pw1, p_b_pw1, D, False, F32, "conv_pw1")
    dwc, sw = conv_fwd(glu, p_w_dw, p_b_dw, p_ln_g, p_ln_b)
    x3 = mm_res(sw, w_pw2, x2, p_b_pw2, "conv_pw2")
    h3, gu1, act1 = rms_mm_gate(x3, ffn_norm[1:2], w_gu[1], zero_gu, DFF, True, BF16, "ffn1_up")
    x4 = mm_res(act1, w_down[1], x3, zero_b, "ffn1_down")
    dx4, loss_part, d_final = final_loss(x4, final_norm.reshape(1, D), target)
    loss = lax.psum(loss_part[0, 0], ("x", "y", "c"))

    dgu1 = swiglu_bwd(dx4, w_down[1], gu1, "ffn1_down_bwd")
    g_down1 = dw_row(act1, dx4, "ffn1_down_dw")
    dx3, d_ffn1 = mm_bt_rmsbwd(dgu1, w_gu[1], x3, ffn_norm[1:2], dx4, "ffn1_up_bwd")
    g_gu1 = dw_col(h3, dgu1, "ffn1_up_dw")

    ddwc, d_ln_g, d_ln_b, d_b_pw2 = ln_silu_bwd(dx3, w_pw2, dwc, p_ln_g, p_ln_b)
    g_pw2 = dw_row(sw, dx3, "conv_pw2_dw")
    dpre, d_w_dw, d_b_dw, d_b_pw1 = conv_bwd(ddwc, glu, pre, p_w_dw)
    dx2, d_conv_norm = mm_bt_rmsbwd(dpre, w_pw1, x2, p_conv_norm, dx3, "conv_pw1_bwd")
    g_pw1 = dw_col(h2, dpre, "conv_pw1_dw")

    dgu0 = swiglu_bwd(dx2, w_down[0], gu0, "ffn0_down_bwd")
    g_down0 = dw_row(act0, dx2, "ffn0_down_dw")
    dx1, d_ffn0 = mm_bt_rmsbwd(dgu0, w_gu[0], x1, ffn_norm[0:1], dx2, "ffn0_up_bwd")
    g_gu0 = dw_col(h1, dgu0, "ffn0_up_dw")

    do = mm_bt(dx1, w_o, "attn_out_bwd")
    g_o = dw_row(o, dx1, "attn_out_dw")
    dq, dkc, dvc, d_sink = attn_bwd(qkv, o, do, sink, rc, rs1, rs2)
    dkv = kv_sum(dkc, dvc, rc, rs1, rs2)
    dqkv = jnp.concatenate([dq, dkv], axis=1)[None]
    dx0, d_attn_norm = mm_bt_rmsbwd(dqkv, w_qkv, x0, attn_norm, dx1, "attn_qkv_bwd")
    g_qkv = dw_col(h0, dqkv, "attn_qkv_dw")

    pad16 = lambda t: jnp.concatenate([t, jnp.zeros((1, D - t.shape[1]), F32)], axis=1)
    small_g = jnp.concatenate([
        d_attn_norm, pad16(d_sink), d_conv_norm, d_b_pw1.reshape(2, D), d_b_dw, d_ln_g, d_ln_b, d_b_pw2,
        d_ffn0, d_ffn1, d_final, jnp.zeros((4, D), F32), d_w_dw], axis=0)
    big_g = [g_qkv, g_o, g_gu0, g_gu1, g_down0, g_down1, g_pw1, g_pw2]
    names = ["qkv", "o", "gu0", "gu1", "down0", "down1", "pw1", "pw2"]
    got = scatter_grads(big_g, small_g, "scatter_grads")
    halves = [sum_pieces(p, "sum_" + nm) for p, nm in zip(got[:-1], names)]
    gs = sum_pieces(got[-1], "sum_small_grads")
    gf_qkv, gf_o, gf_gu, gf_down, gf_pw1, gf_pw2 = swap_halves(halves, [1, 1, 2, 2, 1, 1])

    def take(row0, nrows, width):
        return lax.dynamic_slice(gs, (row0, chip * width), (nrows, width))

    grads = {
        "attn_norm": gs[0:1], "attn_w_qkv": gf_qkv, "attn_w_o": gf_o, "attn_sink": gs[1:2, :N_HEADS],
        "conv_norm": take(2, 1, 256), "conv_w_pw1": gf_pw1,
        "conv_b_pw1": lax.dynamic_slice(gs[3:5].reshape(1, 2 * D), (0, chip * 512), (1, 512)),
        "conv_w_dw": take(16, 32, 256)[None, :CONV_W], "conv_b_dw": take(5, 1, 256), "conv_ln_g": take(6, 1, 256),
        "conv_ln_b": take(7, 1, 256), "conv_w_pw2": gf_pw2, "conv_b_pw2": take(8, 1, 256),
        "ffn_norm": gs[9:11], "ffn_w_gu": gf_gu, "ffn_w_down": gf_down, "final_norm": gs[11],
    }
    weights = dict(attn_norm=attn_norm, attn_w_qkv=attn_w_qkv, attn_w_o=attn_w_o, attn_sink=attn_sink,
                   conv_norm=conv_norm, conv_w_pw1=conv_w_pw1, conv_b_pw1=conv_b_pw1, conv_w_dw=conv_w_dw,
                   conv_b_dw=conv_b_dw, conv_ln_g=conv_ln_g, conv_ln_b=conv_ln_b, conv_w_pw2=conv_w_pw2,
                   conv_b_pw2=conv_b_pw2, ffn_norm=ffn_norm, ffn_w_gu=ffn_w_gu, ffn_w_down=ffn_w_down,
                   final_norm=final_norm)
    m_in = dict(attn_norm=m_attn_norm, attn_w_qkv=m_attn_w_qkv, attn_w_o=m_attn_w_o, attn_sink=m_attn_sink,
                conv_norm=m_conv_norm, conv_w_pw1=m_conv_w_pw1, conv_b_pw1=m_conv_b_pw1, conv_w_dw=m_conv_w_dw,
                conv_b_dw=m_conv_b_dw, conv_ln_g=m_conv_ln_g, conv_ln_b=m_conv_ln_b, conv_w_pw2=m_conv_w_pw2,
                conv_b_pw2=m_conv_b_pw2, ffn_norm=m_ffn_norm, ffn_w_gu=m_ffn_w_gu, ffn_w_down=m_ffn_w_down,
                final_norm=m_final_norm)
    v_in = dict(attn_norm=v_attn_norm, attn_w_qkv=v_attn_w_qkv, attn_w_o=v_attn_w_o, attn_sink=v_attn_sink,
                conv_norm=v_conv_norm, conv_w_pw1=v_conv_w_pw1, conv_b_pw1=v_conv_b_pw1, conv_w_dw=v_conv_w_dw,
                conv_b_dw=v_conv_b_dw, conv_ln_g=v_conv_ln_g, conv_ln_b=v_conv_ln_b, conv_w_pw2=v_conv_w_pw2,
                conv_b_pw2=v_conv_b_pw2, ffn_norm=v_ffn_norm, ffn_w_gu=v_ffn_w_gu, ffn_w_down=v_ffn_w_down,
                final_norm=v_final_norm)
    order = list(weights)
    g_out, d_out, m_out, v_out = [], [], [], []
    for nm in order:
        w = weights[nm]
        shape = w.shape
        as3 = lambda t: t.reshape((1,) * (3 - len(shape)) + shape) if len(shape) < 3 else t.reshape(shape)
        g3 = as3(grads[nm].reshape(shape))
        delta, nm_, nv_ = adamw(as3(w), g3, as3(m_in[nm]), as3(v_in[nm]), "adamw_" + nm)
        g_out.append(g3.reshape(shape))
        d_out.append(delta.reshape(shape))
        m_out.append(nm_.reshape(shape))
        v_out.append(nv_.reshape(shape))
    return (loss, dx0[None], *g_out, *d_out, *m_out, *v_out)
```

```python
import functools
import math

import jax
import jax.numpy as jnp
from jax import lax
from jax.experimental import pallas as pl
from jax.experimental.pallas import tpu as pltpu

F32 = jnp.float32
BF16 = jnp.bfloat16

D = 1024
N_HEADS = 16
N_KV = 4
GROUP = N_HEADS // N_KV
HD = 64
ROT = 16
THETA = 500000.0
BLK = 128
QKV = (N_HEADS + 2 * N_KV) * HD
KV_OFF = N_HEADS * HD
DFF = 2816
CONV_W = 31
CONV_PAD = 15
HALO = 16
CONV_JB = 4
EPS = 1e-6
NEG = -1e30
N_CHIPS = 4
N_DEV = 8
LANES = 128
SUBLANES = 8

ADAM_LR, ADAM_B1, ADAM_B2, ADAM_EPS, ADAM_WD, ADAM_STEP = 0.001, 0.9, 0.999, 1e-08, 0.01, 10

VMEM_LIMIT = 56 * 1024 * 1024
MESH = pl.DeviceIdType.MESH


def _params(*sem):
    return pltpu.CompilerParams(dimension_semantics=sem, vmem_limit_bytes=VMEM_LIMIT)


def _tile(n, want):
    if n <= want:
        return n
    for t in range(want, 7, -1):
        if n % t == 0 and t % 8 == 0:
            return t
    return n


def _sigmoid(v):
    return 1.0 / (1.0 + jnp.exp(-v))


def _rms_fwd(xv, gain):
    r = lax.rsqrt(jnp.mean(xv * xv, axis=-1, keepdims=True) + EPS)
    return xv * r * gain


def _rms_bwd(dh, xv, gain, dres):
    r = lax.rsqrt(jnp.mean(xv * xv, axis=-1, keepdims=True) + EPS)
    xhat = xv * r
    gy = dh * gain
    dx = r * (gy - xhat * jnp.mean(gy * xhat, axis=-1, keepdims=True))
    return dx + dres, dh * xhat


def _rope(blk, c, s1, s2):
    return blk * c + pltpu.roll(blk, LANES - ROT // 2, 1) * s1 + pltpu.roll(blk, ROT // 2, 1) * s2


def _dot(a, b):
    return jnp.dot(a, b, preferred_element_type=F32)


def _dot_tb(a, b):
    return lax.dot_general(a, b, (((1,), (1,)), ((), ())), preferred_element_type=F32)


def _dot_ta(a, b):
    return lax.dot_general(a, b, (((0,), (0,)), ((), ())), preferred_element_type=F32)


def rms_qkv(x, gain, w, rc, rs1, rs2):
    T = x.shape[0]
    tm = _tile(T, 512)

    def body(x_ref, g_ref, w_ref, c_ref, s1_ref, s2_ref, h_ref, qkv_ref):
        h = _rms_fwd(x_ref[...], g_ref[...]).astype(BF16)
        h_ref[...] = h
        acc = _dot(h, w_ref[...])
        c, s1, s2 = c_ref[...], s1_ref[...], s2_ref[...]
        n_rot = (KV_OFF + N_KV * HD) // LANES
        for j in range(n_rot):
            sl = slice(LANES * j, LANES * (j + 1))
            qkv_ref[:, sl] = _rope(acc[:, sl], c, s1, s2).astype(BF16)
        qkv_ref[:, n_rot * LANES:] = acc[:, n_rot * LANES:].astype(BF16)

    row = lambda i: (i, 0)
    full = lambda i: (0, 0)
    return pl.pallas_call(
        body, name="rms_qkv", grid=(T // tm,),
        in_specs=[pl.BlockSpec((tm, D), row), pl.BlockSpec((1, D), full), pl.BlockSpec((D, QKV), full),
                  pl.BlockSpec((tm, LANES), row), pl.BlockSpec((tm, LANES), row), pl.BlockSpec((tm, LANES), row)],
        out_specs=[pl.BlockSpec((tm, D), row), pl.BlockSpec((tm, QKV), row)],
        out_shape=[jax.ShapeDtypeStruct((T, D), BF16), jax.ShapeDtypeStruct((T, QKV), BF16)],
        compiler_params=_params("parallel"),
    )(x, gain, w, rc, rs1, rs2)


def _attn_mask(n, T):
    qi = lax.broadcasted_iota(jnp.int32, (GROUP * BLK, 3 * BLK), 0) & (BLK - 1)
    ci = lax.broadcasted_iota(jnp.int32, (GROUP * BLK, 3 * BLK), 1)
    rel = ci - BLK - qi
    key_pos = n * BLK - BLK + ci
    return (jnp.abs(rel) <= BLK) & (key_pos >= 0) & (key_pos < T)


def _stack_heads(ref, g):
    return jnp.concatenate([ref[:, (g * GROUP + u) * HD:(g * GROUP + u + 1) * HD] for u in range(GROUP)], axis=0)


def _stack_sinks(sink_ref, g):
    return jnp.concatenate([jnp.full((BLK, 1), sink_ref[g * GROUP + u], F32) for u in range(GROUP)], axis=0)


def _attn_specs(T):
    nb = T // BLK
    kv_blk = 2 * N_KV * HD
    kv_col = KV_OFF // kv_blk
    q_spec = pl.BlockSpec((BLK, KV_OFF), lambda n: (n, 0))
    prev = pl.BlockSpec((BLK, kv_blk), lambda n: (jnp.maximum(n - 1, 0), kv_col))
    own = pl.BlockSpec((BLK, kv_blk), lambda n: (n, kv_col))
    nxt = pl.BlockSpec((BLK, kv_blk), lambda n: (jnp.minimum(n + 1, nb - 1), kv_col))
    return nb, q_spec, prev, own, nxt


def attn_fwd(qkv, sink, comm=None):
    T = qkv.shape[0]
    nb, q_spec, prev, own, nxt = _attn_specs(T)
    scale = 1.0 / math.sqrt(HD)

    def body(sink_ref, q_ref, kp_ref, ko_ref, kn_ref, o_ref):
        n = pl.program_id(0)
        valid = _attn_mask(n, T)
        kv = jnp.concatenate([kp_ref[...], ko_ref[...], kn_ref[...]], axis=0)
        ss = [_dot_tb(_stack_heads(q_ref, g), kv[:, g * HD:(g + 1) * HD]) for g in range(N_KV)]
        ps = []
        for g in range(N_KV):
            s = jnp.where(valid, ss[g] * scale, NEG)
            sk = _stack_sinks(sink_ref, g)
            m = jnp.maximum(jnp.max(s, axis=-1, keepdims=True), sk)
            e = jnp.exp(s - m)
            ps.append((e / (jnp.sum(e, axis=-1, keepdims=True) + jnp.exp(sk - m))).astype(BF16))
        for g in range(N_KV):
            o = _dot(ps[g], kv[:, N_KV * HD + g * HD:N_KV * HD + (g + 1) * HD]).astype(BF16)
            for u in range(GROUP):
                h = g * GROUP + u
                o_ref[:, h * HD:(h + 1) * HD] = o[u * BLK:(u + 1) * BLK]

    (o,), got = _call(
        body, name="attn_fwd", grid=(nb,),
        in_specs=[pl.BlockSpec(memory_space=pltpu.SMEM), q_spec, prev, own, nxt],
        out_specs=[pl.BlockSpec((BLK, D), lambda n: (n, 0))],
        out_shape=[jax.ShapeDtypeStruct((T, D), BF16)],
        semantics=("parallel",), args=(sink, qkv, qkv, qkv, qkv), comm=comm)
    return o, got


def mm_res(a, w, resid, bias, name):
    T, K = a.shape
    tm = _tile(T, 512)

    def body(a_ref, w_ref, r_ref, b_ref, o_ref):
        o_ref[...] = _dot(a_ref[...], w_ref[...]) + b_ref[...] + r_ref[...]

    row = lambda i: (i, 0)
    full = lambda i: (0, 0)
    return pl.pallas_call(
        body, name=name, grid=(T // tm,),
        in_specs=[pl.BlockSpec((tm, K), row), pl.BlockSpec((K, D), full), pl.BlockSpec((tm, D), row),
                  pl.BlockSpec((1, D), full)],
        out_specs=pl.BlockSpec((tm, D), row),
        out_shape=jax.ShapeDtypeStruct((T, D), F32),
        compiler_params=_params("parallel"),
    )(a, w, resid, bias)


def rms_mm_gate(x, gain, w, bias, H, swiglu, act_dtype, name, comm=None):
    T = x.shape[0]
    tm = _tile(T, 512)
    tn = 1408 if H % 1408 == 0 else 512
    nj = H // tn

    def body(x_ref, g_ref, w1_ref, w2_ref, b1_ref, b2_ref, h_ref, pre_ref, act_ref, hs):
        @pl.when(pl.program_id(1) == 0)
        def _():
            h = _rms_fwd(x_ref[...], g_ref[...]).astype(BF16)
            hs[...] = h
            h_ref[...] = h

        h = hs[...]
        a = _dot(h, w1_ref[...]) + b1_ref[...]
        b = _dot(h, w2_ref[...]) + b2_ref[...]
        pre_ref[0] = a.astype(BF16)
        pre_ref[1] = b.astype(BF16)
        if swiglu:
            act = a * _sigmoid(a) * b
        else:
            act = a * _sigmoid(b)
        act_ref[...] = act.astype(act_dtype)

    (h, pre, act), got = _call(
        body, name=name, grid=(T // tm, nj),
        in_specs=[pl.BlockSpec((tm, D), lambda i, j: (i, 0)), pl.BlockSpec((1, D), lambda i, j: (0, 0)),
                  pl.BlockSpec((D, tn), lambda i, j: (0, j)), pl.BlockSpec((D, tn), lambda i, j: (0, nj + j)),
                  pl.BlockSpec((1, tn), lambda i, j: (0, j)), pl.BlockSpec((1, tn), lambda i, j: (0, nj + j))],
        out_specs=[pl.BlockSpec((tm, D), lambda i, j: (i, 0)), pl.BlockSpec((2, tm, tn), lambda i, j: (0, i, j)),
                   pl.BlockSpec((tm, tn), lambda i, j: (i, j))],
        out_shape=[jax.ShapeDtypeStruct((T, D), BF16), jax.ShapeDtypeStruct((2, T, H), BF16),
                   jax.ShapeDtypeStruct((T, H), act_dtype)],
        scratch_shapes=[pltpu.VMEM((tm, D), BF16)],
        semantics=("parallel", "arbitrary"), args=(x, gain, w, w, bias, bias), comm=comm)
    return h, pre, act, got


def _conv_tiles(T):
    tt = _tile(T, 512)
    return tt, tt // SUBLANES, D // LANES


def _fill_strided(ext, p, L):
    def ibody(i, carry):
        ext[i] = p[pl.ds(i + 1, SUBLANES, stride=L), :]
        return carry

    lax.fori_loop(0, L + CONV_W - 1, ibody, 0, unroll=2)


def _conv_specs(T, tt):
    main = pl.BlockSpec((tt, D), lambda i: (i, 0))
    per = tt // HALO
    prev = pl.BlockSpec((HALO, D), lambda i: (jnp.maximum(i * per - 1, 0), 0))
    nxt = pl.BlockSpec((HALO, D), lambda i: (jnp.minimum((i + 1) * per, T // HALO - 1), 0))
    return main, prev, nxt


def _fill_pad(pad, main_ref, prev_ref, next_ref, i, n_i, tt, nlt):
    keep_p = (i > 0).astype(F32)
    keep_n = (i < n_i - 1).astype(F32)
    for lt in range(nlt):
        sl = slice(lt * LANES, (lt + 1) * LANES)
        pad[lt, 0:HALO, :] = prev_ref[:, sl] * keep_p
        pad[lt, HALO:HALO + tt, :] = main_ref[:, sl]
        pad[lt, HALO + tt:2 * HALO + tt, :] = next_ref[:, sl] * keep_n


def conv_fwd(glu, w_dw, b_dw, ln_g, ln_b, comm=None):
    T = glu.shape[0]
    tt, L, nlt = _conv_tiles(T)
    n_i = T // tt
    main, prev, nxt = _conv_specs(T, tt)

    def body(x_ref, xp_ref, xn_ref, w_ref, b_ref, g_ref, bb_ref, dwc_ref, sw_ref, pad, ob, ext):
        i = pl.program_id(0)
        _fill_pad(pad, x_ref, xp_ref, xn_ref, i, n_i, tt, nlt)
        for lt in range(nlt):
            sl = slice(lt * LANES, (lt + 1) * LANES)
            o = ob.at[lt]
            _fill_strided(ext, pad.at[lt], L)
            wk = [jnp.broadcast_to(w_ref[k:k + 1, sl], (SUBLANES, LANES)) for k in range(CONV_W)]

            def jbody(jb, carry):
                j = jb * CONV_JB
                accs = [None] * CONV_JB
                for m in range(CONV_W + CONV_JB - 1):
                    e = ext[j + m]
                    for u in range(CONV_JB):
                        if 0 <= m - u < CONV_W:
                            t = e * wk[m - u]
                            accs[u] = t if accs[u] is None else accs[u] + t
                for u in range(CONV_JB):
                    o[pl.ds(j + u, SUBLANES, stride=L), :] = accs[u]
                return carry

            lax.fori_loop(0, L // CONV_JB, jbody, 0)
        y = jnp.concatenate([ob[lt] for lt in range(nlt)], axis=1) + b_ref[...]
        dwc_ref[...] = y
        mu = jnp.mean(y, axis=-1, keepdims=True)
        yc = y - mu
        var = jnp.mean(yc * yc, axis=-1, keepdims=True)
        z = yc * lax.rsqrt(var + EPS) * g_ref[...] + bb_ref[...]
        sw_ref[...] = (z * _sigmoid(z)).astype(BF16)

    full = lambda i: (0, 0)
    (dwc, sw), got = _call(
        body, name="conv_fwd", grid=(n_i,),
        in_specs=[main, prev, nxt, pl.BlockSpec((32, D), full), pl.BlockSpec((1, D), full),
                  pl.BlockSpec((1, D), full), pl.BlockSpec((1, D), full)],
        out_specs=[pl.BlockSpec((tt, D), lambda i: (i, 0)), pl.BlockSpec((tt, D), lambda i: (i, 0))],
        out_shape=[jax.ShapeDtypeStruct((T, D), F32), jax.ShapeDtypeStruct((T, D), BF16)],
        scratch_shapes=[pltpu.VMEM((nlt, tt + 2 * HALO, LANES), F32), pltpu.VMEM((nlt, tt, LANES), F32),
                        pltpu.VMEM((L + 2 * HALO, SUBLANES, LANES), F32)],
        semantics=("parallel",), args=(glu, glu, glu, w_dw, b_dw, ln_g, ln_b), comm=comm)
    return dwc, sw, got


def final_loss(x, gain, target):
    T = x.shape[0]
    tm = _tile(T, 512)

    def body(x_ref, g_ref, t_ref, dx_ref, loss_ref, dg_ref):
        @pl.when(pl.program_id(0) == 0)
        def _():
            loss_ref[...] = jnp.zeros_like(loss_ref)
            dg_ref[...] = jnp.zeros_like(dg_ref)

        xv, gain_v = x_ref[...], g_ref[...]
        err = _rms_fwd(xv, gain_v) - t_ref[...]
        part = 0.5 * jnp.sum(jnp.mean(err * err, axis=-1, keepdims=True), axis=0, keepdims=True)
        loss_ref[...] += jnp.broadcast_to(part, loss_ref.shape)
        dx, dgr = _rms_bwd(err * (1.0 / D), xv, gain_v, 0.0)
        dx_ref[...] = dx
        dg_ref[...] += jnp.sum(dgr, axis=0, keepdims=True)

    row = lambda i: (i, 0)
    full = lambda i: (0, 0)
    return pl.pallas_call(
        body, name="final_loss", grid=(T // tm,),
        in_specs=[pl.BlockSpec((tm, D), row), pl.BlockSpec((1, D), full), pl.BlockSpec((tm, D), row)],
        out_specs=[pl.BlockSpec((tm, D), row), pl.BlockSpec((1, LANES), full), pl.BlockSpec((1, D), full)],
        out_shape=[jax.ShapeDtypeStruct((T, D), F32), jax.ShapeDtypeStruct((1, LANES), F32),
                   jax.ShapeDtypeStruct((1, D), F32)],
        compiler_params=_params("arbitrary"),
    )(x, gain, target)


def swiglu_bwd(dx, w_down, pre, name, comm=None):
    T = dx.shape[0]
    H = w_down.shape[0]
    tm = _tile(T, 512)
    tn = 1408
    nj = H // tn

    def body(dx_ref, w_ref, pre_ref, dpre_ref, dxs):
        @pl.when(pl.program_id(1) == 0)
        def _():
            dxs[...] = dx_ref[...].astype(BF16)

        dact = _dot_tb(dxs[...], w_ref[...])
        g = pre_ref[0].astype(F32)
        u = pre_ref[1].astype(F32)
        sg = _sigmoid(g)
        dpre_ref[0] = (dact * u * sg * (1.0 + g * (1.0 - sg))).astype(BF16)
        dpre_ref[1] = (dact * g * sg).astype(BF16)

    (dpre,), got = _call(
        body, name=name, grid=(T // tm, nj),
        in_specs=[pl.BlockSpec((tm, D), lambda i, j: (i, 0)), pl.BlockSpec((tn, D), lambda i, j: (j, 0)),
                  pl.BlockSpec((2, tm, tn), lambda i, j: (0, i, j))],
        out_specs=[pl.BlockSpec((2, tm, tn), lambda i, j: (0, i, j))],
        out_shape=[jax.ShapeDtypeStruct((2, T, H), BF16)],
        scratch_shapes=[pltpu.VMEM((tm, D), BF16)],
        semantics=("parallel", "arbitrary"), args=(dx, w_down, pre), comm=comm)
    return dpre, got


def mm_bt_rmsbwd(dpre, w, x, gain, dres, name):
    nh, T, H = dpre.shape
    tm = _tile(T, 512)
    tk = 1408 if H % 1408 == 0 else (1024 if H % 1024 == 0 else H)
    nk = H // tk

    def body(dp_ref, w_ref, x_ref, g_ref, dres_ref, dx_ref, dg_ref, acc):
        i, hf, kk = pl.program_id(0), pl.program_id(1), pl.program_id(2)

        @pl.when((i == 0) & (hf == 0) & (kk == 0))
        def _():
            dg_ref[...] = jnp.zeros_like(dg_ref)

        @pl.when((hf == 0) & (kk == 0))
        def _():
            acc[...] = jnp.zeros_like(acc)

        acc[...] += _dot_tb(dp_ref[...], w_ref[...])

        @pl.when((hf == nh - 1) & (kk == nk - 1))
        def _():
            dx, dgr = _rms_bwd(acc[...], x_ref[...], g_ref[...], dres_ref[...])
            dx_ref[...] = dx
            dg_ref[...] += jnp.sum(dgr, axis=0, keepdims=True)

    return pl.pallas_call(
        body, name=name, grid=(T // tm, nh, nk),
        in_specs=[pl.BlockSpec((None, tm, tk), lambda i, hf, kk: (hf, i, kk)),
                  pl.BlockSpec((D, tk), lambda i, hf, kk: (0, hf * nk + kk)),
                  pl.BlockSpec((tm, D), lambda i, hf, kk: (i, 0)), pl.BlockSpec((1, D), lambda i, hf, kk: (0, 0)),
                  pl.BlockSpec((tm, D), lambda i, hf, kk: (i, 0))],
        out_specs=[pl.BlockSpec((tm, D), lambda i, hf, kk: (i, 0)), pl.BlockSpec((1, D), lambda i, hf, kk: (0, 0))],
        out_shape=[jax.ShapeDtypeStruct((T, D), F32), jax.ShapeDtypeStruct((1, D), F32)],
        scratch_shapes=[pltpu.VMEM((tm, D), F32)],
        compiler_params=_params("arbitrary", "arbitrary", "arbitrary"),
    )(dpre, w, x, gain, dres)


def dw_col(a, dpre, name):
    T = a.shape[0]
    nh, _, H = dpre.shape
    per = nh * H // N_CHIPS
    bph = N_CHIPS // nh
    tt = _tile(T, 512)
    nt = T // tt

    def body(a_ref, b_ref, o_ref, acc):
        t = pl.program_id(1)

        @pl.when(t == 0)
        def _():
            acc[...] = jnp.zeros_like(acc)

        acc[...] += _dot_ta(a_ref[...], b_ref[...])

        @pl.when(t == nt - 1)
        def _():
            o_ref[...] = acc[...].astype(BF16)

    return pl.pallas_call(
        body, name=name, grid=(N_CHIPS, nt),
        in_specs=[pl.BlockSpec((tt, D), lambda q, t: (t, 0)),
                  pl.BlockSpec((None, tt, per), lambda q, t: (q // bph, t, q % bph))],
        out_specs=pl.BlockSpec((None, D, per), lambda q, t: (q, 0, 0)),
        out_shape=jax.ShapeDtypeStruct((N_CHIPS, D, per), BF16),
        scratch_shapes=[pltpu.VMEM((D, per), F32)],
        compiler_params=_params("parallel", "arbitrary"),
    )(a, dpre)


def dw_row(a, b, name):
    T, R = a.shape
    cw = 1408 if R % 1408 == 0 else 512
    tt = _tile(T, 512)
    nt = T // tt

    def body(a_ref, b_ref, o_ref, acc):
        t = pl.program_id(1)

        @pl.when(t == 0)
        def _():
            acc[...] = jnp.zeros_like(acc)

        acc[...] += _dot_ta(a_ref[...], b_ref[...].astype(BF16))

        @pl.when(t == nt - 1)
        def _():
            o_ref[...] = acc[...].astype(BF16)

    out = pl.pallas_call(
        body, name=name, grid=(R // cw, nt),
        in_specs=[pl.BlockSpec((tt, cw), lambda q, t: (t, q)), pl.BlockSpec((tt, D), lambda q, t: (t, 0))],
        out_specs=pl.BlockSpec((cw, D), lambda q, t: (q, 0)),
        out_shape=jax.ShapeDtypeStruct((R, D), BF16),
        scratch_shapes=[pltpu.VMEM((cw, D), F32)],
        compiler_params=_params("parallel", "arbitrary"),
    )(a, b)
    return out.reshape(N_CHIPS, R // N_CHIPS, D)


def ln_silu_bwd(dx, w_pw2, dwc, ln_g, ln_b):
    T = dx.shape[0]
    tm = _tile(T, 512)

    def body(dx_ref, w_ref, y_ref, g_ref, b_ref, dy_ref, dg_ref, db_ref, dbo_ref):
        @pl.when(pl.program_id(0) == 0)
        def _():
            dg_ref[...] = jnp.zeros_like(dg_ref)
            db_ref[...] = jnp.zeros_like(db_ref)
            dbo_ref[...] = jnp.zeros_like(dbo_ref)

        dxv = dx_ref[...]
        dsw = _dot_tb(dxv.astype(BF16), w_ref[...])
        y = y_ref[...]
        mu = jnp.mean(y, axis=-1, keepdims=True)
        yc = y - mu
        rstd = lax.rsqrt(jnp.mean(yc * yc, axis=-1, keepdims=True) + EPS)
        xhat = yc * rstd
        z = xhat * g_ref[...] + b_ref[...]
        sg = _sigmoid(z)
        dz = dsw * sg * (1.0 + z * (1.0 - sg))
        dxh = dz * g_ref[...]
        dy_ref[...] = rstd * (dxh - jnp.mean(dxh, axis=-1, keepdims=True)
                              - xhat * jnp.mean(dxh * xhat, axis=-1, keepdims=True))
        dg_ref[...] += jnp.sum(dz * xhat, axis=0, keepdims=True)
        db_ref[...] += jnp.sum(dz, axis=0, keepdims=True)
        dbo_ref[...] += jnp.sum(dxv, axis=0, keepdims=True)

    row = lambda i: (i, 0)
    full = lambda i: (0, 0)
    vec = pl.BlockSpec((1, D), full)
    return pl.pallas_call(
        body, name="ln_silu_bwd", grid=(T // tm,),
        in_specs=[pl.BlockSpec((tm, D), row), pl.BlockSpec((D, D), full), pl.BlockSpec((tm, D), row), vec, vec],
        out_specs=[pl.BlockSpec((tm, D), row), vec, vec, vec],
        out_shape=[jax.ShapeDtypeStruct((T, D), F32)] + [jax.ShapeDtypeStruct((1, D), F32)] * 3,
        compiler_params=_params("arbitrary"),
    )(dx, w_pw2, dwc, ln_g, ln_b)


def conv_bwd(ddwc, glu, pre, w_dw, comm=None):
    T = ddwc.shape[0]
    tt, L, nlt = _conv_tiles(T)
    n_i = T // tt
    main, prev, nxt = _conv_specs(T, tt)

    def body(d_ref, dp_ref, dn_ref, x_ref, xp_ref, xn_ref, pre_ref, w_ref,
             dpre_ref, dw_ref, dbd_ref, dbp_ref, padd, padx, ob, extd, extx):
        i = pl.program_id(0)

        @pl.when(i == 0)
        def _():
            dw_ref[...] = jnp.zeros_like(dw_ref)
            dbd_ref[...] = jnp.zeros_like(dbd_ref)
            dbp_ref[...] = jnp.zeros_like(dbp_ref)

        _fill_pad(padd, d_ref, dp_ref, dn_ref, i, n_i, tt, nlt)
        _fill_pad(padx, x_ref, xp_ref, xn_ref, i, n_i, tt, nlt)
        for lt in range(nlt):
            sl = slice(lt * LANES, (lt + 1) * LANES)
            o = ob.at[lt]
            _fill_strided(extd, padd.at[lt], L)
            _fill_strided(extx, padx.at[lt], L)
            wk = [jnp.broadcast_to(w_ref[k:k + 1, sl], (SUBLANES, LANES)) for k in range(CONV_W)]

            def jbody(jb, accs):
                j = jb * 2
                accs = list(accs)
                d0, d1 = extd[j + CONV_PAD], extd[j + 1 + CONV_PAD]
                g0 = g1 = None
                for m in range(CONV_W + 1):
                    ed = extd[j + 2 * CONV_PAD + 1 - m]
                    ex = extx[j + m]
                    if m < CONV_W:
                        t = ed * wk[m]
                        g1 = t if g1 is None else g1 + t
                        accs[m] = accs[m] + d0 * ex
                    if m >= 1:
                        t = ed * wk[m - 1]
                        g0 = t if g0 is None else g0 + t
                        accs[m - 1] = accs[m - 1] + d1 * ex
                o[pl.ds(j, SUBLANES, stride=L), :] = g0
                o[pl.ds(j + 1, SUBLANES, stride=L), :] = g1
                return tuple(accs)

            accs = lax.fori_loop(0, L // 2, jbody, tuple(jnp.zeros((SUBLANES, LANES), F32) for _ in range(CONV_W)))
            for k in range(CONV_W):
                dw_ref[k:k + 1, sl] += jnp.sum(accs[k], axis=0, keepdims=True)
        dglu = jnp.concatenate([ob[lt] for lt in range(nlt)], axis=1)
        a = pre_ref[0].astype(F32)
        gate = pre_ref[1].astype(F32)
        sg = _sigmoid(gate)
        da = dglu * sg
        dgate = dglu * a * sg * (1.0 - sg)
        dpre_ref[0] = da.astype(BF16)
        dpre_ref[1] = dgate.astype(BF16)
        dbd_ref[...] += jnp.sum(d_ref[...], axis=0, keepdims=True)
        dbp_ref[0] += jnp.sum(da, axis=0, keepdims=True)
        dbp_ref[1] += jnp.sum(dgate, axis=0, keepdims=True)

    full = lambda i: (0, 0)
    (dpre, dw, dbd, dbp), got = _call(
        body, name="conv_bwd", grid=(n_i,),
        in_specs=[main, prev, nxt, main, prev, nxt, pl.BlockSpec((2, tt, D), lambda i: (0, i, 0)),
                  pl.BlockSpec((32, D), full)],
        out_specs=[pl.BlockSpec((2, tt, D), lambda i: (0, i, 0)), pl.BlockSpec((32, D), full),
                   pl.BlockSpec((1, D), full), pl.BlockSpec((2, 1, D), lambda i: (0, 0, 0))],
        out_shape=[jax.ShapeDtypeStruct((2, T, D), BF16), jax.ShapeDtypeStruct((32, D), F32),
                   jax.ShapeDtypeStruct((1, D), F32), jax.ShapeDtypeStruct((2, 1, D), F32)],
        scratch_shapes=[pltpu.VMEM((nlt, tt + 2 * HALO, LANES), F32), pltpu.VMEM((nlt, tt + 2 * HALO, LANES), F32),
                        pltpu.VMEM((nlt, tt, LANES), F32), pltpu.VMEM((L + 2 * HALO, SUBLANES, LANES), F32),
                        pltpu.VMEM((L + 2 * HALO, SUBLANES, LANES), F32)],
        semantics=("arbitrary",), args=(ddwc, ddwc, ddwc, glu, glu, glu, pre, w_dw), comm=comm)
    return dpre, dw, dbd, dbp, got


def mm_bt(a, w, name):
    T = a.shape[0]
    N = w.shape[0]
    tm = _tile(T, 512)

    def body(a_ref, w_ref, o_ref):
        o_ref[...] = _dot_tb(a_ref[...].astype(BF16), w_ref[...]).astype(BF16)

    return pl.pallas_call(
        body, name=name, grid=(T // tm,),
        in_specs=[pl.BlockSpec((tm, D), lambda i: (i, 0)), pl.BlockSpec((N, D), lambda i: (0, 0))],
        out_specs=pl.BlockSpec((tm, N), lambda i: (i, 0)),
        out_shape=jax.ShapeDtypeStruct((T, N), BF16),
        compiler_params=_params("parallel"),
    )(a, w)


def attn_bwd(qkv, o, do, sink, rc, rs1, rs2, comm=None):
    T = qkv.shape[0]
    nb, q_spec, prev, own, nxt = _attn_specs(T)
    scale = 1.0 / math.sqrt(HD)
    kvw = N_KV * HD

    def body(sink_ref, q_ref, kp_ref, ko_ref, kn_ref, o_ref, do_ref, c_ref, s1_ref, s2_ref,
             dq_ref, dkc_ref, dvc_ref, dsink_ref, dqs):
        n = pl.program_id(0)

        @pl.when(n == 0)
        def _():
            dsink_ref[...] = jnp.zeros_like(dsink_ref)

        valid = _attn_mask(n, T)
        kv = jnp.concatenate([kp_ref[...], ko_ref[...], kn_ref[...]], axis=0)
        lane = lax.broadcasted_iota(jnp.int32, (1, N_HEADS), 1)
        dsink = jnp.zeros((1, N_HEADS), F32)
        ks = [kv[:, g * HD:(g + 1) * HD] for g in range(N_KV)]
        qs = [_stack_heads(q_ref, g) for g in range(N_KV)]
        dos = [_stack_heads(do_ref, g) for g in range(N_KV)]
        ss = [_dot_tb(qs[g], ks[g]) for g in range(N_KV)]
        dps = [_dot_tb(dos[g], kv[:, kvw + g * HD:kvw + (g + 1) * HD]) for g in range(N_KV)]
        pbs, dss = [], []
        for g in range(N_KV):
            s = jnp.where(valid, ss[g] * scale, NEG)
            sk = _stack_sinks(sink_ref, g)
            m = jnp.maximum(jnp.max(s, axis=-1, keepdims=True), sk)
            e = jnp.exp(s - m)
            inv = 1.0 / (jnp.sum(e, axis=-1, keepdims=True) + jnp.exp(sk - m))
            p = e * inv
            delta = jnp.sum(dos[g].astype(F32) * _stack_heads(o_ref, g).astype(F32), axis=-1, keepdims=True)
            dss.append((p * (dps[g] - delta) * scale).astype(BF16))
            pbs.append(p.astype(BF16))
            dsk = jnp.exp(sk - m) * inv * delta
            for u in range(GROUP):
                part = -jnp.sum(dsk[u * BLK:(u + 1) * BLK], axis=0, keepdims=True)
                dsink = dsink + jnp.where(lane == g * GROUP + u, part, 0.0)
        for g in range(N_KV):
            dq = _dot(dss[g], ks[g])
            dkc_ref[:, g * HD:(g + 1) * HD] = _dot_ta(dss[g], qs[g])
            dvc_ref[:, g * HD:(g + 1) * HD] = _dot_ta(pbs[g], dos[g])
            for u in range(GROUP):
                h = g * GROUP + u
                dqs[:, h * HD:(h + 1) * HD] = dq[u * BLK:(u + 1) * BLK]
        dsink_ref[...] += dsink
        c, s1, s2 = c_ref[...], s1_ref[...], s2_ref[...]
        for j in range(KV_OFF // LANES):
            sl = slice(LANES * j, LANES * (j + 1))
            dq_ref[:, sl] = _rope(dqs[:, sl], c, -s1, -s2).astype(BF16)

    row = lambda n: (n, 0)
    tab = pl.BlockSpec((BLK, LANES), row)
    (dq, dkc, dvc, dsink), got = _call(
        body, name="attn_bwd", grid=(nb,),
        in_specs=[pl.BlockSpec(memory_space=pltpu.SMEM), q_spec, prev, own, nxt,
                  pl.BlockSpec((BLK, D), row), pl.BlockSpec((BLK, D), row), tab, tab, tab],
        out_specs=[pl.BlockSpec((BLK, D), row), pl.BlockSpec((None, 3 * BLK, kvw), lambda n: (n, 0, 0)),
                   pl.BlockSpec((None, 3 * BLK, kvw), lambda n: (n, 0, 0)), pl.BlockSpec((1, N_HEADS), lambda n: (0, 0))],
        out_shape=[jax.ShapeDtypeStruct((T, D), BF16), jax.ShapeDtypeStruct((nb, 3 * BLK, kvw), F32),
                   jax.ShapeDtypeStruct((nb, 3 * BLK, kvw), F32), jax.ShapeDtypeStruct((1, N_HEADS), F32)],
        scratch_shapes=[pltpu.VMEM((BLK, D), F32)],
        semantics=("arbitrary",), args=(sink, qkv, qkv, qkv, qkv, o, do, rc, rs1, rs2), comm=comm)
    return dq, dkc, dvc, dsink, got


def kv_sum(dkc, dvc, rc, rs1, rs2):
    nb = dkc.shape[0]
    T = nb * BLK
    kvw = N_KV * HD

    def body(kp_ref, ko_ref, kn_ref, vp_ref, vo_ref, vn_ref, c_ref, s1_ref, s2_ref, out_ref):
        m = pl.program_id(0)
        has_p = (m > 0).astype(F32)
        has_n = (m < nb - 1).astype(F32)
        dk = kp_ref[...] * has_p + ko_ref[...] + kn_ref[...] * has_n
        dv = vp_ref[...] * has_p + vo_ref[...] + vn_ref[...] * has_n
        c, s1, s2 = c_ref[...], s1_ref[...], s2_ref[...]
        for j in range(kvw // LANES):
            sl = slice(LANES * j, LANES * (j + 1))
            out_ref[:, sl] = _rope(dk[:, sl], c, -s1, -s2).astype(BF16)
        out_ref[:, kvw:] = dv.astype(BF16)

    from_prev = pl.BlockSpec((None, BLK, kvw), lambda m: (jnp.maximum(m - 1, 0), 2, 0))
    from_own = pl.BlockSpec((None, BLK, kvw), lambda m: (m, 1, 0))
    from_next = pl.BlockSpec((None, BLK, kvw), lambda m: (jnp.minimum(m + 1, nb - 1), 0, 0))
    tab = pl.BlockSpec((BLK, LANES), lambda m: (m, 0))
    return pl.pallas_call(
        body, name="kv_sum", grid=(nb,),
        in_specs=[from_prev, from_own, from_next, from_prev, from_own, from_next, tab, tab, tab],
        out_specs=pl.BlockSpec((BLK, 2 * kvw), lambda m: (m, 0)),
        out_shape=jax.ShapeDtypeStruct((T, 2 * kvw), BF16),
        compiler_params=_params("parallel"),
    )(dkc, dkc, dkc, dvc, dvc, dvc, rc, rs1, rs2)


def _me():
    return lax.axis_index("x"), lax.axis_index("y"), lax.axis_index("c")


def _half_rows(ref, sharded_rows, chip, core):
    R, C = ref.shape[-2], ref.shape[-1]
    lead = (slice(None),) * (len(ref.shape) - 2)
    if sharded_rows:
        per = R // N_CHIPS
        return ref.at[lead + (pl.ds(chip * per + core * (per // 2), per // 2), slice(None))]
    per = C // N_CHIPS
    return ref.at[lead + (pl.ds(core * (R // 2), R // 2), pl.ds(chip * per, per))]


class _Gather:
    def __init__(self, shards, sharded_rows):
        self.inputs = list(shards)
        self.rows = list(sharded_rows)
        self.n = self.n_in = self.n_out = len(shards)
        self.out_shapes = []
        for s, rows in zip(shards, sharded_rows):
            shp = list(s.shape)
            shp[-2 if rows else -1] *= N_CHIPS
            self.out_shapes.append(jax.ShapeDtypeStruct(tuple(shp), s.dtype))
        self.scratch = [pltpu.SemaphoreType.DMA((self.n, 6)), pltpu.SemaphoreType.DMA((self.n, 6)),
                        pltpu.SemaphoreType.DMA((self.n, 2))]

    def _ctx(self, ins, outs, sems):
        send_sems, recv_sems, local_sems = sems
        x, y, c = _me()
        chips = [(1 - x, y), (x, 1 - y), (1 - x, 1 - y)]

        def half_src(w, core):
            s = ins[w]
            R = s.shape[-2]
            return s.at[pl.ds(core * (R // 2), R // 2), :]

        def dst(w, chip, core):
            return _half_rows(outs[w], self.rows[w], chip, core)

        def copy(w, k, src, chip, core, to):
            return pltpu.make_async_remote_copy(
                src_ref=src, dst_ref=dst(w, chip, core), send_sem=send_sems.at[w, k], recv_sem=recv_sems.at[w, k],
                device_id=to, device_id_type=MESH)

        def local(w, core):
            return pltpu.make_async_copy(half_src(w, core), dst(w, 2 * x + y, core), local_sems.at[w, core])

        def first(w, j):
            qx, qy = chips[j]
            return copy(w, j, half_src(w, c), 2 * x + y, c, (qx, qy, c))

        def landed(w, j):
            qx, qy = chips[j]
            return copy(w, j, dst(w, 2 * qx + qy, c), 2 * qx + qy, c, (x, y, c))

        def passed(w, j):
            qx, qy = chips[j]
            return copy(w, 3 + j, dst(w, 2 * qx + qy, c), 2 * qx + qy, c, (x, y, 1 - c))

        def from_sibling(w, j):
            qx, qy = chips[j]
            return copy(w, 3 + j, dst(w, 2 * qx + qy, 1 - c), 2 * qx + qy, 1 - c, (x, y, c))

        return local, first, landed, passed, from_sibling

    def start(self, ins, outs, sems):
        local, first, _, _, _ = self._ctx(ins, outs, sems)
        for w in range(self.n):
            for core in range(2):
                local(w, core).start()
            for j in range(3):
                first(w, j).start()

    def mid(self, ins, outs, sems):
        _, _, landed, passed, _ = self._ctx(ins, outs, sems)
        for w in range(self.n):
            for j in range(3):
                landed(w, j).wait_recv()
                passed(w, j).start()

    def end(self, ins, outs, sems):
        local, first, _, passed, from_sibling = self._ctx(ins, outs, sems)
        for w in range(self.n):
            for j in range(3):
                from_sibling(w, j).wait_recv()
        for w in range(self.n):
            for j in range(3):
                first(w, j).wait_send()
                passed(w, j).wait_send()
            for core in range(2):
                local(w, core).wait()


class _Scatter:
    def __init__(self, grads, small=None):
        self.inputs = list(grads) + ([small] if small is not None else [])
        self.ng = len(grads)
        self.n = self.n_in = self.n_out = len(self.inputs)
        self.out_shapes = [jax.ShapeDtypeStruct((N_DEV, g.shape[1] // 2, g.shape[2]), g.dtype) for g in grads]
        if small is not None:
            self.out_shapes.append(jax.ShapeDtypeStruct((N_DEV,) + small.shape, small.dtype))
        self.scratch = [pltpu.SemaphoreType.DMA((self.n, N_DEV)), pltpu.SemaphoreType.DMA((self.n, N_DEV)),
                        pltpu.SemaphoreType.DMA((self.n,))]

    def _ctx(self, ins, outs, sems):
        send_sems, recv_sems, local_sems = sems
        x, y, c = _me()
        me = 4 * x + 2 * y + c

        def piece(w, chip, core):
            if w >= self.ng:
                return ins[w]
            half = ins[w].shape[1] // 2
            return ins[w].at[chip, pl.ds(core * half, half), :]

        def peer_of(k):
            return x ^ ((k >> 2) & 1), y ^ ((k >> 1) & 1), c ^ (k & 1)

        def local(w):
            return pltpu.make_async_copy(piece(w, 2 * x + y, c), outs[w].at[me], local_sems.at[w])

        def send(w, k):
            px, py, pc = peer_of(k)
            return pltpu.make_async_remote_copy(
                src_ref=piece(w, 2 * px + py, pc), dst_ref=outs[w].at[me], send_sem=send_sems.at[w, k],
                recv_sem=recv_sems.at[w, k], device_id=(px, py, pc), device_id_type=MESH)

        def recv(w, k):
            px, py, pc = peer_of(k)
            return pltpu.make_async_remote_copy(
                src_ref=piece(w, 2 * x + y, c), dst_ref=outs[w].at[4 * px + 2 * py + pc], send_sem=send_sems.at[w, k],
                recv_sem=recv_sems.at[w, k], device_id=(px, py, pc), device_id_type=MESH)

        return local, send, recv

    def start(self, ins, outs, sems):
        local, send, _ = self._ctx(ins, outs, sems)
        for w in range(self.n):
            local(w).start()
            for k in range(1, N_DEV):
                send(w, k).start()

    def mid(self, ins, outs, sems):
        pass

    def end(self, ins, outs, sems):
        local, send, recv = self._ctx(ins, outs, sems)
        for w in range(self.n):
            for k in range(1, N_DEV):
                recv(w, k).wait_recv()
        for w in range(self.n):
            for k in range(1, N_DEV):
                send(w, k).wait_send()
            local(w).wait()


def exchange(plan, name):
    def body(*refs):
        ins, outs, sems = refs[:plan.n_in], refs[plan.n_in:plan.n_in + plan.n_out], refs[plan.n_in + plan.n_out:]
        plan.start(ins, outs, sems)
        plan.mid(ins, outs, sems)
        plan.end(ins, outs, sems)

    any_spec = pl.BlockSpec(memory_space=pl.ANY)
    return pl.pallas_call(
        body, name=name, in_specs=[any_spec] * plan.n_in, out_specs=[any_spec] * plan.n_out,
        out_shape=plan.out_shapes, scratch_shapes=plan.scratch,
    )(*plan.inputs)


def _call(body, *, name, grid, in_specs, out_specs, out_shape, scratch_shapes=(), semantics, args, comm=None):
    if comm is None:
        outs = pl.pallas_call(
            body, name=name, grid=grid, in_specs=in_specs, out_specs=out_specs, out_shape=out_shape,
            scratch_shapes=list(scratch_shapes), compiler_params=_params(*semantics))(*args)
        return outs, []
    n_in, n_out, n_scr = len(in_specs), len(out_specs), len(scratch_shapes)

    def at(step):
        cond = pl.program_id(0) == step[0]
        for d in range(1, len(grid)):
            cond = cond & (pl.program_id(d) == step[d])
        return cond

    first = tuple(0 for _ in grid)
    middle = (grid[0] // 2,) + tuple(g - 1 for g in grid[1:])
    last = tuple(g - 1 for g in grid)
    assert first != middle and middle != last

    def hosted(*refs):
        h_in, c_in = refs[:n_in], refs[n_in:n_in + comm.n_in]
        rest = refs[n_in + comm.n_in:]
        h_out, c_out = rest[:n_out], rest[n_out:n_out + comm.n_out]
        rest = rest[n_out + comm.n_out:]
        h_scr, c_scr = rest[:n_scr], rest[n_scr:]

        @pl.when(at(first))
        def _():
            comm.start(c_in, c_out, c_scr)

        body(*h_in, *h_out, *h_scr)

        @pl.when(at(middle))
        def _():
            comm.mid(c_in, c_out, c_scr)

        @pl.when(at(last))
        def _():
            comm.end(c_in, c_out, c_scr)

    any_spec = pl.BlockSpec(memory_space=pl.ANY)
    outs = pl.pallas_call(
        hosted, name=name, grid=grid, in_specs=list(in_specs) + [any_spec] * comm.n_in,
        out_specs=list(out_specs) + [any_spec] * comm.n_out, out_shape=list(out_shape) + comm.out_shapes,
        scratch_shapes=list(scratch_shapes) + comm.scratch,
        compiler_params=_params(*(["arbitrary"] * len(grid))))(*args, *comm.inputs)
    return outs[:n_out], outs[n_out:]


def swap_halves(halves, layers):
    n = len(halves)
    out_shapes, owner = [], []
    i = 0
    for nl in layers:
        r2, cc = halves[i].shape
        out_shapes.append(jax.ShapeDtypeStruct((nl, 2 * r2, cc), F32))
        owner += [(len(out_shapes) - 1, l) for l in range(nl)]
        i += nl
    n_out = len(out_shapes)

    def body(*refs):
        ins, outs = refs[:n], refs[n:n + n_out]
        send_sems, recv_sems, local_sems = refs[n + n_out:]
        x, y, c = _me()

        def dst(w, core):
            o, l = owner[w]
            r2 = ins[w].shape[0]
            return outs[o].at[l, pl.ds(core * r2, r2), :]

        cps = []
        for w in range(n):
            lc = pltpu.make_async_copy(ins[w], dst(w, c), local_sems.at[w])
            lc.start()
            rc = pltpu.make_async_remote_copy(src_ref=ins[w], dst_ref=dst(w, c), send_sem=send_sems.at[w],
                                              recv_sem=recv_sems.at[w], device_id=(x, y, 1 - c), device_id_type=MESH)
            rc.start()
            cps.append((lc, rc))
        for w in range(n):
            pltpu.make_async_remote_copy(src_ref=ins[w], dst_ref=dst(w, 1 - c), send_sem=send_sems.at[w],
                                         recv_sem=recv_sems.at[w], device_id=(x, y, 1 - c),
                                         device_id_type=MESH).wait_recv()
        for lc, rc in cps:
            rc.wait_send()
            lc.wait()

    any_spec = pl.BlockSpec(memory_space=pl.ANY)
    return pl.pallas_call(
        body, name="swap_halves",
        in_specs=[any_spec] * n, out_specs=[any_spec] * n_out, out_shape=out_shapes,
        scratch_shapes=[pltpu.SemaphoreType.DMA((n,)), pltpu.SemaphoreType.DMA((n,)), pltpu.SemaphoreType.DMA((n,))],
    )(*halves)


def sum_pieces(pieces, name):
    _, R, C = pieces.shape
    tr = _tile(R, 128) if R % 128 == 0 else R

    def body(p_ref, o_ref):
        acc = p_ref[0].astype(F32)
        for d in range(1, N_DEV):
            acc = acc + p_ref[d].astype(F32)
        o_ref[...] = acc

    return pl.pallas_call(
        body, name=name, grid=(R // tr,),
        in_specs=[pl.BlockSpec((N_DEV, tr, C), lambda i: (0, i, 0))],
        out_specs=pl.BlockSpec((tr, C), lambda i: (i, 0)),
        out_shape=jax.ShapeDtypeStruct((R, C), F32),
        compiler_params=_params("parallel"),
    )(pieces)


def adamw(w, g, m, v, name):
    Lyr, R, C = w.shape
    tr = _tile(R, 256) if R % 8 == 0 else R
    c1 = 1.0 / (1.0 - ADAM_B1 ** ADAM_STEP)
    c2 = 1.0 / (1.0 - ADAM_B2 ** ADAM_STEP)

    def body(w_ref, g_ref, m_ref, v_ref, d_ref, nm_ref, nv_ref):
        gv = g_ref[...]
        nm = ADAM_B1 * m_ref[...] + (1.0 - ADAM_B1) * gv
        nv = ADAM_B2 * v_ref[...] + (1.0 - ADAM_B2) * (gv * gv)
        nm_ref[...] = nm
        nv_ref[...] = nv
        d_ref[...] = -ADAM_LR * ((nm * c1) / (jnp.sqrt(nv * c2) + ADAM_EPS) + ADAM_WD * w_ref[...])

    spec = pl.BlockSpec((None, tr, C), lambda l, i: (l, i, 0))
    shp = jax.ShapeDtypeStruct(w.shape, F32)
    return pl.pallas_call(
        body, name=name, grid=(Lyr, R // tr),
        in_specs=[spec] * 4, out_specs=[spec] * 3, out_shape=[shp] * 3,
        compiler_params=_params("parallel", "parallel"),
    )(w, g, m, v)


def _rope_tables(T):
    pos = jnp.arange(T, dtype=F32)
    inv_freq = THETA ** (-jnp.arange(0, ROT, 2, dtype=F32) / ROT)
    ang = pos[:, None] * inv_freq[None, :]
    cos, sin = jnp.cos(ang), jnp.sin(ang)
    half = ROT // 2
    one = jnp.ones((T, HD - ROT), F32)
    zero = jnp.zeros((T, HD - ROT), F32)
    zh = jnp.zeros((T, half), F32)
    c = jnp.concatenate([cos, cos, one], axis=1)
    s1 = jnp.concatenate([-sin, zh, zero], axis=1)
    s2 = jnp.concatenate([zh, sin, zero], axis=1)
    two = lambda t: jnp.concatenate([t, t], axis=1)
    return two(c), two(s1), two(s2)


def kernel(x, attn_norm, attn_w_qkv, attn_w_o, attn_sink, conv_norm, conv_w_pw1, conv_b_pw1, conv_w_dw, conv_b_dw, conv_ln_g, conv_ln_b, conv_w_pw2, conv_b_pw2, ffn_norm, ffn_w_gu, ffn_w_down, final_norm, loss_target, m_attn_norm, m_attn_w_qkv, m_attn_w_o, m_attn_sink, m_conv_norm, m_conv_w_pw1, m_conv_b_pw1, m_conv_w_dw, m_conv_b_dw, m_conv_ln_g, m_conv_ln_b, m_conv_w_pw2, m_conv_b_pw2, m_ffn_norm, m_ffn_w_gu, m_ffn_w_down, m_final_norm, v_attn_norm, v_attn_w_qkv, v_attn_w_o, v_attn_sink, v_conv_norm, v_conv_w_pw1, v_conv_b_pw1, v_conv_w_dw, v_conv_b_dw, v_conv_ln_g, v_conv_ln_b, v_conv_w_pw2, v_conv_b_pw2, v_ffn_norm, v_ffn_w_gu, v_ffn_w_down, v_final_norm):
    T = x.shape[1]
    x0 = x[0]
    target = loss_target[0]
    ix, iy = lax.axis_index("x"), lax.axis_index("y")
    chip = 2 * ix + iy
    rc, rs1, rs2 = _rope_tables(T)

    bf = lambda t: t.astype(BF16)
    col_row = [False, True]
    w_qkv, w_o = exchange(_Gather([bf(attn_w_qkv[0]), bf(attn_w_o[0])], col_row), "gather_attn")

    def place(vec, width):
        return lax.dynamic_update_slice(jnp.zeros((vec.shape[0], N_CHIPS * width), F32), vec, (0, chip * width))

    small_rows = jnp.concatenate([
        place(conv_norm, 256), place(conv_b_pw1, 512).reshape(2, D), place(conv_b_dw, 256), place(conv_ln_g, 256),
        place(conv_ln_b, 256), place(conv_b_pw2, 256), jnp.zeros((1, D), F32),
        place(conv_w_dw[0], 256), jnp.zeros((1, D), F32)], axis=0)
    got = exchange(_Scatter([], small_rows), "gather_small_params")[0]
    psmall = sum_pieces(got, "sum_small_params") * 0.5
    p_conv_norm, p_b_pw1 = psmall[0:1], psmall[1:3].reshape(1, 2 * D)
    p_b_dw, p_ln_g, p_ln_b, p_b_pw2 = psmall[3:4], psmall[4:5], psmall[5:6], psmall[6:7]
    p_w_dw = psmall[8:40]

    h0, qkv = rms_qkv(x0, attn_norm, w_qkv, rc, rs1, rs2)
    sink = attn_sink[0]
    o, (w_gu0, w_down0) = attn_fwd(qkv, sink, comm=_Gather([bf(ffn_w_gu[0]), bf(ffn_w_down[0])], col_row))
    zero_b = jnp.zeros((1, D), F32)
    x1 = mm_res(o, w_o, x0, zero_b, "attn_out")
    zero_gu = jnp.zeros((1, 2 * DFF), F32)
    h1, gu0, act0, (w_pw1, w_pw2) = rms_mm_gate(
        x1, ffn_norm[0:1], w_gu0, zero_gu, DFF, True, BF16, "ffn0_up",
        comm=_Gather([bf(conv_w_pw1[0]), bf(conv_w_pw2[0])], col_row))
    x2 = mm_res(act0, w_down0, x1, zero_b, "ffn0_down")
    h2, pre, glu, _ = rms_mm_gate(x2, p_conv_norm, w_pw1, p_b_pw1, D, False, F32, "conv_pw1")
    dwc, sw, (w_gu1, w_down1) = conv_fwd(glu, p_w_dw, p_b_dw, p_ln_g, p_ln_b,
                                         comm=_Gather([bf(ffn_w_gu[1]), bf(ffn_w_down[1])], col_row))
    x3 = mm_res(sw, w_pw2, x2, p_b_pw2, "conv_pw2")
    h3, gu1, act1, _ = rms_mm_gate(x3, ffn_norm[1:2], w_gu1, zero_gu, DFF, True, BF16, "ffn1_up")
    x4 = mm_res(act1, w_down1, x3, zero_b, "ffn1_down")
    dx4, loss_part, d_final = final_loss(x4, final_norm.reshape(1, D), target)
    loss = lax.psum(loss_part[0, 0], ("x", "y", "c"))

    dgu1, _ = swiglu_bwd(dx4, w_down1, gu1, "ffn1_down_bwd")
    g_down1 = dw_row(act1, dx4, "ffn1_down_dw")
    dx3, d_ffn1 = mm_bt_rmsbwd(dgu1, w_gu1, x3, ffn_norm[1:2], dx4, "ffn1_up_bwd")
    g_gu1 = dw_col(h3, dgu1, "ffn1_up_dw")

    ddwc, d_ln_g, d_ln_b, d_b_pw2 = ln_silu_bwd(dx3, w_pw2, dwc, p_ln_g, p_ln_b)
    g_pw2 = dw_row(sw, dx3, "conv_pw2_dw")
    dpre, d_w_dw, d_b_dw, d_b_pw1, (r_gu1, r_down1) = conv_bwd(ddwc, glu, pre, p_w_dw,
                                                               comm=_Scatter([g_gu1, g_down1]))
    dx2, d_conv_norm = mm_bt_rmsbwd(dpre, w_pw1, x2, p_conv_norm, dx3, "conv_pw1_bwd")
    g_pw1 = dw_col(h2, dpre, "conv_pw1_dw")

    dgu0, (r_pw1, r_pw2) = swiglu_bwd(dx2, w_down0, gu0, "ffn0_down_bwd", comm=_Scatter([g_pw1, g_pw2]))
    g_down0 = dw_row(act0, dx2, "ffn0_down_dw")
    dx1, d_ffn0 = mm_bt_rmsbwd(dgu0, w_gu0, x1, ffn_norm[0:1], dx2, "ffn0_up_bwd")
    g_gu0 = dw_col(h1, dgu0, "ffn0_up_dw")

    do = mm_bt(dx1, w_o, "attn_out_bwd")
    g_o = dw_row(o, dx1, "attn_out_dw")
    dq, dkc, dvc, d_sink, (r_gu0, r_down0) = attn_bwd(qkv, o, do, sink, rc, rs1, rs2,
                                                      comm=_Scatter([g_gu0, g_down0]))
    dkv = kv_sum(dkc, dvc, rc, rs1, rs2)
    dqkv = jnp.concatenate([dq, dkv], axis=1)[None]
    dx0, d_attn_norm = mm_bt_rmsbwd(dqkv, w_qkv, x0, attn_norm, dx1, "attn_qkv_bwd")
    g_qkv = dw_col(h0, dqkv, "attn_qkv_dw")

    pad16 = lambda t: jnp.concatenate([t, jnp.zeros((1, D - t.shape[1]), F32)], axis=1)
    small_g = jnp.concatenate([
        d_attn_norm, pad16(d_sink), d_conv_norm, d_b_pw1.reshape(2, D), d_b_dw, d_ln_g, d_ln_b, d_b_pw2,
        d_ffn0, d_ffn1, d_final, jnp.zeros((4, D), F32), d_w_dw], axis=0)
    r_qkv, r_o, r_small = exchange(_Scatter([g_qkv, g_o], small_g), "scatter_attn")
    received = [r_qkv, r_o, r_gu0, r_gu1, r_down0, r_down1, r_pw1, r_pw2]
    names = ["qkv", "o", "gu0", "gu1", "down0", "down1", "pw1", "pw2"]
    halves = [sum_pieces(p, "sum_" + nm) for p, nm in zip(received, names)]
    gs = sum_pieces(r_small, "sum_small_grads")
    gf_qkv, gf_o, gf_gu, gf_down, gf_pw1, gf_pw2 = swap_halves(halves, [1, 1, 2, 2, 1, 1])

    def take(row0, nrows, width):
        return lax.dynamic_slice(gs, (row0, chip * width), (nrows, width))

    grads = {
        "attn_norm": gs[0:1], "attn_w_qkv": gf_qkv, "attn_w_o": gf_o, "attn_sink": gs[1:2, :N_HEADS],
        "conv_norm": take(2, 1, 256), "conv_w_pw1": gf_pw1,
        "conv_b_pw1": lax.dynamic_slice(gs[3:5].reshape(1, 2 * D), (0, chip * 512), (1, 512)),
        "conv_w_dw": take(16, 32, 256)[None, :CONV_W], "conv_b_dw": take(5, 1, 256), "conv_ln_g": take(6, 1, 256),
        "conv_ln_b": take(7, 1, 256), "conv_w_pw2": gf_pw2, "conv_b_pw2": take(8, 1, 256),
        "ffn_norm": gs[9:11], "ffn_w_gu": gf_gu, "ffn_w_down": gf_down, "final_norm": gs[11],
    }
    weights = dict(attn_norm=attn_norm, attn_w_qkv=attn_w_qkv, attn_w_o=attn_w_o, attn_sink=attn_sink,
                   conv_norm=conv_norm, conv_w_pw1=conv_w_pw1, conv_b_pw1=conv_b_pw1, conv_w_dw=conv_w_dw,
                   conv_b_dw=conv_b_dw, conv_ln_g=conv_ln_g, conv_ln_b=conv_ln_b, conv_w_pw2=conv_w_pw2,
                   conv_b_pw2=conv_b_pw2, ffn_norm=ffn_norm, ffn_w_gu=ffn_w_gu, ffn_w_down=ffn_w_down,
                   final_norm=final_norm)
    m_in = dict(attn_norm=m_attn_norm, attn_w_qkv=m_attn_w_qkv, attn_w_o=m_attn_w_o, attn_sink=m_attn_sink,
                conv_norm=m_conv_norm, conv_w_pw1=m_conv_w_pw1, conv_b_pw1=m_conv_b_pw1, conv_w_dw=m_conv_w_dw,
                conv_b_dw=m_conv_b_dw, conv_ln_g=m_conv_ln_g, conv_ln_b=m_conv_ln_b, conv_w_pw2=m_conv_w_pw2,
                conv_b_pw2=m_conv_b_pw2, ffn_norm=m_ffn_norm, ffn_w_gu=m_ffn_w_gu, ffn_w_down=m_ffn_w_down,
                final_norm=m_final_norm)
    v_in = dict(attn_norm=v_attn_norm, attn_w_qkv=v_attn_w_qkv, attn_w_o=v_attn_w_o, attn_sink=v_attn_sink,
                conv_norm=v_conv_norm, conv_w_pw1=v_conv_w_pw1, conv_b_pw1=v_conv_b_pw1, conv_w_dw=v_conv_w_dw,
                conv_b_dw=v_conv_b_dw, conv_ln_g=v_conv_ln_g, conv_ln_b=v_conv_ln_b, conv_w_pw2=v_conv_w_pw2,
                conv_b_pw2=v_conv_b_pw2, ffn_norm=v_ffn_norm, ffn_w_gu=v_ffn_w_gu, ffn_w_down=v_ffn_w_down,
                final_norm=v_final_norm)
    order = list(weights)
    g_out, d_out, m_out, v_out = [], [], [], []
    for nm in order:
        w = weights[nm]
        shape = w.shape
        as3 = lambda t: t.reshape((1,) * (3 - len(shape)) + shape) if len(shape) < 3 else t.reshape(shape)
        g3 = as3(grads[nm].reshape(shape))
        delta, nm_, nv_ = adamw(as3(w), g3, as3(m_in[nm]), as3(v_in[nm]), "adamw_" + nm)
        g_out.append(g3.reshape(shape))
        d_out.append(delta.reshape(shape))
        m_out.append(nm_.reshape(shape))
        v_out.append(nv_.reshape(shape))
    return (loss, dx0[None], *g_out, *d_out, *m_out, *v_out)
```

```python
import functools
import math

import jax
import jax.numpy as jnp
from jax import lax
from jax.experimental import pallas as pl
from jax.experimental.pallas import tpu as pltpu

F32 = jnp.float32
BF16 = jnp.bfloat16

D = 1024
N_HEADS = 16
N_KV = 4
GROUP = N_HEADS // N_KV
HD = 64
ROT = 16
THETA = 500000.0
BLK = 128
QKV = (N_HEADS + 2 * N_KV) * HD
KV_OFF = N_HEADS * HD
DFF = 2816
CONV_W = 31
CONV_PAD = 15
HALO = 16
CONV_JB = 4
EPS = 1e-6
NEG = -1e30
N_CHIPS = 4
N_DEV = 8
LANES = 128
SUBLANES = 8

ADAM_LR, ADAM_B1, ADAM_B2, ADAM_EPS, ADAM_WD, ADAM_STEP = 0.001, 0.9, 0.999, 1e-08, 0.01, 10

VMEM_LIMIT = 56 * 1024 * 1024
MESH = pl.DeviceIdType.MESH


def _params(*sem):
    return pltpu.CompilerParams(dimension_semantics=sem, vmem_limit_bytes=VMEM_LIMIT)


def _tile(n, want):
    if n <= want:
        return n
    for t in range(want, 7, -1):
        if n % t == 0 and t % 8 == 0:
            return t
    return n


def _sigmoid(v):
    return 1.0 / (1.0 + jnp.exp(-v))


def _rms_fwd(xv, gain):
    r = lax.rsqrt(jnp.mean(xv * xv, axis=-1, keepdims=True) + EPS)
    return xv * r * gain


def _rms_bwd(dh, xv, gain, dres):
    r = lax.rsqrt(jnp.mean(xv * xv, axis=-1, keepdims=True) + EPS)
    xhat = xv * r
    gy = dh * gain
    dx = r * (gy - xhat * jnp.mean(gy * xhat, axis=-1, keepdims=True))
    return dx + dres, dh * xhat


def _rope(blk, c, s1, s2):
    return blk * c + pltpu.roll(blk, LANES - ROT // 2, 1) * s1 + pltpu.roll(blk, ROT // 2, 1) * s2


def _dot(a, b):
    return jnp.dot(a, b, preferred_element_type=F32)


def _dot_tb(a, b):
    return lax.dot_general(a, b, (((1,), (1,)), ((), ())), preferred_element_type=F32)


def _dot_ta(a, b):
    return lax.dot_general(a, b, (((0,), (0,)), ((), ())), preferred_element_type=F32)


def rms_qkv(x, gain, w, rc, rs1, rs2):
    T = x.shape[0]
    tm = _tile(T, 512)

    def body(x_ref, g_ref, w_ref, c_ref, s1_ref, s2_ref, h_ref, qkv_ref):
        h = _rms_fwd(x_ref[...], g_ref[...]).astype(BF16)
        h_ref[...] = h
        acc = _dot(h, w_ref[...])
        c, s1, s2 = c_ref[...], s1_ref[...], s2_ref[...]
        n_rot = (KV_OFF + N_KV * HD) // LANES
        for j in range(n_rot):
            sl = slice(LANES * j, LANES * (j + 1))
            qkv_ref[:, sl] = _rope(acc[:, sl], c, s1, s2).astype(BF16)
        qkv_ref[:, n_rot * LANES:] = acc[:, n_rot * LANES:].astype(BF16)

    row = lambda i: (i, 0)
    full = lambda i: (0, 0)
    return pl.pallas_call(
        body, name="rms_qkv", grid=(T // tm,),
        in_specs=[pl.BlockSpec((tm, D), row), pl.BlockSpec((1, D), full), pl.BlockSpec((D, QKV), full),
                  pl.BlockSpec((tm, LANES), row), pl.BlockSpec((tm, LANES), row), pl.BlockSpec((tm, LANES), row)],
        out_specs=[pl.BlockSpec((tm, D), row), pl.BlockSpec((tm, QKV), row)],
        out_shape=[jax.ShapeDtypeStruct((T, D), BF16), jax.ShapeDtypeStruct((T, QKV), BF16)],
        compiler_params=_params("parallel"),
    )(x, gain, w, rc, rs1, rs2)


def _attn_mask(n, T):
    qi = lax.broadcasted_iota(jnp.int32, (GROUP * BLK, 3 * BLK), 0) & (BLK - 1)
    ci = lax.broadcasted_iota(jnp.int32, (GROUP * BLK, 3 * BLK), 1)
    rel = ci - BLK - qi
    key_pos = n * BLK - BLK + ci
    return (jnp.abs(rel) <= BLK) & (key_pos >= 0) & (key_pos < T)


def _stack_heads(ref, g):
    return jnp.concatenate([ref[:, (g * GROUP + u) * HD:(g * GROUP + u + 1) * HD] for u in range(GROUP)], axis=0)


def _stack_sinks(sink_ref, g):
    return jnp.concatenate([jnp.full((BLK, 1), sink_ref[g * GROUP + u], F32) for u in range(GROUP)], axis=0)


def _attn_specs(T):
    nb = T // BLK
    kv_blk = 2 * N_KV * HD
    kv_col = KV_OFF // kv_blk
    q_spec = pl.BlockSpec((BLK, KV_OFF), lambda n: (n, 0))
    prev = pl.BlockSpec((BLK, kv_blk), lambda n: (jnp.maximum(n - 1, 0), kv_col))
    own = pl.BlockSpec((BLK, kv_blk), lambda n: (n, kv_col))
    nxt = pl.BlockSpec((BLK, kv_blk), lambda n: (jnp.minimum(n + 1, nb - 1), kv_col))
    return nb, q_spec, prev, own, nxt


def attn_fwd(qkv, sink, comm=None):
    T = qkv.shape[0]
    nb, q_spec, prev, own, nxt = _attn_specs(T)
    scale = 1.0 / math.sqrt(HD)

    def body(sink_ref, q_ref, kp_ref, ko_ref, kn_ref, o_ref):
        n = pl.program_id(0)
        valid = _attn_mask(n, T)
        kv = jnp.concatenate([kp_ref[...], ko_ref[...], kn_ref[...]], axis=0)
        ss = [_dot_tb(_stack_heads(q_ref, g), kv[:, g * HD:(g + 1) * HD]) for g in range(N_KV)]
        ps = []
        for g in range(N_KV):
            s = jnp.where(valid, ss[g] * scale, NEG)
            sk = _stack_sinks(sink_ref, g)
            m = jnp.maximum(jnp.max(s, axis=-1, keepdims=True), sk)
            e = jnp.exp(s - m)
            ps.append((e / (jnp.sum(e, axis=-1, keepdims=True) + jnp.exp(sk - m))).astype(BF16))
        for g in range(N_KV):
            o = _dot(ps[g], kv[:, N_KV * HD + g * HD:N_KV * HD + (g + 1) * HD]).astype(BF16)
            for u in range(GROUP):
                h = g * GROUP + u
                o_ref[:, h * HD:(h + 1) * HD] = o[u * BLK:(u + 1) * BLK]

    (o,), got = _call(
        body, name="attn_fwd", grid=(nb,),
        in_specs=[pl.BlockSpec(memory_space=pltpu.SMEM), q_spec, prev, own, nxt],
        out_specs=[pl.BlockSpec((BLK, D), lambda n: (n, 0))],
        out_shape=[jax.ShapeDtypeStruct((T, D), BF16)],
        semantics=("parallel",), args=(sink, qkv, qkv, qkv, qkv), comm=comm)
    return o, got


def mm_res(a, w, resid, bias, name):
    T, K = a.shape
    tm = _tile(T, 512)

    def body(a_ref, w_ref, r_ref, b_ref, o_ref):
        o_ref[...] = _dot(a_ref[...], w_ref[...]) + b_ref[...] + r_ref[...]

    row = lambda i: (i, 0)
    full = lambda i: (0, 0)
    return pl.pallas_call(
        body, name=name, grid=(T // tm,),
        in_specs=[pl.BlockSpec((tm, K), row), pl.BlockSpec((K, D), full), pl.BlockSpec((tm, D), row),
                  pl.BlockSpec((1, D), full)],
        out_specs=pl.BlockSpec((tm, D), row),
        out_shape=jax.ShapeDtypeStruct((T, D), F32),
        compiler_params=_params("parallel"),
    )(a, w, resid, bias)


def rms_mm_gate(x, gain, w, bias, H, swiglu, act_dtype, name, comm=None):
    T = x.shape[0]
    tm = _tile(T, 512)
    tn = 1408 if H % 1408 == 0 else 512
    nj = H // tn

    def body(x_ref, g_ref, w1_ref, w2_ref, b1_ref, b2_ref, h_ref, pre_ref, act_ref, hs):
        @pl.when(pl.program_id(1) == 0)
        def _():
            h = _rms_fwd(x_ref[...], g_ref[...]).astype(BF16)
            hs[...] = h
            h_ref[...] = h

        h = hs[...]
        a = _dot(h, w1_ref[...]) + b1_ref[...]
        b = _dot(h, w2_ref[...]) + b2_ref[...]
        pre_ref[0] = a.astype(BF16)
        pre_ref[1] = b.astype(BF16)
        if swiglu:
            act = a * _sigmoid(a) * b
        else:
            act = a * _sigmoid(b)
        act_ref[...] = act.astype(act_dtype)

    (h, pre, act), got = _call(
        body, name=name, grid=(T // tm, nj),
        in_specs=[pl.BlockSpec((tm, D), lambda i, j: (i, 0)), pl.BlockSpec((1, D), lambda i, j: (0, 0)),
                  pl.BlockSpec((D, tn), lambda i, j: (0, j)), pl.BlockSpec((D, tn), lambda i, j: (0, nj + j)),
                  pl.BlockSpec((1, tn), lambda i, j: (0, j)), pl.BlockSpec((1, tn), lambda i, j: (0, nj + j))],
        out_specs=[pl.BlockSpec((tm, D), lambda i, j: (i, 0)), pl.BlockSpec((2, tm, tn), lambda i, j: (0, i, j)),
                   pl.BlockSpec((tm, tn), lambda i, j: (i, j))],
        out_shape=[jax.ShapeDtypeStruct((T, D), BF16), jax.ShapeDtypeStruct((2, T, H), BF16),
                   jax.ShapeDtypeStruct((T, H), act_dtype)],
        scratch_shapes=[pltpu.VMEM((tm, D), BF16)],
        semantics=("parallel", "arbitrary"), args=(x, gain, w, w, bias, bias), comm=comm)
    return h, pre, act, got


def _conv_tiles(T):
    tt = _tile(T, 512)
    return tt, tt // SUBLANES, D // LANES


def _fill_strided(ext, p, L):
    def ibody(i, carry):
        ext[i] = p[pl.ds(i + 1, SUBLANES, stride=L), :]
        return carry

    lax.fori_loop(0, L + CONV_W - 1, ibody, 0, unroll=2)


def _conv_specs(T, tt):
    main = pl.BlockSpec((tt, D), lambda i: (i, 0))
    per = tt // HALO
    prev = pl.BlockSpec((HALO, D), lambda i: (jnp.maximum(i * per - 1, 0), 0))
    nxt = pl.BlockSpec((HALO, D), lambda i: (jnp.minimum((i + 1) * per, T // HALO - 1), 0))
    return main, prev, nxt


def _fill_pad(pad, main_ref, prev_ref, next_ref, i, n_i, tt, nlt):
    keep_p = (i > 0).astype(F32)
    keep_n = (i < n_i - 1).astype(F32)
    for lt in range(nlt):
        sl = slice(lt * LANES, (lt + 1) * LANES)
        pad[lt, 0:HALO, :] = prev_ref[:, sl] * keep_p
        pad[lt, HALO:HALO + tt, :] = main_ref[:, sl]
        pad[lt, HALO + tt:2 * HALO + tt, :] = next_ref[:, sl] * keep_n


def conv_fwd(glu, w_dw, b_dw, ln_g, ln_b, comm=None):
    T = glu.shape[0]
    tt, L, nlt = _conv_tiles(T)
    n_i = T // tt
    main, prev, nxt = _conv_specs(T, tt)

    def body(x_ref, xp_ref, xn_ref, w_ref, b_ref, g_ref, bb_ref, dwc_ref, sw_ref, pad, ob, ext):
        i = pl.program_id(0)
        _fill_pad(pad, x_ref, xp_ref, xn_ref, i, n_i, tt, nlt)
        for lt in range(nlt):
            sl = slice(lt * LANES, (lt + 1) * LANES)
            o = ob.at[lt]
            _fill_strided(ext, pad.at[lt], L)
            wk = [jnp.broadcast_to(w_ref[k:k + 1, sl], (SUBLANES, LANES)) for k in range(CONV_W)]

            def jbody(jb, carry):
                j = jb * CONV_JB
                accs = [None] * CONV_JB
                for m in range(CONV_W + CONV_JB - 1):
                    e = ext[j + m]
                    for u in range(CONV_JB):
                        if 0 <= m - u < CONV_W:
                            t = e * wk[m - u]
                            accs[u] = t if accs[u] is None else accs[u] + t
                for u in range(CONV_JB):
                    o[pl.ds(j + u, SUBLANES, stride=L), :] = accs[u]
                return carry

            lax.fori_loop(0, L // CONV_JB, jbody, 0)
        y = jnp.concatenate([ob[lt] for lt in range(nlt)], axis=1) + b_ref[...]
        dwc_ref[...] = y
        mu = jnp.mean(y, axis=-1, keepdims=True)
        yc = y - mu
        var = jnp.mean(yc * yc, axis=-1, keepdims=True)
        z = yc * lax.rsqrt(var + EPS) * g_ref[...] + bb_ref[...]
        sw_ref[...] = (z * _sigmoid(z)).astype(BF16)

    full = lambda i: (0, 0)
    (dwc, sw), got = _call(
        body, name="conv_fwd", grid=(n_i,),
        in_specs=[main, prev, nxt, pl.BlockSpec((32, D), full), pl.BlockSpec((1, D), full),
                  pl.BlockSpec((1, D), full), pl.BlockSpec((1, D), full)],
        out_specs=[pl.BlockSpec((tt, D), lambda i: (i, 0)), pl.BlockSpec((tt, D), lambda i: (i, 0))],
        out_shape=[jax.ShapeDtypeStruct((T, D), F32), jax.ShapeDtypeStruct((T, D), BF16)],
        scratch_shapes=[pltpu.VMEM((nlt, tt + 2 * HALO, LANES), F32), pltpu.VMEM((nlt, tt, LANES), F32),
                        pltpu.VMEM((L + 2 * HALO, SUBLANES, LANES), F32)],
        semantics=("parallel",), args=(glu, glu, glu, w_dw, b_dw, ln_g, ln_b), comm=comm)
    return dwc, sw, got


def final_loss(x, gain, target):
    T = x.shape[0]
    tm = _tile(T, 512)

    def body(x_ref, g_ref, t_ref, dx_ref, loss_ref, dg_ref):
        @pl.when(pl.program_id(0) == 0)
        def _():
            loss_ref[...] = jnp.zeros_like(loss_ref)
            dg_ref[...] = jnp.zeros_like(dg_ref)

        xv, gain_v = x_ref[...], g_ref[...]
        err = _rms_fwd(xv, gain_v) - t_ref[...]
        part = 0.5 * jnp.sum(jnp.mean(err * err, axis=-1, keepdims=True), axis=0, keepdims=True)
        loss_ref[...] += jnp.broadcast_to(part, loss_ref.shape)
        dx, dgr = _rms_bwd(err * (1.0 / D), xv, gain_v, 0.0)
        dx_ref[...] = dx
        dg_ref[...] += jnp.sum(dgr, axis=0, keepdims=True)

    row = lambda i: (i, 0)
    full = lambda i: (0, 0)
    return pl.pallas_call(
        body, name="final_loss", grid=(T // tm,),
        in_specs=[pl.BlockSpec((tm, D), row), pl.BlockSpec((1, D), full), pl.BlockSpec((tm, D), row)],
        out_specs=[pl.BlockSpec((tm, D), row), pl.BlockSpec((1, LANES), full), pl.BlockSpec((1, D), full)],
        out_shape=[jax.ShapeDtypeStruct((T, D), F32), jax.ShapeDtypeStruct((1, LANES), F32),
                   jax.ShapeDtypeStruct((1, D), F32)],
        compiler_params=_params("arbitrary"),
    )(x, gain, target)


def swiglu_bwd(dx, w_down, pre, name, comm=None):
    T = dx.shape[0]
    H = w_down.shape[0]
    tm = _tile(T, 512)
    tn = 1408
    nj = H // tn

    def body(dx_ref, w_ref, pre_ref, dpre_ref, dxs):
        @pl.when(pl.program_id(1) == 0)
        def _():
            dxs[...] = dx_ref[...].astype(BF16)

        dact = _dot_tb(dxs[...], w_ref[...])
        g = pre_ref[0].astype(F32)
        u = pre_ref[1].astype(F32)
        sg = _sigmoid(g)
        dpre_ref[0] = (dact * u * sg * (1.0 + g * (1.0 - sg))).astype(BF16)
        dpre_ref[1] = (dact * g * sg).astype(BF16)

    (dpre,), got = _call(
        body, name=name, grid=(T // tm, nj),
        in_specs=[pl.BlockSpec((tm, D), lambda i, j: (i, 0)), pl.BlockSpec((tn, D), lambda i, j: (j, 0)),
                  pl.BlockSpec((2, tm, tn), lambda i, j: (0, i, j))],
        out_specs=[pl.BlockSpec((2, tm, tn), lambda i, j: (0, i, j))],
        out_shape=[jax.ShapeDtypeStruct((2, T, H), BF16)],
        scratch_shapes=[pltpu.VMEM((tm, D), BF16)],
        semantics=("parallel", "arbitrary"), args=(dx, w_down, pre), comm=comm)
    return dpre, got


def mm_bt_rmsbwd(dpre, w, x, gain, dres, name):
    nh, T, H = dpre.shape
    tm = _tile(T, 512)
    tk = 1408 if H % 1408 == 0 else (1024 if H % 1024 == 0 else H)
    nk = H // tk

    def body(dp_ref, w_ref, x_ref, g_ref, dres_ref, dx_ref, dg_ref, acc):
        i, hf, kk = pl.program_id(0), pl.program_id(1), pl.program_id(2)

        @pl.when((i == 0) & (hf == 0) & (kk == 0))
        def _():
            dg_ref[...] = jnp.zeros_like(dg_ref)

        @pl.when((hf == 0) & (kk == 0))
        def _():
            acc[...] = jnp.zeros_like(acc)

        acc[...] += _dot_tb(dp_ref[...], w_ref[...])

        @pl.when((hf == nh - 1) & (kk == nk - 1))
        def _():
            dx, dgr = _rms_bwd(acc[...], x_ref[...], g_ref[...], dres_ref[...])
            dx_ref[...] = dx
            dg_ref[...] += jnp.sum(dgr, axis=0, keepdims=True)

    return pl.pallas_call(
        body, name=name, grid=(T // tm, nh, nk),
        in_specs=[pl.BlockSpec((None, tm, tk), lambda i, hf, kk: (hf, i, kk)),
                  pl.BlockSpec((D, tk), lambda i, hf, kk: (0, hf * nk + kk)),
                  pl.BlockSpec((tm, D), lambda i, hf, kk: (i, 0)), pl.BlockSpec((1, D), lambda i, hf, kk: (0, 0)),
                  pl.BlockSpec((tm, D), lambda i, hf, kk: (i, 0))],
        out_specs=[pl.BlockSpec((tm, D), lambda i, hf, kk: (i, 0)), pl.BlockSpec((1, D), lambda i, hf, kk: (0, 0))],
        out_shape=[jax.ShapeDtypeStruct((T, D), F32), jax.ShapeDtypeStruct((1, D), F32)],
        scratch_shapes=[pltpu.VMEM((tm, D), F32)],
        compiler_params=_params("arbitrary", "arbitrary", "arbitrary"),
    )(dpre, w, x, gain, dres)


def dw_col(a, dpre, name):
    T = a.shape[0]
    nh, _, H = dpre.shape
    per = nh * H // N_CHIPS
    bph = N_CHIPS // nh
    tt = _tile(T, 512)
    nt = T // tt

    def body(a_ref, b_ref, o_ref, acc):
        t = pl.program_id(1)

        @pl.when(t == 0)
        def _():
            acc[...] = jnp.zeros_like(acc)

        acc[...] += _dot_ta(a_ref[...], b_ref[...])

        @pl.when(t == nt - 1)
        def _():
            o_ref[...] = acc[...].astype(BF16)

    return pl.pallas_call(
        body, name=name, grid=(N_CHIPS, nt),
        in_specs=[pl.BlockSpec((tt, D), lambda q, t: (t, 0)),
                  pl.BlockSpec((None, tt, per), lambda q, t: (q // bph, t, q % bph))],
        out_specs=pl.BlockSpec((None, D, per), lambda q, t: (q, 0, 0)),
        out_shape=jax.ShapeDtypeStruct((N_CHIPS, D, per), BF16),
        scratch_shapes=[pltpu.VMEM((D, per), F32)],
        compiler_params=_params("parallel", "arbitrary"),
    )(a, dpre)


def dw_row(a, b, name):
    T, R = a.shape
    cw = 1408 if R % 1408 == 0 else 512
    tt = _tile(T, 512)
    nt = T // tt

    def body(a_ref, b_ref, o_ref, acc):
        t = pl.program_id(1)

        @pl.when(t == 0)
        def _():
            acc[...] = jnp.zeros_like(acc)

        acc[...] += _dot_ta(a_ref[...], b_ref[...].astype(BF16))

        @pl.when(t == nt - 1)
        def _():
            o_ref[...] = acc[...].astype(BF16)

    out = pl.pallas_call(
        body, name=name, grid=(R // cw, nt),
        in_specs=[pl.BlockSpec((tt, cw), lambda q, t: (t, q)), pl.BlockSpec((tt, D), lambda q, t: (t, 0))],
        out_specs=pl.BlockSpec((cw, D), lambda q, t: (q, 0)),
        out_shape=jax.ShapeDtypeStruct((R, D), BF16),
        scratch_shapes=[pltpu.VMEM((cw, D), F32)],
        compiler_params=_params("parallel", "arbitrary"),
    )(a, b)
    return out.reshape(N_CHIPS, R // N_CHIPS, D)


def ln_silu_bwd(dx, w_pw2, dwc, ln_g, ln_b):
    T = dx.shape[0]
    tm = _tile(T, 512)

    def body(dx_ref, w_ref, y_ref, g_ref, b_ref, dy_ref, dg_ref, db_ref, dbo_ref):
        @pl.when(pl.program_id(0) == 0)
        def _():
            dg_ref[...] = jnp.zeros_like(dg_ref)
            db_ref[...] = jnp.zeros_like(db_ref)
            dbo_ref[...] = jnp.zeros_like(dbo_ref)

        dxv = dx_ref[...]
        dsw = _dot_tb(dxv.astype(BF16), w_ref[...])
        y = y_ref[...]
        mu = jnp.mean(y, axis=-1, keepdims=True)
        yc = y - mu
        rstd = lax.rsqrt(jnp.mean(yc * yc, axis=-1, keepdims=True) + EPS)
        xhat = yc * rstd
        z = xhat * g_ref[...] + b_ref[...]
        sg = _sigmoid(z)
        dz = dsw * sg * (1.0 + z * (1.0 - sg))
        dxh = dz * g_ref[...]
        dy_ref[...] = rstd * (dxh - jnp.mean(dxh, axis=-1, keepdims=True)
                              - xhat * jnp.mean(dxh * xhat, axis=-1, keepdims=True))
        dg_ref[...] += jnp.sum(dz * xhat, axis=0, keepdims=True)
        db_ref[...] += jnp.sum(dz, axis=0, keepdims=True)
        dbo_ref[...] += jnp.sum(dxv, axis=0, keepdims=True)

    row = lambda i: (i, 0)
    full = lambda i: (0, 0)
    vec = pl.BlockSpec((1, D), full)
    return pl.pallas_call(
        body, name="ln_silu_bwd", grid=(T // tm,),
        in_specs=[pl.BlockSpec((tm, D), row), pl.BlockSpec((D, D), full), pl.BlockSpec((tm, D), row), vec, vec],
        out_specs=[pl.BlockSpec((tm, D), row), vec, vec, vec],
        out_shape=[jax.ShapeDtypeStruct((T, D), F32)] + [jax.ShapeDtypeStruct((1, D), F32)] * 3,
        compiler_params=_params("arbitrary"),
    )(dx, w_pw2, dwc, ln_g, ln_b)


def conv_bwd(ddwc, glu, pre, w_dw, comm=None):
    T = ddwc.shape[0]
    tt, L, nlt = _conv_tiles(T)
    n_i = T // tt
    main, prev, nxt = _conv_specs(T, tt)

    def body(d_ref, dp_ref, dn_ref, x_ref, xp_ref, xn_ref, pre_ref, w_ref,
             dpre_ref, dw_ref, dbd_ref, dbp_ref, padd, padx, ob, extd, extx):
        i = pl.program_id(0)

        @pl.when(i == 0)
        def _():
            dw_ref[...] = jnp.zeros_like(dw_ref)
            dbd_ref[...] = jnp.zeros_like(dbd_ref)
            dbp_ref[...] = jnp.zeros_like(dbp_ref)

        _fill_pad(padd, d_ref, dp_ref, dn_ref, i, n_i, tt, nlt)
        _fill_pad(padx, x_ref, xp_ref, xn_ref, i, n_i, tt, nlt)
        for lt in range(nlt):
            sl = slice(lt * LANES, (lt + 1) * LANES)
            o = ob.at[lt]
            _fill_strided(extd, padd.at[lt], L)
            _fill_strided(extx, padx.at[lt], L)
            wk = [jnp.broadcast_to(w_ref[k:k + 1, sl], (SUBLANES, LANES)) for k in range(CONV_W)]

            def jbody(jb, accs):
                j = jb * 2
                accs = list(accs)
                d0, d1 = extd[j + CONV_PAD], extd[j + 1 + CONV_PAD]
                g0 = g1 = None
                for m in range(CONV_W + 1):
                    ed = extd[j + 2 * CONV_PAD + 1 - m]
                    ex = extx[j + m]
                    if m < CONV_W:
                        t = ed * wk[m]
                        g1 = t if g1 is None else g1 + t
                        accs[m] = accs[m] + d0 * ex
                    if m >= 1:
                        t = ed * wk[m - 1]
                        g0 = t if g0 is None else g0 + t
                        accs[m - 1] = accs[m - 1] + d1 * ex
                o[pl.ds(j, SUBLANES, stride=L), :] = g0
                o[pl.ds(j + 1, SUBLANES, stride=L), :] = g1
                return tuple(accs)

            accs = lax.fori_loop(0, L // 2, jbody, tuple(jnp.zeros((SUBLANES, LANES), F32) for _ in range(CONV_W)))
            for k in range(CONV_W):
                dw_ref[k:k + 1, sl] += jnp.sum(accs[k], axis=0, keepdims=True)
        dglu = jnp.concatenate([ob[lt] for lt in range(nlt)], axis=1)
        a = pre_ref[0].astype(F32)
        gate = pre_ref[1].astype(F32)
        sg = _sigmoid(gate)
        da = dglu * sg
        dgate = dglu * a * sg * (1.0 - sg)
        dpre_ref[0] = da.astype(BF16)
        dpre_ref[1] = dgate.astype(BF16)
        dbd_ref[...] += jnp.sum(d_ref[...], axis=0, keepdims=True)
        dbp_ref[0] += jnp.sum(da, axis=0, keepdims=True)
        dbp_ref[1] += jnp.sum(dgate, axis=0, keepdims=True)

    full = lambda i: (0, 0)
    (dpre, dw, dbd, dbp), got = _call(
        body, name="conv_bwd", grid=(n_i,),
        in_specs=[main, prev, nxt, main, prev, nxt, pl.BlockSpec((2, tt, D), lambda i: (0, i, 0)),
                  pl.BlockSpec((32, D), full)],
        out_specs=[pl.BlockSpec((2, tt, D), lambda i: (0, i, 0)), pl.BlockSpec((32, D), full),
                   pl.BlockSpec((1, D), full), pl.BlockSpec((2, 1, D), lambda i: (0, 0, 0))],
        out_shape=[jax.ShapeDtypeStruct((2, T, D), BF16), jax.ShapeDtypeStruct((32, D), F32),
                   jax.ShapeDtypeStruct((1, D), F32), jax.ShapeDtypeStruct((2, 1, D), F32)],
        scratch_shapes=[pltpu.VMEM((nlt, tt + 2 * HALO, LANES), F32), pltpu.VMEM((nlt, tt + 2 * HALO, LANES), F32),
                        pltpu.VMEM((nlt, tt, LANES), F32), pltpu.VMEM((L + 2 * HALO, SUBLANES, LANES), F32),
                        pltpu.VMEM((L + 2 * HALO, SUBLANES, LANES), F32)],
        semantics=("arbitrary",), args=(ddwc, ddwc, ddwc, glu, glu, glu, pre, w_dw), comm=comm)
    return dpre, dw, dbd, dbp, got


def mm_bt(a, w, name):
    T = a.shape[0]
    N = w.shape[0]
    tm = _tile(T, 512)

    def body(a_ref, w_ref, o_ref):
        o_ref[...] = _dot_tb(a_ref[...].astype(BF16), w_ref[...]).astype(BF16)

    return pl.pallas_call(
        body, name=name, grid=(T // tm,),
        in_specs=[pl.BlockSpec((tm, D), lambda i: (i, 0)), pl.BlockSpec((N, D), lambda i: (0, 0))],
        out_specs=pl.BlockSpec((tm, N), lambda i: (i, 0)),
        out_shape=jax.ShapeDtypeStruct((T, N), BF16),
        compiler_params=_params("parallel"),
    )(a, w)


def attn_bwd(qkv, o, do, sink, rc, rs1, rs2, comm=None):
    T = qkv.shape[0]
    nb, q_spec, prev, own, nxt = _attn_specs(T)
    scale = 1.0 / math.sqrt(HD)
    kvw = N_KV * HD

    def body(sink_ref, q_ref, kp_ref, ko_ref, kn_ref, o_ref, do_ref, c_ref, s1_ref, s2_ref,
             dq_ref, dkc_ref, dvc_ref, dsink_ref, dqs):
        n = pl.program_id(0)

        @pl.when(n == 0)
        def _():
            dsink_ref[...] = jnp.zeros_like(dsink_ref)

        valid = _attn_mask(n, T)
        kv = jnp.concatenate([kp_ref[...], ko_ref[...], kn_ref[...]], axis=0)
        lane = lax.broadcasted_iota(jnp.int32, (1, N_HEADS), 1)
        dsink = jnp.zeros((1, N_HEADS), F32)
        ks = [kv[:, g * HD:(g + 1) * HD] for g in range(N_KV)]
        qs = [_stack_heads(q_ref, g) for g in range(N_KV)]
        dos = [_stack_heads(do_ref, g) for g in range(N_KV)]
        ss = [_dot_tb(qs[g], ks[g]) for g in range(N_KV)]
        dps = [_dot_tb(dos[g], kv[:, kvw + g * HD:kvw + (g + 1) * HD]) for g in range(N_KV)]
        pbs, dss = [], []
        for g in range(N_KV):
            s = jnp.where(valid, ss[g] * scale, NEG)
            sk = _stack_sinks(sink_ref, g)
            m = jnp.maximum(jnp.max(s, axis=-1, keepdims=True), sk)
            e = jnp.exp(s - m)
            inv = 1.0 / (jnp.sum(e, axis=-1, keepdims=True) + jnp.exp(sk - m))
            p = e * inv
            delta = jnp.sum(dos[g].astype(F32) * _stack_heads(o_ref, g).astype(F32), axis=-1, keepdims=True)
            dss.append((p * (dps[g] - delta) * scale).astype(BF16))
            pbs.append(p.astype(BF16))
            dsk = jnp.exp(sk - m) * inv * delta
            for u in range(GROUP):
                part = -jnp.sum(dsk[u * BLK:(u + 1) * BLK], axis=0, keepdims=True)
                dsink = dsink + jnp.where(lane == g * GROUP + u, part, 0.0)
        for g in range(N_KV):
            dq = _dot(dss[g], ks[g])
            dkc_ref[:, g * HD:(g + 1) * HD] = _dot_ta(dss[g], qs[g])
            dvc_ref[:, g * HD:(g + 1) * HD] = _dot_ta(pbs[g], dos[g])
            for u in range(GROUP):
                h = g * GROUP + u
                dqs[:, h * HD:(h + 1) * HD] = dq[u * BLK:(u + 1) * BLK]
        dsink_ref[...] += dsink
        c, s1, s2 = c_ref[...], s1_ref[...], s2_ref[...]
        for j in range(KV_OFF // LANES):
            sl = slice(LANES * j, LANES * (j + 1))
            dq_ref[:, sl] = _rope(dqs[:, sl], c, -s1, -s2).astype(BF16)

    row = lambda n: (n, 0)
    tab = pl.BlockSpec((BLK, LANES), row)
    (dq, dkc, dvc, dsink), got = _call(
        body, name="attn_bwd", grid=(nb,),
        in_specs=[pl.BlockSpec(memory_space=pltpu.SMEM), q_spec, prev, own, nxt,
                  pl.BlockSpec((BLK, D), row), pl.BlockSpec((BLK, D), row), tab, tab, tab],
        out_specs=[pl.BlockSpec((BLK, D), row), pl.BlockSpec((None, 3 * BLK, kvw), lambda n: (n, 0, 0)),
                   pl.BlockSpec((None, 3 * BLK, kvw), lambda n: (n, 0, 0)), pl.BlockSpec((1, N_HEADS), lambda n: (0, 0))],
        out_shape=[jax.ShapeDtypeStruct((T, D), BF16), jax.ShapeDtypeStruct((nb, 3 * BLK, kvw), F32),
                   jax.ShapeDtypeStruct((nb, 3 * BLK, kvw), F32), jax.ShapeDtypeStruct((1, N_HEADS), F32)],
        scratch_shapes=[pltpu.VMEM((BLK, D), F32)],
        semantics=("arbitrary",), args=(sink, qkv, qkv, qkv, qkv, o, do, rc, rs1, rs2), comm=comm)
    return dq, dkc, dvc, dsink, got


def kv_sum(dkc, dvc, rc, rs1, rs2):
    nb = dkc.shape[0]
    T = nb * BLK
    kvw = N_KV * HD

    def body(kp_ref, ko_ref, kn_ref, vp_ref, vo_ref, vn_ref, c_ref, s1_ref, s2_ref, out_ref):
        m = pl.program_id(0)
        has_p = (m > 0).astype(F32)
        has_n = (m < nb - 1).astype(F32)
        dk = kp_ref[...] * has_p + ko_ref[...] + kn_ref[...] * has_n
        dv = vp_ref[...] * has_p + vo_ref[...] + vn_ref[...] * has_n
        c, s1, s2 = c_ref[...], s1_ref[...], s2_ref[...]
        for j in range(kvw // LANES):
            sl = slice(LANES * j, LANES * (j + 1))
            out_ref[:, sl] = _rope(dk[:, sl], c, -s1, -s2).astype(BF16)
        out_ref[:, kvw:] = dv.astype(BF16)

    from_prev = pl.BlockSpec((None, BLK, kvw), lambda m: (jnp.maximum(m - 1, 0), 2, 0))
    from_own = pl.BlockSpec((None, BLK, kvw), lambda m: (m, 1, 0))
    from_next = pl.BlockSpec((None, BLK, kvw), lambda m: (jnp.minimum(m + 1, nb - 1), 0, 0))
    tab = pl.BlockSpec((BLK, LANES), lambda m: (m, 0))
    return pl.pallas_call(
        body, name="kv_sum", grid=(nb,),
        in_specs=[from_prev, from_own, from_next, from_prev, from_own, from_next, tab, tab, tab],
        out_specs=pl.BlockSpec((BLK, 2 * kvw), lambda m: (m, 0)),
        out_shape=jax.ShapeDtypeStruct((T, 2 * kvw), BF16),
        compiler_params=_params("parallel"),
    )(dkc, dkc, dkc, dvc, dvc, dvc, rc, rs1, rs2)


def _me():
    return lax.axis_index("x"), lax.axis_index("y"), lax.axis_index("c")


def _half_rows(ref, sharded_rows, chip, core):
    R, C = ref.shape[-2], ref.shape[-1]
    lead = (slice(None),) * (len(ref.shape) - 2)
    if sharded_rows:
        per = R // N_CHIPS
        return ref.at[lead + (pl.ds(chip * per + core * (per // 2), per // 2), slice(None))]
    per = C // N_CHIPS
    return ref.at[lead + (pl.ds(core * (R // 2), R // 2), pl.ds(chip * per, per))]


class _Gather:
    def __init__(self, shards, sharded_rows):
        self.inputs = list(shards)
        self.rows = list(sharded_rows)
        self.n = self.n_in = self.n_out = len(shards)
        self.out_shapes = []
        for s, rows in zip(shards, sharded_rows):
            shp = list(s.shape)
            shp[-2 if rows else -1] *= N_CHIPS
            self.out_shapes.append(jax.ShapeDtypeStruct(tuple(shp), s.dtype))
        self.scratch = [pltpu.SemaphoreType.DMA((self.n, 6)), pltpu.SemaphoreType.DMA((self.n, 6)),
                        pltpu.SemaphoreType.DMA((self.n, 2))]

    def _ctx(self, ins, outs, sems):
        send_sems, recv_sems, local_sems = sems
        x, y, c = _me()
        chips = [(1 - x, y), (x, 1 - y), (1 - x, 1 - y)]

        def half_src(w, core):
            s = ins[w]
            R = s.shape[-2]
            return s.at[pl.ds(core * (R // 2), R // 2), :]

        def dst(w, chip, core):
            return _half_rows(outs[w], self.rows[w], chip, core)

        def copy(w, k, src, chip, core, to):
            return pltpu.make_async_remote_copy(
                src_ref=src, dst_ref=dst(w, chip, core), send_sem=send_sems.at[w, k], recv_sem=recv_sems.at[w, k],
                device_id=to, device_id_type=MESH)

        def local(w, core):
            return pltpu.make_async_copy(half_src(w, core), dst(w, 2 * x + y, core), local_sems.at[w, core])

        def first(w, j):
            qx, qy = chips[j]
            return copy(w, j, half_src(w, c), 2 * x + y, c, (qx, qy, c))

        def landed(w, j):
            qx, qy = chips[j]
            return copy(w, j, dst(w, 2 * qx + qy, c), 2 * qx + qy, c, (x, y, c))

        def passed(w, j):
            qx, qy = chips[j]
            return copy(w, 3 + j, dst(w, 2 * qx + qy, c), 2 * qx + qy, c, (x, y, 1 - c))

        def from_sibling(w, j):
            qx, qy = chips[j]
            return copy(w, 3 + j, dst(w, 2 * qx + qy, 1 - c), 2 * qx + qy, 1 - c, (x, y, c))

        return local, first, landed, passed, from_sibling

    def start(self, ins, outs, sems):
        local, first, _, _, _ = self._ctx(ins, outs, sems)
        for w in range(self.n):
            for core in range(2):
                local(w, core).start()
            for j in range(3):
                first(w, j).start()

    def mid(self, ins, outs, sems):
        _, _, landed, passed, _ = self._ctx(ins, outs, sems)
        for w in range(self.n):
            for j in range(3):
                landed(w, j).wait_recv()
                passed(w, j).start()

    def end(self, ins, outs, sems):
        local, first, _, passed, from_sibling = self._ctx(ins, outs, sems)
        for w in range(self.n):
            for j in range(3):
                from_sibling(w, j).wait_recv()
        for w in range(self.n):
            for j in range(3):
                first(w, j).wait_send()
                passed(w, j).wait_send()
            for core in range(2):
                local(w, core).wait()


class _Scatter:
    def __init__(self, grads, small=None):
        self.inputs = list(grads) + ([small] if small is not None else [])
        self.ng = len(grads)
        self.n = self.n_in = self.n_out = len(self.inputs)
        self.out_shapes = [jax.ShapeDtypeStruct((N_DEV, g.shape[1] // 2, g.shape[2]), g.dtype) for g in grads]
        if small is not None:
            self.out_shapes.append(jax.ShapeDtypeStruct((N_DEV,) + small.shape, small.dtype))
        self.scratch = [pltpu.SemaphoreType.DMA((self.n, N_DEV)), pltpu.SemaphoreType.DMA((self.n, N_DEV)),
                        pltpu.SemaphoreType.DMA((self.n,))]

    def _ctx(self, ins, outs, sems):
        send_sems, recv_sems, local_sems = sems
        x, y, c = _me()
        me = 4 * x + 2 * y + c

        def piece(w, chip, core):
            if w >= self.ng:
                return ins[w]
            half = ins[w].shape[1] // 2
            return ins[w].at[chip, pl.ds(core * half, half), :]

        def peer_of(k):
            return x ^ ((k >> 2) & 1), y ^ ((k >> 1) & 1), c ^ (k & 1)

        def local(w):
            return pltpu.make_async_copy(piece(w, 2 * x + y, c), outs[w].at[me], local_sems.at[w])

        def send(w, k):
            px, py, pc = peer_of(k)
            return pltpu.make_async_remote_copy(
                src_ref=piece(w, 2 * px + py, pc), dst_ref=outs[w].at[me], send_sem=send_sems.at[w, k],
                recv_sem=recv_sems.at[w, k], device_id=(px, py, pc), device_id_type=MESH)

        def recv(w, k):
            px, py, pc = peer_of(k)
            return pltpu.make_async_remote_copy(
                src_ref=piece(w, 2 * x + y, c), dst_ref=outs[w].at[4 * px + 2 * py + pc], send_sem=send_sems.at[w, k],
                recv_sem=recv_sems.at[w, k], device_id=(px, py, pc), device_id_type=MESH)

        return local, send, recv

    def start(self, ins, outs, sems):
        local, send, _ = self._ctx(ins, outs, sems)
        for w in range(self.n):
            local(w).start()
            for k in range(1, N_DEV):
                send(w, k).start()

    def mid(self, ins, outs, sems):
        pass

    def end(self, ins, outs, sems):
        local, send, recv = self._ctx(ins, outs, sems)
        for w in range(self.n):
            for k in range(1, N_DEV):
                recv(w, k).wait_recv()
        for w in range(self.n):
            for k in range(1, N_DEV):
                send(w, k).wait_send()
            local(w).wait()


def exchange(plan, name):
    def body(*refs):
        ins, outs, sems = refs[:plan.n_in], refs[plan.n_in:plan.n_in + plan.n_out], refs[plan.n_in + plan.n_out:]
        plan.start(ins, outs, sems)
        plan.mid(ins, outs, sems)
        plan.end(ins, outs, sems)

    any_spec = pl.BlockSpec(memory_space=pl.ANY)
    return pl.pallas_call(
        body, name=name, in_specs=[any_spec] * plan.n_in, out_specs=[any_spec] * plan.n_out,
        out_shape=plan.out_shapes, scratch_shapes=plan.scratch,
    )(*plan.inputs)


def _call(body, *, name, grid, in_specs, out_specs, out_shape, scratch_shapes=(), semantics, args, comm=None):
    if comm is None:
        outs = pl.pallas_call(
            body, name=name, grid=grid, in_specs=in_specs, out_specs=out_specs, out_shape=out_shape,
            scratch_shapes=list(scratch_shapes), compiler_params=_params(*semantics))(*args)
        return outs, []
    n_in, n_out, n_scr = len(in_specs), len(out_specs), len(scratch_shapes)

    def at(step):
        cond = pl.program_id(0) == step[0]
        for d in range(1, len(grid)):
            cond = cond & (pl.program_id(d) == step[d])
        return cond

    first = tuple(0 for _ in grid)
    middle = (grid[0] // 2,) + tuple(g - 1 for g in grid[1:])
    last = tuple(g - 1 for g in grid)
    assert first != middle and middle != last

    def hosted(*refs):
        h_in, c_in = refs[:n_in], refs[n_in:n_in + comm.n_in]
        rest = refs[n_in + comm.n_in:]
        h_out, c_out = rest[:n_out], rest[n_out:n_out + comm.n_out]
        rest = rest[n_out + comm.n_out:]
        h_scr, c_scr = rest[:n_scr], rest[n_scr:]

        @pl.when(at(first))
        def _():
            comm.start(c_in, c_out, c_scr)

        body(*h_in, *h_out, *h_scr)

        @pl.when(at(middle))
        def _():
            comm.mid(c_in, c_out, c_scr)

        @pl.when(at(last))
        def _():
            comm.end(c_in, c_out, c_scr)

    any_spec = pl.BlockSpec(memory_space=pl.ANY)
    outs = pl.pallas_call(
        hosted, name=name, grid=grid, in_specs=list(in_specs) + [any_spec] * comm.n_in,
        out_specs=list(out_specs) + [any_spec] * comm.n_out, out_shape=list(out_shape) + comm.out_shapes,
        scratch_shapes=list(scratch_shapes) + comm.scratch,
        compiler_params=_params(*(["arbitrary"] * len(grid))))(*args, *comm.inputs)
    return outs[:n_out], outs[n_out:]


def sum_swap(pieces, name):
    nl = len(pieces)
    _, r2, cc = pieces[0].shape
    tr = 128 if r2 % 128 == 0 else r2 // 2
    n = r2 // tr

    def body(*refs):
        p_refs, out = refs[:nl], refs[nl]
        slots, send_sems, local_sems, recv_sem = refs[nl + 1:]
        x, y, c = _me()
        sibling = (x, y, 1 - c)
        l, i = pl.program_id(0), pl.program_id(1)
        step = l * n + i

        def rows(st, core):
            return out.at[st // n, pl.ds(core * r2 + (st % n) * tr, tr), :]

        def copies(st):
            slot = st % 2
            local = pltpu.make_async_copy(slots.at[slot], rows(st, c), local_sems.at[slot])
            remote = pltpu.make_async_remote_copy(
                src_ref=slots.at[slot], dst_ref=rows(st, c), send_sem=send_sems.at[slot], recv_sem=recv_sem,
                device_id=sibling, device_id_type=MESH)
            return local, remote

        for ll in range(nl):
            @pl.when(l == ll)
            def _():
                acc = p_refs[ll][0].astype(F32)
                for d in range(1, N_DEV):
                    acc = acc + p_refs[ll][d].astype(F32)
                slots[step % 2] = acc

        for cp in copies(step):
            cp.start()

        @pl.when(step >= 1)
        def _():
            local, remote = copies(step - 1)
            local.wait()
            remote.wait_send()

        @pl.when(step == nl * n - 1)
        def _():
            local, remote = copies(step)
            local.wait()
            remote.wait_send()
            theirs = out.at[:, pl.ds((1 - c) * r2, r2), :]
            pltpu.make_async_remote_copy(src_ref=theirs, dst_ref=theirs, send_sem=send_sems.at[0],
                                         recv_sem=recv_sem, device_id=sibling, device_id_type=MESH).wait_recv()

    def piece_spec(ll):
        def index(l, i):
            return (0, jnp.where(l == ll, i, jnp.where(l < ll, 0, n - 1)), 0)
        return pl.BlockSpec((N_DEV, tr, cc), index)

    return pl.pallas_call(
        body, name=name, grid=(nl, n),
        in_specs=[piece_spec(ll) for ll in range(nl)],
        out_specs=pl.BlockSpec(memory_space=pl.ANY),
        out_shape=jax.ShapeDtypeStruct((nl, 2 * r2, cc), F32),
        scratch_shapes=[pltpu.VMEM((2, tr, cc), F32), pltpu.SemaphoreType.DMA((2,)), pltpu.SemaphoreType.DMA((2,)),
                        pltpu.SemaphoreType.DMA(())],
        compiler_params=_params("arbitrary", "arbitrary"),
    )(*pieces)


def sum_pieces(pieces, name):
    _, R, C = pieces.shape
    tr = _tile(R, 128) if R % 128 == 0 else R

    def body(p_ref, o_ref):
        acc = p_ref[0].astype(F32)
        for d in range(1, N_DEV):
            acc = acc + p_ref[d].astype(F32)
        o_ref[...] = acc

    return pl.pallas_call(
        body, name=name, grid=(R // tr,),
        in_specs=[pl.BlockSpec((N_DEV, tr, C), lambda i: (0, i, 0))],
        out_specs=pl.BlockSpec((tr, C), lambda i: (i, 0)),
        out_shape=jax.ShapeDtypeStruct((R, C), F32),
        compiler_params=_params("parallel"),
    )(pieces)


def adamw(w, g, m, v, name):
    Lyr, R, C = w.shape
    tr = _tile(R, 256) if R % 8 == 0 else R
    c1 = 1.0 / (1.0 - ADAM_B1 ** ADAM_STEP)
    c2 = 1.0 / (1.0 - ADAM_B2 ** ADAM_STEP)

    def body(w_ref, g_ref, m_ref, v_ref, d_ref, nm_ref, nv_ref):
        gv = g_ref[...]
        nm = ADAM_B1 * m_ref[...] + (1.0 - ADAM_B1) * gv
        nv = ADAM_B2 * v_ref[...] + (1.0 - ADAM_B2) * (gv * gv)
        nm_ref[...] = nm
        nv_ref[...] = nv
        d_ref[...] = -ADAM_LR * ((nm * c1) / (jnp.sqrt(nv * c2) + ADAM_EPS) + ADAM_WD * w_ref[...])

    spec = pl.BlockSpec((None, tr, C), lambda l, i: (l, i, 0))
    shp = jax.ShapeDtypeStruct(w.shape, F32)
    return pl.pallas_call(
        body, name=name, grid=(Lyr, R // tr),
        in_specs=[spec] * 4, out_specs=[spec] * 3, out_shape=[shp] * 3,
        compiler_params=_params("parallel", "parallel"),
    )(w, g, m, v)


def _rope_tables(T):
    pos = jnp.arange(T, dtype=F32)
    inv_freq = THETA ** (-jnp.arange(0, ROT, 2, dtype=F32) / ROT)
    ang = pos[:, None] * inv_freq[None, :]
    cos, sin = jnp.cos(ang), jnp.sin(ang)
    half = ROT // 2
    one = jnp.ones((T, HD - ROT), F32)
    zero = jnp.zeros((T, HD - ROT), F32)
    zh = jnp.zeros((T, half), F32)
    c = jnp.concatenate([cos, cos, one], axis=1)
    s1 = jnp.concatenate([-sin, zh, zero], axis=1)
    s2 = jnp.concatenate([zh, sin, zero], axis=1)
    two = lambda t: jnp.concatenate([t, t], axis=1)
    return two(c), two(s1), two(s2)


def kernel(x, attn_norm, attn_w_qkv, attn_w_o, attn_sink, conv_norm, conv_w_pw1, conv_b_pw1, conv_w_dw, conv_b_dw, conv_ln_g, conv_ln_b, conv_w_pw2, conv_b_pw2, ffn_norm, ffn_w_gu, ffn_w_down, final_norm, loss_target, m_attn_norm, m_attn_w_qkv, m_attn_w_o, m_attn_sink, m_conv_norm, m_conv_w_pw1, m_conv_b_pw1, m_conv_w_dw, m_conv_b_dw, m_conv_ln_g, m_conv_ln_b, m_conv_w_pw2, m_conv_b_pw2, m_ffn_norm, m_ffn_w_gu, m_ffn_w_down, m_final_norm, v_attn_norm, v_attn_w_qkv, v_attn_w_o, v_attn_sink, v_conv_norm, v_conv_w_pw1, v_conv_b_pw1, v_conv_w_dw, v_conv_b_dw, v_conv_ln_g, v_conv_ln_b, v_conv_w_pw2, v_conv_b_pw2, v_ffn_norm, v_ffn_w_gu, v_ffn_w_down, v_final_norm):
    T = x.shape[1]
    x0 = x[0]
    target = loss_target[0]
    ix, iy = lax.axis_index("x"), lax.axis_index("y")
    chip = 2 * ix + iy
    rc, rs1, rs2 = _rope_tables(T)

    bf = lambda t: t.astype(BF16)
    col_row = [False, True]
    w_qkv, w_o = exchange(_Gather([bf(attn_w_qkv[0]), bf(attn_w_o[0])], col_row), "gather_attn")

    def place(vec, width):
        return lax.dynamic_update_slice(jnp.zeros((vec.shape[0], N_CHIPS * width), F32), vec, (0, chip * width))

    small_rows = jnp.concatenate([
        place(conv_norm, 256), place(conv_b_pw1, 512).reshape(2, D), place(conv_b_dw, 256), place(conv_ln_g, 256),
        place(conv_ln_b, 256), place(conv_b_pw2, 256), jnp.zeros((1, D), F32),
        place(conv_w_dw[0], 256), jnp.zeros((1, D), F32)], axis=0)
    got = exchange(_Scatter([], small_rows), "gather_small_params")[0]
    psmall = sum_pieces(got, "sum_small_params") * 0.5
    p_conv_norm, p_b_pw1 = psmall[0:1], psmall[1:3].reshape(1, 2 * D)
    p_b_dw, p_ln_g, p_ln_b, p_b_pw2 = psmall[3:4], psmall[4:5], psmall[5:6], psmall[6:7]
    p_w_dw = psmall[8:40]

    h0, qkv = rms_qkv(x0, attn_norm, w_qkv, rc, rs1, rs2)
    sink = attn_sink[0]
    o, (w_gu0, w_down0) = attn_fwd(qkv, sink, comm=_Gather([bf(ffn_w_gu[0]), bf(ffn_w_down[0])], col_row))
    zero_b = jnp.zeros((1, D), F32)
    x1 = mm_res(o, w_o, x0, zero_b, "attn_out")
    zero_gu = jnp.zeros((1, 2 * DFF), F32)
    h1, gu0, act0, (w_pw1, w_pw2) = rms_mm_gate(
        x1, ffn_norm[0:1], w_gu0, zero_gu, DFF, True, BF16, "ffn0_up",
        comm=_Gather([bf(conv_w_pw1[0]), bf(conv_w_pw2[0])], col_row))
    x2 = mm_res(act0, w_down0, x1, zero_b, "ffn0_down")
    h2, pre, glu, _ = rms_mm_gate(x2, p_conv_norm, w_pw1, p_b_pw1, D, False, F32, "conv_pw1")
    dwc, sw, (w_gu1, w_down1) = conv_fwd(glu, p_w_dw, p_b_dw, p_ln_g, p_ln_b,
                                         comm=_Gather([bf(ffn_w_gu[1]), bf(ffn_w_down[1])], col_row))
    x3 = mm_res(sw, w_pw2, x2, p_b_pw2, "conv_pw2")
    h3, gu1, act1, _ = rms_mm_gate(x3, ffn_norm[1:2], w_gu1, zero_gu, DFF, True, BF16, "ffn1_up")
    x4 = mm_res(act1, w_down1, x3, zero_b, "ffn1_down")
    dx4, loss_part, d_final = final_loss(x4, final_norm.reshape(1, D), target)
    loss = lax.psum(loss_part[0, 0], ("x", "y", "c"))

    dgu1, _ = swiglu_bwd(dx4, w_down1, gu1, "ffn1_down_bwd")
    g_down1 = dw_row(act1, dx4, "ffn1_down_dw")
    dx3, d_ffn1 = mm_bt_rmsbwd(dgu1, w_gu1, x3, ffn_norm[1:2], dx4, "ffn1_up_bwd")
    g_gu1 = dw_col(h3, dgu1, "ffn1_up_dw")

    ddwc, d_ln_g, d_ln_b, d_b_pw2 = ln_silu_bwd(dx3, w_pw2, dwc, p_ln_g, p_ln_b)
    g_pw2 = dw_row(sw, dx3, "conv_pw2_dw")
    dpre, d_w_dw, d_b_dw, d_b_pw1, (r_gu1, r_down1) = conv_bwd(ddwc, glu, pre, p_w_dw,
                                                               comm=_Scatter([g_gu1, g_down1]))
    dx2, d_conv_norm = mm_bt_rmsbwd(dpre, w_pw1, x2, p_conv_norm, dx3, "conv_pw1_bwd")
    g_pw1 = dw_col(h2, dpre, "conv_pw1_dw")

    dgu0, (r_pw1, r_pw2) = swiglu_bwd(dx2, w_down0, gu0, "ffn0_down_bwd", comm=_Scatter([g_pw1, g_pw2]))
    g_down0 = dw_row(act0, dx2, "ffn0_down_dw")
    dx1, d_ffn0 = mm_bt_rmsbwd(dgu0, w_gu0, x1, ffn_norm[0:1], dx2, "ffn0_up_bwd")
    g_gu0 = dw_col(h1, dgu0, "ffn0_up_dw")

    do = mm_bt(dx1, w_o, "attn_out_bwd")
    g_o = dw_row(o, dx1, "attn_out_dw")
    dq, dkc, dvc, d_sink, (r_gu0, r_down0) = attn_bwd(qkv, o, do, sink, rc, rs1, rs2,
                                                      comm=_Scatter([g_gu0, g_down0]))
    dkv = kv_sum(dkc, dvc, rc, rs1, rs2)
    dqkv = jnp.concatenate([dq, dkv], axis=1)[None]
    dx0, d_attn_norm = mm_bt_rmsbwd(dqkv, w_qkv, x0, attn_norm, dx1, "attn_qkv_bwd")
    g_qkv = dw_col(h0, dqkv, "attn_qkv_dw")

    pad16 = lambda t: jnp.concatenate([t, jnp.zeros((1, D - t.shape[1]), F32)], axis=1)
    small_g = jnp.concatenate([
        d_attn_norm, pad16(d_sink), d_conv_norm, d_b_pw1.reshape(2, D), d_b_dw, d_ln_g, d_ln_b, d_b_pw2,
        d_ffn0, d_ffn1, d_final, jnp.zeros((4, D), F32), d_w_dw], axis=0)
    r_qkv, r_o, r_small = exchange(_Scatter([g_qkv, g_o], small_g), "scatter_attn")
    gf_gu = sum_swap([r_gu0, r_gu1], "sum_gu")
    gf_down = sum_swap([r_down0, r_down1], "sum_down")
    gf_pw1, gf_pw2 = sum_swap([r_pw1], "sum_pw1"), sum_swap([r_pw2], "sum_pw2")
    gf_qkv, gf_o = sum_swap([r_qkv], "sum_qkv"), sum_swap([r_o], "sum_o")
    gs = sum_pieces(r_small, "sum_small_grads")

    def take(row0, nrows, width):
        return lax.dynamic_slice(gs, (row0, chip * width), (nrows, width))

    grads = {
        "attn_norm": gs[0:1], "attn_w_qkv": gf_qkv, "attn_w_o": gf_o, "attn_sink": gs[1:2, :N_HEADS],
        "conv_norm": take(2, 1, 256), "conv_w_pw1": gf_pw1,
        "conv_b_pw1": lax.dynamic_slice(gs[3:5].reshape(1, 2 * D), (0, chip * 512), (1, 512)),
        "conv_w_dw": take(16, 32, 256)[None, :CONV_W], "conv_b_dw": take(5, 1, 256), "conv_ln_g": take(6, 1, 256),
        "conv_ln_b": take(7, 1, 256), "conv_w_pw2": gf_pw2, "conv_b_pw2": take(8, 1, 256),
        "ffn_norm": gs[9:11], "ffn_w_gu": gf_gu, "ffn_w_down": gf_down, "final_norm": gs[11],
    }
    weights = dict(attn_norm=attn_norm, attn_w_qkv=attn_w_qkv, attn_w_o=attn_w_o, attn_sink=attn_sink,
                   conv_norm=conv_norm, conv_w_pw1=conv_w_pw1, conv_b_pw1=conv_b_pw1, conv_w_dw=conv_w_dw,
                   conv_b_dw=conv_b_dw, conv_ln_g=conv_ln_g, conv_ln_b=conv_ln_b, conv_w_pw2=conv_w_pw2,
                   conv_b_pw2=conv_b_pw2, ffn_norm=ffn_norm, ffn_w_gu=ffn_w_gu, ffn_w_down=ffn_w_down,
                   final_norm=final_norm)
    m_in = dict(attn_norm=m_attn_norm, attn_w_qkv=m_attn_w_qkv, attn_w_o=m_attn_w_o, attn_sink=m_attn_sink,
                conv_norm=m_conv_norm, conv_w_pw1=m_conv_w_pw1, conv_b_pw1=m_conv_b_pw1, conv_w_dw=m_conv_w_dw,
                conv_b_dw=m_conv_b_dw, conv_ln_g=m_conv_ln_g, conv_ln_b=m_conv_ln_b, conv_w_pw2=m_conv_w_pw2,
                conv_b_pw2=m_conv_b_pw2, ffn_norm=m_ffn_norm, ffn_w_gu=m_ffn_w_gu, ffn_w_down=m_ffn_w_down,
                final_norm=m_final_norm)
    v_in = dict(attn_norm=v_attn_norm, attn_w_qkv=v_attn_w_qkv, attn_w_o=v_attn_w_o, attn_sink=v_attn_sink,
                conv_norm=v_conv_norm, conv_w_pw1=v_conv_w_pw1, conv_b_pw1=v_conv_b_pw1, conv_w_dw=v_conv_w_dw,
                conv_b_dw=v_conv_b_dw, conv_ln_g=v_conv_ln_g, conv_ln_b=v_conv_ln_b, conv_w_pw2=v_conv_w_pw2,
                conv_b_pw2=v_conv_b_pw2, ffn_norm=v_ffn_norm, ffn_w_gu=v_ffn_w_gu, ffn_w_down=v_ffn_w_down,
                final_norm=v_final_norm)
    order = list(weights)
    g_out, d_out, m_out, v_out = [], [], [], []
    for nm in order:
        w = weights[nm]
        shape = w.shape
        as3 = lambda t: t.reshape((1,) * (3 - len(shape)) + shape) if len(shape) < 3 else t.reshape(shape)
        g3 = as3(grads[nm].reshape(shape))
        delta, nm_, nv_ = adamw(as3(w), g3, as3(m_in[nm]), as3(v_in[nm]), "adamw_" + nm)
        g_out.append(g3.reshape(shape))
        d_out.append(delta.reshape(shape))
        m_out.append(nm_.reshape(shape))
        v_out.append(nv_.reshape(shape))
    return (loss, dx0[None], *g_out, *d_out, *m_out, *v_out)
```

```python
import functools
import math

import jax
import jax.numpy as jnp
from jax import lax
from jax.experimental import pallas as pl
from jax.experimental.pallas import tpu as pltpu

F32 = jnp.float32
BF16 = jnp.bfloat16

D = 1024
N_HEADS = 16
N_KV = 4
GROUP = N_HEADS // N_KV
HD = 64
ROT = 16
THETA = 500000.0
BLK = 128
QKV = (N_HEADS + 2 * N_KV) * HD
KV_OFF = N_HEADS * HD
DFF = 2816
CONV_W = 31
CONV_PAD = 15
HALO = 16
CONV_JB = 4
EPS = 1e-6
NEG = -1e30
N_CHIPS = 4
N_DEV = 8
LANES = 128
SUBLANES = 8

ADAM_LR, ADAM_B1, ADAM_B2, ADAM_EPS, ADAM_WD, ADAM_STEP = 0.001, 0.9, 0.999, 1e-08, 0.01, 10

VMEM_LIMIT = 56 * 1024 * 1024
MESH = pl.DeviceIdType.MESH


def _params(*sem):
    return pltpu.CompilerParams(dimension_semantics=sem, vmem_limit_bytes=VMEM_LIMIT)


def _tile(n, want):
    if n <= want:
        return n
    for t in range(want, 7, -1):
        if n % t == 0 and t % 8 == 0:
            return t
    return n


def _sigmoid(v):
    return 1.0 / (1.0 + jnp.exp(-v))


def _rms_fwd(xv, gain):
    r = lax.rsqrt(jnp.mean(xv * xv, axis=-1, keepdims=True) + EPS)
    return xv * r * gain


def _rms_bwd(dh, xv, gain, dres):
    r = lax.rsqrt(jnp.mean(xv * xv, axis=-1, keepdims=True) + EPS)
    xhat = xv * r
    gy = dh * gain
    dx = r * (gy - xhat * jnp.mean(gy * xhat, axis=-1, keepdims=True))
    return dx + dres, dh * xhat


def _rope(blk, c, s1, s2):
    return blk * c + pltpu.roll(blk, LANES - ROT // 2, 1) * s1 + pltpu.roll(blk, ROT // 2, 1) * s2


def _dot(a, b):
    return jnp.dot(a, b, preferred_element_type=F32)


def _dot_tb(a, b):
    return lax.dot_general(a, b, (((1,), (1,)), ((), ())), preferred_element_type=F32)


def _dot_ta(a, b):
    return lax.dot_general(a, b, (((0,), (0,)), ((), ())), preferred_element_type=F32)


def rms_qkv(x, gain, w, rc, rs1, rs2):
    T = x.shape[0]
    tm = _tile(T, 512)

    def body(x_ref, g_ref, w_ref, c_ref, s1_ref, s2_ref, h_ref, qkv_ref):
        h = _rms_fwd(x_ref[...], g_ref[...]).astype(BF16)
        h_ref[...] = h
        acc = _dot(h, w_ref[...])
        c, s1, s2 = c_ref[...], s1_ref[...], s2_ref[...]
        n_rot = (KV_OFF + N_KV * HD) // LANES
        for j in range(n_rot):
            sl = slice(LANES * j, LANES * (j + 1))
            qkv_ref[:, sl] = _rope(acc[:, sl], c, s1, s2).astype(BF16)
        qkv_ref[:, n_rot * LANES:] = acc[:, n_rot * LANES:].astype(BF16)

    row = lambda i: (i, 0)
    full = lambda i: (0, 0)
    return pl.pallas_call(
        body, name="rms_qkv", grid=(T // tm,),
        in_specs=[pl.BlockSpec((tm, D), row), pl.BlockSpec((1, D), full), pl.BlockSpec((D, QKV), full),
                  *_tab_specs(tm)],
        out_specs=[pl.BlockSpec((tm, D), row), pl.BlockSpec((tm, QKV), row)],
        out_shape=[jax.ShapeDtypeStruct((T, D), BF16), jax.ShapeDtypeStruct((T, QKV), BF16)],
        compiler_params=_params("parallel"),
    )(x, gain, w, rc, rs1, rs2)


def _attn_mask(n, T):
    qi = lax.broadcasted_iota(jnp.int32, (GROUP * BLK, 3 * BLK), 0) & (BLK - 1)
    ci = lax.broadcasted_iota(jnp.int32, (GROUP * BLK, 3 * BLK), 1)
    rel = ci - BLK - qi
    key_pos = n * BLK - BLK + ci
    return (jnp.abs(rel) <= BLK) & (key_pos >= 0) & (key_pos < T)


def _stack_heads(ref, g):
    return jnp.concatenate([ref[:, (g * GROUP + u) * HD:(g * GROUP + u + 1) * HD] for u in range(GROUP)], axis=0)


def _stack_sinks(sink_ref, g):
    return jnp.concatenate([jnp.full((BLK, 1), sink_ref[g * GROUP + u], F32) for u in range(GROUP)], axis=0)


def _attn_specs(T):
    nb = T // BLK
    kv_blk = 2 * N_KV * HD
    kv_col = KV_OFF // kv_blk
    q_spec = pl.BlockSpec((BLK, KV_OFF), lambda n: (n, 0))
    prev = pl.BlockSpec((BLK, kv_blk), lambda n: (jnp.maximum(n - 1, 0), kv_col))
    own = pl.BlockSpec((BLK, kv_blk), lambda n: (n, kv_col))
    nxt = pl.BlockSpec((BLK, kv_blk), lambda n: (jnp.minimum(n + 1, nb - 1), kv_col))
    return nb, q_spec, prev, own, nxt


def attn_fwd(qkv, sink, comm=None):
    T = qkv.shape[0]
    nb, q_spec, prev, own, nxt = _attn_specs(T)
    scale = 1.0 / math.sqrt(HD)

    def body(sink_ref, q_ref, kp_ref, ko_ref, kn_ref, o_ref):
        n = pl.program_id(0)
        valid = _attn_mask(n, T)
        kv = jnp.concatenate([kp_ref[...], ko_ref[...], kn_ref[...]], axis=0)
        ss = [_dot_tb(_stack_heads(q_ref, g), kv[:, g * HD:(g + 1) * HD]) for g in range(N_KV)]
        ps = []
        for g in range(N_KV):
            s = jnp.where(valid, ss[g] * scale, NEG)
            sk = _stack_sinks(sink_ref, g)
            m = jnp.maximum(jnp.max(s, axis=-1, keepdims=True), sk)
            e = jnp.exp(s - m)
            ps.append((e / (jnp.sum(e, axis=-1, keepdims=True) + jnp.exp(sk - m))).astype(BF16))
        for g in range(N_KV):
            o = _dot(ps[g], kv[:, N_KV * HD + g * HD:N_KV * HD + (g + 1) * HD]).astype(BF16)
            for u in range(GROUP):
                h = g * GROUP + u
                o_ref[:, h * HD:(h + 1) * HD] = o[u * BLK:(u + 1) * BLK]

    (o,), got = _call(
        body, name="attn_fwd", grid=(nb,),
        in_specs=[pl.BlockSpec(memory_space=pltpu.SMEM), q_spec, prev, own, nxt],
        out_specs=[pl.BlockSpec((BLK, D), lambda n: (n, 0))],
        out_shape=[jax.ShapeDtypeStruct((T, D), BF16)],
        semantics=("parallel",), args=(sink, qkv, qkv, qkv, qkv), comm=comm)
    return o, got


def mm_res(a, w, resid, bias, name):
    T, K = a.shape
    tm = _tile(T, 512)

    def body(a_ref, w_ref, r_ref, b_ref, o_ref):
        o_ref[...] = _dot(a_ref[...], w_ref[...]) + b_ref[...] + r_ref[...]

    row = lambda i: (i, 0)
    full = lambda i: (0, 0)
    return pl.pallas_call(
        body, name=name, grid=(T // tm,),
        in_specs=[pl.BlockSpec((tm, K), row), pl.BlockSpec((K, D), full), pl.BlockSpec((tm, D), row),
                  pl.BlockSpec((1, D), full)],
        out_specs=pl.BlockSpec((tm, D), row),
        out_shape=jax.ShapeDtypeStruct((T, D), F32),
        compiler_params=_params("parallel"),
    )(a, w, resid, bias)


def rms_mm_gate(x, gain, w, bias, H, swiglu, act_dtype, name, comm=None):
    T = x.shape[0]
    tm = _tile(T, 512)
    tn = 1408 if H % 1408 == 0 else H
    nj = H // tn
    hw = D // nj

    def body(x_ref, g_ref, w1_ref, w2_ref, b1_ref, b2_ref, h_ref, pre_ref, act_ref):
        h = _rms_fwd(x_ref[...], g_ref[...]).astype(BF16)
        for jj in range(nj):
            @pl.when(pl.program_id(0) == jj)
            def _():
                h_ref[...] = h[:, jj * hw:(jj + 1) * hw]

        a = _dot(h, w1_ref[...]) + b1_ref[...]
        b = _dot(h, w2_ref[...]) + b2_ref[...]
        pre_ref[0] = a.astype(BF16)
        pre_ref[1] = b.astype(BF16)
        if swiglu:
            act = a * _sigmoid(a) * b
        else:
            act = a * _sigmoid(b)
        act_ref[...] = act.astype(act_dtype)

    (h, pre, act), got = _call(
        body, name=name, grid=(nj, T // tm),
        in_specs=[pl.BlockSpec((tm, D), lambda j, i: (i, 0)), pl.BlockSpec((1, D), lambda j, i: (0, 0)),
                  pl.BlockSpec((D, tn), lambda j, i: (0, j)), pl.BlockSpec((D, tn), lambda j, i: (0, nj + j)),
                  pl.BlockSpec((1, tn), lambda j, i: (0, j)), pl.BlockSpec((1, tn), lambda j, i: (0, nj + j))],
        out_specs=[pl.BlockSpec((tm, hw), lambda j, i: (i, j)), pl.BlockSpec((2, tm, tn), lambda j, i: (0, i, j)),
                   pl.BlockSpec((tm, tn), lambda j, i: (i, j))],
        out_shape=[jax.ShapeDtypeStruct((T, D), BF16), jax.ShapeDtypeStruct((2, T, H), BF16),
                   jax.ShapeDtypeStruct((T, H), act_dtype)],
        semantics=("parallel", "parallel"), args=(x, gain, w, w, bias, bias), comm=comm)
    return h, pre, act, got


def _conv_tiles(T):
    tt = _tile(T, 512)
    return tt, tt // SUBLANES, D // LANES


def _fill_strided(ext, p, L):
    def ibody(i, carry):
        ext[i] = p[pl.ds(i + 1, SUBLANES, stride=L), :]
        return carry

    lax.fori_loop(0, L + CONV_W - 1, ibody, 0, unroll=2)


def _conv_specs(T, tt):
    main = pl.BlockSpec((tt, D), lambda i: (i, 0))
    per = tt // HALO
    prev = pl.BlockSpec((HALO, D), lambda i: (jnp.maximum(i * per - 1, 0), 0))
    nxt = pl.BlockSpec((HALO, D), lambda i: (jnp.minimum((i + 1) * per, T // HALO - 1), 0))
    return main, prev, nxt


def _fill_pad(pad, main_ref, prev_ref, next_ref, i, n_i, tt, nlt):
    keep_p = (i > 0).astype(F32)
    keep_n = (i < n_i - 1).astype(F32)
    for lt in range(nlt):
        sl = slice(lt * LANES, (lt + 1) * LANES)
        pad[lt, 0:HALO, :] = prev_ref[:, sl] * keep_p
        pad[lt, HALO:HALO + tt, :] = main_ref[:, sl]
        pad[lt, HALO + tt:2 * HALO + tt, :] = next_ref[:, sl] * keep_n


def conv_fwd(glu, w_dw, b_dw, ln_g, ln_b, comm=None):
    T = glu.shape[0]
    tt, L, nlt = _conv_tiles(T)
    n_i = T // tt
    main, prev, nxt = _conv_specs(T, tt)

    def body(x_ref, xp_ref, xn_ref, w_ref, b_ref, g_ref, bb_ref, dwc_ref, sw_ref, pad, ob, ext):
        i = pl.program_id(0)
        _fill_pad(pad, x_ref, xp_ref, xn_ref, i, n_i, tt, nlt)
        for lt in range(nlt):
            sl = slice(lt * LANES, (lt + 1) * LANES)
            o = ob.at[lt]
            _fill_strided(ext, pad.at[lt], L)
            wk = [jnp.broadcast_to(w_ref[k:k + 1, sl], (SUBLANES, LANES)) for k in range(CONV_W)]

            def jbody(jb, carry):
                j = jb * CONV_JB
                accs = [None] * CONV_JB
                for m in range(CONV_W + CONV_JB - 1):
                    e = ext[j + m]
                    for u in range(CONV_JB):
                        if 0 <= m - u < CONV_W:
                            t = e * wk[m - u]
                            accs[u] = t if accs[u] is None else accs[u] + t
                for u in range(CONV_JB):
                    o[pl.ds(j + u, SUBLANES, stride=L), :] = accs[u]
                return carry

            lax.fori_loop(0, L // CONV_JB, jbody, 0)
        y = jnp.concatenate([ob[lt] for lt in range(nlt)], axis=1) + b_ref[...]
        dwc_ref[...] = y
        mu = jnp.mean(y, axis=-1, keepdims=True)
        yc = y - mu
        var = jnp.mean(yc * yc, axis=-1, keepdims=True)
        z = yc * lax.rsqrt(var + EPS) * g_ref[...] + bb_ref[...]
        sw_ref[...] = (z * _sigmoid(z)).astype(BF16)

    full = lambda i: (0, 0)
    (dwc, sw), got = _call(
        body, name="conv_fwd", grid=(n_i,),
        in_specs=[main, prev, nxt, pl.BlockSpec((32, D), full), pl.BlockSpec((1, D), full),
                  pl.BlockSpec((1, D), full), pl.BlockSpec((1, D), full)],
        out_specs=[pl.BlockSpec((tt, D), lambda i: (i, 0)), pl.BlockSpec((tt, D), lambda i: (i, 0))],
        out_shape=[jax.ShapeDtypeStruct((T, D), F32), jax.ShapeDtypeStruct((T, D), BF16)],
        scratch_shapes=[pltpu.VMEM((nlt, tt + 2 * HALO, LANES), F32), pltpu.VMEM((nlt, tt, LANES), F32),
                        pltpu.VMEM((L + 2 * HALO, SUBLANES, LANES), F32)],
        semantics=("parallel",), args=(glu, glu, glu, w_dw, b_dw, ln_g, ln_b), comm=comm)
    return dwc, sw, got


def final_loss(x, gain, target):
    T = x.shape[0]
    tm = _tile(T, 512)

    def body(x_ref, g_ref, t_ref, dx_ref, loss_ref, dg_ref):
        @pl.when(pl.program_id(0) == 0)
        def _():
            loss_ref[...] = jnp.zeros_like(loss_ref)
            dg_ref[...] = jnp.zeros_like(dg_ref)

        xv, gain_v = x_ref[...], g_ref[...]
        err = _rms_fwd(xv, gain_v) - t_ref[...]
        part = 0.5 * jnp.sum(jnp.mean(err * err, axis=-1, keepdims=True), axis=0, keepdims=True)
        loss_ref[...] += jnp.broadcast_to(part, loss_ref.shape)
        dx, dgr = _rms_bwd(err * (1.0 / D), xv, gain_v, 0.0)
        dx_ref[...] = dx
        dg_ref[...] += jnp.sum(dgr, axis=0, keepdims=True)

    row = lambda i: (i, 0)
    full = lambda i: (0, 0)
    return pl.pallas_call(
        body, name="final_loss", grid=(T // tm,),
        in_specs=[pl.BlockSpec((tm, D), row), pl.BlockSpec((1, D), full), pl.BlockSpec((tm, D), row)],
        out_specs=[pl.BlockSpec((tm, D), row), pl.BlockSpec((1, LANES), full), pl.BlockSpec((1, D), full)],
        out_shape=[jax.ShapeDtypeStruct((T, D), F32), jax.ShapeDtypeStruct((1, LANES), F32),
                   jax.ShapeDtypeStruct((1, D), F32)],
        compiler_params=_params("arbitrary"),
    )(x, gain, target)


def swiglu_bwd(dx, w_down, pre, name, comm=None):
    T = dx.shape[0]
    H = w_down.shape[0]
    tm = _tile(T, 512)
    tn = 1408
    nj = H // tn

    def body(dx_ref, w_ref, pre_ref, dpre_ref):
        dact = _dot_tb(dx_ref[...].astype(BF16), w_ref[...])
        g = pre_ref[0].astype(F32)
        u = pre_ref[1].astype(F32)
        sg = _sigmoid(g)
        dpre_ref[0] = (dact * u * sg * (1.0 + g * (1.0 - sg))).astype(BF16)
        dpre_ref[1] = (dact * g * sg).astype(BF16)

    (dpre,), got = _call(
        body, name=name, grid=(nj, T // tm),
        in_specs=[pl.BlockSpec((tm, D), lambda j, i: (i, 0)), pl.BlockSpec((tn, D), lambda j, i: (j, 0)),
                  pl.BlockSpec((2, tm, tn), lambda j, i: (0, i, j))],
        out_specs=[pl.BlockSpec((2, tm, tn), lambda j, i: (0, i, j))],
        out_shape=[jax.ShapeDtypeStruct((2, T, H), BF16)],
        semantics=("parallel", "parallel"), args=(dx, w_down, pre), comm=comm)
    return dpre, got


def mm_bt_rmsbwd(dpre, w, x, gain, dres, name, comm=None):
    nh, T, H = dpre.shape
    tm = _tile(T, 512)
    tk = 1408 if H % 1408 == 0 else (1024 if H % 1024 == 0 else H)
    nk = H // tk

    def body(dp_ref, w_ref, x_ref, g_ref, dres_ref, dx_ref, dg_ref, acc):
        i, hf, kk = pl.program_id(0), pl.program_id(1), pl.program_id(2)

        @pl.when((i == 0) & (hf == 0) & (kk == 0))
        def _():
            dg_ref[...] = jnp.zeros_like(dg_ref)

        @pl.when((hf == 0) & (kk == 0))
        def _():
            acc[...] = jnp.zeros_like(acc)

        cols = pl.ds(pl.multiple_of((hf * nk + kk) * tk, LANES), tk)
        acc[...] += _dot_tb(dp_ref[...], w_ref[:, cols])

        @pl.when((hf == nh - 1) & (kk == nk - 1))
        def _():
            dx, dgr = _rms_bwd(acc[...], x_ref[...], g_ref[...], dres_ref[...])
            dx_ref[...] = dx
            dg_ref[...] += jnp.sum(dgr, axis=0, keepdims=True)

    (dx, dg), got = _call(
        body, name=name, grid=(T // tm, nh, nk),
        in_specs=[pl.BlockSpec((None, tm, tk), lambda i, hf, kk: (hf, i, kk)),
                  pl.BlockSpec((D, nh * H), lambda i, hf, kk: (0, 0), pipeline_mode=pl.Buffered(1)),
                  pl.BlockSpec((tm, D), lambda i, hf, kk: (i, 0)), pl.BlockSpec((1, D), lambda i, hf, kk: (0, 0)),
                  pl.BlockSpec((tm, D), lambda i, hf, kk: (i, 0))],
        out_specs=[pl.BlockSpec((tm, D), lambda i, hf, kk: (i, 0)), pl.BlockSpec((1, D), lambda i, hf, kk: (0, 0))],
        out_shape=[jax.ShapeDtypeStruct((T, D), F32), jax.ShapeDtypeStruct((1, D), F32)],
        scratch_shapes=[pltpu.VMEM((tm, D), F32)],
        semantics=("arbitrary", "arbitrary", "arbitrary"), args=(dpre, w, x, gain, dres), comm=comm)
    return dx, dg, got


def dw_col(a, dpre, name):
    T = a.shape[0]
    nh, _, H = dpre.shape
    per = nh * H // N_CHIPS
    bph = N_CHIPS // nh
    tt = _tile(T, 1024)
    nt = T // tt

    def body(a_ref, b_ref, o_ref, acc):
        t = pl.program_id(1)

        @pl.when(t == 0)
        def _():
            acc[...] = jnp.zeros_like(acc)

        acc[...] += _dot_ta(a_ref[...], b_ref[...])

        @pl.when(t == nt - 1)
        def _():
            o_ref[...] = acc[...].astype(BF16)

    return pl.pallas_call(
        body, name=name, grid=(N_CHIPS, nt),
        in_specs=[pl.BlockSpec((tt, D), lambda q, t: (t, 0)),
                  pl.BlockSpec((None, tt, per), lambda q, t: (q // bph, t, q % bph))],
        out_specs=pl.BlockSpec((None, D, per), lambda q, t: (q, 0, 0)),
        out_shape=jax.ShapeDtypeStruct((N_CHIPS, D, per), BF16),
        scratch_shapes=[pltpu.VMEM((D, per), F32)],
        compiler_params=_params("parallel", "arbitrary"),
    )(a, dpre)


def dw_row(a, b, name):
    T, R = a.shape
    cw = 1408 if R % 1408 == 0 else 512
    tt = _tile(T, 1024)
    nt = T // tt

    def body(a_ref, b_ref, o_ref, acc):
        t = pl.program_id(1)

        @pl.when(t == 0)
        def _():
            acc[...] = jnp.zeros_like(acc)

        acc[...] += _dot_ta(a_ref[...], b_ref[...].astype(BF16))

        @pl.when(t == nt - 1)
        def _():
            o_ref[...] = acc[...].astype(BF16)

    out = pl.pallas_call(
        body, name=name, grid=(R // cw, nt),
        in_specs=[pl.BlockSpec((tt, cw), lambda q, t: (t, q)), pl.BlockSpec((tt, D), lambda q, t: (t, 0))],
        out_specs=pl.BlockSpec((cw, D), lambda q, t: (q, 0)),
        out_shape=jax.ShapeDtypeStruct((R, D), BF16),
        scratch_shapes=[pltpu.VMEM((cw, D), F32)],
        compiler_params=_params("parallel", "arbitrary"),
    )(a, b)
    return out.reshape(N_CHIPS, R // N_CHIPS, D)


def ln_silu_bwd(dx, w_pw2, dwc, ln_g, ln_b):
    T = dx.shape[0]
    tm = _tile(T, 512)

    def body(dx_ref, w_ref, y_ref, g_ref, b_ref, dy_ref, dg_ref, db_ref, dbo_ref):
        @pl.when(pl.program_id(0) == 0)
        def _():
            dg_ref[...] = jnp.zeros_like(dg_ref)
            db_ref[...] = jnp.zeros_like(db_ref)
            dbo_ref[...] = jnp.zeros_like(dbo_ref)

        dxv = dx_ref[...]
        dsw = _dot_tb(dxv.astype(BF16), w_ref[...])
        y = y_ref[...]
        mu = jnp.mean(y, axis=-1, keepdims=True)
        yc = y - mu
        rstd = lax.rsqrt(jnp.mean(yc * yc, axis=-1, keepdims=True) + EPS)
        xhat = yc * rstd
        z = xhat * g_ref[...] + b_ref[...]
        sg = _sigmoid(z)
        dz = dsw * sg * (1.0 + z * (1.0 - sg))
        dxh = dz * g_ref[...]
        dy_ref[...] = rstd * (dxh - jnp.mean(dxh, axis=-1, keepdims=True)
                              - xhat * jnp.mean(dxh * xhat, axis=-1, keepdims=True))
        dg_ref[...] += jnp.sum(dz * xhat, axis=0, keepdims=True)
        db_ref[...] += jnp.sum(dz, axis=0, keepdims=True)
        dbo_ref[...] += jnp.sum(dxv, axis=0, keepdims=True)

    row = lambda i: (i, 0)
    full = lambda i: (0, 0)
    vec = pl.BlockSpec((1, D), full)
    return pl.pallas_call(
        body, name="ln_silu_bwd", grid=(T // tm,),
        in_specs=[pl.BlockSpec((tm, D), row), pl.BlockSpec((D, D), full), pl.BlockSpec((tm, D), row), vec, vec],
        out_specs=[pl.BlockSpec((tm, D), row), vec, vec, vec],
        out_shape=[jax.ShapeDtypeStruct((T, D), F32)] + [jax.ShapeDtypeStruct((1, D), F32)] * 3,
        compiler_params=_params("arbitrary"),
    )(dx, w_pw2, dwc, ln_g, ln_b)


def conv_bwd(ddwc, glu, pre, w_dw, comm=None):
    T = ddwc.shape[0]
    tt, L, nlt = _conv_tiles(T)
    n_i = T // tt
    main, prev, nxt = _conv_specs(T, tt)

    def body(d_ref, dp_ref, dn_ref, x_ref, xp_ref, xn_ref, pre_ref, w_ref,
             dpre_ref, dw_ref, dbd_ref, dbp_ref, padd, padx, ob, extd, extx):
        i = pl.program_id(0)

        @pl.when(i == 0)
        def _():
            dw_ref[...] = jnp.zeros_like(dw_ref)
            dbd_ref[...] = jnp.zeros_like(dbd_ref)
            dbp_ref[...] = jnp.zeros_like(dbp_ref)

        _fill_pad(padd, d_ref, dp_ref, dn_ref, i, n_i, tt, nlt)
        _fill_pad(padx, x_ref, xp_ref, xn_ref, i, n_i, tt, nlt)
        for lt in range(nlt):
            sl = slice(lt * LANES, (lt + 1) * LANES)
            o = ob.at[lt]
            _fill_strided(extd, padd.at[lt], L)
            _fill_strided(extx, padx.at[lt], L)
            wk = [jnp.broadcast_to(w_ref[k:k + 1, sl], (SUBLANES, LANES)) for k in range(CONV_W)]

            def jbody(jb, accs):
                j = jb * 2
                accs = list(accs)
                d0, d1 = extd[j + CONV_PAD], extd[j + 1 + CONV_PAD]
                g0 = g1 = None
                for m in range(CONV_W + 1):
                    ed = extd[j + 2 * CONV_PAD + 1 - m]
                    ex = extx[j + m]
                    if m < CONV_W:
                        t = ed * wk[m]
                        g1 = t if g1 is None else g1 + t
                        accs[m] = accs[m] + d0 * ex
                    if m >= 1:
                        t = ed * wk[m - 1]
                        g0 = t if g0 is None else g0 + t
                        accs[m - 1] = accs[m - 1] + d1 * ex
                o[pl.ds(j, SUBLANES, stride=L), :] = g0
                o[pl.ds(j + 1, SUBLANES, stride=L), :] = g1
                return tuple(accs)

            accs = lax.fori_loop(0, L // 2, jbody, tuple(jnp.zeros((SUBLANES, LANES), F32) for _ in range(CONV_W)))
            for k in range(CONV_W):
                dw_ref[k:k + 1, sl] += jnp.sum(accs[k], axis=0, keepdims=True)
        dglu = jnp.concatenate([ob[lt] for lt in range(nlt)], axis=1)
        a = pre_ref[0].astype(F32)
        gate = pre_ref[1].astype(F32)
        sg = _sigmoid(gate)
        da = dglu * sg
        dgate = dglu * a * sg * (1.0 - sg)
        dpre_ref[0] = da.astype(BF16)
        dpre_ref[1] = dgate.astype(BF16)
        dbd_ref[...] += jnp.sum(d_ref[...], axis=0, keepdims=True)
        dbp_ref[0] += jnp.sum(da, axis=0, keepdims=True)
        dbp_ref[1] += jnp.sum(dgate, axis=0, keepdims=True)

    full = lambda i: (0, 0)
    (dpre, dw, dbd, dbp), got = _call(
        body, name="conv_bwd", grid=(n_i,),
        in_specs=[main, prev, nxt, main, prev, nxt, pl.BlockSpec((2, tt, D), lambda i: (0, i, 0)),
                  pl.BlockSpec((32, D), full)],
        out_specs=[pl.BlockSpec((2, tt, D), lambda i: (0, i, 0)), pl.BlockSpec((32, D), full),
                   pl.BlockSpec((1, D), full), pl.BlockSpec((2, 1, D), lambda i: (0, 0, 0))],
        out_shape=[jax.ShapeDtypeStruct((2, T, D), BF16), jax.ShapeDtypeStruct((32, D), F32),
                   jax.ShapeDtypeStruct((1, D), F32), jax.ShapeDtypeStruct((2, 1, D), F32)],
        scratch_shapes=[pltpu.VMEM((nlt, tt + 2 * HALO, LANES), F32), pltpu.VMEM((nlt, tt + 2 * HALO, LANES), F32),
                        pltpu.VMEM((nlt, tt, LANES), F32), pltpu.VMEM((L + 2 * HALO, SUBLANES, LANES), F32),
                        pltpu.VMEM((L + 2 * HALO, SUBLANES, LANES), F32)],
        semantics=("arbitrary",), args=(ddwc, ddwc, ddwc, glu, glu, glu, pre, w_dw), comm=comm)
    return dpre, dw, dbd, dbp, got


def mm_bt(a, w, name):
    T = a.shape[0]
    N = w.shape[0]
    tm = _tile(T, 512)

    def body(a_ref, w_ref, o_ref):
        o_ref[...] = _dot_tb(a_ref[...].astype(BF16), w_ref[...]).astype(BF16)

    return pl.pallas_call(
        body, name=name, grid=(T // tm,),
        in_specs=[pl.BlockSpec((tm, D), lambda i: (i, 0)), pl.BlockSpec((N, D), lambda i: (0, 0))],
        out_specs=pl.BlockSpec((tm, N), lambda i: (i, 0)),
        out_shape=jax.ShapeDtypeStruct((T, N), BF16),
        compiler_params=_params("parallel"),
    )(a, w)


def attn_bwd(qkv, o, do, sink, rc, rs1, rs2, comm=None):
    T = qkv.shape[0]
    nb, q_spec, prev, own, nxt = _attn_specs(T)
    scale = 1.0 / math.sqrt(HD)
    kvw = N_KV * HD

    def body(sink_ref, q_ref, kp_ref, ko_ref, kn_ref, o_ref, do_ref, c_ref, s1_ref, s2_ref,
             dq_ref, dkc_ref, dvc_ref, dsink_ref, dqs):
        n = pl.program_id(0)

        @pl.when(n == 0)
        def _():
            dsink_ref[...] = jnp.zeros_like(dsink_ref)

        valid = _attn_mask(n, T)
        kv = jnp.concatenate([kp_ref[...], ko_ref[...], kn_ref[...]], axis=0)
        lane = lax.broadcasted_iota(jnp.int32, (1, N_HEADS), 1)
        dsink = jnp.zeros((1, N_HEADS), F32)
        ks = [kv[:, g * HD:(g + 1) * HD] for g in range(N_KV)]
        qs = [_stack_heads(q_ref, g) for g in range(N_KV)]
        dos = [_stack_heads(do_ref, g) for g in range(N_KV)]
        ss = [_dot_tb(qs[g], ks[g]) for g in range(N_KV)]
        dps = [_dot_tb(dos[g], kv[:, kvw + g * HD:kvw + (g + 1) * HD]) for g in range(N_KV)]
        pbs, dss = [], []
        for g in range(N_KV):
            s = jnp.where(valid, ss[g] * scale, NEG)
            sk = _stack_sinks(sink_ref, g)
            m = jnp.maximum(jnp.max(s, axis=-1, keepdims=True), sk)
            e = jnp.exp(s - m)
            inv = 1.0 / (jnp.sum(e, axis=-1, keepdims=True) + jnp.exp(sk - m))
            p = e * inv
            delta = jnp.sum(dos[g].astype(F32) * _stack_heads(o_ref, g).astype(F32), axis=-1, keepdims=True)
            dss.append((p * (dps[g] - delta) * scale).astype(BF16))
            pbs.append(p.astype(BF16))
            dsk = jnp.exp(sk - m) * inv * delta
            for u in range(GROUP):
                part = -jnp.sum(dsk[u * BLK:(u + 1) * BLK], axis=0, keepdims=True)
                dsink = dsink + jnp.where(lane == g * GROUP + u, part, 0.0)
        for g in range(N_KV):
            dq = _dot(dss[g], ks[g])
            dkc_ref[:, g * HD:(g + 1) * HD] = _dot_ta(dss[g], qs[g])
            dvc_ref[:, g * HD:(g + 1) * HD] = _dot_ta(pbs[g], dos[g])
            for u in range(GROUP):
                h = g * GROUP + u
                dqs[:, h * HD:(h + 1) * HD] = dq[u * BLK:(u + 1) * BLK]
        dsink_ref[...] += dsink
        c, s1, s2 = c_ref[...], s1_ref[...], s2_ref[...]
        for j in range(KV_OFF // LANES):
            sl = slice(LANES * j, LANES * (j + 1))
            dq_ref[:, sl] = _rope(dqs[:, sl], c, -s1, -s2).astype(BF16)

    row = lambda n: (n, 0)
    (dq, dkc, dvc, dsink), got = _call(
        body, name="attn_bwd", grid=(nb,),
        in_specs=[pl.BlockSpec(memory_space=pltpu.SMEM), q_spec, prev, own, nxt,
                  pl.BlockSpec((BLK, D), row), pl.BlockSpec((BLK, D), row), *_tab_specs(BLK)],
        out_specs=[pl.BlockSpec((BLK, D), row), pl.BlockSpec((None, 3 * BLK, kvw), lambda n: (n, 0, 0)),
                   pl.BlockSpec((None, 3 * BLK, kvw), lambda n: (n, 0, 0)), pl.BlockSpec((1, N_HEADS), lambda n: (0, 0))],
        out_shape=[jax.ShapeDtypeStruct((T, D), BF16), jax.ShapeDtypeStruct((nb, 3 * BLK, kvw), F32),
                   jax.ShapeDtypeStruct((nb, 3 * BLK, kvw), F32), jax.ShapeDtypeStruct((1, N_HEADS), F32)],
        scratch_shapes=[pltpu.VMEM((BLK, D), F32)],
        semantics=("arbitrary",), args=(sink, qkv, qkv, qkv, qkv, o, do, rc, rs1, rs2), comm=comm)
    return dq, dkc, dvc, dsink, got


def kv_sum(dkc, dvc, rc, rs1, rs2):
    nb = dkc.shape[0]
    T = nb * BLK
    kvw = N_KV * HD

    def body(kp_ref, ko_ref, kn_ref, vp_ref, vo_ref, vn_ref, c_ref, s1_ref, s2_ref, out_ref):
        m = pl.program_id(0)
        has_p = (m > 0).astype(F32)
        has_n = (m < nb - 1).astype(F32)
        dk = kp_ref[...] * has_p + ko_ref[...] + kn_ref[...] * has_n
        dv = vp_ref[...] * has_p + vo_ref[...] + vn_ref[...] * has_n
        c, s1, s2 = c_ref[...], s1_ref[...], s2_ref[...]
        for j in range(kvw // LANES):
            sl = slice(LANES * j, LANES * (j + 1))
            out_ref[:, sl] = _rope(dk[:, sl], c, -s1, -s2).astype(BF16)
        out_ref[:, kvw:] = dv.astype(BF16)

    from_prev = pl.BlockSpec((None, BLK, kvw), lambda m: (jnp.maximum(m - 1, 0), 2, 0))
    from_own = pl.BlockSpec((None, BLK, kvw), lambda m: (m, 1, 0))
    from_next = pl.BlockSpec((None, BLK, kvw), lambda m: (jnp.minimum(m + 1, nb - 1), 0, 0))
    return pl.pallas_call(
        body, name="kv_sum", grid=(nb,),
        in_specs=[from_prev, from_own, from_next, from_prev, from_own, from_next, *_tab_specs(BLK)],
        out_specs=pl.BlockSpec((BLK, 2 * kvw), lambda m: (m, 0)),
        out_shape=jax.ShapeDtypeStruct((T, 2 * kvw), BF16),
        compiler_params=_params("parallel"),
    )(dkc, dkc, dkc, dvc, dvc, dvc, rc, rs1, rs2)


def _me():
    return lax.axis_index("x"), lax.axis_index("y"), lax.axis_index("c")


def _half_rows(ref, sharded_rows, chip, core):
    R, C = ref.shape[-2], ref.shape[-1]
    lead = (slice(None),) * (len(ref.shape) - 2)
    if sharded_rows:
        per = R // N_CHIPS
        return ref.at[lead + (pl.ds(chip * per + core * (per // 2), per // 2), slice(None))]
    per = C // N_CHIPS
    return ref.at[lead + (pl.ds(core * (R // 2), R // 2), pl.ds(chip * per, per))]


class _Gather:
    def __init__(self, shards, sharded_rows):
        self.inputs = list(shards)
        self.rows = list(sharded_rows)
        self.n = self.n_in = self.n_out = len(shards)
        self.out_shapes = []
        for s, rows in zip(shards, sharded_rows):
            shp = list(s.shape)
            shp[-2 if rows else -1] *= N_CHIPS
            self.out_shapes.append(jax.ShapeDtypeStruct(tuple(shp), s.dtype))
        self.scratch = [pltpu.SemaphoreType.DMA((self.n, 6)), pltpu.SemaphoreType.DMA((self.n, 6)),
                        pltpu.SemaphoreType.DMA((self.n, 2))]

    def _ctx(self, ins, outs, sems):
        send_sems, recv_sems, local_sems = sems
        x, y, c = _me()
        chips = [(1 - x, y), (x, 1 - y), (1 - x, 1 - y)]

        def half_src(w, core):
            s = ins[w]
            R = s.shape[-2]
            return s.at[pl.ds(core * (R // 2), R // 2), :]

        def dst(w, chip, core):
            return _half_rows(outs[w], self.rows[w], chip, core)

        def copy(w, k, src, chip, core, to):
            return pltpu.make_async_remote_copy(
                src_ref=src, dst_ref=dst(w, chip, core), send_sem=send_sems.at[w, k], recv_sem=recv_sems.at[w, k],
                device_id=to, device_id_type=MESH)

        def local(w, core):
            return pltpu.make_async_copy(half_src(w, core), dst(w, 2 * x + y, core), local_sems.at[w, core])

        def first(w, j):
            qx, qy = chips[j]
            return copy(w, j, half_src(w, c), 2 * x + y, c, (qx, qy, c))

        def landed(w, j):
            qx, qy = chips[j]
            return copy(w, j, dst(w, 2 * qx + qy, c), 2 * qx + qy, c, (x, y, c))

        def passed(w, j):
            qx, qy = chips[j]
            return copy(w, 3 + j, dst(w, 2 * qx + qy, c), 2 * qx + qy, c, (x, y, 1 - c))

        def from_sibling(w, j):
            qx, qy = chips[j]
            return copy(w, 3 + j, dst(w, 2 * qx + qy, 1 - c), 2 * qx + qy, 1 - c, (x, y, c))

        return local, first, landed, passed, from_sibling

    def start(self, ins, outs, sems):
        local, first, _, _, _ = self._ctx(ins, outs, sems)
        for w in range(self.n):
            for core in range(2):
                local(w, core).start()
            for j in range(3):
                first(w, j).start()

    def mid(self, ins, outs, sems):
        _, _, landed, passed, _ = self._ctx(ins, outs, sems)
        for w in range(self.n):
            for j in range(3):
                landed(w, j).wait_recv()
                passed(w, j).start()

    def end(self, ins, outs, sems):
        local, first, _, passed, from_sibling = self._ctx(ins, outs, sems)
        for w in range(self.n):
            for j in range(3):
                from_sibling(w, j).wait_recv()
        for w in range(self.n):
            for j in range(3):
                first(w, j).wait_send()
                passed(w, j).wait_send()
            for core in range(2):
                local(w, core).wait()


class _Scatter:
    def __init__(self, grads, small=None):
        self.inputs = list(grads) + ([small] if small is not None else [])
        self.ng = len(grads)
        self.n = self.n_in = self.n_out = len(self.inputs)
        self.out_shapes = [jax.ShapeDtypeStruct((N_DEV, g.shape[1] // 2, g.shape[2]), g.dtype) for g in grads]
        if small is not None:
            self.out_shapes.append(jax.ShapeDtypeStruct((N_DEV,) + small.shape, small.dtype))
        self.scratch = [pltpu.SemaphoreType.DMA((self.n, N_DEV)), pltpu.SemaphoreType.DMA((self.n, N_DEV)),
                        pltpu.SemaphoreType.DMA((self.n,))]

    def _ctx(self, ins, outs, sems):
        send_sems, recv_sems, local_sems = sems
        x, y, c = _me()
        me = 4 * x + 2 * y + c

        def piece(w, chip, core):
            if w >= self.ng:
                return ins[w]
            half = ins[w].shape[1] // 2
            return ins[w].at[chip, pl.ds(core * half, half), :]

        def peer_of(k):
            return x ^ ((k >> 2) & 1), y ^ ((k >> 1) & 1), c ^ (k & 1)

        def local(w):
            return pltpu.make_async_copy(piece(w, 2 * x + y, c), outs[w].at[me], local_sems.at[w])

        def send(w, k):
            px, py, pc = peer_of(k)
            return pltpu.make_async_remote_copy(
                src_ref=piece(w, 2 * px + py, pc), dst_ref=outs[w].at[me], send_sem=send_sems.at[w, k],
                recv_sem=recv_sems.at[w, k], device_id=(px, py, pc), device_id_type=MESH)

        def recv(w, k):
            px, py, pc = peer_of(k)
            return pltpu.make_async_remote_copy(
                src_ref=piece(w, 2 * x + y, c), dst_ref=outs[w].at[4 * px + 2 * py + pc], send_sem=send_sems.at[w, k],
                recv_sem=recv_sems.at[w, k], device_id=(px, py, pc), device_id_type=MESH)

        return local, send, recv

    def start(self, ins, outs, sems):
        local, send, _ = self._ctx(ins, outs, sems)
        for w in range(self.n):
            local(w).start()
            for k in range(1, N_DEV):
                send(w, k).start()

    def mid(self, ins, outs, sems):
        pass

    def end(self, ins, outs, sems):
        local, send, recv = self._ctx(ins, outs, sems)
        for w in range(self.n):
            for k in range(1, N_DEV):
                recv(w, k).wait_recv()
        for w in range(self.n):
            for k in range(1, N_DEV):
                send(w, k).wait_send()
            local(w).wait()


def exchange(plan, name):
    def body(*refs):
        ins, outs, sems = refs[:plan.n_in], refs[plan.n_in:plan.n_in + plan.n_out], refs[plan.n_in + plan.n_out:]
        plan.start(ins, outs, sems)
        plan.mid(ins, outs, sems)
        plan.end(ins, outs, sems)

    any_spec = pl.BlockSpec(memory_space=pl.ANY)
    return pl.pallas_call(
        body, name=name, in_specs=[any_spec] * plan.n_in, out_specs=[any_spec] * plan.n_out,
        out_shape=plan.out_shapes, scratch_shapes=plan.scratch,
    )(*plan.inputs)


def _call(body, *, name, grid, in_specs, out_specs, out_shape, scratch_shapes=(), semantics, args, comm=None):
    if comm is None:
        outs = pl.pallas_call(
            body, name=name, grid=grid, in_specs=in_specs, out_specs=out_specs, out_shape=out_shape,
            scratch_shapes=list(scratch_shapes), compiler_params=_params(*semantics))(*args)
        return outs, []
    n_in, n_out, n_scr = len(in_specs), len(out_specs), len(scratch_shapes)

    total = math.prod(grid)
    first, middle, last = 0, total // 2 - 1, total - 1
    assert first <= middle < last

    def at(step):
        lin = pl.program_id(0)
        for d in range(1, len(grid)):
            lin = lin * grid[d] + pl.program_id(d)
        return lin == step

    def hosted(*refs):
        h_in, c_in = refs[:n_in], refs[n_in:n_in + comm.n_in]
        rest = refs[n_in + comm.n_in:]
        h_out, c_out = rest[:n_out], rest[n_out:n_out + comm.n_out]
        rest = rest[n_out + comm.n_out:]
        h_scr, c_scr = rest[:n_scr], rest[n_scr:]

        @pl.when(at(first))
        def _():
            comm.start(c_in, c_out, c_scr)

        body(*h_in, *h_out, *h_scr)

        @pl.when(at(middle))
        def _():
            comm.mid(c_in, c_out, c_scr)

        @pl.when(at(last))
        def _():
            comm.end(c_in, c_out, c_scr)

    any_spec = pl.BlockSpec(memory_space=pl.ANY)
    outs = pl.pallas_call(
        hosted, name=name, grid=grid, in_specs=list(in_specs) + [any_spec] * comm.n_in,
        out_specs=list(out_specs) + [any_spec] * comm.n_out, out_shape=list(out_shape) + comm.out_shapes,
        scratch_shapes=list(scratch_shapes) + comm.scratch,
        compiler_params=_params(*(["arbitrary"] * len(grid))))(*args, *comm.inputs)
    return outs[:n_out], outs[n_out:]


def sum_swap(pieces, name):
    nl = len(pieces)
    _, r2, cc = pieces[0].shape
    tr = 128 if r2 % 128 == 0 else r2 // 2
    n = r2 // tr

    def body(*refs):
        p_refs, out = refs[:nl], refs[nl]
        slots, send_sems, local_sems, recv_sem = refs[nl + 1:]
        x, y, c = _me()
        sibling = (x, y, 1 - c)
        l, i = pl.program_id(0), pl.program_id(1)
        step = l * n + i

        def rows(st, core):
            return out.at[st // n, pl.ds(core * r2 + (st % n) * tr, tr), :]

        def copies(st):
            slot = st % 2
            local = pltpu.make_async_copy(slots.at[slot], rows(st, c), local_sems.at[slot])
            remote = pltpu.make_async_remote_copy(
                src_ref=slots.at[slot], dst_ref=rows(st, c), send_sem=send_sems.at[slot], recv_sem=recv_sem,
                device_id=sibling, device_id_type=MESH)
            return local, remote

        for ll in range(nl):
            @pl.when(l == ll)
            def _():
                acc = p_refs[ll][0].astype(F32)
                for d in range(1, N_DEV):
                    acc = acc + p_refs[ll][d].astype(F32)
                slots[step % 2] = acc

        for cp in copies(step):
            cp.start()

        @pl.when(step >= 1)
        def _():
            local, remote = copies(step - 1)
            local.wait()
            remote.wait_send()

        @pl.when(step == nl * n - 1)
        def _():
            local, remote = copies(step)
            local.wait()
            remote.wait_send()
            theirs = out.at[:, pl.ds((1 - c) * r2, r2), :]
            pltpu.make_async_remote_copy(src_ref=theirs, dst_ref=theirs, send_sem=send_sems.at[0],
                                         recv_sem=recv_sem, device_id=sibling, device_id_type=MESH).wait_recv()

    def piece_spec(ll):
        def index(l, i):
            return (0, jnp.where(l == ll, i, jnp.where(l < ll, 0, n - 1)), 0)
        return pl.BlockSpec((N_DEV, tr, cc), index)

    return pl.pallas_call(
        body, name=name, grid=(nl, n),
        in_specs=[piece_spec(ll) for ll in range(nl)],
        out_specs=pl.BlockSpec(memory_space=pl.ANY),
        out_shape=jax.ShapeDtypeStruct((nl, 2 * r2, cc), F32),
        scratch_shapes=[pltpu.VMEM((2, tr, cc), F32), pltpu.SemaphoreType.DMA((2,)), pltpu.SemaphoreType.DMA((2,)),
                        pltpu.SemaphoreType.DMA(())],
        compiler_params=_params("arbitrary", "arbitrary"),
    )(*pieces)


def sum_pieces(pieces, name):
    _, R, C = pieces.shape
    tr = _tile(R, 128) if R % 128 == 0 else R

    def body(p_ref, o_ref):
        acc = p_ref[0].astype(F32)
        for d in range(1, N_DEV):
            acc = acc + p_ref[d].astype(F32)
        o_ref[...] = acc

    return pl.pallas_call(
        body, name=name, grid=(R // tr,),
        in_specs=[pl.BlockSpec((N_DEV, tr, C), lambda i: (0, i, 0))],
        out_specs=pl.BlockSpec((tr, C), lambda i: (i, 0)),
        out_shape=jax.ShapeDtypeStruct((R, C), F32),
        compiler_params=_params("parallel"),
    )(pieces)


def adamw(w, g, m, v, name):
    Lyr, R, C = w.shape
    tr = _tile(R, 256) if R % 8 == 0 else R
    c1 = 1.0 / (1.0 - ADAM_B1 ** ADAM_STEP)
    c2 = 1.0 / (1.0 - ADAM_B2 ** ADAM_STEP)

    def body(w_ref, g_ref, m_ref, v_ref, d_ref, nm_ref, nv_ref):
        gv = g_ref[...]
        nm = ADAM_B1 * m_ref[...] + (1.0 - ADAM_B1) * gv
        nv = ADAM_B2 * v_ref[...] + (1.0 - ADAM_B2) * (gv * gv)
        nm_ref[...] = nm
        nv_ref[...] = nv
        d_ref[...] = -ADAM_LR * ((nm * c1) / (jnp.sqrt(nv * c2) + ADAM_EPS) + ADAM_WD * w_ref[...])

    spec = pl.BlockSpec((None, tr, C), lambda l, i: (l, i, 0))
    shp = jax.ShapeDtypeStruct(w.shape, F32)
    return pl.pallas_call(
        body, name=name, grid=(Lyr, R // tr),
        in_specs=[spec] * 4, out_specs=[spec] * 3, out_shape=[shp] * 3,
        compiler_params=_params("parallel", "parallel"),
    )(w, g, m, v)


def _rope_tables(T):
    pos = jnp.arange(T, dtype=F32)
    inv_freq = THETA ** (-jnp.arange(0, ROT, 2, dtype=F32) / ROT)
    ang = pos[:, None] * inv_freq[None, :]
    cos, sin = jnp.cos(ang), jnp.sin(ang)
    half = ROT // 2
    one = jnp.ones((T, HD - ROT), F32)
    zero = jnp.zeros((T, HD - ROT), F32)
    zh = jnp.zeros((T, half), F32)
    c = [cos, cos, one]
    s1 = [-sin, zh, zero]
    s2 = [zh, sin, zero]
    return jnp.concatenate(c + c + s1 + s1 + s2 + s2, axis=1)


def _tab_specs(rows):
    return [pl.BlockSpec((rows, LANES), lambda i, k=k: (i, k)) for k in range(3)]


def kernel(x, attn_norm, attn_w_qkv, attn_w_o, attn_sink, conv_norm, conv_w_pw1, conv_b_pw1, conv_w_dw, conv_b_dw, conv_ln_g, conv_ln_b, conv_w_pw2, conv_b_pw2, ffn_norm, ffn_w_gu, ffn_w_down, final_norm, loss_target, m_attn_norm, m_attn_w_qkv, m_attn_w_o, m_attn_sink, m_conv_norm, m_conv_w_pw1, m_conv_b_pw1, m_conv_w_dw, m_conv_b_dw, m_conv_ln_g, m_conv_ln_b, m_conv_w_pw2, m_conv_b_pw2, m_ffn_norm, m_ffn_w_gu, m_ffn_w_down, m_final_norm, v_attn_norm, v_attn_w_qkv, v_attn_w_o, v_attn_sink, v_conv_norm, v_conv_w_pw1, v_conv_b_pw1, v_conv_w_dw, v_conv_b_dw, v_conv_ln_g, v_conv_ln_b, v_conv_w_pw2, v_conv_b_pw2, v_ffn_norm, v_ffn_w_gu, v_ffn_w_down, v_final_norm):
    T = x.shape[1]
    x0 = x[0]
    target = loss_target[0]
    ix, iy = lax.axis_index("x"), lax.axis_index("y")
    chip = 2 * ix + iy
    rc = rs1 = rs2 = _rope_tables(T)

    bf = lambda t: t.astype(BF16)
    col_row = [False, True]
    w_qkv, w_o = exchange(_Gather([bf(attn_w_qkv[0]), bf(attn_w_o[0])], col_row), "gather_attn")

    def place(vec, width):
        return lax.dynamic_update_slice(jnp.zeros((vec.shape[0], N_CHIPS * width), F32), vec, (0, chip * width))

    small_rows = jnp.concatenate([
        place(conv_norm, 256), place(conv_b_pw1, 512).reshape(2, D), place(conv_b_dw, 256), place(conv_ln_g, 256),
        place(conv_ln_b, 256), place(conv_b_pw2, 256), jnp.zeros((1, D), F32),
        place(conv_w_dw[0], 256), jnp.zeros((1, D), F32)], axis=0)
    got = exchange(_Scatter([], small_rows), "gather_small_params")[0]
    psmall = sum_pieces(got, "sum_small_params") * 0.5
    p_conv_norm, p_b_pw1 = psmall[0:1], psmall[1:3].reshape(1, 2 * D)
    p_b_dw, p_ln_g, p_ln_b, p_b_pw2 = psmall[3:4], psmall[4:5], psmall[5:6], psmall[6:7]
    p_w_dw = psmall[8:40]

    h0, qkv = rms_qkv(x0, attn_norm, w_qkv, rc, rs1, rs2)
    sink = attn_sink[0]
    o, (w_gu0, w_down0) = attn_fwd(qkv, sink, comm=_Gather([bf(ffn_w_gu[0]), bf(ffn_w_down[0])], col_row))
    zero_b = jnp.zeros((1, D), F32)
    x1 = mm_res(o, w_o, x0, zero_b, "attn_out")
    zero_gu = jnp.zeros((1, 2 * DFF), F32)
    h1, gu0, act0, (w_pw1, w_pw2) = rms_mm_gate(
        x1, ffn_norm[0:1], w_gu0, zero_gu, DFF, True, BF16, "ffn0_up",
        comm=_Gather([bf(conv_w_pw1[0]), bf(conv_w_pw2[0])], col_row))
    x2 = mm_res(act0, w_down0, x1, zero_b, "ffn0_down")
    h2, pre, glu, _ = rms_mm_gate(x2, p_conv_norm, w_pw1, p_b_pw1, D, False, F32, "conv_pw1")
    dwc, sw, (w_gu1, w_down1) = conv_fwd(glu, p_w_dw, p_b_dw, p_ln_g, p_ln_b,
                                         comm=_Gather([bf(ffn_w_gu[1]), bf(ffn_w_down[1])], col_row))
    x3 = mm_res(sw, w_pw2, x2, p_b_pw2, "conv_pw2")
    h3, gu1, act1, _ = rms_mm_gate(x3, ffn_norm[1:2], w_gu1, zero_gu, DFF, True, BF16, "ffn1_up")
    x4 = mm_res(act1, w_down1, x3, zero_b, "ffn1_down")
    dx4, loss_part, d_final = final_loss(x4, final_norm.reshape(1, D), target)
    loss = lax.psum(loss_part[0, 0], ("x", "y", "c"))

    dgu1, _ = swiglu_bwd(dx4, w_down1, gu1, "ffn1_down_bwd")
    g_down1 = dw_row(act1, dx4, "ffn1_down_dw")
    dx3, d_ffn1, _ = mm_bt_rmsbwd(dgu1, w_gu1, x3, ffn_norm[1:2], dx4, "ffn1_up_bwd")
    g_gu1 = dw_col(h3, dgu1, "ffn1_up_dw")

    ddwc, d_ln_g, d_ln_b, d_b_pw2 = ln_silu_bwd(dx3, w_pw2, dwc, p_ln_g, p_ln_b)
    g_pw2 = dw_row(sw, dx3, "conv_pw2_dw")
    dpre, d_w_dw, d_b_dw, d_b_pw1, (r_gu1, r_down1) = conv_bwd(ddwc, glu, pre, p_w_dw,
                                                               comm=_Scatter([g_gu1, g_down1]))
    dx2, d_conv_norm, _ = mm_bt_rmsbwd(dpre, w_pw1, x2, p_conv_norm, dx3, "conv_pw1_bwd")
    g_pw1 = dw_col(h2, dpre, "conv_pw1_dw")

    dgu0, (r_pw1, r_pw2) = swiglu_bwd(dx2, w_down0, gu0, "ffn0_down_bwd", comm=_Scatter([g_pw1, g_pw2]))
    g_down0 = dw_row(act0, dx2, "ffn0_down_dw")
    dx1, d_ffn0, _ = mm_bt_rmsbwd(dgu0, w_gu0, x1, ffn_norm[0:1], dx2, "ffn0_up_bwd")
    g_gu0 = dw_col(h1, dgu0, "ffn0_up_dw")

    do = mm_bt(dx1, w_o, "attn_out_bwd")
    g_o = dw_row(o, dx1, "attn_out_dw")
    dq, dkc, dvc, d_sink, (r_gu0, r_down0, r_o) = attn_bwd(qkv, o, do, sink, rc, rs1, rs2,
                                                           comm=_Scatter([g_gu0, g_down0, g_o]))
    dkv = kv_sum(dkc, dvc, rc, rs1, rs2)
    dqkv = jnp.concatenate([dq, dkv], axis=1)[None]
    g_qkv = dw_col(h0, dqkv, "attn_qkv_dw")
    dx0, d_attn_norm, (r_qkv,) = mm_bt_rmsbwd(dqkv, w_qkv, x0, attn_norm, dx1, "attn_qkv_bwd",
                                              comm=_Scatter([g_qkv]))

    pad16 = lambda t: jnp.concatenate([t, jnp.zeros((1, D - t.shape[1]), F32)], axis=1)
    small_g = jnp.concatenate([
        d_attn_norm, pad16(d_sink), d_conv_norm, d_b_pw1.reshape(2, D), d_b_dw, d_ln_g, d_ln_b, d_b_pw2,
        d_ffn0, d_ffn1, d_final, jnp.zeros((4, D), F32), d_w_dw], axis=0)
    r_small, = exchange(_Scatter([], small_g), "scatter_small")
    gf_gu = sum_swap([r_gu0, r_gu1], "sum_gu")
    gf_down = sum_swap([r_down0, r_down1], "sum_down")
    gf_pw1, gf_pw2 = sum_swap([r_pw1], "sum_pw1"), sum_swap([r_pw2], "sum_pw2")
    gf_qkv, gf_o = sum_swap([r_qkv], "sum_qkv"), sum_swap([r_o], "sum_o")
    gs = sum_pieces(r_small, "sum_small_grads")

    def take(row0, nrows, width):
        return lax.dynamic_slice(gs, (row0, chip * width), (nrows, width))

    grads = {
        "attn_norm": gs[0:1], "attn_w_qkv": gf_qkv, "attn_w_o": gf_o, "attn_sink": gs[1:2, :N_HEADS],
        "conv_norm": take(2, 1, 256), "conv_w_pw1": gf_pw1,
        "conv_b_pw1": lax.dynamic_slice(gs[3:5].reshape(1, 2 * D), (0, chip * 512), (1, 512)),
        "conv_w_dw": take(16, 32, 256)[None, :CONV_W], "conv_b_dw": take(5, 1, 256), "conv_ln_g": take(6, 1, 256),
        "conv_ln_b": take(7, 1, 256), "conv_w_pw2": gf_pw2, "conv_b_pw2": take(8, 1, 256),
        "ffn_norm": gs[9:11], "ffn_w_gu": gf_gu, "ffn_w_down": gf_down, "final_norm": gs[11],
    }
    weights = dict(attn_norm=attn_norm, attn_w_qkv=attn_w_qkv, attn_w_o=attn_w_o, attn_sink=attn_sink,
                   conv_norm=conv_norm, conv_w_pw1=conv_w_pw1, conv_b_pw1=conv_b_pw1, conv_w_dw=conv_w_dw,
                   conv_b_dw=conv_b_dw, conv_ln_g=conv_ln_g, conv_ln_b=conv_ln_b, conv_w_pw2=conv_w_pw2,
                   conv_b_pw2=conv_b_pw2, ffn_norm=ffn_norm, ffn_w_gu=ffn_w_gu, ffn_w_down=ffn_w_down,
                   final_norm=final_norm)
    m_in = dict(attn_norm=m_attn_norm, attn_w_qkv=m_attn_w_qkv, attn_w_o=m_attn_w_o, attn_sink=m_attn_sink,
                conv_norm=m_conv_norm, conv_w_pw1=m_conv_w_pw1, conv_b_pw1=m_conv_b_pw1, conv_w_dw=m_conv_w_dw,
                conv_b_dw=m_conv_b_dw, conv_ln_g=m_conv_ln_g, conv_ln_b=m_conv_ln_b, conv_w_pw2=m_conv_w_pw2,
                conv_b_pw2=m_conv_b_pw2, ffn_norm=m_ffn_norm, ffn_w_gu=m_ffn_w_gu, ffn_w_down=m_ffn_w_down,
                final_norm=m_final_norm)
    v_in = dict(attn_norm=v_attn_norm, attn_w_qkv=v_attn_w_qkv, attn_w_o=v_attn_w_o, attn_sink=v_attn_sink,
                conv_norm=v_conv_norm, conv_w_pw1=v_conv_w_pw1, conv_b_pw1=v_conv_b_pw1, conv_w_dw=v_conv_w_dw,
                conv_b_dw=v_conv_b_dw, conv_ln_g=v_conv_ln_g, conv_ln_b=v_conv_ln_b, conv_w_pw2=v_conv_w_pw2,
                conv_b_pw2=v_conv_b_pw2, ffn_norm=v_ffn_norm, ffn_w_gu=v_ffn_w_gu, ffn_w_down=v_ffn_w_down,
                final_norm=v_final_norm)
    order = list(weights)
    g_out, d_out, m_out, v_out = [], [], [], []
    for nm in order:
        w = weights[nm]
        shape = w.shape
        as3 = lambda t: t.reshape((1,) * (3 - len(shape)) + shape) if len(shape) < 3 else t.reshape(shape)
        g3 = as3(grads[nm].reshape(shape))
        delta, nm_, nv_ = adamw(as3(w), g3, as3(m_in[nm]), as3(v_in[nm]), "adamw_" + nm)
        g_out.append(g3.reshape(shape))
        d_out.append(delta.reshape(shape))
        m_out.append(nm_.reshape(shape))
        v_out.append(nv_.reshape(shape))
    return (loss, dx0[None], *g_out, *d_out, *m_out, *v_out)
```

```python
import functools
import math

import jax
import jax.numpy as jnp
from jax import lax
from jax.experimental import pallas as pl
from jax.experimental.pallas import tpu as pltpu

F32 = jnp.float32
BF16 = jnp.bfloat16

D = 1024
N_HEADS = 16
N_KV = 4
GROUP = N_HEADS // N_KV
HD = 64
ROT = 16
THETA = 500000.0
BLK = 128
QKV = (N_HEADS + 2 * N_KV) * HD
KV_OFF = N_HEADS * HD
DFF = 2816
CONV_W = 31
CONV_PAD = 15
HALO = 16
CONV_JB = 4
EPS = 1e-6
NEG = -1e30
N_CHIPS = 4
N_DEV = 8
LANES = 128
SUBLANES = 8

ADAM_LR, ADAM_B1, ADAM_B2, ADAM_EPS, ADAM_WD, ADAM_STEP = 0.001, 0.9, 0.999, 1e-08, 0.01, 10

VMEM_LIMIT = 56 * 1024 * 1024
MESH = pl.DeviceIdType.MESH


def _params(*sem):
    return pltpu.CompilerParams(dimension_semantics=sem, vmem_limit_bytes=VMEM_LIMIT)


def _tile(n, want):
    if n <= want:
        return n
    for t in range(want, 7, -1):
        if n % t == 0 and t % 8 == 0:
            return t
    return n


MXU_COLS = 256


def _col_chunks(n):
    return [slice(c, min(c + MXU_COLS, n)) for c in range(0, n, MXU_COLS)]


def _sigmoid(v):
    return 1.0 / (1.0 + jnp.exp(-v))


def _rms_fwd(xv, gain):
    r = lax.rsqrt(jnp.mean(xv * xv, axis=-1, keepdims=True) + EPS)
    return xv * r * gain


def _rms_bwd(dh, xv, gain, dres):
    r = lax.rsqrt(jnp.mean(xv * xv, axis=-1, keepdims=True) + EPS)
    xhat = xv * r
    gy = dh * gain
    dx = r * (gy - xhat * jnp.mean(gy * xhat, axis=-1, keepdims=True))
    return dx + dres, dh * xhat


def _rope(blk, c, s1, s2):
    return blk * c + pltpu.roll(blk, LANES - ROT // 2, 1) * s1 + pltpu.roll(blk, ROT // 2, 1) * s2


def _dot(a, b):
    return jnp.dot(a, b, preferred_element_type=F32)


def _dot_tb(a, b):
    return lax.dot_general(a, b, (((1,), (1,)), ((), ())), preferred_element_type=F32)


def _dot_ta(a, b):
    return lax.dot_general(a, b, (((0,), (0,)), ((), ())), preferred_element_type=F32)


def rms_qkv(x, gain, w, rc, rs1, rs2):
    T = x.shape[0]
    tm = _tile(T, 512)

    def body(x_ref, g_ref, w_ref, c_ref, s1_ref, s2_ref, h_ref, qkv_ref):
        h = _rms_fwd(x_ref[...], g_ref[...]).astype(BF16)
        h_ref[...] = h
        acc = _dot(h, w_ref[...])
        c, s1, s2 = c_ref[...], s1_ref[...], s2_ref[...]
        n_rot = (KV_OFF + N_KV * HD) // LANES
        for j in range(n_rot):
            sl = slice(LANES * j, LANES * (j + 1))
            roped = _rope(acc[:, sl], c, s1, s2)
            if j < KV_OFF // LANES:
                roped = roped * Q_SCALE
            qkv_ref[:, sl] = roped.astype(BF16)
        qkv_ref[:, n_rot * LANES:] = acc[:, n_rot * LANES:].astype(BF16)

    row = lambda i: (i, 0)
    full = lambda i: (0, 0)
    return pl.pallas_call(
        body, name="rms_qkv", grid=(T // tm,),
        in_specs=[pl.BlockSpec((tm, D), row), pl.BlockSpec((1, D), full), pl.BlockSpec((D, QKV), full),
                  *_tab_specs(tm)],
        out_specs=[pl.BlockSpec((tm, D), row), pl.BlockSpec((tm, QKV), row)],
        out_shape=[jax.ShapeDtypeStruct((T, D), BF16), jax.ShapeDtypeStruct((T, QKV), BF16)],
        compiler_params=_params("parallel"),
    )(x, gain, w, rc, rs1, rs2)


Q_SCALE = 1.0 / math.sqrt(HD)


def _attn_mask(n, T):
    qi = lax.broadcasted_iota(jnp.int32, (BLK, 3 * BLK), 0)
    ci = lax.broadcasted_iota(jnp.int32, (BLK, 3 * BLK), 1)
    key_pos = n * BLK - BLK + ci
    return (jnp.abs(ci - BLK - qi) <= BLK) & (key_pos >= 0) & (key_pos < T)


def _kv_padded(kv, first_tile):
    low = lax.broadcasted_iota(jnp.int32, (3 * BLK, LANES), 1) < HD
    zero = jnp.zeros((3 * BLK, LANES), BF16)
    out = {}
    for g in range(N_KV):
        t = kv[:, (first_tile + g // 2) * LANES:(first_tile + g // 2 + 1) * LANES]
        swapped = jnp.concatenate([t[:, HD:], t[:, :HD]], axis=1)
        for p in range(2):
            out[g, p] = jnp.where(low if p == 0 else ~low, t if g % 2 == p else swapped, zero)
    return out


def _softmax_sink(s, valid, sk):
    s = jnp.where(valid, s, NEG)
    m = jnp.maximum(jnp.max(s, axis=-1, keepdims=True), sk)
    e = jnp.exp(s - m)
    es = jnp.exp(sk - m)
    inv = 1.0 / (jnp.sum(e, axis=-1, keepdims=True) + es)
    return e * inv, es * inv


def _attn_specs(T):
    nb = T // BLK
    kv_blk = 2 * N_KV * HD
    kv_col = KV_OFF // kv_blk
    q_spec = pl.BlockSpec((BLK, KV_OFF), lambda n: (n, 0))
    prev = pl.BlockSpec((BLK, kv_blk), lambda n: (jnp.maximum(n - 1, 0), kv_col))
    own = pl.BlockSpec((BLK, kv_blk), lambda n: (n, kv_col))
    nxt = pl.BlockSpec((BLK, kv_blk), lambda n: (jnp.minimum(n + 1, nb - 1), kv_col))
    return nb, q_spec, prev, own, nxt


def attn_fwd(qkv, sink, comm=None):
    T = qkv.shape[0]
    nb, q_spec, prev, own, nxt = _attn_specs(T)

    def body(sink_ref, q_ref, kp_ref, ko_ref, kn_ref, o_ref):
        valid = _attn_mask(pl.program_id(0), T)
        kv = jnp.concatenate([kp_ref[...], ko_ref[...], kn_ref[...]], axis=0)
        kx, vx = _kv_padded(kv, 0), _kv_padded(kv, 2)
        tile = lambda ref, h: ref[:, (h // 2) * LANES:(h // 2 + 1) * LANES]
        ss = [_dot_tb(tile(q_ref, h), kx[h // GROUP, h % 2]) for h in range(N_HEADS)]
        ps = [_softmax_sink(ss[h], valid, sink_ref[h])[0].astype(BF16) for h in range(N_HEADS)]
        for j in range(N_HEADS // 2):
            g = 2 * j // GROUP
            o = _dot(ps[2 * j], vx[g, 0]) + _dot(ps[2 * j + 1], vx[g, 1])
            o_ref[:, j * LANES:(j + 1) * LANES] = o.astype(BF16)

    (o,), got = _call(
        body, name="attn_fwd", grid=(nb,),
        in_specs=[pl.BlockSpec(memory_space=pltpu.SMEM), q_spec, prev, own, nxt],
        out_specs=[pl.BlockSpec((BLK, D), lambda n: (n, 0))],
        out_shape=[jax.ShapeDtypeStruct((T, D), BF16)],
        semantics=("parallel",), args=(sink, qkv, qkv, qkv, qkv), comm=comm)
    return o, got


def mm_res(a, w, resid, bias, name):
    T, K = a.shape
    tm = _tile(T, 512)

    def body(a_ref, w_ref, r_ref, b_ref, o_ref):
        o_ref[...] = _dot(a_ref[...], w_ref[...]) + b_ref[...] + r_ref[...]

    row = lambda i: (i, 0)
    full = lambda i: (0, 0)
    return pl.pallas_call(
        body, name=name, grid=(T // tm,),
        in_specs=[pl.BlockSpec((tm, K), row), pl.BlockSpec((K, D), full), pl.BlockSpec((tm, D), row),
                  pl.BlockSpec((1, D), full)],
        out_specs=pl.BlockSpec((tm, D), row),
        out_shape=jax.ShapeDtypeStruct((T, D), F32),
        compiler_params=_params("parallel"),
    )(a, w, resid, bias)


def rms_mm_gate(x, gain, w, bias, H, swiglu, act_dtype, name, comm=None):
    T = x.shape[0]
    tm = _tile(T, 512)
    tn = 1408 if H % 1408 == 0 else H
    nj = H // tn
    hw = D // nj

    def body(x_ref, g_ref, w1_ref, w2_ref, b1_ref, b2_ref, h_ref, pre_ref, act_ref):
        h = _rms_fwd(x_ref[...], g_ref[...]).astype(BF16)
        for jj in range(nj):
            @pl.when(pl.program_id(0) == jj)
            def _():
                h_ref[...] = h[:, jj * hw:(jj + 1) * hw]

        for cs in _col_chunks(tn):
            a = _dot(h, w1_ref[:, cs]) + b1_ref[:, cs]
            b = _dot(h, w2_ref[:, cs]) + b2_ref[:, cs]
            pre_ref[0, :, cs] = a.astype(BF16)
            pre_ref[1, :, cs] = b.astype(BF16)
            if swiglu:
                act = a * _sigmoid(a) * b
            else:
                act = a * _sigmoid(b)
            act_ref[:, cs] = act.astype(act_dtype)

    (h, pre, act), got = _call(
        body, name=name, grid=(nj, T // tm),
        in_specs=[pl.BlockSpec((tm, D), lambda j, i: (i, 0)), pl.BlockSpec((1, D), lambda j, i: (0, 0)),
                  pl.BlockSpec((D, tn), lambda j, i: (0, j)), pl.BlockSpec((D, tn), lambda j, i: (0, nj + j)),
                  pl.BlockSpec((1, tn), lambda j, i: (0, j)), pl.BlockSpec((1, tn), lambda j, i: (0, nj + j))],
        out_specs=[pl.BlockSpec((tm, hw), lambda j, i: (i, j)), pl.BlockSpec((2, tm, tn), lambda j, i: (0, i, j)),
                   pl.BlockSpec((tm, tn), lambda j, i: (i, j))],
        out_shape=[jax.ShapeDtypeStruct((T, D), BF16), jax.ShapeDtypeStruct((2, T, H), BF16),
                   jax.ShapeDtypeStruct((T, H), act_dtype)],
        semantics=("parallel", "parallel"), args=(x, gain, w, w, bias, bias), comm=comm)
    return h, pre, act, got


def _conv_tiles(T):
    tt = _tile(T, 512)
    return tt, tt // SUBLANES, D // LANES


def _fill_strided(ext, p, L):
    def ibody(i, carry):
        ext[i] = p[pl.ds(i + 1, SUBLANES, stride=L), :]
        return carry

    lax.fori_loop(0, L + CONV_W - 1, ibody, 0, unroll=2)


def _conv_specs(T, tt):
    main = pl.BlockSpec((tt, D), lambda i: (i, 0))
    per = tt // HALO
    prev = pl.BlockSpec((HALO, D), lambda i: (jnp.maximum(i * per - 1, 0), 0))
    nxt = pl.BlockSpec((HALO, D), lambda i: (jnp.minimum((i + 1) * per, T // HALO - 1), 0))
    return main, prev, nxt


def _fill_pad(pad, main_ref, prev_ref, next_ref, i, n_i, tt, nlt):
    keep_p = (i > 0).astype(F32)
    keep_n = (i < n_i - 1).astype(F32)
    for lt in range(nlt):
        sl = slice(lt * LANES, (lt + 1) * LANES)
        pad[lt, 0:HALO, :] = prev_ref[:, sl] * keep_p
        pad[lt, HALO:HALO + tt, :] = main_ref[:, sl]
        pad[lt, HALO + tt:2 * HALO + tt, :] = next_ref[:, sl] * keep_n


def conv_fwd(glu, w_dw, b_dw, ln_g, ln_b, comm=None):
    T = glu.shape[0]
    tt, L, nlt = _conv_tiles(T)
    n_i = T // tt
    main, prev, nxt = _conv_specs(T, tt)

    def body(x_ref, xp_ref, xn_ref, w_ref, b_ref, g_ref, bb_ref, dwc_ref, sw_ref, pad, ob, ext, wk):
        i = pl.program_id(0)
        _fill_pad(pad, x_ref, xp_ref, xn_ref, i, n_i, tt, nlt)
        for lt in range(nlt):
            sl = slice(lt * LANES, (lt + 1) * LANES)
            o = ob.at[lt]
            _fill_strided(ext, pad.at[lt], L)
            for k in range(CONV_W):
                wk[k] = jnp.broadcast_to(w_ref[k:k + 1, sl], (SUBLANES, LANES))

            def jbody(jb, carry):
                j = jb * CONV_JB
                accs = [None] * CONV_JB
                for m in range(CONV_W + CONV_JB - 1):
                    e = ext[j + m]
                    for u in range(CONV_JB):
                        if 0 <= m - u < CONV_W:
                            t = e * wk[m - u]
                            accs[u] = t if accs[u] is None else accs[u] + t
                for u in range(CONV_JB):
                    o[pl.ds(j + u, SUBLANES, stride=L), :] = accs[u]
                return carry

            lax.fori_loop(0, L // CONV_JB, jbody, 0)
        y = jnp.concatenate([ob[lt] for lt in range(nlt)], axis=1) + b_ref[...]
        dwc_ref[...] = y
        mu = jnp.mean(y, axis=-1, keepdims=True)
        yc = y - mu
        var = jnp.mean(yc * yc, axis=-1, keepdims=True)
        z = yc * lax.rsqrt(var + EPS) * g_ref[...] + bb_ref[...]
        sw_ref[...] = (z * _sigmoid(z)).astype(BF16)

    full = lambda i: (0, 0)
    (dwc, sw), got = _call(
        body, name="conv_fwd", grid=(n_i,),
        in_specs=[main, prev, nxt, pl.BlockSpec((32, D), full), pl.BlockSpec((1, D), full),
                  pl.BlockSpec((1, D), full), pl.BlockSpec((1, D), full)],
        out_specs=[pl.BlockSpec((tt, D), lambda i: (i, 0)), pl.BlockSpec((tt, D), lambda i: (i, 0))],
        out_shape=[jax.ShapeDtypeStruct((T, D), F32), jax.ShapeDtypeStruct((T, D), BF16)],
        scratch_shapes=[pltpu.VMEM((nlt, tt + 2 * HALO, LANES), F32), pltpu.VMEM((nlt, tt, LANES), F32),
                        pltpu.VMEM((L + 2 * HALO, SUBLANES, LANES), F32), pltpu.VMEM((32, SUBLANES, LANES), F32)],
        semantics=("parallel",), args=(glu, glu, glu, w_dw, b_dw, ln_g, ln_b), comm=comm)
    return dwc, sw, got


def final_loss(x, gain, target):
    T = x.shape[0]
    tm = _tile(T, 512)

    def body(x_ref, g_ref, t_ref, dx_ref, loss_ref, dg_ref):
        @pl.when(pl.program_id(0) == 0)
        def _():
            loss_ref[...] = jnp.zeros_like(loss_ref)
            dg_ref[...] = jnp.zeros_like(dg_ref)

        xv, gain_v = x_ref[...], g_ref[...]
        err = _rms_fwd(xv, gain_v) - t_ref[...]
        part = 0.5 * jnp.sum(jnp.mean(err * err, axis=-1, keepdims=True), axis=0, keepdims=True)
        loss_ref[...] += jnp.broadcast_to(part, loss_ref.shape)
        dx, dgr = _rms_bwd(err * (1.0 / D), xv, gain_v, 0.0)
        dx_ref[...] = dx
        dg_ref[...] += jnp.sum(dgr, axis=0, keepdims=True)

    row = lambda i: (i, 0)
    full = lambda i: (0, 0)
    return pl.pallas_call(
        body, name="final_loss", grid=(T // tm,),
        in_specs=[pl.BlockSpec((tm, D), row), pl.BlockSpec((1, D), full), pl.BlockSpec((tm, D), row)],
        out_specs=[pl.BlockSpec((tm, D), row), pl.BlockSpec((1, LANES), full), pl.BlockSpec((1, D), full)],
        out_shape=[jax.ShapeDtypeStruct((T, D), F32), jax.ShapeDtypeStruct((1, LANES), F32),
                   jax.ShapeDtypeStruct((1, D), F32)],
        compiler_params=_params("arbitrary"),
    )(x, gain, target)


def swiglu_bwd(dx, w_down, pre, name, comm=None):
    T = dx.shape[0]
    H = w_down.shape[0]
    tm = _tile(T, 512)
    tn = 1408
    nj = H // tn

    def body(dx_ref, w_ref, pre_ref, dpre_ref):
        dxb = dx_ref[...].astype(BF16)
        for cs in _col_chunks(tn):
            dact = _dot_tb(dxb, w_ref[cs, :])
            g = pre_ref[0, :, cs].astype(F32)
            u = pre_ref[1, :, cs].astype(F32)
            sg = _sigmoid(g)
            dpre_ref[0, :, cs] = (dact * u * sg * (1.0 + g * (1.0 - sg))).astype(BF16)
            dpre_ref[1, :, cs] = (dact * g * sg).astype(BF16)

    (dpre,), got = _call(
        body, name=name, grid=(nj, T // tm),
        in_specs=[pl.BlockSpec((tm, D), lambda j, i: (i, 0)), pl.BlockSpec((tn, D), lambda j, i: (j, 0)),
                  pl.BlockSpec((2, tm, tn), lambda j, i: (0, i, j))],
        out_specs=[pl.BlockSpec((2, tm, tn), lambda j, i: (0, i, j))],
        out_shape=[jax.ShapeDtypeStruct((2, T, H), BF16)],
        semantics=("parallel", "parallel"), args=(dx, w_down, pre), comm=comm)
    return dpre, got


def mm_bt_rmsbwd(dpre, w, x, gain, dres, name, comm=None):
    nh, T, H = dpre.shape
    tm = _tile(T, 512)
    tk = 1408 if H % 1408 == 0 else (1024 if H % 1024 == 0 else H)
    nk = H // tk

    def body(dp_ref, w_ref, x_ref, g_ref, dres_ref, dx_ref, dg_ref, acc):
        i, hf, kk = pl.program_id(0), pl.program_id(1), pl.program_id(2)

        @pl.when((i == 0) & (hf == 0) & (kk == 0))
        def _():
            dg_ref[...] = jnp.zeros_like(dg_ref)

        @pl.when((hf == 0) & (kk == 0))
        def _():
            acc[...] = jnp.zeros_like(acc)

        cols = pl.ds(pl.multiple_of((hf * nk + kk) * tk, LANES), tk)
        acc[...] += _dot_tb(dp_ref[...], w_ref[:, cols])

        @pl.when((hf == nh - 1) & (kk == nk - 1))
        def _():
            dx, dgr = _rms_bwd(acc[...], x_ref[...], g_ref[...], dres_ref[...])
            dx_ref[...] = dx
            dg_ref[...] += jnp.sum(dgr, axis=0, keepdims=True)

    (dx, dg), got = _call(
        body, name=name, grid=(T // tm, nh, nk),
        in_specs=[pl.BlockSpec((None, tm, tk), lambda i, hf, kk: (hf, i, kk)),
                  pl.BlockSpec((D, nh * H), lambda i, hf, kk: (0, 0), pipeline_mode=pl.Buffered(1)),
                  pl.BlockSpec((tm, D), lambda i, hf, kk: (i, 0)), pl.BlockSpec((1, D), lambda i, hf, kk: (0, 0)),
                  pl.BlockSpec((tm, D), lambda i, hf, kk: (i, 0))],
        out_specs=[pl.BlockSpec((tm, D), lambda i, hf, kk: (i, 0)), pl.BlockSpec((1, D), lambda i, hf, kk: (0, 0))],
        out_shape=[jax.ShapeDtypeStruct((T, D), F32), jax.ShapeDtypeStruct((1, D), F32)],
        scratch_shapes=[pltpu.VMEM((tm, D), F32)],
        semantics=("arbitrary", "arbitrary", "arbitrary"), args=(dpre, w, x, gain, dres), comm=comm)
    return dx, dg, got


def dw_col(a, dpre, name):
    T = a.shape[0]
    nh, _, H = dpre.shape
    per = nh * H // N_CHIPS
    bph = N_CHIPS // nh
    tt = _tile(T, 1024)
    nt = T // tt

    def body(a_ref, b_ref, o_ref, acc):
        t = pl.program_id(1)

        @pl.when(t == 0)
        def _():
            acc[...] = jnp.zeros_like(acc)

        acc[...] += _dot_ta(a_ref[...], b_ref[...])

        @pl.when(t == nt - 1)
        def _():
            o_ref[...] = acc[...].astype(BF16)

    return pl.pallas_call(
        body, name=name, grid=(N_CHIPS, nt),
        in_specs=[pl.BlockSpec((tt, D), lambda q, t: (t, 0)),
                  pl.BlockSpec((None, tt, per), lambda q, t: (q // bph, t, q % bph))],
        out_specs=pl.BlockSpec((None, D, per), lambda q, t: (q, 0, 0)),
        out_shape=jax.ShapeDtypeStruct((N_CHIPS, D, per), BF16),
        scratch_shapes=[pltpu.VMEM((D, per), F32)],
        compiler_params=_params("parallel", "arbitrary"),
    )(a, dpre)


def dw_row(a, b, name):
    T, R = a.shape
    cw = 1408 if R % 1408 == 0 else 512
    tt = _tile(T, 1024)
    nt = T // tt

    def body(a_ref, b_ref, o_ref, acc):
        t = pl.program_id(1)

        @pl.when(t == 0)
        def _():
            acc[...] = jnp.zeros_like(acc)

        acc[...] += _dot_ta(a_ref[...], b_ref[...].astype(BF16))

        @pl.when(t == nt - 1)
        def _():
            o_ref[...] = acc[...].astype(BF16)

    out = pl.pallas_call(
        body, name=name, grid=(R // cw, nt),
        in_specs=[pl.BlockSpec((tt, cw), lambda q, t: (t, q)), pl.BlockSpec((tt, D), lambda q, t: (t, 0))],
        out_specs=pl.BlockSpec((cw, D), lambda q, t: (q, 0)),
        out_shape=jax.ShapeDtypeStruct((R, D), BF16),
        scratch_shapes=[pltpu.VMEM((cw, D), F32)],
        compiler_params=_params("parallel", "arbitrary"),
    )(a, b)
    return out.reshape(N_CHIPS, R // N_CHIPS, D)


def ln_silu_bwd(dx, w_pw2, dwc, ln_g, ln_b):
    T = dx.shape[0]
    tm = _tile(T, 512)

    def body(dx_ref, w_ref, y_ref, g_ref, b_ref, dy_ref, dg_ref, db_ref, dbo_ref):
        @pl.when(pl.program_id(0) == 0)
        def _():
            dg_ref[...] = jnp.zeros_like(dg_ref)
            db_ref[...] = jnp.zeros_like(db_ref)
            dbo_ref[...] = jnp.zeros_like(dbo_ref)

        dxv = dx_ref[...]
        dsw = _dot_tb(dxv.astype(BF16), w_ref[...])
        y = y_ref[...]
        mu = jnp.mean(y, axis=-1, keepdims=True)
        yc = y - mu
        rstd = lax.rsqrt(jnp.mean(yc * yc, axis=-1, keepdims=True) + EPS)
        xhat = yc * rstd
        z = xhat * g_ref[...] + b_ref[...]
        sg = _sigmoid(z)
        dz = dsw * sg * (1.0 + z * (1.0 - sg))
        dxh = dz * g_ref[...]
        dy_ref[...] = rstd * (dxh - jnp.mean(dxh, axis=-1, keepdims=True)
                              - xhat * jnp.mean(dxh * xhat, axis=-1, keepdims=True))
        dg_ref[...] += jnp.sum(dz * xhat, axis=0, keepdims=True)
        db_ref[...] += jnp.sum(dz, axis=0, keepdims=True)
        dbo_ref[...] += jnp.sum(dxv, axis=0, keepdims=True)

    row = lambda i: (i, 0)
    full = lambda i: (0, 0)
    vec = pl.BlockSpec((1, D), full)
    return pl.pallas_call(
        body, name="ln_silu_bwd", grid=(T // tm,),
        in_specs=[pl.BlockSpec((tm, D), row), pl.BlockSpec((D, D), full), pl.BlockSpec((tm, D), row), vec, vec],
        out_specs=[pl.BlockSpec((tm, D), row), vec, vec, vec],
        out_shape=[jax.ShapeDtypeStruct((T, D), F32)] + [jax.ShapeDtypeStruct((1, D), F32)] * 3,
        compiler_params=_params("arbitrary"),
    )(dx, w_pw2, dwc, ln_g, ln_b)


def conv_bwd(ddwc, glu, pre, w_dw, comm=None):
    T = ddwc.shape[0]
    tt, L, nlt = _conv_tiles(T)
    n_i = T // tt
    main, prev, nxt = _conv_specs(T, tt)

    def body(d_ref, dp_ref, dn_ref, x_ref, xp_ref, xn_ref, pre_ref, w_ref,
             dpre_ref, dw_ref, dbd_ref, dbp_ref, padd, padx, ob, extd, extx, wk):
        i = pl.program_id(0)

        @pl.when(i == 0)
        def _():
            dw_ref[...] = jnp.zeros_like(dw_ref)
            dbd_ref[...] = jnp.zeros_like(dbd_ref)
            dbp_ref[...] = jnp.zeros_like(dbp_ref)

        _fill_pad(padd, d_ref, dp_ref, dn_ref, i, n_i, tt, nlt)
        _fill_pad(padx, x_ref, xp_ref, xn_ref, i, n_i, tt, nlt)
        for lt in range(nlt):
            sl = slice(lt * LANES, (lt + 1) * LANES)
            o = ob.at[lt]
            _fill_strided(extd, padd.at[lt], L)
            _fill_strided(extx, padx.at[lt], L)
            for k in range(CONV_W):
                wk[k] = jnp.broadcast_to(w_ref[k:k + 1, sl], (SUBLANES, LANES))

            def jbody(jb, accs):
                j = jb * 2
                accs = list(accs)
                d0, d1 = extd[j + CONV_PAD], extd[j + 1 + CONV_PAD]
                g0 = g1 = None
                for m in range(CONV_W + 1):
                    ed = extd[j + 2 * CONV_PAD + 1 - m]
                    ex = extx[j + m]
                    if m < CONV_W:
                        t = ed * wk[m]
                        g1 = t if g1 is None else g1 + t
                        accs[m] = accs[m] + d0 * ex
                    if m >= 1:
                        t = ed * wk[m - 1]
                        g0 = t if g0 is None else g0 + t
                        accs[m - 1] = accs[m - 1] + d1 * ex
                o[pl.ds(j, SUBLANES, stride=L), :] = g0
                o[pl.ds(j + 1, SUBLANES, stride=L), :] = g1
                return tuple(accs)

            accs = lax.fori_loop(0, L // 2, jbody, tuple(jnp.zeros((SUBLANES, LANES), F32) for _ in range(CONV_W)))
            for k in range(CONV_W):
                dw_ref[k:k + 1, sl] += jnp.sum(accs[k], axis=0, keepdims=True)
        dglu = jnp.concatenate([ob[lt] for lt in range(nlt)], axis=1)
        a = pre_ref[0].astype(F32)
        gate = pre_ref[1].astype(F32)
        sg = _sigmoid(gate)
        da = dglu * sg
        dgate = dglu * a * sg * (1.0 - sg)
        dpre_ref[0] = da.astype(BF16)
        dpre_ref[1] = dgate.astype(BF16)
        dbd_ref[...] += jnp.sum(d_ref[...], axis=0, keepdims=True)
        dbp_ref[0] += jnp.sum(da, axis=0, keepdims=True)
        dbp_ref[1] += jnp.sum(dgate, axis=0, keepdims=True)

    full = lambda i: (0, 0)
    (dpre, dw, dbd, dbp), got = _call(
        body, name="conv_bwd", grid=(n_i,),
        in_specs=[main, prev, nxt, main, prev, nxt, pl.BlockSpec((2, tt, D), lambda i: (0, i, 0)),
                  pl.BlockSpec((32, D), full)],
        out_specs=[pl.BlockSpec((2, tt, D), lambda i: (0, i, 0)), pl.BlockSpec((32, D), full),
                   pl.BlockSpec((1, D), full), pl.BlockSpec((2, 1, D), lambda i: (0, 0, 0))],
        out_shape=[jax.ShapeDtypeStruct((2, T, D), BF16), jax.ShapeDtypeStruct((32, D), F32),
                   jax.ShapeDtypeStruct((1, D), F32), jax.ShapeDtypeStruct((2, 1, D), F32)],
        scratch_shapes=[pltpu.VMEM((nlt, tt + 2 * HALO, LANES), F32), pltpu.VMEM((nlt, tt + 2 * HALO, LANES), F32),
                        pltpu.VMEM((nlt, tt, LANES), F32), pltpu.VMEM((L + 2 * HALO, SUBLANES, LANES), F32),
                        pltpu.VMEM((L + 2 * HALO, SUBLANES, LANES), F32), pltpu.VMEM((32, SUBLANES, LANES), F32)],
        semantics=("arbitrary",), args=(ddwc, ddwc, ddwc, glu, glu, glu, pre, w_dw), comm=comm)
    return dpre, dw, dbd, dbp, got


def mm_bt(a, w, name):
    T = a.shape[0]
    N = w.shape[0]
    tm = _tile(T, 512)

    def body(a_ref, w_ref, o_ref):
        o_ref[...] = _dot_tb(a_ref[...].astype(BF16), w_ref[...]).astype(BF16)

    return pl.pallas_call(
        body, name=name, grid=(T // tm,),
        in_specs=[pl.BlockSpec((tm, D), lambda i: (i, 0)), pl.BlockSpec((N, D), lambda i: (0, 0))],
        out_specs=pl.BlockSpec((tm, N), lambda i: (i, 0)),
        out_shape=jax.ShapeDtypeStruct((T, N), BF16),
        compiler_params=_params("parallel"),
    )(a, w)


def attn_bwd(qkv, o, do, sink, rc, rs1, rs2, comm=None):
    T = qkv.shape[0]
    nb, q_spec, prev, own, nxt = _attn_specs(T)
    kvw = N_KV * HD

    def body(sink_ref, q_ref, kp_ref, ko_ref, kn_ref, o_ref, do_ref, c_ref, s1_ref, s2_ref,
             dq_ref, dkc_ref, dvc_ref, dsink_ref):
        n = pl.program_id(0)

        @pl.when(n == 0)
        def _():
            dsink_ref[...] = jnp.zeros_like(dsink_ref)

        valid = _attn_mask(n, T)
        kv = jnp.concatenate([kp_ref[...], ko_ref[...], kn_ref[...]], axis=0)
        kx, vx = _kv_padded(kv, 0), _kv_padded(kv, 2)
        tile = lambda ref, j: ref[:, j * LANES:(j + 1) * LANES]
        ss = [_dot_tb(tile(q_ref, h // 2), kx[h // GROUP, h % 2]) for h in range(N_HEADS)]
        dps = [_dot_tb(tile(do_ref, h // 2), vx[h // GROUP, h % 2]) for h in range(N_HEADS)]
        low_q = lax.broadcasted_iota(jnp.int32, (BLK, LANES), 1) < HD
        deltas = []
        for j in range(N_HEADS // 2):
            prod = tile(do_ref, j).astype(F32) * tile(o_ref, j).astype(F32)
            deltas.append(jnp.sum(jnp.where(low_q, prod, 0.0), axis=-1, keepdims=True))
            deltas.append(jnp.sum(jnp.where(low_q, 0.0, prod), axis=-1, keepdims=True))
        lane = lax.broadcasted_iota(jnp.int32, (1, N_HEADS), 1)
        dsink = jnp.zeros((1, N_HEADS), F32)
        pbs, dss = [], []
        for h in range(N_HEADS):
            p, p_sink = _softmax_sink(ss[h], valid, sink_ref[h])
            dss.append((p * (dps[h] - deltas[h])).astype(BF16))
            pbs.append(p.astype(BF16))
            part = -jnp.sum(p_sink * deltas[h], axis=0, keepdims=True)
            dsink = dsink + jnp.where(lane == h, part, 0.0)
        dsink_ref[...] += dsink
        c, s1, s2 = c_ref[...], s1_ref[...], s2_ref[...]
        for j in range(N_HEADS // 2):
            g = 2 * j // GROUP
            dq = _dot(dss[2 * j], kx[g, 0]) + _dot(dss[2 * j + 1], kx[g, 1])
            dq_ref[:, j * LANES:(j + 1) * LANES] = (_rope(dq, c, -s1, -s2) * Q_SCALE).astype(BF16)
        low_k = lax.broadcasted_iota(jnp.int32, (3 * BLK, LANES), 1) < HD
        rows = lambda xs, g, p: jnp.concatenate([xs[GROUP * g + p], xs[GROUP * g + 2 + p]], axis=0)
        for t in range(N_KV // 2):
            sums = {}
            for g in (2 * t, 2 * t + 1):
                q2 = jnp.concatenate([tile(q_ref, 2 * g), tile(q_ref, 2 * g + 1)], axis=0)
                do2 = jnp.concatenate([tile(do_ref, 2 * g), tile(do_ref, 2 * g + 1)], axis=0)
                for p in range(2):
                    sums[g, p] = (_dot_ta(rows(dss, g, p), q2), _dot_ta(rows(pbs, g, p), do2))
            for which, ref in ((0, dkc_ref), (1, dvc_ref)):
                keep = jnp.where(low_k, sums[2 * t, 0][which], sums[2 * t + 1, 1][which])
                swap = jnp.where(low_k, sums[2 * t + 1, 0][which], sums[2 * t, 1][which])
                ref[:, t * LANES:(t + 1) * LANES] = keep + pltpu.roll(swap, HD, 1)

    row = lambda n: (n, 0)
    (dq, dkc, dvc, dsink), got = _call(
        body, name="attn_bwd", grid=(nb,),
        in_specs=[pl.BlockSpec(memory_space=pltpu.SMEM), q_spec, prev, own, nxt,
                  pl.BlockSpec((BLK, D), row), pl.BlockSpec((BLK, D), row), *_tab_specs(BLK)],
        out_specs=[pl.BlockSpec((BLK, D), row), pl.BlockSpec((None, 3 * BLK, kvw), lambda n: (n, 0, 0)),
                   pl.BlockSpec((None, 3 * BLK, kvw), lambda n: (n, 0, 0)), pl.BlockSpec((1, N_HEADS), lambda n: (0, 0))],
        out_shape=[jax.ShapeDtypeStruct((T, D), BF16), jax.ShapeDtypeStruct((nb, 3 * BLK, kvw), F32),
                   jax.ShapeDtypeStruct((nb, 3 * BLK, kvw), F32), jax.ShapeDtypeStruct((1, N_HEADS), F32)],
        semantics=("arbitrary",), args=(sink, qkv, qkv, qkv, qkv, o, do, rc, rs1, rs2), comm=comm)
    return dq, dkc, dvc, dsink, got


def kv_sum(dkc, dvc, rc, rs1, rs2):
    nb = dkc.shape[0]
    T = nb * BLK
    kvw = N_KV * HD

    def body(kp_ref, ko_ref, kn_ref, vp_ref, vo_ref, vn_ref, c_ref, s1_ref, s2_ref, out_ref):
        m = pl.program_id(0)
        has_p = (m > 0).astype(F32)
        has_n = (m < nb - 1).astype(F32)
        dk = kp_ref[...] * has_p + ko_ref[...] + kn_ref[...] * has_n
        dv = vp_ref[...] * has_p + vo_ref[...] + vn_ref[...] * has_n
        c, s1, s2 = c_ref[...], s1_ref[...], s2_ref[...]
        for j in range(kvw // LANES):
            sl = slice(LANES * j, LANES * (j + 1))
            out_ref[:, sl] = _rope(dk[:, sl], c, -s1, -s2).astype(BF16)
        out_ref[:, kvw:] = dv.astype(BF16)

    from_prev = pl.BlockSpec((None, BLK, kvw), lambda m: (jnp.maximum(m - 1, 0), 2, 0))
    from_own = pl.BlockSpec((None, BLK, kvw), lambda m: (m, 1, 0))
    from_next = pl.BlockSpec((None, BLK, kvw), lambda m: (jnp.minimum(m + 1, nb - 1), 0, 0))
    return pl.pallas_call(
        body, name="kv_sum", grid=(nb,),
        in_specs=[from_prev, from_own, from_next, from_prev, from_own, from_next, *_tab_specs(BLK)],
        out_specs=pl.BlockSpec((BLK, 2 * kvw), lambda m: (m, 0)),
        out_shape=jax.ShapeDtypeStruct((T, 2 * kvw), BF16),
        compiler_params=_params("parallel"),
    )(dkc, dkc, dkc, dvc, dvc, dvc, rc, rs1, rs2)


def _me():
    return lax.axis_index("x"), lax.axis_index("y"), lax.axis_index("c")


def _half_rows(ref, sharded_rows, chip, core):
    R, C = ref.shape[-2], ref.shape[-1]
    lead = (slice(None),) * (len(ref.shape) - 2)
    if sharded_rows:
        per = R // N_CHIPS
        return ref.at[lead + (pl.ds(chip * per + core * (per // 2), per // 2), slice(None))]
    per = C // N_CHIPS
    return ref.at[lead + (pl.ds(core * (R // 2), R // 2), pl.ds(chip * per, per))]


class _Gather:
    def __init__(self, shards, sharded_rows):
        self.inputs = list(shards)
        self.rows = list(sharded_rows)
        self.n = self.n_in = self.n_out = len(shards)
        self.out_shapes = []
        for s, rows in zip(shards, sharded_rows):
            shp = list(s.shape)
            shp[-2 if rows else -1] *= N_CHIPS
            self.out_shapes.append(jax.ShapeDtypeStruct(tuple(shp), s.dtype))
        self.scratch = [pltpu.SemaphoreType.DMA((self.n, 6)), pltpu.SemaphoreType.DMA((self.n, 6)),
                        pltpu.SemaphoreType.DMA((self.n, 2))]

    def _ctx(self, ins, outs, sems):
        send_sems, recv_sems, local_sems = sems
        x, y, c = _me()
        chips = [(1 - x, y), (x, 1 - y), (1 - x, 1 - y)]

        def half_src(w, core):
            s = ins[w]
            R = s.shape[-2]
            return s.at[pl.ds(core * (R // 2), R // 2), :]

        def dst(w, chip, core):
            return _half_rows(outs[w], self.rows[w], chip, core)

        def copy(w, k, src, chip, core, to):
            return pltpu.make_async_remote_copy(
                src_ref=src, dst_ref=dst(w, chip, core), send_sem=send_sems.at[w, k], recv_sem=recv_sems.at[w, k],
                device_id=to, device_id_type=MESH)

        def local(w, core):
            return pltpu.make_async_copy(half_src(w, core), dst(w, 2 * x + y, core), local_sems.at[w, core])

        def first(w, j):
            qx, qy = chips[j]
            return copy(w, j, half_src(w, c), 2 * x + y, c, (qx, qy, c))

        def landed(w, j):
            qx, qy = chips[j]
            return copy(w, j, dst(w, 2 * qx + qy, c), 2 * qx + qy, c, (x, y, c))

        def passed(w, j):
            qx, qy = chips[j]
            return copy(w, 3 + j, dst(w, 2 * qx + qy, c), 2 * qx + qy, c, (x, y, 1 - c))

        def from_sibling(w, j):
            qx, qy = chips[j]
            return copy(w, 3 + j, dst(w, 2 * qx + qy, 1 - c), 2 * qx + qy, 1 - c, (x, y, c))

        return local, first, landed, passed, from_sibling

    def start(self, ins, outs, sems):
        local, first, _, _, _ = self._ctx(ins, outs, sems)
        for w in range(self.n):
            for core in range(2):
                local(w, core).start()
            for j in range(3):
                first(w, j).start()

    def mid(self, ins, outs, sems):
        _, _, landed, passed, _ = self._ctx(ins, outs, sems)
        for w in range(self.n):
            for j in range(3):
                landed(w, j).wait_recv()
                passed(w, j).start()

    def end(self, ins, outs, sems):
        local, first, _, passed, from_sibling = self._ctx(ins, outs, sems)
        for w in range(self.n):
            for j in range(3):
                from_sibling(w, j).wait_recv()
        for w in range(self.n):
            for j in range(3):
                first(w, j).wait_send()
                passed(w, j).wait_send()
            for core in range(2):
                local(w, core).wait()


class _Scatter:
    def __init__(self, grads, small=None):
        self.inputs = list(grads) + ([small] if small is not None else [])
        self.ng = len(grads)
        self.n = self.n_in = self.n_out = len(self.inputs)
        self.out_shapes = [jax.ShapeDtypeStruct((N_DEV, g.shape[1] // 2, g.shape[2]), g.dtype) for g in grads]
        if small is not None:
            self.out_shapes.append(jax.ShapeDtypeStruct((N_DEV,) + small.shape, small.dtype))
        self.scratch = [pltpu.SemaphoreType.DMA((self.n, N_DEV)), pltpu.SemaphoreType.DMA((self.n, N_DEV)),
                        pltpu.SemaphoreType.DMA((self.n,))]

    def _ctx(self, ins, outs, sems):
        send_sems, recv_sems, local_sems = sems
        x, y, c = _me()
        me = 4 * x + 2 * y + c

        def piece(w, chip, core):
            if w >= self.ng:
                return ins[w]
            half = ins[w].shape[1] // 2
            return ins[w].at[chip, pl.ds(core * half, half), :]

        def peer_of(k):
            return x ^ ((k >> 2) & 1), y ^ ((k >> 1) & 1), c ^ (k & 1)

        def local(w):
            return pltpu.make_async_copy(piece(w, 2 * x + y, c), outs[w].at[me], local_sems.at[w])

        def send(w, k):
            px, py, pc = peer_of(k)
            return pltpu.make_async_remote_copy(
                src_ref=piece(w, 2 * px + py, pc), dst_ref=outs[w].at[me], send_sem=send_sems.at[w, k],
                recv_sem=recv_sems.at[w, k], device_id=(px, py, pc), device_id_type=MESH)

        def recv(w, k):
            px, py, pc = peer_of(k)
            return pltpu.make_async_remote_copy(
                src_ref=piece(w, 2 * x + y, c), dst_ref=outs[w].at[4 * px + 2 * py + pc], send_sem=send_sems.at[w, k],
                recv_sem=recv_sems.at[w, k], device_id=(px, py, pc), device_id_type=MESH)

        return local, send, recv

    def start(self, ins, outs, sems):
        local, send, _ = self._ctx(ins, outs, sems)
        for w in range(self.n):
            local(w).start()
            for k in range(1, N_DEV):
                send(w, k).start()

    def mid(self, ins, outs, sems):
        pass

    def end(self, ins, outs, sems):
        local, send, recv = self._ctx(ins, outs, sems)
        for w in range(self.n):
            for k in range(1, N_DEV):
                recv(w, k).wait_recv()
        for w in range(self.n):
            for k in range(1, N_DEV):
                send(w, k).wait_send()
            local(w).wait()


def exchange(plan, name):
    def body(*refs):
        ins, outs, sems = refs[:plan.n_in], refs[plan.n_in:plan.n_in + plan.n_out], refs[plan.n_in + plan.n_out:]
        plan.start(ins, outs, sems)
        plan.mid(ins, outs, sems)
        plan.end(ins, outs, sems)

    any_spec = pl.BlockSpec(memory_space=pl.ANY)
    return pl.pallas_call(
        body, name=name, in_specs=[any_spec] * plan.n_in, out_specs=[any_spec] * plan.n_out,
        out_shape=plan.out_shapes, scratch_shapes=plan.scratch,
    )(*plan.inputs)


def _call(body, *, name, grid, in_specs, out_specs, out_shape, scratch_shapes=(), semantics, args, comm=None):
    if comm is None:
        outs = pl.pallas_call(
            body, name=name, grid=grid, in_specs=in_specs, out_specs=out_specs, out_shape=out_shape,
            scratch_shapes=list(scratch_shapes), compiler_params=_params(*semantics))(*args)
        return outs, []
    n_in, n_out, n_scr = len(in_specs), len(out_specs), len(scratch_shapes)

    total = math.prod(grid)
    first, middle, last = 0, total // 2 - 1, total - 1
    assert first <= middle < last

    def at(step):
        lin = pl.program_id(0)
        for d in range(1, len(grid)):
            lin = lin * grid[d] + pl.program_id(d)
        return lin == step

    def hosted(*refs):
        h_in, c_in = refs[:n_in], refs[n_in:n_in + comm.n_in]
        rest = refs[n_in + comm.n_in:]
        h_out, c_out = rest[:n_out], rest[n_out:n_out + comm.n_out]
        rest = rest[n_out + comm.n_out:]
        h_scr, c_scr = rest[:n_scr], rest[n_scr:]

        @pl.when(at(first))
        def _():
            comm.start(c_in, c_out, c_scr)

        body(*h_in, *h_out, *h_scr)

        @pl.when(at(middle))
        def _():
            comm.mid(c_in, c_out, c_scr)

        @pl.when(at(last))
        def _():
            comm.end(c_in, c_out, c_scr)

    any_spec = pl.BlockSpec(memory_space=pl.ANY)
    outs = pl.pallas_call(
        hosted, name=name, grid=grid, in_specs=list(in_specs) + [any_spec] * comm.n_in,
        out_specs=list(out_specs) + [any_spec] * comm.n_out, out_shape=list(out_shape) + comm.out_shapes,
        scratch_shapes=list(scratch_shapes) + comm.scratch,
        compiler_params=_params(*(["arbitrary"] * len(grid))))(*args, *comm.inputs)
    return outs[:n_out], outs[n_out:]


def sum_swap(pieces, name):
    nl = len(pieces)
    _, r2, cc = pieces[0].shape
    tr = 128 if r2 % 128 == 0 else r2 // 2
    n = r2 // tr

    def body(*refs):
        p_refs, out = refs[:nl], refs[nl]
        slots, send_sems, local_sems, recv_sem = refs[nl + 1:]
        x, y, c = _me()
        sibling = (x, y, 1 - c)
        l, i = pl.program_id(0), pl.program_id(1)
        step = l * n + i

        def rows(st, core):
            return out.at[st // n, pl.ds(core * r2 + (st % n) * tr, tr), :]

        def copies(st):
            slot = st % 2
            local = pltpu.make_async_copy(slots.at[slot], rows(st, c), local_sems.at[slot])
            remote = pltpu.make_async_remote_copy(
                src_ref=slots.at[slot], dst_ref=rows(st, c), send_sem=send_sems.at[slot], recv_sem=recv_sem,
                device_id=sibling, device_id_type=MESH)
            return local, remote

        for ll in range(nl):
            @pl.when(l == ll)
            def _():
                acc = p_refs[ll][0].astype(F32)
                for d in range(1, N_DEV):
                    acc = acc + p_refs[ll][d].astype(F32)
                slots[step % 2] = acc

        for cp in copies(step):
            cp.start()

        @pl.when(step >= 1)
        def _():
            local, remote = copies(step - 1)
            local.wait()
            remote.wait_send()

        @pl.when(step == nl * n - 1)
        def _():
            local, remote = copies(step)
            local.wait()
            remote.wait_send()
            theirs = out.at[:, pl.ds((1 - c) * r2, r2), :]
            pltpu.make_async_remote_copy(src_ref=theirs, dst_ref=theirs, send_sem=send_sems.at[0],
                                         recv_sem=recv_sem, device_id=sibling, device_id_type=MESH).wait_recv()

    def piece_spec(ll):
        def index(l, i):
            return (0, jnp.where(l == ll, i, jnp.where(l < ll, 0, n - 1)), 0)
        return pl.BlockSpec((N_DEV, tr, cc), index)

    return pl.pallas_call(
        body, name=name, grid=(nl, n),
        in_specs=[piece_spec(ll) for ll in range(nl)],
        out_specs=pl.BlockSpec(memory_space=pl.ANY),
        out_shape=jax.ShapeDtypeStruct((nl, 2 * r2, cc), F32),
        scratch_shapes=[pltpu.VMEM((2, tr, cc), F32), pltpu.SemaphoreType.DMA((2,)), pltpu.SemaphoreType.DMA((2,)),
                        pltpu.SemaphoreType.DMA(())],
        compiler_params=_params("arbitrary", "arbitrary"),
    )(*pieces)


def sum_pieces(pieces, name):
    _, R, C = pieces.shape
    tr = _tile(R, 128) if R % 128 == 0 else R

    def body(p_ref, o_ref):
        acc = p_ref[0].astype(F32)
        for d in range(1, N_DEV):
            acc = acc + p_ref[d].astype(F32)
        o_ref[...] = acc

    return pl.pallas_call(
        body, name=name, grid=(R // tr,),
        in_specs=[pl.BlockSpec((N_DEV, tr, C), lambda i: (0, i, 0))],
        out_specs=pl.BlockSpec((tr, C), lambda i: (i, 0)),
        out_shape=jax.ShapeDtypeStruct((R, C), F32),
        compiler_params=_params("parallel"),
    )(pieces)


def adamw(w, g, m, v, name):
    Lyr, R, C = w.shape
    tr = _tile(R, 256) if R % 8 == 0 else R
    c1 = 1.0 / (1.0 - ADAM_B1 ** ADAM_STEP)
    c2 = 1.0 / (1.0 - ADAM_B2 ** ADAM_STEP)

    def body(w_ref, g_ref, m_ref, v_ref, d_ref, nm_ref, nv_ref):
        gv = g_ref[...]
        nm = ADAM_B1 * m_ref[...] + (1.0 - ADAM_B1) * gv
        nv = ADAM_B2 * v_ref[...] + (1.0 - ADAM_B2) * (gv * gv)
        nm_ref[...] = nm
        nv_ref[...] = nv
        d_ref[...] = -ADAM_LR * ((nm * c1) / (jnp.sqrt(nv * c2) + ADAM_EPS) + ADAM_WD * w_ref[...])

    spec = pl.BlockSpec((None, tr, C), lambda l, i: (l, i, 0))
    shp = jax.ShapeDtypeStruct(w.shape, F32)
    return pl.pallas_call(
        body, name=name, grid=(Lyr, R // tr),
        in_specs=[spec] * 4, out_specs=[spec] * 3, out_shape=[shp] * 3,
        compiler_params=_params("parallel", "parallel"),
    )(w, g, m, v)


def _rope_tables(T):
    pos = jnp.arange(T, dtype=F32)
    inv_freq = THETA ** (-jnp.arange(0, ROT, 2, dtype=F32) / ROT)
    ang = pos[:, None] * inv_freq[None, :]
    cos, sin = jnp.cos(ang), jnp.sin(ang)
    half = ROT // 2
    one = jnp.ones((T, HD - ROT), F32)
    zero = jnp.zeros((T, HD - ROT), F32)
    zh = jnp.zeros((T, half), F32)
    c = [cos, cos, one]
    s1 = [-sin, zh, zero]
    s2 = [zh, sin, zero]
    return jnp.concatenate(c + c + s1 + s1 + s2 + s2, axis=1)


def _tab_specs(rows):
    return [pl.BlockSpec((rows, LANES), lambda i, k=k: (i, k)) for k in range(3)]


def kernel(x, attn_norm, attn_w_qkv, attn_w_o, attn_sink, conv_norm, conv_w_pw1, conv_b_pw1, conv_w_dw, conv_b_dw, conv_ln_g, conv_ln_b, conv_w_pw2, conv_b_pw2, ffn_norm, ffn_w_gu, ffn_w_down, final_norm, loss_target, m_attn_norm, m_attn_w_qkv, m_attn_w_o, m_attn_sink, m_conv_norm, m_conv_w_pw1, m_conv_b_pw1, m_conv_w_dw, m_conv_b_dw, m_conv_ln_g, m_conv_ln_b, m_conv_w_pw2, m_conv_b_pw2, m_ffn_norm, m_ffn_w_gu, m_ffn_w_down, m_final_norm, v_attn_norm, v_attn_w_qkv, v_attn_w_o, v_attn_sink, v_conv_norm, v_conv_w_pw1, v_conv_b_pw1, v_conv_w_dw, v_conv_b_dw, v_conv_ln_g, v_conv_ln_b, v_conv_w_pw2, v_conv_b_pw2, v_ffn_norm, v_ffn_w_gu, v_ffn_w_down, v_final_norm):
    T = x.shape[1]
    x0 = x[0]
    target = loss_target[0]
    ix, iy = lax.axis_index("x"), lax.axis_index("y")
    chip = 2 * ix + iy
    rc = rs1 = rs2 = _rope_tables(T)

    bf = lambda t: t.astype(BF16)
    col_row = [False, True]
    w_qkv, w_o = exchange(_Gather([bf(attn_w_qkv[0]), bf(attn_w_o[0])], col_row), "gather_attn")

    def place(vec, width):
        return lax.dynamic_update_slice(jnp.zeros((vec.shape[0], N_CHIPS * width), F32), vec, (0, chip * width))

    small_rows = jnp.concatenate([
        place(conv_norm, 256), place(conv_b_pw1, 512).reshape(2, D), place(conv_b_dw, 256), place(conv_ln_g, 256),
        place(conv_ln_b, 256), place(conv_b_pw2, 256), jnp.zeros((1, D), F32),
        place(conv_w_dw[0], 256), jnp.zeros((1, D), F32)], axis=0)
    got = exchange(_Scatter([], small_rows), "gather_small_params")[0]
    psmall = sum_pieces(got, "sum_small_params") * 0.5
    p_conv_norm, p_b_pw1 = psmall[0:1], psmall[1:3].reshape(1, 2 * D)
    p_b_dw, p_ln_g, p_ln_b, p_b_pw2 = psmall[3:4], psmall[4:5], psmall[5:6], psmall[6:7]
    p_w_dw = psmall[8:40]

    h0, qkv = rms_qkv(x0, attn_norm, w_qkv, rc, rs1, rs2)
    sink = attn_sink[0]
    o, (w_gu0, w_down0) = attn_fwd(qkv, sink, comm=_Gather([bf(ffn_w_gu[0]), bf(ffn_w_down[0])], col_row))
    zero_b = jnp.zeros((1, D), F32)
    x1 = mm_res(o, w_o, x0, zero_b, "attn_out")
    zero_gu = jnp.zeros((1, 2 * DFF), F32)
    h1, gu0, act0, (w_pw1, w_pw2) = rms_mm_gate(
        x1, ffn_norm[0:1], w_gu0, zero_gu, DFF, True, BF16, "ffn0_up",
        comm=_Gather([bf(conv_w_pw1[0]), bf(conv_w_pw2[0])], col_row))
    x2 = mm_res(act0, w_down0, x1, zero_b, "ffn0_down")
    h2, pre, glu, _ = rms_mm_gate(x2, p_conv_norm, w_pw1, p_b_pw1, D, False, F32, "conv_pw1")
    dwc, sw, (w_gu1, w_down1) = conv_fwd(glu, p_w_dw, p_b_dw, p_ln_g, p_ln_b,
                                         comm=_Gather([bf(ffn_w_gu[1]), bf(ffn_w_down[1])], col_row))
    x3 = mm_res(sw, w_pw2, x2, p_b_pw2, "conv_pw2")
    h3, gu1, act1, _ = rms_mm_gate(x3, ffn_norm[1:2], w_gu1, zero_gu, DFF, True, BF16, "ffn1_up")
    x4 = mm_res(act1, w_down1, x3, zero_b, "ffn1_down")
    dx4, loss_part, d_final = final_loss(x4, final_norm.reshape(1, D), target)
    loss = lax.psum(loss_part[0, 0], ("x", "y", "c"))

    dgu1, _ = swiglu_bwd(dx4, w_down1, gu1, "ffn1_down_bwd")
    g_down1 = dw_row(act1, dx4, "ffn1_down_dw")
    dx3, d_ffn1, _ = mm_bt_rmsbwd(dgu1, w_gu1, x3, ffn_norm[1:2], dx4, "ffn1_up_bwd")
    g_gu1 = dw_col(h3, dgu1, "ffn1_up_dw")

    ddwc, d_ln_g, d_ln_b, d_b_pw2 = ln_silu_bwd(dx3, w_pw2, dwc, p_ln_g, p_ln_b)
    g_pw2 = dw_row(sw, dx3, "conv_pw2_dw")
    dpre, d_w_dw, d_b_dw, d_b_pw1, (r_gu1, r_down1) = conv_bwd(ddwc, glu, pre, p_w_dw,
                                                               comm=_Scatter([g_gu1, g_down1]))
    dx2, d_conv_norm, _ = mm_bt_rmsbwd(dpre, w_pw1, x2, p_conv_norm, dx3, "conv_pw1_bwd")
    g_pw1 = dw_col(h2, dpre, "conv_pw1_dw")

    dgu0, (r_pw1, r_pw2) = swiglu_bwd(dx2, w_down0, gu0, "ffn0_down_bwd", comm=_Scatter([g_pw1, g_pw2]))
    g_down0 = dw_row(act0, dx2, "ffn0_down_dw")
    dx1, d_ffn0, _ = mm_bt_rmsbwd(dgu0, w_gu0, x1, ffn_norm[0:1], dx2, "ffn0_up_bwd")
    g_gu0 = dw_col(h1, dgu0, "ffn0_up_dw")

    do = mm_bt(dx1, w_o, "attn_out_bwd")
    g_o = dw_row(o, dx1, "attn_out_dw")
    dq, dkc, dvc, d_sink, (r_gu0, r_down0, r_o) = attn_bwd(qkv, o, do, sink, rc, rs1, rs2,
                                                           comm=_Scatter([g_gu0, g_down0, g_o]))
    dkv = kv_sum(dkc, dvc, rc, rs1, rs2)
    dqkv = jnp.concatenate([dq, dkv], axis=1)[None]
    g_qkv = dw_col(h0, dqkv, "attn_qkv_dw")
    dx0, d_attn_norm, (r_qkv,) = mm_bt_rmsbwd(dqkv, w_qkv, x0, attn_norm, dx1, "attn_qkv_bwd",
                                              comm=_Scatter([g_qkv]))

    pad16 = lambda t: jnp.concatenate([t, jnp.zeros((1, D - t.shape[1]), F32)], axis=1)
    small_g = jnp.concatenate([
        d_attn_norm, pad16(d_sink), d_conv_norm, d_b_pw1.reshape(2, D), d_b_dw, d_ln_g, d_ln_b, d_b_pw2,
        d_ffn0, d_ffn1, d_final, jnp.zeros((4, D), F32), d_w_dw], axis=0)
    r_small, = exchange(_Scatter([], small_g), "scatter_small")
    gf_gu = sum_swap([r_gu0, r_gu1], "sum_gu")
    gf_down = sum_swap([r_down0, r_down1], "sum_down")
    gf_pw1, gf_pw2 = sum_swap([r_pw1], "sum_pw1"), sum_swap([r_pw2], "sum_pw2")
    gf_qkv, gf_o = sum_swap([r_qkv], "sum_qkv"), sum_swap([r_o], "sum_o")
    gs = sum_pieces(r_small, "sum_small_grads")

    def take(row0, nrows, width):
        return lax.dynamic_slice(gs, (row0, chip * width), (nrows, width))

    grads = {
        "attn_norm": gs[0:1], "attn_w_qkv": gf_qkv, "attn_w_o": gf_o, "attn_sink": gs[1:2, :N_HEADS],
        "conv_norm": take(2, 1, 256), "conv_w_pw1": gf_pw1,
        "conv_b_pw1": lax.dynamic_slice(gs[3:5].reshape(1, 2 * D), (0, chip * 512), (1, 512)),
        "conv_w_dw": take(16, 32, 256)[None, :CONV_W], "conv_b_dw": take(5, 1, 256), "conv_ln_g": take(6, 1, 256),
        "conv_ln_b": take(7, 1, 256), "conv_w_pw2": gf_pw2, "conv_b_pw2": take(8, 1, 256),
        "ffn_norm": gs[9:11], "ffn_w_gu": gf_gu, "ffn_w_down": gf_down, "final_norm": gs[11],
    }
    weights = dict(attn_norm=attn_norm, attn_w_qkv=attn_w_qkv, attn_w_o=attn_w_o, attn_sink=attn_sink,
                   conv_norm=conv_norm, conv_w_pw1=conv_w_pw1, conv_b_pw1=conv_b_pw1, conv_w_dw=conv_w_dw,
                   conv_b_dw=conv_b_dw, conv_ln_g=conv_ln_g, conv_ln_b=conv_ln_b, conv_w_pw2=conv_w_pw2,
                   conv_b_pw2=conv_b_pw2, ffn_norm=ffn_norm, ffn_w_gu=ffn_w_gu, ffn_w_down=ffn_w_down,
                   final_norm=final_norm)
    m_in = dict(attn_norm=m_attn_norm, attn_w_qkv=m_attn_w_qkv, attn_w_o=m_attn_w_o, attn_sink=m_attn_sink,
                conv_norm=m_conv_norm, conv_w_pw1=m_conv_w_pw1, conv_b_pw1=m_conv_b_pw1, conv_w_dw=m_conv_w_dw,
                conv_b_dw=m_conv_b_dw, conv_ln_g=m_conv_ln_g, conv_ln_b=m_conv_ln_b, conv_w_pw2=m_conv_w_pw2,
                conv_b_pw2=m_conv_b_pw2, ffn_norm=m_ffn_norm, ffn_w_gu=m_ffn_w_gu, ffn_w_down=m_ffn_w_down,
                final_norm=m_final_norm)
    v_in = dict(attn_norm=v_attn_norm, attn_w_qkv=v_attn_w_qkv, attn_w_o=v_attn_w_o, attn_sink=v_attn_sink,
                conv_norm=v_conv_norm, conv_w_pw1=v_conv_w_pw1, conv_b_pw1=v_conv_b_pw1, conv_w_dw=v_conv_w_dw,
                conv_b_dw=v_conv_b_dw, conv_ln_g=v_conv_ln_g, conv_ln_b=v_conv_ln_b, conv_w_pw2=v_conv_w_pw2,
                conv_b_pw2=v_conv_b_pw2, ffn_norm=v_ffn_norm, ffn_w_gu=v_ffn_w_gu, ffn_w_down=v_ffn_w_down,
                final_norm=v_final_norm)
    order = list(weights)
    g_out, d_out, m_out, v_out = [], [], [], []
    for nm in order:
        w = weights[nm]
        shape = w.shape
        as3 = lambda t: t.reshape((1,) * (3 - len(shape)) + shape) if len(shape) < 3 else t.reshape(shape)
        g3 = as3(grads[nm].reshape(shape))
        delta, nm_, nv_ = adamw(as3(w), g3, as3(m_in[nm]), as3(v_in[nm]), "adamw_" + nm)
        g_out.append(g3.reshape(shape))
        d_out.append(delta.reshape(shape))
        m_out.append(nm_.reshape(shape))
        v_out.append(nv_.reshape(shape))
    return (loss, dx0[None], *g_out, *d_out, *m_out, *v_out)
```

```python
import functools
import math

import jax
import jax.numpy as jnp
from jax import lax
from jax.experimental import pallas as pl
from jax.experimental.pallas import tpu as pltpu

F32 = jnp.float32
BF16 = jnp.bfloat16

D = 1024
N_HEADS = 16
N_KV = 4
GROUP = N_HEADS // N_KV
HD = 64
ROT = 16
THETA = 500000.0
BLK = 128
QKV = (N_HEADS + 2 * N_KV) * HD
KV_OFF = N_HEADS * HD
DFF = 2816
CONV_W = 31
CONV_PAD = 15
HALO = 16
CONV_JB = 4
EPS = 1e-6
NEG = -1e30
N_CHIPS = 4
N_DEV = 8
LANES = 128
SUBLANES = 8

ADAM_LR, ADAM_B1, ADAM_B2, ADAM_EPS, ADAM_WD, ADAM_STEP = 0.001, 0.9, 0.999, 1e-08, 0.01, 10

VMEM_LIMIT = 56 * 1024 * 1024
MESH = pl.DeviceIdType.MESH


def _params(*sem):
    return pltpu.CompilerParams(dimension_semantics=sem, vmem_limit_bytes=VMEM_LIMIT)


def _tile(n, want):
    if n <= want:
        return n
    for t in range(want, 7, -1):
        if n % t == 0 and t % 8 == 0:
            return t
    return n


MXU_COLS = 256


def _col_chunks(n):
    return [slice(c, min(c + MXU_COLS, n)) for c in range(0, n, MXU_COLS)]


def _sigmoid(v):
    return 1.0 / (1.0 + jnp.exp(-v))


def _rms_fwd(xv, gain):
    r = lax.rsqrt(jnp.mean(xv * xv, axis=-1, keepdims=True) + EPS)
    return xv * r * gain


def _rms_bwd(dh, xv, gain, dres):
    r = lax.rsqrt(jnp.mean(xv * xv, axis=-1, keepdims=True) + EPS)
    xhat = xv * r
    gy = dh * gain
    dx = r * (gy - xhat * jnp.mean(gy * xhat, axis=-1, keepdims=True))
    return dx + dres, dh * xhat


def _rope(blk, c, s1, s2):
    return blk * c + pltpu.roll(blk, LANES - ROT // 2, 1) * s1 + pltpu.roll(blk, ROT // 2, 1) * s2


def _dot(a, b):
    return jnp.dot(a, b, preferred_element_type=F32)


def _dot_tb(a, b):
    return lax.dot_general(a, b, (((1,), (1,)), ((), ())), preferred_element_type=F32)


def _dot_ta(a, b):
    return lax.dot_general(a, b, (((0,), (0,)), ((), ())), preferred_element_type=F32)


def rms_qkv(x, gain, w, rc, rs1, rs2):
    T = x.shape[0]
    tm = _tile(T, 512)

    def body(x_ref, g_ref, w_ref, c_ref, s1_ref, s2_ref, h_ref, qkv_ref):
        h = _rms_fwd(x_ref[...], g_ref[...]).astype(BF16)
        h_ref[...] = h
        acc = _dot(h, w_ref[...])
        c, s1, s2 = c_ref[...], s1_ref[...], s2_ref[...]
        n_rot = (KV_OFF + N_KV * HD) // LANES
        for j in range(n_rot):
            sl = slice(LANES * j, LANES * (j + 1))
            roped = _rope(acc[:, sl], c, s1, s2)
            if j < KV_OFF // LANES:
                roped = roped * Q_SCALE
            qkv_ref[:, sl] = roped.astype(BF16)
        qkv_ref[:, n_rot * LANES:] = acc[:, n_rot * LANES:].astype(BF16)

    row = lambda i: (i, 0)
    full = lambda i: (0, 0)
    return pl.pallas_call(
        body, name="rms_qkv", grid=(T // tm,),
        in_specs=[pl.BlockSpec((tm, D), row), pl.BlockSpec((1, D), full), pl.BlockSpec((D, QKV), full),
                  *_tab_specs(tm)],
        out_specs=[pl.BlockSpec((tm, D), row), pl.BlockSpec((tm, QKV), row)],
        out_shape=[jax.ShapeDtypeStruct((T, D), BF16), jax.ShapeDtypeStruct((T, QKV), BF16)],
        compiler_params=_params("parallel"),
    )(x, gain, w, rc, rs1, rs2)


Q_SCALE = 1.0 / math.sqrt(HD)


def _attn_mask(n, T):
    qi = lax.broadcasted_iota(jnp.int32, (BLK, 3 * BLK), 0)
    ci = lax.broadcasted_iota(jnp.int32, (BLK, 3 * BLK), 1)
    key_pos = n * BLK - BLK + ci
    return (jnp.abs(ci - BLK - qi) <= BLK) & (key_pos >= 0) & (key_pos < T)


def _kv_padded(kv, first_tile):
    low = lax.broadcasted_iota(jnp.int32, (3 * BLK, LANES), 1) < HD
    zero = jnp.zeros((3 * BLK, LANES), BF16)
    out = {}
    for g in range(N_KV):
        t = kv[:, (first_tile + g // 2) * LANES:(first_tile + g // 2 + 1) * LANES]
        swapped = jnp.concatenate([t[:, HD:], t[:, :HD]], axis=1)
        for p in range(2):
            out[g, p] = jnp.where(low if p == 0 else ~low, t if g % 2 == p else swapped, zero)
    return out


def _softmax_sink(s, valid, sk):
    s = jnp.where(valid, s, NEG)
    m = jnp.maximum(jnp.max(s, axis=-1, keepdims=True), sk)
    e = jnp.exp(s - m)
    es = jnp.exp(sk - m)
    inv = 1.0 / (jnp.sum(e, axis=-1, keepdims=True) + es)
    return e * inv, es * inv


def _attn_specs(T):
    nb = T // BLK
    kv_blk = 2 * N_KV * HD
    kv_col = KV_OFF // kv_blk
    q_spec = pl.BlockSpec((BLK, KV_OFF), lambda n: (n, 0))
    prev = pl.BlockSpec((BLK, kv_blk), lambda n: (jnp.maximum(n - 1, 0), kv_col))
    own = pl.BlockSpec((BLK, kv_blk), lambda n: (n, kv_col))
    nxt = pl.BlockSpec((BLK, kv_blk), lambda n: (jnp.minimum(n + 1, nb - 1), kv_col))
    return nb, q_spec, prev, own, nxt


def attn_fwd(qkv, sink, comm=None):
    T = qkv.shape[0]
    nb, q_spec, prev, own, nxt = _attn_specs(T)

    def body(sink_ref, q_ref, kp_ref, ko_ref, kn_ref, o_ref):
        valid = _attn_mask(pl.program_id(0), T)
        kv = jnp.concatenate([kp_ref[...], ko_ref[...], kn_ref[...]], axis=0)
        kx, vx = _kv_padded(kv, 0), _kv_padded(kv, 2)
        tile = lambda ref, h: ref[:, (h // 2) * LANES:(h // 2 + 1) * LANES]
        ss = [_dot_tb(tile(q_ref, h), kx[h // GROUP, h % 2]) for h in range(N_HEADS)]
        ps = [_softmax_sink(ss[h], valid, sink_ref[h])[0].astype(BF16) for h in range(N_HEADS)]
        for j in range(N_HEADS // 2):
            g = 2 * j // GROUP
            o = _dot(ps[2 * j], vx[g, 0]) + _dot(ps[2 * j + 1], vx[g, 1])
            o_ref[:, j * LANES:(j + 1) * LANES] = o.astype(BF16)

    (o,), got = _call(
        body, name="attn_fwd", grid=(nb,),
        in_specs=[pl.BlockSpec(memory_space=pltpu.SMEM), q_spec, prev, own, nxt],
        out_specs=[pl.BlockSpec((BLK, D), lambda n: (n, 0))],
        out_shape=[jax.ShapeDtypeStruct((T, D), BF16)],
        semantics=("parallel",), args=(sink, qkv, qkv, qkv, qkv), comm=comm)
    return o, got


def mm_res(a, w, resid, bias, name):
    T, K = a.shape
    tm = _tile(T, 512)

    def body(a_ref, w_ref, r_ref, b_ref, o_ref):
        o_ref[...] = _dot(a_ref[...], w_ref[...]) + b_ref[...] + r_ref[...]

    row = lambda i: (i, 0)
    full = lambda i: (0, 0)
    return pl.pallas_call(
        body, name=name, grid=(T // tm,),
        in_specs=[pl.BlockSpec((tm, K), row), pl.BlockSpec((K, D), full), pl.BlockSpec((tm, D), row),
                  pl.BlockSpec((1, D), full)],
        out_specs=pl.BlockSpec((tm, D), row),
        out_shape=jax.ShapeDtypeStruct((T, D), F32),
        compiler_params=_params("parallel"),
    )(a, w, resid, bias)


def rms_mm_gate(x, gain, w, bias, H, swiglu, act_dtype, name, comm=None):
    T = x.shape[0]
    tm = _tile(T, 512)
    tn = 1408 if H % 1408 == 0 else H
    nj = H // tn
    hw = D // nj

    def body(x_ref, g_ref, w1_ref, w2_ref, b1_ref, b2_ref, h_ref, pre_ref, act_ref):
        h = _rms_fwd(x_ref[...], g_ref[...]).astype(BF16)
        for jj in range(nj):
            @pl.when(pl.program_id(0) == jj)
            def _():
                h_ref[...] = h[:, jj * hw:(jj + 1) * hw]

        for cs in _col_chunks(tn):
            a = _dot(h, w1_ref[:, cs]) + b1_ref[:, cs]
            b = _dot(h, w2_ref[:, cs]) + b2_ref[:, cs]
            pre_ref[0, :, cs] = a.astype(BF16)
            pre_ref[1, :, cs] = b.astype(BF16)
            if swiglu:
                act = a * _sigmoid(a) * b
            else:
                act = a * _sigmoid(b)
            act_ref[:, cs] = act.astype(act_dtype)

    (h, pre, act), got = _call(
        body, name=name, grid=(nj, T // tm),
        in_specs=[pl.BlockSpec((tm, D), lambda j, i: (i, 0)), pl.BlockSpec((1, D), lambda j, i: (0, 0)),
                  pl.BlockSpec((D, tn), lambda j, i: (0, j)), pl.BlockSpec((D, tn), lambda j, i: (0, nj + j)),
                  pl.BlockSpec((1, tn), lambda j, i: (0, j)), pl.BlockSpec((1, tn), lambda j, i: (0, nj + j))],
        out_specs=[pl.BlockSpec((tm, hw), lambda j, i: (i, j)), pl.BlockSpec((2, tm, tn), lambda j, i: (0, i, j)),
                   pl.BlockSpec((tm, tn), lambda j, i: (i, j))],
        out_shape=[jax.ShapeDtypeStruct((T, D), BF16), jax.ShapeDtypeStruct((2, T, H), BF16),
                   jax.ShapeDtypeStruct((T, H), act_dtype)],
        semantics=("parallel", "parallel"), args=(x, gain, w, w, bias, bias), comm=comm)
    return h, pre, act, got


def _conv_tiles(T):
    tt = _tile(T, 512)
    return tt, tt // SUBLANES, D // LANES


def _fill_strided(ext, p, L):
    def ibody(i, carry):
        ext[i] = p[pl.ds(i + 1, SUBLANES, stride=L), :]
        return carry

    lax.fori_loop(0, L + CONV_W - 1, ibody, 0, unroll=2)


def _conv_specs(T, tt):
    main = pl.BlockSpec((tt, D), lambda i: (i, 0))
    per = tt // HALO
    prev = pl.BlockSpec((HALO, D), lambda i: (jnp.maximum(i * per - 1, 0), 0))
    nxt = pl.BlockSpec((HALO, D), lambda i: (jnp.minimum((i + 1) * per, T // HALO - 1), 0))
    return main, prev, nxt


def _fill_pad(pad, main_ref, prev_ref, next_ref, i, n_i, tt, nlt):
    keep_p = (i > 0).astype(F32)
    keep_n = (i < n_i - 1).astype(F32)
    for lt in range(nlt):
        sl = slice(lt * LANES, (lt + 1) * LANES)
        pad[lt, 0:HALO, :] = prev_ref[:, sl] * keep_p
        pad[lt, HALO:HALO + tt, :] = main_ref[:, sl]
        pad[lt, HALO + tt:2 * HALO + tt, :] = next_ref[:, sl] * keep_n


def conv_fwd(glu, w_dw, b_dw, ln_g, ln_b, comm=None):
    T = glu.shape[0]
    tt, L, nlt = _conv_tiles(T)
    n_i = T // tt
    main, prev, nxt = _conv_specs(T, tt)

    def body(x_ref, xp_ref, xn_ref, w_ref, b_ref, g_ref, bb_ref, dwc_ref, sw_ref, pad, ob, ext, wk):
        i = pl.program_id(0)
        _fill_pad(pad, x_ref, xp_ref, xn_ref, i, n_i, tt, nlt)
        for lt in range(nlt):
            sl = slice(lt * LANES, (lt + 1) * LANES)
            o = ob.at[lt]
            _fill_strided(ext, pad.at[lt], L)
            for k in range(CONV_W):
                wk[k] = jnp.broadcast_to(w_ref[k:k + 1, sl], (SUBLANES, LANES))

            def jbody(jb, carry):
                j = jb * CONV_JB
                accs = [None] * CONV_JB
                for m in range(CONV_W + CONV_JB - 1):
                    e = ext[j + m]
                    for u in range(CONV_JB):
                        if 0 <= m - u < CONV_W:
                            t = e * wk[m - u]
                            accs[u] = t if accs[u] is None else accs[u] + t
                for u in range(CONV_JB):
                    o[pl.ds(j + u, SUBLANES, stride=L), :] = accs[u]
                return carry

            lax.fori_loop(0, L // CONV_JB, jbody, 0)
        y = jnp.concatenate([ob[lt] for lt in range(nlt)], axis=1) + b_ref[...]
        dwc_ref[...] = y
        mu = jnp.mean(y, axis=-1, keepdims=True)
        yc = y - mu
        var = jnp.mean(yc * yc, axis=-1, keepdims=True)
        z = yc * lax.rsqrt(var + EPS) * g_ref[...] + bb_ref[...]
        sw_ref[...] = (z * _sigmoid(z)).astype(BF16)

    full = lambda i: (0, 0)
    (dwc, sw), got = _call(
        body, name="conv_fwd", grid=(n_i,),
        in_specs=[main, prev, nxt, pl.BlockSpec((32, D), full), pl.BlockSpec((1, D), full),
                  pl.BlockSpec((1, D), full), pl.BlockSpec((1, D), full)],
        out_specs=[pl.BlockSpec((tt, D), lambda i: (i, 0)), pl.BlockSpec((tt, D), lambda i: (i, 0))],
        out_shape=[jax.ShapeDtypeStruct((T, D), F32), jax.ShapeDtypeStruct((T, D), BF16)],
        scratch_shapes=[pltpu.VMEM((nlt, tt + 2 * HALO, LANES), F32), pltpu.VMEM((nlt, tt, LANES), F32),
                        pltpu.VMEM((L + 2 * HALO, SUBLANES, LANES), F32), pltpu.VMEM((32, SUBLANES, LANES), F32)],
        semantics=("parallel",), args=(glu, glu, glu, w_dw, b_dw, ln_g, ln_b), comm=comm)
    return dwc, sw, got


def final_loss(x, gain, target):
    T = x.shape[0]
    tm = _tile(T, 512)

    def body(x_ref, g_ref, t_ref, dx_ref, loss_ref, dg_ref):
        @pl.when(pl.program_id(0) == 0)
        def _():
            loss_ref[...] = jnp.zeros_like(loss_ref)
            dg_ref[...] = jnp.zeros_like(dg_ref)

        xv, gain_v = x_ref[...], g_ref[...]
        err = _rms_fwd(xv, gain_v) - t_ref[...]
        part = 0.5 * jnp.sum(jnp.mean(err * err, axis=-1, keepdims=True), axis=0, keepdims=True)
        loss_ref[...] += jnp.broadcast_to(part, loss_ref.shape)
        dx, dgr = _rms_bwd(err * (1.0 / D), xv, gain_v, 0.0)
        dx_ref[...] = dx
        dg_ref[...] += jnp.sum(dgr, axis=0, keepdims=True)

    row = lambda i: (i, 0)
    full = lambda i: (0, 0)
    return pl.pallas_call(
        body, name="final_loss", grid=(T // tm,),
        in_specs=[pl.BlockSpec((tm, D), row), pl.BlockSpec((1, D), full), pl.BlockSpec((tm, D), row)],
        out_specs=[pl.BlockSpec((tm, D), row), pl.BlockSpec((1, LANES), full), pl.BlockSpec((1, D), full)],
        out_shape=[jax.ShapeDtypeStruct((T, D), F32), jax.ShapeDtypeStruct((1, LANES), F32),
                   jax.ShapeDtypeStruct((1, D), F32)],
        compiler_params=_params("arbitrary"),
    )(x, gain, target)


def swiglu_bwd(dx, w_down, pre, name, comm=None):
    T = dx.shape[0]
    H = w_down.shape[0]
    tm = _tile(T, 512)
    tn = 1408
    nj = H // tn

    def body(dx_ref, w_ref, pre_ref, dpre_ref):
        dxb = dx_ref[...].astype(BF16)
        for cs in _col_chunks(tn):
            dact = _dot_tb(dxb, w_ref[cs, :])
            g = pre_ref[0, :, cs].astype(F32)
            u = pre_ref[1, :, cs].astype(F32)
            sg = _sigmoid(g)
            dpre_ref[0, :, cs] = (dact * u * sg * (1.0 + g * (1.0 - sg))).astype(BF16)
            dpre_ref[1, :, cs] = (dact * g * sg).astype(BF16)

    (dpre,), got = _call(
        body, name=name, grid=(nj, T // tm),
        in_specs=[pl.BlockSpec((tm, D), lambda j, i: (i, 0)), pl.BlockSpec((tn, D), lambda j, i: (j, 0)),
                  pl.BlockSpec((2, tm, tn), lambda j, i: (0, i, j))],
        out_specs=[pl.BlockSpec((2, tm, tn), lambda j, i: (0, i, j))],
        out_shape=[jax.ShapeDtypeStruct((2, T, H), BF16)],
        semantics=("parallel", "parallel"), args=(dx, w_down, pre), comm=comm)
    return dpre, got


def mm_bt_rmsbwd(dpre, w, x, gain, dres, name, comm=None):
    nh, T, H = dpre.shape
    tm = _tile(T, 512)
    tk = 1408 if H % 1408 == 0 else (1024 if H % 1024 == 0 else H)
    nk = H // tk

    def body(dp_ref, w_ref, x_ref, g_ref, dres_ref, dx_ref, dg_ref, acc):
        i, hf, kk = pl.program_id(0), pl.program_id(1), pl.program_id(2)

        @pl.when((i == 0) & (hf == 0) & (kk == 0))
        def _():
            dg_ref[...] = jnp.zeros_like(dg_ref)

        @pl.when((hf == 0) & (kk == 0))
        def _():
            acc[...] = jnp.zeros_like(acc)

        cols = pl.ds(pl.multiple_of((hf * nk + kk) * tk, LANES), tk)
        acc[...] += _dot_tb(dp_ref[...], w_ref[:, cols])

        @pl.when((hf == nh - 1) & (kk == nk - 1))
        def _():
            dx, dgr = _rms_bwd(acc[...], x_ref[...], g_ref[...], dres_ref[...])
            dx_ref[...] = dx
            dg_ref[...] += jnp.sum(dgr, axis=0, keepdims=True)

    (dx, dg), got = _call(
        body, name=name, grid=(T // tm, nh, nk),
        in_specs=[pl.BlockSpec((None, tm, tk), lambda i, hf, kk: (hf, i, kk)),
                  pl.BlockSpec((D, nh * H), lambda i, hf, kk: (0, 0), pipeline_mode=pl.Buffered(1)),
                  pl.BlockSpec((tm, D), lambda i, hf, kk: (i, 0)), pl.BlockSpec((1, D), lambda i, hf, kk: (0, 0)),
                  pl.BlockSpec((tm, D), lambda i, hf, kk: (i, 0))],
        out_specs=[pl.BlockSpec((tm, D), lambda i, hf, kk: (i, 0)), pl.BlockSpec((1, D), lambda i, hf, kk: (0, 0))],
        out_shape=[jax.ShapeDtypeStruct((T, D), F32), jax.ShapeDtypeStruct((1, D), F32)],
        scratch_shapes=[pltpu.VMEM((tm, D), F32)],
        semantics=("arbitrary", "arbitrary", "arbitrary"), args=(dpre, w, x, gain, dres), comm=comm)
    return dx, dg, got


def dw_col(a, dpre, name):
    T = a.shape[0]
    nh, _, H = dpre.shape
    per = nh * H // N_CHIPS
    bph = N_CHIPS // nh
    tt = _tile(T, 1024)
    nt = T // tt

    def body(a_ref, b_ref, o_ref, acc):
        t = pl.program_id(1)

        @pl.when(t == 0)
        def _():
            acc[...] = jnp.zeros_like(acc)

        acc[...] += _dot_ta(a_ref[...], b_ref[...])

        @pl.when(t == nt - 1)
        def _():
            o_ref[...] = acc[...].astype(BF16)

    return pl.pallas_call(
        body, name=name, grid=(N_CHIPS, nt),
        in_specs=[pl.BlockSpec((tt, D), lambda q, t: (t, 0)),
                  pl.BlockSpec((None, tt, per), lambda q, t: (q // bph, t, q % bph))],
        out_specs=pl.BlockSpec((None, D, per), lambda q, t: (q, 0, 0)),
        out_shape=jax.ShapeDtypeStruct((N_CHIPS, D, per), BF16),
        scratch_shapes=[pltpu.VMEM((D, per), F32)],
        compiler_params=_params("parallel", "arbitrary"),
    )(a, dpre)


def dw_row(a, b, name):
    T, R = a.shape
    cw = 1408 if R % 1408 == 0 else 512
    tt = _tile(T, 1024)
    nt = T // tt

    def body(a_ref, b_ref, o_ref, acc):
        t = pl.program_id(1)

        @pl.when(t == 0)
        def _():
            acc[...] = jnp.zeros_like(acc)

        acc[...] += _dot_ta(a_ref[...], b_ref[...].astype(BF16))

        @pl.when(t == nt - 1)
        def _():
            o_ref[...] = acc[...].astype(BF16)

    out = pl.pallas_call(
        body, name=name, grid=(R // cw, nt),
        in_specs=[pl.BlockSpec((tt, cw), lambda q, t: (t, q)), pl.BlockSpec((tt, D), lambda q, t: (t, 0))],
        out_specs=pl.BlockSpec((cw, D), lambda q, t: (q, 0)),
        out_shape=jax.ShapeDtypeStruct((R, D), BF16),
        scratch_shapes=[pltpu.VMEM((cw, D), F32)],
        compiler_params=_params("parallel", "arbitrary"),
    )(a, b)
    return out.reshape(N_CHIPS, R // N_CHIPS, D)


def ln_silu_bwd(dx, w_pw2, dwc, ln_g, ln_b):
    T = dx.shape[0]
    tm = _tile(T, 512)

    def body(dx_ref, w_ref, y_ref, g_ref, b_ref, dy_ref, dg_ref, db_ref, dbo_ref):
        @pl.when(pl.program_id(0) == 0)
        def _():
            dg_ref[...] = jnp.zeros_like(dg_ref)
            db_ref[...] = jnp.zeros_like(db_ref)
            dbo_ref[...] = jnp.zeros_like(dbo_ref)

        dxv = dx_ref[...]
        dsw = _dot_tb(dxv.astype(BF16), w_ref[...])
        y = y_ref[...]
        mu = jnp.mean(y, axis=-1, keepdims=True)
        yc = y - mu
        rstd = lax.rsqrt(jnp.mean(yc * yc, axis=-1, keepdims=True) + EPS)
        xhat = yc * rstd
        z = xhat * g_ref[...] + b_ref[...]
        sg = _sigmoid(z)
        dz = dsw * sg * (1.0 + z * (1.0 - sg))
        dxh = dz * g_ref[...]
        dy_ref[...] = rstd * (dxh - jnp.mean(dxh, axis=-1, keepdims=True)
                              - xhat * jnp.mean(dxh * xhat, axis=-1, keepdims=True))
        dg_ref[...] += jnp.sum(dz * xhat, axis=0, keepdims=True)
        db_ref[...] += jnp.sum(dz, axis=0, keepdims=True)
        dbo_ref[...] += jnp.sum(dxv, axis=0, keepdims=True)

    row = lambda i: (i, 0)
    full = lambda i: (0, 0)
    vec = pl.BlockSpec((1, D), full)
    return pl.pallas_call(
        body, name="ln_silu_bwd", grid=(T // tm,),
        in_specs=[pl.BlockSpec((tm, D), row), pl.BlockSpec((D, D), full), pl.BlockSpec((tm, D), row), vec, vec],
        out_specs=[pl.BlockSpec((tm, D), row), vec, vec, vec],
        out_shape=[jax.ShapeDtypeStruct((T, D), F32)] + [jax.ShapeDtypeStruct((1, D), F32)] * 3,
        compiler_params=_params("arbitrary"),
    )(dx, w_pw2, dwc, ln_g, ln_b)


def conv_bwd(ddwc, glu, pre, w_dw, comm=None):
    T = ddwc.shape[0]
    tt, L, nlt = _conv_tiles(T)
    n_i = T // tt
    main, prev, nxt = _conv_specs(T, tt)

    def body(d_ref, dp_ref, dn_ref, x_ref, xp_ref, xn_ref, pre_ref, w_ref,
             dpre_ref, dw_ref, dbd_ref, dbp_ref, padd, padx, ob, extd, extx, wk):
        i = pl.program_id(0)

        @pl.when(i == 0)
        def _():
            dw_ref[...] = jnp.zeros_like(dw_ref)
            dbd_ref[...] = jnp.zeros_like(dbd_ref)
            dbp_ref[...] = jnp.zeros_like(dbp_ref)

        _fill_pad(padd, d_ref, dp_ref, dn_ref, i, n_i, tt, nlt)
        _fill_pad(padx, x_ref, xp_ref, xn_ref, i, n_i, tt, nlt)
        for lt in range(nlt):
            sl = slice(lt * LANES, (lt + 1) * LANES)
            o = ob.at[lt]
            _fill_strided(extd, padd.at[lt], L)
            _fill_strided(extx, padx.at[lt], L)
            for k in range(CONV_W):
                wk[k] = jnp.broadcast_to(w_ref[k:k + 1, sl], (SUBLANES, LANES))

            def jbody(jb, accs):
                j = jb * 2
                accs = list(accs)
                d0, d1 = extd[j + CONV_PAD], extd[j + 1 + CONV_PAD]
                g0 = g1 = None
                for m in range(CONV_W + 1):
                    ed = extd[j + 2 * CONV_PAD + 1 - m]
                    ex = extx[j + m]
                    if m < CONV_W:
                        t = ed * wk[m]
                        g1 = t if g1 is None else g1 + t
                        accs[m] = accs[m] + d0 * ex
                    if m >= 1:
                        t = ed * wk[m - 1]
                        g0 = t if g0 is None else g0 + t
                        accs[m - 1] = accs[m - 1] + d1 * ex
                o[pl.ds(j, SUBLANES, stride=L), :] = g0
                o[pl.ds(j + 1, SUBLANES, stride=L), :] = g1
                return tuple(accs)

            accs = lax.fori_loop(0, L // 2, jbody, tuple(jnp.zeros((SUBLANES, LANES), F32) for _ in range(CONV_W)))
            for k in range(CONV_W):
                dw_ref[k:k + 1, sl] += jnp.sum(accs[k], axis=0, keepdims=True)
        dglu = jnp.concatenate([ob[lt] for lt in range(nlt)], axis=1)
        a = pre_ref[0].astype(F32)
        gate = pre_ref[1].astype(F32)
        sg = _sigmoid(gate)
        da = dglu * sg
        dgate = dglu * a * sg * (1.0 - sg)
        dpre_ref[0] = da.astype(BF16)
        dpre_ref[1] = dgate.astype(BF16)
        dbd_ref[...] += jnp.sum(d_ref[...], axis=0, keepdims=True)
        dbp_ref[0] += jnp.sum(da, axis=0, keepdims=True)
        dbp_ref[1] += jnp.sum(dgate, axis=0, keepdims=True)

    full = lambda i: (0, 0)
    (dpre, dw, dbd, dbp), got = _call(
        body, name="conv_bwd", grid=(n_i,),
        in_specs=[main, prev, nxt, main, prev, nxt, pl.BlockSpec((2, tt, D), lambda i: (0, i, 0)),
                  pl.BlockSpec((32, D), full)],
        out_specs=[pl.BlockSpec((2, tt, D), lambda i: (0, i, 0)), pl.BlockSpec((32, D), full),
                   pl.BlockSpec((1, D), full), pl.BlockSpec((2, 1, D), lambda i: (0, 0, 0))],
        out_shape=[jax.ShapeDtypeStruct((2, T, D), BF16), jax.ShapeDtypeStruct((32, D), F32),
                   jax.ShapeDtypeStruct((1, D), F32), jax.ShapeDtypeStruct((2, 1, D), F32)],
        scratch_shapes=[pltpu.VMEM((nlt, tt + 2 * HALO, LANES), F32), pltpu.VMEM((nlt, tt + 2 * HALO, LANES), F32),
                        pltpu.VMEM((nlt, tt, LANES), F32), pltpu.VMEM((L + 2 * HALO, SUBLANES, LANES), F32),
                        pltpu.VMEM((L + 2 * HALO, SUBLANES, LANES), F32), pltpu.VMEM((32, SUBLANES, LANES), F32)],
        semantics=("arbitrary",), args=(ddwc, ddwc, ddwc, glu, glu, glu, pre, w_dw), comm=comm)
    return dpre, dw, dbd, dbp, got


def mm_bt(a, w, name):
    T = a.shape[0]
    N = w.shape[0]
    tm = _tile(T, 512)

    def body(a_ref, w_ref, o_ref):
        o_ref[...] = _dot_tb(a_ref[...].astype(BF16), w_ref[...]).astype(BF16)

    return pl.pallas_call(
        body, name=name, grid=(T // tm,),
        in_specs=[pl.BlockSpec((tm, D), lambda i: (i, 0)), pl.BlockSpec((N, D), lambda i: (0, 0))],
        out_specs=pl.BlockSpec((tm, N), lambda i: (i, 0)),
        out_shape=jax.ShapeDtypeStruct((T, N), BF16),
        compiler_params=_params("parallel"),
    )(a, w)


def attn_bwd(qkv, o, do, sink, rc, rs1, rs2, comm=None):
    T = qkv.shape[0]
    nb, q_spec, prev, own, nxt = _attn_specs(T)
    kvw = N_KV * HD

    def body(sink_ref, q_ref, kp_ref, ko_ref, kn_ref, o_ref, do_ref, c_ref, s1_ref, s2_ref,
             dq_ref, dkc_ref, dvc_ref, dsink_ref):
        n = pl.program_id(0)

        @pl.when(n == 0)
        def _():
            dsink_ref[...] = jnp.zeros_like(dsink_ref)

        valid = _attn_mask(n, T)
        kv = jnp.concatenate([kp_ref[...], ko_ref[...], kn_ref[...]], axis=0)
        kx, vx = _kv_padded(kv, 0), _kv_padded(kv, 2)
        tile = lambda ref, j: ref[:, j * LANES:(j + 1) * LANES]
        ss = [_dot_tb(tile(q_ref, h // 2), kx[h // GROUP, h % 2]) for h in range(N_HEADS)]
        dps = [_dot_tb(tile(do_ref, h // 2), vx[h // GROUP, h % 2]) for h in range(N_HEADS)]
        low_q = lax.broadcasted_iota(jnp.int32, (BLK, LANES), 1) < HD
        deltas = []
        for j in range(N_HEADS // 2):
            prod = tile(do_ref, j).astype(F32) * tile(o_ref, j).astype(F32)
            deltas.append(jnp.sum(jnp.where(low_q, prod, 0.0), axis=-1, keepdims=True))
            deltas.append(jnp.sum(jnp.where(low_q, 0.0, prod), axis=-1, keepdims=True))
        lane = lax.broadcasted_iota(jnp.int32, (1, N_HEADS), 1)
        dsink = jnp.zeros((1, N_HEADS), F32)
        pbs, dss = [], []
        for h in range(N_HEADS):
            p, p_sink = _softmax_sink(ss[h], valid, sink_ref[h])
            dss.append((p * (dps[h] - deltas[h])).astype(BF16))
            pbs.append(p.astype(BF16))
            part = -jnp.sum(p_sink * deltas[h], axis=0, keepdims=True)
            dsink = dsink + jnp.where(lane == h, part, 0.0)
        dsink_ref[...] += dsink
        c, s1, s2 = c_ref[...], s1_ref[...], s2_ref[...]
        for j in range(N_HEADS // 2):
            g = 2 * j // GROUP
            dq = _dot(dss[2 * j], kx[g, 0]) + _dot(dss[2 * j + 1], kx[g, 1])
            dq_ref[:, j * LANES:(j + 1) * LANES] = (_rope(dq, c, -s1, -s2) * Q_SCALE).astype(BF16)
        low_k = lax.broadcasted_iota(jnp.int32, (3 * BLK, LANES), 1) < HD
        rows = lambda xs, g, p: jnp.concatenate([xs[GROUP * g + p], xs[GROUP * g + 2 + p]], axis=0)
        for t in range(N_KV // 2):
            sums = {}
            for g in (2 * t, 2 * t + 1):
                q2 = jnp.concatenate([tile(q_ref, 2 * g), tile(q_ref, 2 * g + 1)], axis=0)
                do2 = jnp.concatenate([tile(do_ref, 2 * g), tile(do_ref, 2 * g + 1)], axis=0)
                for p in range(2):
                    sums[g, p] = (_dot_ta(rows(dss, g, p), q2), _dot_ta(rows(pbs, g, p), do2))
            for which, ref in ((0, dkc_ref), (1, dvc_ref)):
                keep = jnp.where(low_k, sums[2 * t, 0][which], sums[2 * t + 1, 1][which])
                swap = jnp.where(low_k, sums[2 * t + 1, 0][which], sums[2 * t, 1][which])
                ref[:, t * LANES:(t + 1) * LANES] = keep + pltpu.roll(swap, HD, 1)

    row = lambda n: (n, 0)
    (dq, dkc, dvc, dsink), got = _call(
        body, name="attn_bwd", grid=(nb,),
        in_specs=[pl.BlockSpec(memory_space=pltpu.SMEM), q_spec, prev, own, nxt,
                  pl.BlockSpec((BLK, D), row), pl.BlockSpec((BLK, D), row), *_tab_specs(BLK)],
        out_specs=[pl.BlockSpec((BLK, D), row), pl.BlockSpec((None, 3 * BLK, kvw), lambda n: (n, 0, 0)),
                   pl.BlockSpec((None, 3 * BLK, kvw), lambda n: (n, 0, 0)), pl.BlockSpec((1, N_HEADS), lambda n: (0, 0))],
        out_shape=[jax.ShapeDtypeStruct((T, QKV), BF16), jax.ShapeDtypeStruct((nb, 3 * BLK, kvw), F32),
                   jax.ShapeDtypeStruct((nb, 3 * BLK, kvw), F32), jax.ShapeDtypeStruct((1, N_HEADS), F32)],
        semantics=("arbitrary",), args=(sink, qkv, qkv, qkv, qkv, o, do, rc, rs1, rs2), comm=comm)
    return dq, dkc, dvc, dsink, got


def kv_sum(dqkv, dkc, dvc, rc, rs1, rs2):
    nb = dkc.shape[0]
    T = nb * BLK
    kvw = N_KV * HD

    def body(_, kp_ref, ko_ref, kn_ref, vp_ref, vo_ref, vn_ref, c_ref, s1_ref, s2_ref, out_ref):
        m = pl.program_id(0)
        has_p = (m > 0).astype(F32)
        has_n = (m < nb - 1).astype(F32)
        dk = kp_ref[...] * has_p + ko_ref[...] + kn_ref[...] * has_n
        dv = vp_ref[...] * has_p + vo_ref[...] + vn_ref[...] * has_n
        c, s1, s2 = c_ref[...], s1_ref[...], s2_ref[...]
        for j in range(kvw // LANES):
            sl = slice(LANES * j, LANES * (j + 1))
            out_ref[:, sl] = _rope(dk[:, sl], c, -s1, -s2).astype(BF16)
        out_ref[:, kvw:] = dv.astype(BF16)

    from_prev = pl.BlockSpec((None, BLK, kvw), lambda m: (jnp.maximum(m - 1, 0), 2, 0))
    from_own = pl.BlockSpec((None, BLK, kvw), lambda m: (m, 1, 0))
    from_next = pl.BlockSpec((None, BLK, kvw), lambda m: (jnp.minimum(m + 1, nb - 1), 0, 0))
    return pl.pallas_call(
        body, name="kv_sum", grid=(nb,),
        in_specs=[pl.BlockSpec(memory_space=pl.ANY), from_prev, from_own, from_next, from_prev, from_own, from_next,
                  *_tab_specs(BLK)],
        out_specs=pl.BlockSpec((BLK, 2 * kvw), lambda m: (m, KV_OFF // (2 * kvw))),
        out_shape=jax.ShapeDtypeStruct((T, QKV), BF16),
        input_output_aliases={0: 0},
        compiler_params=_params("parallel"),
    )(dqkv, dkc, dkc, dkc, dvc, dvc, dvc, rc, rs1, rs2)


def _me():
    return lax.axis_index("x"), lax.axis_index("y"), lax.axis_index("c")


def _half_rows(ref, sharded_rows, chip, core):
    R, C = ref.shape[-2], ref.shape[-1]
    lead = (slice(None),) * (len(ref.shape) - 2)
    if sharded_rows:
        per = R // N_CHIPS
        return ref.at[lead + (pl.ds(chip * per + core * (per // 2), per // 2), slice(None))]
    per = C // N_CHIPS
    return ref.at[lead + (pl.ds(core * (R // 2), R // 2), pl.ds(chip * per, per))]


class _Gather:
    def __init__(self, shards, sharded_rows):
        self.inputs = list(shards)
        self.rows = list(sharded_rows)
        self.n = self.n_in = self.n_out = len(shards)
        self.out_shapes = []
        for s, rows in zip(shards, sharded_rows):
            shp = list(s.shape)
            shp[-2 if rows else -1] *= N_CHIPS
            self.out_shapes.append(jax.ShapeDtypeStruct(tuple(shp), s.dtype))
        self.scratch = [pltpu.SemaphoreType.DMA((self.n, 6)), pltpu.SemaphoreType.DMA((self.n, 6)),
                        pltpu.SemaphoreType.DMA((self.n, 2))]

    def _ctx(self, ins, outs, sems):
        send_sems, recv_sems, local_sems = sems
        x, y, c = _me()
        chips = [(1 - x, y), (x, 1 - y), (1 - x, 1 - y)]

        def half_src(w, core):
            s = ins[w]
            R = s.shape[-2]
            return s.at[pl.ds(core * (R // 2), R // 2), :]

        def dst(w, chip, core):
            return _half_rows(outs[w], self.rows[w], chip, core)

        def copy(w, k, src, chip, core, to):
            return pltpu.make_async_remote_copy(
                src_ref=src, dst_ref=dst(w, chip, core), send_sem=send_sems.at[w, k], recv_sem=recv_sems.at[w, k],
                device_id=to, device_id_type=MESH)

        def local(w, core):
            return pltpu.make_async_copy(half_src(w, core), dst(w, 2 * x + y, core), local_sems.at[w, core])

        def first(w, j):
            qx, qy = chips[j]
            return copy(w, j, half_src(w, c), 2 * x + y, c, (qx, qy, c))

        def landed(w, j):
            qx, qy = chips[j]
            return copy(w, j, dst(w, 2 * qx + qy, c), 2 * qx + qy, c, (x, y, c))

        def passed(w, j):
            qx, qy = chips[j]
            return copy(w, 3 + j, dst(w, 2 * qx + qy, c), 2 * qx + qy, c, (x, y, 1 - c))

        def from_sibling(w, j):
            qx, qy = chips[j]
            return copy(w, 3 + j, dst(w, 2 * qx + qy, 1 - c), 2 * qx + qy, 1 - c, (x, y, c))

        return local, first, landed, passed, from_sibling

    def start(self, ins, outs, sems):
        local, first, _, _, _ = self._ctx(ins, outs, sems)
        for w in range(self.n):
            for core in range(2):
                local(w, core).start()
            for j in range(3):
                first(w, j).start()

    def mid(self, ins, outs, sems):
        _, _, landed, passed, _ = self._ctx(ins, outs, sems)
        for w in range(self.n):
            for j in range(3):
                landed(w, j).wait_recv()
                passed(w, j).start()

    def end(self, ins, outs, sems):
        local, first, _, passed, from_sibling = self._ctx(ins, outs, sems)
        for w in range(self.n):
            for j in range(3):
                from_sibling(w, j).wait_recv()
        for w in range(self.n):
            for j in range(3):
                first(w, j).wait_send()
                passed(w, j).wait_send()
            for core in range(2):
                local(w, core).wait()


class _Scatter:
    def __init__(self, grads, small=None):
        self.inputs = list(grads) + ([small] if small is not None else [])
        self.ng = len(grads)
        self.n = self.n_in = self.n_out = len(self.inputs)
        self.out_shapes = [jax.ShapeDtypeStruct((N_DEV, g.shape[1] // 2, g.shape[2]), g.dtype) for g in grads]
        if small is not None:
            self.out_shapes.append(jax.ShapeDtypeStruct((N_DEV,) + small.shape, small.dtype))
        self.scratch = [pltpu.SemaphoreType.DMA((self.n, N_DEV)), pltpu.SemaphoreType.DMA((self.n, N_DEV)),
                        pltpu.SemaphoreType.DMA((self.n,))]

    def _ctx(self, ins, outs, sems):
        send_sems, recv_sems, local_sems = sems
        x, y, c = _me()
        me = 4 * x + 2 * y + c

        def piece(w, chip, core):
            if w >= self.ng:
                return ins[w]
            half = ins[w].shape[1] // 2
            return ins[w].at[chip, pl.ds(core * half, half), :]

        def peer_of(k):
            return x ^ ((k >> 2) & 1), y ^ ((k >> 1) & 1), c ^ (k & 1)

        def local(w):
            return pltpu.make_async_copy(piece(w, 2 * x + y, c), outs[w].at[me], local_sems.at[w])

        def send(w, k):
            px, py, pc = peer_of(k)
            return pltpu.make_async_remote_copy(
                src_ref=piece(w, 2 * px + py, pc), dst_ref=outs[w].at[me], send_sem=send_sems.at[w, k],
                recv_sem=recv_sems.at[w, k], device_id=(px, py, pc), device_id_type=MESH)

        def recv(w, k):
            px, py, pc = peer_of(k)
            return pltpu.make_async_remote_copy(
                src_ref=piece(w, 2 * x + y, c), dst_ref=outs[w].at[4 * px + 2 * py + pc], send_sem=send_sems.at[w, k],
                recv_sem=recv_sems.at[w, k], device_id=(px, py, pc), device_id_type=MESH)

        return local, send, recv

    def start(self, ins, outs, sems):
        local, send, _ = self._ctx(ins, outs, sems)
        for w in range(self.n):
            local(w).start()
            for k in range(1, N_DEV):
                send(w, k).start()

    def mid(self, ins, outs, sems):
        pass

    def end(self, ins, outs, sems):
        local, send, recv = self._ctx(ins, outs, sems)
        for w in range(self.n):
            for k in range(1, N_DEV):
                recv(w, k).wait_recv()
        for w in range(self.n):
            for k in range(1, N_DEV):
                send(w, k).wait_send()
            local(w).wait()


class _Both:
    def __init__(self, a, b):
        self.a, self.b = a, b
        self.inputs = a.inputs + b.inputs
        self.out_shapes = a.out_shapes + b.out_shapes
        self.scratch = a.scratch + b.scratch
        self.n_in, self.n_out = a.n_in + b.n_in, a.n_out + b.n_out

    def _split(self, ins, outs, sems):
        a, na = self.a, len(self.a.scratch)
        return (ins[:a.n_in], outs[:a.n_out], sems[:na]), (ins[a.n_in:], outs[a.n_out:], sems[na:])

    def start(self, ins, outs, sems):
        pa, pb = self._split(ins, outs, sems)
        self.a.start(*pa)
        self.b.start(*pb)

    def mid(self, ins, outs, sems):
        pa, pb = self._split(ins, outs, sems)
        self.a.mid(*pa)
        self.b.mid(*pb)

    def end(self, ins, outs, sems):
        pa, pb = self._split(ins, outs, sems)
        self.a.end(*pa)
        self.b.end(*pb)


def exchange(plan, name):
    def body(*refs):
        ins, outs, sems = refs[:plan.n_in], refs[plan.n_in:plan.n_in + plan.n_out], refs[plan.n_in + plan.n_out:]
        plan.start(ins, outs, sems)
        plan.mid(ins, outs, sems)
        plan.end(ins, outs, sems)

    any_spec = pl.BlockSpec(memory_space=pl.ANY)
    return pl.pallas_call(
        body, name=name, in_specs=[any_spec] * plan.n_in, out_specs=[any_spec] * plan.n_out,
        out_shape=plan.out_shapes, scratch_shapes=plan.scratch,
    )(*plan.inputs)


def _call(body, *, name, grid, in_specs, out_specs, out_shape, scratch_shapes=(), semantics, args, comm=None):
    if comm is None:
        outs = pl.pallas_call(
            body, name=name, grid=grid, in_specs=in_specs, out_specs=out_specs, out_shape=out_shape,
            scratch_shapes=list(scratch_shapes), compiler_params=_params(*semantics))(*args)
        return outs, []
    n_in, n_out, n_scr = len(in_specs), len(out_specs), len(scratch_shapes)

    total = math.prod(grid)
    first, middle, last = 0, (3 * total) // 4 - 1, total - 1
    assert first <= middle < last

    def at(step):
        lin = pl.program_id(0)
        for d in range(1, len(grid)):
            lin = lin * grid[d] + pl.program_id(d)
        return lin == step

    def hosted(*refs):
        h_in, c_in = refs[:n_in], refs[n_in:n_in + comm.n_in]
        rest = refs[n_in + comm.n_in:]
        h_out, c_out = rest[:n_out], rest[n_out:n_out + comm.n_out]
        rest = rest[n_out + comm.n_out:]
        h_scr, c_scr = rest[:n_scr], rest[n_scr:]

        @pl.when(at(first))
        def _():
            comm.start(c_in, c_out, c_scr)

        body(*h_in, *h_out, *h_scr)

        @pl.when(at(middle))
        def _():
            comm.mid(c_in, c_out, c_scr)

        @pl.when(at(last))
        def _():
            comm.end(c_in, c_out, c_scr)

    any_spec = pl.BlockSpec(memory_space=pl.ANY)
    outs = pl.pallas_call(
        hosted, name=name, grid=grid, in_specs=list(in_specs) + [any_spec] * comm.n_in,
        out_specs=list(out_specs) + [any_spec] * comm.n_out, out_shape=list(out_shape) + comm.out_shapes,
        scratch_shapes=list(scratch_shapes) + comm.scratch,
        compiler_params=_params(*(["arbitrary"] * len(grid))))(*args, *comm.inputs)
    return outs[:n_out], outs[n_out:]


def sum_swap(pieces, name):
    nl = len(pieces)
    _, r2, cc = pieces[0].shape
    tr = 128 if r2 % 128 == 0 else r2 // 2
    n = r2 // tr

    def body(*refs):
        p_refs, out = refs[:nl], refs[nl]
        slots, send_sems, local_sems, recv_sem = refs[nl + 1:]
        x, y, c = _me()
        sibling = (x, y, 1 - c)
        l, i = pl.program_id(0), pl.program_id(1)
        step = l * n + i

        def rows(st, core):
            return out.at[st // n, pl.ds(core * r2 + (st % n) * tr, tr), :]

        def copies(st):
            slot = st % 2
            local = pltpu.make_async_copy(slots.at[slot], rows(st, c), local_sems.at[slot])
            remote = pltpu.make_async_remote_copy(
                src_ref=slots.at[slot], dst_ref=rows(st, c), send_sem=send_sems.at[slot], recv_sem=recv_sem,
                device_id=sibling, device_id_type=MESH)
            return local, remote

        for ll in range(nl):
            @pl.when(l == ll)
            def _():
                acc = p_refs[ll][0].astype(F32)
                for d in range(1, N_DEV):
                    acc = acc + p_refs[ll][d].astype(F32)
                slots[step % 2] = acc

        for cp in copies(step):
            cp.start()

        @pl.when(step >= 1)
        def _():
            local, remote = copies(step - 1)
            local.wait()
            remote.wait_send()

        @pl.when(step == nl * n - 1)
        def _():
            local, remote = copies(step)
            local.wait()
            remote.wait_send()
            theirs = out.at[:, pl.ds((1 - c) * r2, r2), :]
            pltpu.make_async_remote_copy(src_ref=theirs, dst_ref=theirs, send_sem=send_sems.at[0],
                                         recv_sem=recv_sem, device_id=sibling, device_id_type=MESH).wait_recv()

    def piece_spec(ll):
        def index(l, i):
            return (0, jnp.where(l == ll, i, jnp.where(l < ll, 0, n - 1)), 0)
        return pl.BlockSpec((N_DEV, tr, cc), index)

    return pl.pallas_call(
        body, name=name, grid=(nl, n),
        in_specs=[piece_spec(ll) for ll in range(nl)],
        out_specs=pl.BlockSpec(memory_space=pl.ANY),
        out_shape=jax.ShapeDtypeStruct((nl, 2 * r2, cc), F32),
        scratch_shapes=[pltpu.VMEM((2, tr, cc), F32), pltpu.SemaphoreType.DMA((2,)), pltpu.SemaphoreType.DMA((2,)),
                        pltpu.SemaphoreType.DMA(())],
        compiler_params=_params("arbitrary", "arbitrary"),
    )(*pieces)


def sum_pieces(pieces, name):
    _, R, C = pieces.shape
    tr = _tile(R, 128) if R % 128 == 0 else R

    def body(p_ref, o_ref):
        acc = p_ref[0].astype(F32)
        for d in range(1, N_DEV):
            acc = acc + p_ref[d].astype(F32)
        o_ref[...] = acc

    return pl.pallas_call(
        body, name=name, grid=(R // tr,),
        in_specs=[pl.BlockSpec((N_DEV, tr, C), lambda i: (0, i, 0))],
        out_specs=pl.BlockSpec((tr, C), lambda i: (i, 0)),
        out_shape=jax.ShapeDtypeStruct((R, C), F32),
        compiler_params=_params("parallel"),
    )(pieces)


def adamw(w, g, m, v, name):
    Lyr, R, C = w.shape
    tr = _tile(R, 256) if R % 8 == 0 else R
    c1 = 1.0 / (1.0 - ADAM_B1 ** ADAM_STEP)
    c2 = 1.0 / (1.0 - ADAM_B2 ** ADAM_STEP)

    def body(w_ref, g_ref, m_ref, v_ref, d_ref, nm_ref, nv_ref):
        gv = g_ref[...]
        nm = ADAM_B1 * m_ref[...] + (1.0 - ADAM_B1) * gv
        nv = ADAM_B2 * v_ref[...] + (1.0 - ADAM_B2) * (gv * gv)
        nm_ref[...] = nm
        nv_ref[...] = nv
        d_ref[...] = -ADAM_LR * ((nm * c1) / (jnp.sqrt(nv * c2) + ADAM_EPS) + ADAM_WD * w_ref[...])

    spec = pl.BlockSpec((None, tr, C), lambda l, i: (l, i, 0))
    shp = jax.ShapeDtypeStruct(w.shape, F32)
    return pl.pallas_call(
        body, name=name, grid=(Lyr, R // tr),
        in_specs=[spec] * 4, out_specs=[spec] * 3, out_shape=[shp] * 3,
        compiler_params=_params("parallel", "parallel"),
    )(w, g, m, v)


def _rope_tables(T):
    pos = jnp.arange(T, dtype=F32)
    inv_freq = THETA ** (-jnp.arange(0, ROT, 2, dtype=F32) / ROT)
    ang = pos[:, None] * inv_freq[None, :]
    cs = jnp.concatenate([jnp.cos(ang), jnp.sin(ang)], axis=1)
    half = ROT // 2
    lane = jnp.arange(3 * LANES)
    table, lm = lane // LANES, lane % HD
    src = jnp.where(table == 0, lm % half, half + lm % half)
    i32 = lambda b: b.astype(jnp.int32)
    sign = jnp.where(table == 0, i32(lm < ROT), jnp.where(table == 1, -i32(lm < half), i32((lm >= half) & (lm < ROT))))
    place = (jnp.arange(ROT)[:, None] == src[None, :]) * sign[None, :].astype(F32)
    ones = ((table == 0) & (lm >= ROT)).astype(F32)
    return jnp.dot(cs, place, precision=lax.Precision.HIGHEST) + ones[None, :]


def _tab_specs(rows):
    return [pl.BlockSpec((rows, LANES), lambda i, k=k: (i, k)) for k in range(3)]


def kernel(x, attn_norm, attn_w_qkv, attn_w_o, attn_sink, conv_norm, conv_w_pw1, conv_b_pw1, conv_w_dw, conv_b_dw, conv_ln_g, conv_ln_b, conv_w_pw2, conv_b_pw2, ffn_norm, ffn_w_gu, ffn_w_down, final_norm, loss_target, m_attn_norm, m_attn_w_qkv, m_attn_w_o, m_attn_sink, m_conv_norm, m_conv_w_pw1, m_conv_b_pw1, m_conv_w_dw, m_conv_b_dw, m_conv_ln_g, m_conv_ln_b, m_conv_w_pw2, m_conv_b_pw2, m_ffn_norm, m_ffn_w_gu, m_ffn_w_down, m_final_norm, v_attn_norm, v_attn_w_qkv, v_attn_w_o, v_attn_sink, v_conv_norm, v_conv_w_pw1, v_conv_b_pw1, v_conv_w_dw, v_conv_b_dw, v_conv_ln_g, v_conv_ln_b, v_conv_w_pw2, v_conv_b_pw2, v_ffn_norm, v_ffn_w_gu, v_ffn_w_down, v_final_norm):
    T = x.shape[1]
    x0 = x[0]
    target = loss_target[0]
    ix, iy = lax.axis_index("x"), lax.axis_index("y")
    chip = 2 * ix + iy
    rc = rs1 = rs2 = _rope_tables(T)

    bf = lambda t: t.astype(BF16)
    col_row = [False, True]

    def place(vec, width):
        return lax.dynamic_update_slice(jnp.zeros((vec.shape[0], N_CHIPS * width), F32), vec, (0, chip * width))

    small_rows = jnp.concatenate([
        place(conv_norm, 256), place(conv_b_pw1, 512).reshape(2, D), place(conv_b_dw, 256), place(conv_ln_g, 256),
        place(conv_ln_b, 256), place(conv_b_pw2, 256), jnp.zeros((1, D), F32),
        place(conv_w_dw[0], 256), jnp.zeros((1, D), F32)], axis=0)
    w_qkv, w_o, got = exchange(_Both(_Gather([bf(attn_w_qkv[0]), bf(attn_w_o[0])], col_row),
                                     _Scatter([], small_rows)), "gather_attn")
    psmall = sum_pieces(got, "sum_small_params") * 0.5
    p_conv_norm, p_b_pw1 = psmall[0:1], psmall[1:3].reshape(1, 2 * D)
    p_b_dw, p_ln_g, p_ln_b, p_b_pw2 = psmall[3:4], psmall[4:5], psmall[5:6], psmall[6:7]
    p_w_dw = psmall[8:40]

    h0, qkv = rms_qkv(x0, attn_norm, w_qkv, rc, rs1, rs2)
    sink = attn_sink[0]
    o, (w_gu0, w_down0) = attn_fwd(qkv, sink, comm=_Gather([bf(ffn_w_gu[0]), bf(ffn_w_down[0])], col_row))
    zero_b = jnp.zeros((1, D), F32)
    x1 = mm_res(o, w_o, x0, zero_b, "attn_out")
    zero_gu = jnp.zeros((1, 2 * DFF), F32)
    h1, gu0, act0, (w_pw1, w_pw2) = rms_mm_gate(
        x1, ffn_norm[0:1], w_gu0, zero_gu, DFF, True, BF16, "ffn0_up",
        comm=_Gather([bf(conv_w_pw1[0]), bf(conv_w_pw2[0])], col_row))
    x2 = mm_res(act0, w_down0, x1, zero_b, "ffn0_down")
    h2, pre, glu, _ = rms_mm_gate(x2, p_conv_norm, w_pw1, p_b_pw1, D, False, F32, "conv_pw1")
    dwc, sw, (w_gu1, w_down1) = conv_fwd(glu, p_w_dw, p_b_dw, p_ln_g, p_ln_b,
                                         comm=_Gather([bf(ffn_w_gu[1]), bf(ffn_w_down[1])], col_row))
    x3 = mm_res(sw, w_pw2, x2, p_b_pw2, "conv_pw2")
    h3, gu1, act1, _ = rms_mm_gate(x3, ffn_norm[1:2], w_gu1, zero_gu, DFF, True, BF16, "ffn1_up")
    x4 = mm_res(act1, w_down1, x3, zero_b, "ffn1_down")
    dx4, loss_part, d_final = final_loss(x4, final_norm.reshape(1, D), target)
    loss = lax.psum(loss_part[0, 0], ("x", "y", "c"))

    dgu1, _ = swiglu_bwd(dx4, w_down1, gu1, "ffn1_down_bwd")
    g_down1 = dw_row(act1, dx4, "ffn1_down_dw")
    dx3, d_ffn1, _ = mm_bt_rmsbwd(dgu1, w_gu1, x3, ffn_norm[1:2], dx4, "ffn1_up_bwd")
    g_gu1 = dw_col(h3, dgu1, "ffn1_up_dw")

    ddwc, d_ln_g, d_ln_b, d_b_pw2 = ln_silu_bwd(dx3, w_pw2, dwc, p_ln_g, p_ln_b)
    g_pw2 = dw_row(sw, dx3, "conv_pw2_dw")
    dpre, d_w_dw, d_b_dw, d_b_pw1, (r_gu1, r_down1) = conv_bwd(ddwc, glu, pre, p_w_dw,
                                                               comm=_Scatter([g_gu1, g_down1]))
    dx2, d_conv_norm, _ = mm_bt_rmsbwd(dpre, w_pw1, x2, p_conv_norm, dx3, "conv_pw1_bwd")
    g_pw1 = dw_col(h2, dpre, "conv_pw1_dw")

    dgu0, (r_pw1, r_pw2) = swiglu_bwd(dx2, w_down0, gu0, "ffn0_down_bwd", comm=_Scatter([g_pw1, g_pw2]))
    g_down0 = dw_row(act0, dx2, "ffn0_down_dw")
    dx1, d_ffn0, _ = mm_bt_rmsbwd(dgu0, w_gu0, x1, ffn_norm[0:1], dx2, "ffn0_up_bwd")
    g_gu0 = dw_col(h1, dgu0, "ffn0_up_dw")

    do = mm_bt(dx1, w_o, "attn_out_bwd")
    g_o = dw_row(o, dx1, "attn_out_dw")
    dq, dkc, dvc, d_sink, (r_gu0, r_down0, r_o) = attn_bwd(qkv, o, do, sink, rc, rs1, rs2,
                                                           comm=_Scatter([g_gu0, g_down0, g_o]))
    dqkv = kv_sum(dq, dkc, dvc, rc, rs1, rs2)[None]
    g_qkv = dw_col(h0, dqkv, "attn_qkv_dw")
    dx0, d_attn_norm, (r_qkv,) = mm_bt_rmsbwd(dqkv, w_qkv, x0, attn_norm, dx1, "attn_qkv_bwd",
                                              comm=_Scatter([g_qkv]))

    pad16 = lambda t: jnp.concatenate([t, jnp.zeros((1, D - t.shape[1]), F32)], axis=1)
    small_g = jnp.concatenate([
        d_attn_norm, pad16(d_sink), d_conv_norm, d_b_pw1.reshape(2, D), d_b_dw, d_ln_g, d_ln_b, d_b_pw2,
        d_ffn0, d_ffn1, d_final, jnp.zeros((4, D), F32), d_w_dw], axis=0)
    r_small, = exchange(_Scatter([], small_g), "scatter_small")
    gf_gu = sum_swap([r_gu0, r_gu1], "sum_gu")
    gf_down = sum_swap([r_down0, r_down1], "sum_down")
    gf_pw1, gf_pw2 = sum_swap([r_pw1], "sum_pw1"), sum_swap([r_pw2], "sum_pw2")
    gf_qkv, gf_o = sum_swap([r_qkv], "sum_qkv"), sum_swap([r_o], "sum_o")
    gs = sum_pieces(r_small, "sum_small_grads")

    def take(row0, nrows, width):
        return lax.dynamic_slice(gs, (row0, chip * width), (nrows, width))

    grads = {
        "attn_norm": gs[0:1], "attn_w_qkv": gf_qkv, "attn_w_o": gf_o, "attn_sink": gs[1:2, :N_HEADS],
        "conv_norm": take(2, 1, 256), "conv_w_pw1": gf_pw1,
        "conv_b_pw1": lax.dynamic_slice(gs[3:5].reshape(1, 2 * D), (0, chip * 512), (1, 512)),
        "conv_w_dw": take(16, 32, 256)[None, :CONV_W], "conv_b_dw": take(5, 1, 256), "conv_ln_g": take(6, 1, 256),
        "conv_ln_b": take(7, 1, 256), "conv_w_pw2": gf_pw2, "conv_b_pw2": take(8, 1, 256),
        "ffn_norm": gs[9:11], "ffn_w_gu": gf_gu, "ffn_w_down": gf_down, "final_norm": gs[11],
    }
    weights = dict(attn_norm=attn_norm, attn_w_qkv=attn_w_qkv, attn_w_o=attn_w_o, attn_sink=attn_sink,
                   conv_norm=conv_norm, conv_w_pw1=conv_w_pw1, conv_b_pw1=conv_b_pw1, conv_w_dw=conv_w_dw,
                   conv_b_dw=conv_b_dw, conv_ln_g=conv_ln_g, conv_ln_b=conv_ln_b, conv_w_pw2=conv_w_pw2,
                   conv_b_pw2=conv_b_pw2, ffn_norm=ffn_norm, ffn_w_gu=ffn_w_gu, ffn_w_down=ffn_w_down,
                   final_norm=final_norm)
    m_in = dict(attn_norm=m_attn_norm, attn_w_qkv=m_attn_w_qkv, attn_w_o=m_attn_w_o, attn_sink=m_attn_sink,
                conv_norm=m_conv_norm, conv_w_pw1=m_conv_w_pw1, conv_b_pw1=m_conv_b_pw1, conv_w_dw=m_conv_w_dw,
                conv_b_dw=m_conv_b_dw, conv_ln_g=m_conv_ln_g, conv_ln_b=m_conv_ln_b, conv_w_pw2=m_conv_w_pw2,
                conv_b_pw2=m_conv_b_pw2, ffn_norm=m_ffn_norm, ffn_w_gu=m_ffn_w_gu, ffn_w_down=m_ffn_w_down,
                final_norm=m_final_norm)
    v_in = dict(attn_norm=v_attn_norm, attn_w_qkv=v_attn_w_qkv, attn_w_o=v_attn_w_o, attn_sink=v_attn_sink,
                conv_norm=v_conv_norm, conv_w_pw1=v_conv_w_pw1, conv_b_pw1=v_conv_b_pw1, conv_w_dw=v_conv_w_dw,
                conv_b_dw=v_conv_b_dw, conv_ln_g=v_conv_ln_g, conv_ln_b=v_conv_ln_b, conv_w_pw2=v_conv_w_pw2,
                conv_b_pw2=v_conv_b_pw2, ffn_norm=v_ffn_norm, ffn_w_gu=v_ffn_w_gu, ffn_w_down=v_ffn_w_down,
                final_norm=v_final_norm)
    order = list(weights)
    g_out, d_out, m_out, v_out = [], [], [], []
    for nm in order:
        w = weights[nm]
        shape = w.shape
        as3 = lambda t: t.reshape((1,) * (3 - len(shape)) + shape) if len(shape) < 3 else t.reshape(shape)
        g3 = as3(grads[nm].reshape(shape))
        delta, nm_, nv_ = adamw(as3(w), g3, as3(m_in[nm]), as3(v_in[nm]), "adamw_" + nm)
        g_out.append(g3.reshape(shape))
        d_out.append(delta.reshape(shape))
        m_out.append(nm_.reshape(shape))
        v_out.append(nv_.reshape(shape))
    return (loss, dx0[None], *g_out, *d_out, *m_out, *v_out)
```

```python
import functools
import math

import jax
import jax.numpy as jnp
from jax import lax
from jax.experimental import pallas as pl
from jax.experimental.pallas import tpu as pltpu

F32 = jnp.float32
BF16 = jnp.bfloat16

D = 1024
N_HEADS = 16
N_KV = 4
GROUP = N_HEADS // N_KV
HD = 64
ROT = 16
THETA = 500000.0
BLK = 128
QKV = (N_HEADS + 2 * N_KV) * HD
KV_OFF = N_HEADS * HD
DFF = 2816
CONV_W = 31
CONV_PAD = 15
HALO = 16
CONV_JB = 4
EPS = 1e-6
NEG = -1e30
N_CHIPS = 4
N_DEV = 8
LANES = 128
SUBLANES = 8

ADAM_LR, ADAM_B1, ADAM_B2, ADAM_EPS, ADAM_WD, ADAM_STEP = 0.001, 0.9, 0.999, 1e-08, 0.01, 10

VMEM_LIMIT = 56 * 1024 * 1024
MESH = pl.DeviceIdType.MESH


def _params(*sem):
    return pltpu.CompilerParams(dimension_semantics=sem, vmem_limit_bytes=VMEM_LIMIT)


def _tile(n, want):
    if n <= want:
        return n
    for t in range(want, 7, -1):
        if n % t == 0 and t % 8 == 0:
            return t
    return n


MXU_COLS = 256


def _col_chunks(n):
    return [slice(c, min(c + MXU_COLS, n)) for c in range(0, n, MXU_COLS)]


def _sigmoid(v):
    return 1.0 / (1.0 + jnp.exp(-v))


def _rms_fwd(xv, gain):
    r = lax.rsqrt(jnp.mean(xv * xv, axis=-1, keepdims=True) + EPS)
    return xv * r * gain


def _rms_bwd(dh, xv, gain, dres):
    r = lax.rsqrt(jnp.mean(xv * xv, axis=-1, keepdims=True) + EPS)
    xhat = xv * r
    gy = dh * gain
    dx = r * (gy - xhat * jnp.mean(gy * xhat, axis=-1, keepdims=True))
    return dx + dres, dh * xhat


def _rope(blk, c, s1, s2):
    return blk * c + pltpu.roll(blk, LANES - ROT // 2, 1) * s1 + pltpu.roll(blk, ROT // 2, 1) * s2


def _dot(a, b):
    return jnp.dot(a, b, preferred_element_type=F32)


def _dot_tb(a, b):
    return lax.dot_general(a, b, (((1,), (1,)), ((), ())), preferred_element_type=F32)


def _dot_ta(a, b):
    return lax.dot_general(a, b, (((0,), (0,)), ((), ())), preferred_element_type=F32)


def rms_qkv(x, gain, w, rc, rs1, rs2):
    T = x.shape[0]
    tm = _tile(T, 512)

    def body(x_ref, g_ref, w_ref, c_ref, s1_ref, s2_ref, h_ref, qkv_ref):
        h = _rms_fwd(x_ref[...], g_ref[...]).astype(BF16)
        h_ref[...] = h
        acc = _dot(h, w_ref[...])
        c, s1, s2 = c_ref[...], s1_ref[...], s2_ref[...]
        n_rot = (KV_OFF + N_KV * HD) // LANES
        for j in range(n_rot):
            sl = slice(LANES * j, LANES * (j + 1))
            roped = _rope(acc[:, sl], c, s1, s2)
            if j < KV_OFF // LANES:
                roped = roped * Q_SCALE
            qkv_ref[:, sl] = roped.astype(BF16)
        qkv_ref[:, n_rot * LANES:] = acc[:, n_rot * LANES:].astype(BF16)

    row = lambda i: (i, 0)
    full = lambda i: (0, 0)
    return pl.pallas_call(
        body, name="rms_qkv", grid=(T // tm,),
        in_specs=[pl.BlockSpec((tm, D), row), pl.BlockSpec((1, D), full), pl.BlockSpec((D, QKV), full),
                  *_tab_specs(tm)],
        out_specs=[pl.BlockSpec((tm, D), row), pl.BlockSpec((tm, QKV), row)],
        out_shape=[jax.ShapeDtypeStruct((T, D), BF16), jax.ShapeDtypeStruct((T, QKV), BF16)],
        compiler_params=_params("parallel"),
    )(x, gain, w, rc, rs1, rs2)


Q_SCALE = 1.0 / math.sqrt(HD)


def _attn_mask(n, T):
    ci = lax.broadcasted_iota(jnp.int32, (3 * BLK, BLK), 0)
    qi = lax.broadcasted_iota(jnp.int32, (3 * BLK, BLK), 1)
    key_pos = n * BLK - BLK + ci
    return (jnp.abs(ci - BLK - qi) <= BLK) & (key_pos >= 0) & (key_pos < T)


def _kv_padded(kv, first_tile):
    low = lax.broadcasted_iota(jnp.int32, (3 * BLK, LANES), 1) < HD
    zero = jnp.zeros((3 * BLK, LANES), BF16)
    out = {}
    for g in range(N_KV):
        t = kv[:, (first_tile + g // 2) * LANES:(first_tile + g // 2 + 1) * LANES]
        swapped = jnp.concatenate([t[:, HD:], t[:, :HD]], axis=1)
        for p in range(2):
            out[g, p] = jnp.where(low if p == 0 else ~low, t if g % 2 == p else swapped, zero)
    return out


def _softmax_sink(s, valid, sk):
    s = jnp.where(valid, s, NEG)
    m = jnp.maximum(jnp.max(s, axis=0, keepdims=True), sk)
    e = jnp.exp(s - m)
    es = jnp.exp(sk - m)
    inv = 1.0 / (jnp.sum(e, axis=0, keepdims=True) + es)
    return e * inv, es * inv


def _attn_specs(T):
    nb = T // BLK
    kv_blk = 2 * N_KV * HD
    kv_col = KV_OFF // kv_blk
    q_spec = pl.BlockSpec((BLK, KV_OFF), lambda n: (n, 0))
    prev = pl.BlockSpec((BLK, kv_blk), lambda n: (jnp.maximum(n - 1, 0), kv_col))
    own = pl.BlockSpec((BLK, kv_blk), lambda n: (n, kv_col))
    nxt = pl.BlockSpec((BLK, kv_blk), lambda n: (jnp.minimum(n + 1, nb - 1), kv_col))
    return nb, q_spec, prev, own, nxt


def attn_fwd(qkv, sink, comm=None):
    T = qkv.shape[0]
    nb, q_spec, prev, own, nxt = _attn_specs(T)

    def body(sink_ref, q_ref, kp_ref, ko_ref, kn_ref, o_ref):
        valid = _attn_mask(pl.program_id(0), T)
        kv = jnp.concatenate([kp_ref[...], ko_ref[...], kn_ref[...]], axis=0)
        kx, vx = _kv_padded(kv, 0), _kv_padded(kv, 2)
        tile = lambda ref, h: ref[:, (h // 2) * LANES:(h // 2 + 1) * LANES]
        ss = [_dot_tb(kx[h // GROUP, h % 2], tile(q_ref, h)) for h in range(N_HEADS)]
        ps = [_softmax_sink(ss[h], valid, sink_ref[h])[0].astype(BF16) for h in range(N_HEADS)]
        vxt = {k: v.T for k, v in vx.items()}
        for j in range(N_HEADS // 2):
            g = 2 * j // GROUP
            o_t = _dot(vxt[g, 0], ps[2 * j]) + _dot(vxt[g, 1], ps[2 * j + 1])
            o_ref[:, j * LANES:(j + 1) * LANES] = o_t.T.astype(BF16)

    (o,), got = _call(
        body, name="attn_fwd", grid=(nb,),
        in_specs=[pl.BlockSpec(memory_space=pltpu.SMEM), q_spec, prev, own, nxt],
        out_specs=[pl.BlockSpec((BLK, D), lambda n: (n, 0))],
        out_shape=[jax.ShapeDtypeStruct((T, D), BF16)],
        semantics=("parallel",), args=(sink, qkv, qkv, qkv, qkv), comm=comm)
    return o, got


def mm_res(a, w, resid, bias, name):
    T, K = a.shape
    tm = _tile(T, 512)

    def body(a_ref, w_ref, r_ref, b_ref, o_ref):
        o_ref[...] = _dot(a_ref[...], w_ref[...]) + b_ref[...] + r_ref[...]

    row = lambda i: (i, 0)
    full = lambda i: (0, 0)
    return pl.pallas_call(
        body, name=name, grid=(T // tm,),
        in_specs=[pl.BlockSpec((tm, K), row), pl.BlockSpec((K, D), full), pl.BlockSpec((tm, D), row),
                  pl.BlockSpec((1, D), full)],
        out_specs=pl.BlockSpec((tm, D), row),
        out_shape=jax.ShapeDtypeStruct((T, D), F32),
        compiler_params=_params("parallel"),
    )(a, w, resid, bias)


def rms_mm_gate(x, gain, w, bias, H, swiglu, act_dtype, name, comm=None):
    T = x.shape[0]
    tm = _tile(T, 512)
    tn = 1408 if H % 1408 == 0 else H
    nj = H // tn
    hw = D // nj

    def body(x_ref, g_ref, w1_ref, w2_ref, b1_ref, b2_ref, h_ref, pre_ref, act_ref):
        h = _rms_fwd(x_ref[...], g_ref[...]).astype(BF16)
        for jj in range(nj):
            @pl.when(pl.program_id(0) == jj)
            def _():
                h_ref[...] = h[:, jj * hw:(jj + 1) * hw]

        for cs in _col_chunks(tn):
            a = _dot(h, w1_ref[:, cs]) + b1_ref[:, cs]
            b = _dot(h, w2_ref[:, cs]) + b2_ref[:, cs]
            pre_ref[0, :, cs] = a.astype(BF16)
            pre_ref[1, :, cs] = b.astype(BF16)
            if swiglu:
                act = a * _sigmoid(a) * b
            else:
                act = a * _sigmoid(b)
            act_ref[:, cs] = act.astype(act_dtype)

    (h, pre, act), got = _call(
        body, name=name, grid=(nj, T // tm),
        in_specs=[pl.BlockSpec((tm, D), lambda j, i: (i, 0)), pl.BlockSpec((1, D), lambda j, i: (0, 0)),
                  pl.BlockSpec((D, tn), lambda j, i: (0, j)), pl.BlockSpec((D, tn), lambda j, i: (0, nj + j)),
                  pl.BlockSpec((1, tn), lambda j, i: (0, j)), pl.BlockSpec((1, tn), lambda j, i: (0, nj + j))],
        out_specs=[pl.BlockSpec((tm, hw), lambda j, i: (i, j)), pl.BlockSpec((2, tm, tn), lambda j, i: (0, i, j)),
                   pl.BlockSpec((tm, tn), lambda j, i: (i, j))],
        out_shape=[jax.ShapeDtypeStruct((T, D), BF16), jax.ShapeDtypeStruct((2, T, H), BF16),
                   jax.ShapeDtypeStruct((T, H), act_dtype)],
        semantics=("parallel", "parallel"), args=(x, gain, w, w, bias, bias), comm=comm)
    return h, pre, act, got


def _conv_tiles(T):
    tt = _tile(T, 512)
    return tt, tt // SUBLANES, D // LANES


def _fill_strided(ext, p, L):
    def ibody(i, carry):
        ext[i] = p[pl.ds(i + 1, SUBLANES, stride=L), :]
        return carry

    lax.fori_loop(0, L + CONV_W - 1, ibody, 0, unroll=2)


def _conv_specs(T, tt):
    main = pl.BlockSpec((tt, D), lambda i: (i, 0))
    per = tt // HALO
    prev = pl.BlockSpec((HALO, D), lambda i: (jnp.maximum(i * per - 1, 0), 0))
    nxt = pl.BlockSpec((HALO, D), lambda i: (jnp.minimum((i + 1) * per, T // HALO - 1), 0))
    return main, prev, nxt


def _fill_pad(pad, main_ref, prev_ref, next_ref, i, n_i, tt, nlt):
    keep_p = (i > 0).astype(F32)
    keep_n = (i < n_i - 1).astype(F32)
    for lt in range(nlt):
        sl = slice(lt * LANES, (lt + 1) * LANES)
        pad[lt, 0:HALO, :] = prev_ref[:, sl] * keep_p
        pad[lt, HALO:HALO + tt, :] = main_ref[:, sl]
        pad[lt, HALO + tt:2 * HALO + tt, :] = next_ref[:, sl] * keep_n


def conv_fwd(glu, w_dw, b_dw, ln_g, ln_b, comm=None):
    T = glu.shape[0]
    tt, L, nlt = _conv_tiles(T)
    n_i = T // tt
    main, prev, nxt = _conv_specs(T, tt)

    def body(x_ref, xp_ref, xn_ref, w_ref, b_ref, g_ref, bb_ref, dwc_ref, sw_ref, pad, ob, ext, wk):
        i = pl.program_id(0)
        _fill_pad(pad, x_ref, xp_ref, xn_ref, i, n_i, tt, nlt)
        for lt in range(nlt):
            sl = slice(lt * LANES, (lt + 1) * LANES)
            o = ob.at[lt]
            _fill_strided(ext, pad.at[lt], L)
            for k in range(CONV_W):
                wk[k] = jnp.broadcast_to(w_ref[k:k + 1, sl], (SUBLANES, LANES))

            def jbody(jb, carry):
                j = jb * CONV_JB
                accs = [None] * CONV_JB
                for m in range(CONV_W + CONV_JB - 1):
                    e = ext[j + m]
                    for u in range(CONV_JB):
                        if 0 <= m - u < CONV_W:
                            t = e * wk[m - u]
                            accs[u] = t if accs[u] is None else accs[u] + t
                for u in range(CONV_JB):
                    o[pl.ds(j + u, SUBLANES, stride=L), :] = accs[u]
                return carry

            lax.fori_loop(0, L // CONV_JB, jbody, 0)
        y = jnp.concatenate([ob[lt] for lt in range(nlt)], axis=1) + b_ref[...]
        dwc_ref[...] = y
        mu = jnp.mean(y, axis=-1, keepdims=True)
        yc = y - mu
        var = jnp.mean(yc * yc, axis=-1, keepdims=True)
        z = yc * lax.rsqrt(var + EPS) * g_ref[...] + bb_ref[...]
        sw_ref[...] = (z * _sigmoid(z)).astype(BF16)

    full = lambda i: (0, 0)
    (dwc, sw), got = _call(
        body, name="conv_fwd", grid=(n_i,),
        in_specs=[main, prev, nxt, pl.BlockSpec((32, D), full), pl.BlockSpec((1, D), full),
                  pl.BlockSpec((1, D), full), pl.BlockSpec((1, D), full)],
        out_specs=[pl.BlockSpec((tt, D), lambda i: (i, 0)), pl.BlockSpec((tt, D), lambda i: (i, 0))],
        out_shape=[jax.ShapeDtypeStruct((T, D), F32), jax.ShapeDtypeStruct((T, D), BF16)],
        scratch_shapes=[pltpu.VMEM((nlt, tt + 2 * HALO, LANES), F32), pltpu.VMEM((nlt, tt, LANES), F32),
                        pltpu.VMEM((L + 2 * HALO, SUBLANES, LANES), F32), pltpu.VMEM((32, SUBLANES, LANES), F32)],
        semantics=("parallel",), args=(glu, glu, glu, w_dw, b_dw, ln_g, ln_b), comm=comm)
    return dwc, sw, got


def final_loss(x, gain, target):
    T = x.shape[0]
    tm = _tile(T, 512)

    def body(x_ref, g_ref, t_ref, dx_ref, loss_ref, dg_ref):
        @pl.when(pl.program_id(0) == 0)
        def _():
            loss_ref[...] = jnp.zeros_like(loss_ref)
            dg_ref[...] = jnp.zeros_like(dg_ref)

        xv, gain_v = x_ref[...], g_ref[...]
        err = _rms_fwd(xv, gain_v) - t_ref[...]
        part = 0.5 * jnp.sum(jnp.mean(err * err, axis=-1, keepdims=True), axis=0, keepdims=True)
        loss_ref[...] += jnp.broadcast_to(part, loss_ref.shape)
        dx, dgr = _rms_bwd(err * (1.0 / D), xv, gain_v, 0.0)
        dx_ref[...] = dx
        dg_ref[...] += jnp.sum(dgr, axis=0, keepdims=True)

    row = lambda i: (i, 0)
    full = lambda i: (0, 0)
    return pl.pallas_call(
        body, name="final_loss", grid=(T // tm,),
        in_specs=[pl.BlockSpec((tm, D), row), pl.BlockSpec((1, D), full), pl.BlockSpec((tm, D), row)],
        out_specs=[pl.BlockSpec((tm, D), row), pl.BlockSpec((1, LANES), full), pl.BlockSpec((1, D), full)],
        out_shape=[jax.ShapeDtypeStruct((T, D), F32), jax.ShapeDtypeStruct((1, LANES), F32),
                   jax.ShapeDtypeStruct((1, D), F32)],
        compiler_params=_params("arbitrary"),
    )(x, gain, target)


def swiglu_bwd(dx, w_down, pre, name, comm=None):
    T = dx.shape[0]
    H = w_down.shape[0]
    tm = _tile(T, 512)
    tn = 1408
    nj = H // tn

    def body(dx_ref, w_ref, pre_ref, dpre_ref):
        dxb = dx_ref[...].astype(BF16)
        for cs in _col_chunks(tn):
            dact = _dot_tb(dxb, w_ref[cs, :])
            g = pre_ref[0, :, cs].astype(F32)
            u = pre_ref[1, :, cs].astype(F32)
            sg = _sigmoid(g)
            dpre_ref[0, :, cs] = (dact * u * sg * (1.0 + g * (1.0 - sg))).astype(BF16)
            dpre_ref[1, :, cs] = (dact * g * sg).astype(BF16)

    (dpre,), got = _call(
        body, name=name, grid=(nj, T // tm),
        in_specs=[pl.BlockSpec((tm, D), lambda j, i: (i, 0)), pl.BlockSpec((tn, D), lambda j, i: (j, 0)),
                  pl.BlockSpec((2, tm, tn), lambda j, i: (0, i, j))],
        out_specs=[pl.BlockSpec((2, tm, tn), lambda j, i: (0, i, j))],
        out_shape=[jax.ShapeDtypeStruct((2, T, H), BF16)],
        semantics=("parallel", "parallel"), args=(dx, w_down, pre), comm=comm)
    return dpre, got


def mm_bt_rmsbwd(dpre, w, x, gain, dres, name, comm=None):
    nh, T, H = dpre.shape
    tm = _tile(T, 512)
    tk = 1408 if H % 1408 == 0 else (1024 if H % 1024 == 0 else H)
    nk = H // tk

    def body(dp_ref, w_ref, x_ref, g_ref, dres_ref, dx_ref, dg_ref, acc):
        i, hf, kk = pl.program_id(0), pl.program_id(1), pl.program_id(2)

        @pl.when((i == 0) & (hf == 0) & (kk == 0))
        def _():
            dg_ref[...] = jnp.zeros_like(dg_ref)

        @pl.when((hf == 0) & (kk == 0))
        def _():
            acc[...] = jnp.zeros_like(acc)

        cols = pl.ds(pl.multiple_of((hf * nk + kk) * tk, LANES), tk)
        acc[...] += _dot_tb(dp_ref[...], w_ref[:, cols])

        @pl.when((hf == nh - 1) & (kk == nk - 1))
        def _():
            dx, dgr = _rms_bwd(acc[...], x_ref[...], g_ref[...], dres_ref[...])
            dx_ref[...] = dx
            dg_ref[...] += jnp.sum(dgr, axis=0, keepdims=True)

    (dx, dg), got = _call(
        body, name=name, grid=(T // tm, nh, nk),
        in_specs=[pl.BlockSpec((None, tm, tk), lambda i, hf, kk: (hf, i, kk)),
                  pl.BlockSpec((D, nh * H), lambda i, hf, kk: (0, 0), pipeline_mode=pl.Buffered(1)),
                  pl.BlockSpec((tm, D), lambda i, hf, kk: (i, 0)), pl.BlockSpec((1, D), lambda i, hf, kk: (0, 0)),
                  pl.BlockSpec((tm, D), lambda i, hf, kk: (i, 0))],
        out_specs=[pl.BlockSpec((tm, D), lambda i, hf, kk: (i, 0)), pl.BlockSpec((1, D), lambda i, hf, kk: (0, 0))],
        out_shape=[jax.ShapeDtypeStruct((T, D), F32), jax.ShapeDtypeStruct((1, D), F32)],
        scratch_shapes=[pltpu.VMEM((tm, D), F32)],
        semantics=("arbitrary", "arbitrary", "arbitrary"), args=(dpre, w, x, gain, dres), comm=comm)
    return dx, dg, got


def dw_col(a, dpre, name):
    T = a.shape[0]
    nh, _, H = dpre.shape
    per = nh * H // N_CHIPS
    bph = N_CHIPS // nh
    tt = _tile(T, 1024)
    nt = T // tt

    def body(a_ref, b_ref, o_ref, acc):
        t = pl.program_id(1)

        @pl.when(t == 0)
        def _():
            acc[...] = jnp.zeros_like(acc)

        acc[...] += _dot_ta(a_ref[...], b_ref[...])

        @pl.when(t == nt - 1)
        def _():
            o_ref[...] = acc[...].astype(BF16)

    return pl.pallas_call(
        body, name=name, grid=(N_CHIPS, nt),
        in_specs=[pl.BlockSpec((tt, D), lambda q, t: (t, 0)),
                  pl.BlockSpec((None, tt, per), lambda q, t: (q // bph, t, q % bph))],
        out_specs=pl.BlockSpec((None, D, per), lambda q, t: (q, 0, 0)),
        out_shape=jax.ShapeDtypeStruct((N_CHIPS, D, per), BF16),
        scratch_shapes=[pltpu.VMEM((D, per), F32)],
        compiler_params=_params("parallel", "arbitrary"),
    )(a, dpre)


def dw_row(a, b, name):
    T, R = a.shape
    cw = 1408 if R % 1408 == 0 else 512
    tt = _tile(T, 1024)
    nt = T // tt

    def body(a_ref, b_ref, o_ref, acc):
        t = pl.program_id(1)

        @pl.when(t == 0)
        def _():
            acc[...] = jnp.zeros_like(acc)

        acc[...] += _dot_ta(a_ref[...], b_ref[...].astype(BF16))

        @pl.when(t == nt - 1)
        def _():
            o_ref[...] = acc[...].astype(BF16)

    out = pl.pallas_call(
        body, name=name, grid=(R // cw, nt),
        in_specs=[pl.BlockSpec((tt, cw), lambda q, t: (t, q)), pl.BlockSpec((tt, D), lambda q, t: (t, 0))],
        out_specs=pl.BlockSpec((cw, D), lambda q, t: (q, 0)),
        out_shape=jax.ShapeDtypeStruct((R, D), BF16),
        scratch_shapes=[pltpu.VMEM((cw, D), F32)],
        compiler_params=_params("parallel", "arbitrary"),
    )(a, b)
    return out.reshape(N_CHIPS, R // N_CHIPS, D)


def ln_silu_bwd(dx, w_pw2, dwc, ln_g, ln_b):
    T = dx.shape[0]
    tm = _tile(T, 512)

    def body(dx_ref, w_ref, y_ref, g_ref, b_ref, dy_ref, dg_ref, db_ref, dbo_ref):
        @pl.when(pl.program_id(0) == 0)
        def _():
            dg_ref[...] = jnp.zeros_like(dg_ref)
            db_ref[...] = jnp.zeros_like(db_ref)
            dbo_ref[...] = jnp.zeros_like(dbo_ref)

        dxv = dx_ref[...]
        dsw = _dot_tb(dxv.astype(BF16), w_ref[...])
        y = y_ref[...]
        mu = jnp.mean(y, axis=-1, keepdims=True)
        yc = y - mu
        rstd = lax.rsqrt(jnp.mean(yc * yc, axis=-1, keepdims=True) + EPS)
        xhat = yc * rstd
        z = xhat * g_ref[...] + b_ref[...]
        sg = _sigmoid(z)
        dz = dsw * sg * (1.0 + z * (1.0 - sg))
        dxh = dz * g_ref[...]
        dy_ref[...] = rstd * (dxh - jnp.mean(dxh, axis=-1, keepdims=True)
                              - xhat * jnp.mean(dxh * xhat, axis=-1, keepdims=True))
        dg_ref[...] += jnp.sum(dz * xhat, axis=0, keepdims=True)
        db_ref[...] += jnp.sum(dz, axis=0, keepdims=True)
        dbo_ref[...] += jnp.sum(dxv, axis=0, keepdims=True)

    row = lambda i: (i, 0)
    full = lambda i: (0, 0)
    vec = pl.BlockSpec((1, D), full)
    return pl.pallas_call(
        body, name="ln_silu_bwd", grid=(T // tm,),
        in_specs=[pl.BlockSpec((tm, D), row), pl.BlockSpec((D, D), full), pl.BlockSpec((tm, D), row), vec, vec],
        out_specs=[pl.BlockSpec((tm, D), row), vec, vec, vec],
        out_shape=[jax.ShapeDtypeStruct((T, D), F32)] + [jax.ShapeDtypeStruct((1, D), F32)] * 3,
        compiler_params=_params("arbitrary"),
    )(dx, w_pw2, dwc, ln_g, ln_b)


def conv_bwd(ddwc, glu, pre, w_dw, comm=None):
    T = ddwc.shape[0]
    tt, L, nlt = _conv_tiles(T)
    n_i = T // tt
    main, prev, nxt = _conv_specs(T, tt)

    def body(d_ref, dp_ref, dn_ref, x_ref, xp_ref, xn_ref, pre_ref, w_ref,
             dpre_ref, dw_ref, dbd_ref, dbp_ref, padd, padx, ob, extd, extx, wk):
        i = pl.program_id(0)

        @pl.when(i == 0)
        def _():
            dw_ref[...] = jnp.zeros_like(dw_ref)
            dbd_ref[...] = jnp.zeros_like(dbd_ref)
            dbp_ref[...] = jnp.zeros_like(dbp_ref)

        _fill_pad(padd, d_ref, dp_ref, dn_ref, i, n_i, tt, nlt)
        _fill_pad(padx, x_ref, xp_ref, xn_ref, i, n_i, tt, nlt)
        for lt in range(nlt):
            sl = slice(lt * LANES, (lt + 1) * LANES)
            o = ob.at[lt]
            _fill_strided(extd, padd.at[lt], L)
            _fill_strided(extx, padx.at[lt], L)
            for k in range(CONV_W):
                wk[k] = jnp.broadcast_to(w_ref[k:k + 1, sl], (SUBLANES, LANES))

            def jbody(jb, accs):
                j = jb * 2
                accs = list(accs)
                d0, d1 = extd[j + CONV_PAD], extd[j + 1 + CONV_PAD]
                g0 = g1 = None
                for m in range(CONV_W + 1):
                    ed = extd[j + 2 * CONV_PAD + 1 - m]
                    ex = extx[j + m]
                    if m < CONV_W:
                        t = ed * wk[m]
                        g1 = t if g1 is None else g1 + t
                        accs[m] = accs[m] + d0 * ex
                    if m >= 1:
                        t = ed * wk[m - 1]
                        g0 = t if g0 is None else g0 + t
                        accs[m - 1] = accs[m - 1] + d1 * ex
                o[pl.ds(j, SUBLANES, stride=L), :] = g0
                o[pl.ds(j + 1, SUBLANES, stride=L), :] = g1
                return tuple(accs)

            accs = lax.fori_loop(0, L // 2, jbody, tuple(jnp.zeros((SUBLANES, LANES), F32) for _ in range(CONV_W)))
            for k in range(CONV_W):
                dw_ref[k:k + 1, sl] += jnp.sum(accs[k], axis=0, keepdims=True)
        dglu = jnp.concatenate([ob[lt] for lt in range(nlt)], axis=1)
        a = pre_ref[0].astype(F32)
        gate = pre_ref[1].astype(F32)
        sg = _sigmoid(gate)
        da = dglu * sg
        dgate = dglu * a * sg * (1.0 - sg)
        dpre_ref[0] = da.astype(BF16)
        dpre_ref[1] = dgate.astype(BF16)
        dbd_ref[...] += jnp.sum(d_ref[...], axis=0, keepdims=True)
        dbp_ref[0] += jnp.sum(da, axis=0, keepdims=True)
        dbp_ref[1] += jnp.sum(dgate, axis=0, keepdims=True)

    full = lambda i: (0, 0)
    (dpre, dw, dbd, dbp), got = _call(
        body, name="conv_bwd", grid=(n_i,),
        in_specs=[main, prev, nxt, main, prev, nxt, pl.BlockSpec((2, tt, D), lambda i: (0, i, 0)),
                  pl.BlockSpec((32, D), full)],
        out_specs=[pl.BlockSpec((2, tt, D), lambda i: (0, i, 0)), pl.BlockSpec((32, D), full),
                   pl.BlockSpec((1, D), full), pl.BlockSpec((2, 1, D), lambda i: (0, 0, 0))],
        out_shape=[jax.ShapeDtypeStruct((2, T, D), BF16), jax.ShapeDtypeStruct((32, D), F32),
                   jax.ShapeDtypeStruct((1, D), F32), jax.ShapeDtypeStruct((2, 1, D), F32)],
        scratch_shapes=[pltpu.VMEM((nlt, tt + 2 * HALO, LANES), F32), pltpu.VMEM((nlt, tt + 2 * HALO, LANES), F32),
                        pltpu.VMEM((nlt, tt, LANES), F32), pltpu.VMEM((L + 2 * HALO, SUBLANES, LANES), F32),
                        pltpu.VMEM((L + 2 * HALO, SUBLANES, LANES), F32), pltpu.VMEM((32, SUBLANES, LANES), F32)],
        semantics=("arbitrary",), args=(ddwc, ddwc, ddwc, glu, glu, glu, pre, w_dw), comm=comm)
    return dpre, dw, dbd, dbp, got


def mm_bt(a, w, name):
    T = a.shape[0]
    N = w.shape[0]
    tm = _tile(T, 512)

    def body(a_ref, w_ref, o_ref):
        o_ref[...] = _dot_tb(a_ref[...].astype(BF16), w_ref[...]).astype(BF16)

    return pl.pallas_call(
        body, name=name, grid=(T // tm,),
        in_specs=[pl.BlockSpec((tm, D), lambda i: (i, 0)), pl.BlockSpec((N, D), lambda i: (0, 0))],
        out_specs=pl.BlockSpec((tm, N), lambda i: (i, 0)),
        out_shape=jax.ShapeDtypeStruct((T, N), BF16),
        compiler_params=_params("parallel"),
    )(a, w)


def attn_bwd(qkv, o, do, sink, rc, rs1, rs2, comm=None):
    T = qkv.shape[0]
    nb, q_spec, prev, own, nxt = _attn_specs(T)
    kvw = N_KV * HD

    def body(sink_ref, q_ref, kp_ref, ko_ref, kn_ref, o_ref, do_ref, c_ref, s1_ref, s2_ref,
             dq_ref, dkc_ref, dvc_ref, dsink_ref):
        n = pl.program_id(0)

        @pl.when(n == 0)
        def _():
            dsink_ref[...] = jnp.zeros_like(dsink_ref)

        valid = _attn_mask(n, T)
        kv = jnp.concatenate([kp_ref[...], ko_ref[...], kn_ref[...]], axis=0)
        kx, vx = _kv_padded(kv, 0), _kv_padded(kv, 2)
        tile = lambda ref, j: ref[:, j * LANES:(j + 1) * LANES]
        ss = [_dot_tb(kx[h // GROUP, h % 2], tile(q_ref, h // 2)) for h in range(N_HEADS)]
        dps = [_dot_tb(vx[h // GROUP, h % 2], tile(do_ref, h // 2)) for h in range(N_HEADS)]
        low_d = lax.broadcasted_iota(jnp.int32, (LANES, BLK), 0) < HD
        deltas = []
        for j in range(N_HEADS // 2):
            prod_t = tile(do_ref, j).astype(F32).T * tile(o_ref, j).astype(F32).T
            deltas.append(jnp.sum(jnp.where(low_d, prod_t, 0.0), axis=0, keepdims=True))
            deltas.append(jnp.sum(jnp.where(low_d, 0.0, prod_t), axis=0, keepdims=True))
        lane = lax.broadcasted_iota(jnp.int32, (1, N_HEADS), 1)
        dsink = jnp.zeros((1, N_HEADS), F32)
        pbs, dss = [], []
        for h in range(N_HEADS):
            p, p_sink = _softmax_sink(ss[h], valid, sink_ref[h])
            dss.append((p * (dps[h] - deltas[h])).astype(BF16))
            pbs.append(p.astype(BF16))
            part = -jnp.sum(p_sink * deltas[h], axis=1, keepdims=True)
            dsink = dsink + jnp.where(lane == h, part, 0.0)
        dsink_ref[...] += dsink
        c, s1, s2 = c_ref[...], s1_ref[...], s2_ref[...]
        kxt = {k: v.T for k, v in kx.items()}
        for j in range(N_HEADS // 2):
            g = 2 * j // GROUP
            dq_t = _dot(kxt[g, 0], dss[2 * j]) + _dot(kxt[g, 1], dss[2 * j + 1])
            dq_ref[:, j * LANES:(j + 1) * LANES] = (_rope(dq_t.T, c, -s1, -s2) * Q_SCALE).astype(BF16)
        low_k = lax.broadcasted_iota(jnp.int32, (3 * BLK, LANES), 1) < HD
        cols = lambda xs, g, p: jnp.concatenate([xs[GROUP * g + p], xs[GROUP * g + 2 + p]], axis=1)
        for t in range(N_KV // 2):
            sums = {}
            for g in (2 * t, 2 * t + 1):
                q2 = jnp.concatenate([tile(q_ref, 2 * g), tile(q_ref, 2 * g + 1)], axis=0)
                do2 = jnp.concatenate([tile(do_ref, 2 * g), tile(do_ref, 2 * g + 1)], axis=0)
                for p in range(2):
                    sums[g, p] = (_dot(cols(dss, g, p), q2), _dot(cols(pbs, g, p), do2))
            for which, ref in ((0, dkc_ref), (1, dvc_ref)):
                keep = jnp.where(low_k, sums[2 * t, 0][which], sums[2 * t + 1, 1][which])
                swap = jnp.where(low_k, sums[2 * t + 1, 0][which], sums[2 * t, 1][which])
                ref[:, t * LANES:(t + 1) * LANES] = keep + pltpu.roll(swap, HD, 1)

    row = lambda n: (n, 0)
    (dq, dkc, dvc, dsink), got = _call(
        body, name="attn_bwd", grid=(nb,),
        in_specs=[pl.BlockSpec(memory_space=pltpu.SMEM), q_spec, prev, own, nxt,
                  pl.BlockSpec((BLK, D), row), pl.BlockSpec((BLK, D), row), *_tab_specs(BLK)],
        out_specs=[pl.BlockSpec((BLK, D), row), pl.BlockSpec((None, 3 * BLK, kvw), lambda n: (n, 0, 0)),
                   pl.BlockSpec((None, 3 * BLK, kvw), lambda n: (n, 0, 0)), pl.BlockSpec((1, N_HEADS), lambda n: (0, 0))],
        out_shape=[jax.ShapeDtypeStruct((T, QKV), BF16), jax.ShapeDtypeStruct((nb, 3 * BLK, kvw), F32),
                   jax.ShapeDtypeStruct((nb, 3 * BLK, kvw), F32), jax.ShapeDtypeStruct((1, N_HEADS), F32)],
        semantics=("arbitrary",), args=(sink, qkv, qkv, qkv, qkv, o, do, rc, rs1, rs2), comm=comm)
    return dq, dkc, dvc, dsink, got


def kv_sum(dqkv, dkc, dvc, rc, rs1, rs2):
    nb = dkc.shape[0]
    T = nb * BLK
    kvw = N_KV * HD

    def body(_, kp_ref, ko_ref, kn_ref, vp_ref, vo_ref, vn_ref, c_ref, s1_ref, s2_ref, out_ref):
        m = pl.program_id(0)
        has_p = (m > 0).astype(F32)
        has_n = (m < nb - 1).astype(F32)
        dk = kp_ref[...] * has_p + ko_ref[...] + kn_ref[...] * has_n
        dv = vp_ref[...] * has_p + vo_ref[...] + vn_ref[...] * has_n
        c, s1, s2 = c_ref[...], s1_ref[...], s2_ref[...]
        for j in range(kvw // LANES):
            sl = slice(LANES * j, LANES * (j + 1))
            out_ref[:, sl] = _rope(dk[:, sl], c, -s1, -s2).astype(BF16)
        out_ref[:, kvw:] = dv.astype(BF16)

    from_prev = pl.BlockSpec((None, BLK, kvw), lambda m: (jnp.maximum(m - 1, 0), 2, 0))
    from_own = pl.BlockSpec((None, BLK, kvw), lambda m: (m, 1, 0))
    from_next = pl.BlockSpec((None, BLK, kvw), lambda m: (jnp.minimum(m + 1, nb - 1), 0, 0))
    return pl.pallas_call(
        body, name="kv_sum", grid=(nb,),
        in_specs=[pl.BlockSpec(memory_space=pl.ANY), from_prev, from_own, from_next, from_prev, from_own, from_next,
                  *_tab_specs(BLK)],
        out_specs=pl.BlockSpec((BLK, 2 * kvw), lambda m: (m, KV_OFF // (2 * kvw))),
        out_shape=jax.ShapeDtypeStruct((T, QKV), BF16),
        input_output_aliases={0: 0},
        compiler_params=_params("parallel"),
    )(dqkv, dkc, dkc, dkc, dvc, dvc, dvc, rc, rs1, rs2)


def _me():
    return lax.axis_index("x"), lax.axis_index("y"), lax.axis_index("c")


def _half_rows(ref, sharded_rows, chip, core):
    R, C = ref.shape[-2], ref.shape[-1]
    lead = (slice(None),) * (len(ref.shape) - 2)
    if sharded_rows:
        per = R // N_CHIPS
        return ref.at[lead + (pl.ds(chip * per + core * (per // 2), per // 2), slice(None))]
    per = C // N_CHIPS
    return ref.at[lead + (pl.ds(core * (R // 2), R // 2), pl.ds(chip * per, per))]


class _Gather:
    def __init__(self, shards, sharded_rows):
        self.inputs = list(shards)
        self.rows = list(sharded_rows)
        self.n = self.n_in = self.n_out = len(shards)
        self.out_shapes = []
        for s, rows in zip(shards, sharded_rows):
            shp = list(s.shape)
            shp[-2 if rows else -1] *= N_CHIPS
            self.out_shapes.append(jax.ShapeDtypeStruct(tuple(shp), s.dtype))
        self.scratch = [pltpu.SemaphoreType.DMA((self.n, 6)), pltpu.SemaphoreType.DMA((self.n, 6)),
                        pltpu.SemaphoreType.DMA((self.n, 2))]

    def _ctx(self, ins, outs, sems):
        send_sems, recv_sems, local_sems = sems
        x, y, c = _me()
        chips = [(1 - x, y), (x, 1 - y), (1 - x, 1 - y)]

        def half_src(w, core):
            s = ins[w]
            R = s.shape[-2]
            return s.at[pl.ds(core * (R // 2), R // 2), :]

        def dst(w, chip, core):
            return _half_rows(outs[w], self.rows[w], chip, core)

        def copy(w, k, src, chip, core, to):
            return pltpu.make_async_remote_copy(
                src_ref=src, dst_ref=dst(w, chip, core), send_sem=send_sems.at[w, k], recv_sem=recv_sems.at[w, k],
                device_id=to, device_id_type=MESH)

        def local(w, core):
            return pltpu.make_async_copy(half_src(w, core), dst(w, 2 * x + y, core), local_sems.at[w, core])

        def first(w, j):
            qx, qy = chips[j]
            return copy(w, j, half_src(w, c), 2 * x + y, c, (qx, qy, c))

        def landed(w, j):
            qx, qy = chips[j]
            return copy(w, j, dst(w, 2 * qx + qy, c), 2 * qx + qy, c, (x, y, c))

        def passed(w, j):
            qx, qy = chips[j]
            return copy(w, 3 + j, dst(w, 2 * qx + qy, c), 2 * qx + qy, c, (x, y, 1 - c))

        def from_sibling(w, j):
            qx, qy = chips[j]
            return copy(w, 3 + j, dst(w, 2 * qx + qy, 1 - c), 2 * qx + qy, 1 - c, (x, y, c))

        return local, first, landed, passed, from_sibling

    def start(self, ins, outs, sems):
        local, first, _, _, _ = self._ctx(ins, outs, sems)
        for w in range(self.n):
            for core in range(2):
                local(w, core).start()
            for j in range(3):
                first(w, j).start()

    def mid(self, ins, outs, sems):
        _, _, landed, passed, _ = self._ctx(ins, outs, sems)
        for w in range(self.n):
            for j in range(3):
                landed(w, j).wait_recv()
                passed(w, j).start()

    def end(self, ins, outs, sems):
        local, first, _, passed, from_sibling = self._ctx(ins, outs, sems)
        for w in range(self.n):
            for j in range(3):
                from_sibling(w, j).wait_recv()
        for w in range(self.n):
            for j in range(3):
                first(w, j).wait_send()
                passed(w, j).wait_send()
            for core in range(2):
                local(w, core).wait()


class _Scatter:
    def __init__(self, grads, small=None):
        self.inputs = list(grads) + ([small] if small is not None else [])
        self.ng = len(grads)
        self.n = self.n_in = self.n_out = len(self.inputs)
        self.out_shapes = [jax.ShapeDtypeStruct((N_DEV, g.shape[1] // 2, g.shape[2]), g.dtype) for g in grads]
        if small is not None:
            self.out_shapes.append(jax.ShapeDtypeStruct((N_DEV,) + small.shape, small.dtype))
        self.scratch = [pltpu.SemaphoreType.DMA((self.n, N_DEV)), pltpu.SemaphoreType.DMA((self.n, N_DEV)),
                        pltpu.SemaphoreType.DMA((self.n,))]

    def _ctx(self, ins, outs, sems):
        send_sems, recv_sems, local_sems = sems
        x, y, c = _me()
        me = 4 * x + 2 * y + c

        def piece(w, chip, core):
            if w >= self.ng:
                return ins[w]
            half = ins[w].shape[1] // 2
            return ins[w].at[chip, pl.ds(core * half, half), :]

        def peer_of(k):
            return x ^ ((k >> 2) & 1), y ^ ((k >> 1) & 1), c ^ (k & 1)

        def local(w):
            return pltpu.make_async_copy(piece(w, 2 * x + y, c), outs[w].at[me], local_sems.at[w])

        def send(w, k):
            px, py, pc = peer_of(k)
            return pltpu.make_async_remote_copy(
                src_ref=piece(w, 2 * px + py, pc), dst_ref=outs[w].at[me], send_sem=send_sems.at[w, k],
                recv_sem=recv_sems.at[w, k], device_id=(px, py, pc), device_id_type=MESH)

        def recv(w, k):
            px, py, pc = peer_of(k)
            return pltpu.make_async_remote_copy(
                src_ref=piece(w, 2 * x + y, c), dst_ref=outs[w].at[4 * px + 2 * py + pc], send_sem=send_sems.at[w, k],
                recv_sem=recv_sems.at[w, k], device_id=(px, py, pc), device_id_type=MESH)

        return local, send, recv

    def start(self, ins, outs, sems):
        local, send, _ = self._ctx(ins, outs, sems)
        for w in range(self.n):
            local(w).start()
            for k in range(1, N_DEV):
                send(w, k).start()

    def mid(self, ins, outs, sems):
        pass

    def end(self, ins, outs, sems):
        local, send, recv = self._ctx(ins, outs, sems)
        for w in range(self.n):
            for k in range(1, N_DEV):
                recv(w, k).wait_recv()
        for w in range(self.n):
            for k in range(1, N_DEV):
                send(w, k).wait_send()
            local(w).wait()


class _Both:
    def __init__(self, a, b):
        self.a, self.b = a, b
        self.inputs = a.inputs + b.inputs
        self.out_shapes = a.out_shapes + b.out_shapes
        self.scratch = a.scratch + b.scratch
        self.n_in, self.n_out = a.n_in + b.n_in, a.n_out + b.n_out

    def _split(self, ins, outs, sems):
        a, na = self.a, len(self.a.scratch)
        return (ins[:a.n_in], outs[:a.n_out], sems[:na]), (ins[a.n_in:], outs[a.n_out:], sems[na:])

    def start(self, ins, outs, sems):
        pa, pb = self._split(ins, outs, sems)
        self.a.start(*pa)
        self.b.start(*pb)

    def mid(self, ins, outs, sems):
        pa, pb = self._split(ins, outs, sems)
        self.a.mid(*pa)
        self.b.mid(*pb)

    def end(self, ins, outs, sems):
        pa, pb = self._split(ins, outs, sems)
        self.a.end(*pa)
        self.b.end(*pb)


def exchange(plan, name):
    def body(*refs):
        ins, outs, sems = refs[:plan.n_in], refs[plan.n_in:plan.n_in + plan.n_out], refs[plan.n_in + plan.n_out:]
        plan.start(ins, outs, sems)
        plan.mid(ins, outs, sems)
        plan.end(ins, outs, sems)

    any_spec = pl.BlockSpec(memory_space=pl.ANY)
    return pl.pallas_call(
        body, name=name, in_specs=[any_spec] * plan.n_in, out_specs=[any_spec] * plan.n_out,
        out_shape=plan.out_shapes, scratch_shapes=plan.scratch,
    )(*plan.inputs)


def _call(body, *, name, grid, in_specs, out_specs, out_shape, scratch_shapes=(), semantics, args, comm=None):
    if comm is None:
        outs = pl.pallas_call(
            body, name=name, grid=grid, in_specs=in_specs, out_specs=out_specs, out_shape=out_shape,
            scratch_shapes=list(scratch_shapes), compiler_params=_params(*semantics))(*args)
        return outs, []
    n_in, n_out, n_scr = len(in_specs), len(out_specs), len(scratch_shapes)

    total = math.prod(grid)
    first, middle, last = 0, (3 * total) // 4 - 1, total - 1
    assert first <= middle < last

    def at(step):
        lin = pl.program_id(0)
        for d in range(1, len(grid)):
            lin = lin * grid[d] + pl.program_id(d)
        return lin == step

    def hosted(*refs):
        h_in, c_in = refs[:n_in], refs[n_in:n_in + comm.n_in]
        rest = refs[n_in + comm.n_in:]
        h_out, c_out = rest[:n_out], rest[n_out:n_out + comm.n_out]
        rest = rest[n_out + comm.n_out:]
        h_scr, c_scr = rest[:n_scr], rest[n_scr:]

        @pl.when(at(first))
        def _():
            comm.start(c_in, c_out, c_scr)

        body(*h_in, *h_out, *h_scr)

        @pl.when(at(middle))
        def _():
            comm.mid(c_in, c_out, c_scr)

        @pl.when(at(last))
        def _():
            comm.end(c_in, c_out, c_scr)

    any_spec = pl.BlockSpec(memory_space=pl.ANY)
    outs = pl.pallas_call(
        hosted, name=name, grid=grid, in_specs=list(in_specs) + [any_spec] * comm.n_in,
        out_specs=list(out_specs) + [any_spec] * comm.n_out, out_shape=list(out_shape) + comm.out_shapes,
        scratch_shapes=list(scratch_shapes) + comm.scratch,
        compiler_params=_params(*(["arbitrary"] * len(grid))))(*args, *comm.inputs)
    return outs[:n_out], outs[n_out:]


def sum_swap(pieces, name):
    nl = len(pieces)
    _, r2, cc = pieces[0].shape
    tr = 128 if r2 % 128 == 0 else r2 // 2
    n = r2 // tr

    def body(*refs):
        p_refs, out = refs[:nl], refs[nl]
        slots, send_sems, local_sems, recv_sem = refs[nl + 1:]
        x, y, c = _me()
        sibling = (x, y, 1 - c)
        l, i = pl.program_id(0), pl.program_id(1)
        step = l * n + i

        def rows(st, core):
            return out.at[st // n, pl.ds(core * r2 + (st % n) * tr, tr), :]

        def copies(st):
            slot = st % 2
            local = pltpu.make_async_copy(slots.at[slot], rows(st, c), local_sems.at[slot])
            remote = pltpu.make_async_remote_copy(
                src_ref=slots.at[slot], dst_ref=rows(st, c), send_sem=send_sems.at[slot], recv_sem=recv_sem,
                device_id=sibling, device_id_type=MESH)
            return local, remote

        for ll in range(nl):
            @pl.when(l == ll)
            def _():
                acc = p_refs[ll][0].astype(F32)
                for d in range(1, N_DEV):
                    acc = acc + p_refs[ll][d].astype(F32)
                slots[step % 2] = acc

        for cp in copies(step):
            cp.start()

        @pl.when(step >= 1)
        def _():
            local, remote = copies(step - 1)
            local.wait()
            remote.wait_send()

        @pl.when(step == nl * n - 1)
        def _():
            local, remote = copies(step)
            local.wait()
            remote.wait_send()
            theirs = out.at[:, pl.ds((1 - c) * r2, r2), :]
            pltpu.make_async_remote_copy(src_ref=theirs, dst_ref=theirs, send_sem=send_sems.at[0],
                                         recv_sem=recv_sem, device_id=sibling, device_id_type=MESH).wait_recv()

    def piece_spec(ll):
        def index(l, i):
            return (0, jnp.where(l == ll, i, jnp.where(l < ll, 0, n - 1)), 0)
        return pl.BlockSpec((N_DEV, tr, cc), index)

    return pl.pallas_call(
        body, name=name, grid=(nl, n),
        in_specs=[piece_spec(ll) for ll in range(nl)],
        out_specs=pl.BlockSpec(memory_space=pl.ANY),
        out_shape=jax.ShapeDtypeStruct((nl, 2 * r2, cc), F32),
        scratch_shapes=[pltpu.VMEM((2, tr, cc), F32), pltpu.SemaphoreType.DMA((2,)), pltpu.SemaphoreType.DMA((2,)),
                        pltpu.SemaphoreType.DMA(())],
        compiler_params=_params("arbitrary", "arbitrary"),
    )(*pieces)


def sum_pieces(pieces, name):
    _, R, C = pieces.shape
    tr = _tile(R, 128) if R % 128 == 0 else R

    def body(p_ref, o_ref):
        acc = p_ref[0].astype(F32)
        for d in range(1, N_DEV):
            acc = acc + p_ref[d].astype(F32)
        o_ref[...] = acc

    return pl.pallas_call(
        body, name=name, grid=(R // tr,),
        in_specs=[pl.BlockSpec((N_DEV, tr, C), lambda i: (0, i, 0))],
        out_specs=pl.BlockSpec((tr, C), lambda i: (i, 0)),
        out_shape=jax.ShapeDtypeStruct((R, C), F32),
        compiler_params=_params("parallel"),
    )(pieces)


def adamw(w, g, m, v, name):
    Lyr, R, C = w.shape
    tr = _tile(R, 256) if R % 8 == 0 else R
    c1 = 1.0 / (1.0 - ADAM_B1 ** ADAM_STEP)
    c2 = 1.0 / (1.0 - ADAM_B2 ** ADAM_STEP)

    def body(w_ref, g_ref, m_ref, v_ref, d_ref, nm_ref, nv_ref):
        gv = g_ref[...]
        nm = ADAM_B1 * m_ref[...] + (1.0 - ADAM_B1) * gv
        nv = ADAM_B2 * v_ref[...] + (1.0 - ADAM_B2) * (gv * gv)
        nm_ref[...] = nm
        nv_ref[...] = nv
        d_ref[...] = -ADAM_LR * ((nm * c1) / (jnp.sqrt(nv * c2) + ADAM_EPS) + ADAM_WD * w_ref[...])

    spec = pl.BlockSpec((None, tr, C), lambda l, i: (l, i, 0))
    shp = jax.ShapeDtypeStruct(w.shape, F32)
    return pl.pallas_call(
        body, name=name, grid=(Lyr, R // tr),
        in_specs=[spec] * 4, out_specs=[spec] * 3, out_shape=[shp] * 3,
        compiler_params=_params("parallel", "parallel"),
    )(w, g, m, v)


def _rope_tables(T):
    pos = jnp.arange(T, dtype=F32)
    inv_freq = THETA ** (-jnp.arange(0, ROT, 2, dtype=F32) / ROT)
    ang = pos[:, None] * inv_freq[None, :]
    cs = jnp.concatenate([jnp.cos(ang), jnp.sin(ang)], axis=1)
    half = ROT // 2
    lane = jnp.arange(3 * LANES)
    table, lm = lane // LANES, lane % HD
    src = jnp.where(table == 0, lm % half, half + lm % half)
    i32 = lambda b: b.astype(jnp.int32)
    sign = jnp.where(table == 0, i32(lm < ROT), jnp.where(table == 1, -i32(lm < half), i32((lm >= half) & (lm < ROT))))
    place = (jnp.arange(ROT)[:, None] == src[None, :]) * sign[None, :].astype(F32)
    ones = ((table == 0) & (lm >= ROT)).astype(F32)
    return jnp.dot(cs, place, precision=lax.Precision.HIGHEST) + ones[None, :]


def _tab_specs(rows):
    return [pl.BlockSpec((rows, LANES), lambda i, k=k: (i, k)) for k in range(3)]


def kernel(x, attn_norm, attn_w_qkv, attn_w_o, attn_sink, conv_norm, conv_w_pw1, conv_b_pw1, conv_w_dw, conv_b_dw, conv_ln_g, conv_ln_b, conv_w_pw2, conv_b_pw2, ffn_norm, ffn_w_gu, ffn_w_down, final_norm, loss_target, m_attn_norm, m_attn_w_qkv, m_attn_w_o, m_attn_sink, m_conv_norm, m_conv_w_pw1, m_conv_b_pw1, m_conv_w_dw, m_conv_b_dw, m_conv_ln_g, m_conv_ln_b, m_conv_w_pw2, m_conv_b_pw2, m_ffn_norm, m_ffn_w_gu, m_ffn_w_down, m_final_norm, v_attn_norm, v_attn_w_qkv, v_attn_w_o, v_attn_sink, v_conv_norm, v_conv_w_pw1, v_conv_b_pw1, v_conv_w_dw, v_conv_b_dw, v_conv_ln_g, v_conv_ln_b, v_conv_w_pw2, v_conv_b_pw2, v_ffn_norm, v_ffn_w_gu, v_ffn_w_down, v_final_norm):
    T = x.shape[1]
    x0 = x[0]
    target = loss_target[0]
    ix, iy = lax.axis_index("x"), lax.axis_index("y")
    chip = 2 * ix + iy
    rc = rs1 = rs2 = _rope_tables(T)

    bf = lambda t: t.astype(BF16)
    col_row = [False, True]

    def place(vec, width):
        return lax.dynamic_update_slice(jnp.zeros((vec.shape[0], N_CHIPS * width), F32), vec, (0, chip * width))

    small_rows = jnp.concatenate([
        place(conv_norm, 256), place(conv_b_pw1, 512).reshape(2, D), place(conv_b_dw, 256), place(conv_ln_g, 256),
        place(conv_ln_b, 256), place(conv_b_pw2, 256), jnp.zeros((1, D), F32),
        place(conv_w_dw[0], 256), jnp.zeros((1, D), F32)], axis=0)
    w_qkv, w_o, got = exchange(_Both(_Gather([bf(attn_w_qkv[0]), bf(attn_w_o[0])], col_row),
                                     _Scatter([], small_rows)), "gather_attn")
    psmall = sum_pieces(got, "sum_small_params") * 0.5
    p_conv_norm, p_b_pw1 = psmall[0:1], psmall[1:3].reshape(1, 2 * D)
    p_b_dw, p_ln_g, p_ln_b, p_b_pw2 = psmall[3:4], psmall[4:5], psmall[5:6], psmall[6:7]
    p_w_dw = psmall[8:40]

    h0, qkv = rms_qkv(x0, attn_norm, w_qkv, rc, rs1, rs2)
    sink = attn_sink[0]
    o, (w_gu0, w_down0) = attn_fwd(qkv, sink, comm=_Gather([bf(ffn_w_gu[0]), bf(ffn_w_down[0])], col_row))
    zero_b = jnp.zeros((1, D), F32)
    x1 = mm_res(o, w_o, x0, zero_b, "attn_out")
    zero_gu = jnp.zeros((1, 2 * DFF), F32)
    h1, gu0, act0, (w_pw1, w_pw2) = rms_mm_gate(
        x1, ffn_norm[0:1], w_gu0, zero_gu, DFF, True, BF16, "ffn0_up",
        comm=_Gather([bf(conv_w_pw1[0]), bf(conv_w_pw2[0])], col_row))
    x2 = mm_res(act0, w_down0, x1, zero_b, "ffn0_down")
    h2, pre, glu, _ = rms_mm_gate(x2, p_conv_norm, w_pw1, p_b_pw1, D, False, F32, "conv_pw1")
    dwc, sw, (w_gu1, w_down1) = conv_fwd(glu, p_w_dw, p_b_dw, p_ln_g, p_ln_b,
                                         comm=_Gather([bf(ffn_w_gu[1]), bf(ffn_w_down[1])], col_row))
    x3 = mm_res(sw, w_pw2, x2, p_b_pw2, "conv_pw2")
    h3, gu1, act1, _ = rms_mm_gate(x3, ffn_norm[1:2], w_gu1, zero_gu, DFF, True, BF16, "ffn1_up")
    x4 = mm_res(act1, w_down1, x3, zero_b, "ffn1_down")
    dx4, loss_part, d_final = final_loss(x4, final_norm.reshape(1, D), target)
    loss = lax.psum(loss_part[0, 0], ("x", "y", "c"))

    dgu1, _ = swiglu_bwd(dx4, w_down1, gu1, "ffn1_down_bwd")
    g_down1 = dw_row(act1, dx4, "ffn1_down_dw")
    dx3, d_ffn1, _ = mm_bt_rmsbwd(dgu1, w_gu1, x3, ffn_norm[1:2], dx4, "ffn1_up_bwd")
    g_gu1 = dw_col(h3, dgu1, "ffn1_up_dw")

    ddwc, d_ln_g, d_ln_b, d_b_pw2 = ln_silu_bwd(dx3, w_pw2, dwc, p_ln_g, p_ln_b)
    g_pw2 = dw_row(sw, dx3, "conv_pw2_dw")
    dpre, d_w_dw, d_b_dw, d_b_pw1, (r_gu1, r_down1) = conv_bwd(ddwc, glu, pre, p_w_dw,
                                                               comm=_Scatter([g_gu1, g_down1]))
    dx2, d_conv_norm, _ = mm_bt_rmsbwd(dpre, w_pw1, x2, p_conv_norm, dx3, "conv_pw1_bwd")
    g_pw1 = dw_col(h2, dpre, "conv_pw1_dw")

    dgu0, (r_pw1, r_pw2) = swiglu_bwd(dx2, w_down0, gu0, "ffn0_down_bwd", comm=_Scatter([g_pw1, g_pw2]))
    g_down0 = dw_row(act0, dx2, "ffn0_down_dw")
    dx1, d_ffn0, _ = mm_bt_rmsbwd(dgu0, w_gu0, x1, ffn_norm[0:1], dx2, "ffn0_up_bwd")
    g_gu0 = dw_col(h1, dgu0, "ffn0_up_dw")

    do = mm_bt(dx1, w_o, "attn_out_bwd")
    g_o = dw_row(o, dx1, "attn_out_dw")
    dq, dkc, dvc, d_sink, (r_gu0, r_down0, r_o) = attn_bwd(qkv, o, do, sink, rc, rs1, rs2,
                                                           comm=_Scatter([g_gu0, g_down0, g_o]))
    dqkv = kv_sum(dq, dkc, dvc, rc, rs1, rs2)[None]
    g_qkv = dw_col(h0, dqkv, "attn_qkv_dw")
    dx0, d_attn_norm, (r_qkv,) = mm_bt_rmsbwd(dqkv, w_qkv, x0, attn_norm, dx1, "attn_qkv_bwd",
                                              comm=_Scatter([g_qkv]))

    pad16 = lambda t: jnp.concatenate([t, jnp.zeros((1, D - t.shape[1]), F32)], axis=1)
    small_g = jnp.concatenate([
        d_attn_norm, pad16(d_sink), d_conv_norm, d_b_pw1.reshape(2, D), d_b_dw, d_ln_g, d_ln_b, d_b_pw2,
        d_ffn0, d_ffn1, d_final, jnp.zeros((4, D), F32), d_w_dw], axis=0)
    r_small, = exchange(_Scatter([], small_g), "scatter_small")
    gf_gu = sum_swap([r_gu0, r_gu1], "sum_gu")
    gf_down = sum_swap([r_down0, r_down1], "sum_down")
    gf_pw1, gf_pw2 = sum_swap([r_pw1], "sum_pw1"), sum_swap([r_pw2], "sum_pw2")
    gf_qkv, gf_o = sum_swap([r_qkv], "sum_qkv"), sum_swap([r_o], "sum_o")
    gs = sum_pieces(r_small, "sum_small_grads")

    def take(row0, nrows, width):
        return lax.dynamic_slice(gs, (row0, chip * width), (nrows, width))

    grads = {
        "attn_norm": gs[0:1], "attn_w_qkv": gf_qkv, "attn_w_o": gf_o, "attn_sink": gs[1:2, :N_HEADS],
        "conv_norm": take(2, 1, 256), "conv_w_pw1": gf_pw1,
        "conv_b_pw1": lax.dynamic_slice(gs[3:5].reshape(1, 2 * D), (0, chip * 512), (1, 512)),
        "conv_w_dw": take(16, 32, 256)[None, :CONV_W], "conv_b_dw": take(5, 1, 256), "conv_ln_g": take(6, 1, 256),
        "conv_ln_b": take(7, 1, 256), "conv_w_pw2": gf_pw2, "conv_b_pw2": take(8, 1, 256),
        "ffn_norm": gs[9:11], "ffn_w_gu": gf_gu, "ffn_w_down": gf_down, "final_norm": gs[11],
    }
    weights = dict(attn_norm=attn_norm, attn_w_qkv=attn_w_qkv, attn_w_o=attn_w_o, attn_sink=attn_sink,
                   conv_norm=conv_norm, conv_w_pw1=conv_w_pw1, conv_b_pw1=conv_b_pw1, conv_w_dw=conv_w_dw,
                   conv_b_dw=conv_b_dw, conv_ln_g=conv_ln_g, conv_ln_b=conv_ln_b, conv_w_pw2=conv_w_pw2,
                   conv_b_pw2=conv_b_pw2, ffn_norm=ffn_norm, ffn_w_gu=ffn_w_gu, ffn_w_down=ffn_w_down,
                   final_norm=final_norm)
    m_in = dict(attn_norm=m_attn_norm, attn_w_qkv=m_attn_w_qkv, attn_w_o=m_attn_w_o, attn_sink=m_attn_sink,
                conv_norm=m_conv_norm, conv_w_pw1=m_conv_w_pw1, conv_b_pw1=m_conv_b_pw1, conv_w_dw=m_conv_w_dw,
                conv_b_dw=m_conv_b_dw, conv_ln_g=m_conv_ln_g, conv_ln_b=m_conv_ln_b, conv_w_pw2=m_conv_w_pw2,
                conv_b_pw2=m_conv_b_pw2, ffn_norm=m_ffn_norm, ffn_w_gu=m_ffn_w_gu, ffn_w_down=m_ffn_w_down,
                final_norm=m_final_norm)
    v_in = dict(attn_norm=v_attn_norm, attn_w_qkv=v_attn_w_qkv, attn_w_o=v_attn_w_o, attn_sink=v_attn_sink,
                conv_norm=v_conv_norm, conv_w_pw1=v_conv_w_pw1, conv_b_pw1=v_conv_b_pw1, conv_w_dw=v_conv_w_dw,
                conv_b_dw=v_conv_b_dw, conv_ln_g=v_conv_ln_g, conv_ln_b=v_conv_ln_b, conv_w_pw2=v_conv_w_pw2,
                conv_b_pw2=v_conv_b_pw2, ffn_norm=v_ffn_norm, ffn_w_gu=v_ffn_w_gu, ffn_w_down=v_ffn_w_down,
                final_norm=v_final_norm)
    order = list(weights)
    g_out, d_out, m_out, v_out = [], [], [], []
    for nm in order:
        w = weights[nm]
        shape = w.shape
        as3 = lambda t: t.reshape((1,) * (3 - len(shape)) + shape) if len(shape) < 3 else t.reshape(shape)
        g3 = as3(grads[nm].reshape(shape))
        delta, nm_, nv_ = adamw(as3(w), g3, as3(m_in[nm]), as3(v_in[nm]), "adamw_" + nm)
        g_out.append(g3.reshape(shape))
        d_out.append(delta.reshape(shape))
        m_out.append(nm_.reshape(shape))
        v_out.append(nv_.reshape(shape))
    return (loss, dx0[None], *g_out, *d_out, *m_out, *v_out)
```

```python
import functools
import math

import jax
import jax.numpy as jnp
from jax import lax
from jax.experimental import pallas as pl
from jax.experimental.pallas import tpu as pltpu

F32 = jnp.float32
BF16 = jnp.bfloat16

D = 1024
N_HEADS = 16
N_KV = 4
GROUP = N_HEADS // N_KV
HD = 64
ROT = 16
THETA = 500000.0
BLK = 128
QKV = (N_HEADS + 2 * N_KV) * HD
KV_OFF = N_HEADS * HD
DFF = 2816
CONV_W = 31
CONV_PAD = 15
HALO = 16
CONV_JB = 8
CONV_JB_BWD = 4
EPS = 1e-6
NEG = -1e30
N_CHIPS = 4
N_DEV = 8
LANES = 128
SUBLANES = 8

ADAM_LR, ADAM_B1, ADAM_B2, ADAM_EPS, ADAM_WD, ADAM_STEP = 0.001, 0.9, 0.999, 1e-08, 0.01, 10

VMEM_LIMIT = 56 * 1024 * 1024
MESH = pl.DeviceIdType.MESH


def _params(*sem):
    return pltpu.CompilerParams(dimension_semantics=sem, vmem_limit_bytes=VMEM_LIMIT)


def _tile(n, want):
    if n <= want:
        return n
    for t in range(want, 7, -1):
        if n % t == 0 and t % 8 == 0:
            return t
    return n


MXU_COLS = 256


def _col_chunks(n):
    return [slice(c, min(c + MXU_COLS, n)) for c in range(0, n, MXU_COLS)]


def _sigmoid(v):
    return 1.0 / (1.0 + jnp.exp(-v))


def _rms_fwd(xv, gain):
    r = lax.rsqrt(jnp.mean(xv * xv, axis=-1, keepdims=True) + EPS)
    return xv * r * gain


def _rms_bwd(dh, xv, gain, dres):
    r = lax.rsqrt(jnp.mean(xv * xv, axis=-1, keepdims=True) + EPS)
    xhat = xv * r
    gy = dh * gain
    dx = r * (gy - xhat * jnp.mean(gy * xhat, axis=-1, keepdims=True))
    return dx + dres, dh * xhat


def _rope(blk, c, s1, s2):
    return blk * c + pltpu.roll(blk, LANES - ROT // 2, 1) * s1 + pltpu.roll(blk, ROT // 2, 1) * s2


def _dot(a, b):
    return jnp.dot(a, b, preferred_element_type=F32)


def _dot_tb(a, b):
    return lax.dot_general(a, b, (((1,), (1,)), ((), ())), preferred_element_type=F32)


def _dot_ta(a, b):
    return lax.dot_general(a, b, (((0,), (0,)), ((), ())), preferred_element_type=F32)


def rms_qkv(x, gain, w, rc, rs1, rs2):
    T = x.shape[0]
    tm = _tile(T, 512)

    def body(x_ref, g_ref, w_ref, c_ref, s1_ref, s2_ref, h_ref, qkv_ref):
        h = _rms_fwd(x_ref[...], g_ref[...]).astype(BF16)
        h_ref[...] = h
        acc = _dot(h, w_ref[...])
        c, s1, s2 = c_ref[...], s1_ref[...], s2_ref[...]
        n_rot = (KV_OFF + N_KV * HD) // LANES
        for j in range(n_rot):
            sl = slice(LANES * j, LANES * (j + 1))
            roped = _rope(acc[:, sl], c, s1, s2)
            if j < KV_OFF // LANES:
                roped = roped * Q_SCALE
            qkv_ref[:, sl] = roped.astype(BF16)
        qkv_ref[:, n_rot * LANES:] = acc[:, n_rot * LANES:].astype(BF16)

    row = lambda i: (i, 0)
    full = lambda i: (0, 0)
    return pl.pallas_call(
        body, name="rms_qkv", grid=(T // tm,),
        in_specs=[pl.BlockSpec((tm, D), row), pl.BlockSpec((1, D), full), pl.BlockSpec((D, QKV), full),
                  *_tab_specs(tm)],
        out_specs=[pl.BlockSpec((tm, D), row), pl.BlockSpec((tm, QKV), row)],
        out_shape=[jax.ShapeDtypeStruct((T, D), BF16), jax.ShapeDtypeStruct((T, QKV), BF16)],
        compiler_params=_params("parallel"),
    )(x, gain, w, rc, rs1, rs2)


Q_SCALE = 1.0 / math.sqrt(HD)


def _attn_mask(n, T):
    ci = lax.broadcasted_iota(jnp.int32, (3 * BLK, BLK), 0)
    qi = lax.broadcasted_iota(jnp.int32, (3 * BLK, BLK), 1)
    key_pos = n * BLK - BLK + ci
    return (jnp.abs(ci - BLK - qi) <= BLK) & (key_pos >= 0) & (key_pos < T)


def _kv_padded(kv, first_tile):
    low = lax.broadcasted_iota(jnp.int32, (3 * BLK, LANES), 1) < HD
    zero = jnp.zeros((3 * BLK, LANES), BF16)
    out = {}
    for g in range(N_KV):
        t = kv[:, (first_tile + g // 2) * LANES:(first_tile + g // 2 + 1) * LANES]
        swapped = jnp.concatenate([t[:, HD:], t[:, :HD]], axis=1)
        for p in range(2):
            out[g, p] = jnp.where(low if p == 0 else ~low, t if g % 2 == p else swapped, zero)
    return out


def _softmax_sink(s, valid, sk):
    s = jnp.where(valid, s, NEG)
    m = jnp.maximum(jnp.max(s, axis=0, keepdims=True), sk)
    e = jnp.exp(s - m)
    es = jnp.exp(sk - m)
    inv = 1.0 / (jnp.sum(e, axis=0, keepdims=True) + es)
    return e * inv, es * inv


def _attn_specs(T):
    nb = T // BLK
    kv_blk = 2 * N_KV * HD
    kv_col = KV_OFF // kv_blk
    q_spec = pl.BlockSpec((BLK, KV_OFF), lambda n: (n, 0))
    prev = pl.BlockSpec((BLK, kv_blk), lambda n: (jnp.maximum(n - 1, 0), kv_col))
    own = pl.BlockSpec((BLK, kv_blk), lambda n: (n, kv_col))
    nxt = pl.BlockSpec((BLK, kv_blk), lambda n: (jnp.minimum(n + 1, nb - 1), kv_col))
    return nb, q_spec, prev, own, nxt


def attn_fwd(qkv, sink, comm=None):
    T = qkv.shape[0]
    nb, q_spec, prev, own, nxt = _attn_specs(T)

    def body(sink_ref, q_ref, kp_ref, ko_ref, kn_ref, o_ref):
        valid = _attn_mask(pl.program_id(0), T)
        kv = jnp.concatenate([kp_ref[...], ko_ref[...], kn_ref[...]], axis=0)
        kx, vx = _kv_padded(kv, 0), _kv_padded(kv, 2)
        tile = lambda ref, h: ref[:, (h // 2) * LANES:(h // 2 + 1) * LANES]
        ss = [_dot_tb(kx[h // GROUP, h % 2], tile(q_ref, h)) for h in range(N_HEADS)]
        ps = [_softmax_sink(ss[h], valid, sink_ref[h])[0].astype(BF16) for h in range(N_HEADS)]
        vxt = {k: v.T for k, v in vx.items()}
        for j in range(N_HEADS // 2):
            g = 2 * j // GROUP
            o_t = _dot(vxt[g, 0], ps[2 * j]) + _dot(vxt[g, 1], ps[2 * j + 1])
            o_ref[:, j * LANES:(j + 1) * LANES] = o_t.T.astype(BF16)

    (o,), got = _call(
        body, name="attn_fwd", grid=(nb,),
        in_specs=[pl.BlockSpec(memory_space=pltpu.SMEM), q_spec, prev, own, nxt],
        out_specs=[pl.BlockSpec((BLK, D), lambda n: (n, 0))],
        out_shape=[jax.ShapeDtypeStruct((T, D), BF16)],
        semantics=("parallel",), args=(sink, qkv, qkv, qkv, qkv), comm=comm)
    return o, got


def mm_res(a, w, resid, bias, name):
    T, K = a.shape
    tm = _tile(T, 512)

    def body(a_ref, w_ref, r_ref, b_ref, o_ref):
        o_ref[...] = _dot(a_ref[...], w_ref[...]) + b_ref[...] + r_ref[...]

    row = lambda i: (i, 0)
    full = lambda i: (0, 0)
    return pl.pallas_call(
        body, name=name, grid=(T // tm,),
        in_specs=[pl.BlockSpec((tm, K), row), pl.BlockSpec((K, D), full), pl.BlockSpec((tm, D), row),
                  pl.BlockSpec((1, D), full)],
        out_specs=pl.BlockSpec((tm, D), row),
        out_shape=jax.ShapeDtypeStruct((T, D), F32),
        compiler_params=_params("parallel"),
    )(a, w, resid, bias)


def rms_mm_gate(x, gain, w, bias, H, swiglu, act_dtype, name, comm=None):
    T = x.shape[0]
    tm = _tile(T, 512)
    tn = 1408 if H % 1408 == 0 else H
    nj = H // tn
    hw = D // nj

    def body(x_ref, g_ref, w1_ref, w2_ref, b1_ref, b2_ref, h_ref, pre_ref, act_ref):
        h = _rms_fwd(x_ref[...], g_ref[...]).astype(BF16)
        for jj in range(nj):
            @pl.when(pl.program_id(0) == jj)
            def _():
                h_ref[...] = h[:, jj * hw:(jj + 1) * hw]

        for cs in _col_chunks(tn):
            a = _dot(h, w1_ref[:, cs]) + b1_ref[:, cs]
            b = _dot(h, w2_ref[:, cs]) + b2_ref[:, cs]
            pre_ref[0, :, cs] = a.astype(BF16)
            pre_ref[1, :, cs] = b.astype(BF16)
            if swiglu:
                act = a * _sigmoid(a) * b
            else:
                act = a * _sigmoid(b)
            act_ref[:, cs] = act.astype(act_dtype)

    (h, pre, act), got = _call(
        body, name=name, grid=(nj, T // tm),
        in_specs=[pl.BlockSpec((tm, D), lambda j, i: (i, 0)), pl.BlockSpec((1, D), lambda j, i: (0, 0)),
                  pl.BlockSpec((D, tn), lambda j, i: (0, j)), pl.BlockSpec((D, tn), lambda j, i: (0, nj + j)),
                  pl.BlockSpec((1, tn), lambda j, i: (0, j)), pl.BlockSpec((1, tn), lambda j, i: (0, nj + j))],
        out_specs=[pl.BlockSpec((tm, hw), lambda j, i: (i, j)), pl.BlockSpec((2, tm, tn), lambda j, i: (0, i, j)),
                   pl.BlockSpec((tm, tn), lambda j, i: (i, j))],
        out_shape=[jax.ShapeDtypeStruct((T, D), BF16), jax.ShapeDtypeStruct((2, T, H), BF16),
                   jax.ShapeDtypeStruct((T, H), act_dtype)],
        semantics=("parallel", "parallel"), args=(x, gain, w, w, bias, bias), comm=comm)
    return h, pre, act, got


def _conv_tiles(T):
    tt = _tile(T, 512)
    return tt, tt // SUBLANES, D // LANES


def _fill_strided(ext, p, L):
    def ibody(i, carry):
        ext[i] = p[pl.ds(i + 1, SUBLANES, stride=L), :]
        return carry

    lax.fori_loop(0, L + CONV_W - 1, ibody, 0, unroll=2)


def _conv_specs(T, tt):
    main = pl.BlockSpec((tt, D), lambda i: (i, 0))
    per = tt // HALO
    prev = pl.BlockSpec((HALO, D), lambda i: (jnp.maximum(i * per - 1, 0), 0))
    nxt = pl.BlockSpec((HALO, D), lambda i: (jnp.minimum((i + 1) * per, T // HALO - 1), 0))
    return main, prev, nxt


def _fill_pad(pad, main_ref, prev_ref, next_ref, i, n_i, tt, nlt):
    keep_p = (i > 0).astype(F32)
    keep_n = (i < n_i - 1).astype(F32)
    for lt in range(nlt):
        sl = slice(lt * LANES, (lt + 1) * LANES)
        pad[lt, 0:HALO, :] = prev_ref[:, sl] * keep_p
        pad[lt, HALO:HALO + tt, :] = main_ref[:, sl]
        pad[lt, HALO + tt:2 * HALO + tt, :] = next_ref[:, sl] * keep_n


def conv_fwd(glu, w_dw, b_dw, ln_g, ln_b, comm=None):
    T = glu.shape[0]
    tt, L, nlt = _conv_tiles(T)
    n_i = T // tt
    main, prev, nxt = _conv_specs(T, tt)

    def body(x_ref, xp_ref, xn_ref, w_ref, b_ref, g_ref, bb_ref, dwc_ref, sw_ref, pad, ob, ext, wk):
        i = pl.program_id(0)
        _fill_pad(pad, x_ref, xp_ref, xn_ref, i, n_i, tt, nlt)
        for lt in range(nlt):
            sl = slice(lt * LANES, (lt + 1) * LANES)
            o = ob.at[lt]
            _fill_strided(ext, pad.at[lt], L)
            for k in range(CONV_W):
                wk[k] = jnp.broadcast_to(w_ref[k:k + 1, sl], (SUBLANES, LANES))

            def jbody(jb, carry):
                j = jb * CONV_JB
                accs = [None] * CONV_JB
                for m in range(CONV_W + CONV_JB - 1):
                    e = ext[j + m]
                    for u in range(CONV_JB):
                        if 0 <= m - u < CONV_W:
                            t = e * wk[m - u]
                            accs[u] = t if accs[u] is None else accs[u] + t
                for u in range(CONV_JB):
                    o[pl.ds(j + u, SUBLANES, stride=L), :] = accs[u]
                return carry

            lax.fori_loop(0, L // CONV_JB, jbody, 0)
        y = jnp.concatenate([ob[lt] for lt in range(nlt)], axis=1) + b_ref[...]
        dwc_ref[...] = y
        mu = jnp.mean(y, axis=-1, keepdims=True)
        yc = y - mu
        var = jnp.mean(yc * yc, axis=-1, keepdims=True)
        z = yc * lax.rsqrt(var + EPS) * g_ref[...] + bb_ref[...]
        sw_ref[...] = (z * _sigmoid(z)).astype(BF16)

    full = lambda i: (0, 0)
    (dwc, sw), got = _call(
        body, name="conv_fwd", grid=(n_i,),
        in_specs=[main, prev, nxt, pl.BlockSpec((32, D), full), pl.BlockSpec((1, D), full),
                  pl.BlockSpec((1, D), full), pl.BlockSpec((1, D), full)],
        out_specs=[pl.BlockSpec((tt, D), lambda i: (i, 0)), pl.BlockSpec((tt, D), lambda i: (i, 0))],
        out_shape=[jax.ShapeDtypeStruct((T, D), F32), jax.ShapeDtypeStruct((T, D), BF16)],
        scratch_shapes=[pltpu.VMEM((nlt, tt + 2 * HALO, LANES), F32), pltpu.VMEM((nlt, tt, LANES), F32),
                        pltpu.VMEM((L + 2 * HALO, SUBLANES, LANES), F32), pltpu.VMEM((32, SUBLANES, LANES), F32)],
        semantics=("parallel",), args=(glu, glu, glu, w_dw, b_dw, ln_g, ln_b), comm=comm)
    return dwc, sw, got


def mm_res_loss(a, w, resid, gain, target):
    T, K = a.shape
    tm = _tile(T, 512)

    def body(a_ref, w_ref, r_ref, g_ref, t_ref, dx_ref, loss_ref, dg_ref):
        @pl.when(pl.program_id(0) == 0)
        def _():
            loss_ref[...] = jnp.zeros_like(loss_ref)
            dg_ref[...] = jnp.zeros_like(dg_ref)

        xv, gain_v = _dot(a_ref[...], w_ref[...]) + r_ref[...], g_ref[...]
        err = _rms_fwd(xv, gain_v) - t_ref[...]
        part = 0.5 * jnp.sum(jnp.mean(err * err, axis=-1, keepdims=True), axis=0, keepdims=True)
        loss_ref[...] += jnp.broadcast_to(part, loss_ref.shape)
        dx, dgr = _rms_bwd(err * (1.0 / D), xv, gain_v, 0.0)
        dx_ref[...] = dx
        dg_ref[...] += jnp.sum(dgr, axis=0, keepdims=True)

    row = lambda i: (i, 0)
    full = lambda i: (0, 0)
    return pl.pallas_call(
        body, name="ffn1_down_loss", grid=(T // tm,),
        in_specs=[pl.BlockSpec((tm, K), row), pl.BlockSpec((K, D), full), pl.BlockSpec((tm, D), row),
                  pl.BlockSpec((1, D), full), pl.BlockSpec((tm, D), row)],
        out_specs=[pl.BlockSpec((tm, D), row), pl.BlockSpec((1, LANES), full), pl.BlockSpec((1, D), full)],
        out_shape=[jax.ShapeDtypeStruct((T, D), F32), jax.ShapeDtypeStruct((1, LANES), F32),
                   jax.ShapeDtypeStruct((1, D), F32)],
        compiler_params=_params("arbitrary"),
    )(a, w, resid, gain, target)


def swiglu_bwd(dx, w_down, pre, name, comm=None):
    T = dx.shape[0]
    H = w_down.shape[0]
    tm = _tile(T, 512)
    tn = 1408
    nj = H // tn

    def body(dx_ref, w_ref, pre_ref, dpre_ref):
        dxb = dx_ref[...].astype(BF16)
        for cs in _col_chunks(tn):
            dact = _dot_tb(dxb, w_ref[cs, :])
            g = pre_ref[0, :, cs].astype(F32)
            u = pre_ref[1, :, cs].astype(F32)
            sg = _sigmoid(g)
            dpre_ref[0, :, cs] = (dact * u * sg * (1.0 + g * (1.0 - sg))).astype(BF16)
            dpre_ref[1, :, cs] = (dact * g * sg).astype(BF16)

    (dpre,), got = _call(
        body, name=name, grid=(nj, T // tm),
        in_specs=[pl.BlockSpec((tm, D), lambda j, i: (i, 0)), pl.BlockSpec((tn, D), lambda j, i: (j, 0)),
                  pl.BlockSpec((2, tm, tn), lambda j, i: (0, i, j))],
        out_specs=[pl.BlockSpec((2, tm, tn), lambda j, i: (0, i, j))],
        out_shape=[jax.ShapeDtypeStruct((2, T, H), BF16)],
        semantics=("parallel", "parallel"), args=(dx, w_down, pre), comm=comm)
    return dpre, got


def mm_bt_rmsbwd(dpre, w, x, gain, dres, name, comm=None):
    nh, T, H = dpre.shape
    tm = _tile(T, 512)
    tk = 1408 if H % 1408 == 0 else (1024 if H % 1024 == 0 else H)
    nk = H // tk

    def body(dp_ref, w_ref, x_ref, g_ref, dres_ref, dx_ref, dg_ref, acc):
        i, hf, kk = pl.program_id(0), pl.program_id(1), pl.program_id(2)

        @pl.when((i == 0) & (hf == 0) & (kk == 0))
        def _():
            dg_ref[...] = jnp.zeros_like(dg_ref)

        @pl.when((hf == 0) & (kk == 0))
        def _():
            acc[...] = jnp.zeros_like(acc)

        cols = pl.ds(pl.multiple_of((hf * nk + kk) * tk, LANES), tk)
        acc[...] += _dot_tb(dp_ref[...], w_ref[:, cols])

        @pl.when((hf == nh - 1) & (kk == nk - 1))
        def _():
            dx, dgr = _rms_bwd(acc[...], x_ref[...], g_ref[...], dres_ref[...])
            dx_ref[...] = dx
            dg_ref[...] += jnp.sum(dgr, axis=0, keepdims=True)

    (dx, dg), got = _call(
        body, name=name, grid=(T // tm, nh, nk),
        in_specs=[pl.BlockSpec((None, tm, tk), lambda i, hf, kk: (hf, i, kk)),
                  pl.BlockSpec((D, nh * H), lambda i, hf, kk: (0, 0), pipeline_mode=pl.Buffered(1)),
                  pl.BlockSpec((tm, D), lambda i, hf, kk: (i, 0)), pl.BlockSpec((1, D), lambda i, hf, kk: (0, 0)),
                  pl.BlockSpec((tm, D), lambda i, hf, kk: (i, 0))],
        out_specs=[pl.BlockSpec((tm, D), lambda i, hf, kk: (i, 0)), pl.BlockSpec((1, D), lambda i, hf, kk: (0, 0))],
        out_shape=[jax.ShapeDtypeStruct((T, D), F32), jax.ShapeDtypeStruct((1, D), F32)],
        scratch_shapes=[pltpu.VMEM((tm, D), F32)],
        semantics=("arbitrary", "arbitrary", "arbitrary"), args=(dpre, w, x, gain, dres), comm=comm)
    return dx, dg, got


def dw_col(a, dpre, name):
    T = a.shape[0]
    nh, _, H = dpre.shape
    per = nh * H // N_CHIPS
    bph = N_CHIPS // nh
    tt = _tile(T, 1024)
    nt = T // tt

    def body(a_ref, b_ref, o_ref, acc):
        t = pl.program_id(1)

        @pl.when(t == 0)
        def _():
            acc[...] = jnp.zeros_like(acc)

        acc[...] += _dot_ta(a_ref[...], b_ref[...])

        @pl.when(t == nt - 1)
        def _():
            o_ref[...] = acc[...].astype(BF16)

    return pl.pallas_call(
        body, name=name, grid=(N_CHIPS, nt),
        in_specs=[pl.BlockSpec((tt, D), lambda q, t: (t, 0)),
                  pl.BlockSpec((None, tt, per), lambda q, t: (q // bph, t, q % bph))],
        out_specs=pl.BlockSpec((None, D, per), lambda q, t: (q, 0, 0)),
        out_shape=jax.ShapeDtypeStruct((N_CHIPS, D, per), BF16),
        scratch_shapes=[pltpu.VMEM((D, per), F32)],
        compiler_params=_params("parallel", "arbitrary"),
    )(a, dpre)


def dw_row(a, b, name):
    T, R = a.shape
    cw = 1408 if R % 1408 == 0 else 512
    tt = _tile(T, 1024)
    nt = T // tt

    def body(a_ref, b_ref, o_ref, acc):
        t = pl.program_id(1)

        @pl.when(t == 0)
        def _():
            acc[...] = jnp.zeros_like(acc)

        acc[...] += _dot_ta(a_ref[...], b_ref[...].astype(BF16))

        @pl.when(t == nt - 1)
        def _():
            o_ref[...] = acc[...].astype(BF16)

    out = pl.pallas_call(
        body, name=name, grid=(R // cw, nt),
        in_specs=[pl.BlockSpec((tt, cw), lambda q, t: (t, q)), pl.BlockSpec((tt, D), lambda q, t: (t, 0))],
        out_specs=pl.BlockSpec((cw, D), lambda q, t: (q, 0)),
        out_shape=jax.ShapeDtypeStruct((R, D), BF16),
        scratch_shapes=[pltpu.VMEM((cw, D), F32)],
        compiler_params=_params("parallel", "arbitrary"),
    )(a, b)
    return out.reshape(N_CHIPS, R // N_CHIPS, D)


def ln_silu_bwd(dx, w_pw2, dwc, ln_g, ln_b):
    T = dx.shape[0]
    tm = _tile(T, 512)

    def body(dx_ref, w_ref, y_ref, g_ref, b_ref, dy_ref, dg_ref, db_ref, dbo_ref):
        @pl.when(pl.program_id(0) == 0)
        def _():
            dg_ref[...] = jnp.zeros_like(dg_ref)
            db_ref[...] = jnp.zeros_like(db_ref)
            dbo_ref[...] = jnp.zeros_like(dbo_ref)

        dxv = dx_ref[...]
        dsw = _dot_tb(dxv.astype(BF16), w_ref[...])
        y = y_ref[...]
        mu = jnp.mean(y, axis=-1, keepdims=True)
        yc = y - mu
        rstd = lax.rsqrt(jnp.mean(yc * yc, axis=-1, keepdims=True) + EPS)
        xhat = yc * rstd
        z = xhat * g_ref[...] + b_ref[...]
        sg = _sigmoid(z)
        dz = dsw * sg * (1.0 + z * (1.0 - sg))
        dxh = dz * g_ref[...]
        dy_ref[...] = rstd * (dxh - jnp.mean(dxh, axis=-1, keepdims=True)
                              - xhat * jnp.mean(dxh * xhat, axis=-1, keepdims=True))
        dg_ref[...] += jnp.sum(dz * xhat, axis=0, keepdims=True)
        db_ref[...] += jnp.sum(dz, axis=0, keepdims=True)
        dbo_ref[...] += jnp.sum(dxv, axis=0, keepdims=True)

    row = lambda i: (i, 0)
    full = lambda i: (0, 0)
    vec = pl.BlockSpec((1, D), full)
    return pl.pallas_call(
        body, name="ln_silu_bwd", grid=(T // tm,),
        in_specs=[pl.BlockSpec((tm, D), row), pl.BlockSpec((D, D), full), pl.BlockSpec((tm, D), row), vec, vec],
        out_specs=[pl.BlockSpec((tm, D), row), vec, vec, vec],
        out_shape=[jax.ShapeDtypeStruct((T, D), F32)] + [jax.ShapeDtypeStruct((1, D), F32)] * 3,
        compiler_params=_params("arbitrary"),
    )(dx, w_pw2, dwc, ln_g, ln_b)


def conv_bwd(ddwc, glu, pre, w_dw, comm=None):
    T = ddwc.shape[0]
    tt, L, nlt = _conv_tiles(T)
    n_i = T // tt
    main, prev, nxt = _conv_specs(T, tt)

    def body(d_ref, dp_ref, dn_ref, x_ref, xp_ref, xn_ref, pre_ref, w_ref,
             dpre_ref, dw_ref, dbd_ref, dbp_ref, padd, padx, ob, extd, extx, wk):
        i = pl.program_id(0)

        @pl.when(i == 0)
        def _():
            dw_ref[...] = jnp.zeros_like(dw_ref)
            dbd_ref[...] = jnp.zeros_like(dbd_ref)
            dbp_ref[...] = jnp.zeros_like(dbp_ref)

        _fill_pad(padd, d_ref, dp_ref, dn_ref, i, n_i, tt, nlt)
        _fill_pad(padx, x_ref, xp_ref, xn_ref, i, n_i, tt, nlt)
        for lt in range(nlt):
            sl = slice(lt * LANES, (lt + 1) * LANES)
            o = ob.at[lt]
            _fill_strided(extd, padd.at[lt], L)
            _fill_strided(extx, padx.at[lt], L)
            for k in range(CONV_W):
                wk[k] = jnp.broadcast_to(w_ref[k:k + 1, sl], (SUBLANES, LANES))

            nu = CONV_JB_BWD

            def jbody(jb, accs):
                j = jb * nu
                accs = list(accs)
                d = [extd[j + u + CONV_PAD] for u in range(nu)]
                g = [None] * nu
                for m in range(CONV_W + nu - 1):
                    ed = extd[j + 2 * CONV_PAD + nu - 1 - m]
                    ex = extx[j + m]
                    for u in range(nu):
                        k = m - (nu - 1 - u)
                        if 0 <= k < CONV_W:
                            t = ed * wk[k]
                            g[u] = t if g[u] is None else g[u] + t
                        k = m - u
                        if 0 <= k < CONV_W:
                            accs[k] = accs[k] + d[u] * ex
                for u in range(nu):
                    o[pl.ds(j + u, SUBLANES, stride=L), :] = g[u]
                return tuple(accs)

            accs = lax.fori_loop(0, L // nu, jbody, tuple(jnp.zeros((SUBLANES, LANES), F32) for _ in range(CONV_W)))
            for k in range(CONV_W):
                dw_ref[k:k + 1, sl] += jnp.sum(accs[k], axis=0, keepdims=True)
        dglu = jnp.concatenate([ob[lt] for lt in range(nlt)], axis=1)
        a = pre_ref[0].astype(F32)
        gate = pre_ref[1].astype(F32)
        sg = _sigmoid(gate)
        da = dglu * sg
        dgate = dglu * a * sg * (1.0 - sg)
        dpre_ref[0] = da.astype(BF16)
        dpre_ref[1] = dgate.astype(BF16)
        dbd_ref[...] += jnp.sum(d_ref[...], axis=0, keepdims=True)
        dbp_ref[0] += jnp.sum(da, axis=0, keepdims=True)
        dbp_ref[1] += jnp.sum(dgate, axis=0, keepdims=True)

    full = lambda i: (0, 0)
    (dpre, dw, dbd, dbp), got = _call(
        body, name="conv_bwd", grid=(n_i,),
        in_specs=[main, prev, nxt, main, prev, nxt, pl.BlockSpec((2, tt, D), lambda i: (0, i, 0)),
                  pl.BlockSpec((32, D), full)],
        out_specs=[pl.BlockSpec((2, tt, D), lambda i: (0, i, 0)), pl.BlockSpec((32, D), full),
                   pl.BlockSpec((1, D), full), pl.BlockSpec((2, 1, D), lambda i: (0, 0, 0))],
        out_shape=[jax.ShapeDtypeStruct((2, T, D), BF16), jax.ShapeDtypeStruct((32, D), F32),
                   jax.ShapeDtypeStruct((1, D), F32), jax.ShapeDtypeStruct((2, 1, D), F32)],
        scratch_shapes=[pltpu.VMEM((nlt, tt + 2 * HALO, LANES), F32), pltpu.VMEM((nlt, tt + 2 * HALO, LANES), F32),
                        pltpu.VMEM((nlt, tt, LANES), F32), pltpu.VMEM((L + 2 * HALO, SUBLANES, LANES), F32),
                        pltpu.VMEM((L + 2 * HALO, SUBLANES, LANES), F32), pltpu.VMEM((32, SUBLANES, LANES), F32)],
        semantics=("arbitrary",), args=(ddwc, ddwc, ddwc, glu, glu, glu, pre, w_dw), comm=comm)
    return dpre, dw, dbd, dbp, got


def mm_bt(a, w, name):
    T = a.shape[0]
    N = w.shape[0]
    tm = _tile(T, 512)

    def body(a_ref, w_ref, o_ref):
        o_ref[...] = _dot_tb(a_ref[...].astype(BF16), w_ref[...]).astype(BF16)

    return pl.pallas_call(
        body, name=name, grid=(T // tm,),
        in_specs=[pl.BlockSpec((tm, D), lambda i: (i, 0)), pl.BlockSpec((N, D), lambda i: (0, 0))],
        out_specs=pl.BlockSpec((tm, N), lambda i: (i, 0)),
        out_shape=jax.ShapeDtypeStruct((T, N), BF16),
        compiler_params=_params("parallel"),
    )(a, w)


def attn_bwd(qkv, o, do, sink, rc, rs1, rs2, comm=None):
    T = qkv.shape[0]
    nb, q_spec, prev, own, nxt = _attn_specs(T)
    kvw = N_KV * HD

    def body(sink_ref, q_ref, kp_ref, ko_ref, kn_ref, o_ref, do_ref, c_ref, s1_ref, s2_ref,
             dq_ref, dkc_ref, dvc_ref, dsink_ref):
        n = pl.program_id(0)

        @pl.when(n == 0)
        def _():
            dsink_ref[...] = jnp.zeros_like(dsink_ref)

        valid = _attn_mask(n, T)
        kv = jnp.concatenate([kp_ref[...], ko_ref[...], kn_ref[...]], axis=0)
        kx, vx = _kv_padded(kv, 0), _kv_padded(kv, 2)
        tile = lambda ref, j: ref[:, j * LANES:(j + 1) * LANES]
        ss = [_dot_tb(kx[h // GROUP, h % 2], tile(q_ref, h // 2)) for h in range(N_HEADS)]
        dps = [_dot_tb(vx[h // GROUP, h % 2], tile(do_ref, h // 2)) for h in range(N_HEADS)]
        low_d = lax.broadcasted_iota(jnp.int32, (LANES, BLK), 0) < HD
        deltas = []
        for j in range(N_HEADS // 2):
            prod_t = tile(do_ref, j).astype(F32).T * tile(o_ref, j).astype(F32).T
            deltas.append(jnp.sum(jnp.where(low_d, prod_t, 0.0), axis=0, keepdims=True))
            deltas.append(jnp.sum(jnp.where(low_d, 0.0, prod_t), axis=0, keepdims=True))
        lane = lax.broadcasted_iota(jnp.int32, (1, N_HEADS), 1)
        dsink = jnp.zeros((1, N_HEADS), F32)
        pbs, dss = [], []
        for h in range(N_HEADS):
            p, p_sink = _softmax_sink(ss[h], valid, sink_ref[h])
            dss.append((p * (dps[h] - deltas[h])).astype(BF16))
            pbs.append(p.astype(BF16))
            part = -jnp.sum(p_sink * deltas[h], axis=1, keepdims=True)
            dsink = dsink + jnp.where(lane == h, part, 0.0)
        dsink_ref[...] += dsink
        c, s1, s2 = c_ref[...], s1_ref[...], s2_ref[...]
        kxt = {k: v.T for k, v in kx.items()}
        for j in range(N_HEADS // 2):
            g = 2 * j // GROUP
            dq_t = _dot(kxt[g, 0], dss[2 * j]) + _dot(kxt[g, 1], dss[2 * j + 1])
            dq_ref[:, j * LANES:(j + 1) * LANES] = (_rope(dq_t.T, c, -s1, -s2) * Q_SCALE).astype(BF16)
        low_k = lax.broadcasted_iota(jnp.int32, (3 * BLK, LANES), 1) < HD
        cols = lambda xs, g, p: jnp.concatenate([xs[GROUP * g + p], xs[GROUP * g + 2 + p]], axis=1)
        for t in range(N_KV // 2):
            sums = {}
            for g in (2 * t, 2 * t + 1):
                q2 = jnp.concatenate([tile(q_ref, 2 * g), tile(q_ref, 2 * g + 1)], axis=0)
                do2 = jnp.concatenate([tile(do_ref, 2 * g), tile(do_ref, 2 * g + 1)], axis=0)
                for p in range(2):
                    sums[g, p] = (_dot(cols(dss, g, p), q2), _dot(cols(pbs, g, p), do2))
            for which, ref in ((0, dkc_ref), (1, dvc_ref)):
                keep = jnp.where(low_k, sums[2 * t, 0][which], sums[2 * t + 1, 1][which])
                swap = jnp.where(low_k, sums[2 * t + 1, 0][which], sums[2 * t, 1][which])
                ref[:, t * LANES:(t + 1) * LANES] = keep + pltpu.roll(swap, HD, 1)

    row = lambda n: (n, 0)
    (dq, dkc, dvc, dsink), got = _call(
        body, name="attn_bwd", grid=(nb,),
        in_specs=[pl.BlockSpec(memory_space=pltpu.SMEM), q_spec, prev, own, nxt,
                  pl.BlockSpec((BLK, D), row), pl.BlockSpec((BLK, D), row), *_tab_specs(BLK)],
        out_specs=[pl.BlockSpec((BLK, D), row), pl.BlockSpec((None, 3 * BLK, kvw), lambda n: (n, 0, 0)),
                   pl.BlockSpec((None, 3 * BLK, kvw), lambda n: (n, 0, 0)), pl.BlockSpec((1, N_HEADS), lambda n: (0, 0))],
        out_shape=[jax.ShapeDtypeStruct((T, QKV), BF16), jax.ShapeDtypeStruct((nb, 3 * BLK, kvw), F32),
                   jax.ShapeDtypeStruct((nb, 3 * BLK, kvw), F32), jax.ShapeDtypeStruct((1, N_HEADS), F32)],
        semantics=("arbitrary",), args=(sink, qkv, qkv, qkv, qkv, o, do, rc, rs1, rs2), comm=comm)
    return dq, dkc, dvc, dsink, got


def kv_sum(dqkv, dkc, dvc, rc, rs1, rs2):
    nb = dkc.shape[0]
    T = nb * BLK
    kvw = N_KV * HD

    def body(_, kp_ref, ko_ref, kn_ref, vp_ref, vo_ref, vn_ref, c_ref, s1_ref, s2_ref, out_ref):
        m = pl.program_id(0)
        has_p = (m > 0).astype(F32)
        has_n = (m < nb - 1).astype(F32)
        dk = kp_ref[...] * has_p + ko_ref[...] + kn_ref[...] * has_n
        dv = vp_ref[...] * has_p + vo_ref[...] + vn_ref[...] * has_n
        c, s1, s2 = c_ref[...], s1_ref[...], s2_ref[...]
        for j in range(kvw // LANES):
            sl = slice(LANES * j, LANES * (j + 1))
            out_ref[:, sl] = _rope(dk[:, sl], c, -s1, -s2).astype(BF16)
        out_ref[:, kvw:] = dv.astype(BF16)

    from_prev = pl.BlockSpec((None, BLK, kvw), lambda m: (jnp.maximum(m - 1, 0), 2, 0))
    from_own = pl.BlockSpec((None, BLK, kvw), lambda m: (m, 1, 0))
    from_next = pl.BlockSpec((None, BLK, kvw), lambda m: (jnp.minimum(m + 1, nb - 1), 0, 0))
    return pl.pallas_call(
        body, name="kv_sum", grid=(nb,),
        in_specs=[pl.BlockSpec(memory_space=pl.ANY), from_prev, from_own, from_next, from_prev, from_own, from_next,
                  *_tab_specs(BLK)],
        out_specs=pl.BlockSpec((BLK, 2 * kvw), lambda m: (m, KV_OFF // (2 * kvw))),
        out_shape=jax.ShapeDtypeStruct((T, QKV), BF16),
        input_output_aliases={0: 0},
        compiler_params=_params("parallel"),
    )(dqkv, dkc, dkc, dkc, dvc, dvc, dvc, rc, rs1, rs2)


def _me():
    return lax.axis_index("x"), lax.axis_index("y"), lax.axis_index("c")


def _half_rows(ref, sharded_rows, chip, core):
    R, C = ref.shape[-2], ref.shape[-1]
    lead = (slice(None),) * (len(ref.shape) - 2)
    if sharded_rows:
        per = R // N_CHIPS
        return ref.at[lead + (pl.ds(chip * per + core * (per // 2), per // 2), slice(None))]
    per = C // N_CHIPS
    return ref.at[lead + (pl.ds(core * (R // 2), R // 2), pl.ds(chip * per, per))]


class _Gather:
    def __init__(self, shards, sharded_rows):
        self.inputs = list(shards)
        self.rows = list(sharded_rows)
        self.n = self.n_in = self.n_out = len(shards)
        self.out_shapes = []
        for s, rows in zip(shards, sharded_rows):
            shp = list(s.shape)
            shp[-2 if rows else -1] *= N_CHIPS
            self.out_shapes.append(jax.ShapeDtypeStruct(tuple(shp), s.dtype))
        self.scratch = [pltpu.SemaphoreType.DMA((self.n, 6)), pltpu.SemaphoreType.DMA((self.n, 6)),
                        pltpu.SemaphoreType.DMA((self.n, 2))]

    def _ctx(self, ins, outs, sems):
        send_sems, recv_sems, local_sems = sems
        x, y, c = _me()
        chips = [(1 - x, y), (x, 1 - y), (1 - x, 1 - y)]

        def half_src(w, core):
            s = ins[w]
            R = s.shape[-2]
            return s.at[pl.ds(core * (R // 2), R // 2), :]

        def dst(w, chip, core):
            return _half_rows(outs[w], self.rows[w], chip, core)

        def copy(w, k, src, chip, core, to):
            return pltpu.make_async_remote_copy(
                src_ref=src, dst_ref=dst(w, chip, core), send_sem=send_sems.at[w, k], recv_sem=recv_sems.at[w, k],
                device_id=to, device_id_type=MESH)

        def local(w, core):
            return pltpu.make_async_copy(half_src(w, core), dst(w, 2 * x + y, core), local_sems.at[w, core])

        def first(w, j):
            qx, qy = chips[j]
            return copy(w, j, half_src(w, c), 2 * x + y, c, (qx, qy, c))

        def landed(w, j):
            qx, qy = chips[j]
            return copy(w, j, dst(w, 2 * qx + qy, c), 2 * qx + qy, c, (x, y, c))

        def passed(w, j):
            qx, qy = chips[j]
            return copy(w, 3 + j, dst(w, 2 * qx + qy, c), 2 * qx + qy, c, (x, y, 1 - c))

        def from_sibling(w, j):
            qx, qy = chips[j]
            return copy(w, 3 + j, dst(w, 2 * qx + qy, 1 - c), 2 * qx + qy, 1 - c, (x, y, c))

        return local, first, landed, passed, from_sibling

    def start(self, ins, outs, sems):
        local, first, _, _, _ = self._ctx(ins, outs, sems)
        for w in range(self.n):
            for core in range(2):
                local(w, core).start()
            for j in range(3):
                first(w, j).start()

    def mid(self, ins, outs, sems):
        _, _, landed, passed, _ = self._ctx(ins, outs, sems)
        for w in range(self.n):
            for j in range(3):
                landed(w, j).wait_recv()
                passed(w, j).start()

    def end(self, ins, outs, sems):
        local, first, _, passed, from_sibling = self._ctx(ins, outs, sems)
        for w in range(self.n):
            for j in range(3):
                from_sibling(w, j).wait_recv()
        for w in range(self.n):
            for j in range(3):
                first(w, j).wait_send()
                passed(w, j).wait_send()
            for core in range(2):
                local(w, core).wait()


class _Scatter:
    def __init__(self, grads, small=None):
        self.inputs = list(grads) + ([small] if small is not None else [])
        self.ng = len(grads)
        self.n = self.n_in = self.n_out = len(self.inputs)
        self.out_shapes = [jax.ShapeDtypeStruct((N_DEV, g.shape[1] // 2, g.shape[2]), g.dtype) for g in grads]
        if small is not None:
            self.out_shapes.append(jax.ShapeDtypeStruct((N_DEV,) + small.shape, small.dtype))
        self.scratch = [pltpu.SemaphoreType.DMA((self.n, N_DEV)), pltpu.SemaphoreType.DMA((self.n, N_DEV)),
                        pltpu.SemaphoreType.DMA((self.n,))]

    def _ctx(self, ins, outs, sems):
        send_sems, recv_sems, local_sems = sems
        x, y, c = _me()
        me = 4 * x + 2 * y + c

        def piece(w, chip, core):
            if w >= self.ng:
                return ins[w]
            half = ins[w].shape[1] // 2
            return ins[w].at[chip, pl.ds(core * half, half), :]

        def peer_of(k):
            return x ^ ((k >> 2) & 1), y ^ ((k >> 1) & 1), c ^ (k & 1)

        def local(w):
            return pltpu.make_async_copy(piece(w, 2 * x + y, c), outs[w].at[me], local_sems.at[w])

        def send(w, k):
            px, py, pc = peer_of(k)
            return pltpu.make_async_remote_copy(
                src_ref=piece(w, 2 * px + py, pc), dst_ref=outs[w].at[me], send_sem=send_sems.at[w, k],
                recv_sem=recv_sems.at[w, k], device_id=(px, py, pc), device_id_type=MESH)

        def recv(w, k):
            px, py, pc = peer_of(k)
            return pltpu.make_async_remote_copy(
                src_ref=piece(w, 2 * x + y, c), dst_ref=outs[w].at[4 * px + 2 * py + pc], send_sem=send_sems.at[w, k],
                recv_sem=recv_sems.at[w, k], device_id=(px, py, pc), device_id_type=MESH)

        return local, send, recv

    def start(self, ins, outs, sems):
        local, send, _ = self._ctx(ins, outs, sems)
        for w in range(self.n):
            local(w).start()
            for k in range(1, N_DEV):
                send(w, k).start()

    def mid(self, ins, outs, sems):
        pass

    def end(self, ins, outs, sems):
        local, send, recv = self._ctx(ins, outs, sems)
        for w in range(self.n):
            for k in range(1, N_DEV):
                recv(w, k).wait_recv()
        for w in range(self.n):
            for k in range(1, N_DEV):
                send(w, k).wait_send()
            local(w).wait()


class _Both:
    def __init__(self, a, b):
        self.a, self.b = a, b
        self.inputs = a.inputs + b.inputs
        self.out_shapes = a.out_shapes + b.out_shapes
        self.scratch = a.scratch + b.scratch
        self.n_in, self.n_out = a.n_in + b.n_in, a.n_out + b.n_out

    def _split(self, ins, outs, sems):
        a, na = self.a, len(self.a.scratch)
        return (ins[:a.n_in], outs[:a.n_out], sems[:na]), (ins[a.n_in:], outs[a.n_out:], sems[na:])

    def start(self, ins, outs, sems):
        pa, pb = self._split(ins, outs, sems)
        self.a.start(*pa)
        self.b.start(*pb)

    def mid(self, ins, outs, sems):
        pa, pb = self._split(ins, outs, sems)
        self.a.mid(*pa)
        self.b.mid(*pb)

    def end(self, ins, outs, sems):
        pa, pb = self._split(ins, outs, sems)
        self.a.end(*pa)
        self.b.end(*pb)


def exchange(plan, name):
    def body(*refs):
        ins, outs, sems = refs[:plan.n_in], refs[plan.n_in:plan.n_in + plan.n_out], refs[plan.n_in + plan.n_out:]
        plan.start(ins, outs, sems)
        plan.mid(ins, outs, sems)
        plan.end(ins, outs, sems)

    any_spec = pl.BlockSpec(memory_space=pl.ANY)
    return pl.pallas_call(
        body, name=name, in_specs=[any_spec] * plan.n_in, out_specs=[any_spec] * plan.n_out,
        out_shape=plan.out_shapes, scratch_shapes=plan.scratch,
    )(*plan.inputs)


def _call(body, *, name, grid, in_specs, out_specs, out_shape, scratch_shapes=(), semantics, args, comm=None):
    if comm is None:
        outs = pl.pallas_call(
            body, name=name, grid=grid, in_specs=in_specs, out_specs=out_specs, out_shape=out_shape,
            scratch_shapes=list(scratch_shapes), compiler_params=_params(*semantics))(*args)
        return outs, []
    n_in, n_out, n_scr = len(in_specs), len(out_specs), len(scratch_shapes)

    total = math.prod(grid)
    first, middle, last = 0, (3 * total) // 4 - 1, total - 1
    assert first <= middle < last

    def at(step):
        lin = pl.program_id(0)
        for d in range(1, len(grid)):
            lin = lin * grid[d] + pl.program_id(d)
        return lin == step

    def hosted(*refs):
        h_in, c_in = refs[:n_in], refs[n_in:n_in + comm.n_in]
        rest = refs[n_in + comm.n_in:]
        h_out, c_out = rest[:n_out], rest[n_out:n_out + comm.n_out]
        rest = rest[n_out + comm.n_out:]
        h_scr, c_scr = rest[:n_scr], rest[n_scr:]

        @pl.when(at(first))
        def _():
            comm.start(c_in, c_out, c_scr)

        body(*h_in, *h_out, *h_scr)

        @pl.when(at(middle))
        def _():
            comm.mid(c_in, c_out, c_scr)

        @pl.when(at(last))
        def _():
            comm.end(c_in, c_out, c_scr)

    any_spec = pl.BlockSpec(memory_space=pl.ANY)
    outs = pl.pallas_call(
        hosted, name=name, grid=grid, in_specs=list(in_specs) + [any_spec] * comm.n_in,
        out_specs=list(out_specs) + [any_spec] * comm.n_out, out_shape=list(out_shape) + comm.out_shapes,
        scratch_shapes=list(scratch_shapes) + comm.scratch,
        compiler_params=_params(*(["arbitrary"] * len(grid))))(*args, *comm.inputs)
    return outs[:n_out], outs[n_out:]


def sum_swap(pieces, name):
    nl = len(pieces)
    _, r2, cc = pieces[0].shape
    tr = 128 if r2 % 128 == 0 else r2 // 2
    n = r2 // tr

    def body(*refs):
        p_refs, out = refs[:nl], refs[nl]
        slots, send_sems, local_sems, recv_sem = refs[nl + 1:]
        x, y, c = _me()
        sibling = (x, y, 1 - c)
        l, i = pl.program_id(0), pl.program_id(1)
        step = l * n + i

        def rows(st, core):
            return out.at[st // n, pl.ds(core * r2 + (st % n) * tr, tr), :]

        def copies(st):
            slot = st % 2
            local = pltpu.make_async_copy(slots.at[slot], rows(st, c), local_sems.at[slot])
            remote = pltpu.make_async_remote_copy(
                src_ref=slots.at[slot], dst_ref=rows(st, c), send_sem=send_sems.at[slot], recv_sem=recv_sem,
                device_id=sibling, device_id_type=MESH)
            return local, remote

        for ll in range(nl):
            @pl.when(l == ll)
            def _():
                acc = p_refs[ll][0].astype(F32)
                for d in range(1, N_DEV):
                    acc = acc + p_refs[ll][d].astype(F32)
                slots[step % 2] = acc

        for cp in copies(step):
            cp.start()

        @pl.when(step >= 1)
        def _():
            local, remote = copies(step - 1)
            local.wait()
            remote.wait_send()

        @pl.when(step == nl * n - 1)
        def _():
            local, remote = copies(step)
            local.wait()
            remote.wait_send()
            theirs = out.at[:, pl.ds((1 - c) * r2, r2), :]
            pltpu.make_async_remote_copy(src_ref=theirs, dst_ref=theirs, send_sem=send_sems.at[0],
                                         recv_sem=recv_sem, device_id=sibling, device_id_type=MESH).wait_recv()

    def piece_spec(ll):
        def index(l, i):
            return (0, jnp.where(l == ll, i, jnp.where(l < ll, 0, n - 1)), 0)
        return pl.BlockSpec((N_DEV, tr, cc), index)

    return pl.pallas_call(
        body, name=name, grid=(nl, n),
        in_specs=[piece_spec(ll) for ll in range(nl)],
        out_specs=pl.BlockSpec(memory_space=pl.ANY),
        out_shape=jax.ShapeDtypeStruct((nl, 2 * r2, cc), F32),
        scratch_shapes=[pltpu.VMEM((2, tr, cc), F32), pltpu.SemaphoreType.DMA((2,)), pltpu.SemaphoreType.DMA((2,)),
                        pltpu.SemaphoreType.DMA(())],
        compiler_params=_params("arbitrary", "arbitrary"),
    )(*pieces)


def sum_pieces(pieces, name):
    _, R, C = pieces.shape
    tr = _tile(R, 128) if R % 128 == 0 else R

    def body(p_ref, o_ref):
        acc = p_ref[0].astype(F32)
        for d in range(1, N_DEV):
            acc = acc + p_ref[d].astype(F32)
        o_ref[...] = acc

    return pl.pallas_call(
        body, name=name, grid=(R // tr,),
        in_specs=[pl.BlockSpec((N_DEV, tr, C), lambda i: (0, i, 0))],
        out_specs=pl.BlockSpec((tr, C), lambda i: (i, 0)),
        out_shape=jax.ShapeDtypeStruct((R, C), F32),
        compiler_params=_params("parallel"),
    )(pieces)


def adamw(w, g, m, v, name):
    Lyr, R, C = w.shape
    tr = _tile(R, 256) if R % 8 == 0 else R
    c1 = 1.0 / (1.0 - ADAM_B1 ** ADAM_STEP)
    c2 = 1.0 / (1.0 - ADAM_B2 ** ADAM_STEP)

    def body(w_ref, g_ref, m_ref, v_ref, d_ref, nm_ref, nv_ref):
        gv = g_ref[...]
        nm = ADAM_B1 * m_ref[...] + (1.0 - ADAM_B1) * gv
        nv = ADAM_B2 * v_ref[...] + (1.0 - ADAM_B2) * (gv * gv)
        nm_ref[...] = nm
        nv_ref[...] = nv
        d_ref[...] = -ADAM_LR * ((nm * c1) / (jnp.sqrt(nv * c2) + ADAM_EPS) + ADAM_WD * w_ref[...])

    spec = pl.BlockSpec((None, tr, C), lambda l, i: (l, i, 0))
    shp = jax.ShapeDtypeStruct(w.shape, F32)
    return pl.pallas_call(
        body, name=name, grid=(Lyr, R // tr),
        in_specs=[spec] * 4, out_specs=[spec] * 3, out_shape=[shp] * 3,
        compiler_params=_params("parallel", "parallel"),
    )(w, g, m, v)


def _rope_tables(T):
    pos = jnp.arange(T, dtype=F32)
    inv_freq = THETA ** (-jnp.arange(0, ROT, 2, dtype=F32) / ROT)
    ang = pos[:, None] * inv_freq[None, :]
    cs = jnp.concatenate([jnp.cos(ang), jnp.sin(ang)], axis=1)
    half = ROT // 2
    lane = jnp.arange(3 * LANES)
    table, lm = lane // LANES, lane % HD
    src = jnp.where(table == 0, lm % half, half + lm % half)
    i32 = lambda b: b.astype(jnp.int32)
    sign = jnp.where(table == 0, i32(lm < ROT), jnp.where(table == 1, -i32(lm < half), i32((lm >= half) & (lm < ROT))))
    place = (jnp.arange(ROT)[:, None] == src[None, :]) * sign[None, :].astype(F32)
    ones = ((table == 0) & (lm >= ROT)).astype(F32)
    return jnp.dot(cs, place, precision=lax.Precision.HIGHEST) + ones[None, :]


def _tab_specs(rows):
    return [pl.BlockSpec((rows, LANES), lambda i, k=k: (i, k)) for k in range(3)]


def kernel(x, attn_norm, attn_w_qkv, attn_w_o, attn_sink, conv_norm, conv_w_pw1, conv_b_pw1, conv_w_dw, conv_b_dw, conv_ln_g, conv_ln_b, conv_w_pw2, conv_b_pw2, ffn_norm, ffn_w_gu, ffn_w_down, final_norm, loss_target, m_attn_norm, m_attn_w_qkv, m_attn_w_o, m_attn_sink, m_conv_norm, m_conv_w_pw1, m_conv_b_pw1, m_conv_w_dw, m_conv_b_dw, m_conv_ln_g, m_conv_ln_b, m_conv_w_pw2, m_conv_b_pw2, m_ffn_norm, m_ffn_w_gu, m_ffn_w_down, m_final_norm, v_attn_norm, v_attn_w_qkv, v_attn_w_o, v_attn_sink, v_conv_norm, v_conv_w_pw1, v_conv_b_pw1, v_conv_w_dw, v_conv_b_dw, v_conv_ln_g, v_conv_ln_b, v_conv_w_pw2, v_conv_b_pw2, v_ffn_norm, v_ffn_w_gu, v_ffn_w_down, v_final_norm):
    T = x.shape[1]
    x0 = x[0]
    target = loss_target[0]
    ix, iy = lax.axis_index("x"), lax.axis_index("y")
    chip = 2 * ix + iy
    rc = rs1 = rs2 = _rope_tables(T)

    bf = lambda t: t.astype(BF16)
    col_row = [False, True]

    def place(vec, width):
        return lax.dynamic_update_slice(jnp.zeros((vec.shape[0], N_CHIPS * width), F32), vec, (0, chip * width))

    small_rows = jnp.concatenate([
        place(conv_norm, 256), place(conv_b_pw1, 512).reshape(2, D), place(conv_b_dw, 256), place(conv_ln_g, 256),
        place(conv_ln_b, 256), place(conv_b_pw2, 256), jnp.zeros((1, D), F32),
        place(conv_w_dw[0], 256), jnp.zeros((1, D), F32)], axis=0)
    w_qkv, w_o, got = exchange(_Both(_Gather([bf(attn_w_qkv[0]), bf(attn_w_o[0])], col_row),
                                     _Scatter([], small_rows)), "gather_attn")
    psmall = sum_pieces(got, "sum_small_params") * 0.5
    p_conv_norm, p_b_pw1 = psmall[0:1], psmall[1:3].reshape(1, 2 * D)
    p_b_dw, p_ln_g, p_ln_b, p_b_pw2 = psmall[3:4], psmall[4:5], psmall[5:6], psmall[6:7]
    p_w_dw = psmall[8:40]

    h0, qkv = rms_qkv(x0, attn_norm, w_qkv, rc, rs1, rs2)
    sink = attn_sink[0]
    o, (w_gu0,) = attn_fwd(qkv, sink, comm=_Gather([bf(ffn_w_gu[0])], [False]))
    zero_b = jnp.zeros((1, D), F32)
    x1 = mm_res(o, w_o, x0, zero_b, "attn_out")
    zero_gu = jnp.zeros((1, 2 * DFF), F32)
    h1, gu0, act0, (w_down0, w_pw1, w_pw2) = rms_mm_gate(
        x1, ffn_norm[0:1], w_gu0, zero_gu, DFF, True, BF16, "ffn0_up",
        comm=_Gather([bf(ffn_w_down[0]), bf(conv_w_pw1[0]), bf(conv_w_pw2[0])], [True, False, True]))
    x2 = mm_res(act0, w_down0, x1, zero_b, "ffn0_down")
    h2, pre, glu, _ = rms_mm_gate(x2, p_conv_norm, w_pw1, p_b_pw1, D, False, F32, "conv_pw1")
    dwc, sw, (w_gu1, w_down1) = conv_fwd(glu, p_w_dw, p_b_dw, p_ln_g, p_ln_b,
                                         comm=_Gather([bf(ffn_w_gu[1]), bf(ffn_w_down[1])], col_row))
    x3 = mm_res(sw, w_pw2, x2, p_b_pw2, "conv_pw2")
    h3, gu1, act1, _ = rms_mm_gate(x3, ffn_norm[1:2], w_gu1, zero_gu, DFF, True, BF16, "ffn1_up")
    dx4, loss_part, d_final = mm_res_loss(act1, w_down1, x3, final_norm.reshape(1, D), target)
    loss = lax.psum(loss_part[0, 0], ("x", "y", "c"))

    dgu1, _ = swiglu_bwd(dx4, w_down1, gu1, "ffn1_down_bwd")
    g_down1 = dw_row(act1, dx4, "ffn1_down_dw")
    dx3, d_ffn1, _ = mm_bt_rmsbwd(dgu1, w_gu1, x3, ffn_norm[1:2], dx4, "ffn1_up_bwd")
    g_gu1 = dw_col(h3, dgu1, "ffn1_up_dw")

    ddwc, d_ln_g, d_ln_b, d_b_pw2 = ln_silu_bwd(dx3, w_pw2, dwc, p_ln_g, p_ln_b)
    g_pw2 = dw_row(sw, dx3, "conv_pw2_dw")
    dpre, d_w_dw, d_b_dw, d_b_pw1, (r_gu1, r_down1) = conv_bwd(ddwc, glu, pre, p_w_dw,
                                                               comm=_Scatter([g_gu1, g_down1]))
    dx2, d_conv_norm, _ = mm_bt_rmsbwd(dpre, w_pw1, x2, p_conv_norm, dx3, "conv_pw1_bwd")
    g_pw1 = dw_col(h2, dpre, "conv_pw1_dw")

    dgu0, (r_pw1, r_pw2) = swiglu_bwd(dx2, w_down0, gu0, "ffn0_down_bwd", comm=_Scatter([g_pw1, g_pw2]))
    g_down0 = dw_row(act0, dx2, "ffn0_down_dw")
    dx1, d_ffn0, _ = mm_bt_rmsbwd(dgu0, w_gu0, x1, ffn_norm[0:1], dx2, "ffn0_up_bwd")
    g_gu0 = dw_col(h1, dgu0, "ffn0_up_dw")

    do = mm_bt(dx1, w_o, "attn_out_bwd")
    g_o = dw_row(o, dx1, "attn_out_dw")
    dq, dkc, dvc, d_sink, (r_gu0, r_down0, r_o) = attn_bwd(qkv, o, do, sink, rc, rs1, rs2,
                                                           comm=_Scatter([g_gu0, g_down0, g_o]))
    dqkv = kv_sum(dq, dkc, dvc, rc, rs1, rs2)[None]
    g_qkv = dw_col(h0, dqkv, "attn_qkv_dw")
    dx0, d_attn_norm, (r_qkv,) = mm_bt_rmsbwd(dqkv, w_qkv, x0, attn_norm, dx1, "attn_qkv_bwd",
                                              comm=_Scatter([g_qkv]))

    pad16 = lambda t: jnp.concatenate([t, jnp.zeros((1, D - t.shape[1]), F32)], axis=1)
    small_g = jnp.concatenate([
        d_attn_norm, pad16(d_sink), d_conv_norm, d_b_pw1.reshape(2, D), d_b_dw, d_ln_g, d_ln_b, d_b_pw2,
        d_ffn0, d_ffn1, d_final, jnp.zeros((4, D), F32), d_w_dw], axis=0)
    r_small, = exchange(_Scatter([], small_g), "scatter_small")
    gf_gu = sum_swap([r_gu0, r_gu1], "sum_gu")
    gf_down = sum_swap([r_down0, r_down1], "sum_down")
    gf_pw1, gf_pw2 = sum_swap([r_pw1], "sum_pw1"), sum_swap([r_pw2], "sum_pw2")
    gf_qkv, gf_o = sum_swap([r_qkv], "sum_qkv"), sum_swap([r_o], "sum_o")
    gs = sum_pieces(r_small, "sum_small_grads")

    def take(row0, nrows, width):
        return lax.dynamic_slice(gs, (row0, chip * width), (nrows, width))

    grads = {
        "attn_norm": gs[0:1], "attn_w_qkv": gf_qkv, "attn_w_o": gf_o, "attn_sink": gs[1:2, :N_HEADS],
        "conv_norm": take(2, 1, 256), "conv_w_pw1": gf_pw1,
        "conv_b_pw1": lax.dynamic_slice(gs[3:5].reshape(1, 2 * D), (0, chip * 512), (1, 512)),
        "conv_w_dw": take(16, 32, 256)[None, :CONV_W], "conv_b_dw": take(5, 1, 256), "conv_ln_g": take(6, 1, 256),
        "conv_ln_b": take(7, 1, 256), "conv_w_pw2": gf_pw2, "conv_b_pw2": take(8, 1, 256),
        "ffn_norm": gs[9:11], "ffn_w_gu": gf_gu, "ffn_w_down": gf_down, "final_norm": gs[11],
    }
    weights = dict(attn_norm=attn_norm, attn_w_qkv=attn_w_qkv, attn_w_o=attn_w_o, attn_sink=attn_sink,
                   conv_norm=conv_norm, conv_w_pw1=conv_w_pw1, conv_b_pw1=conv_b_pw1, conv_w_dw=conv_w_dw,
                   conv_b_dw=conv_b_dw, conv_ln_g=conv_ln_g, conv_ln_b=conv_ln_b, conv_w_pw2=conv_w_pw2,
                   conv_b_pw2=conv_b_pw2, ffn_norm=ffn_norm, ffn_w_gu=ffn_w_gu, ffn_w_down=ffn_w_down,
                   final_norm=final_norm)
    m_in = dict(attn_norm=m_attn_norm, attn_w_qkv=m_attn_w_qkv, attn_w_o=m_attn_w_o, attn_sink=m_attn_sink,
                conv_norm=m_conv_norm, conv_w_pw1=m_conv_w_pw1, conv_b_pw1=m_conv_b_pw1, conv_w_dw=m_conv_w_dw,
                conv_b_dw=m_conv_b_dw, conv_ln_g=m_conv_ln_g, conv_ln_b=m_conv_ln_b, conv_w_pw2=m_conv_w_pw2,
                conv_b_pw2=m_conv_b_pw2, ffn_norm=m_ffn_norm, ffn_w_gu=m_ffn_w_gu, ffn_w_down=m_ffn_w_down,
                final_norm=m_final_norm)
    v_in = dict(attn_norm=v_attn_norm, attn_w_qkv=v_attn_w_qkv, attn_w_o=v_attn_w_o, attn_sink=v_attn_sink,
                conv_norm=v_conv_norm, conv_w_pw1=v_conv_w_pw1, conv_b_pw1=v_conv_b_pw1, conv_w_dw=v_conv_w_dw,
                conv_b_dw=v_conv_b_dw, conv_ln_g=v_conv_ln_g, conv_ln_b=v_conv_ln_b, conv_w_pw2=v_conv_w_pw2,
                conv_b_pw2=v_conv_b_pw2, ffn_norm=v_ffn_norm, ffn_w_gu=v_ffn_w_gu, ffn_w_down=v_ffn_w_down,
                final_norm=v_final_norm)
    order = list(weights)
    g_out, d_out, m_out, v_out = [], [], [], []
    for nm in order:
        w = weights[nm]
        shape = w.shape
        as3 = lambda t: t.reshape((1,) * (3 - len(shape)) + shape) if len(shape) < 3 else t.reshape(shape)
        g3 = as3(grads[nm].reshape(shape))
        delta, nm_, nv_ = adamw(as3(w), g3, as3(m_in[nm]), as3(v_in[nm]), "adamw_" + nm)
        g_out.append(g3.reshape(shape))
        d_out.append(delta.reshape(shape))
        m_out.append(nm_.reshape(shape))
        v_out.append(nv_.reshape(shape))
    return (loss, dx0[None], *g_out, *d_out, *m_out, *v_out)
```

```python
import functools
import math

import jax
import jax.numpy as jnp
from jax import lax
from jax.experimental import pallas as pl
from jax.experimental.pallas import tpu as pltpu

F32 = jnp.float32
BF16 = jnp.bfloat16

D = 1024
N_HEADS = 16
N_KV = 4
GROUP = N_HEADS // N_KV
HD = 64
ROT = 16
THETA = 500000.0
BLK = 128
QKV = (N_HEADS + 2 * N_KV) * HD
KV_OFF = N_HEADS * HD
DFF = 2816
CONV_W = 31
CONV_PAD = 15
HALO = 16
CONV_JB = 8
CONV_JB_BWD = 4
EPS = 1e-6
NEG = -1e30
N_CHIPS = 4
N_DEV = 8
LANES = 128
SUBLANES = 8

ADAM_LR, ADAM_B1, ADAM_B2, ADAM_EPS, ADAM_WD, ADAM_STEP = 0.001, 0.9, 0.999, 1e-08, 0.01, 10

VMEM_LIMIT = 56 * 1024 * 1024
MESH = pl.DeviceIdType.MESH


def _params(*sem):
    return pltpu.CompilerParams(dimension_semantics=sem, vmem_limit_bytes=VMEM_LIMIT)


def _tile(n, want):
    if n <= want:
        return n
    for t in range(want, 7, -1):
        if n % t == 0 and t % 8 == 0:
            return t
    return n


MXU_COLS = 256


def _col_chunks(n):
    return [slice(c, min(c + MXU_COLS, n)) for c in range(0, n, MXU_COLS)]


def _sigmoid(v):
    return 1.0 / (1.0 + jnp.exp(-v))


def _rms_fwd(xv, gain):
    r = lax.rsqrt(jnp.mean(xv * xv, axis=-1, keepdims=True) + EPS)
    return xv * r * gain


def _rms_bwd(dh, xv, gain, dres):
    r = lax.rsqrt(jnp.mean(xv * xv, axis=-1, keepdims=True) + EPS)
    xhat = xv * r
    gy = dh * gain
    dx = r * (gy - xhat * jnp.mean(gy * xhat, axis=-1, keepdims=True))
    return dx + dres, dh * xhat


def _rope(blk, c, s1, s2):
    return blk * c + pltpu.roll(blk, LANES - ROT // 2, 1) * s1 + pltpu.roll(blk, ROT // 2, 1) * s2


def _dot(a, b):
    return jnp.dot(a, b, preferred_element_type=F32)


def _dot_tb(a, b):
    return lax.dot_general(a, b, (((1,), (1,)), ((), ())), preferred_element_type=F32)


def _dot_ta(a, b):
    return lax.dot_general(a, b, (((0,), (0,)), ((), ())), preferred_element_type=F32)


def rms_qkv(x, gain, w, rc, rs1, rs2):
    T = x.shape[0]
    tm = _tile(T, 512)

    def body(x_ref, g_ref, w_ref, c_ref, s1_ref, s2_ref, h_ref, qkv_ref):
        h = _rms_fwd(x_ref[...], g_ref[...]).astype(BF16)
        h_ref[...] = h
        acc = _dot(h, w_ref[...])
        c, s1, s2 = c_ref[...], s1_ref[...], s2_ref[...]
        n_rot = (KV_OFF + N_KV * HD) // LANES
        for j in range(n_rot):
            sl = slice(LANES * j, LANES * (j + 1))
            roped = _rope(acc[:, sl], c, s1, s2)
            if j < KV_OFF // LANES:
                roped = roped * Q_SCALE
            qkv_ref[:, sl] = roped.astype(BF16)
        qkv_ref[:, n_rot * LANES:] = acc[:, n_rot * LANES:].astype(BF16)

    row = lambda i: (i, 0)
    full = lambda i: (0, 0)
    return pl.pallas_call(
        body, name="rms_qkv", grid=(T // tm,),
        in_specs=[pl.BlockSpec((tm, D), row), pl.BlockSpec((1, D), full), pl.BlockSpec((D, QKV), full),
                  *_tab_specs(tm)],
        out_specs=[pl.BlockSpec((tm, D), row), pl.BlockSpec((tm, QKV), row)],
        out_shape=[jax.ShapeDtypeStruct((T, D), BF16), jax.ShapeDtypeStruct((T, QKV), BF16)],
        compiler_params=_params("parallel"),
    )(x, gain, w, rc, rs1, rs2)


Q_SCALE = 1.0 / math.sqrt(HD)


def _attn_mask(n, T):
    ci = lax.broadcasted_iota(jnp.int32, (3 * BLK, BLK), 0)
    qi = lax.broadcasted_iota(jnp.int32, (3 * BLK, BLK), 1)
    key_pos = n * BLK - BLK + ci
    return (jnp.abs(ci - BLK - qi) <= BLK) & (key_pos >= 0) & (key_pos < T)


def _kv_padded(kv, first_tile):
    low = lax.broadcasted_iota(jnp.int32, (3 * BLK, LANES), 1) < HD
    zero = jnp.zeros((3 * BLK, LANES), BF16)
    out = {}
    for g in range(N_KV):
        t = kv[:, (first_tile + g // 2) * LANES:(first_tile + g // 2 + 1) * LANES]
        swapped = jnp.concatenate([t[:, HD:], t[:, :HD]], axis=1)
        for p in range(2):
            out[g, p] = jnp.where(low if p == 0 else ~low, t if g % 2 == p else swapped, zero)
    return out


def _softmax_sink(s, valid, sk):
    s = jnp.where(valid, s, NEG)
    m = jnp.maximum(jnp.max(s, axis=0, keepdims=True), sk)
    e = jnp.exp(s - m)
    es = jnp.exp(sk - m)
    inv = 1.0 / (jnp.sum(e, axis=0, keepdims=True) + es)
    return e * inv, es * inv


def _attn_specs(T):
    nb = T // BLK
    kv_blk = 2 * N_KV * HD
    kv_col = KV_OFF // kv_blk
    q_spec = pl.BlockSpec((BLK, KV_OFF), lambda n: (n, 0))
    prev = pl.BlockSpec((BLK, kv_blk), lambda n: (jnp.maximum(n - 1, 0), kv_col))
    own = pl.BlockSpec((BLK, kv_blk), lambda n: (n, kv_col))
    nxt = pl.BlockSpec((BLK, kv_blk), lambda n: (jnp.minimum(n + 1, nb - 1), kv_col))
    return nb, q_spec, prev, own, nxt


def attn_fwd(qkv, sink, comm=None):
    T = qkv.shape[0]
    nb, q_spec, prev, own, nxt = _attn_specs(T)

    def body(sink_ref, q_ref, kp_ref, ko_ref, kn_ref, o_ref):
        valid = _attn_mask(pl.program_id(0), T)
        kv = jnp.concatenate([kp_ref[...], ko_ref[...], kn_ref[...]], axis=0)
        kx, vx = _kv_padded(kv, 0), _kv_padded(kv, 2)
        tile = lambda ref, h: ref[:, (h // 2) * LANES:(h // 2 + 1) * LANES]
        ss = [_dot_tb(kx[h // GROUP, h % 2], tile(q_ref, h)) for h in range(N_HEADS)]
        ps = [_softmax_sink(ss[h], valid, sink_ref[h])[0].astype(BF16) for h in range(N_HEADS)]
        vxt = {k: v.T for k, v in vx.items()}
        for j in range(N_HEADS // 2):
            g = 2 * j // GROUP
            o_t = _dot(vxt[g, 0], ps[2 * j]) + _dot(vxt[g, 1], ps[2 * j + 1])
            o_ref[:, j * LANES:(j + 1) * LANES] = o_t.T.astype(BF16)

    (o,), got = _call(
        body, name="attn_fwd", grid=(nb,),
        in_specs=[pl.BlockSpec(memory_space=pltpu.SMEM), q_spec, prev, own, nxt],
        out_specs=[pl.BlockSpec((BLK, D), lambda n: (n, 0))],
        out_shape=[jax.ShapeDtypeStruct((T, D), BF16)],
        semantics=("parallel",), args=(sink, qkv, qkv, qkv, qkv), comm=comm)
    return o, got


def mm_res(a, w, resid, bias, name):
    T, K = a.shape
    tm = _tile(T, 512)

    def body(a_ref, w_ref, r_ref, b_ref, o_ref):
        o_ref[...] = _dot(a_ref[...], w_ref[...]) + b_ref[...] + r_ref[...]

    row = lambda i: (i, 0)
    full = lambda i: (0, 0)
    return pl.pallas_call(
        body, name=name, grid=(T // tm,),
        in_specs=[pl.BlockSpec((tm, K), row), pl.BlockSpec((K, D), full), pl.BlockSpec((tm, D), row),
                  pl.BlockSpec((1, D), full)],
        out_specs=pl.BlockSpec((tm, D), row),
        out_shape=jax.ShapeDtypeStruct((T, D), F32),
        compiler_params=_params("parallel"),
    )(a, w, resid, bias)


def rms_mm_gate(x, gain, w, bias, H, swiglu, act_dtype, name, comm=None):
    T = x.shape[0]
    tm = _tile(T, 512)
    tn = 1408 if H % 1408 == 0 else H
    nj = H // tn
    hw = D // nj

    def body(x_ref, g_ref, w1_ref, w2_ref, b1_ref, b2_ref, h_ref, pre_ref, act_ref):
        h = _rms_fwd(x_ref[...], g_ref[...]).astype(BF16)
        for jj in range(nj):
            @pl.when(pl.program_id(0) == jj)
            def _():
                h_ref[...] = h[:, jj * hw:(jj + 1) * hw]

        for cs in _col_chunks(tn):
            a = _dot(h, w1_ref[:, cs]) + b1_ref[:, cs]
            b = _dot(h, w2_ref[:, cs]) + b2_ref[:, cs]
            pre_ref[0, :, cs] = a.astype(BF16)
            pre_ref[1, :, cs] = b.astype(BF16)
            if swiglu:
                act = a * _sigmoid(a) * b
            else:
                act = a * _sigmoid(b)
            act_ref[:, cs] = act.astype(act_dtype)

    (h, pre, act), got = _call(
        body, name=name, grid=(nj, T // tm),
        in_specs=[pl.BlockSpec((tm, D), lambda j, i: (i, 0)), pl.BlockSpec((1, D), lambda j, i: (0, 0)),
                  pl.BlockSpec((D, tn), lambda j, i: (0, j)), pl.BlockSpec((D, tn), lambda j, i: (0, nj + j)),
                  pl.BlockSpec((1, tn), lambda j, i: (0, j)), pl.BlockSpec((1, tn), lambda j, i: (0, nj + j))],
        out_specs=[pl.BlockSpec((tm, hw), lambda j, i: (i, j)), pl.BlockSpec((2, tm, tn), lambda j, i: (0, i, j)),
                   pl.BlockSpec((tm, tn), lambda j, i: (i, j))],
        out_shape=[jax.ShapeDtypeStruct((T, D), BF16), jax.ShapeDtypeStruct((2, T, H), BF16),
                   jax.ShapeDtypeStruct((T, H), act_dtype)],
        semantics=("parallel", "parallel"), args=(x, gain, w, w, bias, bias), comm=comm)
    return h, pre, act, got


def _conv_tiles(T):
    tt = _tile(T, 512)
    return tt, tt // SUBLANES, D // LANES


def _fill_strided(ext, p, L):
    main = p[HALO:HALO + SUBLANES * L, :].reshape(SUBLANES, L, LANES)
    ext[CONV_PAD:CONV_PAD + L] = jnp.swapaxes(main, 0, 1)

    def ibody(i, carry):
        ext[i] = p[pl.ds(i + 1, SUBLANES, stride=L), :]
        ext[i + CONV_PAD + L] = p[pl.ds(i + CONV_PAD + L + 1, SUBLANES, stride=L), :]
        return carry

    lax.fori_loop(0, CONV_PAD, ibody, 0, unroll=3)


def _conv_specs(T, tt):
    main = pl.BlockSpec((tt, D), lambda i: (i, 0))
    per = tt // HALO
    prev = pl.BlockSpec((HALO, D), lambda i: (jnp.maximum(i * per - 1, 0), 0))
    nxt = pl.BlockSpec((HALO, D), lambda i: (jnp.minimum((i + 1) * per, T // HALO - 1), 0))
    return main, prev, nxt


def _fill_pad(pad, main_ref, prev_ref, next_ref, i, n_i, tt, nlt):
    keep_p = (i > 0).astype(F32)
    keep_n = (i < n_i - 1).astype(F32)
    for lt in range(nlt):
        sl = slice(lt * LANES, (lt + 1) * LANES)
        pad[lt, 0:HALO, :] = prev_ref[:, sl] * keep_p
        pad[lt, HALO:HALO + tt, :] = main_ref[:, sl]
        pad[lt, HALO + tt:2 * HALO + tt, :] = next_ref[:, sl] * keep_n


def conv_fwd(glu, w_dw, b_dw, ln_g, ln_b, comm=None):
    T = glu.shape[0]
    tt, L, nlt = _conv_tiles(T)
    n_i = T // tt
    main, prev, nxt = _conv_specs(T, tt)

    def body(x_ref, xp_ref, xn_ref, w_ref, b_ref, g_ref, bb_ref, dwc_ref, sw_ref, pad, ob, ext, wk):
        i = pl.program_id(0)
        _fill_pad(pad, x_ref, xp_ref, xn_ref, i, n_i, tt, nlt)
        for lt in range(nlt):
            sl = slice(lt * LANES, (lt + 1) * LANES)
            o = ob.at[lt]
            _fill_strided(ext, pad.at[lt], L)
            for k in range(CONV_W):
                wk[k] = jnp.broadcast_to(w_ref[k:k + 1, sl], (SUBLANES, LANES))

            def jbody(jb, carry):
                j = jb * CONV_JB
                accs = [None] * CONV_JB
                for m in range(CONV_W + CONV_JB - 1):
                    e = ext[j + m]
                    for u in range(CONV_JB):
                        if 0 <= m - u < CONV_W:
                            t = e * wk[m - u]
                            accs[u] = t if accs[u] is None else accs[u] + t
                for u in range(CONV_JB):
                    o[pl.ds(j + u, SUBLANES, stride=L), :] = accs[u]
                return carry

            lax.fori_loop(0, L // CONV_JB, jbody, 0)
        y = jnp.concatenate([ob[lt] for lt in range(nlt)], axis=1) + b_ref[...]
        dwc_ref[...] = y
        mu = jnp.mean(y, axis=-1, keepdims=True)
        yc = y - mu
        var = jnp.mean(yc * yc, axis=-1, keepdims=True)
        z = yc * lax.rsqrt(var + EPS) * g_ref[...] + bb_ref[...]
        sw_ref[...] = (z * _sigmoid(z)).astype(BF16)

    full = lambda i: (0, 0)
    (dwc, sw), got = _call(
        body, name="conv_fwd", grid=(n_i,),
        in_specs=[main, prev, nxt, pl.BlockSpec((32, D), full), pl.BlockSpec((1, D), full),
                  pl.BlockSpec((1, D), full), pl.BlockSpec((1, D), full)],
        out_specs=[pl.BlockSpec((tt, D), lambda i: (i, 0)), pl.BlockSpec((tt, D), lambda i: (i, 0))],
        out_shape=[jax.ShapeDtypeStruct((T, D), F32), jax.ShapeDtypeStruct((T, D), BF16)],
        scratch_shapes=[pltpu.VMEM((nlt, tt + 2 * HALO, LANES), F32), pltpu.VMEM((nlt, tt, LANES), F32),
                        pltpu.VMEM((L + 2 * HALO, SUBLANES, LANES), F32), pltpu.VMEM((32, SUBLANES, LANES), F32)],
        semantics=("parallel",), args=(glu, glu, glu, w_dw, b_dw, ln_g, ln_b), comm=comm)
    return dwc, sw, got


def mm_res_loss(a, w, resid, gain, target):
    T, K = a.shape
    tm = _tile(T, 512)

    def body(a_ref, w_ref, r_ref, g_ref, t_ref, dx_ref, loss_ref, dg_ref):
        @pl.when(pl.program_id(0) == 0)
        def _():
            loss_ref[...] = jnp.zeros_like(loss_ref)
            dg_ref[...] = jnp.zeros_like(dg_ref)

        xv, gain_v = _dot(a_ref[...], w_ref[...]) + r_ref[...], g_ref[...]
        err = _rms_fwd(xv, gain_v) - t_ref[...]
        part = 0.5 * jnp.sum(jnp.mean(err * err, axis=-1, keepdims=True), axis=0, keepdims=True)
        loss_ref[...] += jnp.broadcast_to(part, loss_ref.shape)
        dx, dgr = _rms_bwd(err * (1.0 / D), xv, gain_v, 0.0)
        dx_ref[...] = dx
        dg_ref[...] += jnp.sum(dgr, axis=0, keepdims=True)

    row = lambda i: (i, 0)
    full = lambda i: (0, 0)
    return pl.pallas_call(
        body, name="ffn1_down_loss", grid=(T // tm,),
        in_specs=[pl.BlockSpec((tm, K), row), pl.BlockSpec((K, D), full), pl.BlockSpec((tm, D), row),
                  pl.BlockSpec((1, D), full), pl.BlockSpec((tm, D), row)],
        out_specs=[pl.BlockSpec((tm, D), row), pl.BlockSpec((1, LANES), full), pl.BlockSpec((1, D), full)],
        out_shape=[jax.ShapeDtypeStruct((T, D), F32), jax.ShapeDtypeStruct((1, LANES), F32),
                   jax.ShapeDtypeStruct((1, D), F32)],
        compiler_params=_params("arbitrary"),
    )(a, w, resid, gain, target)


def swiglu_bwd(dx, w_down, pre, name, comm=None):
    T = dx.shape[0]
    H = w_down.shape[0]
    tm = _tile(T, 512)
    tn = 1408
    nj = H // tn

    def body(dx_ref, w_ref, pre_ref, dpre_ref):
        dxb = dx_ref[...].astype(BF16)
        for cs in _col_chunks(tn):
            dact = _dot_tb(dxb, w_ref[cs, :])
            g = pre_ref[0, :, cs].astype(F32)
            u = pre_ref[1, :, cs].astype(F32)
            sg = _sigmoid(g)
            dpre_ref[0, :, cs] = (dact * u * sg * (1.0 + g * (1.0 - sg))).astype(BF16)
            dpre_ref[1, :, cs] = (dact * g * sg).astype(BF16)

    (dpre,), got = _call(
        body, name=name, grid=(nj, T // tm),
        in_specs=[pl.BlockSpec((tm, D), lambda j, i: (i, 0)), pl.BlockSpec((tn, D), lambda j, i: (j, 0)),
                  pl.BlockSpec((2, tm, tn), lambda j, i: (0, i, j))],
        out_specs=[pl.BlockSpec((2, tm, tn), lambda j, i: (0, i, j))],
        out_shape=[jax.ShapeDtypeStruct((2, T, H), BF16)],
        semantics=("parallel", "parallel"), args=(dx, w_down, pre), comm=comm)
    return dpre, got


def mm_bt_rmsbwd(dpre, w, x, gain, dres, name, comm=None):
    nh, T, H = dpre.shape
    tm = _tile(T, 512)
    tk = 1408 if H % 1408 == 0 else (1024 if H % 1024 == 0 else H)
    nk = H // tk

    def body(dp_ref, w_ref, x_ref, g_ref, dres_ref, dx_ref, dg_ref, acc):
        i, hf, kk = pl.program_id(0), pl.program_id(1), pl.program_id(2)

        @pl.when((i == 0) & (hf == 0) & (kk == 0))
        def _():
            dg_ref[...] = jnp.zeros_like(dg_ref)

        @pl.when((hf == 0) & (kk == 0))
        def _():
            acc[...] = jnp.zeros_like(acc)

        cols = pl.ds(pl.multiple_of((hf * nk + kk) * tk, LANES), tk)
        acc[...] += _dot_tb(dp_ref[...], w_ref[:, cols])

        @pl.when((hf == nh - 1) & (kk == nk - 1))
        def _():
            dx, dgr = _rms_bwd(acc[...], x_ref[...], g_ref[...], dres_ref[...])
            dx_ref[...] = dx
            dg_ref[...] += jnp.sum(dgr, axis=0, keepdims=True)

    (dx, dg), got = _call(
        body, name=name, grid=(T // tm, nh, nk),
        in_specs=[pl.BlockSpec((None, tm, tk), lambda i, hf, kk: (hf, i, kk)),
                  pl.BlockSpec((D, nh * H), lambda i, hf, kk: (0, 0), pipeline_mode=pl.Buffered(1)),
                  pl.BlockSpec((tm, D), lambda i, hf, kk: (i, 0)), pl.BlockSpec((1, D), lambda i, hf, kk: (0, 0)),
                  pl.BlockSpec((tm, D), lambda i, hf, kk: (i, 0))],
        out_specs=[pl.BlockSpec((tm, D), lambda i, hf, kk: (i, 0)), pl.BlockSpec((1, D), lambda i, hf, kk: (0, 0))],
        out_shape=[jax.ShapeDtypeStruct((T, D), F32), jax.ShapeDtypeStruct((1, D), F32)],
        scratch_shapes=[pltpu.VMEM((tm, D), F32)],
        semantics=("arbitrary", "arbitrary", "arbitrary"), args=(dpre, w, x, gain, dres), comm=comm)
    return dx, dg, got


def dw_col(a, dpre, name):
    T = a.shape[0]
    nh, _, H = dpre.shape
    per = nh * H // N_CHIPS
    bph = N_CHIPS // nh
    tt = _tile(T, 1024)
    nt = T // tt

    def body(a_ref, b_ref, o_ref, acc):
        t = pl.program_id(1)

        @pl.when(t == 0)
        def _():
            acc[...] = jnp.zeros_like(acc)

        acc[...] += _dot_ta(a_ref[...], b_ref[...])

        @pl.when(t == nt - 1)
        def _():
            o_ref[...] = acc[...].astype(BF16)

    return pl.pallas_call(
        body, name=name, grid=(N_CHIPS, nt),
        in_specs=[pl.BlockSpec((tt, D), lambda q, t: (t, 0)),
                  pl.BlockSpec((None, tt, per), lambda q, t: (q // bph, t, q % bph))],
        out_specs=pl.BlockSpec((None, D, per), lambda q, t: (q, 0, 0)),
        out_shape=jax.ShapeDtypeStruct((N_CHIPS, D, per), BF16),
        scratch_shapes=[pltpu.VMEM((D, per), F32)],
        compiler_params=_params("parallel", "arbitrary"),
    )(a, dpre)


def dw_row(a, b, name):
    T, R = a.shape
    cw = 1408 if R % 1408 == 0 else 512
    tt = _tile(T, 1024)
    nt = T // tt

    def body(a_ref, b_ref, o_ref, acc):
        t = pl.program_id(1)

        @pl.when(t == 0)
        def _():
            acc[...] = jnp.zeros_like(acc)

        acc[...] += _dot_ta(a_ref[...], b_ref[...].astype(BF16))

        @pl.when(t == nt - 1)
        def _():
            o_ref[...] = acc[...].astype(BF16)

    out = pl.pallas_call(
        body, name=name, grid=(R // cw, nt),
        in_specs=[pl.BlockSpec((tt, cw), lambda q, t: (t, q)), pl.BlockSpec((tt, D), lambda q, t: (t, 0))],
        out_specs=pl.BlockSpec((cw, D), lambda q, t: (q, 0)),
        out_shape=jax.ShapeDtypeStruct((R, D), BF16),
        scratch_shapes=[pltpu.VMEM((cw, D), F32)],
        compiler_params=_params("parallel", "arbitrary"),
    )(a, b)
    return out.reshape(N_CHIPS, R // N_CHIPS, D)


def ln_silu_bwd(dx, w_pw2, dwc, ln_g, ln_b):
    T = dx.shape[0]
    tm = _tile(T, 512)

    def body(dx_ref, w_ref, y_ref, g_ref, b_ref, dy_ref, dg_ref, db_ref, dbo_ref):
        @pl.when(pl.program_id(0) == 0)
        def _():
            dg_ref[...] = jnp.zeros_like(dg_ref)
            db_ref[...] = jnp.zeros_like(db_ref)
            dbo_ref[...] = jnp.zeros_like(dbo_ref)

        dxv = dx_ref[...]
        dsw = _dot_tb(dxv.astype(BF16), w_ref[...])
        y = y_ref[...]
        mu = jnp.mean(y, axis=-1, keepdims=True)
        yc = y - mu
        rstd = lax.rsqrt(jnp.mean(yc * yc, axis=-1, keepdims=True) + EPS)
        xhat = yc * rstd
        z = xhat * g_ref[...] + b_ref[...]
        sg = _sigmoid(z)
        dz = dsw * sg * (1.0 + z * (1.0 - sg))
        dxh = dz * g_ref[...]
        dy_ref[...] = rstd * (dxh - jnp.mean(dxh, axis=-1, keepdims=True)
                              - xhat * jnp.mean(dxh * xhat, axis=-1, keepdims=True))
        dg_ref[...] += jnp.sum(dz * xhat, axis=0, keepdims=True)
        db_ref[...] += jnp.sum(dz, axis=0, keepdims=True)
        dbo_ref[...] += jnp.sum(dxv, axis=0, keepdims=True)

    row = lambda i: (i, 0)
    full = lambda i: (0, 0)
    vec = pl.BlockSpec((1, D), full)
    return pl.pallas_call(
        body, name="ln_silu_bwd", grid=(T // tm,),
        in_specs=[pl.BlockSpec((tm, D), row), pl.BlockSpec((D, D), full), pl.BlockSpec((tm, D), row), vec, vec],
        out_specs=[pl.BlockSpec((tm, D), row), vec, vec, vec],
        out_shape=[jax.ShapeDtypeStruct((T, D), F32)] + [jax.ShapeDtypeStruct((1, D), F32)] * 3,
        compiler_params=_params("arbitrary"),
    )(dx, w_pw2, dwc, ln_g, ln_b)


def conv_bwd(ddwc, glu, pre, w_dw, comm=None):
    T = ddwc.shape[0]
    tt, L, nlt = _conv_tiles(T)
    n_i = T // tt
    main, prev, nxt = _conv_specs(T, tt)

    def body(d_ref, dp_ref, dn_ref, x_ref, xp_ref, xn_ref, pre_ref, w_ref,
             dpre_ref, dw_ref, dbd_ref, dbp_ref, padd, padx, ob, extd, extx, wk):
        i = pl.program_id(0)

        @pl.when(i == 0)
        def _():
            dw_ref[...] = jnp.zeros_like(dw_ref)
            dbd_ref[...] = jnp.zeros_like(dbd_ref)
            dbp_ref[...] = jnp.zeros_like(dbp_ref)

        _fill_pad(padd, d_ref, dp_ref, dn_ref, i, n_i, tt, nlt)
        _fill_pad(padx, x_ref, xp_ref, xn_ref, i, n_i, tt, nlt)
        for lt in range(nlt):
            sl = slice(lt * LANES, (lt + 1) * LANES)
            o = ob.at[lt]
            _fill_strided(extd, padd.at[lt], L)
            _fill_strided(extx, padx.at[lt], L)
            for k in range(CONV_W):
                wk[k] = jnp.broadcast_to(w_ref[k:k + 1, sl], (SUBLANES, LANES))

            nu = CONV_JB_BWD

            def jbody(jb, accs):
                j = jb * nu
                accs = list(accs)
                d = [extd[j + u + CONV_PAD] for u in range(nu)]
                g = [None] * nu
                for m in range(CONV_W + nu - 1):
                    ed = extd[j + 2 * CONV_PAD + nu - 1 - m]
                    ex = extx[j + m]
                    for u in range(nu):
                        k = m - (nu - 1 - u)
                        if 0 <= k < CONV_W:
                            t = ed * wk[k]
                            g[u] = t if g[u] is None else g[u] + t
                        k = m - u
                        if 0 <= k < CONV_W:
                            accs[k] = accs[k] + d[u] * ex
                for u in range(nu):
                    o[pl.ds(j + u, SUBLANES, stride=L), :] = g[u]
                return tuple(accs)

            accs = lax.fori_loop(0, L // nu, jbody, tuple(jnp.zeros((SUBLANES, LANES), F32) for _ in range(CONV_W)))
            for k in range(CONV_W):
                dw_ref[k:k + 1, sl] += jnp.sum(accs[k], axis=0, keepdims=True)
        dglu = jnp.concatenate([ob[lt] for lt in range(nlt)], axis=1)
        a = pre_ref[0].astype(F32)
        gate = pre_ref[1].astype(F32)
        sg = _sigmoid(gate)
        da = dglu * sg
        dgate = dglu * a * sg * (1.0 - sg)
        dpre_ref[0] = da.astype(BF16)
        dpre_ref[1] = dgate.astype(BF16)
        dbd_ref[...] += jnp.sum(d_ref[...], axis=0, keepdims=True)
        dbp_ref[0] += jnp.sum(da, axis=0, keepdims=True)
        dbp_ref[1] += jnp.sum(dgate, axis=0, keepdims=True)

    full = lambda i: (0, 0)
    (dpre, dw, dbd, dbp), got = _call(
        body, name="conv_bwd", grid=(n_i,),
        in_specs=[main, prev, nxt, main, prev, nxt, pl.BlockSpec((2, tt, D), lambda i: (0, i, 0)),
                  pl.BlockSpec((32, D), full)],
        out_specs=[pl.BlockSpec((2, tt, D), lambda i: (0, i, 0)), pl.BlockSpec((32, D), full),
                   pl.BlockSpec((1, D), full), pl.BlockSpec((2, 1, D), lambda i: (0, 0, 0))],
        out_shape=[jax.ShapeDtypeStruct((2, T, D), BF16), jax.ShapeDtypeStruct((32, D), F32),
                   jax.ShapeDtypeStruct((1, D), F32), jax.ShapeDtypeStruct((2, 1, D), F32)],
        scratch_shapes=[pltpu.VMEM((nlt, tt + 2 * HALO, LANES), F32), pltpu.VMEM((nlt, tt + 2 * HALO, LANES), F32),
                        pltpu.VMEM((nlt, tt, LANES), F32), pltpu.VMEM((L + 2 * HALO, SUBLANES, LANES), F32),
                        pltpu.VMEM((L + 2 * HALO, SUBLANES, LANES), F32), pltpu.VMEM((32, SUBLANES, LANES), F32)],
        semantics=("arbitrary",), args=(ddwc, ddwc, ddwc, glu, glu, glu, pre, w_dw), comm=comm)
    return dpre, dw, dbd, dbp, got


def mm_bt(a, w, name):
    T = a.shape[0]
    N = w.shape[0]
    tm = _tile(T, 512)

    def body(a_ref, w_ref, o_ref):
        o_ref[...] = _dot_tb(a_ref[...].astype(BF16), w_ref[...]).astype(BF16)

    return pl.pallas_call(
        body, name=name, grid=(T // tm,),
        in_specs=[pl.BlockSpec((tm, D), lambda i: (i, 0)), pl.BlockSpec((N, D), lambda i: (0, 0))],
        out_specs=pl.BlockSpec((tm, N), lambda i: (i, 0)),
        out_shape=jax.ShapeDtypeStruct((T, N), BF16),
        compiler_params=_params("parallel"),
    )(a, w)


def attn_bwd(qkv, o, do, sink, rc, rs1, rs2, comm=None):
    T = qkv.shape[0]
    nb, q_spec, prev, own, nxt = _attn_specs(T)
    kvw = N_KV * HD

    def body(sink_ref, q_ref, kp_ref, ko_ref, kn_ref, o_ref, do_ref, c_ref, s1_ref, s2_ref,
             dq_ref, dkc_ref, dvc_ref, dsink_ref):
        n = pl.program_id(0)

        @pl.when(n == 0)
        def _():
            dsink_ref[...] = jnp.zeros_like(dsink_ref)

        valid = _attn_mask(n, T)
        kv = jnp.concatenate([kp_ref[...], ko_ref[...], kn_ref[...]], axis=0)
        kx, vx = _kv_padded(kv, 0), _kv_padded(kv, 2)
        tile = lambda ref, j: ref[:, j * LANES:(j + 1) * LANES]
        ss = [_dot_tb(kx[h // GROUP, h % 2], tile(q_ref, h // 2)) for h in range(N_HEADS)]
        dps = [_dot_tb(vx[h // GROUP, h % 2], tile(do_ref, h // 2)) for h in range(N_HEADS)]
        low_d = lax.broadcasted_iota(jnp.int32, (LANES, BLK), 0) < HD
        deltas = []
        for j in range(N_HEADS // 2):
            prod_t = tile(do_ref, j).astype(F32).T * tile(o_ref, j).astype(F32).T
            deltas.append(jnp.sum(jnp.where(low_d, prod_t, 0.0), axis=0, keepdims=True))
            deltas.append(jnp.sum(jnp.where(low_d, 0.0, prod_t), axis=0, keepdims=True))
        lane = lax.broadcasted_iota(jnp.int32, (1, N_HEADS), 1)
        dsink = jnp.zeros((1, N_HEADS), F32)
        pbs, dss = [], []
        for h in range(N_HEADS):
            p, p_sink = _softmax_sink(ss[h], valid, sink_ref[h])
            dss.append((p * (dps[h] - deltas[h])).astype(BF16))
            pbs.append(p.astype(BF16))
            part = -jnp.sum(p_sink * deltas[h], axis=1, keepdims=True)
            dsink = dsink + jnp.where(lane == h, part, 0.0)
        dsink_ref[...] += dsink
        c, s1, s2 = c_ref[...], s1_ref[...], s2_ref[...]
        kxt = {k: v.T for k, v in kx.items()}
        for j in range(N_HEADS // 2):
            g = 2 * j // GROUP
            dq_t = _dot(kxt[g, 0], dss[2 * j]) + _dot(kxt[g, 1], dss[2 * j + 1])
            dq_ref[:, j * LANES:(j + 1) * LANES] = (_rope(dq_t.T, c, -s1, -s2) * Q_SCALE).astype(BF16)
        low_k = lax.broadcasted_iota(jnp.int32, (3 * BLK, LANES), 1) < HD
        cols = lambda xs, g, p: jnp.concatenate([xs[GROUP * g + p], xs[GROUP * g + 2 + p]], axis=1)
        for t in range(N_KV // 2):
            sums = {}
            for g in (2 * t, 2 * t + 1):
                q2 = jnp.concatenate([tile(q_ref, 2 * g), tile(q_ref, 2 * g + 1)], axis=0)
                do2 = jnp.concatenate([tile(do_ref, 2 * g), tile(do_ref, 2 * g + 1)], axis=0)
                for p in range(2):
                    sums[g, p] = (_dot(cols(dss, g, p), q2), _dot(cols(pbs, g, p), do2))
            for which, ref in ((0, dkc_ref), (1, dvc_ref)):
                keep = jnp.where(low_k, sums[2 * t, 0][which], sums[2 * t + 1, 1][which])
                swap = jnp.where(low_k, sums[2 * t + 1, 0][which], sums[2 * t, 1][which])
                ref[:, t * LANES:(t + 1) * LANES] = keep + pltpu.roll(swap, HD, 1)

    row = lambda n: (n, 0)
    (dq, dkc, dvc, dsink), got = _call(
        body, name="attn_bwd", grid=(nb,),
        in_specs=[pl.BlockSpec(memory_space=pltpu.SMEM), q_spec, prev, own, nxt,
                  pl.BlockSpec((BLK, D), row), pl.BlockSpec((BLK, D), row), *_tab_specs(BLK)],
        out_specs=[pl.BlockSpec((BLK, D), row), pl.BlockSpec((None, 3 * BLK, kvw), lambda n: (n, 0, 0)),
                   pl.BlockSpec((None, 3 * BLK, kvw), lambda n: (n, 0, 0)), pl.BlockSpec((1, N_HEADS), lambda n: (0, 0))],
        out_shape=[jax.ShapeDtypeStruct((T, QKV), BF16), jax.ShapeDtypeStruct((nb, 3 * BLK, kvw), F32),
                   jax.ShapeDtypeStruct((nb, 3 * BLK, kvw), F32), jax.ShapeDtypeStruct((1, N_HEADS), F32)],
        semantics=("arbitrary",), args=(sink, qkv, qkv, qkv, qkv, o, do, rc, rs1, rs2), comm=comm)
    return dq, dkc, dvc, dsink, got


def kv_sum(dqkv, dkc, dvc, rc, rs1, rs2):
    nb = dkc.shape[0]
    T = nb * BLK
    kvw = N_KV * HD

    def body(_, kp_ref, ko_ref, kn_ref, vp_ref, vo_ref, vn_ref, c_ref, s1_ref, s2_ref, out_ref):
        m = pl.program_id(0)
        has_p = (m > 0).astype(F32)
        has_n = (m < nb - 1).astype(F32)
        dk = kp_ref[...] * has_p + ko_ref[...] + kn_ref[...] * has_n
        dv = vp_ref[...] * has_p + vo_ref[...] + vn_ref[...] * has_n
        c, s1, s2 = c_ref[...], s1_ref[...], s2_ref[...]
        for j in range(kvw // LANES):
            sl = slice(LANES * j, LANES * (j + 1))
            out_ref[:, sl] = _rope(dk[:, sl], c, -s1, -s2).astype(BF16)
        out_ref[:, kvw:] = dv.astype(BF16)

    from_prev = pl.BlockSpec((None, BLK, kvw), lambda m: (jnp.maximum(m - 1, 0), 2, 0))
    from_own = pl.BlockSpec((None, BLK, kvw), lambda m: (m, 1, 0))
    from_next = pl.BlockSpec((None, BLK, kvw), lambda m: (jnp.minimum(m + 1, nb - 1), 0, 0))
    return pl.pallas_call(
        body, name="kv_sum", grid=(nb,),
        in_specs=[pl.BlockSpec(memory_space=pl.ANY), from_prev, from_own, from_next, from_prev, from_own, from_next,
                  *_tab_specs(BLK)],
        out_specs=pl.BlockSpec((BLK, 2 * kvw), lambda m: (m, KV_OFF // (2 * kvw))),
        out_shape=jax.ShapeDtypeStruct((T, QKV), BF16),
        input_output_aliases={0: 0},
        compiler_params=_params("parallel"),
    )(dqkv, dkc, dkc, dkc, dvc, dvc, dvc, rc, rs1, rs2)


def _me():
    return lax.axis_index("x"), lax.axis_index("y"), lax.axis_index("c")


def _half_rows(ref, sharded_rows, chip, core):
    R, C = ref.shape[-2], ref.shape[-1]
    lead = (slice(None),) * (len(ref.shape) - 2)
    if sharded_rows:
        per = R // N_CHIPS
        return ref.at[lead + (pl.ds(chip * per + core * (per // 2), per // 2), slice(None))]
    per = C // N_CHIPS
    return ref.at[lead + (pl.ds(core * (R // 2), R // 2), pl.ds(chip * per, per))]


class _Gather:
    def __init__(self, shards, sharded_rows):
        self.inputs = list(shards)
        self.rows = list(sharded_rows)
        self.n = self.n_in = self.n_out = len(shards)
        self.out_shapes = []
        for s, rows in zip(shards, sharded_rows):
            shp = list(s.shape)
            shp[-2 if rows else -1] *= N_CHIPS
            self.out_shapes.append(jax.ShapeDtypeStruct(tuple(shp), s.dtype))
        self.scratch = [pltpu.SemaphoreType.DMA((self.n, 6)), pltpu.SemaphoreType.DMA((self.n, 6)),
                        pltpu.SemaphoreType.DMA((self.n, 2))]

    def _ctx(self, ins, outs, sems):
        send_sems, recv_sems, local_sems = sems
        x, y, c = _me()
        chips = [(1 - x, y), (x, 1 - y), (1 - x, 1 - y)]

        def half_src(w, core):
            s = ins[w]
            R = s.shape[-2]
            return s.at[pl.ds(core * (R // 2), R // 2), :]

        def dst(w, chip, core):
            return _half_rows(outs[w], self.rows[w], chip, core)

        def copy(w, k, src, chip, core, to):
            return pltpu.make_async_remote_copy(
                src_ref=src, dst_ref=dst(w, chip, core), send_sem=send_sems.at[w, k], recv_sem=recv_sems.at[w, k],
                device_id=to, device_id_type=MESH)

        def local(w, core):
            return pltpu.make_async_copy(half_src(w, core), dst(w, 2 * x + y, core), local_sems.at[w, core])

        def first(w, j):
            qx, qy = chips[j]
            return copy(w, j, half_src(w, c), 2 * x + y, c, (qx, qy, c))

        def landed(w, j):
            qx, qy = chips[j]
            return copy(w, j, dst(w, 2 * qx + qy, c), 2 * qx + qy, c, (x, y, c))

        def passed(w, j):
            qx, qy = chips[j]
            return copy(w, 3 + j, dst(w, 2 * qx + qy, c), 2 * qx + qy, c, (x, y, 1 - c))

        def from_sibling(w, j):
            qx, qy = chips[j]
            return copy(w, 3 + j, dst(w, 2 * qx + qy, 1 - c), 2 * qx + qy, 1 - c, (x, y, c))

        return local, first, landed, passed, from_sibling

    def start(self, ins, outs, sems):
        local, first, _, _, _ = self._ctx(ins, outs, sems)
        for w in range(self.n):
            for core in range(2):
                local(w, core).start()
            for j in range(3):
                first(w, j).start()

    def mid(self, ins, outs, sems):
        _, _, landed, passed, _ = self._ctx(ins, outs, sems)
        for w in range(self.n):
            for j in range(3):
                landed(w, j).wait_recv()
                passed(w, j).start()

    def end(self, ins, outs, sems):
        local, first, _, passed, from_sibling = self._ctx(ins, outs, sems)
        for w in range(self.n):
            for j in range(3):
                from_sibling(w, j).wait_recv()
        for w in range(self.n):
            for j in range(3):
                first(w, j).wait_send()
                passed(w, j).wait_send()
            for core in range(2):
                local(w, core).wait()


class _Scatter:
    def __init__(self, grads, small=None):
        self.inputs = list(grads) + ([small] if small is not None else [])
        self.ng = len(grads)
        self.n = self.n_in = self.n_out = len(self.inputs)
        self.out_shapes = [jax.ShapeDtypeStruct((N_DEV, g.shape[1] // 2, g.shape[2]), g.dtype) for g in grads]
        if small is not None:
            self.out_shapes.append(jax.ShapeDtypeStruct((N_DEV,) + small.shape, small.dtype))
        self.scratch = [pltpu.SemaphoreType.DMA((self.n, N_DEV)), pltpu.SemaphoreType.DMA((self.n, N_DEV)),
                        pltpu.SemaphoreType.DMA((self.n,))]

    def _ctx(self, ins, outs, sems):
        send_sems, recv_sems, local_sems = sems
        x, y, c = _me()
        me = 4 * x + 2 * y + c

        def piece(w, chip, core):
            if w >= self.ng:
                return ins[w]
            half = ins[w].shape[1] // 2
            return ins[w].at[chip, pl.ds(core * half, half), :]

        def peer_of(k):
            return x ^ ((k >> 2) & 1), y ^ ((k >> 1) & 1), c ^ (k & 1)

        def local(w):
            return pltpu.make_async_copy(piece(w, 2 * x + y, c), outs[w].at[me], local_sems.at[w])

        def send(w, k):
            px, py, pc = peer_of(k)
            return pltpu.make_async_remote_copy(
                src_ref=piece(w, 2 * px + py, pc), dst_ref=outs[w].at[me], send_sem=send_sems.at[w, k],
                recv_sem=recv_sems.at[w, k], device_id=(px, py, pc), device_id_type=MESH)

        def recv(w, k):
            px, py, pc = peer_of(k)
            return pltpu.make_async_remote_copy(
                src_ref=piece(w, 2 * x + y, c), dst_ref=outs[w].at[4 * px + 2 * py + pc], send_sem=send_sems.at[w, k],
                recv_sem=recv_sems.at[w, k], device_id=(px, py, pc), device_id_type=MESH)

        return local, send, recv

    def start(self, ins, outs, sems):
        local, send, _ = self._ctx(ins, outs, sems)
        for w in range(self.n):
            local(w).start()
            for k in range(1, N_DEV):
                send(w, k).start()

    def mid(self, ins, outs, sems):
        pass

    def end(self, ins, outs, sems):
        local, send, recv = self._ctx(ins, outs, sems)
        for w in range(self.n):
            for k in range(1, N_DEV):
                recv(w, k).wait_recv()
        for w in range(self.n):
            for k in range(1, N_DEV):
                send(w, k).wait_send()
            local(w).wait()


class _Both:
    def __init__(self, a, b):
        self.a, self.b = a, b
        self.inputs = a.inputs + b.inputs
        self.out_shapes = a.out_shapes + b.out_shapes
        self.scratch = a.scratch + b.scratch
        self.n_in, self.n_out = a.n_in + b.n_in, a.n_out + b.n_out

    def _split(self, ins, outs, sems):
        a, na = self.a, len(self.a.scratch)
        return (ins[:a.n_in], outs[:a.n_out], sems[:na]), (ins[a.n_in:], outs[a.n_out:], sems[na:])

    def start(self, ins, outs, sems):
        pa, pb = self._split(ins, outs, sems)
        self.a.start(*pa)
        self.b.start(*pb)

    def mid(self, ins, outs, sems):
        pa, pb = self._split(ins, outs, sems)
        self.a.mid(*pa)
        self.b.mid(*pb)

    def end(self, ins, outs, sems):
        pa, pb = self._split(ins, outs, sems)
        self.a.end(*pa)
        self.b.end(*pb)


def exchange(plan, name):
    def body(*refs):
        ins, outs, sems = refs[:plan.n_in], refs[plan.n_in:plan.n_in + plan.n_out], refs[plan.n_in + plan.n_out:]
        plan.start(ins, outs, sems)
        plan.mid(ins, outs, sems)
        plan.end(ins, outs, sems)

    any_spec = pl.BlockSpec(memory_space=pl.ANY)
    return pl.pallas_call(
        body, name=name, in_specs=[any_spec] * plan.n_in, out_specs=[any_spec] * plan.n_out,
        out_shape=plan.out_shapes, scratch_shapes=plan.scratch,
    )(*plan.inputs)


def _call(body, *, name, grid, in_specs, out_specs, out_shape, scratch_shapes=(), semantics, args, comm=None):
    if comm is None:
        outs = pl.pallas_call(
            body, name=name, grid=grid, in_specs=in_specs, out_specs=out_specs, out_shape=out_shape,
            scratch_shapes=list(scratch_shapes), compiler_params=_params(*semantics))(*args)
        return outs, []
    n_in, n_out, n_scr = len(in_specs), len(out_specs), len(scratch_shapes)

    total = math.prod(grid)
    first, middle, last = 0, (3 * total) // 4 - 1, total - 1
    assert first <= middle < last

    def at(step):
        lin = pl.program_id(0)
        for d in range(1, len(grid)):
            lin = lin * grid[d] + pl.program_id(d)
        return lin == step

    def hosted(*refs):
        h_in, c_in = refs[:n_in], refs[n_in:n_in + comm.n_in]
        rest = refs[n_in + comm.n_in:]
        h_out, c_out = rest[:n_out], rest[n_out:n_out + comm.n_out]
        rest = rest[n_out + comm.n_out:]
        h_scr, c_scr = rest[:n_scr], rest[n_scr:]

        @pl.when(at(first))
        def _():
            comm.start(c_in, c_out, c_scr)

        body(*h_in, *h_out, *h_scr)

        @pl.when(at(middle))
        def _():
            comm.mid(c_in, c_out, c_scr)

        @pl.when(at(last))
        def _():
            comm.end(c_in, c_out, c_scr)

    any_spec = pl.BlockSpec(memory_space=pl.ANY)
    outs = pl.pallas_call(
        hosted, name=name, grid=grid, in_specs=list(in_specs) + [any_spec] * comm.n_in,
        out_specs=list(out_specs) + [any_spec] * comm.n_out, out_shape=list(out_shape) + comm.out_shapes,
        scratch_shapes=list(scratch_shapes) + comm.scratch,
        compiler_params=_params(*(["arbitrary"] * len(grid))))(*args, *comm.inputs)
    return outs[:n_out], outs[n_out:]


def sum_swap(pieces, name):
    nl = len(pieces)
    _, r2, cc = pieces[0].shape
    tr = 128 if r2 % 128 == 0 else r2 // 2
    n = r2 // tr

    def body(*refs):
        p_refs, out = refs[:nl], refs[nl]
        slots, send_sems, local_sems, recv_sem = refs[nl + 1:]
        x, y, c = _me()
        sibling = (x, y, 1 - c)
        l, i = pl.program_id(0), pl.program_id(1)
        step = l * n + i

        def rows(st, core):
            return out.at[st // n, pl.ds(core * r2 + (st % n) * tr, tr), :]

        def copies(st):
            slot = st % 2
            local = pltpu.make_async_copy(slots.at[slot], rows(st, c), local_sems.at[slot])
            remote = pltpu.make_async_remote_copy(
                src_ref=slots.at[slot], dst_ref=rows(st, c), send_sem=send_sems.at[slot], recv_sem=recv_sem,
                device_id=sibling, device_id_type=MESH)
            return local, remote

        for ll in range(nl):
            @pl.when(l == ll)
            def _():
                acc = p_refs[ll][0].astype(F32)
                for d in range(1, N_DEV):
                    acc = acc + p_refs[ll][d].astype(F32)
                slots[step % 2] = acc

        for cp in copies(step):
            cp.start()

        @pl.when(step >= 1)
        def _():
            local, remote = copies(step - 1)
            local.wait()
            remote.wait_send()

        @pl.when(step == nl * n - 1)
        def _():
            local, remote = copies(step)
            local.wait()
            remote.wait_send()
            theirs = out.at[:, pl.ds((1 - c) * r2, r2), :]
            pltpu.make_async_remote_copy(src_ref=theirs, dst_ref=theirs, send_sem=send_sems.at[0],
                                         recv_sem=recv_sem, device_id=sibling, device_id_type=MESH).wait_recv()

    def piece_spec(ll):
        def index(l, i):
            return (0, jnp.where(l == ll, i, jnp.where(l < ll, 0, n - 1)), 0)
        return pl.BlockSpec((N_DEV, tr, cc), index)

    return pl.pallas_call(
        body, name=name, grid=(nl, n),
        in_specs=[piece_spec(ll) for ll in range(nl)],
        out_specs=pl.BlockSpec(memory_space=pl.ANY),
        out_shape=jax.ShapeDtypeStruct((nl, 2 * r2, cc), F32),
        scratch_shapes=[pltpu.VMEM((2, tr, cc), F32), pltpu.SemaphoreType.DMA((2,)), pltpu.SemaphoreType.DMA((2,)),
                        pltpu.SemaphoreType.DMA(())],
        compiler_params=_params("arbitrary", "arbitrary"),
    )(*pieces)


def sum_pieces(pieces, name):
    _, R, C = pieces.shape
    tr = _tile(R, 128) if R % 128 == 0 else R

    def body(p_ref, o_ref):
        acc = p_ref[0].astype(F32)
        for d in range(1, N_DEV):
            acc = acc + p_ref[d].astype(F32)
        o_ref[...] = acc

    return pl.pallas_call(
        body, name=name, grid=(R // tr,),
        in_specs=[pl.BlockSpec((N_DEV, tr, C), lambda i: (0, i, 0))],
        out_specs=pl.BlockSpec((tr, C), lambda i: (i, 0)),
        out_shape=jax.ShapeDtypeStruct((R, C), F32),
        compiler_params=_params("parallel"),
    )(pieces)


def adamw(w, g, m, v, name):
    Lyr, R, C = w.shape
    tr = _tile(R, 256) if R % 8 == 0 else R
    c1 = 1.0 / (1.0 - ADAM_B1 ** ADAM_STEP)
    c2 = 1.0 / (1.0 - ADAM_B2 ** ADAM_STEP)

    def body(w_ref, g_ref, m_ref, v_ref, d_ref, nm_ref, nv_ref):
        gv = g_ref[...]
        nm = ADAM_B1 * m_ref[...] + (1.0 - ADAM_B1) * gv
        nv = ADAM_B2 * v_ref[...] + (1.0 - ADAM_B2) * (gv * gv)
        nm_ref[...] = nm
        nv_ref[...] = nv
        d_ref[...] = -ADAM_LR * ((nm * c1) / (jnp.sqrt(nv * c2) + ADAM_EPS) + ADAM_WD * w_ref[...])

    spec = pl.BlockSpec((None, tr, C), lambda l, i: (l, i, 0))
    shp = jax.ShapeDtypeStruct(w.shape, F32)
    return pl.pallas_call(
        body, name=name, grid=(Lyr, R // tr),
        in_specs=[spec] * 4, out_specs=[spec] * 3, out_shape=[shp] * 3,
        compiler_params=_params("parallel", "parallel"),
    )(w, g, m, v)


def _rope_tables(T):
    pos = jnp.arange(T, dtype=F32)
    inv_freq = THETA ** (-jnp.arange(0, ROT, 2, dtype=F32) / ROT)
    ang = pos[:, None] * inv_freq[None, :]
    cs = jnp.concatenate([jnp.cos(ang), jnp.sin(ang)], axis=1)
    half = ROT // 2
    lane = jnp.arange(3 * LANES)
    table, lm = lane // LANES, lane % HD
    src = jnp.where(table == 0, lm % half, half + lm % half)
    i32 = lambda b: b.astype(jnp.int32)
    sign = jnp.where(table == 0, i32(lm < ROT), jnp.where(table == 1, -i32(lm < half), i32((lm >= half) & (lm < ROT))))
    place = (jnp.arange(ROT)[:, None] == src[None, :]) * sign[None, :].astype(F32)
    ones = ((table == 0) & (lm >= ROT)).astype(F32)
    return jnp.dot(cs, place, precision=lax.Precision.HIGHEST) + ones[None, :]


def _tab_specs(rows):
    return [pl.BlockSpec((rows, LANES), lambda i, k=k: (i, k)) for k in range(3)]


def kernel(x, attn_norm, attn_w_qkv, attn_w_o, attn_sink, conv_norm, conv_w_pw1, conv_b_pw1, conv_w_dw, conv_b_dw, conv_ln_g, conv_ln_b, conv_w_pw2, conv_b_pw2, ffn_norm, ffn_w_gu, ffn_w_down, final_norm, loss_target, m_attn_norm, m_attn_w_qkv, m_attn_w_o, m_attn_sink, m_conv_norm, m_conv_w_pw1, m_conv_b_pw1, m_conv_w_dw, m_conv_b_dw, m_conv_ln_g, m_conv_ln_b, m_conv_w_pw2, m_conv_b_pw2, m_ffn_norm, m_ffn_w_gu, m_ffn_w_down, m_final_norm, v_attn_norm, v_attn_w_qkv, v_attn_w_o, v_attn_sink, v_conv_norm, v_conv_w_pw1, v_conv_b_pw1, v_conv_w_dw, v_conv_b_dw, v_conv_ln_g, v_conv_ln_b, v_conv_w_pw2, v_conv_b_pw2, v_ffn_norm, v_ffn_w_gu, v_ffn_w_down, v_final_norm):
    T = x.shape[1]
    x0 = x[0]
    target = loss_target[0]
    ix, iy = lax.axis_index("x"), lax.axis_index("y")
    chip = 2 * ix + iy
    rc = rs1 = rs2 = _rope_tables(T)

    bf = lambda t: t.astype(BF16)
    col_row = [False, True]

    def place(vec, width):
        return lax.dynamic_update_slice(jnp.zeros((vec.shape[0], N_CHIPS * width), F32), vec, (0, chip * width))

    small_rows = jnp.concatenate([
        place(conv_norm, 256), place(conv_b_pw1, 512).reshape(2, D), place(conv_b_dw, 256), place(conv_ln_g, 256),
        place(conv_ln_b, 256), place(conv_b_pw2, 256), jnp.zeros((1, D), F32),
        place(conv_w_dw[0], 256), jnp.zeros((1, D), F32)], axis=0)
    w_qkv, w_o, got = exchange(_Both(_Gather([bf(attn_w_qkv[0]), bf(attn_w_o[0])], col_row),
                                     _Scatter([], small_rows)), "gather_attn")
    psmall = sum_pieces(got, "sum_small_params") * 0.5
    p_conv_norm, p_b_pw1 = psmall[0:1], psmall[1:3].reshape(1, 2 * D)
    p_b_dw, p_ln_g, p_ln_b, p_b_pw2 = psmall[3:4], psmall[4:5], psmall[5:6], psmall[6:7]
    p_w_dw = psmall[8:40]

    h0, qkv = rms_qkv(x0, attn_norm, w_qkv, rc, rs1, rs2)
    sink = attn_sink[0]
    o, (w_gu0,) = attn_fwd(qkv, sink, comm=_Gather([bf(ffn_w_gu[0])], [False]))
    zero_b = jnp.zeros((1, D), F32)
    x1 = mm_res(o, w_o, x0, zero_b, "attn_out")
    zero_gu = jnp.zeros((1, 2 * DFF), F32)
    h1, gu0, act0, (w_down0, w_pw1, w_pw2) = rms_mm_gate(
        x1, ffn_norm[0:1], w_gu0, zero_gu, DFF, True, BF16, "ffn0_up",
        comm=_Gather([bf(ffn_w_down[0]), bf(conv_w_pw1[0]), bf(conv_w_pw2[0])], [True, False, True]))
    x2 = mm_res(act0, w_down0, x1, zero_b, "ffn0_down")
    h2, pre, glu, _ = rms_mm_gate(x2, p_conv_norm, w_pw1, p_b_pw1, D, False, F32, "conv_pw1")
    dwc, sw, (w_gu1, w_down1) = conv_fwd(glu, p_w_dw, p_b_dw, p_ln_g, p_ln_b,
                                         comm=_Gather([bf(ffn_w_gu[1]), bf(ffn_w_down[1])], col_row))
    x3 = mm_res(sw, w_pw2, x2, p_b_pw2, "conv_pw2")
    h3, gu1, act1, _ = rms_mm_gate(x3, ffn_norm[1:2], w_gu1, zero_gu, DFF, True, BF16, "ffn1_up")
    dx4, loss_part, d_final = mm_res_loss(act1, w_down1, x3, final_norm.reshape(1, D), target)
    loss = lax.psum(loss_part[0, 0], ("x", "y", "c"))

    dgu1, _ = swiglu_bwd(dx4, w_down1, gu1, "ffn1_down_bwd")
    g_down1 = dw_row(act1, dx4, "ffn1_down_dw")
    dx3, d_ffn1, _ = mm_bt_rmsbwd(dgu1, w_gu1, x3, ffn_norm[1:2], dx4, "ffn1_up_bwd")
    g_gu1 = dw_col(h3, dgu1, "ffn1_up_dw")

    ddwc, d_ln_g, d_ln_b, d_b_pw2 = ln_silu_bwd(dx3, w_pw2, dwc, p_ln_g, p_ln_b)
    g_pw2 = dw_row(sw, dx3, "conv_pw2_dw")
    dpre, d_w_dw, d_b_dw, d_b_pw1, (r_gu1, r_down1) = conv_bwd(ddwc, glu, pre, p_w_dw,
                                                               comm=_Scatter([g_gu1, g_down1]))
    dx2, d_conv_norm, _ = mm_bt_rmsbwd(dpre, w_pw1, x2, p_conv_norm, dx3, "conv_pw1_bwd")
    g_pw1 = dw_col(h2, dpre, "conv_pw1_dw")

    dgu0, (r_pw1, r_pw2) = swiglu_bwd(dx2, w_down0, gu0, "ffn0_down_bwd", comm=_Scatter([g_pw1, g_pw2]))
    g_down0 = dw_row(act0, dx2, "ffn0_down_dw")
    dx1, d_ffn0, _ = mm_bt_rmsbwd(dgu0, w_gu0, x1, ffn_norm[0:1], dx2, "ffn0_up_bwd")
    g_gu0 = dw_col(h1, dgu0, "ffn0_up_dw")

    do = mm_bt(dx1, w_o, "attn_out_bwd")
    g_o = dw_row(o, dx1, "attn_out_dw")
    dq, dkc, dvc, d_sink, (r_gu0, r_down0, r_o) = attn_bwd(qkv, o, do, sink, rc, rs1, rs2,
                                                           comm=_Scatter([g_gu0, g_down0, g_o]))
    dqkv = kv_sum(dq, dkc, dvc, rc, rs1, rs2)[None]
    g_qkv = dw_col(h0, dqkv, "attn_qkv_dw")
    dx0, d_attn_norm, (r_qkv,) = mm_bt_rmsbwd(dqkv, w_qkv, x0, attn_norm, dx1, "attn_qkv_bwd",
                                              comm=_Scatter([g_qkv]))

    pad16 = lambda t: jnp.concatenate([t, jnp.zeros((1, D - t.shape[1]), F32)], axis=1)
    small_g = jnp.concatenate([
        d_attn_norm, pad16(d_sink), d_conv_norm, d_b_pw1.reshape(2, D), d_b_dw, d_ln_g, d_ln_b, d_b_pw2,
        d_ffn0, d_ffn1, d_final, jnp.zeros((4, D), F32), d_w_dw], axis=0)
    r_small, = exchange(_Scatter([], small_g), "scatter_small")
    gf_gu = sum_swap([r_gu0, r_gu1], "sum_gu")
    gf_down = sum_swap([r_down0, r_down1], "sum_down")
    gf_pw1, gf_pw2 = sum_swap([r_pw1], "sum_pw1"), sum_swap([r_pw2], "sum_pw2")
    gf_qkv, gf_o = sum_swap([r_qkv], "sum_qkv"), sum_swap([r_o], "sum_o")
    gs = sum_pieces(r_small, "sum_small_grads")

    def take(row0, nrows, width):
        return lax.dynamic_slice(gs, (row0, chip * width), (nrows, width))

    grads = {
        "attn_norm": gs[0:1], "attn_w_qkv": gf_qkv, "attn_w_o": gf_o, "attn_sink": gs[1:2, :N_HEADS],
        "conv_norm": take(2, 1, 256), "conv_w_pw1": gf_pw1,
        "conv_b_pw1": lax.dynamic_slice(gs[3:5].reshape(1, 2 * D), (0, chip * 512), (1, 512)),
        "conv_w_dw": take(16, 32, 256)[None, :CONV_W], "conv_b_dw": take(5, 1, 256), "conv_ln_g": take(6, 1, 256),
        "conv_ln_b": take(7, 1, 256), "conv_w_pw2": gf_pw2, "conv_b_pw2": take(8, 1, 256),
        "ffn_norm": gs[9:11], "ffn_w_gu": gf_gu, "ffn_w_down": gf_down, "final_norm": gs[11],
    }
    weights = dict(attn_norm=attn_norm, attn_w_qkv=attn_w_qkv, attn_w_o=attn_w_o, attn_sink=attn_sink,
                   conv_norm=conv_norm, conv_w_pw1=conv_w_pw1, conv_b_pw1=conv_b_pw1, conv_w_dw=conv_w_dw,
                   conv_b_dw=conv_b_dw, conv_ln_g=conv_ln_g, conv_ln_b=conv_ln_b, conv_w_pw2=conv_w_pw2,
                   conv_b_pw2=conv_b_pw2, ffn_norm=ffn_norm, ffn_w_gu=ffn_w_gu, ffn_w_down=ffn_w_down,
                   final_norm=final_norm)
    m_in = dict(attn_norm=m_attn_norm, attn_w_qkv=m_attn_w_qkv, attn_w_o=m_attn_w_o, attn_sink=m_attn_sink,
                conv_norm=m_conv_norm, conv_w_pw1=m_conv_w_pw1, conv_b_pw1=m_conv_b_pw1, conv_w_dw=m_conv_w_dw,
                conv_b_dw=m_conv_b_dw, conv_ln_g=m_conv_ln_g, conv_ln_b=m_conv_ln_b, conv_w_pw2=m_conv_w_pw2,
                conv_b_pw2=m_conv_b_pw2, ffn_norm=m_ffn_norm, ffn_w_gu=m_ffn_w_gu, ffn_w_down=m_ffn_w_down,
                final_norm=m_final_norm)
    v_in = dict(attn_norm=v_attn_norm, attn_w_qkv=v_attn_w_qkv, attn_w_o=v_attn_w_o, attn_sink=v_attn_sink,
                conv_norm=v_conv_norm, conv_w_pw1=v_conv_w_pw1, conv_b_pw1=v_conv_b_pw1, conv_w_dw=v_conv_w_dw,
                conv_b_dw=v_conv_b_dw, conv_ln_g=v_conv_ln_g, conv_ln_b=v_conv_ln_b, conv_w_pw2=v_conv_w_pw2,
                conv_b_pw2=v_conv_b_pw2, ffn_norm=v_ffn_norm, ffn_w_gu=v_ffn_w_gu, ffn_w_down=v_ffn_w_down,
                final_norm=v_final_norm)
    order = list(weights)
    g_out, d_out, m_out, v_out = [], [], [], []
    for nm in order:
        w = weights[nm]
        shape = w.shape
        as3 = lambda t: t.reshape((1,) * (3 - len(shape)) + shape) if len(shape) < 3 else t.reshape(shape)
        g3 = as3(grads[nm].reshape(shape))
        delta, nm_, nv_ = adamw(as3(w), g3, as3(m_in[nm]), as3(v_in[nm]), "adamw_" + nm)
        g_out.append(g3.reshape(shape))
        d_out.append(delta.reshape(shape))
        m_out.append(nm_.reshape(shape))
        v_out.append(nv_.reshape(shape))
    return (loss, dx0[None], *g_out, *d_out, *m_out, *v_out)
```

```python
import functools
import math

import jax
import jax.numpy as jnp
from jax import lax
from jax.experimental import pallas as pl
from jax.experimental.pallas import tpu as pltpu

F32 = jnp.float32
BF16 = jnp.bfloat16

D = 1024
N_HEADS = 16
N_KV = 4
GROUP = N_HEADS // N_KV
HD = 64
ROT = 16
THETA = 500000.0
BLK = 128
QKV = (N_HEADS + 2 * N_KV) * HD
KV_OFF = N_HEADS * HD
DFF = 2816
CONV_W = 31
CONV_PAD = 15
HALO = 16
CONV_JB = 8
CONV_JB_BWD = 4
EPS = 1e-6
NEG = -1e30
N_CHIPS = 4
N_DEV = 8
LANES = 128
SUBLANES = 8

ADAM_LR, ADAM_B1, ADAM_B2, ADAM_EPS, ADAM_WD, ADAM_STEP = 0.001, 0.9, 0.999, 1e-08, 0.01, 10

VMEM_LIMIT = 56 * 1024 * 1024
MESH = pl.DeviceIdType.MESH


def _params(*sem):
    return pltpu.CompilerParams(dimension_semantics=sem, vmem_limit_bytes=VMEM_LIMIT)


def _tile(n, want):
    if n <= want:
        return n
    for t in range(want, 7, -1):
        if n % t == 0 and t % 8 == 0:
            return t
    return n


MXU_COLS = 256


def _col_chunks(n):
    return [slice(c, min(c + MXU_COLS, n)) for c in range(0, n, MXU_COLS)]


def _sigmoid(v):
    return 1.0 / (1.0 + jnp.exp(-v))


def _rms_fwd(xv, gain):
    r = lax.rsqrt(jnp.mean(xv * xv, axis=-1, keepdims=True) + EPS)
    return xv * r * gain


def _rms_bwd(dh, xv, gain, dres):
    r = lax.rsqrt(jnp.mean(xv * xv, axis=-1, keepdims=True) + EPS)
    xhat = xv * r
    gy = dh * gain
    dx = r * (gy - xhat * jnp.mean(gy * xhat, axis=-1, keepdims=True))
    return dx + dres, dh * xhat


def _rope(blk, c, s1, s2):
    return blk * c + pltpu.roll(blk, LANES - ROT // 2, 1) * s1 + pltpu.roll(blk, ROT // 2, 1) * s2


def _dot(a, b):
    return jnp.dot(a, b, preferred_element_type=F32)


def _dot_tb(a, b):
    return lax.dot_general(a, b, (((1,), (1,)), ((), ())), preferred_element_type=F32)


def _dot_ta(a, b):
    return lax.dot_general(a, b, (((0,), (0,)), ((), ())), preferred_element_type=F32)


def rms_qkv(x, gain, w, rc, rs1, rs2):
    T = x.shape[0]
    tm = _tile(T, 512)

    def body(x_ref, g_ref, w_ref, c_ref, s1_ref, s2_ref, h_ref, qkv_ref):
        h = _rms_fwd(x_ref[...], g_ref[...]).astype(BF16)
        h_ref[...] = h
        acc = _dot(h, w_ref[...])
        c, s1, s2 = c_ref[...], s1_ref[...], s2_ref[...]
        n_rot = (KV_OFF + N_KV * HD) // LANES
        for j in range(n_rot):
            sl = slice(LANES * j, LANES * (j + 1))
            roped = _rope(acc[:, sl], c, s1, s2)
            if j < KV_OFF // LANES:
                roped = roped * Q_SCALE
            qkv_ref[:, sl] = roped.astype(BF16)
        qkv_ref[:, n_rot * LANES:] = acc[:, n_rot * LANES:].astype(BF16)

    row = lambda i: (i, 0)
    full = lambda i: (0, 0)
    return pl.pallas_call(
        body, name="rms_qkv", grid=(T // tm,),
        in_specs=[pl.BlockSpec((tm, D), row), pl.BlockSpec((1, D), full), pl.BlockSpec((D, QKV), full),
                  *_tab_specs(tm)],
        out_specs=[pl.BlockSpec((tm, D), row), pl.BlockSpec((tm, QKV), row)],
        out_shape=[jax.ShapeDtypeStruct((T, D), BF16), jax.ShapeDtypeStruct((T, QKV), BF16)],
        compiler_params=_params("parallel"),
    )(x, gain, w, rc, rs1, rs2)


Q_SCALE = 1.0 / math.sqrt(HD)


def _attn_mask(n, T):
    ci = lax.broadcasted_iota(jnp.int32, (3 * BLK, BLK), 0)
    qi = lax.broadcasted_iota(jnp.int32, (3 * BLK, BLK), 1)
    key_pos = n * BLK - BLK + ci
    return (jnp.abs(ci - BLK - qi) <= BLK) & (key_pos >= 0) & (key_pos < T)


def _kv_padded(kv, first_tile):
    low = lax.broadcasted_iota(jnp.int32, (3 * BLK, LANES), 1) < HD
    zero = jnp.zeros((3 * BLK, LANES), BF16)
    out = {}
    for g in range(N_KV):
        t = kv[:, (first_tile + g // 2) * LANES:(first_tile + g // 2 + 1) * LANES]
        swapped = jnp.concatenate([t[:, HD:], t[:, :HD]], axis=1)
        for p in range(2):
            out[g, p] = jnp.where(low if p == 0 else ~low, t if g % 2 == p else swapped, zero)
    return out


def _softmax_sink(s, valid, sk):
    s = jnp.where(valid, s, NEG)
    m = jnp.maximum(jnp.max(s, axis=0, keepdims=True), sk)
    e = jnp.exp(s - m)
    es = jnp.exp(sk - m)
    inv = 1.0 / (jnp.sum(e, axis=0, keepdims=True) + es)
    return e * inv, es * inv


def _attn_specs(T):
    nb = T // BLK
    kv_blk = 2 * N_KV * HD
    kv_col = KV_OFF // kv_blk
    q_spec = pl.BlockSpec((BLK, KV_OFF), lambda n: (n, 0))
    prev = pl.BlockSpec((BLK, kv_blk), lambda n: (jnp.maximum(n - 1, 0), kv_col))
    own = pl.BlockSpec((BLK, kv_blk), lambda n: (n, kv_col))
    nxt = pl.BlockSpec((BLK, kv_blk), lambda n: (jnp.minimum(n + 1, nb - 1), kv_col))
    return nb, q_spec, prev, own, nxt


def attn_fwd(qkv, sink, comm=None):
    T = qkv.shape[0]
    nb, q_spec, prev, own, nxt = _attn_specs(T)

    def body(sink_ref, q_ref, kp_ref, ko_ref, kn_ref, o_ref):
        valid = _attn_mask(pl.program_id(0), T)
        kv = jnp.concatenate([kp_ref[...], ko_ref[...], kn_ref[...]], axis=0)
        kx, vx = _kv_padded(kv, 0), _kv_padded(kv, 2)
        tile = lambda ref, h: ref[:, (h // 2) * LANES:(h // 2 + 1) * LANES]
        ss = [_dot_tb(kx[h // GROUP, h % 2], tile(q_ref, h)) for h in range(N_HEADS)]
        ps = [_softmax_sink(ss[h], valid, sink_ref[h])[0].astype(BF16) for h in range(N_HEADS)]
        vxt = {k: v.T for k, v in vx.items()}
        for j in range(N_HEADS // 2):
            g = 2 * j // GROUP
            o_t = _dot(vxt[g, 0], ps[2 * j]) + _dot(vxt[g, 1], ps[2 * j + 1])
            o_ref[:, j * LANES:(j + 1) * LANES] = o_t.T.astype(BF16)

    (o,), got = _call(
        body, name="attn_fwd", grid=(nb,),
        in_specs=[pl.BlockSpec(memory_space=pltpu.SMEM), q_spec, prev, own, nxt],
        out_specs=[pl.BlockSpec((BLK, D), lambda n: (n, 0))],
        out_shape=[jax.ShapeDtypeStruct((T, D), BF16)],
        semantics=("parallel",), args=(sink, qkv, qkv, qkv, qkv), comm=comm)
    return o, got


def mm_res(a, w, resid, bias, name):
    T, K = a.shape
    tm = _tile(T, 512)

    def body(a_ref, w_ref, r_ref, b_ref, o_ref):
        o_ref[...] = _dot(a_ref[...], w_ref[...]) + b_ref[...] + r_ref[...]

    row = lambda i: (i, 0)
    full = lambda i: (0, 0)
    return pl.pallas_call(
        body, name=name, grid=(T // tm,),
        in_specs=[pl.BlockSpec((tm, K), row), pl.BlockSpec((K, D), full), pl.BlockSpec((tm, D), row),
                  pl.BlockSpec((1, D), full)],
        out_specs=pl.BlockSpec((tm, D), row),
        out_shape=jax.ShapeDtypeStruct((T, D), F32),
        compiler_params=_params("parallel"),
    )(a, w, resid, bias)


def rms_mm_gate(x, gain, w, bias, H, swiglu, act_dtype, name, comm=None):
    T = x.shape[0]
    tm = _tile(T, 512)

    def body(x_ref, g_ref, w_ref, b_ref, h_ref, pre_ref, act_ref):
        h = _rms_fwd(x_ref[...], g_ref[...]).astype(BF16)
        h_ref[...] = h
        for cs in _col_chunks(H):
            cs2 = slice(H + cs.start, H + cs.stop)
            a = _dot(h, w_ref[:, cs]) + b_ref[:, cs]
            b = _dot(h, w_ref[:, cs2]) + b_ref[:, cs2]
            pre_ref[0, :, cs] = a.astype(BF16)
            pre_ref[1, :, cs] = b.astype(BF16)
            if swiglu:
                act = a * _sigmoid(a) * b
            else:
                act = a * _sigmoid(b)
            act_ref[:, cs] = act.astype(act_dtype)

    row = lambda i: (i, 0)
    full = lambda i: (0, 0)
    (h, pre, act), got = _call(
        body, name=name, grid=(T // tm,),
        in_specs=[pl.BlockSpec((tm, D), row), pl.BlockSpec((1, D), full),
                  pl.BlockSpec((D, 2 * H), full, pipeline_mode=pl.Buffered(1)), pl.BlockSpec((1, 2 * H), full)],
        out_specs=[pl.BlockSpec((tm, D), row), pl.BlockSpec((2, tm, H), lambda i: (0, i, 0)),
                   pl.BlockSpec((tm, H), row)],
        out_shape=[jax.ShapeDtypeStruct((T, D), BF16), jax.ShapeDtypeStruct((2, T, H), BF16),
                   jax.ShapeDtypeStruct((T, H), act_dtype)],
        semantics=("parallel",), args=(x, gain, w, bias), comm=comm)
    return h, pre, act, got


def _conv_tiles(T):
    tt = _tile(T, 512)
    return tt, tt // SUBLANES, D // LANES


def _fill_strided(ext, p, L):
    main = p[HALO:HALO + SUBLANES * L, :].reshape(SUBLANES, L, LANES)
    ext[CONV_PAD:CONV_PAD + L] = jnp.swapaxes(main, 0, 1)

    def ibody(i, carry):
        ext[i] = p[pl.ds(i + 1, SUBLANES, stride=L), :]
        ext[i + CONV_PAD + L] = p[pl.ds(i + CONV_PAD + L + 1, SUBLANES, stride=L), :]
        return carry

    lax.fori_loop(0, CONV_PAD, ibody, 0, unroll=3)


def _conv_specs(T, tt):
    main = pl.BlockSpec((tt, D), lambda i: (i, 0))
    per = tt // HALO
    prev = pl.BlockSpec((HALO, D), lambda i: (jnp.maximum(i * per - 1, 0), 0))
    nxt = pl.BlockSpec((HALO, D), lambda i: (jnp.minimum((i + 1) * per, T // HALO - 1), 0))
    return main, prev, nxt


def _fill_pad(pad, main_ref, prev_ref, next_ref, i, n_i, tt, nlt):
    keep_p = (i > 0).astype(F32)
    keep_n = (i < n_i - 1).astype(F32)
    for lt in range(nlt):
        sl = slice(lt * LANES, (lt + 1) * LANES)
        pad[lt, 0:HALO, :] = prev_ref[:, sl] * keep_p
        pad[lt, HALO:HALO + tt, :] = main_ref[:, sl]
        pad[lt, HALO + tt:2 * HALO + tt, :] = next_ref[:, sl] * keep_n


def conv_fwd(glu, w_dw, b_dw, ln_g, ln_b, comm=None):
    T = glu.shape[0]
    tt, L, nlt = _conv_tiles(T)
    n_i = T // tt
    main, prev, nxt = _conv_specs(T, tt)

    def body(x_ref, xp_ref, xn_ref, w_ref, b_ref, g_ref, bb_ref, dwc_ref, sw_ref, pad, ob, ext, wk):
        i = pl.program_id(0)
        _fill_pad(pad, x_ref, xp_ref, xn_ref, i, n_i, tt, nlt)
        for lt in range(nlt):
            sl = slice(lt * LANES, (lt + 1) * LANES)
            o = ob.at[lt]
            _fill_strided(ext, pad.at[lt], L)
            for k in range(CONV_W):
                wk[k] = jnp.broadcast_to(w_ref[k:k + 1, sl], (SUBLANES, LANES))

            def jbody(jb, carry):
                j = jb * CONV_JB
                accs = [None] * CONV_JB
                for m in range(CONV_W + CONV_JB - 1):
                    e = ext[j + m]
                    for u in range(CONV_JB):
                        if 0 <= m - u < CONV_W:
                            t = e * wk[m - u]
                            accs[u] = t if accs[u] is None else accs[u] + t
                for u in range(CONV_JB):
                    o[pl.ds(j + u, SUBLANES, stride=L), :] = accs[u]
                return carry

            lax.fori_loop(0, L // CONV_JB, jbody, 0)
        y = jnp.concatenate([ob[lt] for lt in range(nlt)], axis=1) + b_ref[...]
        dwc_ref[...] = y
        mu = jnp.mean(y, axis=-1, keepdims=True)
        yc = y - mu
        var = jnp.mean(yc * yc, axis=-1, keepdims=True)
        z = yc * lax.rsqrt(var + EPS) * g_ref[...] + bb_ref[...]
        sw_ref[...] = (z * _sigmoid(z)).astype(BF16)

    full = lambda i: (0, 0)
    (dwc, sw), got = _call(
        body, name="conv_fwd", grid=(n_i,),
        in_specs=[main, prev, nxt, pl.BlockSpec((32, D), full), pl.BlockSpec((1, D), full),
                  pl.BlockSpec((1, D), full), pl.BlockSpec((1, D), full)],
        out_specs=[pl.BlockSpec((tt, D), lambda i: (i, 0)), pl.BlockSpec((tt, D), lambda i: (i, 0))],
        out_shape=[jax.ShapeDtypeStruct((T, D), F32), jax.ShapeDtypeStruct((T, D), BF16)],
        scratch_shapes=[pltpu.VMEM((nlt, tt + 2 * HALO, LANES), F32), pltpu.VMEM((nlt, tt, LANES), F32),
                        pltpu.VMEM((L + 2 * HALO, SUBLANES, LANES), F32), pltpu.VMEM((32, SUBLANES, LANES), F32)],
        semantics=("parallel",), args=(glu, glu, glu, w_dw, b_dw, ln_g, ln_b), comm=comm)
    return dwc, sw, got


def mm_res_loss(a, w, resid, gain, target):
    T, K = a.shape
    tm = _tile(T, 512)

    def body(a_ref, w_ref, r_ref, g_ref, t_ref, dx_ref, loss_ref, dg_ref):
        @pl.when(pl.program_id(0) == 0)
        def _():
            loss_ref[...] = jnp.zeros_like(loss_ref)
            dg_ref[...] = jnp.zeros_like(dg_ref)

        xv, gain_v = _dot(a_ref[...], w_ref[...]) + r_ref[...], g_ref[...]
        err = _rms_fwd(xv, gain_v) - t_ref[...]
        part = 0.5 * jnp.sum(jnp.mean(err * err, axis=-1, keepdims=True), axis=0, keepdims=True)
        loss_ref[...] += jnp.broadcast_to(part, loss_ref.shape)
        dx, dgr = _rms_bwd(err * (1.0 / D), xv, gain_v, 0.0)
        dx_ref[...] = dx
        dg_ref[...] += jnp.sum(dgr, axis=0, keepdims=True)

    row = lambda i: (i, 0)
    full = lambda i: (0, 0)
    return pl.pallas_call(
        body, name="ffn1_down_loss", grid=(T // tm,),
        in_specs=[pl.BlockSpec((tm, K), row), pl.BlockSpec((K, D), full), pl.BlockSpec((tm, D), row),
                  pl.BlockSpec((1, D), full), pl.BlockSpec((tm, D), row)],
        out_specs=[pl.BlockSpec((tm, D), row), pl.BlockSpec((1, LANES), full), pl.BlockSpec((1, D), full)],
        out_shape=[jax.ShapeDtypeStruct((T, D), F32), jax.ShapeDtypeStruct((1, LANES), F32),
                   jax.ShapeDtypeStruct((1, D), F32)],
        compiler_params=_params("arbitrary"),
    )(a, w, resid, gain, target)


def swiglu_bwd(dx, w_down, pre, name, comm=None):
    T = dx.shape[0]
    H = w_down.shape[0]
    tm = _tile(T, 512)

    def body(dx_ref, w_ref, pre_ref, dpre_ref):
        dxb = dx_ref[...].astype(BF16)
        for cs in _col_chunks(H):
            dact = _dot_tb(dxb, w_ref[cs, :])
            g = pre_ref[0, :, cs].astype(F32)
            u = pre_ref[1, :, cs].astype(F32)
            sg = _sigmoid(g)
            dpre_ref[0, :, cs] = (dact * u * sg * (1.0 + g * (1.0 - sg))).astype(BF16)
            dpre_ref[1, :, cs] = (dact * g * sg).astype(BF16)

    (dpre,), got = _call(
        body, name=name, grid=(T // tm,),
        in_specs=[pl.BlockSpec((tm, D), lambda i: (i, 0)),
                  pl.BlockSpec((H, D), lambda i: (0, 0), pipeline_mode=pl.Buffered(1)),
                  pl.BlockSpec((2, tm, H), lambda i: (0, i, 0))],
        out_specs=[pl.BlockSpec((2, tm, H), lambda i: (0, i, 0))],
        out_shape=[jax.ShapeDtypeStruct((2, T, H), BF16)],
        semantics=("parallel",), args=(dx, w_down, pre), comm=comm)
    return dpre, got


def mm_bt_rmsbwd(dpre, w, x, gain, dres, name, comm=None):
    nh, T, H = dpre.shape
    tm = _tile(T, 512)

    def body(dp_ref, w_ref, x_ref, g_ref, dres_ref, dx_ref, dg_ref):
        @pl.when(pl.program_id(0) == 0)
        def _():
            dg_ref[...] = jnp.zeros_like(dg_ref)

        dh = _dot_tb(dp_ref[0], w_ref[:, 0:H])
        for hf in range(1, nh):
            dh = dh + _dot_tb(dp_ref[hf], w_ref[:, hf * H:(hf + 1) * H])
        dx, dgr = _rms_bwd(dh, x_ref[...], g_ref[...], dres_ref[...])
        dx_ref[...] = dx
        dg_ref[...] += jnp.sum(dgr, axis=0, keepdims=True)

    row = lambda i: (i, 0)
    full = lambda i: (0, 0)
    (dx, dg), got = _call(
        body, name=name, grid=(T // tm,),
        in_specs=[pl.BlockSpec((nh, tm, H), lambda i: (0, i, 0)),
                  pl.BlockSpec((D, nh * H), full, pipeline_mode=pl.Buffered(1)),
                  pl.BlockSpec((tm, D), row), pl.BlockSpec((1, D), full), pl.BlockSpec((tm, D), row)],
        out_specs=[pl.BlockSpec((tm, D), row), pl.BlockSpec((1, D), full)],
        out_shape=[jax.ShapeDtypeStruct((T, D), F32), jax.ShapeDtypeStruct((1, D), F32)],
        semantics=("arbitrary",), args=(dpre, w, x, gain, dres), comm=comm)
    return dx, dg, got


def dw_col(a, dpre, name):
    T = a.shape[0]
    nh, _, H = dpre.shape
    per = nh * H // N_CHIPS
    bph = N_CHIPS // nh
    tt = _tile(T, 2048)
    nt = T // tt

    def body(a_ref, b_ref, o_ref, acc):
        t = pl.program_id(1)

        @pl.when(t == 0)
        def _():
            acc[...] = jnp.zeros_like(acc)

        acc[...] += _dot_ta(a_ref[...], b_ref[...])

        @pl.when(t == nt - 1)
        def _():
            o_ref[...] = acc[...].astype(BF16)

    return pl.pallas_call(
        body, name=name, grid=(N_CHIPS, nt),
        in_specs=[pl.BlockSpec((tt, D), lambda q, t: (t, 0)),
                  pl.BlockSpec((None, tt, per), lambda q, t: (q // bph, t, q % bph))],
        out_specs=pl.BlockSpec((None, D, per), lambda q, t: (q, 0, 0)),
        out_shape=jax.ShapeDtypeStruct((N_CHIPS, D, per), BF16),
        scratch_shapes=[pltpu.VMEM((D, per), F32)],
        compiler_params=_params("parallel", "arbitrary"),
    )(a, dpre)


def dw_row(a, b, name):
    T, R = a.shape
    cw = 1408 if R % 1408 == 0 else 512
    tt = _tile(T, 1024)
    nt = T // tt

    def body(a_ref, b_ref, o_ref, acc):
        t = pl.program_id(1)

        @pl.when(t == 0)
        def _():
            acc[...] = jnp.zeros_like(acc)

        acc[...] += _dot_ta(a_ref[...], b_ref[...].astype(BF16))

        @pl.when(t == nt - 1)
        def _():
            o_ref[...] = acc[...].astype(BF16)

    out = pl.pallas_call(
        body, name=name, grid=(R // cw, nt),
        in_specs=[pl.BlockSpec((tt, cw), lambda q, t: (t, q)), pl.BlockSpec((tt, D), lambda q, t: (t, 0))],
        out_specs=pl.BlockSpec((cw, D), lambda q, t: (q, 0)),
        out_shape=jax.ShapeDtypeStruct((R, D), BF16),
        scratch_shapes=[pltpu.VMEM((cw, D), F32)],
        compiler_params=_params("parallel", "arbitrary"),
    )(a, b)
    return out.reshape(N_CHIPS, R // N_CHIPS, D)


def ln_silu_bwd(dx, w_pw2, dwc, ln_g, ln_b):
    T = dx.shape[0]
    tm = _tile(T, 512)

    def body(dx_ref, w_ref, y_ref, g_ref, b_ref, dy_ref, dg_ref, db_ref, dbo_ref):
        @pl.when(pl.program_id(0) == 0)
        def _():
            dg_ref[...] = jnp.zeros_like(dg_ref)
            db_ref[...] = jnp.zeros_like(db_ref)
            dbo_ref[...] = jnp.zeros_like(dbo_ref)

        dxv = dx_ref[...]
        dsw = _dot_tb(dxv.astype(BF16), w_ref[...])
        y = y_ref[...]
        mu = jnp.mean(y, axis=-1, keepdims=True)
        yc = y - mu
        rstd = lax.rsqrt(jnp.mean(yc * yc, axis=-1, keepdims=True) + EPS)
        xhat = yc * rstd
        z = xhat * g_ref[...] + b_ref[...]
        sg = _sigmoid(z)
        dz = dsw * sg * (1.0 + z * (1.0 - sg))
        dxh = dz * g_ref[...]
        dy_ref[...] = rstd * (dxh - jnp.mean(dxh, axis=-1, keepdims=True)
                              - xhat * jnp.mean(dxh * xhat, axis=-1, keepdims=True))
        dg_ref[...] += jnp.sum(dz * xhat, axis=0, keepdims=True)
        db_ref[...] += jnp.sum(dz, axis=0, keepdims=True)
        dbo_ref[...] += jnp.sum(dxv, axis=0, keepdims=True)

    row = lambda i: (i, 0)
    full = lambda i: (0, 0)
    vec = pl.BlockSpec((1, D), full)
    return pl.pallas_call(
        body, name="ln_silu_bwd", grid=(T // tm,),
        in_specs=[pl.BlockSpec((tm, D), row), pl.BlockSpec((D, D), full), pl.BlockSpec((tm, D), row), vec, vec],
        out_specs=[pl.BlockSpec((tm, D), row), vec, vec, vec],
        out_shape=[jax.ShapeDtypeStruct((T, D), F32)] + [jax.ShapeDtypeStruct((1, D), F32)] * 3,
        compiler_params=_params("arbitrary"),
    )(dx, w_pw2, dwc, ln_g, ln_b)


def conv_bwd(ddwc, glu, pre, w_dw, comm=None):
    T = ddwc.shape[0]
    tt, L, nlt = _conv_tiles(T)
    n_i = T // tt
    main, prev, nxt = _conv_specs(T, tt)

    def body(d_ref, dp_ref, dn_ref, x_ref, xp_ref, xn_ref, pre_ref, w_ref,
             dpre_ref, dw_ref, dbd_ref, dbp_ref, padd, padx, ob, extd, extx, wk):
        i = pl.program_id(0)

        @pl.when(i == 0)
        def _():
            dw_ref[...] = jnp.zeros_like(dw_ref)
            dbd_ref[...] = jnp.zeros_like(dbd_ref)
            dbp_ref[...] = jnp.zeros_like(dbp_ref)

        _fill_pad(padd, d_ref, dp_ref, dn_ref, i, n_i, tt, nlt)
        _fill_pad(padx, x_ref, xp_ref, xn_ref, i, n_i, tt, nlt)
        for lt in range(nlt):
            sl = slice(lt * LANES, (lt + 1) * LANES)
            o = ob.at[lt]
            _fill_strided(extd, padd.at[lt], L)
            _fill_strided(extx, padx.at[lt], L)
            for k in range(CONV_W):
                wk[k] = jnp.broadcast_to(w_ref[k:k + 1, sl], (SUBLANES, LANES))

            nu = CONV_JB_BWD

            def jbody(jb, accs):
                j = jb * nu
                accs = list(accs)
                d = [extd[j + u + CONV_PAD] for u in range(nu)]
                g = [None] * nu
                for m in range(CONV_W + nu - 1):
                    ed = extd[j + 2 * CONV_PAD + nu - 1 - m]
                    ex = extx[j + m]
                    for u in range(nu):
                        k = m - (nu - 1 - u)
                        if 0 <= k < CONV_W:
                            t = ed * wk[k]
                            g[u] = t if g[u] is None else g[u] + t
                        k = m - u
                        if 0 <= k < CONV_W:
                            accs[k] = accs[k] + d[u] * ex
                for u in range(nu):
                    o[pl.ds(j + u, SUBLANES, stride=L), :] = g[u]
                return tuple(accs)

            accs = lax.fori_loop(0, L // nu, jbody, tuple(jnp.zeros((SUBLANES, LANES), F32) for _ in range(CONV_W)))
            for k in range(CONV_W):
                dw_ref[k:k + 1, sl] += jnp.sum(accs[k], axis=0, keepdims=True)
        dglu = jnp.concatenate([ob[lt] for lt in range(nlt)], axis=1)
        a = pre_ref[0].astype(F32)
        gate = pre_ref[1].astype(F32)
        sg = _sigmoid(gate)
        da = dglu * sg
        dgate = dglu * a * sg * (1.0 - sg)
        dpre_ref[0] = da.astype(BF16)
        dpre_ref[1] = dgate.astype(BF16)
        dbd_ref[...] += jnp.sum(d_ref[...], axis=0, keepdims=True)
        dbp_ref[0] += jnp.sum(da, axis=0, keepdims=True)
        dbp_ref[1] += jnp.sum(dgate, axis=0, keepdims=True)

    full = lambda i: (0, 0)
    (dpre, dw, dbd, dbp), got = _call(
        body, name="conv_bwd", grid=(n_i,),
        in_specs=[main, prev, nxt, main, prev, nxt, pl.BlockSpec((2, tt, D), lambda i: (0, i, 0)),
                  pl.BlockSpec((32, D), full)],
        out_specs=[pl.BlockSpec((2, tt, D), lambda i: (0, i, 0)), pl.BlockSpec((32, D), full),
                   pl.BlockSpec((1, D), full), pl.BlockSpec((2, 1, D), lambda i: (0, 0, 0))],
        out_shape=[jax.ShapeDtypeStruct((2, T, D), BF16), jax.ShapeDtypeStruct((32, D), F32),
                   jax.ShapeDtypeStruct((1, D), F32), jax.ShapeDtypeStruct((2, 1, D), F32)],
        scratch_shapes=[pltpu.VMEM((nlt, tt + 2 * HALO, LANES), F32), pltpu.VMEM((nlt, tt + 2 * HALO, LANES), F32),
                        pltpu.VMEM((nlt, tt, LANES), F32), pltpu.VMEM((L + 2 * HALO, SUBLANES, LANES), F32),
                        pltpu.VMEM((L + 2 * HALO, SUBLANES, LANES), F32), pltpu.VMEM((32, SUBLANES, LANES), F32)],
        semantics=("arbitrary",), args=(ddwc, ddwc, ddwc, glu, glu, glu, pre, w_dw), comm=comm)
    return dpre, dw, dbd, dbp, got


def mm_bt(a, w, name):
    T = a.shape[0]
    N = w.shape[0]
    tm = _tile(T, 512)

    def body(a_ref, w_ref, o_ref):
        o_ref[...] = _dot_tb(a_ref[...].astype(BF16), w_ref[...]).astype(BF16)

    return pl.pallas_call(
        body, name=name, grid=(T // tm,),
        in_specs=[pl.BlockSpec((tm, D), lambda i: (i, 0)), pl.BlockSpec((N, D), lambda i: (0, 0))],
        out_specs=pl.BlockSpec((tm, N), lambda i: (i, 0)),
        out_shape=jax.ShapeDtypeStruct((T, N), BF16),
        compiler_params=_params("parallel"),
    )(a, w)


def attn_bwd(qkv, o, do, sink, rc, rs1, rs2, comm=None):
    T = qkv.shape[0]
    nb, q_spec, prev, own, nxt = _attn_specs(T)
    kvw = N_KV * HD

    def body(sink_ref, q_ref, kp_ref, ko_ref, kn_ref, o_ref, do_ref, c_ref, s1_ref, s2_ref,
             dq_ref, dkc_ref, dvc_ref, dsink_ref):
        n = pl.program_id(0)

        @pl.when(n == 0)
        def _():
            dsink_ref[...] = jnp.zeros_like(dsink_ref)

        valid = _attn_mask(n, T)
        kv = jnp.concatenate([kp_ref[...], ko_ref[...], kn_ref[...]], axis=0)
        kx, vx = _kv_padded(kv, 0), _kv_padded(kv, 2)
        tile = lambda ref, j: ref[:, j * LANES:(j + 1) * LANES]
        ss = [_dot_tb(kx[h // GROUP, h % 2], tile(q_ref, h // 2)) for h in range(N_HEADS)]
        dps = [_dot_tb(vx[h // GROUP, h % 2], tile(do_ref, h // 2)) for h in range(N_HEADS)]
        low_d = lax.broadcasted_iota(jnp.int32, (LANES, BLK), 0) < HD
        deltas = []
        for j in range(N_HEADS // 2):
            prod_t = tile(do_ref, j).astype(F32).T * tile(o_ref, j).astype(F32).T
            deltas.append(jnp.sum(jnp.where(low_d, prod_t, 0.0), axis=0, keepdims=True))
            deltas.append(jnp.sum(jnp.where(low_d, 0.0, prod_t), axis=0, keepdims=True))
        lane = lax.broadcasted_iota(jnp.int32, (1, N_HEADS), 1)
        dsink = jnp.zeros((1, N_HEADS), F32)
        pbs, dss = [], []
        for h in range(N_HEADS):
            p, p_sink = _softmax_sink(ss[h], valid, sink_ref[h])
            dss.append((p * (dps[h] - deltas[h])).astype(BF16))
            pbs.append(p.astype(BF16))
            part = -jnp.sum(p_sink * deltas[h], axis=1, keepdims=True)
            dsink = dsink + jnp.where(lane == h, part, 0.0)
        dsink_ref[...] += dsink
        c, s1, s2 = c_ref[...], s1_ref[...], s2_ref[...]
        kxt = {k: v.T for k, v in kx.items()}
        for j in range(N_HEADS // 2):
            g = 2 * j // GROUP
            dq_t = _dot(kxt[g, 0], dss[2 * j]) + _dot(kxt[g, 1], dss[2 * j + 1])
            dq_ref[:, j * LANES:(j + 1) * LANES] = (_rope(dq_t.T, c, -s1, -s2) * Q_SCALE).astype(BF16)
        low_k = lax.broadcasted_iota(jnp.int32, (3 * BLK, LANES), 1) < HD
        cols = lambda xs, g, p: jnp.concatenate([xs[GROUP * g + p], xs[GROUP * g + 2 + p]], axis=1)
        for t in range(N_KV // 2):
            sums = {}
            for g in (2 * t, 2 * t + 1):
                q2 = jnp.concatenate([tile(q_ref, 2 * g), tile(q_ref, 2 * g + 1)], axis=0)
                do2 = jnp.concatenate([tile(do_ref, 2 * g), tile(do_ref, 2 * g + 1)], axis=0)
                for p in range(2):
                    sums[g, p] = (_dot(cols(dss, g, p), q2), _dot(cols(pbs, g, p), do2))
            for which, ref in ((0, dkc_ref), (1, dvc_ref)):
                keep = jnp.where(low_k, sums[2 * t, 0][which], sums[2 * t + 1, 1][which])
                swap = jnp.where(low_k, sums[2 * t + 1, 0][which], sums[2 * t, 1][which])
                ref[:, t * LANES:(t + 1) * LANES] = keep + pltpu.roll(swap, HD, 1)

    row = lambda n: (n, 0)
    (dq, dkc, dvc, dsink), got = _call(
        body, name="attn_bwd", grid=(nb,),
        in_specs=[pl.BlockSpec(memory_space=pltpu.SMEM), q_spec, prev, own, nxt,
                  pl.BlockSpec((BLK, D), row), pl.BlockSpec((BLK, D), row), *_tab_specs(BLK)],
        out_specs=[pl.BlockSpec((BLK, D), row), pl.BlockSpec((None, 3 * BLK, kvw), lambda n: (n, 0, 0)),
                   pl.BlockSpec((None, 3 * BLK, kvw), lambda n: (n, 0, 0)), pl.BlockSpec((1, N_HEADS), lambda n: (0, 0))],
        out_shape=[jax.ShapeDtypeStruct((T, QKV), BF16), jax.ShapeDtypeStruct((nb, 3 * BLK, kvw), F32),
                   jax.ShapeDtypeStruct((nb, 3 * BLK, kvw), F32), jax.ShapeDtypeStruct((1, N_HEADS), F32)],
        semantics=("arbitrary",), args=(sink, qkv, qkv, qkv, qkv, o, do, rc, rs1, rs2), comm=comm)
    return dq, dkc, dvc, dsink, got


def kv_sum(dqkv, dkc, dvc, rc, rs1, rs2):
    nb = dkc.shape[0]
    T = nb * BLK
    kvw = N_KV * HD

    def body(_, kp_ref, ko_ref, kn_ref, vp_ref, vo_ref, vn_ref, c_ref, s1_ref, s2_ref, out_ref):
        m = pl.program_id(0)
        has_p = (m > 0).astype(F32)
        has_n = (m < nb - 1).astype(F32)
        dk = kp_ref[...] * has_p + ko_ref[...] + kn_ref[...] * has_n
        dv = vp_ref[...] * has_p + vo_ref[...] + vn_ref[...] * has_n
        c, s1, s2 = c_ref[...], s1_ref[...], s2_ref[...]
        for j in range(kvw // LANES):
            sl = slice(LANES * j, LANES * (j + 1))
            out_ref[:, sl] = _rope(dk[:, sl], c, -s1, -s2).astype(BF16)
        out_ref[:, kvw:] = dv.astype(BF16)

    from_prev = pl.BlockSpec((None, BLK, kvw), lambda m: (jnp.maximum(m - 1, 0), 2, 0))
    from_own = pl.BlockSpec((None, BLK, kvw), lambda m: (m, 1, 0))
    from_next = pl.BlockSpec((None, BLK, kvw), lambda m: (jnp.minimum(m + 1, nb - 1), 0, 0))
    return pl.pallas_call(
        body, name="kv_sum", grid=(nb,),
        in_specs=[pl.BlockSpec(memory_space=pl.ANY), from_prev, from_own, from_next, from_prev, from_own, from_next,
                  *_tab_specs(BLK)],
        out_specs=pl.BlockSpec((BLK, 2 * kvw), lambda m: (m, KV_OFF // (2 * kvw))),
        out_shape=jax.ShapeDtypeStruct((T, QKV), BF16),
        input_output_aliases={0: 0},
        compiler_params=_params("parallel"),
    )(dqkv, dkc, dkc, dkc, dvc, dvc, dvc, rc, rs1, rs2)


def _me():
    return lax.axis_index("x"), lax.axis_index("y"), lax.axis_index("c")


def _half_rows(ref, sharded_rows, chip, core):
    R, C = ref.shape[-2], ref.shape[-1]
    lead = (slice(None),) * (len(ref.shape) - 2)
    if sharded_rows:
        per = R // N_CHIPS
        return ref.at[lead + (pl.ds(chip * per + core * (per // 2), per // 2), slice(None))]
    per = C // N_CHIPS
    return ref.at[lead + (pl.ds(core * (R // 2), R // 2), pl.ds(chip * per, per))]


class _Gather:
    def __init__(self, shards, sharded_rows):
        self.inputs = list(shards)
        self.rows = list(sharded_rows)
        self.n = self.n_in = self.n_out = len(shards)
        self.out_shapes = []
        for s, rows in zip(shards, sharded_rows):
            shp = list(s.shape)
            shp[-2 if rows else -1] *= N_CHIPS
            self.out_shapes.append(jax.ShapeDtypeStruct(tuple(shp), s.dtype))
        self.scratch = [pltpu.SemaphoreType.DMA((self.n, 6)), pltpu.SemaphoreType.DMA((self.n, 6)),
                        pltpu.SemaphoreType.DMA((self.n, 2))]

    def _ctx(self, ins, outs, sems):
        send_sems, recv_sems, local_sems = sems
        x, y, c = _me()
        chips = [(1 - x, y), (x, 1 - y), (1 - x, 1 - y)]

        def half_src(w, core):
            s = ins[w]
            R = s.shape[-2]
            return s.at[pl.ds(core * (R // 2), R // 2), :]

        def dst(w, chip, core):
            return _half_rows(outs[w], self.rows[w], chip, core)

        def copy(w, k, src, chip, core, to):
            return pltpu.make_async_remote_copy(
                src_ref=src, dst_ref=dst(w, chip, core), send_sem=send_sems.at[w, k], recv_sem=recv_sems.at[w, k],
                device_id=to, device_id_type=MESH)

        def local(w, core):
            return pltpu.make_async_copy(half_src(w, core), dst(w, 2 * x + y, core), local_sems.at[w, core])

        def first(w, j):
            qx, qy = chips[j]
            return copy(w, j, half_src(w, c), 2 * x + y, c, (qx, qy, c))

        def landed(w, j):
            qx, qy = chips[j]
            return copy(w, j, dst(w, 2 * qx + qy, c), 2 * qx + qy, c, (x, y, c))

        def passed(w, j):
            qx, qy = chips[j]
            return copy(w, 3 + j, dst(w, 2 * qx + qy, c), 2 * qx + qy, c, (x, y, 1 - c))

        def from_sibling(w, j):
            qx, qy = chips[j]
            return copy(w, 3 + j, dst(w, 2 * qx + qy, 1 - c), 2 * qx + qy, 1 - c, (x, y, c))

        return local, first, landed, passed, from_sibling

    def start(self, ins, outs, sems):
        local, first, _, _, _ = self._ctx(ins, outs, sems)
        for w in range(self.n):
            for core in range(2):
                local(w, core).start()
            for j in range(3):
                first(w, j).start()

    def mid(self, ins, outs, sems):
        _, _, landed, passed, _ = self._ctx(ins, outs, sems)
        for w in range(self.n):
            for j in range(3):
                landed(w, j).wait_recv()
                passed(w, j).start()

    def end(self, ins, outs, sems):
        local, first, _, passed, from_sibling = self._ctx(ins, outs, sems)
        for w in range(self.n):
            for j in range(3):
                from_sibling(w, j).wait_recv()
        for w in range(self.n):
            for j in range(3):
                first(w, j).wait_send()
                passed(w, j).wait_send()
            for core in range(2):
                local(w, core).wait()


class _Scatter:
    def __init__(self, grads, small=None):
        self.inputs = list(grads) + ([small] if small is not None else [])
        self.ng = len(grads)
        self.n = self.n_in = self.n_out = len(self.inputs)
        self.out_shapes = [jax.ShapeDtypeStruct((N_DEV, g.shape[1] // 2, g.shape[2]), g.dtype) for g in grads]
        if small is not None:
            self.out_shapes.append(jax.ShapeDtypeStruct((N_DEV,) + small.shape, small.dtype))
        self.scratch = [pltpu.SemaphoreType.DMA((self.n, N_DEV)), pltpu.SemaphoreType.DMA((self.n, N_DEV)),
                        pltpu.SemaphoreType.DMA((self.n,))]

    def _ctx(self, ins, outs, sems):
        send_sems, recv_sems, local_sems = sems
        x, y, c = _me()
        me = 4 * x + 2 * y + c

        def piece(w, chip, core):
            if w >= self.ng:
                return ins[w]
            half = ins[w].shape[1] // 2
            return ins[w].at[chip, pl.ds(core * half, half), :]

        def peer_of(k):
            return x ^ ((k >> 2) & 1), y ^ ((k >> 1) & 1), c ^ (k & 1)

        def local(w):
            return pltpu.make_async_copy(piece(w, 2 * x + y, c), outs[w].at[me], local_sems.at[w])

        def send(w, k):
            px, py, pc = peer_of(k)
            return pltpu.make_async_remote_copy(
                src_ref=piece(w, 2 * px + py, pc), dst_ref=outs[w].at[me], send_sem=send_sems.at[w, k],
                recv_sem=recv_sems.at[w, k], device_id=(px, py, pc), device_id_type=MESH)

        def recv(w, k):
            px, py, pc = peer_of(k)
            return pltpu.make_async_remote_copy(
                src_ref=piece(w, 2 * x + y, c), dst_ref=outs[w].at[4 * px + 2 * py + pc], send_sem=send_sems.at[w, k],
                recv_sem=recv_sems.at[w, k], device_id=(px, py, pc), device_id_type=MESH)

        return local, send, recv

    def start(self, ins, outs, sems):
        local, send, _ = self._ctx(ins, outs, sems)
        for w in range(self.n):
            local(w).start()
            for k in range(1, N_DEV):
                send(w, k).start()

    def mid(self, ins, outs, sems):
        pass

    def end(self, ins, outs, sems):
        local, send, recv = self._ctx(ins, outs, sems)
        for w in range(self.n):
            for k in range(1, N_DEV):
                recv(w, k).wait_recv()
        for w in range(self.n):
            for k in range(1, N_DEV):
                send(w, k).wait_send()
            local(w).wait()


class _Both:
    def __init__(self, a, b):
        self.a, self.b = a, b
        self.inputs = a.inputs + b.inputs
        self.out_shapes = a.out_shapes + b.out_shapes
        self.scratch = a.scratch + b.scratch
        self.n_in, self.n_out = a.n_in + b.n_in, a.n_out + b.n_out

    def _split(self, ins, outs, sems):
        a, na = self.a, len(self.a.scratch)
        return (ins[:a.n_in], outs[:a.n_out], sems[:na]), (ins[a.n_in:], outs[a.n_out:], sems[na:])

    def start(self, ins, outs, sems):
        pa, pb = self._split(ins, outs, sems)
        self.a.start(*pa)
        self.b.start(*pb)

    def mid(self, ins, outs, sems):
        pa, pb = self._split(ins, outs, sems)
        self.a.mid(*pa)
        self.b.mid(*pb)

    def end(self, ins, outs, sems):
        pa, pb = self._split(ins, outs, sems)
        self.a.end(*pa)
        self.b.end(*pb)


def exchange(plan, name):
    def body(*refs):
        ins, outs, sems = refs[:plan.n_in], refs[plan.n_in:plan.n_in + plan.n_out], refs[plan.n_in + plan.n_out:]
        plan.start(ins, outs, sems)
        plan.mid(ins, outs, sems)
        plan.end(ins, outs, sems)

    any_spec = pl.BlockSpec(memory_space=pl.ANY)
    return pl.pallas_call(
        body, name=name, in_specs=[any_spec] * plan.n_in, out_specs=[any_spec] * plan.n_out,
        out_shape=plan.out_shapes, scratch_shapes=plan.scratch,
    )(*plan.inputs)


def _call(body, *, name, grid, in_specs, out_specs, out_shape, scratch_shapes=(), semantics, args, comm=None):
    if comm is None:
        outs = pl.pallas_call(
            body, name=name, grid=grid, in_specs=in_specs, out_specs=out_specs, out_shape=out_shape,
            scratch_shapes=list(scratch_shapes), compiler_params=_params(*semantics))(*args)
        return outs, []
    n_in, n_out, n_scr = len(in_specs), len(out_specs), len(scratch_shapes)

    total = math.prod(grid)
    first, middle, last = 0, (3 * total) // 4 - 1, total - 1
    assert first <= middle < last

    def at(step):
        lin = pl.program_id(0)
        for d in range(1, len(grid)):
            lin = lin * grid[d] + pl.program_id(d)
        return lin == step

    def hosted(*refs):
        h_in, c_in = refs[:n_in], refs[n_in:n_in + comm.n_in]
        rest = refs[n_in + comm.n_in:]
        h_out, c_out = rest[:n_out], rest[n_out:n_out + comm.n_out]
        rest = rest[n_out + comm.n_out:]
        h_scr, c_scr = rest[:n_scr], rest[n_scr:]

        @pl.when(at(first))
        def _():
            comm.start(c_in, c_out, c_scr)

        body(*h_in, *h_out, *h_scr)

        @pl.when(at(middle))
        def _():
            comm.mid(c_in, c_out, c_scr)

        @pl.when(at(last))
        def _():
            comm.end(c_in, c_out, c_scr)

    any_spec = pl.BlockSpec(memory_space=pl.ANY)
    outs = pl.pallas_call(
        hosted, name=name, grid=grid, in_specs=list(in_specs) + [any_spec] * comm.n_in,
        out_specs=list(out_specs) + [any_spec] * comm.n_out, out_shape=list(out_shape) + comm.out_shapes,
        scratch_shapes=list(scratch_shapes) + comm.scratch,
        compiler_params=_params(*(["arbitrary"] * len(grid))))(*args, *comm.inputs)
    return outs[:n_out], outs[n_out:]


def sum_swap(pieces, name):
    nl = len(pieces)
    _, r2, cc = pieces[0].shape
    tr = 128 if r2 % 128 == 0 else r2 // 2
    n = r2 // tr

    def body(*refs):
        p_refs, out = refs[:nl], refs[nl]
        slots, send_sems, local_sems, recv_sem = refs[nl + 1:]
        x, y, c = _me()
        sibling = (x, y, 1 - c)
        l, i = pl.program_id(0), pl.program_id(1)
        step = l * n + i

        def rows(st, core):
            return out.at[st // n, pl.ds(core * r2 + (st % n) * tr, tr), :]

        def copies(st):
            slot = st % 2
            local = pltpu.make_async_copy(slots.at[slot], rows(st, c), local_sems.at[slot])
            remote = pltpu.make_async_remote_copy(
                src_ref=slots.at[slot], dst_ref=rows(st, c), send_sem=send_sems.at[slot], recv_sem=recv_sem,
                device_id=sibling, device_id_type=MESH)
            return local, remote

        for ll in range(nl):
            @pl.when(l == ll)
            def _():
                acc = p_refs[ll][0].astype(F32)
                for d in range(1, N_DEV):
                    acc = acc + p_refs[ll][d].astype(F32)
                slots[step % 2] = acc

        for cp in copies(step):
            cp.start()

        @pl.when(step >= 1)
        def _():
            local, remote = copies(step - 1)
            local.wait()
            remote.wait_send()

        @pl.when(step == nl * n - 1)
        def _():
            local, remote = copies(step)
            local.wait()
            remote.wait_send()
            theirs = out.at[:, pl.ds((1 - c) * r2, r2), :]
            pltpu.make_async_remote_copy(src_ref=theirs, dst_ref=theirs, send_sem=send_sems.at[0],
                                         recv_sem=recv_sem, device_id=sibling, device_id_type=MESH).wait_recv()

    def piece_spec(ll):
        def index(l, i):
            return (0, jnp.where(l == ll, i, jnp.where(l < ll, 0, n - 1)), 0)
        return pl.BlockSpec((N_DEV, tr, cc), index)

    return pl.pallas_call(
        body, name=name, grid=(nl, n),
        in_specs=[piece_spec(ll) for ll in range(nl)],
        out_specs=pl.BlockSpec(memory_space=pl.ANY),
        out_shape=jax.ShapeDtypeStruct((nl, 2 * r2, cc), F32),
        scratch_shapes=[pltpu.VMEM((2, tr, cc), F32), pltpu.SemaphoreType.DMA((2,)), pltpu.SemaphoreType.DMA((2,)),
                        pltpu.SemaphoreType.DMA(())],
        compiler_params=_params("arbitrary", "arbitrary"),
    )(*pieces)


def sum_pieces(pieces, name):
    _, R, C = pieces.shape
    tr = _tile(R, 128) if R % 128 == 0 else R

    def body(p_ref, o_ref):
        acc = p_ref[0].astype(F32)
        for d in range(1, N_DEV):
            acc = acc + p_ref[d].astype(F32)
        o_ref[...] = acc

    return pl.pallas_call(
        body, name=name, grid=(R // tr,),
        in_specs=[pl.BlockSpec((N_DEV, tr, C), lambda i: (0, i, 0))],
        out_specs=pl.BlockSpec((tr, C), lambda i: (i, 0)),
        out_shape=jax.ShapeDtypeStruct((R, C), F32),
        compiler_params=_params("parallel"),
    )(pieces)


def adamw(w, g, m, v, name):
    Lyr, R, C = w.shape
    tr = _tile(R, 256) if R % 8 == 0 else R
    c1 = 1.0 / (1.0 - ADAM_B1 ** ADAM_STEP)
    c2 = 1.0 / (1.0 - ADAM_B2 ** ADAM_STEP)

    def body(w_ref, g_ref, m_ref, v_ref, d_ref, nm_ref, nv_ref):
        gv = g_ref[...]
        nm = ADAM_B1 * m_ref[...] + (1.0 - ADAM_B1) * gv
        nv = ADAM_B2 * v_ref[...] + (1.0 - ADAM_B2) * (gv * gv)
        nm_ref[...] = nm
        nv_ref[...] = nv
        d_ref[...] = -ADAM_LR * ((nm * c1) / (jnp.sqrt(nv * c2) + ADAM_EPS) + ADAM_WD * w_ref[...])

    spec = pl.BlockSpec((None, tr, C), lambda l, i: (l, i, 0))
    shp = jax.ShapeDtypeStruct(w.shape, F32)
    return pl.pallas_call(
        body, name=name, grid=(Lyr, R // tr),
        in_specs=[spec] * 4, out_specs=[spec] * 3, out_shape=[shp] * 3,
        compiler_params=_params("parallel", "parallel"),
    )(w, g, m, v)


def _rope_tables(T):
    pos = jnp.arange(T, dtype=F32)
    inv_freq = THETA ** (-jnp.arange(0, ROT, 2, dtype=F32) / ROT)
    ang = pos[:, None] * inv_freq[None, :]
    cs = jnp.concatenate([jnp.cos(ang), jnp.sin(ang)], axis=1)
    half = ROT // 2
    lane = jnp.arange(3 * LANES)
    table, lm = lane // LANES, lane % HD
    src = jnp.where(table == 0, lm % half, half + lm % half)
    i32 = lambda b: b.astype(jnp.int32)
    sign = jnp.where(table == 0, i32(lm < ROT), jnp.where(table == 1, -i32(lm < half), i32((lm >= half) & (lm < ROT))))
    place = (jnp.arange(ROT)[:, None] == src[None, :]) * sign[None, :].astype(F32)
    ones = ((table == 0) & (lm >= ROT)).astype(F32)
    return jnp.dot(cs, place, precision=lax.Precision.HIGHEST) + ones[None, :]


def _tab_specs(rows):
    return [pl.BlockSpec((rows, LANES), lambda i, k=k: (i, k)) for k in range(3)]


def kernel(x, attn_norm, attn_w_qkv, attn_w_o, attn_sink, conv_norm, conv_w_pw1, conv_b_pw1, conv_w_dw, conv_b_dw, conv_ln_g, conv_ln_b, conv_w_pw2, conv_b_pw2, ffn_norm, ffn_w_gu, ffn_w_down, final_norm, loss_target, m_attn_norm, m_attn_w_qkv, m_attn_w_o, m_attn_sink, m_conv_norm, m_conv_w_pw1, m_conv_b_pw1, m_conv_w_dw, m_conv_b_dw, m_conv_ln_g, m_conv_ln_b, m_conv_w_pw2, m_conv_b_pw2, m_ffn_norm, m_ffn_w_gu, m_ffn_w_down, m_final_norm, v_attn_norm, v_attn_w_qkv, v_attn_w_o, v_attn_sink, v_conv_norm, v_conv_w_pw1, v_conv_b_pw1, v_conv_w_dw, v_conv_b_dw, v_conv_ln_g, v_conv_ln_b, v_conv_w_pw2, v_conv_b_pw2, v_ffn_norm, v_ffn_w_gu, v_ffn_w_down, v_final_norm):
    T = x.shape[1]
    x0 = x[0]
    target = loss_target[0]
    ix, iy = lax.axis_index("x"), lax.axis_index("y")
    chip = 2 * ix + iy
    rc = rs1 = rs2 = _rope_tables(T)

    bf = lambda t: t.astype(BF16)
    col_row = [False, True]

    def place(vec, width):
        return lax.dynamic_update_slice(jnp.zeros((vec.shape[0], N_CHIPS * width), F32), vec, (0, chip * width))

    small_rows = jnp.concatenate([
        place(conv_norm, 256), place(conv_b_pw1, 512).reshape(2, D), place(conv_b_dw, 256), place(conv_ln_g, 256),
        place(conv_ln_b, 256), place(conv_b_pw2, 256), jnp.zeros((1, D), F32),
        place(conv_w_dw[0], 256), jnp.zeros((1, D), F32)], axis=0)
    w_qkv, w_o, got = exchange(_Both(_Gather([bf(attn_w_qkv[0]), bf(attn_w_o[0])], col_row),
                                     _Scatter([], small_rows)), "gather_attn")
    psmall = sum_pieces(got, "sum_small_params") * 0.5
    p_conv_norm, p_b_pw1 = psmall[0:1], psmall[1:3].reshape(1, 2 * D)
    p_b_dw, p_ln_g, p_ln_b, p_b_pw2 = psmall[3:4], psmall[4:5], psmall[5:6], psmall[6:7]
    p_w_dw = psmall[8:40]

    h0, qkv = rms_qkv(x0, attn_norm, w_qkv, rc, rs1, rs2)
    sink = attn_sink[0]
    o, (w_gu0,) = attn_fwd(qkv, sink, comm=_Gather([bf(ffn_w_gu[0])], [False]))
    zero_b = jnp.zeros((1, D), F32)
    x1 = mm_res(o, w_o, x0, zero_b, "attn_out")
    zero_gu = jnp.zeros((1, 2 * DFF), F32)
    h1, gu0, act0, (w_down0, w_pw1, w_pw2) = rms_mm_gate(
        x1, ffn_norm[0:1], w_gu0, zero_gu, DFF, True, BF16, "ffn0_up",
        comm=_Gather([bf(ffn_w_down[0]), bf(conv_w_pw1[0]), bf(conv_w_pw2[0])], [True, False, True]))
    x2 = mm_res(act0, w_down0, x1, zero_b, "ffn0_down")
    h2, pre, glu, _ = rms_mm_gate(x2, p_conv_norm, w_pw1, p_b_pw1, D, False, F32, "conv_pw1")
    dwc, sw, (w_gu1, w_down1) = conv_fwd(glu, p_w_dw, p_b_dw, p_ln_g, p_ln_b,
                                         comm=_Gather([bf(ffn_w_gu[1]), bf(ffn_w_down[1])], col_row))
    x3 = mm_res(sw, w_pw2, x2, p_b_pw2, "conv_pw2")
    h3, gu1, act1, _ = rms_mm_gate(x3, ffn_norm[1:2], w_gu1, zero_gu, DFF, True, BF16, "ffn1_up")
    dx4, loss_part, d_final = mm_res_loss(act1, w_down1, x3, final_norm.reshape(1, D), target)
    loss = lax.psum(loss_part[0, 0], ("x", "y", "c"))

    dgu1, _ = swiglu_bwd(dx4, w_down1, gu1, "ffn1_down_bwd")
    g_down1 = dw_row(act1, dx4, "ffn1_down_dw")
    dx3, d_ffn1, _ = mm_bt_rmsbwd(dgu1, w_gu1, x3, ffn_norm[1:2], dx4, "ffn1_up_bwd")
    g_gu1 = dw_col(h3, dgu1, "ffn1_up_dw")

    ddwc, d_ln_g, d_ln_b, d_b_pw2 = ln_silu_bwd(dx3, w_pw2, dwc, p_ln_g, p_ln_b)
    g_pw2 = dw_row(sw, dx3, "conv_pw2_dw")
    dpre, d_w_dw, d_b_dw, d_b_pw1, (r_gu1, r_down1) = conv_bwd(ddwc, glu, pre, p_w_dw,
                                                               comm=_Scatter([g_gu1, g_down1]))
    dx2, d_conv_norm, _ = mm_bt_rmsbwd(dpre, w_pw1, x2, p_conv_norm, dx3, "conv_pw1_bwd")
    g_pw1 = dw_col(h2, dpre, "conv_pw1_dw")

    dgu0, (r_pw1, r_pw2) = swiglu_bwd(dx2, w_down0, gu0, "ffn0_down_bwd", comm=_Scatter([g_pw1, g_pw2]))
    g_down0 = dw_row(act0, dx2, "ffn0_down_dw")
    dx1, d_ffn0, _ = mm_bt_rmsbwd(dgu0, w_gu0, x1, ffn_norm[0:1], dx2, "ffn0_up_bwd")
    g_gu0 = dw_col(h1, dgu0, "ffn0_up_dw")

    do = mm_bt(dx1, w_o, "attn_out_bwd")
    g_o = dw_row(o, dx1, "attn_out_dw")
    dq, dkc, dvc, d_sink, (r_gu0, r_down0, r_o) = attn_bwd(qkv, o, do, sink, rc, rs1, rs2,
                                                           comm=_Scatter([g_gu0, g_down0, g_o]))
    dqkv = kv_sum(dq, dkc, dvc, rc, rs1, rs2)[None]
    g_qkv = dw_col(h0, dqkv, "attn_qkv_dw")
    dx0, d_attn_norm, (r_qkv,) = mm_bt_rmsbwd(dqkv, w_qkv, x0, attn_norm, dx1, "attn_qkv_bwd",
                                              comm=_Scatter([g_qkv]))

    pad16 = lambda t: jnp.concatenate([t, jnp.zeros((1, D - t.shape[1]), F32)], axis=1)
    small_g = jnp.concatenate([
        d_attn_norm, pad16(d_sink), d_conv_norm, d_b_pw1.reshape(2, D), d_b_dw, d_ln_g, d_ln_b, d_b_pw2,
        d_ffn0, d_ffn1, d_final, jnp.zeros((4, D), F32), d_w_dw], axis=0)
    r_small, = exchange(_Scatter([], small_g), "scatter_small")
    gf_gu = sum_swap([r_gu0, r_gu1], "sum_gu")
    gf_down = sum_swap([r_down0, r_down1], "sum_down")
    gf_pw1, gf_pw2 = sum_swap([r_pw1], "sum_pw1"), sum_swap([r_pw2], "sum_pw2")
    gf_qkv, gf_o = sum_swap([r_qkv], "sum_qkv"), sum_swap([r_o], "sum_o")
    gs = sum_pieces(r_small, "sum_small_grads")

    def take(row0, nrows, width):
        return lax.dynamic_slice(gs, (row0, chip * width), (nrows, width))

    grads = {
        "attn_norm": gs[0:1], "attn_w_qkv": gf_qkv, "attn_w_o": gf_o, "attn_sink": gs[1:2, :N_HEADS],
        "conv_norm": take(2, 1, 256), "conv_w_pw1": gf_pw1,
        "conv_b_pw1": lax.dynamic_slice(gs[3:5].reshape(1, 2 * D), (0, chip * 512), (1, 512)),
        "conv_w_dw": take(16, 32, 256)[None, :CONV_W], "conv_b_dw": take(5, 1, 256), "conv_ln_g": take(6, 1, 256),
        "conv_ln_b": take(7, 1, 256), "conv_w_pw2": gf_pw2, "conv_b_pw2": take(8, 1, 256),
        "ffn_norm": gs[9:11], "ffn_w_gu": gf_gu, "ffn_w_down": gf_down, "final_norm": gs[11],
    }
    weights = dict(attn_norm=attn_norm, attn_w_qkv=attn_w_qkv, attn_w_o=attn_w_o, attn_sink=attn_sink,
                   conv_norm=conv_norm, conv_w_pw1=conv_w_pw1, conv_b_pw1=conv_b_pw1, conv_w_dw=conv_w_dw,
                   conv_b_dw=conv_b_dw, conv_ln_g=conv_ln_g, conv_ln_b=conv_ln_b, conv_w_pw2=conv_w_pw2,
                   conv_b_pw2=conv_b_pw2, ffn_norm=ffn_norm, ffn_w_gu=ffn_w_gu, ffn_w_down=ffn_w_down,
                   final_norm=final_norm)
    m_in = dict(attn_norm=m_attn_norm, attn_w_qkv=m_attn_w_qkv, attn_w_o=m_attn_w_o, attn_sink=m_attn_sink,
                conv_norm=m_conv_norm, conv_w_pw1=m_conv_w_pw1, conv_b_pw1=m_conv_b_pw1, conv_w_dw=m_conv_w_dw,
                conv_b_dw=m_conv_b_dw, conv_ln_g=m_conv_ln_g, conv_ln_b=m_conv_ln_b, conv_w_pw2=m_conv_w_pw2,
                conv_b_pw2=m_conv_b_pw2, ffn_norm=m_ffn_norm, ffn_w_gu=m_ffn_w_gu, ffn_w_down=m_ffn_w_down,
                final_norm=m_final_norm)
    v_in = dict(attn_norm=v_attn_norm, attn_w_qkv=v_attn_w_qkv, attn_w_o=v_attn_w_o, attn_sink=v_attn_sink,
                conv_norm=v_conv_norm, conv_w_pw1=v_conv_w_pw1, conv_b_pw1=v_conv_b_pw1, conv_w_dw=v_conv_w_dw,
                conv_b_dw=v_conv_b_dw, conv_ln_g=v_conv_ln_g, conv_ln_b=v_conv_ln_b, conv_w_pw2=v_conv_w_pw2,
                conv_b_pw2=v_conv_b_pw2, ffn_norm=v_ffn_norm, ffn_w_gu=v_ffn_w_gu, ffn_w_down=v_ffn_w_down,
                final_norm=v_final_norm)
    order = list(weights)
    g_out, d_out, m_out, v_out = [], [], [], []
    for nm in order:
        w = weights[nm]
        shape = w.shape
        as3 = lambda t: t.reshape((1,) * (3 - len(shape)) + shape) if len(shape) < 3 else t.reshape(shape)
        g3 = as3(grads[nm].reshape(shape))
        delta, nm_, nv_ = adamw(as3(w), g3, as3(m_in[nm]), as3(v_in[nm]), "adamw_" + nm)
        g_out.append(g3.reshape(shape))
        d_out.append(delta.reshape(shape))
        m_out.append(nm_.reshape(shape))
        v_out.append(nv_.reshape(shape))
    return (loss, dx0[None], *g_out, *d_out, *m_out, *v_out)
```

```python
import functools
import math

import jax
import jax.numpy as jnp
from jax import lax
from jax.experimental import pallas as pl
from jax.experimental.pallas import tpu as pltpu

F32 = jnp.float32
BF16 = jnp.bfloat16

D = 1024
N_HEADS = 16
N_KV = 4
GROUP = N_HEADS // N_KV
HD = 64
ROT = 16
THETA = 500000.0
BLK = 128
QKV = (N_HEADS + 2 * N_KV) * HD
KV_OFF = N_HEADS * HD
DFF = 2816
CONV_W = 31
CONV_PAD = 15
HALO = 16
CONV_JB = 8
CONV_JB_BWD = 4
EPS = 1e-6
NEG = -1e30
N_CHIPS = 4
N_DEV = 8
LANES = 128
SUBLANES = 8

ADAM_LR, ADAM_B1, ADAM_B2, ADAM_EPS, ADAM_WD, ADAM_STEP = 0.001, 0.9, 0.999, 1e-08, 0.01, 10

VMEM_LIMIT = 56 * 1024 * 1024
MESH = pl.DeviceIdType.MESH


def _params(*sem):
    return pltpu.CompilerParams(dimension_semantics=sem, vmem_limit_bytes=VMEM_LIMIT)


def _tile(n, want):
    if n <= want:
        return n
    for t in range(want, 7, -1):
        if n % t == 0 and t % 8 == 0:
            return t
    return n


MXU_COLS = 256


def _col_chunks(n):
    return [slice(c, min(c + MXU_COLS, n)) for c in range(0, n, MXU_COLS)]


def _sigmoid(v):
    return 1.0 / (1.0 + jnp.exp(-v))


def _rms_fwd(xv, gain):
    r = lax.rsqrt(jnp.mean(xv * xv, axis=-1, keepdims=True) + EPS)
    return xv * r * gain


def _rms_bwd(dh, xv, gain, dres):
    r = lax.rsqrt(jnp.mean(xv * xv, axis=-1, keepdims=True) + EPS)
    xhat = xv * r
    gy = dh * gain
    dx = r * (gy - xhat * jnp.mean(gy * xhat, axis=-1, keepdims=True))
    return dx + dres, dh * xhat


def _rope(blk, c, s1, s2):
    return blk * c + pltpu.roll(blk, LANES - ROT // 2, 1) * s1 + pltpu.roll(blk, ROT // 2, 1) * s2


def _dot(a, b):
    return jnp.dot(a, b, preferred_element_type=F32)


def _dot_tb(a, b):
    return lax.dot_general(a, b, (((1,), (1,)), ((), ())), preferred_element_type=F32)


def _dot_ta(a, b):
    return lax.dot_general(a, b, (((0,), (0,)), ((), ())), preferred_element_type=F32)


def rms_qkv(x, gain, w, rc, rs1, rs2, comm=None):
    T = x.shape[0]
    tm = _tile(T, 512)

    def body(x_ref, g_ref, w_ref, c_ref, s1_ref, s2_ref, h_ref, qkv_ref):
        h = _rms_fwd(x_ref[...], g_ref[...]).astype(BF16)
        h_ref[...] = h
        acc = _dot(h, w_ref[...])
        c, s1, s2 = c_ref[...], s1_ref[...], s2_ref[...]
        n_rot = (KV_OFF + N_KV * HD) // LANES
        for j in range(n_rot):
            sl = slice(LANES * j, LANES * (j + 1))
            roped = _rope(acc[:, sl], c, s1, s2)
            if j < KV_OFF // LANES:
                roped = roped * Q_SCALE
            qkv_ref[:, sl] = roped.astype(BF16)
        qkv_ref[:, n_rot * LANES:] = acc[:, n_rot * LANES:].astype(BF16)

    row = lambda i: (i, 0)
    full = lambda i: (0, 0)
    (h, qkv), got = _call(
        body, name="rms_qkv", grid=(T // tm,),
        in_specs=[pl.BlockSpec((tm, D), row), pl.BlockSpec((1, D), full), pl.BlockSpec((D, QKV), full),
                  *_tab_specs(tm)],
        out_specs=[pl.BlockSpec((tm, D), row), pl.BlockSpec((tm, QKV), row)],
        out_shape=[jax.ShapeDtypeStruct((T, D), BF16), jax.ShapeDtypeStruct((T, QKV), BF16)],
        semantics=("parallel",), args=(x, gain, w, rc, rs1, rs2), comm=comm)
    return h, qkv, got


Q_SCALE = 1.0 / math.sqrt(HD)


def _attn_mask(n, T):
    ci = lax.broadcasted_iota(jnp.int32, (3 * BLK, BLK), 0)
    qi = lax.broadcasted_iota(jnp.int32, (3 * BLK, BLK), 1)
    key_pos = n * BLK - BLK + ci
    return (jnp.abs(ci - BLK - qi) <= BLK) & (key_pos >= 0) & (key_pos < T)


def _kv_padded(kv, first_tile):
    low = lax.broadcasted_iota(jnp.int32, (3 * BLK, LANES), 1) < HD
    zero = jnp.zeros((3 * BLK, LANES), BF16)
    out = {}
    for g in range(N_KV):
        t = kv[:, (first_tile + g // 2) * LANES:(first_tile + g // 2 + 1) * LANES]
        swapped = jnp.concatenate([t[:, HD:], t[:, :HD]], axis=1)
        for p in range(2):
            out[g, p] = jnp.where(low if p == 0 else ~low, t if g % 2 == p else swapped, zero)
    return out


def _softmax_sink(s, valid, sk):
    s = jnp.where(valid, s, NEG)
    m = jnp.maximum(jnp.max(s, axis=0, keepdims=True), sk)
    e = jnp.exp(s - m)
    es = jnp.exp(sk - m)
    inv = 1.0 / (jnp.sum(e, axis=0, keepdims=True) + es)
    return e * inv, es * inv


def _attn_specs(T):
    nb = T // BLK
    kv_blk = 2 * N_KV * HD
    kv_col = KV_OFF // kv_blk
    q_spec = pl.BlockSpec((BLK, KV_OFF), lambda n: (n, 0))
    prev = pl.BlockSpec((BLK, kv_blk), lambda n: (jnp.maximum(n - 1, 0), kv_col))
    own = pl.BlockSpec((BLK, kv_blk), lambda n: (n, kv_col))
    nxt = pl.BlockSpec((BLK, kv_blk), lambda n: (jnp.minimum(n + 1, nb - 1), kv_col))
    return nb, q_spec, prev, own, nxt


def attn_fwd(qkv, sink, comm=None):
    T = qkv.shape[0]
    nb, q_spec, prev, own, nxt = _attn_specs(T)

    def body(sink_ref, q_ref, kp_ref, ko_ref, kn_ref, o_ref):
        valid = _attn_mask(pl.program_id(0), T)
        kv = jnp.concatenate([kp_ref[...], ko_ref[...], kn_ref[...]], axis=0)
        kx, vx = _kv_padded(kv, 0), _kv_padded(kv, 2)
        tile = lambda ref, h: ref[:, (h // 2) * LANES:(h // 2 + 1) * LANES]
        ss = [_dot_tb(kx[h // GROUP, h % 2], tile(q_ref, h)) for h in range(N_HEADS)]
        ps = [_softmax_sink(ss[h], valid, sink_ref[h])[0].astype(BF16) for h in range(N_HEADS)]
        vxt = {k: v.T for k, v in vx.items()}
        for j in range(N_HEADS // 2):
            g = 2 * j // GROUP
            o_t = _dot(vxt[g, 0], ps[2 * j]) + _dot(vxt[g, 1], ps[2 * j + 1])
            o_ref[:, j * LANES:(j + 1) * LANES] = o_t.T.astype(BF16)

    (o,), got = _call(
        body, name="attn_fwd", grid=(nb,),
        in_specs=[pl.BlockSpec(memory_space=pltpu.SMEM), q_spec, prev, own, nxt],
        out_specs=[pl.BlockSpec((BLK, D), lambda n: (n, 0))],
        out_shape=[jax.ShapeDtypeStruct((T, D), BF16)],
        semantics=("parallel",), args=(sink, qkv, qkv, qkv, qkv), comm=comm)
    return o, got


def mm_res(a, w, resid, bias, name):
    T, K = a.shape
    tm = _tile(T, 512)

    def body(a_ref, w_ref, r_ref, b_ref, o_ref):
        o_ref[...] = _dot(a_ref[...], w_ref[...]) + b_ref[...] + r_ref[...]

    row = lambda i: (i, 0)
    full = lambda i: (0, 0)
    return pl.pallas_call(
        body, name=name, grid=(T // tm,),
        in_specs=[pl.BlockSpec((tm, K), row), pl.BlockSpec((K, D), full), pl.BlockSpec((tm, D), row),
                  pl.BlockSpec((1, D), full)],
        out_specs=pl.BlockSpec((tm, D), row),
        out_shape=jax.ShapeDtypeStruct((T, D), F32),
        compiler_params=_params("parallel"),
    )(a, w, resid, bias)


def rms_mm_gate(x, gain, w, bias, H, swiglu, act_dtype, name, comm=None):
    T = x.shape[0]
    tm = _tile(T, 512)

    def body(x_ref, g_ref, w_ref, b_ref, h_ref, pre_ref, act_ref):
        h = _rms_fwd(x_ref[...], g_ref[...]).astype(BF16)
        h_ref[...] = h
        for cs in _col_chunks(H):
            cs2 = slice(H + cs.start, H + cs.stop)
            a = _dot(h, w_ref[:, cs]) + b_ref[:, cs]
            b = _dot(h, w_ref[:, cs2]) + b_ref[:, cs2]
            pre_ref[0, :, cs] = a.astype(BF16)
            pre_ref[1, :, cs] = b.astype(BF16)
            if swiglu:
                act = a * _sigmoid(a) * b
            else:
                act = a * _sigmoid(b)
            act_ref[:, cs] = act.astype(act_dtype)

    row = lambda i: (i, 0)
    full = lambda i: (0, 0)
    (h, pre, act), got = _call(
        body, name=name, grid=(T // tm,),
        in_specs=[pl.BlockSpec((tm, D), row), pl.BlockSpec((1, D), full),
                  pl.BlockSpec((D, 2 * H), full, pipeline_mode=pl.Buffered(1)), pl.BlockSpec((1, 2 * H), full)],
        out_specs=[pl.BlockSpec((tm, D), row), pl.BlockSpec((2, tm, H), lambda i: (0, i, 0)),
                   pl.BlockSpec((tm, H), row)],
        out_shape=[jax.ShapeDtypeStruct((T, D), BF16), jax.ShapeDtypeStruct((2, T, H), BF16),
                   jax.ShapeDtypeStruct((T, H), act_dtype)],
        semantics=("parallel",), args=(x, gain, w, bias), comm=comm)
    return h, pre, act, got


def _conv_tiles(T):
    tt = _tile(T, 512)
    return tt, tt // SUBLANES, D // LANES


def _fill_strided(ext, p, L):
    main = p[HALO:HALO + SUBLANES * L, :].reshape(SUBLANES, L, LANES)
    ext[CONV_PAD:CONV_PAD + L] = jnp.swapaxes(main, 0, 1)

    def ibody(i, carry):
        ext[i] = p[pl.ds(i + 1, SUBLANES, stride=L), :]
        ext[i + CONV_PAD + L] = p[pl.ds(i + CONV_PAD + L + 1, SUBLANES, stride=L), :]
        return carry

    lax.fori_loop(0, CONV_PAD, ibody, 0, unroll=3)


def _conv_specs(T, tt):
    main = pl.BlockSpec((tt, D), lambda i: (i, 0))
    per = tt // HALO
    prev = pl.BlockSpec((HALO, D), lambda i: (jnp.maximum(i * per - 1, 0), 0))
    nxt = pl.BlockSpec((HALO, D), lambda i: (jnp.minimum((i + 1) * per, T // HALO - 1), 0))
    return main, prev, nxt


def _fill_pad(pad, main_ref, prev_ref, next_ref, i, n_i, tt, nlt):
    keep_p = (i > 0).astype(F32)
    keep_n = (i < n_i - 1).astype(F32)
    for lt in range(nlt):
        sl = slice(lt * LANES, (lt + 1) * LANES)
        pad[lt, 0:HALO, :] = prev_ref[:, sl] * keep_p
        pad[lt, HALO:HALO + tt, :] = main_ref[:, sl]
        pad[lt, HALO + tt:2 * HALO + tt, :] = next_ref[:, sl] * keep_n


def conv_fwd(glu, w_dw, b_dw, ln_g, ln_b, comm=None):
    T = glu.shape[0]
    tt, L, nlt = _conv_tiles(T)
    n_i = T // tt
    main, prev, nxt = _conv_specs(T, tt)

    def body(x_ref, xp_ref, xn_ref, w_ref, b_ref, g_ref, bb_ref, dwc_ref, sw_ref, pad, ob, ext, wk):
        i = pl.program_id(0)
        _fill_pad(pad, x_ref, xp_ref, xn_ref, i, n_i, tt, nlt)
        for lt in range(nlt):
            sl = slice(lt * LANES, (lt + 1) * LANES)
            o = ob.at[lt]
            _fill_strided(ext, pad.at[lt], L)
            for k in range(CONV_W):
                wk[k] = jnp.broadcast_to(w_ref[k:k + 1, sl], (SUBLANES, LANES))

            def jbody(jb, carry):
                j = jb * CONV_JB
                accs = [None] * CONV_JB
                for m in range(CONV_W + CONV_JB - 1):
                    e = ext[j + m]
                    for u in range(CONV_JB):
                        if 0 <= m - u < CONV_W:
                            t = e * wk[m - u]
                            accs[u] = t if accs[u] is None else accs[u] + t
                for u in range(CONV_JB):
                    o[pl.ds(j + u, SUBLANES, stride=L), :] = accs[u]
                return carry

            lax.fori_loop(0, L // CONV_JB, jbody, 0)
        y = jnp.concatenate([ob[lt] for lt in range(nlt)], axis=1) + b_ref[...]
        dwc_ref[...] = y
        mu = jnp.mean(y, axis=-1, keepdims=True)
        yc = y - mu
        var = jnp.mean(yc * yc, axis=-1, keepdims=True)
        z = yc * lax.rsqrt(var + EPS) * g_ref[...] + bb_ref[...]
        sw_ref[...] = (z * _sigmoid(z)).astype(BF16)

    full = lambda i: (0, 0)
    (dwc, sw), got = _call(
        body, name="conv_fwd", grid=(n_i,),
        in_specs=[main, prev, nxt, pl.BlockSpec((32, D), full), pl.BlockSpec((1, D), full),
                  pl.BlockSpec((1, D), full), pl.BlockSpec((1, D), full)],
        out_specs=[pl.BlockSpec((tt, D), lambda i: (i, 0)), pl.BlockSpec((tt, D), lambda i: (i, 0))],
        out_shape=[jax.ShapeDtypeStruct((T, D), F32), jax.ShapeDtypeStruct((T, D), BF16)],
        scratch_shapes=[pltpu.VMEM((nlt, tt + 2 * HALO, LANES), F32), pltpu.VMEM((nlt, tt, LANES), F32),
                        pltpu.VMEM((L + 2 * HALO, SUBLANES, LANES), F32), pltpu.VMEM((32, SUBLANES, LANES), F32)],
        semantics=("parallel",), args=(glu, glu, glu, w_dw, b_dw, ln_g, ln_b), comm=comm)
    return dwc, sw, got


def mm_res_loss(a, w, resid, gain, target):
    T, K = a.shape
    tm = _tile(T, 512)

    def body(a_ref, w_ref, r_ref, g_ref, t_ref, dx_ref, loss_ref, dg_ref):
        @pl.when(pl.program_id(0) == 0)
        def _():
            loss_ref[...] = jnp.zeros_like(loss_ref)
            dg_ref[...] = jnp.zeros_like(dg_ref)

        xv, gain_v = _dot(a_ref[...], w_ref[...]) + r_ref[...], g_ref[...]
        err = _rms_fwd(xv, gain_v) - t_ref[...]
        part = 0.5 * jnp.sum(jnp.mean(err * err, axis=-1, keepdims=True), axis=0, keepdims=True)
        loss_ref[...] += jnp.broadcast_to(part, loss_ref.shape)
        dx, dgr = _rms_bwd(err * (1.0 / D), xv, gain_v, 0.0)
        dx_ref[...] = dx
        dg_ref[...] += jnp.sum(dgr, axis=0, keepdims=True)

    row = lambda i: (i, 0)
    full = lambda i: (0, 0)
    return pl.pallas_call(
        body, name="ffn1_down_loss", grid=(T // tm,),
        in_specs=[pl.BlockSpec((tm, K), row), pl.BlockSpec((K, D), full), pl.BlockSpec((tm, D), row),
                  pl.BlockSpec((1, D), full), pl.BlockSpec((tm, D), row)],
        out_specs=[pl.BlockSpec((tm, D), row), pl.BlockSpec((1, LANES), full), pl.BlockSpec((1, D), full)],
        out_shape=[jax.ShapeDtypeStruct((T, D), F32), jax.ShapeDtypeStruct((1, LANES), F32),
                   jax.ShapeDtypeStruct((1, D), F32)],
        compiler_params=_params("arbitrary"),
    )(a, w, resid, gain, target)


def swiglu_bwd(dx, w_down, pre, name, comm=None):
    T = dx.shape[0]
    H = w_down.shape[0]
    tm = _tile(T, 512)

    def body(dx_ref, w_ref, pre_ref, dpre_ref):
        dxb = dx_ref[...].astype(BF16)
        for cs in _col_chunks(H):
            dact = _dot_tb(dxb, w_ref[cs, :])
            g = pre_ref[0, :, cs].astype(F32)
            u = pre_ref[1, :, cs].astype(F32)
            sg = _sigmoid(g)
            dpre_ref[0, :, cs] = (dact * u * sg * (1.0 + g * (1.0 - sg))).astype(BF16)
            dpre_ref[1, :, cs] = (dact * g * sg).astype(BF16)

    (dpre,), got = _call(
        body, name=name, grid=(T // tm,),
        in_specs=[pl.BlockSpec((tm, D), lambda i: (i, 0)),
                  pl.BlockSpec((H, D), lambda i: (0, 0), pipeline_mode=pl.Buffered(1)),
                  pl.BlockSpec((2, tm, H), lambda i: (0, i, 0))],
        out_specs=[pl.BlockSpec((2, tm, H), lambda i: (0, i, 0))],
        out_shape=[jax.ShapeDtypeStruct((2, T, H), BF16)],
        semantics=("parallel",), args=(dx, w_down, pre), comm=comm)
    return dpre, got


def mm_bt_rmsbwd(dpre, w, x, gain, dres, name, comm=None):
    nh, T, H = dpre.shape
    tm = _tile(T, 512)

    def body(dp_ref, w_ref, x_ref, g_ref, dres_ref, dx_ref, dg_ref):
        @pl.when(pl.program_id(0) == 0)
        def _():
            dg_ref[...] = jnp.zeros_like(dg_ref)

        dh = _dot_tb(dp_ref[0], w_ref[:, 0:H])
        for hf in range(1, nh):
            dh = dh + _dot_tb(dp_ref[hf], w_ref[:, hf * H:(hf + 1) * H])
        dx, dgr = _rms_bwd(dh, x_ref[...], g_ref[...], dres_ref[...])
        dx_ref[...] = dx
        dg_ref[...] += jnp.sum(dgr, axis=0, keepdims=True)

    row = lambda i: (i, 0)
    full = lambda i: (0, 0)
    (dx, dg), got = _call(
        body, name=name, grid=(T // tm,),
        in_specs=[pl.BlockSpec((nh, tm, H), lambda i: (0, i, 0)),
                  pl.BlockSpec((D, nh * H), full, pipeline_mode=pl.Buffered(1)),
                  pl.BlockSpec((tm, D), row), pl.BlockSpec((1, D), full), pl.BlockSpec((tm, D), row)],
        out_specs=[pl.BlockSpec((tm, D), row), pl.BlockSpec((1, D), full)],
        out_shape=[jax.ShapeDtypeStruct((T, D), F32), jax.ShapeDtypeStruct((1, D), F32)],
        semantics=("arbitrary",), args=(dpre, w, x, gain, dres), comm=comm)
    return dx, dg, got


def dw_col(a, dpre, name):
    T = a.shape[0]
    nh, _, H = dpre.shape
    per = nh * H // N_CHIPS
    bph = N_CHIPS // nh
    tt = _tile(T, 2048)
    nt = T // tt

    def body(a_ref, b_ref, o_ref, acc):
        t = pl.program_id(1)

        @pl.when(t == 0)
        def _():
            acc[...] = jnp.zeros_like(acc)

        acc[...] += _dot_ta(a_ref[...], b_ref[...])

        @pl.when(t == nt - 1)
        def _():
            o_ref[...] = acc[...].astype(BF16)

    return pl.pallas_call(
        body, name=name, grid=(N_CHIPS, nt),
        in_specs=[pl.BlockSpec((tt, D), lambda q, t: (t, 0)),
                  pl.BlockSpec((None, tt, per), lambda q, t: (q // bph, t, q % bph))],
        out_specs=pl.BlockSpec((None, D, per), lambda q, t: (q, 0, 0)),
        out_shape=jax.ShapeDtypeStruct((N_CHIPS, D, per), BF16),
        scratch_shapes=[pltpu.VMEM((D, per), F32)],
        compiler_params=_params("parallel", "arbitrary"),
    )(a, dpre)


def dw_row(a, b, name):
    T, R = a.shape
    cw = 1408 if R % 1408 == 0 else 512
    tt = _tile(T, 1024)
    nt = T // tt

    def body(a_ref, b_ref, o_ref, acc):
        t = pl.program_id(1)

        @pl.when(t == 0)
        def _():
            acc[...] = jnp.zeros_like(acc)

        acc[...] += _dot_ta(a_ref[...], b_ref[...].astype(BF16))

        @pl.when(t == nt - 1)
        def _():
            o_ref[...] = acc[...].astype(BF16)

    out = pl.pallas_call(
        body, name=name, grid=(R // cw, nt),
        in_specs=[pl.BlockSpec((tt, cw), lambda q, t: (t, q)), pl.BlockSpec((tt, D), lambda q, t: (t, 0))],
        out_specs=pl.BlockSpec((cw, D), lambda q, t: (q, 0)),
        out_shape=jax.ShapeDtypeStruct((R, D), BF16),
        scratch_shapes=[pltpu.VMEM((cw, D), F32)],
        compiler_params=_params("parallel", "arbitrary"),
    )(a, b)
    return out.reshape(N_CHIPS, R // N_CHIPS, D)


def ln_silu_bwd(dx, w_pw2, dwc, ln_g, ln_b):
    T = dx.shape[0]
    tm = _tile(T, 512)

    def body(dx_ref, w_ref, y_ref, g_ref, b_ref, dy_ref, dg_ref, db_ref, dbo_ref):
        @pl.when(pl.program_id(0) == 0)
        def _():
            dg_ref[...] = jnp.zeros_like(dg_ref)
            db_ref[...] = jnp.zeros_like(db_ref)
            dbo_ref[...] = jnp.zeros_like(dbo_ref)

        dxv = dx_ref[...]
        dsw = _dot_tb(dxv.astype(BF16), w_ref[...])
        y = y_ref[...]
        mu = jnp.mean(y, axis=-1, keepdims=True)
        yc = y - mu
        rstd = lax.rsqrt(jnp.mean(yc * yc, axis=-1, keepdims=True) + EPS)
        xhat = yc * rstd
        z = xhat * g_ref[...] + b_ref[...]
        sg = _sigmoid(z)
        dz = dsw * sg * (1.0 + z * (1.0 - sg))
        dxh = dz * g_ref[...]
        dy_ref[...] = rstd * (dxh - jnp.mean(dxh, axis=-1, keepdims=True)
                              - xhat * jnp.mean(dxh * xhat, axis=-1, keepdims=True))
        dg_ref[...] += jnp.sum(dz * xhat, axis=0, keepdims=True)
        db_ref[...] += jnp.sum(dz, axis=0, keepdims=True)
        dbo_ref[...] += jnp.sum(dxv, axis=0, keepdims=True)

    row = lambda i: (i, 0)
    full = lambda i: (0, 0)
    vec = pl.BlockSpec((1, D), full)
    return pl.pallas_call(
        body, name="ln_silu_bwd", grid=(T // tm,),
        in_specs=[pl.BlockSpec((tm, D), row), pl.BlockSpec((D, D), full), pl.BlockSpec((tm, D), row), vec, vec],
        out_specs=[pl.BlockSpec((tm, D), row), vec, vec, vec],
        out_shape=[jax.ShapeDtypeStruct((T, D), F32)] + [jax.ShapeDtypeStruct((1, D), F32)] * 3,
        compiler_params=_params("arbitrary"),
    )(dx, w_pw2, dwc, ln_g, ln_b)


def conv_bwd(ddwc, glu, pre, w_dw, comm=None):
    T = ddwc.shape[0]
    tt, L, nlt = _conv_tiles(T)
    n_i = T // tt
    main, prev, nxt = _conv_specs(T, tt)

    def body(d_ref, dp_ref, dn_ref, x_ref, xp_ref, xn_ref, pre_ref, w_ref,
             dpre_ref, dw_ref, dbd_ref, dbp_ref, padd, padx, ob, extd, extx, wk):
        i = pl.program_id(0)

        @pl.when(i == 0)
        def _():
            dw_ref[...] = jnp.zeros_like(dw_ref)
            dbd_ref[...] = jnp.zeros_like(dbd_ref)
            dbp_ref[...] = jnp.zeros_like(dbp_ref)

        _fill_pad(padd, d_ref, dp_ref, dn_ref, i, n_i, tt, nlt)
        _fill_pad(padx, x_ref, xp_ref, xn_ref, i, n_i, tt, nlt)
        for lt in range(nlt):
            sl = slice(lt * LANES, (lt + 1) * LANES)
            o = ob.at[lt]
            _fill_strided(extd, padd.at[lt], L)
            _fill_strided(extx, padx.at[lt], L)
            for k in range(CONV_W):
                wk[k] = jnp.broadcast_to(w_ref[k:k + 1, sl], (SUBLANES, LANES))

            nu = CONV_JB_BWD

            def jbody(jb, accs):
                j = jb * nu
                accs = list(accs)
                d = [extd[j + u + CONV_PAD] for u in range(nu)]
                g = [None] * nu
                for m in range(CONV_W + nu - 1):
                    ed = extd[j + 2 * CONV_PAD + nu - 1 - m]
                    ex = extx[j + m]
                    for u in range(nu):
                        k = m - (nu - 1 - u)
                        if 0 <= k < CONV_W:
                            t = ed * wk[k]
                            g[u] = t if g[u] is None else g[u] + t
                        k = m - u
                        if 0 <= k < CONV_W:
                            accs[k] = accs[k] + d[u] * ex
                for u in range(nu):
                    o[pl.ds(j + u, SUBLANES, stride=L), :] = g[u]
                return tuple(accs)

            accs = lax.fori_loop(0, L // nu, jbody, tuple(jnp.zeros((SUBLANES, LANES), F32) for _ in range(CONV_W)))
            for k in range(CONV_W):
                dw_ref[k:k + 1, sl] += jnp.sum(accs[k], axis=0, keepdims=True)
        dglu = jnp.concatenate([ob[lt] for lt in range(nlt)], axis=1)
        a = pre_ref[0].astype(F32)
        gate = pre_ref[1].astype(F32)
        sg = _sigmoid(gate)
        da = dglu * sg
        dgate = dglu * a * sg * (1.0 - sg)
        dpre_ref[0] = da.astype(BF16)
        dpre_ref[1] = dgate.astype(BF16)
        dbd_ref[...] += jnp.sum(d_ref[...], axis=0, keepdims=True)
        dbp_ref[0] += jnp.sum(da, axis=0, keepdims=True)
        dbp_ref[1] += jnp.sum(dgate, axis=0, keepdims=True)

    full = lambda i: (0, 0)
    (dpre, dw, dbd, dbp), got = _call(
        body, name="conv_bwd", grid=(n_i,),
        in_specs=[main, prev, nxt, main, prev, nxt, pl.BlockSpec((2, tt, D), lambda i: (0, i, 0)),
                  pl.BlockSpec((32, D), full)],
        out_specs=[pl.BlockSpec((2, tt, D), lambda i: (0, i, 0)), pl.BlockSpec((32, D), full),
                   pl.BlockSpec((1, D), full), pl.BlockSpec((2, 1, D), lambda i: (0, 0, 0))],
        out_shape=[jax.ShapeDtypeStruct((2, T, D), BF16), jax.ShapeDtypeStruct((32, D), F32),
                   jax.ShapeDtypeStruct((1, D), F32), jax.ShapeDtypeStruct((2, 1, D), F32)],
        scratch_shapes=[pltpu.VMEM((nlt, tt + 2 * HALO, LANES), F32), pltpu.VMEM((nlt, tt + 2 * HALO, LANES), F32),
                        pltpu.VMEM((nlt, tt, LANES), F32), pltpu.VMEM((L + 2 * HALO, SUBLANES, LANES), F32),
                        pltpu.VMEM((L + 2 * HALO, SUBLANES, LANES), F32), pltpu.VMEM((32, SUBLANES, LANES), F32)],
        semantics=("arbitrary",), args=(ddwc, ddwc, ddwc, glu, glu, glu, pre, w_dw), comm=comm)
    return dpre, dw, dbd, dbp, got


def mm_bt(a, w, name):
    T = a.shape[0]
    N = w.shape[0]
    tm = _tile(T, 512)

    def body(a_ref, w_ref, o_ref):
        o_ref[...] = _dot_tb(a_ref[...].astype(BF16), w_ref[...]).astype(BF16)

    return pl.pallas_call(
        body, name=name, grid=(T // tm,),
        in_specs=[pl.BlockSpec((tm, D), lambda i: (i, 0)), pl.BlockSpec((N, D), lambda i: (0, 0))],
        out_specs=pl.BlockSpec((tm, N), lambda i: (i, 0)),
        out_shape=jax.ShapeDtypeStruct((T, N), BF16),
        compiler_params=_params("parallel"),
    )(a, w)


def attn_bwd(qkv, o, do, sink, rc, rs1, rs2, comm=None):
    T = qkv.shape[0]
    nb, q_spec, prev, own, nxt = _attn_specs(T)
    kvw = N_KV * HD

    def body(sink_ref, q_ref, kp_ref, ko_ref, kn_ref, o_ref, do_ref, c_ref, s1_ref, s2_ref,
             dq_ref, dkc_ref, dvc_ref, dsink_ref):
        n = pl.program_id(0)

        @pl.when(n == 0)
        def _():
            dsink_ref[...] = jnp.zeros_like(dsink_ref)

        valid = _attn_mask(n, T)
        kv = jnp.concatenate([kp_ref[...], ko_ref[...], kn_ref[...]], axis=0)
        kx, vx = _kv_padded(kv, 0), _kv_padded(kv, 2)
        tile = lambda ref, j: ref[:, j * LANES:(j + 1) * LANES]
        ss = [_dot_tb(kx[h // GROUP, h % 2], tile(q_ref, h // 2)) for h in range(N_HEADS)]
        dps = [_dot_tb(vx[h // GROUP, h % 2], tile(do_ref, h // 2)) for h in range(N_HEADS)]
        low_d = lax.broadcasted_iota(jnp.int32, (LANES, BLK), 0) < HD
        deltas = []
        for j in range(N_HEADS // 2):
            prod_t = tile(do_ref, j).astype(F32).T * tile(o_ref, j).astype(F32).T
            deltas.append(jnp.sum(jnp.where(low_d, prod_t, 0.0), axis=0, keepdims=True))
            deltas.append(jnp.sum(jnp.where(low_d, 0.0, prod_t), axis=0, keepdims=True))
        lane = lax.broadcasted_iota(jnp.int32, (1, N_HEADS), 1)
        dsink = jnp.zeros((1, N_HEADS), F32)
        pbs, dss = [], []
        for h in range(N_HEADS):
            p, p_sink = _softmax_sink(ss[h], valid, sink_ref[h])
            dss.append((p * (dps[h] - deltas[h])).astype(BF16))
            pbs.append(p.astype(BF16))
            part = -jnp.sum(p_sink * deltas[h], axis=1, keepdims=True)
            dsink = dsink + jnp.where(lane == h, part, 0.0)
        dsink_ref[...] += dsink
        c, s1, s2 = c_ref[...], s1_ref[...], s2_ref[...]
        kxt = {k: v.T for k, v in kx.items()}
        for j in range(N_HEADS // 2):
            g = 2 * j // GROUP
            dq_t = _dot(kxt[g, 0], dss[2 * j]) + _dot(kxt[g, 1], dss[2 * j + 1])
            dq_ref[:, j * LANES:(j + 1) * LANES] = (_rope(dq_t.T, c, -s1, -s2) * Q_SCALE).astype(BF16)
        low_k = lax.broadcasted_iota(jnp.int32, (3 * BLK, LANES), 1) < HD
        cols = lambda xs, g, p: jnp.concatenate([xs[GROUP * g + p], xs[GROUP * g + 2 + p]], axis=1)
        for t in range(N_KV // 2):
            sums = {}
            for g in (2 * t, 2 * t + 1):
                q2 = jnp.concatenate([tile(q_ref, 2 * g), tile(q_ref, 2 * g + 1)], axis=0)
                do2 = jnp.concatenate([tile(do_ref, 2 * g), tile(do_ref, 2 * g + 1)], axis=0)
                for p in range(2):
                    sums[g, p] = (_dot(cols(dss, g, p), q2), _dot(cols(pbs, g, p), do2))
            for which, ref in ((0, dkc_ref), (1, dvc_ref)):
                keep = jnp.where(low_k, sums[2 * t, 0][which], sums[2 * t + 1, 1][which])
                swap = jnp.where(low_k, sums[2 * t + 1, 0][which], sums[2 * t, 1][which])
                ref[:, t * LANES:(t + 1) * LANES] = keep + pltpu.roll(swap, HD, 1)

    row = lambda n: (n, 0)
    (dq, dkc, dvc, dsink), got = _call(
        body, name="attn_bwd", grid=(nb,),
        in_specs=[pl.BlockSpec(memory_space=pltpu.SMEM), q_spec, prev, own, nxt,
                  pl.BlockSpec((BLK, D), row), pl.BlockSpec((BLK, D), row), *_tab_specs(BLK)],
        out_specs=[pl.BlockSpec((BLK, D), row), pl.BlockSpec((None, 3 * BLK, kvw), lambda n: (n, 0, 0)),
                   pl.BlockSpec((None, 3 * BLK, kvw), lambda n: (n, 0, 0)), pl.BlockSpec((1, N_HEADS), lambda n: (0, 0))],
        out_shape=[jax.ShapeDtypeStruct((T, QKV), BF16), jax.ShapeDtypeStruct((nb, 3 * BLK, kvw), F32),
                   jax.ShapeDtypeStruct((nb, 3 * BLK, kvw), F32), jax.ShapeDtypeStruct((1, N_HEADS), F32)],
        semantics=("arbitrary",), args=(sink, qkv, qkv, qkv, qkv, o, do, rc, rs1, rs2), comm=comm)
    return dq, dkc, dvc, dsink, got


def kv_sum(dqkv, dkc, dvc, rc, rs1, rs2):
    nb = dkc.shape[0]
    T = nb * BLK
    kvw = N_KV * HD

    G = 4
    ng = nb // G

    def gather3(own_ref, prev_ref, before_ref, next_ref, after_ref, m):
        has_before = (m > 0).astype(F32)
        has_after = (m < ng - 1).astype(F32)
        out = []
        for i in range(G):
            from_prev = prev_ref[i - 1] if i > 0 else before_ref[0] * has_before
            from_next = next_ref[i + 1] if i < G - 1 else after_ref[0] * has_after
            out.append(from_prev + own_ref[i] + from_next)
        return jnp.concatenate(out, axis=0)

    def body(_, ko, kp, kb, kn, ka, vo, vp, vb, vn, va, c_ref, s1_ref, s2_ref, out_ref):
        m = pl.program_id(0)
        dk = gather3(ko, kp, kb, kn, ka, m)
        dv = gather3(vo, vp, vb, vn, va, m)
        c, s1, s2 = c_ref[...], s1_ref[...], s2_ref[...]
        for j in range(kvw // LANES):
            sl = slice(LANES * j, LANES * (j + 1))
            out_ref[:, sl] = _rope(dk[:, sl], c, -s1, -s2).astype(BF16)
        out_ref[:, kvw:] = dv.astype(BF16)

    own = pl.BlockSpec((G, BLK, kvw), lambda m: (m, 1, 0))
    prev = pl.BlockSpec((G, BLK, kvw), lambda m: (m, 2, 0))
    before = pl.BlockSpec((1, BLK, kvw), lambda m: (jnp.maximum(G * m - 1, 0), 2, 0))
    nxt = pl.BlockSpec((G, BLK, kvw), lambda m: (m, 0, 0))
    after = pl.BlockSpec((1, BLK, kvw), lambda m: (jnp.minimum(G * m + G, nb - 1), 0, 0))
    five = [own, prev, before, nxt, after]
    return pl.pallas_call(
        body, name="kv_sum", grid=(ng,),
        in_specs=[pl.BlockSpec(memory_space=pl.ANY), *five, *five, *_tab_specs(G * BLK)],
        out_specs=pl.BlockSpec((G * BLK, 2 * kvw), lambda m: (m, KV_OFF // (2 * kvw))),
        out_shape=jax.ShapeDtypeStruct((T, QKV), BF16),
        input_output_aliases={0: 0},
        compiler_params=_params("parallel"),
    )(dqkv, *([dkc] * 5), *([dvc] * 5), rc, rs1, rs2)


def _me():
    return lax.axis_index("x"), lax.axis_index("y"), lax.axis_index("c")


def _half_rows(ref, sharded_rows, chip, core):
    R, C = ref.shape[-2], ref.shape[-1]
    lead = (slice(None),) * (len(ref.shape) - 2)
    if sharded_rows:
        per = R // N_CHIPS
        return ref.at[lead + (pl.ds(chip * per + core * (per // 2), per // 2), slice(None))]
    per = C // N_CHIPS
    return ref.at[lead + (pl.ds(core * (R // 2), R // 2), pl.ds(chip * per, per))]


class _Gather:
    def __init__(self, shards, sharded_rows):
        self.inputs = list(shards)
        self.rows = list(sharded_rows)
        self.n = self.n_in = self.n_out = len(shards)
        self.out_shapes = []
        for s, rows in zip(shards, sharded_rows):
            shp = list(s.shape)
            shp[-2 if rows else -1] *= N_CHIPS
            self.out_shapes.append(jax.ShapeDtypeStruct(tuple(shp), s.dtype))
        self.scratch = [pltpu.SemaphoreType.DMA((self.n, 6)), pltpu.SemaphoreType.DMA((self.n, 6)),
                        pltpu.SemaphoreType.DMA((self.n, 2))]

    def _ctx(self, ins, outs, sems):
        send_sems, recv_sems, local_sems = sems
        x, y, c = _me()
        chips = [(1 - x, y), (x, 1 - y), (1 - x, 1 - y)]

        def half_src(w, core):
            s = ins[w]
            R = s.shape[-2]
            return s.at[pl.ds(core * (R // 2), R // 2), :]

        def dst(w, chip, core):
            return _half_rows(outs[w], self.rows[w], chip, core)

        def copy(w, k, src, chip, core, to):
            return pltpu.make_async_remote_copy(
                src_ref=src, dst_ref=dst(w, chip, core), send_sem=send_sems.at[w, k], recv_sem=recv_sems.at[w, k],
                device_id=to, device_id_type=MESH)

        def local(w, core):
            return pltpu.make_async_copy(half_src(w, core), dst(w, 2 * x + y, core), local_sems.at[w, core])

        def first(w, j):
            qx, qy = chips[j]
            return copy(w, j, half_src(w, c), 2 * x + y, c, (qx, qy, c))

        def landed(w, j):
            qx, qy = chips[j]
            return copy(w, j, dst(w, 2 * qx + qy, c), 2 * qx + qy, c, (x, y, c))

        def passed(w, j):
            qx, qy = chips[j]
            return copy(w, 3 + j, dst(w, 2 * qx + qy, c), 2 * qx + qy, c, (x, y, 1 - c))

        def from_sibling(w, j):
            qx, qy = chips[j]
            return copy(w, 3 + j, dst(w, 2 * qx + qy, 1 - c), 2 * qx + qy, 1 - c, (x, y, c))

        return local, first, landed, passed, from_sibling

    def start(self, ins, outs, sems):
        local, first, _, _, _ = self._ctx(ins, outs, sems)
        for w in range(self.n):
            for core in range(2):
                local(w, core).start()
            for j in range(3):
                first(w, j).start()

    def mid(self, ins, outs, sems):
        _, _, landed, passed, _ = self._ctx(ins, outs, sems)
        for w in range(self.n):
            for j in range(3):
                landed(w, j).wait_recv()
                passed(w, j).start()

    def end(self, ins, outs, sems):
        local, first, _, passed, from_sibling = self._ctx(ins, outs, sems)
        for w in range(self.n):
            for j in range(3):
                from_sibling(w, j).wait_recv()
        for w in range(self.n):
            for j in range(3):
                first(w, j).wait_send()
                passed(w, j).wait_send()
            for core in range(2):
                local(w, core).wait()


class _Scatter:
    def __init__(self, grads, small=None):
        self.inputs = list(grads) + ([small] if small is not None else [])
        self.ng = len(grads)
        self.n = self.n_in = self.n_out = len(self.inputs)
        self.out_shapes = [jax.ShapeDtypeStruct((N_DEV, g.shape[1] // 2, g.shape[2]), g.dtype) for g in grads]
        if small is not None:
            self.out_shapes.append(jax.ShapeDtypeStruct((N_DEV,) + small.shape, small.dtype))
        self.scratch = [pltpu.SemaphoreType.DMA((self.n, N_DEV)), pltpu.SemaphoreType.DMA((self.n, N_DEV)),
                        pltpu.SemaphoreType.DMA((self.n,))]

    def _ctx(self, ins, outs, sems):
        send_sems, recv_sems, local_sems = sems
        x, y, c = _me()
        me = 4 * x + 2 * y + c

        def piece(w, chip, core):
            if w >= self.ng:
                return ins[w]
            half = ins[w].shape[1] // 2
            return ins[w].at[chip, pl.ds(core * half, half), :]

        def peer_of(k):
            return x ^ ((k >> 2) & 1), y ^ ((k >> 1) & 1), c ^ (k & 1)

        def local(w):
            return pltpu.make_async_copy(piece(w, 2 * x + y, c), outs[w].at[me], local_sems.at[w])

        def send(w, k):
            px, py, pc = peer_of(k)
            return pltpu.make_async_remote_copy(
                src_ref=piece(w, 2 * px + py, pc), dst_ref=outs[w].at[me], send_sem=send_sems.at[w, k],
                recv_sem=recv_sems.at[w, k], device_id=(px, py, pc), device_id_type=MESH)

        def recv(w, k):
            px, py, pc = peer_of(k)
            return pltpu.make_async_remote_copy(
                src_ref=piece(w, 2 * x + y, c), dst_ref=outs[w].at[4 * px + 2 * py + pc], send_sem=send_sems.at[w, k],
                recv_sem=recv_sems.at[w, k], device_id=(px, py, pc), device_id_type=MESH)

        return local, send, recv

    def start(self, ins, outs, sems):
        local, send, _ = self._ctx(ins, outs, sems)
        for w in range(self.n):
            local(w).start()
            for k in range(1, N_DEV):
                send(w, k).start()

    def mid(self, ins, outs, sems):
        pass

    def end(self, ins, outs, sems):
        local, send, recv = self._ctx(ins, outs, sems)
        for w in range(self.n):
            for k in range(1, N_DEV):
                recv(w, k).wait_recv()
        for w in range(self.n):
            for k in range(1, N_DEV):
                send(w, k).wait_send()
            local(w).wait()


class _Both:
    def __init__(self, a, b):
        self.a, self.b = a, b
        self.inputs = a.inputs + b.inputs
        self.out_shapes = a.out_shapes + b.out_shapes
        self.scratch = a.scratch + b.scratch
        self.n_in, self.n_out = a.n_in + b.n_in, a.n_out + b.n_out

    def _split(self, ins, outs, sems):
        a, na = self.a, len(self.a.scratch)
        return (ins[:a.n_in], outs[:a.n_out], sems[:na]), (ins[a.n_in:], outs[a.n_out:], sems[na:])

    def start(self, ins, outs, sems):
        pa, pb = self._split(ins, outs, sems)
        self.a.start(*pa)
        self.b.start(*pb)

    def mid(self, ins, outs, sems):
        pa, pb = self._split(ins, outs, sems)
        self.a.mid(*pa)
        self.b.mid(*pb)

    def end(self, ins, outs, sems):
        pa, pb = self._split(ins, outs, sems)
        self.a.end(*pa)
        self.b.end(*pb)


def exchange(plan, name):
    def body(*refs):
        ins, outs, sems = refs[:plan.n_in], refs[plan.n_in:plan.n_in + plan.n_out], refs[plan.n_in + plan.n_out:]
        plan.start(ins, outs, sems)
        plan.mid(ins, outs, sems)
        plan.end(ins, outs, sems)

    any_spec = pl.BlockSpec(memory_space=pl.ANY)
    return pl.pallas_call(
        body, name=name, in_specs=[any_spec] * plan.n_in, out_specs=[any_spec] * plan.n_out,
        out_shape=plan.out_shapes, scratch_shapes=plan.scratch,
    )(*plan.inputs)


def _call(body, *, name, grid, in_specs, out_specs, out_shape, scratch_shapes=(), semantics, args, comm=None):
    if comm is None:
        outs = pl.pallas_call(
            body, name=name, grid=grid, in_specs=in_specs, out_specs=out_specs, out_shape=out_shape,
            scratch_shapes=list(scratch_shapes), compiler_params=_params(*semantics))(*args)
        return outs, []
    n_in, n_out, n_scr = len(in_specs), len(out_specs), len(scratch_shapes)

    total = math.prod(grid)
    first, middle, last = 0, (3 * total) // 4 - 1, total - 1
    assert first <= middle < last

    def at(step):
        lin = pl.program_id(0)
        for d in range(1, len(grid)):
            lin = lin * grid[d] + pl.program_id(d)
        return lin == step

    def hosted(*refs):
        h_in, c_in = refs[:n_in], refs[n_in:n_in + comm.n_in]
        rest = refs[n_in + comm.n_in:]
        h_out, c_out = rest[:n_out], rest[n_out:n_out + comm.n_out]
        rest = rest[n_out + comm.n_out:]
        h_scr, c_scr = rest[:n_scr], rest[n_scr:]

        @pl.when(at(first))
        def _():
            comm.start(c_in, c_out, c_scr)

        body(*h_in, *h_out, *h_scr)

        @pl.when(at(middle))
        def _():
            comm.mid(c_in, c_out, c_scr)

        @pl.when(at(last))
        def _():
            comm.end(c_in, c_out, c_scr)

    any_spec = pl.BlockSpec(memory_space=pl.ANY)
    outs = pl.pallas_call(
        hosted, name=name, grid=grid, in_specs=list(in_specs) + [any_spec] * comm.n_in,
        out_specs=list(out_specs) + [any_spec] * comm.n_out, out_shape=list(out_shape) + comm.out_shapes,
        scratch_shapes=list(scratch_shapes) + comm.scratch,
        compiler_params=_params(*(["arbitrary"] * len(grid))))(*args, *comm.inputs)
    return outs[:n_out], outs[n_out:]


def sum_swap(pieces, name):
    nl = len(pieces)
    _, r2, cc = pieces[0].shape
    tr = 128 if r2 % 128 == 0 else r2 // 2
    n = r2 // tr

    def body(*refs):
        p_refs, out = refs[:nl], refs[nl]
        slots, send_sems, local_sems, recv_sem = refs[nl + 1:]
        x, y, c = _me()
        sibling = (x, y, 1 - c)
        l, i = pl.program_id(0), pl.program_id(1)
        step = l * n + i

        def rows(st, core):
            return out.at[st // n, pl.ds(core * r2 + (st % n) * tr, tr), :]

        def copies(st):
            slot = st % 2
            local = pltpu.make_async_copy(slots.at[slot], rows(st, c), local_sems.at[slot])
            remote = pltpu.make_async_remote_copy(
                src_ref=slots.at[slot], dst_ref=rows(st, c), send_sem=send_sems.at[slot], recv_sem=recv_sem,
                device_id=sibling, device_id_type=MESH)
            return local, remote

        for ll in range(nl):
            @pl.when(l == ll)
            def _():
                acc = p_refs[ll][0].astype(F32)
                for d in range(1, N_DEV):
                    acc = acc + p_refs[ll][d].astype(F32)
                slots[step % 2] = acc

        for cp in copies(step):
            cp.start()

        @pl.when(step >= 1)
        def _():
            local, remote = copies(step - 1)
            local.wait()
            remote.wait_send()

        @pl.when(step == nl * n - 1)
        def _():
            local, remote = copies(step)
            local.wait()
            remote.wait_send()
            theirs = out.at[:, pl.ds((1 - c) * r2, r2), :]
            pltpu.make_async_remote_copy(src_ref=theirs, dst_ref=theirs, send_sem=send_sems.at[0],
                                         recv_sem=recv_sem, device_id=sibling, device_id_type=MESH).wait_recv()

    def piece_spec(ll):
        def index(l, i):
            return (0, jnp.where(l == ll, i, jnp.where(l < ll, 0, n - 1)), 0)
        return pl.BlockSpec((N_DEV, tr, cc), index)

    return pl.pallas_call(
        body, name=name, grid=(nl, n),
        in_specs=[piece_spec(ll) for ll in range(nl)],
        out_specs=pl.BlockSpec(memory_space=pl.ANY),
        out_shape=jax.ShapeDtypeStruct((nl, 2 * r2, cc), F32),
        scratch_shapes=[pltpu.VMEM((2, tr, cc), F32), pltpu.SemaphoreType.DMA((2,)), pltpu.SemaphoreType.DMA((2,)),
                        pltpu.SemaphoreType.DMA(())],
        compiler_params=_params("arbitrary", "arbitrary"),
    )(*pieces)


def sum_pieces(pieces, name):
    _, R, C = pieces.shape
    tr = _tile(R, 128) if R % 128 == 0 else R

    def body(p_ref, o_ref):
        acc = p_ref[0].astype(F32)
        for d in range(1, N_DEV):
            acc = acc + p_ref[d].astype(F32)
        o_ref[...] = acc

    return pl.pallas_call(
        body, name=name, grid=(R // tr,),
        in_specs=[pl.BlockSpec((N_DEV, tr, C), lambda i: (0, i, 0))],
        out_specs=pl.BlockSpec((tr, C), lambda i: (i, 0)),
        out_shape=jax.ShapeDtypeStruct((R, C), F32),
        compiler_params=_params("parallel"),
    )(pieces)


def adamw(w, g, m, v, name):
    Lyr, R, C = w.shape
    tr = _tile(R, 256) if R % 8 == 0 else R
    c1 = 1.0 / (1.0 - ADAM_B1 ** ADAM_STEP)
    c2 = 1.0 / (1.0 - ADAM_B2 ** ADAM_STEP)

    def body(w_ref, g_ref, m_ref, v_ref, d_ref, nm_ref, nv_ref):
        gv = g_ref[...]
        nm = ADAM_B1 * m_ref[...] + (1.0 - ADAM_B1) * gv
        nv = ADAM_B2 * v_ref[...] + (1.0 - ADAM_B2) * (gv * gv)
        nm_ref[...] = nm
        nv_ref[...] = nv
        d_ref[...] = -ADAM_LR * ((nm * c1) / (jnp.sqrt(nv * c2) + ADAM_EPS) + ADAM_WD * w_ref[...])

    spec = pl.BlockSpec((None, tr, C), lambda l, i: (l, i, 0))
    shp = jax.ShapeDtypeStruct(w.shape, F32)
    return pl.pallas_call(
        body, name=name, grid=(Lyr, R // tr),
        in_specs=[spec] * 4, out_specs=[spec] * 3, out_shape=[shp] * 3,
        compiler_params=_params("parallel", "parallel"),
    )(w, g, m, v)


def _rope_tables(T):
    pos = jnp.arange(T, dtype=F32)
    inv_freq = THETA ** (-jnp.arange(0, ROT, 2, dtype=F32) / ROT)
    ang = pos[:, None] * inv_freq[None, :]
    cs = jnp.concatenate([jnp.cos(ang), jnp.sin(ang)], axis=1)
    half = ROT // 2
    lane = jnp.arange(3 * LANES)
    table, lm = lane // LANES, lane % HD
    src = jnp.where(table == 0, lm % half, half + lm % half)
    i32 = lambda b: b.astype(jnp.int32)
    sign = jnp.where(table == 0, i32(lm < ROT), jnp.where(table == 1, -i32(lm < half), i32((lm >= half) & (lm < ROT))))
    place = (jnp.arange(ROT)[:, None] == src[None, :]) * sign[None, :].astype(F32)
    ones = ((table == 0) & (lm >= ROT)).astype(F32)
    return jnp.dot(cs, place, precision=lax.Precision.HIGHEST) + ones[None, :]


def _tab_specs(rows):
    return [pl.BlockSpec((rows, LANES), lambda i, k=k: (i, k)) for k in range(3)]


def kernel(x, attn_norm, attn_w_qkv, attn_w_o, attn_sink, conv_norm, conv_w_pw1, conv_b_pw1, conv_w_dw, conv_b_dw, conv_ln_g, conv_ln_b, conv_w_pw2, conv_b_pw2, ffn_norm, ffn_w_gu, ffn_w_down, final_norm, loss_target, m_attn_norm, m_attn_w_qkv, m_attn_w_o, m_attn_sink, m_conv_norm, m_conv_w_pw1, m_conv_b_pw1, m_conv_w_dw, m_conv_b_dw, m_conv_ln_g, m_conv_ln_b, m_conv_w_pw2, m_conv_b_pw2, m_ffn_norm, m_ffn_w_gu, m_ffn_w_down, m_final_norm, v_attn_norm, v_attn_w_qkv, v_attn_w_o, v_attn_sink, v_conv_norm, v_conv_w_pw1, v_conv_b_pw1, v_conv_w_dw, v_conv_b_dw, v_conv_ln_g, v_conv_ln_b, v_conv_w_pw2, v_conv_b_pw2, v_ffn_norm, v_ffn_w_gu, v_ffn_w_down, v_final_norm):
    T = x.shape[1]
    x0 = x[0]
    target = loss_target[0]
    ix, iy = lax.axis_index("x"), lax.axis_index("y")
    chip = 2 * ix + iy
    rc = rs1 = rs2 = _rope_tables(T)

    bf = lambda t: t.astype(BF16)
    col_row = [False, True]

    def place(vec, width):
        return lax.dynamic_update_slice(jnp.zeros((vec.shape[0], N_CHIPS * width), F32), vec, (0, chip * width))

    small_rows = jnp.concatenate([
        place(conv_norm, 256), place(conv_b_pw1, 512).reshape(2, D), place(conv_b_dw, 256), place(conv_ln_g, 256),
        place(conv_ln_b, 256), place(conv_b_pw2, 256), jnp.zeros((1, D), F32),
        place(conv_w_dw[0], 256), jnp.zeros((1, D), F32)], axis=0)
    w_qkv, = exchange(_Gather([bf(attn_w_qkv[0])], [False]), "gather_qkv")

    h0, qkv, (w_o, got) = rms_qkv(x0, attn_norm, w_qkv, rc, rs1, rs2,
                                  comm=_Both(_Gather([bf(attn_w_o[0])], [True]), _Scatter([], small_rows)))
    psmall = sum_pieces(got, "sum_small_params") * 0.5
    p_conv_norm, p_b_pw1 = psmall[0:1], psmall[1:3].reshape(1, 2 * D)
    p_b_dw, p_ln_g, p_ln_b, p_b_pw2 = psmall[3:4], psmall[4:5], psmall[5:6], psmall[6:7]
    p_w_dw = psmall[8:40]
    sink = attn_sink[0]
    o, (w_gu0,) = attn_fwd(qkv, sink, comm=_Gather([bf(ffn_w_gu[0])], [False]))
    zero_b = jnp.zeros((1, D), F32)
    x1 = mm_res(o, w_o, x0, zero_b, "attn_out")
    zero_gu = jnp.zeros((1, 2 * DFF), F32)
    h1, gu0, act0, (w_down0, w_pw1, w_pw2) = rms_mm_gate(
        x1, ffn_norm[0:1], w_gu0, zero_gu, DFF, True, BF16, "ffn0_up",
        comm=_Gather([bf(ffn_w_down[0]), bf(conv_w_pw1[0]), bf(conv_w_pw2[0])], [True, False, True]))
    x2 = mm_res(act0, w_down0, x1, zero_b, "ffn0_down")
    h2, pre, glu, _ = rms_mm_gate(x2, p_conv_norm, w_pw1, p_b_pw1, D, False, F32, "conv_pw1")
    dwc, sw, (w_gu1, w_down1) = conv_fwd(glu, p_w_dw, p_b_dw, p_ln_g, p_ln_b,
                                         comm=_Gather([bf(ffn_w_gu[1]), bf(ffn_w_down[1])], col_row))
    x3 = mm_res(sw, w_pw2, x2, p_b_pw2, "conv_pw2")
    h3, gu1, act1, _ = rms_mm_gate(x3, ffn_norm[1:2], w_gu1, zero_gu, DFF, True, BF16, "ffn1_up")
    dx4, loss_part, d_final = mm_res_loss(act1, w_down1, x3, final_norm.reshape(1, D), target)
    loss = lax.psum(loss_part[0, 0], ("x", "y", "c"))

    dgu1, _ = swiglu_bwd(dx4, w_down1, gu1, "ffn1_down_bwd")
    g_down1 = dw_row(act1, dx4, "ffn1_down_dw")
    dx3, d_ffn1, _ = mm_bt_rmsbwd(dgu1, w_gu1, x3, ffn_norm[1:2], dx4, "ffn1_up_bwd")
    g_gu1 = dw_col(h3, dgu1, "ffn1_up_dw")

    ddwc, d_ln_g, d_ln_b, d_b_pw2 = ln_silu_bwd(dx3, w_pw2, dwc, p_ln_g, p_ln_b)
    g_pw2 = dw_row(sw, dx3, "conv_pw2_dw")
    dpre, d_w_dw, d_b_dw, d_b_pw1, (r_gu1, r_down1) = conv_bwd(ddwc, glu, pre, p_w_dw,
                                                               comm=_Scatter([g_gu1, g_down1]))
    dx2, d_conv_norm, _ = mm_bt_rmsbwd(dpre, w_pw1, x2, p_conv_norm, dx3, "conv_pw1_bwd")
    g_pw1 = dw_col(h2, dpre, "conv_pw1_dw")

    dgu0, (r_pw1, r_pw2) = swiglu_bwd(dx2, w_down0, gu0, "ffn0_down_bwd", comm=_Scatter([g_pw1, g_pw2]))
    g_down0 = dw_row(act0, dx2, "ffn0_down_dw")
    dx1, d_ffn0, _ = mm_bt_rmsbwd(dgu0, w_gu0, x1, ffn_norm[0:1], dx2, "ffn0_up_bwd")
    g_gu0 = dw_col(h1, dgu0, "ffn0_up_dw")

    do = mm_bt(dx1, w_o, "attn_out_bwd")
    g_o = dw_row(o, dx1, "attn_out_dw")
    dq, dkc, dvc, d_sink, (r_gu0, r_down0, r_o) = attn_bwd(qkv, o, do, sink, rc, rs1, rs2,
                                                           comm=_Scatter([g_gu0, g_down0, g_o]))
    dqkv = kv_sum(dq, dkc, dvc, rc, rs1, rs2)[None]
    g_qkv = dw_col(h0, dqkv, "attn_qkv_dw")
    dx0, d_attn_norm, (r_qkv,) = mm_bt_rmsbwd(dqkv, w_qkv, x0, attn_norm, dx1, "attn_qkv_bwd",
                                              comm=_Scatter([g_qkv]))

    pad16 = lambda t: jnp.concatenate([t, jnp.zeros((1, D - t.shape[1]), F32)], axis=1)
    small_g = jnp.concatenate([
        d_attn_norm, pad16(d_sink), d_conv_norm, d_b_pw1.reshape(2, D), d_b_dw, d_ln_g, d_ln_b, d_b_pw2,
        d_ffn0, d_ffn1, d_final, jnp.zeros((4, D), F32), d_w_dw], axis=0)
    r_small, = exchange(_Scatter([], small_g), "scatter_small")
    gf_gu = sum_swap([r_gu0, r_gu1], "sum_gu")
    gf_down = sum_swap([r_down0, r_down1], "sum_down")
    gf_pw1, gf_pw2 = sum_swap([r_pw1], "sum_pw1"), sum_swap([r_pw2], "sum_pw2")
    gf_qkv, gf_o = sum_swap([r_qkv], "sum_qkv"), sum_swap([r_o], "sum_o")
    gs = sum_pieces(r_small, "sum_small_grads")

    def take(row0, nrows, width):
        return lax.dynamic_slice(gs, (row0, chip * width), (nrows, width))

    grads = {
        "attn_norm": gs[0:1], "attn_w_qkv": gf_qkv, "attn_w_o": gf_o, "attn_sink": gs[1:2, :N_HEADS],
        "conv_norm": take(2, 1, 256), "conv_w_pw1": gf_pw1,
        "conv_b_pw1": lax.dynamic_slice(gs[3:5].reshape(1, 2 * D), (0, chip * 512), (1, 512)),
        "conv_w_dw": take(16, 32, 256)[None, :CONV_W], "conv_b_dw": take(5, 1, 256), "conv_ln_g": take(6, 1, 256),
        "conv_ln_b": take(7, 1, 256), "conv_w_pw2": gf_pw2, "conv_b_pw2": take(8, 1, 256),
        "ffn_norm": gs[9:11], "ffn_w_gu": gf_gu, "ffn_w_down": gf_down, "final_norm": gs[11],
    }
    weights = dict(attn_norm=attn_norm, attn_w_qkv=attn_w_qkv, attn_w_o=attn_w_o, attn_sink=attn_sink,
                   conv_norm=conv_norm, conv_w_pw1=conv_w_pw1, conv_b_pw1=conv_b_pw1, conv_w_dw=conv_w_dw,
                   conv_b_dw=conv_b_dw, conv_ln_g=conv_ln_g, conv_ln_b=conv_ln_b, conv_w_pw2=conv_w_pw2,
                   conv_b_pw2=conv_b_pw2, ffn_norm=ffn_norm, ffn_w_gu=ffn_w_gu, ffn_w_down=ffn_w_down,
                   final_norm=final_norm)
    m_in = dict(attn_norm=m_attn_norm, attn_w_qkv=m_attn_w_qkv, attn_w_o=m_attn_w_o, attn_sink=m_attn_sink,
                conv_norm=m_conv_norm, conv_w_pw1=m_conv_w_pw1, conv_b_pw1=m_conv_b_pw1, conv_w_dw=m_conv_w_dw,
                conv_b_dw=m_conv_b_dw, conv_ln_g=m_conv_ln_g, conv_ln_b=m_conv_ln_b, conv_w_pw2=m_conv_w_pw2,
                conv_b_pw2=m_conv_b_pw2, ffn_norm=m_ffn_norm, ffn_w_gu=m_ffn_w_gu, ffn_w_down=m_ffn_w_down,
                final_norm=m_final_norm)
    v_in = dict(attn_norm=v_attn_norm, attn_w_qkv=v_attn_w_qkv, attn_w_o=v_attn_w_o, attn_sink=v_attn_sink,
                conv_norm=v_conv_norm, conv_w_pw1=v_conv_w_pw1, conv_b_pw1=v_conv_b_pw1, conv_w_dw=v_conv_w_dw,
                conv_b_dw=v_conv_b_dw, conv_ln_g=v_conv_ln_g, conv_ln_b=v_conv_ln_b, conv_w_pw2=v_conv_w_pw2,
                conv_b_pw2=v_conv_b_pw2, ffn_norm=v_ffn_norm, ffn_w_gu=v_ffn_w_gu, ffn_w_down=v_ffn_w_down,
                final_norm=v_final_norm)
    order = list(weights)
    g_out, d_out, m_out, v_out = [], [], [], []
    for nm in order:
        w = weights[nm]
        shape = w.shape
        as3 = lambda t: t.reshape((1,) * (3 - len(shape)) + shape) if len(shape) < 3 else t.reshape(shape)
        g3 = as3(grads[nm].reshape(shape))
        delta, nm_, nv_ = adamw(as3(w), g3, as3(m_in[nm]), as3(v_in[nm]), "adamw_" + nm)
        g_out.append(g3.reshape(shape))
        d_out.append(delta.reshape(shape))
        m_out.append(nm_.reshape(shape))
        v_out.append(nv_.reshape(shape))
    return (loss, dx0[None], *g_out, *d_out, *m_out, *v_out)
```

```python
import functools
import math

import jax
import jax.numpy as jnp
from jax import lax
from jax.experimental import pallas as pl
from jax.experimental.pallas import tpu as pltpu

F32 = jnp.float32
BF16 = jnp.bfloat16

D = 1024
N_HEADS = 16
N_KV = 4
GROUP = N_HEADS // N_KV
HD = 64
ROT = 16
THETA = 500000.0
BLK = 128
QKV = (N_HEADS + 2 * N_KV) * HD
KV_OFF = N_HEADS * HD
DFF = 2816
CONV_W = 31
CONV_PAD = 15
HALO = 16
CONV_JB = 8
CONV_JB_BWD = 8
EPS = 1e-6
NEG = -1e30
N_CHIPS = 4
N_DEV = 8
LANES = 128
SUBLANES = 8

ADAM_LR, ADAM_B1, ADAM_B2, ADAM_EPS, ADAM_WD, ADAM_STEP = 0.001, 0.9, 0.999, 1e-08, 0.01, 10

VMEM_LIMIT = 56 * 1024 * 1024
MESH = pl.DeviceIdType.MESH


def _params(*sem):
    return pltpu.CompilerParams(dimension_semantics=sem, vmem_limit_bytes=VMEM_LIMIT)


def _tile(n, want):
    if n <= want:
        return n
    for t in range(want, 7, -1):
        if n % t == 0 and t % 8 == 0:
            return t
    return n


MXU_COLS = 256


def _col_chunks(n):
    return [slice(c, min(c + MXU_COLS, n)) for c in range(0, n, MXU_COLS)]


def _sigmoid(v):
    return 1.0 / (1.0 + jnp.exp(-v))


def _rms_fwd(xv, gain):
    r = lax.rsqrt(jnp.mean(xv * xv, axis=-1, keepdims=True) + EPS)
    return xv * r * gain


def _rms_bwd(dh, xv, gain, dres):
    r = lax.rsqrt(jnp.mean(xv * xv, axis=-1, keepdims=True) + EPS)
    xhat = xv * r
    gy = dh * gain
    dx = r * (gy - xhat * jnp.mean(gy * xhat, axis=-1, keepdims=True))
    return dx + dres, dh * xhat


def _rope(blk, c, s1, s2):
    return blk * c + pltpu.roll(blk, LANES - ROT // 2, 1) * s1 + pltpu.roll(blk, ROT // 2, 1) * s2


def _dot(a, b):
    return jnp.dot(a, b, preferred_element_type=F32)


def _dot_tb(a, b):
    return lax.dot_general(a, b, (((1,), (1,)), ((), ())), preferred_element_type=F32)


def _dot_ta(a, b):
    return lax.dot_general(a, b, (((0,), (0,)), ((), ())), preferred_element_type=F32)


def rms_qkv(x, gain, w, rc, rs1, rs2, comm=None):
    T = x.shape[0]
    tm = _tile(T, 512)

    def body(x_ref, g_ref, w_ref, c_ref, s1_ref, s2_ref, h_ref, qkv_ref):
        h = _rms_fwd(x_ref[...], g_ref[...]).astype(BF16)
        h_ref[...] = h
        acc = _dot(h, w_ref[...])
        c, s1, s2 = c_ref[...], s1_ref[...], s2_ref[...]
        n_rot = (KV_OFF + N_KV * HD) // LANES
        for j in range(n_rot):
            sl = slice(LANES * j, LANES * (j + 1))
            roped = _rope(acc[:, sl], c, s1, s2)
            if j < KV_OFF // LANES:
                roped = roped * Q_SCALE
            qkv_ref[:, sl] = roped.astype(BF16)
        qkv_ref[:, n_rot * LANES:] = acc[:, n_rot * LANES:].astype(BF16)

    row = lambda i: (i, 0)
    full = lambda i: (0, 0)
    (h, qkv), got = _call(
        body, name="rms_qkv", grid=(T // tm,),
        in_specs=[pl.BlockSpec((tm, D), row), pl.BlockSpec((1, D), full), pl.BlockSpec((D, QKV), full),
                  *_tab_specs(tm)],
        out_specs=[pl.BlockSpec((tm, D), row), pl.BlockSpec((tm, QKV), row)],
        out_shape=[jax.ShapeDtypeStruct((T, D), BF16), jax.ShapeDtypeStruct((T, QKV), BF16)],
        semantics=("parallel",), args=(x, gain, w, rc, rs1, rs2), comm=comm)
    return h, qkv, got


Q_SCALE = 1.0 / math.sqrt(HD)


def _attn_mask(n, T):
    ci = lax.broadcasted_iota(jnp.int32, (3 * BLK, BLK), 0)
    qi = lax.broadcasted_iota(jnp.int32, (3 * BLK, BLK), 1)
    key_pos = n * BLK - BLK + ci
    return (jnp.abs(ci - BLK - qi) <= BLK) & (key_pos >= 0) & (key_pos < T)


def _kv_padded(kv, first_tile):
    low = lax.broadcasted_iota(jnp.int32, (3 * BLK, LANES), 1) < HD
    zero = jnp.zeros((3 * BLK, LANES), BF16)
    out = {}
    for g in range(N_KV):
        t = kv[:, (first_tile + g // 2) * LANES:(first_tile + g // 2 + 1) * LANES]
        swapped = jnp.concatenate([t[:, HD:], t[:, :HD]], axis=1)
        for p in range(2):
            out[g, p] = jnp.where(low if p == 0 else ~low, t if g % 2 == p else swapped, zero)
    return out


def _softmax_sink(s, valid, sk):
    s = jnp.where(valid, s, NEG)
    m = jnp.maximum(jnp.max(s, axis=0, keepdims=True), sk)
    e = jnp.exp(s - m)
    es = jnp.exp(sk - m)
    inv = 1.0 / (jnp.sum(e, axis=0, keepdims=True) + es)
    return e * inv, es * inv


def _attn_specs(T):
    nb = T // BLK
    kv_blk = 2 * N_KV * HD
    kv_col = KV_OFF // kv_blk
    q_spec = pl.BlockSpec((BLK, KV_OFF), lambda n: (n, 0))
    prev = pl.BlockSpec((BLK, kv_blk), lambda n: (jnp.maximum(n - 1, 0), kv_col))
    own = pl.BlockSpec((BLK, kv_blk), lambda n: (n, kv_col))
    nxt = pl.BlockSpec((BLK, kv_blk), lambda n: (jnp.minimum(n + 1, nb - 1), kv_col))
    return nb, q_spec, prev, own, nxt


def attn_fwd(qkv, sink, comm=None):
    T = qkv.shape[0]
    nb, q_spec, prev, own, nxt = _attn_specs(T)

    def body(sink_ref, q_ref, kp_ref, ko_ref, kn_ref, o_ref):
        valid = _attn_mask(pl.program_id(0), T)
        kv = jnp.concatenate([kp_ref[...], ko_ref[...], kn_ref[...]], axis=0)
        kx, vx = _kv_padded(kv, 0), _kv_padded(kv, 2)
        tile = lambda ref, h: ref[:, (h // 2) * LANES:(h // 2 + 1) * LANES]
        ss = [_dot_tb(kx[h // GROUP, h % 2], tile(q_ref, h)) for h in range(N_HEADS)]
        ps = [_softmax_sink(ss[h], valid, sink_ref[h])[0].astype(BF16) for h in range(N_HEADS)]
        vxt = {k: v.T for k, v in vx.items()}
        for j in range(N_HEADS // 2):
            g = 2 * j // GROUP
            o_t = _dot(vxt[g, 0], ps[2 * j]) + _dot(vxt[g, 1], ps[2 * j + 1])
            o_ref[:, j * LANES:(j + 1) * LANES] = o_t.T.astype(BF16)

    (o,), got = _call(
        body, name="attn_fwd", grid=(nb,),
        in_specs=[pl.BlockSpec(memory_space=pltpu.SMEM), q_spec, prev, own, nxt],
        out_specs=[pl.BlockSpec((BLK, D), lambda n: (n, 0))],
        out_shape=[jax.ShapeDtypeStruct((T, D), BF16)],
        semantics=("parallel",), args=(sink, qkv, qkv, qkv, qkv), comm=comm)
    return o, got


def mm_res(a, w, resid, bias, name):
    T, K = a.shape
    tm = _tile(T, 1024 if K <= D else 512)

    def body(a_ref, w_ref, r_ref, b_ref, o_ref):
        o_ref[...] = _dot(a_ref[...], w_ref[...]) + b_ref[...] + r_ref[...]

    row = lambda i: (i, 0)
    full = lambda i: (0, 0)
    return pl.pallas_call(
        body, name=name, grid=(T // tm,),
        in_specs=[pl.BlockSpec((tm, K), row), pl.BlockSpec((K, D), full), pl.BlockSpec((tm, D), row),
                  pl.BlockSpec((1, D), full)],
        out_specs=pl.BlockSpec((tm, D), row),
        out_shape=jax.ShapeDtypeStruct((T, D), F32),
        compiler_params=_params("parallel"),
    )(a, w, resid, bias)


def rms_mm_gate(x, gain, w, bias, H, swiglu, act_dtype, name, comm=None):
    T = x.shape[0]
    tm = _tile(T, 512)

    def body(x_ref, g_ref, w_ref, b_ref, h_ref, pre_ref, act_ref):
        h = _rms_fwd(x_ref[...], g_ref[...]).astype(BF16)
        h_ref[...] = h
        for cs in _col_chunks(H):
            cs2 = slice(H + cs.start, H + cs.stop)
            a = _dot(h, w_ref[:, cs]) + b_ref[:, cs]
            b = _dot(h, w_ref[:, cs2]) + b_ref[:, cs2]
            pre_ref[0, :, cs] = a.astype(BF16)
            pre_ref[1, :, cs] = b.astype(BF16)
            if swiglu:
                act = a * _sigmoid(a) * b
            else:
                act = a * _sigmoid(b)
            act_ref[:, cs] = act.astype(act_dtype)

    row = lambda i: (i, 0)
    full = lambda i: (0, 0)
    (h, pre, act), got = _call(
        body, name=name, grid=(T // tm,),
        in_specs=[pl.BlockSpec((tm, D), row), pl.BlockSpec((1, D), full),
                  pl.BlockSpec((D, 2 * H), full, pipeline_mode=pl.Buffered(1)), pl.BlockSpec((1, 2 * H), full)],
        out_specs=[pl.BlockSpec((tm, D), row), pl.BlockSpec((2, tm, H), lambda i: (0, i, 0)),
                   pl.BlockSpec((tm, H), row)],
        out_shape=[jax.ShapeDtypeStruct((T, D), BF16), jax.ShapeDtypeStruct((2, T, H), BF16),
                   jax.ShapeDtypeStruct((T, H), act_dtype)],
        semantics=("parallel",), args=(x, gain, w, bias), comm=comm)
    return h, pre, act, got


def _conv_tiles(T):
    tt = _tile(T, 512)
    return tt, tt // SUBLANES, D // LANES


def _fill_strided(ext, p, L):
    main = p[HALO:HALO + SUBLANES * L, :].reshape(SUBLANES, L, LANES)
    ext[CONV_PAD:CONV_PAD + L] = jnp.swapaxes(main, 0, 1)

    def ibody(i, carry):
        ext[i] = p[pl.ds(i + 1, SUBLANES, stride=L), :]
        ext[i + CONV_PAD + L] = p[pl.ds(i + CONV_PAD + L + 1, SUBLANES, stride=L), :]
        return carry

    lax.fori_loop(0, CONV_PAD, ibody, 0, unroll=3)


def _conv_specs(T, tt):
    main = pl.BlockSpec((tt, D), lambda i: (i, 0))
    per = tt // HALO
    prev = pl.BlockSpec((HALO, D), lambda i: (jnp.maximum(i * per - 1, 0), 0))
    nxt = pl.BlockSpec((HALO, D), lambda i: (jnp.minimum((i + 1) * per, T // HALO - 1), 0))
    return main, prev, nxt


def _fill_pad(pad, main_ref, prev_ref, next_ref, i, n_i, tt, nlt):
    keep_p = (i > 0).astype(F32)
    keep_n = (i < n_i - 1).astype(F32)
    for lt in range(nlt):
        sl = slice(lt * LANES, (lt + 1) * LANES)
        pad[lt, 0:HALO, :] = prev_ref[:, sl] * keep_p
        pad[lt, HALO:HALO + tt, :] = main_ref[:, sl]
        pad[lt, HALO + tt:2 * HALO + tt, :] = next_ref[:, sl] * keep_n


def conv_fwd(glu, w_dw, b_dw, ln_g, ln_b, comm=None):
    T = glu.shape[0]
    tt, L, nlt = _conv_tiles(T)
    n_i = T // tt
    main, prev, nxt = _conv_specs(T, tt)

    def body(x_ref, xp_ref, xn_ref, w_ref, b_ref, g_ref, bb_ref, dwc_ref, sw_ref, pad, ob, ext, wk):
        i = pl.program_id(0)
        _fill_pad(pad, x_ref, xp_ref, xn_ref, i, n_i, tt, nlt)
        for lt in range(nlt):
            sl = slice(lt * LANES, (lt + 1) * LANES)
            o = ob.at[lt]
            _fill_strided(ext, pad.at[lt], L)
            for k in range(CONV_W):
                wk[k] = jnp.broadcast_to(w_ref[k:k + 1, sl], (SUBLANES, LANES))

            def jbody(jb, carry):
                j = jb * CONV_JB
                accs = [None] * CONV_JB
                for m in range(CONV_W + CONV_JB - 1):
                    e = ext[j + m]
                    for u in range(CONV_JB):
                        if 0 <= m - u < CONV_W:
                            t = e * wk[m - u]
                            accs[u] = t if accs[u] is None else accs[u] + t
                for u in range(CONV_JB):
                    o[pl.ds(j + u, SUBLANES, stride=L), :] = accs[u]
                return carry

            lax.fori_loop(0, L // CONV_JB, jbody, 0)
        y = jnp.concatenate([ob[lt] for lt in range(nlt)], axis=1) + b_ref[...]
        dwc_ref[...] = y
        mu = jnp.mean(y, axis=-1, keepdims=True)
        yc = y - mu
        var = jnp.mean(yc * yc, axis=-1, keepdims=True)
        z = yc * lax.rsqrt(var + EPS) * g_ref[...] + bb_ref[...]
        sw_ref[...] = (z * _sigmoid(z)).astype(BF16)

    full = lambda i: (0, 0)
    (dwc, sw), got = _call(
        body, name="conv_fwd", grid=(n_i,),
        in_specs=[main, prev, nxt, pl.BlockSpec((32, D), full), pl.BlockSpec((1, D), full),
                  pl.BlockSpec((1, D), full), pl.BlockSpec((1, D), full)],
        out_specs=[pl.BlockSpec((tt, D), lambda i: (i, 0)), pl.BlockSpec((tt, D), lambda i: (i, 0))],
        out_shape=[jax.ShapeDtypeStruct((T, D), F32), jax.ShapeDtypeStruct((T, D), BF16)],
        scratch_shapes=[pltpu.VMEM((nlt, tt + 2 * HALO, LANES), F32), pltpu.VMEM((nlt, tt, LANES), F32),
                        pltpu.VMEM((L + 2 * HALO, SUBLANES, LANES), F32), pltpu.VMEM((32, SUBLANES, LANES), F32)],
        semantics=("parallel",), args=(glu, glu, glu, w_dw, b_dw, ln_g, ln_b), comm=comm)
    return dwc, sw, got


def mm_res_loss(a, w, resid, gain, target):
    T, K = a.shape
    tm = _tile(T, 512)

    def body(a_ref, w_ref, r_ref, g_ref, t_ref, dx_ref, loss_ref, dg_ref):
        @pl.when(pl.program_id(0) == 0)
        def _():
            loss_ref[...] = jnp.zeros_like(loss_ref)
            dg_ref[...] = jnp.zeros_like(dg_ref)

        xv, gain_v = _dot(a_ref[...], w_ref[...]) + r_ref[...], g_ref[...]
        err = _rms_fwd(xv, gain_v) - t_ref[...]
        part = 0.5 * jnp.sum(jnp.mean(err * err, axis=-1, keepdims=True), axis=0, keepdims=True)
        loss_ref[...] += jnp.broadcast_to(part, loss_ref.shape)
        dx, dgr = _rms_bwd(err * (1.0 / D), xv, gain_v, 0.0)
        dx_ref[...] = dx
        dg_ref[...] += jnp.sum(dgr, axis=0, keepdims=True)

    row = lambda i: (i, 0)
    full = lambda i: (0, 0)
    return pl.pallas_call(
        body, name="ffn1_down_loss", grid=(T // tm,),
        in_specs=[pl.BlockSpec((tm, K), row), pl.BlockSpec((K, D), full), pl.BlockSpec((tm, D), row),
                  pl.BlockSpec((1, D), full), pl.BlockSpec((tm, D), row)],
        out_specs=[pl.BlockSpec((tm, D), row), pl.BlockSpec((1, LANES), full), pl.BlockSpec((1, D), full)],
        out_shape=[jax.ShapeDtypeStruct((T, D), F32), jax.ShapeDtypeStruct((1, LANES), F32),
                   jax.ShapeDtypeStruct((1, D), F32)],
        compiler_params=_params("arbitrary"),
    )(a, w, resid, gain, target)


def swiglu_bwd(dx, w_down, pre, name, comm=None):
    T = dx.shape[0]
    H = w_down.shape[0]
    tm = _tile(T, 512)

    def body(dx_ref, w_ref, pre_ref, dpre_ref):
        dxb = dx_ref[...].astype(BF16)
        for cs in _col_chunks(H):
            dact = _dot_tb(dxb, w_ref[cs, :])
            g = pre_ref[0, :, cs].astype(F32)
            u = pre_ref[1, :, cs].astype(F32)
            sg = _sigmoid(g)
            dpre_ref[0, :, cs] = (dact * u * sg * (1.0 + g * (1.0 - sg))).astype(BF16)
            dpre_ref[1, :, cs] = (dact * g * sg).astype(BF16)

    (dpre,), got = _call(
        body, name=name, grid=(T // tm,),
        in_specs=[pl.BlockSpec((tm, D), lambda i: (i, 0)),
                  pl.BlockSpec((H, D), lambda i: (0, 0), pipeline_mode=pl.Buffered(1)),
                  pl.BlockSpec((2, tm, H), lambda i: (0, i, 0))],
        out_specs=[pl.BlockSpec((2, tm, H), lambda i: (0, i, 0))],
        out_shape=[jax.ShapeDtypeStruct((2, T, H), BF16)],
        semantics=("parallel",), args=(dx, w_down, pre), comm=comm)
    return dpre, got


def mm_bt_rmsbwd(dpre, w, x, gain, dres, name, comm=None):
    nh, T, H = dpre.shape
    tm = _tile(T, 512)

    def body(dp_ref, w_ref, x_ref, g_ref, dres_ref, dx_ref, dg_ref):
        @pl.when(pl.program_id(0) == 0)
        def _():
            dg_ref[...] = jnp.zeros_like(dg_ref)

        dh = _dot_tb(dp_ref[0], w_ref[:, 0:H])
        for hf in range(1, nh):
            dh = dh + _dot_tb(dp_ref[hf], w_ref[:, hf * H:(hf + 1) * H])
        dx, dgr = _rms_bwd(dh, x_ref[...], g_ref[...], dres_ref[...])
        dx_ref[...] = dx
        dg_ref[...] += jnp.sum(dgr, axis=0, keepdims=True)

    row = lambda i: (i, 0)
    full = lambda i: (0, 0)
    (dx, dg), got = _call(
        body, name=name, grid=(T // tm,),
        in_specs=[pl.BlockSpec((nh, tm, H), lambda i: (0, i, 0)),
                  pl.BlockSpec((D, nh * H), full, pipeline_mode=pl.Buffered(1)),
                  pl.BlockSpec((tm, D), row), pl.BlockSpec((1, D), full), pl.BlockSpec((tm, D), row)],
        out_specs=[pl.BlockSpec((tm, D), row), pl.BlockSpec((1, D), full)],
        out_shape=[jax.ShapeDtypeStruct((T, D), F32), jax.ShapeDtypeStruct((1, D), F32)],
        semantics=("arbitrary",), args=(dpre, w, x, gain, dres), comm=comm)
    return dx, dg, got


def dw_col(a, dpre, name):
    T = a.shape[0]
    nh, _, H = dpre.shape
    per = nh * H // N_CHIPS
    bph = N_CHIPS // nh
    tt = _tile(T, 2048)
    nt = T // tt

    def body(a_ref, b_ref, o_ref, acc):
        t = pl.program_id(1)

        @pl.when(t == 0)
        def _():
            acc[...] = jnp.zeros_like(acc)

        acc[...] += _dot_ta(a_ref[...], b_ref[...])

        @pl.when(t == nt - 1)
        def _():
            o_ref[...] = acc[...].astype(BF16)

    return pl.pallas_call(
        body, name=name, grid=(N_CHIPS, nt),
        in_specs=[pl.BlockSpec((tt, D), lambda q, t: (t, 0)),
                  pl.BlockSpec((None, tt, per), lambda q, t: (q // bph, t, q % bph))],
        out_specs=pl.BlockSpec((None, D, per), lambda q, t: (q, 0, 0)),
        out_shape=jax.ShapeDtypeStruct((N_CHIPS, D, per), BF16),
        scratch_shapes=[pltpu.VMEM((D, per), F32)],
        compiler_params=_params("parallel", "arbitrary"),
    )(a, dpre)


def dw_row(a, b, name):
    T, R = a.shape
    cw = 1408 if R % 1408 == 0 else 512
    tt = _tile(T, 1024)
    nt = T // tt

    def body(a_ref, b_ref, o_ref, acc):
        t = pl.program_id(1)

        @pl.when(t == 0)
        def _():
            acc[...] = jnp.zeros_like(acc)

        acc[...] += _dot_ta(a_ref[...], b_ref[...].astype(BF16))

        @pl.when(t == nt - 1)
        def _():
            o_ref[...] = acc[...].astype(BF16)

    out = pl.pallas_call(
        body, name=name, grid=(R // cw, nt),
        in_specs=[pl.BlockSpec((tt, cw), lambda q, t: (t, q)), pl.BlockSpec((tt, D), lambda q, t: (t, 0))],
        out_specs=pl.BlockSpec((cw, D), lambda q, t: (q, 0)),
        out_shape=jax.ShapeDtypeStruct((R, D), BF16),
        scratch_shapes=[pltpu.VMEM((cw, D), F32)],
        compiler_params=_params("parallel", "arbitrary"),
    )(a, b)
    return out.reshape(N_CHIPS, R // N_CHIPS, D)


def ln_silu_bwd(dx, w_pw2, dwc, ln_g, ln_b):
    T = dx.shape[0]
    tm = _tile(T, 512)

    def body(dx_ref, w_ref, y_ref, g_ref, b_ref, dy_ref, dg_ref, db_ref, dbo_ref):
        @pl.when(pl.program_id(0) == 0)
        def _():
            dg_ref[...] = jnp.zeros_like(dg_ref)
            db_ref[...] = jnp.zeros_like(db_ref)
            dbo_ref[...] = jnp.zeros_like(dbo_ref)

        dxv = dx_ref[...]
        dsw = _dot_tb(dxv.astype(BF16), w_ref[...])
        y = y_ref[...]
        mu = jnp.mean(y, axis=-1, keepdims=True)
        yc = y - mu
        rstd = lax.rsqrt(jnp.mean(yc * yc, axis=-1, keepdims=True) + EPS)
        xhat = yc * rstd
        z = xhat * g_ref[...] + b_ref[...]
        sg = _sigmoid(z)
        dz = dsw * sg * (1.0 + z * (1.0 - sg))
        dxh = dz * g_ref[...]
        dy_ref[...] = rstd * (dxh - jnp.mean(dxh, axis=-1, keepdims=True)
                              - xhat * jnp.mean(dxh * xhat, axis=-1, keepdims=True))
        dg_ref[...] += jnp.sum(dz * xhat, axis=0, keepdims=True)
        db_ref[...] += jnp.sum(dz, axis=0, keepdims=True)
        dbo_ref[...] += jnp.sum(dxv, axis=0, keepdims=True)

    row = lambda i: (i, 0)
    full = lambda i: (0, 0)
    vec = pl.BlockSpec((1, D), full)
    return pl.pallas_call(
        body, name="ln_silu_bwd", grid=(T // tm,),
        in_specs=[pl.BlockSpec((tm, D), row), pl.BlockSpec((D, D), full), pl.BlockSpec((tm, D), row), vec, vec],
        out_specs=[pl.BlockSpec((tm, D), row), vec, vec, vec],
        out_shape=[jax.ShapeDtypeStruct((T, D), F32)] + [jax.ShapeDtypeStruct((1, D), F32)] * 3,
        compiler_params=_params("arbitrary"),
    )(dx, w_pw2, dwc, ln_g, ln_b)


def conv_bwd(ddwc, glu, pre, w_dw, comm=None):
    T = ddwc.shape[0]
    tt, L, nlt = _conv_tiles(T)
    n_i = T // tt
    main, prev, nxt = _conv_specs(T, tt)

    def body(d_ref, dp_ref, dn_ref, x_ref, xp_ref, xn_ref, pre_ref, w_ref,
             dpre_ref, dw_ref, dbd_ref, dbp_ref, padd, padx, ob, extd, extx, wk):
        i = pl.program_id(0)

        @pl.when(i == 0)
        def _():
            dw_ref[...] = jnp.zeros_like(dw_ref)
            dbd_ref[...] = jnp.zeros_like(dbd_ref)
            dbp_ref[...] = jnp.zeros_like(dbp_ref)

        _fill_pad(padd, d_ref, dp_ref, dn_ref, i, n_i, tt, nlt)
        _fill_pad(padx, x_ref, xp_ref, xn_ref, i, n_i, tt, nlt)
        for lt in range(nlt):
            sl = slice(lt * LANES, (lt + 1) * LANES)
            o = ob.at[lt]
            _fill_strided(extd, padd.at[lt], L)
            _fill_strided(extx, padx.at[lt], L)
            for k in range(CONV_W):
                wk[k] = jnp.broadcast_to(w_ref[k:k + 1, sl], (SUBLANES, LANES))

            nu = CONV_JB_BWD

            def jbody(jb, accs):
                j = jb * nu
                accs = list(accs)
                d = [extd[j + u + CONV_PAD] for u in range(nu)]
                g = [None] * nu
                for m in range(CONV_W + nu - 1):
                    ed = extd[j + 2 * CONV_PAD + nu - 1 - m]
                    ex = extx[j + m]
                    for u in range(nu):
                        k = m - (nu - 1 - u)
                        if 0 <= k < CONV_W:
                            t = ed * wk[k]
                            g[u] = t if g[u] is None else g[u] + t
                        k = m - u
                        if 0 <= k < CONV_W:
                            accs[k] = accs[k] + d[u] * ex
                for u in range(nu):
                    o[pl.ds(j + u, SUBLANES, stride=L), :] = g[u]
                return tuple(accs)

            accs = lax.fori_loop(0, L // nu, jbody, tuple(jnp.zeros((SUBLANES, LANES), F32) for _ in range(CONV_W)))
            for k in range(CONV_W):
                dw_ref[k:k + 1, sl] += jnp.sum(accs[k], axis=0, keepdims=True)
        dglu = jnp.concatenate([ob[lt] for lt in range(nlt)], axis=1)
        a = pre_ref[0].astype(F32)
        gate = pre_ref[1].astype(F32)
        sg = _sigmoid(gate)
        da = dglu * sg
        dgate = dglu * a * sg * (1.0 - sg)
        dpre_ref[0] = da.astype(BF16)
        dpre_ref[1] = dgate.astype(BF16)
        dbd_ref[...] += jnp.sum(d_ref[...], axis=0, keepdims=True)
        dbp_ref[0] += jnp.sum(da, axis=0, keepdims=True)
        dbp_ref[1] += jnp.sum(dgate, axis=0, keepdims=True)

    full = lambda i: (0, 0)
    (dpre, dw, dbd, dbp), got = _call(
        body, name="conv_bwd", grid=(n_i,),
        in_specs=[main, prev, nxt, main, prev, nxt, pl.BlockSpec((2, tt, D), lambda i: (0, i, 0)),
                  pl.BlockSpec((32, D), full)],
        out_specs=[pl.BlockSpec((2, tt, D), lambda i: (0, i, 0)), pl.BlockSpec((32, D), full),
                   pl.BlockSpec((1, D), full), pl.BlockSpec((2, 1, D), lambda i: (0, 0, 0))],
        out_shape=[jax.ShapeDtypeStruct((2, T, D), BF16), jax.ShapeDtypeStruct((32, D), F32),
                   jax.ShapeDtypeStruct((1, D), F32), jax.ShapeDtypeStruct((2, 1, D), F32)],
        scratch_shapes=[pltpu.VMEM((nlt, tt + 2 * HALO, LANES), F32), pltpu.VMEM((nlt, tt + 2 * HALO, LANES), F32),
                        pltpu.VMEM((nlt, tt, LANES), F32), pltpu.VMEM((L + 2 * HALO, SUBLANES, LANES), F32),
                        pltpu.VMEM((L + 2 * HALO, SUBLANES, LANES), F32), pltpu.VMEM((32, SUBLANES, LANES), F32)],
        semantics=("arbitrary",), args=(ddwc, ddwc, ddwc, glu, glu, glu, pre, w_dw), comm=comm)
    return dpre, dw, dbd, dbp, got


def mm_bt(a, w, name):
    T = a.shape[0]
    N = w.shape[0]
    tm = _tile(T, 1024)

    def body(a_ref, w_ref, o_ref):
        o_ref[...] = _dot_tb(a_ref[...].astype(BF16), w_ref[...]).astype(BF16)

    return pl.pallas_call(
        body, name=name, grid=(T // tm,),
        in_specs=[pl.BlockSpec((tm, D), lambda i: (i, 0)), pl.BlockSpec((N, D), lambda i: (0, 0))],
        out_specs=pl.BlockSpec((tm, N), lambda i: (i, 0)),
        out_shape=jax.ShapeDtypeStruct((T, N), BF16),
        compiler_params=_params("parallel"),
    )(a, w)


def attn_bwd(qkv, o, do, sink, rc, rs1, rs2, comm=None):
    T = qkv.shape[0]
    nb, q_spec, prev, own, nxt = _attn_specs(T)
    kvw = N_KV * HD

    def body(sink_ref, q_ref, kp_ref, ko_ref, kn_ref, o_ref, do_ref, c_ref, s1_ref, s2_ref,
             dq_ref, dkc_ref, dvc_ref, dsink_ref):
        n = pl.program_id(0)

        @pl.when(n == 0)
        def _():
            dsink_ref[...] = jnp.zeros_like(dsink_ref)

        valid = _attn_mask(n, T)
        kv = jnp.concatenate([kp_ref[...], ko_ref[...], kn_ref[...]], axis=0)
        kx, vx = _kv_padded(kv, 0), _kv_padded(kv, 2)
        tile = lambda ref, j: ref[:, j * LANES:(j + 1) * LANES]
        ss = [_dot_tb(kx[h // GROUP, h % 2], tile(q_ref, h // 2)) for h in range(N_HEADS)]
        dps = [_dot_tb(vx[h // GROUP, h % 2], tile(do_ref, h // 2)) for h in range(N_HEADS)]
        low_d = lax.broadcasted_iota(jnp.int32, (LANES, BLK), 0) < HD
        deltas = []
        for j in range(N_HEADS // 2):
            prod_t = tile(do_ref, j).astype(F32).T * tile(o_ref, j).astype(F32).T
            deltas.append(jnp.sum(jnp.where(low_d, prod_t, 0.0), axis=0, keepdims=True))
            deltas.append(jnp.sum(jnp.where(low_d, 0.0, prod_t), axis=0, keepdims=True))
        lane = lax.broadcasted_iota(jnp.int32, (1, N_HEADS), 1)
        dsink = jnp.zeros((1, N_HEADS), F32)
        pbs, dss = [], []
        for h in range(N_HEADS):
            p, p_sink = _softmax_sink(ss[h], valid, sink_ref[h])
            dss.append((p * (dps[h] - deltas[h])).astype(BF16))
            pbs.append(p.astype(BF16))
            part = -jnp.sum(p_sink * deltas[h], axis=1, keepdims=True)
            dsink = dsink + jnp.where(lane == h, part, 0.0)
        dsink_ref[...] += dsink
        c, s1, s2 = c_ref[...], s1_ref[...], s2_ref[...]
        kxt = {k: v.T for k, v in kx.items()}
        for j in range(N_HEADS // 2):
            g = 2 * j // GROUP
            dq_t = _dot(kxt[g, 0], dss[2 * j]) + _dot(kxt[g, 1], dss[2 * j + 1])
            dq_ref[:, j * LANES:(j + 1) * LANES] = (_rope(dq_t.T, c, -s1, -s2) * Q_SCALE).astype(BF16)
        low_k = lax.broadcasted_iota(jnp.int32, (3 * BLK, LANES), 1) < HD
        cols = lambda xs, g, p: jnp.concatenate([xs[GROUP * g + p], xs[GROUP * g + 2 + p]], axis=1)
        for t in range(N_KV // 2):
            sums = {}
            for g in (2 * t, 2 * t + 1):
                q2 = jnp.concatenate([tile(q_ref, 2 * g), tile(q_ref, 2 * g + 1)], axis=0)
                do2 = jnp.concatenate([tile(do_ref, 2 * g), tile(do_ref, 2 * g + 1)], axis=0)
                for p in range(2):
                    sums[g, p] = (_dot(cols(dss, g, p), q2), _dot(cols(pbs, g, p), do2))
            for which, ref in ((0, dkc_ref), (1, dvc_ref)):
                keep = jnp.where(low_k, sums[2 * t, 0][which], sums[2 * t + 1, 1][which])
                swap = jnp.where(low_k, sums[2 * t + 1, 0][which], sums[2 * t, 1][which])
                ref[:, t * LANES:(t + 1) * LANES] = keep + pltpu.roll(swap, HD, 1)

    row = lambda n: (n, 0)
    (dq, dkc, dvc, dsink), got = _call(
        body, name="attn_bwd", grid=(nb,),
        in_specs=[pl.BlockSpec(memory_space=pltpu.SMEM), q_spec, prev, own, nxt,
                  pl.BlockSpec((BLK, D), row), pl.BlockSpec((BLK, D), row), *_tab_specs(BLK)],
        out_specs=[pl.BlockSpec((BLK, D), row), pl.BlockSpec((None, 3 * BLK, kvw), lambda n: (n, 0, 0)),
                   pl.BlockSpec((None, 3 * BLK, kvw), lambda n: (n, 0, 0)), pl.BlockSpec((1, N_HEADS), lambda n: (0, 0))],
        out_shape=[jax.ShapeDtypeStruct((T, QKV), BF16), jax.ShapeDtypeStruct((nb, 3 * BLK, kvw), F32),
                   jax.ShapeDtypeStruct((nb, 3 * BLK, kvw), F32), jax.ShapeDtypeStruct((1, N_HEADS), F32)],
        semantics=("arbitrary",), args=(sink, qkv, qkv, qkv, qkv, o, do, rc, rs1, rs2), comm=comm)
    return dq, dkc, dvc, dsink, got


def kv_sum(dqkv, dkc, dvc, rc, rs1, rs2):
    nb = dkc.shape[0]
    T = nb * BLK
    kvw = N_KV * HD

    G = 4
    ng = nb // G

    def gather3(own_ref, prev_ref, before_ref, next_ref, after_ref, m):
        has_before = (m > 0).astype(F32)
        has_after = (m < ng - 1).astype(F32)
        out = []
        for i in range(G):
            from_prev = prev_ref[i - 1] if i > 0 else before_ref[0] * has_before
            from_next = next_ref[i + 1] if i < G - 1 else after_ref[0] * has_after
            out.append(from_prev + own_ref[i] + from_next)
        return jnp.concatenate(out, axis=0)

    def body(_, ko, kp, kb, kn, ka, vo, vp, vb, vn, va, c_ref, s1_ref, s2_ref, out_ref):
        m = pl.program_id(0)
        dk = gather3(ko, kp, kb, kn, ka, m)
        dv = gather3(vo, vp, vb, vn, va, m)
        c, s1, s2 = c_ref[...], s1_ref[...], s2_ref[...]
        for j in range(kvw // LANES):
            sl = slice(LANES * j, LANES * (j + 1))
            out_ref[:, sl] = _rope(dk[:, sl], c, -s1, -s2).astype(BF16)
        out_ref[:, kvw:] = dv.astype(BF16)

    own = pl.BlockSpec((G, BLK, kvw), lambda m: (m, 1, 0))
    prev = pl.BlockSpec((G, BLK, kvw), lambda m: (m, 2, 0))
    before = pl.BlockSpec((1, BLK, kvw), lambda m: (jnp.maximum(G * m - 1, 0), 2, 0))
    nxt = pl.BlockSpec((G, BLK, kvw), lambda m: (m, 0, 0))
    after = pl.BlockSpec((1, BLK, kvw), lambda m: (jnp.minimum(G * m + G, nb - 1), 0, 0))
    five = [own, prev, before, nxt, after]
    return pl.pallas_call(
        body, name="kv_sum", grid=(ng,),
        in_specs=[pl.BlockSpec(memory_space=pl.ANY), *five, *five, *_tab_specs(G * BLK)],
        out_specs=pl.BlockSpec((G * BLK, 2 * kvw), lambda m: (m, KV_OFF // (2 * kvw))),
        out_shape=jax.ShapeDtypeStruct((T, QKV), BF16),
        input_output_aliases={0: 0},
        compiler_params=_params("parallel"),
    )(dqkv, *([dkc] * 5), *([dvc] * 5), rc, rs1, rs2)


def _me():
    return lax.axis_index("x"), lax.axis_index("y"), lax.axis_index("c")


def _half_rows(ref, sharded_rows, chip, core):
    R, C = ref.shape[-2], ref.shape[-1]
    lead = (slice(None),) * (len(ref.shape) - 2)
    if sharded_rows:
        per = R // N_CHIPS
        return ref.at[lead + (pl.ds(chip * per + core * (per // 2), per // 2), slice(None))]
    per = C // N_CHIPS
    return ref.at[lead + (pl.ds(core * (R // 2), R // 2), pl.ds(chip * per, per))]


class _Gather:
    def __init__(self, shards, sharded_rows):
        self.inputs = list(shards)
        self.rows = list(sharded_rows)
        self.n = self.n_in = self.n_out = len(shards)
        self.out_shapes = []
        for s, rows in zip(shards, sharded_rows):
            shp = list(s.shape)
            shp[-2 if rows else -1] *= N_CHIPS
            self.out_shapes.append(jax.ShapeDtypeStruct(tuple(shp), s.dtype))
        self.scratch = [pltpu.SemaphoreType.DMA((self.n, 6)), pltpu.SemaphoreType.DMA((self.n, 6)),
                        pltpu.SemaphoreType.DMA((self.n, 2))]

    def _ctx(self, ins, outs, sems):
        send_sems, recv_sems, local_sems = sems
        x, y, c = _me()
        chips = [(1 - x, y), (x, 1 - y), (1 - x, 1 - y)]

        def half_src(w, core):
            s = ins[w]
            R = s.shape[-2]
            return s.at[pl.ds(core * (R // 2), R // 2), :]

        def dst(w, chip, core):
            return _half_rows(outs[w], self.rows[w], chip, core)

        def copy(w, k, src, chip, core, to):
            return pltpu.make_async_remote_copy(
                src_ref=src, dst_ref=dst(w, chip, core), send_sem=send_sems.at[w, k], recv_sem=recv_sems.at[w, k],
                device_id=to, device_id_type=MESH)

        def local(w, core):
            return pltpu.make_async_copy(half_src(w, core), dst(w, 2 * x + y, core), local_sems.at[w, core])

        def first(w, j):
            qx, qy = chips[j]
            return copy(w, j, half_src(w, c), 2 * x + y, c, (qx, qy, c))

        def landed(w, j):
            qx, qy = chips[j]
            return copy(w, j, dst(w, 2 * qx + qy, c), 2 * qx + qy, c, (x, y, c))

        def passed(w, j):
            qx, qy = chips[j]
            return copy(w, 3 + j, dst(w, 2 * qx + qy, c), 2 * qx + qy, c, (x, y, 1 - c))

        def from_sibling(w, j):
            qx, qy = chips[j]
            return copy(w, 3 + j, dst(w, 2 * qx + qy, 1 - c), 2 * qx + qy, 1 - c, (x, y, c))

        return local, first, landed, passed, from_sibling

    def start(self, ins, outs, sems):
        local, first, _, _, _ = self._ctx(ins, outs, sems)
        for w in range(self.n):
            for core in range(2):
                local(w, core).start()
            for j in range(3):
                first(w, j).start()

    def mid(self, ins, outs, sems):
        _, _, landed, passed, _ = self._ctx(ins, outs, sems)
        for w in range(self.n):
            for j in range(3):
                landed(w, j).wait_recv()
                passed(w, j).start()

    def end(self, ins, outs, sems):
        local, first, _, passed, from_sibling = self._ctx(ins, outs, sems)
        for w in range(self.n):
            for j in range(3):
                from_sibling(w, j).wait_recv()
        for w in range(self.n):
            for j in range(3):
                first(w, j).wait_send()
                passed(w, j).wait_send()
            for core in range(2):
                local(w, core).wait()


class _Scatter:
    def __init__(self, grads, small=None):
        self.inputs = list(grads) + ([small] if small is not None else [])
        self.ng = len(grads)
        self.n = self.n_in = self.n_out = len(self.inputs)
        self.out_shapes = [jax.ShapeDtypeStruct((N_DEV, g.shape[1] // 2, g.shape[2]), g.dtype) for g in grads]
        if small is not None:
            self.out_shapes.append(jax.ShapeDtypeStruct((N_DEV,) + small.shape, small.dtype))
        self.scratch = [pltpu.SemaphoreType.DMA((self.n, N_DEV)), pltpu.SemaphoreType.DMA((self.n, N_DEV)),
                        pltpu.SemaphoreType.DMA((self.n,))]

    def _ctx(self, ins, outs, sems):
        send_sems, recv_sems, local_sems = sems
        x, y, c = _me()
        me = 4 * x + 2 * y + c

        def piece(w, chip, core):
            if w >= self.ng:
                return ins[w]
            half = ins[w].shape[1] // 2
            return ins[w].at[chip, pl.ds(core * half, half), :]

        def peer_of(k):
            return x ^ ((k >> 2) & 1), y ^ ((k >> 1) & 1), c ^ (k & 1)

        def local(w):
            return pltpu.make_async_copy(piece(w, 2 * x + y, c), outs[w].at[me], local_sems.at[w])

        def send(w, k):
            px, py, pc = peer_of(k)
            return pltpu.make_async_remote_copy(
                src_ref=piece(w, 2 * px + py, pc), dst_ref=outs[w].at[me], send_sem=send_sems.at[w, k],
                recv_sem=recv_sems.at[w, k], device_id=(px, py, pc), device_id_type=MESH)

        def recv(w, k):
            px, py, pc = peer_of(k)
            return pltpu.make_async_remote_copy(
                src_ref=piece(w, 2 * x + y, c), dst_ref=outs[w].at[4 * px + 2 * py + pc], send_sem=send_sems.at[w, k],
                recv_sem=recv_sems.at[w, k], device_id=(px, py, pc), device_id_type=MESH)

        return local, send, recv

    def start(self, ins, outs, sems):
        local, send, _ = self._ctx(ins, outs, sems)
        for w in range(self.n):
            local(w).start()
            for k in range(1, N_DEV):
                send(w, k).start()

    def mid(self, ins, outs, sems):
        pass

    def end(self, ins, outs, sems):
        local, send, recv = self._ctx(ins, outs, sems)
        for w in range(self.n):
            for k in range(1, N_DEV):
                recv(w, k).wait_recv()
        for w in range(self.n):
            for k in range(1, N_DEV):
                send(w, k).wait_send()
            local(w).wait()


class _Both:
    def __init__(self, a, b):
        self.a, self.b = a, b
        self.inputs = a.inputs + b.inputs
        self.out_shapes = a.out_shapes + b.out_shapes
        self.scratch = a.scratch + b.scratch
        self.n_in, self.n_out = a.n_in + b.n_in, a.n_out + b.n_out

    def _split(self, ins, outs, sems):
        a, na = self.a, len(self.a.scratch)
        return (ins[:a.n_in], outs[:a.n_out], sems[:na]), (ins[a.n_in:], outs[a.n_out:], sems[na:])

    def start(self, ins, outs, sems):
        pa, pb = self._split(ins, outs, sems)
        self.a.start(*pa)
        self.b.start(*pb)

    def mid(self, ins, outs, sems):
        pa, pb = self._split(ins, outs, sems)
        self.a.mid(*pa)
        self.b.mid(*pb)

    def end(self, ins, outs, sems):
        pa, pb = self._split(ins, outs, sems)
        self.a.end(*pa)
        self.b.end(*pb)


def exchange(plan, name):
    def body(*refs):
        ins, outs, sems = refs[:plan.n_in], refs[plan.n_in:plan.n_in + plan.n_out], refs[plan.n_in + plan.n_out:]
        plan.start(ins, outs, sems)
        plan.mid(ins, outs, sems)
        plan.end(ins, outs, sems)

    any_spec = pl.BlockSpec(memory_space=pl.ANY)
    return pl.pallas_call(
        body, name=name, in_specs=[any_spec] * plan.n_in, out_specs=[any_spec] * plan.n_out,
        out_shape=plan.out_shapes, scratch_shapes=plan.scratch,
    )(*plan.inputs)


def _call(body, *, name, grid, in_specs, out_specs, out_shape, scratch_shapes=(), semantics, args, comm=None):
    if comm is None:
        outs = pl.pallas_call(
            body, name=name, grid=grid, in_specs=in_specs, out_specs=out_specs, out_shape=out_shape,
            scratch_shapes=list(scratch_shapes), compiler_params=_params(*semantics))(*args)
        return outs, []
    n_in, n_out, n_scr = len(in_specs), len(out_specs), len(scratch_shapes)

    total = math.prod(grid)
    first, middle, last = 0, (3 * total) // 4 - 1, total - 1
    assert first <= middle < last

    def at(step):
        lin = pl.program_id(0)
        for d in range(1, len(grid)):
            lin = lin * grid[d] + pl.program_id(d)
        return lin == step

    def hosted(*refs):
        h_in, c_in = refs[:n_in], refs[n_in:n_in + comm.n_in]
        rest = refs[n_in + comm.n_in:]
        h_out, c_out = rest[:n_out], rest[n_out:n_out + comm.n_out]
        rest = rest[n_out + comm.n_out:]
        h_scr, c_scr = rest[:n_scr], rest[n_scr:]

        @pl.when(at(first))
        def _():
            comm.start(c_in, c_out, c_scr)

        body(*h_in, *h_out, *h_scr)

        @pl.when(at(middle))
        def _():
            comm.mid(c_in, c_out, c_scr)

        @pl.when(at(last))
        def _():
            comm.end(c_in, c_out, c_scr)

    any_spec = pl.BlockSpec(memory_space=pl.ANY)
    outs = pl.pallas_call(
        hosted, name=name, grid=grid, in_specs=list(in_specs) + [any_spec] * comm.n_in,
        out_specs=list(out_specs) + [any_spec] * comm.n_out, out_shape=list(out_shape) + comm.out_shapes,
        scratch_shapes=list(scratch_shapes) + comm.scratch,
        compiler_params=_params(*(["arbitrary"] * len(grid))))(*args, *comm.inputs)
    return outs[:n_out], outs[n_out:]


def sum_swap(pieces, name, comm=None):
    nl = len(pieces)
    _, r2, cc = pieces[0].shape
    tr = 128 if r2 % 128 == 0 else r2 // 2
    n = r2 // tr

    def body(*refs):
        p_refs, out = refs[:nl], refs[nl]
        slots, send_sems, local_sems, recv_sem = refs[nl + 1:]
        x, y, c = _me()
        sibling = (x, y, 1 - c)
        l, i = pl.program_id(0), pl.program_id(1)
        step = l * n + i

        def rows(st, core):
            return out.at[st // n, pl.ds(core * r2 + (st % n) * tr, tr), :]

        def copies(st):
            slot = st % 2
            local = pltpu.make_async_copy(slots.at[slot], rows(st, c), local_sems.at[slot])
            remote = pltpu.make_async_remote_copy(
                src_ref=slots.at[slot], dst_ref=rows(st, c), send_sem=send_sems.at[slot], recv_sem=recv_sem,
                device_id=sibling, device_id_type=MESH)
            return local, remote

        for ll in range(nl):
            @pl.when(l == ll)
            def _():
                acc = p_refs[ll][0].astype(F32)
                for d in range(1, N_DEV):
                    acc = acc + p_refs[ll][d].astype(F32)
                slots[step % 2] = acc

        for cp in copies(step):
            cp.start()

        @pl.when(step >= 1)
        def _():
            local, remote = copies(step - 1)
            local.wait()
            remote.wait_send()

        @pl.when(step == nl * n - 1)
        def _():
            local, remote = copies(step)
            local.wait()
            remote.wait_send()
            theirs = out.at[:, pl.ds((1 - c) * r2, r2), :]
            pltpu.make_async_remote_copy(src_ref=theirs, dst_ref=theirs, send_sem=send_sems.at[0],
                                         recv_sem=recv_sem, device_id=sibling, device_id_type=MESH).wait_recv()

    def piece_spec(ll):
        def index(l, i):
            return (0, jnp.where(l == ll, i, jnp.where(l < ll, 0, n - 1)), 0)
        return pl.BlockSpec((N_DEV, tr, cc), index)

    (out,), got = _call(
        body, name=name, grid=(nl, n),
        in_specs=[piece_spec(ll) for ll in range(nl)],
        out_specs=[pl.BlockSpec(memory_space=pl.ANY)],
        out_shape=[jax.ShapeDtypeStruct((nl, 2 * r2, cc), F32)],
        scratch_shapes=[pltpu.VMEM((2, tr, cc), F32), pltpu.SemaphoreType.DMA((2,)), pltpu.SemaphoreType.DMA((2,)),
                        pltpu.SemaphoreType.DMA(())],
        semantics=("arbitrary", "arbitrary"), args=tuple(pieces), comm=comm)
    return (out, got) if comm is not None else out


def sum_pieces(pieces, name):
    _, R, C = pieces.shape
    tr = _tile(R, 128) if R % 128 == 0 else R

    def body(p_ref, o_ref):
        acc = p_ref[0].astype(F32)
        for d in range(1, N_DEV):
            acc = acc + p_ref[d].astype(F32)
        o_ref[...] = acc

    return pl.pallas_call(
        body, name=name, grid=(R // tr,),
        in_specs=[pl.BlockSpec((N_DEV, tr, C), lambda i: (0, i, 0))],
        out_specs=pl.BlockSpec((tr, C), lambda i: (i, 0)),
        out_shape=jax.ShapeDtypeStruct((R, C), F32),
        compiler_params=_params("parallel"),
    )(pieces)


def adamw(w, g, m, v, name):
    Lyr, R, C = w.shape
    tr = _tile(R, 256) if R % 8 == 0 else R
    c1 = 1.0 / (1.0 - ADAM_B1 ** ADAM_STEP)
    c2 = 1.0 / (1.0 - ADAM_B2 ** ADAM_STEP)

    def body(w_ref, g_ref, m_ref, v_ref, d_ref, nm_ref, nv_ref):
        gv = g_ref[...]
        nm = ADAM_B1 * m_ref[...] + (1.0 - ADAM_B1) * gv
        nv = ADAM_B2 * v_ref[...] + (1.0 - ADAM_B2) * (gv * gv)
        nm_ref[...] = nm
        nv_ref[...] = nv
        d_ref[...] = -ADAM_LR * ((nm * c1) / (jnp.sqrt(nv * c2) + ADAM_EPS) + ADAM_WD * w_ref[...])

    spec = pl.BlockSpec((None, tr, C), lambda l, i: (l, i, 0))
    shp = jax.ShapeDtypeStruct(w.shape, F32)
    return pl.pallas_call(
        body, name=name, grid=(Lyr, R // tr),
        in_specs=[spec] * 4, out_specs=[spec] * 3, out_shape=[shp] * 3,
        compiler_params=_params("parallel", "parallel"),
    )(w, g, m, v)


def _rope_tables(T):
    pos = jnp.arange(T, dtype=F32)
    inv_freq = THETA ** (-jnp.arange(0, ROT, 2, dtype=F32) / ROT)
    ang = pos[:, None] * inv_freq[None, :]
    cs = jnp.concatenate([jnp.cos(ang), jnp.sin(ang)], axis=1)
    half = ROT // 2
    lane = jnp.arange(3 * LANES)
    table, lm = lane // LANES, lane % HD
    src = jnp.where(table == 0, lm % half, half + lm % half)
    i32 = lambda b: b.astype(jnp.int32)
    sign = jnp.where(table == 0, i32(lm < ROT), jnp.where(table == 1, -i32(lm < half), i32((lm >= half) & (lm < ROT))))
    place = (jnp.arange(ROT)[:, None] == src[None, :]) * sign[None, :].astype(F32)
    ones = ((table == 0) & (lm >= ROT)).astype(F32)
    return jnp.dot(cs, place, precision=lax.Precision.HIGHEST) + ones[None, :]


def _tab_specs(rows):
    return [pl.BlockSpec((rows, LANES), lambda i, k=k: (i, k)) for k in range(3)]


def kernel(x, attn_norm, attn_w_qkv, attn_w_o, attn_sink, conv_norm, conv_w_pw1, conv_b_pw1, conv_w_dw, conv_b_dw, conv_ln_g, conv_ln_b, conv_w_pw2, conv_b_pw2, ffn_norm, ffn_w_gu, ffn_w_down, final_norm, loss_target, m_attn_norm, m_attn_w_qkv, m_attn_w_o, m_attn_sink, m_conv_norm, m_conv_w_pw1, m_conv_b_pw1, m_conv_w_dw, m_conv_b_dw, m_conv_ln_g, m_conv_ln_b, m_conv_w_pw2, m_conv_b_pw2, m_ffn_norm, m_ffn_w_gu, m_ffn_w_down, m_final_norm, v_attn_norm, v_attn_w_qkv, v_attn_w_o, v_attn_sink, v_conv_norm, v_conv_w_pw1, v_conv_b_pw1, v_conv_w_dw, v_conv_b_dw, v_conv_ln_g, v_conv_ln_b, v_conv_w_pw2, v_conv_b_pw2, v_ffn_norm, v_ffn_w_gu, v_ffn_w_down, v_final_norm):
    T = x.shape[1]
    x0 = x[0]
    target = loss_target[0]
    ix, iy = lax.axis_index("x"), lax.axis_index("y")
    chip = 2 * ix + iy
    rc = rs1 = rs2 = _rope_tables(T)

    bf = lambda t: t.astype(BF16)
    col_row = [False, True]

    def place(vec, width):
        return lax.dynamic_update_slice(jnp.zeros((vec.shape[0], N_CHIPS * width), F32), vec, (0, chip * width))

    small_rows = jnp.concatenate([
        place(conv_norm, 256), place(conv_b_pw1, 512).reshape(2, D), place(conv_b_dw, 256), place(conv_ln_g, 256),
        place(conv_ln_b, 256), place(conv_b_pw2, 256), jnp.zeros((1, D), F32),
        place(conv_w_dw[0], 256), jnp.zeros((1, D), F32)], axis=0)
    w_qkv, = exchange(_Gather([bf(attn_w_qkv[0])], [False]), "gather_qkv")

    h0, qkv, (w_o, got) = rms_qkv(x0, attn_norm, w_qkv, rc, rs1, rs2,
                                  comm=_Both(_Gather([bf(attn_w_o[0])], [True]), _Scatter([], small_rows)))
    psmall = sum_pieces(got, "sum_small_params") * 0.5
    p_conv_norm, p_b_pw1 = psmall[0:1], psmall[1:3].reshape(1, 2 * D)
    p_b_dw, p_ln_g, p_ln_b, p_b_pw2 = psmall[3:4], psmall[4:5], psmall[5:6], psmall[6:7]
    p_w_dw = psmall[8:40]
    sink = attn_sink[0]
    o, (w_gu0,) = attn_fwd(qkv, sink, comm=_Gather([bf(ffn_w_gu[0])], [False]))
    zero_b = jnp.zeros((1, D), F32)
    x1 = mm_res(o, w_o, x0, zero_b, "attn_out")
    zero_gu = jnp.zeros((1, 2 * DFF), F32)
    h1, gu0, act0, (w_down0, w_pw1, w_pw2) = rms_mm_gate(
        x1, ffn_norm[0:1], w_gu0, zero_gu, DFF, True, BF16, "ffn0_up",
        comm=_Gather([bf(ffn_w_down[0]), bf(conv_w_pw1[0]), bf(conv_w_pw2[0])], [True, False, True]))
    x2 = mm_res(act0, w_down0, x1, zero_b, "ffn0_down")
    h2, pre, glu, _ = rms_mm_gate(x2, p_conv_norm, w_pw1, p_b_pw1, D, False, F32, "conv_pw1")
    dwc, sw, (w_gu1, w_down1) = conv_fwd(glu, p_w_dw, p_b_dw, p_ln_g, p_ln_b,
                                         comm=_Gather([bf(ffn_w_gu[1]), bf(ffn_w_down[1])], col_row))
    x3 = mm_res(sw, w_pw2, x2, p_b_pw2, "conv_pw2")
    h3, gu1, act1, _ = rms_mm_gate(x3, ffn_norm[1:2], w_gu1, zero_gu, DFF, True, BF16, "ffn1_up")
    dx4, loss_part, d_final = mm_res_loss(act1, w_down1, x3, final_norm.reshape(1, D), target)
    loss = lax.psum(loss_part[0, 0], ("x", "y", "c"))

    dgu1, _ = swiglu_bwd(dx4, w_down1, gu1, "ffn1_down_bwd")
    g_down1 = dw_row(act1, dx4, "ffn1_down_dw")
    dx3, d_ffn1, _ = mm_bt_rmsbwd(dgu1, w_gu1, x3, ffn_norm[1:2], dx4, "ffn1_up_bwd")
    g_gu1 = dw_col(h3, dgu1, "ffn1_up_dw")

    ddwc, d_ln_g, d_ln_b, d_b_pw2 = ln_silu_bwd(dx3, w_pw2, dwc, p_ln_g, p_ln_b)
    g_pw2 = dw_row(sw, dx3, "conv_pw2_dw")
    dpre, d_w_dw, d_b_dw, d_b_pw1, (r_gu1, r_down1) = conv_bwd(ddwc, glu, pre, p_w_dw,
                                                               comm=_Scatter([g_gu1, g_down1]))
    dx2, d_conv_norm, _ = mm_bt_rmsbwd(dpre, w_pw1, x2, p_conv_norm, dx3, "conv_pw1_bwd")
    g_pw1 = dw_col(h2, dpre, "conv_pw1_dw")

    dgu0, (r_pw1, r_pw2) = swiglu_bwd(dx2, w_down0, gu0, "ffn0_down_bwd", comm=_Scatter([g_pw1, g_pw2]))
    g_down0 = dw_row(act0, dx2, "ffn0_down_dw")
    dx1, d_ffn0, _ = mm_bt_rmsbwd(dgu0, w_gu0, x1, ffn_norm[0:1], dx2, "ffn0_up_bwd")
    g_gu0 = dw_col(h1, dgu0, "ffn0_up_dw")

    do = mm_bt(dx1, w_o, "attn_out_bwd")
    g_o = dw_row(o, dx1, "attn_out_dw")
    dq, dkc, dvc, d_sink, (r_gu0, r_down0, r_o) = attn_bwd(qkv, o, do, sink, rc, rs1, rs2,
                                                           comm=_Scatter([g_gu0, g_down0, g_o]))
    dqkv = kv_sum(dq, dkc, dvc, rc, rs1, rs2)[None]
    g_qkv = dw_col(h0, dqkv, "attn_qkv_dw")
    dx0, d_attn_norm, (r_qkv,) = mm_bt_rmsbwd(dqkv, w_qkv, x0, attn_norm, dx1, "attn_qkv_bwd",
                                              comm=_Scatter([g_qkv]))

    pad16 = lambda t: jnp.concatenate([t, jnp.zeros((1, D - t.shape[1]), F32)], axis=1)
    small_g = jnp.concatenate([
        d_attn_norm, pad16(d_sink), d_conv_norm, d_b_pw1.reshape(2, D), d_b_dw, d_ln_g, d_ln_b, d_b_pw2,
        d_ffn0, d_ffn1, d_final, jnp.zeros((4, D), F32), d_w_dw], axis=0)
    gf_gu, (r_small,) = sum_swap([r_gu0, r_gu1], "sum_gu", comm=_Scatter([], small_g))
    gf_down = sum_swap([r_down0, r_down1], "sum_down")
    gf_pw1, gf_pw2 = sum_swap([r_pw1], "sum_pw1"), sum_swap([r_pw2], "sum_pw2")
    gf_qkv, gf_o = sum_swap([r_qkv], "sum_qkv"), sum_swap([r_o], "sum_o")
    gs = sum_pieces(r_small, "sum_small_grads")

    def take(row0, nrows, width):
        return lax.dynamic_slice(gs, (row0, chip * width), (nrows, width))

    grads = {
        "attn_norm": gs[0:1], "attn_w_qkv": gf_qkv, "attn_w_o": gf_o, "attn_sink": gs[1:2, :N_HEADS],
        "conv_norm": take(2, 1, 256), "conv_w_pw1": gf_pw1,
        "conv_b_pw1": lax.dynamic_slice(gs[3:5].reshape(1, 2 * D), (0, chip * 512), (1, 512)),
        "conv_w_dw": take(16, 32, 256)[None, :CONV_W], "conv_b_dw": take(5, 1, 256), "conv_ln_g": take(6, 1, 256),
        "conv_ln_b": take(7, 1, 256), "conv_w_pw2": gf_pw2, "conv_b_pw2": take(8, 1, 256),
        "ffn_norm": gs[9:11], "ffn_w_gu": gf_gu, "ffn_w_down": gf_down, "final_norm": gs[11],
    }
    weights = dict(attn_norm=attn_norm, attn_w_qkv=attn_w_qkv, attn_w_o=attn_w_o, attn_sink=attn_sink,
                   conv_norm=conv_norm, conv_w_pw1=conv_w_pw1, conv_b_pw1=conv_b_pw1, conv_w_dw=conv_w_dw,
                   conv_b_dw=conv_b_dw, conv_ln_g=conv_ln_g, conv_ln_b=conv_ln_b, conv_w_pw2=conv_w_pw2,
                   conv_b_pw2=conv_b_pw2, ffn_norm=ffn_norm, ffn_w_gu=ffn_w_gu, ffn_w_down=ffn_w_down,
                   final_norm=final_norm)
    m_in = dict(attn_norm=m_attn_norm, attn_w_qkv=m_attn_w_qkv, attn_w_o=m_attn_w_o, attn_sink=m_attn_sink,
                conv_norm=m_conv_norm, conv_w_pw1=m_conv_w_pw1, conv_b_pw1=m_conv_b_pw1, conv_w_dw=m_conv_w_dw,
                conv_b_dw=m_conv_b_dw, conv_ln_g=m_conv_ln_g, conv_ln_b=m_conv_ln_b, conv_w_pw2=m_conv_w_pw2,
                conv_b_pw2=m_conv_b_pw2, ffn_norm=m_ffn_norm, ffn_w_gu=m_ffn_w_gu, ffn_w_down=m_ffn_w_down,
                final_norm=m_final_norm)
    v_in = dict(attn_norm=v_attn_norm, attn_w_qkv=v_attn_w_qkv, attn_w_o=v_attn_w_o, attn_sink=v_attn_sink,
                conv_norm=v_conv_norm, conv_w_pw1=v_conv_w_pw1, conv_b_pw1=v_conv_b_pw1, conv_w_dw=v_conv_w_dw,
                conv_b_dw=v_conv_b_dw, conv_ln_g=v_conv_ln_g, conv_ln_b=v_conv_ln_b, conv_w_pw2=v_conv_w_pw2,
                conv_b_pw2=v_conv_b_pw2, ffn_norm=v_ffn_norm, ffn_w_gu=v_ffn_w_gu, ffn_w_down=v_ffn_w_down,
                final_norm=v_final_norm)
    order = list(weights)
    g_out, d_out, m_out, v_out = [], [], [], []
    for nm in order:
        w = weights[nm]
        shape = w.shape
        as3 = lambda t: t.reshape((1,) * (3 - len(shape)) + shape) if len(shape) < 3 else t.reshape(shape)
        g3 = as3(grads[nm].reshape(shape))
        delta, nm_, nv_ = adamw(as3(w), g3, as3(m_in[nm]), as3(v_in[nm]), "adamw_" + nm)
        g_out.append(g3.reshape(shape))
        d_out.append(delta.reshape(shape))
        m_out.append(nm_.reshape(shape))
        v_out.append(nv_.reshape(shape))
    return (loss, dx0[None], *g_out, *d_out, *m_out, *v_out)
```

```python
import functools
import math

import jax
import jax.numpy as jnp
from jax import lax
from jax.experimental import pallas as pl
from jax.experimental.pallas import tpu as pltpu

F32 = jnp.float32
BF16 = jnp.bfloat16

D = 1024
N_HEADS = 16
N_KV = 4
GROUP = N_HEADS // N_KV
HD = 64
ROT = 16
THETA = 500000.0
BLK = 128
QKV = (N_HEADS + 2 * N_KV) * HD
KV_OFF = N_HEADS * HD
DFF = 2816
CONV_W = 31
CONV_PAD = 15
HALO = 16
CONV_JB = 8
CONV_JB_BWD = 8
EPS = 1e-6
NEG = -1e30
N_CHIPS = 4
N_DEV = 8
LANES = 128
SUBLANES = 8

ADAM_LR, ADAM_B1, ADAM_B2, ADAM_EPS, ADAM_WD, ADAM_STEP = 0.001, 0.9, 0.999, 1e-08, 0.01, 10

VMEM_LIMIT = 56 * 1024 * 1024
MESH = pl.DeviceIdType.MESH


def _params(*sem):
    return pltpu.CompilerParams(dimension_semantics=sem, vmem_limit_bytes=VMEM_LIMIT)


def _tile(n, want):
    if n <= want:
        return n
    for t in range(want, 7, -1):
        if n % t == 0 and t % 8 == 0:
            return t
    return n


MXU_COLS = 256


def _col_chunks(n):
    return [slice(c, min(c + MXU_COLS, n)) for c in range(0, n, MXU_COLS)]


def _sigmoid(v):
    return 1.0 / (1.0 + jnp.exp(-v))


def _rms_fwd(xv, gain):
    r = lax.rsqrt(jnp.mean(xv * xv, axis=-1, keepdims=True) + EPS)
    return xv * r * gain


def _rms_bwd(dh, xv, gain, dres):
    r = lax.rsqrt(jnp.mean(xv * xv, axis=-1, keepdims=True) + EPS)
    xhat = xv * r
    gy = dh * gain
    dx = r * (gy - xhat * jnp.mean(gy * xhat, axis=-1, keepdims=True))
    return dx + dres, dh * xhat


def _rope(blk, c, s1, s2):
    return blk * c + pltpu.roll(blk, LANES - ROT // 2, 1) * s1 + pltpu.roll(blk, ROT // 2, 1) * s2


def _dot(a, b):
    return jnp.dot(a, b, preferred_element_type=F32)


def _dot_tb(a, b):
    return lax.dot_general(a, b, (((1,), (1,)), ((), ())), preferred_element_type=F32)


def _dot_ta(a, b):
    return lax.dot_general(a, b, (((0,), (0,)), ((), ())), preferred_element_type=F32)


def rms_qkv(x, gain, w, rc, rs1, rs2, comm=None):
    T = x.shape[0]
    tm = _tile(T, 512)

    def body(x_ref, g_ref, w_ref, c_ref, s1_ref, s2_ref, h_ref, qkv_ref):
        h = _rms_fwd(x_ref[...], g_ref[...]).astype(BF16)
        h_ref[...] = h
        acc = _dot(h, w_ref[...])
        c, s1, s2 = c_ref[...], s1_ref[...], s2_ref[...]
        n_rot = (KV_OFF + N_KV * HD) // LANES
        for j in range(n_rot):
            sl = slice(LANES * j, LANES * (j + 1))
            roped = _rope(acc[:, sl], c, s1, s2)
            if j < KV_OFF // LANES:
                roped = roped * Q_SCALE
            qkv_ref[:, sl] = roped.astype(BF16)
        qkv_ref[:, n_rot * LANES:] = acc[:, n_rot * LANES:].astype(BF16)

    row = lambda i: (i, 0)
    full = lambda i: (0, 0)
    (h, qkv), got = _call(
        body, name="rms_qkv", grid=(T // tm,),
        in_specs=[pl.BlockSpec((tm, D), row), pl.BlockSpec((1, D), full), pl.BlockSpec((D, QKV), full),
                  *_tab_specs(tm)],
        out_specs=[pl.BlockSpec((tm, D), row), pl.BlockSpec((tm, QKV), row)],
        out_shape=[jax.ShapeDtypeStruct((T, D), BF16), jax.ShapeDtypeStruct((T, QKV), BF16)],
        semantics=("parallel",), args=(x, gain, w, rc, rs1, rs2), comm=comm)
    return h, qkv, got


Q_SCALE = 1.0 / math.sqrt(HD)


def _attn_mask(n, T):
    ci = lax.broadcasted_iota(jnp.int32, (3 * BLK, BLK), 0)
    qi = lax.broadcasted_iota(jnp.int32, (3 * BLK, BLK), 1)
    key_pos = n * BLK - BLK + ci
    return (jnp.abs(ci - BLK - qi) <= BLK) & (key_pos >= 0) & (key_pos < T)


def _kv_padded(kv, first_tile):
    low = lax.broadcasted_iota(jnp.int32, (3 * BLK, LANES), 1) < HD
    zero = jnp.zeros((3 * BLK, LANES), BF16)
    out = {}
    for g in range(N_KV):
        t = kv[:, (first_tile + g // 2) * LANES:(first_tile + g // 2 + 1) * LANES]
        swapped = jnp.concatenate([t[:, HD:], t[:, :HD]], axis=1)
        for p in range(2):
            out[g, p] = jnp.where(low if p == 0 else ~low, t if g % 2 == p else swapped, zero)
    return out


def _softmax_sink(s, valid, sk):
    s = jnp.where(valid, s, NEG)
    m = jnp.maximum(jnp.max(s, axis=0, keepdims=True), sk)
    e = jnp.exp(s - m)
    es = jnp.exp(sk - m)
    inv = 1.0 / (jnp.sum(e, axis=0, keepdims=True) + es)
    return e * inv, es * inv


def _attn_specs(T):
    nb = T // BLK
    kv_blk = 2 * N_KV * HD
    kv_col = KV_OFF // kv_blk
    q_spec = pl.BlockSpec((BLK, KV_OFF), lambda n: (n, 0))
    prev = pl.BlockSpec((BLK, kv_blk), lambda n: (jnp.maximum(n - 1, 0), kv_col))
    own = pl.BlockSpec((BLK, kv_blk), lambda n: (n, kv_col))
    nxt = pl.BlockSpec((BLK, kv_blk), lambda n: (jnp.minimum(n + 1, nb - 1), kv_col))
    return nb, q_spec, prev, own, nxt


def attn_fwd(qkv, sink, comm=None):
    T = qkv.shape[0]
    nb, q_spec, prev, own, nxt = _attn_specs(T)

    def body(sink_ref, q_ref, kp_ref, ko_ref, kn_ref, o_ref):
        valid = _attn_mask(pl.program_id(0), T)
        kv = jnp.concatenate([kp_ref[...], ko_ref[...], kn_ref[...]], axis=0)
        kx, vx = _kv_padded(kv, 0), _kv_padded(kv, 2)
        tile = lambda ref, h: ref[:, (h // 2) * LANES:(h // 2 + 1) * LANES]
        ss = [_dot_tb(kx[h // GROUP, h % 2], tile(q_ref, h)) for h in range(N_HEADS)]
        ps = [_softmax_sink(ss[h], valid, sink_ref[h])[0].astype(BF16) for h in range(N_HEADS)]
        vxt = {k: v.T for k, v in vx.items()}
        for j in range(N_HEADS // 2):
            g = 2 * j // GROUP
            o_t = _dot(vxt[g, 0], ps[2 * j]) + _dot(vxt[g, 1], ps[2 * j + 1])
            o_ref[:, j * LANES:(j + 1) * LANES] = o_t.T.astype(BF16)

    (o,), got = _call(
        body, name="attn_fwd", grid=(nb,),
        in_specs=[pl.BlockSpec(memory_space=pltpu.SMEM), q_spec, prev, own, nxt],
        out_specs=[pl.BlockSpec((BLK, D), lambda n: (n, 0))],
        out_shape=[jax.ShapeDtypeStruct((T, D), BF16)],
        semantics=("parallel",), args=(sink, qkv, qkv, qkv, qkv), comm=comm)
    return o, got


def mm_res(a, w, resid, bias, name):
    T, K = a.shape
    tm = _tile(T, 1024 if K <= D else 512)

    def body(a_ref, w_ref, r_ref, b_ref, o_ref):
        o_ref[...] = _dot(a_ref[...], w_ref[...]) + b_ref[...] + r_ref[...]

    row = lambda i: (i, 0)
    full = lambda i: (0, 0)
    return pl.pallas_call(
        body, name=name, grid=(T // tm,),
        in_specs=[pl.BlockSpec((tm, K), row), pl.BlockSpec((K, D), full), pl.BlockSpec((tm, D), row),
                  pl.BlockSpec((1, D), full)],
        out_specs=pl.BlockSpec((tm, D), row),
        out_shape=jax.ShapeDtypeStruct((T, D), F32),
        compiler_params=_params("parallel"),
    )(a, w, resid, bias)


def rms_mm_gate(x, gain, w, bias, H, swiglu, act_dtype, name, comm=None):
    T = x.shape[0]
    tm = _tile(T, 512)
    n_keep = 3 if swiglu else 2

    def body(x_ref, g_ref, w_ref, b_ref, h_ref, pre_ref, act_ref):
        h = _rms_fwd(x_ref[...], g_ref[...]).astype(BF16)
        h_ref[...] = h
        for cs in _col_chunks(H):
            cs2 = slice(H + cs.start, H + cs.stop)
            a = _dot(h, w_ref[:, cs]) + b_ref[:, cs]
            b = _dot(h, w_ref[:, cs2]) + b_ref[:, cs2]
            if swiglu:
                sg = _sigmoid(a)
                silu = a * sg
                pre_ref[0, :, cs] = (sg + silu * (1.0 - sg)).astype(BF16)
                pre_ref[1, :, cs] = b.astype(BF16)
                pre_ref[2, :, cs] = silu.astype(BF16)
                act = silu * b
            else:
                pre_ref[0, :, cs] = a.astype(BF16)
                pre_ref[1, :, cs] = b.astype(BF16)
                act = a * _sigmoid(b)
            act_ref[:, cs] = act.astype(act_dtype)

    row = lambda i: (i, 0)
    full = lambda i: (0, 0)
    (h, pre, act), got = _call(
        body, name=name, grid=(T // tm,),
        in_specs=[pl.BlockSpec((tm, D), row), pl.BlockSpec((1, D), full),
                  pl.BlockSpec((D, 2 * H), full, pipeline_mode=pl.Buffered(1)), pl.BlockSpec((1, 2 * H), full)],
        out_specs=[pl.BlockSpec((tm, D), row), pl.BlockSpec((n_keep, tm, H), lambda i: (0, i, 0)),
                   pl.BlockSpec((tm, H), row)],
        out_shape=[jax.ShapeDtypeStruct((T, D), BF16), jax.ShapeDtypeStruct((n_keep, T, H), BF16),
                   jax.ShapeDtypeStruct((T, H), act_dtype)],
        semantics=("parallel",), args=(x, gain, w, bias), comm=comm)
    return h, pre, act, got


def _conv_tiles(T):
    tt = _tile(T, 512)
    return tt, tt // SUBLANES, D // LANES


def _fill_strided(ext, p, L):
    main = p[HALO:HALO + SUBLANES * L, :].reshape(SUBLANES, L, LANES)
    ext[CONV_PAD:CONV_PAD + L] = jnp.swapaxes(main, 0, 1)

    def ibody(i, carry):
        ext[i] = p[pl.ds(i + 1, SUBLANES, stride=L), :]
        ext[i + CONV_PAD + L] = p[pl.ds(i + CONV_PAD + L + 1, SUBLANES, stride=L), :]
        return carry

    lax.fori_loop(0, CONV_PAD, ibody, 0, unroll=3)


def _conv_specs(T, tt):
    main = pl.BlockSpec((tt, D), lambda i: (i, 0))
    per = tt // HALO
    prev = pl.BlockSpec((HALO, D), lambda i: (jnp.maximum(i * per - 1, 0), 0))
    nxt = pl.BlockSpec((HALO, D), lambda i: (jnp.minimum((i + 1) * per, T // HALO - 1), 0))
    return main, prev, nxt


def _fill_pad(pad, main_ref, prev_ref, next_ref, i, n_i, tt, nlt):
    keep_p = (i > 0).astype(F32)
    keep_n = (i < n_i - 1).astype(F32)
    for lt in range(nlt):
        sl = slice(lt * LANES, (lt + 1) * LANES)
        pad[lt, 0:HALO, :] = prev_ref[:, sl] * keep_p
        pad[lt, HALO:HALO + tt, :] = main_ref[:, sl]
        pad[lt, HALO + tt:2 * HALO + tt, :] = next_ref[:, sl] * keep_n


def conv_fwd(glu, w_dw, b_dw, ln_g, ln_b, comm=None):
    T = glu.shape[0]
    tt, L, nlt = _conv_tiles(T)
    n_i = T // tt
    main, prev, nxt = _conv_specs(T, tt)

    def body(x_ref, xp_ref, xn_ref, w_ref, b_ref, g_ref, bb_ref, dwc_ref, sw_ref, pad, ob, ext, wk):
        i = pl.program_id(0)
        _fill_pad(pad, x_ref, xp_ref, xn_ref, i, n_i, tt, nlt)
        for lt in range(nlt):
            sl = slice(lt * LANES, (lt + 1) * LANES)
            o = ob.at[lt]
            _fill_strided(ext, pad.at[lt], L)
            for k in range(CONV_W):
                wk[k] = jnp.broadcast_to(w_ref[k:k + 1, sl], (SUBLANES, LANES))

            def jbody(jb, carry):
                j = jb * CONV_JB
                accs = [None] * CONV_JB
                for m in range(CONV_W + CONV_JB - 1):
                    e = ext[j + m]
                    for u in range(CONV_JB):
                        if 0 <= m - u < CONV_W:
                            t = e * wk[m - u]
                            accs[u] = t if accs[u] is None else accs[u] + t
                for u in range(CONV_JB):
                    o[pl.ds(j + u, SUBLANES, stride=L), :] = accs[u]
                return carry

            lax.fori_loop(0, L // CONV_JB, jbody, 0)
        y = jnp.concatenate([ob[lt] for lt in range(nlt)], axis=1) + b_ref[...]
        dwc_ref[...] = y
        mu = jnp.mean(y, axis=-1, keepdims=True)
        yc = y - mu
        var = jnp.mean(yc * yc, axis=-1, keepdims=True)
        z = yc * lax.rsqrt(var + EPS) * g_ref[...] + bb_ref[...]
        sw_ref[...] = (z * _sigmoid(z)).astype(BF16)

    full = lambda i: (0, 0)
    (dwc, sw), got = _call(
        body, name="conv_fwd", grid=(n_i,),
        in_specs=[main, prev, nxt, pl.BlockSpec((32, D), full), pl.BlockSpec((1, D), full),
                  pl.BlockSpec((1, D), full), pl.BlockSpec((1, D), full)],
        out_specs=[pl.BlockSpec((tt, D), lambda i: (i, 0)), pl.BlockSpec((tt, D), lambda i: (i, 0))],
        out_shape=[jax.ShapeDtypeStruct((T, D), F32), jax.ShapeDtypeStruct((T, D), BF16)],
        scratch_shapes=[pltpu.VMEM((nlt, tt + 2 * HALO, LANES), F32), pltpu.VMEM((nlt, tt, LANES), F32),
                        pltpu.VMEM((L + 2 * HALO, SUBLANES, LANES), F32), pltpu.VMEM((32, SUBLANES, LANES), F32)],
        semantics=("parallel",), args=(glu, glu, glu, w_dw, b_dw, ln_g, ln_b), comm=comm)
    return dwc, sw, got


def mm_res_loss(a, w, resid, gain, target):
    T, K = a.shape
    tm = _tile(T, 512)

    def body(a_ref, w_ref, r_ref, g_ref, t_ref, dx_ref, loss_ref, dg_ref):
        @pl.when(pl.program_id(0) == 0)
        def _():
            loss_ref[...] = jnp.zeros_like(loss_ref)
            dg_ref[...] = jnp.zeros_like(dg_ref)

        xv, gain_v = _dot(a_ref[...], w_ref[...]) + r_ref[...], g_ref[...]
        err = _rms_fwd(xv, gain_v) - t_ref[...]
        part = 0.5 * jnp.sum(jnp.mean(err * err, axis=-1, keepdims=True), axis=0, keepdims=True)
        loss_ref[...] += jnp.broadcast_to(part, loss_ref.shape)
        dx, dgr = _rms_bwd(err * (1.0 / D), xv, gain_v, 0.0)
        dx_ref[...] = dx
        dg_ref[...] += jnp.sum(dgr, axis=0, keepdims=True)

    row = lambda i: (i, 0)
    full = lambda i: (0, 0)
    return pl.pallas_call(
        body, name="ffn1_down_loss", grid=(T // tm,),
        in_specs=[pl.BlockSpec((tm, K), row), pl.BlockSpec((K, D), full), pl.BlockSpec((tm, D), row),
                  pl.BlockSpec((1, D), full), pl.BlockSpec((tm, D), row)],
        out_specs=[pl.BlockSpec((tm, D), row), pl.BlockSpec((1, LANES), full), pl.BlockSpec((1, D), full)],
        out_shape=[jax.ShapeDtypeStruct((T, D), F32), jax.ShapeDtypeStruct((1, LANES), F32),
                   jax.ShapeDtypeStruct((1, D), F32)],
        compiler_params=_params("arbitrary"),
    )(a, w, resid, gain, target)


def swiglu_bwd(dx, w_down, pre, name, comm=None):
    T = dx.shape[0]
    H = w_down.shape[0]
    tm = _tile(T, 512)

    def body(dx_ref, w_ref, pre_ref, dpre_ref):
        dxb = dx_ref[...].astype(BF16)
        for cs in _col_chunks(H):
            dact = _dot_tb(dxb, w_ref[cs, :])
            dpre_ref[0, :, cs] = (dact * pre_ref[1, :, cs].astype(F32) * pre_ref[0, :, cs].astype(F32)).astype(BF16)
            dpre_ref[1, :, cs] = (dact * pre_ref[2, :, cs].astype(F32)).astype(BF16)

    (dpre,), got = _call(
        body, name=name, grid=(T // tm,),
        in_specs=[pl.BlockSpec((tm, D), lambda i: (i, 0)),
                  pl.BlockSpec((H, D), lambda i: (0, 0), pipeline_mode=pl.Buffered(1)),
                  pl.BlockSpec((3, tm, H), lambda i: (0, i, 0))],
        out_specs=[pl.BlockSpec((2, tm, H), lambda i: (0, i, 0))],
        out_shape=[jax.ShapeDtypeStruct((2, T, H), BF16)],
        semantics=("parallel",), args=(dx, w_down, pre), comm=comm)
    return dpre, got


def mm_bt_rmsbwd(dpre, w, x, gain, dres, name, comm=None):
    nh, T, H = dpre.shape
    tm = _tile(T, 512)

    def body(dp_ref, w_ref, x_ref, g_ref, dres_ref, dx_ref, dg_ref):
        @pl.when(pl.program_id(0) == 0)
        def _():
            dg_ref[...] = jnp.zeros_like(dg_ref)

        dh = _dot_tb(dp_ref[0], w_ref[:, 0:H])
        for hf in range(1, nh):
            dh = dh + _dot_tb(dp_ref[hf], w_ref[:, hf * H:(hf + 1) * H])
        dx, dgr = _rms_bwd(dh, x_ref[...], g_ref[...], dres_ref[...])
        dx_ref[...] = dx
        dg_ref[...] += jnp.sum(dgr, axis=0, keepdims=True)

    row = lambda i: (i, 0)
    full = lambda i: (0, 0)
    (dx, dg), got = _call(
        body, name=name, grid=(T // tm,),
        in_specs=[pl.BlockSpec((nh, tm, H), lambda i: (0, i, 0)),
                  pl.BlockSpec((D, nh * H), full, pipeline_mode=pl.Buffered(1)),
                  pl.BlockSpec((tm, D), row), pl.BlockSpec((1, D), full), pl.BlockSpec((tm, D), row)],
        out_specs=[pl.BlockSpec((tm, D), row), pl.BlockSpec((1, D), full)],
        out_shape=[jax.ShapeDtypeStruct((T, D), F32), jax.ShapeDtypeStruct((1, D), F32)],
        semantics=("arbitrary",), args=(dpre, w, x, gain, dres), comm=comm)
    return dx, dg, got


def dw_col(a, dpre, name):
    T = a.shape[0]
    nh, _, H = dpre.shape
    per = nh * H // N_CHIPS
    bph = N_CHIPS // nh
    tt = _tile(T, 2048)
    nt = T // tt

    def body(a_ref, b_ref, o_ref, acc):
        t = pl.program_id(1)

        @pl.when(t == 0)
        def _():
            acc[...] = jnp.zeros_like(acc)

        acc[...] += _dot_ta(a_ref[...], b_ref[...])

        @pl.when(t == nt - 1)
        def _():
            o_ref[...] = acc[...].astype(BF16)

    return pl.pallas_call(
        body, name=name, grid=(N_CHIPS, nt),
        in_specs=[pl.BlockSpec((tt, D), lambda q, t: (t, 0)),
                  pl.BlockSpec((None, tt, per), lambda q, t: (q // bph, t, q % bph))],
        out_specs=pl.BlockSpec((None, D, per), lambda q, t: (q, 0, 0)),
        out_shape=jax.ShapeDtypeStruct((N_CHIPS, D, per), BF16),
        scratch_shapes=[pltpu.VMEM((D, per), F32)],
        compiler_params=_params("parallel", "arbitrary"),
    )(a, dpre)


def dw_row(a, b, name):
    T, R = a.shape
    cw = 1408 if R % 1408 == 0 else R
    tt = _tile(T, 1024)
    nt = T // tt

    def body(a_ref, b_ref, o_ref, acc):
        t = pl.program_id(1)

        @pl.when(t == 0)
        def _():
            acc[...] = jnp.zeros_like(acc)

        acc[...] += _dot_ta(a_ref[...], b_ref[...].astype(BF16))

        @pl.when(t == nt - 1)
        def _():
            o_ref[...] = acc[...].astype(BF16)

    out = pl.pallas_call(
        body, name=name, grid=(R // cw, nt),
        in_specs=[pl.BlockSpec((tt, cw), lambda q, t: (t, q)), pl.BlockSpec((tt, D), lambda q, t: (t, 0))],
        out_specs=pl.BlockSpec((cw, D), lambda q, t: (q, 0)),
        out_shape=jax.ShapeDtypeStruct((R, D), BF16),
        scratch_shapes=[pltpu.VMEM((cw, D), F32)],
        compiler_params=_params("parallel", "arbitrary"),
    )(a, b)
    return out.reshape(N_CHIPS, R // N_CHIPS, D)


def ln_silu_bwd(dx, w_pw2, dwc, ln_g, ln_b):
    T = dx.shape[0]
    tm = _tile(T, 512)

    def body(dx_ref, w_ref, y_ref, g_ref, b_ref, dy_ref, dg_ref, db_ref, dbo_ref):
        @pl.when(pl.program_id(0) == 0)
        def _():
            dg_ref[...] = jnp.zeros_like(dg_ref)
            db_ref[...] = jnp.zeros_like(db_ref)
            dbo_ref[...] = jnp.zeros_like(dbo_ref)

        dxv = dx_ref[...]
        dsw = _dot_tb(dxv.astype(BF16), w_ref[...])
        y = y_ref[...]
        mu = jnp.mean(y, axis=-1, keepdims=True)
        yc = y - mu
        rstd = lax.rsqrt(jnp.mean(yc * yc, axis=-1, keepdims=True) + EPS)
        xhat = yc * rstd
        z = xhat * g_ref[...] + b_ref[...]
        sg = _sigmoid(z)
        dz = dsw * sg * (1.0 + z * (1.0 - sg))
        dxh = dz * g_ref[...]
        dy_ref[...] = rstd * (dxh - jnp.mean(dxh, axis=-1, keepdims=True)
                              - xhat * jnp.mean(dxh * xhat, axis=-1, keepdims=True))
        dg_ref[...] += jnp.sum(dz * xhat, axis=0, keepdims=True)
        db_ref[...] += jnp.sum(dz, axis=0, keepdims=True)
        dbo_ref[...] += jnp.sum(dxv, axis=0, keepdims=True)

    row = lambda i: (i, 0)
    full = lambda i: (0, 0)
    vec = pl.BlockSpec((1, D), full)
    return pl.pallas_call(
        body, name="ln_silu_bwd", grid=(T // tm,),
        in_specs=[pl.BlockSpec((tm, D), row), pl.BlockSpec((D, D), full), pl.BlockSpec((tm, D), row), vec, vec],
        out_specs=[pl.BlockSpec((tm, D), row), vec, vec, vec],
        out_shape=[jax.ShapeDtypeStruct((T, D), F32)] + [jax.ShapeDtypeStruct((1, D), F32)] * 3,
        compiler_params=_params("arbitrary"),
    )(dx, w_pw2, dwc, ln_g, ln_b)


def conv_bwd(ddwc, glu, pre, w_dw, comm=None):
    T = ddwc.shape[0]
    tt, L, nlt = _conv_tiles(T)
    n_i = T // tt
    main, prev, nxt = _conv_specs(T, tt)

    def body(d_ref, dp_ref, dn_ref, x_ref, xp_ref, xn_ref, pre_ref, w_ref,
             dpre_ref, dw_ref, dbd_ref, dbp_ref, padd, padx, ob, extd, extx, wk):
        i = pl.program_id(0)

        @pl.when(i == 0)
        def _():
            dw_ref[...] = jnp.zeros_like(dw_ref)
            dbd_ref[...] = jnp.zeros_like(dbd_ref)
            dbp_ref[...] = jnp.zeros_like(dbp_ref)

        _fill_pad(padd, d_ref, dp_ref, dn_ref, i, n_i, tt, nlt)
        _fill_pad(padx, x_ref, xp_ref, xn_ref, i, n_i, tt, nlt)
        for lt in range(nlt):
            sl = slice(lt * LANES, (lt + 1) * LANES)
            o = ob.at[lt]
            _fill_strided(extd, padd.at[lt], L)
            _fill_strided(extx, padx.at[lt], L)
            for k in range(CONV_W):
                wk[k] = jnp.broadcast_to(w_ref[k:k + 1, sl], (SUBLANES, LANES))

            nu = CONV_JB_BWD

            def jbody(jb, accs):
                j = jb * nu
                accs = list(accs)
                d = [extd[j + u + CONV_PAD] for u in range(nu)]
                g = [None] * nu
                for m in range(CONV_W + nu - 1):
                    ed = extd[j + 2 * CONV_PAD + nu - 1 - m]
                    ex = extx[j + m]
                    for u in range(nu):
                        k = m - (nu - 1 - u)
                        if 0 <= k < CONV_W:
                            t = ed * wk[k]
                            g[u] = t if g[u] is None else g[u] + t
                        k = m - u
                        if 0 <= k < CONV_W:
                            accs[k] = accs[k] + d[u] * ex
                for u in range(nu):
                    o[pl.ds(j + u, SUBLANES, stride=L), :] = g[u]
                return tuple(accs)

            accs = lax.fori_loop(0, L // nu, jbody, tuple(jnp.zeros((SUBLANES, LANES), F32) for _ in range(CONV_W)))
            for k in range(CONV_W):
                dw_ref[k:k + 1, sl] += jnp.sum(accs[k], axis=0, keepdims=True)
        dglu = jnp.concatenate([ob[lt] for lt in range(nlt)], axis=1)
        a = pre_ref[0].astype(F32)
        gate = pre_ref[1].astype(F32)
        sg = _sigmoid(gate)
        da = dglu * sg
        dgate = dglu * a * sg * (1.0 - sg)
        dpre_ref[0] = da.astype(BF16)
        dpre_ref[1] = dgate.astype(BF16)
        dbd_ref[...] += jnp.sum(d_ref[...], axis=0, keepdims=True)
        dbp_ref[0] += jnp.sum(da, axis=0, keepdims=True)
        dbp_ref[1] += jnp.sum(dgate, axis=0, keepdims=True)

    full = lambda i: (0, 0)
    (dpre, dw, dbd, dbp), got = _call(
        body, name="conv_bwd", grid=(n_i,),
        in_specs=[main, prev, nxt, main, prev, nxt, pl.BlockSpec((2, tt, D), lambda i: (0, i, 0)),
                  pl.BlockSpec((32, D), full)],
        out_specs=[pl.BlockSpec((2, tt, D), lambda i: (0, i, 0)), pl.BlockSpec((32, D), full),
                   pl.BlockSpec((1, D), full), pl.BlockSpec((2, 1, D), lambda i: (0, 0, 0))],
        out_shape=[jax.ShapeDtypeStruct((2, T, D), BF16), jax.ShapeDtypeStruct((32, D), F32),
                   jax.ShapeDtypeStruct((1, D), F32), jax.ShapeDtypeStruct((2, 1, D), F32)],
        scratch_shapes=[pltpu.VMEM((nlt, tt + 2 * HALO, LANES), F32), pltpu.VMEM((nlt, tt + 2 * HALO, LANES), F32),
                        pltpu.VMEM((nlt, tt, LANES), F32), pltpu.VMEM((L + 2 * HALO, SUBLANES, LANES), F32),
                        pltpu.VMEM((L + 2 * HALO, SUBLANES, LANES), F32), pltpu.VMEM((32, SUBLANES, LANES), F32)],
        semantics=("arbitrary",), args=(ddwc, ddwc, ddwc, glu, glu, glu, pre, w_dw), comm=comm)
    return dpre, dw, dbd, dbp, got


def mm_bt(a, w, name):
    T = a.shape[0]
    N = w.shape[0]
    tm = _tile(T, 1024)

    def body(a_ref, w_ref, o_ref):
        o_ref[...] = _dot_tb(a_ref[...].astype(BF16), w_ref[...]).astype(BF16)

    return pl.pallas_call(
        body, name=name, grid=(T // tm,),
        in_specs=[pl.BlockSpec((tm, D), lambda i: (i, 0)), pl.BlockSpec((N, D), lambda i: (0, 0))],
        out_specs=pl.BlockSpec((tm, N), lambda i: (i, 0)),
        out_shape=jax.ShapeDtypeStruct((T, N), BF16),
        compiler_params=_params("parallel"),
    )(a, w)


def attn_bwd(qkv, o, do, sink, rc, rs1, rs2, comm=None):
    T = qkv.shape[0]
    nb, q_spec, prev, own, nxt = _attn_specs(T)
    kvw = N_KV * HD

    def body(sink_ref, q_ref, kp_ref, ko_ref, kn_ref, o_ref, do_ref, c_ref, s1_ref, s2_ref,
             dq_ref, dkc_ref, dvc_ref, dsink_ref):
        n = pl.program_id(0)

        @pl.when(n == 0)
        def _():
            dsink_ref[...] = jnp.zeros_like(dsink_ref)

        valid = _attn_mask(n, T)
        kv = jnp.concatenate([kp_ref[...], ko_ref[...], kn_ref[...]], axis=0)
        kx, vx = _kv_padded(kv, 0), _kv_padded(kv, 2)
        tile = lambda ref, j: ref[:, j * LANES:(j + 1) * LANES]
        ss = [_dot_tb(kx[h // GROUP, h % 2], tile(q_ref, h // 2)) for h in range(N_HEADS)]
        dps = [_dot_tb(vx[h // GROUP, h % 2], tile(do_ref, h // 2)) for h in range(N_HEADS)]
        low_d = lax.broadcasted_iota(jnp.int32, (LANES, BLK), 0) < HD
        deltas = []
        for j in range(N_HEADS // 2):
            prod_t = tile(do_ref, j).astype(F32).T * tile(o_ref, j).astype(F32).T
            deltas.append(jnp.sum(jnp.where(low_d, prod_t, 0.0), axis=0, keepdims=True))
            deltas.append(jnp.sum(jnp.where(low_d, 0.0, prod_t), axis=0, keepdims=True))
        lane = lax.broadcasted_iota(jnp.int32, (1, N_HEADS), 1)
        dsink = jnp.zeros((1, N_HEADS), F32)
        pbs, dss = [], []
        for h in range(N_HEADS):
            p, p_sink = _softmax_sink(ss[h], valid, sink_ref[h])
            dss.append((p * (dps[h] - deltas[h])).astype(BF16))
            pbs.append(p.astype(BF16))
            part = -jnp.sum(p_sink * deltas[h], axis=1, keepdims=True)
            dsink = dsink + jnp.where(lane == h, part, 0.0)
        dsink_ref[...] += dsink
        c, s1, s2 = c_ref[...], s1_ref[...], s2_ref[...]
        kxt = {k: v.T for k, v in kx.items()}
        for j in range(N_HEADS // 2):
            g = 2 * j // GROUP
            dq_t = _dot(kxt[g, 0], dss[2 * j]) + _dot(kxt[g, 1], dss[2 * j + 1])
            dq_ref[:, j * LANES:(j + 1) * LANES] = (_rope(dq_t.T, c, -s1, -s2) * Q_SCALE).astype(BF16)
        low_k = lax.broadcasted_iota(jnp.int32, (3 * BLK, LANES), 1) < HD
        cols = lambda xs, g, p: jnp.concatenate([xs[GROUP * g + p], xs[GROUP * g + 2 + p]], axis=1)
        for t in range(N_KV // 2):
            sums = {}
            for g in (2 * t, 2 * t + 1):
                q2 = jnp.concatenate([tile(q_ref, 2 * g), tile(q_ref, 2 * g + 1)], axis=0)
                do2 = jnp.concatenate([tile(do_ref, 2 * g), tile(do_ref, 2 * g + 1)], axis=0)
                for p in range(2):
                    sums[g, p] = (_dot(cols(dss, g, p), q2), _dot(cols(pbs, g, p), do2))
            for which, ref in ((0, dkc_ref), (1, dvc_ref)):
                keep = jnp.where(low_k, sums[2 * t, 0][which], sums[2 * t + 1, 1][which])
                swap = jnp.where(low_k, sums[2 * t + 1, 0][which], sums[2 * t, 1][which])
                ref[:, t * LANES:(t + 1) * LANES] = keep + pltpu.roll(swap, HD, 1)

    row = lambda n: (n, 0)
    (dq, dkc, dvc, dsink), got = _call(
        body, name="attn_bwd", grid=(nb,),
        in_specs=[pl.BlockSpec(memory_space=pltpu.SMEM), q_spec, prev, own, nxt,
                  pl.BlockSpec((BLK, D), row), pl.BlockSpec((BLK, D), row), *_tab_specs(BLK)],
        out_specs=[pl.BlockSpec((BLK, D), row), pl.BlockSpec((None, 3 * BLK, kvw), lambda n: (n, 0, 0)),
                   pl.BlockSpec((None, 3 * BLK, kvw), lambda n: (n, 0, 0)), pl.BlockSpec((1, N_HEADS), lambda n: (0, 0))],
        out_shape=[jax.ShapeDtypeStruct((T, QKV), BF16), jax.ShapeDtypeStruct((nb, 3 * BLK, kvw), F32),
                   jax.ShapeDtypeStruct((nb, 3 * BLK, kvw), F32), jax.ShapeDtypeStruct((1, N_HEADS), F32)],
        semantics=("arbitrary",), args=(sink, qkv, qkv, qkv, qkv, o, do, rc, rs1, rs2), comm=comm)
    return dq, dkc, dvc, dsink, got


def kv_sum(dqkv, dkc, dvc, rc, rs1, rs2):
    nb = dkc.shape[0]
    T = nb * BLK
    kvw = N_KV * HD

    G = 4
    ng = nb // G

    def gather3(own_ref, prev_ref, before_ref, next_ref, after_ref, m):
        has_before = (m > 0).astype(F32)
        has_after = (m < ng - 1).astype(F32)
        out = []
        for i in range(G):
            from_prev = prev_ref[i - 1] if i > 0 else before_ref[0] * has_before
            from_next = next_ref[i + 1] if i < G - 1 else after_ref[0] * has_after
            out.append(from_prev + own_ref[i] + from_next)
        return jnp.concatenate(out, axis=0)

    def body(_, ko, kp, kb, kn, ka, vo, vp, vb, vn, va, c_ref, s1_ref, s2_ref, out_ref):
        m = pl.program_id(0)
        dk = gather3(ko, kp, kb, kn, ka, m)
        dv = gather3(vo, vp, vb, vn, va, m)
        c, s1, s2 = c_ref[...], s1_ref[...], s2_ref[...]
        for j in range(kvw // LANES):
            sl = slice(LANES * j, LANES * (j + 1))
            out_ref[:, sl] = _rope(dk[:, sl], c, -s1, -s2).astype(BF16)
        out_ref[:, kvw:] = dv.astype(BF16)

    own = pl.BlockSpec((G, BLK, kvw), lambda m: (m, 1, 0))
    prev = pl.BlockSpec((G, BLK, kvw), lambda m: (m, 2, 0))
    before = pl.BlockSpec((1, BLK, kvw), lambda m: (jnp.maximum(G * m - 1, 0), 2, 0))
    nxt = pl.BlockSpec((G, BLK, kvw), lambda m: (m, 0, 0))
    after = pl.BlockSpec((1, BLK, kvw), lambda m: (jnp.minimum(G * m + G, nb - 1), 0, 0))
    five = [own, prev, before, nxt, after]
    return pl.pallas_call(
        body, name="kv_sum", grid=(ng,),
        in_specs=[pl.BlockSpec(memory_space=pl.ANY), *five, *five, *_tab_specs(G * BLK)],
        out_specs=pl.BlockSpec((G * BLK, 2 * kvw), lambda m: (m, KV_OFF // (2 * kvw))),
        out_shape=jax.ShapeDtypeStruct((T, QKV), BF16),
        input_output_aliases={0: 0},
        compiler_params=_params("parallel"),
    )(dqkv, *([dkc] * 5), *([dvc] * 5), rc, rs1, rs2)


def _me():
    return lax.axis_index("x"), lax.axis_index("y"), lax.axis_index("c")


def _half_rows(ref, sharded_rows, chip, core):
    R, C = ref.shape[-2], ref.shape[-1]
    lead = (slice(None),) * (len(ref.shape) - 2)
    if sharded_rows:
        per = R // N_CHIPS
        return ref.at[lead + (pl.ds(chip * per + core * (per // 2), per // 2), slice(None))]
    per = C // N_CHIPS
    return ref.at[lead + (pl.ds(core * (R // 2), R // 2), pl.ds(chip * per, per))]


class _Gather:
    def __init__(self, shards, sharded_rows):
        self.inputs = list(shards)
        self.rows = list(sharded_rows)
        self.n = self.n_in = self.n_out = len(shards)
        self.out_shapes = []
        for s, rows in zip(shards, sharded_rows):
            shp = list(s.shape)
            shp[-2 if rows else -1] *= N_CHIPS
            self.out_shapes.append(jax.ShapeDtypeStruct(tuple(shp), s.dtype))
        self.scratch = [pltpu.SemaphoreType.DMA((self.n, 6)), pltpu.SemaphoreType.DMA((self.n, 6)),
                        pltpu.SemaphoreType.DMA((self.n, 2))]

    def _ctx(self, ins, outs, sems):
        send_sems, recv_sems, local_sems = sems
        x, y, c = _me()
        chips = [(1 - x, y), (x, 1 - y), (1 - x, 1 - y)]

        def half_src(w, core):
            s = ins[w]
            R = s.shape[-2]
            return s.at[pl.ds(core * (R // 2), R // 2), :]

        def dst(w, chip, core):
            return _half_rows(outs[w], self.rows[w], chip, core)

        def copy(w, k, src, chip, core, to):
            return pltpu.make_async_remote_copy(
                src_ref=src, dst_ref=dst(w, chip, core), send_sem=send_sems.at[w, k], recv_sem=recv_sems.at[w, k],
                device_id=to, device_id_type=MESH)

        def local(w, core):
            return pltpu.make_async_copy(half_src(w, core), dst(w, 2 * x + y, core), local_sems.at[w, core])

        def first(w, j):
            qx, qy = chips[j]
            return copy(w, j, half_src(w, c), 2 * x + y, c, (qx, qy, c))

        def landed(w, j):
            qx, qy = chips[j]
            return copy(w, j, dst(w, 2 * qx + qy, c), 2 * qx + qy, c, (x, y, c))

        def passed(w, j):
            qx, qy = chips[j]
            return copy(w, 3 + j, dst(w, 2 * qx + qy, c), 2 * qx + qy, c, (x, y, 1 - c))

        def from_sibling(w, j):
            qx, qy = chips[j]
            return copy(w, 3 + j, dst(w, 2 * qx + qy, 1 - c), 2 * qx + qy, 1 - c, (x, y, c))

        return local, first, landed, passed, from_sibling

    def start(self, ins, outs, sems):
        local, first, _, _, _ = self._ctx(ins, outs, sems)
        for w in range(self.n):
            for core in range(2):
                local(w, core).start()
            for j in range(3):
                first(w, j).start()

    def mid(self, ins, outs, sems):
        _, _, landed, passed, _ = self._ctx(ins, outs, sems)
        for w in range(self.n):
            for j in range(3):
                landed(w, j).wait_recv()
                passed(w, j).start()

    def end(self, ins, outs, sems):
        local, first, _, passed, from_sibling = self._ctx(ins, outs, sems)
        for w in range(self.n):
            for j in range(3):
                from_sibling(w, j).wait_recv()
        for w in range(self.n):
            for j in range(3):
                first(w, j).wait_send()
                passed(w, j).wait_send()
            for core in range(2):
                local(w, core).wait()


class _Scatter:
    def __init__(self, grads, small=None):
        self.inputs = list(grads) + ([small] if small is not None else [])
        self.ng = len(grads)
        self.n = self.n_in = self.n_out = len(self.inputs)
        self.out_shapes = [jax.ShapeDtypeStruct((N_DEV, g.shape[1] // 2, g.shape[2]), g.dtype) for g in grads]
        if small is not None:
            self.out_shapes.append(jax.ShapeDtypeStruct((N_DEV,) + small.shape, small.dtype))
        self.scratch = [pltpu.SemaphoreType.DMA((self.n, N_DEV)), pltpu.SemaphoreType.DMA((self.n, N_DEV)),
                        pltpu.SemaphoreType.DMA((self.n,))]

    def _ctx(self, ins, outs, sems):
        send_sems, recv_sems, local_sems = sems
        x, y, c = _me()
        me = 4 * x + 2 * y + c

        def piece(w, chip, core):
            if w >= self.ng:
                return ins[w]
            half = ins[w].shape[1] // 2
            return ins[w].at[chip, pl.ds(core * half, half), :]

        def peer_of(k):
            return x ^ ((k >> 2) & 1), y ^ ((k >> 1) & 1), c ^ (k & 1)

        def local(w):
            return pltpu.make_async_copy(piece(w, 2 * x + y, c), outs[w].at[me], local_sems.at[w])

        def send(w, k):
            px, py, pc = peer_of(k)
            return pltpu.make_async_remote_copy(
                src_ref=piece(w, 2 * px + py, pc), dst_ref=outs[w].at[me], send_sem=send_sems.at[w, k],
                recv_sem=recv_sems.at[w, k], device_id=(px, py, pc), device_id_type=MESH)

        def recv(w, k):
            px, py, pc = peer_of(k)
            return pltpu.make_async_remote_copy(
                src_ref=piece(w, 2 * x + y, c), dst_ref=outs[w].at[4 * px + 2 * py + pc], send_sem=send_sems.at[w, k],
                recv_sem=recv_sems.at[w, k], device_id=(px, py, pc), device_id_type=MESH)

        return local, send, recv

    def start(self, ins, outs, sems):
        local, send, _ = self._ctx(ins, outs, sems)
        for w in range(self.n):
            local(w).start()
            for k in range(1, N_DEV):
                send(w, k).start()

    def mid(self, ins, outs, sems):
        pass

    def end(self, ins, outs, sems):
        local, send, recv = self._ctx(ins, outs, sems)
        for w in range(self.n):
            for k in range(1, N_DEV):
                recv(w, k).wait_recv()
        for w in range(self.n):
            for k in range(1, N_DEV):
                send(w, k).wait_send()
            local(w).wait()


class _Both:
    def __init__(self, a, b):
        self.a, self.b = a, b
        self.inputs = a.inputs + b.inputs
        self.out_shapes = a.out_shapes + b.out_shapes
        self.scratch = a.scratch + b.scratch
        self.n_in, self.n_out = a.n_in + b.n_in, a.n_out + b.n_out

    def _split(self, ins, outs, sems):
        a, na = self.a, len(self.a.scratch)
        return (ins[:a.n_in], outs[:a.n_out], sems[:na]), (ins[a.n_in:], outs[a.n_out:], sems[na:])

    def start(self, ins, outs, sems):
        pa, pb = self._split(ins, outs, sems)
        self.a.start(*pa)
        self.b.start(*pb)

    def mid(self, ins, outs, sems):
        pa, pb = self._split(ins, outs, sems)
        self.a.mid(*pa)
        self.b.mid(*pb)

    def end(self, ins, outs, sems):
        pa, pb = self._split(ins, outs, sems)
        self.a.end(*pa)
        self.b.end(*pb)


def exchange(plan, name):
    def body(*refs):
        ins, outs, sems = refs[:plan.n_in], refs[plan.n_in:plan.n_in + plan.n_out], refs[plan.n_in + plan.n_out:]
        plan.start(ins, outs, sems)
        plan.mid(ins, outs, sems)
        plan.end(ins, outs, sems)

    any_spec = pl.BlockSpec(memory_space=pl.ANY)
    return pl.pallas_call(
        body, name=name, in_specs=[any_spec] * plan.n_in, out_specs=[any_spec] * plan.n_out,
        out_shape=plan.out_shapes, scratch_shapes=plan.scratch,
    )(*plan.inputs)


def _call(body, *, name, grid, in_specs, out_specs, out_shape, scratch_shapes=(), semantics, args, comm=None):
    if comm is None:
        outs = pl.pallas_call(
            body, name=name, grid=grid, in_specs=in_specs, out_specs=out_specs, out_shape=out_shape,
            scratch_shapes=list(scratch_shapes), compiler_params=_params(*semantics))(*args)
        return outs, []
    n_in, n_out, n_scr = len(in_specs), len(out_specs), len(scratch_shapes)

    total = math.prod(grid)
    first, middle, last = 0, (3 * total) // 4 - 1, total - 1
    assert first <= middle < last

    def at(step):
        lin = pl.program_id(0)
        for d in range(1, len(grid)):
            lin = lin * grid[d] + pl.program_id(d)
        return lin == step

    def hosted(*refs):
        h_in, c_in = refs[:n_in], refs[n_in:n_in + comm.n_in]
        rest = refs[n_in + comm.n_in:]
        h_out, c_out = rest[:n_out], rest[n_out:n_out + comm.n_out]
        rest = rest[n_out + comm.n_out:]
        h_scr, c_scr = rest[:n_scr], rest[n_scr:]

        @pl.when(at(first))
        def _():
            comm.start(c_in, c_out, c_scr)

        body(*h_in, *h_out, *h_scr)

        @pl.when(at(middle))
        def _():
            comm.mid(c_in, c_out, c_scr)

        @pl.when(at(last))
        def _():
            comm.end(c_in, c_out, c_scr)

    any_spec = pl.BlockSpec(memory_space=pl.ANY)
    outs = pl.pallas_call(
        hosted, name=name, grid=grid, in_specs=list(in_specs) + [any_spec] * comm.n_in,
        out_specs=list(out_specs) + [any_spec] * comm.n_out, out_shape=list(out_shape) + comm.out_shapes,
        scratch_shapes=list(scratch_shapes) + comm.scratch,
        compiler_params=_params(*(["arbitrary"] * len(grid))))(*args, *comm.inputs)
    return outs[:n_out], outs[n_out:]


def sum_swap(pieces, name, comm=None):
    nl = len(pieces)
    _, r2, cc = pieces[0].shape
    tr = 128 if r2 % 128 == 0 else r2 // 2
    n = r2 // tr

    def body(*refs):
        p_refs, out = refs[:nl], refs[nl]
        slots, send_sems, local_sems, recv_sem = refs[nl + 1:]
        x, y, c = _me()
        sibling = (x, y, 1 - c)
        l, i = pl.program_id(0), pl.program_id(1)
        step = l * n + i

        def rows(st, core):
            return out.at[st // n, pl.ds(core * r2 + (st % n) * tr, tr), :]

        def copies(st):
            slot = st % 2
            local = pltpu.make_async_copy(slots.at[slot], rows(st, c), local_sems.at[slot])
            remote = pltpu.make_async_remote_copy(
                src_ref=slots.at[slot], dst_ref=rows(st, c), send_sem=send_sems.at[slot], recv_sem=recv_sem,
                device_id=sibling, device_id_type=MESH)
            return local, remote

        for ll in range(nl):
            @pl.when(l == ll)
            def _():
                acc = p_refs[ll][0].astype(F32)
                for d in range(1, N_DEV):
                    acc = acc + p_refs[ll][d].astype(F32)
                slots[step % 2] = acc

        for cp in copies(step):
            cp.start()

        @pl.when(step >= 1)
        def _():
            local, remote = copies(step - 1)
            local.wait()
            remote.wait_send()

        @pl.when(step == nl * n - 1)
        def _():
            local, remote = copies(step)
            local.wait()
            remote.wait_send()
            theirs = out.at[:, pl.ds((1 - c) * r2, r2), :]
            pltpu.make_async_remote_copy(src_ref=theirs, dst_ref=theirs, send_sem=send_sems.at[0],
                                         recv_sem=recv_sem, device_id=sibling, device_id_type=MESH).wait_recv()

    def piece_spec(ll):
        def index(l, i):
            return (0, jnp.where(l == ll, i, jnp.where(l < ll, 0, n - 1)), 0)
        return pl.BlockSpec((N_DEV, tr, cc), index)

    (out,), got = _call(
        body, name=name, grid=(nl, n),
        in_specs=[piece_spec(ll) for ll in range(nl)],
        out_specs=[pl.BlockSpec(memory_space=pl.ANY)],
        out_shape=[jax.ShapeDtypeStruct((nl, 2 * r2, cc), F32)],
        scratch_shapes=[pltpu.VMEM((2, tr, cc), F32), pltpu.SemaphoreType.DMA((2,)), pltpu.SemaphoreType.DMA((2,)),
                        pltpu.SemaphoreType.DMA(())],
        semantics=("arbitrary", "arbitrary"), args=tuple(pieces), comm=comm)
    return (out, got) if comm is not None else out


def sum_pieces(pieces, name):
    _, R, C = pieces.shape
    tr = _tile(R, 128) if R % 128 == 0 else R

    def body(p_ref, o_ref):
        acc = p_ref[0].astype(F32)
        for d in range(1, N_DEV):
            acc = acc + p_ref[d].astype(F32)
        o_ref[...] = acc

    return pl.pallas_call(
        body, name=name, grid=(R // tr,),
        in_specs=[pl.BlockSpec((N_DEV, tr, C), lambda i: (0, i, 0))],
        out_specs=pl.BlockSpec((tr, C), lambda i: (i, 0)),
        out_shape=jax.ShapeDtypeStruct((R, C), F32),
        compiler_params=_params("parallel"),
    )(pieces)


def adamw(w, g, m, v, name):
    Lyr, R, C = w.shape
    tr = _tile(R, 256) if R % 8 == 0 else R
    c1 = 1.0 / (1.0 - ADAM_B1 ** ADAM_STEP)
    c2 = 1.0 / (1.0 - ADAM_B2 ** ADAM_STEP)

    def body(w_ref, g_ref, m_ref, v_ref, d_ref, nm_ref, nv_ref):
        gv = g_ref[...]
        nm = ADAM_B1 * m_ref[...] + (1.0 - ADAM_B1) * gv
        nv = ADAM_B2 * v_ref[...] + (1.0 - ADAM_B2) * (gv * gv)
        nm_ref[...] = nm
        nv_ref[...] = nv
        d_ref[...] = -ADAM_LR * ((nm * c1) / (jnp.sqrt(nv * c2) + ADAM_EPS) + ADAM_WD * w_ref[...])

    spec = pl.BlockSpec((None, tr, C), lambda l, i: (l, i, 0))
    shp = jax.ShapeDtypeStruct(w.shape, F32)
    return pl.pallas_call(
        body, name=name, grid=(Lyr, R // tr),
        in_specs=[spec] * 4, out_specs=[spec] * 3, out_shape=[shp] * 3,
        compiler_params=_params("parallel", "parallel"),
    )(w, g, m, v)


def _rope_tables(T):
    pos = jnp.arange(T, dtype=F32)
    inv_freq = THETA ** (-jnp.arange(0, ROT, 2, dtype=F32) / ROT)
    ang = pos[:, None] * inv_freq[None, :]
    cs = jnp.concatenate([jnp.cos(ang), jnp.sin(ang)], axis=1)
    half = ROT // 2
    lane = jnp.arange(3 * LANES)
    table, lm = lane // LANES, lane % HD
    src = jnp.where(table == 0, lm % half, half + lm % half)
    i32 = lambda b: b.astype(jnp.int32)
    sign = jnp.where(table == 0, i32(lm < ROT), jnp.where(table == 1, -i32(lm < half), i32((lm >= half) & (lm < ROT))))
    place = (jnp.arange(ROT)[:, None] == src[None, :]) * sign[None, :].astype(F32)
    ones = ((table == 0) & (lm >= ROT)).astype(F32)
    return jnp.dot(cs, place, precision=lax.Precision.HIGHEST) + ones[None, :]


def _tab_specs(rows):
    return [pl.BlockSpec((rows, LANES), lambda i, k=k: (i, k)) for k in range(3)]


def kernel(x, attn_norm, attn_w_qkv, attn_w_o, attn_sink, conv_norm, conv_w_pw1, conv_b_pw1, conv_w_dw, conv_b_dw, conv_ln_g, conv_ln_b, conv_w_pw2, conv_b_pw2, ffn_norm, ffn_w_gu, ffn_w_down, final_norm, loss_target, m_attn_norm, m_attn_w_qkv, m_attn_w_o, m_attn_sink, m_conv_norm, m_conv_w_pw1, m_conv_b_pw1, m_conv_w_dw, m_conv_b_dw, m_conv_ln_g, m_conv_ln_b, m_conv_w_pw2, m_conv_b_pw2, m_ffn_norm, m_ffn_w_gu, m_ffn_w_down, m_final_norm, v_attn_norm, v_attn_w_qkv, v_attn_w_o, v_attn_sink, v_conv_norm, v_conv_w_pw1, v_conv_b_pw1, v_conv_w_dw, v_conv_b_dw, v_conv_ln_g, v_conv_ln_b, v_conv_w_pw2, v_conv_b_pw2, v_ffn_norm, v_ffn_w_gu, v_ffn_w_down, v_final_norm):
    T = x.shape[1]
    x0 = x[0]
    target = loss_target[0]
    ix, iy = lax.axis_index("x"), lax.axis_index("y")
    chip = 2 * ix + iy
    rc = rs1 = rs2 = _rope_tables(T)

    bf = lambda t: t.astype(BF16)
    col_row = [False, True]

    def place(vec, width):
        return lax.dynamic_update_slice(jnp.zeros((vec.shape[0], N_CHIPS * width), F32), vec, (0, chip * width))

    small_rows = jnp.concatenate([
        place(conv_norm, 256), place(conv_b_pw1, 512).reshape(2, D), place(conv_b_dw, 256), place(conv_ln_g, 256),
        place(conv_ln_b, 256), place(conv_b_pw2, 256), jnp.zeros((1, D), F32),
        place(conv_w_dw[0], 256), jnp.zeros((1, D), F32)], axis=0)
    w_qkv, = exchange(_Gather([bf(attn_w_qkv[0])], [False]), "gather_qkv")

    h0, qkv, (w_o, got) = rms_qkv(x0, attn_norm, w_qkv, rc, rs1, rs2,
                                  comm=_Both(_Gather([bf(attn_w_o[0])], [True]), _Scatter([], small_rows)))
    psmall = sum_pieces(got, "sum_small_params") * 0.5
    p_conv_norm, p_b_pw1 = psmall[0:1], psmall[1:3].reshape(1, 2 * D)
    p_b_dw, p_ln_g, p_ln_b, p_b_pw2 = psmall[3:4], psmall[4:5], psmall[5:6], psmall[6:7]
    p_w_dw = psmall[8:40]
    sink = attn_sink[0]
    o, (w_gu0,) = attn_fwd(qkv, sink, comm=_Gather([bf(ffn_w_gu[0])], [False]))
    zero_b = jnp.zeros((1, D), F32)
    x1 = mm_res(o, w_o, x0, zero_b, "attn_out")
    zero_gu = jnp.zeros((1, 2 * DFF), F32)
    h1, gu0, act0, (w_down0, w_pw1, w_pw2) = rms_mm_gate(
        x1, ffn_norm[0:1], w_gu0, zero_gu, DFF, True, BF16, "ffn0_up",
        comm=_Gather([bf(ffn_w_down[0]), bf(conv_w_pw1[0]), bf(conv_w_pw2[0])], [True, False, True]))
    x2 = mm_res(act0, w_down0, x1, zero_b, "ffn0_down")
    h2, pre, glu, _ = rms_mm_gate(x2, p_conv_norm, w_pw1, p_b_pw1, D, False, F32, "conv_pw1")
    dwc, sw, (w_gu1, w_down1) = conv_fwd(glu, p_w_dw, p_b_dw, p_ln_g, p_ln_b,
                                         comm=_Gather([bf(ffn_w_gu[1]), bf(ffn_w_down[1])], col_row))
    x3 = mm_res(sw, w_pw2, x2, p_b_pw2, "conv_pw2")
    h3, gu1, act1, _ = rms_mm_gate(x3, ffn_norm[1:2], w_gu1, zero_gu, DFF, True, BF16, "ffn1_up")
    dx4, loss_part, d_final = mm_res_loss(act1, w_down1, x3, final_norm.reshape(1, D), target)
    loss = lax.psum(loss_part[0, 0], ("x", "y", "c"))

    dgu1, _ = swiglu_bwd(dx4, w_down1, gu1, "ffn1_down_bwd")
    g_down1 = dw_row(act1, dx4, "ffn1_down_dw")
    dx3, d_ffn1, _ = mm_bt_rmsbwd(dgu1, w_gu1, x3, ffn_norm[1:2], dx4, "ffn1_up_bwd")
    g_gu1 = dw_col(h3, dgu1, "ffn1_up_dw")

    ddwc, d_ln_g, d_ln_b, d_b_pw2 = ln_silu_bwd(dx3, w_pw2, dwc, p_ln_g, p_ln_b)
    g_pw2 = dw_row(sw, dx3, "conv_pw2_dw")
    dpre, d_w_dw, d_b_dw, d_b_pw1, (r_gu1, r_down1) = conv_bwd(ddwc, glu, pre, p_w_dw,
                                                               comm=_Scatter([g_gu1, g_down1]))
    dx2, d_conv_norm, _ = mm_bt_rmsbwd(dpre, w_pw1, x2, p_conv_norm, dx3, "conv_pw1_bwd")
    g_pw1 = dw_col(h2, dpre, "conv_pw1_dw")

    dgu0, (r_pw1, r_pw2) = swiglu_bwd(dx2, w_down0, gu0, "ffn0_down_bwd", comm=_Scatter([g_pw1, g_pw2]))
    g_down0 = dw_row(act0, dx2, "ffn0_down_dw")
    dx1, d_ffn0, _ = mm_bt_rmsbwd(dgu0, w_gu0, x1, ffn_norm[0:1], dx2, "ffn0_up_bwd")
    g_gu0 = dw_col(h1, dgu0, "ffn0_up_dw")

    do = mm_bt(dx1, w_o, "attn_out_bwd")
    g_o = dw_row(o, dx1, "attn_out_dw")
    dq, dkc, dvc, d_sink, (r_gu0, r_down0, r_o) = attn_bwd(qkv, o, do, sink, rc, rs1, rs2,
                                                           comm=_Scatter([g_gu0, g_down0, g_o]))
    dqkv = kv_sum(dq, dkc, dvc, rc, rs1, rs2)[None]
    g_qkv = dw_col(h0, dqkv, "attn_qkv_dw")
    dx0, d_attn_norm, (r_qkv,) = mm_bt_rmsbwd(dqkv, w_qkv, x0, attn_norm, dx1, "attn_qkv_bwd",
                                              comm=_Scatter([g_qkv]))

    pad16 = lambda t: jnp.concatenate([t, jnp.zeros((1, D - t.shape[1]), F32)], axis=1)
    small_g = jnp.concatenate([
        d_attn_norm, pad16(d_sink), d_conv_norm, d_b_pw1.reshape(2, D), d_b_dw, d_ln_g, d_ln_b, d_b_pw2,
        d_ffn0, d_ffn1, d_final, jnp.zeros((4, D), F32), d_w_dw], axis=0)
    gf_gu, (r_small,) = sum_swap([r_gu0, r_gu1], "sum_gu", comm=_Scatter([], small_g))
    gf_down = sum_swap([r_down0, r_down1], "sum_down")
    gf_pw1, gf_pw2 = sum_swap([r_pw1], "sum_pw1"), sum_swap([r_pw2], "sum_pw2")
    gf_qkv, gf_o = sum_swap([r_qkv], "sum_qkv"), sum_swap([r_o], "sum_o")
    gs = sum_pieces(r_small, "sum_small_grads")

    def take(row0, nrows, width):
        return lax.dynamic_slice(gs, (row0, chip * width), (nrows, width))

    grads = {
        "attn_norm": gs[0:1], "attn_w_qkv": gf_qkv, "attn_w_o": gf_o, "attn_sink": gs[1:2, :N_HEADS],
        "conv_norm": take(2, 1, 256), "conv_w_pw1": gf_pw1,
        "conv_b_pw1": lax.dynamic_slice(gs[3:5].reshape(1, 2 * D), (0, chip * 512), (1, 512)),
        "conv_w_dw": take(16, 32, 256)[None, :CONV_W], "conv_b_dw": take(5, 1, 256), "conv_ln_g": take(6, 1, 256),
        "conv_ln_b": take(7, 1, 256), "conv_w_pw2": gf_pw2, "conv_b_pw2": take(8, 1, 256),
        "ffn_norm": gs[9:11], "ffn_w_gu": gf_gu, "ffn_w_down": gf_down, "final_norm": gs[11],
    }
    weights = dict(attn_norm=attn_norm, attn_w_qkv=attn_w_qkv, attn_w_o=attn_w_o, attn_sink=attn_sink,
                   conv_norm=conv_norm, conv_w_pw1=conv_w_pw1, conv_b_pw1=conv_b_pw1, conv_w_dw=conv_w_dw,
                   conv_b_dw=conv_b_dw, conv_ln_g=conv_ln_g, conv_ln_b=conv_ln_b, conv_w_pw2=conv_w_pw2,
                   conv_b_pw2=conv_b_pw2, ffn_norm=ffn_norm, ffn_w_gu=ffn_w_gu, ffn_w_down=ffn_w_down,
                   final_norm=final_norm)
    m_in = dict(attn_norm=m_attn_norm, attn_w_qkv=m_attn_w_qkv, attn_w_o=m_attn_w_o, attn_sink=m_attn_sink,
                conv_norm=m_conv_norm, conv_w_pw1=m_conv_w_pw1, conv_b_pw1=m_conv_b_pw1, conv_w_dw=m_conv_w_dw,
                conv_b_dw=m_conv_b_dw, conv_ln_g=m_conv_ln_g, conv_ln_b=m_conv_ln_b, conv_w_pw2=m_conv_w_pw2,
                conv_b_pw2=m_conv_b_pw2, ffn_norm=m_ffn_norm, ffn_w_gu=m_ffn_w_gu, ffn_w_down=m_ffn_w_down,
                final_norm=m_final_norm)
    v_in = dict(attn_norm=v_attn_norm, attn_w_qkv=v_attn_w_qkv, attn_w_o=v_attn_w_o, attn_sink=v_attn_sink,
                conv_norm=v_conv_norm, conv_w_pw1=v_conv_w_pw1, conv_b_pw1=v_conv_b_pw1, conv_w_dw=v_conv_w_dw,
                conv_b_dw=v_conv_b_dw, conv_ln_g=v_conv_ln_g, conv_ln_b=v_conv_ln_b, conv_w_pw2=v_conv_w_pw2,
                conv_b_pw2=v_conv_b_pw2, ffn_norm=v_ffn_norm, ffn_w_gu=v_ffn_w_gu, ffn_w_down=v_ffn_w_down,
                final_norm=v_final_norm)
    order = list(weights)
    g_out, d_out, m_out, v_out = [], [], [], []
    for nm in order:
        w = weights[nm]
        shape = w.shape
        as3 = lambda t: t.reshape((1,) * (3 - len(shape)) + shape) if len(shape) < 3 else t.reshape(shape)
        g3 = as3(grads[nm].reshape(shape))
        delta, nm_, nv_ = adamw(as3(w), g3, as3(m_in[nm]), as3(v_in[nm]), "adamw_" + nm)
        g_out.append(g3.reshape(shape))
        d_out.append(delta.reshape(shape))
        m_out.append(nm_.reshape(shape))
        v_out.append(nv_.reshape(shape))
    return (loss, dx0[None], *g_out, *d_out, *m_out, *v_out)
```

```python
import functools
import math

import jax
import jax.numpy as jnp
from jax import lax
from jax.experimental import pallas as pl
from jax.experimental.pallas import tpu as pltpu

F32 = jnp.float32
BF16 = jnp.bfloat16

D = 1024
N_HEADS = 16
N_KV = 4
GROUP = N_HEADS // N_KV
HD = 64
ROT = 16
THETA = 500000.0
BLK = 128
QKV = (N_HEADS + 2 * N_KV) * HD
KV_OFF = N_HEADS * HD
DFF = 2816
CONV_W = 31
CONV_PAD = 15
HALO = 16
CONV_JB = 8
CONV_JB_BWD = 8
EPS = 1e-6
NEG = -1e30
N_CHIPS = 4
N_DEV = 8
LANES = 128
SUBLANES = 8

ADAM_LR, ADAM_B1, ADAM_B2, ADAM_EPS, ADAM_WD, ADAM_STEP = 0.001, 0.9, 0.999, 1e-08, 0.01, 10

VMEM_LIMIT = 56 * 1024 * 1024
MESH = pl.DeviceIdType.MESH


def _params(*sem):
    return pltpu.CompilerParams(dimension_semantics=sem, vmem_limit_bytes=VMEM_LIMIT)


def _tile(n, want):
    if n <= want:
        return n
    for t in range(want, 7, -1):
        if n % t == 0 and t % 8 == 0:
            return t
    return n


MXU_COLS = 256


def _col_chunks(n):
    return [slice(c, min(c + MXU_COLS, n)) for c in range(0, n, MXU_COLS)]


def _sigmoid(v):
    return 1.0 / (1.0 + jnp.exp(-v))


def _rms_fwd(xv, gain):
    r = lax.rsqrt(jnp.mean(xv * xv, axis=-1, keepdims=True) + EPS)
    return xv * r * gain


def _rms_bwd(dh, xv, gain, dres):
    r = lax.rsqrt(jnp.mean(xv * xv, axis=-1, keepdims=True) + EPS)
    xhat = xv * r
    gy = dh * gain
    dx = r * (gy - xhat * jnp.mean(gy * xhat, axis=-1, keepdims=True))
    return dx + dres, dh * xhat


def _rope(blk, c, s1, s2):
    return blk * c + pltpu.roll(blk, LANES - ROT // 2, 1) * s1 + pltpu.roll(blk, ROT // 2, 1) * s2


def _dot(a, b):
    return jnp.dot(a, b, preferred_element_type=F32)


def _dot_tb(a, b):
    return lax.dot_general(a, b, (((1,), (1,)), ((), ())), preferred_element_type=F32)


def _dot_ta(a, b):
    return lax.dot_general(a, b, (((0,), (0,)), ((), ())), preferred_element_type=F32)


def rms_qkv(x, gain, w, rc, rs1, rs2, comm=None):
    T = x.shape[0]
    tm = _tile(T, 512)

    def body(x_ref, g_ref, w_ref, c_ref, s1_ref, s2_ref, h_ref, qkv_ref):
        h = _rms_fwd(x_ref[...], g_ref[...]).astype(BF16)
        h_ref[...] = h
        acc = _dot(h, w_ref[...])
        c, s1, s2 = c_ref[...], s1_ref[...], s2_ref[...]
        n_rot = (KV_OFF + N_KV * HD) // LANES
        for j in range(n_rot):
            sl = slice(LANES * j, LANES * (j + 1))
            roped = _rope(acc[:, sl], c, s1, s2)
            if j < KV_OFF // LANES:
                roped = roped * Q_SCALE
            qkv_ref[:, sl] = roped.astype(BF16)
        qkv_ref[:, n_rot * LANES:] = acc[:, n_rot * LANES:].astype(BF16)

    row = lambda i: (i, 0)
    full = lambda i: (0, 0)
    (h, qkv), got = _call(
        body, name="rms_qkv", grid=(T // tm,),
        in_specs=[pl.BlockSpec((tm, D), row), pl.BlockSpec((1, D), full), pl.BlockSpec((D, QKV), full),
                  *_tab_specs(tm)],
        out_specs=[pl.BlockSpec((tm, D), row), pl.BlockSpec((tm, QKV), row)],
        out_shape=[jax.ShapeDtypeStruct((T, D), BF16), jax.ShapeDtypeStruct((T, QKV), BF16)],
        semantics=("parallel",), args=(x, gain, w, rc, rs1, rs2), comm=comm)
    return h, qkv, got


Q_SCALE = 1.0 / math.sqrt(HD)


def _attn_mask(n, T):
    ci = lax.broadcasted_iota(jnp.int32, (3 * BLK, BLK), 0)
    qi = lax.broadcasted_iota(jnp.int32, (3 * BLK, BLK), 1)
    key_pos = n * BLK - BLK + ci
    return (jnp.abs(ci - BLK - qi) <= BLK) & (key_pos >= 0) & (key_pos < T)


def _kv_padded(kv, first_tile):
    low = lax.broadcasted_iota(jnp.int32, (3 * BLK, LANES), 1) < HD
    zero = jnp.zeros((3 * BLK, LANES), BF16)
    out = {}
    for g in range(N_KV):
        t = kv[:, (first_tile + g // 2) * LANES:(first_tile + g // 2 + 1) * LANES]
        swapped = jnp.concatenate([t[:, HD:], t[:, :HD]], axis=1)
        for p in range(2):
            out[g, p] = jnp.where(low if p == 0 else ~low, t if g % 2 == p else swapped, zero)
    return out


def _softmax_sink(s, valid, sk):
    s = jnp.where(valid, s, NEG)
    m = jnp.maximum(jnp.max(s, axis=0, keepdims=True), sk)
    e = jnp.exp(s - m)
    es = jnp.exp(sk - m)
    inv = 1.0 / (jnp.sum(e, axis=0, keepdims=True) + es)
    return e * inv, es * inv


def _attn_specs(T):
    nb = T // BLK
    kv_blk = 2 * N_KV * HD
    kv_col = KV_OFF // kv_blk
    q_spec = pl.BlockSpec((BLK, KV_OFF), lambda n: (n, 0))
    prev = pl.BlockSpec((BLK, kv_blk), lambda n: (jnp.maximum(n - 1, 0), kv_col))
    own = pl.BlockSpec((BLK, kv_blk), lambda n: (n, kv_col))
    nxt = pl.BlockSpec((BLK, kv_blk), lambda n: (jnp.minimum(n + 1, nb - 1), kv_col))
    return nb, q_spec, prev, own, nxt


def attn_fwd(qkv, sink, comm=None):
    T = qkv.shape[0]
    nb, q_spec, prev, own, nxt = _attn_specs(T)

    def body(sink_ref, q_ref, kp_ref, ko_ref, kn_ref, o_ref):
        valid = _attn_mask(pl.program_id(0), T)
        kv = jnp.concatenate([kp_ref[...], ko_ref[...], kn_ref[...]], axis=0)
        kx, vx = _kv_padded(kv, 0), _kv_padded(kv, 2)
        tile = lambda ref, h: ref[:, (h // 2) * LANES:(h // 2 + 1) * LANES]
        ss = [_dot_tb(kx[h // GROUP, h % 2], tile(q_ref, h)) for h in range(N_HEADS)]
        ps = [_softmax_sink(ss[h], valid, sink_ref[h])[0].astype(BF16) for h in range(N_HEADS)]
        vxt = {k: v.T for k, v in vx.items()}
        for j in range(N_HEADS // 2):
            g = 2 * j // GROUP
            o_t = _dot(vxt[g, 0], ps[2 * j]) + _dot(vxt[g, 1], ps[2 * j + 1])
            o_ref[:, j * LANES:(j + 1) * LANES] = o_t.T.astype(BF16)

    (o,), got = _call(
        body, name="attn_fwd", grid=(nb,),
        in_specs=[pl.BlockSpec(memory_space=pltpu.SMEM), q_spec, prev, own, nxt],
        out_specs=[pl.BlockSpec((BLK, D), lambda n: (n, 0))],
        out_shape=[jax.ShapeDtypeStruct((T, D), BF16)],
        semantics=("parallel",), args=(sink, qkv, qkv, qkv, qkv), comm=comm)
    return o, got


def mm_res(a, w, resid, bias, name):
    T, K = a.shape
    tm = _tile(T, 1024 if K <= D else 512)

    def body(a_ref, w_ref, r_ref, b_ref, o_ref):
        o_ref[...] = _dot(a_ref[...], w_ref[...]) + b_ref[...] + r_ref[...]

    row = lambda i: (i, 0)
    full = lambda i: (0, 0)
    return pl.pallas_call(
        body, name=name, grid=(T // tm,),
        in_specs=[pl.BlockSpec((tm, K), row), pl.BlockSpec((K, D), full), pl.BlockSpec((tm, D), row),
                  pl.BlockSpec((1, D), full)],
        out_specs=pl.BlockSpec((tm, D), row),
        out_shape=jax.ShapeDtypeStruct((T, D), F32),
        compiler_params=_params("parallel"),
    )(a, w, resid, bias)


def rms_mm_gate(x, gain, w, bias, H, swiglu, act_dtype, name, comm=None):
    T = x.shape[0]
    tm = _tile(T, 512)

    def body(x_ref, g_ref, w_ref, b_ref, h_ref, pre_ref, act_ref):
        h = _rms_fwd(x_ref[...], g_ref[...]).astype(BF16)
        h_ref[...] = h
        for cs in _col_chunks(H):
            cs2 = slice(H + cs.start, H + cs.stop)
            a = _dot(h, w_ref[:, cs]) + b_ref[:, cs]
            b = _dot(h, w_ref[:, cs2]) + b_ref[:, cs2]
            pre_ref[0, :, cs] = a.astype(BF16)
            pre_ref[1, :, cs] = b.astype(BF16)
            if swiglu:
                act = a * _sigmoid(a) * b
            else:
                act = a * _sigmoid(b)
            act_ref[:, cs] = act.astype(act_dtype)

    row = lambda i: (i, 0)
    full = lambda i: (0, 0)
    (h, pre, act), got = _call(
        body, name=name, grid=(T // tm,),
        in_specs=[pl.BlockSpec((tm, D), row), pl.BlockSpec((1, D), full),
                  pl.BlockSpec((D, 2 * H), full, pipeline_mode=pl.Buffered(1)), pl.BlockSpec((1, 2 * H), full)],
        out_specs=[pl.BlockSpec((tm, D), row), pl.BlockSpec((2, tm, H), lambda i: (0, i, 0)),
                   pl.BlockSpec((tm, H), row)],
        out_shape=[jax.ShapeDtypeStruct((T, D), BF16), jax.ShapeDtypeStruct((2, T, H), BF16),
                   jax.ShapeDtypeStruct((T, H), act_dtype)],
        semantics=("parallel",), args=(x, gain, w, bias), comm=comm)
    return h, pre, act, got


def _conv_tiles(T):
    tt = _tile(T, 512)
    return tt, tt // SUBLANES, D // LANES


def _fill_strided(ext, p, L):
    main = p[HALO:HALO + SUBLANES * L, :].reshape(SUBLANES, L, LANES)
    ext[CONV_PAD:CONV_PAD + L] = jnp.swapaxes(main, 0, 1)

    def ibody(i, carry):
        ext[i] = p[pl.ds(i + 1, SUBLANES, stride=L), :]
        ext[i + CONV_PAD + L] = p[pl.ds(i + CONV_PAD + L + 1, SUBLANES, stride=L), :]
        return carry

    lax.fori_loop(0, CONV_PAD, ibody, 0, unroll=3)


def _conv_specs(T, tt):
    main = pl.BlockSpec((tt, D), lambda i: (i, 0))
    per = tt // HALO
    prev = pl.BlockSpec((HALO, D), lambda i: (jnp.maximum(i * per - 1, 0), 0))
    nxt = pl.BlockSpec((HALO, D), lambda i: (jnp.minimum((i + 1) * per, T // HALO - 1), 0))
    return main, prev, nxt


def _fill_pad(pad, main_ref, prev_ref, next_ref, i, n_i, tt, nlt):
    keep_p = (i > 0).astype(F32)
    keep_n = (i < n_i - 1).astype(F32)
    for lt in range(nlt):
        sl = slice(lt * LANES, (lt + 1) * LANES)
        pad[lt, 0:HALO, :] = prev_ref[:, sl] * keep_p
        pad[lt, HALO:HALO + tt, :] = main_ref[:, sl]
        pad[lt, HALO + tt:2 * HALO + tt, :] = next_ref[:, sl] * keep_n


def conv_fwd(glu, w_dw, b_dw, ln_g, ln_b, comm=None):
    T = glu.shape[0]
    tt, L, nlt = _conv_tiles(T)
    n_i = T // tt
    main, prev, nxt = _conv_specs(T, tt)

    def body(x_ref, xp_ref, xn_ref, w_ref, b_ref, g_ref, bb_ref, dwc_ref, sw_ref, pad, ob, ext, wk):
        i = pl.program_id(0)
        _fill_pad(pad, x_ref, xp_ref, xn_ref, i, n_i, tt, nlt)
        for lt in range(nlt):
            sl = slice(lt * LANES, (lt + 1) * LANES)
            o = ob.at[lt]
            _fill_strided(ext, pad.at[lt], L)
            for k in range(CONV_W):
                wk[k] = jnp.broadcast_to(w_ref[k:k + 1, sl], (SUBLANES, LANES))

            def jbody(jb, carry):
                j = jb * CONV_JB
                accs = [None] * CONV_JB
                for m in range(CONV_W + CONV_JB - 1):
                    e = ext[j + m]
                    for u in range(CONV_JB):
                        if 0 <= m - u < CONV_W:
                            t = e * wk[m - u]
                            accs[u] = t if accs[u] is None else accs[u] + t
                for u in range(CONV_JB):
                    o[pl.ds(j + u, SUBLANES, stride=L), :] = accs[u]
                return carry

            lax.fori_loop(0, L // CONV_JB, jbody, 0)
        y = jnp.concatenate([ob[lt] for lt in range(nlt)], axis=1) + b_ref[...]
        dwc_ref[...] = y
        mu = jnp.mean(y, axis=-1, keepdims=True)
        yc = y - mu
        var = jnp.mean(yc * yc, axis=-1, keepdims=True)
        z = yc * lax.rsqrt(var + EPS) * g_ref[...] + bb_ref[...]
        sw_ref[...] = (z * _sigmoid(z)).astype(BF16)

    full = lambda i: (0, 0)
    (dwc, sw), got = _call(
        body, name="conv_fwd", grid=(n_i,),
        in_specs=[main, prev, nxt, pl.BlockSpec((32, D), full), pl.BlockSpec((1, D), full),
                  pl.BlockSpec((1, D), full), pl.BlockSpec((1, D), full)],
        out_specs=[pl.BlockSpec((tt, D), lambda i: (i, 0)), pl.BlockSpec((tt, D), lambda i: (i, 0))],
        out_shape=[jax.ShapeDtypeStruct((T, D), F32), jax.ShapeDtypeStruct((T, D), BF16)],
        scratch_shapes=[pltpu.VMEM((nlt, tt + 2 * HALO, LANES), F32), pltpu.VMEM((nlt, tt, LANES), F32),
                        pltpu.VMEM((L + 2 * HALO, SUBLANES, LANES), F32), pltpu.VMEM((32, SUBLANES, LANES), F32)],
        semantics=("parallel",), args=(glu, glu, glu, w_dw, b_dw, ln_g, ln_b), comm=comm)
    return dwc, sw, got


def mm_res_loss(a, w, resid, gain, target):
    T, K = a.shape
    tm = _tile(T, 512)

    def body(a_ref, w_ref, r_ref, g_ref, t_ref, dx_ref, loss_ref, dg_ref):
        @pl.when(pl.program_id(0) == 0)
        def _():
            loss_ref[...] = jnp.zeros_like(loss_ref)
            dg_ref[...] = jnp.zeros_like(dg_ref)

        xv, gain_v = _dot(a_ref[...], w_ref[...]) + r_ref[...], g_ref[...]
        err = _rms_fwd(xv, gain_v) - t_ref[...]
        part = 0.5 * jnp.sum(jnp.mean(err * err, axis=-1, keepdims=True), axis=0, keepdims=True)
        loss_ref[...] += jnp.broadcast_to(part, loss_ref.shape)
        dx, dgr = _rms_bwd(err * (1.0 / D), xv, gain_v, 0.0)
        dx_ref[...] = dx
        dg_ref[...] += jnp.sum(dgr, axis=0, keepdims=True)

    row = lambda i: (i, 0)
    full = lambda i: (0, 0)
    return pl.pallas_call(
        body, name="ffn1_down_loss", grid=(T // tm,),
        in_specs=[pl.BlockSpec((tm, K), row), pl.BlockSpec((K, D), full), pl.BlockSpec((tm, D), row),
                  pl.BlockSpec((1, D), full), pl.BlockSpec((tm, D), row)],
        out_specs=[pl.BlockSpec((tm, D), row), pl.BlockSpec((1, LANES), full), pl.BlockSpec((1, D), full)],
        out_shape=[jax.ShapeDtypeStruct((T, D), F32), jax.ShapeDtypeStruct((1, LANES), F32),
                   jax.ShapeDtypeStruct((1, D), F32)],
        compiler_params=_params("arbitrary"),
    )(a, w, resid, gain, target)


def swiglu_bwd(dx, w_down, pre, name, comm=None):
    T = dx.shape[0]
    H = w_down.shape[0]
    tm = _tile(T, 512)

    def body(dx_ref, w_ref, pre_ref, dpre_ref):
        dxb = dx_ref[...].astype(BF16)
        for cs in _col_chunks(H):
            dact = _dot_tb(dxb, w_ref[cs, :])
            g = pre_ref[0, :, cs].astype(F32)
            u = pre_ref[1, :, cs].astype(F32)
            sg = _sigmoid(g)
            dpre_ref[0, :, cs] = (dact * u * sg * (1.0 + g * (1.0 - sg))).astype(BF16)
            dpre_ref[1, :, cs] = (dact * g * sg).astype(BF16)

    (dpre,), got = _call(
        body, name=name, grid=(T // tm,),
        in_specs=[pl.BlockSpec((tm, D), lambda i: (i, 0)),
                  pl.BlockSpec((H, D), lambda i: (0, 0), pipeline_mode=pl.Buffered(1)),
                  pl.BlockSpec((2, tm, H), lambda i: (0, i, 0))],
        out_specs=[pl.BlockSpec((2, tm, H), lambda i: (0, i, 0))],
        out_shape=[jax.ShapeDtypeStruct((2, T, H), BF16)],
        semantics=("parallel",), args=(dx, w_down, pre), comm=comm)
    return dpre, got


def mm_bt_rmsbwd(dpre, w, x, gain, dres, name, comm=None):
    nh, T, H = dpre.shape
    tm = _tile(T, 512)

    def body(dp_ref, w_ref, x_ref, g_ref, dres_ref, dx_ref, dg_ref):
        @pl.when(pl.program_id(0) == 0)
        def _():
            dg_ref[...] = jnp.zeros_like(dg_ref)

        dh = _dot_tb(dp_ref[0], w_ref[:, 0:H])
        for hf in range(1, nh):
            dh = dh + _dot_tb(dp_ref[hf], w_ref[:, hf * H:(hf + 1) * H])
        dx, dgr = _rms_bwd(dh, x_ref[...], g_ref[...], dres_ref[...])
        dx_ref[...] = dx
        dg_ref[...] += jnp.sum(dgr, axis=0, keepdims=True)

    row = lambda i: (i, 0)
    full = lambda i: (0, 0)
    (dx, dg), got = _call(
        body, name=name, grid=(T // tm,),
        in_specs=[pl.BlockSpec((nh, tm, H), lambda i: (0, i, 0)),
                  pl.BlockSpec((D, nh * H), full, pipeline_mode=pl.Buffered(1)),
                  pl.BlockSpec((tm, D), row), pl.BlockSpec((1, D), full), pl.BlockSpec((tm, D), row)],
        out_specs=[pl.BlockSpec((tm, D), row), pl.BlockSpec((1, D), full)],
        out_shape=[jax.ShapeDtypeStruct((T, D), F32), jax.ShapeDtypeStruct((1, D), F32)],
        semantics=("arbitrary",), args=(dpre, w, x, gain, dres), comm=comm)
    return dx, dg, got


def dw_col(a, dpre, name):
    T = a.shape[0]
    nh, _, H = dpre.shape
    per = nh * H // N_CHIPS
    bph = N_CHIPS // nh
    tt = _tile(T, 2048)
    nt = T // tt

    def body(a_ref, b_ref, o_ref, acc):
        t = pl.program_id(1)

        @pl.when(t == 0)
        def _():
            acc[...] = jnp.zeros_like(acc)

        acc[...] += _dot_ta(a_ref[...], b_ref[...])

        @pl.when(t == nt - 1)
        def _():
            o_ref[...] = acc[...].astype(BF16)

    return pl.pallas_call(
        body, name=name, grid=(N_CHIPS, nt),
        in_specs=[pl.BlockSpec((tt, D), lambda q, t: (t, 0)),
                  pl.BlockSpec((None, tt, per), lambda q, t: (q // bph, t, q % bph))],
        out_specs=pl.BlockSpec((None, D, per), lambda q, t: (q, 0, 0)),
        out_shape=jax.ShapeDtypeStruct((N_CHIPS, D, per), BF16),
        scratch_shapes=[pltpu.VMEM((D, per), F32)],
        compiler_params=_params("parallel", "arbitrary"),
    )(a, dpre)


def dw_row(a, b, name):
    T, R = a.shape
    cw = 1408 if R % 1408 == 0 else R
    tt = _tile(T, 1024)
    nt = T // tt

    def body(a_ref, b_ref, o_ref, acc):
        t = pl.program_id(1)

        @pl.when(t == 0)
        def _():
            acc[...] = jnp.zeros_like(acc)

        acc[...] += _dot_ta(a_ref[...], b_ref[...].astype(BF16))

        @pl.when(t == nt - 1)
        def _():
            o_ref[...] = acc[...].astype(BF16)

    out = pl.pallas_call(
        body, name=name, grid=(R // cw, nt),
        in_specs=[pl.BlockSpec((tt, cw), lambda q, t: (t, q)), pl.BlockSpec((tt, D), lambda q, t: (t, 0))],
        out_specs=pl.BlockSpec((cw, D), lambda q, t: (q, 0)),
        out_shape=jax.ShapeDtypeStruct((R, D), BF16),
        scratch_shapes=[pltpu.VMEM((cw, D), F32)],
        compiler_params=_params("parallel", "arbitrary"),
    )(a, b)
    return out.reshape(N_CHIPS, R // N_CHIPS, D)


def ln_silu_bwd(dx, w_pw2, dwc, ln_g, ln_b):
    T = dx.shape[0]
    tm = _tile(T, 512)

    def body(dx_ref, w_ref, y_ref, g_ref, b_ref, dy_ref, dg_ref, db_ref, dbo_ref):
        @pl.when(pl.program_id(0) == 0)
        def _():
            dg_ref[...] = jnp.zeros_like(dg_ref)
            db_ref[...] = jnp.zeros_like(db_ref)
            dbo_ref[...] = jnp.zeros_like(dbo_ref)

        dxv = dx_ref[...]
        dsw = _dot_tb(dxv.astype(BF16), w_ref[...])
        y = y_ref[...]
        mu = jnp.mean(y, axis=-1, keepdims=True)
        yc = y - mu
        rstd = lax.rsqrt(jnp.mean(yc * yc, axis=-1, keepdims=True) + EPS)
        xhat = yc * rstd
        z = xhat * g_ref[...] + b_ref[...]
        sg = _sigmoid(z)
        dz = dsw * sg * (1.0 + z * (1.0 - sg))
        dxh = dz * g_ref[...]
        dy_ref[...] = rstd * (dxh - jnp.mean(dxh, axis=-1, keepdims=True)
                              - xhat * jnp.mean(dxh * xhat, axis=-1, keepdims=True))
        dg_ref[...] += jnp.sum(dz * xhat, axis=0, keepdims=True)
        db_ref[...] += jnp.sum(dz, axis=0, keepdims=True)
        dbo_ref[...] += jnp.sum(dxv, axis=0, keepdims=True)

    row = lambda i: (i, 0)
    full = lambda i: (0, 0)
    vec = pl.BlockSpec((1, D), full)
    return pl.pallas_call(
        body, name="ln_silu_bwd", grid=(T // tm,),
        in_specs=[pl.BlockSpec((tm, D), row), pl.BlockSpec((D, D), full), pl.BlockSpec((tm, D), row), vec, vec],
        out_specs=[pl.BlockSpec((tm, D), row), vec, vec, vec],
        out_shape=[jax.ShapeDtypeStruct((T, D), F32)] + [jax.ShapeDtypeStruct((1, D), F32)] * 3,
        compiler_params=_params("arbitrary"),
    )(dx, w_pw2, dwc, ln_g, ln_b)


def conv_bwd(ddwc, glu, pre, w_dw, comm=None):
    T = ddwc.shape[0]
    tt, L, nlt = _conv_tiles(T)
    n_i = T // tt
    main, prev, nxt = _conv_specs(T, tt)

    def body(d_ref, dp_ref, dn_ref, x_ref, xp_ref, xn_ref, pre_ref, w_ref,
             dpre_ref, dw_ref, dbd_ref, dbp_ref, padd, padx, ob, extd, extx, wk):
        i = pl.program_id(0)

        @pl.when(i == 0)
        def _():
            dw_ref[...] = jnp.zeros_like(dw_ref)
            dbd_ref[...] = jnp.zeros_like(dbd_ref)
            dbp_ref[...] = jnp.zeros_like(dbp_ref)

        _fill_pad(padd, d_ref, dp_ref, dn_ref, i, n_i, tt, nlt)
        _fill_pad(padx, x_ref, xp_ref, xn_ref, i, n_i, tt, nlt)
        for lt in range(nlt):
            sl = slice(lt * LANES, (lt + 1) * LANES)
            o = ob.at[lt]
            _fill_strided(extd, padd.at[lt], L)
            _fill_strided(extx, padx.at[lt], L)
            for k in range(CONV_W):
                wk[k] = jnp.broadcast_to(w_ref[k:k + 1, sl], (SUBLANES, LANES))

            nu = CONV_JB_BWD

            def jbody(jb, accs):
                j = jb * nu
                accs = list(accs)
                d = [extd[j + u + CONV_PAD] for u in range(nu)]
                g = [None] * nu
                for m in range(CONV_W + nu - 1):
                    ed = extd[j + 2 * CONV_PAD + nu - 1 - m]
                    ex = extx[j + m]
                    for u in range(nu):
                        k = m - (nu - 1 - u)
                        if 0 <= k < CONV_W:
                            t = ed * wk[k]
                            g[u] = t if g[u] is None else g[u] + t
                        k = m - u
                        if 0 <= k < CONV_W:
                            accs[k] = accs[k] + d[u] * ex
                for u in range(nu):
                    o[pl.ds(j + u, SUBLANES, stride=L), :] = g[u]
                return tuple(accs)

            accs = lax.fori_loop(0, L // nu, jbody, tuple(jnp.zeros((SUBLANES, LANES), F32) for _ in range(CONV_W)))
            for k in range(CONV_W):
                dw_ref[k:k + 1, sl] += jnp.sum(accs[k], axis=0, keepdims=True)
        dglu = jnp.concatenate([ob[lt] for lt in range(nlt)], axis=1)
        a = pre_ref[0].astype(F32)
        gate = pre_ref[1].astype(F32)
        sg = _sigmoid(gate)
        da = dglu * sg
        dgate = dglu * a * sg * (1.0 - sg)
        dpre_ref[0] = da.astype(BF16)
        dpre_ref[1] = dgate.astype(BF16)
        dbd_ref[...] += jnp.sum(d_ref[...], axis=0, keepdims=True)
        dbp_ref[0] += jnp.sum(da, axis=0, keepdims=True)
        dbp_ref[1] += jnp.sum(dgate, axis=0, keepdims=True)

    full = lambda i: (0, 0)
    (dpre, dw, dbd, dbp), got = _call(
        body, name="conv_bwd", grid=(n_i,),
        in_specs=[main, prev, nxt, main, prev, nxt, pl.BlockSpec((2, tt, D), lambda i: (0, i, 0)),
                  pl.BlockSpec((32, D), full)],
        out_specs=[pl.BlockSpec((2, tt, D), lambda i: (0, i, 0)), pl.BlockSpec((32, D), full),
                   pl.BlockSpec((1, D), full), pl.BlockSpec((2, 1, D), lambda i: (0, 0, 0))],
        out_shape=[jax.ShapeDtypeStruct((2, T, D), BF16), jax.ShapeDtypeStruct((32, D), F32),
                   jax.ShapeDtypeStruct((1, D), F32), jax.ShapeDtypeStruct((2, 1, D), F32)],
        scratch_shapes=[pltpu.VMEM((nlt, tt + 2 * HALO, LANES), F32), pltpu.VMEM((nlt, tt + 2 * HALO, LANES), F32),
                        pltpu.VMEM((nlt, tt, LANES), F32), pltpu.VMEM((L + 2 * HALO, SUBLANES, LANES), F32),
                        pltpu.VMEM((L + 2 * HALO, SUBLANES, LANES), F32), pltpu.VMEM((32, SUBLANES, LANES), F32)],
        semantics=("arbitrary",), args=(ddwc, ddwc, ddwc, glu, glu, glu, pre, w_dw), comm=comm)
    return dpre, dw, dbd, dbp, got


def mm_bt(a, w, name):
    T = a.shape[0]
    N = w.shape[0]
    tm = _tile(T, 1024)

    def body(a_ref, w_ref, o_ref):
        o_ref[...] = _dot_tb(a_ref[...].astype(BF16), w_ref[...]).astype(BF16)

    return pl.pallas_call(
        body, name=name, grid=(T // tm,),
        in_specs=[pl.BlockSpec((tm, D), lambda i: (i, 0)), pl.BlockSpec((N, D), lambda i: (0, 0))],
        out_specs=pl.BlockSpec((tm, N), lambda i: (i, 0)),
        out_shape=jax.ShapeDtypeStruct((T, N), BF16),
        compiler_params=_params("parallel"),
    )(a, w)


def attn_bwd(qkv, o, do, sink, rc, rs1, rs2, comm=None):
    T = qkv.shape[0]
    nb, q_spec, prev, own, nxt = _attn_specs(T)
    kvw = N_KV * HD

    def body(sink_ref, q_ref, kp_ref, ko_ref, kn_ref, o_ref, do_ref, c_ref, s1_ref, s2_ref,
             dq_ref, dkc_ref, dvc_ref, dsink_ref):
        n = pl.program_id(0)

        @pl.when(n == 0)
        def _():
            dsink_ref[...] = jnp.zeros_like(dsink_ref)

        valid = _attn_mask(n, T)
        kv = jnp.concatenate([kp_ref[...], ko_ref[...], kn_ref[...]], axis=0)
        kx, vx = _kv_padded(kv, 0), _kv_padded(kv, 2)
        tile = lambda ref, j: ref[:, j * LANES:(j + 1) * LANES]
        ss = [_dot_tb(kx[h // GROUP, h % 2], tile(q_ref, h // 2)) for h in range(N_HEADS)]
        dps = [_dot_tb(vx[h // GROUP, h % 2], tile(do_ref, h // 2)) for h in range(N_HEADS)]
        low_d = lax.broadcasted_iota(jnp.int32, (LANES, BLK), 0) < HD
        deltas = []
        for j in range(N_HEADS // 2):
            prod_t = tile(do_ref, j).astype(F32).T * tile(o_ref, j).astype(F32).T
            deltas.append(jnp.sum(jnp.where(low_d, prod_t, 0.0), axis=0, keepdims=True))
            deltas.append(jnp.sum(jnp.where(low_d, 0.0, prod_t), axis=0, keepdims=True))
        lane = lax.broadcasted_iota(jnp.int32, (1, N_HEADS), 1)
        dsink = jnp.zeros((1, N_HEADS), F32)
        pbs, dss = [], []
        for h in range(N_HEADS):
            p, p_sink = _softmax_sink(ss[h], valid, sink_ref[h])
            dss.append((p * (dps[h] - deltas[h])).astype(BF16))
            pbs.append(p.astype(BF16))
            part = -jnp.sum(p_sink * deltas[h], axis=1, keepdims=True)
            dsink = dsink + jnp.where(lane == h, part, 0.0)
        dsink_ref[...] += dsink
        c, s1, s2 = c_ref[...], s1_ref[...], s2_ref[...]
        kxt = {k: v.T for k, v in kx.items()}
        for j in range(N_HEADS // 2):
            g = 2 * j // GROUP
            dq_t = _dot(kxt[g, 0], dss[2 * j]) + _dot(kxt[g, 1], dss[2 * j + 1])
            dq_ref[:, j * LANES:(j + 1) * LANES] = (_rope(dq_t.T, c, -s1, -s2) * Q_SCALE).astype(BF16)
        low_k = lax.broadcasted_iota(jnp.int32, (3 * BLK, LANES), 1) < HD
        cols = lambda xs, g, p: jnp.concatenate([xs[GROUP * g + p], xs[GROUP * g + 2 + p]], axis=1)
        for t in range(N_KV // 2):
            sums = {}
            for g in (2 * t, 2 * t + 1):
                q2 = jnp.concatenate([tile(q_ref, 2 * g), tile(q_ref, 2 * g + 1)], axis=0)
                do2 = jnp.concatenate([tile(do_ref, 2 * g), tile(do_ref, 2 * g + 1)], axis=0)
                for p in range(2):
                    sums[g, p] = (_dot(cols(dss, g, p), q2), _dot(cols(pbs, g, p), do2))
            for which, ref in ((0, dkc_ref), (1, dvc_ref)):
                keep = jnp.where(low_k, sums[2 * t, 0][which], sums[2 * t + 1, 1][which])
                swap = jnp.where(low_k, sums[2 * t + 1, 0][which], sums[2 * t, 1][which])
                ref[:, t * LANES:(t + 1) * LANES] = keep + pltpu.roll(swap, HD, 1)

    row = lambda n: (n, 0)
    (dq, dkc, dvc, dsink), got = _call(
        body, name="attn_bwd", grid=(nb,),
        in_specs=[pl.BlockSpec(memory_space=pltpu.SMEM), q_spec, prev, own, nxt,
                  pl.BlockSpec((BLK, D), row), pl.BlockSpec((BLK, D), row), *_tab_specs(BLK)],
        out_specs=[pl.BlockSpec((BLK, D), row), pl.BlockSpec((None, 3 * BLK, kvw), lambda n: (n, 0, 0)),
                   pl.BlockSpec((None, 3 * BLK, kvw), lambda n: (n, 0, 0)), pl.BlockSpec((1, N_HEADS), lambda n: (0, 0))],
        out_shape=[jax.ShapeDtypeStruct((T, QKV), BF16), jax.ShapeDtypeStruct((nb, 3 * BLK, kvw), F32),
                   jax.ShapeDtypeStruct((nb, 3 * BLK, kvw), F32), jax.ShapeDtypeStruct((1, N_HEADS), F32)],
        semantics=("arbitrary",), args=(sink, qkv, qkv, qkv, qkv, o, do, rc, rs1, rs2), comm=comm)
    return dq, dkc, dvc, dsink, got


def kv_sum(dqkv, dkc, dvc, rc, rs1, rs2):
    nb = dkc.shape[0]
    T = nb * BLK
    kvw = N_KV * HD

    G = 4
    ng = nb // G

    def gather3(own_ref, prev_ref, before_ref, next_ref, after_ref, m):
        has_before = (m > 0).astype(F32)
        has_after = (m < ng - 1).astype(F32)
        out = []
        for i in range(G):
            from_prev = prev_ref[i - 1] if i > 0 else before_ref[0] * has_before
            from_next = next_ref[i + 1] if i < G - 1 else after_ref[0] * has_after
            out.append(from_prev + own_ref[i] + from_next)
        return jnp.concatenate(out, axis=0)

    def body(_, ko, kp, kb, kn, ka, vo, vp, vb, vn, va, c_ref, s1_ref, s2_ref, out_ref):
        m = pl.program_id(0)
        dk = gather3(ko, kp, kb, kn, ka, m)
        dv = gather3(vo, vp, vb, vn, va, m)
        c, s1, s2 = c_ref[...], s1_ref[...], s2_ref[...]
        for j in range(kvw // LANES):
            sl = slice(LANES * j, LANES * (j + 1))
            out_ref[:, sl] = _rope(dk[:, sl], c, -s1, -s2).astype(BF16)
        out_ref[:, kvw:] = dv.astype(BF16)

    own = pl.BlockSpec((G, BLK, kvw), lambda m: (m, 1, 0))
    prev = pl.BlockSpec((G, BLK, kvw), lambda m: (m, 2, 0))
    before = pl.BlockSpec((1, BLK, kvw), lambda m: (jnp.maximum(G * m - 1, 0), 2, 0))
    nxt = pl.BlockSpec((G, BLK, kvw), lambda m: (m, 0, 0))
    after = pl.BlockSpec((1, BLK, kvw), lambda m: (jnp.minimum(G * m + G, nb - 1), 0, 0))
    five = [own, prev, before, nxt, after]
    return pl.pallas_call(
        body, name="kv_sum", grid=(ng,),
        in_specs=[pl.BlockSpec(memory_space=pl.ANY), *five, *five, *_tab_specs(G * BLK)],
        out_specs=pl.BlockSpec((G * BLK, 2 * kvw), lambda m: (m, KV_OFF // (2 * kvw))),
        out_shape=jax.ShapeDtypeStruct((T, QKV), BF16),
        input_output_aliases={0: 0},
        compiler_params=_params("parallel"),
    )(dqkv, *([dkc] * 5), *([dvc] * 5), rc, rs1, rs2)


def _me():
    return lax.axis_index("x"), lax.axis_index("y"), lax.axis_index("c")


def _half_rows(ref, sharded_rows, chip, core):
    R, C = ref.shape[-2], ref.shape[-1]
    lead = (slice(None),) * (len(ref.shape) - 2)
    if sharded_rows:
        per = R // N_CHIPS
        return ref.at[lead + (pl.ds(chip * per + core * (per // 2), per // 2), slice(None))]
    per = C // N_CHIPS
    return ref.at[lead + (pl.ds(core * (R // 2), R // 2), pl.ds(chip * per, per))]


class _Gather:
    def __init__(self, shards, sharded_rows):
        self.inputs = list(shards)
        self.rows = list(sharded_rows)
        self.n = self.n_in = self.n_out = len(shards)
        self.out_shapes = []
        for s, rows in zip(shards, sharded_rows):
            shp = list(s.shape)
            shp[-2 if rows else -1] *= N_CHIPS
            self.out_shapes.append(jax.ShapeDtypeStruct(tuple(shp), s.dtype))
        self.scratch = [pltpu.SemaphoreType.DMA((self.n, 6)), pltpu.SemaphoreType.DMA((self.n, 6)),
                        pltpu.SemaphoreType.DMA((self.n, 2))]

    def _ctx(self, ins, outs, sems):
        send_sems, recv_sems, local_sems = sems
        x, y, c = _me()
        chips = [(1 - x, y), (x, 1 - y), (1 - x, 1 - y)]

        def half_src(w, core):
            s = ins[w]
            R = s.shape[-2]
            return s.at[pl.ds(core * (R // 2), R // 2), :]

        def dst(w, chip, core):
            return _half_rows(outs[w], self.rows[w], chip, core)

        def copy(w, k, src, chip, core, to):
            return pltpu.make_async_remote_copy(
                src_ref=src, dst_ref=dst(w, chip, core), send_sem=send_sems.at[w, k], recv_sem=recv_sems.at[w, k],
                device_id=to, device_id_type=MESH)

        def local(w, core):
            return pltpu.make_async_copy(half_src(w, core), dst(w, 2 * x + y, core), local_sems.at[w, core])

        def first(w, j):
            qx, qy = chips[j]
            return copy(w, j, half_src(w, c), 2 * x + y, c, (qx, qy, c))

        def landed(w, j):
            qx, qy = chips[j]
            return copy(w, j, dst(w, 2 * qx + qy, c), 2 * qx + qy, c, (x, y, c))

        def passed(w, j):
            qx, qy = chips[j]
            return copy(w, 3 + j, dst(w, 2 * qx + qy, c), 2 * qx + qy, c, (x, y, 1 - c))

        def from_sibling(w, j):
            qx, qy = chips[j]
            return copy(w, 3 + j, dst(w, 2 * qx + qy, 1 - c), 2 * qx + qy, 1 - c, (x, y, c))

        return local, first, landed, passed, from_sibling

    def start(self, ins, outs, sems):
        local, first, _, _, _ = self._ctx(ins, outs, sems)
        for w in range(self.n):
            for core in range(2):
                local(w, core).start()
            for j in range(3):
                first(w, j).start()

    def mid(self, ins, outs, sems):
        _, _, landed, passed, _ = self._ctx(ins, outs, sems)
        for w in range(self.n):
            for j in range(3):
                landed(w, j).wait_recv()
                passed(w, j).start()

    def end(self, ins, outs, sems):
        local, first, _, passed, from_sibling = self._ctx(ins, outs, sems)
        for w in range(self.n):
            for j in range(3):
                from_sibling(w, j).wait_recv()
        for w in range(self.n):
            for j in range(3):
                first(w, j).wait_send()
                passed(w, j).wait_send()
            for core in range(2):
                local(w, core).wait()


class _Scatter:
    def __init__(self, grads, small=None):
        self.inputs = list(grads) + ([small] if small is not None else [])
        self.ng = len(grads)
        self.n = self.n_in = self.n_out = len(self.inputs)
        self.out_shapes = [jax.ShapeDtypeStruct((N_DEV, g.shape[1] // 2, g.shape[2]), g.dtype) for g in grads]
        if small is not None:
            self.out_shapes.append(jax.ShapeDtypeStruct((N_DEV,) + small.shape, small.dtype))
        self.scratch = [pltpu.SemaphoreType.DMA((self.n, N_DEV)), pltpu.SemaphoreType.DMA((self.n, N_DEV)),
                        pltpu.SemaphoreType.DMA((self.n,))]

    def _ctx(self, ins, outs, sems):
        send_sems, recv_sems, local_sems = sems
        x, y, c = _me()
        me = 4 * x + 2 * y + c

        def piece(w, chip, core):
            if w >= self.ng:
                return ins[w]
            half = ins[w].shape[1] // 2
            return ins[w].at[chip, pl.ds(core * half, half), :]

        def peer_of(k):
            return x ^ ((k >> 2) & 1), y ^ ((k >> 1) & 1), c ^ (k & 1)

        def local(w):
            return pltpu.make_async_copy(piece(w, 2 * x + y, c), outs[w].at[me], local_sems.at[w])

        def send(w, k):
            px, py, pc = peer_of(k)
            return pltpu.make_async_remote_copy(
                src_ref=piece(w, 2 * px + py, pc), dst_ref=outs[w].at[me], send_sem=send_sems.at[w, k],
                recv_sem=recv_sems.at[w, k], device_id=(px, py, pc), device_id_type=MESH)

        def recv(w, k):
            px, py, pc = peer_of(k)
            return pltpu.make_async_remote_copy(
                src_ref=piece(w, 2 * x + y, c), dst_ref=outs[w].at[4 * px + 2 * py + pc], send_sem=send_sems.at[w, k],
                recv_sem=recv_sems.at[w, k], device_id=(px, py, pc), device_id_type=MESH)

        return local, send, recv

    def start(self, ins, outs, sems):
        local, send, _ = self._ctx(ins, outs, sems)
        for w in range(self.n):
            local(w).start()
            for k in range(1, N_DEV):
                send(w, k).start()

    def mid(self, ins, outs, sems):
        pass

    def end(self, ins, outs, sems):
        local, send, recv = self._ctx(ins, outs, sems)
        for w in range(self.n):
            for k in range(1, N_DEV):
                recv(w, k).wait_recv()
        for w in range(self.n):
            for k in range(1, N_DEV):
                send(w, k).wait_send()
            local(w).wait()


class _Both:
    def __init__(self, a, b):
        self.a, self.b = a, b
        self.inputs = a.inputs + b.inputs
        self.out_shapes = a.out_shapes + b.out_shapes
        self.scratch = a.scratch + b.scratch
        self.n_in, self.n_out = a.n_in + b.n_in, a.n_out + b.n_out

    def _split(self, ins, outs, sems):
        a, na = self.a, len(self.a.scratch)
        return (ins[:a.n_in], outs[:a.n_out], sems[:na]), (ins[a.n_in:], outs[a.n_out:], sems[na:])

    def start(self, ins, outs, sems):
        pa, pb = self._split(ins, outs, sems)
        self.a.start(*pa)
        self.b.start(*pb)

    def mid(self, ins, outs, sems):
        pa, pb = self._split(ins, outs, sems)
        self.a.mid(*pa)
        self.b.mid(*pb)

    def end(self, ins, outs, sems):
        pa, pb = self._split(ins, outs, sems)
        self.a.end(*pa)
        self.b.end(*pb)


def exchange(plan, name):
    def body(*refs):
        ins, outs, sems = refs[:plan.n_in], refs[plan.n_in:plan.n_in + plan.n_out], refs[plan.n_in + plan.n_out:]
        plan.start(ins, outs, sems)
        plan.mid(ins, outs, sems)
        plan.end(ins, outs, sems)

    any_spec = pl.BlockSpec(memory_space=pl.ANY)
    return pl.pallas_call(
        body, name=name, in_specs=[any_spec] * plan.n_in, out_specs=[any_spec] * plan.n_out,
        out_shape=plan.out_shapes, scratch_shapes=plan.scratch,
    )(*plan.inputs)


def _call(body, *, name, grid, in_specs, out_specs, out_shape, scratch_shapes=(), semantics, args, comm=None):
    if comm is None:
        outs = pl.pallas_call(
            body, name=name, grid=grid, in_specs=in_specs, out_specs=out_specs, out_shape=out_shape,
            scratch_shapes=list(scratch_shapes), compiler_params=_params(*semantics))(*args)
        return outs, []
    n_in, n_out, n_scr = len(in_specs), len(out_specs), len(scratch_shapes)

    total = math.prod(grid)
    first, middle, last = 0, (3 * total) // 4 - 1, total - 1
    assert first <= middle < last

    def at(step):
        lin = pl.program_id(0)
        for d in range(1, len(grid)):
            lin = lin * grid[d] + pl.program_id(d)
        return lin == step

    def hosted(*refs):
        h_in, c_in = refs[:n_in], refs[n_in:n_in + comm.n_in]
        rest = refs[n_in + comm.n_in:]
        h_out, c_out = rest[:n_out], rest[n_out:n_out + comm.n_out]
        rest = rest[n_out + comm.n_out:]
        h_scr, c_scr = rest[:n_scr], rest[n_scr:]

        @pl.when(at(first))
        def _():
            comm.start(c_in, c_out, c_scr)

        body(*h_in, *h_out, *h_scr)

        @pl.when(at(middle))
        def _():
            comm.mid(c_in, c_out, c_scr)

        @pl.when(at(last))
        def _():
            comm.end(c_in, c_out, c_scr)

    any_spec = pl.BlockSpec(memory_space=pl.ANY)
    outs = pl.pallas_call(
        hosted, name=name, grid=grid, in_specs=list(in_specs) + [any_spec] * comm.n_in,
        out_specs=list(out_specs) + [any_spec] * comm.n_out, out_shape=list(out_shape) + comm.out_shapes,
        scratch_shapes=list(scratch_shapes) + comm.scratch,
        compiler_params=_params(*(["arbitrary"] * len(grid))))(*args, *comm.inputs)
    return outs[:n_out], outs[n_out:]


def sum_swap(pieces, name, comm=None):
    nl = len(pieces)
    _, r2, cc = pieces[0].shape
    tr = 128 if r2 % 128 == 0 else r2 // 2
    n = r2 // tr

    def body(*refs):
        p_refs, out = refs[:nl], refs[nl]
        slots, send_sems, local_sems, recv_sem = refs[nl + 1:]
        x, y, c = _me()
        sibling = (x, y, 1 - c)
        l, i = pl.program_id(0), pl.program_id(1)
        step = l * n + i

        def rows(st, core):
            return out.at[st // n, pl.ds(core * r2 + (st % n) * tr, tr), :]

        def copies(st):
            slot = st % 2
            local = pltpu.make_async_copy(slots.at[slot], rows(st, c), local_sems.at[slot])
            remote = pltpu.make_async_remote_copy(
                src_ref=slots.at[slot], dst_ref=rows(st, c), send_sem=send_sems.at[slot], recv_sem=recv_sem,
                device_id=sibling, device_id_type=MESH)
            return local, remote

        for ll in range(nl):
            @pl.when(l == ll)
            def _():
                acc = p_refs[ll][0].astype(F32)
                for d in range(1, N_DEV):
                    acc = acc + p_refs[ll][d].astype(F32)
                slots[step % 2] = acc

        for cp in copies(step):
            cp.start()

        @pl.when(step >= 1)
        def _():
            local, remote = copies(step - 1)
            local.wait()
            remote.wait_send()

        @pl.when(step == nl * n - 1)
        def _():
            local, remote = copies(step)
            local.wait()
            remote.wait_send()
            theirs = out.at[:, pl.ds((1 - c) * r2, r2), :]
            pltpu.make_async_remote_copy(src_ref=theirs, dst_ref=theirs, send_sem=send_sems.at[0],
                                         recv_sem=recv_sem, device_id=sibling, device_id_type=MESH).wait_recv()

    def piece_spec(ll):
        def index(l, i):
            return (0, jnp.where(l == ll, i, jnp.where(l < ll, 0, n - 1)), 0)
        return pl.BlockSpec((N_DEV, tr, cc), index)

    (out,), got = _call(
        body, name=name, grid=(nl, n),
        in_specs=[piece_spec(ll) for ll in range(nl)],
        out_specs=[pl.BlockSpec(memory_space=pl.ANY)],
        out_shape=[jax.ShapeDtypeStruct((nl, 2 * r2, cc), F32)],
        scratch_shapes=[pltpu.VMEM((2, tr, cc), F32), pltpu.SemaphoreType.DMA((2,)), pltpu.SemaphoreType.DMA((2,)),
                        pltpu.SemaphoreType.DMA(())],
        semantics=("arbitrary", "arbitrary"), args=tuple(pieces), comm=comm)
    return (out, got) if comm is not None else out


def sum_pieces(pieces, name):
    _, R, C = pieces.shape
    tr = _tile(R, 128) if R % 128 == 0 else R

    def body(p_ref, o_ref):
        acc = p_ref[0].astype(F32)
        for d in range(1, N_DEV):
            acc = acc + p_ref[d].astype(F32)
        o_ref[...] = acc

    return pl.pallas_call(
        body, name=name, grid=(R // tr,),
        in_specs=[pl.BlockSpec((N_DEV, tr, C), lambda i: (0, i, 0))],
        out_specs=pl.BlockSpec((tr, C), lambda i: (i, 0)),
        out_shape=jax.ShapeDtypeStruct((R, C), F32),
        compiler_params=_params("parallel"),
    )(pieces)


def adamw(w, g, m, v, name):
    Lyr, R, C = w.shape
    tr = _tile(R, 256) if R % 8 == 0 else R
    c1 = 1.0 / (1.0 - ADAM_B1 ** ADAM_STEP)
    c2 = 1.0 / (1.0 - ADAM_B2 ** ADAM_STEP)

    def body(w_ref, g_ref, m_ref, v_ref, d_ref, nm_ref, nv_ref):
        gv = g_ref[...]
        nm = ADAM_B1 * m_ref[...] + (1.0 - ADAM_B1) * gv
        nv = ADAM_B2 * v_ref[...] + (1.0 - ADAM_B2) * (gv * gv)
        nm_ref[...] = nm
        nv_ref[...] = nv
        d_ref[...] = -ADAM_LR * ((nm * c1) / (jnp.sqrt(nv * c2) + ADAM_EPS) + ADAM_WD * w_ref[...])

    spec = pl.BlockSpec((None, tr, C), lambda l, i: (l, i, 0))
    shp = jax.ShapeDtypeStruct(w.shape, F32)
    return pl.pallas_call(
        body, name=name, grid=(Lyr, R // tr),
        in_specs=[spec] * 4, out_specs=[spec] * 3, out_shape=[shp] * 3,
        compiler_params=_params("parallel", "parallel"),
    )(w, g, m, v)


def _rope_tables(T):
    pos = jnp.arange(T, dtype=F32)
    inv_freq = THETA ** (-jnp.arange(0, ROT, 2, dtype=F32) / ROT)
    ang = pos[:, None] * inv_freq[None, :]
    cs = jnp.concatenate([jnp.cos(ang), jnp.sin(ang)], axis=1)
    half = ROT // 2
    lane = jnp.arange(3 * LANES)
    table, lm = lane // LANES, lane % HD
    src = jnp.where(table == 0, lm % half, half + lm % half)
    i32 = lambda b: b.astype(jnp.int32)
    sign = jnp.where(table == 0, i32(lm < ROT), jnp.where(table == 1, -i32(lm < half), i32((lm >= half) & (lm < ROT))))
    place = (jnp.arange(ROT)[:, None] == src[None, :]) * sign[None, :].astype(F32)
    ones = ((table == 0) & (lm >= ROT)).astype(F32)
    return jnp.dot(cs, place, precision=lax.Precision.HIGHEST) + ones[None, :]


def _tab_specs(rows):
    return [pl.BlockSpec((rows, LANES), lambda i, k=k: (i, k)) for k in range(3)]


def kernel(x, attn_norm, attn_w_qkv, attn_w_o, attn_sink, conv_norm, conv_w_pw1, conv_b_pw1, conv_w_dw, conv_b_dw, conv_ln_g, conv_ln_b, conv_w_pw2, conv_b_pw2, ffn_norm, ffn_w_gu, ffn_w_down, final_norm, loss_target, m_attn_norm, m_attn_w_qkv, m_attn_w_o, m_attn_sink, m_conv_norm, m_conv_w_pw1, m_conv_b_pw1, m_conv_w_dw, m_conv_b_dw, m_conv_ln_g, m_conv_ln_b, m_conv_w_pw2, m_conv_b_pw2, m_ffn_norm, m_ffn_w_gu, m_ffn_w_down, m_final_norm, v_attn_norm, v_attn_w_qkv, v_attn_w_o, v_attn_sink, v_conv_norm, v_conv_w_pw1, v_conv_b_pw1, v_conv_w_dw, v_conv_b_dw, v_conv_ln_g, v_conv_ln_b, v_conv_w_pw2, v_conv_b_pw2, v_ffn_norm, v_ffn_w_gu, v_ffn_w_down, v_final_norm):
    T = x.shape[1]
    x0 = x[0]
    target = loss_target[0]
    ix, iy = lax.axis_index("x"), lax.axis_index("y")
    chip = 2 * ix + iy
    rc = rs1 = rs2 = _rope_tables(T)

    bf = lambda t: t.astype(BF16)
    col_row = [False, True]

    def place(vec, width):
        return lax.dynamic_update_slice(jnp.zeros((vec.shape[0], N_CHIPS * width), F32), vec, (0, chip * width))

    small_rows = jnp.concatenate([
        place(conv_norm, 256), place(conv_b_pw1, 512).reshape(2, D), place(conv_b_dw, 256), place(conv_ln_g, 256),
        place(conv_ln_b, 256), place(conv_b_pw2, 256), jnp.zeros((1, D), F32),
        place(conv_w_dw[0], 256), jnp.zeros((1, D), F32)], axis=0)
    w_qkv, = exchange(_Gather([bf(attn_w_qkv[0])], [False]), "gather_qkv")

    h0, qkv, (w_o, got) = rms_qkv(x0, attn_norm, w_qkv, rc, rs1, rs2,
                                  comm=_Both(_Gather([bf(attn_w_o[0])], [True]), _Scatter([], small_rows)))
    psmall = sum_pieces(got, "sum_small_params") * 0.5
    p_conv_norm, p_b_pw1 = psmall[0:1], psmall[1:3].reshape(1, 2 * D)
    p_b_dw, p_ln_g, p_ln_b, p_b_pw2 = psmall[3:4], psmall[4:5], psmall[5:6], psmall[6:7]
    p_w_dw = psmall[8:40]
    sink = attn_sink[0]
    o, (w_gu0,) = attn_fwd(qkv, sink, comm=_Gather([bf(ffn_w_gu[0])], [False]))
    zero_b = jnp.zeros((1, D), F32)
    x1 = mm_res(o, w_o, x0, zero_b, "attn_out")
    zero_gu = jnp.zeros((1, 2 * DFF), F32)
    h1, gu0, act0, (w_down0, w_pw1, w_pw2) = rms_mm_gate(
        x1, ffn_norm[0:1], w_gu0, zero_gu, DFF, True, BF16, "ffn0_up",
        comm=_Gather([bf(ffn_w_down[0]), bf(conv_w_pw1[0]), bf(conv_w_pw2[0])], [True, False, True]))
    x2 = mm_res(act0, w_down0, x1, zero_b, "ffn0_down")
    h2, pre, glu, _ = rms_mm_gate(x2, p_conv_norm, w_pw1, p_b_pw1, D, False, F32, "conv_pw1")
    dwc, sw, (w_gu1, w_down1) = conv_fwd(glu, p_w_dw, p_b_dw, p_ln_g, p_ln_b,
                                         comm=_Gather([bf(ffn_w_gu[1]), bf(ffn_w_down[1])], col_row))
    x3 = mm_res(sw, w_pw2, x2, p_b_pw2, "conv_pw2")
    h3, gu1, act1, _ = rms_mm_gate(x3, ffn_norm[1:2], w_gu1, zero_gu, DFF, True, BF16, "ffn1_up")
    dx4, loss_part, d_final = mm_res_loss(act1, w_down1, x3, final_norm.reshape(1, D), target)

    dgu1, _ = swiglu_bwd(dx4, w_down1, gu1, "ffn1_down_bwd")
    g_down1 = dw_row(act1, dx4, "ffn1_down_dw")
    dx3, d_ffn1, _ = mm_bt_rmsbwd(dgu1, w_gu1, x3, ffn_norm[1:2], dx4, "ffn1_up_bwd")
    g_gu1 = dw_col(h3, dgu1, "ffn1_up_dw")

    ddwc, d_ln_g, d_ln_b, d_b_pw2 = ln_silu_bwd(dx3, w_pw2, dwc, p_ln_g, p_ln_b)
    g_pw2 = dw_row(sw, dx3, "conv_pw2_dw")
    dpre, d_w_dw, d_b_dw, d_b_pw1, (r_gu1, r_down1) = conv_bwd(ddwc, glu, pre, p_w_dw,
                                                               comm=_Scatter([g_gu1, g_down1]))
    dx2, d_conv_norm, _ = mm_bt_rmsbwd(dpre, w_pw1, x2, p_conv_norm, dx3, "conv_pw1_bwd")
    g_pw1 = dw_col(h2, dpre, "conv_pw1_dw")

    dgu0, (r_pw1, r_pw2) = swiglu_bwd(dx2, w_down0, gu0, "ffn0_down_bwd", comm=_Scatter([g_pw1, g_pw2]))
    g_down0 = dw_row(act0, dx2, "ffn0_down_dw")
    dx1, d_ffn0, _ = mm_bt_rmsbwd(dgu0, w_gu0, x1, ffn_norm[0:1], dx2, "ffn0_up_bwd")
    g_gu0 = dw_col(h1, dgu0, "ffn0_up_dw")

    do = mm_bt(dx1, w_o, "attn_out_bwd")
    g_o = dw_row(o, dx1, "attn_out_dw")
    dq, dkc, dvc, d_sink, (r_gu0, r_down0, r_o) = attn_bwd(qkv, o, do, sink, rc, rs1, rs2,
                                                           comm=_Scatter([g_gu0, g_down0, g_o]))
    dqkv = kv_sum(dq, dkc, dvc, rc, rs1, rs2)[None]
    g_qkv = dw_col(h0, dqkv, "attn_qkv_dw")
    dx0, d_attn_norm, (r_qkv,) = mm_bt_rmsbwd(dqkv, w_qkv, x0, attn_norm, dx1, "attn_qkv_bwd",
                                              comm=_Scatter([g_qkv]))

    pad16 = lambda t: jnp.concatenate([t, jnp.zeros((1, D - t.shape[1]), F32)], axis=1)
    small_g = jnp.concatenate([
        d_attn_norm, pad16(d_sink), d_conv_norm, d_b_pw1.reshape(2, D), d_b_dw, d_ln_g, d_ln_b, d_b_pw2,
        d_ffn0, d_ffn1, d_final, pad16(loss_part), jnp.zeros((3, D), F32), d_w_dw], axis=0)
    gf_gu, (r_small,) = sum_swap([r_gu0, r_gu1], "sum_gu", comm=_Scatter([], small_g))
    gf_down = sum_swap([r_down0, r_down1], "sum_down")
    gf_pw1, gf_pw2 = sum_swap([r_pw1], "sum_pw1"), sum_swap([r_pw2], "sum_pw2")
    gf_qkv, gf_o = sum_swap([r_qkv], "sum_qkv"), sum_swap([r_o], "sum_o")
    gs = sum_pieces(r_small, "sum_small_grads")
    loss = gs[12, 0]

    def take(row0, nrows, width):
        return lax.dynamic_slice(gs, (row0, chip * width), (nrows, width))

    grads = {
        "attn_norm": gs[0:1], "attn_w_qkv": gf_qkv, "attn_w_o": gf_o, "attn_sink": gs[1:2, :N_HEADS],
        "conv_norm": take(2, 1, 256), "conv_w_pw1": gf_pw1,
        "conv_b_pw1": lax.dynamic_slice(gs[3:5].reshape(1, 2 * D), (0, chip * 512), (1, 512)),
        "conv_w_dw": take(16, 32, 256)[None, :CONV_W], "conv_b_dw": take(5, 1, 256), "conv_ln_g": take(6, 1, 256),
        "conv_ln_b": take(7, 1, 256), "conv_w_pw2": gf_pw2, "conv_b_pw2": take(8, 1, 256),
        "ffn_norm": gs[9:11], "ffn_w_gu": gf_gu, "ffn_w_down": gf_down, "final_norm": gs[11],
    }
    weights = dict(attn_norm=attn_norm, attn_w_qkv=attn_w_qkv, attn_w_o=attn_w_o, attn_sink=attn_sink,
                   conv_norm=conv_norm, conv_w_pw1=conv_w_pw1, conv_b_pw1=conv_b_pw1, conv_w_dw=conv_w_dw,
                   conv_b_dw=conv_b_dw, conv_ln_g=conv_ln_g, conv_ln_b=conv_ln_b, conv_w_pw2=conv_w_pw2,
                   conv_b_pw2=conv_b_pw2, ffn_norm=ffn_norm, ffn_w_gu=ffn_w_gu, ffn_w_down=ffn_w_down,
                   final_norm=final_norm)
    m_in = dict(attn_norm=m_attn_norm, attn_w_qkv=m_attn_w_qkv, attn_w_o=m_attn_w_o, attn_sink=m_attn_sink,
                conv_norm=m_conv_norm, conv_w_pw1=m_conv_w_pw1, conv_b_pw1=m_conv_b_pw1, conv_w_dw=m_conv_w_dw,
                conv_b_dw=m_conv_b_dw, conv_ln_g=m_conv_ln_g, conv_ln_b=m_conv_ln_b, conv_w_pw2=m_conv_w_pw2,
                conv_b_pw2=m_conv_b_pw2, ffn_norm=m_ffn_norm, ffn_w_gu=m_ffn_w_gu, ffn_w_down=m_ffn_w_down,
                final_norm=m_final_norm)
    v_in = dict(attn_norm=v_attn_norm, attn_w_qkv=v_attn_w_qkv, attn_w_o=v_attn_w_o, attn_sink=v_attn_sink,
                conv_norm=v_conv_norm, conv_w_pw1=v_conv_w_pw1, conv_b_pw1=v_conv_b_pw1, conv_w_dw=v_conv_w_dw,
                conv_b_dw=v_conv_b_dw, conv_ln_g=v_conv_ln_g, conv_ln_b=v_conv_ln_b, conv_w_pw2=v_conv_w_pw2,
                conv_b_pw2=v_conv_b_pw2, ffn_norm=v_ffn_norm, ffn_w_gu=v_ffn_w_gu, ffn_w_down=v_ffn_w_down,
                final_norm=v_final_norm)
    order = list(weights)
    g_out, d_out, m_out, v_out = [], [], [], []
    for nm in order:
        w = weights[nm]
        shape = w.shape
        as3 = lambda t: t.reshape((1,) * (3 - len(shape)) + shape) if len(shape) < 3 else t.reshape(shape)
        g3 = as3(grads[nm].reshape(shape))
        delta, nm_, nv_ = adamw(as3(w), g3, as3(m_in[nm]), as3(v_in[nm]), "adamw_" + nm)
        g_out.append(g3.reshape(shape))
        d_out.append(delta.reshape(shape))
        m_out.append(nm_.reshape(shape))
        v_out.append(nv_.reshape(shape))
    return (loss, dx0[None], *g_out, *d_out, *m_out, *v_out)
```

```python
import functools
import math

import jax
import jax.numpy as jnp
from jax import lax
from jax.experimental import pallas as pl
from jax.experimental.pallas import tpu as pltpu

F32 = jnp.float32
BF16 = jnp.bfloat16

D = 1024
N_HEADS = 16
N_KV = 4
GROUP = N_HEADS // N_KV
HD = 64
ROT = 16
THETA = 500000.0
BLK = 128
QKV = (N_HEADS + 2 * N_KV) * HD
KV_OFF = N_HEADS * HD
DFF = 2816
CONV_W = 31
CONV_PAD = 15
HALO = 16
CONV_JB = 8
CONV_JB_BWD = 8
EPS = 1e-6
NEG = -1e30
N_CHIPS = 4
N_DEV = 8
LANES = 128
SUBLANES = 8

ADAM_LR, ADAM_B1, ADAM_B2, ADAM_EPS, ADAM_WD, ADAM_STEP = 0.001, 0.9, 0.999, 1e-08, 0.01, 10

VMEM_LIMIT = 56 * 1024 * 1024
MESH = pl.DeviceIdType.MESH


def _params(*sem):
    return pltpu.CompilerParams(dimension_semantics=sem, vmem_limit_bytes=VMEM_LIMIT)


def _tile(n, want):
    if n <= want:
        return n
    for t in range(want, 7, -1):
        if n % t == 0 and t % 8 == 0:
            return t
    return n


MXU_COLS = 256


def _col_chunks(n):
    return [slice(c, min(c + MXU_COLS, n)) for c in range(0, n, MXU_COLS)]


def _sigmoid(v):
    return 1.0 / (1.0 + jnp.exp(-v))


def _rms_fwd(xv, gain):
    r = lax.rsqrt(jnp.mean(xv * xv, axis=-1, keepdims=True) + EPS)
    return xv * r * gain


def _rms_bwd(dh, xv, gain, dres):
    r = lax.rsqrt(jnp.mean(xv * xv, axis=-1, keepdims=True) + EPS)
    xhat = xv * r
    gy = dh * gain
    dx = r * (gy - xhat * jnp.mean(gy * xhat, axis=-1, keepdims=True))
    return dx + dres, dh * xhat


def _rope(blk, c, s1, s2):
    return blk * c + pltpu.roll(blk, LANES - ROT // 2, 1) * s1 + pltpu.roll(blk, ROT // 2, 1) * s2


def _dot(a, b):
    return jnp.dot(a, b, preferred_element_type=F32)


def _dot_tb(a, b):
    return lax.dot_general(a, b, (((1,), (1,)), ((), ())), preferred_element_type=F32)


def _dot_ta(a, b):
    return lax.dot_general(a, b, (((0,), (0,)), ((), ())), preferred_element_type=F32)


def rms_qkv(x, gain, w, rc, rs1, rs2, comm=None):
    T = x.shape[0]
    tm = _tile(T, 512)

    def body(x_ref, g_ref, w_ref, c_ref, s1_ref, s2_ref, h_ref, qkv_ref):
        h = _rms_fwd(x_ref[...], g_ref[...]).astype(BF16)
        h_ref[...] = h
        acc = _dot(h, w_ref[...])
        c, s1, s2 = c_ref[...], s1_ref[...], s2_ref[...]
        n_rot = (KV_OFF + N_KV * HD) // LANES
        for j in range(n_rot):
            sl = slice(LANES * j, LANES * (j + 1))
            roped = _rope(acc[:, sl], c, s1, s2)
            if j < KV_OFF // LANES:
                roped = roped * Q_SCALE
            qkv_ref[:, sl] = roped.astype(BF16)
        qkv_ref[:, n_rot * LANES:] = acc[:, n_rot * LANES:].astype(BF16)

    row = lambda i: (i, 0)
    full = lambda i: (0, 0)
    (h, qkv), got = _call(
        body, name="rms_qkv", grid=(T // tm,),
        in_specs=[pl.BlockSpec((tm, D), row), pl.BlockSpec((1, D), full), pl.BlockSpec((D, QKV), full),
                  *_tab_specs(tm)],
        out_specs=[pl.BlockSpec((tm, D), row), pl.BlockSpec((tm, QKV), row)],
        out_shape=[jax.ShapeDtypeStruct((T, D), BF16), jax.ShapeDtypeStruct((T, QKV), BF16)],
        semantics=("parallel",), args=(x, gain, w, rc, rs1, rs2), comm=comm)
    return h, qkv, got


Q_SCALE = 1.0 / math.sqrt(HD)


def _attn_mask(n, T):
    ci = lax.broadcasted_iota(jnp.int32, (3 * BLK, BLK), 0)
    qi = lax.broadcasted_iota(jnp.int32, (3 * BLK, BLK), 1)
    key_pos = n * BLK - BLK + ci
    return (jnp.abs(ci - BLK - qi) <= BLK) & (key_pos >= 0) & (key_pos < T)


def _kv_padded(kv, first_tile):
    low = lax.broadcasted_iota(jnp.int32, (3 * BLK, LANES), 1) < HD
    zero = jnp.zeros((3 * BLK, LANES), BF16)
    out = {}
    for g in range(N_KV):
        t = kv[:, (first_tile + g // 2) * LANES:(first_tile + g // 2 + 1) * LANES]
        swapped = jnp.concatenate([t[:, HD:], t[:, :HD]], axis=1)
        for p in range(2):
            out[g, p] = jnp.where(low if p == 0 else ~low, t if g % 2 == p else swapped, zero)
    return out


def _softmax_sink(s, valid, sk):
    s = jnp.where(valid, s, NEG)
    m = jnp.maximum(jnp.max(s, axis=0, keepdims=True), sk)
    e = jnp.exp(s - m)
    es = jnp.exp(sk - m)
    inv = 1.0 / (jnp.sum(e, axis=0, keepdims=True) + es)
    return e * inv, es * inv


def _attn_specs(T):
    nb = T // BLK
    kv_blk = 2 * N_KV * HD
    kv_col = KV_OFF // kv_blk
    q_spec = pl.BlockSpec((BLK, KV_OFF), lambda n: (n, 0))
    prev = pl.BlockSpec((BLK, kv_blk), lambda n: (jnp.maximum(n - 1, 0), kv_col))
    own = pl.BlockSpec((BLK, kv_blk), lambda n: (n, kv_col))
    nxt = pl.BlockSpec((BLK, kv_blk), lambda n: (jnp.minimum(n + 1, nb - 1), kv_col))
    return nb, q_spec, prev, own, nxt


def attn_fwd(qkv, sink, comm=None):
    T = qkv.shape[0]
    nb, q_spec, prev, own, nxt = _attn_specs(T)

    def body(sink_ref, q_ref, kp_ref, ko_ref, kn_ref, o_ref):
        valid = _attn_mask(pl.program_id(0), T)
        kv = jnp.concatenate([kp_ref[...], ko_ref[...], kn_ref[...]], axis=0)
        kx, vx = _kv_padded(kv, 0), _kv_padded(kv, 2)
        tile = lambda ref, h: ref[:, (h // 2) * LANES:(h // 2 + 1) * LANES]
        ss = [_dot_tb(kx[h // GROUP, h % 2], tile(q_ref, h)) for h in range(N_HEADS)]
        ps = [_softmax_sink(ss[h], valid, sink_ref[h])[0].astype(BF16) for h in range(N_HEADS)]
        vxt = {k: v.T for k, v in vx.items()}
        for j in range(N_HEADS // 2):
            g = 2 * j // GROUP
            o_t = _dot(vxt[g, 0], ps[2 * j]) + _dot(vxt[g, 1], ps[2 * j + 1])
            o_ref[:, j * LANES:(j + 1) * LANES] = o_t.T.astype(BF16)

    (o,), got = _call(
        body, name="attn_fwd", grid=(nb,),
        in_specs=[pl.BlockSpec(memory_space=pltpu.SMEM), q_spec, prev, own, nxt],
        out_specs=[pl.BlockSpec((BLK, D), lambda n: (n, 0))],
        out_shape=[jax.ShapeDtypeStruct((T, D), BF16)],
        semantics=("parallel",), args=(sink, qkv, qkv, qkv, qkv), comm=comm)
    return o, got


def rms_mm_gate(x, gain, w, bias, H, swiglu, act_dtype, name, comm=None):
    fused = isinstance(x, tuple)
    T = (x[0] if fused else x).shape[0]
    tm = _tile(T, 512)

    def body(*refs):
        if fused:
            a_ref, wp_ref, bp_ref, r_ref, g_ref, w_ref, b_ref, x_ref, h_ref, pre_ref, act_ref = refs
            xv = _dot(a_ref[...], wp_ref[...]) + bp_ref[...] + r_ref[...]
            x_ref[...] = xv
        else:
            x_ref, g_ref, w_ref, b_ref, h_ref, pre_ref, act_ref = refs
            xv = x_ref[...]
        h = _rms_fwd(xv, g_ref[...]).astype(BF16)
        h_ref[...] = h
        for cs in _col_chunks(H):
            cs2 = slice(H + cs.start, H + cs.stop)
            a = _dot(h, w_ref[:, cs]) + b_ref[:, cs]
            b = _dot(h, w_ref[:, cs2]) + b_ref[:, cs2]
            pre_ref[0, :, cs] = a.astype(BF16)
            pre_ref[1, :, cs] = b.astype(BF16)
            if swiglu:
                act = a * _sigmoid(a) * b
            else:
                act = a * _sigmoid(b)
            act_ref[:, cs] = act.astype(act_dtype)

    row = lambda i: (i, 0)
    full = lambda i: (0, 0)
    if fused:
        K = x[0].shape[1]
        x_specs = [pl.BlockSpec((tm, K), row), pl.BlockSpec((K, D), full, pipeline_mode=pl.Buffered(1)),
                   pl.BlockSpec((1, D), full), pl.BlockSpec((tm, D), row)]
        x_out = ([pl.BlockSpec((tm, D), row)], [jax.ShapeDtypeStruct((T, D), F32)])
        x_args = tuple(x)
    else:
        x_specs, x_out, x_args = [pl.BlockSpec((tm, D), row)], ([], []), (x,)
    outs, got = _call(
        body, name=name, grid=(T // tm,),
        in_specs=x_specs + [pl.BlockSpec((1, D), full),
                            pl.BlockSpec((D, 2 * H), full, pipeline_mode=pl.Buffered(1)), pl.BlockSpec((1, 2 * H), full)],
        out_specs=x_out[0] + [pl.BlockSpec((tm, D), row), pl.BlockSpec((2, tm, H), lambda i: (0, i, 0)),
                              pl.BlockSpec((tm, H), row)],
        out_shape=x_out[1] + [jax.ShapeDtypeStruct((T, D), BF16), jax.ShapeDtypeStruct((2, T, H), BF16),
                              jax.ShapeDtypeStruct((T, H), act_dtype)],
        semantics=("parallel",), args=x_args + (gain, w, bias), comm=comm)
    return (*outs, got)


def _conv_tiles(T):
    tt = _tile(T, 512)
    return tt, tt // SUBLANES, D // LANES


def _fill_strided(ext, p, L):
    main = p[HALO:HALO + SUBLANES * L, :].reshape(SUBLANES, L, LANES)
    ext[CONV_PAD:CONV_PAD + L] = jnp.swapaxes(main, 0, 1)

    def ibody(i, carry):
        ext[i] = p[pl.ds(i + 1, SUBLANES, stride=L), :]
        ext[i + CONV_PAD + L] = p[pl.ds(i + CONV_PAD + L + 1, SUBLANES, stride=L), :]
        return carry

    lax.fori_loop(0, CONV_PAD, ibody, 0, unroll=3)


def _conv_specs(T, tt):
    main = pl.BlockSpec((tt, D), lambda i: (i, 0))
    per = tt // HALO
    prev = pl.BlockSpec((HALO, D), lambda i: (jnp.maximum(i * per - 1, 0), 0))
    nxt = pl.BlockSpec((HALO, D), lambda i: (jnp.minimum((i + 1) * per, T // HALO - 1), 0))
    return main, prev, nxt


def _fill_pad(pad, main_ref, prev_ref, next_ref, i, n_i, tt, nlt):
    keep_p = (i > 0).astype(F32)
    keep_n = (i < n_i - 1).astype(F32)
    for lt in range(nlt):
        sl = slice(lt * LANES, (lt + 1) * LANES)
        pad[lt, 0:HALO, :] = prev_ref[:, sl] * keep_p
        pad[lt, HALO:HALO + tt, :] = main_ref[:, sl]
        pad[lt, HALO + tt:2 * HALO + tt, :] = next_ref[:, sl] * keep_n


def conv_fwd(glu, w_dw, b_dw, ln_g, ln_b, comm=None):
    T = glu.shape[0]
    tt, L, nlt = _conv_tiles(T)
    n_i = T // tt
    main, prev, nxt = _conv_specs(T, tt)

    def body(x_ref, xp_ref, xn_ref, w_ref, b_ref, g_ref, bb_ref, dwc_ref, sw_ref, pad, ob, ext, wk):
        i = pl.program_id(0)
        _fill_pad(pad, x_ref, xp_ref, xn_ref, i, n_i, tt, nlt)
        for lt in range(nlt):
            sl = slice(lt * LANES, (lt + 1) * LANES)
            o = ob.at[lt]
            _fill_strided(ext, pad.at[lt], L)
            for k in range(CONV_W):
                wk[k] = jnp.broadcast_to(w_ref[k:k + 1, sl], (SUBLANES, LANES))

            def jbody(jb, carry):
                j = jb * CONV_JB
                accs = [None] * CONV_JB
                for m in range(CONV_W + CONV_JB - 1):
                    e = ext[j + m]
                    for u in range(CONV_JB):
                        if 0 <= m - u < CONV_W:
                            t = e * wk[m - u]
                            accs[u] = t if accs[u] is None else accs[u] + t
                for u in range(CONV_JB):
                    o[pl.ds(j + u, SUBLANES, stride=L), :] = accs[u]
                return carry

            lax.fori_loop(0, L // CONV_JB, jbody, 0)
        y = jnp.concatenate([ob[lt] for lt in range(nlt)], axis=1) + b_ref[...]
        dwc_ref[...] = y
        mu = jnp.mean(y, axis=-1, keepdims=True)
        yc = y - mu
        var = jnp.mean(yc * yc, axis=-1, keepdims=True)
        z = yc * lax.rsqrt(var + EPS) * g_ref[...] + bb_ref[...]
        sw_ref[...] = (z * _sigmoid(z)).astype(BF16)

    full = lambda i: (0, 0)
    (dwc, sw), got = _call(
        body, name="conv_fwd", grid=(n_i,),
        in_specs=[main, prev, nxt, pl.BlockSpec((32, D), full), pl.BlockSpec((1, D), full),
                  pl.BlockSpec((1, D), full), pl.BlockSpec((1, D), full)],
        out_specs=[pl.BlockSpec((tt, D), lambda i: (i, 0)), pl.BlockSpec((tt, D), lambda i: (i, 0))],
        out_shape=[jax.ShapeDtypeStruct((T, D), F32), jax.ShapeDtypeStruct((T, D), BF16)],
        scratch_shapes=[pltpu.VMEM((nlt, tt + 2 * HALO, LANES), F32), pltpu.VMEM((nlt, tt, LANES), F32),
                        pltpu.VMEM((L + 2 * HALO, SUBLANES, LANES), F32), pltpu.VMEM((32, SUBLANES, LANES), F32)],
        semantics=("parallel",), args=(glu, glu, glu, w_dw, b_dw, ln_g, ln_b), comm=comm)
    return dwc, sw, got


def mm_res_loss(a, w, resid, gain, target):
    T, K = a.shape
    tm = _tile(T, 512)

    def body(a_ref, w_ref, r_ref, g_ref, t_ref, dx_ref, loss_ref, dg_ref):
        @pl.when(pl.program_id(0) == 0)
        def _():
            loss_ref[...] = jnp.zeros_like(loss_ref)
            dg_ref[...] = jnp.zeros_like(dg_ref)

        xv, gain_v = _dot(a_ref[...], w_ref[...]) + r_ref[...], g_ref[...]
        err = _rms_fwd(xv, gain_v) - t_ref[...]
        part = 0.5 * jnp.sum(jnp.mean(err * err, axis=-1, keepdims=True), axis=0, keepdims=True)
        loss_ref[...] += jnp.broadcast_to(part, loss_ref.shape)
        dx, dgr = _rms_bwd(err * (1.0 / D), xv, gain_v, 0.0)
        dx_ref[...] = dx
        dg_ref[...] += jnp.sum(dgr, axis=0, keepdims=True)

    row = lambda i: (i, 0)
    full = lambda i: (0, 0)
    return pl.pallas_call(
        body, name="ffn1_down_loss", grid=(T // tm,),
        in_specs=[pl.BlockSpec((tm, K), row), pl.BlockSpec((K, D), full), pl.BlockSpec((tm, D), row),
                  pl.BlockSpec((1, D), full), pl.BlockSpec((tm, D), row)],
        out_specs=[pl.BlockSpec((tm, D), row), pl.BlockSpec((1, LANES), full), pl.BlockSpec((1, D), full)],
        out_shape=[jax.ShapeDtypeStruct((T, D), F32), jax.ShapeDtypeStruct((1, LANES), F32),
                   jax.ShapeDtypeStruct((1, D), F32)],
        compiler_params=_params("arbitrary"),
    )(a, w, resid, gain, target)


def swiglu_bwd(dx, w_down, pre, name, comm=None):
    T = dx.shape[0]
    H = w_down.shape[0]
    tm = _tile(T, 512)

    def body(dx_ref, w_ref, pre_ref, dpre_ref):
        dxb = dx_ref[...].astype(BF16)
        for cs in _col_chunks(H):
            dact = _dot_tb(dxb, w_ref[cs, :])
            g = pre_ref[0, :, cs].astype(F32)
            u = pre_ref[1, :, cs].astype(F32)
            sg = _sigmoid(g)
            dpre_ref[0, :, cs] = (dact * u * sg * (1.0 + g * (1.0 - sg))).astype(BF16)
            dpre_ref[1, :, cs] = (dact * g * sg).astype(BF16)

    (dpre,), got = _call(
        body, name=name, grid=(T // tm,),
        in_specs=[pl.BlockSpec((tm, D), lambda i: (i, 0)),
                  pl.BlockSpec((H, D), lambda i: (0, 0), pipeline_mode=pl.Buffered(1)),
                  pl.BlockSpec((2, tm, H), lambda i: (0, i, 0))],
        out_specs=[pl.BlockSpec((2, tm, H), lambda i: (0, i, 0))],
        out_shape=[jax.ShapeDtypeStruct((2, T, H), BF16)],
        semantics=("parallel",), args=(dx, w_down, pre), comm=comm)
    return dpre, got


def mm_bt_rmsbwd(dpre, w, x, gain, dres, name, comm=None):
    nh, T, H = dpre.shape
    tm = _tile(T, 512)

    def body(dp_ref, w_ref, x_ref, g_ref, dres_ref, dx_ref, dg_ref):
        @pl.when(pl.program_id(0) == 0)
        def _():
            dg_ref[...] = jnp.zeros_like(dg_ref)

        dh = _dot_tb(dp_ref[0], w_ref[:, 0:H])
        for hf in range(1, nh):
            dh = dh + _dot_tb(dp_ref[hf], w_ref[:, hf * H:(hf + 1) * H])
        dx, dgr = _rms_bwd(dh, x_ref[...], g_ref[...], dres_ref[...])
        dx_ref[...] = dx
        dg_ref[...] += jnp.sum(dgr, axis=0, keepdims=True)

    row = lambda i: (i, 0)
    full = lambda i: (0, 0)
    (dx, dg), got = _call(
        body, name=name, grid=(T // tm,),
        in_specs=[pl.BlockSpec((nh, tm, H), lambda i: (0, i, 0)),
                  pl.BlockSpec((D, nh * H), full, pipeline_mode=pl.Buffered(1)),
                  pl.BlockSpec((tm, D), row), pl.BlockSpec((1, D), full), pl.BlockSpec((tm, D), row)],
        out_specs=[pl.BlockSpec((tm, D), row), pl.BlockSpec((1, D), full)],
        out_shape=[jax.ShapeDtypeStruct((T, D), F32), jax.ShapeDtypeStruct((1, D), F32)],
        semantics=("arbitrary",), args=(dpre, w, x, gain, dres), comm=comm)
    return dx, dg, got


def dw_col(a, dpre, name):
    T = a.shape[0]
    nh, _, H = dpre.shape
    per = nh * H // N_CHIPS
    bph = N_CHIPS // nh
    tt = _tile(T, 2048)
    nt = T // tt

    def body(a_ref, b_ref, o_ref, acc):
        t = pl.program_id(1)

        @pl.when(t == 0)
        def _():
            acc[...] = jnp.zeros_like(acc)

        acc[...] += _dot_ta(a_ref[...], b_ref[...])

        @pl.when(t == nt - 1)
        def _():
            o_ref[...] = acc[...].astype(BF16)

    return pl.pallas_call(
        body, name=name, grid=(N_CHIPS, nt),
        in_specs=[pl.BlockSpec((tt, D), lambda q, t: (t, 0)),
                  pl.BlockSpec((None, tt, per), lambda q, t: (q // bph, t, q % bph))],
        out_specs=pl.BlockSpec((None, D, per), lambda q, t: (q, 0, 0)),
        out_shape=jax.ShapeDtypeStruct((N_CHIPS, D, per), BF16),
        scratch_shapes=[pltpu.VMEM((D, per), F32)],
        compiler_params=_params("parallel", "arbitrary"),
    )(a, dpre)


def dw_row(a, b, name):
    T, R = a.shape
    cw = 1408 if R % 1408 == 0 else R
    tt = _tile(T, 1024)
    nt = T // tt

    def body(a_ref, b_ref, o_ref, acc):
        t = pl.program_id(1)

        @pl.when(t == 0)
        def _():
            acc[...] = jnp.zeros_like(acc)

        acc[...] += _dot_ta(a_ref[...], b_ref[...].astype(BF16))

        @pl.when(t == nt - 1)
        def _():
            o_ref[...] = acc[...].astype(BF16)

    out = pl.pallas_call(
        body, name=name, grid=(R // cw, nt),
        in_specs=[pl.BlockSpec((tt, cw), lambda q, t: (t, q)), pl.BlockSpec((tt, D), lambda q, t: (t, 0))],
        out_specs=pl.BlockSpec((cw, D), lambda q, t: (q, 0)),
        out_shape=jax.ShapeDtypeStruct((R, D), BF16),
        scratch_shapes=[pltpu.VMEM((cw, D), F32)],
        compiler_params=_params("parallel", "arbitrary"),
    )(a, b)
    return out.reshape(N_CHIPS, R // N_CHIPS, D)


def ln_silu_bwd(dx, w_pw2, dwc, ln_g, ln_b):
    T = dx.shape[0]
    tm = _tile(T, 512)

    def body(dx_ref, w_ref, y_ref, g_ref, b_ref, dy_ref, dg_ref, db_ref, dbo_ref):
        @pl.when(pl.program_id(0) == 0)
        def _():
            dg_ref[...] = jnp.zeros_like(dg_ref)
            db_ref[...] = jnp.zeros_like(db_ref)
            dbo_ref[...] = jnp.zeros_like(dbo_ref)

        dxv = dx_ref[...]
        dsw = _dot_tb(dxv.astype(BF16), w_ref[...])
        y = y_ref[...]
        mu = jnp.mean(y, axis=-1, keepdims=True)
        yc = y - mu
        rstd = lax.rsqrt(jnp.mean(yc * yc, axis=-1, keepdims=True) + EPS)
        xhat = yc * rstd
        z = xhat * g_ref[...] + b_ref[...]
        sg = _sigmoid(z)
        dz = dsw * sg * (1.0 + z * (1.0 - sg))
        dxh = dz * g_ref[...]
        dy_ref[...] = rstd * (dxh - jnp.mean(dxh, axis=-1, keepdims=True)
                              - xhat * jnp.mean(dxh * xhat, axis=-1, keepdims=True))
        dg_ref[...] += jnp.sum(dz * xhat, axis=0, keepdims=True)
        db_ref[...] += jnp.sum(dz, axis=0, keepdims=True)
        dbo_ref[...] += jnp.sum(dxv, axis=0, keepdims=True)

    row = lambda i: (i, 0)
    full = lambda i: (0, 0)
    vec = pl.BlockSpec((1, D), full)
    return pl.pallas_call(
        body, name="ln_silu_bwd", grid=(T // tm,),
        in_specs=[pl.BlockSpec((tm, D), row), pl.BlockSpec((D, D), full), pl.BlockSpec((tm, D), row), vec, vec],
        out_specs=[pl.BlockSpec((tm, D), row), vec, vec, vec],
        out_shape=[jax.ShapeDtypeStruct((T, D), F32)] + [jax.ShapeDtypeStruct((1, D), F32)] * 3,
        compiler_params=_params("arbitrary"),
    )(dx, w_pw2, dwc, ln_g, ln_b)


def conv_bwd(ddwc, glu, pre, w_dw, comm=None):
    T = ddwc.shape[0]
    tt, L, nlt = _conv_tiles(T)
    n_i = T // tt
    main, prev, nxt = _conv_specs(T, tt)

    def body(d_ref, dp_ref, dn_ref, x_ref, xp_ref, xn_ref, pre_ref, w_ref,
             dpre_ref, dw_ref, dbd_ref, dbp_ref, padd, padx, ob, extd, extx, wk):
        i = pl.program_id(0)

        @pl.when(i == 0)
        def _():
            dw_ref[...] = jnp.zeros_like(dw_ref)
            dbd_ref[...] = jnp.zeros_like(dbd_ref)
            dbp_ref[...] = jnp.zeros_like(dbp_ref)

        _fill_pad(padd, d_ref, dp_ref, dn_ref, i, n_i, tt, nlt)
        _fill_pad(padx, x_ref, xp_ref, xn_ref, i, n_i, tt, nlt)
        for lt in range(nlt):
            sl = slice(lt * LANES, (lt + 1) * LANES)
            o = ob.at[lt]
            _fill_strided(extd, padd.at[lt], L)
            _fill_strided(extx, padx.at[lt], L)
            for k in range(CONV_W):
                wk[k] = jnp.broadcast_to(w_ref[k:k + 1, sl], (SUBLANES, LANES))

            nu = CONV_JB_BWD

            def jbody(jb, accs):
                j = jb * nu
                accs = list(accs)
                d = [extd[j + u + CONV_PAD] for u in range(nu)]
                g = [None] * nu
                for m in range(CONV_W + nu - 1):
                    ed = extd[j + 2 * CONV_PAD + nu - 1 - m]
                    ex = extx[j + m]
                    for u in range(nu):
                        k = m - (nu - 1 - u)
                        if 0 <= k < CONV_W:
                            t = ed * wk[k]
                            g[u] = t if g[u] is None else g[u] + t
                        k = m - u
                        if 0 <= k < CONV_W:
                            accs[k] = accs[k] + d[u] * ex
                for u in range(nu):
                    o[pl.ds(j + u, SUBLANES, stride=L), :] = g[u]
                return tuple(accs)

            accs = lax.fori_loop(0, L // nu, jbody, tuple(jnp.zeros((SUBLANES, LANES), F32) for _ in range(CONV_W)))
            for k in range(CONV_W):
                dw_ref[k:k + 1, sl] += jnp.sum(accs[k], axis=0, keepdims=True)
        dglu = jnp.concatenate([ob[lt] for lt in range(nlt)], axis=1)
        a = pre_ref[0].astype(F32)
        gate = pre_ref[1].astype(F32)
        sg = _sigmoid(gate)
        da = dglu * sg
        dgate = dglu * a * sg * (1.0 - sg)
        dpre_ref[0] = da.astype(BF16)
        dpre_ref[1] = dgate.astype(BF16)
        dbd_ref[...] += jnp.sum(d_ref[...], axis=0, keepdims=True)
        dbp_ref[0] += jnp.sum(da, axis=0, keepdims=True)
        dbp_ref[1] += jnp.sum(dgate, axis=0, keepdims=True)

    full = lambda i: (0, 0)
    (dpre, dw, dbd, dbp), got = _call(
        body, name="conv_bwd", grid=(n_i,),
        in_specs=[main, prev, nxt, main, prev, nxt, pl.BlockSpec((2, tt, D), lambda i: (0, i, 0)),
                  pl.BlockSpec((32, D), full)],
        out_specs=[pl.BlockSpec((2, tt, D), lambda i: (0, i, 0)), pl.BlockSpec((32, D), full),
                   pl.BlockSpec((1, D), full), pl.BlockSpec((2, 1, D), lambda i: (0, 0, 0))],
        out_shape=[jax.ShapeDtypeStruct((2, T, D), BF16), jax.ShapeDtypeStruct((32, D), F32),
                   jax.ShapeDtypeStruct((1, D), F32), jax.ShapeDtypeStruct((2, 1, D), F32)],
        scratch_shapes=[pltpu.VMEM((nlt, tt + 2 * HALO, LANES), F32), pltpu.VMEM((nlt, tt + 2 * HALO, LANES), F32),
                        pltpu.VMEM((nlt, tt, LANES), F32), pltpu.VMEM((L + 2 * HALO, SUBLANES, LANES), F32),
                        pltpu.VMEM((L + 2 * HALO, SUBLANES, LANES), F32), pltpu.VMEM((32, SUBLANES, LANES), F32)],
        semantics=("arbitrary",), args=(ddwc, ddwc, ddwc, glu, glu, glu, pre, w_dw), comm=comm)
    return dpre, dw, dbd, dbp, got


def mm_bt(a, w, name):
    T = a.shape[0]
    N = w.shape[0]
    tm = _tile(T, 1024)

    def body(a_ref, w_ref, o_ref):
        o_ref[...] = _dot_tb(a_ref[...].astype(BF16), w_ref[...]).astype(BF16)

    return pl.pallas_call(
        body, name=name, grid=(T // tm,),
        in_specs=[pl.BlockSpec((tm, D), lambda i: (i, 0)), pl.BlockSpec((N, D), lambda i: (0, 0))],
        out_specs=pl.BlockSpec((tm, N), lambda i: (i, 0)),
        out_shape=jax.ShapeDtypeStruct((T, N), BF16),
        compiler_params=_params("parallel"),
    )(a, w)


def attn_bwd(qkv, o, do, sink, rc, rs1, rs2, comm=None):
    T = qkv.shape[0]
    nb, q_spec, prev, own, nxt = _attn_specs(T)
    kvw = N_KV * HD

    def body(sink_ref, q_ref, kp_ref, ko_ref, kn_ref, o_ref, do_ref, c_ref, s1_ref, s2_ref,
             dq_ref, dkc_ref, dvc_ref, dsink_ref):
        n = pl.program_id(0)

        @pl.when(n == 0)
        def _():
            dsink_ref[...] = jnp.zeros_like(dsink_ref)

        valid = _attn_mask(n, T)
        kv = jnp.concatenate([kp_ref[...], ko_ref[...], kn_ref[...]], axis=0)
        kx, vx = _kv_padded(kv, 0), _kv_padded(kv, 2)
        tile = lambda ref, j: ref[:, j * LANES:(j + 1) * LANES]
        ss = [_dot_tb(kx[h // GROUP, h % 2], tile(q_ref, h // 2)) for h in range(N_HEADS)]
        dps = [_dot_tb(vx[h // GROUP, h % 2], tile(do_ref, h // 2)) for h in range(N_HEADS)]
        low_d = lax.broadcasted_iota(jnp.int32, (LANES, BLK), 0) < HD
        deltas = []
        for j in range(N_HEADS // 2):
            prod_t = tile(do_ref, j).astype(F32).T * tile(o_ref, j).astype(F32).T
            deltas.append(jnp.sum(jnp.where(low_d, prod_t, 0.0), axis=0, keepdims=True))
            deltas.append(jnp.sum(jnp.where(low_d, 0.0, prod_t), axis=0, keepdims=True))
        lane = lax.broadcasted_iota(jnp.int32, (1, N_HEADS), 1)
        dsink = jnp.zeros((1, N_HEADS), F32)
        pbs, dss = [], []
        for h in range(N_HEADS):
            p, p_sink = _softmax_sink(ss[h], valid, sink_ref[h])
            dss.append((p * (dps[h] - deltas[h])).astype(BF16))
            pbs.append(p.astype(BF16))
            part = -jnp.sum(p_sink * deltas[h], axis=1, keepdims=True)
            dsink = dsink + jnp.where(lane == h, part, 0.0)
        dsink_ref[...] += dsink
        c, s1, s2 = c_ref[...], s1_ref[...], s2_ref[...]
        kxt = {k: v.T for k, v in kx.items()}
        for j in range(N_HEADS // 2):
            g = 2 * j // GROUP
            dq_t = _dot(kxt[g, 0], dss[2 * j]) + _dot(kxt[g, 1], dss[2 * j + 1])
            dq_ref[:, j * LANES:(j + 1) * LANES] = (_rope(dq_t.T, c, -s1, -s2) * Q_SCALE).astype(BF16)
        low_k = lax.broadcasted_iota(jnp.int32, (3 * BLK, LANES), 1) < HD
        cols = lambda xs, g, p: jnp.concatenate([xs[GROUP * g + p], xs[GROUP * g + 2 + p]], axis=1)
        for t in range(N_KV // 2):
            sums = {}
            for g in (2 * t, 2 * t + 1):
                q2 = jnp.concatenate([tile(q_ref, 2 * g), tile(q_ref, 2 * g + 1)], axis=0)
                do2 = jnp.concatenate([tile(do_ref, 2 * g), tile(do_ref, 2 * g + 1)], axis=0)
                for p in range(2):
                    sums[g, p] = (_dot(cols(dss, g, p), q2), _dot(cols(pbs, g, p), do2))
            for which, ref in ((0, dkc_ref), (1, dvc_ref)):
                keep = jnp.where(low_k, sums[2 * t, 0][which], sums[2 * t + 1, 1][which])
                swap = jnp.where(low_k, sums[2 * t + 1, 0][which], sums[2 * t, 1][which])
                ref[:, t * LANES:(t + 1) * LANES] = keep + pltpu.roll(swap, HD, 1)

    row = lambda n: (n, 0)
    (dq, dkc, dvc, dsink), got = _call(
        body, name="attn_bwd", grid=(nb,),
        in_specs=[pl.BlockSpec(memory_space=pltpu.SMEM), q_spec, prev, own, nxt,
                  pl.BlockSpec((BLK, D), row), pl.BlockSpec((BLK, D), row), *_tab_specs(BLK)],
        out_specs=[pl.BlockSpec((BLK, D), row), pl.BlockSpec((None, 3 * BLK, kvw), lambda n: (n, 0, 0)),
                   pl.BlockSpec((None, 3 * BLK, kvw), lambda n: (n, 0, 0)), pl.BlockSpec((1, N_HEADS), lambda n: (0, 0))],
        out_shape=[jax.ShapeDtypeStruct((T, QKV), BF16), jax.ShapeDtypeStruct((nb, 3 * BLK, kvw), F32),
                   jax.ShapeDtypeStruct((nb, 3 * BLK, kvw), F32), jax.ShapeDtypeStruct((1, N_HEADS), F32)],
        semantics=("arbitrary",), args=(sink, qkv, qkv, qkv, qkv, o, do, rc, rs1, rs2), comm=comm)
    return dq, dkc, dvc, dsink, got


def kv_sum(dqkv, dkc, dvc, rc, rs1, rs2):
    nb = dkc.shape[0]
    T = nb * BLK
    kvw = N_KV * HD

    G = 4
    ng = nb // G

    def gather3(own_ref, prev_ref, before_ref, next_ref, after_ref, m):
        has_before = (m > 0).astype(F32)
        has_after = (m < ng - 1).astype(F32)
        out = []
        for i in range(G):
            from_prev = prev_ref[i - 1] if i > 0 else before_ref[0] * has_before
            from_next = next_ref[i + 1] if i < G - 1 else after_ref[0] * has_after
            out.append(from_prev + own_ref[i] + from_next)
        return jnp.concatenate(out, axis=0)

    def body(_, ko, kp, kb, kn, ka, vo, vp, vb, vn, va, c_ref, s1_ref, s2_ref, out_ref):
        m = pl.program_id(0)
        dk = gather3(ko, kp, kb, kn, ka, m)
        dv = gather3(vo, vp, vb, vn, va, m)
        c, s1, s2 = c_ref[...], s1_ref[...], s2_ref[...]
        for j in range(kvw // LANES):
            sl = slice(LANES * j, LANES * (j + 1))
            out_ref[:, sl] = _rope(dk[:, sl], c, -s1, -s2).astype(BF16)
        out_ref[:, kvw:] = dv.astype(BF16)

    own = pl.BlockSpec((G, BLK, kvw), lambda m: (m, 1, 0))
    prev = pl.BlockSpec((G, BLK, kvw), lambda m: (m, 2, 0))
    before = pl.BlockSpec((1, BLK, kvw), lambda m: (jnp.maximum(G * m - 1, 0), 2, 0))
    nxt = pl.BlockSpec((G, BLK, kvw), lambda m: (m, 0, 0))
    after = pl.BlockSpec((1, BLK, kvw), lambda m: (jnp.minimum(G * m + G, nb - 1), 0, 0))
    five = [own, prev, before, nxt, after]
    return pl.pallas_call(
        body, name="kv_sum", grid=(ng,),
        in_specs=[pl.BlockSpec(memory_space=pl.ANY), *five, *five, *_tab_specs(G * BLK)],
        out_specs=pl.BlockSpec((G * BLK, 2 * kvw), lambda m: (m, KV_OFF // (2 * kvw))),
        out_shape=jax.ShapeDtypeStruct((T, QKV), BF16),
        input_output_aliases={0: 0},
        compiler_params=_params("parallel"),
    )(dqkv, *([dkc] * 5), *([dvc] * 5), rc, rs1, rs2)


def _me():
    return lax.axis_index("x"), lax.axis_index("y"), lax.axis_index("c")


def _half_rows(ref, sharded_rows, chip, core):
    R, C = ref.shape[-2], ref.shape[-1]
    lead = (slice(None),) * (len(ref.shape) - 2)
    if sharded_rows:
        per = R // N_CHIPS
        return ref.at[lead + (pl.ds(chip * per + core * (per // 2), per // 2), slice(None))]
    per = C // N_CHIPS
    return ref.at[lead + (pl.ds(core * (R // 2), R // 2), pl.ds(chip * per, per))]


class _Gather:
    def __init__(self, shards, sharded_rows):
        self.inputs = list(shards)
        self.rows = list(sharded_rows)
        self.n = self.n_in = self.n_out = len(shards)
        self.out_shapes = []
        for s, rows in zip(shards, sharded_rows):
            shp = list(s.shape)
            shp[-2 if rows else -1] *= N_CHIPS
            self.out_shapes.append(jax.ShapeDtypeStruct(tuple(shp), s.dtype))
        self.scratch = [pltpu.SemaphoreType.DMA((self.n, 6)), pltpu.SemaphoreType.DMA((self.n, 6)),
                        pltpu.SemaphoreType.DMA((self.n, 2))]

    def _ctx(self, ins, outs, sems):
        send_sems, recv_sems, local_sems = sems
        x, y, c = _me()
        chips = [(1 - x, y), (x, 1 - y), (1 - x, 1 - y)]

        def half_src(w, core):
            s = ins[w]
            R = s.shape[-2]
            return s.at[pl.ds(core * (R // 2), R // 2), :]

        def dst(w, chip, core):
            return _half_rows(outs[w], self.rows[w], chip, core)

        def copy(w, k, src, chip, core, to):
            return pltpu.make_async_remote_copy(
                src_ref=src, dst_ref=dst(w, chip, core), send_sem=send_sems.at[w, k], recv_sem=recv_sems.at[w, k],
                device_id=to, device_id_type=MESH)

        def local(w, core):
            return pltpu.make_async_copy(half_src(w, core), dst(w, 2 * x + y, core), local_sems.at[w, core])

        def first(w, j):
            qx, qy = chips[j]
            return copy(w, j, half_src(w, c), 2 * x + y, c, (qx, qy, c))

        def landed(w, j):
            qx, qy = chips[j]
            return copy(w, j, dst(w, 2 * qx + qy, c), 2 * qx + qy, c, (x, y, c))

        def passed(w, j):
            qx, qy = chips[j]
            return copy(w, 3 + j, dst(w, 2 * qx + qy, c), 2 * qx + qy, c, (x, y, 1 - c))

        def from_sibling(w, j):
            qx, qy = chips[j]
            return copy(w, 3 + j, dst(w, 2 * qx + qy, 1 - c), 2 * qx + qy, 1 - c, (x, y, c))

        return local, first, landed, passed, from_sibling

    def start(self, ins, outs, sems):
        local, first, _, _, _ = self._ctx(ins, outs, sems)
        for w in range(self.n):
            for core in range(2):
                local(w, core).start()
            for j in range(3):
                first(w, j).start()

    def mid(self, ins, outs, sems):
        _, _, landed, passed, _ = self._ctx(ins, outs, sems)
        for w in range(self.n):
            for j in range(3):
                landed(w, j).wait_recv()
                passed(w, j).start()

    def end(self, ins, outs, sems):
        local, first, _, passed, from_sibling = self._ctx(ins, outs, sems)
        for w in range(self.n):
            for j in range(3):
                from_sibling(w, j).wait_recv()
        for w in range(self.n):
            for j in range(3):
                first(w, j).wait_send()
                passed(w, j).wait_send()
            for core in range(2):
                local(w, core).wait()


class _Scatter:
    def __init__(self, grads, small=None):
        self.inputs = list(grads) + ([small] if small is not None else [])
        self.ng = len(grads)
        self.n = self.n_in = self.n_out = len(self.inputs)
        self.out_shapes = [jax.ShapeDtypeStruct((N_DEV, g.shape[1] // 2, g.shape[2]), g.dtype) for g in grads]
        if small is not None:
            self.out_shapes.append(jax.ShapeDtypeStruct((N_DEV,) + small.shape, small.dtype))
        self.scratch = [pltpu.SemaphoreType.DMA((self.n, N_DEV)), pltpu.SemaphoreType.DMA((self.n, N_DEV)),
                        pltpu.SemaphoreType.DMA((self.n,))]

    def _ctx(self, ins, outs, sems):
        send_sems, recv_sems, local_sems = sems
        x, y, c = _me()
        me = 4 * x + 2 * y + c

        def piece(w, chip, core):
            if w >= self.ng:
                return ins[w]
            half = ins[w].shape[1] // 2
            return ins[w].at[chip, pl.ds(core * half, half), :]

        def peer_of(k):
            return x ^ ((k >> 2) & 1), y ^ ((k >> 1) & 1), c ^ (k & 1)

        def local(w):
            return pltpu.make_async_copy(piece(w, 2 * x + y, c), outs[w].at[me], local_sems.at[w])

        def send(w, k):
            px, py, pc = peer_of(k)
            return pltpu.make_async_remote_copy(
                src_ref=piece(w, 2 * px + py, pc), dst_ref=outs[w].at[me], send_sem=send_sems.at[w, k],
                recv_sem=recv_sems.at[w, k], device_id=(px, py, pc), device_id_type=MESH)

        def recv(w, k):
            px, py, pc = peer_of(k)
            return pltpu.make_async_remote_copy(
                src_ref=piece(w, 2 * x + y, c), dst_ref=outs[w].at[4 * px + 2 * py + pc], send_sem=send_sems.at[w, k],
                recv_sem=recv_sems.at[w, k], device_id=(px, py, pc), device_id_type=MESH)

        return local, send, recv

    def start(self, ins, outs, sems):
        local, send, _ = self._ctx(ins, outs, sems)
        for w in range(self.n):
            local(w).start()
            for k in range(1, N_DEV):
                send(w, k).start()

    def mid(self, ins, outs, sems):
        pass

    def end(self, ins, outs, sems):
        local, send, recv = self._ctx(ins, outs, sems)
        for w in range(self.n):
            for k in range(1, N_DEV):
                recv(w, k).wait_recv()
        for w in range(self.n):
            for k in range(1, N_DEV):
                send(w, k).wait_send()
            local(w).wait()


class _Both:
    def __init__(self, a, b):
        self.a, self.b = a, b
        self.inputs = a.inputs + b.inputs
        self.out_shapes = a.out_shapes + b.out_shapes
        self.scratch = a.scratch + b.scratch
        self.n_in, self.n_out = a.n_in + b.n_in, a.n_out + b.n_out

    def _split(self, ins, outs, sems):
        a, na = self.a, len(self.a.scratch)
        return (ins[:a.n_in], outs[:a.n_out], sems[:na]), (ins[a.n_in:], outs[a.n_out:], sems[na:])

    def start(self, ins, outs, sems):
        pa, pb = self._split(ins, outs, sems)
        self.a.start(*pa)
        self.b.start(*pb)

    def mid(self, ins, outs, sems):
        pa, pb = self._split(ins, outs, sems)
        self.a.mid(*pa)
        self.b.mid(*pb)

    def end(self, ins, outs, sems):
        pa, pb = self._split(ins, outs, sems)
        self.a.end(*pa)
        self.b.end(*pb)


def exchange(plan, name):
    def body(*refs):
        ins, outs, sems = refs[:plan.n_in], refs[plan.n_in:plan.n_in + plan.n_out], refs[plan.n_in + plan.n_out:]
        plan.start(ins, outs, sems)
        plan.mid(ins, outs, sems)
        plan.end(ins, outs, sems)

    any_spec = pl.BlockSpec(memory_space=pl.ANY)
    return pl.pallas_call(
        body, name=name, in_specs=[any_spec] * plan.n_in, out_specs=[any_spec] * plan.n_out,
        out_shape=plan.out_shapes, scratch_shapes=plan.scratch,
    )(*plan.inputs)


def _call(body, *, name, grid, in_specs, out_specs, out_shape, scratch_shapes=(), semantics, args, comm=None):
    if comm is None:
        outs = pl.pallas_call(
            body, name=name, grid=grid, in_specs=in_specs, out_specs=out_specs, out_shape=out_shape,
            scratch_shapes=list(scratch_shapes), compiler_params=_params(*semantics))(*args)
        return outs, []
    n_in, n_out, n_scr = len(in_specs), len(out_specs), len(scratch_shapes)

    total = math.prod(grid)
    first, middle, last = 0, (3 * total) // 4 - 1, total - 1
    assert first <= middle < last

    def at(step):
        lin = pl.program_id(0)
        for d in range(1, len(grid)):
            lin = lin * grid[d] + pl.program_id(d)
        return lin == step

    def hosted(*refs):
        h_in, c_in = refs[:n_in], refs[n_in:n_in + comm.n_in]
        rest = refs[n_in + comm.n_in:]
        h_out, c_out = rest[:n_out], rest[n_out:n_out + comm.n_out]
        rest = rest[n_out + comm.n_out:]
        h_scr, c_scr = rest[:n_scr], rest[n_scr:]

        @pl.when(at(first))
        def _():
            comm.start(c_in, c_out, c_scr)

        body(*h_in, *h_out, *h_scr)

        @pl.when(at(middle))
        def _():
            comm.mid(c_in, c_out, c_scr)

        @pl.when(at(last))
        def _():
            comm.end(c_in, c_out, c_scr)

    any_spec = pl.BlockSpec(memory_space=pl.ANY)
    outs = pl.pallas_call(
        hosted, name=name, grid=grid, in_specs=list(in_specs) + [any_spec] * comm.n_in,
        out_specs=list(out_specs) + [any_spec] * comm.n_out, out_shape=list(out_shape) + comm.out_shapes,
        scratch_shapes=list(scratch_shapes) + comm.scratch,
        compiler_params=_params(*(["arbitrary"] * len(grid))))(*args, *comm.inputs)
    return outs[:n_out], outs[n_out:]


def sum_swap(pieces, name, comm=None):
    nl = len(pieces)
    _, r2, cc = pieces[0].shape
    tr = 128 if r2 % 128 == 0 else r2 // 2
    n = r2 // tr

    def body(*refs):
        p_refs, out = refs[:nl], refs[nl]
        slots, send_sems, local_sems, recv_sem = refs[nl + 1:]
        x, y, c = _me()
        sibling = (x, y, 1 - c)
        l, i = pl.program_id(0), pl.program_id(1)
        step = l * n + i

        def rows(st, core):
            return out.at[st // n, pl.ds(core * r2 + (st % n) * tr, tr), :]

        def copies(st):
            slot = st % 2
            local = pltpu.make_async_copy(slots.at[slot], rows(st, c), local_sems.at[slot])
            remote = pltpu.make_async_remote_copy(
                src_ref=slots.at[slot], dst_ref=rows(st, c), send_sem=send_sems.at[slot], recv_sem=recv_sem,
                device_id=sibling, device_id_type=MESH)
            return local, remote

        for ll in range(nl):
            @pl.when(l == ll)
            def _():
                acc = p_refs[ll][0].astype(F32)
                for d in range(1, N_DEV):
                    acc = acc + p_refs[ll][d].astype(F32)
                slots[step % 2] = acc

        for cp in copies(step):
            cp.start()

        @pl.when(step >= 1)
        def _():
            local, remote = copies(step - 1)
            local.wait()
            remote.wait_send()

        @pl.when(step == nl * n - 1)
        def _():
            local, remote = copies(step)
            local.wait()
            remote.wait_send()
            theirs = out.at[:, pl.ds((1 - c) * r2, r2), :]
            pltpu.make_async_remote_copy(src_ref=theirs, dst_ref=theirs, send_sem=send_sems.at[0],
                                         recv_sem=recv_sem, device_id=sibling, device_id_type=MESH).wait_recv()

    def piece_spec(ll):
        def index(l, i):
            return (0, jnp.where(l == ll, i, jnp.where(l < ll, 0, n - 1)), 0)
        return pl.BlockSpec((N_DEV, tr, cc), index)

    (out,), got = _call(
        body, name=name, grid=(nl, n),
        in_specs=[piece_spec(ll) for ll in range(nl)],
        out_specs=[pl.BlockSpec(memory_space=pl.ANY)],
        out_shape=[jax.ShapeDtypeStruct((nl, 2 * r2, cc), F32)],
        scratch_shapes=[pltpu.VMEM((2, tr, cc), F32), pltpu.SemaphoreType.DMA((2,)), pltpu.SemaphoreType.DMA((2,)),
                        pltpu.SemaphoreType.DMA(())],
        semantics=("arbitrary", "arbitrary"), args=tuple(pieces), comm=comm)
    return (out, got) if comm is not None else out


def sum_pieces(pieces, name):
    _, R, C = pieces.shape
    tr = _tile(R, 128) if R % 128 == 0 else R

    def body(p_ref, o_ref):
        acc = p_ref[0].astype(F32)
        for d in range(1, N_DEV):
            acc = acc + p_ref[d].astype(F32)
        o_ref[...] = acc

    return pl.pallas_call(
        body, name=name, grid=(R // tr,),
        in_specs=[pl.BlockSpec((N_DEV, tr, C), lambda i: (0, i, 0))],
        out_specs=pl.BlockSpec((tr, C), lambda i: (i, 0)),
        out_shape=jax.ShapeDtypeStruct((R, C), F32),
        compiler_params=_params("parallel"),
    )(pieces)


def adamw(w, g, m, v, name):
    Lyr, R, C = w.shape
    tr = _tile(R, 256) if R % 8 == 0 else R
    c1 = 1.0 / (1.0 - ADAM_B1 ** ADAM_STEP)
    c2 = 1.0 / (1.0 - ADAM_B2 ** ADAM_STEP)

    def body(w_ref, g_ref, m_ref, v_ref, d_ref, nm_ref, nv_ref):
        gv = g_ref[...]
        nm = ADAM_B1 * m_ref[...] + (1.0 - ADAM_B1) * gv
        nv = ADAM_B2 * v_ref[...] + (1.0 - ADAM_B2) * (gv * gv)
        nm_ref[...] = nm
        nv_ref[...] = nv
        d_ref[...] = -ADAM_LR * ((nm * c1) / (jnp.sqrt(nv * c2) + ADAM_EPS) + ADAM_WD * w_ref[...])

    spec = pl.BlockSpec((None, tr, C), lambda l, i: (l, i, 0))
    shp = jax.ShapeDtypeStruct(w.shape, F32)
    return pl.pallas_call(
        body, name=name, grid=(Lyr, R // tr),
        in_specs=[spec] * 4, out_specs=[spec] * 3, out_shape=[shp] * 3,
        compiler_params=_params("parallel", "parallel"),
    )(w, g, m, v)


def _rope_tables(T):
    pos = jnp.arange(T, dtype=F32)
    inv_freq = THETA ** (-jnp.arange(0, ROT, 2, dtype=F32) / ROT)
    ang = pos[:, None] * inv_freq[None, :]
    cs = jnp.concatenate([jnp.cos(ang), jnp.sin(ang)], axis=1)
    half = ROT // 2
    lane = jnp.arange(3 * LANES)
    table, lm = lane // LANES, lane % HD
    src = jnp.where(table == 0, lm % half, half + lm % half)
    i32 = lambda b: b.astype(jnp.int32)
    sign = jnp.where(table == 0, i32(lm < ROT), jnp.where(table == 1, -i32(lm < half), i32((lm >= half) & (lm < ROT))))
    place = (jnp.arange(ROT)[:, None] == src[None, :]) * sign[None, :].astype(F32)
    ones = ((table == 0) & (lm >= ROT)).astype(F32)
    return jnp.dot(cs, place, precision=lax.Precision.HIGHEST) + ones[None, :]


def _tab_specs(rows):
    return [pl.BlockSpec((rows, LANES), lambda i, k=k: (i, k)) for k in range(3)]


def kernel(x, attn_norm, attn_w_qkv, attn_w_o, attn_sink, conv_norm, conv_w_pw1, conv_b_pw1, conv_w_dw, conv_b_dw, conv_ln_g, conv_ln_b, conv_w_pw2, conv_b_pw2, ffn_norm, ffn_w_gu, ffn_w_down, final_norm, loss_target, m_attn_norm, m_attn_w_qkv, m_attn_w_o, m_attn_sink, m_conv_norm, m_conv_w_pw1, m_conv_b_pw1, m_conv_w_dw, m_conv_b_dw, m_conv_ln_g, m_conv_ln_b, m_conv_w_pw2, m_conv_b_pw2, m_ffn_norm, m_ffn_w_gu, m_ffn_w_down, m_final_norm, v_attn_norm, v_attn_w_qkv, v_attn_w_o, v_attn_sink, v_conv_norm, v_conv_w_pw1, v_conv_b_pw1, v_conv_w_dw, v_conv_b_dw, v_conv_ln_g, v_conv_ln_b, v_conv_w_pw2, v_conv_b_pw2, v_ffn_norm, v_ffn_w_gu, v_ffn_w_down, v_final_norm):
    T = x.shape[1]
    x0 = x[0]
    target = loss_target[0]
    ix, iy = lax.axis_index("x"), lax.axis_index("y")
    chip = 2 * ix + iy
    rc = rs1 = rs2 = _rope_tables(T)

    bf = lambda t: t.astype(BF16)
    col_row = [False, True]

    def place(vec, width):
        return lax.dynamic_update_slice(jnp.zeros((vec.shape[0], N_CHIPS * width), F32), vec, (0, chip * width))

    small_rows = jnp.concatenate([
        place(conv_norm, 256), place(conv_b_pw1, 512).reshape(2, D), place(conv_b_dw, 256), place(conv_ln_g, 256),
        place(conv_ln_b, 256), place(conv_b_pw2, 256), jnp.zeros((1, D), F32),
        place(conv_w_dw[0], 256), jnp.zeros((1, D), F32)], axis=0)
    w_qkv, = exchange(_Gather([bf(attn_w_qkv[0])], [False]), "gather_qkv")

    h0, qkv, (w_o, got) = rms_qkv(x0, attn_norm, w_qkv, rc, rs1, rs2,
                                  comm=_Both(_Gather([bf(attn_w_o[0])], [True]), _Scatter([], small_rows)))
    psmall = sum_pieces(got, "sum_small_params") * 0.5
    p_conv_norm, p_b_pw1 = psmall[0:1], psmall[1:3].reshape(1, 2 * D)
    p_b_dw, p_ln_g, p_ln_b, p_b_pw2 = psmall[3:4], psmall[4:5], psmall[5:6], psmall[6:7]
    p_w_dw = psmall[8:40]
    sink = attn_sink[0]
    o, (w_gu0,) = attn_fwd(qkv, sink, comm=_Gather([bf(ffn_w_gu[0])], [False]))
    zero_b = jnp.zeros((1, D), F32)
    zero_gu = jnp.zeros((1, 2 * DFF), F32)
    x1, h1, gu0, act0, (w_down0, w_pw1, w_pw2) = rms_mm_gate(
        (o, w_o, zero_b, x0), ffn_norm[0:1], w_gu0, zero_gu, DFF, True, BF16, "ffn0_up",
        comm=_Gather([bf(ffn_w_down[0]), bf(conv_w_pw1[0]), bf(conv_w_pw2[0])], [True, False, True]))
    x2, h2, pre, glu, _ = rms_mm_gate((act0, w_down0, zero_b, x1), p_conv_norm, w_pw1, p_b_pw1, D, False, F32,
                                      "conv_pw1")
    dwc, sw, (w_gu1, w_down1) = conv_fwd(glu, p_w_dw, p_b_dw, p_ln_g, p_ln_b,
                                         comm=_Gather([bf(ffn_w_gu[1]), bf(ffn_w_down[1])], col_row))
    x3, h3, gu1, act1, _ = rms_mm_gate((sw, w_pw2, p_b_pw2, x2), ffn_norm[1:2], w_gu1, zero_gu, DFF, True, BF16,
                                       "ffn1_up")
    dx4, loss_part, d_final = mm_res_loss(act1, w_down1, x3, final_norm.reshape(1, D), target)

    dgu1, _ = swiglu_bwd(dx4, w_down1, gu1, "ffn1_down_bwd")
    g_down1 = dw_row(act1, dx4, "ffn1_down_dw")
    dx3, d_ffn1, _ = mm_bt_rmsbwd(dgu1, w_gu1, x3, ffn_norm[1:2], dx4, "ffn1_up_bwd")
    g_gu1 = dw_col(h3, dgu1, "ffn1_up_dw")

    ddwc, d_ln_g, d_ln_b, d_b_pw2 = ln_silu_bwd(dx3, w_pw2, dwc, p_ln_g, p_ln_b)
    g_pw2 = dw_row(sw, dx3, "conv_pw2_dw")
    dpre, d_w_dw, d_b_dw, d_b_pw1, (r_gu1, r_down1) = conv_bwd(ddwc, glu, pre, p_w_dw,
                                                               comm=_Scatter([g_gu1, g_down1]))
    dx2, d_conv_norm, _ = mm_bt_rmsbwd(dpre, w_pw1, x2, p_conv_norm, dx3, "conv_pw1_bwd")
    g_pw1 = dw_col(h2, dpre, "conv_pw1_dw")

    dgu0, (r_pw1, r_pw2) = swiglu_bwd(dx2, w_down0, gu0, "ffn0_down_bwd", comm=_Scatter([g_pw1, g_pw2]))
    g_down0 = dw_row(act0, dx2, "ffn0_down_dw")
    dx1, d_ffn0, _ = mm_bt_rmsbwd(dgu0, w_gu0, x1, ffn_norm[0:1], dx2, "ffn0_up_bwd")
    g_gu0 = dw_col(h1, dgu0, "ffn0_up_dw")

    do = mm_bt(dx1, w_o, "attn_out_bwd")
    g_o = dw_row(o, dx1, "attn_out_dw")
    dq, dkc, dvc, d_sink, (r_gu0, r_down0, r_o) = attn_bwd(qkv, o, do, sink, rc, rs1, rs2,
                                                           comm=_Scatter([g_gu0, g_down0, g_o]))
    dqkv = kv_sum(dq, dkc, dvc, rc, rs1, rs2)[None]
    g_qkv = dw_col(h0, dqkv, "attn_qkv_dw")
    dx0, d_attn_norm, (r_qkv,) = mm_bt_rmsbwd(dqkv, w_qkv, x0, attn_norm, dx1, "attn_qkv_bwd",
                                              comm=_Scatter([g_qkv]))

    pad16 = lambda t: jnp.concatenate([t, jnp.zeros((1, D - t.shape[1]), F32)], axis=1)
    small_g = jnp.concatenate([
        d_attn_norm, pad16(d_sink), d_conv_norm, d_b_pw1.reshape(2, D), d_b_dw, d_ln_g, d_ln_b, d_b_pw2,
        d_ffn0, d_ffn1, d_final, pad16(loss_part), jnp.zeros((3, D), F32), d_w_dw], axis=0)
    gf_gu, (r_small,) = sum_swap([r_gu0, r_gu1], "sum_gu", comm=_Scatter([], small_g))
    gf_down = sum_swap([r_down0, r_down1], "sum_down")
    gf_pw1, gf_pw2 = sum_swap([r_pw1], "sum_pw1"), sum_swap([r_pw2], "sum_pw2")
    gf_qkv, gf_o = sum_swap([r_qkv], "sum_qkv"), sum_swap([r_o], "sum_o")
    gs = sum_pieces(r_small, "sum_small_grads")
    loss = gs[12, 0]

    def take(row0, nrows, width):
        return lax.dynamic_slice(gs, (row0, chip * width), (nrows, width))

    grads = {
        "attn_norm": gs[0:1], "attn_w_qkv": gf_qkv, "attn_w_o": gf_o, "attn_sink": gs[1:2, :N_HEADS],
        "conv_norm": take(2, 1, 256), "conv_w_pw1": gf_pw1,
        "conv_b_pw1": lax.dynamic_slice(gs[3:5].reshape(1, 2 * D), (0, chip * 512), (1, 512)),
        "conv_w_dw": take(16, 32, 256)[None, :CONV_W], "conv_b_dw": take(5, 1, 256), "conv_ln_g": take(6, 1, 256),
        "conv_ln_b": take(7, 1, 256), "conv_w_pw2": gf_pw2, "conv_b_pw2": take(8, 1, 256),
        "ffn_norm": gs[9:11], "ffn_w_gu": gf_gu, "ffn_w_down": gf_down, "final_norm": gs[11],
    }
    weights = dict(attn_norm=attn_norm, attn_w_qkv=attn_w_qkv, attn_w_o=attn_w_o, attn_sink=attn_sink,
                   conv_norm=conv_norm, conv_w_pw1=conv_w_pw1, conv_b_pw1=conv_b_pw1, conv_w_dw=conv_w_dw,
                   conv_b_dw=conv_b_dw, conv_ln_g=conv_ln_g, conv_ln_b=conv_ln_b, conv_w_pw2=conv_w_pw2,
                   conv_b_pw2=conv_b_pw2, ffn_norm=ffn_norm, ffn_w_gu=ffn_w_gu, ffn_w_down=ffn_w_down,
                   final_norm=final_norm)
    m_in = dict(attn_norm=m_attn_norm, attn_w_qkv=m_attn_w_qkv, attn_w_o=m_attn_w_o, attn_sink=m_attn_sink,
                conv_norm=m_conv_norm, conv_w_pw1=m_conv_w_pw1, conv_b_pw1=m_conv_b_pw1, conv_w_dw=m_conv_w_dw,
                conv_b_dw=m_conv_b_dw, conv_ln_g=m_conv_ln_g, conv_ln_b=m_conv_ln_b, conv_w_pw2=m_conv_w_pw2,
                conv_b_pw2=m_conv_b_pw2, ffn_norm=m_ffn_norm, ffn_w_gu=m_ffn_w_gu, ffn_w_down=m_ffn_w_down,
                final_norm=m_final_norm)
    v_in = dict(attn_norm=v_attn_norm, attn_w_qkv=v_attn_w_qkv, attn_w_o=v_attn_w_o, attn_sink=v_attn_sink,
                conv_norm=v_conv_norm, conv_w_pw1=v_conv_w_pw1, conv_b_pw1=v_conv_b_pw1, conv_w_dw=v_conv_w_dw,
                conv_b_dw=v_conv_b_dw, conv_ln_g=v_conv_ln_g, conv_ln_b=v_conv_ln_b, conv_w_pw2=v_conv_w_pw2,
                conv_b_pw2=v_conv_b_pw2, ffn_norm=v_ffn_norm, ffn_w_gu=v_ffn_w_gu, ffn_w_down=v_ffn_w_down,
                final_norm=v_final_norm)
    order = list(weights)
    g_out, d_out, m_out, v_out = [], [], [], []
    for nm in order:
        w = weights[nm]
        shape = w.shape
        as3 = lambda t: t.reshape((1,) * (3 - len(shape)) + shape) if len(shape) < 3 else t.reshape(shape)
        g3 = as3(grads[nm].reshape(shape))
        delta, nm_, nv_ = adamw(as3(w), g3, as3(m_in[nm]), as3(v_in[nm]), "adamw_" + nm)
        g_out.append(g3.reshape(shape))
        d_out.append(delta.reshape(shape))
        m_out.append(nm_.reshape(shape))
        v_out.append(nv_.reshape(shape))
    return (loss, dx0[None], *g_out, *d_out, *m_out, *v_out)
```

```python
import functools
import math

import jax
import jax.numpy as jnp
from jax import lax
from jax.experimental import pallas as pl
from jax.experimental.pallas import tpu as pltpu

F32 = jnp.float32
BF16 = jnp.bfloat16

D = 1024
N_HEADS = 16
N_KV = 4
GROUP = N_HEADS // N_KV
HD = 64
ROT = 16
THETA = 500000.0
BLK = 128
QKV = (N_HEADS + 2 * N_KV) * HD
KV_OFF = N_HEADS * HD
DFF = 2816
CONV_W = 31
CONV_PAD = 15
HALO = 16
CONV_JB = 8
CONV_JB_BWD = 8
EPS = 1e-6
NEG = -1e30
N_CHIPS = 4
N_DEV = 8
LANES = 128
SUBLANES = 8

ADAM_LR, ADAM_B1, ADAM_B2, ADAM_EPS, ADAM_WD, ADAM_STEP = 0.001, 0.9, 0.999, 1e-08, 0.01, 10

VMEM_LIMIT = 56 * 1024 * 1024
MESH = pl.DeviceIdType.MESH


def _params(*sem):
    return pltpu.CompilerParams(dimension_semantics=sem, vmem_limit_bytes=VMEM_LIMIT)


def _tile(n, want):
    if n <= want:
        return n
    for t in range(want, 7, -1):
        if n % t == 0 and t % 8 == 0:
            return t
    return n


MXU_COLS = 256


def _col_chunks(n):
    return [slice(c, min(c + MXU_COLS, n)) for c in range(0, n, MXU_COLS)]


def _sigmoid(v):
    return 1.0 / (1.0 + jnp.exp(-v))


def _rms_fwd(xv, gain):
    r = lax.rsqrt(jnp.mean(xv * xv, axis=-1, keepdims=True) + EPS)
    return xv * r * gain


def _rms_bwd(dh, xv, gain, dres):
    r = lax.rsqrt(jnp.mean(xv * xv, axis=-1, keepdims=True) + EPS)
    xhat = xv * r
    gy = dh * gain
    dx = r * (gy - xhat * jnp.mean(gy * xhat, axis=-1, keepdims=True))
    return dx + dres, dh * xhat


def _rope(blk, c, s1, s2):
    return blk * c + pltpu.roll(blk, LANES - ROT // 2, 1) * s1 + pltpu.roll(blk, ROT // 2, 1) * s2


def _dot(a, b):
    return jnp.dot(a, b, preferred_element_type=F32)


def _dot_tb(a, b):
    return lax.dot_general(a, b, (((1,), (1,)), ((), ())), preferred_element_type=F32)


def _dot_ta(a, b):
    return lax.dot_general(a, b, (((0,), (0,)), ((), ())), preferred_element_type=F32)


def rms_qkv(x, gain, w, rc, rs1, rs2, comm=None):
    T = x.shape[0]
    tm = _tile(T, 512)

    def body(x_ref, g_ref, w_ref, c_ref, s1_ref, s2_ref, h_ref, qkv_ref):
        h = _rms_fwd(x_ref[...], g_ref[...]).astype(BF16)
        h_ref[...] = h
        acc = _dot(h, w_ref[...])
        c, s1, s2 = c_ref[...], s1_ref[...], s2_ref[...]
        n_rot = (KV_OFF + N_KV * HD) // LANES
        for j in range(n_rot):
            sl = slice(LANES * j, LANES * (j + 1))
            roped = _rope(acc[:, sl], c, s1, s2)
            if j < KV_OFF // LANES:
                roped = roped * Q_SCALE
            qkv_ref[:, sl] = roped.astype(BF16)
        qkv_ref[:, n_rot * LANES:] = acc[:, n_rot * LANES:].astype(BF16)

    row = lambda i: (i, 0)
    full = lambda i: (0, 0)
    (h, qkv), got = _call(
        body, name="rms_qkv", grid=(T // tm,),
        in_specs=[pl.BlockSpec((tm, D), row), pl.BlockSpec((1, D), full), pl.BlockSpec((D, QKV), full),
                  *_tab_specs(tm)],
        out_specs=[pl.BlockSpec((tm, D), row), pl.BlockSpec((tm, QKV), row)],
        out_shape=[jax.ShapeDtypeStruct((T, D), BF16), jax.ShapeDtypeStruct((T, QKV), BF16)],
        semantics=("parallel",), args=(x, gain, w, rc, rs1, rs2), comm=comm)
    return h, qkv, got


Q_SCALE = 1.0 / math.sqrt(HD)


def _attn_mask(n, T):
    ci = lax.broadcasted_iota(jnp.int32, (3 * BLK, BLK), 0)
    qi = lax.broadcasted_iota(jnp.int32, (3 * BLK, BLK), 1)
    key_pos = n * BLK - BLK + ci
    return (jnp.abs(ci - BLK - qi) <= BLK) & (key_pos >= 0) & (key_pos < T)


def _kv_padded(kv, first_tile):
    low = lax.broadcasted_iota(jnp.int32, (3 * BLK, LANES), 1) < HD
    zero = jnp.zeros((3 * BLK, LANES), BF16)
    out = {}
    for g in range(N_KV):
        t = kv[:, (first_tile + g // 2) * LANES:(first_tile + g // 2 + 1) * LANES]
        swapped = jnp.concatenate([t[:, HD:], t[:, :HD]], axis=1)
        for p in range(2):
            out[g, p] = jnp.where(low if p == 0 else ~low, t if g % 2 == p else swapped, zero)
    return out


def _softmax_sink(s, valid, sk):
    s = jnp.where(valid, s, NEG)
    m = jnp.maximum(jnp.max(s, axis=0, keepdims=True), sk)
    e = jnp.exp(s - m)
    es = jnp.exp(sk - m)
    inv = 1.0 / (jnp.sum(e, axis=0, keepdims=True) + es)
    return e * inv, es * inv


def _attn_specs(T):
    nb = T // BLK
    kv_blk = 2 * N_KV * HD
    kv_col = KV_OFF // kv_blk
    q_spec = pl.BlockSpec((BLK, KV_OFF), lambda n: (n, 0))
    prev = pl.BlockSpec((BLK, kv_blk), lambda n: (jnp.maximum(n - 1, 0), kv_col))
    own = pl.BlockSpec((BLK, kv_blk), lambda n: (n, kv_col))
    nxt = pl.BlockSpec((BLK, kv_blk), lambda n: (jnp.minimum(n + 1, nb - 1), kv_col))
    return nb, q_spec, prev, own, nxt


def attn_fwd(qkv, sink, comm=None):
    T = qkv.shape[0]
    nb, q_spec, prev, own, nxt = _attn_specs(T)

    def body(sink_ref, q_ref, kp_ref, ko_ref, kn_ref, o_ref):
        valid = _attn_mask(pl.program_id(0), T)
        kv = jnp.concatenate([kp_ref[...], ko_ref[...], kn_ref[...]], axis=0)
        kx, vx = _kv_padded(kv, 0), _kv_padded(kv, 2)
        tile = lambda ref, h: ref[:, (h // 2) * LANES:(h // 2 + 1) * LANES]
        ss = [_dot_tb(kx[h // GROUP, h % 2], tile(q_ref, h)) for h in range(N_HEADS)]
        ps = [_softmax_sink(ss[h], valid, sink_ref[h])[0].astype(BF16) for h in range(N_HEADS)]
        vxt = {k: v.T for k, v in vx.items()}
        for j in range(N_HEADS // 2):
            g = 2 * j // GROUP
            o_t = _dot(vxt[g, 0], ps[2 * j]) + _dot(vxt[g, 1], ps[2 * j + 1])
            o_ref[:, j * LANES:(j + 1) * LANES] = o_t.T.astype(BF16)

    (o,), got = _call(
        body, name="attn_fwd", grid=(nb,),
        in_specs=[pl.BlockSpec(memory_space=pltpu.SMEM), q_spec, prev, own, nxt],
        out_specs=[pl.BlockSpec((BLK, D), lambda n: (n, 0))],
        out_shape=[jax.ShapeDtypeStruct((T, D), BF16)],
        semantics=("parallel",), args=(sink, qkv, qkv, qkv, qkv), comm=comm)
    return o, got


def rms_mm_gate(x, gain, w, bias, H, swiglu, act_dtype, name, comm=None):
    fused = isinstance(x, tuple)
    T = (x[0] if fused else x).shape[0]
    tm = _tile(T, 512)

    def body(*refs):
        if fused:
            a_ref, wp_ref, bp_ref, r_ref, g_ref, w_ref, b_ref, x_ref, h_ref, pre_ref, act_ref = refs
            xv = _dot(a_ref[...], wp_ref[...]) + bp_ref[...] + r_ref[...]
            x_ref[...] = xv
        else:
            x_ref, g_ref, w_ref, b_ref, h_ref, pre_ref, act_ref = refs
            xv = x_ref[...]
        h = _rms_fwd(xv, g_ref[...]).astype(BF16)
        h_ref[...] = h
        for cs in _col_chunks(H):
            cs2 = slice(H + cs.start, H + cs.stop)
            a = _dot(h, w_ref[:, cs]) + b_ref[:, cs]
            b = _dot(h, w_ref[:, cs2]) + b_ref[:, cs2]
            pre_ref[0, :, cs] = a.astype(BF16)
            pre_ref[1, :, cs] = b.astype(BF16)
            if swiglu:
                act = a * _sigmoid(a) * b
            else:
                act = a * _sigmoid(b)
            act_ref[:, cs] = act.astype(act_dtype)

    row = lambda i: (i, 0)
    full = lambda i: (0, 0)
    if fused:
        K = x[0].shape[1]
        x_specs = [pl.BlockSpec((tm, K), row), pl.BlockSpec((K, D), full, pipeline_mode=pl.Buffered(1)),
                   pl.BlockSpec((1, D), full), pl.BlockSpec((tm, D), row)]
        x_out = ([pl.BlockSpec((tm, D), row)], [jax.ShapeDtypeStruct((T, D), F32)])
        x_args = tuple(x)
    else:
        x_specs, x_out, x_args = [pl.BlockSpec((tm, D), row)], ([], []), (x,)
    outs, got = _call(
        body, name=name, grid=(T // tm,),
        in_specs=x_specs + [pl.BlockSpec((1, D), full),
                            pl.BlockSpec((D, 2 * H), full, pipeline_mode=pl.Buffered(1)), pl.BlockSpec((1, 2 * H), full)],
        out_specs=x_out[0] + [pl.BlockSpec((tm, D), row), pl.BlockSpec((2, tm, H), lambda i: (0, i, 0)),
                              pl.BlockSpec((tm, H), row)],
        out_shape=x_out[1] + [jax.ShapeDtypeStruct((T, D), BF16), jax.ShapeDtypeStruct((2, T, H), BF16),
                              jax.ShapeDtypeStruct((T, H), act_dtype)],
        semantics=("parallel",), args=x_args + (gain, w, bias), comm=comm)
    return (*outs, got)


def _conv_tiles(T):
    tt = _tile(T, 512)
    return tt, tt // SUBLANES, D // LANES


def _fill_strided(ext, p, L):
    main = p[HALO:HALO + SUBLANES * L, :].reshape(SUBLANES, L, LANES)
    ext[CONV_PAD:CONV_PAD + L] = jnp.swapaxes(main, 0, 1)

    def ibody(i, carry):
        ext[i] = p[pl.ds(i + 1, SUBLANES, stride=L), :]
        ext[i + CONV_PAD + L] = p[pl.ds(i + CONV_PAD + L + 1, SUBLANES, stride=L), :]
        return carry

    lax.fori_loop(0, CONV_PAD, ibody, 0, unroll=3)


def _conv_specs(T, tt):
    main = pl.BlockSpec((tt, D), lambda i: (i, 0))
    per = tt // HALO
    prev = pl.BlockSpec((HALO, D), lambda i: (jnp.maximum(i * per - 1, 0), 0))
    nxt = pl.BlockSpec((HALO, D), lambda i: (jnp.minimum((i + 1) * per, T // HALO - 1), 0))
    return main, prev, nxt


def _fill_pad(pad, main_ref, prev_ref, next_ref, i, n_i, tt, nlt):
    keep_p = (i > 0).astype(F32)
    keep_n = (i < n_i - 1).astype(F32)
    for lt in range(nlt):
        sl = slice(lt * LANES, (lt + 1) * LANES)
        pad[lt, 0:HALO, :] = prev_ref[:, sl] * keep_p
        pad[lt, HALO:HALO + tt, :] = main_ref[:, sl]
        pad[lt, HALO + tt:2 * HALO + tt, :] = next_ref[:, sl] * keep_n


def conv_fwd(glu, w_dw, b_dw, ln_g, ln_b, comm=None):
    T = glu.shape[0]
    tt, L, nlt = _conv_tiles(T)
    n_i = T // tt
    main, prev, nxt = _conv_specs(T, tt)

    def body(x_ref, xp_ref, xn_ref, w_ref, b_ref, g_ref, bb_ref, dwc_ref, sw_ref, pad, ob, ext, wk):
        i = pl.program_id(0)
        _fill_pad(pad, x_ref, xp_ref, xn_ref, i, n_i, tt, nlt)
        for lt in range(nlt):
            sl = slice(lt * LANES, (lt + 1) * LANES)
            o = ob.at[lt]
            _fill_strided(ext, pad.at[lt], L)
            for k in range(CONV_W):
                wk[k] = jnp.broadcast_to(w_ref[k:k + 1, sl], (SUBLANES, LANES))

            def jbody(jb, carry):
                j = jb * CONV_JB
                accs = [None] * CONV_JB
                for m in range(CONV_W + CONV_JB - 1):
                    e = ext[j + m]
                    for u in range(CONV_JB):
                        if 0 <= m - u < CONV_W:
                            t = e * wk[m - u]
                            accs[u] = t if accs[u] is None else accs[u] + t
                for u in range(CONV_JB):
                    o[pl.ds(j + u, SUBLANES, stride=L), :] = accs[u]
                return carry

            lax.fori_loop(0, L // CONV_JB, jbody, 0)
        y = jnp.concatenate([ob[lt] for lt in range(nlt)], axis=1) + b_ref[...]
        dwc_ref[...] = y
        mu = jnp.mean(y, axis=-1, keepdims=True)
        yc = y - mu
        var = jnp.mean(yc * yc, axis=-1, keepdims=True)
        z = yc * lax.rsqrt(var + EPS) * g_ref[...] + bb_ref[...]
        sw_ref[...] = (z * _sigmoid(z)).astype(BF16)

    full = lambda i: (0, 0)
    (dwc, sw), got = _call(
        body, name="conv_fwd", grid=(n_i,),
        in_specs=[main, prev, nxt, pl.BlockSpec((32, D), full), pl.BlockSpec((1, D), full),
                  pl.BlockSpec((1, D), full), pl.BlockSpec((1, D), full)],
        out_specs=[pl.BlockSpec((tt, D), lambda i: (i, 0)), pl.BlockSpec((tt, D), lambda i: (i, 0))],
        out_shape=[jax.ShapeDtypeStruct((T, D), F32), jax.ShapeDtypeStruct((T, D), BF16)],
        scratch_shapes=[pltpu.VMEM((nlt, tt + 2 * HALO, LANES), F32), pltpu.VMEM((nlt, tt, LANES), F32),
                        pltpu.VMEM((L + 2 * HALO, SUBLANES, LANES), F32), pltpu.VMEM((32, SUBLANES, LANES), F32)],
        semantics=("parallel",), args=(glu, glu, glu, w_dw, b_dw, ln_g, ln_b), comm=comm)
    return dwc, sw, got


def mm_res_loss(a, w, resid, gain, target):
    T, K = a.shape
    tm = _tile(T, 512)

    def body(a_ref, w_ref, r_ref, g_ref, t_ref, dx_ref, loss_ref, dg_ref):
        @pl.when(pl.program_id(0) == 0)
        def _():
            loss_ref[...] = jnp.zeros_like(loss_ref)
            dg_ref[...] = jnp.zeros_like(dg_ref)

        xv, gain_v = _dot(a_ref[...], w_ref[...]) + r_ref[...], g_ref[...]
        err = _rms_fwd(xv, gain_v) - t_ref[...]
        part = 0.5 * jnp.sum(jnp.mean(err * err, axis=-1, keepdims=True), axis=0, keepdims=True)
        loss_ref[...] += jnp.broadcast_to(part, loss_ref.shape)
        dx, dgr = _rms_bwd(err * (1.0 / D), xv, gain_v, 0.0)
        dx_ref[...] = dx
        dg_ref[...] += jnp.sum(dgr, axis=0, keepdims=True)

    row = lambda i: (i, 0)
    full = lambda i: (0, 0)
    return pl.pallas_call(
        body, name="ffn1_down_loss", grid=(T // tm,),
        in_specs=[pl.BlockSpec((tm, K), row), pl.BlockSpec((K, D), full), pl.BlockSpec((tm, D), row),
                  pl.BlockSpec((1, D), full), pl.BlockSpec((tm, D), row)],
        out_specs=[pl.BlockSpec((tm, D), row), pl.BlockSpec((1, LANES), full), pl.BlockSpec((1, D), full)],
        out_shape=[jax.ShapeDtypeStruct((T, D), F32), jax.ShapeDtypeStruct((1, LANES), F32),
                   jax.ShapeDtypeStruct((1, D), F32)],
        compiler_params=_params("arbitrary"),
    )(a, w, resid, gain, target)


def swiglu_bwd(dx, w_down, pre, name, comm=None):
    T = dx.shape[0]
    H = w_down.shape[0]
    tm = _tile(T, 512)

    def body(dx_ref, w_ref, pre_ref, dpre_ref):
        dxb = dx_ref[...].astype(BF16)
        for cs in _col_chunks(H):
            dact = _dot_tb(dxb, w_ref[cs, :])
            g = pre_ref[0, :, cs].astype(F32)
            u = pre_ref[1, :, cs].astype(F32)
            sg = _sigmoid(g)
            dpre_ref[0, :, cs] = (dact * u * sg * (1.0 + g * (1.0 - sg))).astype(BF16)
            dpre_ref[1, :, cs] = (dact * g * sg).astype(BF16)

    (dpre,), got = _call(
        body, name=name, grid=(T // tm,),
        in_specs=[pl.BlockSpec((tm, D), lambda i: (i, 0)),
                  pl.BlockSpec((H, D), lambda i: (0, 0), pipeline_mode=pl.Buffered(1)),
                  pl.BlockSpec((2, tm, H), lambda i: (0, i, 0))],
        out_specs=[pl.BlockSpec((2, tm, H), lambda i: (0, i, 0))],
        out_shape=[jax.ShapeDtypeStruct((2, T, H), BF16)],
        semantics=("parallel",), args=(dx, w_down, pre), comm=comm)
    return dpre, got


def _ln_silu_bwd(dsw, y, ln_g, ln_b):
    mu = jnp.mean(y, axis=-1, keepdims=True)
    yc = y - mu
    rstd = lax.rsqrt(jnp.mean(yc * yc, axis=-1, keepdims=True) + EPS)
    xhat = yc * rstd
    z = xhat * ln_g + ln_b
    sg = _sigmoid(z)
    dz = dsw * sg * (1.0 + z * (1.0 - sg))
    dxh = dz * ln_g
    dy = rstd * (dxh - jnp.mean(dxh, axis=-1, keepdims=True) - xhat * jnp.mean(dxh * xhat, axis=-1, keepdims=True))
    return dy, dz * xhat, dz


def mm_bt_rmsbwd(dpre, w, x, gain, dres, name, comm=None, proj_w=None, conv_tail=None):
    nh, T, H = dpre.shape
    tm = _tile(T, 512)
    n_extra_in = 1 if proj_w is not None else (4 if conv_tail is not None else 0)

    def body(*refs):
        dp_ref, w_ref, x_ref, g_ref, dres_ref = refs[:5]
        extra_in = refs[5:5 + n_extra_in]
        dx_ref, dg_ref = refs[5 + n_extra_in:7 + n_extra_in]
        extra_out = refs[7 + n_extra_in:]

        @pl.when(pl.program_id(0) == 0)
        def _():
            dg_ref[...] = jnp.zeros_like(dg_ref)
            for r in extra_out[1:]:
                r[...] = jnp.zeros_like(r)

        dh = _dot_tb(dp_ref[0], w_ref[:, 0:H])
        for hf in range(1, nh):
            dh = dh + _dot_tb(dp_ref[hf], w_ref[:, hf * H:(hf + 1) * H])
        dx, dgr = _rms_bwd(dh, x_ref[...], g_ref[...], dres_ref[...])
        dx_ref[...] = dx
        dg_ref[...] += jnp.sum(dgr, axis=0, keepdims=True)
        if proj_w is not None:
            extra_out[0][...] = _dot_tb(dx.astype(BF16), extra_in[0][...]).astype(BF16)
        elif conv_tail is not None:
            wt_ref, y_ref, lg_ref, lb_ref = extra_in
            dy, dgl, dbl = _ln_silu_bwd(_dot_tb(dx.astype(BF16), wt_ref[...]), y_ref[...], lg_ref[...], lb_ref[...])
            extra_out[0][...] = dy
            extra_out[1][...] += jnp.sum(dgl, axis=0, keepdims=True)
            extra_out[2][...] += jnp.sum(dbl, axis=0, keepdims=True)
            extra_out[3][...] += jnp.sum(dx, axis=0, keepdims=True)

    row = lambda i: (i, 0)
    full = lambda i: (0, 0)
    vec = pl.BlockSpec((1, D), full)
    vec_shape = jax.ShapeDtypeStruct((1, D), F32)
    in_specs = [pl.BlockSpec((nh, tm, H), lambda i: (0, i, 0)),
                pl.BlockSpec((D, nh * H), full, pipeline_mode=pl.Buffered(1)),
                pl.BlockSpec((tm, D), row), vec, pl.BlockSpec((tm, D), row)]
    out_specs = [pl.BlockSpec((tm, D), row), vec]
    out_shape = [jax.ShapeDtypeStruct((T, D), F32), vec_shape]
    args = (dpre, w, x, gain, dres)
    if proj_w is not None:
        N = proj_w.shape[0]
        in_specs.append(pl.BlockSpec((N, D), full, pipeline_mode=pl.Buffered(1)))
        out_specs.append(pl.BlockSpec((tm, N), row))
        out_shape.append(jax.ShapeDtypeStruct((T, N), BF16))
        args += (proj_w,)
    elif conv_tail is not None:
        in_specs += [pl.BlockSpec((D, D), full, pipeline_mode=pl.Buffered(1)), pl.BlockSpec((tm, D), row), vec, vec]
        out_specs += [pl.BlockSpec((tm, D), row), vec, vec, vec]
        out_shape += [jax.ShapeDtypeStruct((T, D), F32), vec_shape, vec_shape, vec_shape]
        args += tuple(conv_tail)
    outs, got = _call(body, name=name, grid=(T // tm,), in_specs=in_specs, out_specs=out_specs, out_shape=out_shape,
                      semantics=("arbitrary",), args=args, comm=comm)
    return (*outs, got)


def dw_col(a, dpre, name):
    T = a.shape[0]
    nh, _, H = dpre.shape
    per = nh * H // N_CHIPS
    bph = N_CHIPS // nh
    tt = _tile(T, 2048)
    nt = T // tt

    def body(a_ref, b_ref, o_ref, acc):
        t = pl.program_id(1)

        @pl.when(t == 0)
        def _():
            acc[...] = jnp.zeros_like(acc)

        acc[...] += _dot_ta(a_ref[...], b_ref[...])

        @pl.when(t == nt - 1)
        def _():
            o_ref[...] = acc[...].astype(BF16)

    return pl.pallas_call(
        body, name=name, grid=(N_CHIPS, nt),
        in_specs=[pl.BlockSpec((tt, D), lambda q, t: (t, 0)),
                  pl.BlockSpec((None, tt, per), lambda q, t: (q // bph, t, q % bph))],
        out_specs=pl.BlockSpec((None, D, per), lambda q, t: (q, 0, 0)),
        out_shape=jax.ShapeDtypeStruct((N_CHIPS, D, per), BF16),
        scratch_shapes=[pltpu.VMEM((D, per), F32)],
        compiler_params=_params("parallel", "arbitrary"),
    )(a, dpre)


def dw_row(a, b, name):
    T, R = a.shape
    cw = 1408 if R % 1408 == 0 else R
    tt = _tile(T, 1024)
    nt = T // tt

    def body(a_ref, b_ref, o_ref, acc):
        t = pl.program_id(1)

        @pl.when(t == 0)
        def _():
            acc[...] = jnp.zeros_like(acc)

        acc[...] += _dot_ta(a_ref[...], b_ref[...].astype(BF16))

        @pl.when(t == nt - 1)
        def _():
            o_ref[...] = acc[...].astype(BF16)

    out = pl.pallas_call(
        body, name=name, grid=(R // cw, nt),
        in_specs=[pl.BlockSpec((tt, cw), lambda q, t: (t, q)), pl.BlockSpec((tt, D), lambda q, t: (t, 0))],
        out_specs=pl.BlockSpec((cw, D), lambda q, t: (q, 0)),
        out_shape=jax.ShapeDtypeStruct((R, D), BF16),
        scratch_shapes=[pltpu.VMEM((cw, D), F32)],
        compiler_params=_params("parallel", "arbitrary"),
    )(a, b)
    return out.reshape(N_CHIPS, R // N_CHIPS, D)


def conv_bwd(ddwc, glu, pre, w_dw, comm=None):
    T = ddwc.shape[0]
    tt, L, nlt = _conv_tiles(T)
    n_i = T // tt
    main, prev, nxt = _conv_specs(T, tt)

    def body(d_ref, dp_ref, dn_ref, x_ref, xp_ref, xn_ref, pre_ref, w_ref,
             dpre_ref, dw_ref, dbd_ref, dbp_ref, padd, padx, ob, extd, extx, wk):
        i = pl.program_id(0)

        @pl.when(i == 0)
        def _():
            dw_ref[...] = jnp.zeros_like(dw_ref)
            dbd_ref[...] = jnp.zeros_like(dbd_ref)
            dbp_ref[...] = jnp.zeros_like(dbp_ref)

        _fill_pad(padd, d_ref, dp_ref, dn_ref, i, n_i, tt, nlt)
        _fill_pad(padx, x_ref, xp_ref, xn_ref, i, n_i, tt, nlt)
        for lt in range(nlt):
            sl = slice(lt * LANES, (lt + 1) * LANES)
            o = ob.at[lt]
            _fill_strided(extd, padd.at[lt], L)
            _fill_strided(extx, padx.at[lt], L)
            for k in range(CONV_W):
                wk[k] = jnp.broadcast_to(w_ref[k:k + 1, sl], (SUBLANES, LANES))

            nu = CONV_JB_BWD

            def jbody(jb, accs):
                j = jb * nu
                accs = list(accs)
                d = [extd[j + u + CONV_PAD] for u in range(nu)]
                g = [None] * nu
                for m in range(CONV_W + nu - 1):
                    ed = extd[j + 2 * CONV_PAD + nu - 1 - m]
                    ex = extx[j + m]
                    for u in range(nu):
                        k = m - (nu - 1 - u)
                        if 0 <= k < CONV_W:
                            t = ed * wk[k]
                            g[u] = t if g[u] is None else g[u] + t
                        k = m - u
                        if 0 <= k < CONV_W:
                            accs[k] = accs[k] + d[u] * ex
                for u in range(nu):
                    o[pl.ds(j + u, SUBLANES, stride=L), :] = g[u]
                return tuple(accs)

            accs = lax.fori_loop(0, L // nu, jbody, tuple(jnp.zeros((SUBLANES, LANES), F32) for _ in range(CONV_W)))
            for k in range(CONV_W):
                dw_ref[k:k + 1, sl] += jnp.sum(accs[k], axis=0, keepdims=True)
        dglu = jnp.concatenate([ob[lt] for lt in range(nlt)], axis=1)
        a = pre_ref[0].astype(F32)
        gate = pre_ref[1].astype(F32)
        sg = _sigmoid(gate)
        da = dglu * sg
        dgate = dglu * a * sg * (1.0 - sg)
        dpre_ref[0] = da.astype(BF16)
        dpre_ref[1] = dgate.astype(BF16)
        dbd_ref[...] += jnp.sum(d_ref[...], axis=0, keepdims=True)
        dbp_ref[0] += jnp.sum(da, axis=0, keepdims=True)
        dbp_ref[1] += jnp.sum(dgate, axis=0, keepdims=True)

    full = lambda i: (0, 0)
    (dpre, dw, dbd, dbp), got = _call(
        body, name="conv_bwd", grid=(n_i,),
        in_specs=[main, prev, nxt, main, prev, nxt, pl.BlockSpec((2, tt, D), lambda i: (0, i, 0)),
                  pl.BlockSpec((32, D), full)],
        out_specs=[pl.BlockSpec((2, tt, D), lambda i: (0, i, 0)), pl.BlockSpec((32, D), full),
                   pl.BlockSpec((1, D), full), pl.BlockSpec((2, 1, D), lambda i: (0, 0, 0))],
        out_shape=[jax.ShapeDtypeStruct((2, T, D), BF16), jax.ShapeDtypeStruct((32, D), F32),
                   jax.ShapeDtypeStruct((1, D), F32), jax.ShapeDtypeStruct((2, 1, D), F32)],
        scratch_shapes=[pltpu.VMEM((nlt, tt + 2 * HALO, LANES), F32), pltpu.VMEM((nlt, tt + 2 * HALO, LANES), F32),
                        pltpu.VMEM((nlt, tt, LANES), F32), pltpu.VMEM((L + 2 * HALO, SUBLANES, LANES), F32),
                        pltpu.VMEM((L + 2 * HALO, SUBLANES, LANES), F32), pltpu.VMEM((32, SUBLANES, LANES), F32)],
        semantics=("arbitrary",), args=(ddwc, ddwc, ddwc, glu, glu, glu, pre, w_dw), comm=comm)
    return dpre, dw, dbd, dbp, got


def attn_bwd(qkv, o, do, sink, rc, rs1, rs2, comm=None):
    T = qkv.shape[0]
    nb, q_spec, prev, own, nxt = _attn_specs(T)
    kvw = N_KV * HD

    def body(sink_ref, q_ref, kp_ref, ko_ref, kn_ref, o_ref, do_ref, c_ref, s1_ref, s2_ref,
             dq_ref, dkc_ref, dvc_ref, dsink_ref):
        n = pl.program_id(0)

        @pl.when(n == 0)
        def _():
            dsink_ref[...] = jnp.zeros_like(dsink_ref)

        valid = _attn_mask(n, T)
        kv = jnp.concatenate([kp_ref[...], ko_ref[...], kn_ref[...]], axis=0)
        kx, vx = _kv_padded(kv, 0), _kv_padded(kv, 2)
        tile = lambda ref, j: ref[:, j * LANES:(j + 1) * LANES]
        ss = [_dot_tb(kx[h // GROUP, h % 2], tile(q_ref, h // 2)) for h in range(N_HEADS)]
        dps = [_dot_tb(vx[h // GROUP, h % 2], tile(do_ref, h // 2)) for h in range(N_HEADS)]
        low_d = lax.broadcasted_iota(jnp.int32, (LANES, BLK), 0) < HD
        deltas = []
        for j in range(N_HEADS // 2):
            prod_t = tile(do_ref, j).astype(F32).T * tile(o_ref, j).astype(F32).T
            deltas.append(jnp.sum(jnp.where(low_d, prod_t, 0.0), axis=0, keepdims=True))
            deltas.append(jnp.sum(jnp.where(low_d, 0.0, prod_t), axis=0, keepdims=True))
        lane = lax.broadcasted_iota(jnp.int32, (1, N_HEADS), 1)
        dsink = jnp.zeros((1, N_HEADS), F32)
        pbs, dss = [], []
        for h in range(N_HEADS):
            p, p_sink = _softmax_sink(ss[h], valid, sink_ref[h])
            dss.append((p * (dps[h] - deltas[h])).astype(BF16))
            pbs.append(p.astype(BF16))
            part = -jnp.sum(p_sink * deltas[h], axis=1, keepdims=True)
            dsink = dsink + jnp.where(lane == h, part, 0.0)
        dsink_ref[...] += dsink
        c, s1, s2 = c_ref[...], s1_ref[...], s2_ref[...]
        kxt = {k: v.T for k, v in kx.items()}
        for j in range(N_HEADS // 2):
            g = 2 * j // GROUP
            dq_t = _dot(kxt[g, 0], dss[2 * j]) + _dot(kxt[g, 1], dss[2 * j + 1])
            dq_ref[:, j * LANES:(j + 1) * LANES] = (_rope(dq_t.T, c, -s1, -s2) * Q_SCALE).astype(BF16)
        low_k = lax.broadcasted_iota(jnp.int32, (3 * BLK, LANES), 1) < HD
        cols = lambda xs, g, p: jnp.concatenate([xs[GROUP * g + p], xs[GROUP * g + 2 + p]], axis=1)
        for t in range(N_KV // 2):
            sums = {}
            for g in (2 * t, 2 * t + 1):
                q2 = jnp.concatenate([tile(q_ref, 2 * g), tile(q_ref, 2 * g + 1)], axis=0)
                do2 = jnp.concatenate([tile(do_ref, 2 * g), tile(do_ref, 2 * g + 1)], axis=0)
                for p in range(2):
                    sums[g, p] = (_dot(cols(dss, g, p), q2), _dot(cols(pbs, g, p), do2))
            for which, ref in ((0, dkc_ref), (1, dvc_ref)):
                keep = jnp.where(low_k, sums[2 * t, 0][which], sums[2 * t + 1, 1][which])
                swap = jnp.where(low_k, sums[2 * t + 1, 0][which], sums[2 * t, 1][which])
                ref[:, t * LANES:(t + 1) * LANES] = keep + pltpu.roll(swap, HD, 1)

    row = lambda n: (n, 0)
    (dq, dkc, dvc, dsink), got = _call(
        body, name="attn_bwd", grid=(nb,),
        in_specs=[pl.BlockSpec(memory_space=pltpu.SMEM), q_spec, prev, own, nxt,
                  pl.BlockSpec((BLK, D), row), pl.BlockSpec((BLK, D), row), *_tab_specs(BLK)],
        out_specs=[pl.BlockSpec((BLK, D), row), pl.BlockSpec((None, 3 * BLK, kvw), lambda n: (n, 0, 0)),
                   pl.BlockSpec((None, 3 * BLK, kvw), lambda n: (n, 0, 0)), pl.BlockSpec((1, N_HEADS), lambda n: (0, 0))],
        out_shape=[jax.ShapeDtypeStruct((T, QKV), BF16), jax.ShapeDtypeStruct((nb, 3 * BLK, kvw), F32),
                   jax.ShapeDtypeStruct((nb, 3 * BLK, kvw), F32), jax.ShapeDtypeStruct((1, N_HEADS), F32)],
        semantics=("arbitrary",), args=(sink, qkv, qkv, qkv, qkv, o, do, rc, rs1, rs2), comm=comm)
    return dq, dkc, dvc, dsink, got


def kv_sum(dqkv, dkc, dvc, rc, rs1, rs2):
    nb = dkc.shape[0]
    T = nb * BLK
    kvw = N_KV * HD

    G = 4
    ng = nb // G

    def gather3(own_ref, prev_ref, before_ref, next_ref, after_ref, m):
        has_before = (m > 0).astype(F32)
        has_after = (m < ng - 1).astype(F32)
        out = []
        for i in range(G):
            from_prev = prev_ref[i - 1] if i > 0 else before_ref[0] * has_before
            from_next = next_ref[i + 1] if i < G - 1 else after_ref[0] * has_after
            out.append(from_prev + own_ref[i] + from_next)
        return jnp.concatenate(out, axis=0)

    def body(_, ko, kp, kb, kn, ka, vo, vp, vb, vn, va, c_ref, s1_ref, s2_ref, out_ref):
        m = pl.program_id(0)
        dk = gather3(ko, kp, kb, kn, ka, m)
        dv = gather3(vo, vp, vb, vn, va, m)
        c, s1, s2 = c_ref[...], s1_ref[...], s2_ref[...]
        for j in range(kvw // LANES):
            sl = slice(LANES * j, LANES * (j + 1))
            out_ref[:, sl] = _rope(dk[:, sl], c, -s1, -s2).astype(BF16)
        out_ref[:, kvw:] = dv.astype(BF16)

    own = pl.BlockSpec((G, BLK, kvw), lambda m: (m, 1, 0))
    prev = pl.BlockSpec((G, BLK, kvw), lambda m: (m, 2, 0))
    before = pl.BlockSpec((1, BLK, kvw), lambda m: (jnp.maximum(G * m - 1, 0), 2, 0))
    nxt = pl.BlockSpec((G, BLK, kvw), lambda m: (m, 0, 0))
    after = pl.BlockSpec((1, BLK, kvw), lambda m: (jnp.minimum(G * m + G, nb - 1), 0, 0))
    five = [own, prev, before, nxt, after]
    return pl.pallas_call(
        body, name="kv_sum", grid=(ng,),
        in_specs=[pl.BlockSpec(memory_space=pl.ANY), *five, *five, *_tab_specs(G * BLK)],
        out_specs=pl.BlockSpec((G * BLK, 2 * kvw), lambda m: (m, KV_OFF // (2 * kvw))),
        out_shape=jax.ShapeDtypeStruct((T, QKV), BF16),
        input_output_aliases={0: 0},
        compiler_params=_params("parallel"),
    )(dqkv, *([dkc] * 5), *([dvc] * 5), rc, rs1, rs2)


def _me():
    return lax.axis_index("x"), lax.axis_index("y"), lax.axis_index("c")


def _half_rows(ref, sharded_rows, chip, core):
    R, C = ref.shape[-2], ref.shape[-1]
    lead = (slice(None),) * (len(ref.shape) - 2)
    if sharded_rows:
        per = R // N_CHIPS
        return ref.at[lead + (pl.ds(chip * per + core * (per // 2), per // 2), slice(None))]
    per = C // N_CHIPS
    return ref.at[lead + (pl.ds(core * (R // 2), R // 2), pl.ds(chip * per, per))]


class _Gather:
    def __init__(self, shards, sharded_rows):
        self.inputs = list(shards)
        self.rows = list(sharded_rows)
        self.n = self.n_in = self.n_out = len(shards)
        self.out_shapes = []
        for s, rows in zip(shards, sharded_rows):
            shp = list(s.shape)
            shp[-2 if rows else -1] *= N_CHIPS
            self.out_shapes.append(jax.ShapeDtypeStruct(tuple(shp), s.dtype))
        self.scratch = [pltpu.SemaphoreType.DMA((self.n, 6)), pltpu.SemaphoreType.DMA((self.n, 6)),
                        pltpu.SemaphoreType.DMA((self.n, 2))]

    def _ctx(self, ins, outs, sems):
        send_sems, recv_sems, local_sems = sems
        x, y, c = _me()
        chips = [(1 - x, y), (x, 1 - y), (1 - x, 1 - y)]

        def half_src(w, core):
            s = ins[w]
            R = s.shape[-2]
            return s.at[pl.ds(core * (R // 2), R // 2), :]

        def dst(w, chip, core):
            return _half_rows(outs[w], self.rows[w], chip, core)

        def copy(w, k, src, chip, core, to):
            return pltpu.make_async_remote_copy(
                src_ref=src, dst_ref=dst(w, chip, core), send_sem=send_sems.at[w, k], recv_sem=recv_sems.at[w, k],
                device_id=to, device_id_type=MESH)

        def local(w, core):
            return pltpu.make_async_copy(half_src(w, core), dst(w, 2 * x + y, core), local_sems.at[w, core])

        def first(w, j):
            qx, qy = chips[j]
            return copy(w, j, half_src(w, c), 2 * x + y, c, (qx, qy, c))

        def landed(w, j):
            qx, qy = chips[j]
            return copy(w, j, dst(w, 2 * qx + qy, c), 2 * qx + qy, c, (x, y, c))

        def passed(w, j):
            qx, qy = chips[j]
            return copy(w, 3 + j, dst(w, 2 * qx + qy, c), 2 * qx + qy, c, (x, y, 1 - c))

        def from_sibling(w, j):
            qx, qy = chips[j]
            return copy(w, 3 + j, dst(w, 2 * qx + qy, 1 - c), 2 * qx + qy, 1 - c, (x, y, c))

        return local, first, landed, passed, from_sibling

    def start(self, ins, outs, sems):
        local, first, _, _, _ = self._ctx(ins, outs, sems)
        for w in range(self.n):
            for core in range(2):
                local(w, core).start()
            for j in range(3):
                first(w, j).start()

    def mid(self, ins, outs, sems):
        _, _, landed, passed, _ = self._ctx(ins, outs, sems)
        for w in range(self.n):
            for j in range(3):
                landed(w, j).wait_recv()
                passed(w, j).start()

    def end(self, ins, outs, sems):
        local, first, _, passed, from_sibling = self._ctx(ins, outs, sems)
        for w in range(self.n):
            for j in range(3):
                from_sibling(w, j).wait_recv()
        for w in range(self.n):
            for j in range(3):
                first(w, j).wait_send()
                passed(w, j).wait_send()
            for core in range(2):
                local(w, core).wait()


class _Scatter:
    def __init__(self, grads, small=None):
        self.inputs = list(grads) + ([small] if small is not None else [])
        self.ng = len(grads)
        self.n = self.n_in = self.n_out = len(self.inputs)
        self.out_shapes = [jax.ShapeDtypeStruct((N_DEV, g.shape[1] // 2, g.shape[2]), g.dtype) for g in grads]
        if small is not None:
            self.out_shapes.append(jax.ShapeDtypeStruct((N_DEV,) + small.shape, small.dtype))
        self.scratch = [pltpu.SemaphoreType.DMA((self.n, N_DEV)), pltpu.SemaphoreType.DMA((self.n, N_DEV)),
                        pltpu.SemaphoreType.DMA((self.n,))]

    def _ctx(self, ins, outs, sems):
        send_sems, recv_sems, local_sems = sems
        x, y, c = _me()
        me = 4 * x + 2 * y + c

        def piece(w, chip, core):
            if w >= self.ng:
                return ins[w]
            half = ins[w].shape[1] // 2
            return ins[w].at[chip, pl.ds(core * half, half), :]

        def peer_of(k):
            return x ^ ((k >> 2) & 1), y ^ ((k >> 1) & 1), c ^ (k & 1)

        def local(w):
            return pltpu.make_async_copy(piece(w, 2 * x + y, c), outs[w].at[me], local_sems.at[w])

        def send(w, k):
            px, py, pc = peer_of(k)
            return pltpu.make_async_remote_copy(
                src_ref=piece(w, 2 * px + py, pc), dst_ref=outs[w].at[me], send_sem=send_sems.at[w, k],
                recv_sem=recv_sems.at[w, k], device_id=(px, py, pc), device_id_type=MESH)

        def recv(w, k):
            px, py, pc = peer_of(k)
            return pltpu.make_async_remote_copy(
                src_ref=piece(w, 2 * x + y, c), dst_ref=outs[w].at[4 * px + 2 * py + pc], send_sem=send_sems.at[w, k],
                recv_sem=recv_sems.at[w, k], device_id=(px, py, pc), device_id_type=MESH)

        return local, send, recv

    def start(self, ins, outs, sems):
        local, send, _ = self._ctx(ins, outs, sems)
        for w in range(self.n):
            local(w).start()
            for k in range(1, N_DEV):
                send(w, k).start()

    def mid(self, ins, outs, sems):
        pass

    def end(self, ins, outs, sems):
        local, send, recv = self._ctx(ins, outs, sems)
        for w in range(self.n):
            for k in range(1, N_DEV):
                recv(w, k).wait_recv()
        for w in range(self.n):
            for k in range(1, N_DEV):
                send(w, k).wait_send()
            local(w).wait()


class _Both:
    def __init__(self, a, b):
        self.a, self.b = a, b
        self.inputs = a.inputs + b.inputs
        self.out_shapes = a.out_shapes + b.out_shapes
        self.scratch = a.scratch + b.scratch
        self.n_in, self.n_out = a.n_in + b.n_in, a.n_out + b.n_out

    def _split(self, ins, outs, sems):
        a, na = self.a, len(self.a.scratch)
        return (ins[:a.n_in], outs[:a.n_out], sems[:na]), (ins[a.n_in:], outs[a.n_out:], sems[na:])

    def start(self, ins, outs, sems):
        pa, pb = self._split(ins, outs, sems)
        self.a.start(*pa)
        self.b.start(*pb)

    def mid(self, ins, outs, sems):
        pa, pb = self._split(ins, outs, sems)
        self.a.mid(*pa)
        self.b.mid(*pb)

    def end(self, ins, outs, sems):
        pa, pb = self._split(ins, outs, sems)
        self.a.end(*pa)
        self.b.end(*pb)


def exchange(plan, name):
    def body(*refs):
        ins, outs, sems = refs[:plan.n_in], refs[plan.n_in:plan.n_in + plan.n_out], refs[plan.n_in + plan.n_out:]
        plan.start(ins, outs, sems)
        plan.mid(ins, outs, sems)
        plan.end(ins, outs, sems)

    any_spec = pl.BlockSpec(memory_space=pl.ANY)
    return pl.pallas_call(
        body, name=name, in_specs=[any_spec] * plan.n_in, out_specs=[any_spec] * plan.n_out,
        out_shape=plan.out_shapes, scratch_shapes=plan.scratch,
    )(*plan.inputs)


def _call(body, *, name, grid, in_specs, out_specs, out_shape, scratch_shapes=(), semantics, args, comm=None):
    if comm is None:
        outs = pl.pallas_call(
            body, name=name, grid=grid, in_specs=in_specs, out_specs=out_specs, out_shape=out_shape,
            scratch_shapes=list(scratch_shapes), compiler_params=_params(*semantics))(*args)
        return outs, []
    n_in, n_out, n_scr = len(in_specs), len(out_specs), len(scratch_shapes)

    total = math.prod(grid)
    first, middle, last = 0, (3 * total) // 4 - 1, total - 1
    assert first <= middle < last

    def at(step):
        lin = pl.program_id(0)
        for d in range(1, len(grid)):
            lin = lin * grid[d] + pl.program_id(d)
        return lin == step

    def hosted(*refs):
        h_in, c_in = refs[:n_in], refs[n_in:n_in + comm.n_in]
        rest = refs[n_in + comm.n_in:]
        h_out, c_out = rest[:n_out], rest[n_out:n_out + comm.n_out]
        rest = rest[n_out + comm.n_out:]
        h_scr, c_scr = rest[:n_scr], rest[n_scr:]

        @pl.when(at(first))
        def _():
            comm.start(c_in, c_out, c_scr)

        body(*h_in, *h_out, *h_scr)

        @pl.when(at(middle))
        def _():
            comm.mid(c_in, c_out, c_scr)

        @pl.when(at(last))
        def _():
            comm.end(c_in, c_out, c_scr)

    any_spec = pl.BlockSpec(memory_space=pl.ANY)
    outs = pl.pallas_call(
        hosted, name=name, grid=grid, in_specs=list(in_specs) + [any_spec] * comm.n_in,
        out_specs=list(out_specs) + [any_spec] * comm.n_out, out_shape=list(out_shape) + comm.out_shapes,
        scratch_shapes=list(scratch_shapes) + comm.scratch,
        compiler_params=_params(*(["arbitrary"] * len(grid))))(*args, *comm.inputs)
    return outs[:n_out], outs[n_out:]


def sum_swap(pieces, name, comm=None):
    nl = len(pieces)
    _, r2, cc = pieces[0].shape
    tr = 128 if r2 % 128 == 0 else r2 // 2
    n = r2 // tr

    def body(*refs):
        p_refs, out = refs[:nl], refs[nl]
        slots, send_sems, local_sems, recv_sem = refs[nl + 1:]
        x, y, c = _me()
        sibling = (x, y, 1 - c)
        l, i = pl.program_id(0), pl.program_id(1)
        step = l * n + i

        def rows(st, core):
            return out.at[st // n, pl.ds(core * r2 + (st % n) * tr, tr), :]

        def copies(st):
            slot = st % 2
            local = pltpu.make_async_copy(slots.at[slot], rows(st, c), local_sems.at[slot])
            remote = pltpu.make_async_remote_copy(
                src_ref=slots.at[slot], dst_ref=rows(st, c), send_sem=send_sems.at[slot], recv_sem=recv_sem,
                device_id=sibling, device_id_type=MESH)
            return local, remote

        for ll in range(nl):
            @pl.when(l == ll)
            def _():
                acc = p_refs[ll][0].astype(F32)
                for d in range(1, N_DEV):
                    acc = acc + p_refs[ll][d].astype(F32)
                slots[step % 2] = acc

        for cp in copies(step):
            cp.start()

        @pl.when(step >= 1)
        def _():
            local, remote = copies(step - 1)
            local.wait()
            remote.wait_send()

        @pl.when(step == nl * n - 1)
        def _():
            local, remote = copies(step)
            local.wait()
            remote.wait_send()
            theirs = out.at[:, pl.ds((1 - c) * r2, r2), :]
            pltpu.make_async_remote_copy(src_ref=theirs, dst_ref=theirs, send_sem=send_sems.at[0],
                                         recv_sem=recv_sem, device_id=sibling, device_id_type=MESH).wait_recv()

    def piece_spec(ll):
        def index(l, i):
            return (0, jnp.where(l == ll, i, jnp.where(l < ll, 0, n - 1)), 0)
        return pl.BlockSpec((N_DEV, tr, cc), index)

    (out,), got = _call(
        body, name=name, grid=(nl, n),
        in_specs=[piece_spec(ll) for ll in range(nl)],
        out_specs=[pl.BlockSpec(memory_space=pl.ANY)],
        out_shape=[jax.ShapeDtypeStruct((nl, 2 * r2, cc), F32)],
        scratch_shapes=[pltpu.VMEM((2, tr, cc), F32), pltpu.SemaphoreType.DMA((2,)), pltpu.SemaphoreType.DMA((2,)),
                        pltpu.SemaphoreType.DMA(())],
        semantics=("arbitrary", "arbitrary"), args=tuple(pieces), comm=comm)
    return (out, got) if comm is not None else out


def sum_pieces(pieces, name):
    _, R, C = pieces.shape
    tr = _tile(R, 128) if R % 128 == 0 else R

    def body(p_ref, o_ref):
        acc = p_ref[0].astype(F32)
        for d in range(1, N_DEV):
            acc = acc + p_ref[d].astype(F32)
        o_ref[...] = acc

    return pl.pallas_call(
        body, name=name, grid=(R // tr,),
        in_specs=[pl.BlockSpec((N_DEV, tr, C), lambda i: (0, i, 0))],
        out_specs=pl.BlockSpec((tr, C), lambda i: (i, 0)),
        out_shape=jax.ShapeDtypeStruct((R, C), F32),
        compiler_params=_params("parallel"),
    )(pieces)


def adamw(w, g, m, v, name):
    Lyr, R, C = w.shape
    tr = _tile(R, 256) if R % 8 == 0 else R
    c1 = 1.0 / (1.0 - ADAM_B1 ** ADAM_STEP)
    c2 = 1.0 / (1.0 - ADAM_B2 ** ADAM_STEP)

    def body(w_ref, g_ref, m_ref, v_ref, d_ref, nm_ref, nv_ref):
        gv = g_ref[...]
        nm = ADAM_B1 * m_ref[...] + (1.0 - ADAM_B1) * gv
        nv = ADAM_B2 * v_ref[...] + (1.0 - ADAM_B2) * (gv * gv)
        nm_ref[...] = nm
        nv_ref[...] = nv
        d_ref[...] = -ADAM_LR * ((nm * c1) / (jnp.sqrt(nv * c2) + ADAM_EPS) + ADAM_WD * w_ref[...])

    spec = pl.BlockSpec((None, tr, C), lambda l, i: (l, i, 0))
    shp = jax.ShapeDtypeStruct(w.shape, F32)
    return pl.pallas_call(
        body, name=name, grid=(Lyr, R // tr),
        in_specs=[spec] * 4, out_specs=[spec] * 3, out_shape=[shp] * 3,
        compiler_params=_params("parallel", "parallel"),
    )(w, g, m, v)


def _rope_tables(T):
    pos = jnp.arange(T, dtype=F32)
    inv_freq = THETA ** (-jnp.arange(0, ROT, 2, dtype=F32) / ROT)
    ang = pos[:, None] * inv_freq[None, :]
    cs = jnp.concatenate([jnp.cos(ang), jnp.sin(ang)], axis=1)
    half = ROT // 2
    lane = jnp.arange(3 * LANES)
    table, lm = lane // LANES, lane % HD
    src = jnp.where(table == 0, lm % half, half + lm % half)
    i32 = lambda b: b.astype(jnp.int32)
    sign = jnp.where(table == 0, i32(lm < ROT), jnp.where(table == 1, -i32(lm < half), i32((lm >= half) & (lm < ROT))))
    place = (jnp.arange(ROT)[:, None] == src[None, :]) * sign[None, :].astype(F32)
    ones = ((table == 0) & (lm >= ROT)).astype(F32)
    return jnp.dot(cs, place, precision=lax.Precision.HIGHEST) + ones[None, :]


def _tab_specs(rows):
    return [pl.BlockSpec((rows, LANES), lambda i, k=k: (i, k)) for k in range(3)]


def kernel(x, attn_norm, attn_w_qkv, attn_w_o, attn_sink, conv_norm, conv_w_pw1, conv_b_pw1, conv_w_dw, conv_b_dw, conv_ln_g, conv_ln_b, conv_w_pw2, conv_b_pw2, ffn_norm, ffn_w_gu, ffn_w_down, final_norm, loss_target, m_attn_norm, m_attn_w_qkv, m_attn_w_o, m_attn_sink, m_conv_norm, m_conv_w_pw1, m_conv_b_pw1, m_conv_w_dw, m_conv_b_dw, m_conv_ln_g, m_conv_ln_b, m_conv_w_pw2, m_conv_b_pw2, m_ffn_norm, m_ffn_w_gu, m_ffn_w_down, m_final_norm, v_attn_norm, v_attn_w_qkv, v_attn_w_o, v_attn_sink, v_conv_norm, v_conv_w_pw1, v_conv_b_pw1, v_conv_w_dw, v_conv_b_dw, v_conv_ln_g, v_conv_ln_b, v_conv_w_pw2, v_conv_b_pw2, v_ffn_norm, v_ffn_w_gu, v_ffn_w_down, v_final_norm):
    T = x.shape[1]
    x0 = x[0]
    target = loss_target[0]
    ix, iy = lax.axis_index("x"), lax.axis_index("y")
    chip = 2 * ix + iy
    rc = rs1 = rs2 = _rope_tables(T)

    bf = lambda t: t.astype(BF16)
    col_row = [False, True]

    def place(vec, width):
        return lax.dynamic_update_slice(jnp.zeros((vec.shape[0], N_CHIPS * width), F32), vec, (0, chip * width))

    small_rows = jnp.concatenate([
        place(conv_norm, 256), place(conv_b_pw1, 512).reshape(2, D), place(conv_b_dw, 256), place(conv_ln_g, 256),
        place(conv_ln_b, 256), place(conv_b_pw2, 256), jnp.zeros((1, D), F32),
        place(conv_w_dw[0], 256), jnp.zeros((1, D), F32)], axis=0)
    w_qkv, = exchange(_Gather([bf(attn_w_qkv[0])], [False]), "gather_qkv")

    h0, qkv, (w_o, got) = rms_qkv(x0, attn_norm, w_qkv, rc, rs1, rs2,
                                  comm=_Both(_Gather([bf(attn_w_o[0])], [True]), _Scatter([], small_rows)))
    psmall = sum_pieces(got, "sum_small_params") * 0.5
    p_conv_norm, p_b_pw1 = psmall[0:1], psmall[1:3].reshape(1, 2 * D)
    p_b_dw, p_ln_g, p_ln_b, p_b_pw2 = psmall[3:4], psmall[4:5], psmall[5:6], psmall[6:7]
    p_w_dw = psmall[8:40]
    sink = attn_sink[0]
    o, (w_gu0,) = attn_fwd(qkv, sink, comm=_Gather([bf(ffn_w_gu[0])], [False]))
    zero_b = jnp.zeros((1, D), F32)
    zero_gu = jnp.zeros((1, 2 * DFF), F32)
    x1, h1, gu0, act0, (w_down0, w_pw1, w_pw2) = rms_mm_gate(
        (o, w_o, zero_b, x0), ffn_norm[0:1], w_gu0, zero_gu, DFF, True, BF16, "ffn0_up",
        comm=_Gather([bf(ffn_w_down[0]), bf(conv_w_pw1[0]), bf(conv_w_pw2[0])], [True, False, True]))
    x2, h2, pre, glu, _ = rms_mm_gate((act0, w_down0, zero_b, x1), p_conv_norm, w_pw1, p_b_pw1, D, False, F32,
                                      "conv_pw1")
    dwc, sw, (w_gu1, w_down1) = conv_fwd(glu, p_w_dw, p_b_dw, p_ln_g, p_ln_b,
                                         comm=_Gather([bf(ffn_w_gu[1]), bf(ffn_w_down[1])], col_row))
    x3, h3, gu1, act1, _ = rms_mm_gate((sw, w_pw2, p_b_pw2, x2), ffn_norm[1:2], w_gu1, zero_gu, DFF, True, BF16,
                                       "ffn1_up")
    dx4, loss_part, d_final = mm_res_loss(act1, w_down1, x3, final_norm.reshape(1, D), target)

    dgu1, _ = swiglu_bwd(dx4, w_down1, gu1, "ffn1_down_bwd")
    g_down1 = dw_row(act1, dx4, "ffn1_down_dw")
    dx3, d_ffn1, ddwc, d_ln_g, d_ln_b, d_b_pw2, _ = mm_bt_rmsbwd(
        dgu1, w_gu1, x3, ffn_norm[1:2], dx4, "ffn1_up_bwd", conv_tail=(w_pw2, dwc, p_ln_g, p_ln_b))
    g_gu1 = dw_col(h3, dgu1, "ffn1_up_dw")

    g_pw2 = dw_row(sw, dx3, "conv_pw2_dw")
    dpre, d_w_dw, d_b_dw, d_b_pw1, (r_gu1, r_down1) = conv_bwd(ddwc, glu, pre, p_w_dw,
                                                               comm=_Scatter([g_gu1, g_down1]))
    dx2, d_conv_norm, _ = mm_bt_rmsbwd(dpre, w_pw1, x2, p_conv_norm, dx3, "conv_pw1_bwd")
    g_pw1 = dw_col(h2, dpre, "conv_pw1_dw")

    dgu0, (r_pw1, r_pw2) = swiglu_bwd(dx2, w_down0, gu0, "ffn0_down_bwd", comm=_Scatter([g_pw1, g_pw2]))
    g_down0 = dw_row(act0, dx2, "ffn0_down_dw")
    dx1, d_ffn0, do, _ = mm_bt_rmsbwd(dgu0, w_gu0, x1, ffn_norm[0:1], dx2, "ffn0_up_bwd", proj_w=w_o)
    g_gu0 = dw_col(h1, dgu0, "ffn0_up_dw")

    g_o = dw_row(o, dx1, "attn_out_dw")
    dq, dkc, dvc, d_sink, (r_gu0, r_down0, r_o) = attn_bwd(qkv, o, do, sink, rc, rs1, rs2,
                                                           comm=_Scatter([g_gu0, g_down0, g_o]))
    dqkv = kv_sum(dq, dkc, dvc, rc, rs1, rs2)[None]
    g_qkv = dw_col(h0, dqkv, "attn_qkv_dw")
    dx0, d_attn_norm, (r_qkv,) = mm_bt_rmsbwd(dqkv, w_qkv, x0, attn_norm, dx1, "attn_qkv_bwd",
                                              comm=_Scatter([g_qkv]))

    pad16 = lambda t: jnp.concatenate([t, jnp.zeros((1, D - t.shape[1]), F32)], axis=1)
    small_g = jnp.concatenate([
        d_attn_norm, pad16(d_sink), d_conv_norm, d_b_pw1.reshape(2, D), d_b_dw, d_ln_g, d_ln_b, d_b_pw2,
        d_ffn0, d_ffn1, d_final, pad16(loss_part), jnp.zeros((3, D), F32), d_w_dw], axis=0)
    gf_gu, (r_small,) = sum_swap([r_gu0, r_gu1], "sum_gu", comm=_Scatter([], small_g))
    gf_down = sum_swap([r_down0, r_down1], "sum_down")
    gf_pw1, gf_pw2 = sum_swap([r_pw1], "sum_pw1"), sum_swap([r_pw2], "sum_pw2")
    gf_qkv, gf_o = sum_swap([r_qkv], "sum_qkv"), sum_swap([r_o], "sum_o")
    gs = sum_pieces(r_small, "sum_small_grads")
    loss = gs[12, 0]

    def take(row0, nrows, width):
        return lax.dynamic_slice(gs, (row0, chip * width), (nrows, width))

    grads = {
        "attn_norm": gs[0:1], "attn_w_qkv": gf_qkv, "attn_w_o": gf_o, "attn_sink": gs[1:2, :N_HEADS],
        "conv_norm": take(2, 1, 256), "conv_w_pw1": gf_pw1,
        "conv_b_pw1": lax.dynamic_slice(gs[3:5].reshape(1, 2 * D), (0, chip * 512), (1, 512)),
        "conv_w_dw": take(16, 32, 256)[None, :CONV_W], "conv_b_dw": take(5, 1, 256), "conv_ln_g": take(6, 1, 256),
        "conv_ln_b": take(7, 1, 256), "conv_w_pw2": gf_pw2, "conv_b_pw2": take(8, 1, 256),
        "ffn_norm": gs[9:11], "ffn_w_gu": gf_gu, "ffn_w_down": gf_down, "final_norm": gs[11],
    }
    weights = dict(attn_norm=attn_norm, attn_w_qkv=attn_w_qkv, attn_w_o=attn_w_o, attn_sink=attn_sink,
                   conv_norm=conv_norm, conv_w_pw1=conv_w_pw1, conv_b_pw1=conv_b_pw1, conv_w_dw=conv_w_dw,
                   conv_b_dw=conv_b_dw, conv_ln_g=conv_ln_g, conv_ln_b=conv_ln_b, conv_w_pw2=conv_w_pw2,
                   conv_b_pw2=conv_b_pw2, ffn_norm=ffn_norm, ffn_w_gu=ffn_w_gu, ffn_w_down=ffn_w_down,
                   final_norm=final_norm)
    m_in = dict(attn_norm=m_attn_norm, attn_w_qkv=m_attn_w_qkv, attn_w_o=m_attn_w_o, attn_sink=m_attn_sink,
                conv_norm=m_conv_norm, conv_w_pw1=m_conv_w_pw1, conv_b_pw1=m_conv_b_pw1, conv_w_dw=m_conv_w_dw,
                conv_b_dw=m_conv_b_dw, conv_ln_g=m_conv_ln_g, conv_ln_b=m_conv_ln_b, conv_w_pw2=m_conv_w_pw2,
                conv_b_pw2=m_conv_b_pw2, ffn_norm=m_ffn_norm, ffn_w_gu=m_ffn_w_gu, ffn_w_down=m_ffn_w_down,
                final_norm=m_final_norm)
    v_in = dict(attn_norm=v_attn_norm, attn_w_qkv=v_attn_w_qkv, attn_w_o=v_attn_w_o, attn_sink=v_attn_sink,
                conv_norm=v_conv_norm, conv_w_pw1=v_conv_w_pw1, conv_b_pw1=v_conv_b_pw1, conv_w_dw=v_conv_w_dw,
                conv_b_dw=v_conv_b_dw, conv_ln_g=v_conv_ln_g, conv_ln_b=v_conv_ln_b, conv_w_pw2=v_conv_w_pw2,
                conv_b_pw2=v_conv_b_pw2, ffn_norm=v_ffn_norm, ffn_w_gu=v_ffn_w_gu, ffn_w_down=v_ffn_w_down,
                final_norm=v_final_norm)
    order = list(weights)
    g_out, d_out, m_out, v_out = [], [], [], []
    for nm in order:
        w = weights[nm]
        shape = w.shape
        as3 = lambda t: t.reshape((1,) * (3 - len(shape)) + shape) if len(shape) < 3 else t.reshape(shape)
        g3 = as3(grads[nm].reshape(shape))
        delta, nm_, nv_ = adamw(as3(w), g3, as3(m_in[nm]), as3(v_in[nm]), "adamw_" + nm)
        g_out.append(g3.reshape(shape))
        d_out.append(delta.reshape(shape))
        m_out.append(nm_.reshape(shape))
        v_out.append(nv_.reshape(shape))
    return (loss, dx0[None], *g_out, *d_out, *m_out, *v_out)
```

```python
import functools
import math

import jax
import jax.numpy as jnp
from jax import lax
from jax.experimental import pallas as pl
from jax.experimental.pallas import tpu as pltpu

F32 = jnp.float32
BF16 = jnp.bfloat16

D = 1024
N_HEADS = 16
N_KV = 4
GROUP = N_HEADS // N_KV
HD = 64
ROT = 16
THETA = 500000.0
BLK = 128
QKV = (N_HEADS + 2 * N_KV) * HD
KV_OFF = N_HEADS * HD
DFF = 2816
CONV_W = 31
CONV_PAD = 15
HALO = 16
CONV_JB = 8
CONV_JB_BWD = 8
EPS = 1e-6
NEG = -1e30
N_CHIPS = 4
N_DEV = 8
LANES = 128
SUBLANES = 8

ADAM_LR, ADAM_B1, ADAM_B2, ADAM_EPS, ADAM_WD, ADAM_STEP = 0.001, 0.9, 0.999, 1e-08, 0.01, 10

VMEM_LIMIT = 56 * 1024 * 1024
MESH = pl.DeviceIdType.MESH


def _params(*sem):
    return pltpu.CompilerParams(dimension_semantics=sem, vmem_limit_bytes=VMEM_LIMIT)


def _tile(n, want):
    if n <= want:
        return n
    for t in range(want, 7, -1):
        if n % t == 0 and t % 8 == 0:
            return t
    return n


MXU_COLS = 256


def _col_chunks(n):
    return [slice(c, min(c + MXU_COLS, n)) for c in range(0, n, MXU_COLS)]


def _sigmoid(v):
    return 1.0 / (1.0 + jnp.exp(-v))


def _rms_fwd(xv, gain):
    r = lax.rsqrt(jnp.mean(xv * xv, axis=-1, keepdims=True) + EPS)
    return xv * r * gain


def _rms_bwd(dh, xv, gain, dres):
    r = lax.rsqrt(jnp.mean(xv * xv, axis=-1, keepdims=True) + EPS)
    xhat = xv * r
    gy = dh * gain
    dx = r * (gy - xhat * jnp.mean(gy * xhat, axis=-1, keepdims=True))
    return dx + dres, dh * xhat


def _rope(blk, c, s1, s2):
    return blk * c + pltpu.roll(blk, LANES - ROT // 2, 1) * s1 + pltpu.roll(blk, ROT // 2, 1) * s2


def _dot(a, b):
    return jnp.dot(a, b, preferred_element_type=F32)


def _dot_tb(a, b):
    return lax.dot_general(a, b, (((1,), (1,)), ((), ())), preferred_element_type=F32)


def _dot_ta(a, b):
    return lax.dot_general(a, b, (((0,), (0,)), ((), ())), preferred_element_type=F32)


def rms_first(x, gain, comm):
    T = x.shape[0]
    tm = _tile(T, 512)

    def body(x_ref, g_ref, h_ref):
        h_ref[...] = _rms_fwd(x_ref[...], g_ref[...]).astype(BF16)

    (h,), got = _call(
        body, name="rms_first", grid=(T // tm,),
        in_specs=[pl.BlockSpec((tm, D), lambda i: (i, 0)), pl.BlockSpec((1, D), lambda i: (0, 0))],
        out_specs=[pl.BlockSpec((tm, D), lambda i: (i, 0))], out_shape=[jax.ShapeDtypeStruct((T, D), BF16)],
        semantics=("parallel",), args=(x, gain), comm=comm)
    return h, got


def qkv_proj(h, w, rc, rs1, rs2, comm=None):
    T = h.shape[0]
    tm = _tile(T, 512)

    def body(h_ref, w_ref, c_ref, s1_ref, s2_ref, qkv_ref):
        acc = _dot(h_ref[...], w_ref[...])
        c, s1, s2 = c_ref[...], s1_ref[...], s2_ref[...]
        n_rot = (KV_OFF + N_KV * HD) // LANES
        for j in range(n_rot):
            sl = slice(LANES * j, LANES * (j + 1))
            roped = _rope(acc[:, sl], c, s1, s2)
            if j < KV_OFF // LANES:
                roped = roped * Q_SCALE
            qkv_ref[:, sl] = roped.astype(BF16)
        qkv_ref[:, n_rot * LANES:] = acc[:, n_rot * LANES:].astype(BF16)

    row = lambda i: (i, 0)
    full = lambda i: (0, 0)
    (qkv,), got = _call(
        body, name="qkv_proj", grid=(T // tm,),
        in_specs=[pl.BlockSpec((tm, D), row), pl.BlockSpec((D, QKV), full), *_tab_specs(tm)],
        out_specs=[pl.BlockSpec((tm, QKV), row)],
        out_shape=[jax.ShapeDtypeStruct((T, QKV), BF16)],
        semantics=("parallel",), args=(h, w, rc, rs1, rs2), comm=comm)
    return qkv, got


Q_SCALE = 1.0 / math.sqrt(HD)


def _attn_mask(n, T):
    ci = lax.broadcasted_iota(jnp.int32, (3 * BLK, BLK), 0)
    qi = lax.broadcasted_iota(jnp.int32, (3 * BLK, BLK), 1)
    key_pos = n * BLK - BLK + ci
    return (jnp.abs(ci - BLK - qi) <= BLK) & (key_pos >= 0) & (key_pos < T)


def _kv_padded(kv, first_tile):
    low = lax.broadcasted_iota(jnp.int32, (3 * BLK, LANES), 1) < HD
    zero = jnp.zeros((3 * BLK, LANES), BF16)
    out = {}
    for g in range(N_KV):
        t = kv[:, (first_tile + g // 2) * LANES:(first_tile + g // 2 + 1) * LANES]
        swapped = jnp.concatenate([t[:, HD:], t[:, :HD]], axis=1)
        for p in range(2):
            out[g, p] = jnp.where(low if p == 0 else ~low, t if g % 2 == p else swapped, zero)
    return out


def _softmax_sink(s, valid, sk):
    s = jnp.where(valid, s, NEG)
    m = jnp.maximum(jnp.max(s, axis=0, keepdims=True), sk)
    e = jnp.exp(s - m)
    es = jnp.exp(sk - m)
    inv = 1.0 / (jnp.sum(e, axis=0, keepdims=True) + es)
    return e * inv, es * inv


def _attn_specs(T):
    nb = T // BLK
    kv_blk = 2 * N_KV * HD
    kv_col = KV_OFF // kv_blk
    q_spec = pl.BlockSpec((BLK, KV_OFF), lambda n: (n, 0))
    prev = pl.BlockSpec((BLK, kv_blk), lambda n: (jnp.maximum(n - 1, 0), kv_col))
    own = pl.BlockSpec((BLK, kv_blk), lambda n: (n, kv_col))
    nxt = pl.BlockSpec((BLK, kv_blk), lambda n: (jnp.minimum(n + 1, nb - 1), kv_col))
    return nb, q_spec, prev, own, nxt


def attn_fwd(qkv, sink, comm=None):
    T = qkv.shape[0]
    nb, q_spec, prev, own, nxt = _attn_specs(T)

    def body(sink_ref, q_ref, kp_ref, ko_ref, kn_ref, o_ref):
        valid = _attn_mask(pl.program_id(0), T)
        kv = jnp.concatenate([kp_ref[...], ko_ref[...], kn_ref[...]], axis=0)
        kx, vx = _kv_padded(kv, 0), _kv_padded(kv, 2)
        tile = lambda ref, h: ref[:, (h // 2) * LANES:(h // 2 + 1) * LANES]
        ss = [_dot_tb(kx[h // GROUP, h % 2], tile(q_ref, h)) for h in range(N_HEADS)]
        ps = [_softmax_sink(ss[h], valid, sink_ref[h])[0].astype(BF16) for h in range(N_HEADS)]
        vxt = {k: v.T for k, v in vx.items()}
        for j in range(N_HEADS // 2):
            g = 2 * j // GROUP
            o_t = _dot(vxt[g, 0], ps[2 * j]) + _dot(vxt[g, 1], ps[2 * j + 1])
            o_ref[:, j * LANES:(j + 1) * LANES] = o_t.T.astype(BF16)

    (o,), got = _call(
        body, name="attn_fwd", grid=(nb,),
        in_specs=[pl.BlockSpec(memory_space=pltpu.SMEM), q_spec, prev, own, nxt],
        out_specs=[pl.BlockSpec((BLK, D), lambda n: (n, 0))],
        out_shape=[jax.ShapeDtypeStruct((T, D), BF16)],
        semantics=("parallel",), args=(sink, qkv, qkv, qkv, qkv), comm=comm)
    return o, got


def rms_mm_gate(x, gain, w, bias, H, swiglu, act_dtype, name, comm=None):
    fused = isinstance(x, tuple)
    T = (x[0] if fused else x).shape[0]
    tm = _tile(T, 512)

    def body(*refs):
        if fused:
            a_ref, wp_ref, bp_ref, r_ref, g_ref, w_ref, b_ref, x_ref, h_ref, pre_ref, act_ref = refs
            xv = _dot(a_ref[...], wp_ref[...]) + bp_ref[...] + r_ref[...]
            x_ref[...] = xv
        else:
            x_ref, g_ref, w_ref, b_ref, h_ref, pre_ref, act_ref = refs
            xv = x_ref[...]
        h = _rms_fwd(xv, g_ref[...]).astype(BF16)
        h_ref[...] = h
        for cs in _col_chunks(H):
            cs2 = slice(H + cs.start, H + cs.stop)
            a = _dot(h, w_ref[:, cs]) + b_ref[:, cs]
            b = _dot(h, w_ref[:, cs2]) + b_ref[:, cs2]
            pre_ref[0, :, cs] = a.astype(BF16)
            pre_ref[1, :, cs] = b.astype(BF16)
            if swiglu:
                act = a * _sigmoid(a) * b
            else:
                act = a * _sigmoid(b)
            act_ref[:, cs] = act.astype(act_dtype)

    row = lambda i: (i, 0)
    full = lambda i: (0, 0)
    if fused:
        K = x[0].shape[1]
        x_specs = [pl.BlockSpec((tm, K), row), pl.BlockSpec((K, D), full, pipeline_mode=pl.Buffered(1)),
                   pl.BlockSpec((1, D), full), pl.BlockSpec((tm, D), row)]
        x_out = ([pl.BlockSpec((tm, D), row)], [jax.ShapeDtypeStruct((T, D), F32)])
        x_args = tuple(x)
    else:
        x_specs, x_out, x_args = [pl.BlockSpec((tm, D), row)], ([], []), (x,)
    outs, got = _call(
        body, name=name, grid=(T // tm,),
        in_specs=x_specs + [pl.BlockSpec((1, D), full),
                            pl.BlockSpec((D, 2 * H), full, pipeline_mode=pl.Buffered(1)), pl.BlockSpec((1, 2 * H), full)],
        out_specs=x_out[0] + [pl.BlockSpec((tm, D), row), pl.BlockSpec((2, tm, H), lambda i: (0, i, 0)),
                              pl.BlockSpec((tm, H), row)],
        out_shape=x_out[1] + [jax.ShapeDtypeStruct((T, D), BF16), jax.ShapeDtypeStruct((2, T, H), BF16),
                              jax.ShapeDtypeStruct((T, H), act_dtype)],
        semantics=("parallel",), args=x_args + (gain, w, bias), comm=comm)
    return (*outs, got)


def _conv_tiles(T):
    tt = _tile(T, 512)
    return tt, tt // SUBLANES, D // LANES


def _fill_strided(ext, p, L):
    main = p[HALO:HALO + SUBLANES * L, :].reshape(SUBLANES, L, LANES)
    ext[CONV_PAD:CONV_PAD + L] = jnp.swapaxes(main, 0, 1)

    def ibody(i, carry):
        ext[i] = p[pl.ds(i + 1, SUBLANES, stride=L), :]
        ext[i + CONV_PAD + L] = p[pl.ds(i + CONV_PAD + L + 1, SUBLANES, stride=L), :]
        return carry

    lax.fori_loop(0, CONV_PAD, ibody, 0, unroll=3)


def _conv_specs(T, tt):
    main = pl.BlockSpec((tt, D), lambda i: (i, 0))
    per = tt // HALO
    prev = pl.BlockSpec((HALO, D), lambda i: (jnp.maximum(i * per - 1, 0), 0))
    nxt = pl.BlockSpec((HALO, D), lambda i: (jnp.minimum((i + 1) * per, T // HALO - 1), 0))
    return main, prev, nxt


def _fill_pad(pad, main_ref, prev_ref, next_ref, i, n_i, tt, nlt):
    keep_p = (i > 0).astype(F32)
    keep_n = (i < n_i - 1).astype(F32)
    for lt in range(nlt):
        sl = slice(lt * LANES, (lt + 1) * LANES)
        pad[lt, 0:HALO, :] = prev_ref[:, sl] * keep_p
        pad[lt, HALO:HALO + tt, :] = main_ref[:, sl]
        pad[lt, HALO + tt:2 * HALO + tt, :] = next_ref[:, sl] * keep_n


def conv_fwd(glu, w_dw, b_dw, ln_g, ln_b, comm=None):
    T = glu.shape[0]
    tt, L, nlt = _conv_tiles(T)
    n_i = T // tt
    main, prev, nxt = _conv_specs(T, tt)

    def body(x_ref, xp_ref, xn_ref, w_ref, b_ref, g_ref, bb_ref, dwc_ref, sw_ref, pad, ob, ext, wk):
        i = pl.program_id(0)
        _fill_pad(pad, x_ref, xp_ref, xn_ref, i, n_i, tt, nlt)
        for lt in range(nlt):
            sl = slice(lt * LANES, (lt + 1) * LANES)
            o = ob.at[lt]
            _fill_strided(ext, pad.at[lt], L)
            for k in range(CONV_W):
                wk[k] = jnp.broadcast_to(w_ref[k:k + 1, sl], (SUBLANES, LANES))

            def jbody(jb, carry):
                j = jb * CONV_JB
                accs = [None] * CONV_JB
                for m in range(CONV_W + CONV_JB - 1):
                    e = ext[j + m]
                    for u in range(CONV_JB):
                        if 0 <= m - u < CONV_W:
                            t = e * wk[m - u]
                            accs[u] = t if accs[u] is None else accs[u] + t
                for u in range(CONV_JB):
                    o[pl.ds(j + u, SUBLANES, stride=L), :] = accs[u]
                return carry

            lax.fori_loop(0, L // CONV_JB, jbody, 0)
        y = jnp.concatenate([ob[lt] for lt in range(nlt)], axis=1) + b_ref[...]
        dwc_ref[...] = y
        mu = jnp.mean(y, axis=-1, keepdims=True)
        yc = y - mu
        var = jnp.mean(yc * yc, axis=-1, keepdims=True)
        z = yc * lax.rsqrt(var + EPS) * g_ref[...] + bb_ref[...]
        sw_ref[...] = (z * _sigmoid(z)).astype(BF16)

    full = lambda i: (0, 0)
    (dwc, sw), got = _call(
        body, name="conv_fwd", grid=(n_i,),
        in_specs=[main, prev, nxt, pl.BlockSpec((32, D), full), pl.BlockSpec((1, D), full),
                  pl.BlockSpec((1, D), full), pl.BlockSpec((1, D), full)],
        out_specs=[pl.BlockSpec((tt, D), lambda i: (i, 0)), pl.BlockSpec((tt, D), lambda i: (i, 0))],
        out_shape=[jax.ShapeDtypeStruct((T, D), F32), jax.ShapeDtypeStruct((T, D), BF16)],
        scratch_shapes=[pltpu.VMEM((nlt, tt + 2 * HALO, LANES), F32), pltpu.VMEM((nlt, tt, LANES), F32),
                        pltpu.VMEM((L + 2 * HALO, SUBLANES, LANES), F32), pltpu.VMEM((32, SUBLANES, LANES), F32)],
        semantics=("parallel",), args=(glu, glu, glu, w_dw, b_dw, ln_g, ln_b), comm=comm)
    return dwc, sw, got


def mm_res_loss(a, w, resid, gain, target):
    T, K = a.shape
    tm = _tile(T, 512)

    def body(a_ref, w_ref, r_ref, g_ref, t_ref, dx_ref, loss_ref, dg_ref):
        @pl.when(pl.program_id(0) == 0)
        def _():
            loss_ref[...] = jnp.zeros_like(loss_ref)
            dg_ref[...] = jnp.zeros_like(dg_ref)

        xv, gain_v = _dot(a_ref[...], w_ref[...]) + r_ref[...], g_ref[...]
        err = _rms_fwd(xv, gain_v) - t_ref[...]
        part = 0.5 * jnp.sum(jnp.mean(err * err, axis=-1, keepdims=True), axis=0, keepdims=True)
        loss_ref[...] += jnp.broadcast_to(part, loss_ref.shape)
        dx, dgr = _rms_bwd(err * (1.0 / D), xv, gain_v, 0.0)
        dx_ref[...] = dx
        dg_ref[...] += jnp.sum(dgr, axis=0, keepdims=True)

    row = lambda i: (i, 0)
    full = lambda i: (0, 0)
    return pl.pallas_call(
        body, name="ffn1_down_loss", grid=(T // tm,),
        in_specs=[pl.BlockSpec((tm, K), row), pl.BlockSpec((K, D), full), pl.BlockSpec((tm, D), row),
                  pl.BlockSpec((1, D), full), pl.BlockSpec((tm, D), row)],
        out_specs=[pl.BlockSpec((tm, D), row), pl.BlockSpec((1, LANES), full), pl.BlockSpec((1, D), full)],
        out_shape=[jax.ShapeDtypeStruct((T, D), F32), jax.ShapeDtypeStruct((1, LANES), F32),
                   jax.ShapeDtypeStruct((1, D), F32)],
        compiler_params=_params("arbitrary"),
    )(a, w, resid, gain, target)


def swiglu_bwd(dx, w_down, pre, name, comm=None):
    T = dx.shape[0]
    H = w_down.shape[0]
    tm = _tile(T, 512)

    def body(dx_ref, w_ref, pre_ref, dpre_ref):
        dxb = dx_ref[...].astype(BF16)
        for cs in _col_chunks(H):
            dact = _dot_tb(dxb, w_ref[cs, :])
            g = pre_ref[0, :, cs].astype(F32)
            u = pre_ref[1, :, cs].astype(F32)
            sg = _sigmoid(g)
            dpre_ref[0, :, cs] = (dact * u * sg * (1.0 + g * (1.0 - sg))).astype(BF16)
            dpre_ref[1, :, cs] = (dact * g * sg).astype(BF16)

    (dpre,), got = _call(
        body, name=name, grid=(T // tm,),
        in_specs=[pl.BlockSpec((tm, D), lambda i: (i, 0)),
                  pl.BlockSpec((H, D), lambda i: (0, 0), pipeline_mode=pl.Buffered(1)),
                  pl.BlockSpec((2, tm, H), lambda i: (0, i, 0))],
        out_specs=[pl.BlockSpec((2, tm, H), lambda i: (0, i, 0))],
        out_shape=[jax.ShapeDtypeStruct((2, T, H), BF16)],
        semantics=("parallel",), args=(dx, w_down, pre), comm=comm)
    return dpre, got


def _ln_silu_bwd(dsw, y, ln_g, ln_b):
    mu = jnp.mean(y, axis=-1, keepdims=True)
    yc = y - mu
    rstd = lax.rsqrt(jnp.mean(yc * yc, axis=-1, keepdims=True) + EPS)
    xhat = yc * rstd
    z = xhat * ln_g + ln_b
    sg = _sigmoid(z)
    dz = dsw * sg * (1.0 + z * (1.0 - sg))
    dxh = dz * ln_g
    dy = rstd * (dxh - jnp.mean(dxh, axis=-1, keepdims=True) - xhat * jnp.mean(dxh * xhat, axis=-1, keepdims=True))
    return dy, dz * xhat, dz


def mm_bt_rmsbwd(dpre, w, x, gain, dres, name, comm=None, proj_w=None, conv_tail=None):
    nh, T, H = dpre.shape
    tm = _tile(T, 512)
    n_extra_in = 1 if proj_w is not None else (4 if conv_tail is not None else 0)

    def body(*refs):
        dp_ref, w_ref, x_ref, g_ref, dres_ref = refs[:5]
        extra_in = refs[5:5 + n_extra_in]
        dx_ref, dg_ref = refs[5 + n_extra_in:7 + n_extra_in]
        extra_out = refs[7 + n_extra_in:]

        @pl.when(pl.program_id(0) == 0)
        def _():
            dg_ref[...] = jnp.zeros_like(dg_ref)
            for r in extra_out[1:]:
                r[...] = jnp.zeros_like(r)

        dh = _dot_tb(dp_ref[0], w_ref[:, 0:H])
        for hf in range(1, nh):
            dh = dh + _dot_tb(dp_ref[hf], w_ref[:, hf * H:(hf + 1) * H])
        dx, dgr = _rms_bwd(dh, x_ref[...], g_ref[...], dres_ref[...])
        dx_ref[...] = dx
        dg_ref[...] += jnp.sum(dgr, axis=0, keepdims=True)
        if proj_w is not None:
            extra_out[0][...] = _dot_tb(dx.astype(BF16), extra_in[0][...]).astype(BF16)
        elif conv_tail is not None:
            wt_ref, y_ref, lg_ref, lb_ref = extra_in
            dy, dgl, dbl = _ln_silu_bwd(_dot_tb(dx.astype(BF16), wt_ref[...]), y_ref[...], lg_ref[...], lb_ref[...])
            extra_out[0][...] = dy
            extra_out[1][...] += jnp.sum(dgl, axis=0, keepdims=True)
            extra_out[2][...] += jnp.sum(dbl, axis=0, keepdims=True)
            extra_out[3][...] += jnp.sum(dx, axis=0, keepdims=True)

    row = lambda i: (i, 0)
    full = lambda i: (0, 0)
    vec = pl.BlockSpec((1, D), full)
    vec_shape = jax.ShapeDtypeStruct((1, D), F32)
    in_specs = [pl.BlockSpec((nh, tm, H), lambda i: (0, i, 0)),
                pl.BlockSpec((D, nh * H), full, pipeline_mode=pl.Buffered(1)),
                pl.BlockSpec((tm, D), row), vec, pl.BlockSpec((tm, D), row)]
    out_specs = [pl.BlockSpec((tm, D), row), vec]
    out_shape = [jax.ShapeDtypeStruct((T, D), F32), vec_shape]
    args = (dpre, w, x, gain, dres)
    if proj_w is not None:
        N = proj_w.shape[0]
        in_specs.append(pl.BlockSpec((N, D), full, pipeline_mode=pl.Buffered(1)))
        out_specs.append(pl.BlockSpec((tm, N), row))
        out_shape.append(jax.ShapeDtypeStruct((T, N), BF16))
        args += (proj_w,)
    elif conv_tail is not None:
        in_specs += [pl.BlockSpec((D, D), full, pipeline_mode=pl.Buffered(1)), pl.BlockSpec((tm, D), row), vec, vec]
        out_specs += [pl.BlockSpec((tm, D), row), vec, vec, vec]
        out_shape += [jax.ShapeDtypeStruct((T, D), F32), vec_shape, vec_shape, vec_shape]
        args += tuple(conv_tail)
    outs, got = _call(body, name=name, grid=(T // tm,), in_specs=in_specs, out_specs=out_specs, out_shape=out_shape,
                      semantics=("arbitrary",), args=args, comm=comm)
    return (*outs, got)


def dw_col(a, dpre, name):
    T = a.shape[0]
    nh, _, H = dpre.shape
    per = nh * H // N_CHIPS
    bph = N_CHIPS // nh
    tt = _tile(T, 2048)
    nt = T // tt

    def body(a_ref, b_ref, o_ref, acc):
        t = pl.program_id(1)

        @pl.when(t == 0)
        def _():
            acc[...] = jnp.zeros_like(acc)

        acc[...] += _dot_ta(a_ref[...], b_ref[...])

        @pl.when(t == nt - 1)
        def _():
            o_ref[...] = acc[...].astype(BF16)

    return pl.pallas_call(
        body, name=name, grid=(N_CHIPS, nt),
        in_specs=[pl.BlockSpec((tt, D), lambda q, t: (t, 0)),
                  pl.BlockSpec((None, tt, per), lambda q, t: (q // bph, t, q % bph))],
        out_specs=pl.BlockSpec((None, D, per), lambda q, t: (q, 0, 0)),
        out_shape=jax.ShapeDtypeStruct((N_CHIPS, D, per), BF16),
        scratch_shapes=[pltpu.VMEM((D, per), F32)],
        compiler_params=_params("parallel", "arbitrary"),
    )(a, dpre)


def dw_row(a, b, name):
    T, R = a.shape
    cw = 1408 if R % 1408 == 0 else R
    tt = _tile(T, 1024)
    nt = T // tt

    def body(a_ref, b_ref, o_ref, acc):
        t = pl.program_id(1)

        @pl.when(t == 0)
        def _():
            acc[...] = jnp.zeros_like(acc)

        acc[...] += _dot_ta(a_ref[...], b_ref[...].astype(BF16))

        @pl.when(t == nt - 1)
        def _():
            o_ref[...] = acc[...].astype(BF16)

    out = pl.pallas_call(
        body, name=name, grid=(R // cw, nt),
        in_specs=[pl.BlockSpec((tt, cw), lambda q, t: (t, q)), pl.BlockSpec((tt, D), lambda q, t: (t, 0))],
        out_specs=pl.BlockSpec((cw, D), lambda q, t: (q, 0)),
        out_shape=jax.ShapeDtypeStruct((R, D), BF16),
        scratch_shapes=[pltpu.VMEM((cw, D), F32)],
        compiler_params=_params("parallel", "arbitrary"),
    )(a, b)
    return out.reshape(N_CHIPS, R // N_CHIPS, D)


def conv_bwd(ddwc, glu, pre, w_dw, comm=None):
    T = ddwc.shape[0]
    tt, L, nlt = _conv_tiles(T)
    n_i = T // tt
    main, prev, nxt = _conv_specs(T, tt)

    def body(d_ref, dp_ref, dn_ref, x_ref, xp_ref, xn_ref, pre_ref, w_ref,
             dpre_ref, dw_ref, dbd_ref, dbp_ref, padd, padx, ob, extd, extx, wk):
        i = pl.program_id(0)

        @pl.when(i == 0)
        def _():
            dw_ref[...] = jnp.zeros_like(dw_ref)
            dbd_ref[...] = jnp.zeros_like(dbd_ref)
            dbp_ref[...] = jnp.zeros_like(dbp_ref)

        _fill_pad(padd, d_ref, dp_ref, dn_ref, i, n_i, tt, nlt)
        _fill_pad(padx, x_ref, xp_ref, xn_ref, i, n_i, tt, nlt)
        for lt in range(nlt):
            sl = slice(lt * LANES, (lt + 1) * LANES)
            o = ob.at[lt]
            _fill_strided(extd, padd.at[lt], L)
            _fill_strided(extx, padx.at[lt], L)
            for k in range(CONV_W):
                wk[k] = jnp.broadcast_to(w_ref[k:k + 1, sl], (SUBLANES, LANES))

            nu = CONV_JB_BWD

            def jbody(jb, accs):
                j = jb * nu
                accs = list(accs)
                d = [extd[j + u + CONV_PAD] for u in range(nu)]
                g = [None] * nu
                for m in range(CONV_W + nu - 1):
                    ed = extd[j + 2 * CONV_PAD + nu - 1 - m]
                    ex = extx[j + m]
                    for u in range(nu):
                        k = m - (nu - 1 - u)
                        if 0 <= k < CONV_W:
                            t = ed * wk[k]
                            g[u] = t if g[u] is None else g[u] + t
                        k = m - u
                        if 0 <= k < CONV_W:
                            accs[k] = accs[k] + d[u] * ex
                for u in range(nu):
                    o[pl.ds(j + u, SUBLANES, stride=L), :] = g[u]
                return tuple(accs)

            accs = lax.fori_loop(0, L // nu, jbody, tuple(jnp.zeros((SUBLANES, LANES), F32) for _ in range(CONV_W)))
            for k in range(CONV_W):
                dw_ref[k:k + 1, sl] += jnp.sum(accs[k], axis=0, keepdims=True)
        dglu = jnp.concatenate([ob[lt] for lt in range(nlt)], axis=1)
        a = pre_ref[0].astype(F32)
        gate = pre_ref[1].astype(F32)
        sg = _sigmoid(gate)
        da = dglu * sg
        dgate = dglu * a * sg * (1.0 - sg)
        dpre_ref[0] = da.astype(BF16)
        dpre_ref[1] = dgate.astype(BF16)
        dbd_ref[...] += jnp.sum(d_ref[...], axis=0, keepdims=True)
        dbp_ref[0] += jnp.sum(da, axis=0, keepdims=True)
        dbp_ref[1] += jnp.sum(dgate, axis=0, keepdims=True)

    full = lambda i: (0, 0)
    (dpre, dw, dbd, dbp), got = _call(
        body, name="conv_bwd", grid=(n_i,),
        in_specs=[main, prev, nxt, main, prev, nxt, pl.BlockSpec((2, tt, D), lambda i: (0, i, 0)),
                  pl.BlockSpec((32, D), full)],
        out_specs=[pl.BlockSpec((2, tt, D), lambda i: (0, i, 0)), pl.BlockSpec((32, D), full),
                   pl.BlockSpec((1, D), full), pl.BlockSpec((2, 1, D), lambda i: (0, 0, 0))],
        out_shape=[jax.ShapeDtypeStruct((2, T, D), BF16), jax.ShapeDtypeStruct((32, D), F32),
                   jax.ShapeDtypeStruct((1, D), F32), jax.ShapeDtypeStruct((2, 1, D), F32)],
        scratch_shapes=[pltpu.VMEM((nlt, tt + 2 * HALO, LANES), F32), pltpu.VMEM((nlt, tt + 2 * HALO, LANES), F32),
                        pltpu.VMEM((nlt, tt, LANES), F32), pltpu.VMEM((L + 2 * HALO, SUBLANES, LANES), F32),
                        pltpu.VMEM((L + 2 * HALO, SUBLANES, LANES), F32), pltpu.VMEM((32, SUBLANES, LANES), F32)],
        semantics=("arbitrary",), args=(ddwc, ddwc, ddwc, glu, glu, glu, pre, w_dw), comm=comm)
    return dpre, dw, dbd, dbp, got


def attn_bwd(qkv, o, do, sink, rc, rs1, rs2, comm=None):
    T = qkv.shape[0]
    nb, q_spec, prev, own, nxt = _attn_specs(T)
    kvw = N_KV * HD

    def body(sink_ref, q_ref, kp_ref, ko_ref, kn_ref, o_ref, do_ref, c_ref, s1_ref, s2_ref,
             dq_ref, dkc_ref, dvc_ref, dsink_ref):
        n = pl.program_id(0)

        @pl.when(n == 0)
        def _():
            dsink_ref[...] = jnp.zeros_like(dsink_ref)

        valid = _attn_mask(n, T)
        kv = jnp.concatenate([kp_ref[...], ko_ref[...], kn_ref[...]], axis=0)
        kx, vx = _kv_padded(kv, 0), _kv_padded(kv, 2)
        tile = lambda ref, j: ref[:, j * LANES:(j + 1) * LANES]
        ss = [_dot_tb(kx[h // GROUP, h % 2], tile(q_ref, h // 2)) for h in range(N_HEADS)]
        dps = [_dot_tb(vx[h // GROUP, h % 2], tile(do_ref, h // 2)) for h in range(N_HEADS)]
        low_d = lax.broadcasted_iota(jnp.int32, (LANES, BLK), 0) < HD
        deltas = []
        for j in range(N_HEADS // 2):
            prod_t = tile(do_ref, j).astype(F32).T * tile(o_ref, j).astype(F32).T
            deltas.append(jnp.sum(jnp.where(low_d, prod_t, 0.0), axis=0, keepdims=True))
            deltas.append(jnp.sum(jnp.where(low_d, 0.0, prod_t), axis=0, keepdims=True))
        lane = lax.broadcasted_iota(jnp.int32, (1, N_HEADS), 1)
        dsink = jnp.zeros((1, N_HEADS), F32)
        pbs, dss = [], []
        for h in range(N_HEADS):
            p, p_sink = _softmax_sink(ss[h], valid, sink_ref[h])
            dss.append((p * (dps[h] - deltas[h])).astype(BF16))
            pbs.append(p.astype(BF16))
            part = -jnp.sum(p_sink * deltas[h], axis=1, keepdims=True)
            dsink = dsink + jnp.where(lane == h, part, 0.0)
        dsink_ref[...] += dsink
        c, s1, s2 = c_ref[...], s1_ref[...], s2_ref[...]
        kxt = {k: v.T for k, v in kx.items()}
        for j in range(N_HEADS // 2):
            g = 2 * j // GROUP
            dq_t = _dot(kxt[g, 0], dss[2 * j]) + _dot(kxt[g, 1], dss[2 * j + 1])
            dq_ref[:, j * LANES:(j + 1) * LANES] = (_rope(dq_t.T, c, -s1, -s2) * Q_SCALE).astype(BF16)
        low_k = lax.broadcasted_iota(jnp.int32, (3 * BLK, LANES), 1) < HD
        cols = lambda xs, g, p: jnp.concatenate([xs[GROUP * g + p], xs[GROUP * g + 2 + p]], axis=1)
        for t in range(N_KV // 2):
            sums = {}
            for g in (2 * t, 2 * t + 1):
                q2 = jnp.concatenate([tile(q_ref, 2 * g), tile(q_ref, 2 * g + 1)], axis=0)
                do2 = jnp.concatenate([tile(do_ref, 2 * g), tile(do_ref, 2 * g + 1)], axis=0)
                for p in range(2):
                    sums[g, p] = (_dot(cols(dss, g, p), q2), _dot(cols(pbs, g, p), do2))
            for which, ref in ((0, dkc_ref), (1, dvc_ref)):
                keep = jnp.where(low_k, sums[2 * t, 0][which], sums[2 * t + 1, 1][which])
                swap = jnp.where(low_k, sums[2 * t + 1, 0][which], sums[2 * t, 1][which])
                ref[:, t * LANES:(t + 1) * LANES] = keep + pltpu.roll(swap, HD, 1)

    row = lambda n: (n, 0)
    (dq, dkc, dvc, dsink), got = _call(
        body, name="attn_bwd", grid=(nb,),
        in_specs=[pl.BlockSpec(memory_space=pltpu.SMEM), q_spec, prev, own, nxt,
                  pl.BlockSpec((BLK, D), row), pl.BlockSpec((BLK, D), row), *_tab_specs(BLK)],
        out_specs=[pl.BlockSpec((BLK, D), row), pl.BlockSpec((None, 3 * BLK, kvw), lambda n: (n, 0, 0)),
                   pl.BlockSpec((None, 3 * BLK, kvw), lambda n: (n, 0, 0)), pl.BlockSpec((1, N_HEADS), lambda n: (0, 0))],
        out_shape=[jax.ShapeDtypeStruct((T, QKV), BF16), jax.ShapeDtypeStruct((nb, 3 * BLK, kvw), F32),
                   jax.ShapeDtypeStruct((nb, 3 * BLK, kvw), F32), jax.ShapeDtypeStruct((1, N_HEADS), F32)],
        semantics=("arbitrary",), args=(sink, qkv, qkv, qkv, qkv, o, do, rc, rs1, rs2), comm=comm)
    return dq, dkc, dvc, dsink, got


def kv_sum(dqkv, dkc, dvc, rc, rs1, rs2):
    nb = dkc.shape[0]
    T = nb * BLK
    kvw = N_KV * HD

    G = 4
    ng = nb // G

    def gather3(own_ref, prev_ref, before_ref, next_ref, after_ref, m):
        has_before = (m > 0).astype(F32)
        has_after = (m < ng - 1).astype(F32)
        out = []
        for i in range(G):
            from_prev = prev_ref[i - 1] if i > 0 else before_ref[0] * has_before
            from_next = next_ref[i + 1] if i < G - 1 else after_ref[0] * has_after
            out.append(from_prev + own_ref[i] + from_next)
        return jnp.concatenate(out, axis=0)

    def body(_, ko, kp, kb, kn, ka, vo, vp, vb, vn, va, c_ref, s1_ref, s2_ref, out_ref):
        m = pl.program_id(0)
        dk = gather3(ko, kp, kb, kn, ka, m)
        dv = gather3(vo, vp, vb, vn, va, m)
        c, s1, s2 = c_ref[...], s1_ref[...], s2_ref[...]
        for j in range(kvw // LANES):
            sl = slice(LANES * j, LANES * (j + 1))
            out_ref[:, sl] = _rope(dk[:, sl], c, -s1, -s2).astype(BF16)
        out_ref[:, kvw:] = dv.astype(BF16)

    own = pl.BlockSpec((G, BLK, kvw), lambda m: (m, 1, 0))
    prev = pl.BlockSpec((G, BLK, kvw), lambda m: (m, 2, 0))
    before = pl.BlockSpec((1, BLK, kvw), lambda m: (jnp.maximum(G * m - 1, 0), 2, 0))
    nxt = pl.BlockSpec((G, BLK, kvw), lambda m: (m, 0, 0))
    after = pl.BlockSpec((1, BLK, kvw), lambda m: (jnp.minimum(G * m + G, nb - 1), 0, 0))
    five = [own, prev, before, nxt, after]
    return pl.pallas_call(
        body, name="kv_sum", grid=(ng,),
        in_specs=[pl.BlockSpec(memory_space=pl.ANY), *five, *five, *_tab_specs(G * BLK)],
        out_specs=pl.BlockSpec((G * BLK, 2 * kvw), lambda m: (m, KV_OFF // (2 * kvw))),
        out_shape=jax.ShapeDtypeStruct((T, QKV), BF16),
        input_output_aliases={0: 0},
        compiler_params=_params("parallel"),
    )(dqkv, *([dkc] * 5), *([dvc] * 5), rc, rs1, rs2)


def _me():
    return lax.axis_index("x"), lax.axis_index("y"), lax.axis_index("c")


def _half_rows(ref, sharded_rows, chip, core):
    R, C = ref.shape[-2], ref.shape[-1]
    lead = (slice(None),) * (len(ref.shape) - 2)
    if sharded_rows:
        per = R // N_CHIPS
        return ref.at[lead + (pl.ds(chip * per + core * (per // 2), per // 2), slice(None))]
    per = C // N_CHIPS
    return ref.at[lead + (pl.ds(core * (R // 2), R // 2), pl.ds(chip * per, per))]


class _Gather:
    def __init__(self, shards, sharded_rows):
        self.inputs = list(shards)
        self.rows = list(sharded_rows)
        self.n = self.n_in = self.n_out = len(shards)
        self.out_shapes = []
        for s, rows in zip(shards, sharded_rows):
            shp = list(s.shape)
            shp[-2 if rows else -1] *= N_CHIPS
            self.out_shapes.append(jax.ShapeDtypeStruct(tuple(shp), s.dtype))
        self.scratch = [pltpu.SemaphoreType.DMA((self.n, 6)), pltpu.SemaphoreType.DMA((self.n, 6)),
                        pltpu.SemaphoreType.DMA((self.n, 2))]

    def _ctx(self, ins, outs, sems):
        send_sems, recv_sems, local_sems = sems
        x, y, c = _me()
        chips = [(1 - x, y), (x, 1 - y), (1 - x, 1 - y)]

        def half_src(w, core):
            s = ins[w]
            R = s.shape[-2]
            return s.at[pl.ds(core * (R // 2), R // 2), :]

        def dst(w, chip, core):
            return _half_rows(outs[w], self.rows[w], chip, core)

        def copy(w, k, src, chip, core, to):
            return pltpu.make_async_remote_copy(
                src_ref=src, dst_ref=dst(w, chip, core), send_sem=send_sems.at[w, k], recv_sem=recv_sems.at[w, k],
                device_id=to, device_id_type=MESH)

        def local(w, core):
            return pltpu.make_async_copy(half_src(w, core), dst(w, 2 * x + y, core), local_sems.at[w, core])

        def first(w, j):
            qx, qy = chips[j]
            return copy(w, j, half_src(w, c), 2 * x + y, c, (qx, qy, c))

        def landed(w, j):
            qx, qy = chips[j]
            return copy(w, j, dst(w, 2 * qx + qy, c), 2 * qx + qy, c, (x, y, c))

        def passed(w, j):
            qx, qy = chips[j]
            return copy(w, 3 + j, dst(w, 2 * qx + qy, c), 2 * qx + qy, c, (x, y, 1 - c))

        def from_sibling(w, j):
            qx, qy = chips[j]
            return copy(w, 3 + j, dst(w, 2 * qx + qy, 1 - c), 2 * qx + qy, 1 - c, (x, y, c))

        return local, first, landed, passed, from_sibling

    def start(self, ins, outs, sems):
        local, first, _, _, _ = self._ctx(ins, outs, sems)
        for w in range(self.n):
            for core in range(2):
                local(w, core).start()
            for j in range(3):
                first(w, j).start()

    def mid(self, ins, outs, sems):
        _, _, landed, passed, _ = self._ctx(ins, outs, sems)
        for w in range(self.n):
            for j in range(3):
                landed(w, j).wait_recv()
                passed(w, j).start()

    def end(self, ins, outs, sems):
        local, first, _, passed, from_sibling = self._ctx(ins, outs, sems)
        for w in range(self.n):
            for j in range(3):
                from_sibling(w, j).wait_recv()
        for w in range(self.n):
            for j in range(3):
                first(w, j).wait_send()
                passed(w, j).wait_send()
            for core in range(2):
                local(w, core).wait()


class _Scatter:
    def __init__(self, grads, small=None):
        self.inputs = list(grads) + ([small] if small is not None else [])
        self.ng = len(grads)
        self.n = self.n_in = self.n_out = len(self.inputs)
        self.out_shapes = [jax.ShapeDtypeStruct((N_DEV, g.shape[1] // 2, g.shape[2]), g.dtype) for g in grads]
        if small is not None:
            self.out_shapes.append(jax.ShapeDtypeStruct((N_DEV,) + small.shape, small.dtype))
        self.scratch = [pltpu.SemaphoreType.DMA((self.n, N_DEV)), pltpu.SemaphoreType.DMA((self.n, N_DEV)),
                        pltpu.SemaphoreType.DMA((self.n,))]

    def _ctx(self, ins, outs, sems):
        send_sems, recv_sems, local_sems = sems
        x, y, c = _me()
        me = 4 * x + 2 * y + c

        def piece(w, chip, core):
            if w >= self.ng:
                return ins[w]
            half = ins[w].shape[1] // 2
            return ins[w].at[chip, pl.ds(core * half, half), :]

        def peer_of(k):
            return x ^ ((k >> 2) & 1), y ^ ((k >> 1) & 1), c ^ (k & 1)

        def local(w):
            return pltpu.make_async_copy(piece(w, 2 * x + y, c), outs[w].at[me], local_sems.at[w])

        def send(w, k):
            px, py, pc = peer_of(k)
            return pltpu.make_async_remote_copy(
                src_ref=piece(w, 2 * px + py, pc), dst_ref=outs[w].at[me], send_sem=send_sems.at[w, k],
                recv_sem=recv_sems.at[w, k], device_id=(px, py, pc), device_id_type=MESH)

        def recv(w, k):
            px, py, pc = peer_of(k)
            return pltpu.make_async_remote_copy(
                src_ref=piece(w, 2 * x + y, c), dst_ref=outs[w].at[4 * px + 2 * py + pc], send_sem=send_sems.at[w, k],
                recv_sem=recv_sems.at[w, k], device_id=(px, py, pc), device_id_type=MESH)

        return local, send, recv

    def start(self, ins, outs, sems):
        local, send, _ = self._ctx(ins, outs, sems)
        for w in range(self.n):
            local(w).start()
            for k in range(1, N_DEV):
                send(w, k).start()

    def mid(self, ins, outs, sems):
        pass

    def end(self, ins, outs, sems):
        local, send, recv = self._ctx(ins, outs, sems)
        for w in range(self.n):
            for k in range(1, N_DEV):
                recv(w, k).wait_recv()
        for w in range(self.n):
            for k in range(1, N_DEV):
                send(w, k).wait_send()
            local(w).wait()


class _Both:
    def __init__(self, a, b):
        self.a, self.b = a, b
        self.inputs = a.inputs + b.inputs
        self.out_shapes = a.out_shapes + b.out_shapes
        self.scratch = a.scratch + b.scratch
        self.n_in, self.n_out = a.n_in + b.n_in, a.n_out + b.n_out

    def _split(self, ins, outs, sems):
        a, na = self.a, len(self.a.scratch)
        return (ins[:a.n_in], outs[:a.n_out], sems[:na]), (ins[a.n_in:], outs[a.n_out:], sems[na:])

    def start(self, ins, outs, sems):
        pa, pb = self._split(ins, outs, sems)
        self.a.start(*pa)
        self.b.start(*pb)

    def mid(self, ins, outs, sems):
        pa, pb = self._split(ins, outs, sems)
        self.a.mid(*pa)
        self.b.mid(*pb)

    def end(self, ins, outs, sems):
        pa, pb = self._split(ins, outs, sems)
        self.a.end(*pa)
        self.b.end(*pb)


def _call(body, *, name, grid, in_specs, out_specs, out_shape, scratch_shapes=(), semantics, args, comm=None):
    if comm is None:
        outs = pl.pallas_call(
            body, name=name, grid=grid, in_specs=in_specs, out_specs=out_specs, out_shape=out_shape,
            scratch_shapes=list(scratch_shapes), compiler_params=_params(*semantics))(*args)
        return outs, []
    n_in, n_out, n_scr = len(in_specs), len(out_specs), len(scratch_shapes)

    total = math.prod(grid)
    first, middle, last = 0, (3 * total) // 4 - 1, total - 1
    assert first <= middle < last

    def at(step):
        lin = pl.program_id(0)
        for d in range(1, len(grid)):
            lin = lin * grid[d] + pl.program_id(d)
        return lin == step

    def hosted(*refs):
        h_in, c_in = refs[:n_in], refs[n_in:n_in + comm.n_in]
        rest = refs[n_in + comm.n_in:]
        h_out, c_out = rest[:n_out], rest[n_out:n_out + comm.n_out]
        rest = rest[n_out + comm.n_out:]
        h_scr, c_scr = rest[:n_scr], rest[n_scr:]

        @pl.when(at(first))
        def _():
            comm.start(c_in, c_out, c_scr)

        body(*h_in, *h_out, *h_scr)

        @pl.when(at(middle))
        def _():
            comm.mid(c_in, c_out, c_scr)

        @pl.when(at(last))
        def _():
            comm.end(c_in, c_out, c_scr)

    any_spec = pl.BlockSpec(memory_space=pl.ANY)
    outs = pl.pallas_call(
        hosted, name=name, grid=grid, in_specs=list(in_specs) + [any_spec] * comm.n_in,
        out_specs=list(out_specs) + [any_spec] * comm.n_out, out_shape=list(out_shape) + comm.out_shapes,
        scratch_shapes=list(scratch_shapes) + comm.scratch,
        compiler_params=_params(*(["arbitrary"] * len(grid))))(*args, *comm.inputs)
    return outs[:n_out], outs[n_out:]


def sum_swap(pieces, name, comm=None):
    nl = len(pieces)
    _, r2, cc = pieces[0].shape
    tr = 128 if r2 % 128 == 0 else r2 // 2
    n = r2 // tr

    def body(*refs):
        p_refs, out = refs[:nl], refs[nl]
        slots, send_sems, local_sems, recv_sem = refs[nl + 1:]
        x, y, c = _me()
        sibling = (x, y, 1 - c)
        l, i = pl.program_id(0), pl.program_id(1)
        step = l * n + i

        def rows(st, core):
            return out.at[st // n, pl.ds(core * r2 + (st % n) * tr, tr), :]

        def copies(st):
            slot = st % 2
            local = pltpu.make_async_copy(slots.at[slot], rows(st, c), local_sems.at[slot])
            remote = pltpu.make_async_remote_copy(
                src_ref=slots.at[slot], dst_ref=rows(st, c), send_sem=send_sems.at[slot], recv_sem=recv_sem,
                device_id=sibling, device_id_type=MESH)
            return local, remote

        for ll in range(nl):
            @pl.when(l == ll)
            def _():
                acc = p_refs[ll][0].astype(F32)
                for d in range(1, N_DEV):
                    acc = acc + p_refs[ll][d].astype(F32)
                slots[step % 2] = acc

        for cp in copies(step):
            cp.start()

        @pl.when(step >= 1)
        def _():
            local, remote = copies(step - 1)
            local.wait()
            remote.wait_send()

        @pl.when(step == nl * n - 1)
        def _():
            local, remote = copies(step)
            local.wait()
            remote.wait_send()
            theirs = out.at[:, pl.ds((1 - c) * r2, r2), :]
            pltpu.make_async_remote_copy(src_ref=theirs, dst_ref=theirs, send_sem=send_sems.at[0],
                                         recv_sem=recv_sem, device_id=sibling, device_id_type=MESH).wait_recv()

    def piece_spec(ll):
        def index(l, i):
            return (0, jnp.where(l == ll, i, jnp.where(l < ll, 0, n - 1)), 0)
        return pl.BlockSpec((N_DEV, tr, cc), index)

    (out,), got = _call(
        body, name=name, grid=(nl, n),
        in_specs=[piece_spec(ll) for ll in range(nl)],
        out_specs=[pl.BlockSpec(memory_space=pl.ANY)],
        out_shape=[jax.ShapeDtypeStruct((nl, 2 * r2, cc), F32)],
        scratch_shapes=[pltpu.VMEM((2, tr, cc), F32), pltpu.SemaphoreType.DMA((2,)), pltpu.SemaphoreType.DMA((2,)),
                        pltpu.SemaphoreType.DMA(())],
        semantics=("arbitrary", "arbitrary"), args=tuple(pieces), comm=comm)
    return (out, got) if comm is not None else out


def sum_pieces(pieces, name):
    _, R, C = pieces.shape
    tr = _tile(R, 128) if R % 128 == 0 else R

    def body(p_ref, o_ref):
        acc = p_ref[0].astype(F32)
        for d in range(1, N_DEV):
            acc = acc + p_ref[d].astype(F32)
        o_ref[...] = acc

    return pl.pallas_call(
        body, name=name, grid=(R // tr,),
        in_specs=[pl.BlockSpec((N_DEV, tr, C), lambda i: (0, i, 0))],
        out_specs=pl.BlockSpec((tr, C), lambda i: (i, 0)),
        out_shape=jax.ShapeDtypeStruct((R, C), F32),
        compiler_params=_params("parallel"),
    )(pieces)


def adamw(w, g, m, v, name):
    Lyr, R, C = w.shape
    tr = _tile(R, 256) if R % 8 == 0 else R
    c1 = 1.0 / (1.0 - ADAM_B1 ** ADAM_STEP)
    c2 = 1.0 / (1.0 - ADAM_B2 ** ADAM_STEP)

    def body(w_ref, g_ref, m_ref, v_ref, d_ref, nm_ref, nv_ref):
        gv = g_ref[...]
        nm = ADAM_B1 * m_ref[...] + (1.0 - ADAM_B1) * gv
        nv = ADAM_B2 * v_ref[...] + (1.0 - ADAM_B2) * (gv * gv)
        nm_ref[...] = nm
        nv_ref[...] = nv
        d_ref[...] = -ADAM_LR * ((nm * c1) / (jnp.sqrt(nv * c2) + ADAM_EPS) + ADAM_WD * w_ref[...])

    spec = pl.BlockSpec((None, tr, C), lambda l, i: (l, i, 0))
    shp = jax.ShapeDtypeStruct(w.shape, F32)
    return pl.pallas_call(
        body, name=name, grid=(Lyr, R // tr),
        in_specs=[spec] * 4, out_specs=[spec] * 3, out_shape=[shp] * 3,
        compiler_params=_params("parallel", "parallel"),
    )(w, g, m, v)


def _rope_tables(T):
    pos = jnp.arange(T, dtype=F32)
    inv_freq = THETA ** (-jnp.arange(0, ROT, 2, dtype=F32) / ROT)
    ang = pos[:, None] * inv_freq[None, :]
    cs = jnp.concatenate([jnp.cos(ang), jnp.sin(ang)], axis=1)
    half = ROT // 2
    lane = jnp.arange(3 * LANES)
    table, lm = lane // LANES, lane % HD
    src = jnp.where(table == 0, lm % half, half + lm % half)
    i32 = lambda b: b.astype(jnp.int32)
    sign = jnp.where(table == 0, i32(lm < ROT), jnp.where(table == 1, -i32(lm < half), i32((lm >= half) & (lm < ROT))))
    place = (jnp.arange(ROT)[:, None] == src[None, :]) * sign[None, :].astype(F32)
    ones = ((table == 0) & (lm >= ROT)).astype(F32)
    return jnp.dot(cs, place, precision=lax.Precision.HIGHEST) + ones[None, :]


def _tab_specs(rows):
    return [pl.BlockSpec((rows, LANES), lambda i, k=k: (i, k)) for k in range(3)]


def kernel(x, attn_norm, attn_w_qkv, attn_w_o, attn_sink, conv_norm, conv_w_pw1, conv_b_pw1, conv_w_dw, conv_b_dw, conv_ln_g, conv_ln_b, conv_w_pw2, conv_b_pw2, ffn_norm, ffn_w_gu, ffn_w_down, final_norm, loss_target, m_attn_norm, m_attn_w_qkv, m_attn_w_o, m_attn_sink, m_conv_norm, m_conv_w_pw1, m_conv_b_pw1, m_conv_w_dw, m_conv_b_dw, m_conv_ln_g, m_conv_ln_b, m_conv_w_pw2, m_conv_b_pw2, m_ffn_norm, m_ffn_w_gu, m_ffn_w_down, m_final_norm, v_attn_norm, v_attn_w_qkv, v_attn_w_o, v_attn_sink, v_conv_norm, v_conv_w_pw1, v_conv_b_pw1, v_conv_w_dw, v_conv_b_dw, v_conv_ln_g, v_conv_ln_b, v_conv_w_pw2, v_conv_b_pw2, v_ffn_norm, v_ffn_w_gu, v_ffn_w_down, v_final_norm):
    T = x.shape[1]
    x0 = x[0]
    target = loss_target[0]
    ix, iy = lax.axis_index("x"), lax.axis_index("y")
    chip = 2 * ix + iy
    rc = rs1 = rs2 = _rope_tables(T)

    bf = lambda t: t.astype(BF16)
    col_row = [False, True]

    def place(vec, width):
        return lax.dynamic_update_slice(jnp.zeros((vec.shape[0], N_CHIPS * width), F32), vec, (0, chip * width))

    small_rows = jnp.concatenate([
        place(conv_norm, 256), place(conv_b_pw1, 512).reshape(2, D), place(conv_b_dw, 256), place(conv_ln_g, 256),
        place(conv_ln_b, 256), place(conv_b_pw2, 256), jnp.zeros((1, D), F32),
        place(conv_w_dw[0], 256), jnp.zeros((1, D), F32)], axis=0)

    h0, (w_qkv,) = rms_first(x0, attn_norm, comm=_Gather([bf(attn_w_qkv[0])], [False]))
    qkv, (w_o, got) = qkv_proj(h0, w_qkv, rc, rs1, rs2,
                               comm=_Both(_Gather([bf(attn_w_o[0])], [True]), _Scatter([], small_rows)))
    psmall = sum_pieces(got, "sum_small_params") * 0.5
    p_conv_norm, p_b_pw1 = psmall[0:1], psmall[1:3].reshape(1, 2 * D)
    p_b_dw, p_ln_g, p_ln_b, p_b_pw2 = psmall[3:4], psmall[4:5], psmall[5:6], psmall[6:7]
    p_w_dw = psmall[8:40]
    sink = attn_sink[0]
    o, (w_gu0,) = attn_fwd(qkv, sink, comm=_Gather([bf(ffn_w_gu[0])], [False]))
    zero_b = jnp.zeros((1, D), F32)
    zero_gu = jnp.zeros((1, 2 * DFF), F32)
    x1, h1, gu0, act0, (w_down0, w_pw1, w_pw2) = rms_mm_gate(
        (o, w_o, zero_b, x0), ffn_norm[0:1], w_gu0, zero_gu, DFF, True, BF16, "ffn0_up",
        comm=_Gather([bf(ffn_w_down[0]), bf(conv_w_pw1[0]), bf(conv_w_pw2[0])], [True, False, True]))
    x2, h2, pre, glu, _ = rms_mm_gate((act0, w_down0, zero_b, x1), p_conv_norm, w_pw1, p_b_pw1, D, False, F32,
                                      "conv_pw1")
    dwc, sw, (w_gu1, w_down1) = conv_fwd(glu, p_w_dw, p_b_dw, p_ln_g, p_ln_b,
                                         comm=_Gather([bf(ffn_w_gu[1]), bf(ffn_w_down[1])], col_row))
    x3, h3, gu1, act1, _ = rms_mm_gate((sw, w_pw2, p_b_pw2, x2), ffn_norm[1:2], w_gu1, zero_gu, DFF, True, BF16,
                                       "ffn1_up")
    dx4, loss_part, d_final = mm_res_loss(act1, w_down1, x3, final_norm.reshape(1, D), target)

    dgu1, _ = swiglu_bwd(dx4, w_down1, gu1, "ffn1_down_bwd")
    g_down1 = dw_row(act1, dx4, "ffn1_down_dw")
    dx3, d_ffn1, ddwc, d_ln_g, d_ln_b, d_b_pw2, _ = mm_bt_rmsbwd(
        dgu1, w_gu1, x3, ffn_norm[1:2], dx4, "ffn1_up_bwd", conv_tail=(w_pw2, dwc, p_ln_g, p_ln_b))
    g_gu1 = dw_col(h3, dgu1, "ffn1_up_dw")

    g_pw2 = dw_row(sw, dx3, "conv_pw2_dw")
    dpre, d_w_dw, d_b_dw, d_b_pw1, (r_gu1, r_down1) = conv_bwd(ddwc, glu, pre, p_w_dw,
                                                               comm=_Scatter([g_gu1, g_down1]))
    dx2, d_conv_norm, _ = mm_bt_rmsbwd(dpre, w_pw1, x2, p_conv_norm, dx3, "conv_pw1_bwd")
    g_pw1 = dw_col(h2, dpre, "conv_pw1_dw")

    dgu0, (r_pw1, r_pw2) = swiglu_bwd(dx2, w_down0, gu0, "ffn0_down_bwd", comm=_Scatter([g_pw1, g_pw2]))
    g_down0 = dw_row(act0, dx2, "ffn0_down_dw")
    dx1, d_ffn0, do, _ = mm_bt_rmsbwd(dgu0, w_gu0, x1, ffn_norm[0:1], dx2, "ffn0_up_bwd", proj_w=w_o)
    g_gu0 = dw_col(h1, dgu0, "ffn0_up_dw")

    g_o = dw_row(o, dx1, "attn_out_dw")
    dq, dkc, dvc, d_sink, (r_gu0, r_down0, r_o) = attn_bwd(qkv, o, do, sink, rc, rs1, rs2,
                                                           comm=_Scatter([g_gu0, g_down0, g_o]))
    dqkv = kv_sum(dq, dkc, dvc, rc, rs1, rs2)[None]
    g_qkv = dw_col(h0, dqkv, "attn_qkv_dw")
    dx0, d_attn_norm, _ = mm_bt_rmsbwd(dqkv, w_qkv, x0, attn_norm, dx1, "attn_qkv_bwd")

    pad16 = lambda t: jnp.concatenate([t, jnp.zeros((1, D - t.shape[1]), F32)], axis=1)
    small_g = jnp.concatenate([
        d_attn_norm, pad16(d_sink), d_conv_norm, d_b_pw1.reshape(2, D), d_b_dw, d_ln_g, d_ln_b, d_b_pw2,
        d_ffn0, d_ffn1, d_final, pad16(loss_part), jnp.zeros((3, D), F32), d_w_dw], axis=0)
    gf_gu, (r_qkv, r_small) = sum_swap([r_gu0, r_gu1], "sum_gu", comm=_Scatter([g_qkv], small_g))
    gf_down = sum_swap([r_down0, r_down1], "sum_down")
    gf_pw1, gf_pw2 = sum_swap([r_pw1], "sum_pw1"), sum_swap([r_pw2], "sum_pw2")
    gf_qkv, gf_o = sum_swap([r_qkv], "sum_qkv"), sum_swap([r_o], "sum_o")
    gs = sum_pieces(r_small, "sum_small_grads")
    loss = gs[12, 0]

    def take(row0, nrows, width):
        return lax.dynamic_slice(gs, (row0, chip * width), (nrows, width))

    grads = {
        "attn_norm": gs[0:1], "attn_w_qkv": gf_qkv, "attn_w_o": gf_o, "attn_sink": gs[1:2, :N_HEADS],
        "conv_norm": take(2, 1, 256), "conv_w_pw1": gf_pw1,
        "conv_b_pw1": lax.dynamic_slice(gs[3:5].reshape(1, 2 * D), (0, chip * 512), (1, 512)),
        "conv_w_dw": take(16, 32, 256)[None, :CONV_W], "conv_b_dw": take(5, 1, 256), "conv_ln_g": take(6, 1, 256),
        "conv_ln_b": take(7, 1, 256), "conv_w_pw2": gf_pw2, "conv_b_pw2": take(8, 1, 256),
        "ffn_norm": gs[9:11], "ffn_w_gu": gf_gu, "ffn_w_down": gf_down, "final_norm": gs[11],
    }
    weights = dict(attn_norm=attn_norm, attn_w_qkv=attn_w_qkv, attn_w_o=attn_w_o, attn_sink=attn_sink,
                   conv_norm=conv_norm, conv_w_pw1=conv_w_pw1, conv_b_pw1=conv_b_pw1, conv_w_dw=conv_w_dw,
                   conv_b_dw=conv_b_dw, conv_ln_g=conv_ln_g, conv_ln_b=conv_ln_b, conv_w_pw2=conv_w_pw2,
                   conv_b_pw2=conv_b_pw2, ffn_norm=ffn_norm, ffn_w_gu=ffn_w_gu, ffn_w_down=ffn_w_down,
                   final_norm=final_norm)
    m_in = dict(attn_norm=m_attn_norm, attn_w_qkv=m_attn_w_qkv, attn_w_o=m_attn_w_o, attn_sink=m_attn_sink,
                conv_norm=m_conv_norm, conv_w_pw1=m_conv_w_pw1, conv_b_pw1=m_conv_b_pw1, conv_w_dw=m_conv_w_dw,
                conv_b_dw=m_conv_b_dw, conv_ln_g=m_conv_ln_g, conv_ln_b=m_conv_ln_b, conv_w_pw2=m_conv_w_pw2,
                conv_b_pw2=m_conv_b_pw2, ffn_norm=m_ffn_norm, ffn_w_gu=m_ffn_w_gu, ffn_w_down=m_ffn_w_down,
                final_norm=m_final_norm)
    v_in = dict(attn_norm=v_attn_norm, attn_w_qkv=v_attn_w_qkv, attn_w_o=v_attn_w_o, attn_sink=v_attn_sink,
                conv_norm=v_conv_norm, conv_w_pw1=v_conv_w_pw1, conv_b_pw1=v_conv_b_pw1, conv_w_dw=v_conv_w_dw,
                conv_b_dw=v_conv_b_dw, conv_ln_g=v_conv_ln_g, conv_ln_b=v_conv_ln_b, conv_w_pw2=v_conv_w_pw2,
                conv_b_pw2=v_conv_b_pw2, ffn_norm=v_ffn_norm, ffn_w_gu=v_ffn_w_gu, ffn_w_down=v_ffn_w_down,
                final_norm=v_final_norm)
    order = list(weights)
    g_out, d_out, m_out, v_out = [], [], [], []
    for nm in order:
        w = weights[nm]
        shape = w.shape
        as3 = lambda t: t.reshape((1,) * (3 - len(shape)) + shape) if len(shape) < 3 else t.reshape(shape)
        g3 = as3(grads[nm].reshape(shape))
        delta, nm_, nv_ = adamw(as3(w), g3, as3(m_in[nm]), as3(v_in[nm]), "adamw_" + nm)
        g_out.append(g3.reshape(shape))
        d_out.append(delta.reshape(shape))
        m_out.append(nm_.reshape(shape))
        v_out.append(nv_.reshape(shape))
    return (loss, dx0[None], *g_out, *d_out, *m_out, *v_out)
```

```python
import functools
import math

import jax
import jax.numpy as jnp
from jax import lax
from jax.experimental import pallas as pl
from jax.experimental.pallas import tpu as pltpu

F32 = jnp.float32
BF16 = jnp.bfloat16

D = 1024
N_HEADS = 16
N_KV = 4
GROUP = N_HEADS // N_KV
HD = 64
ROT = 16
THETA = 500000.0
BLK = 128
QKV = (N_HEADS + 2 * N_KV) * HD
KV_OFF = N_HEADS * HD
DFF = 2816
CONV_W = 31
CONV_PAD = 15
HALO = 16
CONV_JB = 8
CONV_JB_BWD = 8
EPS = 1e-6
NEG = -1e30
N_CHIPS = 4
N_DEV = 8
LANES = 128
SUBLANES = 8

ADAM_LR, ADAM_B1, ADAM_B2, ADAM_EPS, ADAM_WD, ADAM_STEP = 0.001, 0.9, 0.999, 1e-08, 0.01, 10

VMEM_LIMIT = 56 * 1024 * 1024
MESH = pl.DeviceIdType.MESH


def _params(*sem):
    return pltpu.CompilerParams(dimension_semantics=sem, vmem_limit_bytes=VMEM_LIMIT)


def _tile(n, want):
    if n <= want:
        return n
    for t in range(want, 7, -1):
        if n % t == 0 and t % 8 == 0:
            return t
    return n


MXU_COLS = 256


def _col_chunks(n):
    return [slice(c, min(c + MXU_COLS, n)) for c in range(0, n, MXU_COLS)]


def _sigmoid(v):
    return jax.nn.sigmoid(v)


def _rms_fwd(xv, gain):
    r = lax.rsqrt(jnp.mean(xv * xv, axis=-1, keepdims=True) + EPS)
    return xv * r * gain


def _rms_bwd(dh, xv, gain, dres):
    r = lax.rsqrt(jnp.mean(xv * xv, axis=-1, keepdims=True) + EPS)
    xhat = xv * r
    gy = dh * gain
    dx = r * (gy - xhat * jnp.mean(gy * xhat, axis=-1, keepdims=True))
    return dx + dres, dh * xhat


def _rope(blk, c, s1, s2):
    return blk * c + pltpu.roll(blk, LANES - ROT // 2, 1) * s1 + pltpu.roll(blk, ROT // 2, 1) * s2


def _dot(a, b):
    return jnp.dot(a, b, preferred_element_type=F32)


def _dot_tb(a, b):
    return lax.dot_general(a, b, (((1,), (1,)), ((), ())), preferred_element_type=F32)


def _dot_ta(a, b):
    return lax.dot_general(a, b, (((0,), (0,)), ((), ())), preferred_element_type=F32)


def rms_first(x, gain, comm):
    T = x.shape[0]
    tm = _tile(T, 512)

    def body(x_ref, g_ref, h_ref):
        h_ref[...] = _rms_fwd(x_ref[...], g_ref[...]).astype(BF16)

    (h,), got = _call(
        body, name="rms_first", grid=(T // tm,),
        in_specs=[pl.BlockSpec((tm, D), lambda i: (i, 0)), pl.BlockSpec((1, D), lambda i: (0, 0))],
        out_specs=[pl.BlockSpec((tm, D), lambda i: (i, 0))], out_shape=[jax.ShapeDtypeStruct((T, D), BF16)],
        semantics=("parallel",), args=(x, gain), comm=comm)
    return h, got


def qkv_proj(h, w, rc, rs1, rs2, comm=None):
    T = h.shape[0]
    tm = _tile(T, 512)

    def body(h_ref, w_ref, c_ref, s1_ref, s2_ref, qkv_ref):
        acc = _dot(h_ref[...], w_ref[...])
        c, s1, s2 = c_ref[...], s1_ref[...], s2_ref[...]
        n_rot = (KV_OFF + N_KV * HD) // LANES
        for j in range(n_rot):
            sl = slice(LANES * j, LANES * (j + 1))
            roped = _rope(acc[:, sl], c, s1, s2)
            if j < KV_OFF // LANES:
                roped = roped * Q_SCALE
            qkv_ref[:, sl] = roped.astype(BF16)
        qkv_ref[:, n_rot * LANES:] = acc[:, n_rot * LANES:].astype(BF16)

    row = lambda i: (i, 0)
    full = lambda i: (0, 0)
    (qkv,), got = _call(
        body, name="qkv_proj", grid=(T // tm,),
        in_specs=[pl.BlockSpec((tm, D), row), pl.BlockSpec((D, QKV), full), *_tab_specs(tm)],
        out_specs=[pl.BlockSpec((tm, QKV), row)],
        out_shape=[jax.ShapeDtypeStruct((T, QKV), BF16)],
        semantics=("parallel",), args=(h, w, rc, rs1, rs2), comm=comm)
    return qkv, got


Q_SCALE = 1.0 / math.sqrt(HD)


def _attn_mask(n, T):
    ci = lax.broadcasted_iota(jnp.int32, (3 * BLK, BLK), 0)
    qi = lax.broadcasted_iota(jnp.int32, (3 * BLK, BLK), 1)
    key_pos = n * BLK - BLK + ci
    return (jnp.abs(ci - BLK - qi) <= BLK) & (key_pos >= 0) & (key_pos < T)


def _kv_padded(kv, first_tile):
    low = lax.broadcasted_iota(jnp.int32, (3 * BLK, LANES), 1) < HD
    zero = jnp.zeros((3 * BLK, LANES), BF16)
    out = {}
    for g in range(N_KV):
        t = kv[:, (first_tile + g // 2) * LANES:(first_tile + g // 2 + 1) * LANES]
        swapped = jnp.concatenate([t[:, HD:], t[:, :HD]], axis=1)
        for p in range(2):
            out[g, p] = jnp.where(low if p == 0 else ~low, t if g % 2 == p else swapped, zero)
    return out


def _pair_products(kvx, tile_of):
    both = {g: jnp.concatenate([kvx[g, 0], kvx[g, 1]], axis=0) for g in range(N_KV)}
    out = []
    for j in range(N_HEADS // 2):
        prod = _dot_tb(both[2 * j // GROUP], tile_of(j))
        out += [prod[:3 * BLK], prod[3 * BLK:]]
    return out


def _softmax_sink(s, valid, sk):
    s = jnp.where(valid, s, NEG)
    m = jnp.maximum(jnp.max(s, axis=0, keepdims=True), sk)
    e = jnp.exp(s - m)
    es = jnp.exp(sk - m)
    inv = 1.0 / (jnp.sum(e, axis=0, keepdims=True) + es)
    return e * inv, es * inv


def _attn_specs(T):
    nb = T // BLK
    kv_blk = 2 * N_KV * HD
    kv_col = KV_OFF // kv_blk
    q_spec = pl.BlockSpec((BLK, KV_OFF), lambda n: (n, 0))
    prev = pl.BlockSpec((BLK, kv_blk), lambda n: (jnp.maximum(n - 1, 0), kv_col))
    own = pl.BlockSpec((BLK, kv_blk), lambda n: (n, kv_col))
    nxt = pl.BlockSpec((BLK, kv_blk), lambda n: (jnp.minimum(n + 1, nb - 1), kv_col))
    return nb, q_spec, prev, own, nxt


def attn_fwd(qkv, sink, comm=None):
    T = qkv.shape[0]
    nb, q_spec, prev, own, nxt = _attn_specs(T)

    def body(sink_ref, q_ref, kp_ref, ko_ref, kn_ref, o_ref):
        valid = _attn_mask(pl.program_id(0), T)
        kv = jnp.concatenate([kp_ref[...], ko_ref[...], kn_ref[...]], axis=0)
        kx, vx = _kv_padded(kv, 0), _kv_padded(kv, 2)
        ss = _pair_products(kx, lambda j: q_ref[:, j * LANES:(j + 1) * LANES])
        ps = [_softmax_sink(ss[h], valid, sink_ref[h])[0].astype(BF16) for h in range(N_HEADS)]
        vxt = {k: v.T for k, v in vx.items()}
        for j in range(N_HEADS // 2):
            g = 2 * j // GROUP
            o_t = _dot(vxt[g, 0], ps[2 * j]) + _dot(vxt[g, 1], ps[2 * j + 1])
            o_ref[:, j * LANES:(j + 1) * LANES] = o_t.T.astype(BF16)

    (o,), got = _call(
        body, name="attn_fwd", grid=(nb,),
        in_specs=[pl.BlockSpec(memory_space=pltpu.SMEM), q_spec, prev, own, nxt],
        out_specs=[pl.BlockSpec((BLK, D), lambda n: (n, 0))],
        out_shape=[jax.ShapeDtypeStruct((T, D), BF16)],
        semantics=("parallel",), args=(sink, qkv, qkv, qkv, qkv), comm=comm)
    return o, got


def rms_mm_gate(x, gain, w, bias, H, swiglu, act_dtype, name, comm=None):
    fused = isinstance(x, tuple)
    T = (x[0] if fused else x).shape[0]
    tm = _tile(T, 512)

    def body(*refs):
        if fused:
            a_ref, wp_ref, bp_ref, r_ref, g_ref, w_ref, b_ref, x_ref, h_ref, pre_ref, act_ref = refs
            xv = _dot(a_ref[...], wp_ref[...]) + bp_ref[...] + r_ref[...]
            x_ref[...] = xv
        else:
            x_ref, g_ref, w_ref, b_ref, h_ref, pre_ref, act_ref = refs
            xv = x_ref[...]
        h = _rms_fwd(xv, g_ref[...]).astype(BF16)
        h_ref[...] = h
        for cs in _col_chunks(H):
            cs2 = slice(H + cs.start, H + cs.stop)
            a = _dot(h, w_ref[:, cs]) + b_ref[:, cs]
            b = _dot(h, w_ref[:, cs2]) + b_ref[:, cs2]
            pre_ref[0, :, cs] = a.astype(BF16)
            pre_ref[1, :, cs] = b.astype(BF16)
            if swiglu:
                act = a * _sigmoid(a) * b
            else:
                act = a * _sigmoid(b)
            act_ref[:, cs] = act.astype(act_dtype)

    row = lambda i: (i, 0)
    full = lambda i: (0, 0)
    if fused:
        K = x[0].shape[1]
        x_specs = [pl.BlockSpec((tm, K), row), pl.BlockSpec((K, D), full, pipeline_mode=pl.Buffered(1)),
                   pl.BlockSpec((1, D), full), pl.BlockSpec((tm, D), row)]
        x_out = ([pl.BlockSpec((tm, D), row)], [jax.ShapeDtypeStruct((T, D), F32)])
        x_args = tuple(x)
    else:
        x_specs, x_out, x_args = [pl.BlockSpec((tm, D), row)], ([], []), (x,)
    outs, got = _call(
        body, name=name, grid=(T // tm,),
        in_specs=x_specs + [pl.BlockSpec((1, D), full),
                            pl.BlockSpec((D, 2 * H), full, pipeline_mode=pl.Buffered(1)), pl.BlockSpec((1, 2 * H), full)],
        out_specs=x_out[0] + [pl.BlockSpec((tm, D), row), pl.BlockSpec((2, tm, H), lambda i: (0, i, 0)),
                              pl.BlockSpec((tm, H), row)],
        out_shape=x_out[1] + [jax.ShapeDtypeStruct((T, D), BF16), jax.ShapeDtypeStruct((2, T, H), BF16),
                              jax.ShapeDtypeStruct((T, H), act_dtype)],
        semantics=("parallel",), args=x_args + (gain, w, bias), comm=comm)
    return (*outs, got)


def _conv_tiles(T):
    tt = _tile(T, 512)
    return tt, tt // SUBLANES, D // LANES


def _fill_strided(ext, p, L):
    main = p[HALO:HALO + SUBLANES * L, :].reshape(SUBLANES, L, LANES)
    ext[CONV_PAD:CONV_PAD + L] = jnp.swapaxes(main, 0, 1)

    def ibody(i, carry):
        ext[i] = p[pl.ds(i + 1, SUBLANES, stride=L), :]
        ext[i + CONV_PAD + L] = p[pl.ds(i + CONV_PAD + L + 1, SUBLANES, stride=L), :]
        return carry

    lax.fori_loop(0, CONV_PAD, ibody, 0, unroll=3)


def _conv_specs(T, tt):
    main = pl.BlockSpec((tt, D), lambda i: (i, 0))
    per = tt // HALO
    prev = pl.BlockSpec((HALO, D), lambda i: (jnp.maximum(i * per - 1, 0), 0))
    nxt = pl.BlockSpec((HALO, D), lambda i: (jnp.minimum((i + 1) * per, T // HALO - 1), 0))
    return main, prev, nxt


def _fill_pad(pad, main_ref, prev_ref, next_ref, i, n_i, tt, nlt):
    keep_p = (i > 0).astype(F32)
    keep_n = (i < n_i - 1).astype(F32)
    for lt in range(nlt):
        sl = slice(lt * LANES, (lt + 1) * LANES)
        pad[lt, 0:HALO, :] = prev_ref[:, sl] * keep_p
        pad[lt, HALO:HALO + tt, :] = main_ref[:, sl]
        pad[lt, HALO + tt:2 * HALO + tt, :] = next_ref[:, sl] * keep_n


def conv_fwd(glu, w_dw, b_dw, ln_g, ln_b, comm=None):
    T = glu.shape[0]
    tt, L, nlt = _conv_tiles(T)
    n_i = T // tt
    main, prev, nxt = _conv_specs(T, tt)

    def body(x_ref, xp_ref, xn_ref, w_ref, b_ref, g_ref, bb_ref, dwc_ref, sw_ref, pad, ob, ext, wk):
        i = pl.program_id(0)
        _fill_pad(pad, x_ref, xp_ref, xn_ref, i, n_i, tt, nlt)
        for lt in range(nlt):
            sl = slice(lt * LANES, (lt + 1) * LANES)
            o = ob.at[lt]
            _fill_strided(ext, pad.at[lt], L)
            for k in range(CONV_W):
                wk[k] = jnp.broadcast_to(w_ref[k:k + 1, sl], (SUBLANES, LANES))

            def jbody(jb, carry):
                j = jb * CONV_JB
                accs = [None] * CONV_JB
                for m in range(CONV_W + CONV_JB - 1):
                    e = ext[j + m]
                    for u in range(CONV_JB):
                        if 0 <= m - u < CONV_W:
                            t = e * wk[m - u]
                            accs[u] = t if accs[u] is None else accs[u] + t
                for u in range(CONV_JB):
                    o[pl.ds(j + u, SUBLANES, stride=L), :] = accs[u]
                return carry

            lax.fori_loop(0, L // CONV_JB, jbody, 0)
        y = jnp.concatenate([ob[lt] for lt in range(nlt)], axis=1) + b_ref[...]
        dwc_ref[...] = y
        mu = jnp.mean(y, axis=-1, keepdims=True)
        yc = y - mu
        var = jnp.mean(yc * yc, axis=-1, keepdims=True)
        z = yc * lax.rsqrt(var + EPS) * g_ref[...] + bb_ref[...]
        sw_ref[...] = (z * _sigmoid(z)).astype(BF16)

    full = lambda i: (0, 0)
    (dwc, sw), got = _call(
        body, name="conv_fwd", grid=(n_i,),
        in_specs=[main, prev, nxt, pl.BlockSpec((32, D), full), pl.BlockSpec((1, D), full),
                  pl.BlockSpec((1, D), full), pl.BlockSpec((1, D), full)],
        out_specs=[pl.BlockSpec((tt, D), lambda i: (i, 0)), pl.BlockSpec((tt, D), lambda i: (i, 0))],
        out_shape=[jax.ShapeDtypeStruct((T, D), F32), jax.ShapeDtypeStruct((T, D), BF16)],
        scratch_shapes=[pltpu.VMEM((nlt, tt + 2 * HALO, LANES), F32), pltpu.VMEM((nlt, tt, LANES), F32),
                        pltpu.VMEM((L + 2 * HALO, SUBLANES, LANES), F32), pltpu.VMEM((32, SUBLANES, LANES), F32)],
        semantics=("parallel",), args=(glu, glu, glu, w_dw, b_dw, ln_g, ln_b), comm=comm)
    return dwc, sw, got


def mm_res_loss(a, w, resid, gain, target):
    T, K = a.shape
    tm = _tile(T, 512)

    def body(a_ref, w_ref, r_ref, g_ref, t_ref, dx_ref, loss_ref, dg_ref):
        @pl.when(pl.program_id(0) == 0)
        def _():
            loss_ref[...] = jnp.zeros_like(loss_ref)
            dg_ref[...] = jnp.zeros_like(dg_ref)

        xv, gain_v = _dot(a_ref[...], w_ref[...]) + r_ref[...], g_ref[...]
        err = _rms_fwd(xv, gain_v) - t_ref[...]
        part = 0.5 * jnp.sum(jnp.mean(err * err, axis=-1, keepdims=True), axis=0, keepdims=True)
        loss_ref[...] += jnp.broadcast_to(part, loss_ref.shape)
        dx, dgr = _rms_bwd(err * (1.0 / D), xv, gain_v, 0.0)
        dx_ref[...] = dx
        dg_ref[...] += jnp.sum(dgr, axis=0, keepdims=True)

    row = lambda i: (i, 0)
    full = lambda i: (0, 0)
    return pl.pallas_call(
        body, name="ffn1_down_loss", grid=(T // tm,),
        in_specs=[pl.BlockSpec((tm, K), row), pl.BlockSpec((K, D), full), pl.BlockSpec((tm, D), row),
                  pl.BlockSpec((1, D), full), pl.BlockSpec((tm, D), row)],
        out_specs=[pl.BlockSpec((tm, D), row), pl.BlockSpec((1, LANES), full), pl.BlockSpec((1, D), full)],
        out_shape=[jax.ShapeDtypeStruct((T, D), F32), jax.ShapeDtypeStruct((1, LANES), F32),
                   jax.ShapeDtypeStruct((1, D), F32)],
        compiler_params=_params("arbitrary"),
    )(a, w, resid, gain, target)


def swiglu_bwd(dx, w_down, pre, name, comm=None):
    T = dx.shape[0]
    H = w_down.shape[0]
    tm = _tile(T, 512)

    def body(dx_ref, w_ref, pre_ref, dpre_ref):
        dxb = dx_ref[...].astype(BF16)
        for cs in _col_chunks(H):
            dact = _dot_tb(dxb, w_ref[cs, :])
            g = pre_ref[0, :, cs].astype(F32)
            u = pre_ref[1, :, cs].astype(F32)
            sg = _sigmoid(g)
            dpre_ref[0, :, cs] = (dact * u * sg * (1.0 + g * (1.0 - sg))).astype(BF16)
            dpre_ref[1, :, cs] = (dact * g * sg).astype(BF16)

    (dpre,), got = _call(
        body, name=name, grid=(T // tm,),
        in_specs=[pl.BlockSpec((tm, D), lambda i: (i, 0)),
                  pl.BlockSpec((H, D), lambda i: (0, 0), pipeline_mode=pl.Buffered(1)),
                  pl.BlockSpec((2, tm, H), lambda i: (0, i, 0))],
        out_specs=[pl.BlockSpec((2, tm, H), lambda i: (0, i, 0))],
        out_shape=[jax.ShapeDtypeStruct((2, T, H), BF16)],
        semantics=("parallel",), args=(dx, w_down, pre), comm=comm)
    return dpre, got


def _ln_silu_bwd(dsw, y, ln_g, ln_b):
    mu = jnp.mean(y, axis=-1, keepdims=True)
    yc = y - mu
    rstd = lax.rsqrt(jnp.mean(yc * yc, axis=-1, keepdims=True) + EPS)
    xhat = yc * rstd
    z = xhat * ln_g + ln_b
    sg = _sigmoid(z)
    dz = dsw * sg * (1.0 + z * (1.0 - sg))
    dxh = dz * ln_g
    dy = rstd * (dxh - jnp.mean(dxh, axis=-1, keepdims=True) - xhat * jnp.mean(dxh * xhat, axis=-1, keepdims=True))
    return dy, dz * xhat, dz


def mm_bt_rmsbwd(dpre, w, x, gain, dres, name, comm=None, proj_w=None, conv_tail=None):
    nh, T, H = dpre.shape
    tm = _tile(T, 512)
    n_extra_in = 1 if proj_w is not None else (4 if conv_tail is not None else 0)

    def body(*refs):
        dp_ref, w_ref, x_ref, g_ref, dres_ref = refs[:5]
        extra_in = refs[5:5 + n_extra_in]
        dx_ref, dg_ref = refs[5 + n_extra_in:7 + n_extra_in]
        extra_out = refs[7 + n_extra_in:]

        @pl.when(pl.program_id(0) == 0)
        def _():
            dg_ref[...] = jnp.zeros_like(dg_ref)
            for r in extra_out[1:]:
                r[...] = jnp.zeros_like(r)

        dh = _dot_tb(dp_ref[0], w_ref[:, 0:H])
        for hf in range(1, nh):
            dh = dh + _dot_tb(dp_ref[hf], w_ref[:, hf * H:(hf + 1) * H])
        dx, dgr = _rms_bwd(dh, x_ref[...], g_ref[...], dres_ref[...])
        dx_ref[...] = dx
        dg_ref[...] += jnp.sum(dgr, axis=0, keepdims=True)
        if proj_w is not None:
            extra_out[0][...] = _dot_tb(dx.astype(BF16), extra_in[0][...]).astype(BF16)
        elif conv_tail is not None:
            wt_ref, y_ref, lg_ref, lb_ref = extra_in
            dy, dgl, dbl = _ln_silu_bwd(_dot_tb(dx.astype(BF16), wt_ref[...]), y_ref[...], lg_ref[...], lb_ref[...])
            extra_out[0][...] = dy
            extra_out[1][...] += jnp.sum(dgl, axis=0, keepdims=True)
            extra_out[2][...] += jnp.sum(dbl, axis=0, keepdims=True)
            extra_out[3][...] += jnp.sum(dx, axis=0, keepdims=True)

    row = lambda i: (i, 0)
    full = lambda i: (0, 0)
    vec = pl.BlockSpec((1, D), full)
    vec_shape = jax.ShapeDtypeStruct((1, D), F32)
    in_specs = [pl.BlockSpec((nh, tm, H), lambda i: (0, i, 0)),
                pl.BlockSpec((D, nh * H), full, pipeline_mode=pl.Buffered(1)),
                pl.BlockSpec((tm, D), row), vec, pl.BlockSpec((tm, D), row)]
    out_specs = [pl.BlockSpec((tm, D), row), vec]
    out_shape = [jax.ShapeDtypeStruct((T, D), F32), vec_shape]
    args = (dpre, w, x, gain, dres)
    if proj_w is not None:
        N = proj_w.shape[0]
        in_specs.append(pl.BlockSpec((N, D), full, pipeline_mode=pl.Buffered(1)))
        out_specs.append(pl.BlockSpec((tm, N), row))
        out_shape.append(jax.ShapeDtypeStruct((T, N), BF16))
        args += (proj_w,)
    elif conv_tail is not None:
        in_specs += [pl.BlockSpec((D, D), full, pipeline_mode=pl.Buffered(1)), pl.BlockSpec((tm, D), row), vec, vec]
        out_specs += [pl.BlockSpec((tm, D), row), vec, vec, vec]
        out_shape += [jax.ShapeDtypeStruct((T, D), F32), vec_shape, vec_shape, vec_shape]
        args += tuple(conv_tail)
    outs, got = _call(body, name=name, grid=(T // tm,), in_specs=in_specs, out_specs=out_specs, out_shape=out_shape,
                      semantics=("arbitrary",), args=args, comm=comm)
    return (*outs, got)


def dw_col(a, dpre, name):
    T = a.shape[0]
    nh, _, H = dpre.shape
    per = nh * H // N_CHIPS
    bph = N_CHIPS // nh
    tt = _tile(T, 2048)
    nt = T // tt

    def body(a_ref, b_ref, o_ref, acc):
        t = pl.program_id(1)

        @pl.when(t == 0)
        def _():
            acc[...] = jnp.zeros_like(acc)

        acc[...] += _dot_ta(a_ref[...], b_ref[...])

        @pl.when(t == nt - 1)
        def _():
            o_ref[...] = acc[...].astype(BF16)

    return pl.pallas_call(
        body, name=name, grid=(N_CHIPS, nt),
        in_specs=[pl.BlockSpec((tt, D), lambda q, t: (t, 0)),
                  pl.BlockSpec((None, tt, per), lambda q, t: (q // bph, t, q % bph))],
        out_specs=pl.BlockSpec((None, D, per), lambda q, t: (q, 0, 0)),
        out_shape=jax.ShapeDtypeStruct((N_CHIPS, D, per), BF16),
        scratch_shapes=[pltpu.VMEM((D, per), F32)],
        compiler_params=_params("parallel", "arbitrary"),
    )(a, dpre)


def dw_row(a, b, name):
    T, R = a.shape
    cw = 1408 if R % 1408 == 0 else R
    tt = _tile(T, 1024)
    nt = T // tt

    def body(a_ref, b_ref, o_ref, acc):
        t = pl.program_id(1)

        @pl.when(t == 0)
        def _():
            acc[...] = jnp.zeros_like(acc)

        acc[...] += _dot_ta(a_ref[...], b_ref[...].astype(BF16))

        @pl.when(t == nt - 1)
        def _():
            o_ref[...] = acc[...].astype(BF16)

    out = pl.pallas_call(
        body, name=name, grid=(R // cw, nt),
        in_specs=[pl.BlockSpec((tt, cw), lambda q, t: (t, q)), pl.BlockSpec((tt, D), lambda q, t: (t, 0))],
        out_specs=pl.BlockSpec((cw, D), lambda q, t: (q, 0)),
        out_shape=jax.ShapeDtypeStruct((R, D), BF16),
        scratch_shapes=[pltpu.VMEM((cw, D), F32)],
        compiler_params=_params("parallel", "arbitrary"),
    )(a, b)
    return out.reshape(N_CHIPS, R // N_CHIPS, D)


def conv_bwd(ddwc, glu, pre, w_dw, comm=None):
    T = ddwc.shape[0]
    tt, L, nlt = _conv_tiles(T)
    n_i = T // tt
    main, prev, nxt = _conv_specs(T, tt)

    def body(d_ref, dp_ref, dn_ref, x_ref, xp_ref, xn_ref, pre_ref, w_ref,
             dpre_ref, dw_ref, dbd_ref, dbp_ref, padd, padx, ob, extd, extx, wk):
        i = pl.program_id(0)

        @pl.when(i == 0)
        def _():
            dw_ref[...] = jnp.zeros_like(dw_ref)
            dbd_ref[...] = jnp.zeros_like(dbd_ref)
            dbp_ref[...] = jnp.zeros_like(dbp_ref)

        _fill_pad(padd, d_ref, dp_ref, dn_ref, i, n_i, tt, nlt)
        _fill_pad(padx, x_ref, xp_ref, xn_ref, i, n_i, tt, nlt)
        for lt in range(nlt):
            sl = slice(lt * LANES, (lt + 1) * LANES)
            o = ob.at[lt]
            _fill_strided(extd, padd.at[lt], L)
            _fill_strided(extx, padx.at[lt], L)
            for k in range(CONV_W):
                wk[k] = jnp.broadcast_to(w_ref[k:k + 1, sl], (SUBLANES, LANES))

            nu = CONV_JB_BWD

            def jbody(jb, accs):
                j = jb * nu
                accs = list(accs)
                d = [extd[j + u + CONV_PAD] for u in range(nu)]
                g = [None] * nu
                for m in range(CONV_W + nu - 1):
                    ed = extd[j + 2 * CONV_PAD + nu - 1 - m]
                    ex = extx[j + m]
                    for u in range(nu):
                        k = m - (nu - 1 - u)
                        if 0 <= k < CONV_W:
                            t = ed * wk[k]
                            g[u] = t if g[u] is None else g[u] + t
                        k = m - u
                        if 0 <= k < CONV_W:
                            accs[k] = accs[k] + d[u] * ex
                for u in range(nu):
                    o[pl.ds(j + u, SUBLANES, stride=L), :] = g[u]
                return tuple(accs)

            accs = lax.fori_loop(0, L // nu, jbody, tuple(jnp.zeros((SUBLANES, LANES), F32) for _ in range(CONV_W)))
            for k in range(CONV_W):
                dw_ref[k:k + 1, sl] += jnp.sum(accs[k], axis=0, keepdims=True)
        dglu = jnp.concatenate([ob[lt] for lt in range(nlt)], axis=1)
        a = pre_ref[0].astype(F32)
        gate = pre_ref[1].astype(F32)
        sg = _sigmoid(gate)
        da = dglu * sg
        dgate = dglu * a * sg * (1.0 - sg)
        dpre_ref[0] = da.astype(BF16)
        dpre_ref[1] = dgate.astype(BF16)
        dbd_ref[...] += jnp.sum(d_ref[...], axis=0, keepdims=True)
        dbp_ref[0] += jnp.sum(da, axis=0, keepdims=True)
        dbp_ref[1] += jnp.sum(dgate, axis=0, keepdims=True)

    full = lambda i: (0, 0)
    (dpre, dw, dbd, dbp), got = _call(
        body, name="conv_bwd", grid=(n_i,),
        in_specs=[main, prev, nxt, main, prev, nxt, pl.BlockSpec((2, tt, D), lambda i: (0, i, 0)),
                  pl.BlockSpec((32, D), full)],
        out_specs=[pl.BlockSpec((2, tt, D), lambda i: (0, i, 0)), pl.BlockSpec((32, D), full),
                   pl.BlockSpec((1, D), full), pl.BlockSpec((2, 1, D), lambda i: (0, 0, 0))],
        out_shape=[jax.ShapeDtypeStruct((2, T, D), BF16), jax.ShapeDtypeStruct((32, D), F32),
                   jax.ShapeDtypeStruct((1, D), F32), jax.ShapeDtypeStruct((2, 1, D), F32)],
        scratch_shapes=[pltpu.VMEM((nlt, tt + 2 * HALO, LANES), F32), pltpu.VMEM((nlt, tt + 2 * HALO, LANES), F32),
                        pltpu.VMEM((nlt, tt, LANES), F32), pltpu.VMEM((L + 2 * HALO, SUBLANES, LANES), F32),
                        pltpu.VMEM((L + 2 * HALO, SUBLANES, LANES), F32), pltpu.VMEM((32, SUBLANES, LANES), F32)],
        semantics=("arbitrary",), args=(ddwc, ddwc, ddwc, glu, glu, glu, pre, w_dw), comm=comm)
    return dpre, dw, dbd, dbp, got


def attn_bwd(qkv, o, do, sink, rc, rs1, rs2, comm=None):
    T = qkv.shape[0]
    nb, q_spec, prev, own, nxt = _attn_specs(T)
    kvw = N_KV * HD

    def body(sink_ref, q_ref, kp_ref, ko_ref, kn_ref, o_ref, do_ref, c_ref, s1_ref, s2_ref,
             dq_ref, dkc_ref, dvc_ref, dsink_ref):
        n = pl.program_id(0)

        @pl.when(n == 0)
        def _():
            dsink_ref[...] = jnp.zeros_like(dsink_ref)

        valid = _attn_mask(n, T)
        kv = jnp.concatenate([kp_ref[...], ko_ref[...], kn_ref[...]], axis=0)
        kx, vx = _kv_padded(kv, 0), _kv_padded(kv, 2)
        tile = lambda ref, j: ref[:, j * LANES:(j + 1) * LANES]
        ss = _pair_products(kx, lambda j: tile(q_ref, j))
        dps = _pair_products(vx, lambda j: tile(do_ref, j))
        low_d = lax.broadcasted_iota(jnp.int32, (LANES, BLK), 0) < HD
        deltas = []
        for j in range(N_HEADS // 2):
            prod_t = tile(do_ref, j).astype(F32).T * tile(o_ref, j).astype(F32).T
            deltas.append(jnp.sum(jnp.where(low_d, prod_t, 0.0), axis=0, keepdims=True))
            deltas.append(jnp.sum(jnp.where(low_d, 0.0, prod_t), axis=0, keepdims=True))
        lane = lax.broadcasted_iota(jnp.int32, (1, N_HEADS), 1)
        dsink = jnp.zeros((1, N_HEADS), F32)
        pbs, dss = [], []
        for h in range(N_HEADS):
            p, p_sink = _softmax_sink(ss[h], valid, sink_ref[h])
            dss.append((p * (dps[h] - deltas[h])).astype(BF16))
            pbs.append(p.astype(BF16))
            part = -jnp.sum(p_sink * deltas[h], axis=1, keepdims=True)
            dsink = dsink + jnp.where(lane == h, part, 0.0)
        dsink_ref[...] += dsink
        c, s1, s2 = c_ref[...], s1_ref[...], s2_ref[...]
        kxt = {k: v.T for k, v in kx.items()}
        for j in range(N_HEADS // 2):
            g = 2 * j // GROUP
            dq_t = _dot(kxt[g, 0], dss[2 * j]) + _dot(kxt[g, 1], dss[2 * j + 1])
            dq_ref[:, j * LANES:(j + 1) * LANES] = (_rope(dq_t.T, c, -s1, -s2) * Q_SCALE).astype(BF16)
        low_k = lax.broadcasted_iota(jnp.int32, (3 * BLK, LANES), 1) < HD
        cols = lambda xs, g, p: jnp.concatenate([xs[GROUP * g + p], xs[GROUP * g + 2 + p]], axis=1)
        for t in range(N_KV // 2):
            sums = {}
            for g in (2 * t, 2 * t + 1):
                q2 = jnp.concatenate([tile(q_ref, 2 * g), tile(q_ref, 2 * g + 1)], axis=0)
                do2 = jnp.concatenate([tile(do_ref, 2 * g), tile(do_ref, 2 * g + 1)], axis=0)
                dk2 = _dot(jnp.concatenate([cols(dss, g, 0), cols(dss, g, 1)], axis=0), q2)
                dv2 = _dot(jnp.concatenate([cols(pbs, g, 0), cols(pbs, g, 1)], axis=0), do2)
                for p in range(2):
                    sums[g, p] = (dk2[p * 3 * BLK:(p + 1) * 3 * BLK], dv2[p * 3 * BLK:(p + 1) * 3 * BLK])
            for which, ref in ((0, dkc_ref), (1, dvc_ref)):
                keep = jnp.where(low_k, sums[2 * t, 0][which], sums[2 * t + 1, 1][which])
                swap = jnp.where(low_k, sums[2 * t + 1, 0][which], sums[2 * t, 1][which])
                ref[:, t * LANES:(t + 1) * LANES] = keep + pltpu.roll(swap, HD, 1)

    row = lambda n: (n, 0)
    (dq, dkc, dvc, dsink), got = _call(
        body, name="attn_bwd", grid=(nb,),
        in_specs=[pl.BlockSpec(memory_space=pltpu.SMEM), q_spec, prev, own, nxt,
                  pl.BlockSpec((BLK, D), row), pl.BlockSpec((BLK, D), row), *_tab_specs(BLK)],
        out_specs=[pl.BlockSpec((BLK, D), row), pl.BlockSpec((None, 3 * BLK, kvw), lambda n: (n, 0, 0)),
                   pl.BlockSpec((None, 3 * BLK, kvw), lambda n: (n, 0, 0)), pl.BlockSpec((1, N_HEADS), lambda n: (0, 0))],
        out_shape=[jax.ShapeDtypeStruct((T, QKV), BF16), jax.ShapeDtypeStruct((nb, 3 * BLK, kvw), F32),
                   jax.ShapeDtypeStruct((nb, 3 * BLK, kvw), F32), jax.ShapeDtypeStruct((1, N_HEADS), F32)],
        semantics=("arbitrary",), args=(sink, qkv, qkv, qkv, qkv, o, do, rc, rs1, rs2), comm=comm)
    return dq, dkc, dvc, dsink, got


def kv_sum(dqkv, dkc, dvc, rc, rs1, rs2):
    nb = dkc.shape[0]
    T = nb * BLK
    kvw = N_KV * HD

    G = 4
    ng = nb // G

    def gather3(own_ref, prev_ref, before_ref, next_ref, after_ref, m):
        has_before = (m > 0).astype(F32)
        has_after = (m < ng - 1).astype(F32)
        out = []
        for i in range(G):
            from_prev = prev_ref[i - 1] if i > 0 else before_ref[0] * has_before
            from_next = next_ref[i + 1] if i < G - 1 else after_ref[0] * has_after
            out.append(from_prev + own_ref[i] + from_next)
        return jnp.concatenate(out, axis=0)

    def body(_, ko, kp, kb, kn, ka, vo, vp, vb, vn, va, c_ref, s1_ref, s2_ref, out_ref):
        m = pl.program_id(0)
        dk = gather3(ko, kp, kb, kn, ka, m)
        dv = gather3(vo, vp, vb, vn, va, m)
        c, s1, s2 = c_ref[...], s1_ref[...], s2_ref[...]
        for j in range(kvw // LANES):
            sl = slice(LANES * j, LANES * (j + 1))
            out_ref[:, sl] = _rope(dk[:, sl], c, -s1, -s2).astype(BF16)
        out_ref[:, kvw:] = dv.astype(BF16)

    own = pl.BlockSpec((G, BLK, kvw), lambda m: (m, 1, 0))
    prev = pl.BlockSpec((G, BLK, kvw), lambda m: (m, 2, 0))
    before = pl.BlockSpec((1, BLK, kvw), lambda m: (jnp.maximum(G * m - 1, 0), 2, 0))
    nxt = pl.BlockSpec((G, BLK, kvw), lambda m: (m, 0, 0))
    after = pl.BlockSpec((1, BLK, kvw), lambda m: (jnp.minimum(G * m + G, nb - 1), 0, 0))
    five = [own, prev, before, nxt, after]
    return pl.pallas_call(
        body, name="kv_sum", grid=(ng,),
        in_specs=[pl.BlockSpec(memory_space=pl.ANY), *five, *five, *_tab_specs(G * BLK)],
        out_specs=pl.BlockSpec((G * BLK, 2 * kvw), lambda m: (m, KV_OFF // (2 * kvw))),
        out_shape=jax.ShapeDtypeStruct((T, QKV), BF16),
        input_output_aliases={0: 0},
        compiler_params=_params("parallel"),
    )(dqkv, *([dkc] * 5), *([dvc] * 5), rc, rs1, rs2)


def _me():
    return lax.axis_index("x"), lax.axis_index("y"), lax.axis_index("c")


def _half_rows(ref, sharded_rows, chip, core):
    R, C = ref.shape[-2], ref.shape[-1]
    lead = (slice(None),) * (len(ref.shape) - 2)
    if sharded_rows:
        per = R // N_CHIPS
        return ref.at[lead + (pl.ds(chip * per + core * (per // 2), per // 2), slice(None))]
    per = C // N_CHIPS
    return ref.at[lead + (pl.ds(core * (R // 2), R // 2), pl.ds(chip * per, per))]


class _Gather:
    def __init__(self, shards, sharded_rows):
        self.inputs = list(shards)
        self.rows = list(sharded_rows)
        self.n = self.n_in = self.n_out = len(shards)
        self.out_shapes = []
        for s, rows in zip(shards, sharded_rows):
            shp = list(s.shape)
            shp[-2 if rows else -1] *= N_CHIPS
            self.out_shapes.append(jax.ShapeDtypeStruct(tuple(shp), s.dtype))
        self.scratch = [pltpu.SemaphoreType.DMA((self.n, 6)), pltpu.SemaphoreType.DMA((self.n, 6)),
                        pltpu.SemaphoreType.DMA((self.n, 2))]

    def _ctx(self, ins, outs, sems):
        send_sems, recv_sems, local_sems = sems
        x, y, c = _me()
        chips = [(1 - x, y), (x, 1 - y), (1 - x, 1 - y)]

        def half_src(w, core):
            s = ins[w]
            R = s.shape[-2]
            return s.at[pl.ds(core * (R // 2), R // 2), :]

        def dst(w, chip, core):
            return _half_rows(outs[w], self.rows[w], chip, core)

        def copy(w, k, src, chip, core, to):
            return pltpu.make_async_remote_copy(
                src_ref=src, dst_ref=dst(w, chip, core), send_sem=send_sems.at[w, k], recv_sem=recv_sems.at[w, k],
                device_id=to, device_id_type=MESH)

        def local(w, core):
            return pltpu.make_async_copy(half_src(w, core), dst(w, 2 * x + y, core), local_sems.at[w, core])

        def first(w, j):
            qx, qy = chips[j]
            return copy(w, j, half_src(w, c), 2 * x + y, c, (qx, qy, c))

        def landed(w, j):
            qx, qy = chips[j]
            return copy(w, j, dst(w, 2 * qx + qy, c), 2 * qx + qy, c, (x, y, c))

        def passed(w, j):
            qx, qy = chips[j]
            return copy(w, 3 + j, dst(w, 2 * qx + qy, c), 2 * qx + qy, c, (x, y, 1 - c))

        def from_sibling(w, j):
            qx, qy = chips[j]
            return copy(w, 3 + j, dst(w, 2 * qx + qy, 1 - c), 2 * qx + qy, 1 - c, (x, y, c))

        return local, first, landed, passed, from_sibling

    def start(self, ins, outs, sems):
        local, first, _, _, _ = self._ctx(ins, outs, sems)
        for w in range(self.n):
            for core in range(2):
                local(w, core).start()
            for j in range(3):
                first(w, j).start()

    def mid(self, ins, outs, sems):
        _, _, landed, passed, _ = self._ctx(ins, outs, sems)
        for w in range(self.n):
            for j in range(3):
                landed(w, j).wait_recv()
                passed(w, j).start()

    def end(self, ins, outs, sems):
        local, first, _, passed, from_sibling = self._ctx(ins, outs, sems)
        for w in range(self.n):
            for j in range(3):
                from_sibling(w, j).wait_recv()
        for w in range(self.n):
            for j in range(3):
                first(w, j).wait_send()
                passed(w, j).wait_send()
            for core in range(2):
                local(w, core).wait()


class _Scatter:
    def __init__(self, grads, small=None):
        self.inputs = list(grads) + ([small] if small is not None else [])
        self.ng = len(grads)
        self.n = self.n_in = self.n_out = len(self.inputs)
        self.out_shapes = [jax.ShapeDtypeStruct((N_DEV, g.shape[1] // 2, g.shape[2]), g.dtype) for g in grads]
        if small is not None:
            self.out_shapes.append(jax.ShapeDtypeStruct((N_DEV,) + small.shape, small.dtype))
        self.scratch = [pltpu.SemaphoreType.DMA((self.n, N_DEV)), pltpu.SemaphoreType.DMA((self.n, N_DEV)),
                        pltpu.SemaphoreType.DMA((self.n,))]

    def _ctx(self, ins, outs, sems):
        send_sems, recv_sems, local_sems = sems
        x, y, c = _me()
        me = 4 * x + 2 * y + c

        def piece(w, chip, core):
            if w >= self.ng:
                return ins[w]
            half = ins[w].shape[1] // 2
            return ins[w].at[chip, pl.ds(core * half, half), :]

        def peer_of(k):
            return x ^ ((k >> 2) & 1), y ^ ((k >> 1) & 1), c ^ (k & 1)

        def local(w):
            return pltpu.make_async_copy(piece(w, 2 * x + y, c), outs[w].at[me], local_sems.at[w])

        def send(w, k):
            px, py, pc = peer_of(k)
            return pltpu.make_async_remote_copy(
                src_ref=piece(w, 2 * px + py, pc), dst_ref=outs[w].at[me], send_sem=send_sems.at[w, k],
                recv_sem=recv_sems.at[w, k], device_id=(px, py, pc), device_id_type=MESH)

        def recv(w, k):
            px, py, pc = peer_of(k)
            return pltpu.make_async_remote_copy(
                src_ref=piece(w, 2 * x + y, c), dst_ref=outs[w].at[4 * px + 2 * py + pc], send_sem=send_sems.at[w, k],
                recv_sem=recv_sems.at[w, k], device_id=(px, py, pc), device_id_type=MESH)

        return local, send, recv

    def start(self, ins, outs, sems):
        local, send, _ = self._ctx(ins, outs, sems)
        for w in range(self.n):
            local(w).start()
            for k in range(1, N_DEV):
                send(w, k).start()

    def mid(self, ins, outs, sems):
        pass

    def end(self, ins, outs, sems):
        local, send, recv = self._ctx(ins, outs, sems)
        for w in range(self.n):
            for k in range(1, N_DEV):
                recv(w, k).wait_recv()
        for w in range(self.n):
            for k in range(1, N_DEV):
                send(w, k).wait_send()
            local(w).wait()


class _Both:
    def __init__(self, a, b):
        self.a, self.b = a, b
        self.inputs = a.inputs + b.inputs
        self.out_shapes = a.out_shapes + b.out_shapes
        self.scratch = a.scratch + b.scratch
        self.n_in, self.n_out = a.n_in + b.n_in, a.n_out + b.n_out

    def _split(self, ins, outs, sems):
        a, na = self.a, len(self.a.scratch)
        return (ins[:a.n_in], outs[:a.n_out], sems[:na]), (ins[a.n_in:], outs[a.n_out:], sems[na:])

    def start(self, ins, outs, sems):
        pa, pb = self._split(ins, outs, sems)
        self.a.start(*pa)
        self.b.start(*pb)

    def mid(self, ins, outs, sems):
        pa, pb = self._split(ins, outs, sems)
        self.a.mid(*pa)
        self.b.mid(*pb)

    def end(self, ins, outs, sems):
        pa, pb = self._split(ins, outs, sems)
        self.a.end(*pa)
        self.b.end(*pb)


def _call(body, *, name, grid, in_specs, out_specs, out_shape, scratch_shapes=(), semantics, args, comm=None):
    if comm is None:
        outs = pl.pallas_call(
            body, name=name, grid=grid, in_specs=in_specs, out_specs=out_specs, out_shape=out_shape,
            scratch_shapes=list(scratch_shapes), compiler_params=_params(*semantics))(*args)
        return outs, []
    n_in, n_out, n_scr = len(in_specs), len(out_specs), len(scratch_shapes)

    total = math.prod(grid)
    first, middle, last = 0, (3 * total) // 4 - 1, total - 1
    assert first <= middle < last

    def at(step):
        lin = pl.program_id(0)
        for d in range(1, len(grid)):
            lin = lin * grid[d] + pl.program_id(d)
        return lin == step

    def hosted(*refs):
        h_in, c_in = refs[:n_in], refs[n_in:n_in + comm.n_in]
        rest = refs[n_in + comm.n_in:]
        h_out, c_out = rest[:n_out], rest[n_out:n_out + comm.n_out]
        rest = rest[n_out + comm.n_out:]
        h_scr, c_scr = rest[:n_scr], rest[n_scr:]

        @pl.when(at(first))
        def _():
            comm.start(c_in, c_out, c_scr)

        body(*h_in, *h_out, *h_scr)

        @pl.when(at(middle))
        def _():
            comm.mid(c_in, c_out, c_scr)

        @pl.when(at(last))
        def _():
            comm.end(c_in, c_out, c_scr)

    any_spec = pl.BlockSpec(memory_space=pl.ANY)
    outs = pl.pallas_call(
        hosted, name=name, grid=grid, in_specs=list(in_specs) + [any_spec] * comm.n_in,
        out_specs=list(out_specs) + [any_spec] * comm.n_out, out_shape=list(out_shape) + comm.out_shapes,
        scratch_shapes=list(scratch_shapes) + comm.scratch,
        compiler_params=_params(*(["arbitrary"] * len(grid))))(*args, *comm.inputs)
    return outs[:n_out], outs[n_out:]


def sum_swap(pieces, name, comm=None):
    nl = len(pieces)
    _, r2, cc = pieces[0].shape
    tr = 128 if r2 % 128 == 0 else r2 // 2
    n = r2 // tr

    def body(*refs):
        p_refs, out = refs[:nl], refs[nl]
        slots, send_sems, local_sems, recv_sem = refs[nl + 1:]
        x, y, c = _me()
        sibling = (x, y, 1 - c)
        l, i = pl.program_id(0), pl.program_id(1)
        step = l * n + i

        def rows(st, core):
            return out.at[st // n, pl.ds(core * r2 + (st % n) * tr, tr), :]

        def copies(st):
            slot = st % 2
            local = pltpu.make_async_copy(slots.at[slot], rows(st, c), local_sems.at[slot])
            remote = pltpu.make_async_remote_copy(
                src_ref=slots.at[slot], dst_ref=rows(st, c), send_sem=send_sems.at[slot], recv_sem=recv_sem,
                device_id=sibling, device_id_type=MESH)
            return local, remote

        for ll in range(nl):
            @pl.when(l == ll)
            def _():
                acc = p_refs[ll][0].astype(F32)
                for d in range(1, N_DEV):
                    acc = acc + p_refs[ll][d].astype(F32)
                slots[step % 2] = acc

        for cp in copies(step):
            cp.start()

        @pl.when(step >= 1)
        def _():
            local, remote = copies(step - 1)
            local.wait()
            remote.wait_send()

        @pl.when(step == nl * n - 1)
        def _():
            local, remote = copies(step)
            local.wait()
            remote.wait_send()
            theirs = out.at[:, pl.ds((1 - c) * r2, r2), :]
            pltpu.make_async_remote_copy(src_ref=theirs, dst_ref=theirs, send_sem=send_sems.at[0],
                                         recv_sem=recv_sem, device_id=sibling, device_id_type=MESH).wait_recv()

    def piece_spec(ll):
        def index(l, i):
            return (0, jnp.where(l == ll, i, jnp.where(l < ll, 0, n - 1)), 0)
        return pl.BlockSpec((N_DEV, tr, cc), index)

    (out,), got = _call(
        body, name=name, grid=(nl, n),
        in_specs=[piece_spec(ll) for ll in range(nl)],
        out_specs=[pl.BlockSpec(memory_space=pl.ANY)],
        out_shape=[jax.ShapeDtypeStruct((nl, 2 * r2, cc), F32)],
        scratch_shapes=[pltpu.VMEM((2, tr, cc), F32), pltpu.SemaphoreType.DMA((2,)), pltpu.SemaphoreType.DMA((2,)),
                        pltpu.SemaphoreType.DMA(())],
        semantics=("arbitrary", "arbitrary"), args=tuple(pieces), comm=comm)
    return (out, got) if comm is not None else out


def sum_pieces(pieces, name):
    _, R, C = pieces.shape
    tr = _tile(R, 128) if R % 128 == 0 else R

    def body(p_ref, o_ref):
        acc = p_ref[0].astype(F32)
        for d in range(1, N_DEV):
            acc = acc + p_ref[d].astype(F32)
        o_ref[...] = acc

    return pl.pallas_call(
        body, name=name, grid=(R // tr,),
        in_specs=[pl.BlockSpec((N_DEV, tr, C), lambda i: (0, i, 0))],
        out_specs=pl.BlockSpec((tr, C), lambda i: (i, 0)),
        out_shape=jax.ShapeDtypeStruct((R, C), F32),
        compiler_params=_params("parallel"),
    )(pieces)


def adamw(w, g, m, v, name):
    Lyr, R, C = w.shape
    tr = _tile(R, 256) if R % 8 == 0 else R
    c1 = 1.0 / (1.0 - ADAM_B1 ** ADAM_STEP)
    c2 = 1.0 / (1.0 - ADAM_B2 ** ADAM_STEP)

    def body(w_ref, g_ref, m_ref, v_ref, d_ref, nm_ref, nv_ref):
        gv = g_ref[...]
        nm = ADAM_B1 * m_ref[...] + (1.0 - ADAM_B1) * gv
        nv = ADAM_B2 * v_ref[...] + (1.0 - ADAM_B2) * (gv * gv)
        nm_ref[...] = nm
        nv_ref[...] = nv
        d_ref[...] = -ADAM_LR * ((nm * c1) / (jnp.sqrt(nv * c2) + ADAM_EPS) + ADAM_WD * w_ref[...])

    spec = pl.BlockSpec((None, tr, C), lambda l, i: (l, i, 0))
    shp = jax.ShapeDtypeStruct(w.shape, F32)
    return pl.pallas_call(
        body, name=name, grid=(Lyr, R // tr),
        in_specs=[spec] * 4, out_specs=[spec] * 3, out_shape=[shp] * 3,
        compiler_params=_params("parallel", "parallel"),
    )(w, g, m, v)


def _rope_tables(T):
    pos = jnp.arange(T, dtype=F32)
    inv_freq = THETA ** (-jnp.arange(0, ROT, 2, dtype=F32) / ROT)
    ang = pos[:, None] * inv_freq[None, :]
    cs = jnp.concatenate([jnp.cos(ang), jnp.sin(ang)], axis=1)
    half = ROT // 2
    lane = jnp.arange(3 * LANES)
    table, lm = lane // LANES, lane % HD
    src = jnp.where(table == 0, lm % half, half + lm % half)
    i32 = lambda b: b.astype(jnp.int32)
    sign = jnp.where(table == 0, i32(lm < ROT), jnp.where(table == 1, -i32(lm < half), i32((lm >= half) & (lm < ROT))))
    place = (jnp.arange(ROT)[:, None] == src[None, :]) * sign[None, :].astype(F32)
    ones = ((table == 0) & (lm >= ROT)).astype(F32)
    return jnp.dot(cs, place, precision=lax.Precision.HIGHEST) + ones[None, :]


def _tab_specs(rows):
    return [pl.BlockSpec((rows, LANES), lambda i, k=k: (i, k)) for k in range(3)]


def kernel(x, attn_norm, attn_w_qkv, attn_w_o, attn_sink, conv_norm, conv_w_pw1, conv_b_pw1, conv_w_dw, conv_b_dw, conv_ln_g, conv_ln_b, conv_w_pw2, conv_b_pw2, ffn_norm, ffn_w_gu, ffn_w_down, final_norm, loss_target, m_attn_norm, m_attn_w_qkv, m_attn_w_o, m_attn_sink, m_conv_norm, m_conv_w_pw1, m_conv_b_pw1, m_conv_w_dw, m_conv_b_dw, m_conv_ln_g, m_conv_ln_b, m_conv_w_pw2, m_conv_b_pw2, m_ffn_norm, m_ffn_w_gu, m_ffn_w_down, m_final_norm, v_attn_norm, v_attn_w_qkv, v_attn_w_o, v_attn_sink, v_conv_norm, v_conv_w_pw1, v_conv_b_pw1, v_conv_w_dw, v_conv_b_dw, v_conv_ln_g, v_conv_ln_b, v_conv_w_pw2, v_conv_b_pw2, v_ffn_norm, v_ffn_w_gu, v_ffn_w_down, v_final_norm):
    T = x.shape[1]
    x0 = x[0]
    target = loss_target[0]
    ix, iy = lax.axis_index("x"), lax.axis_index("y")
    chip = 2 * ix + iy
    rc = rs1 = rs2 = _rope_tables(T)

    bf = lambda t: t.astype(BF16)
    col_row = [False, True]

    def place(vec, width):
        return lax.dynamic_update_slice(jnp.zeros((vec.shape[0], N_CHIPS * width), F32), vec, (0, chip * width))

    small_rows = jnp.concatenate([
        place(conv_norm, 256), place(conv_b_pw1, 512).reshape(2, D), place(conv_b_dw, 256), place(conv_ln_g, 256),
        place(conv_ln_b, 256), place(conv_b_pw2, 256), jnp.zeros((1, D), F32),
        place(conv_w_dw[0], 256), jnp.zeros((1, D), F32)], axis=0)

    h0, (w_qkv,) = rms_first(x0, attn_norm, comm=_Gather([bf(attn_w_qkv[0])], [False]))
    qkv, (w_o, got) = qkv_proj(h0, w_qkv, rc, rs1, rs2,
                               comm=_Both(_Gather([bf(attn_w_o[0])], [True]), _Scatter([], small_rows)))
    psmall = sum_pieces(got, "sum_small_params") * 0.5
    p_conv_norm, p_b_pw1 = psmall[0:1], psmall[1:3].reshape(1, 2 * D)
    p_b_dw, p_ln_g, p_ln_b, p_b_pw2 = psmall[3:4], psmall[4:5], psmall[5:6], psmall[6:7]
    p_w_dw = psmall[8:40]
    sink = attn_sink[0]
    o, (w_gu0,) = attn_fwd(qkv, sink, comm=_Gather([bf(ffn_w_gu[0])], [False]))
    zero_b = jnp.zeros((1, D), F32)
    zero_gu = jnp.zeros((1, 2 * DFF), F32)
    x1, h1, gu0, act0, (w_down0, w_pw1, w_pw2) = rms_mm_gate(
        (o, w_o, zero_b, x0), ffn_norm[0:1], w_gu0, zero_gu, DFF, True, BF16, "ffn0_up",
        comm=_Gather([bf(ffn_w_down[0]), bf(conv_w_pw1[0]), bf(conv_w_pw2[0])], [True, False, True]))
    x2, h2, pre, glu, _ = rms_mm_gate((act0, w_down0, zero_b, x1), p_conv_norm, w_pw1, p_b_pw1, D, False, F32,
                                      "conv_pw1")
    dwc, sw, (w_gu1, w_down1) = conv_fwd(glu, p_w_dw, p_b_dw, p_ln_g, p_ln_b,
                                         comm=_Gather([bf(ffn_w_gu[1]), bf(ffn_w_down[1])], col_row))
    x3, h3, gu1, act1, _ = rms_mm_gate((sw, w_pw2, p_b_pw2, x2), ffn_norm[1:2], w_gu1, zero_gu, DFF, True, BF16,
                                       "ffn1_up")
    dx4, loss_part, d_final = mm_res_loss(act1, w_down1, x3, final_norm.reshape(1, D), target)

    dgu1, _ = swiglu_bwd(dx4, w_down1, gu1, "ffn1_down_bwd")
    g_down1 = dw_row(act1, dx4, "ffn1_down_dw")
    dx3, d_ffn1, ddwc, d_ln_g, d_ln_b, d_b_pw2, _ = mm_bt_rmsbwd(
        dgu1, w_gu1, x3, ffn_norm[1:2], dx4, "ffn1_up_bwd", conv_tail=(w_pw2, dwc, p_ln_g, p_ln_b))
    g_gu1 = dw_col(h3, dgu1, "ffn1_up_dw")

    g_pw2 = dw_row(sw, dx3, "conv_pw2_dw")
    dpre, d_w_dw, d_b_dw, d_b_pw1, (r_gu1, r_down1) = conv_bwd(ddwc, glu, pre, p_w_dw,
                                                               comm=_Scatter([g_gu1, g_down1]))
    dx2, d_conv_norm, _ = mm_bt_rmsbwd(dpre, w_pw1, x2, p_conv_norm, dx3, "conv_pw1_bwd")
    g_pw1 = dw_col(h2, dpre, "conv_pw1_dw")

    dgu0, (r_pw1, r_pw2) = swiglu_bwd(dx2, w_down0, gu0, "ffn0_down_bwd", comm=_Scatter([g_pw1, g_pw2]))
    g_down0 = dw_row(act0, dx2, "ffn0_down_dw")
    dx1, d_ffn0, do, _ = mm_bt_rmsbwd(dgu0, w_gu0, x1, ffn_norm[0:1], dx2, "ffn0_up_bwd", proj_w=w_o)
    g_gu0 = dw_col(h1, dgu0, "ffn0_up_dw")

    g_o = dw_row(o, dx1, "attn_out_dw")
    dq, dkc, dvc, d_sink, (r_gu0, r_down0, r_o) = attn_bwd(qkv, o, do, sink, rc, rs1, rs2,
                                                           comm=_Scatter([g_gu0, g_down0, g_o]))
    dqkv = kv_sum(dq, dkc, dvc, rc, rs1, rs2)[None]
    g_qkv = dw_col(h0, dqkv, "attn_qkv_dw")
    dx0, d_attn_norm, _ = mm_bt_rmsbwd(dqkv, w_qkv, x0, attn_norm, dx1, "attn_qkv_bwd")

    pad16 = lambda t: jnp.concatenate([t, jnp.zeros((1, D - t.shape[1]), F32)], axis=1)
    small_g = jnp.concatenate([
        d_attn_norm, pad16(d_sink), d_conv_norm, d_b_pw1.reshape(2, D), d_b_dw, d_ln_g, d_ln_b, d_b_pw2,
        d_ffn0, d_ffn1, d_final, pad16(loss_part), jnp.zeros((3, D), F32), d_w_dw], axis=0)
    gf_gu, (r_qkv, r_small) = sum_swap([r_gu0, r_gu1], "sum_gu", comm=_Scatter([g_qkv], small_g))
    gf_down = sum_swap([r_down0, r_down1], "sum_down")
    gf_pw1, gf_pw2 = sum_swap([r_pw1], "sum_pw1"), sum_swap([r_pw2], "sum_pw2")
    gf_qkv, gf_o = sum_swap([r_qkv], "sum_qkv"), sum_swap([r_o], "sum_o")
    gs = sum_pieces(r_small, "sum_small_grads")
    loss = gs[12, 0]

    def take(row0, nrows, width):
        return lax.dynamic_slice(gs, (row0, chip * width), (nrows, width))

    grads = {
        "attn_norm": gs[0:1], "attn_w_qkv": gf_qkv, "attn_w_o": gf_o, "attn_sink": gs[1:2, :N_HEADS],
        "conv_norm": take(2, 1, 256), "conv_w_pw1": gf_pw1,
        "conv_b_pw1": lax.dynamic_slice(gs[3:5].reshape(1, 2 * D), (0, chip * 512), (1, 512)),
        "conv_w_dw": take(16, 32, 256)[None, :CONV_W], "conv_b_dw": take(5, 1, 256), "conv_ln_g": take(6, 1, 256),
        "conv_ln_b": take(7, 1, 256), "conv_w_pw2": gf_pw2, "conv_b_pw2": take(8, 1, 256),
        "ffn_norm": gs[9:11], "ffn_w_gu": gf_gu, "ffn_w_down": gf_down, "final_norm": gs[11],
    }
    weights = dict(attn_norm=attn_norm, attn_w_qkv=attn_w_qkv, attn_w_o=attn_w_o, attn_sink=attn_sink,
                   conv_norm=conv_norm, conv_w_pw1=conv_w_pw1, conv_b_pw1=conv_b_pw1, conv_w_dw=conv_w_dw,
                   conv_b_dw=conv_b_dw, conv_ln_g=conv_ln_g, conv_ln_b=conv_ln_b, conv_w_pw2=conv_w_pw2,
                   conv_b_pw2=conv_b_pw2, ffn_norm=ffn_norm, ffn_w_gu=ffn_w_gu, ffn_w_down=ffn_w_down,
                   final_norm=final_norm)
    m_in = dict(attn_norm=m_attn_norm, attn_w_qkv=m_attn_w_qkv, attn_w_o=m_attn_w_o, attn_sink=m_attn_sink,
                conv_norm=m_conv_norm, conv_w_pw1=m_conv_w_pw1, conv_b_pw1=m_conv_b_pw1, conv_w_dw=m_conv_w_dw,
                conv_b_dw=m_conv_b_dw, conv_ln_g=m_conv_ln_g, conv_ln_b=m_conv_ln_b, conv_w_pw2=m_conv_w_pw2,
                conv_b_pw2=m_conv_b_pw2, ffn_norm=m_ffn_norm, ffn_w_gu=m_ffn_w_gu, ffn_w_down=m_ffn_w_down,
                final_norm=m_final_norm)
    v_in = dict(attn_norm=v_attn_norm, attn_w_qkv=v_attn_w_qkv, attn_w_o=v_attn_w_o, attn_sink=v_attn_sink,
                conv_norm=v_conv_norm, conv_w_pw1=v_conv_w_pw1, conv_b_pw1=v_conv_b_pw1, conv_w_dw=v_conv_w_dw,
                conv_b_dw=v_conv_b_dw, conv_ln_g=v_conv_ln_g, conv_ln_b=v_conv_ln_b, conv_w_pw2=v_conv_w_pw2,
                conv_b_pw2=v_conv_b_pw2, ffn_norm=v_ffn_norm, ffn_w_gu=v_ffn_w_gu, ffn_w_down=v_ffn_w_down,
                final_norm=v_final_norm)
    order = list(weights)
    g_out, d_out, m_out, v_out = [], [], [], []
    for nm in order:
        w = weights[nm]
        shape = w.shape
        as3 = lambda t: t.reshape((1,) * (3 - len(shape)) + shape) if len(shape) < 3 else t.reshape(shape)
        g3 = as3(grads[nm].reshape(shape))
        delta, nm_, nv_ = adamw(as3(w), g3, as3(m_in[nm]), as3(v_in[nm]), "adamw_" + nm)
        g_out.append(g3.reshape(shape))
        d_out.append(delta.reshape(shape))
        m_out.append(nm_.reshape(shape))
        v_out.append(nv_.reshape(shape))
    return (loss, dx0[None], *g_out, *d_out, *m_out, *v_out)
```

```python
import math

import jax
import jax.numpy as jnp
from jax import lax
from jax.experimental import pallas as pl
from jax.experimental.pallas import tpu as pltpu

F32 = jnp.float32
BF16 = jnp.bfloat16

D = 1024
N_HEADS = 16
N_KV = 4
GROUP = N_HEADS // N_KV
HD = 64
ROT = 16
THETA = 500000.0
BLK = 128
QKV = (N_HEADS + 2 * N_KV) * HD
KV_OFF = N_HEADS * HD
DFF = 2816
CONV_W = 31
CONV_PAD = 15
HALO = 16
CONV_JB = 16
CONV_JB_BWD = 8
EPS = 1e-6
NEG = -1e30
N_CHIPS = 4
N_DEV = 8
LANES = 128
SUBLANES = 8

ADAM_LR, ADAM_B1, ADAM_B2, ADAM_EPS, ADAM_WD, ADAM_STEP = 0.001, 0.9, 0.999, 1e-08, 0.01, 10

VMEM_LIMIT = 56 * 1024 * 1024
MESH = pl.DeviceIdType.MESH


def _params(*sem):
    return pltpu.CompilerParams(dimension_semantics=sem, vmem_limit_bytes=VMEM_LIMIT)


def _tile(n, want):
    if n <= want:
        return n
    for t in range(want, 7, -1):
        if n % t == 0 and t % 8 == 0:
            return t
    return n


MXU_COLS = 256


def _col_chunks(n):
    return [slice(c, min(c + MXU_COLS, n)) for c in range(0, n, MXU_COLS)]


def _sigmoid(v):
    return jax.nn.sigmoid(v)


def _rms_fwd(xv, gain):
    r = lax.rsqrt(jnp.mean(xv * xv, axis=-1, keepdims=True) + EPS)
    return xv * r * gain


def _rms_bwd(dh, xv, gain, dres):
    r = lax.rsqrt(jnp.mean(xv * xv, axis=-1, keepdims=True) + EPS)
    xhat = xv * r
    gy = dh * gain
    dx = r * (gy - xhat * jnp.mean(gy * xhat, axis=-1, keepdims=True))
    return dx + dres, dh * xhat


def _rope(blk, c, s1, s2):
    return blk * c + pltpu.roll(blk, LANES - ROT // 2, 1) * s1 + pltpu.roll(blk, ROT // 2, 1) * s2


def _dot(a, b):
    return jnp.dot(a, b, preferred_element_type=F32)


def _dot_tb(a, b):
    return lax.dot_general(a, b, (((1,), (1,)), ((), ())), preferred_element_type=F32)


def _dot_ta(a, b):
    return lax.dot_general(a, b, (((0,), (0,)), ((), ())), preferred_element_type=F32)


def rms_first(x, gain, comm):
    T = x.shape[0]
    tm = _tile(T, 512)

    def body(x_ref, g_ref, h_ref):
        h_ref[...] = _rms_fwd(x_ref[...], g_ref[...]).astype(BF16)

    (h,), got = _call(
        body, name="rms_first", grid=(T // tm,),
        in_specs=[pl.BlockSpec((tm, D), lambda i: (i, 0)), pl.BlockSpec((1, D), lambda i: (0, 0))],
        out_specs=[pl.BlockSpec((tm, D), lambda i: (i, 0))], out_shape=[jax.ShapeDtypeStruct((T, D), BF16)],
        semantics=("parallel",), args=(x, gain), comm=comm)
    return h, got


def qkv_proj(h, w, rc, rs1, rs2, comm=None):
    T = h.shape[0]
    tm = _tile(T, 512)

    def body(h_ref, w_ref, c_ref, s1_ref, s2_ref, qkv_ref):
        acc = _dot(h_ref[...], w_ref[...])
        c, s1, s2 = c_ref[...], s1_ref[...], s2_ref[...]
        n_rot = (KV_OFF + N_KV * HD) // LANES
        for j in range(n_rot):
            sl = slice(LANES * j, LANES * (j + 1))
            roped = _rope(acc[:, sl], c, s1, s2)
            if j < KV_OFF // LANES:
                roped = roped * Q_SCALE
            qkv_ref[:, sl] = roped.astype(BF16)
        qkv_ref[:, n_rot * LANES:] = acc[:, n_rot * LANES:].astype(BF16)

    row = lambda i: (i, 0)
    full = lambda i: (0, 0)
    (qkv,), got = _call(
        body, name="qkv_proj", grid=(T // tm,),
        in_specs=[pl.BlockSpec((tm, D), row), pl.BlockSpec((D, QKV), full), *_tab_specs(tm)],
        out_specs=[pl.BlockSpec((tm, QKV), row)],
        out_shape=[jax.ShapeDtypeStruct((T, QKV), BF16)],
        semantics=("parallel",), args=(h, w, rc, rs1, rs2), comm=comm)
    return qkv, got


Q_SCALE = 1.0 / math.sqrt(HD)


def _attn_mask(n, T):
    ci = lax.broadcasted_iota(jnp.int32, (3 * BLK, BLK), 0)
    qi = lax.broadcasted_iota(jnp.int32, (3 * BLK, BLK), 1)
    key_pos = n * BLK - BLK + ci
    return (jnp.abs(ci - BLK - qi) <= BLK) & (key_pos >= 0) & (key_pos < T)


def _kv_padded(kv, first_tile):
    low = lax.broadcasted_iota(jnp.int32, (3 * BLK, LANES), 1) < HD
    zero = jnp.zeros((3 * BLK, LANES), BF16)
    out = {}
    for g in range(N_KV):
        t = kv[:, (first_tile + g // 2) * LANES:(first_tile + g // 2 + 1) * LANES]
        swapped = jnp.concatenate([t[:, HD:], t[:, :HD]], axis=1)
        for p in range(2):
            out[g, p] = jnp.where(low if p == 0 else ~low, t if g % 2 == p else swapped, zero)
    return out


def _pair_products(kvx, tile_of):
    both = {g: jnp.concatenate([kvx[g, 0], kvx[g, 1]], axis=0) for g in range(N_KV)}
    out = []
    for j in range(N_HEADS // 2):
        prod = _dot_tb(both[2 * j // GROUP], tile_of(j))
        out += [prod[:3 * BLK], prod[3 * BLK:]]
    return out


def _softmax_sink(s, valid, sk):
    s = jnp.where(valid, s, NEG)
    m = jnp.maximum(jnp.max(s, axis=0, keepdims=True), sk)
    e = jnp.exp(s - m)
    es = jnp.exp(sk - m)
    inv = 1.0 / (jnp.sum(e, axis=0, keepdims=True) + es)
    return e * inv, es * inv


def _attn_specs(T):
    nb = T // BLK
    kv_blk = 2 * N_KV * HD
    kv_col = KV_OFF // kv_blk
    q_spec = pl.BlockSpec((BLK, KV_OFF), lambda n: (n, 0))
    prev = pl.BlockSpec((BLK, kv_blk), lambda n: (jnp.maximum(n - 1, 0), kv_col))
    own = pl.BlockSpec((BLK, kv_blk), lambda n: (n, kv_col))
    nxt = pl.BlockSpec((BLK, kv_blk), lambda n: (jnp.minimum(n + 1, nb - 1), kv_col))
    return nb, q_spec, prev, own, nxt


def attn_fwd(qkv, sink, comm=None):
    T = qkv.shape[0]
    nb, q_spec, prev, own, nxt = _attn_specs(T)

    def body(sink_ref, q_ref, kp_ref, ko_ref, kn_ref, o_ref):
        valid = _attn_mask(pl.program_id(0), T)
        kv = jnp.concatenate([kp_ref[...], ko_ref[...], kn_ref[...]], axis=0)
        kx, vx = _kv_padded(kv, 0), _kv_padded(kv, 2)
        ss = _pair_products(kx, lambda j: q_ref[:, j * LANES:(j + 1) * LANES])
        ps = [_softmax_sink(ss[h], valid, sink_ref[h])[0].astype(BF16) for h in range(N_HEADS)]
        vxt = {k: v.T for k, v in vx.items()}
        for j in range(N_HEADS // 2):
            g = 2 * j // GROUP
            o_t = _dot(vxt[g, 0], ps[2 * j]) + _dot(vxt[g, 1], ps[2 * j + 1])
            o_ref[:, j * LANES:(j + 1) * LANES] = o_t.T.astype(BF16)

    (o,), got = _call(
        body, name="attn_fwd", grid=(nb,),
        in_specs=[pl.BlockSpec(memory_space=pltpu.SMEM), q_spec, prev, own, nxt],
        out_specs=[pl.BlockSpec((BLK, D), lambda n: (n, 0))],
        out_shape=[jax.ShapeDtypeStruct((T, D), BF16)],
        semantics=("parallel",), args=(sink, qkv, qkv, qkv, qkv), comm=comm)
    return o, got


def rms_mm_gate(x, gain, w, bias, H, swiglu, act_dtype, name, comm=None):
    fused = isinstance(x, tuple)
    T = (x[0] if fused else x).shape[0]
    tm = _tile(T, 512)

    def body(*refs):
        if fused:
            a_ref, wp_ref, bp_ref, r_ref, g_ref, w_ref, b_ref, x_ref, h_ref, pre_ref, act_ref = refs
            xv = _dot(a_ref[...], wp_ref[...]) + bp_ref[...] + r_ref[...]
            x_ref[...] = xv
        else:
            x_ref, g_ref, w_ref, b_ref, h_ref, pre_ref, act_ref = refs
            xv = x_ref[...]
        h = _rms_fwd(xv, g_ref[...]).astype(BF16)
        h_ref[...] = h
        for cs in _col_chunks(H):
            cs2 = slice(H + cs.start, H + cs.stop)
            a = _dot(h, w_ref[:, cs]) + b_ref[:, cs]
            b = _dot(h, w_ref[:, cs2]) + b_ref[:, cs2]
            pre_ref[0, :, cs] = a.astype(BF16)
            pre_ref[1, :, cs] = b.astype(BF16)
            if swiglu:
                act = a * _sigmoid(a) * b
            else:
                act = a * _sigmoid(b)
            act_ref[:, cs] = act.astype(act_dtype)

    row = lambda i: (i, 0)
    full = lambda i: (0, 0)
    if fused:
        K = x[0].shape[1]
        x_specs = [pl.BlockSpec((tm, K), row), pl.BlockSpec((K, D), full, pipeline_mode=pl.Buffered(1)),
                   pl.BlockSpec((1, D), full), pl.BlockSpec((tm, D), row)]
        x_out = ([pl.BlockSpec((tm, D), row)], [jax.ShapeDtypeStruct((T, D), F32)])
        x_args = tuple(x)
    else:
        x_specs, x_out, x_args = [pl.BlockSpec((tm, D), row)], ([], []), (x,)
    outs, got = _call(
        body, name=name, grid=(T // tm,),
        in_specs=x_specs + [pl.BlockSpec((1, D), full),
                            pl.BlockSpec((D, 2 * H), full, pipeline_mode=pl.Buffered(1)), pl.BlockSpec((1, 2 * H), full)],
        out_specs=x_out[0] + [pl.BlockSpec((tm, D), row), pl.BlockSpec((2, tm, H), lambda i: (0, i, 0)),
                              pl.BlockSpec((tm, H), row)],
        out_shape=x_out[1] + [jax.ShapeDtypeStruct((T, D), BF16), jax.ShapeDtypeStruct((2, T, H), BF16),
                              jax.ShapeDtypeStruct((T, H), act_dtype)],
        semantics=("parallel",), args=x_args + (gain, w, bias), comm=comm)
    return (*outs, got)


def _conv_tiles(T):
    tt = _tile(T, 512)
    return tt, tt // SUBLANES, D // LANES


def _fill_strided(ext, p, L):
    main = p[HALO:HALO + SUBLANES * L, :].reshape(SUBLANES, L, LANES)
    ext[CONV_PAD:CONV_PAD + L] = jnp.swapaxes(main, 0, 1)

    def ibody(i, carry):
        ext[i] = p[pl.ds(i + 1, SUBLANES, stride=L), :]
        ext[i + CONV_PAD + L] = p[pl.ds(i + CONV_PAD + L + 1, SUBLANES, stride=L), :]
        return carry

    lax.fori_loop(0, CONV_PAD, ibody, 0, unroll=3)


def _conv_specs(T, tt):
    main = pl.BlockSpec((tt, D), lambda i: (i, 0))
    per = tt // HALO
    prev = pl.BlockSpec((HALO, D), lambda i: (jnp.maximum(i * per - 1, 0), 0))
    nxt = pl.BlockSpec((HALO, D), lambda i: (jnp.minimum((i + 1) * per, T // HALO - 1), 0))
    return main, prev, nxt


def _fill_pad(pad, main_ref, prev_ref, next_ref, i, n_i, tt, nlt):
    keep_p = (i > 0).astype(F32)
    keep_n = (i < n_i - 1).astype(F32)
    for lt in range(nlt):
        sl = slice(lt * LANES, (lt + 1) * LANES)
        pad[lt, 0:HALO, :] = prev_ref[:, sl] * keep_p
        pad[lt, HALO:HALO + tt, :] = main_ref[:, sl]
        pad[lt, HALO + tt:2 * HALO + tt, :] = next_ref[:, sl] * keep_n


def conv_fwd(glu, w_dw, b_dw, ln_g, ln_b, comm=None):
    T = glu.shape[0]
    tt, L, nlt = _conv_tiles(T)
    n_i = T // tt
    main, prev, nxt = _conv_specs(T, tt)

    def body(x_ref, xp_ref, xn_ref, w_ref, b_ref, g_ref, bb_ref, dwc_ref, sw_ref, pad, ob, ext, wk):
        i = pl.program_id(0)
        _fill_pad(pad, x_ref, xp_ref, xn_ref, i, n_i, tt, nlt)
        for lt in range(nlt):
            sl = slice(lt * LANES, (lt + 1) * LANES)
            o = ob.at[lt]
            _fill_strided(ext, pad.at[lt], L)
            for k in range(CONV_W):
                wk[k] = jnp.broadcast_to(w_ref[k:k + 1, sl], (SUBLANES, LANES))

            def jbody(jb, carry):
                j = jb * CONV_JB
                accs = [None] * CONV_JB
                for m in range(CONV_W + CONV_JB - 1):
                    e = ext[j + m]
                    for u in range(CONV_JB):
                        if 0 <= m - u < CONV_W:
                            t = e * wk[m - u]
                            accs[u] = t if accs[u] is None else accs[u] + t
                for u in range(CONV_JB):
                    o[pl.ds(j + u, SUBLANES, stride=L), :] = accs[u]
                return carry

            lax.fori_loop(0, L // CONV_JB, jbody, 0)
        y = jnp.concatenate([ob[lt] for lt in range(nlt)], axis=1) + b_ref[...]
        dwc_ref[...] = y
        mu = jnp.mean(y, axis=-1, keepdims=True)
        yc = y - mu
        var = jnp.mean(yc * yc, axis=-1, keepdims=True)
        z = yc * lax.rsqrt(var + EPS) * g_ref[...] + bb_ref[...]
        sw_ref[...] = (z * _sigmoid(z)).astype(BF16)

    full = lambda i: (0, 0)
    (dwc, sw), got = _call(
        body, name="conv_fwd", grid=(n_i,),
        in_specs=[main, prev, nxt, pl.BlockSpec((32, D), full), pl.BlockSpec((1, D), full),
                  pl.BlockSpec((1, D), full), pl.BlockSpec((1, D), full)],
        out_specs=[pl.BlockSpec((tt, D), lambda i: (i, 0)), pl.BlockSpec((tt, D), lambda i: (i, 0))],
        out_shape=[jax.ShapeDtypeStruct((T, D), F32), jax.ShapeDtypeStruct((T, D), BF16)],
        scratch_shapes=[pltpu.VMEM((nlt, tt + 2 * HALO, LANES), F32), pltpu.VMEM((nlt, tt, LANES), F32),
                        pltpu.VMEM((L + 2 * HALO, SUBLANES, LANES), F32), pltpu.VMEM((32, SUBLANES, LANES), F32)],
        semantics=("parallel",), args=(glu, glu, glu, w_dw, b_dw, ln_g, ln_b), comm=comm)
    return dwc, sw, got


def mm_res_loss(a, w, resid, gain, target):
    T, K = a.shape
    tm = _tile(T, 512)

    def body(a_ref, w_ref, r_ref, g_ref, t_ref, dx_ref, loss_ref, dg_ref):
        @pl.when(pl.program_id(0) == 0)
        def _():
            loss_ref[...] = jnp.zeros_like(loss_ref)
            dg_ref[...] = jnp.zeros_like(dg_ref)

        xv, gain_v = _dot(a_ref[...], w_ref[...]) + r_ref[...], g_ref[...]
        err = _rms_fwd(xv, gain_v) - t_ref[...]
        part = 0.5 * jnp.sum(jnp.mean(err * err, axis=-1, keepdims=True), axis=0, keepdims=True)
        loss_ref[...] += jnp.broadcast_to(part, loss_ref.shape)
        dx, dgr = _rms_bwd(err * (1.0 / D), xv, gain_v, 0.0)
        dx_ref[...] = dx
        dg_ref[...] += jnp.sum(dgr, axis=0, keepdims=True)

    row = lambda i: (i, 0)
    full = lambda i: (0, 0)
    return pl.pallas_call(
        body, name="ffn1_down_loss", grid=(T // tm,),
        in_specs=[pl.BlockSpec((tm, K), row), pl.BlockSpec((K, D), full), pl.BlockSpec((tm, D), row),
                  pl.BlockSpec((1, D), full), pl.BlockSpec((tm, D), row)],
        out_specs=[pl.BlockSpec((tm, D), row), pl.BlockSpec((1, LANES), full), pl.BlockSpec((1, D), full)],
        out_shape=[jax.ShapeDtypeStruct((T, D), F32), jax.ShapeDtypeStruct((1, LANES), F32),
                   jax.ShapeDtypeStruct((1, D), F32)],
        compiler_params=_params("arbitrary"),
    )(a, w, resid, gain, target)


def swiglu_bwd(dx, w_down, pre, name, comm=None):
    T = dx.shape[0]
    H = w_down.shape[0]
    tm = _tile(T, 512)

    def body(dx_ref, w_ref, pre_ref, dpre_ref):
        dxb = dx_ref[...].astype(BF16)
        for cs in _col_chunks(H):
            dact = _dot_tb(dxb, w_ref[cs, :])
            g = pre_ref[0, :, cs].astype(F32)
            u = pre_ref[1, :, cs].astype(F32)
            sg = _sigmoid(g)
            dpre_ref[0, :, cs] = (dact * u * sg * (1.0 + g * (1.0 - sg))).astype(BF16)
            dpre_ref[1, :, cs] = (dact * g * sg).astype(BF16)

    (dpre,), got = _call(
        body, name=name, grid=(T // tm,),
        in_specs=[pl.BlockSpec((tm, D), lambda i: (i, 0)),
                  pl.BlockSpec((H, D), lambda i: (0, 0), pipeline_mode=pl.Buffered(1)),
                  pl.BlockSpec((2, tm, H), lambda i: (0, i, 0))],
        out_specs=[pl.BlockSpec((2, tm, H), lambda i: (0, i, 0))],
        out_shape=[jax.ShapeDtypeStruct((2, T, H), BF16)],
        semantics=("parallel",), args=(dx, w_down, pre), comm=comm)
    return dpre, got


def _ln_silu_bwd(dsw, y, ln_g, ln_b):
    mu = jnp.mean(y, axis=-1, keepdims=True)
    yc = y - mu
    rstd = lax.rsqrt(jnp.mean(yc * yc, axis=-1, keepdims=True) + EPS)
    xhat = yc * rstd
    z = xhat * ln_g + ln_b
    sg = _sigmoid(z)
    dz = dsw * sg * (1.0 + z * (1.0 - sg))
    dxh = dz * ln_g
    dy = rstd * (dxh - jnp.mean(dxh, axis=-1, keepdims=True) - xhat * jnp.mean(dxh * xhat, axis=-1, keepdims=True))
    return dy, dz * xhat, dz


def mm_bt_rmsbwd(dpre, w, x, gain, dres, name, comm=None, proj_w=None, conv_tail=None):
    nh, T, H = dpre.shape
    tm = _tile(T, 512)
    n_extra_in = 1 if proj_w is not None else (4 if conv_tail is not None else 0)

    def body(*refs):
        dp_ref, w_ref, x_ref, g_ref, dres_ref = refs[:5]
        extra_in = refs[5:5 + n_extra_in]
        dx_ref, dg_ref = refs[5 + n_extra_in:7 + n_extra_in]
        extra_out = refs[7 + n_extra_in:]

        @pl.when(pl.program_id(0) == 0)
        def _():
            dg_ref[...] = jnp.zeros_like(dg_ref)
            for r in extra_out[1:]:
                r[...] = jnp.zeros_like(r)

        dh = _dot_tb(dp_ref[0], w_ref[:, 0:H])
        for hf in range(1, nh):
            dh = dh + _dot_tb(dp_ref[hf], w_ref[:, hf * H:(hf + 1) * H])
        dx, dgr = _rms_bwd(dh, x_ref[...], g_ref[...], dres_ref[...])
        dx_ref[...] = dx
        dg_ref[...] += jnp.sum(dgr, axis=0, keepdims=True)
        if proj_w is not None:
            extra_out[0][...] = _dot_tb(dx.astype(BF16), extra_in[0][...]).astype(BF16)
        elif conv_tail is not None:
            wt_ref, y_ref, lg_ref, lb_ref = extra_in
            dy, dgl, dbl = _ln_silu_bwd(_dot_tb(dx.astype(BF16), wt_ref[...]), y_ref[...], lg_ref[...], lb_ref[...])
            extra_out[0][...] = dy
            extra_out[1][...] += jnp.sum(dgl, axis=0, keepdims=True)
            extra_out[2][...] += jnp.sum(dbl, axis=0, keepdims=True)
            extra_out[3][...] += jnp.sum(dx, axis=0, keepdims=True)

    row = lambda i: (i, 0)
    full = lambda i: (0, 0)
    vec = pl.BlockSpec((1, D), full)
    vec_shape = jax.ShapeDtypeStruct((1, D), F32)
    in_specs = [pl.BlockSpec((nh, tm, H), lambda i: (0, i, 0)),
                pl.BlockSpec((D, nh * H), full, pipeline_mode=pl.Buffered(1)),
                pl.BlockSpec((tm, D), row), vec, pl.BlockSpec((tm, D), row)]
    out_specs = [pl.BlockSpec((tm, D), row), vec]
    out_shape = [jax.ShapeDtypeStruct((T, D), F32), vec_shape]
    args = (dpre, w, x, gain, dres)
    if proj_w is not None:
        N = proj_w.shape[0]
        in_specs.append(pl.BlockSpec((N, D), full, pipeline_mode=pl.Buffered(1)))
        out_specs.append(pl.BlockSpec((tm, N), row))
        out_shape.append(jax.ShapeDtypeStruct((T, N), BF16))
        args += (proj_w,)
    elif conv_tail is not None:
        in_specs += [pl.BlockSpec((D, D), full, pipeline_mode=pl.Buffered(1)), pl.BlockSpec((tm, D), row), vec, vec]
        out_specs += [pl.BlockSpec((tm, D), row), vec, vec, vec]
        out_shape += [jax.ShapeDtypeStruct((T, D), F32), vec_shape, vec_shape, vec_shape]
        args += tuple(conv_tail)
    outs, got = _call(body, name=name, grid=(T // tm,), in_specs=in_specs, out_specs=out_specs, out_shape=out_shape,
                      semantics=("arbitrary",), args=args, comm=comm)
    return (*outs, got)


def dw_col(a, dpre, name):
    T = a.shape[0]
    nh, _, H = dpre.shape
    per = nh * H // N_CHIPS
    bph = N_CHIPS // nh
    tt = _tile(T, 2048)
    nt = T // tt

    def body(a_ref, b_ref, o_ref, acc):
        t = pl.program_id(1)

        @pl.when(t == 0)
        def _():
            acc[...] = jnp.zeros_like(acc)

        acc[...] += _dot_ta(a_ref[...], b_ref[...])

        @pl.when(t == nt - 1)
        def _():
            o_ref[...] = acc[...].astype(BF16)

    return pl.pallas_call(
        body, name=name, grid=(N_CHIPS, nt),
        in_specs=[pl.BlockSpec((tt, D), lambda q, t: (t, 0)),
                  pl.BlockSpec((None, tt, per), lambda q, t: (q // bph, t, q % bph))],
        out_specs=pl.BlockSpec((None, D, per), lambda q, t: (q, 0, 0)),
        out_shape=jax.ShapeDtypeStruct((N_CHIPS, D, per), BF16),
        scratch_shapes=[pltpu.VMEM((D, per), F32)],
        compiler_params=_params("parallel", "arbitrary"),
    )(a, dpre)


def dw_row(a, b, name):
    T, R = a.shape
    cw = 1408 if R % 1408 == 0 else R
    tt = _tile(T, 1024)
    nt = T // tt

    def body(a_ref, b_ref, o_ref, acc):
        t = pl.program_id(1)

        @pl.when(t == 0)
        def _():
            acc[...] = jnp.zeros_like(acc)

        acc[...] += _dot_ta(a_ref[...], b_ref[...].astype(BF16))

        @pl.when(t == nt - 1)
        def _():
            o_ref[...] = acc[...].astype(BF16)

    out = pl.pallas_call(
        body, name=name, grid=(R // cw, nt),
        in_specs=[pl.BlockSpec((tt, cw), lambda q, t: (t, q)), pl.BlockSpec((tt, D), lambda q, t: (t, 0))],
        out_specs=pl.BlockSpec((cw, D), lambda q, t: (q, 0)),
        out_shape=jax.ShapeDtypeStruct((R, D), BF16),
        scratch_shapes=[pltpu.VMEM((cw, D), F32)],
        compiler_params=_params("parallel", "arbitrary"),
    )(a, b)
    return out.reshape(N_CHIPS, R // N_CHIPS, D)


def conv_bwd(ddwc, glu, pre, w_dw, comm=None):
    T = ddwc.shape[0]
    tt, L, nlt = _conv_tiles(T)
    n_i = T // tt
    main, prev, nxt = _conv_specs(T, tt)

    def body(d_ref, dp_ref, dn_ref, x_ref, xp_ref, xn_ref, pre_ref, w_ref,
             dpre_ref, dw_ref, dbd_ref, dbp_ref, padd, padx, ob, extd, extx, wk):
        i = pl.program_id(0)

        @pl.when(i == 0)
        def _():
            dw_ref[...] = jnp.zeros_like(dw_ref)
            dbd_ref[...] = jnp.zeros_like(dbd_ref)
            dbp_ref[...] = jnp.zeros_like(dbp_ref)

        _fill_pad(padd, d_ref, dp_ref, dn_ref, i, n_i, tt, nlt)
        _fill_pad(padx, x_ref, xp_ref, xn_ref, i, n_i, tt, nlt)
        for lt in range(nlt):
            sl = slice(lt * LANES, (lt + 1) * LANES)
            o = ob.at[lt]
            _fill_strided(extd, padd.at[lt], L)
            _fill_strided(extx, padx.at[lt], L)
            for k in range(CONV_W):
                wk[k] = jnp.broadcast_to(w_ref[k:k + 1, sl], (SUBLANES, LANES))

            nu = CONV_JB_BWD

            def jbody(jb, accs):
                j = jb * nu
                accs = list(accs)
                d = [extd[j + u + CONV_PAD] for u in range(nu)]
                g = [None] * nu
                for m in range(CONV_W + nu - 1):
                    ed = extd[j + 2 * CONV_PAD + nu - 1 - m]
                    ex = extx[j + m]
                    for u in range(nu):
                        k = m - (nu - 1 - u)
                        if 0 <= k < CONV_W:
                            t = ed * wk[k]
                            g[u] = t if g[u] is None else g[u] + t
                        k = m - u
                        if 0 <= k < CONV_W:
                            accs[k] = accs[k] + d[u] * ex
                for u in range(nu):
                    o[pl.ds(j + u, SUBLANES, stride=L), :] = g[u]
                return tuple(accs)

            accs = lax.fori_loop(0, L // nu, jbody, tuple(jnp.zeros((SUBLANES, LANES), F32) for _ in range(CONV_W)))
            for k in range(CONV_W):
                dw_ref[k:k + 1, sl] += jnp.sum(accs[k], axis=0, keepdims=True)
        dglu = jnp.concatenate([ob[lt] for lt in range(nlt)], axis=1)
        a = pre_ref[0].astype(F32)
        gate = pre_ref[1].astype(F32)
        sg = _sigmoid(gate)
        da = dglu * sg
        dgate = dglu * a * sg * (1.0 - sg)
        dpre_ref[0] = da.astype(BF16)
        dpre_ref[1] = dgate.astype(BF16)
        dbd_ref[...] += jnp.sum(d_ref[...], axis=0, keepdims=True)
        dbp_ref[0] += jnp.sum(da, axis=0, keepdims=True)
        dbp_ref[1] += jnp.sum(dgate, axis=0, keepdims=True)

    full = lambda i: (0, 0)
    (dpre, dw, dbd, dbp), got = _call(
        body, name="conv_bwd", grid=(n_i,),
        in_specs=[main, prev, nxt, main, prev, nxt, pl.BlockSpec((2, tt, D), lambda i: (0, i, 0)),
                  pl.BlockSpec((32, D), full)],
        out_specs=[pl.BlockSpec((2, tt, D), lambda i: (0, i, 0)), pl.BlockSpec((32, D), full),
                   pl.BlockSpec((1, D), full), pl.BlockSpec((2, 1, D), lambda i: (0, 0, 0))],
        out_shape=[jax.ShapeDtypeStruct((2, T, D), BF16), jax.ShapeDtypeStruct((32, D), F32),
                   jax.ShapeDtypeStruct((1, D), F32), jax.ShapeDtypeStruct((2, 1, D), F32)],
        scratch_shapes=[pltpu.VMEM((nlt, tt + 2 * HALO, LANES), F32), pltpu.VMEM((nlt, tt + 2 * HALO, LANES), F32),
                        pltpu.VMEM((nlt, tt, LANES), F32), pltpu.VMEM((L + 2 * HALO, SUBLANES, LANES), F32),
                        pltpu.VMEM((L + 2 * HALO, SUBLANES, LANES), F32), pltpu.VMEM((32, SUBLANES, LANES), F32)],
        semantics=("arbitrary",), args=(ddwc, ddwc, ddwc, glu, glu, glu, pre, w_dw), comm=comm)
    return dpre, dw, dbd, dbp, got


def attn_bwd(qkv, o, do, sink, rc, rs1, rs2, comm=None):
    T = qkv.shape[0]
    nb, q_spec, prev, own, nxt = _attn_specs(T)
    kvw = N_KV * HD

    def body(sink_ref, q_ref, kp_ref, ko_ref, kn_ref, o_ref, do_ref, c_ref, s1_ref, s2_ref,
             dq_ref, dkc_ref, dvc_ref, dsink_ref):
        n = pl.program_id(0)

        @pl.when(n == 0)
        def _():
            dsink_ref[...] = jnp.zeros_like(dsink_ref)

        valid = _attn_mask(n, T)
        kv = jnp.concatenate([kp_ref[...], ko_ref[...], kn_ref[...]], axis=0)
        kx, vx = _kv_padded(kv, 0), _kv_padded(kv, 2)
        tile = lambda ref, j: ref[:, j * LANES:(j + 1) * LANES]
        ss = _pair_products(kx, lambda j: tile(q_ref, j))
        dps = _pair_products(vx, lambda j: tile(do_ref, j))
        low_d = lax.broadcasted_iota(jnp.int32, (LANES, BLK), 0) < HD
        deltas = []
        for j in range(N_HEADS // 2):
            prod_t = tile(do_ref, j).astype(F32).T * tile(o_ref, j).astype(F32).T
            deltas.append(jnp.sum(jnp.where(low_d, prod_t, 0.0), axis=0, keepdims=True))
            deltas.append(jnp.sum(jnp.where(low_d, 0.0, prod_t), axis=0, keepdims=True))
        lane = lax.broadcasted_iota(jnp.int32, (1, N_HEADS), 1)
        dsink = jnp.zeros((1, N_HEADS), F32)
        pbs, dss = [], []
        for h in range(N_HEADS):
            p, p_sink = _softmax_sink(ss[h], valid, sink_ref[h])
            dss.append((p * (dps[h] - deltas[h])).astype(BF16))
            pbs.append(p.astype(BF16))
            part = -jnp.sum(p_sink * deltas[h], axis=1, keepdims=True)
            dsink = dsink + jnp.where(lane == h, part, 0.0)
        dsink_ref[...] += dsink
        c, s1, s2 = c_ref[...], s1_ref[...], s2_ref[...]
        kxt = {k: v.T for k, v in kx.items()}
        for j in range(N_HEADS // 2):
            g = 2 * j // GROUP
            dq_t = _dot(kxt[g, 0], dss[2 * j]) + _dot(kxt[g, 1], dss[2 * j + 1])
            dq_ref[:, j * LANES:(j + 1) * LANES] = (_rope(dq_t.T, c, -s1, -s2) * Q_SCALE).astype(BF16)
        low_k = lax.broadcasted_iota(jnp.int32, (3 * BLK, LANES), 1) < HD
        cols = lambda xs, g, p: jnp.concatenate([xs[GROUP * g + p], xs[GROUP * g + 2 + p]], axis=1)
        for t in range(N_KV // 2):
            sums = {}
            for g in (2 * t, 2 * t + 1):
                q2 = jnp.concatenate([tile(q_ref, 2 * g), tile(q_ref, 2 * g + 1)], axis=0)
                do2 = jnp.concatenate([tile(do_ref, 2 * g), tile(do_ref, 2 * g + 1)], axis=0)
                dk2 = _dot(jnp.concatenate([cols(dss, g, 0), cols(dss, g, 1)], axis=0), q2)
                dv2 = _dot(jnp.concatenate([cols(pbs, g, 0), cols(pbs, g, 1)], axis=0), do2)
                for p in range(2):
                    sums[g, p] = (dk2[p * 3 * BLK:(p + 1) * 3 * BLK], dv2[p * 3 * BLK:(p + 1) * 3 * BLK])
            for which, ref in ((0, dkc_ref), (1, dvc_ref)):
                keep = jnp.where(low_k, sums[2 * t, 0][which], sums[2 * t + 1, 1][which])
                swap = jnp.where(low_k, sums[2 * t + 1, 0][which], sums[2 * t, 1][which])
                ref[:, t * LANES:(t + 1) * LANES] = keep + pltpu.roll(swap, HD, 1)

    row = lambda n: (n, 0)
    (dq, dkc, dvc, dsink), got = _call(
        body, name="attn_bwd", grid=(nb,),
        in_specs=[pl.BlockSpec(memory_space=pltpu.SMEM), q_spec, prev, own, nxt,
                  pl.BlockSpec((BLK, D), row), pl.BlockSpec((BLK, D), row), *_tab_specs(BLK)],
        out_specs=[pl.BlockSpec((BLK, D), row), pl.BlockSpec((None, 3 * BLK, kvw), lambda n: (n, 0, 0)),
                   pl.BlockSpec((None, 3 * BLK, kvw), lambda n: (n, 0, 0)), pl.BlockSpec((1, N_HEADS), lambda n: (0, 0))],
        out_shape=[jax.ShapeDtypeStruct((T, QKV), BF16), jax.ShapeDtypeStruct((nb, 3 * BLK, kvw), F32),
                   jax.ShapeDtypeStruct((nb, 3 * BLK, kvw), F32), jax.ShapeDtypeStruct((1, N_HEADS), F32)],
        semantics=("arbitrary",), args=(sink, qkv, qkv, qkv, qkv, o, do, rc, rs1, rs2), comm=comm)
    return dq, dkc, dvc, dsink, got


def kv_sum(dqkv, dkc, dvc, rc, rs1, rs2):
    nb = dkc.shape[0]
    T = nb * BLK
    kvw = N_KV * HD

    G = 4
    ng = nb // G

    def gather3(own_ref, prev_ref, before_ref, next_ref, after_ref, m):
        has_before = (m > 0).astype(F32)
        has_after = (m < ng - 1).astype(F32)
        out = []
        for i in range(G):
            from_prev = prev_ref[i - 1] if i > 0 else before_ref[0] * has_before
            from_next = next_ref[i + 1] if i < G - 1 else after_ref[0] * has_after
            out.append(from_prev + own_ref[i] + from_next)
        return jnp.concatenate(out, axis=0)

    def body(_, ko, kp, kb, kn, ka, vo, vp, vb, vn, va, c_ref, s1_ref, s2_ref, out_ref):
        m = pl.program_id(0)
        dk = gather3(ko, kp, kb, kn, ka, m)
        dv = gather3(vo, vp, vb, vn, va, m)
        c, s1, s2 = c_ref[...], s1_ref[...], s2_ref[...]
        for j in range(kvw // LANES):
            sl = slice(LANES * j, LANES * (j + 1))
            out_ref[:, sl] = _rope(dk[:, sl], c, -s1, -s2).astype(BF16)
        out_ref[:, kvw:] = dv.astype(BF16)

    own = pl.BlockSpec((G, BLK, kvw), lambda m: (m, 1, 0))
    prev = pl.BlockSpec((G, BLK, kvw), lambda m: (m, 2, 0))
    before = pl.BlockSpec((1, BLK, kvw), lambda m: (jnp.maximum(G * m - 1, 0), 2, 0))
    nxt = pl.BlockSpec((G, BLK, kvw), lambda m: (m, 0, 0))
    after = pl.BlockSpec((1, BLK, kvw), lambda m: (jnp.minimum(G * m + G, nb - 1), 0, 0))
    five = [own, prev, before, nxt, after]
    return pl.pallas_call(
        body, name="kv_sum", grid=(ng,),
        in_specs=[pl.BlockSpec(memory_space=pl.ANY), *five, *five, *_tab_specs(G * BLK)],
        out_specs=pl.BlockSpec((G * BLK, 2 * kvw), lambda m: (m, KV_OFF // (2 * kvw))),
        out_shape=jax.ShapeDtypeStruct((T, QKV), BF16),
        input_output_aliases={0: 0},
        compiler_params=_params("parallel"),
    )(dqkv, *([dkc] * 5), *([dvc] * 5), rc, rs1, rs2)


def _me():
    return lax.axis_index("x"), lax.axis_index("y"), lax.axis_index("c")


def _half_rows(ref, sharded_rows, chip, core):
    R, C = ref.shape[-2], ref.shape[-1]
    lead = (slice(None),) * (len(ref.shape) - 2)
    if sharded_rows:
        per = R // N_CHIPS
        return ref.at[lead + (pl.ds(chip * per + core * (per // 2), per // 2), slice(None))]
    per = C // N_CHIPS
    return ref.at[lead + (pl.ds(core * (R // 2), R // 2), pl.ds(chip * per, per))]


class _Gather:
    def __init__(self, shards, sharded_rows):
        self.inputs = list(shards)
        self.rows = list(sharded_rows)
        self.n = self.n_in = self.n_out = len(shards)
        self.out_shapes = []
        for s, rows in zip(shards, sharded_rows):
            shp = list(s.shape)
            shp[-2 if rows else -1] *= N_CHIPS
            self.out_shapes.append(jax.ShapeDtypeStruct(tuple(shp), s.dtype))
        self.scratch = [pltpu.SemaphoreType.DMA((self.n, 6)), pltpu.SemaphoreType.DMA((self.n, 6)),
                        pltpu.SemaphoreType.DMA((self.n, 2))]

    def _ctx(self, ins, outs, sems):
        send_sems, recv_sems, local_sems = sems
        x, y, c = _me()
        chips = [(1 - x, y), (x, 1 - y), (1 - x, 1 - y)]

        def half_src(w, core):
            s = ins[w]
            R = s.shape[-2]
            return s.at[pl.ds(core * (R // 2), R // 2), :]

        def dst(w, chip, core):
            return _half_rows(outs[w], self.rows[w], chip, core)

        def copy(w, k, src, chip, core, to):
            return pltpu.make_async_remote_copy(
                src_ref=src, dst_ref=dst(w, chip, core), send_sem=send_sems.at[w, k], recv_sem=recv_sems.at[w, k],
                device_id=to, device_id_type=MESH)

        def local(w, core):
            return pltpu.make_async_copy(half_src(w, core), dst(w, 2 * x + y, core), local_sems.at[w, core])

        def first(w, j):
            qx, qy = chips[j]
            return copy(w, j, half_src(w, c), 2 * x + y, c, (qx, qy, c))

        def landed(w, j):
            qx, qy = chips[j]
            return copy(w, j, dst(w, 2 * qx + qy, c), 2 * qx + qy, c, (x, y, c))

        def passed(w, j):
            qx, qy = chips[j]
            return copy(w, 3 + j, dst(w, 2 * qx + qy, c), 2 * qx + qy, c, (x, y, 1 - c))

        def from_sibling(w, j):
            qx, qy = chips[j]
            return copy(w, 3 + j, dst(w, 2 * qx + qy, 1 - c), 2 * qx + qy, 1 - c, (x, y, c))

        return local, first, landed, passed, from_sibling

    def start(self, ins, outs, sems):
        local, first, _, _, _ = self._ctx(ins, outs, sems)
        for w in range(self.n):
            for core in range(2):
                local(w, core).start()
            for j in range(3):
                first(w, j).start()

    def mid(self, ins, outs, sems):
        _, _, landed, passed, _ = self._ctx(ins, outs, sems)
        for w in range(self.n):
            for j in range(3):
                landed(w, j).wait_recv()
                passed(w, j).start()

    def end(self, ins, outs, sems):
        local, first, _, passed, from_sibling = self._ctx(ins, outs, sems)
        for w in range(self.n):
            for j in range(3):
                from_sibling(w, j).wait_recv()
        for w in range(self.n):
            for j in range(3):
                first(w, j).wait_send()
                passed(w, j).wait_send()
            for core in range(2):
                local(w, core).wait()


class _Scatter:
    def __init__(self, grads, small=None):
        self.inputs = list(grads) + ([small] if small is not None else [])
        self.ng = len(grads)
        self.n = self.n_in = self.n_out = len(self.inputs)
        self.out_shapes = [jax.ShapeDtypeStruct((N_DEV, g.shape[1] // 2, g.shape[2]), g.dtype) for g in grads]
        if small is not None:
            self.out_shapes.append(jax.ShapeDtypeStruct((N_DEV,) + small.shape, small.dtype))
        self.scratch = [pltpu.SemaphoreType.DMA((self.n, N_DEV)), pltpu.SemaphoreType.DMA((self.n, N_DEV)),
                        pltpu.SemaphoreType.DMA((self.n,))]

    def _ctx(self, ins, outs, sems):
        send_sems, recv_sems, local_sems = sems
        x, y, c = _me()
        me = 4 * x + 2 * y + c

        def piece(w, chip, core):
            if w >= self.ng:
                return ins[w]
            half = ins[w].shape[1] // 2
            return ins[w].at[chip, pl.ds(core * half, half), :]

        def peer_of(k):
            return x ^ ((k >> 2) & 1), y ^ ((k >> 1) & 1), c ^ (k & 1)

        def local(w):
            return pltpu.make_async_copy(piece(w, 2 * x + y, c), outs[w].at[me], local_sems.at[w])

        def send(w, k):
            px, py, pc = peer_of(k)
            return pltpu.make_async_remote_copy(
                src_ref=piece(w, 2 * px + py, pc), dst_ref=outs[w].at[me], send_sem=send_sems.at[w, k],
                recv_sem=recv_sems.at[w, k], device_id=(px, py, pc), device_id_type=MESH)

        def recv(w, k):
            px, py, pc = peer_of(k)
            return pltpu.make_async_remote_copy(
                src_ref=piece(w, 2 * x + y, c), dst_ref=outs[w].at[4 * px + 2 * py + pc], send_sem=send_sems.at[w, k],
                recv_sem=recv_sems.at[w, k], device_id=(px, py, pc), device_id_type=MESH)

        return local, send, recv

    def start(self, ins, outs, sems):
        local, send, _ = self._ctx(ins, outs, sems)
        for w in range(self.n):
            local(w).start()
            for k in range(1, N_DEV):
                send(w, k).start()

    def mid(self, ins, outs, sems):
        pass

    def end(self, ins, outs, sems):
        local, send, recv = self._ctx(ins, outs, sems)
        for w in range(self.n):
            for k in range(1, N_DEV):
                recv(w, k).wait_recv()
        for w in range(self.n):
            for k in range(1, N_DEV):
                send(w, k).wait_send()
            local(w).wait()


class _Both:
    def __init__(self, a, b):
        self.a, self.b = a, b
        self.inputs = a.inputs + b.inputs
        self.out_shapes = a.out_shapes + b.out_shapes
        self.scratch = a.scratch + b.scratch
        self.n_in, self.n_out = a.n_in + b.n_in, a.n_out + b.n_out

    def _split(self, ins, outs, sems):
        a, na = self.a, len(self.a.scratch)
        return (ins[:a.n_in], outs[:a.n_out], sems[:na]), (ins[a.n_in:], outs[a.n_out:], sems[na:])

    def start(self, ins, outs, sems):
        pa, pb = self._split(ins, outs, sems)
        self.a.start(*pa)
        self.b.start(*pb)

    def mid(self, ins, outs, sems):
        pa, pb = self._split(ins, outs, sems)
        self.a.mid(*pa)
        self.b.mid(*pb)

    def end(self, ins, outs, sems):
        pa, pb = self._split(ins, outs, sems)
        self.a.end(*pa)
        self.b.end(*pb)


def _call(body, *, name, grid, in_specs, out_specs, out_shape, scratch_shapes=(), semantics, args, comm=None):
    if comm is None:
        outs = pl.pallas_call(
            body, name=name, grid=grid, in_specs=in_specs, out_specs=out_specs, out_shape=out_shape,
            scratch_shapes=list(scratch_shapes), compiler_params=_params(*semantics))(*args)
        return outs, []
    n_in, n_out, n_scr = len(in_specs), len(out_specs), len(scratch_shapes)

    total = math.prod(grid)
    first, middle, last = 0, (3 * total) // 4 - 1, total - 1
    assert first <= middle < last

    def at(step):
        lin = pl.program_id(0)
        for d in range(1, len(grid)):
            lin = lin * grid[d] + pl.program_id(d)
        return lin == step

    def hosted(*refs):
        h_in, c_in = refs[:n_in], refs[n_in:n_in + comm.n_in]
        rest = refs[n_in + comm.n_in:]
        h_out, c_out = rest[:n_out], rest[n_out:n_out + comm.n_out]
        rest = rest[n_out + comm.n_out:]
        h_scr, c_scr = rest[:n_scr], rest[n_scr:]

        @pl.when(at(first))
        def _():
            comm.start(c_in, c_out, c_scr)

        body(*h_in, *h_out, *h_scr)

        @pl.when(at(middle))
        def _():
            comm.mid(c_in, c_out, c_scr)

        @pl.when(at(last))
        def _():
            comm.end(c_in, c_out, c_scr)

    any_spec = pl.BlockSpec(memory_space=pl.ANY)
    outs = pl.pallas_call(
        hosted, name=name, grid=grid, in_specs=list(in_specs) + [any_spec] * comm.n_in,
        out_specs=list(out_specs) + [any_spec] * comm.n_out, out_shape=list(out_shape) + comm.out_shapes,
        scratch_shapes=list(scratch_shapes) + comm.scratch,
        compiler_params=_params(*(["arbitrary"] * len(grid))))(*args, *comm.inputs)
    return outs[:n_out], outs[n_out:]


def sum_swap(pieces, name, comm=None):
    nl = len(pieces)
    _, r2, cc = pieces[0].shape
    tr = 128 if r2 % 128 == 0 else r2 // 2
    n = r2 // tr

    def body(*refs):
        p_refs, out = refs[:nl], refs[nl]
        slots, send_sems, local_sems, recv_sem = refs[nl + 1:]
        x, y, c = _me()
        sibling = (x, y, 1 - c)
        l, i = pl.program_id(0), pl.program_id(1)
        step = l * n + i

        def rows(st, core):
            return out.at[st // n, pl.ds(core * r2 + (st % n) * tr, tr), :]

        def copies(st):
            slot = st % 2
            local = pltpu.make_async_copy(slots.at[slot], rows(st, c), local_sems.at[slot])
            remote = pltpu.make_async_remote_copy(
                src_ref=slots.at[slot], dst_ref=rows(st, c), send_sem=send_sems.at[slot], recv_sem=recv_sem,
                device_id=sibling, device_id_type=MESH)
            return local, remote

        for ll in range(nl):
            @pl.when(l == ll)
            def _():
                acc = p_refs[ll][0].astype(F32)
                for d in range(1, N_DEV):
                    acc = acc + p_refs[ll][d].astype(F32)
                slots[step % 2] = acc

        for cp in copies(step):
            cp.start()

        @pl.when(step >= 1)
        def _():
            local, remote = copies(step - 1)
            local.wait()
            remote.wait_send()

        @pl.when(step == nl * n - 1)
        def _():
            local, remote = copies(step)
            local.wait()
            remote.wait_send()
            theirs = out.at[:, pl.ds((1 - c) * r2, r2), :]
            pltpu.make_async_remote_copy(src_ref=theirs, dst_ref=theirs, send_sem=send_sems.at[0],
                                         recv_sem=recv_sem, device_id=sibling, device_id_type=MESH).wait_recv()

    def piece_spec(ll):
        def index(l, i):
            return (0, jnp.where(l == ll, i, jnp.where(l < ll, 0, n - 1)), 0)
        return pl.BlockSpec((N_DEV, tr, cc), index)

    (out,), got = _call(
        body, name=name, grid=(nl, n),
        in_specs=[piece_spec(ll) for ll in range(nl)],
        out_specs=[pl.BlockSpec(memory_space=pl.ANY)],
        out_shape=[jax.ShapeDtypeStruct((nl, 2 * r2, cc), F32)],
        scratch_shapes=[pltpu.VMEM((2, tr, cc), F32), pltpu.SemaphoreType.DMA((2,)), pltpu.SemaphoreType.DMA((2,)),
                        pltpu.SemaphoreType.DMA(())],
        semantics=("arbitrary", "arbitrary"), args=tuple(pieces), comm=comm)
    return (out, got) if comm is not None else out


def sum_pieces(pieces, name):
    _, R, C = pieces.shape
    tr = _tile(R, 128) if R % 128 == 0 else R

    def body(p_ref, o_ref):
        acc = p_ref[0].astype(F32)
        for d in range(1, N_DEV):
            acc = acc + p_ref[d].astype(F32)
        o_ref[...] = acc

    return pl.pallas_call(
        body, name=name, grid=(R // tr,),
        in_specs=[pl.BlockSpec((N_DEV, tr, C), lambda i: (0, i, 0))],
        out_specs=pl.BlockSpec((tr, C), lambda i: (i, 0)),
        out_shape=jax.ShapeDtypeStruct((R, C), F32),
        compiler_params=_params("parallel"),
    )(pieces)


def adamw(w, g, m, v, name):
    Lyr, R, C = w.shape
    tr = _tile(R, 256) if R % 8 == 0 else R
    c1 = 1.0 / (1.0 - ADAM_B1 ** ADAM_STEP)
    c2 = 1.0 / (1.0 - ADAM_B2 ** ADAM_STEP)

    def body(w_ref, g_ref, m_ref, v_ref, d_ref, nm_ref, nv_ref):
        gv = g_ref[...]
        nm = ADAM_B1 * m_ref[...] + (1.0 - ADAM_B1) * gv
        nv = ADAM_B2 * v_ref[...] + (1.0 - ADAM_B2) * (gv * gv)
        nm_ref[...] = nm
        nv_ref[...] = nv
        d_ref[...] = -ADAM_LR * ((nm * c1) / (jnp.sqrt(nv * c2) + ADAM_EPS) + ADAM_WD * w_ref[...])

    spec = pl.BlockSpec((None, tr, C), lambda l, i: (l, i, 0))
    shp = jax.ShapeDtypeStruct(w.shape, F32)
    return pl.pallas_call(
        body, name=name, grid=(Lyr, R // tr),
        in_specs=[spec] * 4, out_specs=[spec] * 3, out_shape=[shp] * 3,
        compiler_params=_params("parallel", "parallel"),
    )(w, g, m, v)


def _rope_tables(T):
    pos = jnp.arange(T, dtype=F32)
    inv_freq = THETA ** (-jnp.arange(0, ROT, 2, dtype=F32) / ROT)
    ang = pos[:, None] * inv_freq[None, :]
    cs = jnp.concatenate([jnp.cos(ang), jnp.sin(ang)], axis=1)
    half = ROT // 2
    lane = jnp.arange(3 * LANES)
    table, lm = lane // LANES, lane % HD
    src = jnp.where(table == 0, lm % half, half + lm % half)
    i32 = lambda b: b.astype(jnp.int32)
    sign = jnp.where(table == 0, i32(lm < ROT), jnp.where(table == 1, -i32(lm < half), i32((lm >= half) & (lm < ROT))))
    place = (jnp.arange(ROT)[:, None] == src[None, :]) * sign[None, :].astype(F32)
    ones = ((table == 0) & (lm >= ROT)).astype(F32)
    return jnp.dot(cs, place, precision=lax.Precision.HIGHEST) + ones[None, :]


def _tab_specs(rows):
    return [pl.BlockSpec((rows, LANES), lambda i, k=k: (i, k)) for k in range(3)]


def kernel(x, attn_norm, attn_w_qkv, attn_w_o, attn_sink, conv_norm, conv_w_pw1, conv_b_pw1, conv_w_dw, conv_b_dw, conv_ln_g, conv_ln_b, conv_w_pw2, conv_b_pw2, ffn_norm, ffn_w_gu, ffn_w_down, final_norm, loss_target, m_attn_norm, m_attn_w_qkv, m_attn_w_o, m_attn_sink, m_conv_norm, m_conv_w_pw1, m_conv_b_pw1, m_conv_w_dw, m_conv_b_dw, m_conv_ln_g, m_conv_ln_b, m_conv_w_pw2, m_conv_b_pw2, m_ffn_norm, m_ffn_w_gu, m_ffn_w_down, m_final_norm, v_attn_norm, v_attn_w_qkv, v_attn_w_o, v_attn_sink, v_conv_norm, v_conv_w_pw1, v_conv_b_pw1, v_conv_w_dw, v_conv_b_dw, v_conv_ln_g, v_conv_ln_b, v_conv_w_pw2, v_conv_b_pw2, v_ffn_norm, v_ffn_w_gu, v_ffn_w_down, v_final_norm):
    T = x.shape[1]
    x0 = x[0]
    target = loss_target[0]
    ix, iy = lax.axis_index("x"), lax.axis_index("y")
    chip = 2 * ix + iy
    rc = rs1 = rs2 = _rope_tables(T)

    bf = lambda t: t.astype(BF16)
    col_row = [False, True]

    def place(vec, width):
        return lax.dynamic_update_slice(jnp.zeros((vec.shape[0], N_CHIPS * width), F32), vec, (0, chip * width))

    small_rows = jnp.concatenate([
        place(conv_norm, 256), place(conv_b_pw1, 512).reshape(2, D), place(conv_b_dw, 256), place(conv_ln_g, 256),
        place(conv_ln_b, 256), place(conv_b_pw2, 256), jnp.zeros((1, D), F32),
        place(conv_w_dw[0], 256), jnp.zeros((1, D), F32)], axis=0)

    h0, (w_qkv,) = rms_first(x0, attn_norm, comm=_Gather([bf(attn_w_qkv[0])], [False]))
    qkv, (w_o, got) = qkv_proj(h0, w_qkv, rc, rs1, rs2,
                               comm=_Both(_Gather([bf(attn_w_o[0])], [True]), _Scatter([], small_rows)))
    psmall = sum_pieces(got, "sum_small_params") * 0.5
    p_conv_norm, p_b_pw1 = psmall[0:1], psmall[1:3].reshape(1, 2 * D)
    p_b_dw, p_ln_g, p_ln_b, p_b_pw2 = psmall[3:4], psmall[4:5], psmall[5:6], psmall[6:7]
    p_w_dw = psmall[8:40]
    sink = attn_sink[0]
    o, (w_gu0,) = attn_fwd(qkv, sink, comm=_Gather([bf(ffn_w_gu[0])], [False]))
    zero_b = jnp.zeros((1, D), F32)
    zero_gu = jnp.zeros((1, 2 * DFF), F32)
    x1, h1, gu0, act0, (w_down0, w_pw1, w_pw2) = rms_mm_gate(
        (o, w_o, zero_b, x0), ffn_norm[0:1], w_gu0, zero_gu, DFF, True, BF16, "ffn0_up",
        comm=_Gather([bf(ffn_w_down[0]), bf(conv_w_pw1[0]), bf(conv_w_pw2[0])], [True, False, True]))
    x2, h2, pre, glu, _ = rms_mm_gate((act0, w_down0, zero_b, x1), p_conv_norm, w_pw1, p_b_pw1, D, False, F32,
                                      "conv_pw1")
    dwc, sw, (w_gu1, w_down1) = conv_fwd(glu, p_w_dw, p_b_dw, p_ln_g, p_ln_b,
                                         comm=_Gather([bf(ffn_w_gu[1]), bf(ffn_w_down[1])], col_row))
    x3, h3, gu1, act1, _ = rms_mm_gate((sw, w_pw2, p_b_pw2, x2), ffn_norm[1:2], w_gu1, zero_gu, DFF, True, BF16,
                                       "ffn1_up")
    dx4, loss_part, d_final = mm_res_loss(act1, w_down1, x3, final_norm.reshape(1, D), target)

    dgu1, _ = swiglu_bwd(dx4, w_down1, gu1, "ffn1_down_bwd")
    g_down1 = dw_row(act1, dx4, "ffn1_down_dw")
    dx3, d_ffn1, ddwc, d_ln_g, d_ln_b, d_b_pw2, _ = mm_bt_rmsbwd(
        dgu1, w_gu1, x3, ffn_norm[1:2], dx4, "ffn1_up_bwd", conv_tail=(w_pw2, dwc, p_ln_g, p_ln_b))
    g_gu1 = dw_col(h3, dgu1, "ffn1_up_dw")

    g_pw2 = dw_row(sw, dx3, "conv_pw2_dw")
    dpre, d_w_dw, d_b_dw, d_b_pw1, (r_gu1, r_down1) = conv_bwd(ddwc, glu, pre, p_w_dw,
                                                               comm=_Scatter([g_gu1, g_down1]))
    dx2, d_conv_norm, _ = mm_bt_rmsbwd(dpre, w_pw1, x2, p_conv_norm, dx3, "conv_pw1_bwd")
    g_pw1 = dw_col(h2, dpre, "conv_pw1_dw")

    dgu0, (r_pw1, r_pw2) = swiglu_bwd(dx2, w_down0, gu0, "ffn0_down_bwd", comm=_Scatter([g_pw1, g_pw2]))
    g_down0 = dw_row(act0, dx2, "ffn0_down_dw")
    dx1, d_ffn0, do, _ = mm_bt_rmsbwd(dgu0, w_gu0, x1, ffn_norm[0:1], dx2, "ffn0_up_bwd", proj_w=w_o)
    g_gu0 = dw_col(h1, dgu0, "ffn0_up_dw")

    g_o = dw_row(o, dx1, "attn_out_dw")
    dq, dkc, dvc, d_sink, (r_gu0, r_down0, r_o) = attn_bwd(qkv, o, do, sink, rc, rs1, rs2,
                                                           comm=_Scatter([g_gu0, g_down0, g_o]))
    dqkv = kv_sum(dq, dkc, dvc, rc, rs1, rs2)[None]
    g_qkv = dw_col(h0, dqkv, "attn_qkv_dw")
    dx0, d_attn_norm, _ = mm_bt_rmsbwd(dqkv, w_qkv, x0, attn_norm, dx1, "attn_qkv_bwd")

    pad16 = lambda t: jnp.concatenate([t, jnp.zeros((1, D - t.shape[1]), F32)], axis=1)
    small_g = jnp.concatenate([
        d_attn_norm, pad16(d_sink), d_conv_norm, d_b_pw1.reshape(2, D), d_b_dw, d_ln_g, d_ln_b, d_b_pw2,
        d_ffn0, d_ffn1, d_final, pad16(loss_part), jnp.zeros((3, D), F32), d_w_dw], axis=0)
    gf_gu, (r_qkv, r_small) = sum_swap([r_gu0, r_gu1], "sum_gu", comm=_Scatter([g_qkv], small_g))
    gf_down = sum_swap([r_down0, r_down1], "sum_down")
    gf_pw1, gf_pw2 = sum_swap([r_pw1], "sum_pw1"), sum_swap([r_pw2], "sum_pw2")
    gf_qkv, gf_o = sum_swap([r_qkv], "sum_qkv"), sum_swap([r_o], "sum_o")
    gs = sum_pieces(r_small, "sum_small_grads")
    loss = gs[12, 0]

    def take(row0, nrows, width):
        return lax.dynamic_slice(gs, (row0, chip * width), (nrows, width))

    grads = {
        "attn_norm": gs[0:1], "attn_w_qkv": gf_qkv, "attn_w_o": gf_o, "attn_sink": gs[1:2, :N_HEADS],
        "conv_norm": take(2, 1, 256), "conv_w_pw1": gf_pw1,
        "conv_b_pw1": lax.dynamic_slice(gs[3:5].reshape(1, 2 * D), (0, chip * 512), (1, 512)),
        "conv_w_dw": take(16, 32, 256)[None, :CONV_W], "conv_b_dw": take(5, 1, 256), "conv_ln_g": take(6, 1, 256),
        "conv_ln_b": take(7, 1, 256), "conv_w_pw2": gf_pw2, "conv_b_pw2": take(8, 1, 256),
        "ffn_norm": gs[9:11], "ffn_w_gu": gf_gu, "ffn_w_down": gf_down, "final_norm": gs[11],
    }
    weights = dict(attn_norm=attn_norm, attn_w_qkv=attn_w_qkv, attn_w_o=attn_w_o, attn_sink=attn_sink,
                   conv_norm=conv_norm, conv_w_pw1=conv_w_pw1, conv_b_pw1=conv_b_pw1, conv_w_dw=conv_w_dw,
                   conv_b_dw=conv_b_dw, conv_ln_g=conv_ln_g, conv_ln_b=conv_ln_b, conv_w_pw2=conv_w_pw2,
                   conv_b_pw2=conv_b_pw2, ffn_norm=ffn_norm, ffn_w_gu=ffn_w_gu, ffn_w_down=ffn_w_down,
                   final_norm=final_norm)
    m_in = dict(attn_norm=m_attn_norm, attn_w_qkv=m_attn_w_qkv, attn_w_o=m_attn_w_o, attn_sink=m_attn_sink,
                conv_norm=m_conv_norm, conv_w_pw1=m_conv_w_pw1, conv_b_pw1=m_conv_b_pw1, conv_w_dw=m_conv_w_dw,
                conv_b_dw=m_conv_b_dw, conv_ln_g=m_conv_ln_g, conv_ln_b=m_conv_ln_b, conv_w_pw2=m_conv_w_pw2,
                conv_b_pw2=m_conv_b_pw2, ffn_norm=m_ffn_norm, ffn_w_gu=m_ffn_w_gu, ffn_w_down=m_ffn_w_down,
                final_norm=m_final_norm)
    v_in = dict(attn_norm=v_attn_norm, attn_w_qkv=v_attn_w_qkv, attn_w_o=v_attn_w_o, attn_sink=v_attn_sink,
                conv_norm=v_conv_norm, conv_w_pw1=v_conv_w_pw1, conv_b_pw1=v_conv_b_pw1, conv_w_dw=v_conv_w_dw,
                conv_b_dw=v_conv_b_dw, conv_ln_g=v_conv_ln_g, conv_ln_b=v_conv_ln_b, conv_w_pw2=v_conv_w_pw2,
                conv_b_pw2=v_conv_b_pw2, ffn_norm=v_ffn_norm, ffn_w_gu=v_ffn_w_gu, ffn_w_down=v_ffn_w_down,
                final_norm=v_final_norm)
    order = list(weights)
    g_out, d_out, m_out, v_out = [], [], [], []
    for nm in order:
        w = weights[nm]
        shape = w.shape
        as3 = lambda t: t.reshape((1,) * (3 - len(shape)) + shape) if len(shape) < 3 else t.reshape(shape)
        g3 = as3(grads[nm].reshape(shape))
        delta, nm_, nv_ = adamw(as3(w), g3, as3(m_in[nm]), as3(v_in[nm]), "adamw_" + nm)
        g_out.append(g3.reshape(shape))
        d_out.append(delta.reshape(shape))
        m_out.append(nm_.reshape(shape))
        v_out.append(nv_.reshape(shape))
    return (loss, dx0[None], *g_out, *d_out, *m_out, *v_out)
```

```python
import math

import jax
import jax.numpy as jnp
from jax import lax
from jax.experimental import pallas as pl
from jax.experimental.pallas import tpu as pltpu

F32 = jnp.float32
BF16 = jnp.bfloat16

D = 1024
N_HEADS = 16
N_KV = 4
GROUP = N_HEADS // N_KV
HD = 64
ROT = 16
THETA = 500000.0
BLK = 128
QKV = (N_HEADS + 2 * N_KV) * HD
KV_OFF = N_HEADS * HD
DFF = 2816
CONV_W = 31
CONV_PAD = 15
HALO = 16
CONV_JB = 16
CONV_JB_BWD = 8
EPS = 1e-6
NEG = -1e30
N_CHIPS = 4
N_DEV = 8
LANES = 128
SUBLANES = 8

ADAM_LR, ADAM_B1, ADAM_B2, ADAM_EPS, ADAM_WD, ADAM_STEP = 0.001, 0.9, 0.999, 1e-08, 0.01, 10

VMEM_LIMIT = 56 * 1024 * 1024
MESH = pl.DeviceIdType.MESH


def _params(*sem):
    return pltpu.CompilerParams(dimension_semantics=sem, vmem_limit_bytes=VMEM_LIMIT)


def _tile(n, want):
    if n <= want:
        return n
    for t in range(want, 7, -1):
        if n % t == 0 and t % 8 == 0:
            return t
    return n


MXU_COLS = 256


def _col_chunks(n):
    return [slice(c, min(c + MXU_COLS, n)) for c in range(0, n, MXU_COLS)]


def _sigmoid(v):
    return jax.nn.sigmoid(v)


def _rms_fwd(xv, gain):
    r = lax.rsqrt(jnp.mean(xv * xv, axis=-1, keepdims=True) + EPS)
    return xv * r * gain


def _rms_bwd(dh, xv, gain, dres):
    r = lax.rsqrt(jnp.mean(xv * xv, axis=-1, keepdims=True) + EPS)
    xhat = xv * r
    gy = dh * gain
    dx = r * (gy - xhat * jnp.mean(gy * xhat, axis=-1, keepdims=True))
    return dx + dres, dh * xhat


def _rope(blk, c, s1, s2):
    return blk * c + pltpu.roll(blk, LANES - ROT // 2, 1) * s1 + pltpu.roll(blk, ROT // 2, 1) * s2


def _dot(a, b):
    return jnp.dot(a, b, preferred_element_type=F32)


def _dot_tb(a, b):
    return lax.dot_general(a, b, (((1,), (1,)), ((), ())), preferred_element_type=F32)


def _dot_ta(a, b):
    return lax.dot_general(a, b, (((0,), (0,)), ((), ())), preferred_element_type=F32)


def rms_first(x, gain, comm):
    T = x.shape[0]
    tm = _tile(T, 512)

    def body(x_ref, g_ref, h_ref):
        h_ref[...] = _rms_fwd(x_ref[...], g_ref[...]).astype(BF16)

    (h,), got = _call(
        body, name="rms_first", grid=(T // tm,),
        in_specs=[pl.BlockSpec((tm, D), lambda i: (i, 0)), pl.BlockSpec((1, D), lambda i: (0, 0))],
        out_specs=[pl.BlockSpec((tm, D), lambda i: (i, 0))], out_shape=[jax.ShapeDtypeStruct((T, D), BF16)],
        semantics=("parallel",), args=(x, gain), comm=comm)
    return h, got


def qkv_proj(h, w, rc, rs1, rs2, comm=None):
    T = h.shape[0]
    tm = _tile(T, 512)

    def body(h_ref, w_ref, c_ref, s1_ref, s2_ref, qkv_ref):
        acc = _dot(h_ref[...], w_ref[...])
        c, s1, s2 = c_ref[...], s1_ref[...], s2_ref[...]
        n_rot = (KV_OFF + N_KV * HD) // LANES
        for j in range(n_rot):
            sl = slice(LANES * j, LANES * (j + 1))
            roped = _rope(acc[:, sl], c, s1, s2)
            if j < KV_OFF // LANES:
                roped = roped * Q_SCALE
            qkv_ref[:, sl] = roped.astype(BF16)
        qkv_ref[:, n_rot * LANES:] = acc[:, n_rot * LANES:].astype(BF16)

    row = lambda i: (i, 0)
    full = lambda i: (0, 0)
    (qkv,), got = _call(
        body, name="qkv_proj", grid=(T // tm,),
        in_specs=[pl.BlockSpec((tm, D), row), pl.BlockSpec((D, QKV), full), *_tab_specs(tm)],
        out_specs=[pl.BlockSpec((tm, QKV), row)],
        out_shape=[jax.ShapeDtypeStruct((T, QKV), BF16)],
        semantics=("parallel",), args=(h, w, rc, rs1, rs2), comm=comm)
    return qkv, got


Q_SCALE = 1.0 / math.sqrt(HD)


def _attn_mask(n, T):
    ci = lax.broadcasted_iota(jnp.int32, (3 * BLK, BLK), 0)
    qi = lax.broadcasted_iota(jnp.int32, (3 * BLK, BLK), 1)
    key_pos = n * BLK - BLK + ci
    return (jnp.abs(ci - BLK - qi) <= BLK) & (key_pos >= 0) & (key_pos < T)


def _kv_padded(kv, first_tile):
    low = lax.broadcasted_iota(jnp.int32, (3 * BLK, LANES), 1) < HD
    zero = jnp.zeros((3 * BLK, LANES), BF16)
    out = {}
    for g in range(N_KV):
        t = kv[:, (first_tile + g // 2) * LANES:(first_tile + g // 2 + 1) * LANES]
        swapped = jnp.concatenate([t[:, HD:], t[:, :HD]], axis=1)
        for p in range(2):
            out[g, p] = jnp.where(low if p == 0 else ~low, t if g % 2 == p else swapped, zero)
    return out


def _pair_products(kvx, tile_of):
    both = {g: jnp.concatenate([kvx[g, 0], kvx[g, 1]], axis=0) for g in range(N_KV)}
    out = []
    for j in range(N_HEADS // 2):
        prod = _dot_tb(both[2 * j // GROUP], tile_of(j))
        out += [prod[:3 * BLK], prod[3 * BLK:]]
    return out


def _softmax_sink(s, valid, sk):
    s = jnp.where(valid, s, NEG)
    m = jnp.maximum(jnp.max(s, axis=0, keepdims=True), sk)
    e = jnp.exp(s - m)
    es = jnp.exp(sk - m)
    inv = 1.0 / (jnp.sum(e, axis=0, keepdims=True) + es)
    return e * inv, es * inv


def _attn_specs(T):
    nb = T // BLK
    kv_blk = 2 * N_KV * HD
    kv_col = KV_OFF // kv_blk
    q_spec = pl.BlockSpec((BLK, KV_OFF), lambda n: (n, 0))
    prev = pl.BlockSpec((BLK, kv_blk), lambda n: (jnp.maximum(n - 1, 0), kv_col))
    own = pl.BlockSpec((BLK, kv_blk), lambda n: (n, kv_col))
    nxt = pl.BlockSpec((BLK, kv_blk), lambda n: (jnp.minimum(n + 1, nb - 1), kv_col))
    return nb, q_spec, prev, own, nxt


def attn_fwd(qkv, sink, comm=None):
    T = qkv.shape[0]
    nb, q_spec, prev, own, nxt = _attn_specs(T)

    def body(sink_ref, q_ref, kp_ref, ko_ref, kn_ref, o_ref):
        valid = _attn_mask(pl.program_id(0), T)
        kv = jnp.concatenate([kp_ref[...], ko_ref[...], kn_ref[...]], axis=0)
        kx, vx = _kv_padded(kv, 0), _kv_padded(kv, 2)
        ss = _pair_products(kx, lambda j: q_ref[:, j * LANES:(j + 1) * LANES])
        ps = [_softmax_sink(ss[h], valid, sink_ref[h])[0].astype(BF16) for h in range(N_HEADS)]
        vxt = {k: v.T for k, v in vx.items()}
        for j in range(N_HEADS // 2):
            g = 2 * j // GROUP
            o_t = _dot(vxt[g, 0], ps[2 * j]) + _dot(vxt[g, 1], ps[2 * j + 1])
            o_ref[:, j * LANES:(j + 1) * LANES] = o_t.T.astype(BF16)

    (o,), got = _call(
        body, name="attn_fwd", grid=(nb,),
        in_specs=[pl.BlockSpec(memory_space=pltpu.SMEM), q_spec, prev, own, nxt],
        out_specs=[pl.BlockSpec((BLK, D), lambda n: (n, 0))],
        out_shape=[jax.ShapeDtypeStruct((T, D), BF16)],
        semantics=("parallel",), args=(sink, qkv, qkv, qkv, qkv), comm=comm)
    return o, got


def rms_mm_gate(x, gain, w, bias, H, swiglu, act_dtype, name, comm=None):
    fused = isinstance(x, tuple)
    T = (x[0] if fused else x).shape[0]
    tm = _tile(T, 512)

    def body(*refs):
        if fused:
            a_ref, wp_ref, bp_ref, r_ref, g_ref, w_ref, b_ref, x_ref, h_ref, pre_ref, act_ref = refs
            xv = _dot(a_ref[...], wp_ref[...]) + bp_ref[...] + r_ref[...]
            x_ref[...] = xv
        else:
            x_ref, g_ref, w_ref, b_ref, h_ref, pre_ref, act_ref = refs
            xv = x_ref[...]
        h = _rms_fwd(xv, g_ref[...]).astype(BF16)
        h_ref[...] = h
        for cs in _col_chunks(H):
            cs2 = slice(H + cs.start, H + cs.stop)
            a = _dot(h, w_ref[:, cs]) + b_ref[:, cs]
            b = _dot(h, w_ref[:, cs2]) + b_ref[:, cs2]
            pre_ref[0, :, cs] = a.astype(BF16)
            pre_ref[1, :, cs] = b.astype(BF16)
            if swiglu:
                act = a * _sigmoid(a) * b
            else:
                act = a * _sigmoid(b)
            act_ref[:, cs] = act.astype(act_dtype)

    row = lambda i: (i, 0)
    full = lambda i: (0, 0)
    if fused:
        K = x[0].shape[1]
        x_specs = [pl.BlockSpec((tm, K), row), pl.BlockSpec((K, D), full, pipeline_mode=pl.Buffered(1)),
                   pl.BlockSpec((1, D), full), pl.BlockSpec((tm, D), row)]
        x_out = ([pl.BlockSpec((tm, D), row)], [jax.ShapeDtypeStruct((T, D), F32)])
        x_args = tuple(x)
    else:
        x_specs, x_out, x_args = [pl.BlockSpec((tm, D), row)], ([], []), (x,)
    outs, got = _call(
        body, name=name, grid=(T // tm,),
        in_specs=x_specs + [pl.BlockSpec((1, D), full),
                            pl.BlockSpec((D, 2 * H), full, pipeline_mode=pl.Buffered(1)), pl.BlockSpec((1, 2 * H), full)],
        out_specs=x_out[0] + [pl.BlockSpec((tm, D), row), pl.BlockSpec((2, tm, H), lambda i: (0, i, 0)),
                              pl.BlockSpec((tm, H), row)],
        out_shape=x_out[1] + [jax.ShapeDtypeStruct((T, D), BF16), jax.ShapeDtypeStruct((2, T, H), BF16),
                              jax.ShapeDtypeStruct((T, H), act_dtype)],
        semantics=("parallel",), args=x_args + (gain, w, bias), comm=comm)
    return (*outs, got)


def _conv_tiles(T):
    tt = _tile(T, 512)
    return tt, tt // SUBLANES, D // LANES


def _fill_strided(ext, p, L):
    main = p[HALO:HALO + SUBLANES * L, :].reshape(SUBLANES, L, LANES)
    ext[CONV_PAD:CONV_PAD + L] = jnp.swapaxes(main, 0, 1)

    def ibody(i, carry):
        ext[i] = p[pl.ds(i + 1, SUBLANES, stride=L), :]
        ext[i + CONV_PAD + L] = p[pl.ds(i + CONV_PAD + L + 1, SUBLANES, stride=L), :]
        return carry

    lax.fori_loop(0, CONV_PAD, ibody, 0, unroll=3)


def _conv_specs(T, tt):
    main = pl.BlockSpec((tt, D), lambda i: (i, 0))
    per = tt // HALO
    prev = pl.BlockSpec((HALO, D), lambda i: (jnp.maximum(i * per - 1, 0), 0))
    nxt = pl.BlockSpec((HALO, D), lambda i: (jnp.minimum((i + 1) * per, T // HALO - 1), 0))
    return main, prev, nxt


def _fill_pad(pad, main_ref, prev_ref, next_ref, i, n_i, tt, nlt):
    keep_p = (i > 0).astype(F32)
    keep_n = (i < n_i - 1).astype(F32)
    for lt in range(nlt):
        sl = slice(lt * LANES, (lt + 1) * LANES)
        pad[lt, 0:HALO, :] = prev_ref[:, sl] * keep_p
        pad[lt, HALO:HALO + tt, :] = main_ref[:, sl]
        pad[lt, HALO + tt:2 * HALO + tt, :] = next_ref[:, sl] * keep_n


def conv_fwd(glu, w_dw, b_dw, ln_g, ln_b, comm=None):
    T = glu.shape[0]
    tt, L, nlt = _conv_tiles(T)
    n_i = T // tt
    main, prev, nxt = _conv_specs(T, tt)

    def body(x_ref, xp_ref, xn_ref, w_ref, b_ref, g_ref, bb_ref, dwc_ref, sw_ref, pad, ob, ext, wk):
        i = pl.program_id(0)
        _fill_pad(pad, x_ref, xp_ref, xn_ref, i, n_i, tt, nlt)
        for lt in range(nlt):
            sl = slice(lt * LANES, (lt + 1) * LANES)
            o = ob.at[lt]
            _fill_strided(ext, pad.at[lt], L)
            for k in range(CONV_W):
                wk[k] = jnp.broadcast_to(w_ref[k:k + 1, sl], (SUBLANES, LANES))

            def jbody(jb, carry):
                j = jb * CONV_JB
                accs = [None] * CONV_JB
                for m in range(CONV_W + CONV_JB - 1):
                    e = ext[j + m]
                    for u in range(CONV_JB):
                        if 0 <= m - u < CONV_W:
                            t = e * wk[m - u]
                            accs[u] = t if accs[u] is None else accs[u] + t
                for u in range(CONV_JB):
                    o[pl.ds(j + u, SUBLANES, stride=L), :] = accs[u]
                return carry

            lax.fori_loop(0, L // CONV_JB, jbody, 0)
        y = jnp.concatenate([ob[lt] for lt in range(nlt)], axis=1) + b_ref[...]
        dwc_ref[...] = y
        mu = jnp.mean(y, axis=-1, keepdims=True)
        yc = y - mu
        var = jnp.mean(yc * yc, axis=-1, keepdims=True)
        z = yc * lax.rsqrt(var + EPS) * g_ref[...] + bb_ref[...]
        sw_ref[...] = (z * _sigmoid(z)).astype(BF16)

    full = lambda i: (0, 0)
    (dwc, sw), got = _call(
        body, name="conv_fwd", grid=(n_i,),
        in_specs=[main, prev, nxt, pl.BlockSpec((32, D), full), pl.BlockSpec((1, D), full),
                  pl.BlockSpec((1, D), full), pl.BlockSpec((1, D), full)],
        out_specs=[pl.BlockSpec((tt, D), lambda i: (i, 0)), pl.BlockSpec((tt, D), lambda i: (i, 0))],
        out_shape=[jax.ShapeDtypeStruct((T, D), F32), jax.ShapeDtypeStruct((T, D), BF16)],
        scratch_shapes=[pltpu.VMEM((nlt, tt + 2 * HALO, LANES), F32), pltpu.VMEM((nlt, tt, LANES), F32),
                        pltpu.VMEM((L + 2 * HALO, SUBLANES, LANES), F32), pltpu.VMEM((32, SUBLANES, LANES), F32)],
        semantics=("parallel",), args=(glu, glu, glu, w_dw, b_dw, ln_g, ln_b), comm=comm)
    return dwc, sw, got


def mm_res_loss(a, w, resid, gain, target):
    T, K = a.shape
    tm = _tile(T, 512)

    def body(a_ref, w_ref, r_ref, g_ref, t_ref, dx_ref, loss_ref, dg_ref):
        @pl.when(pl.program_id(0) == 0)
        def _():
            loss_ref[...] = jnp.zeros_like(loss_ref)
            dg_ref[...] = jnp.zeros_like(dg_ref)

        xv, gain_v = _dot(a_ref[...], w_ref[...]) + r_ref[...], g_ref[...]
        err = _rms_fwd(xv, gain_v) - t_ref[...]
        part = 0.5 * jnp.sum(jnp.mean(err * err, axis=-1, keepdims=True), axis=0, keepdims=True)
        loss_ref[...] += jnp.broadcast_to(part, loss_ref.shape)
        dx, dgr = _rms_bwd(err * (1.0 / D), xv, gain_v, 0.0)
        dx_ref[...] = dx
        dg_ref[...] += jnp.sum(dgr, axis=0, keepdims=True)

    row = lambda i: (i, 0)
    full = lambda i: (0, 0)
    return pl.pallas_call(
        body, name="ffn1_down_loss", grid=(T // tm,),
        in_specs=[pl.BlockSpec((tm, K), row), pl.BlockSpec((K, D), full), pl.BlockSpec((tm, D), row),
                  pl.BlockSpec((1, D), full), pl.BlockSpec((tm, D), row)],
        out_specs=[pl.BlockSpec((tm, D), row), pl.BlockSpec((1, LANES), full), pl.BlockSpec((1, D), full)],
        out_shape=[jax.ShapeDtypeStruct((T, D), F32), jax.ShapeDtypeStruct((1, LANES), F32),
                   jax.ShapeDtypeStruct((1, D), F32)],
        compiler_params=_params("arbitrary"),
    )(a, w, resid, gain, target)


def swiglu_bwd(dx, w_down, pre, name, comm=None):
    T = dx.shape[0]
    H = w_down.shape[0]
    tm = _tile(T, 512)

    def body(dx_ref, w_ref, pre_ref, dpre_ref):
        dxb = dx_ref[...].astype(BF16)
        for cs in _col_chunks(H):
            dact = _dot_tb(dxb, w_ref[cs, :])
            g = pre_ref[0, :, cs].astype(F32)
            u = pre_ref[1, :, cs].astype(F32)
            sg = _sigmoid(g)
            dpre_ref[0, :, cs] = (dact * u * sg * (1.0 + g * (1.0 - sg))).astype(BF16)
            dpre_ref[1, :, cs] = (dact * g * sg).astype(BF16)

    (dpre,), got = _call(
        body, name=name, grid=(T // tm,),
        in_specs=[pl.BlockSpec((tm, D), lambda i: (i, 0)),
                  pl.BlockSpec((H, D), lambda i: (0, 0), pipeline_mode=pl.Buffered(1)),
                  pl.BlockSpec((2, tm, H), lambda i: (0, i, 0))],
        out_specs=[pl.BlockSpec((2, tm, H), lambda i: (0, i, 0))],
        out_shape=[jax.ShapeDtypeStruct((2, T, H), BF16)],
        semantics=("parallel",), args=(dx, w_down, pre), comm=comm)
    return dpre, got


def _ln_silu_bwd(dsw, y, ln_g, ln_b):
    mu = jnp.mean(y, axis=-1, keepdims=True)
    yc = y - mu
    rstd = lax.rsqrt(jnp.mean(yc * yc, axis=-1, keepdims=True) + EPS)
    xhat = yc * rstd
    z = xhat * ln_g + ln_b
    sg = _sigmoid(z)
    dz = dsw * sg * (1.0 + z * (1.0 - sg))
    dxh = dz * ln_g
    dy = rstd * (dxh - jnp.mean(dxh, axis=-1, keepdims=True) - xhat * jnp.mean(dxh * xhat, axis=-1, keepdims=True))
    return dy, dz * xhat, dz


def mm_bt_rmsbwd(dpre, w, x, gain, dres, name, comm=None, proj_w=None, conv_tail=None):
    nh, T, H = dpre.shape
    tm = _tile(T, 512)
    n_extra_in = 1 if proj_w is not None else (4 if conv_tail is not None else 0)

    def body(*refs):
        dp_ref, w_ref, x_ref, g_ref, dres_ref = refs[:5]
        extra_in = refs[5:5 + n_extra_in]
        dx_ref, dg_ref = refs[5 + n_extra_in:7 + n_extra_in]
        extra_out = refs[7 + n_extra_in:]

        @pl.when(pl.program_id(0) == 0)
        def _():
            dg_ref[...] = jnp.zeros_like(dg_ref)
            for r in extra_out[1:]:
                r[...] = jnp.zeros_like(r)

        dh = _dot_tb(dp_ref[0], w_ref[:, 0:H])
        for hf in range(1, nh):
            dh = dh + _dot_tb(dp_ref[hf], w_ref[:, hf * H:(hf + 1) * H])
        dx, dgr = _rms_bwd(dh, x_ref[...], g_ref[...], dres_ref[...])
        dx_ref[...] = dx
        dg_ref[...] += jnp.sum(dgr, axis=0, keepdims=True)
        if proj_w is not None:
            extra_out[0][...] = _dot_tb(dx.astype(BF16), extra_in[0][...]).astype(BF16)
        elif conv_tail is not None:
            wt_ref, y_ref, lg_ref, lb_ref = extra_in
            dy, dgl, dbl = _ln_silu_bwd(_dot_tb(dx.astype(BF16), wt_ref[...]), y_ref[...], lg_ref[...], lb_ref[...])
            extra_out[0][...] = dy
            extra_out[1][...] += jnp.sum(dgl, axis=0, keepdims=True)
            extra_out[2][...] += jnp.sum(dbl, axis=0, keepdims=True)
            extra_out[3][...] += jnp.sum(dx, axis=0, keepdims=True)

    row = lambda i: (i, 0)
    full = lambda i: (0, 0)
    vec = pl.BlockSpec((1, D), full)
    vec_shape = jax.ShapeDtypeStruct((1, D), F32)
    in_specs = [pl.BlockSpec((nh, tm, H), lambda i: (0, i, 0)),
                pl.BlockSpec((D, nh * H), full, pipeline_mode=pl.Buffered(1)),
                pl.BlockSpec((tm, D), row), vec, pl.BlockSpec((tm, D), row)]
    out_specs = [pl.BlockSpec((tm, D), row), vec]
    out_shape = [jax.ShapeDtypeStruct((T, D), F32), vec_shape]
    args = (dpre, w, x, gain, dres)
    if proj_w is not None:
        N = proj_w.shape[0]
        in_specs.append(pl.BlockSpec((N, D), full, pipeline_mode=pl.Buffered(1)))
        out_specs.append(pl.BlockSpec((tm, N), row))
        out_shape.append(jax.ShapeDtypeStruct((T, N), BF16))
        args += (proj_w,)
    elif conv_tail is not None:
        in_specs += [pl.BlockSpec((D, D), full, pipeline_mode=pl.Buffered(1)), pl.BlockSpec((tm, D), row), vec, vec]
        out_specs += [pl.BlockSpec((tm, D), row), vec, vec, vec]
        out_shape += [jax.ShapeDtypeStruct((T, D), F32), vec_shape, vec_shape, vec_shape]
        args += tuple(conv_tail)
    outs, got = _call(body, name=name, grid=(T // tm,), in_specs=in_specs, out_specs=out_specs, out_shape=out_shape,
                      semantics=("arbitrary",), args=args, comm=comm)
    return (*outs, got)


def dw_col(a, dpre, name):
    T = a.shape[0]
    nh, _, H = dpre.shape
    per = nh * H // N_CHIPS
    bph = N_CHIPS // nh
    tt = _tile(T, 2048)
    nt = T // tt

    def body(a_ref, b_ref, o_ref, acc):
        t = pl.program_id(1)

        @pl.when(t == 0)
        def _():
            acc[...] = jnp.zeros_like(acc)

        acc[...] += _dot_ta(a_ref[...], b_ref[...])

        @pl.when(t == nt - 1)
        def _():
            o_ref[...] = acc[...].astype(BF16)

    return pl.pallas_call(
        body, name=name, grid=(N_CHIPS, nt),
        in_specs=[pl.BlockSpec((tt, D), lambda q, t: (t, 0)),
                  pl.BlockSpec((None, tt, per), lambda q, t: (q // bph, t, q % bph))],
        out_specs=pl.BlockSpec((None, D, per), lambda q, t: (q, 0, 0)),
        out_shape=jax.ShapeDtypeStruct((N_CHIPS, D, per), BF16),
        scratch_shapes=[pltpu.VMEM((D, per), F32)],
        compiler_params=_params("parallel", "arbitrary"),
    )(a, dpre)


def dw_row(a, b, name):
    T, R = a.shape
    cw = 1408 if R % 1408 == 0 else R
    tt = _tile(T, 1024)
    nt = T // tt

    def body(a_ref, b_ref, o_ref, acc):
        t = pl.program_id(1)

        @pl.when(t == 0)
        def _():
            acc[...] = jnp.zeros_like(acc)

        acc[...] += _dot_ta(a_ref[...], b_ref[...].astype(BF16))

        @pl.when(t == nt - 1)
        def _():
            o_ref[...] = acc[...].astype(BF16)

    out = pl.pallas_call(
        body, name=name, grid=(R // cw, nt),
        in_specs=[pl.BlockSpec((tt, cw), lambda q, t: (t, q)), pl.BlockSpec((tt, D), lambda q, t: (t, 0))],
        out_specs=pl.BlockSpec((cw, D), lambda q, t: (q, 0)),
        out_shape=jax.ShapeDtypeStruct((R, D), BF16),
        scratch_shapes=[pltpu.VMEM((cw, D), F32)],
        compiler_params=_params("parallel", "arbitrary"),
    )(a, b)
    return out.reshape(N_CHIPS, R // N_CHIPS, D)


def conv_bwd(ddwc, glu, pre, w_dw, comm=None):
    T = ddwc.shape[0]
    tt, L, nlt = _conv_tiles(T)
    n_i = T // tt
    main, prev, nxt = _conv_specs(T, tt)

    def body(d_ref, dp_ref, dn_ref, x_ref, xp_ref, xn_ref, pre_ref, w_ref,
             dpre_ref, dw_ref, dbd_ref, dbp_ref, padd, padx, ob, extd, extx, wk):
        i = pl.program_id(0)

        @pl.when(i == 0)
        def _():
            dw_ref[...] = jnp.zeros_like(dw_ref)
            dbd_ref[...] = jnp.zeros_like(dbd_ref)
            dbp_ref[...] = jnp.zeros_like(dbp_ref)

        _fill_pad(padd, d_ref, dp_ref, dn_ref, i, n_i, tt, nlt)
        _fill_pad(padx, x_ref, xp_ref, xn_ref, i, n_i, tt, nlt)
        for lt in range(nlt):
            sl = slice(lt * LANES, (lt + 1) * LANES)
            o = ob.at[lt]
            _fill_strided(extd, padd.at[lt], L)
            _fill_strided(extx, padx.at[lt], L)
            for k in range(CONV_W):
                wk[k] = jnp.broadcast_to(w_ref[k:k + 1, sl], (SUBLANES, LANES))

            nu = CONV_JB_BWD

            def jbody(jb, accs):
                j = jb * nu
                accs = list(accs)
                d = [extd[j + u + CONV_PAD] for u in range(nu)]
                g = [None] * nu
                for m in range(CONV_W + nu - 1):
                    ed = extd[j + 2 * CONV_PAD + nu - 1 - m]
                    ex = extx[j + m]
                    for u in range(nu):
                        k = m - (nu - 1 - u)
                        if 0 <= k < CONV_W:
                            t = ed * wk[k]
                            g[u] = t if g[u] is None else g[u] + t
                        k = m - u
                        if 0 <= k < CONV_W:
                            accs[k] = accs[k] + d[u] * ex
                for u in range(nu):
                    o[pl.ds(j + u, SUBLANES, stride=L), :] = g[u]
                return tuple(accs)

            accs = lax.fori_loop(0, L // nu, jbody, tuple(jnp.zeros((SUBLANES, LANES), F32) for _ in range(CONV_W)))
            for k in range(CONV_W):
                dw_ref[k:k + 1, sl] += jnp.sum(accs[k], axis=0, keepdims=True)
        dglu = jnp.concatenate([ob[lt] for lt in range(nlt)], axis=1)
        a = pre_ref[0].astype(F32)
        gate = pre_ref[1].astype(F32)
        sg = _sigmoid(gate)
        da = dglu * sg
        dgate = dglu * a * sg * (1.0 - sg)
        dpre_ref[0] = da.astype(BF16)
        dpre_ref[1] = dgate.astype(BF16)
        dbd_ref[...] += jnp.sum(d_ref[...], axis=0, keepdims=True)
        dbp_ref[0] += jnp.sum(da, axis=0, keepdims=True)
        dbp_ref[1] += jnp.sum(dgate, axis=0, keepdims=True)

    full = lambda i: (0, 0)
    (dpre, dw, dbd, dbp), got = _call(
        body, name="conv_bwd", grid=(n_i,),
        in_specs=[main, prev, nxt, main, prev, nxt, pl.BlockSpec((2, tt, D), lambda i: (0, i, 0)),
                  pl.BlockSpec((32, D), full)],
        out_specs=[pl.BlockSpec((2, tt, D), lambda i: (0, i, 0)), pl.BlockSpec((32, D), full),
                   pl.BlockSpec((1, D), full), pl.BlockSpec((2, 1, D), lambda i: (0, 0, 0))],
        out_shape=[jax.ShapeDtypeStruct((2, T, D), BF16), jax.ShapeDtypeStruct((32, D), F32),
                   jax.ShapeDtypeStruct((1, D), F32), jax.ShapeDtypeStruct((2, 1, D), F32)],
        scratch_shapes=[pltpu.VMEM((nlt, tt + 2 * HALO, LANES), F32), pltpu.VMEM((nlt, tt + 2 * HALO, LANES), F32),
                        pltpu.VMEM((nlt, tt, LANES), F32), pltpu.VMEM((L + 2 * HALO, SUBLANES, LANES), F32),
                        pltpu.VMEM((L + 2 * HALO, SUBLANES, LANES), F32), pltpu.VMEM((32, SUBLANES, LANES), F32)],
        semantics=("arbitrary",), args=(ddwc, ddwc, ddwc, glu, glu, glu, pre, w_dw), comm=comm)
    return dpre, dw, dbd, dbp, got


def attn_bwd(qkv, o, do, sink, rc, rs1, rs2, comm=None):
    T = qkv.shape[0]
    nb, q_spec, prev, own, nxt = _attn_specs(T)
    kvw = N_KV * HD

    def body(sink_ref, q_ref, kp_ref, ko_ref, kn_ref, o_ref, do_ref, c_ref, s1_ref, s2_ref,
             dq_ref, dkc_ref, dvc_ref, dsink_ref):
        n = pl.program_id(0)

        @pl.when(n == 0)
        def _():
            dsink_ref[...] = jnp.zeros_like(dsink_ref)

        valid = _attn_mask(n, T)
        kv = jnp.concatenate([kp_ref[...], ko_ref[...], kn_ref[...]], axis=0)
        kx, vx = _kv_padded(kv, 0), _kv_padded(kv, 2)
        tile = lambda ref, j: ref[:, j * LANES:(j + 1) * LANES]
        ss = _pair_products(kx, lambda j: tile(q_ref, j))
        dps = _pair_products(vx, lambda j: tile(do_ref, j))
        low_d = lax.broadcasted_iota(jnp.int32, (LANES, BLK), 0) < HD
        deltas = []
        for j in range(N_HEADS // 2):
            prod_t = tile(do_ref, j).astype(F32).T * tile(o_ref, j).astype(F32).T
            deltas.append(jnp.sum(jnp.where(low_d, prod_t, 0.0), axis=0, keepdims=True))
            deltas.append(jnp.sum(jnp.where(low_d, 0.0, prod_t), axis=0, keepdims=True))
        lane = lax.broadcasted_iota(jnp.int32, (1, N_HEADS), 1)
        dsink = jnp.zeros((1, N_HEADS), F32)
        pbs, dss = [], []
        for h in range(N_HEADS):
            p, p_sink = _softmax_sink(ss[h], valid, sink_ref[h])
            dss.append((p * (dps[h] - deltas[h])).astype(BF16))
            pbs.append(p.astype(BF16))
            part = -jnp.sum(p_sink * deltas[h], axis=1, keepdims=True)
            dsink = dsink + jnp.where(lane == h, part, 0.0)
        dsink_ref[...] += dsink
        c, s1, s2 = c_ref[...], s1_ref[...], s2_ref[...]
        kxt = {k: v.T for k, v in kx.items()}
        for j in range(N_HEADS // 2):
            g = 2 * j // GROUP
            dq_t = _dot(kxt[g, 0], dss[2 * j]) + _dot(kxt[g, 1], dss[2 * j + 1])
            dq_ref[:, j * LANES:(j + 1) * LANES] = (_rope(dq_t.T, c, -s1, -s2) * Q_SCALE).astype(BF16)
        low_k = lax.broadcasted_iota(jnp.int32, (3 * BLK, LANES), 1) < HD
        cols = lambda xs, g, p: jnp.concatenate([xs[GROUP * g + p], xs[GROUP * g + 2 + p]], axis=1)
        for t in range(N_KV // 2):
            sums = {}
            for g in (2 * t, 2 * t + 1):
                q2 = jnp.concatenate([tile(q_ref, 2 * g), tile(q_ref, 2 * g + 1)], axis=0)
                do2 = jnp.concatenate([tile(do_ref, 2 * g), tile(do_ref, 2 * g + 1)], axis=0)
                dk2 = _dot(jnp.concatenate([cols(dss, g, 0), cols(dss, g, 1)], axis=0), q2)
                dv2 = _dot(jnp.concatenate([cols(pbs, g, 0), cols(pbs, g, 1)], axis=0), do2)
                for p in range(2):
                    sums[g, p] = (dk2[p * 3 * BLK:(p + 1) * 3 * BLK], dv2[p * 3 * BLK:(p + 1) * 3 * BLK])
            for which, ref in ((0, dkc_ref), (1, dvc_ref)):
                keep = jnp.where(low_k, sums[2 * t, 0][which], sums[2 * t + 1, 1][which])
                swap = jnp.where(low_k, sums[2 * t + 1, 0][which], sums[2 * t, 1][which])
                ref[:, t * LANES:(t + 1) * LANES] = keep + pltpu.roll(swap, HD, 1)

    row = lambda n: (n, 0)
    (dq, dkc, dvc, dsink), got = _call(
        body, name="attn_bwd", grid=(nb,),
        in_specs=[pl.BlockSpec(memory_space=pltpu.SMEM), q_spec, prev, own, nxt,
                  pl.BlockSpec((BLK, D), row), pl.BlockSpec((BLK, D), row), *_tab_specs(BLK)],
        out_specs=[pl.BlockSpec((BLK, D), row), pl.BlockSpec((None, 3 * BLK, kvw), lambda n: (n, 0, 0)),
                   pl.BlockSpec((None, 3 * BLK, kvw), lambda n: (n, 0, 0)), pl.BlockSpec((1, N_HEADS), lambda n: (0, 0))],
        out_shape=[jax.ShapeDtypeStruct((T, QKV), BF16), jax.ShapeDtypeStruct((nb, 3 * BLK, kvw), F32),
                   jax.ShapeDtypeStruct((nb, 3 * BLK, kvw), F32), jax.ShapeDtypeStruct((1, N_HEADS), F32)],
        semantics=("arbitrary",), args=(sink, qkv, qkv, qkv, qkv, o, do, rc, rs1, rs2), comm=comm)
    return dq, dkc, dvc, dsink, got


def kv_sum(dqkv, dkc, dvc, rc, rs1, rs2):
    nb = dkc.shape[0]
    T = nb * BLK
    kvw = N_KV * HD

    G = 4
    ng = nb // G

    def gather3(own_ref, prev_ref, before_ref, next_ref, after_ref, m):
        has_before = (m > 0).astype(F32)
        has_after = (m < ng - 1).astype(F32)
        out = []
        for i in range(G):
            from_prev = prev_ref[i - 1] if i > 0 else before_ref[0] * has_before
            from_next = next_ref[i + 1] if i < G - 1 else after_ref[0] * has_after
            out.append(from_prev + own_ref[i] + from_next)
        return jnp.concatenate(out, axis=0)

    def body(_, ko, kp, kb, kn, ka, vo, vp, vb, vn, va, c_ref, s1_ref, s2_ref, out_ref):
        m = pl.program_id(0)
        dk = gather3(ko, kp, kb, kn, ka, m)
        dv = gather3(vo, vp, vb, vn, va, m)
        c, s1, s2 = c_ref[...], s1_ref[...], s2_ref[...]
        for j in range(kvw // LANES):
            sl = slice(LANES * j, LANES * (j + 1))
            out_ref[:, sl] = _rope(dk[:, sl], c, -s1, -s2).astype(BF16)
        out_ref[:, kvw:] = dv.astype(BF16)

    own = pl.BlockSpec((G, BLK, kvw), lambda m: (m, 1, 0))
    prev = pl.BlockSpec((G, BLK, kvw), lambda m: (m, 2, 0))
    before = pl.BlockSpec((1, BLK, kvw), lambda m: (jnp.maximum(G * m - 1, 0), 2, 0))
    nxt = pl.BlockSpec((G, BLK, kvw), lambda m: (m, 0, 0))
    after = pl.BlockSpec((1, BLK, kvw), lambda m: (jnp.minimum(G * m + G, nb - 1), 0, 0))
    five = [own, prev, before, nxt, after]
    return pl.pallas_call(
        body, name="kv_sum", grid=(ng,),
        in_specs=[pl.BlockSpec(memory_space=pl.ANY), *five, *five, *_tab_specs(G * BLK)],
        out_specs=pl.BlockSpec((G * BLK, 2 * kvw), lambda m: (m, KV_OFF // (2 * kvw))),
        out_shape=jax.ShapeDtypeStruct((T, QKV), BF16),
        input_output_aliases={0: 0},
        compiler_params=_params("parallel"),
    )(dqkv, *([dkc] * 5), *([dvc] * 5), rc, rs1, rs2)


def _me():
    return lax.axis_index("x"), lax.axis_index("y"), lax.axis_index("c")


def _half_rows(ref, sharded_rows, chip, core):
    R, C = ref.shape[-2], ref.shape[-1]
    lead = (slice(None),) * (len(ref.shape) - 2)
    if sharded_rows:
        per = R // N_CHIPS
        return ref.at[lead + (pl.ds(chip * per + core * (per // 2), per // 2), slice(None))]
    per = C // N_CHIPS
    return ref.at[lead + (pl.ds(core * (R // 2), R // 2), pl.ds(chip * per, per))]


class _Gather:
    def __init__(self, shards, sharded_rows):
        self.inputs = list(shards)
        self.rows = list(sharded_rows)
        self.n = self.n_in = self.n_out = len(shards)
        self.out_shapes = []
        for s, rows in zip(shards, sharded_rows):
            shp = list(s.shape)
            shp[-2 if rows else -1] *= N_CHIPS
            self.out_shapes.append(jax.ShapeDtypeStruct(tuple(shp), s.dtype))
        self.scratch = [pltpu.SemaphoreType.DMA((self.n, 6)), pltpu.SemaphoreType.DMA((self.n, 6)),
                        pltpu.SemaphoreType.DMA((self.n, 2))]

    def _ctx(self, ins, outs, sems):
        send_sems, recv_sems, local_sems = sems
        x, y, c = _me()
        chips = [(1 - x, y), (x, 1 - y), (1 - x, 1 - y)]

        def half_src(w, core):
            s = ins[w]
            R = s.shape[-2]
            return s.at[pl.ds(core * (R // 2), R // 2), :]

        def dst(w, chip, core):
            return _half_rows(outs[w], self.rows[w], chip, core)

        def copy(w, k, src, chip, core, to):
            return pltpu.make_async_remote_copy(
                src_ref=src, dst_ref=dst(w, chip, core), send_sem=send_sems.at[w, k], recv_sem=recv_sems.at[w, k],
                device_id=to, device_id_type=MESH)

        def local(w, core):
            return pltpu.make_async_copy(half_src(w, core), dst(w, 2 * x + y, core), local_sems.at[w, core])

        def first(w, j):
            qx, qy = chips[j]
            return copy(w, j, half_src(w, c), 2 * x + y, c, (qx, qy, c))

        def landed(w, j):
            qx, qy = chips[j]
            return copy(w, j, dst(w, 2 * qx + qy, c), 2 * qx + qy, c, (x, y, c))

        def passed(w, j):
            qx, qy = chips[j]
            return copy(w, 3 + j, dst(w, 2 * qx + qy, c), 2 * qx + qy, c, (x, y, 1 - c))

        def from_sibling(w, j):
            qx, qy = chips[j]
            return copy(w, 3 + j, dst(w, 2 * qx + qy, 1 - c), 2 * qx + qy, 1 - c, (x, y, c))

        return local, first, landed, passed, from_sibling

    def start(self, ins, outs, sems):
        local, first, _, _, _ = self._ctx(ins, outs, sems)
        for w in range(self.n):
            for core in range(2):
                local(w, core).start()
            for j in range(3):
                first(w, j).start()

    def mid(self, ins, outs, sems):
        _, _, landed, passed, _ = self._ctx(ins, outs, sems)
        for w in range(self.n):
            for j in range(3):
                landed(w, j).wait_recv()
                passed(w, j).start()

    def end(self, ins, outs, sems):
        local, first, _, passed, from_sibling = self._ctx(ins, outs, sems)
        for w in range(self.n):
            for j in range(3):
                from_sibling(w, j).wait_recv()
        for w in range(self.n):
            for j in range(3):
                first(w, j).wait_send()
                passed(w, j).wait_send()
            for core in range(2):
                local(w, core).wait()


class _Scatter:
    def __init__(self, grads, small=None):
        self.inputs = list(grads) + ([small] if small is not None else [])
        self.ng = len(grads)
        self.n = self.n_in = self.n_out = len(self.inputs)
        self.out_shapes = [jax.ShapeDtypeStruct((N_DEV, g.shape[1] // 2, g.shape[2]), g.dtype) for g in grads]
        if small is not None:
            self.out_shapes.append(jax.ShapeDtypeStruct((N_DEV,) + small.shape, small.dtype))
        self.scratch = [pltpu.SemaphoreType.DMA((self.n, N_DEV)), pltpu.SemaphoreType.DMA((self.n, N_DEV)),
                        pltpu.SemaphoreType.DMA((self.n,))]

    def _ctx(self, ins, outs, sems):
        send_sems, recv_sems, local_sems = sems
        x, y, c = _me()
        me = 4 * x + 2 * y + c

        def piece(w, chip, core):
            if w >= self.ng:
                return ins[w]
            half = ins[w].shape[1] // 2
            return ins[w].at[chip, pl.ds(core * half, half), :]

        def peer_of(k):
            return x ^ ((k >> 2) & 1), y ^ ((k >> 1) & 1), c ^ (k & 1)

        def local(w):
            return pltpu.make_async_copy(piece(w, 2 * x + y, c), outs[w].at[me], local_sems.at[w])

        def send(w, k):
            px, py, pc = peer_of(k)
            return pltpu.make_async_remote_copy(
                src_ref=piece(w, 2 * px + py, pc), dst_ref=outs[w].at[me], send_sem=send_sems.at[w, k],
                recv_sem=recv_sems.at[w, k], device_id=(px, py, pc), device_id_type=MESH)

        def recv(w, k):
            px, py, pc = peer_of(k)
            return pltpu.make_async_remote_copy(
                src_ref=piece(w, 2 * x + y, c), dst_ref=outs[w].at[4 * px + 2 * py + pc], send_sem=send_sems.at[w, k],
                recv_sem=recv_sems.at[w, k], device_id=(px, py, pc), device_id_type=MESH)

        return local, send, recv

    def start(self, ins, outs, sems):
        local, send, _ = self._ctx(ins, outs, sems)
        for w in range(self.n):
            local(w).start()
            for k in range(1, N_DEV):
                send(w, k).start()

    def mid(self, ins, outs, sems):
        pass

    def end(self, ins, outs, sems):
        local, send, recv = self._ctx(ins, outs, sems)
        for w in range(self.n):
            for k in range(1, N_DEV):
                recv(w, k).wait_recv()
        for w in range(self.n):
            for k in range(1, N_DEV):
                send(w, k).wait_send()
            local(w).wait()


class _Both:
    def __init__(self, a, b):
        self.a, self.b = a, b
        self.inputs = a.inputs + b.inputs
        self.out_shapes = a.out_shapes + b.out_shapes
        self.scratch = a.scratch + b.scratch
        self.n_in, self.n_out = a.n_in + b.n_in, a.n_out + b.n_out

    def _split(self, ins, outs, sems):
        a, na = self.a, len(self.a.scratch)
        return (ins[:a.n_in], outs[:a.n_out], sems[:na]), (ins[a.n_in:], outs[a.n_out:], sems[na:])

    def start(self, ins, outs, sems):
        pa, pb = self._split(ins, outs, sems)
        self.a.start(*pa)
        self.b.start(*pb)

    def mid(self, ins, outs, sems):
        pa, pb = self._split(ins, outs, sems)
        self.a.mid(*pa)
        self.b.mid(*pb)

    def end(self, ins, outs, sems):
        pa, pb = self._split(ins, outs, sems)
        self.a.end(*pa)
        self.b.end(*pb)


def _call(body, *, name, grid, in_specs, out_specs, out_shape, scratch_shapes=(), semantics, args, comm=None):
    if comm is None:
        outs = pl.pallas_call(
            body, name=name, grid=grid, in_specs=in_specs, out_specs=out_specs, out_shape=out_shape,
            scratch_shapes=list(scratch_shapes), compiler_params=_params(*semantics))(*args)
        return outs, []
    n_in, n_out, n_scr = len(in_specs), len(out_specs), len(scratch_shapes)

    total = math.prod(grid)
    first, middle, last = 0, (3 * total) // 4 - 1, total - 1
    assert first <= middle < last

    def at(step):
        lin = pl.program_id(0)
        for d in range(1, len(grid)):
            lin = lin * grid[d] + pl.program_id(d)
        return lin == step

    def hosted(*refs):
        h_in, c_in = refs[:n_in], refs[n_in:n_in + comm.n_in]
        rest = refs[n_in + comm.n_in:]
        h_out, c_out = rest[:n_out], rest[n_out:n_out + comm.n_out]
        rest = rest[n_out + comm.n_out:]
        h_scr, c_scr = rest[:n_scr], rest[n_scr:]

        @pl.when(at(first))
        def _():
            comm.start(c_in, c_out, c_scr)

        body(*h_in, *h_out, *h_scr)

        @pl.when(at(middle))
        def _():
            comm.mid(c_in, c_out, c_scr)

        @pl.when(at(last))
        def _():
            comm.end(c_in, c_out, c_scr)

    any_spec = pl.BlockSpec(memory_space=pl.ANY)
    outs = pl.pallas_call(
        hosted, name=name, grid=grid, in_specs=list(in_specs) + [any_spec] * comm.n_in,
        out_specs=list(out_specs) + [any_spec] * comm.n_out, out_shape=list(out_shape) + comm.out_shapes,
        scratch_shapes=list(scratch_shapes) + comm.scratch,
        compiler_params=_params(*(["arbitrary"] * len(grid))))(*args, *comm.inputs)
    return outs[:n_out], outs[n_out:]


def sum_swap(pieces, name, comm=None):
    nl = len(pieces)
    _, r2, cc = pieces[0].shape
    tr = 128 if r2 % 128 == 0 else r2 // 2
    n = r2 // tr

    def body(*refs):
        p_refs, out = refs[:nl], refs[nl]
        slots, send_sems, local_sems, recv_sem = refs[nl + 1:]
        x, y, c = _me()
        sibling = (x, y, 1 - c)
        l, i = pl.program_id(0), pl.program_id(1)
        step = l * n + i

        def rows(st, core):
            return out.at[st // n, pl.ds(core * r2 + (st % n) * tr, tr), :]

        def copies(st):
            slot = st % 2
            local = pltpu.make_async_copy(slots.at[slot], rows(st, c), local_sems.at[slot])
            remote = pltpu.make_async_remote_copy(
                src_ref=slots.at[slot], dst_ref=rows(st, c), send_sem=send_sems.at[slot], recv_sem=recv_sem,
                device_id=sibling, device_id_type=MESH)
            return local, remote

        for ll in range(nl):
            @pl.when(l == ll)
            def _():
                acc = p_refs[ll][0].astype(F32)
                for d in range(1, N_DEV):
                    acc = acc + p_refs[ll][d].astype(F32)
                slots[step % 2] = acc

        for cp in copies(step):
            cp.start()

        @pl.when(step >= 1)
        def _():
            local, remote = copies(step - 1)
            local.wait()
            remote.wait_send()

        @pl.when(step == nl * n - 1)
        def _():
            local, remote = copies(step)
            local.wait()
            remote.wait_send()
            theirs = out.at[:, pl.ds((1 - c) * r2, r2), :]
            pltpu.make_async_remote_copy(src_ref=theirs, dst_ref=theirs, send_sem=send_sems.at[0],
                                         recv_sem=recv_sem, device_id=sibling, device_id_type=MESH).wait_recv()

    def piece_spec(ll):
        def index(l, i):
            return (0, jnp.where(l == ll, i, jnp.where(l < ll, 0, n - 1)), 0)
        return pl.BlockSpec((N_DEV, tr, cc), index)

    (out,), got = _call(
        body, name=name, grid=(nl, n),
        in_specs=[piece_spec(ll) for ll in range(nl)],
        out_specs=[pl.BlockSpec(memory_space=pl.ANY)],
        out_shape=[jax.ShapeDtypeStruct((nl, 2 * r2, cc), F32)],
        scratch_shapes=[pltpu.VMEM((2, tr, cc), F32), pltpu.SemaphoreType.DMA((2,)), pltpu.SemaphoreType.DMA((2,)),
                        pltpu.SemaphoreType.DMA(())],
        semantics=("arbitrary", "arbitrary"), args=tuple(pieces), comm=comm)
    return (out, got) if comm is not None else out


def sum_pieces(pieces, name):
    _, R, C = pieces.shape
    tr = _tile(R, 128) if R % 128 == 0 else R

    def body(p_ref, o_ref):
        acc = p_ref[0].astype(F32)
        for d in range(1, N_DEV):
            acc = acc + p_ref[d].astype(F32)
        o_ref[...] = acc

    return pl.pallas_call(
        body, name=name, grid=(R // tr,),
        in_specs=[pl.BlockSpec((N_DEV, tr, C), lambda i: (0, i, 0))],
        out_specs=pl.BlockSpec((tr, C), lambda i: (i, 0)),
        out_shape=jax.ShapeDtypeStruct((R, C), F32),
        compiler_params=_params("parallel"),
    )(pieces)


def adamw(w, g, m, v, name):
    Lyr, R, C = w.shape
    tr = _tile(R, 256) if R % 8 == 0 else R
    c1 = 1.0 / (1.0 - ADAM_B1 ** ADAM_STEP)
    c2 = 1.0 / (1.0 - ADAM_B2 ** ADAM_STEP)

    def body(w_ref, g_ref, m_ref, v_ref, d_ref, nm_ref, nv_ref):
        gv = g_ref[...]
        nm = ADAM_B1 * m_ref[...] + (1.0 - ADAM_B1) * gv
        nv = ADAM_B2 * v_ref[...] + (1.0 - ADAM_B2) * (gv * gv)
        nm_ref[...] = nm
        nv_ref[...] = nv
        d_ref[...] = -ADAM_LR * ((nm * c1) / (jnp.sqrt(nv * c2) + ADAM_EPS) + ADAM_WD * w_ref[...])

    spec = pl.BlockSpec((None, tr, C), lambda l, i: (l, i, 0))
    shp = jax.ShapeDtypeStruct(w.shape, F32)
    return pl.pallas_call(
        body, name=name, grid=(Lyr, R // tr),
        in_specs=[spec] * 4, out_specs=[spec] * 3, out_shape=[shp] * 3,
        compiler_params=_params("parallel", "parallel"),
    )(w, g, m, v)


def _rope_tables(T):
    pos = jnp.arange(T, dtype=F32)
    inv_freq = THETA ** (-jnp.arange(0, ROT, 2, dtype=F32) / ROT)
    ang = pos[:, None] * inv_freq[None, :]
    cs = jnp.concatenate([jnp.cos(ang), jnp.sin(ang)], axis=1)
    half = ROT // 2
    lane = jnp.arange(3 * LANES)
    table, lm = lane // LANES, lane % HD
    src = jnp.where(table == 0, lm % half, half + lm % half)
    i32 = lambda b: b.astype(jnp.int32)
    sign = jnp.where(table == 0, i32(lm < ROT), jnp.where(table == 1, -i32(lm < half), i32((lm >= half) & (lm < ROT))))
    place = (jnp.arange(ROT)[:, None] == src[None, :]) * sign[None, :].astype(F32)
    ones = ((table == 0) & (lm >= ROT)).astype(F32)
    return jnp.dot(cs, place, precision=lax.Precision.HIGHEST) + ones[None, :]


def _tab_specs(rows):
    return [pl.BlockSpec((rows, LANES), lambda i, k=k: (i, k)) for k in range(3)]


def kernel(x, attn_norm, attn_w_qkv, attn_w_o, attn_sink, conv_norm, conv_w_pw1, conv_b_pw1, conv_w_dw, conv_b_dw, conv_ln_g, conv_ln_b, conv_w_pw2, conv_b_pw2, ffn_norm, ffn_w_gu, ffn_w_down, final_norm, loss_target, m_attn_norm, m_attn_w_qkv, m_attn_w_o, m_attn_sink, m_conv_norm, m_conv_w_pw1, m_conv_b_pw1, m_conv_w_dw, m_conv_b_dw, m_conv_ln_g, m_conv_ln_b, m_conv_w_pw2, m_conv_b_pw2, m_ffn_norm, m_ffn_w_gu, m_ffn_w_down, m_final_norm, v_attn_norm, v_attn_w_qkv, v_attn_w_o, v_attn_sink, v_conv_norm, v_conv_w_pw1, v_conv_b_pw1, v_conv_w_dw, v_conv_b_dw, v_conv_ln_g, v_conv_ln_b, v_conv_w_pw2, v_conv_b_pw2, v_ffn_norm, v_ffn_w_gu, v_ffn_w_down, v_final_norm):
    T = x.shape[1]
    x0 = x[0]
    target = loss_target[0]
    ix, iy = lax.axis_index("x"), lax.axis_index("y")
    chip = 2 * ix + iy
    rc = rs1 = rs2 = _rope_tables(T)

    bf = lambda t: t.astype(BF16)

    def place(vec, width):
        return lax.dynamic_update_slice(jnp.zeros((vec.shape[0], N_CHIPS * width), F32), vec, (0, chip * width))

    small_rows = jnp.concatenate([
        place(conv_norm, 256), place(conv_b_pw1, 512).reshape(2, D), place(conv_b_dw, 256), place(conv_ln_g, 256),
        place(conv_ln_b, 256), place(conv_b_pw2, 256), jnp.zeros((1, D), F32),
        place(conv_w_dw[0], 256), jnp.zeros((1, D), F32)], axis=0)

    h0, (w_qkv,) = rms_first(x0, attn_norm, comm=_Gather([bf(attn_w_qkv[0])], [False]))
    qkv, (w_o, got) = qkv_proj(h0, w_qkv, rc, rs1, rs2,
                               comm=_Both(_Gather([bf(attn_w_o[0])], [True]), _Scatter([], small_rows)))
    psmall = sum_pieces(got, "sum_small_params") * 0.5
    p_conv_norm, p_b_pw1 = psmall[0:1], psmall[1:3].reshape(1, 2 * D)
    p_b_dw, p_ln_g, p_ln_b, p_b_pw2 = psmall[3:4], psmall[4:5], psmall[5:6], psmall[6:7]
    p_w_dw = psmall[8:40]
    sink = attn_sink[0]
    o, (w_gu0,) = attn_fwd(qkv, sink, comm=_Gather([bf(ffn_w_gu[0])], [False]))
    zero_b = jnp.zeros((1, D), F32)
    zero_gu = jnp.zeros((1, 2 * DFF), F32)
    x1, h1, gu0, act0, (w_down0, w_pw1, w_pw2) = rms_mm_gate(
        (o, w_o, zero_b, x0), ffn_norm[0:1], w_gu0, zero_gu, DFF, True, BF16, "ffn0_up",
        comm=_Gather([bf(ffn_w_down[0]), bf(conv_w_pw1[0]), bf(conv_w_pw2[0])], [True, False, True]))
    x2, h2, pre, glu, (w_down1,) = rms_mm_gate((act0, w_down0, zero_b, x1), p_conv_norm, w_pw1, p_b_pw1, D, False, F32,
                                               "conv_pw1", comm=_Gather([bf(ffn_w_down[1])], [True]))
    dwc, sw, (w_gu1,) = conv_fwd(glu, p_w_dw, p_b_dw, p_ln_g, p_ln_b, comm=_Gather([bf(ffn_w_gu[1])], [False]))
    x3, h3, gu1, act1, _ = rms_mm_gate((sw, w_pw2, p_b_pw2, x2), ffn_norm[1:2], w_gu1, zero_gu, DFF, True, BF16,
                                       "ffn1_up")
    dx4, loss_part, d_final = mm_res_loss(act1, w_down1, x3, final_norm.reshape(1, D), target)

    dgu1, _ = swiglu_bwd(dx4, w_down1, gu1, "ffn1_down_bwd")
    g_down1 = dw_row(act1, dx4, "ffn1_down_dw")
    dx3, d_ffn1, ddwc, d_ln_g, d_ln_b, d_b_pw2, _ = mm_bt_rmsbwd(
        dgu1, w_gu1, x3, ffn_norm[1:2], dx4, "ffn1_up_bwd", conv_tail=(w_pw2, dwc, p_ln_g, p_ln_b))
    g_gu1 = dw_col(h3, dgu1, "ffn1_up_dw")

    g_pw2 = dw_row(sw, dx3, "conv_pw2_dw")
    dpre, d_w_dw, d_b_dw, d_b_pw1, (r_gu1, r_down1) = conv_bwd(ddwc, glu, pre, p_w_dw,
                                                               comm=_Scatter([g_gu1, g_down1]))
    dx2, d_conv_norm, _ = mm_bt_rmsbwd(dpre, w_pw1, x2, p_conv_norm, dx3, "conv_pw1_bwd")
    g_pw1 = dw_col(h2, dpre, "conv_pw1_dw")

    dgu0, (r_pw1, r_pw2) = swiglu_bwd(dx2, w_down0, gu0, "ffn0_down_bwd", comm=_Scatter([g_pw1, g_pw2]))
    g_down0 = dw_row(act0, dx2, "ffn0_down_dw")
    dx1, d_ffn0, do, _ = mm_bt_rmsbwd(dgu0, w_gu0, x1, ffn_norm[0:1], dx2, "ffn0_up_bwd", proj_w=w_o)
    g_gu0 = dw_col(h1, dgu0, "ffn0_up_dw")

    g_o = dw_row(o, dx1, "attn_out_dw")
    dq, dkc, dvc, d_sink, (r_gu0, r_down0, r_o) = attn_bwd(qkv, o, do, sink, rc, rs1, rs2,
                                                           comm=_Scatter([g_gu0, g_down0, g_o]))
    dqkv = kv_sum(dq, dkc, dvc, rc, rs1, rs2)[None]
    g_qkv = dw_col(h0, dqkv, "attn_qkv_dw")
    dx0, d_attn_norm, _ = mm_bt_rmsbwd(dqkv, w_qkv, x0, attn_norm, dx1, "attn_qkv_bwd")

    pad16 = lambda t: jnp.concatenate([t, jnp.zeros((1, D - t.shape[1]), F32)], axis=1)
    small_g = jnp.concatenate([
        d_attn_norm, pad16(d_sink), d_conv_norm, d_b_pw1.reshape(2, D), d_b_dw, d_ln_g, d_ln_b, d_b_pw2,
        d_ffn0, d_ffn1, d_final, pad16(loss_part), jnp.zeros((3, D), F32), d_w_dw], axis=0)
    gf_gu, (r_qkv, r_small) = sum_swap([r_gu0, r_gu1], "sum_gu", comm=_Scatter([g_qkv], small_g))
    gf_down = sum_swap([r_down0, r_down1], "sum_down")
    gf_pw1, gf_pw2 = sum_swap([r_pw1], "sum_pw1"), sum_swap([r_pw2], "sum_pw2")
    gf_qkv, gf_o = sum_swap([r_qkv], "sum_qkv"), sum_swap([r_o], "sum_o")
    gs = sum_pieces(r_small, "sum_small_grads")
    loss = gs[12, 0]

    def take(row0, nrows, width):
        return lax.dynamic_slice(gs, (row0, chip * width), (nrows, width))

    grads = {
        "attn_norm": gs[0:1], "attn_w_qkv": gf_qkv, "attn_w_o": gf_o, "attn_sink": gs[1:2, :N_HEADS],
        "conv_norm": take(2, 1, 256), "conv_w_pw1": gf_pw1,
        "conv_b_pw1": lax.dynamic_slice(gs[3:5].reshape(1, 2 * D), (0, chip * 512), (1, 512)),
        "conv_w_dw": take(16, 32, 256)[None, :CONV_W], "conv_b_dw": take(5, 1, 256), "conv_ln_g": take(6, 1, 256),
        "conv_ln_b": take(7, 1, 256), "conv_w_pw2": gf_pw2, "conv_b_pw2": take(8, 1, 256),
        "ffn_norm": gs[9:11], "ffn_w_gu": gf_gu, "ffn_w_down": gf_down, "final_norm": gs[11],
    }
    weights = dict(attn_norm=attn_norm, attn_w_qkv=attn_w_qkv, attn_w_o=attn_w_o, attn_sink=attn_sink,
                   conv_norm=conv_norm, conv_w_pw1=conv_w_pw1, conv_b_pw1=conv_b_pw1, conv_w_dw=conv_w_dw,
                   conv_b_dw=conv_b_dw, conv_ln_g=conv_ln_g, conv_ln_b=conv_ln_b, conv_w_pw2=conv_w_pw2,
                   conv_b_pw2=conv_b_pw2, ffn_norm=ffn_norm, ffn_w_gu=ffn_w_gu, ffn_w_down=ffn_w_down,
                   final_norm=final_norm)
    m_in = dict(attn_norm=m_attn_norm, attn_w_qkv=m_attn_w_qkv, attn_w_o=m_attn_w_o, attn_sink=m_attn_sink,
                conv_norm=m_conv_norm, conv_w_pw1=m_conv_w_pw1, conv_b_pw1=m_conv_b_pw1, conv_w_dw=m_conv_w_dw,
                conv_b_dw=m_conv_b_dw, conv_ln_g=m_conv_ln_g, conv_ln_b=m_conv_ln_b, conv_w_pw2=m_conv_w_pw2,
                conv_b_pw2=m_conv_b_pw2, ffn_norm=m_ffn_norm, ffn_w_gu=m_ffn_w_gu, ffn_w_down=m_ffn_w_down,
                final_norm=m_final_norm)
    v_in = dict(attn_norm=v_attn_norm, attn_w_qkv=v_attn_w_qkv, attn_w_o=v_attn_w_o, attn_sink=v_attn_sink,
                conv_norm=v_conv_norm, conv_w_pw1=v_conv_w_pw1, conv_b_pw1=v_conv_b_pw1, conv_w_dw=v_conv_w_dw,
                conv_b_dw=v_conv_b_dw, conv_ln_g=v_conv_ln_g, conv_ln_b=v_conv_ln_b, conv_w_pw2=v_conv_w_pw2,
                conv_b_pw2=v_conv_b_pw2, ffn_norm=v_ffn_norm, ffn_w_gu=v_ffn_w_gu, ffn_w_down=v_ffn_w_down,
                final_norm=v_final_norm)
    order = list(weights)
    g_out, d_out, m_out, v_out = [], [], [], []
    for nm in order:
        w = weights[nm]
        shape = w.shape
        as3 = lambda t: t.reshape((1,) * (3 - len(shape)) + shape) if len(shape) < 3 else t.reshape(shape)
        g3 = as3(grads[nm].reshape(shape))
        delta, nm_, nv_ = adamw(as3(w), g3, as3(m_in[nm]), as3(v_in[nm]), "adamw_" + nm)
        g_out.append(g3.reshape(shape))
        d_out.append(delta.reshape(shape))
        m_out.append(nm_.reshape(shape))
        v_out.append(nv_.reshape(shape))
    return (loss, dx0[None], *g_out, *d_out, *m_out, *v_out)
```

```python
import math

import jax
import jax.numpy as jnp
from jax import lax
from jax.experimental import pallas as pl
from jax.experimental.pallas import tpu as pltpu

F32 = jnp.float32
BF16 = jnp.bfloat16

D = 1024
N_HEADS = 16
N_KV = 4
GROUP = N_HEADS // N_KV
HD = 64
ROT = 16
THETA = 500000.0
BLK = 128
QKV = (N_HEADS + 2 * N_KV) * HD
KV_OFF = N_HEADS * HD
DFF = 2816
CONV_W = 31
CONV_PAD = 15
HALO = 16
CONV_JB = 16
CONV_JB_BWD = 8
EPS = 1e-6
NEG = -1e30
N_CHIPS = 4
N_DEV = 8
LANES = 128
SUBLANES = 8

ADAM_LR, ADAM_B1, ADAM_B2, ADAM_EPS, ADAM_WD, ADAM_STEP = 0.001, 0.9, 0.999, 1e-08, 0.01, 10

VMEM_LIMIT = 56 * 1024 * 1024
MESH = pl.DeviceIdType.MESH


def _params(*sem):
    return pltpu.CompilerParams(dimension_semantics=sem, vmem_limit_bytes=VMEM_LIMIT)


def _tile(n, want):
    if n <= want:
        return n
    for t in range(want, 7, -1):
        if n % t == 0 and t % 8 == 0:
            return t
    return n


MXU_COLS = 256


def _col_chunks(n):
    return [slice(c, min(c + MXU_COLS, n)) for c in range(0, n, MXU_COLS)]


def _sigmoid(v):
    return jax.nn.sigmoid(v)


def _rms_fwd(xv, gain):
    r = lax.rsqrt(jnp.mean(xv * xv, axis=-1, keepdims=True) + EPS)
    return xv * r * gain


def _rms_bwd(dh, xv, gain, dres):
    r = lax.rsqrt(jnp.mean(xv * xv, axis=-1, keepdims=True) + EPS)
    xhat = xv * r
    gy = dh * gain
    dx = r * (gy - xhat * jnp.mean(gy * xhat, axis=-1, keepdims=True))
    return dx + dres, dh * xhat


def _rope(blk, c, s1, s2):
    return blk * c + pltpu.roll(blk, LANES - ROT // 2, 1) * s1 + pltpu.roll(blk, ROT // 2, 1) * s2


def _dot(a, b):
    return jnp.dot(a, b, preferred_element_type=F32)


def _dot_tb(a, b):
    return lax.dot_general(a, b, (((1,), (1,)), ((), ())), preferred_element_type=F32)


def _dot_ta(a, b):
    return lax.dot_general(a, b, (((0,), (0,)), ((), ())), preferred_element_type=F32)


def rms_first(x, gain, comm):
    T = x.shape[0]
    tm = _tile(T, 512)

    def body(x_ref, g_ref, h_ref):
        h_ref[...] = _rms_fwd(x_ref[...], g_ref[...]).astype(BF16)

    (h,), got = _call(
        body, name="rms_first", grid=(T // tm,),
        in_specs=[pl.BlockSpec((tm, D), lambda i: (i, 0)), pl.BlockSpec((1, D), lambda i: (0, 0))],
        out_specs=[pl.BlockSpec((tm, D), lambda i: (i, 0))], out_shape=[jax.ShapeDtypeStruct((T, D), BF16)],
        semantics=("parallel",), args=(x, gain), comm=comm)
    return h, got


def qkv_proj(h, w, rc, rs1, rs2, comm=None):
    T = h.shape[0]
    tm = _tile(T, 512)

    def body(h_ref, w_ref, c_ref, s1_ref, s2_ref, qkv_ref):
        acc = _dot(h_ref[...], w_ref[...])
        c, s1, s2 = c_ref[...], s1_ref[...], s2_ref[...]
        n_rot = (KV_OFF + N_KV * HD) // LANES
        for j in range(n_rot):
            sl = slice(LANES * j, LANES * (j + 1))
            roped = _rope(acc[:, sl], c, s1, s2)
            if j < KV_OFF // LANES:
                roped = roped * Q_SCALE
            qkv_ref[:, sl] = roped.astype(BF16)
        qkv_ref[:, n_rot * LANES:] = acc[:, n_rot * LANES:].astype(BF16)

    row = lambda i: (i, 0)
    full = lambda i: (0, 0)
    (qkv,), got = _call(
        body, name="qkv_proj", grid=(T // tm,),
        in_specs=[pl.BlockSpec((tm, D), row), pl.BlockSpec((D, QKV), full), *_tab_specs(tm)],
        out_specs=[pl.BlockSpec((tm, QKV), row)],
        out_shape=[jax.ShapeDtypeStruct((T, QKV), BF16)],
        semantics=("parallel",), args=(h, w, rc, rs1, rs2), comm=comm)
    return qkv, got


Q_SCALE = 1.0 / math.sqrt(HD)


def _attn_mask(n, T):
    ci = lax.broadcasted_iota(jnp.int32, (3 * BLK, BLK), 0)
    qi = lax.broadcasted_iota(jnp.int32, (3 * BLK, BLK), 1)
    key_pos = n * BLK - BLK + ci
    return (jnp.abs(ci - BLK - qi) <= BLK) & (key_pos >= 0) & (key_pos < T)


def _kv_padded(kv, first_tile):
    low = lax.broadcasted_iota(jnp.int32, (3 * BLK, LANES), 1) < HD
    zero = jnp.zeros((3 * BLK, LANES), BF16)
    out = {}
    for g in range(N_KV):
        t = kv[:, (first_tile + g // 2) * LANES:(first_tile + g // 2 + 1) * LANES]
        swapped = jnp.concatenate([t[:, HD:], t[:, :HD]], axis=1)
        for p in range(2):
            out[g, p] = jnp.where(low if p == 0 else ~low, t if g % 2 == p else swapped, zero)
    return out


def _pair_products(kvx, tile_of):
    both = {g: jnp.concatenate([kvx[g, 0], kvx[g, 1]], axis=0) for g in range(N_KV)}
    out = []
    for j in range(N_HEADS // 2):
        prod = _dot_tb(both[2 * j // GROUP], tile_of(j))
        out += [prod[:3 * BLK], prod[3 * BLK:]]
    return out


def _softmax_sink(s, valid, sk):
    s = jnp.where(valid, s, NEG)
    m = jnp.maximum(jnp.max(s, axis=0, keepdims=True), sk)
    e = jnp.exp(s - m)
    es = jnp.exp(sk - m)
    inv = 1.0 / (jnp.sum(e, axis=0, keepdims=True) + es)
    return e * inv, es * inv


def _attn_specs(T):
    nb = T // BLK
    kv_blk = 2 * N_KV * HD
    kv_col = KV_OFF // kv_blk
    q_spec = pl.BlockSpec((BLK, KV_OFF), lambda n: (n, 0))
    prev = pl.BlockSpec((BLK, kv_blk), lambda n: (jnp.maximum(n - 1, 0), kv_col))
    own = pl.BlockSpec((BLK, kv_blk), lambda n: (n, kv_col))
    nxt = pl.BlockSpec((BLK, kv_blk), lambda n: (jnp.minimum(n + 1, nb - 1), kv_col))
    return nb, q_spec, prev, own, nxt


def attn_fwd(qkv, sink, comm=None):
    T = qkv.shape[0]
    nb, q_spec, prev, own, nxt = _attn_specs(T)

    def body(sink_ref, q_ref, kp_ref, ko_ref, kn_ref, o_ref):
        valid = _attn_mask(pl.program_id(0), T)
        kv = jnp.concatenate([kp_ref[...], ko_ref[...], kn_ref[...]], axis=0)
        kx, vx = _kv_padded(kv, 0), _kv_padded(kv, 2)
        ss = _pair_products(kx, lambda j: q_ref[:, j * LANES:(j + 1) * LANES])
        ps = [_softmax_sink(ss[h], valid, sink_ref[h])[0].astype(BF16) for h in range(N_HEADS)]
        vxt = {k: v.T for k, v in vx.items()}
        for j in range(N_HEADS // 2):
            g = 2 * j // GROUP
            o_t = _dot(vxt[g, 0], ps[2 * j]) + _dot(vxt[g, 1], ps[2 * j + 1])
            o_ref[:, j * LANES:(j + 1) * LANES] = o_t.T.astype(BF16)

    (o,), got = _call(
        body, name="attn_fwd", grid=(nb,),
        in_specs=[pl.BlockSpec(memory_space=pltpu.SMEM), q_spec, prev, own, nxt],
        out_specs=[pl.BlockSpec((BLK, D), lambda n: (n, 0))],
        out_shape=[jax.ShapeDtypeStruct((T, D), BF16)],
        semantics=("parallel",), args=(sink, qkv, qkv, qkv, qkv), comm=comm)
    return o, got


def rms_mm_gate(x, gain, w, bias, H, swiglu, act_dtype, name, comm=None):
    fused = isinstance(x, tuple)
    T = (x[0] if fused else x).shape[0]
    tm = _tile(T, 512)

    def body(*refs):
        if fused:
            a_ref, wp_ref, bp_ref, r_ref, g_ref, w_ref, b_ref, x_ref, h_ref, pre_ref, act_ref = refs
            xv = _dot(a_ref[...], wp_ref[...]) + bp_ref[...] + r_ref[...]
            x_ref[...] = xv
        else:
            x_ref, g_ref, w_ref, b_ref, h_ref, pre_ref, act_ref = refs
            xv = x_ref[...]
        h = _rms_fwd(xv, g_ref[...]).astype(BF16)
        h_ref[...] = h
        for cs in _col_chunks(H):
            cs2 = slice(H + cs.start, H + cs.stop)
            a = _dot(h, w_ref[:, cs]) + b_ref[:, cs]
            b = _dot(h, w_ref[:, cs2]) + b_ref[:, cs2]
            pre_ref[0, :, cs] = a.astype(BF16)
            pre_ref[1, :, cs] = b.astype(BF16)
            if swiglu:
                act = a * _sigmoid(a) * b
            else:
                act = a * _sigmoid(b)
            act_ref[:, cs] = act.astype(act_dtype)

    row = lambda i: (i, 0)
    full = lambda i: (0, 0)
    if fused:
        K = x[0].shape[1]
        x_specs = [pl.BlockSpec((tm, K), row), pl.BlockSpec((K, D), full, pipeline_mode=pl.Buffered(1)),
                   pl.BlockSpec((1, D), full), pl.BlockSpec((tm, D), row)]
        x_out = ([pl.BlockSpec((tm, D), row)], [jax.ShapeDtypeStruct((T, D), F32)])
        x_args = tuple(x)
    else:
        x_specs, x_out, x_args = [pl.BlockSpec((tm, D), row)], ([], []), (x,)
    outs, got = _call(
        body, name=name, grid=(T // tm,),
        in_specs=x_specs + [pl.BlockSpec((1, D), full),
                            pl.BlockSpec((D, 2 * H), full, pipeline_mode=pl.Buffered(1)), pl.BlockSpec((1, 2 * H), full)],
        out_specs=x_out[0] + [pl.BlockSpec((tm, D), row), pl.BlockSpec((2, tm, H), lambda i: (0, i, 0)),
                              pl.BlockSpec((tm, H), row)],
        out_shape=x_out[1] + [jax.ShapeDtypeStruct((T, D), BF16), jax.ShapeDtypeStruct((2, T, H), BF16),
                              jax.ShapeDtypeStruct((T, H), act_dtype)],
        semantics=("parallel",), args=x_args + (gain, w, bias), comm=comm)
    return (*outs, got)


def _conv_tiles(T):
    tt = _tile(T, 512)
    return tt, tt // SUBLANES, D // LANES


def _fill_strided(ext, p, L):
    main = p[HALO:HALO + SUBLANES * L, :].reshape(SUBLANES, L, LANES)
    ext[CONV_PAD:CONV_PAD + L] = jnp.swapaxes(main, 0, 1)

    def ibody(i, carry):
        ext[i] = p[pl.ds(i + 1, SUBLANES, stride=L), :]
        ext[i + CONV_PAD + L] = p[pl.ds(i + CONV_PAD + L + 1, SUBLANES, stride=L), :]
        return carry

    lax.fori_loop(0, CONV_PAD, ibody, 0, unroll=3)


def _conv_specs(T, tt):
    main = pl.BlockSpec((tt, D), lambda i: (i, 0))
    per = tt // HALO
    prev = pl.BlockSpec((HALO, D), lambda i: (jnp.maximum(i * per - 1, 0), 0))
    nxt = pl.BlockSpec((HALO, D), lambda i: (jnp.minimum((i + 1) * per, T // HALO - 1), 0))
    return main, prev, nxt


def _fill_pad(pad, main_ref, prev_ref, next_ref, i, n_i, tt, nlt):
    keep_p = (i > 0).astype(F32)
    keep_n = (i < n_i - 1).astype(F32)
    for lt in range(nlt):
        sl = slice(lt * LANES, (lt + 1) * LANES)
        pad[lt, 0:HALO, :] = prev_ref[:, sl] * keep_p
        pad[lt, HALO:HALO + tt, :] = main_ref[:, sl]
        pad[lt, HALO + tt:2 * HALO + tt, :] = next_ref[:, sl] * keep_n


def conv_fwd(glu, w_dw, b_dw, ln_g, ln_b, comm=None):
    T = glu.shape[0]
    tt, L, nlt = _conv_tiles(T)
    n_i = T // tt
    main, prev, nxt = _conv_specs(T, tt)

    def body(x_ref, xp_ref, xn_ref, w_ref, b_ref, g_ref, bb_ref, dwc_ref, sw_ref, pad, ob, ext, wk):
        i = pl.program_id(0)
        _fill_pad(pad, x_ref, xp_ref, xn_ref, i, n_i, tt, nlt)
        for lt in range(nlt):
            sl = slice(lt * LANES, (lt + 1) * LANES)
            o = ob.at[lt]
            _fill_strided(ext, pad.at[lt], L)
            for k in range(CONV_W):
                wk[k] = jnp.broadcast_to(w_ref[k:k + 1, sl], (SUBLANES, LANES))

            def jbody(jb, carry):
                j = jb * CONV_JB
                accs = [None] * CONV_JB
                for m in range(CONV_W + CONV_JB - 1):
                    e = ext[j + m]
                    for u in range(CONV_JB):
                        if 0 <= m - u < CONV_W:
                            t = e * wk[m - u]
                            accs[u] = t if accs[u] is None else accs[u] + t
                for u in range(CONV_JB):
                    o[pl.ds(j + u, SUBLANES, stride=L), :] = accs[u]
                return carry

            lax.fori_loop(0, L // CONV_JB, jbody, 0)
        y = jnp.concatenate([ob[lt] for lt in range(nlt)], axis=1) + b_ref[...]
        dwc_ref[...] = y
        mu = jnp.mean(y, axis=-1, keepdims=True)
        yc = y - mu
        var = jnp.mean(yc * yc, axis=-1, keepdims=True)
        z = yc * lax.rsqrt(var + EPS) * g_ref[...] + bb_ref[...]
        sw_ref[...] = (z * _sigmoid(z)).astype(BF16)

    full = lambda i: (0, 0)
    (dwc, sw), got = _call(
        body, name="conv_fwd", grid=(n_i,),
        in_specs=[main, prev, nxt, pl.BlockSpec((32, D), full), pl.BlockSpec((1, D), full),
                  pl.BlockSpec((1, D), full), pl.BlockSpec((1, D), full)],
        out_specs=[pl.BlockSpec((tt, D), lambda i: (i, 0)), pl.BlockSpec((tt, D), lambda i: (i, 0))],
        out_shape=[jax.ShapeDtypeStruct((T, D), F32), jax.ShapeDtypeStruct((T, D), BF16)],
        scratch_shapes=[pltpu.VMEM((nlt, tt + 2 * HALO, LANES), F32), pltpu.VMEM((nlt, tt, LANES), F32),
                        pltpu.VMEM((L + 2 * HALO, SUBLANES, LANES), F32), pltpu.VMEM((32, SUBLANES, LANES), F32)],
        semantics=("parallel",), args=(glu, glu, glu, w_dw, b_dw, ln_g, ln_b), comm=comm)
    return dwc, sw, got


def mm_res_loss(a, w, resid, gain, target):
    T, K = a.shape
    tm = _tile(T, 512)

    def body(a_ref, w_ref, r_ref, g_ref, t_ref, dx_ref, loss_ref, dg_ref):
        @pl.when(pl.program_id(0) == 0)
        def _():
            loss_ref[...] = jnp.zeros_like(loss_ref)
            dg_ref[...] = jnp.zeros_like(dg_ref)

        xv, gain_v = _dot(a_ref[...], w_ref[...]) + r_ref[...], g_ref[...]
        err = _rms_fwd(xv, gain_v) - t_ref[...]
        part = 0.5 * jnp.sum(jnp.mean(err * err, axis=-1, keepdims=True), axis=0, keepdims=True)
        loss_ref[...] += jnp.broadcast_to(part, loss_ref.shape)
        dx, dgr = _rms_bwd(err * (1.0 / D), xv, gain_v, 0.0)
        dx_ref[...] = dx
        dg_ref[...] += jnp.sum(dgr, axis=0, keepdims=True)

    row = lambda i: (i, 0)
    full = lambda i: (0, 0)
    return pl.pallas_call(
        body, name="ffn1_down_loss", grid=(T // tm,),
        in_specs=[pl.BlockSpec((tm, K), row), pl.BlockSpec((K, D), full), pl.BlockSpec((tm, D), row),
                  pl.BlockSpec((1, D), full), pl.BlockSpec((tm, D), row)],
        out_specs=[pl.BlockSpec((tm, D), row), pl.BlockSpec((1, LANES), full), pl.BlockSpec((1, D), full)],
        out_shape=[jax.ShapeDtypeStruct((T, D), F32), jax.ShapeDtypeStruct((1, LANES), F32),
                   jax.ShapeDtypeStruct((1, D), F32)],
        compiler_params=_params("arbitrary"),
    )(a, w, resid, gain, target)


def swiglu_bwd(dx, w_down, pre, name, comm=None):
    T = dx.shape[0]
    H = w_down.shape[0]
    tm = _tile(T, 512)

    def body(dx_ref, w_ref, pre_ref, dpre_ref):
        dxb = dx_ref[...].astype(BF16)
        for cs in _col_chunks(H):
            dact = _dot_tb(dxb, w_ref[cs, :])
            g = pre_ref[0, :, cs].astype(F32)
            u = pre_ref[1, :, cs].astype(F32)
            sg = _sigmoid(g)
            dpre_ref[0, :, cs] = (dact * u * sg * (1.0 + g * (1.0 - sg))).astype(BF16)
            dpre_ref[1, :, cs] = (dact * g * sg).astype(BF16)

    (dpre,), got = _call(
        body, name=name, grid=(T // tm,),
        in_specs=[pl.BlockSpec((tm, D), lambda i: (i, 0)),
                  pl.BlockSpec((H, D), lambda i: (0, 0), pipeline_mode=pl.Buffered(1)),
                  pl.BlockSpec((2, tm, H), lambda i: (0, i, 0))],
        out_specs=[pl.BlockSpec((2, tm, H), lambda i: (0, i, 0))],
        out_shape=[jax.ShapeDtypeStruct((2, T, H), BF16)],
        semantics=("parallel",), args=(dx, w_down, pre), comm=comm)
    return dpre, got


def _ln_silu_bwd(dsw, y, ln_g, ln_b):
    mu = jnp.mean(y, axis=-1, keepdims=True)
    yc = y - mu
    rstd = lax.rsqrt(jnp.mean(yc * yc, axis=-1, keepdims=True) + EPS)
    xhat = yc * rstd
    z = xhat * ln_g + ln_b
    sg = _sigmoid(z)
    dz = dsw * sg * (1.0 + z * (1.0 - sg))
    dxh = dz * ln_g
    dy = rstd * (dxh - jnp.mean(dxh, axis=-1, keepdims=True) - xhat * jnp.mean(dxh * xhat, axis=-1, keepdims=True))
    return dy, dz * xhat, dz


def mm_bt_rmsbwd(dpre, w, x, gain, dres, name, comm=None, proj_w=None, conv_tail=None):
    nh, T, H = dpre.shape
    tm = _tile(T, 512)
    n_extra_in = 1 if proj_w is not None else (4 if conv_tail is not None else 0)

    def body(*refs):
        dp_ref, w_ref, x_ref, g_ref, dres_ref = refs[:5]
        extra_in = refs[5:5 + n_extra_in]
        dx_ref, dg_ref = refs[5 + n_extra_in:7 + n_extra_in]
        extra_out = refs[7 + n_extra_in:]

        @pl.when(pl.program_id(0) == 0)
        def _():
            dg_ref[...] = jnp.zeros_like(dg_ref)
            for r in extra_out[1:]:
                r[...] = jnp.zeros_like(r)

        dh = _dot_tb(dp_ref[0], w_ref[:, 0:H])
        for hf in range(1, nh):
            dh = dh + _dot_tb(dp_ref[hf], w_ref[:, hf * H:(hf + 1) * H])
        dx, dgr = _rms_bwd(dh, x_ref[...], g_ref[...], dres_ref[...])
        dx_ref[...] = dx
        dg_ref[...] += jnp.sum(dgr, axis=0, keepdims=True)
        if proj_w is not None:
            extra_out[0][...] = _dot_tb(dx.astype(BF16), extra_in[0][...]).astype(BF16)
        elif conv_tail is not None:
            wt_ref, y_ref, lg_ref, lb_ref = extra_in
            dy, dgl, dbl = _ln_silu_bwd(_dot_tb(dx.astype(BF16), wt_ref[...]), y_ref[...], lg_ref[...], lb_ref[...])
            extra_out[0][...] = dy
            extra_out[1][...] += jnp.sum(dgl, axis=0, keepdims=True)
            extra_out[2][...] += jnp.sum(dbl, axis=0, keepdims=True)
            extra_out[3][...] += jnp.sum(dx, axis=0, keepdims=True)

    row = lambda i: (i, 0)
    full = lambda i: (0, 0)
    vec = pl.BlockSpec((1, D), full)
    vec_shape = jax.ShapeDtypeStruct((1, D), F32)
    in_specs = [pl.BlockSpec((nh, tm, H), lambda i: (0, i, 0)),
                pl.BlockSpec((D, nh * H), full, pipeline_mode=pl.Buffered(1)),
                pl.BlockSpec((tm, D), row), vec, pl.BlockSpec((tm, D), row)]
    out_specs = [pl.BlockSpec((tm, D), row), vec]
    out_shape = [jax.ShapeDtypeStruct((T, D), F32), vec_shape]
    args = (dpre, w, x, gain, dres)
    if proj_w is not None:
        N = proj_w.shape[0]
        in_specs.append(pl.BlockSpec((N, D), full, pipeline_mode=pl.Buffered(1)))
        out_specs.append(pl.BlockSpec((tm, N), row))
        out_shape.append(jax.ShapeDtypeStruct((T, N), BF16))
        args += (proj_w,)
    elif conv_tail is not None:
        in_specs += [pl.BlockSpec((D, D), full, pipeline_mode=pl.Buffered(1)), pl.BlockSpec((tm, D), row), vec, vec]
        out_specs += [pl.BlockSpec((tm, D), row), vec, vec, vec]
        out_shape += [jax.ShapeDtypeStruct((T, D), F32), vec_shape, vec_shape, vec_shape]
        args += tuple(conv_tail)
    outs, got = _call(body, name=name, grid=(T // tm,), in_specs=in_specs, out_specs=out_specs, out_shape=out_shape,
                      semantics=("arbitrary",), args=args, comm=comm)
    return (*outs, got)


def dw_col(a, dpre, name):
    T = a.shape[0]
    nh, _, H = dpre.shape
    per = nh * H // N_CHIPS
    bph = N_CHIPS // nh
    tt = _tile(T, 2048)
    nt = T // tt

    def body(a_ref, b_ref, o_ref, acc):
        t = pl.program_id(1)

        @pl.when(t == 0)
        def _():
            acc[...] = jnp.zeros_like(acc)

        acc[...] += _dot_ta(a_ref[...], b_ref[...])

        @pl.when(t == nt - 1)
        def _():
            o_ref[...] = acc[...].astype(BF16)

    return pl.pallas_call(
        body, name=name, grid=(N_CHIPS, nt),
        in_specs=[pl.BlockSpec((tt, D), lambda q, t: (t, 0)),
                  pl.BlockSpec((None, tt, per), lambda q, t: (q // bph, t, q % bph))],
        out_specs=pl.BlockSpec((None, D, per), lambda q, t: (q, 0, 0)),
        out_shape=jax.ShapeDtypeStruct((N_CHIPS, D, per), BF16),
        scratch_shapes=[pltpu.VMEM((D, per), F32)],
        compiler_params=_params("parallel", "arbitrary"),
    )(a, dpre)


def dw_row(a, b, name):
    T, R = a.shape
    cw = 1408 if R % 1408 == 0 else R
    tt = _tile(T, 1024)
    nt = T // tt

    def body(a_ref, b_ref, o_ref, acc):
        t = pl.program_id(1)

        @pl.when(t == 0)
        def _():
            acc[...] = jnp.zeros_like(acc)

        acc[...] += _dot_ta(a_ref[...], b_ref[...].astype(BF16))

        @pl.when(t == nt - 1)
        def _():
            o_ref[...] = acc[...].astype(BF16)

    out = pl.pallas_call(
        body, name=name, grid=(R // cw, nt),
        in_specs=[pl.BlockSpec((tt, cw), lambda q, t: (t, q)), pl.BlockSpec((tt, D), lambda q, t: (t, 0))],
        out_specs=pl.BlockSpec((cw, D), lambda q, t: (q, 0)),
        out_shape=jax.ShapeDtypeStruct((R, D), BF16),
        scratch_shapes=[pltpu.VMEM((cw, D), F32)],
        compiler_params=_params("parallel", "arbitrary"),
    )(a, b)
    return out.reshape(N_CHIPS, R // N_CHIPS, D)


def conv_bwd(ddwc, glu, pre, w_dw, comm=None):
    T = ddwc.shape[0]
    tt, L, nlt = _conv_tiles(T)
    n_i = T // tt
    main, prev, nxt = _conv_specs(T, tt)

    def body(d_ref, dp_ref, dn_ref, x_ref, xp_ref, xn_ref, pre_ref, w_ref,
             dpre_ref, dw_ref, dbd_ref, dbp_ref, padd, padx, ob, extd, extx, wk):
        i = pl.program_id(0)

        @pl.when(i == 0)
        def _():
            dw_ref[...] = jnp.zeros_like(dw_ref)
            dbd_ref[...] = jnp.zeros_like(dbd_ref)
            dbp_ref[...] = jnp.zeros_like(dbp_ref)

        _fill_pad(padd, d_ref, dp_ref, dn_ref, i, n_i, tt, nlt)
        _fill_pad(padx, x_ref, xp_ref, xn_ref, i, n_i, tt, nlt)
        for lt in range(nlt):
            sl = slice(lt * LANES, (lt + 1) * LANES)
            o = ob.at[lt]
            _fill_strided(extd, padd.at[lt], L)
            _fill_strided(extx, padx.at[lt], L)
            for k in range(CONV_W):
                wk[k] = jnp.broadcast_to(w_ref[k:k + 1, sl], (SUBLANES, LANES))

            nu = CONV_JB_BWD

            def jbody(jb, accs):
                j = jb * nu
                accs = list(accs)
                d = [extd[j + u + CONV_PAD] for u in range(nu)]
                g = [None] * nu
                for m in range(CONV_W + nu - 1):
                    ed = extd[j + 2 * CONV_PAD + nu - 1 - m]
                    ex = extx[j + m]
                    for u in range(nu):
                        k = m - (nu - 1 - u)
                        if 0 <= k < CONV_W:
                            t = ed * wk[k]
                            g[u] = t if g[u] is None else g[u] + t
                        k = m - u
                        if 0 <= k < CONV_W:
                            accs[k] = accs[k] + d[u] * ex
                for u in range(nu):
                    o[pl.ds(j + u, SUBLANES, stride=L), :] = g[u]
                return tuple(accs)

            accs = lax.fori_loop(0, L // nu, jbody, tuple(jnp.zeros((SUBLANES, LANES), F32) for _ in range(CONV_W)))
            for k in range(CONV_W):
                dw_ref[k:k + 1, sl] += jnp.sum(accs[k], axis=0, keepdims=True)
        dglu = jnp.concatenate([ob[lt] for lt in range(nlt)], axis=1)
        a = pre_ref[0].astype(F32)
        gate = pre_ref[1].astype(F32)
        sg = _sigmoid(gate)
        da = dglu * sg
        dgate = dglu * a * sg * (1.0 - sg)
        dpre_ref[0] = da.astype(BF16)
        dpre_ref[1] = dgate.astype(BF16)
        dbd_ref[...] += jnp.sum(d_ref[...], axis=0, keepdims=True)
        dbp_ref[0] += jnp.sum(da, axis=0, keepdims=True)
        dbp_ref[1] += jnp.sum(dgate, axis=0, keepdims=True)

    full = lambda i: (0, 0)
    (dpre, dw, dbd, dbp), got = _call(
        body, name="conv_bwd", grid=(n_i,),
        in_specs=[main, prev, nxt, main, prev, nxt, pl.BlockSpec((2, tt, D), lambda i: (0, i, 0)),
                  pl.BlockSpec((32, D), full)],
        out_specs=[pl.BlockSpec((2, tt, D), lambda i: (0, i, 0)), pl.BlockSpec((32, D), full),
                   pl.BlockSpec((1, D), full), pl.BlockSpec((2, 1, D), lambda i: (0, 0, 0))],
        out_shape=[jax.ShapeDtypeStruct((2, T, D), BF16), jax.ShapeDtypeStruct((32, D), F32),
                   jax.ShapeDtypeStruct((1, D), F32), jax.ShapeDtypeStruct((2, 1, D), F32)],
        scratch_shapes=[pltpu.VMEM((nlt, tt + 2 * HALO, LANES), F32), pltpu.VMEM((nlt, tt + 2 * HALO, LANES), F32),
                        pltpu.VMEM((nlt, tt, LANES), F32), pltpu.VMEM((L + 2 * HALO, SUBLANES, LANES), F32),
                        pltpu.VMEM((L + 2 * HALO, SUBLANES, LANES), F32), pltpu.VMEM((32, SUBLANES, LANES), F32)],
        semantics=("arbitrary",), args=(ddwc, ddwc, ddwc, glu, glu, glu, pre, w_dw), comm=comm)
    return dpre, dw, dbd, dbp, got


def attn_bwd(qkv, o, do, sink, rc, rs1, rs2, comm=None):
    T = qkv.shape[0]
    nb, q_spec, prev, own, nxt = _attn_specs(T)
    kvw = N_KV * HD

    def body(sink_ref, q_ref, kp_ref, ko_ref, kn_ref, o_ref, do_ref, c_ref, s1_ref, s2_ref,
             dq_ref, dkc_ref, dvc_ref, dsink_ref):
        n = pl.program_id(0)

        @pl.when(n == 0)
        def _():
            dsink_ref[...] = jnp.zeros_like(dsink_ref)

        valid = _attn_mask(n, T)
        kv = jnp.concatenate([kp_ref[...], ko_ref[...], kn_ref[...]], axis=0)
        kx, vx = _kv_padded(kv, 0), _kv_padded(kv, 2)
        tile = lambda ref, j: ref[:, j * LANES:(j + 1) * LANES]
        ss = _pair_products(kx, lambda j: tile(q_ref, j))
        dps = _pair_products(vx, lambda j: tile(do_ref, j))
        low_d = lax.broadcasted_iota(jnp.int32, (LANES, BLK), 0) < HD
        deltas = []
        for j in range(N_HEADS // 2):
            prod_t = tile(do_ref, j).astype(F32).T * tile(o_ref, j).astype(F32).T
            deltas.append(jnp.sum(jnp.where(low_d, prod_t, 0.0), axis=0, keepdims=True))
            deltas.append(jnp.sum(jnp.where(low_d, 0.0, prod_t), axis=0, keepdims=True))
        lane = lax.broadcasted_iota(jnp.int32, (1, N_HEADS), 1)
        dsink = jnp.zeros((1, N_HEADS), F32)
        pbs, dss = [], []
        for h in range(N_HEADS):
            p, p_sink = _softmax_sink(ss[h], valid, sink_ref[h])
            dss.append((p * (dps[h] - deltas[h])).astype(BF16))
            pbs.append(p.astype(BF16))
            part = -jnp.sum(p_sink * deltas[h], axis=1, keepdims=True)
            dsink = dsink + jnp.where(lane == h, part, 0.0)
        dsink_ref[...] += dsink
        c, s1, s2 = c_ref[...], s1_ref[...], s2_ref[...]
        kxt = {k: v.T for k, v in kx.items()}
        for j in range(N_HEADS // 2):
            g = 2 * j // GROUP
            dq_t = _dot(kxt[g, 0], dss[2 * j]) + _dot(kxt[g, 1], dss[2 * j + 1])
            dq_ref[:, j * LANES:(j + 1) * LANES] = (_rope(dq_t.T, c, -s1, -s2) * Q_SCALE).astype(BF16)
        low_k = lax.broadcasted_iota(jnp.int32, (3 * BLK, LANES), 1) < HD
        cols = lambda xs, g, p: jnp.concatenate([xs[GROUP * g + p], xs[GROUP * g + 2 + p]], axis=1)
        for t in range(N_KV // 2):
            sums = {}
            for g in (2 * t, 2 * t + 1):
                q2 = jnp.concatenate([tile(q_ref, 2 * g), tile(q_ref, 2 * g + 1)], axis=0)
                do2 = jnp.concatenate([tile(do_ref, 2 * g), tile(do_ref, 2 * g + 1)], axis=0)
                dk2 = _dot(jnp.concatenate([cols(dss, g, 0), cols(dss, g, 1)], axis=0), q2)
                dv2 = _dot(jnp.concatenate([cols(pbs, g, 0), cols(pbs, g, 1)], axis=0), do2)
                for p in range(2):
                    sums[g, p] = (dk2[p * 3 * BLK:(p + 1) * 3 * BLK], dv2[p * 3 * BLK:(p + 1) * 3 * BLK])
            for which, ref in ((0, dkc_ref), (1, dvc_ref)):
                keep = jnp.where(low_k, sums[2 * t, 0][which], sums[2 * t + 1, 1][which])
                swap = jnp.where(low_k, sums[2 * t + 1, 0][which], sums[2 * t, 1][which])
                ref[:, t * LANES:(t + 1) * LANES] = keep + pltpu.roll(swap, HD, 1)

    row = lambda n: (n, 0)
    (dq, dkc, dvc, dsink), got = _call(
        body, name="attn_bwd", grid=(nb,),
        in_specs=[pl.BlockSpec(memory_space=pltpu.SMEM), q_spec, prev, own, nxt,
                  pl.BlockSpec((BLK, D), row), pl.BlockSpec((BLK, D), row), *_tab_specs(BLK)],
        out_specs=[pl.BlockSpec((BLK, D), row), pl.BlockSpec((None, 3 * BLK, kvw), lambda n: (n, 0, 0)),
                   pl.BlockSpec((None, 3 * BLK, kvw), lambda n: (n, 0, 0)), pl.BlockSpec((1, N_HEADS), lambda n: (0, 0))],
        out_shape=[jax.ShapeDtypeStruct((T, QKV), BF16), jax.ShapeDtypeStruct((nb, 3 * BLK, kvw), F32),
                   jax.ShapeDtypeStruct((nb, 3 * BLK, kvw), F32), jax.ShapeDtypeStruct((1, N_HEADS), F32)],
        semantics=("arbitrary",), args=(sink, qkv, qkv, qkv, qkv, o, do, rc, rs1, rs2), comm=comm)
    return dq, dkc, dvc, dsink, got


def kv_sum(dqkv, dkc, dvc, rc, rs1, rs2):
    nb = dkc.shape[0]
    T = nb * BLK
    kvw = N_KV * HD

    G = 8
    ng = nb // G

    def gather3(own_ref, prev_ref, before_ref, next_ref, after_ref, m):
        has_before = (m > 0).astype(F32)
        has_after = (m < ng - 1).astype(F32)
        out = []
        for i in range(G):
            from_prev = prev_ref[i - 1] if i > 0 else before_ref[0] * has_before
            from_next = next_ref[i + 1] if i < G - 1 else after_ref[0] * has_after
            out.append(from_prev + own_ref[i] + from_next)
        return jnp.concatenate(out, axis=0)

    def body(_, ko, kp, kb, kn, ka, vo, vp, vb, vn, va, c_ref, s1_ref, s2_ref, out_ref):
        m = pl.program_id(0)
        dk = gather3(ko, kp, kb, kn, ka, m)
        dv = gather3(vo, vp, vb, vn, va, m)
        c, s1, s2 = c_ref[...], s1_ref[...], s2_ref[...]
        for j in range(kvw // LANES):
            sl = slice(LANES * j, LANES * (j + 1))
            out_ref[:, sl] = _rope(dk[:, sl], c, -s1, -s2).astype(BF16)
        out_ref[:, kvw:] = dv.astype(BF16)

    own = pl.BlockSpec((G, BLK, kvw), lambda m: (m, 1, 0))
    prev = pl.BlockSpec((G, BLK, kvw), lambda m: (m, 2, 0))
    before = pl.BlockSpec((1, BLK, kvw), lambda m: (jnp.maximum(G * m - 1, 0), 2, 0))
    nxt = pl.BlockSpec((G, BLK, kvw), lambda m: (m, 0, 0))
    after = pl.BlockSpec((1, BLK, kvw), lambda m: (jnp.minimum(G * m + G, nb - 1), 0, 0))
    five = [own, prev, before, nxt, after]
    return pl.pallas_call(
        body, name="kv_sum", grid=(ng,),
        in_specs=[pl.BlockSpec(memory_space=pl.ANY), *five, *five, *_tab_specs(G * BLK)],
        out_specs=pl.BlockSpec((G * BLK, 2 * kvw), lambda m: (m, KV_OFF // (2 * kvw))),
        out_shape=jax.ShapeDtypeStruct((T, QKV), BF16),
        input_output_aliases={0: 0},
        compiler_params=_params("parallel"),
    )(dqkv, *([dkc] * 5), *([dvc] * 5), rc, rs1, rs2)


def _me():
    return lax.axis_index("x"), lax.axis_index("y"), lax.axis_index("c")


def _half_rows(ref, sharded_rows, chip, core):
    R, C = ref.shape[-2], ref.shape[-1]
    lead = (slice(None),) * (len(ref.shape) - 2)
    if sharded_rows:
        per = R // N_CHIPS
        return ref.at[lead + (pl.ds(chip * per + core * (per // 2), per // 2), slice(None))]
    per = C // N_CHIPS
    return ref.at[lead + (pl.ds(core * (R // 2), R // 2), pl.ds(chip * per, per))]


class _Gather:
    def __init__(self, shards, sharded_rows):
        self.inputs = list(shards)
        self.rows = list(sharded_rows)
        self.n = self.n_in = self.n_out = len(shards)
        self.out_shapes = []
        for s, rows in zip(shards, sharded_rows):
            shp = list(s.shape)
            shp[-2 if rows else -1] *= N_CHIPS
            self.out_shapes.append(jax.ShapeDtypeStruct(tuple(shp), s.dtype))
        self.scratch = [pltpu.SemaphoreType.DMA((self.n, 6)), pltpu.SemaphoreType.DMA((self.n, 6)),
                        pltpu.SemaphoreType.DMA((self.n, 2))]

    def _ctx(self, ins, outs, sems):
        send_sems, recv_sems, local_sems = sems
        x, y, c = _me()
        chips = [(1 - x, y), (x, 1 - y), (1 - x, 1 - y)]

        def half_src(w, core):
            s = ins[w]
            R = s.shape[-2]
            return s.at[pl.ds(core * (R // 2), R // 2), :]

        def dst(w, chip, core):
            return _half_rows(outs[w], self.rows[w], chip, core)

        def copy(w, k, src, chip, core, to):
            return pltpu.make_async_remote_copy(
                src_ref=src, dst_ref=dst(w, chip, core), send_sem=send_sems.at[w, k], recv_sem=recv_sems.at[w, k],
                device_id=to, device_id_type=MESH)

        def local(w, core):
            return pltpu.make_async_copy(half_src(w, core), dst(w, 2 * x + y, core), local_sems.at[w, core])

        def first(w, j):
            qx, qy = chips[j]
            return copy(w, j, half_src(w, c), 2 * x + y, c, (qx, qy, c))

        def landed(w, j):
            qx, qy = chips[j]
            return copy(w, j, dst(w, 2 * qx + qy, c), 2 * qx + qy, c, (x, y, c))

        def passed(w, j):
            qx, qy = chips[j]
            return copy(w, 3 + j, dst(w, 2 * qx + qy, c), 2 * qx + qy, c, (x, y, 1 - c))

        def from_sibling(w, j):
            qx, qy = chips[j]
            return copy(w, 3 + j, dst(w, 2 * qx + qy, 1 - c), 2 * qx + qy, 1 - c, (x, y, c))

        return local, first, landed, passed, from_sibling

    def start(self, ins, outs, sems):
        local, first, _, _, _ = self._ctx(ins, outs, sems)
        for w in range(self.n):
            for core in range(2):
                local(w, core).start()
            for j in range(3):
                first(w, j).start()

    def mid(self, ins, outs, sems):
        _, _, landed, passed, _ = self._ctx(ins, outs, sems)
        for w in range(self.n):
            for j in range(3):
                landed(w, j).wait_recv()
                passed(w, j).start()

    def end(self, ins, outs, sems):
        local, first, _, passed, from_sibling = self._ctx(ins, outs, sems)
        for w in range(self.n):
            for j in range(3):
                from_sibling(w, j).wait_recv()
        for w in range(self.n):
            for j in range(3):
                first(w, j).wait_send()
                passed(w, j).wait_send()
            for core in range(2):
                local(w, core).wait()


class _Scatter:
    def __init__(self, grads, small=None):
        self.inputs = list(grads) + ([small] if small is not None else [])
        self.ng = len(grads)
        self.n = self.n_in = self.n_out = len(self.inputs)
        self.out_shapes = [jax.ShapeDtypeStruct((N_DEV, g.shape[1] // 2, g.shape[2]), g.dtype) for g in grads]
        if small is not None:
            self.out_shapes.append(jax.ShapeDtypeStruct((N_DEV,) + small.shape, small.dtype))
        self.scratch = [pltpu.SemaphoreType.DMA((self.n, N_DEV)), pltpu.SemaphoreType.DMA((self.n, N_DEV)),
                        pltpu.SemaphoreType.DMA((self.n,))]

    def _ctx(self, ins, outs, sems):
        send_sems, recv_sems, local_sems = sems
        x, y, c = _me()
        me = 4 * x + 2 * y + c

        def piece(w, chip, core):
            if w >= self.ng:
                return ins[w]
            half = ins[w].shape[1] // 2
            return ins[w].at[chip, pl.ds(core * half, half), :]

        def peer_of(k):
            return x ^ ((k >> 2) & 1), y ^ ((k >> 1) & 1), c ^ (k & 1)

        def local(w):
            return pltpu.make_async_copy(piece(w, 2 * x + y, c), outs[w].at[me], local_sems.at[w])

        def send(w, k):
            px, py, pc = peer_of(k)
            return pltpu.make_async_remote_copy(
                src_ref=piece(w, 2 * px + py, pc), dst_ref=outs[w].at[me], send_sem=send_sems.at[w, k],
                recv_sem=recv_sems.at[w, k], device_id=(px, py, pc), device_id_type=MESH)

        def recv(w, k):
            px, py, pc = peer_of(k)
            return pltpu.make_async_remote_copy(
                src_ref=piece(w, 2 * x + y, c), dst_ref=outs[w].at[4 * px + 2 * py + pc], send_sem=send_sems.at[w, k],
                recv_sem=recv_sems.at[w, k], device_id=(px, py, pc), device_id_type=MESH)

        return local, send, recv

    def start(self, ins, outs, sems):
        local, send, _ = self._ctx(ins, outs, sems)
        for w in range(self.n):
            local(w).start()
            for k in range(1, N_DEV):
                send(w, k).start()

    def mid(self, ins, outs, sems):
        pass

    def end(self, ins, outs, sems):
        local, send, recv = self._ctx(ins, outs, sems)
        for w in range(self.n):
            for k in range(1, N_DEV):
                recv(w, k).wait_recv()
        for w in range(self.n):
            for k in range(1, N_DEV):
                send(w, k).wait_send()
            local(w).wait()


class _Both:
    def __init__(self, a, b):
        self.a, self.b = a, b
        self.inputs = a.inputs + b.inputs
        self.out_shapes = a.out_shapes + b.out_shapes
        self.scratch = a.scratch + b.scratch
        self.n_in, self.n_out = a.n_in + b.n_in, a.n_out + b.n_out

    def _split(self, ins, outs, sems):
        a, na = self.a, len(self.a.scratch)
        return (ins[:a.n_in], outs[:a.n_out], sems[:na]), (ins[a.n_in:], outs[a.n_out:], sems[na:])

    def start(self, ins, outs, sems):
        pa, pb = self._split(ins, outs, sems)
        self.a.start(*pa)
        self.b.start(*pb)

    def mid(self, ins, outs, sems):
        pa, pb = self._split(ins, outs, sems)
        self.a.mid(*pa)
        self.b.mid(*pb)

    def end(self, ins, outs, sems):
        pa, pb = self._split(ins, outs, sems)
        self.a.end(*pa)
        self.b.end(*pb)


def _call(body, *, name, grid, in_specs, out_specs, out_shape, scratch_shapes=(), semantics, args, comm=None):
    if comm is None:
        outs = pl.pallas_call(
            body, name=name, grid=grid, in_specs=in_specs, out_specs=out_specs, out_shape=out_shape,
            scratch_shapes=list(scratch_shapes), compiler_params=_params(*semantics))(*args)
        return outs, []
    n_in, n_out, n_scr = len(in_specs), len(out_specs), len(scratch_shapes)

    total = math.prod(grid)
    first, middle, last = 0, (3 * total) // 4 - 1, total - 1
    assert first <= middle < last

    def at(step):
        lin = pl.program_id(0)
        for d in range(1, len(grid)):
            lin = lin * grid[d] + pl.program_id(d)
        return lin == step

    def hosted(*refs):
        h_in, c_in = refs[:n_in], refs[n_in:n_in + comm.n_in]
        rest = refs[n_in + comm.n_in:]
        h_out, c_out = rest[:n_out], rest[n_out:n_out + comm.n_out]
        rest = rest[n_out + comm.n_out:]
        h_scr, c_scr = rest[:n_scr], rest[n_scr:]

        @pl.when(at(first))
        def _():
            comm.start(c_in, c_out, c_scr)

        body(*h_in, *h_out, *h_scr)

        @pl.when(at(middle))
        def _():
            comm.mid(c_in, c_out, c_scr)

        @pl.when(at(last))
        def _():
            comm.end(c_in, c_out, c_scr)

    any_spec = pl.BlockSpec(memory_space=pl.ANY)
    outs = pl.pallas_call(
        hosted, name=name, grid=grid, in_specs=list(in_specs) + [any_spec] * comm.n_in,
        out_specs=list(out_specs) + [any_spec] * comm.n_out, out_shape=list(out_shape) + comm.out_shapes,
        scratch_shapes=list(scratch_shapes) + comm.scratch,
        compiler_params=_params(*(["arbitrary"] * len(grid))))(*args, *comm.inputs)
    return outs[:n_out], outs[n_out:]


def sum_swap(pieces, name, comm=None):
    nl = len(pieces)
    _, r2, cc = pieces[0].shape
    tr = 256 if r2 % 256 == 0 else (128 if r2 % 128 == 0 else r2 // 2)
    n = r2 // tr

    def body(*refs):
        p_refs, out = refs[:nl], refs[nl]
        slots, send_sems, local_sems, recv_sem = refs[nl + 1:]
        x, y, c = _me()
        sibling = (x, y, 1 - c)
        l, i = pl.program_id(0), pl.program_id(1)
        step = l * n + i

        def rows(st, core):
            return out.at[st // n, pl.ds(core * r2 + (st % n) * tr, tr), :]

        def copies(st):
            slot = st % 2
            local = pltpu.make_async_copy(slots.at[slot], rows(st, c), local_sems.at[slot])
            remote = pltpu.make_async_remote_copy(
                src_ref=slots.at[slot], dst_ref=rows(st, c), send_sem=send_sems.at[slot], recv_sem=recv_sem,
                device_id=sibling, device_id_type=MESH)
            return local, remote

        for ll in range(nl):
            @pl.when(l == ll)
            def _():
                acc = p_refs[ll][0].astype(F32)
                for d in range(1, N_DEV):
                    acc = acc + p_refs[ll][d].astype(F32)
                slots[step % 2] = acc

        for cp in copies(step):
            cp.start()

        @pl.when(step >= 1)
        def _():
            local, remote = copies(step - 1)
            local.wait()
            remote.wait_send()

        @pl.when(step == nl * n - 1)
        def _():
            local, remote = copies(step)
            local.wait()
            remote.wait_send()
            theirs = out.at[:, pl.ds((1 - c) * r2, r2), :]
            pltpu.make_async_remote_copy(src_ref=theirs, dst_ref=theirs, send_sem=send_sems.at[0],
                                         recv_sem=recv_sem, device_id=sibling, device_id_type=MESH).wait_recv()

    def piece_spec(ll):
        def index(l, i):
            return (0, jnp.where(l == ll, i, jnp.where(l < ll, 0, n - 1)), 0)
        return pl.BlockSpec((N_DEV, tr, cc), index)

    (out,), got = _call(
        body, name=name, grid=(nl, n),
        in_specs=[piece_spec(ll) for ll in range(nl)],
        out_specs=[pl.BlockSpec(memory_space=pl.ANY)],
        out_shape=[jax.ShapeDtypeStruct((nl, 2 * r2, cc), F32)],
        scratch_shapes=[pltpu.VMEM((2, tr, cc), F32), pltpu.SemaphoreType.DMA((2,)), pltpu.SemaphoreType.DMA((2,)),
                        pltpu.SemaphoreType.DMA(())],
        semantics=("arbitrary", "arbitrary"), args=tuple(pieces), comm=comm)
    return (out, got) if comm is not None else out


def sum_pieces(pieces, name):
    _, R, C = pieces.shape
    tr = _tile(R, 128) if R % 128 == 0 else R

    def body(p_ref, o_ref):
        acc = p_ref[0].astype(F32)
        for d in range(1, N_DEV):
            acc = acc + p_ref[d].astype(F32)
        o_ref[...] = acc

    return pl.pallas_call(
        body, name=name, grid=(R // tr,),
        in_specs=[pl.BlockSpec((N_DEV, tr, C), lambda i: (0, i, 0))],
        out_specs=pl.BlockSpec((tr, C), lambda i: (i, 0)),
        out_shape=jax.ShapeDtypeStruct((R, C), F32),
        compiler_params=_params("parallel"),
    )(pieces)


def adamw(w, g, m, v, name):
    Lyr, R, C = w.shape
    tr = _tile(R, 256) if R % 8 == 0 else R
    c1 = 1.0 / (1.0 - ADAM_B1 ** ADAM_STEP)
    c2 = 1.0 / (1.0 - ADAM_B2 ** ADAM_STEP)

    def body(w_ref, g_ref, m_ref, v_ref, d_ref, nm_ref, nv_ref):
        gv = g_ref[...]
        nm = ADAM_B1 * m_ref[...] + (1.0 - ADAM_B1) * gv
        nv = ADAM_B2 * v_ref[...] + (1.0 - ADAM_B2) * (gv * gv)
        nm_ref[...] = nm
        nv_ref[...] = nv
        d_ref[...] = -ADAM_LR * ((nm * c1) / (jnp.sqrt(nv * c2) + ADAM_EPS) + ADAM_WD * w_ref[...])

    spec = pl.BlockSpec((None, tr, C), lambda l, i: (l, i, 0))
    shp = jax.ShapeDtypeStruct(w.shape, F32)
    return pl.pallas_call(
        body, name=name, grid=(Lyr, R // tr),
        in_specs=[spec] * 4, out_specs=[spec] * 3, out_shape=[shp] * 3,
        compiler_params=_params("parallel", "parallel"),
    )(w, g, m, v)


def _rope_tables(T):
    pos = jnp.arange(T, dtype=F32)
    inv_freq = THETA ** (-jnp.arange(0, ROT, 2, dtype=F32) / ROT)
    ang = pos[:, None] * inv_freq[None, :]
    cs = jnp.concatenate([jnp.cos(ang), jnp.sin(ang)], axis=1)
    half = ROT // 2
    lane = jnp.arange(3 * LANES)
    table, lm = lane // LANES, lane % HD
    src = jnp.where(table == 0, lm % half, half + lm % half)
    i32 = lambda b: b.astype(jnp.int32)
    sign = jnp.where(table == 0, i32(lm < ROT), jnp.where(table == 1, -i32(lm < half), i32((lm >= half) & (lm < ROT))))
    place = (jnp.arange(ROT)[:, None] == src[None, :]) * sign[None, :].astype(F32)
    ones = ((table == 0) & (lm >= ROT)).astype(F32)
    return jnp.dot(cs, place, precision=lax.Precision.HIGHEST) + ones[None, :]


def _tab_specs(rows):
    return [pl.BlockSpec((rows, LANES), lambda i, k=k: (i, k)) for k in range(3)]


def kernel(x, attn_norm, attn_w_qkv, attn_w_o, attn_sink, conv_norm, conv_w_pw1, conv_b_pw1, conv_w_dw, conv_b_dw, conv_ln_g, conv_ln_b, conv_w_pw2, conv_b_pw2, ffn_norm, ffn_w_gu, ffn_w_down, final_norm, loss_target, m_attn_norm, m_attn_w_qkv, m_attn_w_o, m_attn_sink, m_conv_norm, m_conv_w_pw1, m_conv_b_pw1, m_conv_w_dw, m_conv_b_dw, m_conv_ln_g, m_conv_ln_b, m_conv_w_pw2, m_conv_b_pw2, m_ffn_norm, m_ffn_w_gu, m_ffn_w_down, m_final_norm, v_attn_norm, v_attn_w_qkv, v_attn_w_o, v_attn_sink, v_conv_norm, v_conv_w_pw1, v_conv_b_pw1, v_conv_w_dw, v_conv_b_dw, v_conv_ln_g, v_conv_ln_b, v_conv_w_pw2, v_conv_b_pw2, v_ffn_norm, v_ffn_w_gu, v_ffn_w_down, v_final_norm):
    T = x.shape[1]
    x0 = x[0]
    target = loss_target[0]
    ix, iy = lax.axis_index("x"), lax.axis_index("y")
    chip = 2 * ix + iy
    rc = rs1 = rs2 = _rope_tables(T)

    bf = lambda t: t.astype(BF16)

    def place(vec, width):
        return lax.dynamic_update_slice(jnp.zeros((vec.shape[0], N_CHIPS * width), F32), vec, (0, chip * width))

    small_rows = jnp.concatenate([
        place(conv_norm, 256), place(conv_b_pw1, 512).reshape(2, D), place(conv_b_dw, 256), place(conv_ln_g, 256),
        place(conv_ln_b, 256), place(conv_b_pw2, 256), jnp.zeros((1, D), F32),
        place(conv_w_dw[0], 256), jnp.zeros((1, D), F32)], axis=0)

    h0, (w_qkv,) = rms_first(x0, attn_norm, comm=_Gather([bf(attn_w_qkv[0])], [False]))
    qkv, (w_o, got) = qkv_proj(h0, w_qkv, rc, rs1, rs2,
                               comm=_Both(_Gather([bf(attn_w_o[0])], [True]), _Scatter([], small_rows)))
    psmall = sum_pieces(got, "sum_small_params") * 0.5
    p_conv_norm, p_b_pw1 = psmall[0:1], psmall[1:3].reshape(1, 2 * D)
    p_b_dw, p_ln_g, p_ln_b, p_b_pw2 = psmall[3:4], psmall[4:5], psmall[5:6], psmall[6:7]
    p_w_dw = psmall[8:40]
    sink = attn_sink[0]
    o, (w_gu0,) = attn_fwd(qkv, sink, comm=_Gather([bf(ffn_w_gu[0])], [False]))
    zero_b = jnp.zeros((1, D), F32)
    zero_gu = jnp.zeros((1, 2 * DFF), F32)
    x1, h1, gu0, act0, (w_down0, w_pw1, w_pw2) = rms_mm_gate(
        (o, w_o, zero_b, x0), ffn_norm[0:1], w_gu0, zero_gu, DFF, True, BF16, "ffn0_up",
        comm=_Gather([bf(ffn_w_down[0]), bf(conv_w_pw1[0]), bf(conv_w_pw2[0])], [True, False, True]))
    x2, h2, pre, glu, (w_down1,) = rms_mm_gate((act0, w_down0, zero_b, x1), p_conv_norm, w_pw1, p_b_pw1, D, False, F32,
                                               "conv_pw1", comm=_Gather([bf(ffn_w_down[1])], [True]))
    dwc, sw, (w_gu1,) = conv_fwd(glu, p_w_dw, p_b_dw, p_ln_g, p_ln_b, comm=_Gather([bf(ffn_w_gu[1])], [False]))
    x3, h3, gu1, act1, _ = rms_mm_gate((sw, w_pw2, p_b_pw2, x2), ffn_norm[1:2], w_gu1, zero_gu, DFF, True, BF16,
                                       "ffn1_up")
    dx4, loss_part, d_final = mm_res_loss(act1, w_down1, x3, final_norm.reshape(1, D), target)

    dgu1, _ = swiglu_bwd(dx4, w_down1, gu1, "ffn1_down_bwd")
    g_down1 = dw_row(act1, dx4, "ffn1_down_dw")
    dx3, d_ffn1, ddwc, d_ln_g, d_ln_b, d_b_pw2, _ = mm_bt_rmsbwd(
        dgu1, w_gu1, x3, ffn_norm[1:2], dx4, "ffn1_up_bwd", conv_tail=(w_pw2, dwc, p_ln_g, p_ln_b))
    g_gu1 = dw_col(h3, dgu1, "ffn1_up_dw")

    g_pw2 = dw_row(sw, dx3, "conv_pw2_dw")
    dpre, d_w_dw, d_b_dw, d_b_pw1, (r_gu1, r_down1) = conv_bwd(ddwc, glu, pre, p_w_dw,
                                                               comm=_Scatter([g_gu1, g_down1]))
    dx2, d_conv_norm, _ = mm_bt_rmsbwd(dpre, w_pw1, x2, p_conv_norm, dx3, "conv_pw1_bwd")
    g_pw1 = dw_col(h2, dpre, "conv_pw1_dw")

    dgu0, (r_pw1, r_pw2) = swiglu_bwd(dx2, w_down0, gu0, "ffn0_down_bwd", comm=_Scatter([g_pw1, g_pw2]))
    g_down0 = dw_row(act0, dx2, "ffn0_down_dw")
    dx1, d_ffn0, do, _ = mm_bt_rmsbwd(dgu0, w_gu0, x1, ffn_norm[0:1], dx2, "ffn0_up_bwd", proj_w=w_o)
    g_gu0 = dw_col(h1, dgu0, "ffn0_up_dw")

    g_o = dw_row(o, dx1, "attn_out_dw")
    dq, dkc, dvc, d_sink, (r_gu0, r_down0, r_o) = attn_bwd(qkv, o, do, sink, rc, rs1, rs2,
                                                           comm=_Scatter([g_gu0, g_down0, g_o]))
    dqkv = kv_sum(dq, dkc, dvc, rc, rs1, rs2)[None]
    g_qkv = dw_col(h0, dqkv, "attn_qkv_dw")
    dx0, d_attn_norm, _ = mm_bt_rmsbwd(dqkv, w_qkv, x0, attn_norm, dx1, "attn_qkv_bwd")

    pad16 = lambda t: jnp.concatenate([t, jnp.zeros((1, D - t.shape[1]), F32)], axis=1)
    small_g = jnp.concatenate([
        d_attn_norm, pad16(d_sink), d_conv_norm, d_b_pw1.reshape(2, D), d_b_dw, d_ln_g, d_ln_b, d_b_pw2,
        d_ffn0, d_ffn1, d_final, pad16(loss_part), jnp.zeros((3, D), F32), d_w_dw], axis=0)
    gf_gu, (r_qkv, r_small) = sum_swap([r_gu0, r_gu1], "sum_gu", comm=_Scatter([g_qkv], small_g))
    gf_down = sum_swap([r_down0, r_down1], "sum_down")
    gf_pw1, gf_pw2 = sum_swap([r_pw1], "sum_pw1"), sum_swap([r_pw2], "sum_pw2")
    gf_qkv, gf_o = sum_swap([r_qkv], "sum_qkv"), sum_swap([r_o], "sum_o")
    gs = sum_pieces(r_small, "sum_small_grads")
    loss = gs[12, 0]

    def take(row0, nrows, width):
        return lax.dynamic_slice(gs, (row0, chip * width), (nrows, width))

    grads = {
        "attn_norm": gs[0:1], "attn_w_qkv": gf_qkv, "attn_w_o": gf_o, "attn_sink": gs[1:2, :N_HEADS],
        "conv_norm": take(2, 1, 256), "conv_w_pw1": gf_pw1,
        "conv_b_pw1": lax.dynamic_slice(gs[3:5].reshape(1, 2 * D), (0, chip * 512), (1, 512)),
        "conv_w_dw": take(16, 32, 256)[None, :CONV_W], "conv_b_dw": take(5, 1, 256), "conv_ln_g": take(6, 1, 256),
        "conv_ln_b": take(7, 1, 256), "conv_w_pw2": gf_pw2, "conv_b_pw2": take(8, 1, 256),
        "ffn_norm": gs[9:11], "ffn_w_gu": gf_gu, "ffn_w_down": gf_down, "final_norm": gs[11],
    }
    weights = dict(attn_norm=attn_norm, attn_w_qkv=attn_w_qkv, attn_w_o=attn_w_o, attn_sink=attn_sink,
                   conv_norm=conv_norm, conv_w_pw1=conv_w_pw1, conv_b_pw1=conv_b_pw1, conv_w_dw=conv_w_dw,
                   conv_b_dw=conv_b_dw, conv_ln_g=conv_ln_g, conv_ln_b=conv_ln_b, conv_w_pw2=conv_w_pw2,
                   conv_b_pw2=conv_b_pw2, ffn_norm=ffn_norm, ffn_w_gu=ffn_w_gu, ffn_w_down=ffn_w_down,
                   final_norm=final_norm)
    m_in = dict(attn_norm=m_attn_norm, attn_w_qkv=m_attn_w_qkv, attn_w_o=m_attn_w_o, attn_sink=m_attn_sink,
                conv_norm=m_conv_norm, conv_w_pw1=m_conv_w_pw1, conv_b_pw1=m_conv_b_pw1, conv_w_dw=m_conv_w_dw,
                conv_b_dw=m_conv_b_dw, conv_ln_g=m_conv_ln_g, conv_ln_b=m_conv_ln_b, conv_w_pw2=m_conv_w_pw2,
                conv_b_pw2=m_conv_b_pw2, ffn_norm=m_ffn_norm, ffn_w_gu=m_ffn_w_gu, ffn_w_down=m_ffn_w_down,
                final_norm=m_final_norm)
    v_in = dict(attn_norm=v_attn_norm, attn_w_qkv=v_attn_w_qkv, attn_w_o=v_attn_w_o, attn_sink=v_attn_sink,
                conv_norm=v_conv_norm, conv_w_pw1=v_conv_w_pw1, conv_b_pw1=v_conv_b_pw1, conv_w_dw=v_conv_w_dw,
                conv_b_dw=v_conv_b_dw, conv_ln_g=v_conv_ln_g, conv_ln_b=v_conv_ln_b, conv_w_pw2=v_conv_w_pw2,
                conv_b_pw2=v_conv_b_pw2, ffn_norm=v_ffn_norm, ffn_w_gu=v_ffn_w_gu, ffn_w_down=v_ffn_w_down,
                final_norm=v_final_norm)
    order = list(weights)
    g_out, d_out, m_out, v_out = [], [], [], []
    for nm in order:
        w = weights[nm]
        shape = w.shape
        as3 = lambda t: t.reshape((1,) * (3 - len(shape)) + shape) if len(shape) < 3 else t.reshape(shape)
        g3 = as3(grads[nm].reshape(shape))
        delta, nm_, nv_ = adamw(as3(w), g3, as3(m_in[nm]), as3(v_in[nm]), "adamw_" + nm)
        g_out.append(g3.reshape(shape))
        d_out.append(delta.reshape(shape))
        m_out.append(nm_.reshape(shape))
        v_out.append(nv_.reshape(shape))
    return (loss, dx0[None], *g_out, *d_out, *m_out, *v_out)
```

```python
import math

import jax
import jax.numpy as jnp
from jax import lax
from jax.experimental import pallas as pl
from jax.experimental.pallas import tpu as pltpu

F32 = jnp.float32
BF16 = jnp.bfloat16

D = 1024
N_HEADS = 16
N_KV = 4
GROUP = N_HEADS // N_KV
HD = 64
ROT = 16
THETA = 500000.0
BLK = 128
QKV = (N_HEADS + 2 * N_KV) * HD
KV_OFF = N_HEADS * HD
DFF = 2816
CONV_W = 31
CONV_PAD = 15
HALO = 16
CONV_JB = 16
CONV_JB_BWD = 16
EPS = 1e-6
NEG = -1e30
N_CHIPS = 4
N_DEV = 8
LANES = 128
SUBLANES = 8

ADAM_LR, ADAM_B1, ADAM_B2, ADAM_EPS, ADAM_WD, ADAM_STEP = 0.001, 0.9, 0.999, 1e-08, 0.01, 10

VMEM_LIMIT = 56 * 1024 * 1024
MESH = pl.DeviceIdType.MESH


def _params(*sem):
    return pltpu.CompilerParams(dimension_semantics=sem, vmem_limit_bytes=VMEM_LIMIT)


def _tile(n, want):
    if n <= want:
        return n
    for t in range(want, 7, -1):
        if n % t == 0 and t % 8 == 0:
            return t
    return n


MXU_COLS = 256


def _col_chunks(n):
    return [slice(c, min(c + MXU_COLS, n)) for c in range(0, n, MXU_COLS)]


def _sigmoid(v):
    return jax.nn.sigmoid(v)


def _rms_fwd(xv, gain):
    r = lax.rsqrt(jnp.mean(xv * xv, axis=-1, keepdims=True) + EPS)
    return xv * r * gain


def _rms_bwd(dh, xv, gain, dres):
    r = lax.rsqrt(jnp.mean(xv * xv, axis=-1, keepdims=True) + EPS)
    xhat = xv * r
    gy = dh * gain
    dx = r * (gy - xhat * jnp.mean(gy * xhat, axis=-1, keepdims=True))
    return dx + dres, dh * xhat


def _rope(blk, c, s1, s2):
    return blk * c + pltpu.roll(blk, LANES - ROT // 2, 1) * s1 + pltpu.roll(blk, ROT // 2, 1) * s2


def _dot(a, b):
    return jnp.dot(a, b, preferred_element_type=F32)


def _dot_tb(a, b):
    return lax.dot_general(a, b, (((1,), (1,)), ((), ())), preferred_element_type=F32)


def _dot_ta(a, b):
    return lax.dot_general(a, b, (((0,), (0,)), ((), ())), preferred_element_type=F32)


def rms_first(x, gain, comm):
    T = x.shape[0]
    tm = _tile(T, 512)

    def body(x_ref, g_ref, h_ref):
        h_ref[...] = _rms_fwd(x_ref[...], g_ref[...]).astype(BF16)

    (h,), got = _call(
        body, name="rms_first", grid=(T // tm,),
        in_specs=[pl.BlockSpec((tm, D), lambda i: (i, 0)), pl.BlockSpec((1, D), lambda i: (0, 0))],
        out_specs=[pl.BlockSpec((tm, D), lambda i: (i, 0))], out_shape=[jax.ShapeDtypeStruct((T, D), BF16)],
        semantics=("parallel",), args=(x, gain), comm=comm)
    return h, got


def qkv_proj(h, w, rc, rs1, rs2, comm=None):
    T = h.shape[0]
    tm = _tile(T, 512)

    def body(h_ref, w_ref, c_ref, s1_ref, s2_ref, qkv_ref):
        acc = _dot(h_ref[...], w_ref[...])
        c, s1, s2 = c_ref[...], s1_ref[...], s2_ref[...]
        n_rot = (KV_OFF + N_KV * HD) // LANES
        for j in range(n_rot):
            sl = slice(LANES * j, LANES * (j + 1))
            roped = _rope(acc[:, sl], c, s1, s2)
            if j < KV_OFF // LANES:
                roped = roped * Q_SCALE
            qkv_ref[:, sl] = roped.astype(BF16)
        qkv_ref[:, n_rot * LANES:] = acc[:, n_rot * LANES:].astype(BF16)

    row = lambda i: (i, 0)
    full = lambda i: (0, 0)
    (qkv,), got = _call(
        body, name="qkv_proj", grid=(T // tm,),
        in_specs=[pl.BlockSpec((tm, D), row), pl.BlockSpec((D, QKV), full), *_tab_specs(tm)],
        out_specs=[pl.BlockSpec((tm, QKV), row)],
        out_shape=[jax.ShapeDtypeStruct((T, QKV), BF16)],
        semantics=("parallel",), args=(h, w, rc, rs1, rs2), comm=comm)
    return qkv, got


Q_SCALE = 1.0 / math.sqrt(HD)


def _attn_mask(n, T):
    ci = lax.broadcasted_iota(jnp.int32, (3 * BLK, BLK), 0)
    qi = lax.broadcasted_iota(jnp.int32, (3 * BLK, BLK), 1)
    key_pos = n * BLK - BLK + ci
    return (jnp.abs(ci - BLK - qi) <= BLK) & (key_pos >= 0) & (key_pos < T)


def _kv_padded(kv, first_tile):
    low = lax.broadcasted_iota(jnp.int32, (3 * BLK, LANES), 1) < HD
    zero = jnp.zeros((3 * BLK, LANES), BF16)
    out = {}
    for g in range(N_KV):
        t = kv[:, (first_tile + g // 2) * LANES:(first_tile + g // 2 + 1) * LANES]
        swapped = jnp.concatenate([t[:, HD:], t[:, :HD]], axis=1)
        for p in range(2):
            out[g, p] = jnp.where(low if p == 0 else ~low, t if g % 2 == p else swapped, zero)
    return out


def _pair_products(kvx, tile_of):
    both = {g: jnp.concatenate([kvx[g, 0], kvx[g, 1]], axis=0) for g in range(N_KV)}
    out = []
    for j in range(N_HEADS // 2):
        prod = _dot_tb(both[2 * j // GROUP], tile_of(j))
        out += [prod[:3 * BLK], prod[3 * BLK:]]
    return out


def _softmax_sink(s, valid, sk):
    s = jnp.where(valid, s, NEG)
    m = jnp.maximum(jnp.max(s, axis=0, keepdims=True), sk)
    e = jnp.exp(s - m)
    es = jnp.exp(sk - m)
    inv = 1.0 / (jnp.sum(e, axis=0, keepdims=True) + es)
    return e * inv, es * inv


def _attn_specs(T):
    nb = T // BLK
    kv_blk = 2 * N_KV * HD
    kv_col = KV_OFF // kv_blk
    q_spec = pl.BlockSpec((BLK, KV_OFF), lambda n: (n, 0))
    prev = pl.BlockSpec((BLK, kv_blk), lambda n: (jnp.maximum(n - 1, 0), kv_col))
    own = pl.BlockSpec((BLK, kv_blk), lambda n: (n, kv_col))
    nxt = pl.BlockSpec((BLK, kv_blk), lambda n: (jnp.minimum(n + 1, nb - 1), kv_col))
    return nb, q_spec, prev, own, nxt


def attn_fwd(qkv, sink, comm=None):
    T = qkv.shape[0]
    nb, q_spec, prev, own, nxt = _attn_specs(T)

    def body(sink_ref, q_ref, kp_ref, ko_ref, kn_ref, o_ref):
        valid = _attn_mask(pl.program_id(0), T)
        kv = jnp.concatenate([kp_ref[...], ko_ref[...], kn_ref[...]], axis=0)
        kx, vx = _kv_padded(kv, 0), _kv_padded(kv, 2)
        ss = _pair_products(kx, lambda j: q_ref[:, j * LANES:(j + 1) * LANES])
        ps = [_softmax_sink(ss[h], valid, sink_ref[h])[0].astype(BF16) for h in range(N_HEADS)]
        vxt = {k: v.T for k, v in vx.items()}
        for j in range(N_HEADS // 2):
            g = 2 * j // GROUP
            o_t = _dot(vxt[g, 0], ps[2 * j]) + _dot(vxt[g, 1], ps[2 * j + 1])
            o_ref[:, j * LANES:(j + 1) * LANES] = o_t.T.astype(BF16)

    (o,), got = _call(
        body, name="attn_fwd", grid=(nb,),
        in_specs=[pl.BlockSpec(memory_space=pltpu.SMEM), q_spec, prev, own, nxt],
        out_specs=[pl.BlockSpec((BLK, D), lambda n: (n, 0))],
        out_shape=[jax.ShapeDtypeStruct((T, D), BF16)],
        semantics=("parallel",), args=(sink, qkv, qkv, qkv, qkv), comm=comm)
    return o, got


def rms_mm_gate(x, gain, w, bias, H, swiglu, act_dtype, name, comm=None):
    fused = isinstance(x, tuple)
    T = (x[0] if fused else x).shape[0]
    tm = _tile(T, 512)

    def body(*refs):
        if fused:
            a_ref, wp_ref, bp_ref, r_ref, g_ref, w_ref, b_ref, x_ref, h_ref, pre_ref, act_ref = refs
            xv = _dot(a_ref[...], wp_ref[...]) + bp_ref[...] + r_ref[...]
            x_ref[...] = xv
        else:
            x_ref, g_ref, w_ref, b_ref, h_ref, pre_ref, act_ref = refs
            xv = x_ref[...]
        h = _rms_fwd(xv, g_ref[...]).astype(BF16)
        h_ref[...] = h
        for cs in _col_chunks(H):
            cs2 = slice(H + cs.start, H + cs.stop)
            a = _dot(h, w_ref[:, cs]) + b_ref[:, cs]
            b = _dot(h, w_ref[:, cs2]) + b_ref[:, cs2]
            pre_ref[0, :, cs] = a.astype(BF16)
            pre_ref[1, :, cs] = b.astype(BF16)
            if swiglu:
                act = a * _sigmoid(a) * b
            else:
                act = a * _sigmoid(b)
            act_ref[:, cs] = act.astype(act_dtype)

    row = lambda i: (i, 0)
    full = lambda i: (0, 0)
    if fused:
        K = x[0].shape[1]
        x_specs = [pl.BlockSpec((tm, K), row), pl.BlockSpec((K, D), full, pipeline_mode=pl.Buffered(1)),
                   pl.BlockSpec((1, D), full), pl.BlockSpec((tm, D), row)]
        x_out = ([pl.BlockSpec((tm, D), row)], [jax.ShapeDtypeStruct((T, D), F32)])
        x_args = tuple(x)
    else:
        x_specs, x_out, x_args = [pl.BlockSpec((tm, D), row)], ([], []), (x,)
    outs, got = _call(
        body, name=name, grid=(T // tm,),
        in_specs=x_specs + [pl.BlockSpec((1, D), full),
                            pl.BlockSpec((D, 2 * H), full, pipeline_mode=pl.Buffered(1)), pl.BlockSpec((1, 2 * H), full)],
        out_specs=x_out[0] + [pl.BlockSpec((tm, D), row), pl.BlockSpec((2, tm, H), lambda i: (0, i, 0)),
                              pl.BlockSpec((tm, H), row)],
        out_shape=x_out[1] + [jax.ShapeDtypeStruct((T, D), BF16), jax.ShapeDtypeStruct((2, T, H), BF16),
                              jax.ShapeDtypeStruct((T, H), act_dtype)],
        semantics=("parallel",), args=x_args + (gain, w, bias), comm=comm)
    return (*outs, got)


def _conv_tiles(T):
    tt = _tile(T, 512)
    return tt, tt // SUBLANES, D // LANES


def _fill_strided(ext, p, L):
    main = p[HALO:HALO + SUBLANES * L, :].reshape(SUBLANES, L, LANES)
    ext[CONV_PAD:CONV_PAD + L] = jnp.swapaxes(main, 0, 1)

    def ibody(i, carry):
        ext[i] = p[pl.ds(i + 1, SUBLANES, stride=L), :]
        ext[i + CONV_PAD + L] = p[pl.ds(i + CONV_PAD + L + 1, SUBLANES, stride=L), :]
        return carry

    lax.fori_loop(0, CONV_PAD, ibody, 0, unroll=3)


def _conv_specs(T, tt):
    main = pl.BlockSpec((tt, D), lambda i: (i, 0))
    per = tt // HALO
    prev = pl.BlockSpec((HALO, D), lambda i: (jnp.maximum(i * per - 1, 0), 0))
    nxt = pl.BlockSpec((HALO, D), lambda i: (jnp.minimum((i + 1) * per, T // HALO - 1), 0))
    return main, prev, nxt


def _fill_pad(pad, main_ref, prev_ref, next_ref, i, n_i, tt, nlt):
    keep_p = (i > 0).astype(F32)
    keep_n = (i < n_i - 1).astype(F32)
    for lt in range(nlt):
        sl = slice(lt * LANES, (lt + 1) * LANES)
        pad[lt, 0:HALO, :] = prev_ref[:, sl] * keep_p
        pad[lt, HALO:HALO + tt, :] = main_ref[:, sl]
        pad[lt, HALO + tt:2 * HALO + tt, :] = next_ref[:, sl] * keep_n


def conv_fwd(glu, w_dw, b_dw, ln_g, ln_b, comm=None):
    T = glu.shape[0]
    tt, L, nlt = _conv_tiles(T)
    n_i = T // tt
    main, prev, nxt = _conv_specs(T, tt)

    def body(x_ref, xp_ref, xn_ref, w_ref, b_ref, g_ref, bb_ref, dwc_ref, sw_ref, pad, ob, ext, wk):
        i = pl.program_id(0)
        _fill_pad(pad, x_ref, xp_ref, xn_ref, i, n_i, tt, nlt)
        for lt in range(nlt):
            sl = slice(lt * LANES, (lt + 1) * LANES)
            o = ob.at[lt]
            _fill_strided(ext, pad.at[lt], L)
            for k in range(CONV_W):
                wk[k] = jnp.broadcast_to(w_ref[k:k + 1, sl], (SUBLANES, LANES))

            def jbody(jb, carry):
                j = jb * CONV_JB
                accs = [None] * CONV_JB
                for m in range(CONV_W + CONV_JB - 1):
                    e = ext[j + m]
                    for u in range(CONV_JB):
                        if 0 <= m - u < CONV_W:
                            t = e * wk[m - u]
                            accs[u] = t if accs[u] is None else accs[u] + t
                for u in range(CONV_JB):
                    o[pl.ds(j + u, SUBLANES, stride=L), :] = accs[u]
                return carry

            lax.fori_loop(0, L // CONV_JB, jbody, 0)
        y = jnp.concatenate([ob[lt] for lt in range(nlt)], axis=1) + b_ref[...]
        dwc_ref[...] = y
        mu = jnp.mean(y, axis=-1, keepdims=True)
        yc = y - mu
        var = jnp.mean(yc * yc, axis=-1, keepdims=True)
        z = yc * lax.rsqrt(var + EPS) * g_ref[...] + bb_ref[...]
        sw_ref[...] = (z * _sigmoid(z)).astype(BF16)

    full = lambda i: (0, 0)
    (dwc, sw), got = _call(
        body, name="conv_fwd", grid=(n_i,),
        in_specs=[main, prev, nxt, pl.BlockSpec((32, D), full), pl.BlockSpec((1, D), full),
                  pl.BlockSpec((1, D), full), pl.BlockSpec((1, D), full)],
        out_specs=[pl.BlockSpec((tt, D), lambda i: (i, 0)), pl.BlockSpec((tt, D), lambda i: (i, 0))],
        out_shape=[jax.ShapeDtypeStruct((T, D), F32), jax.ShapeDtypeStruct((T, D), BF16)],
        scratch_shapes=[pltpu.VMEM((nlt, tt + 2 * HALO, LANES), F32), pltpu.VMEM((nlt, tt, LANES), F32),
                        pltpu.VMEM((L + 2 * HALO, SUBLANES, LANES), F32), pltpu.VMEM((32, SUBLANES, LANES), F32)],
        semantics=("parallel",), args=(glu, glu, glu, w_dw, b_dw, ln_g, ln_b), comm=comm)
    return dwc, sw, got


def mm_res_loss(a, w, resid, gain, target):
    T, K = a.shape
    tm = _tile(T, 512)

    def body(a_ref, w_ref, r_ref, g_ref, t_ref, dx_ref, loss_ref, dg_ref):
        @pl.when(pl.program_id(0) == 0)
        def _():
            loss_ref[...] = jnp.zeros_like(loss_ref)
            dg_ref[...] = jnp.zeros_like(dg_ref)

        xv, gain_v = _dot(a_ref[...], w_ref[...]) + r_ref[...], g_ref[...]
        err = _rms_fwd(xv, gain_v) - t_ref[...]
        part = 0.5 * jnp.sum(jnp.mean(err * err, axis=-1, keepdims=True), axis=0, keepdims=True)
        loss_ref[...] += jnp.broadcast_to(part, loss_ref.shape)
        dx, dgr = _rms_bwd(err * (1.0 / D), xv, gain_v, 0.0)
        dx_ref[...] = dx
        dg_ref[...] += jnp.sum(dgr, axis=0, keepdims=True)

    row = lambda i: (i, 0)
    full = lambda i: (0, 0)
    return pl.pallas_call(
        body, name="ffn1_down_loss", grid=(T // tm,),
        in_specs=[pl.BlockSpec((tm, K), row), pl.BlockSpec((K, D), full), pl.BlockSpec((tm, D), row),
                  pl.BlockSpec((1, D), full), pl.BlockSpec((tm, D), row)],
        out_specs=[pl.BlockSpec((tm, D), row), pl.BlockSpec((1, LANES), full), pl.BlockSpec((1, D), full)],
        out_shape=[jax.ShapeDtypeStruct((T, D), F32), jax.ShapeDtypeStruct((1, LANES), F32),
                   jax.ShapeDtypeStruct((1, D), F32)],
        compiler_params=_params("arbitrary"),
    )(a, w, resid, gain, target)


def swiglu_bwd(dx, w_down, pre, name, comm=None):
    T = dx.shape[0]
    H = w_down.shape[0]
    tm = _tile(T, 512)

    def body(dx_ref, w_ref, pre_ref, dpre_ref):
        dxb = dx_ref[...].astype(BF16)
        for cs in _col_chunks(H):
            dact = _dot_tb(dxb, w_ref[cs, :])
            g = pre_ref[0, :, cs].astype(F32)
            u = pre_ref[1, :, cs].astype(F32)
            sg = _sigmoid(g)
            dpre_ref[0, :, cs] = (dact * u * sg * (1.0 + g * (1.0 - sg))).astype(BF16)
            dpre_ref[1, :, cs] = (dact * g * sg).astype(BF16)

    (dpre,), got = _call(
        body, name=name, grid=(T // tm,),
        in_specs=[pl.BlockSpec((tm, D), lambda i: (i, 0)),
                  pl.BlockSpec((H, D), lambda i: (0, 0), pipeline_mode=pl.Buffered(1)),
                  pl.BlockSpec((2, tm, H), lambda i: (0, i, 0))],
        out_specs=[pl.BlockSpec((2, tm, H), lambda i: (0, i, 0))],
        out_shape=[jax.ShapeDtypeStruct((2, T, H), BF16)],
        semantics=("parallel",), args=(dx, w_down, pre), comm=comm)
    return dpre, got


def _ln_silu_bwd(dsw, y, ln_g, ln_b):
    mu = jnp.mean(y, axis=-1, keepdims=True)
    yc = y - mu
    rstd = lax.rsqrt(jnp.mean(yc * yc, axis=-1, keepdims=True) + EPS)
    xhat = yc * rstd
    z = xhat * ln_g + ln_b
    sg = _sigmoid(z)
    dz = dsw * sg * (1.0 + z * (1.0 - sg))
    dxh = dz * ln_g
    dy = rstd * (dxh - jnp.mean(dxh, axis=-1, keepdims=True) - xhat * jnp.mean(dxh * xhat, axis=-1, keepdims=True))
    return dy, dz * xhat, dz


def mm_bt_rmsbwd(dpre, w, x, gain, dres, name, comm=None, proj_w=None, conv_tail=None):
    nh, T, H = dpre.shape
    tm = _tile(T, 512)
    n_extra_in = 1 if proj_w is not None else (4 if conv_tail is not None else 0)

    def body(*refs):
        dp_ref, w_ref, x_ref, g_ref, dres_ref = refs[:5]
        extra_in = refs[5:5 + n_extra_in]
        dx_ref, dg_ref = refs[5 + n_extra_in:7 + n_extra_in]
        extra_out = refs[7 + n_extra_in:]

        @pl.when(pl.program_id(0) == 0)
        def _():
            dg_ref[...] = jnp.zeros_like(dg_ref)
            for r in extra_out[1:]:
                r[...] = jnp.zeros_like(r)

        dh = _dot_tb(dp_ref[0], w_ref[:, 0:H])
        for hf in range(1, nh):
            dh = dh + _dot_tb(dp_ref[hf], w_ref[:, hf * H:(hf + 1) * H])
        dx, dgr = _rms_bwd(dh, x_ref[...], g_ref[...], dres_ref[...])
        dx_ref[...] = dx
        dg_ref[...] += jnp.sum(dgr, axis=0, keepdims=True)
        if proj_w is not None:
            extra_out[0][...] = _dot_tb(dx.astype(BF16), extra_in[0][...]).astype(BF16)
        elif conv_tail is not None:
            wt_ref, y_ref, lg_ref, lb_ref = extra_in
            dy, dgl, dbl = _ln_silu_bwd(_dot_tb(dx.astype(BF16), wt_ref[...]), y_ref[...], lg_ref[...], lb_ref[...])
            extra_out[0][...] = dy
            extra_out[1][...] += jnp.sum(dgl, axis=0, keepdims=True)
            extra_out[2][...] += jnp.sum(dbl, axis=0, keepdims=True)
            extra_out[3][...] += jnp.sum(dx, axis=0, keepdims=True)

    row = lambda i: (i, 0)
    full = lambda i: (0, 0)
    vec = pl.BlockSpec((1, D), full)
    vec_shape = jax.ShapeDtypeStruct((1, D), F32)
    in_specs = [pl.BlockSpec((nh, tm, H), lambda i: (0, i, 0)),
                pl.BlockSpec((D, nh * H), full, pipeline_mode=pl.Buffered(1)),
                pl.BlockSpec((tm, D), row), vec, pl.BlockSpec((tm, D), row)]
    out_specs = [pl.BlockSpec((tm, D), row), vec]
    out_shape = [jax.ShapeDtypeStruct((T, D), F32), vec_shape]
    args = (dpre, w, x, gain, dres)
    if proj_w is not None:
        N = proj_w.shape[0]
        in_specs.append(pl.BlockSpec((N, D), full, pipeline_mode=pl.Buffered(1)))
        out_specs.append(pl.BlockSpec((tm, N), row))
        out_shape.append(jax.ShapeDtypeStruct((T, N), BF16))
        args += (proj_w,)
    elif conv_tail is not None:
        in_specs += [pl.BlockSpec((D, D), full, pipeline_mode=pl.Buffered(1)), pl.BlockSpec((tm, D), row), vec, vec]
        out_specs += [pl.BlockSpec((tm, D), row), vec, vec, vec]
        out_shape += [jax.ShapeDtypeStruct((T, D), F32), vec_shape, vec_shape, vec_shape]
        args += tuple(conv_tail)
    outs, got = _call(body, name=name, grid=(T // tm,), in_specs=in_specs, out_specs=out_specs, out_shape=out_shape,
                      semantics=("arbitrary",), args=args, comm=comm)
    return (*outs, got)


def dw_col(a, dpre, name):
    T = a.shape[0]
    nh, _, H = dpre.shape
    per = nh * H // N_CHIPS
    bph = N_CHIPS // nh
    tt = _tile(T, 2048)
    nt = T // tt

    def body(a_ref, b_ref, o_ref, acc):
        t = pl.program_id(1)

        @pl.when(t == 0)
        def _():
            acc[...] = jnp.zeros_like(acc)

        acc[...] += _dot_ta(a_ref[...], b_ref[...])

        @pl.when(t == nt - 1)
        def _():
            o_ref[...] = acc[...].astype(BF16)

    return pl.pallas_call(
        body, name=name, grid=(N_CHIPS, nt),
        in_specs=[pl.BlockSpec((tt, D), lambda q, t: (t, 0)),
                  pl.BlockSpec((None, tt, per), lambda q, t: (q // bph, t, q % bph))],
        out_specs=pl.BlockSpec((None, D, per), lambda q, t: (q, 0, 0)),
        out_shape=jax.ShapeDtypeStruct((N_CHIPS, D, per), BF16),
        scratch_shapes=[pltpu.VMEM((D, per), F32)],
        compiler_params=_params("parallel", "arbitrary"),
    )(a, dpre)


def dw_row(a, b, name):
    T, R = a.shape
    cw = 1408 if R % 1408 == 0 else R
    tt = _tile(T, 1024)
    nt = T // tt

    def body(a_ref, b_ref, o_ref, acc):
        t = pl.program_id(1)

        @pl.when(t == 0)
        def _():
            acc[...] = jnp.zeros_like(acc)

        acc[...] += _dot_ta(a_ref[...], b_ref[...].astype(BF16))

        @pl.when(t == nt - 1)
        def _():
            o_ref[...] = acc[...].astype(BF16)

    out = pl.pallas_call(
        body, name=name, grid=(R // cw, nt),
        in_specs=[pl.BlockSpec((tt, cw), lambda q, t: (t, q)), pl.BlockSpec((tt, D), lambda q, t: (t, 0))],
        out_specs=pl.BlockSpec((cw, D), lambda q, t: (q, 0)),
        out_shape=jax.ShapeDtypeStruct((R, D), BF16),
        scratch_shapes=[pltpu.VMEM((cw, D), F32)],
        compiler_params=_params("parallel", "arbitrary"),
    )(a, b)
    return out.reshape(N_CHIPS, R // N_CHIPS, D)


def conv_bwd(ddwc, glu, pre, w_dw, comm=None):
    T = ddwc.shape[0]
    tt, L, nlt = _conv_tiles(T)
    n_i = T // tt
    main, prev, nxt = _conv_specs(T, tt)

    def body(d_ref, dp_ref, dn_ref, x_ref, xp_ref, xn_ref, pre_ref, w_ref,
             dpre_ref, dw_ref, dbd_ref, dbp_ref, padd, padx, ob, extd, extx, wk):
        i = pl.program_id(0)

        @pl.when(i == 0)
        def _():
            dw_ref[...] = jnp.zeros_like(dw_ref)
            dbd_ref[...] = jnp.zeros_like(dbd_ref)
            dbp_ref[...] = jnp.zeros_like(dbp_ref)

        _fill_pad(padd, d_ref, dp_ref, dn_ref, i, n_i, tt, nlt)
        _fill_pad(padx, x_ref, xp_ref, xn_ref, i, n_i, tt, nlt)
        for lt in range(nlt):
            sl = slice(lt * LANES, (lt + 1) * LANES)
            o = ob.at[lt]
            _fill_strided(extd, padd.at[lt], L)
            _fill_strided(extx, padx.at[lt], L)
            for k in range(CONV_W):
                wk[k] = jnp.broadcast_to(w_ref[k:k + 1, sl], (SUBLANES, LANES))

            nu = CONV_JB_BWD

            def jbody(jb, accs):
                j = jb * nu
                accs = list(accs)
                d = [extd[j + u + CONV_PAD] for u in range(nu)]
                g = [None] * nu
                for m in range(CONV_W + nu - 1):
                    ed = extd[j + 2 * CONV_PAD + nu - 1 - m]
                    ex = extx[j + m]
                    for u in range(nu):
                        k = m - (nu - 1 - u)
                        if 0 <= k < CONV_W:
                            t = ed * wk[k]
                            g[u] = t if g[u] is None else g[u] + t
                        k = m - u
                        if 0 <= k < CONV_W:
                            accs[k] = accs[k] + d[u] * ex
                for u in range(nu):
                    o[pl.ds(j + u, SUBLANES, stride=L), :] = g[u]
                return tuple(accs)

            accs = lax.fori_loop(0, L // nu, jbody, tuple(jnp.zeros((SUBLANES, LANES), F32) for _ in range(CONV_W)))
            for k in range(CONV_W):
                dw_ref[k:k + 1, sl] += jnp.sum(accs[k], axis=0, keepdims=True)
        dglu = jnp.concatenate([ob[lt] for lt in range(nlt)], axis=1)
        a = pre_ref[0].astype(F32)
        gate = pre_ref[1].astype(F32)
        sg = _sigmoid(gate)
        da = dglu * sg
        dgate = dglu * a * sg * (1.0 - sg)
        dpre_ref[0] = da.astype(BF16)
        dpre_ref[1] = dgate.astype(BF16)
        dbd_ref[...] += jnp.sum(d_ref[...], axis=0, keepdims=True)
        dbp_ref[0] += jnp.sum(da, axis=0, keepdims=True)
        dbp_ref[1] += jnp.sum(dgate, axis=0, keepdims=True)

    full = lambda i: (0, 0)
    (dpre, dw, dbd, dbp), got = _call(
        body, name="conv_bwd", grid=(n_i,),
        in_specs=[main, prev, nxt, main, prev, nxt, pl.BlockSpec((2, tt, D), lambda i: (0, i, 0)),
                  pl.BlockSpec((32, D), full)],
        out_specs=[pl.BlockSpec((2, tt, D), lambda i: (0, i, 0)), pl.BlockSpec((32, D), full),
                   pl.BlockSpec((1, D), full), pl.BlockSpec((2, 1, D), lambda i: (0, 0, 0))],
        out_shape=[jax.ShapeDtypeStruct((2, T, D), BF16), jax.ShapeDtypeStruct((32, D), F32),
                   jax.ShapeDtypeStruct((1, D), F32), jax.ShapeDtypeStruct((2, 1, D), F32)],
        scratch_shapes=[pltpu.VMEM((nlt, tt + 2 * HALO, LANES), F32), pltpu.VMEM((nlt, tt + 2 * HALO, LANES), F32),
                        pltpu.VMEM((nlt, tt, LANES), F32), pltpu.VMEM((L + 2 * HALO, SUBLANES, LANES), F32),
                        pltpu.VMEM((L + 2 * HALO, SUBLANES, LANES), F32), pltpu.VMEM((32, SUBLANES, LANES), F32)],
        semantics=("arbitrary",), args=(ddwc, ddwc, ddwc, glu, glu, glu, pre, w_dw), comm=comm)
    return dpre, dw, dbd, dbp, got


def attn_bwd(qkv, o, do, sink, rc, rs1, rs2, comm=None):
    T = qkv.shape[0]
    nb, q_spec, prev, own, nxt = _attn_specs(T)
    kvw = N_KV * HD

    def body(sink_ref, q_ref, kp_ref, ko_ref, kn_ref, o_ref, do_ref, c_ref, s1_ref, s2_ref,
             dq_ref, dkc_ref, dvc_ref, dsink_ref):
        n = pl.program_id(0)

        @pl.when(n == 0)
        def _():
            dsink_ref[...] = jnp.zeros_like(dsink_ref)

        valid = _attn_mask(n, T)
        kv = jnp.concatenate([kp_ref[...], ko_ref[...], kn_ref[...]], axis=0)
        kx, vx = _kv_padded(kv, 0), _kv_padded(kv, 2)
        tile = lambda ref, j: ref[:, j * LANES:(j + 1) * LANES]
        ss = _pair_products(kx, lambda j: tile(q_ref, j))
        dps = _pair_products(vx, lambda j: tile(do_ref, j))
        low_d = lax.broadcasted_iota(jnp.int32, (LANES, BLK), 0) < HD
        deltas = []
        for j in range(N_HEADS // 2):
            prod_t = tile(do_ref, j).astype(F32).T * tile(o_ref, j).astype(F32).T
            deltas.append(jnp.sum(jnp.where(low_d, prod_t, 0.0), axis=0, keepdims=True))
            deltas.append(jnp.sum(jnp.where(low_d, 0.0, prod_t), axis=0, keepdims=True))
        lane = lax.broadcasted_iota(jnp.int32, (1, N_HEADS), 1)
        dsink = jnp.zeros((1, N_HEADS), F32)
        pbs, dss = [], []
        for h in range(N_HEADS):
            p, p_sink = _softmax_sink(ss[h], valid, sink_ref[h])
            dss.append((p * (dps[h] - deltas[h])).astype(BF16))
            pbs.append(p.astype(BF16))
            part = -jnp.sum(p_sink * deltas[h], axis=1, keepdims=True)
            dsink = dsink + jnp.where(lane == h, part, 0.0)
        dsink_ref[...] += dsink
        c, s1, s2 = c_ref[...], s1_ref[...], s2_ref[...]
        kxt = {k: v.T for k, v in kx.items()}
        for j in range(N_HEADS // 2):
            g = 2 * j // GROUP
            dq_t = _dot(kxt[g, 0], dss[2 * j]) + _dot(kxt[g, 1], dss[2 * j + 1])
            dq_ref[:, j * LANES:(j + 1) * LANES] = (_rope(dq_t.T, c, -s1, -s2) * Q_SCALE).astype(BF16)
        low_k = lax.broadcasted_iota(jnp.int32, (3 * BLK, LANES), 1) < HD
        cols = lambda xs, g, p: jnp.concatenate([xs[GROUP * g + p], xs[GROUP * g + 2 + p]], axis=1)
        for t in range(N_KV // 2):
            sums = {}
            for g in (2 * t, 2 * t + 1):
                q2 = jnp.concatenate([tile(q_ref, 2 * g), tile(q_ref, 2 * g + 1)], axis=0)
                do2 = jnp.concatenate([tile(do_ref, 2 * g), tile(do_ref, 2 * g + 1)], axis=0)
                dk2 = _dot(jnp.concatenate([cols(dss, g, 0), cols(dss, g, 1)], axis=0), q2)
                dv2 = _dot(jnp.concatenate([cols(pbs, g, 0), cols(pbs, g, 1)], axis=0), do2)
                for p in range(2):
                    sums[g, p] = (dk2[p * 3 * BLK:(p + 1) * 3 * BLK], dv2[p * 3 * BLK:(p + 1) * 3 * BLK])
            for which, ref in ((0, dkc_ref), (1, dvc_ref)):
                keep = jnp.where(low_k, sums[2 * t, 0][which], sums[2 * t + 1, 1][which])
                swap = jnp.where(low_k, sums[2 * t + 1, 0][which], sums[2 * t, 1][which])
                ref[:, t * LANES:(t + 1) * LANES] = keep + pltpu.roll(swap, HD, 1)

    row = lambda n: (n, 0)
    (dq, dkc, dvc, dsink), got = _call(
        body, name="attn_bwd", grid=(nb,),
        in_specs=[pl.BlockSpec(memory_space=pltpu.SMEM), q_spec, prev, own, nxt,
                  pl.BlockSpec((BLK, D), row), pl.BlockSpec((BLK, D), row), *_tab_specs(BLK)],
        out_specs=[pl.BlockSpec((BLK, D), row), pl.BlockSpec((None, 3 * BLK, kvw), lambda n: (n, 0, 0)),
                   pl.BlockSpec((None, 3 * BLK, kvw), lambda n: (n, 0, 0)), pl.BlockSpec((1, N_HEADS), lambda n: (0, 0))],
        out_shape=[jax.ShapeDtypeStruct((T, QKV), BF16), jax.ShapeDtypeStruct((nb, 3 * BLK, kvw), F32),
                   jax.ShapeDtypeStruct((nb, 3 * BLK, kvw), F32), jax.ShapeDtypeStruct((1, N_HEADS), F32)],
        semantics=("arbitrary",), args=(sink, qkv, qkv, qkv, qkv, o, do, rc, rs1, rs2), comm=comm)
    return dq, dkc, dvc, dsink, got


def kv_sum(dqkv, dkc, dvc, rc, rs1, rs2):
    nb = dkc.shape[0]
    T = nb * BLK
    kvw = N_KV * HD

    G = 8
    ng = nb // G

    def gather3(own_ref, prev_ref, before_ref, next_ref, after_ref, m):
        has_before = (m > 0).astype(F32)
        has_after = (m < ng - 1).astype(F32)
        out = []
        for i in range(G):
            from_prev = prev_ref[i - 1] if i > 0 else before_ref[0] * has_before
            from_next = next_ref[i + 1] if i < G - 1 else after_ref[0] * has_after
            out.append(from_prev + own_ref[i] + from_next)
        return jnp.concatenate(out, axis=0)

    def body(_, ko, kp, kb, kn, ka, vo, vp, vb, vn, va, c_ref, s1_ref, s2_ref, out_ref):
        m = pl.program_id(0)
        dk = gather3(ko, kp, kb, kn, ka, m)
        dv = gather3(vo, vp, vb, vn, va, m)
        c, s1, s2 = c_ref[...], s1_ref[...], s2_ref[...]
        for j in range(kvw // LANES):
            sl = slice(LANES * j, LANES * (j + 1))
            out_ref[:, sl] = _rope(dk[:, sl], c, -s1, -s2).astype(BF16)
        out_ref[:, kvw:] = dv.astype(BF16)

    own = pl.BlockSpec((G, BLK, kvw), lambda m: (m, 1, 0))
    prev = pl.BlockSpec((G, BLK, kvw), lambda m: (m, 2, 0))
    before = pl.BlockSpec((1, BLK, kvw), lambda m: (jnp.maximum(G * m - 1, 0), 2, 0))
    nxt = pl.BlockSpec((G, BLK, kvw), lambda m: (m, 0, 0))
    after = pl.BlockSpec((1, BLK, kvw), lambda m: (jnp.minimum(G * m + G, nb - 1), 0, 0))
    five = [own, prev, before, nxt, after]
    return pl.pallas_call(
        body, name="kv_sum", grid=(ng,),
        in_specs=[pl.BlockSpec(memory_space=pl.ANY), *five, *five, *_tab_specs(G * BLK)],
        out_specs=pl.BlockSpec((G * BLK, 2 * kvw), lambda m: (m, KV_OFF // (2 * kvw))),
        out_shape=jax.ShapeDtypeStruct((T, QKV), BF16),
        input_output_aliases={0: 0},
        compiler_params=_params("parallel"),
    )(dqkv, *([dkc] * 5), *([dvc] * 5), rc, rs1, rs2)


def _me():
    return lax.axis_index("x"), lax.axis_index("y"), lax.axis_index("c")


def _half_rows(ref, sharded_rows, chip, core):
    R, C = ref.shape[-2], ref.shape[-1]
    lead = (slice(None),) * (len(ref.shape) - 2)
    if sharded_rows:
        per = R // N_CHIPS
        return ref.at[lead + (pl.ds(chip * per + core * (per // 2), per // 2), slice(None))]
    per = C // N_CHIPS
    return ref.at[lead + (pl.ds(core * (R // 2), R // 2), pl.ds(chip * per, per))]


class _Gather:
    def __init__(self, shards, sharded_rows):
        self.inputs = list(shards)
        self.rows = list(sharded_rows)
        self.n = self.n_in = self.n_out = len(shards)
        self.out_shapes = []
        for s, rows in zip(shards, sharded_rows):
            shp = list(s.shape)
            shp[-2 if rows else -1] *= N_CHIPS
            self.out_shapes.append(jax.ShapeDtypeStruct(tuple(shp), s.dtype))
        self.scratch = [pltpu.SemaphoreType.DMA((self.n, 6)), pltpu.SemaphoreType.DMA((self.n, 6)),
                        pltpu.SemaphoreType.DMA((self.n, 2))]

    def _ctx(self, ins, outs, sems):
        send_sems, recv_sems, local_sems = sems
        x, y, c = _me()
        chips = [(1 - x, y), (x, 1 - y), (1 - x, 1 - y)]

        def half_src(w, core):
            s = ins[w]
            R = s.shape[-2]
            return s.at[pl.ds(core * (R // 2), R // 2), :]

        def dst(w, chip, core):
            return _half_rows(outs[w], self.rows[w], chip, core)

        def copy(w, k, src, chip, core, to):
            return pltpu.make_async_remote_copy(
                src_ref=src, dst_ref=dst(w, chip, core), send_sem=send_sems.at[w, k], recv_sem=recv_sems.at[w, k],
                device_id=to, device_id_type=MESH)

        def local(w, core):
            return pltpu.make_async_copy(half_src(w, core), dst(w, 2 * x + y, core), local_sems.at[w, core])

        def first(w, j):
            qx, qy = chips[j]
            return copy(w, j, half_src(w, c), 2 * x + y, c, (qx, qy, c))

        def landed(w, j):
            qx, qy = chips[j]
            return copy(w, j, dst(w, 2 * qx + qy, c), 2 * qx + qy, c, (x, y, c))

        def passed(w, j):
            qx, qy = chips[j]
            return copy(w, 3 + j, dst(w, 2 * qx + qy, c), 2 * qx + qy, c, (x, y, 1 - c))

        def from_sibling(w, j):
            qx, qy = chips[j]
            return copy(w, 3 + j, dst(w, 2 * qx + qy, 1 - c), 2 * qx + qy, 1 - c, (x, y, c))

        return local, first, landed, passed, from_sibling

    def start(self, ins, outs, sems):
        local, first, _, _, _ = self._ctx(ins, outs, sems)
        for w in range(self.n):
            for core in range(2):
                local(w, core).start()
            for j in range(3):
                first(w, j).start()

    def mid(self, ins, outs, sems):
        _, _, landed, passed, _ = self._ctx(ins, outs, sems)
        for w in range(self.n):
            for j in range(3):
                landed(w, j).wait_recv()
                passed(w, j).start()

    def end(self, ins, outs, sems):
        local, first, _, passed, from_sibling = self._ctx(ins, outs, sems)
        for w in range(self.n):
            for j in range(3):
                from_sibling(w, j).wait_recv()
        for w in range(self.n):
            for j in range(3):
                first(w, j).wait_send()
                passed(w, j).wait_send()
            for core in range(2):
                local(w, core).wait()


class _Scatter:
    def __init__(self, grads, small=None):
        self.inputs = list(grads) + ([small] if small is not None else [])
        self.ng = len(grads)
        self.n = self.n_in = self.n_out = len(self.inputs)
        self.out_shapes = [jax.ShapeDtypeStruct((N_DEV, g.shape[1] // 2, g.shape[2]), g.dtype) for g in grads]
        if small is not None:
            self.out_shapes.append(jax.ShapeDtypeStruct((N_DEV,) + small.shape, small.dtype))
        self.scratch = [pltpu.SemaphoreType.DMA((self.n, N_DEV)), pltpu.SemaphoreType.DMA((self.n, N_DEV)),
                        pltpu.SemaphoreType.DMA((self.n,))]

    def _ctx(self, ins, outs, sems):
        send_sems, recv_sems, local_sems = sems
        x, y, c = _me()
        me = 4 * x + 2 * y + c

        def piece(w, chip, core):
            if w >= self.ng:
                return ins[w]
            half = ins[w].shape[1] // 2
            return ins[w].at[chip, pl.ds(core * half, half), :]

        def peer_of(k):
            return x ^ ((k >> 2) & 1), y ^ ((k >> 1) & 1), c ^ (k & 1)

        def local(w):
            return pltpu.make_async_copy(piece(w, 2 * x + y, c), outs[w].at[me], local_sems.at[w])

        def send(w, k):
            px, py, pc = peer_of(k)
            return pltpu.make_async_remote_copy(
                src_ref=piece(w, 2 * px + py, pc), dst_ref=outs[w].at[me], send_sem=send_sems.at[w, k],
                recv_sem=recv_sems.at[w, k], device_id=(px, py, pc), device_id_type=MESH)

        def recv(w, k):
            px, py, pc = peer_of(k)
            return pltpu.make_async_remote_copy(
                src_ref=piece(w, 2 * x + y, c), dst_ref=outs[w].at[4 * px + 2 * py + pc], send_sem=send_sems.at[w, k],
                recv_sem=recv_sems.at[w, k], device_id=(px, py, pc), device_id_type=MESH)

        return local, send, recv

    def start(self, ins, outs, sems):
        local, send, _ = self._ctx(ins, outs, sems)
        for w in range(self.n):
            local(w).start()
            for k in range(1, N_DEV):
                send(w, k).start()

    def mid(self, ins, outs, sems):
        pass

    def end(self, ins, outs, sems):
        local, send, recv = self._ctx(ins, outs, sems)
        for w in range(self.n):
            for k in range(1, N_DEV):
                recv(w, k).wait_recv()
        for w in range(self.n):
            for k in range(1, N_DEV):
                send(w, k).wait_send()
            local(w).wait()


class _Both:
    def __init__(self, a, b):
        self.a, self.b = a, b
        self.inputs = a.inputs + b.inputs
        self.out_shapes = a.out_shapes + b.out_shapes
        self.scratch = a.scratch + b.scratch
        self.n_in, self.n_out = a.n_in + b.n_in, a.n_out + b.n_out

    def _split(self, ins, outs, sems):
        a, na = self.a, len(self.a.scratch)
        return (ins[:a.n_in], outs[:a.n_out], sems[:na]), (ins[a.n_in:], outs[a.n_out:], sems[na:])

    def start(self, ins, outs, sems):
        pa, pb = self._split(ins, outs, sems)
        self.a.start(*pa)
        self.b.start(*pb)

    def mid(self, ins, outs, sems):
        pa, pb = self._split(ins, outs, sems)
        self.a.mid(*pa)
        self.b.mid(*pb)

    def end(self, ins, outs, sems):
        pa, pb = self._split(ins, outs, sems)
        self.a.end(*pa)
        self.b.end(*pb)


def _call(body, *, name, grid, in_specs, out_specs, out_shape, scratch_shapes=(), semantics, args, comm=None):
    if comm is None:
        outs = pl.pallas_call(
            body, name=name, grid=grid, in_specs=in_specs, out_specs=out_specs, out_shape=out_shape,
            scratch_shapes=list(scratch_shapes), compiler_params=_params(*semantics))(*args)
        return outs, []
    n_in, n_out, n_scr = len(in_specs), len(out_specs), len(scratch_shapes)

    total = math.prod(grid)
    first, middle, last = 0, (3 * total) // 4 - 1, total - 1
    assert first <= middle < last

    def at(step):
        lin = pl.program_id(0)
        for d in range(1, len(grid)):
            lin = lin * grid[d] + pl.program_id(d)
        return lin == step

    def hosted(*refs):
        h_in, c_in = refs[:n_in], refs[n_in:n_in + comm.n_in]
        rest = refs[n_in + comm.n_in:]
        h_out, c_out = rest[:n_out], rest[n_out:n_out + comm.n_out]
        rest = rest[n_out + comm.n_out:]
        h_scr, c_scr = rest[:n_scr], rest[n_scr:]

        @pl.when(at(first))
        def _():
            comm.start(c_in, c_out, c_scr)

        body(*h_in, *h_out, *h_scr)

        @pl.when(at(middle))
        def _():
            comm.mid(c_in, c_out, c_scr)

        @pl.when(at(last))
        def _():
            comm.end(c_in, c_out, c_scr)

    any_spec = pl.BlockSpec(memory_space=pl.ANY)
    outs = pl.pallas_call(
        hosted, name=name, grid=grid, in_specs=list(in_specs) + [any_spec] * comm.n_in,
        out_specs=list(out_specs) + [any_spec] * comm.n_out, out_shape=list(out_shape) + comm.out_shapes,
        scratch_shapes=list(scratch_shapes) + comm.scratch,
        compiler_params=_params(*(["arbitrary"] * len(grid))))(*args, *comm.inputs)
    return outs[:n_out], outs[n_out:]


def sum_swap(pieces, name, comm=None):
    nl = len(pieces)
    _, r2, cc = pieces[0].shape
    tr = 256 if r2 % 256 == 0 else (128 if r2 % 128 == 0 else r2 // 2)
    n = r2 // tr

    def body(*refs):
        p_refs, out = refs[:nl], refs[nl]
        slots, send_sems, local_sems, recv_sem = refs[nl + 1:]
        x, y, c = _me()
        sibling = (x, y, 1 - c)
        l, i = pl.program_id(0), pl.program_id(1)
        step = l * n + i

        def rows(st, core):
            return out.at[st // n, pl.ds(core * r2 + (st % n) * tr, tr), :]

        def copies(st):
            slot = st % 2
            local = pltpu.make_async_copy(slots.at[slot], rows(st, c), local_sems.at[slot])
            remote = pltpu.make_async_remote_copy(
                src_ref=slots.at[slot], dst_ref=rows(st, c), send_sem=send_sems.at[slot], recv_sem=recv_sem,
                device_id=sibling, device_id_type=MESH)
            return local, remote

        for ll in range(nl):
            @pl.when(l == ll)
            def _():
                acc = p_refs[ll][0].astype(F32)
                for d in range(1, N_DEV):
                    acc = acc + p_refs[ll][d].astype(F32)
                slots[step % 2] = acc

        for cp in copies(step):
            cp.start()

        @pl.when(step >= 1)
        def _():
            local, remote = copies(step - 1)
            local.wait()
            remote.wait_send()

        @pl.when(step == nl * n - 1)
        def _():
            local, remote = copies(step)
            local.wait()
            remote.wait_send()
            theirs = out.at[:, pl.ds((1 - c) * r2, r2), :]
            pltpu.make_async_remote_copy(src_ref=theirs, dst_ref=theirs, send_sem=send_sems.at[0],
                                         recv_sem=recv_sem, device_id=sibling, device_id_type=MESH).wait_recv()

    def piece_spec(ll):
        def index(l, i):
            return (0, jnp.where(l == ll, i, jnp.where(l < ll, 0, n - 1)), 0)
        return pl.BlockSpec((N_DEV, tr, cc), index)

    (out,), got = _call(
        body, name=name, grid=(nl, n),
        in_specs=[piece_spec(ll) for ll in range(nl)],
        out_specs=[pl.BlockSpec(memory_space=pl.ANY)],
        out_shape=[jax.ShapeDtypeStruct((nl, 2 * r2, cc), F32)],
        scratch_shapes=[pltpu.VMEM((2, tr, cc), F32), pltpu.SemaphoreType.DMA((2,)), pltpu.SemaphoreType.DMA((2,)),
                        pltpu.SemaphoreType.DMA(())],
        semantics=("arbitrary", "arbitrary"), args=tuple(pieces), comm=comm)
    return (out, got) if comm is not None else out


def sum_pieces(pieces, name):
    _, R, C = pieces.shape
    tr = _tile(R, 128) if R % 128 == 0 else R

    def body(p_ref, o_ref):
        acc = p_ref[0].astype(F32)
        for d in range(1, N_DEV):
            acc = acc + p_ref[d].astype(F32)
        o_ref[...] = acc

    return pl.pallas_call(
        body, name=name, grid=(R // tr,),
        in_specs=[pl.BlockSpec((N_DEV, tr, C), lambda i: (0, i, 0))],
        out_specs=pl.BlockSpec((tr, C), lambda i: (i, 0)),
        out_shape=jax.ShapeDtypeStruct((R, C), F32),
        compiler_params=_params("parallel"),
    )(pieces)


def adamw(w, g, m, v, name):
    Lyr, R, C = w.shape
    tr = _tile(R, 256) if R % 8 == 0 else R
    c1 = 1.0 / (1.0 - ADAM_B1 ** ADAM_STEP)
    c2 = 1.0 / (1.0 - ADAM_B2 ** ADAM_STEP)

    def body(w_ref, g_ref, m_ref, v_ref, d_ref, nm_ref, nv_ref):
        gv = g_ref[...]
        nm = ADAM_B1 * m_ref[...] + (1.0 - ADAM_B1) * gv
        nv = ADAM_B2 * v_ref[...] + (1.0 - ADAM_B2) * (gv * gv)
        nm_ref[...] = nm
        nv_ref[...] = nv
        d_ref[...] = -ADAM_LR * ((nm * c1) / (jnp.sqrt(nv * c2) + ADAM_EPS) + ADAM_WD * w_ref[...])

    spec = pl.BlockSpec((None, tr, C), lambda l, i: (l, i, 0))
    shp = jax.ShapeDtypeStruct(w.shape, F32)
    return pl.pallas_call(
        body, name=name, grid=(Lyr, R // tr),
        in_specs=[spec] * 4, out_specs=[spec] * 3, out_shape=[shp] * 3,
        compiler_params=_params("parallel", "parallel"),
    )(w, g, m, v)


def _rope_tables(T):
    pos = jnp.arange(T, dtype=F32)
    inv_freq = THETA ** (-jnp.arange(0, ROT, 2, dtype=F32) / ROT)
    ang = pos[:, None] * inv_freq[None, :]
    cs = jnp.concatenate([jnp.cos(ang), jnp.sin(ang)], axis=1)
    half = ROT // 2
    lane = jnp.arange(3 * LANES)
    table, lm = lane // LANES, lane % HD
    src = jnp.where(table == 0, lm % half, half + lm % half)
    i32 = lambda b: b.astype(jnp.int32)
    sign = jnp.where(table == 0, i32(lm < ROT), jnp.where(table == 1, -i32(lm < half), i32((lm >= half) & (lm < ROT))))
    place = (jnp.arange(ROT)[:, None] == src[None, :]) * sign[None, :].astype(F32)
    ones = ((table == 0) & (lm >= ROT)).astype(F32)
    return jnp.dot(cs, place, precision=lax.Precision.HIGHEST) + ones[None, :]


def _tab_specs(rows):
    return [pl.BlockSpec((rows, LANES), lambda i, k=k: (i, k)) for k in range(3)]


def kernel(x, attn_norm, attn_w_qkv, attn_w_o, attn_sink, conv_norm, conv_w_pw1, conv_b_pw1, conv_w_dw, conv_b_dw, conv_ln_g, conv_ln_b, conv_w_pw2, conv_b_pw2, ffn_norm, ffn_w_gu, ffn_w_down, final_norm, loss_target, m_attn_norm, m_attn_w_qkv, m_attn_w_o, m_attn_sink, m_conv_norm, m_conv_w_pw1, m_conv_b_pw1, m_conv_w_dw, m_conv_b_dw, m_conv_ln_g, m_conv_ln_b, m_conv_w_pw2, m_conv_b_pw2, m_ffn_norm, m_ffn_w_gu, m_ffn_w_down, m_final_norm, v_attn_norm, v_attn_w_qkv, v_attn_w_o, v_attn_sink, v_conv_norm, v_conv_w_pw1, v_conv_b_pw1, v_conv_w_dw, v_conv_b_dw, v_conv_ln_g, v_conv_ln_b, v_conv_w_pw2, v_conv_b_pw2, v_ffn_norm, v_ffn_w_gu, v_ffn_w_down, v_final_norm):
    T = x.shape[1]
    x0 = x[0]
    target = loss_target[0]
    ix, iy = lax.axis_index("x"), lax.axis_index("y")
    chip = 2 * ix + iy
    rc = rs1 = rs2 = _rope_tables(T)

    bf = lambda t: t.astype(BF16)

    def place(vec, width):
        return lax.dynamic_update_slice(jnp.zeros((vec.shape[0], N_CHIPS * width), F32), vec, (0, chip * width))

    small_rows = jnp.concatenate([
        place(conv_norm, 256), place(conv_b_pw1, 512).reshape(2, D), place(conv_b_dw, 256), place(conv_ln_g, 256),
        place(conv_ln_b, 256), place(conv_b_pw2, 256), jnp.zeros((1, D), F32),
        place(conv_w_dw[0], 256), jnp.zeros((1, D), F32)], axis=0)

    h0, (w_qkv,) = rms_first(x0, attn_norm, comm=_Gather([bf(attn_w_qkv[0])], [False]))
    qkv, (w_o, got) = qkv_proj(h0, w_qkv, rc, rs1, rs2,
                               comm=_Both(_Gather([bf(attn_w_o[0])], [True]), _Scatter([], small_rows)))
    psmall = sum_pieces(got, "sum_small_params") * 0.5
    p_conv_norm, p_b_pw1 = psmall[0:1], psmall[1:3].reshape(1, 2 * D)
    p_b_dw, p_ln_g, p_ln_b, p_b_pw2 = psmall[3:4], psmall[4:5], psmall[5:6], psmall[6:7]
    p_w_dw = psmall[8:40]
    sink = attn_sink[0]
    o, (w_gu0,) = attn_fwd(qkv, sink, comm=_Gather([bf(ffn_w_gu[0])], [False]))
    zero_b = jnp.zeros((1, D), F32)
    zero_gu = jnp.zeros((1, 2 * DFF), F32)
    x1, h1, gu0, act0, (w_down0, w_pw1, w_pw2) = rms_mm_gate(
        (o, w_o, zero_b, x0), ffn_norm[0:1], w_gu0, zero_gu, DFF, True, BF16, "ffn0_up",
        comm=_Gather([bf(ffn_w_down[0]), bf(conv_w_pw1[0]), bf(conv_w_pw2[0])], [True, False, True]))
    x2, h2, pre, glu, (w_down1,) = rms_mm_gate((act0, w_down0, zero_b, x1), p_conv_norm, w_pw1, p_b_pw1, D, False, F32,
                                               "conv_pw1", comm=_Gather([bf(ffn_w_down[1])], [True]))
    dwc, sw, (w_gu1,) = conv_fwd(glu, p_w_dw, p_b_dw, p_ln_g, p_ln_b, comm=_Gather([bf(ffn_w_gu[1])], [False]))
    x3, h3, gu1, act1, _ = rms_mm_gate((sw, w_pw2, p_b_pw2, x2), ffn_norm[1:2], w_gu1, zero_gu, DFF, True, BF16,
                                       "ffn1_up")
    dx4, loss_part, d_final = mm_res_loss(act1, w_down1, x3, final_norm.reshape(1, D), target)

    dgu1, _ = swiglu_bwd(dx4, w_down1, gu1, "ffn1_down_bwd")
    g_down1 = dw_row(act1, dx4, "ffn1_down_dw")
    dx3, d_ffn1, ddwc, d_ln_g, d_ln_b, d_b_pw2, _ = mm_bt_rmsbwd(
        dgu1, w_gu1, x3, ffn_norm[1:2], dx4, "ffn1_up_bwd", conv_tail=(w_pw2, dwc, p_ln_g, p_ln_b))
    g_gu1 = dw_col(h3, dgu1, "ffn1_up_dw")

    g_pw2 = dw_row(sw, dx3, "conv_pw2_dw")
    dpre, d_w_dw, d_b_dw, d_b_pw1, (r_gu1, r_down1) = conv_bwd(ddwc, glu, pre, p_w_dw,
                                                               comm=_Scatter([g_gu1, g_down1]))
    dx2, d_conv_norm, _ = mm_bt_rmsbwd(dpre, w_pw1, x2, p_conv_norm, dx3, "conv_pw1_bwd")
    g_pw1 = dw_col(h2, dpre, "conv_pw1_dw")

    dgu0, (r_pw1, r_pw2) = swiglu_bwd(dx2, w_down0, gu0, "ffn0_down_bwd", comm=_Scatter([g_pw1, g_pw2]))
    g_down0 = dw_row(act0, dx2, "ffn0_down_dw")
    dx1, d_ffn0, do, _ = mm_bt_rmsbwd(dgu0, w_gu0, x1, ffn_norm[0:1], dx2, "ffn0_up_bwd", proj_w=w_o)
    g_gu0 = dw_col(h1, dgu0, "ffn0_up_dw")

    g_o = dw_row(o, dx1, "attn_out_dw")
    dq, dkc, dvc, d_sink, (r_gu0, r_down0, r_o) = attn_bwd(qkv, o, do, sink, rc, rs1, rs2,
                                                           comm=_Scatter([g_gu0, g_down0, g_o]))
    dqkv = kv_sum(dq, dkc, dvc, rc, rs1, rs2)[None]
    g_qkv = dw_col(h0, dqkv, "attn_qkv_dw")
    dx0, d_attn_norm, _ = mm_bt_rmsbwd(dqkv, w_qkv, x0, attn_norm, dx1, "attn_qkv_bwd")

    pad16 = lambda t: jnp.concatenate([t, jnp.zeros((1, D - t.shape[1]), F32)], axis=1)
    small_g = jnp.concatenate([
        d_attn_norm, pad16(d_sink), d_conv_norm, d_b_pw1.reshape(2, D), d_b_dw, d_ln_g, d_ln_b, d_b_pw2,
        d_ffn0, d_ffn1, d_final, pad16(loss_part), jnp.zeros((3, D), F32), d_w_dw], axis=0)
    gf_gu, (r_qkv, r_small) = sum_swap([r_gu0, r_gu1], "sum_gu", comm=_Scatter([g_qkv], small_g))
    gf_down = sum_swap([r_down0, r_down1], "sum_down")
    gf_pw1, gf_pw2 = sum_swap([r_pw1], "sum_pw1"), sum_swap([r_pw2], "sum_pw2")
    gf_qkv, gf_o = sum_swap([r_qkv], "sum_qkv"), sum_swap([r_o], "sum_o")
    gs = sum_pieces(r_small, "sum_small_grads")
    loss = gs[12, 0]

    def take(row0, nrows, width):
        return lax.dynamic_slice(gs, (row0, chip * width), (nrows, width))

    grads = {
        "attn_norm": gs[0:1], "attn_w_qkv": gf_qkv, "attn_w_o": gf_o, "attn_sink": gs[1:2, :N_HEADS],
        "conv_norm": take(2, 1, 256), "conv_w_pw1": gf_pw1,
        "conv_b_pw1": lax.dynamic_slice(gs[3:5].reshape(1, 2 * D), (0, chip * 512), (1, 512)),
        "conv_w_dw": take(16, 32, 256)[None, :CONV_W], "conv_b_dw": take(5, 1, 256), "conv_ln_g": take(6, 1, 256),
        "conv_ln_b": take(7, 1, 256), "conv_w_pw2": gf_pw2, "conv_b_pw2": take(8, 1, 256),
        "ffn_norm": gs[9:11], "ffn_w_gu": gf_gu, "ffn_w_down": gf_down, "final_norm": gs[11],
    }
    weights = dict(attn_norm=attn_norm, attn_w_qkv=attn_w_qkv, attn_w_o=attn_w_o, attn_sink=attn_sink,
                   conv_norm=conv_norm, conv_w_pw1=conv_w_pw1, conv_b_pw1=conv_b_pw1, conv_w_dw=conv_w_dw,
                   conv_b_dw=conv_b_dw, conv_ln_g=conv_ln_g, conv_ln_b=conv_ln_b, conv_w_pw2=conv_w_pw2,
                   conv_b_pw2=conv_b_pw2, ffn_norm=ffn_norm, ffn_w_gu=ffn_w_gu, ffn_w_down=ffn_w_down,
                   final_norm=final_norm)
    m_in = dict(attn_norm=m_attn_norm, attn_w_qkv=m_attn_w_qkv, attn_w_o=m_attn_w_o, attn_sink=m_attn_sink,
                conv_norm=m_conv_norm, conv_w_pw1=m_conv_w_pw1, conv_b_pw1=m_conv_b_pw1, conv_w_dw=m_conv_w_dw,
                conv_b_dw=m_conv_b_dw, conv_ln_g=m_conv_ln_g, conv_ln_b=m_conv_ln_b, conv_w_pw2=m_conv_w_pw2,
                conv_b_pw2=m_conv_b_pw2, ffn_norm=m_ffn_norm, ffn_w_gu=m_ffn_w_gu, ffn_w_down=m_ffn_w_down,
                final_norm=m_final_norm)
    v_in = dict(attn_norm=v_attn_norm, attn_w_qkv=v_attn_w_qkv, attn_w_o=v_attn_w_o, attn_sink=v_attn_sink,
                conv_norm=v_conv_norm, conv_w_pw1=v_conv_w_pw1, conv_b_pw1=v_conv_b_pw1, conv_w_dw=v_conv_w_dw,
                conv_b_dw=v_conv_b_dw, conv_ln_g=v_conv_ln_g, conv_ln_b=v_conv_ln_b, conv_w_pw2=v_conv_w_pw2,
                conv_b_pw2=v_conv_b_pw2, ffn_norm=v_ffn_norm, ffn_w_gu=v_ffn_w_gu, ffn_w_down=v_ffn_w_down,
                final_norm=v_final_norm)
    order = list(weights)
    g_out, d_out, m_out, v_out = [], [], [], []
    for nm in order:
        w = weights[nm]
        shape = w.shape
        as3 = lambda t: t.reshape((1,) * (3 - len(shape)) + shape) if len(shape) < 3 else t.reshape(shape)
        g3 = as3(grads[nm].reshape(shape))
        delta, nm_, nv_ = adamw(as3(w), g3, as3(m_in[nm]), as3(v_in[nm]), "adamw_" + nm)
        g_out.append(g3.reshape(shape))
        d_out.append(delta.reshape(shape))
        m_out.append(nm_.reshape(shape))
        v_out.append(nv_.reshape(shape))
    return (loss, dx0[None], *g_out, *d_out, *m_out, *v_out)
```

```python
import math

import jax
import jax.numpy as jnp
from jax import lax
from jax.experimental import pallas as pl
from jax.experimental.pallas import tpu as pltpu

F32 = jnp.float32
BF16 = jnp.bfloat16

D = 1024
N_HEADS = 16
N_KV = 4
GROUP = N_HEADS // N_KV
HD = 64
ROT = 16
THETA = 500000.0
BLK = 128
QKV = (N_HEADS + 2 * N_KV) * HD
KV_OFF = N_HEADS * HD
DFF = 2816
CONV_W = 31
CONV_PAD = 15
HALO = 16
CONV_JB = 16
CONV_JB_BWD = 16
EPS = 1e-6
NEG = -1e30
N_CHIPS = 4
N_DEV = 8
LANES = 128
SUBLANES = 8

ADAM_LR, ADAM_B1, ADAM_B2, ADAM_EPS, ADAM_WD, ADAM_STEP = 0.001, 0.9, 0.999, 1e-08, 0.01, 10

VMEM_LIMIT = 56 * 1024 * 1024
MESH = pl.DeviceIdType.MESH


def _params(*sem):
    return pltpu.CompilerParams(dimension_semantics=sem, vmem_limit_bytes=VMEM_LIMIT)


def _tile(n, want):
    if n <= want:
        return n
    for t in range(want, 7, -1):
        if n % t == 0 and t % 8 == 0:
            return t
    return n


MXU_COLS = 256


def _col_chunks(n):
    return [slice(c, min(c + MXU_COLS, n)) for c in range(0, n, MXU_COLS)]


def _sigmoid(v):
    return jax.nn.sigmoid(v)


def _rms_fwd(xv, gain):
    r = lax.rsqrt(jnp.mean(xv * xv, axis=-1, keepdims=True) + EPS)
    return xv * r * gain


def _rms_bwd(dh, xv, gain, dres):
    r = lax.rsqrt(jnp.mean(xv * xv, axis=-1, keepdims=True) + EPS)
    xhat = xv * r
    gy = dh * gain
    dx = r * (gy - xhat * jnp.mean(gy * xhat, axis=-1, keepdims=True))
    return dx + dres, dh * xhat


def _rope(blk, c, s1, s2):
    return blk * c + pltpu.roll(blk, LANES - ROT // 2, 1) * s1 + pltpu.roll(blk, ROT // 2, 1) * s2


def _dot(a, b):
    return jnp.dot(a, b, preferred_element_type=F32)


def _dot_tb(a, b):
    return lax.dot_general(a, b, (((1,), (1,)), ((), ())), preferred_element_type=F32)


def _dot_ta(a, b):
    return lax.dot_general(a, b, (((0,), (0,)), ((), ())), preferred_element_type=F32)


def rms_first(x, gain, comm):
    T = x.shape[0]
    tm = _tile(T, 512)

    def body(x_ref, g_ref, h_ref):
        h_ref[...] = _rms_fwd(x_ref[...], g_ref[...]).astype(BF16)

    (h,), got = _call(
        body, name="rms_first", grid=(T // tm,),
        in_specs=[pl.BlockSpec((tm, D), lambda i: (i, 0)), pl.BlockSpec((1, D), lambda i: (0, 0))],
        out_specs=[pl.BlockSpec((tm, D), lambda i: (i, 0))], out_shape=[jax.ShapeDtypeStruct((T, D), BF16)],
        semantics=("parallel",), args=(x, gain), comm=comm)
    return h, got


def qkv_proj(h, w, rc, rs1, rs2, comm=None):
    T = h.shape[0]
    tm = _tile(T, 1024)

    def body(h_ref, w_ref, c_ref, s1_ref, s2_ref, qkv_ref):
        acc = _dot(h_ref[...], w_ref[...])
        c, s1, s2 = c_ref[...], s1_ref[...], s2_ref[...]
        n_rot = (KV_OFF + N_KV * HD) // LANES
        for j in range(n_rot):
            sl = slice(LANES * j, LANES * (j + 1))
            roped = _rope(acc[:, sl], c, s1, s2)
            if j < KV_OFF // LANES:
                roped = roped * Q_SCALE
            qkv_ref[:, sl] = roped.astype(BF16)
        qkv_ref[:, n_rot * LANES:] = acc[:, n_rot * LANES:].astype(BF16)

    row = lambda i: (i, 0)
    full = lambda i: (0, 0)
    (qkv,), got = _call(
        body, name="qkv_proj", grid=(T // tm,),
        in_specs=[pl.BlockSpec((tm, D), row), pl.BlockSpec((D, QKV), full), *_tab_specs(tm)],
        out_specs=[pl.BlockSpec((tm, QKV), row)],
        out_shape=[jax.ShapeDtypeStruct((T, QKV), BF16)],
        semantics=("parallel",), args=(h, w, rc, rs1, rs2), comm=comm)
    return qkv, got


Q_SCALE = 1.0 / math.sqrt(HD)


def _attn_mask(n, T):
    ci = lax.broadcasted_iota(jnp.int32, (3 * BLK, BLK), 0)
    qi = lax.broadcasted_iota(jnp.int32, (3 * BLK, BLK), 1)
    key_pos = n * BLK - BLK + ci
    return (jnp.abs(ci - BLK - qi) <= BLK) & (key_pos >= 0) & (key_pos < T)


def _kv_padded(kv, first_tile):
    low = lax.broadcasted_iota(jnp.int32, (3 * BLK, LANES), 1) < HD
    zero = jnp.zeros((3 * BLK, LANES), BF16)
    out = {}
    for g in range(N_KV):
        t = kv[:, (first_tile + g // 2) * LANES:(first_tile + g // 2 + 1) * LANES]
        swapped = jnp.concatenate([t[:, HD:], t[:, :HD]], axis=1)
        for p in range(2):
            out[g, p] = jnp.where(low if p == 0 else ~low, t if g % 2 == p else swapped, zero)
    return out


def _pair_products(kvx, tile_of):
    both = {g: jnp.concatenate([kvx[g, 0], kvx[g, 1]], axis=0) for g in range(N_KV)}
    out = []
    for j in range(N_HEADS // 2):
        prod = _dot_tb(both[2 * j // GROUP], tile_of(j))
        out += [prod[:3 * BLK], prod[3 * BLK:]]
    return out


def _softmax_sink(s, valid, sk):
    s = jnp.where(valid, s, NEG)
    m = jnp.maximum(jnp.max(s, axis=0, keepdims=True), sk)
    e = jnp.exp(s - m)
    es = jnp.exp(sk - m)
    inv = 1.0 / (jnp.sum(e, axis=0, keepdims=True) + es)
    return e * inv, es * inv


def _attn_specs(T):
    nb = T // BLK
    kv_blk = 2 * N_KV * HD
    kv_col = KV_OFF // kv_blk
    q_spec = pl.BlockSpec((BLK, KV_OFF), lambda n: (n, 0))
    prev = pl.BlockSpec((BLK, kv_blk), lambda n: (jnp.maximum(n - 1, 0), kv_col))
    own = pl.BlockSpec((BLK, kv_blk), lambda n: (n, kv_col))
    nxt = pl.BlockSpec((BLK, kv_blk), lambda n: (jnp.minimum(n + 1, nb - 1), kv_col))
    return nb, q_spec, prev, own, nxt


def attn_fwd(qkv, sink, comm=None):
    T = qkv.shape[0]
    nb, q_spec, prev, own, nxt = _attn_specs(T)

    def body(sink_ref, q_ref, kp_ref, ko_ref, kn_ref, o_ref):
        valid = _attn_mask(pl.program_id(0), T)
        kv = jnp.concatenate([kp_ref[...], ko_ref[...], kn_ref[...]], axis=0)
        kx, vx = _kv_padded(kv, 0), _kv_padded(kv, 2)
        ss = _pair_products(kx, lambda j: q_ref[:, j * LANES:(j + 1) * LANES])
        ps = [_softmax_sink(ss[h], valid, sink_ref[h])[0].astype(BF16) for h in range(N_HEADS)]
        vxt = {k: v.T for k, v in vx.items()}
        for j in range(N_HEADS // 2):
            g = 2 * j // GROUP
            o_t = _dot(vxt[g, 0], ps[2 * j]) + _dot(vxt[g, 1], ps[2 * j + 1])
            o_ref[:, j * LANES:(j + 1) * LANES] = o_t.T.astype(BF16)

    (o,), got = _call(
        body, name="attn_fwd", grid=(nb,),
        in_specs=[pl.BlockSpec(memory_space=pltpu.SMEM), q_spec, prev, own, nxt],
        out_specs=[pl.BlockSpec((BLK, D), lambda n: (n, 0))],
        out_shape=[jax.ShapeDtypeStruct((T, D), BF16)],
        semantics=("parallel",), args=(sink, qkv, qkv, qkv, qkv), comm=comm)
    return o, got


def rms_mm_gate(x, gain, w, bias, H, swiglu, act_dtype, name, comm=None):
    fused = isinstance(x, tuple)
    T = (x[0] if fused else x).shape[0]
    tm = _tile(T, 512)

    def body(*refs):
        if fused:
            a_ref, wp_ref, bp_ref, r_ref, g_ref, w_ref, b_ref, x_ref, h_ref, pre_ref, act_ref = refs
            xv = _dot(a_ref[...], wp_ref[...]) + bp_ref[...] + r_ref[...]
            x_ref[...] = xv
        else:
            x_ref, g_ref, w_ref, b_ref, h_ref, pre_ref, act_ref = refs
            xv = x_ref[...]
        h = _rms_fwd(xv, g_ref[...]).astype(BF16)
        h_ref[...] = h
        for cs in _col_chunks(H):
            cs2 = slice(H + cs.start, H + cs.stop)
            a = _dot(h, w_ref[:, cs]) + b_ref[:, cs]
            b = _dot(h, w_ref[:, cs2]) + b_ref[:, cs2]
            pre_ref[0, :, cs] = a.astype(BF16)
            pre_ref[1, :, cs] = b.astype(BF16)
            if swiglu:
                act = a * _sigmoid(a) * b
            else:
                act = a * _sigmoid(b)
            act_ref[:, cs] = act.astype(act_dtype)

    row = lambda i: (i, 0)
    full = lambda i: (0, 0)
    if fused:
        K = x[0].shape[1]
        x_specs = [pl.BlockSpec((tm, K), row), pl.BlockSpec((K, D), full, pipeline_mode=pl.Buffered(1)),
                   pl.BlockSpec((1, D), full), pl.BlockSpec((tm, D), row)]
        x_out = ([pl.BlockSpec((tm, D), row)], [jax.ShapeDtypeStruct((T, D), F32)])
        x_args = tuple(x)
    else:
        x_specs, x_out, x_args = [pl.BlockSpec((tm, D), row)], ([], []), (x,)
    outs, got = _call(
        body, name=name, grid=(T // tm,),
        in_specs=x_specs + [pl.BlockSpec((1, D), full),
                            pl.BlockSpec((D, 2 * H), full, pipeline_mode=pl.Buffered(1)), pl.BlockSpec((1, 2 * H), full)],
        out_specs=x_out[0] + [pl.BlockSpec((tm, D), row), pl.BlockSpec((2, tm, H), lambda i: (0, i, 0)),
                              pl.BlockSpec((tm, H), row)],
        out_shape=x_out[1] + [jax.ShapeDtypeStruct((T, D), BF16), jax.ShapeDtypeStruct((2, T, H), BF16),
                              jax.ShapeDtypeStruct((T, H), act_dtype)],
        semantics=("parallel",), args=x_args + (gain, w, bias), comm=comm)
    return (*outs, got)


def _conv_tiles(T):
    tt = _tile(T, 512)
    return tt, tt // SUBLANES, D // LANES


def _fill_strided(ext, p, L):
    main = p[HALO:HALO + SUBLANES * L, :].reshape(SUBLANES, L, LANES)
    ext[CONV_PAD:CONV_PAD + L] = jnp.swapaxes(main, 0, 1)

    def ibody(i, carry):
        ext[i] = p[pl.ds(i + 1, SUBLANES, stride=L), :]
        ext[i + CONV_PAD + L] = p[pl.ds(i + CONV_PAD + L + 1, SUBLANES, stride=L), :]
        return carry

    lax.fori_loop(0, CONV_PAD, ibody, 0, unroll=3)


def _conv_specs(T, tt):
    main = pl.BlockSpec((tt, D), lambda i: (i, 0))
    per = tt // HALO
    prev = pl.BlockSpec((HALO, D), lambda i: (jnp.maximum(i * per - 1, 0), 0))
    nxt = pl.BlockSpec((HALO, D), lambda i: (jnp.minimum((i + 1) * per, T // HALO - 1), 0))
    return main, prev, nxt


def _fill_pad(pad, main_ref, prev_ref, next_ref, i, n_i, tt, nlt):
    keep_p = (i > 0).astype(F32)
    keep_n = (i < n_i - 1).astype(F32)
    for lt in range(nlt):
        sl = slice(lt * LANES, (lt + 1) * LANES)
        pad[lt, 0:HALO, :] = prev_ref[:, sl] * keep_p
        pad[lt, HALO:HALO + tt, :] = main_ref[:, sl]
        pad[lt, HALO + tt:2 * HALO + tt, :] = next_ref[:, sl] * keep_n


def conv_fwd(glu, w_dw, b_dw, ln_g, ln_b, comm=None):
    T = glu.shape[0]
    tt, L, nlt = _conv_tiles(T)
    n_i = T // tt
    main, prev, nxt = _conv_specs(T, tt)

    def body(x_ref, xp_ref, xn_ref, w_ref, b_ref, g_ref, bb_ref, dwc_ref, sw_ref, pad, ob, ext, wk):
        i = pl.program_id(0)
        _fill_pad(pad, x_ref, xp_ref, xn_ref, i, n_i, tt, nlt)
        for lt in range(nlt):
            sl = slice(lt * LANES, (lt + 1) * LANES)
            o = ob.at[lt]
            _fill_strided(ext, pad.at[lt], L)
            for k in range(CONV_W):
                wk[k] = jnp.broadcast_to(w_ref[k:k + 1, sl], (SUBLANES, LANES))

            def jbody(jb, carry):
                j = jb * CONV_JB
                accs = [None] * CONV_JB
                for m in range(CONV_W + CONV_JB - 1):
                    e = ext[j + m]
                    for u in range(CONV_JB):
                        if 0 <= m - u < CONV_W:
                            t = e * wk[m - u]
                            accs[u] = t if accs[u] is None else accs[u] + t
                for u in range(CONV_JB):
                    o[pl.ds(j + u, SUBLANES, stride=L), :] = accs[u]
                return carry

            lax.fori_loop(0, L // CONV_JB, jbody, 0)
        y = jnp.concatenate([ob[lt] for lt in range(nlt)], axis=1) + b_ref[...]
        dwc_ref[...] = y
        mu = jnp.mean(y, axis=-1, keepdims=True)
        yc = y - mu
        var = jnp.mean(yc * yc, axis=-1, keepdims=True)
        z = yc * lax.rsqrt(var + EPS) * g_ref[...] + bb_ref[...]
        sw_ref[...] = (z * _sigmoid(z)).astype(BF16)

    full = lambda i: (0, 0)
    (dwc, sw), got = _call(
        body, name="conv_fwd", grid=(n_i,),
        in_specs=[main, prev, nxt, pl.BlockSpec((32, D), full), pl.BlockSpec((1, D), full),
                  pl.BlockSpec((1, D), full), pl.BlockSpec((1, D), full)],
        out_specs=[pl.BlockSpec((tt, D), lambda i: (i, 0)), pl.BlockSpec((tt, D), lambda i: (i, 0))],
        out_shape=[jax.ShapeDtypeStruct((T, D), F32), jax.ShapeDtypeStruct((T, D), BF16)],
        scratch_shapes=[pltpu.VMEM((nlt, tt + 2 * HALO, LANES), F32), pltpu.VMEM((nlt, tt, LANES), F32),
                        pltpu.VMEM((L + 2 * HALO, SUBLANES, LANES), F32), pltpu.VMEM((32, SUBLANES, LANES), F32)],
        semantics=("parallel",), args=(glu, glu, glu, w_dw, b_dw, ln_g, ln_b), comm=comm)
    return dwc, sw, got


def mm_res_loss(a, w, resid, gain, target):
    T, K = a.shape
    tm = _tile(T, 512)

    def body(a_ref, w_ref, r_ref, g_ref, t_ref, dx_ref, loss_ref, dg_ref):
        @pl.when(pl.program_id(0) == 0)
        def _():
            loss_ref[...] = jnp.zeros_like(loss_ref)
            dg_ref[...] = jnp.zeros_like(dg_ref)

        xv, gain_v = _dot(a_ref[...], w_ref[...]) + r_ref[...], g_ref[...]
        err = _rms_fwd(xv, gain_v) - t_ref[...]
        part = 0.5 * jnp.sum(jnp.mean(err * err, axis=-1, keepdims=True), axis=0, keepdims=True)
        loss_ref[...] += jnp.broadcast_to(part, loss_ref.shape)
        dx, dgr = _rms_bwd(err * (1.0 / D), xv, gain_v, 0.0)
        dx_ref[...] = dx
        dg_ref[...] += jnp.sum(dgr, axis=0, keepdims=True)

    row = lambda i: (i, 0)
    full = lambda i: (0, 0)
    return pl.pallas_call(
        body, name="ffn1_down_loss", grid=(T // tm,),
        in_specs=[pl.BlockSpec((tm, K), row), pl.BlockSpec((K, D), full), pl.BlockSpec((tm, D), row),
                  pl.BlockSpec((1, D), full), pl.BlockSpec((tm, D), row)],
        out_specs=[pl.BlockSpec((tm, D), row), pl.BlockSpec((1, LANES), full), pl.BlockSpec((1, D), full)],
        out_shape=[jax.ShapeDtypeStruct((T, D), F32), jax.ShapeDtypeStruct((1, LANES), F32),
                   jax.ShapeDtypeStruct((1, D), F32)],
        compiler_params=_params("arbitrary"),
    )(a, w, resid, gain, target)


def swiglu_bwd(dx, w_down, pre, name, comm=None):
    T = dx.shape[0]
    H = w_down.shape[0]
    tm = _tile(T, 512)

    def body(dx_ref, w_ref, pre_ref, dpre_ref):
        dxb = dx_ref[...].astype(BF16)
        for cs in _col_chunks(H):
            dact = _dot_tb(dxb, w_ref[cs, :])
            g = pre_ref[0, :, cs].astype(F32)
            u = pre_ref[1, :, cs].astype(F32)
            sg = _sigmoid(g)
            dpre_ref[0, :, cs] = (dact * u * sg * (1.0 + g * (1.0 - sg))).astype(BF16)
            dpre_ref[1, :, cs] = (dact * g * sg).astype(BF16)

    (dpre,), got = _call(
        body, name=name, grid=(T // tm,),
        in_specs=[pl.BlockSpec((tm, D), lambda i: (i, 0)),
                  pl.BlockSpec((H, D), lambda i: (0, 0), pipeline_mode=pl.Buffered(1)),
                  pl.BlockSpec((2, tm, H), lambda i: (0, i, 0))],
        out_specs=[pl.BlockSpec((2, tm, H), lambda i: (0, i, 0))],
        out_shape=[jax.ShapeDtypeStruct((2, T, H), BF16)],
        semantics=("parallel",), args=(dx, w_down, pre), comm=comm)
    return dpre, got


def _ln_silu_bwd(dsw, y, ln_g, ln_b):
    mu = jnp.mean(y, axis=-1, keepdims=True)
    yc = y - mu
    rstd = lax.rsqrt(jnp.mean(yc * yc, axis=-1, keepdims=True) + EPS)
    xhat = yc * rstd
    z = xhat * ln_g + ln_b
    sg = _sigmoid(z)
    dz = dsw * sg * (1.0 + z * (1.0 - sg))
    dxh = dz * ln_g
    dy = rstd * (dxh - jnp.mean(dxh, axis=-1, keepdims=True) - xhat * jnp.mean(dxh * xhat, axis=-1, keepdims=True))
    return dy, dz * xhat, dz


def mm_bt_rmsbwd(dpre, w, x, gain, dres, name, comm=None, proj_w=None, conv_tail=None):
    nh, T, H = dpre.shape
    tm = _tile(T, 512)
    n_extra_in = 1 if proj_w is not None else (4 if conv_tail is not None else 0)

    def body(*refs):
        dp_ref, w_ref, x_ref, g_ref, dres_ref = refs[:5]
        extra_in = refs[5:5 + n_extra_in]
        dx_ref, dg_ref = refs[5 + n_extra_in:7 + n_extra_in]
        extra_out = refs[7 + n_extra_in:]

        @pl.when(pl.program_id(0) == 0)
        def _():
            dg_ref[...] = jnp.zeros_like(dg_ref)
            for r in extra_out[1:]:
                r[...] = jnp.zeros_like(r)

        dh = _dot_tb(dp_ref[0], w_ref[:, 0:H])
        for hf in range(1, nh):
            dh = dh + _dot_tb(dp_ref[hf], w_ref[:, hf * H:(hf + 1) * H])
        dx, dgr = _rms_bwd(dh, x_ref[...], g_ref[...], dres_ref[...])
        dx_ref[...] = dx
        dg_ref[...] += jnp.sum(dgr, axis=0, keepdims=True)
        if proj_w is not None:
            extra_out[0][...] = _dot_tb(dx.astype(BF16), extra_in[0][...]).astype(BF16)
        elif conv_tail is not None:
            wt_ref, y_ref, lg_ref, lb_ref = extra_in
            dy, dgl, dbl = _ln_silu_bwd(_dot_tb(dx.astype(BF16), wt_ref[...]), y_ref[...], lg_ref[...], lb_ref[...])
            extra_out[0][...] = dy
            extra_out[1][...] += jnp.sum(dgl, axis=0, keepdims=True)
            extra_out[2][...] += jnp.sum(dbl, axis=0, keepdims=True)
            extra_out[3][...] += jnp.sum(dx, axis=0, keepdims=True)

    row = lambda i: (i, 0)
    full = lambda i: (0, 0)
    vec = pl.BlockSpec((1, D), full)
    vec_shape = jax.ShapeDtypeStruct((1, D), F32)
    in_specs = [pl.BlockSpec((nh, tm, H), lambda i: (0, i, 0)),
                pl.BlockSpec((D, nh * H), full, pipeline_mode=pl.Buffered(1)),
                pl.BlockSpec((tm, D), row), vec, pl.BlockSpec((tm, D), row)]
    out_specs = [pl.BlockSpec((tm, D), row), vec]
    out_shape = [jax.ShapeDtypeStruct((T, D), F32), vec_shape]
    args = (dpre, w, x, gain, dres)
    if proj_w is not None:
        N = proj_w.shape[0]
        in_specs.append(pl.BlockSpec((N, D), full, pipeline_mode=pl.Buffered(1)))
        out_specs.append(pl.BlockSpec((tm, N), row))
        out_shape.append(jax.ShapeDtypeStruct((T, N), BF16))
        args += (proj_w,)
    elif conv_tail is not None:
        in_specs += [pl.BlockSpec((D, D), full, pipeline_mode=pl.Buffered(1)), pl.BlockSpec((tm, D), row), vec, vec]
        out_specs += [pl.BlockSpec((tm, D), row), vec, vec, vec]
        out_shape += [jax.ShapeDtypeStruct((T, D), F32), vec_shape, vec_shape, vec_shape]
        args += tuple(conv_tail)
    outs, got = _call(body, name=name, grid=(T // tm,), in_specs=in_specs, out_specs=out_specs, out_shape=out_shape,
                      semantics=("arbitrary",), args=args, comm=comm)
    return (*outs, got)


def dw_col(a, dpre, name):
    T = a.shape[0]
    nh, _, H = dpre.shape
    per = nh * H // N_CHIPS
    bph = N_CHIPS // nh
    tt = _tile(T, 2048)
    nt = T // tt

    def body(a_ref, b_ref, o_ref, acc):
        t = pl.program_id(1)

        @pl.when(t == 0)
        def _():
            acc[...] = jnp.zeros_like(acc)

        acc[...] += _dot_ta(a_ref[...], b_ref[...])

        @pl.when(t == nt - 1)
        def _():
            o_ref[...] = acc[...].astype(BF16)

    return pl.pallas_call(
        body, name=name, grid=(N_CHIPS, nt),
        in_specs=[pl.BlockSpec((tt, D), lambda q, t: (t, 0)),
                  pl.BlockSpec((None, tt, per), lambda q, t: (q // bph, t, q % bph))],
        out_specs=pl.BlockSpec((None, D, per), lambda q, t: (q, 0, 0)),
        out_shape=jax.ShapeDtypeStruct((N_CHIPS, D, per), BF16),
        scratch_shapes=[pltpu.VMEM((D, per), F32)],
        compiler_params=_params("parallel", "arbitrary"),
    )(a, dpre)


def dw_row(a, b, name):
    T, R = a.shape
    cw = 1408 if R % 1408 == 0 else R
    tt = _tile(T, 2048 if R <= D else 1024)
    nt = T // tt

    def body(a_ref, b_ref, o_ref, acc):
        t = pl.program_id(1)

        @pl.when(t == 0)
        def _():
            acc[...] = jnp.zeros_like(acc)

        acc[...] += _dot_ta(a_ref[...], b_ref[...].astype(BF16))

        @pl.when(t == nt - 1)
        def _():
            o_ref[...] = acc[...].astype(BF16)

    out = pl.pallas_call(
        body, name=name, grid=(R // cw, nt),
        in_specs=[pl.BlockSpec((tt, cw), lambda q, t: (t, q)), pl.BlockSpec((tt, D), lambda q, t: (t, 0))],
        out_specs=pl.BlockSpec((cw, D), lambda q, t: (q, 0)),
        out_shape=jax.ShapeDtypeStruct((R, D), BF16),
        scratch_shapes=[pltpu.VMEM((cw, D), F32)],
        compiler_params=_params("parallel", "arbitrary"),
    )(a, b)
    return out.reshape(N_CHIPS, R // N_CHIPS, D)


def conv_bwd(ddwc, glu, pre, w_dw, comm=None):
    T = ddwc.shape[0]
    tt, L, nlt = _conv_tiles(T)
    n_i = T // tt
    main, prev, nxt = _conv_specs(T, tt)

    def body(d_ref, dp_ref, dn_ref, x_ref, xp_ref, xn_ref, pre_ref, w_ref,
             dpre_ref, dw_ref, dbd_ref, dbp_ref, padd, padx, ob, extd, extx, wk):
        i = pl.program_id(0)

        @pl.when(i == 0)
        def _():
            dw_ref[...] = jnp.zeros_like(dw_ref)
            dbd_ref[...] = jnp.zeros_like(dbd_ref)
            dbp_ref[...] = jnp.zeros_like(dbp_ref)

        _fill_pad(padd, d_ref, dp_ref, dn_ref, i, n_i, tt, nlt)
        _fill_pad(padx, x_ref, xp_ref, xn_ref, i, n_i, tt, nlt)
        for lt in range(nlt):
            sl = slice(lt * LANES, (lt + 1) * LANES)
            o = ob.at[lt]
            _fill_strided(extd, padd.at[lt], L)
            _fill_strided(extx, padx.at[lt], L)
            for k in range(CONV_W):
                wk[k] = jnp.broadcast_to(w_ref[k:k + 1, sl], (SUBLANES, LANES))

            nu = CONV_JB_BWD

            def jbody(jb, accs):
                j = jb * nu
                accs = list(accs)
                d = [extd[j + u + CONV_PAD] for u in range(nu)]
                g = [None] * nu
                for m in range(CONV_W + nu - 1):
                    ed = extd[j + 2 * CONV_PAD + nu - 1 - m]
                    ex = extx[j + m]
                    for u in range(nu):
                        k = m - (nu - 1 - u)
                        if 0 <= k < CONV_W:
                            t = ed * wk[k]
                            g[u] = t if g[u] is None else g[u] + t
                        k = m - u
                        if 0 <= k < CONV_W:
                            accs[k] = accs[k] + d[u] * ex
                for u in range(nu):
                    o[pl.ds(j + u, SUBLANES, stride=L), :] = g[u]
                return tuple(accs)

            accs = lax.fori_loop(0, L // nu, jbody, tuple(jnp.zeros((SUBLANES, LANES), F32) for _ in range(CONV_W)))
            for k in range(CONV_W):
                dw_ref[k:k + 1, sl] += jnp.sum(accs[k], axis=0, keepdims=True)
        dglu = jnp.concatenate([ob[lt] for lt in range(nlt)], axis=1)
        a = pre_ref[0].astype(F32)
        gate = pre_ref[1].astype(F32)
        sg = _sigmoid(gate)
        da = dglu * sg
        dgate = dglu * a * sg * (1.0 - sg)
        dpre_ref[0] = da.astype(BF16)
        dpre_ref[1] = dgate.astype(BF16)
        dbd_ref[...] += jnp.sum(d_ref[...], axis=0, keepdims=True)
        dbp_ref[0] += jnp.sum(da, axis=0, keepdims=True)
        dbp_ref[1] += jnp.sum(dgate, axis=0, keepdims=True)

    full = lambda i: (0, 0)
    (dpre, dw, dbd, dbp), got = _call(
        body, name="conv_bwd", grid=(n_i,),
        in_specs=[main, prev, nxt, main, prev, nxt, pl.BlockSpec((2, tt, D), lambda i: (0, i, 0)),
                  pl.BlockSpec((32, D), full)],
        out_specs=[pl.BlockSpec((2, tt, D), lambda i: (0, i, 0)), pl.BlockSpec((32, D), full),
                   pl.BlockSpec((1, D), full), pl.BlockSpec((2, 1, D), lambda i: (0, 0, 0))],
        out_shape=[jax.ShapeDtypeStruct((2, T, D), BF16), jax.ShapeDtypeStruct((32, D), F32),
                   jax.ShapeDtypeStruct((1, D), F32), jax.ShapeDtypeStruct((2, 1, D), F32)],
        scratch_shapes=[pltpu.VMEM((nlt, tt + 2 * HALO, LANES), F32), pltpu.VMEM((nlt, tt + 2 * HALO, LANES), F32),
                        pltpu.VMEM((nlt, tt, LANES), F32), pltpu.VMEM((L + 2 * HALO, SUBLANES, LANES), F32),
                        pltpu.VMEM((L + 2 * HALO, SUBLANES, LANES), F32), pltpu.VMEM((32, SUBLANES, LANES), F32)],
        semantics=("arbitrary",), args=(ddwc, ddwc, ddwc, glu, glu, glu, pre, w_dw), comm=comm)
    return dpre, dw, dbd, dbp, got


def attn_bwd(qkv, o, do, sink, rc, rs1, rs2, comm=None):
    T = qkv.shape[0]
    nb, q_spec, prev, own, nxt = _attn_specs(T)
    kvw = N_KV * HD

    def body(sink_ref, q_ref, kp_ref, ko_ref, kn_ref, o_ref, do_ref, c_ref, s1_ref, s2_ref,
             dq_ref, dkc_ref, dvc_ref, dsink_ref):
        n = pl.program_id(0)

        @pl.when(n == 0)
        def _():
            dsink_ref[...] = jnp.zeros_like(dsink_ref)

        valid = _attn_mask(n, T)
        kv = jnp.concatenate([kp_ref[...], ko_ref[...], kn_ref[...]], axis=0)
        kx, vx = _kv_padded(kv, 0), _kv_padded(kv, 2)
        tile = lambda ref, j: ref[:, j * LANES:(j + 1) * LANES]
        ss = _pair_products(kx, lambda j: tile(q_ref, j))
        dps = _pair_products(vx, lambda j: tile(do_ref, j))
        low_d = lax.broadcasted_iota(jnp.int32, (LANES, BLK), 0) < HD
        deltas = []
        for j in range(N_HEADS // 2):
            prod_t = tile(do_ref, j).astype(F32).T * tile(o_ref, j).astype(F32).T
            deltas.append(jnp.sum(jnp.where(low_d, prod_t, 0.0), axis=0, keepdims=True))
            deltas.append(jnp.sum(jnp.where(low_d, 0.0, prod_t), axis=0, keepdims=True))
        lane = lax.broadcasted_iota(jnp.int32, (1, N_HEADS), 1)
        dsink = jnp.zeros((1, N_HEADS), F32)
        pbs, dss = [], []
        for h in range(N_HEADS):
            p, p_sink = _softmax_sink(ss[h], valid, sink_ref[h])
            dss.append((p * (dps[h] - deltas[h])).astype(BF16))
            pbs.append(p.astype(BF16))
            part = -jnp.sum(p_sink * deltas[h], axis=1, keepdims=True)
            dsink = dsink + jnp.where(lane == h, part, 0.0)
        dsink_ref[...] += dsink
        c, s1, s2 = c_ref[...], s1_ref[...], s2_ref[...]
        kxt = {k: v.T for k, v in kx.items()}
        for j in range(N_HEADS // 2):
            g = 2 * j // GROUP
            dq_t = _dot(kxt[g, 0], dss[2 * j]) + _dot(kxt[g, 1], dss[2 * j + 1])
            dq_ref[:, j * LANES:(j + 1) * LANES] = (_rope(dq_t.T, c, -s1, -s2) * Q_SCALE).astype(BF16)
        low_k = lax.broadcasted_iota(jnp.int32, (3 * BLK, LANES), 1) < HD
        cols = lambda xs, g, p: jnp.concatenate([xs[GROUP * g + p], xs[GROUP * g + 2 + p]], axis=1)
        for t in range(N_KV // 2):
            sums = {}
            for g in (2 * t, 2 * t + 1):
                q2 = jnp.concatenate([tile(q_ref, 2 * g), tile(q_ref, 2 * g + 1)], axis=0)
                do2 = jnp.concatenate([tile(do_ref, 2 * g), tile(do_ref, 2 * g + 1)], axis=0)
                dk2 = _dot(jnp.concatenate([cols(dss, g, 0), cols(dss, g, 1)], axis=0), q2)
                dv2 = _dot(jnp.concatenate([cols(pbs, g, 0), cols(pbs, g, 1)], axis=0), do2)
                for p in range(2):
                    sums[g, p] = (dk2[p * 3 * BLK:(p + 1) * 3 * BLK], dv2[p * 3 * BLK:(p + 1) * 3 * BLK])
            for which, ref in ((0, dkc_ref), (1, dvc_ref)):
                keep = jnp.where(low_k, sums[2 * t, 0][which], sums[2 * t + 1, 1][which])
                swap = jnp.where(low_k, sums[2 * t + 1, 0][which], sums[2 * t, 1][which])
                ref[:, t * LANES:(t + 1) * LANES] = keep + pltpu.roll(swap, HD, 1)

    row = lambda n: (n, 0)
    (dq, dkc, dvc, dsink), got = _call(
        body, name="attn_bwd", grid=(nb,),
        in_specs=[pl.BlockSpec(memory_space=pltpu.SMEM), q_spec, prev, own, nxt,
                  pl.BlockSpec((BLK, D), row), pl.BlockSpec((BLK, D), row), *_tab_specs(BLK)],
        out_specs=[pl.BlockSpec((BLK, D), row), pl.BlockSpec((None, 3 * BLK, kvw), lambda n: (n, 0, 0)),
                   pl.BlockSpec((None, 3 * BLK, kvw), lambda n: (n, 0, 0)), pl.BlockSpec((1, N_HEADS), lambda n: (0, 0))],
        out_shape=[jax.ShapeDtypeStruct((T, QKV), BF16), jax.ShapeDtypeStruct((nb, 3 * BLK, kvw), F32),
                   jax.ShapeDtypeStruct((nb, 3 * BLK, kvw), F32), jax.ShapeDtypeStruct((1, N_HEADS), F32)],
        semantics=("arbitrary",), args=(sink, qkv, qkv, qkv, qkv, o, do, rc, rs1, rs2), comm=comm)
    return dq, dkc, dvc, dsink, got


def kv_sum(dqkv, dkc, dvc, rc, rs1, rs2):
    nb = dkc.shape[0]
    T = nb * BLK
    kvw = N_KV * HD

    G = 8
    ng = nb // G

    def gather3(own_ref, prev_ref, before_ref, next_ref, after_ref, m):
        has_before = (m > 0).astype(F32)
        has_after = (m < ng - 1).astype(F32)
        out = []
        for i in range(G):
            from_prev = prev_ref[i - 1] if i > 0 else before_ref[0] * has_before
            from_next = next_ref[i + 1] if i < G - 1 else after_ref[0] * has_after
            out.append(from_prev + own_ref[i] + from_next)
        return jnp.concatenate(out, axis=0)

    def body(_, ko, kp, kb, kn, ka, vo, vp, vb, vn, va, c_ref, s1_ref, s2_ref, out_ref):
        m = pl.program_id(0)
        dk = gather3(ko, kp, kb, kn, ka, m)
        dv = gather3(vo, vp, vb, vn, va, m)
        c, s1, s2 = c_ref[...], s1_ref[...], s2_ref[...]
        for j in range(kvw // LANES):
            sl = slice(LANES * j, LANES * (j + 1))
            out_ref[:, sl] = _rope(dk[:, sl], c, -s1, -s2).astype(BF16)
        out_ref[:, kvw:] = dv.astype(BF16)

    own = pl.BlockSpec((G, BLK, kvw), lambda m: (m, 1, 0))
    prev = pl.BlockSpec((G, BLK, kvw), lambda m: (m, 2, 0))
    before = pl.BlockSpec((1, BLK, kvw), lambda m: (jnp.maximum(G * m - 1, 0), 2, 0))
    nxt = pl.BlockSpec((G, BLK, kvw), lambda m: (m, 0, 0))
    after = pl.BlockSpec((1, BLK, kvw), lambda m: (jnp.minimum(G * m + G, nb - 1), 0, 0))
    five = [own, prev, before, nxt, after]
    return pl.pallas_call(
        body, name="kv_sum", grid=(ng,),
        in_specs=[pl.BlockSpec(memory_space=pl.ANY), *five, *five, *_tab_specs(G * BLK)],
        out_specs=pl.BlockSpec((G * BLK, 2 * kvw), lambda m: (m, KV_OFF // (2 * kvw))),
        out_shape=jax.ShapeDtypeStruct((T, QKV), BF16),
        input_output_aliases={0: 0},
        compiler_params=_params("parallel"),
    )(dqkv, *([dkc] * 5), *([dvc] * 5), rc, rs1, rs2)


def _me():
    return lax.axis_index("x"), lax.axis_index("y"), lax.axis_index("c")


def _half_rows(ref, sharded_rows, chip, core):
    R, C = ref.shape[-2], ref.shape[-1]
    lead = (slice(None),) * (len(ref.shape) - 2)
    if sharded_rows:
        per = R // N_CHIPS
        return ref.at[lead + (pl.ds(chip * per + core * (per // 2), per // 2), slice(None))]
    per = C // N_CHIPS
    return ref.at[lead + (pl.ds(core * (R // 2), R // 2), pl.ds(chip * per, per))]


class _Gather:
    def __init__(self, shards, sharded_rows):
        self.inputs = list(shards)
        self.rows = list(sharded_rows)
        self.n = self.n_in = self.n_out = len(shards)
        self.out_shapes = []
        for s, rows in zip(shards, sharded_rows):
            shp = list(s.shape)
            shp[-2 if rows else -1] *= N_CHIPS
            self.out_shapes.append(jax.ShapeDtypeStruct(tuple(shp), s.dtype))
        self.scratch = [pltpu.SemaphoreType.DMA((self.n, 6)), pltpu.SemaphoreType.DMA((self.n, 6)),
                        pltpu.SemaphoreType.DMA((self.n, 2))]

    def _ctx(self, ins, outs, sems):
        send_sems, recv_sems, local_sems = sems
        x, y, c = _me()
        chips = [(1 - x, y), (x, 1 - y), (1 - x, 1 - y)]

        def half_src(w, core):
            s = ins[w]
            R = s.shape[-2]
            return s.at[pl.ds(core * (R // 2), R // 2), :]

        def dst(w, chip, core):
            return _half_rows(outs[w], self.rows[w], chip, core)

        def copy(w, k, src, chip, core, to):
            return pltpu.make_async_remote_copy(
                src_ref=src, dst_ref=dst(w, chip, core), send_sem=send_sems.at[w, k], recv_sem=recv_sems.at[w, k],
                device_id=to, device_id_type=MESH)

        def local(w, core):
            return pltpu.make_async_copy(half_src(w, core), dst(w, 2 * x + y, core), local_sems.at[w, core])

        def first(w, j):
            qx, qy = chips[j]
            return copy(w, j, half_src(w, c), 2 * x + y, c, (qx, qy, c))

        def landed(w, j):
            qx, qy = chips[j]
            return copy(w, j, dst(w, 2 * qx + qy, c), 2 * qx + qy, c, (x, y, c))

        def passed(w, j):
            qx, qy = chips[j]
            return copy(w, 3 + j, dst(w, 2 * qx + qy, c), 2 * qx + qy, c, (x, y, 1 - c))

        def from_sibling(w, j):
            qx, qy = chips[j]
            return copy(w, 3 + j, dst(w, 2 * qx + qy, 1 - c), 2 * qx + qy, 1 - c, (x, y, c))

        return local, first, landed, passed, from_sibling

    def start(self, ins, outs, sems):
        local, first, _, _, _ = self._ctx(ins, outs, sems)
        for w in range(self.n):
            for core in range(2):
                local(w, core).start()
            for j in range(3):
                first(w, j).start()

    def mid(self, ins, outs, sems):
        _, _, landed, passed, _ = self._ctx(ins, outs, sems)
        for w in range(self.n):
            for j in range(3):
                landed(w, j).wait_recv()
                passed(w, j).start()

    def end(self, ins, outs, sems):
        local, first, _, passed, from_sibling = self._ctx(ins, outs, sems)
        for w in range(self.n):
            for j in range(3):
                from_sibling(w, j).wait_recv()
        for w in range(self.n):
            for j in range(3):
                first(w, j).wait_send()
                passed(w, j).wait_send()
            for core in range(2):
                local(w, core).wait()


class _Scatter:
    def __init__(self, grads, small=None):
        self.inputs = list(grads) + ([small] if small is not None else [])
        self.ng = len(grads)
        self.n = self.n_in = self.n_out = len(self.inputs)
        self.out_shapes = [jax.ShapeDtypeStruct((N_DEV, g.shape[1] // 2, g.shape[2]), g.dtype) for g in grads]
        if small is not None:
            self.out_shapes.append(jax.ShapeDtypeStruct((N_DEV,) + small.shape, small.dtype))
        self.scratch = [pltpu.SemaphoreType.DMA((self.n, N_DEV)), pltpu.SemaphoreType.DMA((self.n, N_DEV)),
                        pltpu.SemaphoreType.DMA((self.n,))]

    def _ctx(self, ins, outs, sems):
        send_sems, recv_sems, local_sems = sems
        x, y, c = _me()
        me = 4 * x + 2 * y + c

        def piece(w, chip, core):
            if w >= self.ng:
                return ins[w]
            half = ins[w].shape[1] // 2
            return ins[w].at[chip, pl.ds(core * half, half), :]

        def peer_of(k):
            return x ^ ((k >> 2) & 1), y ^ ((k >> 1) & 1), c ^ (k & 1)

        def local(w):
            return pltpu.make_async_copy(piece(w, 2 * x + y, c), outs[w].at[me], local_sems.at[w])

        def send(w, k):
            px, py, pc = peer_of(k)
            return pltpu.make_async_remote_copy(
                src_ref=piece(w, 2 * px + py, pc), dst_ref=outs[w].at[me], send_sem=send_sems.at[w, k],
                recv_sem=recv_sems.at[w, k], device_id=(px, py, pc), device_id_type=MESH)

        def recv(w, k):
            px, py, pc = peer_of(k)
            return pltpu.make_async_remote_copy(
                src_ref=piece(w, 2 * x + y, c), dst_ref=outs[w].at[4 * px + 2 * py + pc], send_sem=send_sems.at[w, k],
                recv_sem=recv_sems.at[w, k], device_id=(px, py, pc), device_id_type=MESH)

        return local, send, recv

    def start(self, ins, outs, sems):
        local, send, _ = self._ctx(ins, outs, sems)
        for w in range(self.n):
            local(w).start()
            for k in range(1, N_DEV):
                send(w, k).start()

    def mid(self, ins, outs, sems):
        pass

    def end(self, ins, outs, sems):
        local, send, recv = self._ctx(ins, outs, sems)
        for w in range(self.n):
            for k in range(1, N_DEV):
                recv(w, k).wait_recv()
        for w in range(self.n):
            for k in range(1, N_DEV):
                send(w, k).wait_send()
            local(w).wait()


class _Both:
    def __init__(self, a, b):
        self.a, self.b = a, b
        self.inputs = a.inputs + b.inputs
        self.out_shapes = a.out_shapes + b.out_shapes
        self.scratch = a.scratch + b.scratch
        self.n_in, self.n_out = a.n_in + b.n_in, a.n_out + b.n_out

    def _split(self, ins, outs, sems):
        a, na = self.a, len(self.a.scratch)
        return (ins[:a.n_in], outs[:a.n_out], sems[:na]), (ins[a.n_in:], outs[a.n_out:], sems[na:])

    def start(self, ins, outs, sems):
        pa, pb = self._split(ins, outs, sems)
        self.a.start(*pa)
        self.b.start(*pb)

    def mid(self, ins, outs, sems):
        pa, pb = self._split(ins, outs, sems)
        self.a.mid(*pa)
        self.b.mid(*pb)

    def end(self, ins, outs, sems):
        pa, pb = self._split(ins, outs, sems)
        self.a.end(*pa)
        self.b.end(*pb)


def _call(body, *, name, grid, in_specs, out_specs, out_shape, scratch_shapes=(), semantics, args, comm=None):
    if comm is None:
        outs = pl.pallas_call(
            body, name=name, grid=grid, in_specs=in_specs, out_specs=out_specs, out_shape=out_shape,
            scratch_shapes=list(scratch_shapes), compiler_params=_params(*semantics))(*args)
        return outs, []
    n_in, n_out, n_scr = len(in_specs), len(out_specs), len(scratch_shapes)

    total = math.prod(grid)
    first, middle, last = 0, (3 * total) // 4 - 1, total - 1
    assert first <= middle < last

    def at(step):
        lin = pl.program_id(0)
        for d in range(1, len(grid)):
            lin = lin * grid[d] + pl.program_id(d)
        return lin == step

    def hosted(*refs):
        h_in, c_in = refs[:n_in], refs[n_in:n_in + comm.n_in]
        rest = refs[n_in + comm.n_in:]
        h_out, c_out = rest[:n_out], rest[n_out:n_out + comm.n_out]
        rest = rest[n_out + comm.n_out:]
        h_scr, c_scr = rest[:n_scr], rest[n_scr:]

        @pl.when(at(first))
        def _():
            comm.start(c_in, c_out, c_scr)

        body(*h_in, *h_out, *h_scr)

        @pl.when(at(middle))
        def _():
            comm.mid(c_in, c_out, c_scr)

        @pl.when(at(last))
        def _():
            comm.end(c_in, c_out, c_scr)

    any_spec = pl.BlockSpec(memory_space=pl.ANY)
    outs = pl.pallas_call(
        hosted, name=name, grid=grid, in_specs=list(in_specs) + [any_spec] * comm.n_in,
        out_specs=list(out_specs) + [any_spec] * comm.n_out, out_shape=list(out_shape) + comm.out_shapes,
        scratch_shapes=list(scratch_shapes) + comm.scratch,
        compiler_params=_params(*(["arbitrary"] * len(grid))))(*args, *comm.inputs)
    return outs[:n_out], outs[n_out:]


def sum_swap(pieces, name, comm=None):
    nl = len(pieces)
    _, r2, cc = pieces[0].shape
    tr = 256 if r2 % 256 == 0 else (128 if r2 % 128 == 0 else r2 // 2)
    n = r2 // tr

    def body(*refs):
        p_refs, out = refs[:nl], refs[nl]
        slots, send_sems, local_sems, recv_sem = refs[nl + 1:]
        x, y, c = _me()
        sibling = (x, y, 1 - c)
        l, i = pl.program_id(0), pl.program_id(1)
        step = l * n + i

        def rows(st, core):
            return out.at[st // n, pl.ds(core * r2 + (st % n) * tr, tr), :]

        def copies(st):
            slot = st % 2
            local = pltpu.make_async_copy(slots.at[slot], rows(st, c), local_sems.at[slot])
            remote = pltpu.make_async_remote_copy(
                src_ref=slots.at[slot], dst_ref=rows(st, c), send_sem=send_sems.at[slot], recv_sem=recv_sem,
                device_id=sibling, device_id_type=MESH)
            return local, remote

        for ll in range(nl):
            @pl.when(l == ll)
            def _():
                acc = p_refs[ll][0].astype(F32)
                for d in range(1, N_DEV):
                    acc = acc + p_refs[ll][d].astype(F32)
                slots[step % 2] = acc

        for cp in copies(step):
            cp.start()

        @pl.when(step >= 1)
        def _():
            local, remote = copies(step - 1)
            local.wait()
            remote.wait_send()

        @pl.when(step == nl * n - 1)
        def _():
            local, remote = copies(step)
            local.wait()
            remote.wait_send()
            theirs = out.at[:, pl.ds((1 - c) * r2, r2), :]
            pltpu.make_async_remote_copy(src_ref=theirs, dst_ref=theirs, send_sem=send_sems.at[0],
                                         recv_sem=recv_sem, device_id=sibling, device_id_type=MESH).wait_recv()

    def piece_spec(ll):
        def index(l, i):
            return (0, jnp.where(l == ll, i, jnp.where(l < ll, 0, n - 1)), 0)
        return pl.BlockSpec((N_DEV, tr, cc), index)

    (out,), got = _call(
        body, name=name, grid=(nl, n),
        in_specs=[piece_spec(ll) for ll in range(nl)],
        out_specs=[pl.BlockSpec(memory_space=pl.ANY)],
        out_shape=[jax.ShapeDtypeStruct((nl, 2 * r2, cc), F32)],
        scratch_shapes=[pltpu.VMEM((2, tr, cc), F32), pltpu.SemaphoreType.DMA((2,)), pltpu.SemaphoreType.DMA((2,)),
                        pltpu.SemaphoreType.DMA(())],
        semantics=("arbitrary", "arbitrary"), args=tuple(pieces), comm=comm)
    return (out, got) if comm is not None else out


def sum_pieces(pieces, name):
    _, R, C = pieces.shape
    tr = _tile(R, 128) if R % 128 == 0 else R

    def body(p_ref, o_ref):
        acc = p_ref[0].astype(F32)
        for d in range(1, N_DEV):
            acc = acc + p_ref[d].astype(F32)
        o_ref[...] = acc

    return pl.pallas_call(
        body, name=name, grid=(R // tr,),
        in_specs=[pl.BlockSpec((N_DEV, tr, C), lambda i: (0, i, 0))],
        out_specs=pl.BlockSpec((tr, C), lambda i: (i, 0)),
        out_shape=jax.ShapeDtypeStruct((R, C), F32),
        compiler_params=_params("parallel"),
    )(pieces)


def adamw(w, g, m, v, name):
    Lyr, R, C = w.shape
    tr = _tile(R, 256) if R % 8 == 0 else R
    c1 = 1.0 / (1.0 - ADAM_B1 ** ADAM_STEP)
    c2 = 1.0 / (1.0 - ADAM_B2 ** ADAM_STEP)

    def body(w_ref, g_ref, m_ref, v_ref, d_ref, nm_ref, nv_ref):
        gv = g_ref[...]
        nm = ADAM_B1 * m_ref[...] + (1.0 - ADAM_B1) * gv
        nv = ADAM_B2 * v_ref[...] + (1.0 - ADAM_B2) * (gv * gv)
        nm_ref[...] = nm
        nv_ref[...] = nv
        d_ref[...] = -ADAM_LR * ((nm * c1) / (jnp.sqrt(nv * c2) + ADAM_EPS) + ADAM_WD * w_ref[...])

    spec = pl.BlockSpec((None, tr, C), lambda l, i: (l, i, 0))
    shp = jax.ShapeDtypeStruct(w.shape, F32)
    return pl.pallas_call(
        body, name=name, grid=(Lyr, R // tr),
        in_specs=[spec] * 4, out_specs=[spec] * 3, out_shape=[shp] * 3,
        compiler_params=_params("parallel", "parallel"),
    )(w, g, m, v)


def _rope_tables(T):
    pos = jnp.arange(T, dtype=F32)
    inv_freq = THETA ** (-jnp.arange(0, ROT, 2, dtype=F32) / ROT)
    ang = pos[:, None] * inv_freq[None, :]
    cs = jnp.concatenate([jnp.cos(ang), jnp.sin(ang)], axis=1)
    half = ROT // 2
    lane = jnp.arange(3 * LANES)
    table, lm = lane // LANES, lane % HD
    src = jnp.where(table == 0, lm % half, half + lm % half)
    i32 = lambda b: b.astype(jnp.int32)
    sign = jnp.where(table == 0, i32(lm < ROT), jnp.where(table == 1, -i32(lm < half), i32((lm >= half) & (lm < ROT))))
    place = (jnp.arange(ROT)[:, None] == src[None, :]) * sign[None, :].astype(F32)
    ones = ((table == 0) & (lm >= ROT)).astype(F32)
    return jnp.dot(cs, place, precision=lax.Precision.HIGHEST) + ones[None, :]


def _tab_specs(rows):
    return [pl.BlockSpec((rows, LANES), lambda i, k=k: (i, k)) for k in range(3)]


def kernel(x, attn_norm, attn_w_qkv, attn_w_o, attn_sink, conv_norm, conv_w_pw1, conv_b_pw1, conv_w_dw, conv_b_dw, conv_ln_g, conv_ln_b, conv_w_pw2, conv_b_pw2, ffn_norm, ffn_w_gu, ffn_w_down, final_norm, loss_target, m_attn_norm, m_attn_w_qkv, m_attn_w_o, m_attn_sink, m_conv_norm, m_conv_w_pw1, m_conv_b_pw1, m_conv_w_dw, m_conv_b_dw, m_conv_ln_g, m_conv_ln_b, m_conv_w_pw2, m_conv_b_pw2, m_ffn_norm, m_ffn_w_gu, m_ffn_w_down, m_final_norm, v_attn_norm, v_attn_w_qkv, v_attn_w_o, v_attn_sink, v_conv_norm, v_conv_w_pw1, v_conv_b_pw1, v_conv_w_dw, v_conv_b_dw, v_conv_ln_g, v_conv_ln_b, v_conv_w_pw2, v_conv_b_pw2, v_ffn_norm, v_ffn_w_gu, v_ffn_w_down, v_final_norm):
    T = x.shape[1]
    x0 = x[0]
    target = loss_target[0]
    ix, iy = lax.axis_index("x"), lax.axis_index("y")
    chip = 2 * ix + iy
    rc = rs1 = rs2 = _rope_tables(T)

    bf = lambda t: t.astype(BF16)

    def place(vec, width):
        return lax.dynamic_update_slice(jnp.zeros((vec.shape[0], N_CHIPS * width), F32), vec, (0, chip * width))

    small_rows = jnp.concatenate([
        place(conv_norm, 256), place(conv_b_pw1, 512).reshape(2, D), place(conv_b_dw, 256), place(conv_ln_g, 256),
        place(conv_ln_b, 256), place(conv_b_pw2, 256), jnp.zeros((1, D), F32),
        place(conv_w_dw[0], 256), jnp.zeros((1, D), F32)], axis=0)

    h0, (w_qkv,) = rms_first(x0, attn_norm, comm=_Gather([bf(attn_w_qkv[0])], [False]))
    qkv, (w_o, got) = qkv_proj(h0, w_qkv, rc, rs1, rs2,
                               comm=_Both(_Gather([bf(attn_w_o[0])], [True]), _Scatter([], small_rows)))
    psmall = sum_pieces(got, "sum_small_params") * 0.5
    p_conv_norm, p_b_pw1 = psmall[0:1], psmall[1:3].reshape(1, 2 * D)
    p_b_dw, p_ln_g, p_ln_b, p_b_pw2 = psmall[3:4], psmall[4:5], psmall[5:6], psmall[6:7]
    p_w_dw = psmall[8:40]
    sink = attn_sink[0]
    o, (w_gu0,) = attn_fwd(qkv, sink, comm=_Gather([bf(ffn_w_gu[0])], [False]))
    zero_b = jnp.zeros((1, D), F32)
    zero_gu = jnp.zeros((1, 2 * DFF), F32)
    x1, h1, gu0, act0, (w_down0, w_pw1, w_pw2) = rms_mm_gate(
        (o, w_o, zero_b, x0), ffn_norm[0:1], w_gu0, zero_gu, DFF, True, BF16, "ffn0_up",
        comm=_Gather([bf(ffn_w_down[0]), bf(conv_w_pw1[0]), bf(conv_w_pw2[0])], [True, False, True]))
    x2, h2, pre, glu, (w_down1,) = rms_mm_gate((act0, w_down0, zero_b, x1), p_conv_norm, w_pw1, p_b_pw1, D, False, F32,
                                               "conv_pw1", comm=_Gather([bf(ffn_w_down[1])], [True]))
    dwc, sw, (w_gu1,) = conv_fwd(glu, p_w_dw, p_b_dw, p_ln_g, p_ln_b, comm=_Gather([bf(ffn_w_gu[1])], [False]))
    x3, h3, gu1, act1, _ = rms_mm_gate((sw, w_pw2, p_b_pw2, x2), ffn_norm[1:2], w_gu1, zero_gu, DFF, True, BF16,
                                       "ffn1_up")
    dx4, loss_part, d_final = mm_res_loss(act1, w_down1, x3, final_norm.reshape(1, D), target)

    dgu1, _ = swiglu_bwd(dx4, w_down1, gu1, "ffn1_down_bwd")
    g_down1 = dw_row(act1, dx4, "ffn1_down_dw")
    dx3, d_ffn1, ddwc, d_ln_g, d_ln_b, d_b_pw2, _ = mm_bt_rmsbwd(
        dgu1, w_gu1, x3, ffn_norm[1:2], dx4, "ffn1_up_bwd", conv_tail=(w_pw2, dwc, p_ln_g, p_ln_b))
    g_gu1 = dw_col(h3, dgu1, "ffn1_up_dw")

    g_pw2 = dw_row(sw, dx3, "conv_pw2_dw")
    dpre, d_w_dw, d_b_dw, d_b_pw1, (r_gu1, r_down1) = conv_bwd(ddwc, glu, pre, p_w_dw,
                                                               comm=_Scatter([g_gu1, g_down1]))
    dx2, d_conv_norm, _ = mm_bt_rmsbwd(dpre, w_pw1, x2, p_conv_norm, dx3, "conv_pw1_bwd")
    g_pw1 = dw_col(h2, dpre, "conv_pw1_dw")

    dgu0, (r_pw1, r_pw2) = swiglu_bwd(dx2, w_down0, gu0, "ffn0_down_bwd", comm=_Scatter([g_pw1, g_pw2]))
    g_down0 = dw_row(act0, dx2, "ffn0_down_dw")
    dx1, d_ffn0, do, _ = mm_bt_rmsbwd(dgu0, w_gu0, x1, ffn_norm[0:1], dx2, "ffn0_up_bwd", proj_w=w_o)
    g_gu0 = dw_col(h1, dgu0, "ffn0_up_dw")

    g_o = dw_row(o, dx1, "attn_out_dw")
    dq, dkc, dvc, d_sink, (r_gu0, r_down0, r_o) = attn_bwd(qkv, o, do, sink, rc, rs1, rs2,
                                                           comm=_Scatter([g_gu0, g_down0, g_o]))
    dqkv = kv_sum(dq, dkc, dvc, rc, rs1, rs2)[None]
    g_qkv = dw_col(h0, dqkv, "attn_qkv_dw")
    dx0, d_attn_norm, _ = mm_bt_rmsbwd(dqkv, w_qkv, x0, attn_norm, dx1, "attn_qkv_bwd")

    pad16 = lambda t: jnp.concatenate([t, jnp.zeros((1, D - t.shape[1]), F32)], axis=1)
    small_g = jnp.concatenate([
        d_attn_norm, pad16(d_sink), d_conv_norm, d_b_pw1.reshape(2, D), d_b_dw, d_ln_g, d_ln_b, d_b_pw2,
        d_ffn0, d_ffn1, d_final, pad16(loss_part), jnp.zeros((3, D), F32), d_w_dw], axis=0)
    gf_gu, (r_qkv, r_small) = sum_swap([r_gu0, r_gu1], "sum_gu", comm=_Scatter([g_qkv], small_g))
    gf_down = sum_swap([r_down0, r_down1], "sum_down")
    gf_pw1, gf_pw2 = sum_swap([r_pw1], "sum_pw1"), sum_swap([r_pw2], "sum_pw2")
    gf_qkv, gf_o = sum_swap([r_qkv], "sum_qkv"), sum_swap([r_o], "sum_o")
    gs = sum_pieces(r_small, "sum_small_grads")
    loss = gs[12, 0]

    def take(row0, nrows, width):
        return lax.dynamic_slice(gs, (row0, chip * width), (nrows, width))

    grads = {
        "attn_norm": gs[0:1], "attn_w_qkv": gf_qkv, "attn_w_o": gf_o, "attn_sink": gs[1:2, :N_HEADS],
        "conv_norm": take(2, 1, 256), "conv_w_pw1": gf_pw1,
        "conv_b_pw1": lax.dynamic_slice(gs[3:5].reshape(1, 2 * D), (0, chip * 512), (1, 512)),
        "conv_w_dw": take(16, 32, 256)[None, :CONV_W], "conv_b_dw": take(5, 1, 256), "conv_ln_g": take(6, 1, 256),
        "conv_ln_b": take(7, 1, 256), "conv_w_pw2": gf_pw2, "conv_b_pw2": take(8, 1, 256),
        "ffn_norm": gs[9:11], "ffn_w_gu": gf_gu, "ffn_w_down": gf_down, "final_norm": gs[11],
    }
    weights = dict(attn_norm=attn_norm, attn_w_qkv=attn_w_qkv, attn_w_o=attn_w_o, attn_sink=attn_sink,
                   conv_norm=conv_norm, conv_w_pw1=conv_w_pw1, conv_b_pw1=conv_b_pw1, conv_w_dw=conv_w_dw,
                   conv_b_dw=conv_b_dw, conv_ln_g=conv_ln_g, conv_ln_b=conv_ln_b, conv_w_pw2=conv_w_pw2,
                   conv_b_pw2=conv_b_pw2, ffn_norm=ffn_norm, ffn_w_gu=ffn_w_gu, ffn_w_down=ffn_w_down,
                   final_norm=final_norm)
    m_in = dict(attn_norm=m_attn_norm, attn_w_qkv=m_attn_w_qkv, attn_w_o=m_attn_w_o, attn_sink=m_attn_sink,
                conv_norm=m_conv_norm, conv_w_pw1=m_conv_w_pw1, conv_b_pw1=m_conv_b_pw1, conv_w_dw=m_conv_w_dw,
                conv_b_dw=m_conv_b_dw, conv_ln_g=m_conv_ln_g, conv_ln_b=m_conv_ln_b, conv_w_pw2=m_conv_w_pw2,
                conv_b_pw2=m_conv_b_pw2, ffn_norm=m_ffn_norm, ffn_w_gu=m_ffn_w_gu, ffn_w_down=m_ffn_w_down,
                final_norm=m_final_norm)
    v_in = dict(attn_norm=v_attn_norm, attn_w_qkv=v_attn_w_qkv, attn_w_o=v_attn_w_o, attn_sink=v_attn_sink,
                conv_norm=v_conv_norm, conv_w_pw1=v_conv_w_pw1, conv_b_pw1=v_conv_b_pw1, conv_w_dw=v_conv_w_dw,
                conv_b_dw=v_conv_b_dw, conv_ln_g=v_conv_ln_g, conv_ln_b=v_conv_ln_b, conv_w_pw2=v_conv_w_pw2,
                conv_b_pw2=v_conv_b_pw2, ffn_norm=v_ffn_norm, ffn_w_gu=v_ffn_w_gu, ffn_w_down=v_ffn_w_down,
                final_norm=v_final_norm)
    order = list(weights)
    g_out, d_out, m_out, v_out = [], [], [], []
    for nm in order:
        w = weights[nm]
        shape = w.shape
        as3 = lambda t: t.reshape((1,) * (3 - len(shape)) + shape) if len(shape) < 3 else t.reshape(shape)
        g3 = as3(grads[nm].reshape(shape))
        delta, nm_, nv_ = adamw(as3(w), g3, as3(m_in[nm]), as3(v_in[nm]), "adamw_" + nm)
        g_out.append(g3.reshape(shape))
        d_out.append(delta.reshape(shape))
        m_out.append(nm_.reshape(shape))
        v_out.append(nv_.reshape(shape))
    return (loss, dx0[None], *g_out, *d_out, *m_out, *v_out)
```

```python
import math

import jax
import jax.numpy as jnp
from jax import lax
from jax.experimental import pallas as pl
from jax.experimental.pallas import tpu as pltpu

F32 = jnp.float32
BF16 = jnp.bfloat16

D = 1024
N_HEADS = 16
N_KV = 4
GROUP = N_HEADS // N_KV
HD = 64
ROT = 16
THETA = 500000.0
BLK = 128
QKV = (N_HEADS + 2 * N_KV) * HD
KV_OFF = N_HEADS * HD
DFF = 2816
CONV_W = 31
CONV_PAD = 15
HALO = 16
CONV_JB = 16
CONV_JB_BWD = 16
EPS = 1e-6
NEG = -1e30
N_CHIPS = 4
N_DEV = 8
LANES = 128
SUBLANES = 8

ADAM_LR, ADAM_B1, ADAM_B2, ADAM_EPS, ADAM_WD, ADAM_STEP = 0.001, 0.9, 0.999, 1e-08, 0.01, 10

VMEM_LIMIT = 56 * 1024 * 1024
MESH = pl.DeviceIdType.MESH


def _params(*sem):
    return pltpu.CompilerParams(dimension_semantics=sem, vmem_limit_bytes=VMEM_LIMIT)


def _tile(n, want):
    if n <= want:
        return n
    for t in range(want, 7, -1):
        if n % t == 0 and t % 8 == 0:
            return t
    return n


MXU_COLS = 256


def _col_chunks(n):
    return [slice(c, min(c + MXU_COLS, n)) for c in range(0, n, MXU_COLS)]


def _sigmoid(v):
    return jax.nn.sigmoid(v)


def _rms_fwd(xv, gain):
    r = lax.rsqrt(jnp.mean(xv * xv, axis=-1, keepdims=True) + EPS)
    return xv * r * gain


def _rms_bwd(dh, xv, gain, dres):
    r = lax.rsqrt(jnp.mean(xv * xv, axis=-1, keepdims=True) + EPS)
    xhat = xv * r
    gy = dh * gain
    dx = r * (gy - xhat * jnp.mean(gy * xhat, axis=-1, keepdims=True))
    return dx + dres, dh * xhat


def _rope(blk, c, s1, s2):
    return blk * c + pltpu.roll(blk, LANES - ROT // 2, 1) * s1 + pltpu.roll(blk, ROT // 2, 1) * s2


def _dot(a, b):
    return jnp.dot(a, b, preferred_element_type=F32)


def _dot_tb(a, b):
    return lax.dot_general(a, b, (((1,), (1,)), ((), ())), preferred_element_type=F32)


def _dot_ta(a, b):
    return lax.dot_general(a, b, (((0,), (0,)), ((), ())), preferred_element_type=F32)


def rms_first(x, gain, comm):
    T = x.shape[0]
    tm = _tile(T, 512)

    def body(x_ref, g_ref, h_ref):
        h_ref[...] = _rms_fwd(x_ref[...], g_ref[...]).astype(BF16)

    (h,), got = _call(
        body, name="rms_first", grid=(T // tm,),
        in_specs=[pl.BlockSpec((tm, D), lambda i: (i, 0)), pl.BlockSpec((1, D), lambda i: (0, 0))],
        out_specs=[pl.BlockSpec((tm, D), lambda i: (i, 0))], out_shape=[jax.ShapeDtypeStruct((T, D), BF16)],
        semantics=("parallel",), args=(x, gain), comm=comm)
    return h, got


def qkv_proj(h, w, rc, rs1, rs2, comm=None):
    T = h.shape[0]
    tm = _tile(T, 1024)

    def body(h_ref, w_ref, c_ref, s1_ref, s2_ref, qkv_ref):
        acc = _dot(h_ref[...], w_ref[...])
        c, s1, s2 = c_ref[...], s1_ref[...], s2_ref[...]
        n_rot = (KV_OFF + N_KV * HD) // LANES
        for j in range(n_rot):
            sl = slice(LANES * j, LANES * (j + 1))
            roped = _rope(acc[:, sl], c, s1, s2)
            if j < KV_OFF // LANES:
                roped = roped * Q_SCALE
            qkv_ref[:, sl] = roped.astype(BF16)
        qkv_ref[:, n_rot * LANES:] = acc[:, n_rot * LANES:].astype(BF16)

    row = lambda i: (i, 0)
    full = lambda i: (0, 0)
    (qkv,), got = _call(
        body, name="qkv_proj", grid=(T // tm,),
        in_specs=[pl.BlockSpec((tm, D), row), pl.BlockSpec((D, QKV), full), *_tab_specs(tm)],
        out_specs=[pl.BlockSpec((tm, QKV), row)],
        out_shape=[jax.ShapeDtypeStruct((T, QKV), BF16)],
        semantics=("parallel",), args=(h, w, rc, rs1, rs2), comm=comm)
    return qkv, got


Q_SCALE = 1.0 / math.sqrt(HD)


def _attn_mask(n, T):
    ci = lax.broadcasted_iota(jnp.int32, (3 * BLK, BLK), 0)
    qi = lax.broadcasted_iota(jnp.int32, (3 * BLK, BLK), 1)
    key_pos = n * BLK - BLK + ci
    return (jnp.abs(ci - BLK - qi) <= BLK) & (key_pos >= 0) & (key_pos < T)


def _kv_padded(kv, first_tile):
    low = lax.broadcasted_iota(jnp.int32, (3 * BLK, LANES), 1) < HD
    zero = jnp.zeros((3 * BLK, LANES), BF16)
    out = {}
    for g in range(N_KV):
        t = kv[:, (first_tile + g // 2) * LANES:(first_tile + g // 2 + 1) * LANES]
        swapped = jnp.concatenate([t[:, HD:], t[:, :HD]], axis=1)
        for p in range(2):
            out[g, p] = jnp.where(low if p == 0 else ~low, t if g % 2 == p else swapped, zero)
    return out


def _pair_products(kvx, tile_of):
    both = {g: jnp.concatenate([kvx[g, 0], kvx[g, 1]], axis=0) for g in range(N_KV)}
    out = []
    for j in range(N_HEADS // 2):
        prod = _dot_tb(both[2 * j // GROUP], tile_of(j))
        out += [prod[:3 * BLK], prod[3 * BLK:]]
    return out


def _softmax_sink(s, valid, sk):
    s = jnp.where(valid, s, NEG)
    m = jnp.maximum(jnp.max(s, axis=0, keepdims=True), sk)
    e = jnp.exp(s - m)
    es = jnp.exp(sk - m)
    inv = 1.0 / (jnp.sum(e, axis=0, keepdims=True) + es)
    return e * inv, es * inv


def _attn_specs(T):
    nb = T // BLK
    kv_blk = 2 * N_KV * HD
    kv_col = KV_OFF // kv_blk
    q_spec = pl.BlockSpec((BLK, KV_OFF), lambda n: (n, 0))
    prev = pl.BlockSpec((BLK, kv_blk), lambda n: (jnp.maximum(n - 1, 0), kv_col))
    own = pl.BlockSpec((BLK, kv_blk), lambda n: (n, kv_col))
    nxt = pl.BlockSpec((BLK, kv_blk), lambda n: (jnp.minimum(n + 1, nb - 1), kv_col))
    return nb, q_spec, prev, own, nxt


def attn_fwd(qkv, sink, comm=None):
    T = qkv.shape[0]
    nb, q_spec, prev, own, nxt = _attn_specs(T)

    def body(sink_ref, q_ref, kp_ref, ko_ref, kn_ref, o_ref):
        valid = _attn_mask(pl.program_id(0), T)
        kv = jnp.concatenate([kp_ref[...], ko_ref[...], kn_ref[...]], axis=0)
        kx, vx = _kv_padded(kv, 0), _kv_padded(kv, 2)
        ss = _pair_products(kx, lambda j: q_ref[:, j * LANES:(j + 1) * LANES])
        ps = [_softmax_sink(ss[h], valid, sink_ref[h])[0].astype(BF16) for h in range(N_HEADS)]
        vxt = {k: v.T for k, v in vx.items()}
        for j in range(N_HEADS // 2):
            g = 2 * j // GROUP
            o_t = _dot(vxt[g, 0], ps[2 * j]) + _dot(vxt[g, 1], ps[2 * j + 1])
            o_ref[:, j * LANES:(j + 1) * LANES] = o_t.T.astype(BF16)

    (o,), got = _call(
        body, name="attn_fwd", grid=(nb,),
        in_specs=[pl.BlockSpec(memory_space=pltpu.SMEM), q_spec, prev, own, nxt],
        out_specs=[pl.BlockSpec((BLK, D), lambda n: (n, 0))],
        out_shape=[jax.ShapeDtypeStruct((T, D), BF16)],
        semantics=("parallel",), args=(sink, qkv, qkv, qkv, qkv), comm=comm)
    return o, got


def rms_mm_gate(x, gain, w, bias, H, swiglu, act_dtype, name, comm=None):
    fused = isinstance(x, tuple)
    T = (x[0] if fused else x).shape[0]
    tm = _tile(T, 512)

    def body(*refs):
        if fused:
            a_ref, wp_ref, bp_ref, r_ref, g_ref, w_ref, b_ref, x_ref, h_ref, pre_ref, act_ref = refs
            xv = _dot(a_ref[...], wp_ref[...]) + bp_ref[...] + r_ref[...]
            x_ref[...] = xv
        else:
            x_ref, g_ref, w_ref, b_ref, h_ref, pre_ref, act_ref = refs
            xv = x_ref[...]
        h = _rms_fwd(xv, g_ref[...]).astype(BF16)
        h_ref[...] = h
        for cs in _col_chunks(H):
            cs2 = slice(H + cs.start, H + cs.stop)
            a = _dot(h, w_ref[:, cs]) + b_ref[:, cs]
            b = _dot(h, w_ref[:, cs2]) + b_ref[:, cs2]
            pre_ref[0, :, cs] = a.astype(BF16)
            pre_ref[1, :, cs] = b.astype(BF16)
            if swiglu:
                act = a * _sigmoid(a) * b
            else:
                act = a * _sigmoid(b)
            act_ref[:, cs] = act.astype(act_dtype)

    row = lambda i: (i, 0)
    full = lambda i: (0, 0)
    if fused:
        K = x[0].shape[1]
        x_specs = [pl.BlockSpec((tm, K), row), pl.BlockSpec((K, D), full, pipeline_mode=pl.Buffered(1)),
                   pl.BlockSpec((1, D), full), pl.BlockSpec((tm, D), row)]
        x_out = ([pl.BlockSpec((tm, D), row)], [jax.ShapeDtypeStruct((T, D), F32)])
        x_args = tuple(x)
    else:
        x_specs, x_out, x_args = [pl.BlockSpec((tm, D), row)], ([], []), (x,)
    outs, got = _call(
        body, name=name, grid=(T // tm,),
        in_specs=x_specs + [pl.BlockSpec((1, D), full),
                            pl.BlockSpec((D, 2 * H), full, pipeline_mode=pl.Buffered(1)), pl.BlockSpec((1, 2 * H), full)],
        out_specs=x_out[0] + [pl.BlockSpec((tm, D), row), pl.BlockSpec((2, tm, H), lambda i: (0, i, 0)),
                              pl.BlockSpec((tm, H), row)],
        out_shape=x_out[1] + [jax.ShapeDtypeStruct((T, D), BF16), jax.ShapeDtypeStruct((2, T, H), BF16),
                              jax.ShapeDtypeStruct((T, H), act_dtype)],
        semantics=("parallel",), args=x_args + (gain, w, bias), comm=comm)
    return (*outs, got)


def _conv_tiles(T):
    tt = _tile(T, 512)
    return tt, tt // SUBLANES, D // LANES


def _fill_strided(ext, p, L):
    main = p[HALO:HALO + SUBLANES * L, :].reshape(SUBLANES, L, LANES)
    ext[CONV_PAD:CONV_PAD + L] = jnp.swapaxes(main, 0, 1)

    def ibody(i, carry):
        ext[i] = p[pl.ds(i + 1, SUBLANES, stride=L), :]
        ext[i + CONV_PAD + L] = p[pl.ds(i + CONV_PAD + L + 1, SUBLANES, stride=L), :]
        return carry

    lax.fori_loop(0, CONV_PAD, ibody, 0, unroll=3)


def _conv_specs(T, tt):
    main = pl.BlockSpec((tt, D), lambda i: (i, 0))
    per = tt // HALO
    prev = pl.BlockSpec((HALO, D), lambda i: (jnp.maximum(i * per - 1, 0), 0))
    nxt = pl.BlockSpec((HALO, D), lambda i: (jnp.minimum((i + 1) * per, T // HALO - 1), 0))
    return main, prev, nxt


def _fill_pad(pad, main_ref, prev_ref, next_ref, i, n_i, tt, nlt):
    keep_p = (i > 0).astype(F32)
    keep_n = (i < n_i - 1).astype(F32)
    for lt in range(nlt):
        sl = slice(lt * LANES, (lt + 1) * LANES)
        pad[lt, 0:HALO, :] = prev_ref[:, sl] * keep_p
        pad[lt, HALO:HALO + tt, :] = main_ref[:, sl]
        pad[lt, HALO + tt:2 * HALO + tt, :] = next_ref[:, sl] * keep_n


def conv_fwd(glu, w_dw, b_dw, ln_g, ln_b, comm=None):
    T = glu.shape[0]
    tt, L, nlt = _conv_tiles(T)
    n_i = T // tt
    main, prev, nxt = _conv_specs(T, tt)

    def body(x_ref, xp_ref, xn_ref, w_ref, b_ref, g_ref, bb_ref, dwc_ref, sw_ref, pad, ob, ext, wk):
        i = pl.program_id(0)
        _fill_pad(pad, x_ref, xp_ref, xn_ref, i, n_i, tt, nlt)
        for lt in range(nlt):
            sl = slice(lt * LANES, (lt + 1) * LANES)
            o = ob.at[lt]
            _fill_strided(ext, pad.at[lt], L)
            for k in range(CONV_W):
                wk[k] = jnp.broadcast_to(w_ref[k:k + 1, sl], (SUBLANES, LANES))

            def jbody(jb, carry):
                j = jb * CONV_JB
                accs = [None] * CONV_JB
                for m in range(CONV_W + CONV_JB - 1):
                    e = ext[j + m]
                    for u in range(CONV_JB):
                        if 0 <= m - u < CONV_W:
                            t = e * wk[m - u]
                            accs[u] = t if accs[u] is None else accs[u] + t
                for u in range(CONV_JB):
                    o[pl.ds(j + u, SUBLANES, stride=L), :] = accs[u]
                return carry

            lax.fori_loop(0, L // CONV_JB, jbody, 0)
        y = jnp.concatenate([ob[lt] for lt in range(nlt)], axis=1) + b_ref[...]
        dwc_ref[...] = y
        mu = jnp.mean(y, axis=-1, keepdims=True)
        yc = y - mu
        var = jnp.mean(yc * yc, axis=-1, keepdims=True)
        z = yc * lax.rsqrt(var + EPS) * g_ref[...] + bb_ref[...]
        sw_ref[...] = (z * _sigmoid(z)).astype(BF16)

    full = lambda i: (0, 0)
    (dwc, sw), got = _call(
        body, name="conv_fwd", grid=(n_i,),
        in_specs=[main, prev, nxt, pl.BlockSpec((32, D), full), pl.BlockSpec((1, D), full),
                  pl.BlockSpec((1, D), full), pl.BlockSpec((1, D), full)],
        out_specs=[pl.BlockSpec((tt, D), lambda i: (i, 0)), pl.BlockSpec((tt, D), lambda i: (i, 0))],
        out_shape=[jax.ShapeDtypeStruct((T, D), F32), jax.ShapeDtypeStruct((T, D), BF16)],
        scratch_shapes=[pltpu.VMEM((nlt, tt + 2 * HALO, LANES), F32), pltpu.VMEM((nlt, tt, LANES), F32),
                        pltpu.VMEM((L + 2 * HALO, SUBLANES, LANES), F32), pltpu.VMEM((32, SUBLANES, LANES), F32)],
        semantics=("parallel",), args=(glu, glu, glu, w_dw, b_dw, ln_g, ln_b), comm=comm)
    return dwc, sw, got


def mm_res_loss(a, w, resid, gain, target):
    T, K = a.shape
    tm = _tile(T, 512)

    def body(a_ref, w_ref, r_ref, g_ref, t_ref, dx_ref, loss_ref, dg_ref):
        @pl.when(pl.program_id(0) == 0)
        def _():
            loss_ref[...] = jnp.zeros_like(loss_ref)
            dg_ref[...] = jnp.zeros_like(dg_ref)

        xv, gain_v = _dot(a_ref[...], w_ref[...]) + r_ref[...], g_ref[...]
        err = _rms_fwd(xv, gain_v) - t_ref[...]
        part = 0.5 * jnp.sum(jnp.mean(err * err, axis=-1, keepdims=True), axis=0, keepdims=True)
        loss_ref[...] += jnp.broadcast_to(part, loss_ref.shape)
        dx, dgr = _rms_bwd(err * (1.0 / D), xv, gain_v, 0.0)
        dx_ref[...] = dx
        dg_ref[...] += jnp.sum(dgr, axis=0, keepdims=True)

    row = lambda i: (i, 0)
    full = lambda i: (0, 0)
    return pl.pallas_call(
        body, name="ffn1_down_loss", grid=(T // tm,),
        in_specs=[pl.BlockSpec((tm, K), row), pl.BlockSpec((K, D), full), pl.BlockSpec((tm, D), row),
                  pl.BlockSpec((1, D), full), pl.BlockSpec((tm, D), row)],
        out_specs=[pl.BlockSpec((tm, D), row), pl.BlockSpec((1, LANES), full), pl.BlockSpec((1, D), full)],
        out_shape=[jax.ShapeDtypeStruct((T, D), F32), jax.ShapeDtypeStruct((1, LANES), F32),
                   jax.ShapeDtypeStruct((1, D), F32)],
        compiler_params=_params("arbitrary"),
    )(a, w, resid, gain, target)


def swiglu_bwd(dx, w_down, pre, name, comm=None):
    T = dx.shape[0]
    H = w_down.shape[0]
    tm = _tile(T, 512)

    def body(dx_ref, w_ref, pre_ref, dpre_ref):
        dxb = dx_ref[...].astype(BF16)
        for cs in _col_chunks(H):
            dact = _dot_tb(dxb, w_ref[cs, :])
            g = pre_ref[0, :, cs].astype(F32)
            u = pre_ref[1, :, cs].astype(F32)
            sg = _sigmoid(g)
            dpre_ref[0, :, cs] = (dact * u * sg * (1.0 + g * (1.0 - sg))).astype(BF16)
            dpre_ref[1, :, cs] = (dact * g * sg).astype(BF16)

    (dpre,), got = _call(
        body, name=name, grid=(T // tm,),
        in_specs=[pl.BlockSpec((tm, D), lambda i: (i, 0)),
                  pl.BlockSpec((H, D), lambda i: (0, 0), pipeline_mode=pl.Buffered(1)),
                  pl.BlockSpec((2, tm, H), lambda i: (0, i, 0))],
        out_specs=[pl.BlockSpec((2, tm, H), lambda i: (0, i, 0))],
        out_shape=[jax.ShapeDtypeStruct((2, T, H), BF16)],
        semantics=("parallel",), args=(dx, w_down, pre), comm=comm)
    return dpre, got


def _ln_silu_bwd(dsw, y, ln_g, ln_b):
    mu = jnp.mean(y, axis=-1, keepdims=True)
    yc = y - mu
    rstd = lax.rsqrt(jnp.mean(yc * yc, axis=-1, keepdims=True) + EPS)
    xhat = yc * rstd
    z = xhat * ln_g + ln_b
    sg = _sigmoid(z)
    dz = dsw * sg * (1.0 + z * (1.0 - sg))
    dxh = dz * ln_g
    dy = rstd * (dxh - jnp.mean(dxh, axis=-1, keepdims=True) - xhat * jnp.mean(dxh * xhat, axis=-1, keepdims=True))
    return dy, dz * xhat, dz


def mm_bt_rmsbwd(dpre, w, x, gain, dres, name, comm=None, proj_w=None, conv_tail=None):
    nh, T, H = dpre.shape
    tm = _tile(T, 1024 if nh * H <= 2 * D else 512)
    n_extra_in = 1 if proj_w is not None else (4 if conv_tail is not None else 0)

    def body(*refs):
        dp_ref, w_ref, x_ref, g_ref, dres_ref = refs[:5]
        extra_in = refs[5:5 + n_extra_in]
        dx_ref, dg_ref = refs[5 + n_extra_in:7 + n_extra_in]
        extra_out = refs[7 + n_extra_in:]

        @pl.when(pl.program_id(0) == 0)
        def _():
            dg_ref[...] = jnp.zeros_like(dg_ref)
            for r in extra_out[1:]:
                r[...] = jnp.zeros_like(r)

        dh = _dot_tb(dp_ref[0], w_ref[:, 0:H])
        for hf in range(1, nh):
            dh = dh + _dot_tb(dp_ref[hf], w_ref[:, hf * H:(hf + 1) * H])
        dx, dgr = _rms_bwd(dh, x_ref[...], g_ref[...], dres_ref[...])
        dx_ref[...] = dx
        dg_ref[...] += jnp.sum(dgr, axis=0, keepdims=True)
        if proj_w is not None:
            extra_out[0][...] = _dot_tb(dx.astype(BF16), extra_in[0][...]).astype(BF16)
        elif conv_tail is not None:
            wt_ref, y_ref, lg_ref, lb_ref = extra_in
            dy, dgl, dbl = _ln_silu_bwd(_dot_tb(dx.astype(BF16), wt_ref[...]), y_ref[...], lg_ref[...], lb_ref[...])
            extra_out[0][...] = dy
            extra_out[1][...] += jnp.sum(dgl, axis=0, keepdims=True)
            extra_out[2][...] += jnp.sum(dbl, axis=0, keepdims=True)
            extra_out[3][...] += jnp.sum(dx, axis=0, keepdims=True)

    row = lambda i: (i, 0)
    full = lambda i: (0, 0)
    vec = pl.BlockSpec((1, D), full)
    vec_shape = jax.ShapeDtypeStruct((1, D), F32)
    in_specs = [pl.BlockSpec((nh, tm, H), lambda i: (0, i, 0)),
                pl.BlockSpec((D, nh * H), full, pipeline_mode=pl.Buffered(1)),
                pl.BlockSpec((tm, D), row), vec, pl.BlockSpec((tm, D), row)]
    out_specs = [pl.BlockSpec((tm, D), row), vec]
    out_shape = [jax.ShapeDtypeStruct((T, D), F32), vec_shape]
    args = (dpre, w, x, gain, dres)
    if proj_w is not None:
        N = proj_w.shape[0]
        in_specs.append(pl.BlockSpec((N, D), full, pipeline_mode=pl.Buffered(1)))
        out_specs.append(pl.BlockSpec((tm, N), row))
        out_shape.append(jax.ShapeDtypeStruct((T, N), BF16))
        args += (proj_w,)
    elif conv_tail is not None:
        in_specs += [pl.BlockSpec((D, D), full, pipeline_mode=pl.Buffered(1)), pl.BlockSpec((tm, D), row), vec, vec]
        out_specs += [pl.BlockSpec((tm, D), row), vec, vec, vec]
        out_shape += [jax.ShapeDtypeStruct((T, D), F32), vec_shape, vec_shape, vec_shape]
        args += tuple(conv_tail)
    outs, got = _call(body, name=name, grid=(T // tm,), in_specs=in_specs, out_specs=out_specs, out_shape=out_shape,
                      semantics=("arbitrary",), args=args, comm=comm)
    return (*outs, got)


def dw_col(a, dpre, name):
    T = a.shape[0]
    nh, _, H = dpre.shape
    per = nh * H // N_CHIPS
    bph = N_CHIPS // nh
    tt = _tile(T, 2048)
    nt = T // tt

    def body(a_ref, b_ref, o_ref, acc):
        t = pl.program_id(1)

        @pl.when(t == 0)
        def _():
            acc[...] = jnp.zeros_like(acc)

        acc[...] += _dot_ta(a_ref[...], b_ref[...])

        @pl.when(t == nt - 1)
        def _():
            o_ref[...] = acc[...].astype(BF16)

    return pl.pallas_call(
        body, name=name, grid=(N_CHIPS, nt),
        in_specs=[pl.BlockSpec((tt, D), lambda q, t: (t, 0)),
                  pl.BlockSpec((None, tt, per), lambda q, t: (q // bph, t, q % bph))],
        out_specs=pl.BlockSpec((None, D, per), lambda q, t: (q, 0, 0)),
        out_shape=jax.ShapeDtypeStruct((N_CHIPS, D, per), BF16),
        scratch_shapes=[pltpu.VMEM((D, per), F32)],
        compiler_params=_params("parallel", "arbitrary"),
    )(a, dpre)


def dw_row(a, b, name):
    T, R = a.shape
    cw = 1408 if R % 1408 == 0 else R
    tt = _tile(T, 2048 if R <= D else 1024)
    nt = T // tt

    def body(a_ref, b_ref, o_ref, acc):
        t = pl.program_id(1)

        @pl.when(t == 0)
        def _():
            acc[...] = jnp.zeros_like(acc)

        acc[...] += _dot_ta(a_ref[...], b_ref[...].astype(BF16))

        @pl.when(t == nt - 1)
        def _():
            o_ref[...] = acc[...].astype(BF16)

    out = pl.pallas_call(
        body, name=name, grid=(R // cw, nt),
        in_specs=[pl.BlockSpec((tt, cw), lambda q, t: (t, q)), pl.BlockSpec((tt, D), lambda q, t: (t, 0))],
        out_specs=pl.BlockSpec((cw, D), lambda q, t: (q, 0)),
        out_shape=jax.ShapeDtypeStruct((R, D), BF16),
        scratch_shapes=[pltpu.VMEM((cw, D), F32)],
        compiler_params=_params("parallel", "arbitrary"),
    )(a, b)
    return out.reshape(N_CHIPS, R // N_CHIPS, D)


def conv_bwd(ddwc, glu, pre, w_dw, comm=None):
    T = ddwc.shape[0]
    tt, L, nlt = _conv_tiles(T)
    n_i = T // tt
    main, prev, nxt = _conv_specs(T, tt)

    def body(d_ref, dp_ref, dn_ref, x_ref, xp_ref, xn_ref, pre_ref, w_ref,
             dpre_ref, dw_ref, dbd_ref, dbp_ref, padd, padx, ob, extd, extx, wk):
        i = pl.program_id(0)

        @pl.when(i == 0)
        def _():
            dw_ref[...] = jnp.zeros_like(dw_ref)
            dbd_ref[...] = jnp.zeros_like(dbd_ref)
            dbp_ref[...] = jnp.zeros_like(dbp_ref)

        _fill_pad(padd, d_ref, dp_ref, dn_ref, i, n_i, tt, nlt)
        _fill_pad(padx, x_ref, xp_ref, xn_ref, i, n_i, tt, nlt)
        for lt in range(nlt):
            sl = slice(lt * LANES, (lt + 1) * LANES)
            o = ob.at[lt]
            _fill_strided(extd, padd.at[lt], L)
            _fill_strided(extx, padx.at[lt], L)
            for k in range(CONV_W):
                wk[k] = jnp.broadcast_to(w_ref[k:k + 1, sl], (SUBLANES, LANES))

            nu = CONV_JB_BWD

            def jbody(jb, accs):
                j = jb * nu
                accs = list(accs)
                d = [extd[j + u + CONV_PAD] for u in range(nu)]
                g = [None] * nu
                for m in range(CONV_W + nu - 1):
                    ed = extd[j + 2 * CONV_PAD + nu - 1 - m]
                    ex = extx[j + m]
                    for u in range(nu):
                        k = m - (nu - 1 - u)
                        if 0 <= k < CONV_W:
                            t = ed * wk[k]
                            g[u] = t if g[u] is None else g[u] + t
                        k = m - u
                        if 0 <= k < CONV_W:
                            accs[k] = accs[k] + d[u] * ex
                for u in range(nu):
                    o[pl.ds(j + u, SUBLANES, stride=L), :] = g[u]
                return tuple(accs)

            accs = lax.fori_loop(0, L // nu, jbody, tuple(jnp.zeros((SUBLANES, LANES), F32) for _ in range(CONV_W)))
            for k in range(CONV_W):
                dw_ref[k:k + 1, sl] += jnp.sum(accs[k], axis=0, keepdims=True)
        dglu = jnp.concatenate([ob[lt] for lt in range(nlt)], axis=1)
        a = pre_ref[0].astype(F32)
        gate = pre_ref[1].astype(F32)
        sg = _sigmoid(gate)
        da = dglu * sg
        dgate = dglu * a * sg * (1.0 - sg)
        dpre_ref[0] = da.astype(BF16)
        dpre_ref[1] = dgate.astype(BF16)
        dbd_ref[...] += jnp.sum(d_ref[...], axis=0, keepdims=True)
        dbp_ref[0] += jnp.sum(da, axis=0, keepdims=True)
        dbp_ref[1] += jnp.sum(dgate, axis=0, keepdims=True)

    full = lambda i: (0, 0)
    (dpre, dw, dbd, dbp), got = _call(
        body, name="conv_bwd", grid=(n_i,),
        in_specs=[main, prev, nxt, main, prev, nxt, pl.BlockSpec((2, tt, D), lambda i: (0, i, 0)),
                  pl.BlockSpec((32, D), full)],
        out_specs=[pl.BlockSpec((2, tt, D), lambda i: (0, i, 0)), pl.BlockSpec((32, D), full),
                   pl.BlockSpec((1, D), full), pl.BlockSpec((2, 1, D), lambda i: (0, 0, 0))],
        out_shape=[jax.ShapeDtypeStruct((2, T, D), BF16), jax.ShapeDtypeStruct((32, D), F32),
                   jax.ShapeDtypeStruct((1, D), F32), jax.ShapeDtypeStruct((2, 1, D), F32)],
        scratch_shapes=[pltpu.VMEM((nlt, tt + 2 * HALO, LANES), F32), pltpu.VMEM((nlt, tt + 2 * HALO, LANES), F32),
                        pltpu.VMEM((nlt, tt, LANES), F32), pltpu.VMEM((L + 2 * HALO, SUBLANES, LANES), F32),
                        pltpu.VMEM((L + 2 * HALO, SUBLANES, LANES), F32), pltpu.VMEM((32, SUBLANES, LANES), F32)],
        semantics=("arbitrary",), args=(ddwc, ddwc, ddwc, glu, glu, glu, pre, w_dw), comm=comm)
    return dpre, dw, dbd, dbp, got


def attn_bwd(qkv, o, do, sink, rc, rs1, rs2, comm=None):
    T = qkv.shape[0]
    nb, q_spec, prev, own, nxt = _attn_specs(T)
    kvw = N_KV * HD

    def body(sink_ref, q_ref, kp_ref, ko_ref, kn_ref, o_ref, do_ref, c_ref, s1_ref, s2_ref,
             dq_ref, dkc_ref, dvc_ref, dsink_ref):
        n = pl.program_id(0)

        @pl.when(n == 0)
        def _():
            dsink_ref[...] = jnp.zeros_like(dsink_ref)

        valid = _attn_mask(n, T)
        kv = jnp.concatenate([kp_ref[...], ko_ref[...], kn_ref[...]], axis=0)
        kx, vx = _kv_padded(kv, 0), _kv_padded(kv, 2)
        tile = lambda ref, j: ref[:, j * LANES:(j + 1) * LANES]
        ss = _pair_products(kx, lambda j: tile(q_ref, j))
        dps = _pair_products(vx, lambda j: tile(do_ref, j))
        low_d = lax.broadcasted_iota(jnp.int32, (LANES, BLK), 0) < HD
        deltas = []
        for j in range(N_HEADS // 2):
            prod_t = tile(do_ref, j).astype(F32).T * tile(o_ref, j).astype(F32).T
            deltas.append(jnp.sum(jnp.where(low_d, prod_t, 0.0), axis=0, keepdims=True))
            deltas.append(jnp.sum(jnp.where(low_d, 0.0, prod_t), axis=0, keepdims=True))
        lane = lax.broadcasted_iota(jnp.int32, (1, N_HEADS), 1)
        dsink = jnp.zeros((1, N_HEADS), F32)
        pbs, dss = [], []
        for h in range(N_HEADS):
            p, p_sink = _softmax_sink(ss[h], valid, sink_ref[h])
            dss.append((p * (dps[h] - deltas[h])).astype(BF16))
            pbs.append(p.astype(BF16))
            part = -jnp.sum(p_sink * deltas[h], axis=1, keepdims=True)
            dsink = dsink + jnp.where(lane == h, part, 0.0)
        dsink_ref[...] += dsink
        c, s1, s2 = c_ref[...], s1_ref[...], s2_ref[...]
        kxt = {k: v.T for k, v in kx.items()}
        for j in range(N_HEADS // 2):
            g = 2 * j // GROUP
            dq_t = _dot(kxt[g, 0], dss[2 * j]) + _dot(kxt[g, 1], dss[2 * j + 1])
            dq_ref[:, j * LANES:(j + 1) * LANES] = (_rope(dq_t.T, c, -s1, -s2) * Q_SCALE).astype(BF16)
        low_k = lax.broadcasted_iota(jnp.int32, (3 * BLK, LANES), 1) < HD
        cols = lambda xs, g, p: jnp.concatenate([xs[GROUP * g + p], xs[GROUP * g + 2 + p]], axis=1)
        for t in range(N_KV // 2):
            sums = {}
            for g in (2 * t, 2 * t + 1):
                q2 = jnp.concatenate([tile(q_ref, 2 * g), tile(q_ref, 2 * g + 1)], axis=0)
                do2 = jnp.concatenate([tile(do_ref, 2 * g), tile(do_ref, 2 * g + 1)], axis=0)
                dk2 = _dot(jnp.concatenate([cols(dss, g, 0), cols(dss, g, 1)], axis=0), q2)
                dv2 = _dot(jnp.concatenate([cols(pbs, g, 0), cols(pbs, g, 1)], axis=0), do2)
                for p in range(2):
                    sums[g, p] = (dk2[p * 3 * BLK:(p + 1) * 3 * BLK], dv2[p * 3 * BLK:(p + 1) * 3 * BLK])
            for which, ref in ((0, dkc_ref), (1, dvc_ref)):
                keep = jnp.where(low_k, sums[2 * t, 0][which], sums[2 * t + 1, 1][which])
                swap = jnp.where(low_k, sums[2 * t + 1, 0][which], sums[2 * t, 1][which])
                ref[:, t * LANES:(t + 1) * LANES] = keep + pltpu.roll(swap, HD, 1)

    row = lambda n: (n, 0)
    (dq, dkc, dvc, dsink), got = _call(
        body, name="attn_bwd", grid=(nb,),
        in_specs=[pl.BlockSpec(memory_space=pltpu.SMEM), q_spec, prev, own, nxt,
                  pl.BlockSpec((BLK, D), row), pl.BlockSpec((BLK, D), row), *_tab_specs(BLK)],
        out_specs=[pl.BlockSpec((BLK, D), row), pl.BlockSpec((None, 3 * BLK, kvw), lambda n: (n, 0, 0)),
                   pl.BlockSpec((None, 3 * BLK, kvw), lambda n: (n, 0, 0)), pl.BlockSpec((1, N_HEADS), lambda n: (0, 0))],
        out_shape=[jax.ShapeDtypeStruct((T, QKV), BF16), jax.ShapeDtypeStruct((nb, 3 * BLK, kvw), F32),
                   jax.ShapeDtypeStruct((nb, 3 * BLK, kvw), F32), jax.ShapeDtypeStruct((1, N_HEADS), F32)],
        semantics=("arbitrary",), args=(sink, qkv, qkv, qkv, qkv, o, do, rc, rs1, rs2), comm=comm)
    return dq, dkc, dvc, dsink, got


def kv_sum(dqkv, dkc, dvc, rc, rs1, rs2):
    nb = dkc.shape[0]
    T = nb * BLK
    kvw = N_KV * HD

    G = 8
    ng = nb // G

    def gather3(own_ref, prev_ref, before_ref, next_ref, after_ref, m):
        has_before = (m > 0).astype(F32)
        has_after = (m < ng - 1).astype(F32)
        out = []
        for i in range(G):
            from_prev = prev_ref[i - 1] if i > 0 else before_ref[0] * has_before
            from_next = next_ref[i + 1] if i < G - 1 else after_ref[0] * has_after
            out.append(from_prev + own_ref[i] + from_next)
        return jnp.concatenate(out, axis=0)

    def body(_, ko, kp, kb, kn, ka, vo, vp, vb, vn, va, c_ref, s1_ref, s2_ref, out_ref):
        m = pl.program_id(0)
        dk = gather3(ko, kp, kb, kn, ka, m)
        dv = gather3(vo, vp, vb, vn, va, m)
        c, s1, s2 = c_ref[...], s1_ref[...], s2_ref[...]
        for j in range(kvw // LANES):
            sl = slice(LANES * j, LANES * (j + 1))
            out_ref[:, sl] = _rope(dk[:, sl], c, -s1, -s2).astype(BF16)
        out_ref[:, kvw:] = dv.astype(BF16)

    own = pl.BlockSpec((G, BLK, kvw), lambda m: (m, 1, 0))
    prev = pl.BlockSpec((G, BLK, kvw), lambda m: (m, 2, 0))
    before = pl.BlockSpec((1, BLK, kvw), lambda m: (jnp.maximum(G * m - 1, 0), 2, 0))
    nxt = pl.BlockSpec((G, BLK, kvw), lambda m: (m, 0, 0))
    after = pl.BlockSpec((1, BLK, kvw), lambda m: (jnp.minimum(G * m + G, nb - 1), 0, 0))
    five = [own, prev, before, nxt, after]
    return pl.pallas_call(
        body, name="kv_sum", grid=(ng,),
        in_specs=[pl.BlockSpec(memory_space=pl.ANY), *five, *five, *_tab_specs(G * BLK)],
        out_specs=pl.BlockSpec((G * BLK, 2 * kvw), lambda m: (m, KV_OFF // (2 * kvw))),
        out_shape=jax.ShapeDtypeStruct((T, QKV), BF16),
        input_output_aliases={0: 0},
        compiler_params=_params("parallel"),
    )(dqkv, *([dkc] * 5), *([dvc] * 5), rc, rs1, rs2)


def _me():
    return lax.axis_index("x"), lax.axis_index("y"), lax.axis_index("c")


def _half_rows(ref, sharded_rows, chip, core):
    R, C = ref.shape[-2], ref.shape[-1]
    lead = (slice(None),) * (len(ref.shape) - 2)
    if sharded_rows:
        per = R // N_CHIPS
        return ref.at[lead + (pl.ds(chip * per + core * (per // 2), per // 2), slice(None))]
    per = C // N_CHIPS
    return ref.at[lead + (pl.ds(core * (R // 2), R // 2), pl.ds(chip * per, per))]


class _Gather:
    def __init__(self, shards, sharded_rows):
        self.inputs = list(shards)
        self.rows = list(sharded_rows)
        self.n = self.n_in = self.n_out = len(shards)
        self.out_shapes = []
        for s, rows in zip(shards, sharded_rows):
            shp = list(s.shape)
            shp[-2 if rows else -1] *= N_CHIPS
            self.out_shapes.append(jax.ShapeDtypeStruct(tuple(shp), s.dtype))
        self.scratch = [pltpu.SemaphoreType.DMA((self.n, 6)), pltpu.SemaphoreType.DMA((self.n, 6)),
                        pltpu.SemaphoreType.DMA((self.n, 2))]

    def _ctx(self, ins, outs, sems):
        send_sems, recv_sems, local_sems = sems
        x, y, c = _me()
        chips = [(1 - x, y), (x, 1 - y), (1 - x, 1 - y)]

        def half_src(w, core):
            s = ins[w]
            R = s.shape[-2]
            return s.at[pl.ds(core * (R // 2), R // 2), :]

        def dst(w, chip, core):
            return _half_rows(outs[w], self.rows[w], chip, core)

        def copy(w, k, src, chip, core, to):
            return pltpu.make_async_remote_copy(
                src_ref=src, dst_ref=dst(w, chip, core), send_sem=send_sems.at[w, k], recv_sem=recv_sems.at[w, k],
                device_id=to, device_id_type=MESH)

        def local(w, core):
            return pltpu.make_async_copy(half_src(w, core), dst(w, 2 * x + y, core), local_sems.at[w, core])

        def first(w, j):
            qx, qy = chips[j]
            return copy(w, j, half_src(w, c), 2 * x + y, c, (qx, qy, c))

        def landed(w, j):
            qx, qy = chips[j]
            return copy(w, j, dst(w, 2 * qx + qy, c), 2 * qx + qy, c, (x, y, c))

        def passed(w, j):
            qx, qy = chips[j]
            return copy(w, 3 + j, dst(w, 2 * qx + qy, c), 2 * qx + qy, c, (x, y, 1 - c))

        def from_sibling(w, j):
            qx, qy = chips[j]
            return copy(w, 3 + j, dst(w, 2 * qx + qy, 1 - c), 2 * qx + qy, 1 - c, (x, y, c))

        return local, first, landed, passed, from_sibling

    def start(self, ins, outs, sems):
        local, first, _, _, _ = self._ctx(ins, outs, sems)
        for w in range(self.n):
            for core in range(2):
                local(w, core).start()
            for j in range(3):
                first(w, j).start()

    def mid(self, ins, outs, sems):
        _, _, landed, passed, _ = self._ctx(ins, outs, sems)
        for w in range(self.n):
            for j in range(3):
                landed(w, j).wait_recv()
                passed(w, j).start()

    def end(self, ins, outs, sems):
        local, first, _, passed, from_sibling = self._ctx(ins, outs, sems)
        for w in range(self.n):
            for j in range(3):
                from_sibling(w, j).wait_recv()
        for w in range(self.n):
            for j in range(3):
                first(w, j).wait_send()
                passed(w, j).wait_send()
            for core in range(2):
                local(w, core).wait()


class _Scatter:
    def __init__(self, grads, small=None):
        self.inputs = list(grads) + ([small] if small is not None else [])
        self.ng = len(grads)
        self.n = self.n_in = self.n_out = len(self.inputs)
        self.out_shapes = [jax.ShapeDtypeStruct((N_DEV, g.shape[1] // 2, g.shape[2]), g.dtype) for g in grads]
        if small is not None:
            self.out_shapes.append(jax.ShapeDtypeStruct((N_DEV,) + small.shape, small.dtype))
        self.scratch = [pltpu.SemaphoreType.DMA((self.n, N_DEV)), pltpu.SemaphoreType.DMA((self.n, N_DEV)),
                        pltpu.SemaphoreType.DMA((self.n,))]

    def _ctx(self, ins, outs, sems):
        send_sems, recv_sems, local_sems = sems
        x, y, c = _me()
        me = 4 * x + 2 * y + c

        def piece(w, chip, core):
            if w >= self.ng:
                return ins[w]
            half = ins[w].shape[1] // 2
            return ins[w].at[chip, pl.ds(core * half, half), :]

        def peer_of(k):
            return x ^ ((k >> 2) & 1), y ^ ((k >> 1) & 1), c ^ (k & 1)

        def local(w):
            return pltpu.make_async_copy(piece(w, 2 * x + y, c), outs[w].at[me], local_sems.at[w])

        def send(w, k):
            px, py, pc = peer_of(k)
            return pltpu.make_async_remote_copy(
                src_ref=piece(w, 2 * px + py, pc), dst_ref=outs[w].at[me], send_sem=send_sems.at[w, k],
                recv_sem=recv_sems.at[w, k], device_id=(px, py, pc), device_id_type=MESH)

        def recv(w, k):
            px, py, pc = peer_of(k)
            return pltpu.make_async_remote_copy(
                src_ref=piece(w, 2 * x + y, c), dst_ref=outs[w].at[4 * px + 2 * py + pc], send_sem=send_sems.at[w, k],
                recv_sem=recv_sems.at[w, k], device_id=(px, py, pc), device_id_type=MESH)

        return local, send, recv

    def start(self, ins, outs, sems):
        local, send, _ = self._ctx(ins, outs, sems)
        for w in range(self.n):
            local(w).start()
            for k in range(1, N_DEV):
                send(w, k).start()

    def mid(self, ins, outs, sems):
        pass

    def end(self, ins, outs, sems):
        local, send, recv = self._ctx(ins, outs, sems)
        for w in range(self.n):
            for k in range(1, N_DEV):
                recv(w, k).wait_recv()
        for w in range(self.n):
            for k in range(1, N_DEV):
                send(w, k).wait_send()
            local(w).wait()


class _Both:
    def __init__(self, a, b):
        self.a, self.b = a, b
        self.inputs = a.inputs + b.inputs
        self.out_shapes = a.out_shapes + b.out_shapes
        self.scratch = a.scratch + b.scratch
        self.n_in, self.n_out = a.n_in + b.n_in, a.n_out + b.n_out

    def _split(self, ins, outs, sems):
        a, na = self.a, len(self.a.scratch)
        return (ins[:a.n_in], outs[:a.n_out], sems[:na]), (ins[a.n_in:], outs[a.n_out:], sems[na:])

    def start(self, ins, outs, sems):
        pa, pb = self._split(ins, outs, sems)
        self.a.start(*pa)
        self.b.start(*pb)

    def mid(self, ins, outs, sems):
        pa, pb = self._split(ins, outs, sems)
        self.a.mid(*pa)
        self.b.mid(*pb)

    def end(self, ins, outs, sems):
        pa, pb = self._split(ins, outs, sems)
        self.a.end(*pa)
        self.b.end(*pb)


def _call(body, *, name, grid, in_specs, out_specs, out_shape, scratch_shapes=(), semantics, args, comm=None):
    if comm is None:
        outs = pl.pallas_call(
            body, name=name, grid=grid, in_specs=in_specs, out_specs=out_specs, out_shape=out_shape,
            scratch_shapes=list(scratch_shapes), compiler_params=_params(*semantics))(*args)
        return outs, []
    n_in, n_out, n_scr = len(in_specs), len(out_specs), len(scratch_shapes)

    total = math.prod(grid)
    first, middle, last = 0, (3 * total) // 4 - 1, total - 1
    assert first <= middle < last

    def at(step):
        lin = pl.program_id(0)
        for d in range(1, len(grid)):
            lin = lin * grid[d] + pl.program_id(d)
        return lin == step

    def hosted(*refs):
        h_in, c_in = refs[:n_in], refs[n_in:n_in + comm.n_in]
        rest = refs[n_in + comm.n_in:]
        h_out, c_out = rest[:n_out], rest[n_out:n_out + comm.n_out]
        rest = rest[n_out + comm.n_out:]
        h_scr, c_scr = rest[:n_scr], rest[n_scr:]

        @pl.when(at(first))
        def _():
            comm.start(c_in, c_out, c_scr)

        body(*h_in, *h_out, *h_scr)

        @pl.when(at(middle))
        def _():
            comm.mid(c_in, c_out, c_scr)

        @pl.when(at(last))
        def _():
            comm.end(c_in, c_out, c_scr)

    any_spec = pl.BlockSpec(memory_space=pl.ANY)
    outs = pl.pallas_call(
        hosted, name=name, grid=grid, in_specs=list(in_specs) + [any_spec] * comm.n_in,
        out_specs=list(out_specs) + [any_spec] * comm.n_out, out_shape=list(out_shape) + comm.out_shapes,
        scratch_shapes=list(scratch_shapes) + comm.scratch,
        compiler_params=_params(*(["arbitrary"] * len(grid))))(*args, *comm.inputs)
    return outs[:n_out], outs[n_out:]


def sum_swap(pieces, name, comm=None):
    nl = len(pieces)
    _, r2, cc = pieces[0].shape
    tr = 256 if r2 % 256 == 0 else (128 if r2 % 128 == 0 else r2 // 2)
    n = r2 // tr

    def body(*refs):
        p_refs, out = refs[:nl], refs[nl]
        slots, send_sems, local_sems, recv_sem = refs[nl + 1:]
        x, y, c = _me()
        sibling = (x, y, 1 - c)
        l, i = pl.program_id(0), pl.program_id(1)
        step = l * n + i

        def rows(st, core):
            return out.at[st // n, pl.ds(core * r2 + (st % n) * tr, tr), :]

        def copies(st):
            slot = st % 2
            local = pltpu.make_async_copy(slots.at[slot], rows(st, c), local_sems.at[slot])
            remote = pltpu.make_async_remote_copy(
                src_ref=slots.at[slot], dst_ref=rows(st, c), send_sem=send_sems.at[slot], recv_sem=recv_sem,
                device_id=sibling, device_id_type=MESH)
            return local, remote

        for ll in range(nl):
            @pl.when(l == ll)
            def _():
                acc = p_refs[ll][0].astype(F32)
                for d in range(1, N_DEV):
                    acc = acc + p_refs[ll][d].astype(F32)
                slots[step % 2] = acc

        for cp in copies(step):
            cp.start()

        @pl.when(step >= 1)
        def _():
            local, remote = copies(step - 1)
            local.wait()
            remote.wait_send()

        @pl.when(step == nl * n - 1)
        def _():
            local, remote = copies(step)
            local.wait()
            remote.wait_send()
            theirs = out.at[:, pl.ds((1 - c) * r2, r2), :]
            pltpu.make_async_remote_copy(src_ref=theirs, dst_ref=theirs, send_sem=send_sems.at[0],
                                         recv_sem=recv_sem, device_id=sibling, device_id_type=MESH).wait_recv()

    def piece_spec(ll):
        def index(l, i):
            return (0, jnp.where(l == ll, i, jnp.where(l < ll, 0, n - 1)), 0)
        return pl.BlockSpec((N_DEV, tr, cc), index)

    (out,), got = _call(
        body, name=name, grid=(nl, n),
        in_specs=[piece_spec(ll) for ll in range(nl)],
        out_specs=[pl.BlockSpec(memory_space=pl.ANY)],
        out_shape=[jax.ShapeDtypeStruct((nl, 2 * r2, cc), F32)],
        scratch_shapes=[pltpu.VMEM((2, tr, cc), F32), pltpu.SemaphoreType.DMA((2,)), pltpu.SemaphoreType.DMA((2,)),
                        pltpu.SemaphoreType.DMA(())],
        semantics=("arbitrary", "arbitrary"), args=tuple(pieces), comm=comm)
    return (out, got) if comm is not None else out


def sum_pieces(pieces, name):
    _, R, C = pieces.shape
    tr = _tile(R, 128) if R % 128 == 0 else R

    def body(p_ref, o_ref):
        acc = p_ref[0].astype(F32)
        for d in range(1, N_DEV):
            acc = acc + p_ref[d].astype(F32)
        o_ref[...] = acc

    return pl.pallas_call(
        body, name=name, grid=(R // tr,),
        in_specs=[pl.BlockSpec((N_DEV, tr, C), lambda i: (0, i, 0))],
        out_specs=pl.BlockSpec((tr, C), lambda i: (i, 0)),
        out_shape=jax.ShapeDtypeStruct((R, C), F32),
        compiler_params=_params("parallel"),
    )(pieces)


def adamw(w, g, m, v, name):
    Lyr, R, C = w.shape
    tr = _tile(R, 256) if R % 8 == 0 else R
    c1 = 1.0 / (1.0 - ADAM_B1 ** ADAM_STEP)
    c2 = 1.0 / (1.0 - ADAM_B2 ** ADAM_STEP)

    def body(w_ref, g_ref, m_ref, v_ref, d_ref, nm_ref, nv_ref):
        gv = g_ref[...]
        nm = ADAM_B1 * m_ref[...] + (1.0 - ADAM_B1) * gv
        nv = ADAM_B2 * v_ref[...] + (1.0 - ADAM_B2) * (gv * gv)
        nm_ref[...] = nm
        nv_ref[...] = nv
        d_ref[...] = -ADAM_LR * ((nm * c1) / (jnp.sqrt(nv * c2) + ADAM_EPS) + ADAM_WD * w_ref[...])

    spec = pl.BlockSpec((None, tr, C), lambda l, i: (l, i, 0))
    shp = jax.ShapeDtypeStruct(w.shape, F32)
    return pl.pallas_call(
        body, name=name, grid=(Lyr, R // tr),
        in_specs=[spec] * 4, out_specs=[spec] * 3, out_shape=[shp] * 3,
        compiler_params=_params("parallel", "parallel"),
    )(w, g, m, v)


def _rope_tables(T):
    pos = jnp.arange(T, dtype=F32)
    inv_freq = THETA ** (-jnp.arange(0, ROT, 2, dtype=F32) / ROT)
    ang = pos[:, None] * inv_freq[None, :]
    cs = jnp.concatenate([jnp.cos(ang), jnp.sin(ang)], axis=1)
    half = ROT // 2
    lane = jnp.arange(3 * LANES)
    table, lm = lane // LANES, lane % HD
    src = jnp.where(table == 0, lm % half, half + lm % half)
    i32 = lambda b: b.astype(jnp.int32)
    sign = jnp.where(table == 0, i32(lm < ROT), jnp.where(table == 1, -i32(lm < half), i32((lm >= half) & (lm < ROT))))
    place = (jnp.arange(ROT)[:, None] == src[None, :]) * sign[None, :].astype(F32)
    ones = ((table == 0) & (lm >= ROT)).astype(F32)
    return jnp.dot(cs, place, precision=lax.Precision.HIGHEST) + ones[None, :]


def _tab_specs(rows):
    return [pl.BlockSpec((rows, LANES), lambda i, k=k: (i, k)) for k in range(3)]


def kernel(x, attn_norm, attn_w_qkv, attn_w_o, attn_sink, conv_norm, conv_w_pw1, conv_b_pw1, conv_w_dw, conv_b_dw, conv_ln_g, conv_ln_b, conv_w_pw2, conv_b_pw2, ffn_norm, ffn_w_gu, ffn_w_down, final_norm, loss_target, m_attn_norm, m_attn_w_qkv, m_attn_w_o, m_attn_sink, m_conv_norm, m_conv_w_pw1, m_conv_b_pw1, m_conv_w_dw, m_conv_b_dw, m_conv_ln_g, m_conv_ln_b, m_conv_w_pw2, m_conv_b_pw2, m_ffn_norm, m_ffn_w_gu, m_ffn_w_down, m_final_norm, v_attn_norm, v_attn_w_qkv, v_attn_w_o, v_attn_sink, v_conv_norm, v_conv_w_pw1, v_conv_b_pw1, v_conv_w_dw, v_conv_b_dw, v_conv_ln_g, v_conv_ln_b, v_conv_w_pw2, v_conv_b_pw2, v_ffn_norm, v_ffn_w_gu, v_ffn_w_down, v_final_norm):
    T = x.shape[1]
    x0 = x[0]
    target = loss_target[0]
    ix, iy = lax.axis_index("x"), lax.axis_index("y")
    chip = 2 * ix + iy
    rc = rs1 = rs2 = _rope_tables(T)

    bf = lambda t: t.astype(BF16)

    def place(vec, width):
        return lax.dynamic_update_slice(jnp.zeros((vec.shape[0], N_CHIPS * width), F32), vec, (0, chip * width))

    small_rows = jnp.concatenate([
        place(conv_norm, 256), place(conv_b_pw1, 512).reshape(2, D), place(conv_b_dw, 256), place(conv_ln_g, 256),
        place(conv_ln_b, 256), place(conv_b_pw2, 256), jnp.zeros((1, D), F32),
        place(conv_w_dw[0], 256), jnp.zeros((1, D), F32)], axis=0)

    h0, (w_qkv,) = rms_first(x0, attn_norm, comm=_Gather([bf(attn_w_qkv[0])], [False]))
    qkv, (w_o, got) = qkv_proj(h0, w_qkv, rc, rs1, rs2,
                               comm=_Both(_Gather([bf(attn_w_o[0])], [True]), _Scatter([], small_rows)))
    psmall = sum_pieces(got, "sum_small_params") * 0.5
    p_conv_norm, p_b_pw1 = psmall[0:1], psmall[1:3].reshape(1, 2 * D)
    p_b_dw, p_ln_g, p_ln_b, p_b_pw2 = psmall[3:4], psmall[4:5], psmall[5:6], psmall[6:7]
    p_w_dw = psmall[8:40]
    sink = attn_sink[0]
    o, (w_gu0,) = attn_fwd(qkv, sink, comm=_Gather([bf(ffn_w_gu[0])], [False]))
    zero_b = jnp.zeros((1, D), F32)
    zero_gu = jnp.zeros((1, 2 * DFF), F32)
    x1, h1, gu0, act0, (w_down0, w_pw1, w_pw2) = rms_mm_gate(
        (o, w_o, zero_b, x0), ffn_norm[0:1], w_gu0, zero_gu, DFF, True, BF16, "ffn0_up",
        comm=_Gather([bf(ffn_w_down[0]), bf(conv_w_pw1[0]), bf(conv_w_pw2[0])], [True, False, True]))
    x2, h2, pre, glu, (w_down1,) = rms_mm_gate((act0, w_down0, zero_b, x1), p_conv_norm, w_pw1, p_b_pw1, D, False, F32,
                                               "conv_pw1", comm=_Gather([bf(ffn_w_down[1])], [True]))
    dwc, sw, (w_gu1,) = conv_fwd(glu, p_w_dw, p_b_dw, p_ln_g, p_ln_b, comm=_Gather([bf(ffn_w_gu[1])], [False]))
    x3, h3, gu1, act1, _ = rms_mm_gate((sw, w_pw2, p_b_pw2, x2), ffn_norm[1:2], w_gu1, zero_gu, DFF, True, BF16,
                                       "ffn1_up")
    dx4, loss_part, d_final = mm_res_loss(act1, w_down1, x3, final_norm.reshape(1, D), target)

    dgu1, _ = swiglu_bwd(dx4, w_down1, gu1, "ffn1_down_bwd")
    g_down1 = dw_row(act1, dx4, "ffn1_down_dw")
    dx3, d_ffn1, ddwc, d_ln_g, d_ln_b, d_b_pw2, _ = mm_bt_rmsbwd(
        dgu1, w_gu1, x3, ffn_norm[1:2], dx4, "ffn1_up_bwd", conv_tail=(w_pw2, dwc, p_ln_g, p_ln_b))
    g_gu1 = dw_col(h3, dgu1, "ffn1_up_dw")

    g_pw2 = dw_row(sw, dx3, "conv_pw2_dw")
    dpre, d_w_dw, d_b_dw, d_b_pw1, (r_gu1, r_down1) = conv_bwd(ddwc, glu, pre, p_w_dw,
                                                               comm=_Scatter([g_gu1, g_down1]))
    dx2, d_conv_norm, _ = mm_bt_rmsbwd(dpre, w_pw1, x2, p_conv_norm, dx3, "conv_pw1_bwd")
    g_pw1 = dw_col(h2, dpre, "conv_pw1_dw")

    dgu0, (r_pw1, r_pw2) = swiglu_bwd(dx2, w_down0, gu0, "ffn0_down_bwd", comm=_Scatter([g_pw1, g_pw2]))
    g_down0 = dw_row(act0, dx2, "ffn0_down_dw")
    dx1, d_ffn0, do, _ = mm_bt_rmsbwd(dgu0, w_gu0, x1, ffn_norm[0:1], dx2, "ffn0_up_bwd", proj_w=w_o)
    g_gu0 = dw_col(h1, dgu0, "ffn0_up_dw")

    g_o = dw_row(o, dx1, "attn_out_dw")
    dq, dkc, dvc, d_sink, (r_gu0, r_down0, r_o) = attn_bwd(qkv, o, do, sink, rc, rs1, rs2,
                                                           comm=_Scatter([g_gu0, g_down0, g_o]))
    dqkv = kv_sum(dq, dkc, dvc, rc, rs1, rs2)[None]
    g_qkv = dw_col(h0, dqkv, "attn_qkv_dw")
    dx0, d_attn_norm, _ = mm_bt_rmsbwd(dqkv, w_qkv, x0, attn_norm, dx1, "attn_qkv_bwd")

    pad16 = lambda t: jnp.concatenate([t, jnp.zeros((1, D - t.shape[1]), F32)], axis=1)
    small_g = jnp.concatenate([
        d_attn_norm, pad16(d_sink), d_conv_norm, d_b_pw1.reshape(2, D), d_b_dw, d_ln_g, d_ln_b, d_b_pw2,
        d_ffn0, d_ffn1, d_final, pad16(loss_part), jnp.zeros((3, D), F32), d_w_dw], axis=0)
    gf_gu, (r_qkv, r_small) = sum_swap([r_gu0, r_gu1], "sum_gu", comm=_Scatter([g_qkv], small_g))
    gf_down = sum_swap([r_down0, r_down1], "sum_down")
    gf_pw1, gf_pw2 = sum_swap([r_pw1], "sum_pw1"), sum_swap([r_pw2], "sum_pw2")
    gf_qkv, gf_o = sum_swap([r_qkv], "sum_qkv"), sum_swap([r_o], "sum_o")
    gs = sum_pieces(r_small, "sum_small_grads")
    loss = gs[12, 0]

    def take(row0, nrows, width):
        return lax.dynamic_slice(gs, (row0, chip * width), (nrows, width))

    grads = {
        "attn_norm": gs[0:1], "attn_w_qkv": gf_qkv, "attn_w_o": gf_o, "attn_sink": gs[1:2, :N_HEADS],
        "conv_norm": take(2, 1, 256), "conv_w_pw1": gf_pw1,
        "conv_b_pw1": lax.dynamic_slice(gs[3:5].reshape(1, 2 * D), (0, chip * 512), (1, 512)),
        "conv_w_dw": take(16, 32, 256)[None, :CONV_W], "conv_b_dw": take(5, 1, 256), "conv_ln_g": take(6, 1, 256),
        "conv_ln_b": take(7, 1, 256), "conv_w_pw2": gf_pw2, "conv_b_pw2": take(8, 1, 256),
        "ffn_norm": gs[9:11], "ffn_w_gu": gf_gu, "ffn_w_down": gf_down, "final_norm": gs[11],
    }
    weights = dict(attn_norm=attn_norm, attn_w_qkv=attn_w_qkv, attn_w_o=attn_w_o, attn_sink=attn_sink,
                   conv_norm=conv_norm, conv_w_pw1=conv_w_pw1, conv_b_pw1=conv_b_pw1, conv_w_dw=conv_w_dw,
                   conv_b_dw=conv_b_dw, conv_ln_g=conv_ln_g, conv_ln_b=conv_ln_b, conv_w_pw2=conv_w_pw2,
                   conv_b_pw2=conv_b_pw2, ffn_norm=ffn_norm, ffn_w_gu=ffn_w_gu, ffn_w_down=ffn_w_down,
                   final_norm=final_norm)
    m_in = dict(attn_norm=m_attn_norm, attn_w_qkv=m_attn_w_qkv, attn_w_o=m_attn_w_o, attn_sink=m_attn_sink,
                conv_norm=m_conv_norm, conv_w_pw1=m_conv_w_pw1, conv_b_pw1=m_conv_b_pw1, conv_w_dw=m_conv_w_dw,
                conv_b_dw=m_conv_b_dw, conv_ln_g=m_conv_ln_g, conv_ln_b=m_conv_ln_b, conv_w_pw2=m_conv_w_pw2,
                conv_b_pw2=m_conv_b_pw2, ffn_norm=m_ffn_norm, ffn_w_gu=m_ffn_w_gu, ffn_w_down=m_ffn_w_down,
                final_norm=m_final_norm)
    v_in = dict(attn_norm=v_attn_norm, attn_w_qkv=v_attn_w_qkv, attn_w_o=v_attn_w_o, attn_sink=v_attn_sink,
                conv_norm=v_conv_norm, conv_w_pw1=v_conv_w_pw1, conv_b_pw1=v_conv_b_pw1, conv_w_dw=v_conv_w_dw,
                conv_b_dw=v_conv_b_dw, conv_ln_g=v_conv_ln_g, conv_ln_b=v_conv_ln_b, conv_w_pw2=v_conv_w_pw2,
                conv_b_pw2=v_conv_b_pw2, ffn_norm=v_ffn_norm, ffn_w_gu=v_ffn_w_gu, ffn_w_down=v_ffn_w_down,
                final_norm=v_final_norm)
    order = list(weights)
    g_out, d_out, m_out, v_out = [], [], [], []
    for nm in order:
        w = weights[nm]
        shape = w.shape
        as3 = lambda t: t.reshape((1,) * (3 - len(shape)) + shape) if len(shape) < 3 else t.reshape(shape)
        g3 = as3(grads[nm].reshape(shape))
        delta, nm_, nv_ = adamw(as3(w), g3, as3(m_in[nm]), as3(v_in[nm]), "adamw_" + nm)
        g_out.append(g3.reshape(shape))
        d_out.append(delta.reshape(shape))
        m_out.append(nm_.reshape(shape))
        v_out.append(nv_.reshape(shape))
    return (loss, dx0[None], *g_out, *d_out, *m_out, *v_out)
```

```python
import math

import jax
import jax.numpy as jnp
from jax import lax
from jax.experimental import pallas as pl
from jax.experimental.pallas import tpu as pltpu

F32 = jnp.float32
BF16 = jnp.bfloat16

D = 1024
N_HEADS = 16
N_KV = 4
GROUP = N_HEADS // N_KV
HD = 64
ROT = 16
THETA = 500000.0
BLK = 128
QKV = (N_HEADS + 2 * N_KV) * HD
KV_OFF = N_HEADS * HD
DFF = 2816
CONV_W = 31
CONV_PAD = 15
HALO = 16
CONV_JB = 16
CONV_JB_BWD = 16
EPS = 1e-6
NEG = -1e30
N_CHIPS = 4
N_DEV = 8
LANES = 128
SUBLANES = 8

ADAM_LR, ADAM_B1, ADAM_B2, ADAM_EPS, ADAM_WD, ADAM_STEP = 0.001, 0.9, 0.999, 1e-08, 0.01, 10

VMEM_LIMIT = 56 * 1024 * 1024
MESH = pl.DeviceIdType.MESH


def _params(*sem):
    return pltpu.CompilerParams(dimension_semantics=sem, vmem_limit_bytes=VMEM_LIMIT)


def _tile(n, want):
    if n <= want:
        return n
    for t in range(want, 7, -1):
        if n % t == 0 and t % 8 == 0:
            return t
    return n


MXU_COLS = 256


def _col_chunks(n):
    return [slice(c, min(c + MXU_COLS, n)) for c in range(0, n, MXU_COLS)]


def _sigmoid(v):
    return jax.nn.sigmoid(v)


def _rms_fwd(xv, gain):
    r = lax.rsqrt(jnp.mean(xv * xv, axis=-1, keepdims=True) + EPS)
    return xv * r * gain


def _rms_bwd(dh, xv, gain, dres):
    r = lax.rsqrt(jnp.mean(xv * xv, axis=-1, keepdims=True) + EPS)
    xhat = xv * r
    gy = dh * gain
    dx = r * (gy - xhat * jnp.mean(gy * xhat, axis=-1, keepdims=True))
    return dx + dres, dh * xhat


def _rope(blk, c, s1, s2):
    return blk * c + pltpu.roll(blk, LANES - ROT // 2, 1) * s1 + pltpu.roll(blk, ROT // 2, 1) * s2


def _dot(a, b):
    return jnp.dot(a, b, preferred_element_type=F32)


def _dot_tb(a, b):
    return lax.dot_general(a, b, (((1,), (1,)), ((), ())), preferred_element_type=F32)


def _dot_ta(a, b):
    return lax.dot_general(a, b, (((0,), (0,)), ((), ())), preferred_element_type=F32)


def rms_first(x, gain, comm):
    T = x.shape[0]
    tm = _tile(T, 512)

    def body(x_ref, g_ref, h_ref):
        h_ref[...] = _rms_fwd(x_ref[...], g_ref[...]).astype(BF16)

    (h,), got = _call(
        body, name="rms_first", grid=(T // tm,),
        in_specs=[pl.BlockSpec((tm, D), lambda i: (i, 0)), pl.BlockSpec((1, D), lambda i: (0, 0))],
        out_specs=[pl.BlockSpec((tm, D), lambda i: (i, 0))], out_shape=[jax.ShapeDtypeStruct((T, D), BF16)],
        semantics=("parallel",), args=(x, gain), comm=comm)
    return h, got


def qkv_proj(h, w, rc, rs1, rs2, comm=None):
    T = h.shape[0]
    tm = _tile(T, 1024)

    def body(h_ref, w_ref, c_ref, s1_ref, s2_ref, qkv_ref):
        acc = _dot(h_ref[...], w_ref[...])
        c, s1, s2 = c_ref[...], s1_ref[...], s2_ref[...]
        n_rot = (KV_OFF + N_KV * HD) // LANES
        for j in range(n_rot):
            sl = slice(LANES * j, LANES * (j + 1))
            roped = _rope(acc[:, sl], c, s1, s2)
            if j < KV_OFF // LANES:
                roped = roped * Q_SCALE
            qkv_ref[:, sl] = roped.astype(BF16)
        qkv_ref[:, n_rot * LANES:] = acc[:, n_rot * LANES:].astype(BF16)

    row = lambda i: (i, 0)
    full = lambda i: (0, 0)
    (qkv,), got = _call(
        body, name="qkv_proj", grid=(T // tm,),
        in_specs=[pl.BlockSpec((tm, D), row), pl.BlockSpec((D, QKV), full), *_tab_specs(tm)],
        out_specs=[pl.BlockSpec((tm, QKV), row)],
        out_shape=[jax.ShapeDtypeStruct((T, QKV), BF16)],
        semantics=("parallel",), args=(h, w, rc, rs1, rs2), comm=comm)
    return qkv, got


Q_SCALE = 1.0 / math.sqrt(HD)


def _attn_mask(n, T):
    ci = lax.broadcasted_iota(jnp.int32, (3 * BLK, BLK), 0)
    qi = lax.broadcasted_iota(jnp.int32, (3 * BLK, BLK), 1)
    key_pos = n * BLK - BLK + ci
    return (jnp.abs(ci - BLK - qi) <= BLK) & (key_pos >= 0) & (key_pos < T)


def _kv_padded(kv, first_tile):
    low = lax.broadcasted_iota(jnp.int32, (3 * BLK, LANES), 1) < HD
    zero = jnp.zeros((3 * BLK, LANES), BF16)
    out = {}
    for g in range(N_KV):
        t = kv[:, (first_tile + g // 2) * LANES:(first_tile + g // 2 + 1) * LANES]
        swapped = jnp.concatenate([t[:, HD:], t[:, :HD]], axis=1)
        for p in range(2):
            out[g, p] = jnp.where(low if p == 0 else ~low, t if g % 2 == p else swapped, zero)
    return out


def _pair_products(kvx, tile_of):
    both = {g: jnp.concatenate([kvx[g, 0], kvx[g, 1]], axis=0) for g in range(N_KV)}
    out = []
    for j in range(N_HEADS // 2):
        prod = _dot_tb(both[2 * j // GROUP], tile_of(j))
        out += [prod[:3 * BLK], prod[3 * BLK:]]
    return out


def _softmax_sink(s, valid, sk):
    s = jnp.where(valid, s, NEG)
    m = jnp.maximum(jnp.max(s, axis=0, keepdims=True), sk)
    e = jnp.exp(s - m)
    es = jnp.exp(sk - m)
    inv = 1.0 / (jnp.sum(e, axis=0, keepdims=True) + es)
    return e * inv, es * inv


def _attn_specs(T):
    nb = T // BLK
    kv_blk = 2 * N_KV * HD
    kv_col = KV_OFF // kv_blk
    q_spec = pl.BlockSpec((BLK, KV_OFF), lambda n: (n, 0))
    prev = pl.BlockSpec((BLK, kv_blk), lambda n: (jnp.maximum(n - 1, 0), kv_col))
    own = pl.BlockSpec((BLK, kv_blk), lambda n: (n, kv_col))
    nxt = pl.BlockSpec((BLK, kv_blk), lambda n: (jnp.minimum(n + 1, nb - 1), kv_col))
    return nb, q_spec, prev, own, nxt


def attn_fwd(qkv, sink, comm=None):
    T = qkv.shape[0]
    nb, q_spec, prev, own, nxt = _attn_specs(T)

    def body(sink_ref, q_ref, kp_ref, ko_ref, kn_ref, o_ref):
        valid = _attn_mask(pl.program_id(0), T)
        kv = jnp.concatenate([kp_ref[...], ko_ref[...], kn_ref[...]], axis=0)
        kx, vx = _kv_padded(kv, 0), _kv_padded(kv, 2)
        ss = _pair_products(kx, lambda j: q_ref[:, j * LANES:(j + 1) * LANES])
        ps = [_softmax_sink(ss[h], valid, sink_ref[h])[0].astype(BF16) for h in range(N_HEADS)]
        vxt = {k: v.T for k, v in vx.items()}
        for j in range(N_HEADS // 2):
            g = 2 * j // GROUP
            o_t = _dot(vxt[g, 0], ps[2 * j]) + _dot(vxt[g, 1], ps[2 * j + 1])
            o_ref[:, j * LANES:(j + 1) * LANES] = o_t.T.astype(BF16)

    (o,), got = _call(
        body, name="attn_fwd", grid=(nb,),
        in_specs=[pl.BlockSpec(memory_space=pltpu.SMEM), q_spec, prev, own, nxt],
        out_specs=[pl.BlockSpec((BLK, D), lambda n: (n, 0))],
        out_shape=[jax.ShapeDtypeStruct((T, D), BF16)],
        semantics=("parallel",), args=(sink, qkv, qkv, qkv, qkv), comm=comm)
    return o, got


def rms_mm_gate(x, gain, w, bias, H, swiglu, act_dtype, name, comm=None):
    fused = isinstance(x, tuple)
    T = (x[0] if fused else x).shape[0]
    tm = _tile(T, 512)

    def body(*refs):
        if fused:
            a_ref, wp_ref, bp_ref, r_ref, g_ref, w_ref, b_ref, x_ref, h_ref, pre_ref, act_ref = refs
            xv = _dot(a_ref[...], wp_ref[...]) + bp_ref[...] + r_ref[...]
            x_ref[...] = xv
        else:
            x_ref, g_ref, w_ref, b_ref, h_ref, pre_ref, act_ref = refs
            xv = x_ref[...]
        h = _rms_fwd(xv, g_ref[...]).astype(BF16)
        h_ref[...] = h
        for cs in _col_chunks(H):
            cs2 = slice(H + cs.start, H + cs.stop)
            a = _dot(h, w_ref[:, cs]) + b_ref[:, cs]
            b = _dot(h, w_ref[:, cs2]) + b_ref[:, cs2]
            pre_ref[0, :, cs] = a.astype(BF16)
            pre_ref[1, :, cs] = b.astype(BF16)
            if swiglu:
                act = a * _sigmoid(a) * b
            else:
                act = a * _sigmoid(b)
            act_ref[:, cs] = act.astype(act_dtype)

    row = lambda i: (i, 0)
    full = lambda i: (0, 0)
    if fused:
        K = x[0].shape[1]
        x_specs = [pl.BlockSpec((tm, K), row), pl.BlockSpec((K, D), full, pipeline_mode=pl.Buffered(1)),
                   pl.BlockSpec((1, D), full), pl.BlockSpec((tm, D), row)]
        x_out = ([pl.BlockSpec((tm, D), row)], [jax.ShapeDtypeStruct((T, D), F32)])
        x_args = tuple(x)
    else:
        x_specs, x_out, x_args = [pl.BlockSpec((tm, D), row)], ([], []), (x,)
    outs, got = _call(
        body, name=name, grid=(T // tm,),
        in_specs=x_specs + [pl.BlockSpec((1, D), full),
                            pl.BlockSpec((D, 2 * H), full, pipeline_mode=pl.Buffered(1)), pl.BlockSpec((1, 2 * H), full)],
        out_specs=x_out[0] + [pl.BlockSpec((tm, D), row), pl.BlockSpec((2, tm, H), lambda i: (0, i, 0)),
                              pl.BlockSpec((tm, H), row)],
        out_shape=x_out[1] + [jax.ShapeDtypeStruct((T, D), BF16), jax.ShapeDtypeStruct((2, T, H), BF16),
                              jax.ShapeDtypeStruct((T, H), act_dtype)],
        semantics=("parallel",), args=x_args + (gain, w, bias), comm=comm)
    return (*outs, got)


def _conv_tiles(T):
    tt = _tile(T, 512)
    return tt, tt // SUBLANES, D // LANES


def _fill_strided(ext, p, L):
    main = p[HALO:HALO + SUBLANES * L, :].reshape(SUBLANES, L, LANES)
    ext[CONV_PAD:CONV_PAD + L] = jnp.swapaxes(main, 0, 1)

    def ibody(i, carry):
        ext[i] = p[pl.ds(i + 1, SUBLANES, stride=L), :]
        ext[i + CONV_PAD + L] = p[pl.ds(i + CONV_PAD + L + 1, SUBLANES, stride=L), :]
        return carry

    lax.fori_loop(0, CONV_PAD, ibody, 0, unroll=3)


def _conv_specs(T, tt):
    main = pl.BlockSpec((tt, D), lambda i: (i, 0))
    per = tt // HALO
    prev = pl.BlockSpec((HALO, D), lambda i: (jnp.maximum(i * per - 1, 0), 0))
    nxt = pl.BlockSpec((HALO, D), lambda i: (jnp.minimum((i + 1) * per, T // HALO - 1), 0))
    return main, prev, nxt


def _fill_pad(pad, main_ref, prev_ref, next_ref, i, n_i, tt, nlt):
    keep_p = (i > 0).astype(F32)
    keep_n = (i < n_i - 1).astype(F32)
    for lt in range(nlt):
        sl = slice(lt * LANES, (lt + 1) * LANES)
        pad[lt, 0:HALO, :] = prev_ref[:, sl] * keep_p
        pad[lt, HALO:HALO + tt, :] = main_ref[:, sl]
        pad[lt, HALO + tt:2 * HALO + tt, :] = next_ref[:, sl] * keep_n


def conv_fwd(glu, w_dw, b_dw, ln_g, ln_b, comm=None):
    T = glu.shape[0]
    tt, L, nlt = _conv_tiles(T)
    n_i = T // tt
    main, prev, nxt = _conv_specs(T, tt)

    def body(x_ref, xp_ref, xn_ref, w_ref, b_ref, g_ref, bb_ref, dwc_ref, sw_ref, pad, ob, ext, wk):
        i = pl.program_id(0)
        _fill_pad(pad, x_ref, xp_ref, xn_ref, i, n_i, tt, nlt)
        for lt in range(nlt):
            sl = slice(lt * LANES, (lt + 1) * LANES)
            o = ob.at[lt]
            _fill_strided(ext, pad.at[lt], L)
            for k in range(CONV_W):
                wk[k] = jnp.broadcast_to(w_ref[k:k + 1, sl], (SUBLANES, LANES))

            def jbody(jb, carry):
                j = jb * CONV_JB
                accs = [None] * CONV_JB
                for m in range(CONV_W + CONV_JB - 1):
                    e = ext[j + m]
                    for u in range(CONV_JB):
                        if 0 <= m - u < CONV_W:
                            t = e * wk[m - u]
                            accs[u] = t if accs[u] is None else accs[u] + t
                for u in range(CONV_JB):
                    o[pl.ds(j + u, SUBLANES, stride=L), :] = accs[u]
                return carry

            lax.fori_loop(0, L // CONV_JB, jbody, 0)
        y = jnp.concatenate([ob[lt] for lt in range(nlt)], axis=1) + b_ref[...]
        dwc_ref[...] = y
        mu = jnp.mean(y, axis=-1, keepdims=True)
        yc = y - mu
        var = jnp.mean(yc * yc, axis=-1, keepdims=True)
        z = yc * lax.rsqrt(var + EPS) * g_ref[...] + bb_ref[...]
        sw_ref[...] = (z * _sigmoid(z)).astype(BF16)

    full = lambda i: (0, 0)
    (dwc, sw), got = _call(
        body, name="conv_fwd", grid=(n_i,),
        in_specs=[main, prev, nxt, pl.BlockSpec((32, D), full), pl.BlockSpec((1, D), full),
                  pl.BlockSpec((1, D), full), pl.BlockSpec((1, D), full)],
        out_specs=[pl.BlockSpec((tt, D), lambda i: (i, 0)), pl.BlockSpec((tt, D), lambda i: (i, 0))],
        out_shape=[jax.ShapeDtypeStruct((T, D), F32), jax.ShapeDtypeStruct((T, D), BF16)],
        scratch_shapes=[pltpu.VMEM((nlt, tt + 2 * HALO, LANES), F32), pltpu.VMEM((nlt, tt, LANES), F32),
                        pltpu.VMEM((L + 2 * HALO, SUBLANES, LANES), F32), pltpu.VMEM((32, SUBLANES, LANES), F32)],
        semantics=("parallel",), args=(glu, glu, glu, w_dw, b_dw, ln_g, ln_b), comm=comm)
    return dwc, sw, got


def mm_res_loss(a, w, resid, gain, target):
    T, K = a.shape
    tm = _tile(T, 512)

    def body(a_ref, w_ref, r_ref, g_ref, t_ref, dx_ref, loss_ref, dg_ref):
        @pl.when(pl.program_id(0) == 0)
        def _():
            loss_ref[...] = jnp.zeros_like(loss_ref)
            dg_ref[...] = jnp.zeros_like(dg_ref)

        xv, gain_v = _dot(a_ref[...], w_ref[...]) + r_ref[...], g_ref[...]
        err = _rms_fwd(xv, gain_v) - t_ref[...]
        part = 0.5 * jnp.sum(jnp.mean(err * err, axis=-1, keepdims=True), axis=0, keepdims=True)
        loss_ref[...] += jnp.broadcast_to(part, loss_ref.shape)
        dx, dgr = _rms_bwd(err * (1.0 / D), xv, gain_v, 0.0)
        dx_ref[...] = dx
        dg_ref[...] += jnp.sum(dgr, axis=0, keepdims=True)

    row = lambda i: (i, 0)
    full = lambda i: (0, 0)
    return pl.pallas_call(
        body, name="ffn1_down_loss", grid=(T // tm,),
        in_specs=[pl.BlockSpec((tm, K), row), pl.BlockSpec((K, D), full), pl.BlockSpec((tm, D), row),
                  pl.BlockSpec((1, D), full), pl.BlockSpec((tm, D), row)],
        out_specs=[pl.BlockSpec((tm, D), row), pl.BlockSpec((1, LANES), full), pl.BlockSpec((1, D), full)],
        out_shape=[jax.ShapeDtypeStruct((T, D), F32), jax.ShapeDtypeStruct((1, LANES), F32),
                   jax.ShapeDtypeStruct((1, D), F32)],
        compiler_params=_params("arbitrary"),
    )(a, w, resid, gain, target)


def swiglu_bwd(dx, w_down, pre, name, comm=None):
    T = dx.shape[0]
    H = w_down.shape[0]
    tm = _tile(T, 512)

    def body(dx_ref, w_ref, pre_ref, dpre_ref):
        dxb = dx_ref[...].astype(BF16)
        for cs in _col_chunks(H):
            dact = _dot_tb(dxb, w_ref[cs, :])
            g = pre_ref[0, :, cs].astype(F32)
            u = pre_ref[1, :, cs].astype(F32)
            sg = _sigmoid(g)
            dpre_ref[0, :, cs] = (dact * u * sg * (1.0 + g * (1.0 - sg))).astype(BF16)
            dpre_ref[1, :, cs] = (dact * g * sg).astype(BF16)

    (dpre,), got = _call(
        body, name=name, grid=(T // tm,),
        in_specs=[pl.BlockSpec((tm, D), lambda i: (i, 0)),
                  pl.BlockSpec((H, D), lambda i: (0, 0), pipeline_mode=pl.Buffered(1)),
                  pl.BlockSpec((2, tm, H), lambda i: (0, i, 0))],
        out_specs=[pl.BlockSpec((2, tm, H), lambda i: (0, i, 0))],
        out_shape=[jax.ShapeDtypeStruct((2, T, H), BF16)],
        semantics=("parallel",), args=(dx, w_down, pre), comm=comm)
    return dpre, got


def _ln_silu_bwd(dsw, y, ln_g, ln_b):
    mu = jnp.mean(y, axis=-1, keepdims=True)
    yc = y - mu
    rstd = lax.rsqrt(jnp.mean(yc * yc, axis=-1, keepdims=True) + EPS)
    xhat = yc * rstd
    z = xhat * ln_g + ln_b
    sg = _sigmoid(z)
    dz = dsw * sg * (1.0 + z * (1.0 - sg))
    dxh = dz * ln_g
    dy = rstd * (dxh - jnp.mean(dxh, axis=-1, keepdims=True) - xhat * jnp.mean(dxh * xhat, axis=-1, keepdims=True))
    return dy, dz * xhat, dz


def mm_bt_rmsbwd(dpre, w, x, gain, dres, name, comm=None, proj_w=None, conv_tail=None):
    nh, T, H = dpre.shape
    tm = _tile(T, 1024 if nh * H <= 2 * D else 512)
    n_extra_in = 1 if proj_w is not None else (4 if conv_tail is not None else 0)

    def body(*refs):
        dp_ref, w_ref, x_ref, g_ref, dres_ref = refs[:5]
        extra_in = refs[5:5 + n_extra_in]
        dx_ref, dg_ref = refs[5 + n_extra_in:7 + n_extra_in]
        extra_out = refs[7 + n_extra_in:]

        @pl.when(pl.program_id(0) == 0)
        def _():
            dg_ref[...] = jnp.zeros_like(dg_ref)
            for r in extra_out[1:]:
                r[...] = jnp.zeros_like(r)

        dh = _dot_tb(dp_ref[0], w_ref[:, 0:H])
        for hf in range(1, nh):
            dh = dh + _dot_tb(dp_ref[hf], w_ref[:, hf * H:(hf + 1) * H])
        dx, dgr = _rms_bwd(dh, x_ref[...], g_ref[...], dres_ref[...])
        dx_ref[...] = dx
        dg_ref[...] += jnp.sum(dgr, axis=0, keepdims=True)
        if proj_w is not None:
            extra_out[0][...] = _dot_tb(dx.astype(BF16), extra_in[0][...]).astype(BF16)
        elif conv_tail is not None:
            wt_ref, y_ref, lg_ref, lb_ref = extra_in
            dy, dgl, dbl = _ln_silu_bwd(_dot_tb(dx.astype(BF16), wt_ref[...]), y_ref[...], lg_ref[...], lb_ref[...])
            extra_out[0][...] = dy
            extra_out[1][...] += jnp.sum(dgl, axis=0, keepdims=True)
            extra_out[2][...] += jnp.sum(dbl, axis=0, keepdims=True)
            extra_out[3][...] += jnp.sum(dx, axis=0, keepdims=True)

    row = lambda i: (i, 0)
    full = lambda i: (0, 0)
    vec = pl.BlockSpec((1, D), full)
    vec_shape = jax.ShapeDtypeStruct((1, D), F32)
    in_specs = [pl.BlockSpec((nh, tm, H), lambda i: (0, i, 0)),
                pl.BlockSpec((D, nh * H), full, pipeline_mode=pl.Buffered(1)),
                pl.BlockSpec((tm, D), row), vec, pl.BlockSpec((tm, D), row)]
    out_specs = [pl.BlockSpec((tm, D), row), vec]
    out_shape = [jax.ShapeDtypeStruct((T, D), F32), vec_shape]
    args = (dpre, w, x, gain, dres)
    if proj_w is not None:
        N = proj_w.shape[0]
        in_specs.append(pl.BlockSpec((N, D), full, pipeline_mode=pl.Buffered(1)))
        out_specs.append(pl.BlockSpec((tm, N), row))
        out_shape.append(jax.ShapeDtypeStruct((T, N), BF16))
        args += (proj_w,)
    elif conv_tail is not None:
        in_specs += [pl.BlockSpec((D, D), full, pipeline_mode=pl.Buffered(1)), pl.BlockSpec((tm, D), row), vec, vec]
        out_specs += [pl.BlockSpec((tm, D), row), vec, vec, vec]
        out_shape += [jax.ShapeDtypeStruct((T, D), F32), vec_shape, vec_shape, vec_shape]
        args += tuple(conv_tail)
    outs, got = _call(body, name=name, grid=(T // tm,), in_specs=in_specs, out_specs=out_specs, out_shape=out_shape,
                      semantics=("arbitrary",), args=args, comm=comm)
    return (*outs, got)


def dw_col(a, dpre, name):
    T = a.shape[0]
    nh, _, H = dpre.shape
    per = nh * H // N_CHIPS
    bph = N_CHIPS // nh
    tt = _tile(T, 2048)
    nt = T // tt

    def body(a_ref, b_ref, o_ref, acc):
        t = pl.program_id(1)

        @pl.when(t == 0)
        def _():
            acc[...] = jnp.zeros_like(acc)

        acc[...] += _dot_ta(a_ref[...], b_ref[...])

        @pl.when(t == nt - 1)
        def _():
            o_ref[...] = acc[...].astype(BF16)

    return pl.pallas_call(
        body, name=name, grid=(N_CHIPS, nt),
        in_specs=[pl.BlockSpec((tt, D), lambda q, t: (t, 0)),
                  pl.BlockSpec((None, tt, per), lambda q, t: (q // bph, t, q % bph))],
        out_specs=pl.BlockSpec((None, D, per), lambda q, t: (q, 0, 0)),
        out_shape=jax.ShapeDtypeStruct((N_CHIPS, D, per), BF16),
        scratch_shapes=[pltpu.VMEM((D, per), F32)],
        compiler_params=_params("parallel", "arbitrary"),
    )(a, dpre)


def dw_row(a, b, name):
    T, R = a.shape
    cw = 1408 if R % 1408 == 0 else R
    tt = _tile(T, 2048 if R <= D else 1024)
    nt = T // tt

    def body(a_ref, b_ref, o_ref, acc):
        t = pl.program_id(1)

        @pl.when(t == 0)
        def _():
            acc[...] = jnp.zeros_like(acc)

        acc[...] += _dot_ta(a_ref[...], b_ref[...].astype(BF16))

        @pl.when(t == nt - 1)
        def _():
            o_ref[...] = acc[...].astype(BF16)

    out = pl.pallas_call(
        body, name=name, grid=(R // cw, nt),
        in_specs=[pl.BlockSpec((tt, cw), lambda q, t: (t, q)), pl.BlockSpec((tt, D), lambda q, t: (t, 0))],
        out_specs=pl.BlockSpec((cw, D), lambda q, t: (q, 0)),
        out_shape=jax.ShapeDtypeStruct((R, D), BF16),
        scratch_shapes=[pltpu.VMEM((cw, D), F32)],
        compiler_params=_params("parallel", "arbitrary"),
    )(a, b)
    return out.reshape(N_CHIPS, R // N_CHIPS, D)


def conv_bwd(ddwc, glu, pre, w_dw, comm=None):
    T = ddwc.shape[0]
    tt, L, nlt = _conv_tiles(T)
    n_i = T // tt
    main, prev, nxt = _conv_specs(T, tt)

    def body(d_ref, dp_ref, dn_ref, x_ref, xp_ref, xn_ref, pre_ref, w_ref,
             dpre_ref, dw_ref, dbd_ref, dbp_ref, padd, padx, ob, extd, extx, wk):
        i = pl.program_id(0)

        @pl.when(i == 0)
        def _():
            dw_ref[...] = jnp.zeros_like(dw_ref)
            dbd_ref[...] = jnp.zeros_like(dbd_ref)
            dbp_ref[...] = jnp.zeros_like(dbp_ref)

        _fill_pad(padd, d_ref, dp_ref, dn_ref, i, n_i, tt, nlt)
        _fill_pad(padx, x_ref, xp_ref, xn_ref, i, n_i, tt, nlt)
        for lt in range(nlt):
            sl = slice(lt * LANES, (lt + 1) * LANES)
            o = ob.at[lt]
            _fill_strided(extd, padd.at[lt], L)
            _fill_strided(extx, padx.at[lt], L)
            for k in range(CONV_W):
                wk[k] = jnp.broadcast_to(w_ref[k:k + 1, sl], (SUBLANES, LANES))

            nu = CONV_JB_BWD

            def jbody(jb, accs):
                j = jb * nu
                accs = list(accs)
                d = [extd[j + u + CONV_PAD] for u in range(nu)]
                g = [None] * nu
                for m in range(CONV_W + nu - 1):
                    ed = extd[j + 2 * CONV_PAD + nu - 1 - m]
                    ex = extx[j + m]
                    for u in range(nu):
                        k = m - (nu - 1 - u)
                        if 0 <= k < CONV_W:
                            t = ed * wk[k]
                            g[u] = t if g[u] is None else g[u] + t
                        k = m - u
                        if 0 <= k < CONV_W:
                            accs[k] = accs[k] + d[u] * ex
                for u in range(nu):
                    o[pl.ds(j + u, SUBLANES, stride=L), :] = g[u]
                return tuple(accs)

            accs = lax.fori_loop(0, L // nu, jbody, tuple(jnp.zeros((SUBLANES, LANES), F32) for _ in range(CONV_W)))
            for k in range(CONV_W):
                dw_ref[k:k + 1, sl] += jnp.sum(accs[k], axis=0, keepdims=True)
        dglu = jnp.concatenate([ob[lt] for lt in range(nlt)], axis=1)
        a = pre_ref[0].astype(F32)
        gate = pre_ref[1].astype(F32)
        sg = _sigmoid(gate)
        da = dglu * sg
        dgate = dglu * a * sg * (1.0 - sg)
        dpre_ref[0] = da.astype(BF16)
        dpre_ref[1] = dgate.astype(BF16)
        dbd_ref[...] += jnp.sum(d_ref[...], axis=0, keepdims=True)
        dbp_ref[0] += jnp.sum(da, axis=0, keepdims=True)
        dbp_ref[1] += jnp.sum(dgate, axis=0, keepdims=True)

    full = lambda i: (0, 0)
    (dpre, dw, dbd, dbp), got = _call(
        body, name="conv_bwd", grid=(n_i,),
        in_specs=[main, prev, nxt, main, prev, nxt, pl.BlockSpec((2, tt, D), lambda i: (0, i, 0)),
                  pl.BlockSpec((32, D), full)],
        out_specs=[pl.BlockSpec((2, tt, D), lambda i: (0, i, 0)), pl.BlockSpec((32, D), full),
                   pl.BlockSpec((1, D), full), pl.BlockSpec((2, 1, D), lambda i: (0, 0, 0))],
        out_shape=[jax.ShapeDtypeStruct((2, T, D), BF16), jax.ShapeDtypeStruct((32, D), F32),
                   jax.ShapeDtypeStruct((1, D), F32), jax.ShapeDtypeStruct((2, 1, D), F32)],
        scratch_shapes=[pltpu.VMEM((nlt, tt + 2 * HALO, LANES), F32), pltpu.VMEM((nlt, tt + 2 * HALO, LANES), F32),
                        pltpu.VMEM((nlt, tt, LANES), F32), pltpu.VMEM((L + 2 * HALO, SUBLANES, LANES), F32),
                        pltpu.VMEM((L + 2 * HALO, SUBLANES, LANES), F32), pltpu.VMEM((32, SUBLANES, LANES), F32)],
        semantics=("arbitrary",), args=(ddwc, ddwc, ddwc, glu, glu, glu, pre, w_dw), comm=comm)
    return dpre, dw, dbd, dbp, got


def attn_bwd(qkv, o, do, sink, rc, rs1, rs2, comm=None):
    T = qkv.shape[0]
    nb, q_spec, prev, own, nxt = _attn_specs(T)
    kvw = N_KV * HD

    def body(sink_ref, q_ref, kp_ref, ko_ref, kn_ref, o_ref, do_ref, c_ref, s1_ref, s2_ref,
             dq_ref, dkc_ref, dvc_ref, dsink_ref):
        n = pl.program_id(0)

        @pl.when(n == 0)
        def _():
            dsink_ref[...] = jnp.zeros_like(dsink_ref)

        valid = _attn_mask(n, T)
        kv = jnp.concatenate([kp_ref[...], ko_ref[...], kn_ref[...]], axis=0)
        kx, vx = _kv_padded(kv, 0), _kv_padded(kv, 2)
        tile = lambda ref, j: ref[:, j * LANES:(j + 1) * LANES]
        ss = _pair_products(kx, lambda j: tile(q_ref, j))
        dps = _pair_products(vx, lambda j: tile(do_ref, j))
        low_d = lax.broadcasted_iota(jnp.int32, (LANES, BLK), 0) < HD
        deltas = []
        for j in range(N_HEADS // 2):
            prod_t = tile(do_ref, j).astype(F32).T * tile(o_ref, j).astype(F32).T
            deltas.append(jnp.sum(jnp.where(low_d, prod_t, 0.0), axis=0, keepdims=True))
            deltas.append(jnp.sum(jnp.where(low_d, 0.0, prod_t), axis=0, keepdims=True))
        lane = lax.broadcasted_iota(jnp.int32, (1, N_HEADS), 1)
        dsink = jnp.zeros((1, N_HEADS), F32)
        pbs, dss = [], []
        for h in range(N_HEADS):
            p, p_sink = _softmax_sink(ss[h], valid, sink_ref[h])
            dss.append((p * (dps[h] - deltas[h])).astype(BF16))
            pbs.append(p.astype(BF16))
            part = -jnp.sum(p_sink * deltas[h], axis=1, keepdims=True)
            dsink = dsink + jnp.where(lane == h, part, 0.0)
        dsink_ref[...] += dsink
        c, s1, s2 = c_ref[...], s1_ref[...], s2_ref[...]
        kxt = {k: v.T for k, v in kx.items()}
        for j in range(N_HEADS // 2):
            g = 2 * j // GROUP
            dq_t = _dot(kxt[g, 0], dss[2 * j]) + _dot(kxt[g, 1], dss[2 * j + 1])
            dq_ref[:, j * LANES:(j + 1) * LANES] = (_rope(dq_t.T, c, -s1, -s2) * Q_SCALE).astype(BF16)
        low_k = lax.broadcasted_iota(jnp.int32, (3 * BLK, LANES), 1) < HD
        cols = lambda xs, g, p: jnp.concatenate([xs[GROUP * g + p], xs[GROUP * g + 2 + p]], axis=1)
        for t in range(N_KV // 2):
            sums = {}
            for g in (2 * t, 2 * t + 1):
                q2 = jnp.concatenate([tile(q_ref, 2 * g), tile(q_ref, 2 * g + 1)], axis=0)
                do2 = jnp.concatenate([tile(do_ref, 2 * g), tile(do_ref, 2 * g + 1)], axis=0)
                dk2 = _dot(jnp.concatenate([cols(dss, g, 0), cols(dss, g, 1)], axis=0), q2)
                dv2 = _dot(jnp.concatenate([cols(pbs, g, 0), cols(pbs, g, 1)], axis=0), do2)
                for p in range(2):
                    sums[g, p] = (dk2[p * 3 * BLK:(p + 1) * 3 * BLK], dv2[p * 3 * BLK:(p + 1) * 3 * BLK])
            for which, ref in ((0, dkc_ref), (1, dvc_ref)):
                keep = jnp.where(low_k, sums[2 * t, 0][which], sums[2 * t + 1, 1][which])
                swap = jnp.where(low_k, sums[2 * t + 1, 0][which], sums[2 * t, 1][which])
                ref[:, t * LANES:(t + 1) * LANES] = keep + pltpu.roll(swap, HD, 1)

    row = lambda n: (n, 0)
    (dq, dkc, dvc, dsink), got = _call(
        body, name="attn_bwd", grid=(nb,),
        in_specs=[pl.BlockSpec(memory_space=pltpu.SMEM), q_spec, prev, own, nxt,
                  pl.BlockSpec((BLK, D), row), pl.BlockSpec((BLK, D), row), *_tab_specs(BLK)],
        out_specs=[pl.BlockSpec((BLK, D), row), pl.BlockSpec((None, 3 * BLK, kvw), lambda n: (n, 0, 0)),
                   pl.BlockSpec((None, 3 * BLK, kvw), lambda n: (n, 0, 0)), pl.BlockSpec((1, N_HEADS), lambda n: (0, 0))],
        out_shape=[jax.ShapeDtypeStruct((T, QKV), BF16), jax.ShapeDtypeStruct((nb, 3 * BLK, kvw), F32),
                   jax.ShapeDtypeStruct((nb, 3 * BLK, kvw), F32), jax.ShapeDtypeStruct((1, N_HEADS), F32)],
        semantics=("arbitrary",), args=(sink, qkv, qkv, qkv, qkv, o, do, rc, rs1, rs2), comm=comm)
    return dq, dkc, dvc, dsink, got


def kv_sum(dqkv, dkc, dvc, rc, rs1, rs2):
    nb = dkc.shape[0]
    T = nb * BLK
    kvw = N_KV * HD

    G = 8
    ng = nb // G

    def gather3(own_ref, prev_ref, before_ref, next_ref, after_ref, m):
        has_before = (m > 0).astype(F32)
        has_after = (m < ng - 1).astype(F32)
        out = []
        for i in range(G):
            from_prev = prev_ref[i - 1] if i > 0 else before_ref[0] * has_before
            from_next = next_ref[i + 1] if i < G - 1 else after_ref[0] * has_after
            out.append(from_prev + own_ref[i] + from_next)
        return jnp.concatenate(out, axis=0)

    def body(_, ko, kp, kb, kn, ka, vo, vp, vb, vn, va, c_ref, s1_ref, s2_ref, out_ref):
        m = pl.program_id(0)
        dk = gather3(ko, kp, kb, kn, ka, m)
        dv = gather3(vo, vp, vb, vn, va, m)
        c, s1, s2 = c_ref[...], s1_ref[...], s2_ref[...]
        for j in range(kvw // LANES):
            sl = slice(LANES * j, LANES * (j + 1))
            out_ref[:, sl] = _rope(dk[:, sl], c, -s1, -s2).astype(BF16)
        out_ref[:, kvw:] = dv.astype(BF16)

    own = pl.BlockSpec((G, BLK, kvw), lambda m: (m, 1, 0))
    prev = pl.BlockSpec((G, BLK, kvw), lambda m: (m, 2, 0))
    before = pl.BlockSpec((1, BLK, kvw), lambda m: (jnp.maximum(G * m - 1, 0), 2, 0))
    nxt = pl.BlockSpec((G, BLK, kvw), lambda m: (m, 0, 0))
    after = pl.BlockSpec((1, BLK, kvw), lambda m: (jnp.minimum(G * m + G, nb - 1), 0, 0))
    five = [own, prev, before, nxt, after]
    return pl.pallas_call(
        body, name="kv_sum", grid=(ng,),
        in_specs=[pl.BlockSpec(memory_space=pl.ANY), *five, *five, *_tab_specs(G * BLK)],
        out_specs=pl.BlockSpec((G * BLK, 2 * kvw), lambda m: (m, KV_OFF // (2 * kvw))),
        out_shape=jax.ShapeDtypeStruct((T, QKV), BF16),
        input_output_aliases={0: 0},
        compiler_params=_params("parallel"),
    )(dqkv, *([dkc] * 5), *([dvc] * 5), rc, rs1, rs2)


def _me():
    return lax.axis_index("x"), lax.axis_index("y"), lax.axis_index("c")


def _half_rows(ref, sharded_rows, chip, core):
    R, C = ref.shape[-2], ref.shape[-1]
    lead = (slice(None),) * (len(ref.shape) - 2)
    if sharded_rows:
        per = R // N_CHIPS
        return ref.at[lead + (pl.ds(chip * per + core * (per // 2), per // 2), slice(None))]
    per = C // N_CHIPS
    return ref.at[lead + (pl.ds(core * (R // 2), R // 2), pl.ds(chip * per, per))]


class _Gather:
    def __init__(self, shards, sharded_rows):
        self.inputs = list(shards)
        self.rows = list(sharded_rows)
        self.n = self.n_in = self.n_out = len(shards)
        self.out_shapes = []
        for s, rows in zip(shards, sharded_rows):
            shp = list(s.shape)
            shp[-2 if rows else -1] *= N_CHIPS
            self.out_shapes.append(jax.ShapeDtypeStruct(tuple(shp), s.dtype))
        self.scratch = [pltpu.SemaphoreType.DMA((self.n, 6)), pltpu.SemaphoreType.DMA((self.n, 6)),
                        pltpu.SemaphoreType.DMA((self.n, 2))]

    def _ctx(self, ins, outs, sems):
        send_sems, recv_sems, local_sems = sems
        x, y, c = _me()
        chips = [(1 - x, y), (x, 1 - y), (1 - x, 1 - y)]

        def half_src(w, core):
            s = ins[w]
            R = s.shape[-2]
            return s.at[pl.ds(core * (R // 2), R // 2), :]

        def dst(w, chip, core):
            return _half_rows(outs[w], self.rows[w], chip, core)

        def copy(w, k, src, chip, core, to):
            return pltpu.make_async_remote_copy(
                src_ref=src, dst_ref=dst(w, chip, core), send_sem=send_sems.at[w, k], recv_sem=recv_sems.at[w, k],
                device_id=to, device_id_type=MESH)

        def local(w, core):
            return pltpu.make_async_copy(half_src(w, core), dst(w, 2 * x + y, core), local_sems.at[w, core])

        def first(w, j):
            qx, qy = chips[j]
            return copy(w, j, half_src(w, c), 2 * x + y, c, (qx, qy, c))

        def landed(w, j):
            qx, qy = chips[j]
            return copy(w, j, dst(w, 2 * qx + qy, c), 2 * qx + qy, c, (x, y, c))

        def passed(w, j):
            qx, qy = chips[j]
            return copy(w, 3 + j, dst(w, 2 * qx + qy, c), 2 * qx + qy, c, (x, y, 1 - c))

        def from_sibling(w, j):
            qx, qy = chips[j]
            return copy(w, 3 + j, dst(w, 2 * qx + qy, 1 - c), 2 * qx + qy, 1 - c, (x, y, c))

        return local, first, landed, passed, from_sibling

    def start(self, ins, outs, sems):
        local, first, _, _, _ = self._ctx(ins, outs, sems)
        for w in range(self.n):
            for core in range(2):
                local(w, core).start()
            for j in range(3):
                first(w, j).start()

    def mid(self, ins, outs, sems):
        _, _, landed, passed, _ = self._ctx(ins, outs, sems)
        for w in range(self.n):
            for j in range(3):
                landed(w, j).wait_recv()
                passed(w, j).start()

    def end(self, ins, outs, sems):
        local, first, _, passed, from_sibling = self._ctx(ins, outs, sems)
        for w in range(self.n):
            for j in range(3):
                from_sibling(w, j).wait_recv()
        for w in range(self.n):
            for j in range(3):
                first(w, j).wait_send()
                passed(w, j).wait_send()
            for core in range(2):
                local(w, core).wait()


class _Scatter:
    def __init__(self, grads, small=None):
        self.inputs = list(grads) + ([small] if small is not None else [])
        self.ng = len(grads)
        self.n = self.n_in = self.n_out = len(self.inputs)
        self.out_shapes = [jax.ShapeDtypeStruct((N_DEV, g.shape[1] // 2, g.shape[2]), g.dtype) for g in grads]
        if small is not None:
            self.out_shapes.append(jax.ShapeDtypeStruct((N_DEV,) + small.shape, small.dtype))
        self.scratch = [pltpu.SemaphoreType.DMA((self.n, N_DEV)), pltpu.SemaphoreType.DMA((self.n, N_DEV)),
                        pltpu.SemaphoreType.DMA((self.n,))]

    def _ctx(self, ins, outs, sems):
        send_sems, recv_sems, local_sems = sems
        x, y, c = _me()
        me = 4 * x + 2 * y + c

        def piece(w, chip, core):
            if w >= self.ng:
                return ins[w]
            half = ins[w].shape[1] // 2
            return ins[w].at[chip, pl.ds(core * half, half), :]

        def peer_of(k):
            return x ^ ((k >> 2) & 1), y ^ ((k >> 1) & 1), c ^ (k & 1)

        def local(w):
            return pltpu.make_async_copy(piece(w, 2 * x + y, c), outs[w].at[me], local_sems.at[w])

        def send(w, k):
            px, py, pc = peer_of(k)
            return pltpu.make_async_remote_copy(
                src_ref=piece(w, 2 * px + py, pc), dst_ref=outs[w].at[me], send_sem=send_sems.at[w, k],
                recv_sem=recv_sems.at[w, k], device_id=(px, py, pc), device_id_type=MESH)

        def recv(w, k):
            px, py, pc = peer_of(k)
            return pltpu.make_async_remote_copy(
                src_ref=piece(w, 2 * x + y, c), dst_ref=outs[w].at[4 * px + 2 * py + pc], send_sem=send_sems.at[w, k],
                recv_sem=recv_sems.at[w, k], device_id=(px, py, pc), device_id_type=MESH)

        return local, send, recv

    def start(self, ins, outs, sems):
        local, send, _ = self._ctx(ins, outs, sems)
        for w in range(self.n):
            local(w).start()
            for k in range(1, N_DEV):
                send(w, k).start()

    def mid(self, ins, outs, sems):
        pass

    def end(self, ins, outs, sems):
        local, send, recv = self._ctx(ins, outs, sems)
        for w in range(self.n):
            for k in range(1, N_DEV):
                recv(w, k).wait_recv()
        for w in range(self.n):
            for k in range(1, N_DEV):
                send(w, k).wait_send()
            local(w).wait()


class _Both:
    def __init__(self, a, b):
        self.a, self.b = a, b
        self.inputs = a.inputs + b.inputs
        self.out_shapes = a.out_shapes + b.out_shapes
        self.scratch = a.scratch + b.scratch
        self.n_in, self.n_out = a.n_in + b.n_in, a.n_out + b.n_out

    def _split(self, ins, outs, sems):
        a, na = self.a, len(self.a.scratch)
        return (ins[:a.n_in], outs[:a.n_out], sems[:na]), (ins[a.n_in:], outs[a.n_out:], sems[na:])

    def start(self, ins, outs, sems):
        pa, pb = self._split(ins, outs, sems)
        self.a.start(*pa)
        self.b.start(*pb)

    def mid(self, ins, outs, sems):
        pa, pb = self._split(ins, outs, sems)
        self.a.mid(*pa)
        self.b.mid(*pb)

    def end(self, ins, outs, sems):
        pa, pb = self._split(ins, outs, sems)
        self.a.end(*pa)
        self.b.end(*pb)


def _call(body, *, name, grid, in_specs, out_specs, out_shape, scratch_shapes=(), semantics, args, comm=None):
    if comm is None:
        outs = pl.pallas_call(
            body, name=name, grid=grid, in_specs=in_specs, out_specs=out_specs, out_shape=out_shape,
            scratch_shapes=list(scratch_shapes), compiler_params=_params(*semantics))(*args)
        return outs, []
    n_in, n_out, n_scr = len(in_specs), len(out_specs), len(scratch_shapes)

    total = math.prod(grid)
    first, middle, last = 0, (3 * total) // 4 - 1, total - 1
    assert first <= middle < last

    def at(step):
        lin = pl.program_id(0)
        for d in range(1, len(grid)):
            lin = lin * grid[d] + pl.program_id(d)
        return lin == step

    def hosted(*refs):
        h_in, c_in = refs[:n_in], refs[n_in:n_in + comm.n_in]
        rest = refs[n_in + comm.n_in:]
        h_out, c_out = rest[:n_out], rest[n_out:n_out + comm.n_out]
        rest = rest[n_out + comm.n_out:]
        h_scr, c_scr = rest[:n_scr], rest[n_scr:]

        @pl.when(at(first))
        def _():
            comm.start(c_in, c_out, c_scr)

        body(*h_in, *h_out, *h_scr)

        @pl.when(at(middle))
        def _():
            comm.mid(c_in, c_out, c_scr)

        @pl.when(at(last))
        def _():
            comm.end(c_in, c_out, c_scr)

    any_spec = pl.BlockSpec(memory_space=pl.ANY)
    outs = pl.pallas_call(
        hosted, name=name, grid=grid, in_specs=list(in_specs) + [any_spec] * comm.n_in,
        out_specs=list(out_specs) + [any_spec] * comm.n_out, out_shape=list(out_shape) + comm.out_shapes,
        scratch_shapes=list(scratch_shapes) + comm.scratch,
        compiler_params=_params(*(["arbitrary"] * len(grid))))(*args, *comm.inputs)
    return outs[:n_out], outs[n_out:]


def sum_swap(pieces, name, comm=None):
    nl = len(pieces)
    _, r2, cc = pieces[0].shape
    tr = 256 if r2 % 256 == 0 else (128 if r2 % 128 == 0 else r2 // 2)
    n = r2 // tr

    def body(*refs):
        p_refs, out = refs[:nl], refs[nl]
        slots, send_sems, local_sems, recv_sem = refs[nl + 1:]
        x, y, c = _me()
        sibling = (x, y, 1 - c)
        l, i = pl.program_id(0), pl.program_id(1)
        step = l * n + i

        def rows(st, core):
            return out.at[st // n, pl.ds(core * r2 + (st % n) * tr, tr), :]

        def copies(st):
            slot = st % 2
            local = pltpu.make_async_copy(slots.at[slot], rows(st, c), local_sems.at[slot])
            remote = pltpu.make_async_remote_copy(
                src_ref=slots.at[slot], dst_ref=rows(st, c), send_sem=send_sems.at[slot], recv_sem=recv_sem,
                device_id=sibling, device_id_type=MESH)
            return local, remote

        for ll in range(nl):
            @pl.when(l == ll)
            def _():
                acc = p_refs[ll][0].astype(F32)
                for d in range(1, N_DEV):
                    acc = acc + p_refs[ll][d].astype(F32)
                slots[step % 2] = acc

        for cp in copies(step):
            cp.start()

        @pl.when(step >= 1)
        def _():
            local, remote = copies(step - 1)
            local.wait()
            remote.wait_send()

        @pl.when(step == nl * n - 1)
        def _():
            local, remote = copies(step)
            local.wait()
            remote.wait_send()
            theirs = out.at[:, pl.ds((1 - c) * r2, r2), :]
            pltpu.make_async_remote_copy(src_ref=theirs, dst_ref=theirs, send_sem=send_sems.at[0],
                                         recv_sem=recv_sem, device_id=sibling, device_id_type=MESH).wait_recv()

    def piece_spec(ll):
        def index(l, i):
            return (0, jnp.where(l == ll, i, jnp.where(l < ll, 0, n - 1)), 0)
        return pl.BlockSpec((N_DEV, tr, cc), index)

    (out,), got = _call(
        body, name=name, grid=(nl, n),
        in_specs=[piece_spec(ll) for ll in range(nl)],
        out_specs=[pl.BlockSpec(memory_space=pl.ANY)],
        out_shape=[jax.ShapeDtypeStruct((nl, 2 * r2, cc), F32)],
        scratch_shapes=[pltpu.VMEM((2, tr, cc), F32), pltpu.SemaphoreType.DMA((2,)), pltpu.SemaphoreType.DMA((2,)),
                        pltpu.SemaphoreType.DMA(())],
        semantics=("arbitrary", "arbitrary"), args=tuple(pieces), comm=comm)
    return (out, got) if comm is not None else out


def sum_pieces(pieces, name):
    _, R, C = pieces.shape
    tr = _tile(R, 128) if R % 128 == 0 else R

    def body(p_ref, o_ref):
        acc = p_ref[0].astype(F32)
        for d in range(1, N_DEV):
            acc = acc + p_ref[d].astype(F32)
        o_ref[...] = acc

    return pl.pallas_call(
        body, name=name, grid=(R // tr,),
        in_specs=[pl.BlockSpec((N_DEV, tr, C), lambda i: (0, i, 0))],
        out_specs=pl.BlockSpec((tr, C), lambda i: (i, 0)),
        out_shape=jax.ShapeDtypeStruct((R, C), F32),
        compiler_params=_params("parallel"),
    )(pieces)


def adamw(w, g, m, v, name):
    Lyr, R, C = w.shape
    tr = _tile(R, 256) if R % 8 == 0 else R
    c1 = 1.0 / (1.0 - ADAM_B1 ** ADAM_STEP)
    c2 = 1.0 / (1.0 - ADAM_B2 ** ADAM_STEP)

    def body(w_ref, g_ref, m_ref, v_ref, d_ref, nm_ref, nv_ref):
        gv = g_ref[...]
        nm = ADAM_B1 * m_ref[...] + (1.0 - ADAM_B1) * gv
        nv = ADAM_B2 * v_ref[...] + (1.0 - ADAM_B2) * (gv * gv)
        nm_ref[...] = nm
        nv_ref[...] = nv
        d_ref[...] = -ADAM_LR * ((nm * c1) / (jnp.sqrt(nv * c2) + ADAM_EPS) + ADAM_WD * w_ref[...])

    spec = pl.BlockSpec((None, tr, C), lambda l, i: (l, i, 0))
    shp = jax.ShapeDtypeStruct(w.shape, F32)
    return pl.pallas_call(
        body, name=name, grid=(Lyr, R // tr),
        in_specs=[spec] * 4, out_specs=[spec] * 3, out_shape=[shp] * 3,
        compiler_params=_params("parallel", "parallel"),
    )(w, g, m, v)


def _rope_tables(T):
    pos = jnp.arange(T, dtype=F32)
    inv_freq = THETA ** (-jnp.arange(0, ROT, 2, dtype=F32) / ROT)
    ang = pos[:, None] * inv_freq[None, :]
    cs = jnp.concatenate([jnp.cos(ang), jnp.sin(ang)], axis=1)
    half = ROT // 2
    lane = jnp.arange(3 * LANES)
    table, lm = lane // LANES, lane % HD
    src = jnp.where(table == 0, lm % half, half + lm % half)
    i32 = lambda b: b.astype(jnp.int32)
    sign = jnp.where(table == 0, i32(lm < ROT), jnp.where(table == 1, -i32(lm < half), i32((lm >= half) & (lm < ROT))))
    place = (jnp.arange(ROT)[:, None] == src[None, :]) * sign[None, :].astype(F32)
    ones = ((table == 0) & (lm >= ROT)).astype(F32)
    return jnp.dot(cs, place, precision=lax.Precision.HIGHEST) + ones[None, :]


def _tab_specs(rows):
    return [pl.BlockSpec((rows, LANES), lambda i, k=k: (i, k)) for k in range(3)]


def kernel(x, attn_norm, attn_w_qkv, attn_w_o, attn_sink, conv_norm, conv_w_pw1, conv_b_pw1, conv_w_dw, conv_b_dw, conv_ln_g, conv_ln_b, conv_w_pw2, conv_b_pw2, ffn_norm, ffn_w_gu, ffn_w_down, final_norm, loss_target, m_attn_norm, m_attn_w_qkv, m_attn_w_o, m_attn_sink, m_conv_norm, m_conv_w_pw1, m_conv_b_pw1, m_conv_w_dw, m_conv_b_dw, m_conv_ln_g, m_conv_ln_b, m_conv_w_pw2, m_conv_b_pw2, m_ffn_norm, m_ffn_w_gu, m_ffn_w_down, m_final_norm, v_attn_norm, v_attn_w_qkv, v_attn_w_o, v_attn_sink, v_conv_norm, v_conv_w_pw1, v_conv_b_pw1, v_conv_w_dw, v_conv_b_dw, v_conv_ln_g, v_conv_ln_b, v_conv_w_pw2, v_conv_b_pw2, v_ffn_norm, v_ffn_w_gu, v_ffn_w_down, v_final_norm):
    T = x.shape[1]
    x0 = x[0]
    target = loss_target[0]
    ix, iy = lax.axis_index("x"), lax.axis_index("y")
    chip = 2 * ix + iy
    rc = rs1 = rs2 = _rope_tables(T)

    bf = lambda t: t.astype(BF16)

    def place(vec, width):
        return lax.dynamic_update_slice(jnp.zeros((vec.shape[0], N_CHIPS * width), F32), vec, (0, chip * width))

    small_rows = jnp.concatenate([
        place(conv_norm, 256), place(conv_b_pw1, 512).reshape(2, D), place(conv_b_dw, 256), place(conv_ln_g, 256),
        place(conv_ln_b, 256), place(conv_b_pw2, 256), jnp.zeros((1, D), F32),
        place(conv_w_dw[0], 256), jnp.zeros((1, D), F32)], axis=0)

    h0, (w_qkv,) = rms_first(x0, attn_norm, comm=_Gather([bf(attn_w_qkv[0])], [False]))
    qkv, (w_o, got) = qkv_proj(h0, w_qkv, rc, rs1, rs2,
                               comm=_Both(_Gather([bf(attn_w_o[0])], [True]), _Scatter([], small_rows)))
    psmall = sum_pieces(got, "sum_small_params") * 0.5
    p_conv_norm, p_b_pw1 = psmall[0:1], psmall[1:3].reshape(1, 2 * D)
    p_b_dw, p_ln_g, p_ln_b, p_b_pw2 = psmall[3:4], psmall[4:5], psmall[5:6], psmall[6:7]
    p_w_dw = psmall[8:40]
    sink = attn_sink[0]
    o, (w_gu0,) = attn_fwd(qkv, sink, comm=_Gather([bf(ffn_w_gu[0])], [False]))
    zero_b = jnp.zeros((1, D), F32)
    zero_gu = jnp.zeros((1, 2 * DFF), F32)
    x1, h1, gu0, act0, (w_down0, w_pw1, w_pw2) = rms_mm_gate(
        (o, w_o, zero_b, x0), ffn_norm[0:1], w_gu0, zero_gu, DFF, True, BF16, "ffn0_up",
        comm=_Gather([bf(ffn_w_down[0]), bf(conv_w_pw1[0]), bf(conv_w_pw2[0])], [True, False, True]))
    x2, h2, pre, glu, (w_down1,) = rms_mm_gate((act0, w_down0, zero_b, x1), p_conv_norm, w_pw1, p_b_pw1, D, False, F32,
                                               "conv_pw1", comm=_Gather([bf(ffn_w_down[1])], [True]))
    dwc, sw, (w_gu1,) = conv_fwd(glu, p_w_dw, p_b_dw, p_ln_g, p_ln_b, comm=_Gather([bf(ffn_w_gu[1])], [False]))
    x3, h3, gu1, act1, _ = rms_mm_gate((sw, w_pw2, p_b_pw2, x2), ffn_norm[1:2], w_gu1, zero_gu, DFF, True, BF16,
                                       "ffn1_up")
    dx4, loss_part, d_final = mm_res_loss(act1, w_down1, x3, final_norm.reshape(1, D), target)

    dgu1, _ = swiglu_bwd(dx4, w_down1, gu1, "ffn1_down_bwd")
    g_down1 = dw_row(act1, dx4, "ffn1_down_dw")
    dx3, d_ffn1, ddwc, d_ln_g, d_ln_b, d_b_pw2, _ = mm_bt_rmsbwd(
        dgu1, w_gu1, x3, ffn_norm[1:2], dx4, "ffn1_up_bwd", conv_tail=(w_pw2, dwc, p_ln_g, p_ln_b))
    g_gu1 = dw_col(h3, dgu1, "ffn1_up_dw")

    g_pw2 = dw_row(sw, dx3, "conv_pw2_dw")
    dpre, d_w_dw, d_b_dw, d_b_pw1, (r_gu1, r_down1) = conv_bwd(ddwc, glu, pre, p_w_dw,
                                                               comm=_Scatter([g_gu1, g_down1]))
    dx2, d_conv_norm, _ = mm_bt_rmsbwd(dpre, w_pw1, x2, p_conv_norm, dx3, "conv_pw1_bwd")
    g_pw1 = dw_col(h2, dpre, "conv_pw1_dw")

    dgu0, (r_pw1, r_pw2) = swiglu_bwd(dx2, w_down0, gu0, "ffn0_down_bwd", comm=_Scatter([g_pw1, g_pw2]))
    g_down0 = dw_row(act0, dx2, "ffn0_down_dw")
    dx1, d_ffn0, do, _ = mm_bt_rmsbwd(dgu0, w_gu0, x1, ffn_norm[0:1], dx2, "ffn0_up_bwd", proj_w=w_o)
    g_gu0 = dw_col(h1, dgu0, "ffn0_up_dw")

    g_o = dw_row(o, dx1, "attn_out_dw")
    dq, dkc, dvc, d_sink, (r_gu0, r_down0, r_o) = attn_bwd(qkv, o, do, sink, rc, rs1, rs2,
                                                           comm=_Scatter([g_gu0, g_down0, g_o]))
    dqkv = kv_sum(dq, dkc, dvc, rc, rs1, rs2)[None]
    g_qkv = dw_col(h0, dqkv, "attn_qkv_dw")
    dx0, d_attn_norm, (r_qkv,) = mm_bt_rmsbwd(dqkv, w_qkv, x0, attn_norm, dx1, "attn_qkv_bwd",
                                              comm=_Scatter([g_qkv]))

    pad16 = lambda t: jnp.concatenate([t, jnp.zeros((1, D - t.shape[1]), F32)], axis=1)
    small_g = jnp.concatenate([
        d_attn_norm, pad16(d_sink), d_conv_norm, d_b_pw1.reshape(2, D), d_b_dw, d_ln_g, d_ln_b, d_b_pw2,
        d_ffn0, d_ffn1, d_final, pad16(loss_part), jnp.zeros((3, D), F32), d_w_dw], axis=0)
    gf_gu, (r_small,) = sum_swap([r_gu0, r_gu1], "sum_gu", comm=_Scatter([], small_g))
    gf_down = sum_swap([r_down0, r_down1], "sum_down")
    gf_pw1, gf_pw2 = sum_swap([r_pw1], "sum_pw1"), sum_swap([r_pw2], "sum_pw2")
    gf_qkv, gf_o = sum_swap([r_qkv], "sum_qkv"), sum_swap([r_o], "sum_o")
    gs = sum_pieces(r_small, "sum_small_grads")
    loss = gs[12, 0]

    def take(row0, nrows, width):
        return lax.dynamic_slice(gs, (row0, chip * width), (nrows, width))

    grads = {
        "attn_norm": gs[0:1], "attn_w_qkv": gf_qkv, "attn_w_o": gf_o, "attn_sink": gs[1:2, :N_HEADS],
        "conv_norm": take(2, 1, 256), "conv_w_pw1": gf_pw1,
        "conv_b_pw1": lax.dynamic_slice(gs[3:5].reshape(1, 2 * D), (0, chip * 512), (1, 512)),
        "conv_w_dw": take(16, 32, 256)[None, :CONV_W], "conv_b_dw": take(5, 1, 256), "conv_ln_g": take(6, 1, 256),
        "conv_ln_b": take(7, 1, 256), "conv_w_pw2": gf_pw2, "conv_b_pw2": take(8, 1, 256),
        "ffn_norm": gs[9:11], "ffn_w_gu": gf_gu, "ffn_w_down": gf_down, "final_norm": gs[11],
    }
    weights = dict(attn_norm=attn_norm, attn_w_qkv=attn_w_qkv, attn_w_o=attn_w_o, attn_sink=attn_sink,
                   conv_norm=conv_norm, conv_w_pw1=conv_w_pw1, conv_b_pw1=conv_b_pw1, conv_w_dw=conv_w_dw,
                   conv_b_dw=conv_b_dw, conv_ln_g=conv_ln_g, conv_ln_b=conv_ln_b, conv_w_pw2=conv_w_pw2,
                   conv_b_pw2=conv_b_pw2, ffn_norm=ffn_norm, ffn_w_gu=ffn_w_gu, ffn_w_down=ffn_w_down,
                   final_norm=final_norm)
    m_in = dict(attn_norm=m_attn_norm, attn_w_qkv=m_attn_w_qkv, attn_w_o=m_attn_w_o, attn_sink=m_attn_sink,
                conv_norm=m_conv_norm, conv_w_pw1=m_conv_w_pw1, conv_b_pw1=m_conv_b_pw1, conv_w_dw=m_conv_w_dw,
                conv_b_dw=m_conv_b_dw, conv_ln_g=m_conv_ln_g, conv_ln_b=m_conv_ln_b, conv_w_pw2=m_conv_w_pw2,
                conv_b_pw2=m_conv_b_pw2, ffn_norm=m_ffn_norm, ffn_w_gu=m_ffn_w_gu, ffn_w_down=m_ffn_w_down,
                final_norm=m_final_norm)
    v_in = dict(attn_norm=v_attn_norm, attn_w_qkv=v_attn_w_qkv, attn_w_o=v_attn_w_o, attn_sink=v_attn_sink,
                conv_norm=v_conv_norm, conv_w_pw1=v_conv_w_pw1, conv_b_pw1=v_conv_b_pw1, conv_w_dw=v_conv_w_dw,
                conv_b_dw=v_conv_b_dw, conv_ln_g=v_conv_ln_g, conv_ln_b=v_conv_ln_b, conv_w_pw2=v_conv_w_pw2,
                conv_b_pw2=v_conv_b_pw2, ffn_norm=v_ffn_norm, ffn_w_gu=v_ffn_w_gu, ffn_w_down=v_ffn_w_down,
                final_norm=v_final_norm)
    order = list(weights)
    g_out, d_out, m_out, v_out = [], [], [], []
    for nm in order:
        w = weights[nm]
        shape = w.shape
        as3 = lambda t: t.reshape((1,) * (3 - len(shape)) + shape) if len(shape) < 3 else t.reshape(shape)
        g3 = as3(grads[nm].reshape(shape))
        delta, nm_, nv_ = adamw(as3(w), g3, as3(m_in[nm]), as3(v_in[nm]), "adamw_" + nm)
        g_out.append(g3.reshape(shape))
        d_out.append(delta.reshape(shape))
        m_out.append(nm_.reshape(shape))
        v_out.append(nv_.reshape(shape))
    return (loss, dx0[None], *g_out, *d_out, *m_out, *v_out)
```

```python
import math

import jax
import jax.numpy as jnp
from jax import lax
from jax.experimental import pallas as pl
from jax.experimental.pallas import tpu as pltpu

F32 = jnp.float32
BF16 = jnp.bfloat16

D = 1024
N_HEADS = 16
N_KV = 4
GROUP = N_HEADS // N_KV
HD = 64
ROT = 16
THETA = 500000.0
BLK = 128
QKV = (N_HEADS + 2 * N_KV) * HD
KV_OFF = N_HEADS * HD
DFF = 2816
CONV_W = 31
CONV_PAD = 15
HALO = 16
CONV_JB = 16
CONV_JB_BWD = 16
EPS = 1e-6
NEG = -1e30
N_CHIPS = 4
N_DEV = 8
LANES = 128
SUBLANES = 8

ADAM_LR, ADAM_B1, ADAM_B2, ADAM_EPS, ADAM_WD, ADAM_STEP = 0.001, 0.9, 0.999, 1e-08, 0.01, 10

VMEM_LIMIT = 56 * 1024 * 1024
MESH = pl.DeviceIdType.MESH


def _params(*sem):
    return pltpu.CompilerParams(dimension_semantics=sem, vmem_limit_bytes=VMEM_LIMIT)


def _tile(n, want):
    if n <= want:
        return n
    for t in range(want, 7, -1):
        if n % t == 0 and t % 8 == 0:
            return t
    return n


MXU_COLS = 256


def _col_chunks(n):
    return [slice(c, min(c + MXU_COLS, n)) for c in range(0, n, MXU_COLS)]


def _sigmoid(v):
    return jax.nn.sigmoid(v)


def _rms_fwd(xv, gain):
    r = lax.rsqrt(jnp.mean(xv * xv, axis=-1, keepdims=True) + EPS)
    return xv * r * gain


def _rms_bwd(dh, xv, gain, dres):
    r = lax.rsqrt(jnp.mean(xv * xv, axis=-1, keepdims=True) + EPS)
    xhat = xv * r
    gy = dh * gain
    dx = r * (gy - xhat * jnp.mean(gy * xhat, axis=-1, keepdims=True))
    return dx + dres, dh * xhat


def _rope(blk, c, s1, s2):
    return blk * c + pltpu.roll(blk, LANES - ROT // 2, 1) * s1 + pltpu.roll(blk, ROT // 2, 1) * s2


def _dot(a, b):
    return jnp.dot(a, b, preferred_element_type=F32)


def _dot_tb(a, b):
    return lax.dot_general(a, b, (((1,), (1,)), ((), ())), preferred_element_type=F32)


def _dot_ta(a, b):
    return lax.dot_general(a, b, (((0,), (0,)), ((), ())), preferred_element_type=F32)


def rms_first(x, gain, comm):
    T = x.shape[0]
    tm = _tile(T, 512)

    def body(x_ref, g_ref, h_ref):
        h_ref[...] = _rms_fwd(x_ref[...], g_ref[...]).astype(BF16)

    (h,), got = _call(
        body, name="rms_first", grid=(T // tm,),
        in_specs=[pl.BlockSpec((tm, D), lambda i: (i, 0)), pl.BlockSpec((1, D), lambda i: (0, 0))],
        out_specs=[pl.BlockSpec((tm, D), lambda i: (i, 0))], out_shape=[jax.ShapeDtypeStruct((T, D), BF16)],
        semantics=("parallel",), args=(x, gain), comm=comm)
    return h, got


def qkv_proj(h, w, rc, rs1, rs2, comm=None):
    T = h.shape[0]
    tm = _tile(T, 1024)

    def body(h_ref, w_ref, c_ref, s1_ref, s2_ref, qkv_ref):
        acc = _dot(h_ref[...], w_ref[...])
        c, s1, s2 = c_ref[...], s1_ref[...], s2_ref[...]
        n_rot = (KV_OFF + N_KV * HD) // LANES
        for j in range(n_rot):
            sl = slice(LANES * j, LANES * (j + 1))
            roped = _rope(acc[:, sl], c, s1, s2)
            if j < KV_OFF // LANES:
                roped = roped * Q_SCALE
            qkv_ref[:, sl] = roped.astype(BF16)
        qkv_ref[:, n_rot * LANES:] = acc[:, n_rot * LANES:].astype(BF16)

    row = lambda i: (i, 0)
    full = lambda i: (0, 0)
    (qkv,), got = _call(
        body, name="qkv_proj", grid=(T // tm,),
        in_specs=[pl.BlockSpec((tm, D), row), pl.BlockSpec((D, QKV), full), *_tab_specs(tm)],
        out_specs=[pl.BlockSpec((tm, QKV), row)],
        out_shape=[jax.ShapeDtypeStruct((T, QKV), BF16)],
        semantics=("parallel",), args=(h, w, rc, rs1, rs2), comm=comm)
    return qkv, got


Q_SCALE = 1.0 / math.sqrt(HD)


def _attn_mask(n, T):
    ci = lax.broadcasted_iota(jnp.int32, (3 * BLK, BLK), 0)
    qi = lax.broadcasted_iota(jnp.int32, (3 * BLK, BLK), 1)
    key_pos = n * BLK - BLK + ci
    return (jnp.abs(ci - BLK - qi) <= BLK) & (key_pos >= 0) & (key_pos < T)


def _kv_padded(kv, first_tile):
    low = lax.broadcasted_iota(jnp.int32, (3 * BLK, LANES), 1) < HD
    zero = jnp.zeros((3 * BLK, LANES), BF16)
    out = {}
    for g in range(N_KV):
        t = kv[:, (first_tile + g // 2) * LANES:(first_tile + g // 2 + 1) * LANES]
        swapped = jnp.concatenate([t[:, HD:], t[:, :HD]], axis=1)
        for p in range(2):
            out[g, p] = jnp.where(low if p == 0 else ~low, t if g % 2 == p else swapped, zero)
    return out


def _pair_products(kvx, tile_of):
    both = {g: jnp.concatenate([kvx[g, 0], kvx[g, 1]], axis=0) for g in range(N_KV)}
    out = []
    for j in range(N_HEADS // 2):
        prod = _dot_tb(both[2 * j // GROUP], tile_of(j))
        out += [prod[:3 * BLK], prod[3 * BLK:]]
    return out


def _softmax_parts(s, valid, sk):
    s = jnp.where(valid, s, NEG)
    m = jnp.maximum(jnp.max(s, axis=0, keepdims=True), sk)
    e = jnp.exp(s - m)
    es = jnp.exp(sk - m)
    return e, es, 1.0 / (jnp.sum(e, axis=0, keepdims=True) + es)


def _softmax_sink(s, valid, sk):
    e, es, inv = _softmax_parts(s, valid, sk)
    return e * inv, es * inv


def _attn_specs(T):
    nb = T // BLK
    kv_blk = 2 * N_KV * HD
    kv_col = KV_OFF // kv_blk
    q_spec = pl.BlockSpec((BLK, KV_OFF), lambda n: (n, 0))
    prev = pl.BlockSpec((BLK, kv_blk), lambda n: (jnp.maximum(n - 1, 0), kv_col))
    own = pl.BlockSpec((BLK, kv_blk), lambda n: (n, kv_col))
    nxt = pl.BlockSpec((BLK, kv_blk), lambda n: (jnp.minimum(n + 1, nb - 1), kv_col))
    return nb, q_spec, prev, own, nxt


def attn_fwd(qkv, sink, comm=None):
    T = qkv.shape[0]
    nb, q_spec, prev, own, nxt = _attn_specs(T)

    def body(sink_ref, q_ref, kp_ref, ko_ref, kn_ref, o_ref):
        valid = _attn_mask(pl.program_id(0), T)
        kv = jnp.concatenate([kp_ref[...], ko_ref[...], kn_ref[...]], axis=0)
        kx, vx = _kv_padded(kv, 0), _kv_padded(kv, 2)
        ss = _pair_products(kx, lambda j: q_ref[:, j * LANES:(j + 1) * LANES])
        es, invs = [], []
        for h in range(N_HEADS):
            e, _, inv = _softmax_parts(ss[h], valid, sink_ref[h])
            es.append(e.astype(BF16))
            invs.append(inv)
        vxt = {k: v.T for k, v in vx.items()}
        for j in range(N_HEADS // 2):
            g = 2 * j // GROUP
            o_t = _dot(vxt[g, 0], es[2 * j]) * invs[2 * j] + _dot(vxt[g, 1], es[2 * j + 1]) * invs[2 * j + 1]
            o_ref[:, j * LANES:(j + 1) * LANES] = o_t.T.astype(BF16)

    (o,), got = _call(
        body, name="attn_fwd", grid=(nb,),
        in_specs=[pl.BlockSpec(memory_space=pltpu.SMEM), q_spec, prev, own, nxt],
        out_specs=[pl.BlockSpec((BLK, D), lambda n: (n, 0))],
        out_shape=[jax.ShapeDtypeStruct((T, D), BF16)],
        semantics=("parallel",), args=(sink, qkv, qkv, qkv, qkv), comm=comm)
    return o, got


def rms_mm_gate(x, gain, w, bias, H, swiglu, act_dtype, name, comm=None):
    fused = isinstance(x, tuple)
    T = (x[0] if fused else x).shape[0]
    tm = _tile(T, 512)

    def body(*refs):
        if fused:
            a_ref, wp_ref, bp_ref, r_ref, g_ref, w_ref, b_ref, x_ref, h_ref, pre_ref, act_ref = refs
            xv = _dot(a_ref[...], wp_ref[...]) + bp_ref[...] + r_ref[...]
            x_ref[...] = xv
        else:
            x_ref, g_ref, w_ref, b_ref, h_ref, pre_ref, act_ref = refs
            xv = x_ref[...]
        h = _rms_fwd(xv, g_ref[...]).astype(BF16)
        h_ref[...] = h
        for cs in _col_chunks(H):
            cs2 = slice(H + cs.start, H + cs.stop)
            a = _dot(h, w_ref[:, cs]) + b_ref[:, cs]
            b = _dot(h, w_ref[:, cs2]) + b_ref[:, cs2]
            pre_ref[0, :, cs] = a.astype(BF16)
            pre_ref[1, :, cs] = b.astype(BF16)
            if swiglu:
                act = a * _sigmoid(a) * b
            else:
                act = a * _sigmoid(b)
            act_ref[:, cs] = act.astype(act_dtype)

    row = lambda i: (i, 0)
    full = lambda i: (0, 0)
    if fused:
        K = x[0].shape[1]
        x_specs = [pl.BlockSpec((tm, K), row), pl.BlockSpec((K, D), full, pipeline_mode=pl.Buffered(1)),
                   pl.BlockSpec((1, D), full), pl.BlockSpec((tm, D), row)]
        x_out = ([pl.BlockSpec((tm, D), row)], [jax.ShapeDtypeStruct((T, D), F32)])
        x_args = tuple(x)
    else:
        x_specs, x_out, x_args = [pl.BlockSpec((tm, D), row)], ([], []), (x,)
    outs, got = _call(
        body, name=name, grid=(T // tm,),
        in_specs=x_specs + [pl.BlockSpec((1, D), full),
                            pl.BlockSpec((D, 2 * H), full, pipeline_mode=pl.Buffered(1)), pl.BlockSpec((1, 2 * H), full)],
        out_specs=x_out[0] + [pl.BlockSpec((tm, D), row), pl.BlockSpec((2, tm, H), lambda i: (0, i, 0)),
                              pl.BlockSpec((tm, H), row)],
        out_shape=x_out[1] + [jax.ShapeDtypeStruct((T, D), BF16), jax.ShapeDtypeStruct((2, T, H), BF16),
                              jax.ShapeDtypeStruct((T, H), act_dtype)],
        semantics=("parallel",), args=x_args + (gain, w, bias), comm=comm)
    return (*outs, got)


def _conv_tiles(T):
    tt = _tile(T, 512)
    return tt, tt // SUBLANES, D // LANES


def _fill_strided(ext, p, L):
    main = p[HALO:HALO + SUBLANES * L, :].reshape(SUBLANES, L, LANES)
    ext[CONV_PAD:CONV_PAD + L] = jnp.swapaxes(main, 0, 1)

    def ibody(i, carry):
        ext[i] = p[pl.ds(i + 1, SUBLANES, stride=L), :]
        ext[i + CONV_PAD + L] = p[pl.ds(i + CONV_PAD + L + 1, SUBLANES, stride=L), :]
        return carry

    lax.fori_loop(0, CONV_PAD, ibody, 0, unroll=3)


def _conv_specs(T, tt):
    main = pl.BlockSpec((tt, D), lambda i: (i, 0))
    per = tt // HALO
    prev = pl.BlockSpec((HALO, D), lambda i: (jnp.maximum(i * per - 1, 0), 0))
    nxt = pl.BlockSpec((HALO, D), lambda i: (jnp.minimum((i + 1) * per, T // HALO - 1), 0))
    return main, prev, nxt


def _fill_pad(pad, main_ref, prev_ref, next_ref, i, n_i, tt, nlt):
    keep_p = (i > 0).astype(F32)
    keep_n = (i < n_i - 1).astype(F32)
    for lt in range(nlt):
        sl = slice(lt * LANES, (lt + 1) * LANES)
        pad[lt, 0:HALO, :] = prev_ref[:, sl] * keep_p
        pad[lt, HALO:HALO + tt, :] = main_ref[:, sl]
        pad[lt, HALO + tt:2 * HALO + tt, :] = next_ref[:, sl] * keep_n


def conv_fwd(glu, w_dw, b_dw, ln_g, ln_b, comm=None):
    T = glu.shape[0]
    tt, L, nlt = _conv_tiles(T)
    n_i = T // tt
    main, prev, nxt = _conv_specs(T, tt)

    def body(x_ref, xp_ref, xn_ref, w_ref, b_ref, g_ref, bb_ref, dwc_ref, sw_ref, pad, ob, ext, wk):
        i = pl.program_id(0)
        _fill_pad(pad, x_ref, xp_ref, xn_ref, i, n_i, tt, nlt)
        for lt in range(nlt):
            sl = slice(lt * LANES, (lt + 1) * LANES)
            o = ob.at[lt]
            _fill_strided(ext, pad.at[lt], L)
            for k in range(CONV_W):
                wk[k] = jnp.broadcast_to(w_ref[k:k + 1, sl], (SUBLANES, LANES))

            def jbody(jb, carry):
                j = jb * CONV_JB
                accs = [None] * CONV_JB
                for m in range(CONV_W + CONV_JB - 1):
                    e = ext[j + m]
                    for u in range(CONV_JB):
                        if 0 <= m - u < CONV_W:
                            t = e * wk[m - u]
                            accs[u] = t if accs[u] is None else accs[u] + t
                for u in range(CONV_JB):
                    o[pl.ds(j + u, SUBLANES, stride=L), :] = accs[u]
                return carry

            lax.fori_loop(0, L // CONV_JB, jbody, 0)
        y = jnp.concatenate([ob[lt] for lt in range(nlt)], axis=1) + b_ref[...]
        dwc_ref[...] = y
        mu = jnp.mean(y, axis=-1, keepdims=True)
        yc = y - mu
        var = jnp.mean(yc * yc, axis=-1, keepdims=True)
        z = yc * lax.rsqrt(var + EPS) * g_ref[...] + bb_ref[...]
        sw_ref[...] = (z * _sigmoid(z)).astype(BF16)

    full = lambda i: (0, 0)
    (dwc, sw), got = _call(
        body, name="conv_fwd", grid=(n_i,),
        in_specs=[main, prev, nxt, pl.BlockSpec((32, D), full), pl.BlockSpec((1, D), full),
                  pl.BlockSpec((1, D), full), pl.BlockSpec((1, D), full)],
        out_specs=[pl.BlockSpec((tt, D), lambda i: (i, 0)), pl.BlockSpec((tt, D), lambda i: (i, 0))],
        out_shape=[jax.ShapeDtypeStruct((T, D), F32), jax.ShapeDtypeStruct((T, D), BF16)],
        scratch_shapes=[pltpu.VMEM((nlt, tt + 2 * HALO, LANES), F32), pltpu.VMEM((nlt, tt, LANES), F32),
                        pltpu.VMEM((L + 2 * HALO, SUBLANES, LANES), F32), pltpu.VMEM((32, SUBLANES, LANES), F32)],
        semantics=("parallel",), args=(glu, glu, glu, w_dw, b_dw, ln_g, ln_b), comm=comm)
    return dwc, sw, got


def mm_res_loss(a, w, resid, gain, target):
    T, K = a.shape
    tm = _tile(T, 512)

    def body(a_ref, w_ref, r_ref, g_ref, t_ref, dx_ref, loss_ref, dg_ref):
        @pl.when(pl.program_id(0) == 0)
        def _():
            loss_ref[...] = jnp.zeros_like(loss_ref)
            dg_ref[...] = jnp.zeros_like(dg_ref)

        xv, gain_v = _dot(a_ref[...], w_ref[...]) + r_ref[...], g_ref[...]
        err = _rms_fwd(xv, gain_v) - t_ref[...]
        part = 0.5 * jnp.sum(jnp.mean(err * err, axis=-1, keepdims=True), axis=0, keepdims=True)
        loss_ref[...] += jnp.broadcast_to(part, loss_ref.shape)
        dx, dgr = _rms_bwd(err * (1.0 / D), xv, gain_v, 0.0)
        dx_ref[...] = dx
        dg_ref[...] += jnp.sum(dgr, axis=0, keepdims=True)

    row = lambda i: (i, 0)
    full = lambda i: (0, 0)
    return pl.pallas_call(
        body, name="ffn1_down_loss", grid=(T // tm,),
        in_specs=[pl.BlockSpec((tm, K), row), pl.BlockSpec((K, D), full), pl.BlockSpec((tm, D), row),
                  pl.BlockSpec((1, D), full), pl.BlockSpec((tm, D), row)],
        out_specs=[pl.BlockSpec((tm, D), row), pl.BlockSpec((1, LANES), full), pl.BlockSpec((1, D), full)],
        out_shape=[jax.ShapeDtypeStruct((T, D), F32), jax.ShapeDtypeStruct((1, LANES), F32),
                   jax.ShapeDtypeStruct((1, D), F32)],
        compiler_params=_params("arbitrary"),
    )(a, w, resid, gain, target)


def swiglu_bwd(dx, w_down, pre, name, comm=None):
    T = dx.shape[0]
    H = w_down.shape[0]
    tm = _tile(T, 512)

    def body(dx_ref, w_ref, pre_ref, dpre_ref):
        dxb = dx_ref[...].astype(BF16)
        for cs in _col_chunks(H):
            dact = _dot_tb(dxb, w_ref[cs, :])
            g = pre_ref[0, :, cs].astype(F32)
            u = pre_ref[1, :, cs].astype(F32)
            sg = _sigmoid(g)
            dpre_ref[0, :, cs] = (dact * u * sg * (1.0 + g * (1.0 - sg))).astype(BF16)
            dpre_ref[1, :, cs] = (dact * g * sg).astype(BF16)

    (dpre,), got = _call(
        body, name=name, grid=(T // tm,),
        in_specs=[pl.BlockSpec((tm, D), lambda i: (i, 0)),
                  pl.BlockSpec((H, D), lambda i: (0, 0), pipeline_mode=pl.Buffered(1)),
                  pl.BlockSpec((2, tm, H), lambda i: (0, i, 0))],
        out_specs=[pl.BlockSpec((2, tm, H), lambda i: (0, i, 0))],
        out_shape=[jax.ShapeDtypeStruct((2, T, H), BF16)],
        semantics=("parallel",), args=(dx, w_down, pre), comm=comm)
    return dpre, got


def _ln_silu_bwd(dsw, y, ln_g, ln_b):
    mu = jnp.mean(y, axis=-1, keepdims=True)
    yc = y - mu
    rstd = lax.rsqrt(jnp.mean(yc * yc, axis=-1, keepdims=True) + EPS)
    xhat = yc * rstd
    z = xhat * ln_g + ln_b
    sg = _sigmoid(z)
    dz = dsw * sg * (1.0 + z * (1.0 - sg))
    dxh = dz * ln_g
    dy = rstd * (dxh - jnp.mean(dxh, axis=-1, keepdims=True) - xhat * jnp.mean(dxh * xhat, axis=-1, keepdims=True))
    return dy, dz * xhat, dz


def mm_bt_rmsbwd(dpre, w, x, gain, dres, name, comm=None, proj_w=None, conv_tail=None):
    nh, T, H = dpre.shape
    tm = _tile(T, 1024 if nh * H <= 2 * D else 512)
    n_extra_in = 1 if proj_w is not None else (4 if conv_tail is not None else 0)

    def body(*refs):
        dp_ref, w_ref, x_ref, g_ref, dres_ref = refs[:5]
        extra_in = refs[5:5 + n_extra_in]
        dx_ref, dg_ref = refs[5 + n_extra_in:7 + n_extra_in]
        extra_out = refs[7 + n_extra_in:]

        @pl.when(pl.program_id(0) == 0)
        def _():
            dg_ref[...] = jnp.zeros_like(dg_ref)
            for r in extra_out[1:]:
                r[...] = jnp.zeros_like(r)

        dh = _dot_tb(dp_ref[0], w_ref[:, 0:H])
        for hf in range(1, nh):
            dh = dh + _dot_tb(dp_ref[hf], w_ref[:, hf * H:(hf + 1) * H])
        dx, dgr = _rms_bwd(dh, x_ref[...], g_ref[...], dres_ref[...])
        dx_ref[...] = dx
        dg_ref[...] += jnp.sum(dgr, axis=0, keepdims=True)
        if proj_w is not None:
            extra_out[0][...] = _dot_tb(dx.astype(BF16), extra_in[0][...]).astype(BF16)
        elif conv_tail is not None:
            wt_ref, y_ref, lg_ref, lb_ref = extra_in
            dy, dgl, dbl = _ln_silu_bwd(_dot_tb(dx.astype(BF16), wt_ref[...]), y_ref[...], lg_ref[...], lb_ref[...])
            extra_out[0][...] = dy
            extra_out[1][...] += jnp.sum(dgl, axis=0, keepdims=True)
            extra_out[2][...] += jnp.sum(dbl, axis=0, keepdims=True)
            extra_out[3][...] += jnp.sum(dx, axis=0, keepdims=True)

    row = lambda i: (i, 0)
    full = lambda i: (0, 0)
    vec = pl.BlockSpec((1, D), full)
    vec_shape = jax.ShapeDtypeStruct((1, D), F32)
    in_specs = [pl.BlockSpec((nh, tm, H), lambda i: (0, i, 0)),
                pl.BlockSpec((D, nh * H), full, pipeline_mode=pl.Buffered(1)),
                pl.BlockSpec((tm, D), row), vec, pl.BlockSpec((tm, D), row)]
    out_specs = [pl.BlockSpec((tm, D), row), vec]
    out_shape = [jax.ShapeDtypeStruct((T, D), F32), vec_shape]
    args = (dpre, w, x, gain, dres)
    if proj_w is not None:
        N = proj_w.shape[0]
        in_specs.append(pl.BlockSpec((N, D), full, pipeline_mode=pl.Buffered(1)))
        out_specs.append(pl.BlockSpec((tm, N), row))
        out_shape.append(jax.ShapeDtypeStruct((T, N), BF16))
        args += (proj_w,)
    elif conv_tail is not None:
        in_specs += [pl.BlockSpec((D, D), full, pipeline_mode=pl.Buffered(1)), pl.BlockSpec((tm, D), row), vec, vec]
        out_specs += [pl.BlockSpec((tm, D), row), vec, vec, vec]
        out_shape += [jax.ShapeDtypeStruct((T, D), F32), vec_shape, vec_shape, vec_shape]
        args += tuple(conv_tail)
    outs, got = _call(body, name=name, grid=(T // tm,), in_specs=in_specs, out_specs=out_specs, out_shape=out_shape,
                      semantics=("arbitrary",), args=args, comm=comm)
    return (*outs, got)


def dw_col(a, dpre, name):
    T = a.shape[0]
    nh, _, H = dpre.shape
    per = nh * H // N_CHIPS
    bph = N_CHIPS // nh
    tt = _tile(T, 2048)
    nt = T // tt

    def body(a_ref, b_ref, o_ref, acc):
        t = pl.program_id(1)

        @pl.when(t == 0)
        def _():
            acc[...] = jnp.zeros_like(acc)

        acc[...] += _dot_ta(a_ref[...], b_ref[...])

        @pl.when(t == nt - 1)
        def _():
            o_ref[...] = acc[...].astype(BF16)

    return pl.pallas_call(
        body, name=name, grid=(N_CHIPS, nt),
        in_specs=[pl.BlockSpec((tt, D), lambda q, t: (t, 0)),
                  pl.BlockSpec((None, tt, per), lambda q, t: (q // bph, t, q % bph))],
        out_specs=pl.BlockSpec((None, D, per), lambda q, t: (q, 0, 0)),
        out_shape=jax.ShapeDtypeStruct((N_CHIPS, D, per), BF16),
        scratch_shapes=[pltpu.VMEM((D, per), F32)],
        compiler_params=_params("parallel", "arbitrary"),
    )(a, dpre)


def dw_row(a, b, name):
    T, R = a.shape
    cw = 1408 if R % 1408 == 0 else R
    tt = _tile(T, 2048 if R <= D else 1024)
    nt = T // tt

    def body(a_ref, b_ref, o_ref, acc):
        t = pl.program_id(1)

        @pl.when(t == 0)
        def _():
            acc[...] = jnp.zeros_like(acc)

        acc[...] += _dot_ta(a_ref[...], b_ref[...].astype(BF16))

        @pl.when(t == nt - 1)
        def _():
            o_ref[...] = acc[...].astype(BF16)

    out = pl.pallas_call(
        body, name=name, grid=(R // cw, nt),
        in_specs=[pl.BlockSpec((tt, cw), lambda q, t: (t, q)), pl.BlockSpec((tt, D), lambda q, t: (t, 0))],
        out_specs=pl.BlockSpec((cw, D), lambda q, t: (q, 0)),
        out_shape=jax.ShapeDtypeStruct((R, D), BF16),
        scratch_shapes=[pltpu.VMEM((cw, D), F32)],
        compiler_params=_params("parallel", "arbitrary"),
    )(a, b)
    return out.reshape(N_CHIPS, R // N_CHIPS, D)


def conv_bwd(ddwc, glu, pre, w_dw, comm=None):
    T = ddwc.shape[0]
    tt, L, nlt = _conv_tiles(T)
    n_i = T // tt
    main, prev, nxt = _conv_specs(T, tt)

    def body(d_ref, dp_ref, dn_ref, x_ref, xp_ref, xn_ref, pre_ref, w_ref,
             dpre_ref, dw_ref, dbd_ref, dbp_ref, padd, padx, ob, extd, extx, wk):
        i = pl.program_id(0)

        @pl.when(i == 0)
        def _():
            dw_ref[...] = jnp.zeros_like(dw_ref)
            dbd_ref[...] = jnp.zeros_like(dbd_ref)
            dbp_ref[...] = jnp.zeros_like(dbp_ref)

        _fill_pad(padd, d_ref, dp_ref, dn_ref, i, n_i, tt, nlt)
        _fill_pad(padx, x_ref, xp_ref, xn_ref, i, n_i, tt, nlt)
        for lt in range(nlt):
            sl = slice(lt * LANES, (lt + 1) * LANES)
            o = ob.at[lt]
            _fill_strided(extd, padd.at[lt], L)
            _fill_strided(extx, padx.at[lt], L)
            for k in range(CONV_W):
                wk[k] = jnp.broadcast_to(w_ref[k:k + 1, sl], (SUBLANES, LANES))

            nu = CONV_JB_BWD

            def jbody(jb, accs):
                j = jb * nu
                accs = list(accs)
                d = [extd[j + u + CONV_PAD] for u in range(nu)]
                g = [None] * nu
                for m in range(CONV_W + nu - 1):
                    ed = extd[j + 2 * CONV_PAD + nu - 1 - m]
                    ex = extx[j + m]
                    for u in range(nu):
                        k = m - (nu - 1 - u)
                        if 0 <= k < CONV_W:
                            t = ed * wk[k]
                            g[u] = t if g[u] is None else g[u] + t
                        k = m - u
                        if 0 <= k < CONV_W:
                            accs[k] = accs[k] + d[u] * ex
                for u in range(nu):
                    o[pl.ds(j + u, SUBLANES, stride=L), :] = g[u]
                return tuple(accs)

            accs = lax.fori_loop(0, L // nu, jbody, tuple(jnp.zeros((SUBLANES, LANES), F32) for _ in range(CONV_W)))
            for k in range(CONV_W):
                dw_ref[k:k + 1, sl] += jnp.sum(accs[k], axis=0, keepdims=True)
        dglu = jnp.concatenate([ob[lt] for lt in range(nlt)], axis=1)
        a = pre_ref[0].astype(F32)
        gate = pre_ref[1].astype(F32)
        sg = _sigmoid(gate)
        da = dglu * sg
        dgate = dglu * a * sg * (1.0 - sg)
        dpre_ref[0] = da.astype(BF16)
        dpre_ref[1] = dgate.astype(BF16)
        dbd_ref[...] += jnp.sum(d_ref[...], axis=0, keepdims=True)
        dbp_ref[0] += jnp.sum(da, axis=0, keepdims=True)
        dbp_ref[1] += jnp.sum(dgate, axis=0, keepdims=True)

    full = lambda i: (0, 0)
    (dpre, dw, dbd, dbp), got = _call(
        body, name="conv_bwd", grid=(n_i,),
        in_specs=[main, prev, nxt, main, prev, nxt, pl.BlockSpec((2, tt, D), lambda i: (0, i, 0)),
                  pl.BlockSpec((32, D), full)],
        out_specs=[pl.BlockSpec((2, tt, D), lambda i: (0, i, 0)), pl.BlockSpec((32, D), full),
                   pl.BlockSpec((1, D), full), pl.BlockSpec((2, 1, D), lambda i: (0, 0, 0))],
        out_shape=[jax.ShapeDtypeStruct((2, T, D), BF16), jax.ShapeDtypeStruct((32, D), F32),
                   jax.ShapeDtypeStruct((1, D), F32), jax.ShapeDtypeStruct((2, 1, D), F32)],
        scratch_shapes=[pltpu.VMEM((nlt, tt + 2 * HALO, LANES), F32), pltpu.VMEM((nlt, tt + 2 * HALO, LANES), F32),
                        pltpu.VMEM((nlt, tt, LANES), F32), pltpu.VMEM((L + 2 * HALO, SUBLANES, LANES), F32),
                        pltpu.VMEM((L + 2 * HALO, SUBLANES, LANES), F32), pltpu.VMEM((32, SUBLANES, LANES), F32)],
        semantics=("arbitrary",), args=(ddwc, ddwc, ddwc, glu, glu, glu, pre, w_dw), comm=comm)
    return dpre, dw, dbd, dbp, got


def attn_bwd(qkv, o, do, sink, rc, rs1, rs2, comm=None):
    T = qkv.shape[0]
    nb, q_spec, prev, own, nxt = _attn_specs(T)
    kvw = N_KV * HD

    def body(sink_ref, q_ref, kp_ref, ko_ref, kn_ref, o_ref, do_ref, c_ref, s1_ref, s2_ref,
             dq_ref, dkc_ref, dvc_ref, dsink_ref):
        n = pl.program_id(0)

        @pl.when(n == 0)
        def _():
            dsink_ref[...] = jnp.zeros_like(dsink_ref)

        valid = _attn_mask(n, T)
        kv = jnp.concatenate([kp_ref[...], ko_ref[...], kn_ref[...]], axis=0)
        kx, vx = _kv_padded(kv, 0), _kv_padded(kv, 2)
        tile = lambda ref, j: ref[:, j * LANES:(j + 1) * LANES]
        ss = _pair_products(kx, lambda j: tile(q_ref, j))
        dps = _pair_products(vx, lambda j: tile(do_ref, j))
        low_d = lax.broadcasted_iota(jnp.int32, (LANES, BLK), 0) < HD
        deltas = []
        for j in range(N_HEADS // 2):
            prod_t = tile(do_ref, j).astype(F32).T * tile(o_ref, j).astype(F32).T
            deltas.append(jnp.sum(jnp.where(low_d, prod_t, 0.0), axis=0, keepdims=True))
            deltas.append(jnp.sum(jnp.where(low_d, 0.0, prod_t), axis=0, keepdims=True))
        lane = lax.broadcasted_iota(jnp.int32, (1, N_HEADS), 1)
        dsink = jnp.zeros((1, N_HEADS), F32)
        pbs, dss = [], []
        for h in range(N_HEADS):
            p, p_sink = _softmax_sink(ss[h], valid, sink_ref[h])
            dss.append((p * (dps[h] - deltas[h])).astype(BF16))
            pbs.append(p.astype(BF16))
            part = -jnp.sum(p_sink * deltas[h], axis=1, keepdims=True)
            dsink = dsink + jnp.where(lane == h, part, 0.0)
        dsink_ref[...] += dsink
        c, s1, s2 = c_ref[...], s1_ref[...], s2_ref[...]
        kxt = {k: v.T for k, v in kx.items()}
        for j in range(N_HEADS // 2):
            g = 2 * j // GROUP
            dq_t = _dot(kxt[g, 0], dss[2 * j]) + _dot(kxt[g, 1], dss[2 * j + 1])
            dq_ref[:, j * LANES:(j + 1) * LANES] = (_rope(dq_t.T, c, -s1, -s2) * Q_SCALE).astype(BF16)
        low_k = lax.broadcasted_iota(jnp.int32, (3 * BLK, LANES), 1) < HD
        cols = lambda xs, g, p: jnp.concatenate([xs[GROUP * g + p], xs[GROUP * g + 2 + p]], axis=1)
        for t in range(N_KV // 2):
            sums = {}
            for g in (2 * t, 2 * t + 1):
                q2 = jnp.concatenate([tile(q_ref, 2 * g), tile(q_ref, 2 * g + 1)], axis=0)
                do2 = jnp.concatenate([tile(do_ref, 2 * g), tile(do_ref, 2 * g + 1)], axis=0)
                dk2 = _dot(jnp.concatenate([cols(dss, g, 0), cols(dss, g, 1)], axis=0), q2)
                dv2 = _dot(jnp.concatenate([cols(pbs, g, 0), cols(pbs, g, 1)], axis=0), do2)
                for p in range(2):
                    sums[g, p] = (dk2[p * 3 * BLK:(p + 1) * 3 * BLK], dv2[p * 3 * BLK:(p + 1) * 3 * BLK])
            for which, ref in ((0, dkc_ref), (1, dvc_ref)):
                keep = jnp.where(low_k, sums[2 * t, 0][which], sums[2 * t + 1, 1][which])
                swap = jnp.where(low_k, sums[2 * t + 1, 0][which], sums[2 * t, 1][which])
                ref[:, t * LANES:(t + 1) * LANES] = keep + pltpu.roll(swap, HD, 1)

    row = lambda n: (n, 0)
    (dq, dkc, dvc, dsink), got = _call(
        body, name="attn_bwd", grid=(nb,),
        in_specs=[pl.BlockSpec(memory_space=pltpu.SMEM), q_spec, prev, own, nxt,
                  pl.BlockSpec((BLK, D), row), pl.BlockSpec((BLK, D), row), *_tab_specs(BLK)],
        out_specs=[pl.BlockSpec((BLK, D), row), pl.BlockSpec((None, 3 * BLK, kvw), lambda n: (n, 0, 0)),
                   pl.BlockSpec((None, 3 * BLK, kvw), lambda n: (n, 0, 0)), pl.BlockSpec((1, N_HEADS), lambda n: (0, 0))],
        out_shape=[jax.ShapeDtypeStruct((T, QKV), BF16), jax.ShapeDtypeStruct((nb, 3 * BLK, kvw), F32),
                   jax.ShapeDtypeStruct((nb, 3 * BLK, kvw), F32), jax.ShapeDtypeStruct((1, N_HEADS), F32)],
        semantics=("arbitrary",), args=(sink, qkv, qkv, qkv, qkv, o, do, rc, rs1, rs2), comm=comm)
    return dq, dkc, dvc, dsink, got


def kv_sum(dqkv, dkc, dvc, rc, rs1, rs2):
    nb = dkc.shape[0]
    T = nb * BLK
    kvw = N_KV * HD

    G = 8
    ng = nb // G

    def gather3(own_ref, prev_ref, before_ref, next_ref, after_ref, m):
        has_before = (m > 0).astype(F32)
        has_after = (m < ng - 1).astype(F32)
        out = []
        for i in range(G):
            from_prev = prev_ref[i - 1] if i > 0 else before_ref[0] * has_before
            from_next = next_ref[i + 1] if i < G - 1 else after_ref[0] * has_after
            out.append(from_prev + own_ref[i] + from_next)
        return jnp.concatenate(out, axis=0)

    def body(_, ko, kp, kb, kn, ka, vo, vp, vb, vn, va, c_ref, s1_ref, s2_ref, out_ref):
        m = pl.program_id(0)
        dk = gather3(ko, kp, kb, kn, ka, m)
        dv = gather3(vo, vp, vb, vn, va, m)
        c, s1, s2 = c_ref[...], s1_ref[...], s2_ref[...]
        for j in range(kvw // LANES):
            sl = slice(LANES * j, LANES * (j + 1))
            out_ref[:, sl] = _rope(dk[:, sl], c, -s1, -s2).astype(BF16)
        out_ref[:, kvw:] = dv.astype(BF16)

    own = pl.BlockSpec((G, BLK, kvw), lambda m: (m, 1, 0))
    prev = pl.BlockSpec((G, BLK, kvw), lambda m: (m, 2, 0))
    before = pl.BlockSpec((1, BLK, kvw), lambda m: (jnp.maximum(G * m - 1, 0), 2, 0))
    nxt = pl.BlockSpec((G, BLK, kvw), lambda m: (m, 0, 0))
    after = pl.BlockSpec((1, BLK, kvw), lambda m: (jnp.minimum(G * m + G, nb - 1), 0, 0))
    five = [own, prev, before, nxt, after]
    return pl.pallas_call(
        body, name="kv_sum", grid=(ng,),
        in_specs=[pl.BlockSpec(memory_space=pl.ANY), *five, *five, *_tab_specs(G * BLK)],
        out_specs=pl.BlockSpec((G * BLK, 2 * kvw), lambda m: (m, KV_OFF // (2 * kvw))),
        out_shape=jax.ShapeDtypeStruct((T, QKV), BF16),
        input_output_aliases={0: 0},
        compiler_params=_params("parallel"),
    )(dqkv, *([dkc] * 5), *([dvc] * 5), rc, rs1, rs2)


def _me():
    return lax.axis_index("x"), lax.axis_index("y"), lax.axis_index("c")


def _half_rows(ref, sharded_rows, chip, core):
    R, C = ref.shape[-2], ref.shape[-1]
    lead = (slice(None),) * (len(ref.shape) - 2)
    if sharded_rows:
        per = R // N_CHIPS
        return ref.at[lead + (pl.ds(chip * per + core * (per // 2), per // 2), slice(None))]
    per = C // N_CHIPS
    return ref.at[lead + (pl.ds(core * (R // 2), R // 2), pl.ds(chip * per, per))]


class _Gather:
    def __init__(self, shards, sharded_rows):
        self.inputs = list(shards)
        self.rows = list(sharded_rows)
        self.n = self.n_in = self.n_out = len(shards)
        self.out_shapes = []
        for s, rows in zip(shards, sharded_rows):
            shp = list(s.shape)
            shp[-2 if rows else -1] *= N_CHIPS
            self.out_shapes.append(jax.ShapeDtypeStruct(tuple(shp), s.dtype))
        self.scratch = [pltpu.SemaphoreType.DMA((self.n, 6)), pltpu.SemaphoreType.DMA((self.n, 6)),
                        pltpu.SemaphoreType.DMA((self.n, 2))]

    def _ctx(self, ins, outs, sems):
        send_sems, recv_sems, local_sems = sems
        x, y, c = _me()
        chips = [(1 - x, y), (x, 1 - y), (1 - x, 1 - y)]

        def half_src(w, core):
            s = ins[w]
            R = s.shape[-2]
            return s.at[pl.ds(core * (R // 2), R // 2), :]

        def dst(w, chip, core):
            return _half_rows(outs[w], self.rows[w], chip, core)

        def copy(w, k, src, chip, core, to):
            return pltpu.make_async_remote_copy(
                src_ref=src, dst_ref=dst(w, chip, core), send_sem=send_sems.at[w, k], recv_sem=recv_sems.at[w, k],
                device_id=to, device_id_type=MESH)

        def local(w, core):
            return pltpu.make_async_copy(half_src(w, core), dst(w, 2 * x + y, core), local_sems.at[w, core])

        def first(w, j):
            qx, qy = chips[j]
            return copy(w, j, half_src(w, c), 2 * x + y, c, (qx, qy, c))

        def landed(w, j):
            qx, qy = chips[j]
            return copy(w, j, dst(w, 2 * qx + qy, c), 2 * qx + qy, c, (x, y, c))

        def passed(w, j):
            qx, qy = chips[j]
            return copy(w, 3 + j, dst(w, 2 * qx + qy, c), 2 * qx + qy, c, (x, y, 1 - c))

        def from_sibling(w, j):
            qx, qy = chips[j]
            return copy(w, 3 + j, dst(w, 2 * qx + qy, 1 - c), 2 * qx + qy, 1 - c, (x, y, c))

        return local, first, landed, passed, from_sibling

    def start(self, ins, outs, sems):
        local, first, _, _, _ = self._ctx(ins, outs, sems)
        for w in range(self.n):
            for core in range(2):
                local(w, core).start()
            for j in range(3):
                first(w, j).start()

    def mid(self, ins, outs, sems):
        _, _, landed, passed, _ = self._ctx(ins, outs, sems)
        for w in range(self.n):
            for j in range(3):
                landed(w, j).wait_recv()
                passed(w, j).start()

    def end(self, ins, outs, sems):
        local, first, _, passed, from_sibling = self._ctx(ins, outs, sems)
        for w in range(self.n):
            for j in range(3):
                from_sibling(w, j).wait_recv()
        for w in range(self.n):
            for j in range(3):
                first(w, j).wait_send()
                passed(w, j).wait_send()
            for core in range(2):
                local(w, core).wait()


class _Scatter:
    def __init__(self, grads, small=None):
        self.inputs = list(grads) + ([small] if small is not None else [])
        self.ng = len(grads)
        self.n = self.n_in = self.n_out = len(self.inputs)
        self.out_shapes = [jax.ShapeDtypeStruct((N_DEV, g.shape[1] // 2, g.shape[2]), g.dtype) for g in grads]
        if small is not None:
            self.out_shapes.append(jax.ShapeDtypeStruct((N_DEV,) + small.shape, small.dtype))
        self.scratch = [pltpu.SemaphoreType.DMA((self.n, N_DEV)), pltpu.SemaphoreType.DMA((self.n, N_DEV)),
                        pltpu.SemaphoreType.DMA((self.n,))]

    def _ctx(self, ins, outs, sems):
        send_sems, recv_sems, local_sems = sems
        x, y, c = _me()
        me = 4 * x + 2 * y + c

        def piece(w, chip, core):
            if w >= self.ng:
                return ins[w]
            half = ins[w].shape[1] // 2
            return ins[w].at[chip, pl.ds(core * half, half), :]

        def peer_of(k):
            return x ^ ((k >> 2) & 1), y ^ ((k >> 1) & 1), c ^ (k & 1)

        def local(w):
            return pltpu.make_async_copy(piece(w, 2 * x + y, c), outs[w].at[me], local_sems.at[w])

        def send(w, k):
            px, py, pc = peer_of(k)
            return pltpu.make_async_remote_copy(
                src_ref=piece(w, 2 * px + py, pc), dst_ref=outs[w].at[me], send_sem=send_sems.at[w, k],
                recv_sem=recv_sems.at[w, k], device_id=(px, py, pc), device_id_type=MESH)

        def recv(w, k):
            px, py, pc = peer_of(k)
            return pltpu.make_async_remote_copy(
                src_ref=piece(w, 2 * x + y, c), dst_ref=outs[w].at[4 * px + 2 * py + pc], send_sem=send_sems.at[w, k],
                recv_sem=recv_sems.at[w, k], device_id=(px, py, pc), device_id_type=MESH)

        return local, send, recv

    def start(self, ins, outs, sems):
        local, send, _ = self._ctx(ins, outs, sems)
        for w in range(self.n):
            local(w).start()
            for k in range(1, N_DEV):
                send(w, k).start()

    def mid(self, ins, outs, sems):
        pass

    def end(self, ins, outs, sems):
        local, send, recv = self._ctx(ins, outs, sems)
        for w in range(self.n):
            for k in range(1, N_DEV):
                recv(w, k).wait_recv()
        for w in range(self.n):
            for k in range(1, N_DEV):
                send(w, k).wait_send()
            local(w).wait()


class _Both:
    def __init__(self, a, b):
        self.a, self.b = a, b
        self.inputs = a.inputs + b.inputs
        self.out_shapes = a.out_shapes + b.out_shapes
        self.scratch = a.scratch + b.scratch
        self.n_in, self.n_out = a.n_in + b.n_in, a.n_out + b.n_out

    def _split(self, ins, outs, sems):
        a, na = self.a, len(self.a.scratch)
        return (ins[:a.n_in], outs[:a.n_out], sems[:na]), (ins[a.n_in:], outs[a.n_out:], sems[na:])

    def start(self, ins, outs, sems):
        pa, pb = self._split(ins, outs, sems)
        self.a.start(*pa)
        self.b.start(*pb)

    def mid(self, ins, outs, sems):
        pa, pb = self._split(ins, outs, sems)
        self.a.mid(*pa)
        self.b.mid(*pb)

    def end(self, ins, outs, sems):
        pa, pb = self._split(ins, outs, sems)
        self.a.end(*pa)
        self.b.end(*pb)


def _call(body, *, name, grid, in_specs, out_specs, out_shape, scratch_shapes=(), semantics, args, comm=None):
    if comm is None:
        outs = pl.pallas_call(
            body, name=name, grid=grid, in_specs=in_specs, out_specs=out_specs, out_shape=out_shape,
            scratch_shapes=list(scratch_shapes), compiler_params=_params(*semantics))(*args)
        return outs, []
    n_in, n_out, n_scr = len(in_specs), len(out_specs), len(scratch_shapes)

    total = math.prod(grid)
    first, middle, last = 0, (3 * total) // 4 - 1, total - 1
    assert first <= middle < last

    def at(step):
        lin = pl.program_id(0)
        for d in range(1, len(grid)):
            lin = lin * grid[d] + pl.program_id(d)
        return lin == step

    def hosted(*refs):
        h_in, c_in = refs[:n_in], refs[n_in:n_in + comm.n_in]
        rest = refs[n_in + comm.n_in:]
        h_out, c_out = rest[:n_out], rest[n_out:n_out + comm.n_out]
        rest = rest[n_out + comm.n_out:]
        h_scr, c_scr = rest[:n_scr], rest[n_scr:]

        @pl.when(at(first))
        def _():
            comm.start(c_in, c_out, c_scr)

        body(*h_in, *h_out, *h_scr)

        @pl.when(at(middle))
        def _():
            comm.mid(c_in, c_out, c_scr)

        @pl.when(at(last))
        def _():
            comm.end(c_in, c_out, c_scr)

    any_spec = pl.BlockSpec(memory_space=pl.ANY)
    outs = pl.pallas_call(
        hosted, name=name, grid=grid, in_specs=list(in_specs) + [any_spec] * comm.n_in,
        out_specs=list(out_specs) + [any_spec] * comm.n_out, out_shape=list(out_shape) + comm.out_shapes,
        scratch_shapes=list(scratch_shapes) + comm.scratch,
        compiler_params=_params(*(["arbitrary"] * len(grid))))(*args, *comm.inputs)
    return outs[:n_out], outs[n_out:]


def sum_swap(pieces, name, comm=None):
    nl = len(pieces)
    _, r2, cc = pieces[0].shape
    tr = 256 if r2 % 256 == 0 else (128 if r2 % 128 == 0 else r2 // 2)
    n = r2 // tr

    def body(*refs):
        p_refs, out = refs[:nl], refs[nl]
        slots, send_sems, local_sems, recv_sem = refs[nl + 1:]
        x, y, c = _me()
        sibling = (x, y, 1 - c)
        l, i = pl.program_id(0), pl.program_id(1)
        step = l * n + i

        def rows(st, core):
            return out.at[st // n, pl.ds(core * r2 + (st % n) * tr, tr), :]

        def copies(st):
            slot = st % 2
            local = pltpu.make_async_copy(slots.at[slot], rows(st, c), local_sems.at[slot])
            remote = pltpu.make_async_remote_copy(
                src_ref=slots.at[slot], dst_ref=rows(st, c), send_sem=send_sems.at[slot], recv_sem=recv_sem,
                device_id=sibling, device_id_type=MESH)
            return local, remote

        for ll in range(nl):
            @pl.when(l == ll)
            def _():
                acc = p_refs[ll][0].astype(F32)
                for d in range(1, N_DEV):
                    acc = acc + p_refs[ll][d].astype(F32)
                slots[step % 2] = acc

        for cp in copies(step):
            cp.start()

        @pl.when(step >= 1)
        def _():
            local, remote = copies(step - 1)
            local.wait()
            remote.wait_send()

        @pl.when(step == nl * n - 1)
        def _():
            local, remote = copies(step)
            local.wait()
            remote.wait_send()
            theirs = out.at[:, pl.ds((1 - c) * r2, r2), :]
            pltpu.make_async_remote_copy(src_ref=theirs, dst_ref=theirs, send_sem=send_sems.at[0],
                                         recv_sem=recv_sem, device_id=sibling, device_id_type=MESH).wait_recv()

    def piece_spec(ll):
        def index(l, i):
            return (0, jnp.where(l == ll, i, jnp.where(l < ll, 0, n - 1)), 0)
        return pl.BlockSpec((N_DEV, tr, cc), index)

    (out,), got = _call(
        body, name=name, grid=(nl, n),
        in_specs=[piece_spec(ll) for ll in range(nl)],
        out_specs=[pl.BlockSpec(memory_space=pl.ANY)],
        out_shape=[jax.ShapeDtypeStruct((nl, 2 * r2, cc), F32)],
        scratch_shapes=[pltpu.VMEM((2, tr, cc), F32), pltpu.SemaphoreType.DMA((2,)), pltpu.SemaphoreType.DMA((2,)),
                        pltpu.SemaphoreType.DMA(())],
        semantics=("arbitrary", "arbitrary"), args=tuple(pieces), comm=comm)
    return (out, got) if comm is not None else out


def sum_pieces(pieces, name):
    _, R, C = pieces.shape
    tr = _tile(R, 128) if R % 128 == 0 else R

    def body(p_ref, o_ref):
        acc = p_ref[0].astype(F32)
        for d in range(1, N_DEV):
            acc = acc + p_ref[d].astype(F32)
        o_ref[...] = acc

    return pl.pallas_call(
        body, name=name, grid=(R // tr,),
        in_specs=[pl.BlockSpec((N_DEV, tr, C), lambda i: (0, i, 0))],
        out_specs=pl.BlockSpec((tr, C), lambda i: (i, 0)),
        out_shape=jax.ShapeDtypeStruct((R, C), F32),
        compiler_params=_params("parallel"),
    )(pieces)


def adamw(w, g, m, v, name):
    Lyr, R, C = w.shape
    tr = _tile(R, 256) if R % 8 == 0 else R
    c1 = 1.0 / (1.0 - ADAM_B1 ** ADAM_STEP)
    c2 = 1.0 / (1.0 - ADAM_B2 ** ADAM_STEP)

    def body(w_ref, g_ref, m_ref, v_ref, d_ref, nm_ref, nv_ref):
        gv = g_ref[...]
        nm = ADAM_B1 * m_ref[...] + (1.0 - ADAM_B1) * gv
        nv = ADAM_B2 * v_ref[...] + (1.0 - ADAM_B2) * (gv * gv)
        nm_ref[...] = nm
        nv_ref[...] = nv
        d_ref[...] = -ADAM_LR * ((nm * c1) / (jnp.sqrt(nv * c2) + ADAM_EPS) + ADAM_WD * w_ref[...])

    spec = pl.BlockSpec((None, tr, C), lambda l, i: (l, i, 0))
    shp = jax.ShapeDtypeStruct(w.shape, F32)
    return pl.pallas_call(
        body, name=name, grid=(Lyr, R // tr),
        in_specs=[spec] * 4, out_specs=[spec] * 3, out_shape=[shp] * 3,
        compiler_params=_params("parallel", "parallel"),
    )(w, g, m, v)


def _rope_tables(T):
    pos = jnp.arange(T, dtype=F32)
    inv_freq = THETA ** (-jnp.arange(0, ROT, 2, dtype=F32) / ROT)
    ang = pos[:, None] * inv_freq[None, :]
    cs = jnp.concatenate([jnp.cos(ang), jnp.sin(ang)], axis=1)
    half = ROT // 2
    lane = jnp.arange(3 * LANES)
    table, lm = lane // LANES, lane % HD
    src = jnp.where(table == 0, lm % half, half + lm % half)
    i32 = lambda b: b.astype(jnp.int32)
    sign = jnp.where(table == 0, i32(lm < ROT), jnp.where(table == 1, -i32(lm < half), i32((lm >= half) & (lm < ROT))))
    place = (jnp.arange(ROT)[:, None] == src[None, :]) * sign[None, :].astype(F32)
    ones = ((table == 0) & (lm >= ROT)).astype(F32)
    return jnp.dot(cs, place, precision=lax.Precision.HIGHEST) + ones[None, :]


def _tab_specs(rows):
    return [pl.BlockSpec((rows, LANES), lambda i, k=k: (i, k)) for k in range(3)]


def kernel(x, attn_norm, attn_w_qkv, attn_w_o, attn_sink, conv_norm, conv_w_pw1, conv_b_pw1, conv_w_dw, conv_b_dw, conv_ln_g, conv_ln_b, conv_w_pw2, conv_b_pw2, ffn_norm, ffn_w_gu, ffn_w_down, final_norm, loss_target, m_attn_norm, m_attn_w_qkv, m_attn_w_o, m_attn_sink, m_conv_norm, m_conv_w_pw1, m_conv_b_pw1, m_conv_w_dw, m_conv_b_dw, m_conv_ln_g, m_conv_ln_b, m_conv_w_pw2, m_conv_b_pw2, m_ffn_norm, m_ffn_w_gu, m_ffn_w_down, m_final_norm, v_attn_norm, v_attn_w_qkv, v_attn_w_o, v_attn_sink, v_conv_norm, v_conv_w_pw1, v_conv_b_pw1, v_conv_w_dw, v_conv_b_dw, v_conv_ln_g, v_conv_ln_b, v_conv_w_pw2, v_conv_b_pw2, v_ffn_norm, v_ffn_w_gu, v_ffn_w_down, v_final_norm):
    T = x.shape[1]
    x0 = x[0]
    target = loss_target[0]
    ix, iy = lax.axis_index("x"), lax.axis_index("y")
    chip = 2 * ix + iy
    rc = rs1 = rs2 = _rope_tables(T)

    bf = lambda t: t.astype(BF16)

    def place(vec, width):
        return lax.dynamic_update_slice(jnp.zeros((vec.shape[0], N_CHIPS * width), F32), vec, (0, chip * width))

    small_rows = jnp.concatenate([
        place(conv_norm, 256), place(conv_b_pw1, 512).reshape(2, D), place(conv_b_dw, 256), place(conv_ln_g, 256),
        place(conv_ln_b, 256), place(conv_b_pw2, 256), jnp.zeros((1, D), F32),
        place(conv_w_dw[0], 256), jnp.zeros((1, D), F32)], axis=0)

    h0, (w_qkv,) = rms_first(x0, attn_norm, comm=_Gather([bf(attn_w_qkv[0])], [False]))
    qkv, (w_o, got) = qkv_proj(h0, w_qkv, rc, rs1, rs2,
                               comm=_Both(_Gather([bf(attn_w_o[0])], [True]), _Scatter([], small_rows)))
    psmall = sum_pieces(got, "sum_small_params") * 0.5
    p_conv_norm, p_b_pw1 = psmall[0:1], psmall[1:3].reshape(1, 2 * D)
    p_b_dw, p_ln_g, p_ln_b, p_b_pw2 = psmall[3:4], psmall[4:5], psmall[5:6], psmall[6:7]
    p_w_dw = psmall[8:40]
    sink = attn_sink[0]
    o, (w_gu0,) = attn_fwd(qkv, sink, comm=_Gather([bf(ffn_w_gu[0])], [False]))
    zero_b = jnp.zeros((1, D), F32)
    zero_gu = jnp.zeros((1, 2 * DFF), F32)
    x1, h1, gu0, act0, (w_down0, w_pw1, w_pw2) = rms_mm_gate(
        (o, w_o, zero_b, x0), ffn_norm[0:1], w_gu0, zero_gu, DFF, True, BF16, "ffn0_up",
        comm=_Gather([bf(ffn_w_down[0]), bf(conv_w_pw1[0]), bf(conv_w_pw2[0])], [True, False, True]))
    x2, h2, pre, glu, (w_down1,) = rms_mm_gate((act0, w_down0, zero_b, x1), p_conv_norm, w_pw1, p_b_pw1, D, False, F32,
                                               "conv_pw1", comm=_Gather([bf(ffn_w_down[1])], [True]))
    dwc, sw, (w_gu1,) = conv_fwd(glu, p_w_dw, p_b_dw, p_ln_g, p_ln_b, comm=_Gather([bf(ffn_w_gu[1])], [False]))
    x3, h3, gu1, act1, _ = rms_mm_gate((sw, w_pw2, p_b_pw2, x2), ffn_norm[1:2], w_gu1, zero_gu, DFF, True, BF16,
                                       "ffn1_up")
    dx4, loss_part, d_final = mm_res_loss(act1, w_down1, x3, final_norm.reshape(1, D), target)

    dgu1, _ = swiglu_bwd(dx4, w_down1, gu1, "ffn1_down_bwd")
    g_down1 = dw_row(act1, dx4, "ffn1_down_dw")
    dx3, d_ffn1, ddwc, d_ln_g, d_ln_b, d_b_pw2, _ = mm_bt_rmsbwd(
        dgu1, w_gu1, x3, ffn_norm[1:2], dx4, "ffn1_up_bwd", conv_tail=(w_pw2, dwc, p_ln_g, p_ln_b))
    g_gu1 = dw_col(h3, dgu1, "ffn1_up_dw")

    g_pw2 = dw_row(sw, dx3, "conv_pw2_dw")
    dpre, d_w_dw, d_b_dw, d_b_pw1, (r_gu1, r_down1) = conv_bwd(ddwc, glu, pre, p_w_dw,
                                                               comm=_Scatter([g_gu1, g_down1]))
    dx2, d_conv_norm, _ = mm_bt_rmsbwd(dpre, w_pw1, x2, p_conv_norm, dx3, "conv_pw1_bwd")
    g_pw1 = dw_col(h2, dpre, "conv_pw1_dw")

    dgu0, (r_pw1, r_pw2) = swiglu_bwd(dx2, w_down0, gu0, "ffn0_down_bwd", comm=_Scatter([g_pw1, g_pw2]))
    g_down0 = dw_row(act0, dx2, "ffn0_down_dw")
    dx1, d_ffn0, do, _ = mm_bt_rmsbwd(dgu0, w_gu0, x1, ffn_norm[0:1], dx2, "ffn0_up_bwd", proj_w=w_o)
    g_gu0 = dw_col(h1, dgu0, "ffn0_up_dw")

    g_o = dw_row(o, dx1, "attn_out_dw")
    dq, dkc, dvc, d_sink, (r_gu0, r_down0, r_o) = attn_bwd(qkv, o, do, sink, rc, rs1, rs2,
                                                           comm=_Scatter([g_gu0, g_down0, g_o]))
    dqkv = kv_sum(dq, dkc, dvc, rc, rs1, rs2)[None]
    g_qkv = dw_col(h0, dqkv, "attn_qkv_dw")
    dx0, d_attn_norm, (r_qkv,) = mm_bt_rmsbwd(dqkv, w_qkv, x0, attn_norm, dx1, "attn_qkv_bwd",
                                              comm=_Scatter([g_qkv]))

    pad16 = lambda t: jnp.concatenate([t, jnp.zeros((1, D - t.shape[1]), F32)], axis=1)
    small_g = jnp.concatenate([
        d_attn_norm, pad16(d_sink), d_conv_norm, d_b_pw1.reshape(2, D), d_b_dw, d_ln_g, d_ln_b, d_b_pw2,
        d_ffn0, d_ffn1, d_final, pad16(loss_part), jnp.zeros((3, D), F32), d_w_dw], axis=0)
    gf_gu, (r_small,) = sum_swap([r_gu0, r_gu1], "sum_gu", comm=_Scatter([], small_g))
    gf_down = sum_swap([r_down0, r_down1], "sum_down")
    gf_pw1, gf_qkv = sum_swap([r_pw1], "sum_pw1"), sum_swap([r_qkv], "sum_qkv")
    gf_o_pw2 = sum_swap([r_o, r_pw2], "sum_o_pw2")
    gf_o, gf_pw2 = gf_o_pw2[0:1], gf_o_pw2[1:2]
    gs = sum_pieces(r_small, "sum_small_grads")
    loss = gs[12, 0]

    def take(row0, nrows, width):
        return lax.dynamic_slice(gs, (row0, chip * width), (nrows, width))

    grads = {
        "attn_norm": gs[0:1], "attn_w_qkv": gf_qkv, "attn_w_o": gf_o, "attn_sink": gs[1:2, :N_HEADS],
        "conv_norm": take(2, 1, 256), "conv_w_pw1": gf_pw1,
        "conv_b_pw1": lax.dynamic_slice(gs[3:5].reshape(1, 2 * D), (0, chip * 512), (1, 512)),
        "conv_w_dw": take(16, 32, 256)[None, :CONV_W], "conv_b_dw": take(5, 1, 256), "conv_ln_g": take(6, 1, 256),
        "conv_ln_b": take(7, 1, 256), "conv_w_pw2": gf_pw2, "conv_b_pw2": take(8, 1, 256),
        "ffn_norm": gs[9:11], "ffn_w_gu": gf_gu, "ffn_w_down": gf_down, "final_norm": gs[11],
    }
    weights = dict(attn_norm=attn_norm, attn_w_qkv=attn_w_qkv, attn_w_o=attn_w_o, attn_sink=attn_sink,
                   conv_norm=conv_norm, conv_w_pw1=conv_w_pw1, conv_b_pw1=conv_b_pw1, conv_w_dw=conv_w_dw,
                   conv_b_dw=conv_b_dw, conv_ln_g=conv_ln_g, conv_ln_b=conv_ln_b, conv_w_pw2=conv_w_pw2,
                   conv_b_pw2=conv_b_pw2, ffn_norm=ffn_norm, ffn_w_gu=ffn_w_gu, ffn_w_down=ffn_w_down,
                   final_norm=final_norm)
    m_in = dict(attn_norm=m_attn_norm, attn_w_qkv=m_attn_w_qkv, attn_w_o=m_attn_w_o, attn_sink=m_attn_sink,
                conv_norm=m_conv_norm, conv_w_pw1=m_conv_w_pw1, conv_b_pw1=m_conv_b_pw1, conv_w_dw=m_conv_w_dw,
                conv_b_dw=m_conv_b_dw, conv_ln_g=m_conv_ln_g, conv_ln_b=m_conv_ln_b, conv_w_pw2=m_conv_w_pw2,
                conv_b_pw2=m_conv_b_pw2, ffn_norm=m_ffn_norm, ffn_w_gu=m_ffn_w_gu, ffn_w_down=m_ffn_w_down,
                final_norm=m_final_norm)
    v_in = dict(attn_norm=v_attn_norm, attn_w_qkv=v_attn_w_qkv, attn_w_o=v_attn_w_o, attn_sink=v_attn_sink,
                conv_norm=v_conv_norm, conv_w_pw1=v_conv_w_pw1, conv_b_pw1=v_conv_b_pw1, conv_w_dw=v_conv_w_dw,
                conv_b_dw=v_conv_b_dw, conv_ln_g=v_conv_ln_g, conv_ln_b=v_conv_ln_b, conv_w_pw2=v_conv_w_pw2,
                conv_b_pw2=v_conv_b_pw2, ffn_norm=v_ffn_norm, ffn_w_gu=v_ffn_w_gu, ffn_w_down=v_ffn_w_down,
                final_norm=v_final_norm)
    order = list(weights)
    g_out, d_out, m_out, v_out = [], [], [], []
    for nm in order:
        w = weights[nm]
        shape = w.shape
        as3 = lambda t: t.reshape((1,) * (3 - len(shape)) + shape) if len(shape) < 3 else t.reshape(shape)
        g3 = as3(grads[nm].reshape(shape))
        delta, nm_, nv_ = adamw(as3(w), g3, as3(m_in[nm]), as3(v_in[nm]), "adamw_" + nm)
        g_out.append(g3.reshape(shape))
        d_out.append(delta.reshape(shape))
        m_out.append(nm_.reshape(shape))
        v_out.append(nv_.reshape(shape))
    return (loss, dx0[None], *g_out, *d_out, *m_out, *v_out)
```

```python
import math

import jax
import jax.numpy as jnp
from jax import lax
from jax.experimental import pallas as pl
from jax.experimental.pallas import tpu as pltpu

F32 = jnp.float32
BF16 = jnp.bfloat16

D = 1024
N_HEADS = 16
N_KV = 4
GROUP = N_HEADS // N_KV
HD = 64
ROT = 16
THETA = 500000.0
BLK = 128
QKV = (N_HEADS + 2 * N_KV) * HD
KV_OFF = N_HEADS * HD
DFF = 2816
CONV_W = 31
CONV_PAD = 15
HALO = 16
CONV_JB = 16
CONV_JB_BWD = 16
EPS = 1e-6
NEG = -1e30
N_CHIPS = 4
N_DEV = 8
LANES = 128
SUBLANES = 8

ADAM_LR, ADAM_B1, ADAM_B2, ADAM_EPS, ADAM_WD, ADAM_STEP = 0.001, 0.9, 0.999, 1e-08, 0.01, 10

VMEM_LIMIT = 56 * 1024 * 1024
MESH = pl.DeviceIdType.MESH


def _params(*sem):
    return pltpu.CompilerParams(dimension_semantics=sem, vmem_limit_bytes=VMEM_LIMIT)


def _tile(n, want):
    if n <= want:
        return n
    for t in range(want, 7, -1):
        if n % t == 0 and t % 8 == 0:
            return t
    return n


MXU_COLS = 256


def _col_chunks(n):
    return [slice(c, min(c + MXU_COLS, n)) for c in range(0, n, MXU_COLS)]


def _sigmoid(v):
    return jax.nn.sigmoid(v)


def _rms_fwd(xv, gain):
    r = lax.rsqrt(jnp.mean(xv * xv, axis=-1, keepdims=True) + EPS)
    return xv * r * gain


def _rms_bwd(dh, xv, gain, dres):
    r = lax.rsqrt(jnp.mean(xv * xv, axis=-1, keepdims=True) + EPS)
    xhat = xv * r
    gy = dh * gain
    dx = r * (gy - xhat * jnp.mean(gy * xhat, axis=-1, keepdims=True))
    return dx + dres, dh * xhat


def _rope(blk, c, s1, s2):
    return blk * c + pltpu.roll(blk, LANES - ROT // 2, 1) * s1 + pltpu.roll(blk, ROT // 2, 1) * s2


def _dot(a, b):
    return jnp.dot(a, b, preferred_element_type=F32)


def _dot_tb(a, b):
    return lax.dot_general(a, b, (((1,), (1,)), ((), ())), preferred_element_type=F32)


def _dot_ta(a, b):
    return lax.dot_general(a, b, (((0,), (0,)), ((), ())), preferred_element_type=F32)


def rms_first(x, gain, comm):
    T = x.shape[0]
    tm = _tile(T, 512)

    def body(x_ref, g_ref, h_ref):
        h_ref[...] = _rms_fwd(x_ref[...], g_ref[...]).astype(BF16)

    (h,), got = _call(
        body, name="rms_first", grid=(T // tm,),
        in_specs=[pl.BlockSpec((tm, D), lambda i: (i, 0)), pl.BlockSpec((1, D), lambda i: (0, 0))],
        out_specs=[pl.BlockSpec((tm, D), lambda i: (i, 0))], out_shape=[jax.ShapeDtypeStruct((T, D), BF16)],
        semantics=("parallel",), args=(x, gain), comm=comm)
    return h, got


def qkv_proj(h, w, rc, rs1, rs2, comm=None):
    T = h.shape[0]
    tm = _tile(T, 1024)

    def body(h_ref, w_ref, c_ref, s1_ref, s2_ref, qkv_ref):
        acc = _dot(h_ref[...], w_ref[...])
        c, s1, s2 = c_ref[...], s1_ref[...], s2_ref[...]
        n_rot = (KV_OFF + N_KV * HD) // LANES
        for j in range(n_rot):
            sl = slice(LANES * j, LANES * (j + 1))
            roped = _rope(acc[:, sl], c, s1, s2)
            if j < KV_OFF // LANES:
                roped = roped * Q_SCALE
            qkv_ref[:, sl] = roped.astype(BF16)
        qkv_ref[:, n_rot * LANES:] = acc[:, n_rot * LANES:].astype(BF16)

    row = lambda i: (i, 0)
    full = lambda i: (0, 0)
    (qkv,), got = _call(
        body, name="qkv_proj", grid=(T // tm,),
        in_specs=[pl.BlockSpec((tm, D), row), pl.BlockSpec((D, QKV), full), *_tab_specs(tm)],
        out_specs=[pl.BlockSpec((tm, QKV), row)],
        out_shape=[jax.ShapeDtypeStruct((T, QKV), BF16)],
        semantics=("parallel",), args=(h, w, rc, rs1, rs2), comm=comm)
    return qkv, got


Q_SCALE = 1.0 / math.sqrt(HD)


def _attn_mask(n, T):
    ci = lax.broadcasted_iota(jnp.int32, (3 * BLK, BLK), 0)
    qi = lax.broadcasted_iota(jnp.int32, (3 * BLK, BLK), 1)
    key_pos = n * BLK - BLK + ci
    return (jnp.abs(ci - BLK - qi) <= BLK) & (key_pos >= 0) & (key_pos < T)


def _kv_padded(kv, first_tile):
    low = lax.broadcasted_iota(jnp.int32, (3 * BLK, LANES), 1) < HD
    zero = jnp.zeros((3 * BLK, LANES), BF16)
    out = {}
    for g in range(N_KV):
        t = kv[:, (first_tile + g // 2) * LANES:(first_tile + g // 2 + 1) * LANES]
        swapped = jnp.concatenate([t[:, HD:], t[:, :HD]], axis=1)
        for p in range(2):
            out[g, p] = jnp.where(low if p == 0 else ~low, t if g % 2 == p else swapped, zero)
    return out


def _pair_products(kvx, tile_of):
    both = {g: jnp.concatenate([kvx[g, 0], kvx[g, 1]], axis=0) for g in range(N_KV)}
    out = []
    for j in range(N_HEADS // 2):
        prod = _dot_tb(both[2 * j // GROUP], tile_of(j))
        out += [prod[:3 * BLK], prod[3 * BLK:]]
    return out


def _softmax_parts(s, valid, sk):
    s = jnp.where(valid, s, NEG)
    m = jnp.maximum(jnp.max(s, axis=0, keepdims=True), sk)
    e = jnp.exp(s - m)
    es = jnp.exp(sk - m)
    return e, es, 1.0 / (jnp.sum(e, axis=0, keepdims=True) + es)


def _softmax_sink(s, valid, sk):
    e, es, inv = _softmax_parts(s, valid, sk)
    return e * inv, es * inv


def _attn_specs(T):
    nb = T // BLK
    kv_blk = 2 * N_KV * HD
    kv_col = KV_OFF // kv_blk
    q_spec = pl.BlockSpec((BLK, KV_OFF), lambda n: (n, 0))
    prev = pl.BlockSpec((BLK, kv_blk), lambda n: (jnp.maximum(n - 1, 0), kv_col))
    own = pl.BlockSpec((BLK, kv_blk), lambda n: (n, kv_col))
    nxt = pl.BlockSpec((BLK, kv_blk), lambda n: (jnp.minimum(n + 1, nb - 1), kv_col))
    return nb, q_spec, prev, own, nxt


def attn_fwd(qkv, sink, comm=None):
    T = qkv.shape[0]
    nb, q_spec, prev, own, nxt = _attn_specs(T)

    def body(sink_ref, q_ref, kp_ref, ko_ref, kn_ref, o_ref):
        valid = _attn_mask(pl.program_id(0), T)
        kv = jnp.concatenate([kp_ref[...], ko_ref[...], kn_ref[...]], axis=0)
        kx, vx = _kv_padded(kv, 0), _kv_padded(kv, 2)
        ss = _pair_products(kx, lambda j: q_ref[:, j * LANES:(j + 1) * LANES])
        es, invs = [], []
        for h in range(N_HEADS):
            e, _, inv = _softmax_parts(ss[h], valid, sink_ref[h])
            es.append(e.astype(BF16))
            invs.append(inv)
        vxt = {k: v.T for k, v in vx.items()}
        for j in range(N_HEADS // 2):
            g = 2 * j // GROUP
            o_t = _dot(vxt[g, 0], es[2 * j]) * invs[2 * j] + _dot(vxt[g, 1], es[2 * j + 1]) * invs[2 * j + 1]
            o_ref[:, j * LANES:(j + 1) * LANES] = o_t.T.astype(BF16)

    (o,), got = _call(
        body, name="attn_fwd", grid=(nb,),
        in_specs=[pl.BlockSpec(memory_space=pltpu.SMEM), q_spec, prev, own, nxt],
        out_specs=[pl.BlockSpec((BLK, D), lambda n: (n, 0))],
        out_shape=[jax.ShapeDtypeStruct((T, D), BF16)],
        semantics=("parallel",), args=(sink, qkv, qkv, qkv, qkv), comm=comm)
    return o, got


def rms_mm_gate(x, gain, w, bias, H, swiglu, act_dtype, name, comm=None):
    fused = isinstance(x, tuple)
    T = (x[0] if fused else x).shape[0]
    tm = _tile(T, 512)

    def body(*refs):
        if fused:
            a_ref, wp_ref, bp_ref, r_ref, g_ref, w_ref, b_ref, x_ref, h_ref, pre_ref, act_ref = refs
            xv = _dot(a_ref[...], wp_ref[...]) + bp_ref[...] + r_ref[...]
            x_ref[...] = xv
        else:
            x_ref, g_ref, w_ref, b_ref, h_ref, pre_ref, act_ref = refs
            xv = x_ref[...]
        h = _rms_fwd(xv, g_ref[...]).astype(BF16)
        h_ref[...] = h
        for cs in _col_chunks(H):
            cs2 = slice(H + cs.start, H + cs.stop)
            a = _dot(h, w_ref[:, cs]) + b_ref[:, cs]
            b = _dot(h, w_ref[:, cs2]) + b_ref[:, cs2]
            pre_ref[0, :, cs] = a.astype(BF16)
            pre_ref[1, :, cs] = b.astype(BF16)
            if swiglu:
                act = a * _sigmoid(a) * b
            else:
                act = a * _sigmoid(b)
            act_ref[:, cs] = act.astype(act_dtype)

    row = lambda i: (i, 0)
    full = lambda i: (0, 0)
    if fused:
        K = x[0].shape[1]
        x_specs = [pl.BlockSpec((tm, K), row), pl.BlockSpec((K, D), full, pipeline_mode=pl.Buffered(1)),
                   pl.BlockSpec((1, D), full), pl.BlockSpec((tm, D), row)]
        x_out = ([pl.BlockSpec((tm, D), row)], [jax.ShapeDtypeStruct((T, D), F32)])
        x_args = tuple(x)
    else:
        x_specs, x_out, x_args = [pl.BlockSpec((tm, D), row)], ([], []), (x,)
    outs, got = _call(
        body, name=name, grid=(T // tm,),
        in_specs=x_specs + [pl.BlockSpec((1, D), full),
                            pl.BlockSpec((D, 2 * H), full, pipeline_mode=pl.Buffered(1)), pl.BlockSpec((1, 2 * H), full)],
        out_specs=x_out[0] + [pl.BlockSpec((tm, D), row), pl.BlockSpec((2, tm, H), lambda i: (0, i, 0)),
                              pl.BlockSpec((tm, H), row)],
        out_shape=x_out[1] + [jax.ShapeDtypeStruct((T, D), BF16), jax.ShapeDtypeStruct((2, T, H), BF16),
                              jax.ShapeDtypeStruct((T, H), act_dtype)],
        semantics=("parallel",), args=x_args + (gain, w, bias), comm=comm)
    return (*outs, got)


def _conv_tiles(T, rows=512):
    tt = _tile(T, rows)
    return tt, tt // SUBLANES, D // LANES


def _fill_strided(ext, p, L):
    main = p[HALO:HALO + SUBLANES * L, :].reshape(SUBLANES, L, LANES)
    ext[CONV_PAD:CONV_PAD + L] = jnp.swapaxes(main, 0, 1)

    def ibody(i, carry):
        ext[i] = p[pl.ds(i + 1, SUBLANES, stride=L), :]
        ext[i + CONV_PAD + L] = p[pl.ds(i + CONV_PAD + L + 1, SUBLANES, stride=L), :]
        return carry

    lax.fori_loop(0, CONV_PAD, ibody, 0, unroll=3)


def _conv_specs(T, tt):
    main = pl.BlockSpec((tt, D), lambda i: (i, 0))
    per = tt // HALO
    prev = pl.BlockSpec((HALO, D), lambda i: (jnp.maximum(i * per - 1, 0), 0))
    nxt = pl.BlockSpec((HALO, D), lambda i: (jnp.minimum((i + 1) * per, T // HALO - 1), 0))
    return main, prev, nxt


def _fill_pad(pad, main_ref, prev_ref, next_ref, i, n_i, tt, nlt):
    keep_p = (i > 0).astype(F32)
    keep_n = (i < n_i - 1).astype(F32)
    for lt in range(nlt):
        sl = slice(lt * LANES, (lt + 1) * LANES)
        pad[lt, 0:HALO, :] = prev_ref[:, sl] * keep_p
        pad[lt, HALO:HALO + tt, :] = main_ref[:, sl]
        pad[lt, HALO + tt:2 * HALO + tt, :] = next_ref[:, sl] * keep_n


def conv_fwd(glu, w_dw, b_dw, ln_g, ln_b, comm=None):
    T = glu.shape[0]
    tt, L, nlt = _conv_tiles(T, 1024)
    n_i = T // tt
    main, prev, nxt = _conv_specs(T, tt)

    def body(x_ref, xp_ref, xn_ref, w_ref, b_ref, g_ref, bb_ref, dwc_ref, sw_ref, pad, ob, ext, wk):
        i = pl.program_id(0)
        _fill_pad(pad, x_ref, xp_ref, xn_ref, i, n_i, tt, nlt)
        for lt in range(nlt):
            sl = slice(lt * LANES, (lt + 1) * LANES)
            o = ob.at[lt]
            _fill_strided(ext, pad.at[lt], L)
            for k in range(CONV_W):
                wk[k] = jnp.broadcast_to(w_ref[k:k + 1, sl], (SUBLANES, LANES))

            def jbody(jb, carry):
                j = jb * CONV_JB
                accs = [None] * CONV_JB
                for m in range(CONV_W + CONV_JB - 1):
                    e = ext[j + m]
                    for u in range(CONV_JB):
                        if 0 <= m - u < CONV_W:
                            t = e * wk[m - u]
                            accs[u] = t if accs[u] is None else accs[u] + t
                for u in range(CONV_JB):
                    o[pl.ds(j + u, SUBLANES, stride=L), :] = accs[u]
                return carry

            lax.fori_loop(0, L // CONV_JB, jbody, 0)
        y = jnp.concatenate([ob[lt] for lt in range(nlt)], axis=1) + b_ref[...]
        dwc_ref[...] = y
        mu = jnp.mean(y, axis=-1, keepdims=True)
        yc = y - mu
        var = jnp.mean(yc * yc, axis=-1, keepdims=True)
        z = yc * lax.rsqrt(var + EPS) * g_ref[...] + bb_ref[...]
        sw_ref[...] = (z * _sigmoid(z)).astype(BF16)

    full = lambda i: (0, 0)
    (dwc, sw), got = _call(
        body, name="conv_fwd", grid=(n_i,),
        in_specs=[main, prev, nxt, pl.BlockSpec((32, D), full), pl.BlockSpec((1, D), full),
                  pl.BlockSpec((1, D), full), pl.BlockSpec((1, D), full)],
        out_specs=[pl.BlockSpec((tt, D), lambda i: (i, 0)), pl.BlockSpec((tt, D), lambda i: (i, 0))],
        out_shape=[jax.ShapeDtypeStruct((T, D), F32), jax.ShapeDtypeStruct((T, D), BF16)],
        scratch_shapes=[pltpu.VMEM((nlt, tt + 2 * HALO, LANES), F32), pltpu.VMEM((nlt, tt, LANES), F32),
                        pltpu.VMEM((L + 2 * HALO, SUBLANES, LANES), F32), pltpu.VMEM((32, SUBLANES, LANES), F32)],
        semantics=("parallel",), args=(glu, glu, glu, w_dw, b_dw, ln_g, ln_b), comm=comm)
    return dwc, sw, got


def mm_res_loss(a, w, resid, gain, target):
    T, K = a.shape
    tm = _tile(T, 512)

    def body(a_ref, w_ref, r_ref, g_ref, t_ref, dx_ref, loss_ref, dg_ref):
        @pl.when(pl.program_id(0) == 0)
        def _():
            loss_ref[...] = jnp.zeros_like(loss_ref)
            dg_ref[...] = jnp.zeros_like(dg_ref)

        xv, gain_v = _dot(a_ref[...], w_ref[...]) + r_ref[...], g_ref[...]
        err = _rms_fwd(xv, gain_v) - t_ref[...]
        part = 0.5 * jnp.sum(jnp.mean(err * err, axis=-1, keepdims=True), axis=0, keepdims=True)
        loss_ref[...] += jnp.broadcast_to(part, loss_ref.shape)
        dx, dgr = _rms_bwd(err * (1.0 / D), xv, gain_v, 0.0)
        dx_ref[...] = dx
        dg_ref[...] += jnp.sum(dgr, axis=0, keepdims=True)

    row = lambda i: (i, 0)
    full = lambda i: (0, 0)
    return pl.pallas_call(
        body, name="ffn1_down_loss", grid=(T // tm,),
        in_specs=[pl.BlockSpec((tm, K), row), pl.BlockSpec((K, D), full), pl.BlockSpec((tm, D), row),
                  pl.BlockSpec((1, D), full), pl.BlockSpec((tm, D), row)],
        out_specs=[pl.BlockSpec((tm, D), row), pl.BlockSpec((1, LANES), full), pl.BlockSpec((1, D), full)],
        out_shape=[jax.ShapeDtypeStruct((T, D), F32), jax.ShapeDtypeStruct((1, LANES), F32),
                   jax.ShapeDtypeStruct((1, D), F32)],
        compiler_params=_params("arbitrary"),
    )(a, w, resid, gain, target)


def swiglu_bwd(dx, w_down, pre, name, comm=None):
    T = dx.shape[0]
    H = w_down.shape[0]
    tm = _tile(T, 512)

    def body(dx_ref, w_ref, pre_ref, dpre_ref):
        dxb = dx_ref[...].astype(BF16)
        for cs in _col_chunks(H):
            dact = _dot_tb(dxb, w_ref[cs, :])
            g = pre_ref[0, :, cs].astype(F32)
            u = pre_ref[1, :, cs].astype(F32)
            sg = _sigmoid(g)
            dpre_ref[0, :, cs] = (dact * u * sg * (1.0 + g * (1.0 - sg))).astype(BF16)
            dpre_ref[1, :, cs] = (dact * g * sg).astype(BF16)

    (dpre,), got = _call(
        body, name=name, grid=(T // tm,),
        in_specs=[pl.BlockSpec((tm, D), lambda i: (i, 0)),
                  pl.BlockSpec((H, D), lambda i: (0, 0), pipeline_mode=pl.Buffered(1)),
                  pl.BlockSpec((2, tm, H), lambda i: (0, i, 0))],
        out_specs=[pl.BlockSpec((2, tm, H), lambda i: (0, i, 0))],
        out_shape=[jax.ShapeDtypeStruct((2, T, H), BF16)],
        semantics=("parallel",), args=(dx, w_down, pre), comm=comm)
    return dpre, got


def _ln_silu_bwd(dsw, y, ln_g, ln_b):
    mu = jnp.mean(y, axis=-1, keepdims=True)
    yc = y - mu
    rstd = lax.rsqrt(jnp.mean(yc * yc, axis=-1, keepdims=True) + EPS)
    xhat = yc * rstd
    z = xhat * ln_g + ln_b
    sg = _sigmoid(z)
    dz = dsw * sg * (1.0 + z * (1.0 - sg))
    dxh = dz * ln_g
    dy = rstd * (dxh - jnp.mean(dxh, axis=-1, keepdims=True) - xhat * jnp.mean(dxh * xhat, axis=-1, keepdims=True))
    return dy, dz * xhat, dz


def mm_bt_rmsbwd(dpre, w, x, gain, dres, name, comm=None, proj_w=None, conv_tail=None):
    nh, T, H = dpre.shape
    tm = _tile(T, 1024 if nh * H <= 2 * D else 512)
    n_extra_in = 1 if proj_w is not None else (4 if conv_tail is not None else 0)

    def body(*refs):
        dp_ref, w_ref, x_ref, g_ref, dres_ref = refs[:5]
        extra_in = refs[5:5 + n_extra_in]
        dx_ref, dg_ref = refs[5 + n_extra_in:7 + n_extra_in]
        extra_out = refs[7 + n_extra_in:]

        @pl.when(pl.program_id(0) == 0)
        def _():
            dg_ref[...] = jnp.zeros_like(dg_ref)
            for r in extra_out[1:]:
                r[...] = jnp.zeros_like(r)

        dh = _dot_tb(dp_ref[0], w_ref[:, 0:H])
        for hf in range(1, nh):
            dh = dh + _dot_tb(dp_ref[hf], w_ref[:, hf * H:(hf + 1) * H])
        dx, dgr = _rms_bwd(dh, x_ref[...], g_ref[...], dres_ref[...])
        dx_ref[...] = dx
        dg_ref[...] += jnp.sum(dgr, axis=0, keepdims=True)
        if proj_w is not None:
            extra_out[0][...] = _dot_tb(dx.astype(BF16), extra_in[0][...]).astype(BF16)
        elif conv_tail is not None:
            wt_ref, y_ref, lg_ref, lb_ref = extra_in
            dy, dgl, dbl = _ln_silu_bwd(_dot_tb(dx.astype(BF16), wt_ref[...]), y_ref[...], lg_ref[...], lb_ref[...])
            extra_out[0][...] = dy
            extra_out[1][...] += jnp.sum(dgl, axis=0, keepdims=True)
            extra_out[2][...] += jnp.sum(dbl, axis=0, keepdims=True)
            extra_out[3][...] += jnp.sum(dx, axis=0, keepdims=True)

    row = lambda i: (i, 0)
    full = lambda i: (0, 0)
    vec = pl.BlockSpec((1, D), full)
    vec_shape = jax.ShapeDtypeStruct((1, D), F32)
    in_specs = [pl.BlockSpec((nh, tm, H), lambda i: (0, i, 0)),
                pl.BlockSpec((D, nh * H), full, pipeline_mode=pl.Buffered(1)),
                pl.BlockSpec((tm, D), row), vec, pl.BlockSpec((tm, D), row)]
    out_specs = [pl.BlockSpec((tm, D), row), vec]
    out_shape = [jax.ShapeDtypeStruct((T, D), F32), vec_shape]
    args = (dpre, w, x, gain, dres)
    if proj_w is not None:
        N = proj_w.shape[0]
        in_specs.append(pl.BlockSpec((N, D), full, pipeline_mode=pl.Buffered(1)))
        out_specs.append(pl.BlockSpec((tm, N), row))
        out_shape.append(jax.ShapeDtypeStruct((T, N), BF16))
        args += (proj_w,)
    elif conv_tail is not None:
        in_specs += [pl.BlockSpec((D, D), full, pipeline_mode=pl.Buffered(1)), pl.BlockSpec((tm, D), row), vec, vec]
        out_specs += [pl.BlockSpec((tm, D), row), vec, vec, vec]
        out_shape += [jax.ShapeDtypeStruct((T, D), F32), vec_shape, vec_shape, vec_shape]
        args += tuple(conv_tail)
    outs, got = _call(body, name=name, grid=(T // tm,), in_specs=in_specs, out_specs=out_specs, out_shape=out_shape,
                      semantics=("arbitrary",), args=args, comm=comm)
    return (*outs, got)


def dw_col(a, dpre, name):
    T = a.shape[0]
    nh, _, H = dpre.shape
    per = nh * H // N_CHIPS
    bph = N_CHIPS // nh
    tt = _tile(T, 2048)
    nt = T // tt

    def body(a_ref, b_ref, o_ref, acc):
        t = pl.program_id(1)

        @pl.when(t == 0)
        def _():
            acc[...] = jnp.zeros_like(acc)

        acc[...] += _dot_ta(a_ref[...], b_ref[...])

        @pl.when(t == nt - 1)
        def _():
            o_ref[...] = acc[...].astype(BF16)

    return pl.pallas_call(
        body, name=name, grid=(N_CHIPS, nt),
        in_specs=[pl.BlockSpec((tt, D), lambda q, t: (t, 0)),
                  pl.BlockSpec((None, tt, per), lambda q, t: (q // bph, t, q % bph))],
        out_specs=pl.BlockSpec((None, D, per), lambda q, t: (q, 0, 0)),
        out_shape=jax.ShapeDtypeStruct((N_CHIPS, D, per), BF16),
        scratch_shapes=[pltpu.VMEM((D, per), F32)],
        compiler_params=_params("parallel", "arbitrary"),
    )(a, dpre)


def dw_row(a, b, name):
    T, R = a.shape
    cw = 1408 if R % 1408 == 0 else R
    tt = _tile(T, 2048 if R <= D else 1024)
    nt = T // tt

    def body(a_ref, b_ref, o_ref, acc):
        t = pl.program_id(1)

        @pl.when(t == 0)
        def _():
            acc[...] = jnp.zeros_like(acc)

        acc[...] += _dot_ta(a_ref[...], b_ref[...].astype(BF16))

        @pl.when(t == nt - 1)
        def _():
            o_ref[...] = acc[...].astype(BF16)

    out = pl.pallas_call(
        body, name=name, grid=(R // cw, nt),
        in_specs=[pl.BlockSpec((tt, cw), lambda q, t: (t, q)), pl.BlockSpec((tt, D), lambda q, t: (t, 0))],
        out_specs=pl.BlockSpec((cw, D), lambda q, t: (q, 0)),
        out_shape=jax.ShapeDtypeStruct((R, D), BF16),
        scratch_shapes=[pltpu.VMEM((cw, D), F32)],
        compiler_params=_params("parallel", "arbitrary"),
    )(a, b)
    return out.reshape(N_CHIPS, R // N_CHIPS, D)


def conv_bwd(ddwc, glu, pre, w_dw, comm=None):
    T = ddwc.shape[0]
    tt, L, nlt = _conv_tiles(T)
    n_i = T // tt
    main, prev, nxt = _conv_specs(T, tt)

    def body(d_ref, dp_ref, dn_ref, x_ref, xp_ref, xn_ref, pre_ref, w_ref,
             dpre_ref, dw_ref, dbd_ref, dbp_ref, padd, padx, ob, extd, extx, wk):
        i = pl.program_id(0)

        @pl.when(i == 0)
        def _():
            dw_ref[...] = jnp.zeros_like(dw_ref)
            dbd_ref[...] = jnp.zeros_like(dbd_ref)
            dbp_ref[...] = jnp.zeros_like(dbp_ref)

        _fill_pad(padd, d_ref, dp_ref, dn_ref, i, n_i, tt, nlt)
        _fill_pad(padx, x_ref, xp_ref, xn_ref, i, n_i, tt, nlt)
        for lt in range(nlt):
            sl = slice(lt * LANES, (lt + 1) * LANES)
            o = ob.at[lt]
            _fill_strided(extd, padd.at[lt], L)
            _fill_strided(extx, padx.at[lt], L)
            for k in range(CONV_W):
                wk[k] = jnp.broadcast_to(w_ref[k:k + 1, sl], (SUBLANES, LANES))

            nu = CONV_JB_BWD

            def jbody(jb, accs):
                j = jb * nu
                accs = list(accs)
                d = [extd[j + u + CONV_PAD] for u in range(nu)]
                g = [None] * nu
                for m in range(CONV_W + nu - 1):
                    ed = extd[j + 2 * CONV_PAD + nu - 1 - m]
                    ex = extx[j + m]
                    for u in range(nu):
                        k = m - (nu - 1 - u)
                        if 0 <= k < CONV_W:
                            t = ed * wk[k]
                            g[u] = t if g[u] is None else g[u] + t
                        k = m - u
                        if 0 <= k < CONV_W:
                            accs[k] = accs[k] + d[u] * ex
                for u in range(nu):
                    o[pl.ds(j + u, SUBLANES, stride=L), :] = g[u]
                return tuple(accs)

            accs = lax.fori_loop(0, L // nu, jbody, tuple(jnp.zeros((SUBLANES, LANES), F32) for _ in range(CONV_W)))
            for k in range(CONV_W):
                dw_ref[k:k + 1, sl] += jnp.sum(accs[k], axis=0, keepdims=True)
        dglu = jnp.concatenate([ob[lt] for lt in range(nlt)], axis=1)
        a = pre_ref[0].astype(F32)
        gate = pre_ref[1].astype(F32)
        sg = _sigmoid(gate)
        da = dglu * sg
        dgate = dglu * a * sg * (1.0 - sg)
        dpre_ref[0] = da.astype(BF16)
        dpre_ref[1] = dgate.astype(BF16)
        dbd_ref[...] += jnp.sum(d_ref[...], axis=0, keepdims=True)
        dbp_ref[0] += jnp.sum(da, axis=0, keepdims=True)
        dbp_ref[1] += jnp.sum(dgate, axis=0, keepdims=True)

    full = lambda i: (0, 0)
    (dpre, dw, dbd, dbp), got = _call(
        body, name="conv_bwd", grid=(n_i,),
        in_specs=[main, prev, nxt, main, prev, nxt, pl.BlockSpec((2, tt, D), lambda i: (0, i, 0)),
                  pl.BlockSpec((32, D), full)],
        out_specs=[pl.BlockSpec((2, tt, D), lambda i: (0, i, 0)), pl.BlockSpec((32, D), full),
                   pl.BlockSpec((1, D), full), pl.BlockSpec((2, 1, D), lambda i: (0, 0, 0))],
        out_shape=[jax.ShapeDtypeStruct((2, T, D), BF16), jax.ShapeDtypeStruct((32, D), F32),
                   jax.ShapeDtypeStruct((1, D), F32), jax.ShapeDtypeStruct((2, 1, D), F32)],
        scratch_shapes=[pltpu.VMEM((nlt, tt + 2 * HALO, LANES), F32), pltpu.VMEM((nlt, tt + 2 * HALO, LANES), F32),
                        pltpu.VMEM((nlt, tt, LANES), F32), pltpu.VMEM((L + 2 * HALO, SUBLANES, LANES), F32),
                        pltpu.VMEM((L + 2 * HALO, SUBLANES, LANES), F32), pltpu.VMEM((32, SUBLANES, LANES), F32)],
        semantics=("arbitrary",), args=(ddwc, ddwc, ddwc, glu, glu, glu, pre, w_dw), comm=comm)
    return dpre, dw, dbd, dbp, got


def attn_bwd(qkv, o, do, sink, rc, rs1, rs2, comm=None):
    T = qkv.shape[0]
    nb, q_spec, prev, own, nxt = _attn_specs(T)
    kvw = N_KV * HD

    def body(sink_ref, q_ref, kp_ref, ko_ref, kn_ref, o_ref, do_ref, c_ref, s1_ref, s2_ref,
             dq_ref, dkc_ref, dvc_ref, dsink_ref):
        n = pl.program_id(0)

        @pl.when(n == 0)
        def _():
            dsink_ref[...] = jnp.zeros_like(dsink_ref)

        valid = _attn_mask(n, T)
        kv = jnp.concatenate([kp_ref[...], ko_ref[...], kn_ref[...]], axis=0)
        kx, vx = _kv_padded(kv, 0), _kv_padded(kv, 2)
        tile = lambda ref, j: ref[:, j * LANES:(j + 1) * LANES]
        ss = _pair_products(kx, lambda j: tile(q_ref, j))
        dps = _pair_products(vx, lambda j: tile(do_ref, j))
        low_d = lax.broadcasted_iota(jnp.int32, (LANES, BLK), 0) < HD
        deltas = []
        for j in range(N_HEADS // 2):
            prod_t = tile(do_ref, j).astype(F32).T * tile(o_ref, j).astype(F32).T
            deltas.append(jnp.sum(jnp.where(low_d, prod_t, 0.0), axis=0, keepdims=True))
            deltas.append(jnp.sum(jnp.where(low_d, 0.0, prod_t), axis=0, keepdims=True))
        lane = lax.broadcasted_iota(jnp.int32, (1, N_HEADS), 1)
        dsink = jnp.zeros((1, N_HEADS), F32)
        pbs, dss = [], []
        for h in range(N_HEADS):
            p, p_sink = _softmax_sink(ss[h], valid, sink_ref[h])
            dss.append((p * (dps[h] - deltas[h])).astype(BF16))
            pbs.append(p.astype(BF16))
            part = -jnp.sum(p_sink * deltas[h], axis=1, keepdims=True)
            dsink = dsink + jnp.where(lane == h, part, 0.0)
        dsink_ref[...] += dsink
        c, s1, s2 = c_ref[...], s1_ref[...], s2_ref[...]
        kxt = {k: v.T for k, v in kx.items()}
        for j in range(N_HEADS // 2):
            g = 2 * j // GROUP
            dq_t = _dot(kxt[g, 0], dss[2 * j]) + _dot(kxt[g, 1], dss[2 * j + 1])
            dq_ref[:, j * LANES:(j + 1) * LANES] = (_rope(dq_t.T, c, -s1, -s2) * Q_SCALE).astype(BF16)
        low_k = lax.broadcasted_iota(jnp.int32, (3 * BLK, LANES), 1) < HD
        cols = lambda xs, g, p: jnp.concatenate([xs[GROUP * g + p], xs[GROUP * g + 2 + p]], axis=1)
        for t in range(N_KV // 2):
            sums = {}
            for g in (2 * t, 2 * t + 1):
                q2 = jnp.concatenate([tile(q_ref, 2 * g), tile(q_ref, 2 * g + 1)], axis=0)
                do2 = jnp.concatenate([tile(do_ref, 2 * g), tile(do_ref, 2 * g + 1)], axis=0)
                dk2 = _dot(jnp.concatenate([cols(dss, g, 0), cols(dss, g, 1)], axis=0), q2)
                dv2 = _dot(jnp.concatenate([cols(pbs, g, 0), cols(pbs, g, 1)], axis=0), do2)
                for p in range(2):
                    sums[g, p] = (dk2[p * 3 * BLK:(p + 1) * 3 * BLK], dv2[p * 3 * BLK:(p + 1) * 3 * BLK])
            for which, ref in ((0, dkc_ref), (1, dvc_ref)):
                keep = jnp.where(low_k, sums[2 * t, 0][which], sums[2 * t + 1, 1][which])
                swap = jnp.where(low_k, sums[2 * t + 1, 0][which], sums[2 * t, 1][which])
                ref[:, t * LANES:(t + 1) * LANES] = keep + pltpu.roll(swap, HD, 1)

    row = lambda n: (n, 0)
    (dq, dkc, dvc, dsink), got = _call(
        body, name="attn_bwd", grid=(nb,),
        in_specs=[pl.BlockSpec(memory_space=pltpu.SMEM), q_spec, prev, own, nxt,
                  pl.BlockSpec((BLK, D), row), pl.BlockSpec((BLK, D), row), *_tab_specs(BLK)],
        out_specs=[pl.BlockSpec((BLK, D), row), pl.BlockSpec((None, 3 * BLK, kvw), lambda n: (n, 0, 0)),
                   pl.BlockSpec((None, 3 * BLK, kvw), lambda n: (n, 0, 0)), pl.BlockSpec((1, N_HEADS), lambda n: (0, 0))],
        out_shape=[jax.ShapeDtypeStruct((T, QKV), BF16), jax.ShapeDtypeStruct((nb, 3 * BLK, kvw), F32),
                   jax.ShapeDtypeStruct((nb, 3 * BLK, kvw), F32), jax.ShapeDtypeStruct((1, N_HEADS), F32)],
        semantics=("arbitrary",), args=(sink, qkv, qkv, qkv, qkv, o, do, rc, rs1, rs2), comm=comm)
    return dq, dkc, dvc, dsink, got


def kv_sum(dqkv, dkc, dvc, rc, rs1, rs2):
    nb = dkc.shape[0]
    T = nb * BLK
    kvw = N_KV * HD

    G = 8
    ng = nb // G

    def gather3(own_ref, prev_ref, before_ref, next_ref, after_ref, m):
        has_before = (m > 0).astype(F32)
        has_after = (m < ng - 1).astype(F32)
        out = []
        for i in range(G):
            from_prev = prev_ref[i - 1] if i > 0 else before_ref[0] * has_before
            from_next = next_ref[i + 1] if i < G - 1 else after_ref[0] * has_after
            out.append(from_prev + own_ref[i] + from_next)
        return jnp.concatenate(out, axis=0)

    def body(_, ko, kp, kb, kn, ka, vo, vp, vb, vn, va, c_ref, s1_ref, s2_ref, out_ref):
        m = pl.program_id(0)
        dk = gather3(ko, kp, kb, kn, ka, m)
        dv = gather3(vo, vp, vb, vn, va, m)
        c, s1, s2 = c_ref[...], s1_ref[...], s2_ref[...]
        for j in range(kvw // LANES):
            sl = slice(LANES * j, LANES * (j + 1))
            out_ref[:, sl] = _rope(dk[:, sl], c, -s1, -s2).astype(BF16)
        out_ref[:, kvw:] = dv.astype(BF16)

    own = pl.BlockSpec((G, BLK, kvw), lambda m: (m, 1, 0))
    prev = pl.BlockSpec((G, BLK, kvw), lambda m: (m, 2, 0))
    before = pl.BlockSpec((1, BLK, kvw), lambda m: (jnp.maximum(G * m - 1, 0), 2, 0))
    nxt = pl.BlockSpec((G, BLK, kvw), lambda m: (m, 0, 0))
    after = pl.BlockSpec((1, BLK, kvw), lambda m: (jnp.minimum(G * m + G, nb - 1), 0, 0))
    five = [own, prev, before, nxt, after]
    return pl.pallas_call(
        body, name="kv_sum", grid=(ng,),
        in_specs=[pl.BlockSpec(memory_space=pl.ANY), *five, *five, *_tab_specs(G * BLK)],
        out_specs=pl.BlockSpec((G * BLK, 2 * kvw), lambda m: (m, KV_OFF // (2 * kvw))),
        out_shape=jax.ShapeDtypeStruct((T, QKV), BF16),
        input_output_aliases={0: 0},
        compiler_params=_params("parallel"),
    )(dqkv, *([dkc] * 5), *([dvc] * 5), rc, rs1, rs2)


def _me():
    return lax.axis_index("x"), lax.axis_index("y"), lax.axis_index("c")


def _half_rows(ref, sharded_rows, chip, core):
    R, C = ref.shape[-2], ref.shape[-1]
    lead = (slice(None),) * (len(ref.shape) - 2)
    if sharded_rows:
        per = R // N_CHIPS
        return ref.at[lead + (pl.ds(chip * per + core * (per // 2), per // 2), slice(None))]
    per = C // N_CHIPS
    return ref.at[lead + (pl.ds(core * (R // 2), R // 2), pl.ds(chip * per, per))]


class _Gather:
    def __init__(self, shards, sharded_rows):
        self.inputs = list(shards)
        self.rows = list(sharded_rows)
        self.n = self.n_in = self.n_out = len(shards)
        self.out_shapes = []
        for s, rows in zip(shards, sharded_rows):
            shp = list(s.shape)
            shp[-2 if rows else -1] *= N_CHIPS
            self.out_shapes.append(jax.ShapeDtypeStruct(tuple(shp), s.dtype))
        self.scratch = [pltpu.SemaphoreType.DMA((self.n, 6)), pltpu.SemaphoreType.DMA((self.n, 6)),
                        pltpu.SemaphoreType.DMA((self.n, 2))]

    def _ctx(self, ins, outs, sems):
        send_sems, recv_sems, local_sems = sems
        x, y, c = _me()
        chips = [(1 - x, y), (x, 1 - y), (1 - x, 1 - y)]

        def half_src(w, core):
            s = ins[w]
            R = s.shape[-2]
            return s.at[pl.ds(core * (R // 2), R // 2), :]

        def dst(w, chip, core):
            return _half_rows(outs[w], self.rows[w], chip, core)

        def copy(w, k, src, chip, core, to):
            return pltpu.make_async_remote_copy(
                src_ref=src, dst_ref=dst(w, chip, core), send_sem=send_sems.at[w, k], recv_sem=recv_sems.at[w, k],
                device_id=to, device_id_type=MESH)

        def local(w, core):
            return pltpu.make_async_copy(half_src(w, core), dst(w, 2 * x + y, core), local_sems.at[w, core])

        def first(w, j):
            qx, qy = chips[j]
            return copy(w, j, half_src(w, c), 2 * x + y, c, (qx, qy, c))

        def landed(w, j):
            qx, qy = chips[j]
            return copy(w, j, dst(w, 2 * qx + qy, c), 2 * qx + qy, c, (x, y, c))

        def passed(w, j):
            qx, qy = chips[j]
            return copy(w, 3 + j, dst(w, 2 * qx + qy, c), 2 * qx + qy, c, (x, y, 1 - c))

        def from_sibling(w, j):
            qx, qy = chips[j]
            return copy(w, 3 + j, dst(w, 2 * qx + qy, 1 - c), 2 * qx + qy, 1 - c, (x, y, c))

        return local, first, landed, passed, from_sibling

    def start(self, ins, outs, sems):
        local, first, _, _, _ = self._ctx(ins, outs, sems)
        for w in range(self.n):
            for core in range(2):
                local(w, core).start()
            for j in range(3):
                first(w, j).start()

    def mid(self, ins, outs, sems):
        _, _, landed, passed, _ = self._ctx(ins, outs, sems)
        for w in range(self.n):
            for j in range(3):
                landed(w, j).wait_recv()
                passed(w, j).start()

    def end(self, ins, outs, sems):
        local, first, _, passed, from_sibling = self._ctx(ins, outs, sems)
        for w in range(self.n):
            for j in range(3):
                from_sibling(w, j).wait_recv()
        for w in range(self.n):
            for j in range(3):
                first(w, j).wait_send()
                passed(w, j).wait_send()
            for core in range(2):
                local(w, core).wait()


class _Scatter:
    def __init__(self, grads, small=None):
        self.inputs = list(grads) + ([small] if small is not None else [])
        self.ng = len(grads)
        self.n = self.n_in = self.n_out = len(self.inputs)
        self.out_shapes = [jax.ShapeDtypeStruct((N_DEV, g.shape[1] // 2, g.shape[2]), g.dtype) for g in grads]
        if small is not None:
            self.out_shapes.append(jax.ShapeDtypeStruct((N_DEV,) + small.shape, small.dtype))
        self.scratch = [pltpu.SemaphoreType.DMA((self.n, N_DEV)), pltpu.SemaphoreType.DMA((self.n, N_DEV)),
                        pltpu.SemaphoreType.DMA((self.n,))]

    def _ctx(self, ins, outs, sems):
        send_sems, recv_sems, local_sems = sems
        x, y, c = _me()
        me = 4 * x + 2 * y + c

        def piece(w, chip, core):
            if w >= self.ng:
                return ins[w]
            half = ins[w].shape[1] // 2
            return ins[w].at[chip, pl.ds(core * half, half), :]

        def peer_of(k):
            return x ^ ((k >> 2) & 1), y ^ ((k >> 1) & 1), c ^ (k & 1)

        def local(w):
            return pltpu.make_async_copy(piece(w, 2 * x + y, c), outs[w].at[me], local_sems.at[w])

        def send(w, k):
            px, py, pc = peer_of(k)
            return pltpu.make_async_remote_copy(
                src_ref=piece(w, 2 * px + py, pc), dst_ref=outs[w].at[me], send_sem=send_sems.at[w, k],
                recv_sem=recv_sems.at[w, k], device_id=(px, py, pc), device_id_type=MESH)

        def recv(w, k):
            px, py, pc = peer_of(k)
            return pltpu.make_async_remote_copy(
                src_ref=piece(w, 2 * x + y, c), dst_ref=outs[w].at[4 * px + 2 * py + pc], send_sem=send_sems.at[w, k],
                recv_sem=recv_sems.at[w, k], device_id=(px, py, pc), device_id_type=MESH)

        return local, send, recv

    def start(self, ins, outs, sems):
        local, send, _ = self._ctx(ins, outs, sems)
        for w in range(self.n):
            local(w).start()
            for k in range(1, N_DEV):
                send(w, k).start()

    def mid(self, ins, outs, sems):
        pass

    def end(self, ins, outs, sems):
        local, send, recv = self._ctx(ins, outs, sems)
        for w in range(self.n):
            for k in range(1, N_DEV):
                recv(w, k).wait_recv()
        for w in range(self.n):
            for k in range(1, N_DEV):
                send(w, k).wait_send()
            local(w).wait()


class _Both:
    def __init__(self, a, b):
        self.a, self.b = a, b
        self.inputs = a.inputs + b.inputs
        self.out_shapes = a.out_shapes + b.out_shapes
        self.scratch = a.scratch + b.scratch
        self.n_in, self.n_out = a.n_in + b.n_in, a.n_out + b.n_out

    def _split(self, ins, outs, sems):
        a, na = self.a, len(self.a.scratch)
        return (ins[:a.n_in], outs[:a.n_out], sems[:na]), (ins[a.n_in:], outs[a.n_out:], sems[na:])

    def start(self, ins, outs, sems):
        pa, pb = self._split(ins, outs, sems)
        self.a.start(*pa)
        self.b.start(*pb)

    def mid(self, ins, outs, sems):
        pa, pb = self._split(ins, outs, sems)
        self.a.mid(*pa)
        self.b.mid(*pb)

    def end(self, ins, outs, sems):
        pa, pb = self._split(ins, outs, sems)
        self.a.end(*pa)
        self.b.end(*pb)


def _call(body, *, name, grid, in_specs, out_specs, out_shape, scratch_shapes=(), semantics, args, comm=None):
    if comm is None:
        outs = pl.pallas_call(
            body, name=name, grid=grid, in_specs=in_specs, out_specs=out_specs, out_shape=out_shape,
            scratch_shapes=list(scratch_shapes), compiler_params=_params(*semantics))(*args)
        return outs, []
    n_in, n_out, n_scr = len(in_specs), len(out_specs), len(scratch_shapes)

    total = math.prod(grid)
    first, middle, last = 0, (3 * total) // 4 - 1, total - 1
    assert first <= middle < last

    def at(step):
        lin = pl.program_id(0)
        for d in range(1, len(grid)):
            lin = lin * grid[d] + pl.program_id(d)
        return lin == step

    def hosted(*refs):
        h_in, c_in = refs[:n_in], refs[n_in:n_in + comm.n_in]
        rest = refs[n_in + comm.n_in:]
        h_out, c_out = rest[:n_out], rest[n_out:n_out + comm.n_out]
        rest = rest[n_out + comm.n_out:]
        h_scr, c_scr = rest[:n_scr], rest[n_scr:]

        @pl.when(at(first))
        def _():
            comm.start(c_in, c_out, c_scr)

        body(*h_in, *h_out, *h_scr)

        @pl.when(at(middle))
        def _():
            comm.mid(c_in, c_out, c_scr)

        @pl.when(at(last))
        def _():
            comm.end(c_in, c_out, c_scr)

    any_spec = pl.BlockSpec(memory_space=pl.ANY)
    outs = pl.pallas_call(
        hosted, name=name, grid=grid, in_specs=list(in_specs) + [any_spec] * comm.n_in,
        out_specs=list(out_specs) + [any_spec] * comm.n_out, out_shape=list(out_shape) + comm.out_shapes,
        scratch_shapes=list(scratch_shapes) + comm.scratch,
        compiler_params=_params(*(["arbitrary"] * len(grid))))(*args, *comm.inputs)
    return outs[:n_out], outs[n_out:]


def sum_swap(pieces, name, comm=None):
    nl = len(pieces)
    _, r2, cc = pieces[0].shape
    tr = 256 if r2 % 256 == 0 else (128 if r2 % 128 == 0 else r2 // 2)
    n = r2 // tr

    def body(*refs):
        p_refs, out = refs[:nl], refs[nl]
        slots, send_sems, local_sems, recv_sem = refs[nl + 1:]
        x, y, c = _me()
        sibling = (x, y, 1 - c)
        l, i = pl.program_id(0), pl.program_id(1)
        step = l * n + i

        def rows(st, core):
            return out.at[st // n, pl.ds(core * r2 + (st % n) * tr, tr), :]

        def copies(st):
            slot = st % 2
            local = pltpu.make_async_copy(slots.at[slot], rows(st, c), local_sems.at[slot])
            remote = pltpu.make_async_remote_copy(
                src_ref=slots.at[slot], dst_ref=rows(st, c), send_sem=send_sems.at[slot], recv_sem=recv_sem,
                device_id=sibling, device_id_type=MESH)
            return local, remote

        for ll in range(nl):
            @pl.when(l == ll)
            def _():
                acc = p_refs[ll][0].astype(F32)
                for d in range(1, N_DEV):
                    acc = acc + p_refs[ll][d].astype(F32)
                slots[step % 2] = acc

        for cp in copies(step):
            cp.start()

        @pl.when(step >= 1)
        def _():
            local, remote = copies(step - 1)
            local.wait()
            remote.wait_send()

        @pl.when(step == nl * n - 1)
        def _():
            local, remote = copies(step)
            local.wait()
            remote.wait_send()
            theirs = out.at[:, pl.ds((1 - c) * r2, r2), :]
            pltpu.make_async_remote_copy(src_ref=theirs, dst_ref=theirs, send_sem=send_sems.at[0],
                                         recv_sem=recv_sem, device_id=sibling, device_id_type=MESH).wait_recv()

    def piece_spec(ll):
        def index(l, i):
            return (0, jnp.where(l == ll, i, jnp.where(l < ll, 0, n - 1)), 0)
        return pl.BlockSpec((N_DEV, tr, cc), index)

    (out,), got = _call(
        body, name=name, grid=(nl, n),
        in_specs=[piece_spec(ll) for ll in range(nl)],
        out_specs=[pl.BlockSpec(memory_space=pl.ANY)],
        out_shape=[jax.ShapeDtypeStruct((nl, 2 * r2, cc), F32)],
        scratch_shapes=[pltpu.VMEM((2, tr, cc), F32), pltpu.SemaphoreType.DMA((2,)), pltpu.SemaphoreType.DMA((2,)),
                        pltpu.SemaphoreType.DMA(())],
        semantics=("arbitrary", "arbitrary"), args=tuple(pieces), comm=comm)
    return (out, got) if comm is not None else out


def sum_pieces(pieces, name):
    _, R, C = pieces.shape
    tr = _tile(R, 128) if R % 128 == 0 else R

    def body(p_ref, o_ref):
        acc = p_ref[0].astype(F32)
        for d in range(1, N_DEV):
            acc = acc + p_ref[d].astype(F32)
        o_ref[...] = acc

    return pl.pallas_call(
        body, name=name, grid=(R // tr,),
        in_specs=[pl.BlockSpec((N_DEV, tr, C), lambda i: (0, i, 0))],
        out_specs=pl.BlockSpec((tr, C), lambda i: (i, 0)),
        out_shape=jax.ShapeDtypeStruct((R, C), F32),
        compiler_params=_params("parallel"),
    )(pieces)


def adamw(w, g, m, v, name):
    Lyr, R, C = w.shape
    tr = _tile(R, 256) if R % 8 == 0 else R
    c1 = 1.0 / (1.0 - ADAM_B1 ** ADAM_STEP)
    c2 = 1.0 / (1.0 - ADAM_B2 ** ADAM_STEP)

    def body(w_ref, g_ref, m_ref, v_ref, d_ref, nm_ref, nv_ref):
        gv = g_ref[...]
        nm = ADAM_B1 * m_ref[...] + (1.0 - ADAM_B1) * gv
        nv = ADAM_B2 * v_ref[...] + (1.0 - ADAM_B2) * (gv * gv)
        nm_ref[...] = nm
        nv_ref[...] = nv
        d_ref[...] = -ADAM_LR * ((nm * c1) / (jnp.sqrt(nv * c2) + ADAM_EPS) + ADAM_WD * w_ref[...])

    spec = pl.BlockSpec((None, tr, C), lambda l, i: (l, i, 0))
    shp = jax.ShapeDtypeStruct(w.shape, F32)
    return pl.pallas_call(
        body, name=name, grid=(Lyr, R // tr),
        in_specs=[spec] * 4, out_specs=[spec] * 3, out_shape=[shp] * 3,
        compiler_params=_params("parallel", "parallel"),
    )(w, g, m, v)


def _rope_tables(T):
    pos = jnp.arange(T, dtype=F32)
    inv_freq = THETA ** (-jnp.arange(0, ROT, 2, dtype=F32) / ROT)
    ang = pos[:, None] * inv_freq[None, :]
    cs = jnp.concatenate([jnp.cos(ang), jnp.sin(ang)], axis=1)
    half = ROT // 2
    lane = jnp.arange(3 * LANES)
    table, lm = lane // LANES, lane % HD
    src = jnp.where(table == 0, lm % half, half + lm % half)
    i32 = lambda b: b.astype(jnp.int32)
    sign = jnp.where(table == 0, i32(lm < ROT), jnp.where(table == 1, -i32(lm < half), i32((lm >= half) & (lm < ROT))))
    place = (jnp.arange(ROT)[:, None] == src[None, :]) * sign[None, :].astype(F32)
    ones = ((table == 0) & (lm >= ROT)).astype(F32)
    return jnp.dot(cs, place, precision=lax.Precision.HIGHEST) + ones[None, :]


def _tab_specs(rows):
    return [pl.BlockSpec((rows, LANES), lambda i, k=k: (i, k)) for k in range(3)]


def kernel(x, attn_norm, attn_w_qkv, attn_w_o, attn_sink, conv_norm, conv_w_pw1, conv_b_pw1, conv_w_dw, conv_b_dw, conv_ln_g, conv_ln_b, conv_w_pw2, conv_b_pw2, ffn_norm, ffn_w_gu, ffn_w_down, final_norm, loss_target, m_attn_norm, m_attn_w_qkv, m_attn_w_o, m_attn_sink, m_conv_norm, m_conv_w_pw1, m_conv_b_pw1, m_conv_w_dw, m_conv_b_dw, m_conv_ln_g, m_conv_ln_b, m_conv_w_pw2, m_conv_b_pw2, m_ffn_norm, m_ffn_w_gu, m_ffn_w_down, m_final_norm, v_attn_norm, v_attn_w_qkv, v_attn_w_o, v_attn_sink, v_conv_norm, v_conv_w_pw1, v_conv_b_pw1, v_conv_w_dw, v_conv_b_dw, v_conv_ln_g, v_conv_ln_b, v_conv_w_pw2, v_conv_b_pw2, v_ffn_norm, v_ffn_w_gu, v_ffn_w_down, v_final_norm):
    T = x.shape[1]
    x0 = x[0]
    target = loss_target[0]
    ix, iy = lax.axis_index("x"), lax.axis_index("y")
    chip = 2 * ix + iy
    rc = rs1 = rs2 = _rope_tables(T)

    bf = lambda t: t.astype(BF16)

    def place(vec, width):
        return lax.dynamic_update_slice(jnp.zeros((vec.shape[0], N_CHIPS * width), F32), vec, (0, chip * width))

    small_rows = jnp.concatenate([
        place(conv_norm, 256), place(conv_b_pw1, 512).reshape(2, D), place(conv_b_dw, 256), place(conv_ln_g, 256),
        place(conv_ln_b, 256), place(conv_b_pw2, 256), jnp.zeros((1, D), F32),
        place(conv_w_dw[0], 256), jnp.zeros((1, D), F32)], axis=0)

    h0, (w_qkv,) = rms_first(x0, attn_norm, comm=_Gather([bf(attn_w_qkv[0])], [False]))
    qkv, (w_o, got) = qkv_proj(h0, w_qkv, rc, rs1, rs2,
                               comm=_Both(_Gather([bf(attn_w_o[0])], [True]), _Scatter([], small_rows)))
    psmall = sum_pieces(got, "sum_small_params") * 0.5
    p_conv_norm, p_b_pw1 = psmall[0:1], psmall[1:3].reshape(1, 2 * D)
    p_b_dw, p_ln_g, p_ln_b, p_b_pw2 = psmall[3:4], psmall[4:5], psmall[5:6], psmall[6:7]
    p_w_dw = psmall[8:40]
    sink = attn_sink[0]
    o, (w_gu0,) = attn_fwd(qkv, sink, comm=_Gather([bf(ffn_w_gu[0])], [False]))
    zero_b = jnp.zeros((1, D), F32)
    zero_gu = jnp.zeros((1, 2 * DFF), F32)
    x1, h1, gu0, act0, (w_down0, w_pw1, w_pw2) = rms_mm_gate(
        (o, w_o, zero_b, x0), ffn_norm[0:1], w_gu0, zero_gu, DFF, True, BF16, "ffn0_up",
        comm=_Gather([bf(ffn_w_down[0]), bf(conv_w_pw1[0]), bf(conv_w_pw2[0])], [True, False, True]))
    x2, h2, pre, glu, (w_down1,) = rms_mm_gate((act0, w_down0, zero_b, x1), p_conv_norm, w_pw1, p_b_pw1, D, False, F32,
                                               "conv_pw1", comm=_Gather([bf(ffn_w_down[1])], [True]))
    dwc, sw, (w_gu1,) = conv_fwd(glu, p_w_dw, p_b_dw, p_ln_g, p_ln_b, comm=_Gather([bf(ffn_w_gu[1])], [False]))
    x3, h3, gu1, act1, _ = rms_mm_gate((sw, w_pw2, p_b_pw2, x2), ffn_norm[1:2], w_gu1, zero_gu, DFF, True, BF16,
                                       "ffn1_up")
    dx4, loss_part, d_final = mm_res_loss(act1, w_down1, x3, final_norm.reshape(1, D), target)

    dgu1, _ = swiglu_bwd(dx4, w_down1, gu1, "ffn1_down_bwd")
    g_down1 = dw_row(act1, dx4, "ffn1_down_dw")
    dx3, d_ffn1, ddwc, d_ln_g, d_ln_b, d_b_pw2, _ = mm_bt_rmsbwd(
        dgu1, w_gu1, x3, ffn_norm[1:2], dx4, "ffn1_up_bwd", conv_tail=(w_pw2, dwc, p_ln_g, p_ln_b))
    g_gu1 = dw_col(h3, dgu1, "ffn1_up_dw")

    g_pw2 = dw_row(sw, dx3, "conv_pw2_dw")
    dpre, d_w_dw, d_b_dw, d_b_pw1, (r_gu1, r_down1) = conv_bwd(ddwc, glu, pre, p_w_dw,
                                                               comm=_Scatter([g_gu1, g_down1]))
    dx2, d_conv_norm, _ = mm_bt_rmsbwd(dpre, w_pw1, x2, p_conv_norm, dx3, "conv_pw1_bwd")
    g_pw1 = dw_col(h2, dpre, "conv_pw1_dw")

    dgu0, (r_pw1, r_pw2) = swiglu_bwd(dx2, w_down0, gu0, "ffn0_down_bwd", comm=_Scatter([g_pw1, g_pw2]))
    g_down0 = dw_row(act0, dx2, "ffn0_down_dw")
    dx1, d_ffn0, do, _ = mm_bt_rmsbwd(dgu0, w_gu0, x1, ffn_norm[0:1], dx2, "ffn0_up_bwd", proj_w=w_o)
    g_gu0 = dw_col(h1, dgu0, "ffn0_up_dw")

    g_o = dw_row(o, dx1, "attn_out_dw")
    dq, dkc, dvc, d_sink, (r_gu0, r_down0, r_o) = attn_bwd(qkv, o, do, sink, rc, rs1, rs2,
                                                           comm=_Scatter([g_gu0, g_down0, g_o]))
    dqkv = kv_sum(dq, dkc, dvc, rc, rs1, rs2)[None]
    g_qkv = dw_col(h0, dqkv, "attn_qkv_dw")
    dx0, d_attn_norm, (r_qkv,) = mm_bt_rmsbwd(dqkv, w_qkv, x0, attn_norm, dx1, "attn_qkv_bwd",
                                              comm=_Scatter([g_qkv]))

    pad16 = lambda t: jnp.concatenate([t, jnp.zeros((1, D - t.shape[1]), F32)], axis=1)
    small_g = jnp.concatenate([
        d_attn_norm, pad16(d_sink), d_conv_norm, d_b_pw1.reshape(2, D), d_b_dw, d_ln_g, d_ln_b, d_b_pw2,
        d_ffn0, d_ffn1, d_final, pad16(loss_part), jnp.zeros((3, D), F32), d_w_dw], axis=0)
    gf_gu, (r_small,) = sum_swap([r_gu0, r_gu1], "sum_gu", comm=_Scatter([], small_g))
    gf_down = sum_swap([r_down0, r_down1], "sum_down")
    gf_pw1, gf_qkv = sum_swap([r_pw1], "sum_pw1"), sum_swap([r_qkv], "sum_qkv")
    gf_o_pw2 = sum_swap([r_o, r_pw2], "sum_o_pw2")
    gf_o, gf_pw2 = gf_o_pw2[0:1], gf_o_pw2[1:2]
    gs = sum_pieces(r_small, "sum_small_grads")
    loss = gs[12, 0]

    def take(row0, nrows, width):
        return lax.dynamic_slice(gs, (row0, chip * width), (nrows, width))

    grads = {
        "attn_norm": gs[0:1], "attn_w_qkv": gf_qkv, "attn_w_o": gf_o, "attn_sink": gs[1:2, :N_HEADS],
        "conv_norm": take(2, 1, 256), "conv_w_pw1": gf_pw1,
        "conv_b_pw1": lax.dynamic_slice(gs[3:5].reshape(1, 2 * D), (0, chip * 512), (1, 512)),
        "conv_w_dw": take(16, 32, 256)[None, :CONV_W], "conv_b_dw": take(5, 1, 256), "conv_ln_g": take(6, 1, 256),
        "conv_ln_b": take(7, 1, 256), "conv_w_pw2": gf_pw2, "conv_b_pw2": take(8, 1, 256),
        "ffn_norm": gs[9:11], "ffn_w_gu": gf_gu, "ffn_w_down": gf_down, "final_norm": gs[11],
    }
    weights = dict(attn_norm=attn_norm, attn_w_qkv=attn_w_qkv, attn_w_o=attn_w_o, attn_sink=attn_sink,
                   conv_norm=conv_norm, conv_w_pw1=conv_w_pw1, conv_b_pw1=conv_b_pw1, conv_w_dw=conv_w_dw,
                   conv_b_dw=conv_b_dw, conv_ln_g=conv_ln_g, conv_ln_b=conv_ln_b, conv_w_pw2=conv_w_pw2,
                   conv_b_pw2=conv_b_pw2, ffn_norm=ffn_norm, ffn_w_gu=ffn_w_gu, ffn_w_down=ffn_w_down,
                   final_norm=final_norm)
    m_in = dict(attn_norm=m_attn_norm, attn_w_qkv=m_attn_w_qkv, attn_w_o=m_attn_w_o, attn_sink=m_attn_sink,
                conv_norm=m_conv_norm, conv_w_pw1=m_conv_w_pw1, conv_b_pw1=m_conv_b_pw1, conv_w_dw=m_conv_w_dw,
                conv_b_dw=m_conv_b_dw, conv_ln_g=m_conv_ln_g, conv_ln_b=m_conv_ln_b, conv_w_pw2=m_conv_w_pw2,
                conv_b_pw2=m_conv_b_pw2, ffn_norm=m_ffn_norm, ffn_w_gu=m_ffn_w_gu, ffn_w_down=m_ffn_w_down,
                final_norm=m_final_norm)
    v_in = dict(attn_norm=v_attn_norm, attn_w_qkv=v_attn_w_qkv, attn_w_o=v_attn_w_o, attn_sink=v_attn_sink,
                conv_norm=v_conv_norm, conv_w_pw1=v_conv_w_pw1, conv_b_pw1=v_conv_b_pw1, conv_w_dw=v_conv_w_dw,
                conv_b_dw=v_conv_b_dw, conv_ln_g=v_conv_ln_g, conv_ln_b=v_conv_ln_b, conv_w_pw2=v_conv_w_pw2,
                conv_b_pw2=v_conv_b_pw2, ffn_norm=v_ffn_norm, ffn_w_gu=v_ffn_w_gu, ffn_w_down=v_ffn_w_down,
                final_norm=v_final_norm)
    order = list(weights)
    g_out, d_out, m_out, v_out = [], [], [], []
    for nm in order:
        w = weights[nm]
        shape = w.shape
        as3 = lambda t: t.reshape((1,) * (3 - len(shape)) + shape) if len(shape) < 3 else t.reshape(shape)
        g3 = as3(grads[nm].reshape(shape))
        delta, nm_, nv_ = adamw(as3(w), g3, as3(m_in[nm]), as3(v_in[nm]), "adamw_" + nm)
        g_out.append(g3.reshape(shape))
        d_out.append(delta.reshape(shape))
        m_out.append(nm_.reshape(shape))
        v_out.append(nv_.reshape(shape))
    return (loss, dx0[None], *g_out, *d_out, *m_out, *v_out)
```
